```python
import math
import jax, jax.numpy as jnp
from jax import lax
import numpy as np

D_MODEL = 1024
BATCH = 8
SEQ = 8192
DEPTH = 1

HEAD_DIM = 64
N_Q_HEADS = 8
N_KV_HEADS = 2
Q_PER_KV = N_Q_HEADS // N_KV_HEADS
ATTN_WIDTH = N_Q_HEADS * HEAD_DIM
KV_WIDTH = N_KV_HEADS * HEAD_DIM
WINDOW = 128
ATTN_BLOCK = 128
ROPE_THETA = 10000.0
SSM_HEADS = 8
SSM_HEAD_DIM = 64
SSM_WIDTH = SSM_HEADS * SSM_HEAD_DIM
SSM_GROUPS = 2
HEADS_PER_GROUP = SSM_HEADS // SSM_GROUPS
D_STATE = 128
CONV_WIDTH = 4
CHUNK = 128
CONV_CH = SSM_WIDTH + 2 * SSM_GROUPS * D_STATE
MIX_WIDTH = ATTN_WIDTH + SSM_WIDTH
IN_PROJ = ATTN_WIDTH + 2 * KV_WIDTH + SSM_WIDTH + CONV_CH + SSM_HEADS
D_FF = -(-8 * D_MODEL // (3 * 256)) * 256
N_MOD = 6
EPS = 1e-6

kernel_name = "hymba_swa_sink_ssd_adaln_layer"


def rmsnorm(x, w):
    xf = x.astype(jnp.float32)
    y = xf * lax.rsqrt(jnp.mean(xf * xf, axis=-1, keepdims=True) + EPS)
    return (y * w.astype(jnp.float32)).astype(x.dtype)


def modulate(h, shift, scale):
    return h * (1.0 + scale[:, None, :]) + shift[:, None, :]


def rope(t, positions):
    half = HEAD_DIM // 2
    inv_freq = ROPE_THETA ** (-jnp.arange(half, dtype=jnp.float32) / half)
    ang = positions.astype(jnp.float32)[..., None] * inv_freq
    cos = jnp.cos(ang)[:, :, None, :]
    sin = jnp.sin(ang)[:, :, None, :]
    tf = t.astype(jnp.float32)
    t1, t2 = tf[..., :half], tf[..., half:]
    out = jnp.concatenate([t1 * cos - t2 * sin, t2 * cos + t1 * sin], axis=-1)
    return out.astype(t.dtype)


def sliding_window_attention(q, k, v, sinks):
    b, s = q.shape[0], q.shape[1]
    nb = s // ATTN_BLOCK
    qb = q.reshape(b, nb, ATTN_BLOCK, N_KV_HEADS, Q_PER_KV, HEAD_DIM)

    def band(t):
        tb = t.reshape(b, nb, ATTN_BLOCK, N_KV_HEADS, HEAD_DIM)
        prev = jnp.pad(tb, ((0, 0), (1, 0), (0, 0), (0, 0), (0, 0)))[:, :-1]
        return jnp.concatenate([prev, tb], axis=2)

    kb, vb = band(k), band(v)
    scores = jnp.einsum('bnqhgd,bnkhd->bnhgqk', qb, kb).astype(jnp.float32)
    scores = scores * (1.0 / math.sqrt(HEAD_DIM))
    blk = jnp.arange(nb)[:, None] * ATTN_BLOCK
    qpos = blk + jnp.arange(ATTN_BLOCK)[None, :]
    kpos = blk - ATTN_BLOCK + jnp.arange(2 * ATTN_BLOCK)[None, :]
    diff = qpos[:, :, None] - kpos[:, None, :]
    mask = (diff >= 0) & (diff < WINDOW) & (kpos[:, None, :] >= 0)
    scores = jnp.where(mask[None, :, None, None], scores, -jnp.inf)
    sink = sinks.astype(jnp.float32).reshape(N_KV_HEADS, Q_PER_KV)[None, None, :, :, None, None]
    m = jnp.maximum(jnp.max(scores, axis=-1, keepdims=True), sink)
    p = jnp.exp(scores - m)
    denom = jnp.sum(p, axis=-1, keepdims=True) + jnp.exp(sink - m)
    probs = (p / denom).astype(v.dtype)
    out = jnp.einsum('bnhgqk,bnkhd->bnqhgd', probs, vb)
    return out.reshape(b, s, ATTN_WIDTH)


def causal_depthwise_conv(u, w, bias):
    out = lax.conv_general_dilated(
        u, w[:, None, :].astype(u.dtype), window_strides=(1,),
        padding=[(CONV_WIDTH - 1, 0)], dimension_numbers=('NWC', 'WIO', 'NWC'),
        feature_group_count=u.shape[-1])
    return out + bias.astype(u.dtype)


def ssd_chunked_scan(xs, dt, A, Bm, Cm, d_skip):
    b, s = xs.shape[0], xs.shape[1]
    nc = s // CHUNK
    xf = xs.astype(jnp.float32)
    xdt = (xf * dt[..., None]).reshape(b, nc, CHUNK, SSM_GROUPS, HEADS_PER_GROUP, SSM_HEAD_DIM)
    a = (dt * A).reshape(b, nc, CHUNK, SSM_GROUPS, HEADS_PER_GROUP)
    a_cs = jnp.cumsum(a, axis=2)
    Bc = Bm.astype(jnp.float32).reshape(b, nc, CHUNK, SSM_GROUPS, D_STATE)
    Cc = Cm.astype(jnp.float32).reshape(b, nc, CHUNK, SSM_GROUPS, D_STATE)
    causal = jnp.tril(jnp.ones((CHUNK, CHUNK), dtype=bool))[None, None, :, :, None, None]
    seg = a_cs[:, :, :, None] - a_cs[:, :, None, :]
    decay = jnp.exp(jnp.where(causal, seg, -jnp.inf))
    cb = jnp.einsum('bclgn,bcsgn->bclsg', Cc, Bc)
    y_diag = jnp.einsum('bclsg,bclsgj,bcsgjp->bclgjp', cb, decay, xdt)
    decay_to_end = jnp.exp(a_cs[:, :, -1:] - a_cs)
    states = jnp.einsum('bclgn,bclgj,bclgjp->bcgjpn', Bc, decay_to_end, xdt)
    chunk_decay = jnp.exp(a_cs[:, :, -1])

    def step(h, inp):
        st, dec = inp
        return h * dec[..., None, None] + st, h

    init = jnp.zeros((b, SSM_GROUPS, HEADS_PER_GROUP, SSM_HEAD_DIM, D_STATE), jnp.float32)
    _, prev = lax.scan(step, init, (jnp.moveaxis(states, 1, 0), jnp.moveaxis(chunk_decay, 1, 0)))
    prev = jnp.moveaxis(prev, 0, 1)
    y_off = jnp.einsum('bclgn,bcgjpn,bclgj->bclgjp', Cc, prev, jnp.exp(a_cs))
    y = (y_diag + y_off).reshape(b, s, SSM_HEADS, SSM_HEAD_DIM)
    return y + xf * d_skip.astype(jnp.float32)[:, None]


def hybrid_mixer(h, positions, w_in, conv_w, conv_b, dt_bias, a_log, d_skip, sinks, ssm_norm_w, w_out):
    b, s = h.shape[0], h.shape[1]
    proj = h @ w_in
    o1 = ATTN_WIDTH
    o2 = o1 + KV_WIDTH
    o3 = o2 + KV_WIDTH
    o4 = o3 + SSM_WIDTH
    o5 = o4 + CONV_CH
    q, k, v, z, xbc, dt_raw = jnp.split(proj, [o1, o2, o3, o4, o5], axis=-1)
    q = rope(q.reshape(b, s, N_Q_HEADS, HEAD_DIM), positions)
    k = rope(k.reshape(b, s, N_KV_HEADS, HEAD_DIM), positions)
    v = v.reshape(b, s, N_KV_HEADS, HEAD_DIM)
    attn = sliding_window_attention(q, k, v, sinks)
    xbc = jax.nn.silu(causal_depthwise_conv(xbc, conv_w, conv_b))
    xs, Bm, Cm = jnp.split(xbc, [SSM_WIDTH, SSM_WIDTH + SSM_GROUPS * D_STATE], axis=-1)
    xs = xs.reshape(b, s, SSM_HEADS, SSM_HEAD_DIM)
    Bm = Bm.reshape(b, s, SSM_GROUPS, D_STATE)
    Cm = Cm.reshape(b, s, SSM_GROUPS, D_STATE)
    dt = jax.nn.softplus(dt_raw.astype(jnp.float32) + dt_bias.astype(jnp.float32))
    A = -jnp.exp(a_log.astype(jnp.float32))
    y = ssd_chunked_scan(xs, dt, A, Bm, Cm, d_skip).reshape(b, s, SSM_WIDTH)
    y = y * jax.nn.silu(z.astype(jnp.float32))
    yg = y.reshape(b, s, SSM_GROUPS, SSM_WIDTH // SSM_GROUPS)
    yg = yg * lax.rsqrt(jnp.mean(yg * yg, axis=-1, keepdims=True) + EPS)
    y = (yg.reshape(b, s, SSM_WIDTH) * ssm_norm_w.astype(jnp.float32)).astype(h.dtype)
    return jnp.concatenate([attn.astype(h.dtype), y], axis=-1) @ w_out


def swiglu(h, w_gate_up, w_down):
    gu = h @ w_gate_up
    g, u = jnp.split(gu, 2, axis=-1)
    return (jax.nn.silu(g) * u) @ w_down


def _fwd_setup_inputs(seed: int = 0) -> dict:
    key = jax.random.key(seed)
    ks = jax.random.split(key, 20)
    f32 = jnp.float32
    L = DEPTH
    x = jax.random.normal(ks[0], (BATCH, SEQ, D_MODEL), f32)
    c = jax.random.normal(ks[1], (BATCH, D_MODEL), f32)
    offset = jax.random.randint(ks[2], (BATCH, 1), 0, 4096, dtype=jnp.int32)
    positions = (jnp.arange(SEQ, dtype=jnp.int32)[None, :] + offset).astype(jnp.int32)
    w_ada = jax.random.normal(ks[3], (L, D_MODEL, N_MOD * D_MODEL), f32) * (0.5 * D_MODEL ** -0.5)
    b_ada = 0.01 * jax.random.normal(ks[4], (L, N_MOD * D_MODEL), f32)
    norm1_w = 1.0 + 0.02 * jax.random.normal(ks[5], (L, D_MODEL), f32)
    w_in = jax.random.normal(ks[6], (L, D_MODEL, IN_PROJ), f32) * D_MODEL ** -0.5
    conv_w = jax.random.normal(ks[7], (L, CONV_WIDTH, CONV_CH), f32) * CONV_WIDTH ** -0.5
    conv_b = 0.01 * jax.random.normal(ks[8], (L, CONV_CH), f32)
    dt0 = jnp.exp(jax.random.uniform(ks[9], (L, SSM_HEADS), f32, math.log(1e-3), math.log(1e-1)))
    dt_bias = dt0 + jnp.log(-jnp.expm1(-dt0))
    a_log = jnp.log(jax.random.uniform(ks[10], (L, SSM_HEADS), f32, 1.0, 16.0))
    d_skip = 1.0 + 0.1 * jax.random.normal(ks[11], (L, SSM_HEADS), f32)
    attn_sinks = jax.random.normal(ks[12], (L, N_Q_HEADS), f32)
    ssm_norm_w = 1.0 + 0.02 * jax.random.normal(ks[13], (L, SSM_WIDTH), f32)
    w_out = jax.random.normal(ks[14], (L, MIX_WIDTH, D_MODEL), f32) * MIX_WIDTH ** -0.5
    norm2_w = 1.0 + 0.02 * jax.random.normal(ks[15], (L, D_MODEL), f32)
    w_gate_up = jax.random.normal(ks[16], (L, D_MODEL, 2 * D_FF), f32) * D_MODEL ** -0.5
    w_down = jax.random.normal(ks[17], (L, D_FF, D_MODEL), f32) * D_FF ** -0.5
    final_norm_w = 1.0 + 0.02 * jax.random.normal(ks[18], (D_MODEL,), f32)
    return {"x": x, "c": c, "positions": positions, "w_ada": w_ada, "b_ada": b_ada,
            "norm1_w": norm1_w, "w_in": w_in, "conv_w": conv_w, "conv_b": conv_b,
            "dt_bias": dt_bias, "a_log": a_log, "d_skip": d_skip, "attn_sinks": attn_sinks,
            "ssm_norm_w": ssm_norm_w, "w_out": w_out, "norm2_w": norm2_w,
            "w_gate_up": w_gate_up, "w_down": w_down, "final_norm_w": final_norm_w}


def _fwd_reference(x, c, positions, w_ada, b_ada, norm1_w, w_in, conv_w, conv_b, dt_bias, a_log,
              d_skip, attn_sinks, ssm_norm_w, w_out, norm2_w, w_gate_up, w_down, final_norm_w):
    for layer in range(DEPTH):
        mod = jax.nn.silu(c) @ w_ada[layer] + b_ada[layer]
        shift1, scale1, gate1, shift2, scale2, gate2 = jnp.split(mod, N_MOD, axis=-1)
        h = modulate(rmsnorm(x, norm1_w[layer]), shift1, scale1)
        x = x + gate1[:, None, :] * hybrid_mixer(
            h, positions, w_in[layer], conv_w[layer], conv_b[layer], dt_bias[layer],
            a_log[layer], d_skip[layer], attn_sinks[layer], ssm_norm_w[layer], w_out[layer])
        h = modulate(rmsnorm(x, norm2_w[layer]), shift2, scale2)
        x = x + gate2[:, None, :] * swiglu(h, w_gate_up[layer], w_down[layer])
    return rmsnorm(x, final_norm_w)


import jax as _jax
import jax.numpy as _jnp

TWIN_FORMAT = 'train_step'
FWD_PARAMS = ['x', 'c', 'positions', 'w_ada', 'b_ada', 'norm1_w', 'w_in', 'conv_w', 'conv_b', 'dt_bias', 'a_log', 'd_skip', 'attn_sinks', 'ssm_norm_w', 'w_out', 'norm2_w', 'w_gate_up', 'w_down', 'final_norm_w']
TWIN_WEIGHTS = ['w_ada', 'b_ada', 'norm1_w', 'w_in', 'conv_w', 'conv_b', 'dt_bias', 'a_log', 'd_skip', 'attn_sinks', 'ssm_norm_w', 'w_out', 'norm2_w', 'w_gate_up', 'w_down', 'final_norm_w']
TWIN_DIFF_INPUT = 'x'
TWIN_INPUTS = ['x', 'c', 'positions', 'w_ada', 'b_ada', 'norm1_w', 'w_in', 'conv_w', 'conv_b', 'dt_bias', 'a_log', 'd_skip', 'attn_sinks', 'ssm_norm_w', 'w_out', 'norm2_w', 'w_gate_up', 'w_down', 'final_norm_w', 'loss_target', 'm_w_ada', 'm_b_ada', 'm_norm1_w', 'm_w_in', 'm_conv_w', 'm_conv_b', 'm_dt_bias', 'm_a_log', 'm_d_skip', 'm_attn_sinks', 'm_ssm_norm_w', 'm_w_out', 'm_norm2_w', 'm_w_gate_up', 'm_w_down', 'm_final_norm_w', 'v_w_ada', 'v_b_ada', 'v_norm1_w', 'v_w_in', 'v_conv_w', 'v_conv_b', 'v_dt_bias', 'v_a_log', 'v_d_skip', 'v_attn_sinks', 'v_ssm_norm_w', 'v_w_out', 'v_norm2_w', 'v_w_gate_up', 'v_w_down', 'v_final_norm_w']
TWIN_OUTPUTS = ['loss', 'grad_x', 'grad_w_ada', 'grad_b_ada', 'grad_norm1_w', 'grad_w_in', 'grad_conv_w', 'grad_conv_b', 'grad_dt_bias', 'grad_a_log', 'grad_d_skip', 'grad_attn_sinks', 'grad_ssm_norm_w', 'grad_w_out', 'grad_norm2_w', 'grad_w_gate_up', 'grad_w_down', 'grad_final_norm_w', 'delta_w_ada', 'delta_b_ada', 'delta_norm1_w', 'delta_w_in', 'delta_conv_w', 'delta_conv_b', 'delta_dt_bias', 'delta_a_log', 'delta_d_skip', 'delta_attn_sinks', 'delta_ssm_norm_w', 'delta_w_out', 'delta_norm2_w', 'delta_w_gate_up', 'delta_w_down', 'delta_final_norm_w', 'new_m_w_ada', 'new_m_b_ada', 'new_m_norm1_w', 'new_m_w_in', 'new_m_conv_w', 'new_m_conv_b', 'new_m_dt_bias', 'new_m_a_log', 'new_m_d_skip', 'new_m_attn_sinks', 'new_m_ssm_norm_w', 'new_m_w_out', 'new_m_norm2_w', 'new_m_w_gate_up', 'new_m_w_down', 'new_m_final_norm_w', 'new_v_w_ada', 'new_v_b_ada', 'new_v_norm1_w', 'new_v_w_in', 'new_v_conv_w', 'new_v_conv_b', 'new_v_dt_bias', 'new_v_a_log', 'new_v_d_skip', 'new_v_attn_sinks', 'new_v_ssm_norm_w', 'new_v_w_out', 'new_v_norm2_w', 'new_v_w_gate_up', 'new_v_w_down', 'new_v_final_norm_w']
TWIN_LEAF_KINDS = {'loss': 'loss', 'grad_x': 'grad_x', 'grad_w_ada': 'grad_w', 'grad_b_ada': 'grad_w', 'grad_norm1_w': 'grad_w', 'grad_w_in': 'grad_w', 'grad_conv_w': 'grad_w', 'grad_conv_b': 'grad_w', 'grad_dt_bias': 'grad_w', 'grad_a_log': 'grad_w', 'grad_d_skip': 'grad_w', 'grad_attn_sinks': 'grad_w', 'grad_ssm_norm_w': 'grad_w', 'grad_w_out': 'grad_w', 'grad_norm2_w': 'grad_w', 'grad_w_gate_up': 'grad_w', 'grad_w_down': 'grad_w', 'grad_final_norm_w': 'grad_w', 'delta_w_ada': 'delta_w', 'delta_b_ada': 'delta_w', 'delta_norm1_w': 'delta_w', 'delta_w_in': 'delta_w', 'delta_conv_w': 'delta_w', 'delta_conv_b': 'delta_w', 'delta_dt_bias': 'delta_w', 'delta_a_log': 'delta_w', 'delta_d_skip': 'delta_w', 'delta_attn_sinks': 'delta_w', 'delta_ssm_norm_w': 'delta_w', 'delta_w_out': 'delta_w', 'delta_norm2_w': 'delta_w', 'delta_w_gate_up': 'delta_w', 'delta_w_down': 'delta_w', 'delta_final_norm_w': 'delta_w', 'new_m_w_ada': 'new_m', 'new_m_b_ada': 'new_m', 'new_m_norm1_w': 'new_m', 'new_m_w_in': 'new_m', 'new_m_conv_w': 'new_m', 'new_m_conv_b': 'new_m', 'new_m_dt_bias': 'new_m', 'new_m_a_log': 'new_m', 'new_m_d_skip': 'new_m', 'new_m_attn_sinks': 'new_m', 'new_m_ssm_norm_w': 'new_m', 'new_m_w_out': 'new_m', 'new_m_norm2_w': 'new_m', 'new_m_w_gate_up': 'new_m', 'new_m_w_down': 'new_m', 'new_m_final_norm_w': 'new_m', 'new_v_w_ada': 'new_v', 'new_v_b_ada': 'new_v', 'new_v_norm1_w': 'new_v', 'new_v_w_in': 'new_v', 'new_v_conv_w': 'new_v', 'new_v_conv_b': 'new_v', 'new_v_dt_bias': 'new_v', 'new_v_a_log': 'new_v', 'new_v_d_skip': 'new_v', 'new_v_attn_sinks': 'new_v', 'new_v_ssm_norm_w': 'new_v', 'new_v_w_out': 'new_v', 'new_v_norm2_w': 'new_v', 'new_v_w_gate_up': 'new_v', 'new_v_w_down': 'new_v', 'new_v_final_norm_w': 'new_v'}


def _forward(args):
    return _fwd_reference(*[args[k] for k in FWD_PARAMS])


def _output_shape():
    def fwd():
        inp = _fwd_setup_inputs(0)
        return _fwd_reference(*[inp[k] for k in FWD_PARAMS])
    out = _jax.eval_shape(fwd)
    return out.shape, out.dtype

N_MICROBATCH = 1
ADAM_LR = 0.001
ADAM_B1 = 0.9
ADAM_B2 = 0.999
ADAM_EPS = 1e-08
ADAM_WD = 0.01
ADAM_STEP = 10
PER_EXAMPLE_BATCH_AXIS = {'x': 0, 'c': 0, 'positions': 0, 'loss_target': 0}
SHARED_INPUTS = []
_WEIGHT_DTYPES = {'w_ada': _jnp.float32, 'b_ada': _jnp.float32, 'norm1_w': _jnp.float32, 'w_in': _jnp.float32, 'conv_w': _jnp.float32, 'conv_b': _jnp.float32, 'dt_bias': _jnp.float32, 'a_log': _jnp.float32, 'd_skip': _jnp.float32, 'attn_sinks': _jnp.float32, 'ssm_norm_w': _jnp.float32, 'w_out': _jnp.float32, 'norm2_w': _jnp.float32, 'w_gate_up': _jnp.float32, 'w_down': _jnp.float32, 'final_norm_w': _jnp.float32}
MOMENT_SCALE = {'w_ada': 7.559983e-02, 'b_ada': 1.249351e-01, 'norm1_w': 7.869829e-02, 'w_in': 5.666148e-02, 'conv_w': 5.793973e-02, 'conv_b': 6.927860e-02, 'dt_bias': 1.507342e-01, 'a_log': 2.183354e-01, 'd_skip': 5.072762e-01, 'attn_sinks': 1.755924e-02, 'ssm_norm_w': 7.587390e-02, 'w_out': 5.705797e-02, 'norm2_w': 7.310372e-02, 'w_gate_up': 3.180807e-02, 'w_down': 5.196385e-02, 'final_norm_w': 6.400600e+01}


def _to_microbatches(a, axis):
    t = _jnp.moveaxis(a, axis, 0)
    t = t.reshape((N_MICROBATCH, t.shape[0] // N_MICROBATCH) + t.shape[1:])
    return _jnp.moveaxis(t, 1, axis + 1)


def setup_inputs(seed: int = 0) -> dict:
    inp = _fwd_setup_inputs(seed)
    key = _jax.random.fold_in(_jax.random.key(seed), 7919)
    shape, _ = _output_shape()
    out = dict(inp)
    out["loss_target"] = _jax.random.normal(_jax.random.fold_in(key, 0), shape, _jnp.float32)
    for i, name in enumerate(TWIN_WEIGHTS):
        w = inp[name].astype(_jnp.float32)
        if MOMENT_SCALE is None:
            s = _jnp.sqrt(_jnp.mean(_jnp.square(w)) + 1e-30)
        else:
            s = MOMENT_SCALE[name]
        km, kv = _jax.random.split(_jax.random.fold_in(key, i + 1))
        out[name] = w
        out["m_" + name] = s * _jax.random.normal(km, w.shape, _jnp.float32)
        out["v_" + name] = (s * s) * _jax.random.uniform(kv, w.shape, _jnp.float32, 0.5, 1.5)
    if N_MICROBATCH > 1:
        for name, axis in PER_EXAMPLE_BATCH_AXIS.items():
            out[name] = _to_microbatches(out[name], axis)
    return {'x': out['x'], 'c': out['c'], 'positions': out['positions'], 'w_ada': out['w_ada'], 'b_ada': out['b_ada'], 'norm1_w': out['norm1_w'], 'w_in': out['w_in'], 'conv_w': out['conv_w'], 'conv_b': out['conv_b'], 'dt_bias': out['dt_bias'], 'a_log': out['a_log'], 'd_skip': out['d_skip'], 'attn_sinks': out['attn_sinks'], 'ssm_norm_w': out['ssm_norm_w'], 'w_out': out['w_out'], 'norm2_w': out['norm2_w'], 'w_gate_up': out['w_gate_up'], 'w_down': out['w_down'], 'final_norm_w': out['final_norm_w'], 'loss_target': out['loss_target'], 'm_w_ada': out['m_w_ada'], 'm_b_ada': out['m_b_ada'], 'm_norm1_w': out['m_norm1_w'], 'm_w_in': out['m_w_in'], 'm_conv_w': out['m_conv_w'], 'm_conv_b': out['m_conv_b'], 'm_dt_bias': out['m_dt_bias'], 'm_a_log': out['m_a_log'], 'm_d_skip': out['m_d_skip'], 'm_attn_sinks': out['m_attn_sinks'], 'm_ssm_norm_w': out['m_ssm_norm_w'], 'm_w_out': out['m_w_out'], 'm_norm2_w': out['m_norm2_w'], 'm_w_gate_up': out['m_w_gate_up'], 'm_w_down': out['m_w_down'], 'm_final_norm_w': out['m_final_norm_w'], 'v_w_ada': out['v_w_ada'], 'v_b_ada': out['v_b_ada'], 'v_norm1_w': out['v_norm1_w'], 'v_w_in': out['v_w_in'], 'v_conv_w': out['v_conv_w'], 'v_conv_b': out['v_conv_b'], 'v_dt_bias': out['v_dt_bias'], 'v_a_log': out['v_a_log'], 'v_d_skip': out['v_d_skip'], 'v_attn_sinks': out['v_attn_sinks'], 'v_ssm_norm_w': out['v_ssm_norm_w'], 'v_w_out': out['v_w_out'], 'v_norm2_w': out['v_norm2_w'], 'v_w_gate_up': out['v_w_gate_up'], 'v_w_down': out['v_w_down'], 'v_final_norm_w': out['v_final_norm_w']}


def _loss(weights, diff, rest, loss_target):
    with _jax.named_scope("forward"):
        args = {**rest, TWIN_DIFF_INPUT: diff, **{k: w.astype(_WEIGHT_DTYPES[k]) for k, w in weights.items()}}
        y = _forward(args)
    with _jax.named_scope("loss_head"):
        err = _jnp.square(y.astype(_jnp.float32) - loss_target)
        return 0.5 * _jnp.sum(_jnp.mean(err, axis=-1)) if err.ndim else 0.5 * err


def _adamw(w, g, m, v):
    m = ADAM_B1 * m + (1.0 - ADAM_B1) * g
    v = ADAM_B2 * v + (1.0 - ADAM_B2) * _jnp.square(g)
    m_hat = m / (1.0 - ADAM_B1 ** ADAM_STEP)
    v_hat = v / (1.0 - ADAM_B2 ** ADAM_STEP)
    delta = -ADAM_LR * (m_hat / (_jnp.sqrt(v_hat) + ADAM_EPS) + ADAM_WD * w)
    return delta, m, v


def reference(x, c, positions, w_ada, b_ada, norm1_w, w_in, conv_w, conv_b, dt_bias, a_log, d_skip, attn_sinks, ssm_norm_w, w_out, norm2_w, w_gate_up, w_down, final_norm_w, loss_target, m_w_ada, m_b_ada, m_norm1_w, m_w_in, m_conv_w, m_conv_b, m_dt_bias, m_a_log, m_d_skip, m_attn_sinks, m_ssm_norm_w, m_w_out, m_norm2_w, m_w_gate_up, m_w_down, m_final_norm_w, v_w_ada, v_b_ada, v_norm1_w, v_w_in, v_conv_w, v_conv_b, v_dt_bias, v_a_log, v_d_skip, v_attn_sinks, v_ssm_norm_w, v_w_out, v_norm2_w, v_w_gate_up, v_w_down, v_final_norm_w):
    given = dict(x=x, c=c, positions=positions, w_ada=w_ada, b_ada=b_ada, norm1_w=norm1_w, w_in=w_in, conv_w=conv_w, conv_b=conv_b, dt_bias=dt_bias, a_log=a_log, d_skip=d_skip, attn_sinks=attn_sinks, ssm_norm_w=ssm_norm_w, w_out=w_out, norm2_w=norm2_w, w_gate_up=w_gate_up, w_down=w_down, final_norm_w=final_norm_w, loss_target=loss_target, m_w_ada=m_w_ada, m_b_ada=m_b_ada, m_norm1_w=m_norm1_w, m_w_in=m_w_in, m_conv_w=m_conv_w, m_conv_b=m_conv_b, m_dt_bias=m_dt_bias, m_a_log=m_a_log, m_d_skip=m_d_skip, m_attn_sinks=m_attn_sinks, m_ssm_norm_w=m_ssm_norm_w, m_w_out=m_w_out, m_norm2_w=m_norm2_w, m_w_gate_up=m_w_gate_up, m_w_down=m_w_down, m_final_norm_w=m_final_norm_w, v_w_ada=v_w_ada, v_b_ada=v_b_ada, v_norm1_w=v_norm1_w, v_w_in=v_w_in, v_conv_w=v_conv_w, v_conv_b=v_conv_b, v_dt_bias=v_dt_bias, v_a_log=v_a_log, v_d_skip=v_d_skip, v_attn_sinks=v_attn_sinks, v_ssm_norm_w=v_ssm_norm_w, v_w_out=v_w_out, v_norm2_w=v_norm2_w, v_w_gate_up=v_w_gate_up, v_w_down=v_w_down, v_final_norm_w=v_final_norm_w)
    weights = {n: given[n] for n in TWIN_WEIGHTS}
    shared = {n: given[n] for n in SHARED_INPUTS}
    per_example = {n: given[n] for n in ['x', 'c', 'positions']}
    grad_fn = _jax.value_and_grad(_loss, argnums=(0, 1))

    def one_microbatch(ex, loss_target):
        ex = dict(ex)
        diff = ex.pop(TWIN_DIFF_INPUT)
        return grad_fn(weights, diff, {**shared, **ex}, loss_target)

    if N_MICROBATCH == 1:
        loss, (grad_w, grad_x) = one_microbatch(per_example, given["loss_target"])
    else:
        def body(carry, xs):
            loss_sum, grad_sum = carry
            l_k, (gw_k, gx_k) = one_microbatch(xs[0], xs[1])
            with _jax.named_scope("update"):
                return (loss_sum + l_k, _jax.tree.map(_jnp.add, grad_sum, gw_k)), gx_k

        init = (_jnp.zeros((), _jnp.float32), _jax.tree.map(_jnp.zeros_like, weights))
        (loss, grad_w), grad_x = _jax.lax.scan(body, init, (per_example, given["loss_target"]))
    with _jax.named_scope("update"):
        delta_w, new_m, new_v = {}, {}, {}
        for n in TWIN_WEIGHTS:
            delta_w[n], new_m[n], new_v[n] = _adamw(weights[n], grad_w[n], given["m_" + n], given["v_" + n])
    return (loss, grad_x, *[grad_w[n] for n in TWIN_WEIGHTS], *[delta_w[n] for n in TWIN_WEIGHTS],
            *[new_m[n] for n in TWIN_WEIGHTS], *[new_v[n] for n in TWIN_WEIGHTS])
```

```python
import functools
import math

import jax
import jax.numpy as jnp
from jax import lax
from jax.experimental import pallas as pl
from jax.experimental.pallas import tpu as pltpu

F32 = jnp.float32
BF16 = jnp.bfloat16
HI = lax.Precision.HIGHEST
MESH = pl.DeviceIdType.MESH

D = 1024
HD = 64
NQ = 8
AW = 512
KVW = 128
SW = 512
NST = 128
CONVK = 4
CONVC = 1024
BLK = 128
IN_PROJ = 2312
IN_PAD = 2432
DFF = 2816
GU_SH = 1408
EPS = 1e-6
NEG = -1e30
LR, B1, B2, AEPS, WD, STEP = 0.001, 0.9, 0.999, 1e-08, 0.01, 10
VMEM_LIMIT = 58 * 1024 * 1024


def _cp(*sem):
    return pltpu.CompilerParams(dimension_semantics=sem or None, vmem_limit_bytes=VMEM_LIMIT)


def _dot(a, b):
    return jnp.dot(a, b, preferred_element_type=F32)


def _dot_nt(a, b):
    return lax.dot_general(a, b, (((1,), (1,)), ((), ())), preferred_element_type=F32)


def _dot_tn(a, b):
    return lax.dot_general(a, b, (((0,), (0,)), ((), ())), preferred_element_type=F32)


def _dot_hi(a, b):
    return jnp.dot(a, b, precision=HI, preferred_element_type=F32)


def _sigmoid(x):
    return 1.0 / (1.0 + jnp.exp(-x))


def _iota(shape, dim):
    return lax.broadcasted_iota(jnp.int32, shape, dim)


def _load_resident(hbm_ref, vmem_ref, sem):
    @pl.when(pl.program_id(0) == 0)
    def _():
        cp = pltpu.make_async_copy(hbm_ref, vmem_ref, sem)
        cp.start()
        cp.wait()


def _swap32(t):
    lane = _iota(t.shape, 1)
    return jnp.where((lane & 63) < 32, pltpu.roll(t, 96, 1), pltpu.roll(t, 32, 1))


def _rope_fwd(t, cos, sin_s):
    return t * cos + _swap32(t) * sin_s


def _rope_bwd(t, cos, sin_s):
    return t * cos - _swap32(t) * sin_s


def _rope_tables(pos_col, inv_freq_row, tm):
    T = pos_col.shape[0]

    def body(p_ref, f_ref, cos_ref, sin_ref):
        ang = p_ref[...].astype(F32) * f_ref[...]
        lane = _iota((tm, 128), 1)
        s = jnp.sin(ang)
        cos_ref[...] = jnp.cos(ang)
        sin_ref[...] = jnp.where((lane & 63) < 32, -s, s)

    return pl.pallas_call(
        body, name="rope_tables", grid=(T // tm,),
        in_specs=[pl.BlockSpec((tm, 1), lambda i: (i, 0)), pl.BlockSpec((1, 128), lambda i: (0, 0))],
        out_specs=[pl.BlockSpec((tm, 128), lambda i: (i, 0))] * 2,
        out_shape=[jax.ShapeDtypeStruct((T, 128), F32)] * 2,
        compiler_params=_cp("parallel"),
    )(pos_col, inv_freq_row)


def _in_proj_fwd(x, cos, sin_s, mod6, norm1_w, w_pad, tm):
    T = x.shape[0]

    def body(x_ref, cos_ref, sin_ref, mod_ref, nw_ref, w_hbm, qkv_ref, z_ref, xbc_ref, dt_ref, h_ref, w_vmem, sem):
        _load_resident(w_hbm, w_vmem, sem)
        xv = x_ref[...]
        r = lax.rsqrt(jnp.mean(xv * xv, axis=-1, keepdims=True) + EPS)
        h = (xv * r * nw_ref[...]) * (1.0 + mod_ref[1:2, :]) + mod_ref[0:1, :]
        hb = h.astype(BF16)
        h_ref[...] = hb
        proj = _dot(hb, w_vmem[...])
        cs, sn = cos_ref[...], sin_ref[...]
        for j in range(5):
            qkv_ref[:, 128 * j:128 * (j + 1)] = _rope_fwd(proj[:, 128 * j:128 * (j + 1)], cs, sn).astype(BF16)
        qkv_ref[:, 640:768] = proj[:, 640:768].astype(BF16)
        z_ref[...] = proj[:, 768:1280]
        xbc_ref[...] = proj[:, 1280:2304]
        dt_ref[...] = proj[:, 2304:2432]

    row = lambda w: pl.BlockSpec((tm, w), lambda i: (i, 0))
    full = lambda a: pl.BlockSpec(a.shape, lambda i: (0,) * a.ndim)
    return pl.pallas_call(
        body, name="in_proj_fwd", grid=(T // tm,),
        in_specs=[row(D), row(128), row(128), full(mod6), full(norm1_w), pl.BlockSpec(memory_space=pl.ANY)],
        out_specs=[row(768), row(512), row(1024), row(128), row(D)],
        out_shape=[jax.ShapeDtypeStruct((T, 768), BF16), jax.ShapeDtypeStruct((T, 512), F32),
                   jax.ShapeDtypeStruct((T, 1024), F32), jax.ShapeDtypeStruct((T, 128), F32),
                   jax.ShapeDtypeStruct((T, D), BF16)],
        scratch_shapes=[pltpu.VMEM((D, IN_PAD), BF16), pltpu.SemaphoreType.DMA],
        compiler_params=_cp("arbitrary"),
    )(x, cos, sin_s, mod6, norm1_w, w_pad)


def _head_variants(pair, j):
    lane = _iota(pair.shape, 1)
    lo = lane < 64
    kv = j // 2
    ev = jnp.where(lo, pair, 0.0)
    od = jnp.where(lo, 0.0, pair)
    if kv == 0:
        od = pltpu.roll(od, 64, 1)
    else:
        ev = pltpu.roll(ev, 64, 1)
    return ev.astype(BF16), od.astype(BF16)


def _kv_variants(vcat):
    lane = _iota(vcat.shape, 1)
    lo = lane < 64
    v0 = jnp.where(lo, vcat, 0.0)
    v1 = jnp.where(lo, 0.0, vcat)
    out = {
        (0, 0): v0, (0, 1): pltpu.roll(v0, 64, 1),
        (1, 0): pltpu.roll(v1, 64, 1), (1, 1): v1,
    }
    return {k: v.astype(BF16) for k, v in out.items()}


def _attn_mask(n):
    i = _iota((BLK, 2 * BLK), 0)
    j = _iota((BLK, 2 * BLK), 1)
    return (j > i) & (j <= i + BLK) & ((n > 0) | (j >= BLK))


def _attn_fwd(qkv, sinks):
    T = qkv.shape[0]
    nb = T // BLK

    def body(sink_ref, q_ref, kc_ref, kp_ref, vc_ref, vp_ref, o_ref, lse_ref):
        n = pl.program_id(0)
        valid = _attn_mask(n)
        kcat = jnp.concatenate([kp_ref[...], kc_ref[...]], axis=0)
        vvar = _kv_variants(jnp.concatenate([vp_ref[...], vc_ref[...]], axis=0).astype(F32))
        lane = _iota((BLK, 128), 1)
        lse_acc = jnp.zeros((BLK, 128), F32)
        for j in range(4):
            qv = _head_variants(q_ref[:, 128 * j:128 * (j + 1)].astype(F32), j)
            acc = jnp.zeros((BLK, 128), F32)
            for par in range(2):
                h = 2 * j + par
                sink = sink_ref[0, h]
                s = jnp.where(valid, _dot_nt(qv[par], kcat) * 0.125, NEG)
                m = jnp.maximum(jnp.max(s, axis=1, keepdims=True), sink)
                p = jnp.exp(s - m)
                den = jnp.sum(p, axis=1, keepdims=True) + jnp.exp(sink - m)
                probs = (p * (1.0 / den)).astype(BF16)
                acc = acc + _dot(probs, vvar[(j // 2, par)])
                lse_acc = jnp.where(lane == h, m + jnp.log(den), lse_acc)
            o_ref[:, 128 * j:128 * (j + 1)] = acc.astype(BF16)
        lse_ref[...] = lse_acc

    prev = lambda n: jnp.maximum(n - 1, 0)
    return pl.pallas_call(
        body, name="attn_fwd", grid=(nb,),
        in_specs=[pl.BlockSpec(memory_space=pltpu.SMEM),
                  pl.BlockSpec((BLK, 512), lambda n: (n, 0)),
                  pl.BlockSpec((BLK, 128), lambda n: (n, 4)),
                  pl.BlockSpec((BLK, 128), lambda n: (prev(n), 4)),
                  pl.BlockSpec((BLK, 128), lambda n: (n, 5)),
                  pl.BlockSpec((BLK, 128), lambda n: (prev(n), 5))],
        out_specs=[pl.BlockSpec((BLK, 512), lambda n: (n, 0)), pl.BlockSpec((BLK, 128), lambda n: (n, 0))],
        out_shape=[jax.ShapeDtypeStruct((T, 512), BF16), jax.ShapeDtypeStruct((T, 128), F32)],
        compiler_params=_cp("parallel"),
    )(sinks, qkv, qkv, qkv, qkv, qkv)


def _attn_bwd(qkv, sinks, lse, dmix, cos, sin_s):
    T = qkv.shape[0]
    nb = T // BLK

    def body(sink_ref, q_ref, kc_ref, kp_ref, vc_ref, vp_ref, lse_ref, do_ref, cq_ref, sq_ref, ck_ref, sk_ref,
             dq_ref, dk_ref, dv_ref, ds_ref, dk_car, dv_car):
        n = pl.program_id(0)
        lane = _iota((BLK, 128), 1)

        @pl.when(n == 0)
        def _():
            ds_ref[...] = jnp.zeros_like(ds_ref)
            dk_car[...] = jnp.zeros_like(dk_car)
            dv_car[...] = jnp.zeros_like(dv_car)

        @pl.when(n < nb)
        def _():
            valid = _attn_mask(n)
            kcat = jnp.concatenate([kp_ref[...], kc_ref[...]], axis=0)
            vcat = jnp.concatenate([vp_ref[...], vc_ref[...]], axis=0)
            kvar = _kv_variants(kcat.astype(F32))
            lse_v = lse_ref[...]
            dkc = jnp.zeros((2 * BLK, 128), F32)
            dvc = jnp.zeros((2 * BLK, 128), F32)
            dsk = jnp.zeros((1, 128), F32)
            for j in range(4):
                qv = _head_variants(q_ref[:, 128 * j:128 * (j + 1)].astype(F32), j)
                dov = _head_variants(do_ref[:, 128 * j:128 * (j + 1)], j)
                dq_acc = jnp.zeros((BLK, 128), F32)
                for par in range(2):
                    h = 2 * j + par
                    sink = sink_ref[0, h]
                    lse_h = jnp.sum(jnp.where(lane == h, lse_v, 0.0), axis=1, keepdims=True)
                    s = jnp.where(valid, _dot_nt(qv[par], kcat) * 0.125, NEG)
                    p = jnp.exp(s - lse_h)
                    dp = _dot_nt(dov[par], vcat)
                    delta = jnp.sum(p * dp, axis=1, keepdims=True)
                    dsc = (p * (dp - delta) * 0.125).astype(BF16)
                    dq_acc = dq_acc + _dot(dsc, kvar[(j // 2, par)])
                    dkc = dkc + _dot_tn(dsc, qv[par])
                    dvc = dvc + _dot_tn(p.astype(BF16), dov[par])
                    psink = jnp.exp(sink - lse_h)
                    dsk = dsk + jnp.where(lane[0:1] == h, -jnp.sum(psink * delta), 0.0)
                dq_ref[:, 128 * j:128 * (j + 1)] = _rope_bwd(dq_acc, cq_ref[...], sq_ref[...]).astype(BF16)
            ds_ref[...] += dsk
            dk_ref[...] = _rope_bwd(dk_car[...] + dkc[:BLK], ck_ref[...], sk_ref[...]).astype(BF16)
            dv_ref[...] = (dv_car[...] + dvc[:BLK]).astype(BF16)
            dk_car[...] = dkc[BLK:]
            dv_car[...] = dvc[BLK:]

        @pl.when(n == nb)
        def _():
            dk_ref[...] = _rope_bwd(dk_car[...], ck_ref[...], sk_ref[...]).astype(BF16)
            dv_ref[...] = dv_car[...].astype(BF16)

    cur = lambda n: jnp.minimum(n, nb - 1)
    prev = lambda n: jnp.maximum(cur(n) - 1, 0)
    outb = lambda n: jnp.maximum(n - 1, 0)
    return pl.pallas_call(
        body, name="attn_bwd", grid=(nb + 1,),
        in_specs=[pl.BlockSpec(memory_space=pltpu.SMEM),
                  pl.BlockSpec((BLK, 512), lambda n: (cur(n), 0)),
                  pl.BlockSpec((BLK, 128), lambda n: (cur(n), 4)),
                  pl.BlockSpec((BLK, 128), lambda n: (prev(n), 4)),
                  pl.BlockSpec((BLK, 128), lambda n: (cur(n), 5)),
                  pl.BlockSpec((BLK, 128), lambda n: (prev(n), 5)),
                  pl.BlockSpec((BLK, 128), lambda n: (cur(n), 0)),
                  pl.BlockSpec((BLK, 512), lambda n: (cur(n), 0)),
                  pl.BlockSpec((BLK, 128), lambda n: (cur(n), 0)),
                  pl.BlockSpec((BLK, 128), lambda n: (cur(n), 0)),
                  pl.BlockSpec((BLK, 128), lambda n: (outb(n), 0)),
                  pl.BlockSpec((BLK, 128), lambda n: (outb(n), 0))],
        out_specs=[pl.BlockSpec((BLK, 512), lambda n: (cur(n), 0)),
                   pl.BlockSpec((BLK, 128), lambda n: (outb(n), 0)),
                   pl.BlockSpec((BLK, 128), lambda n: (outb(n), 0)),
                   pl.BlockSpec((1, 128), lambda n: (0, 0))],
        out_shape=[jax.ShapeDtypeStruct((T, 512), BF16), jax.ShapeDtypeStruct((T, 128), BF16),
                   jax.ShapeDtypeStruct((T, 128), BF16), jax.ShapeDtypeStruct((1, 128), F32)],
        scratch_shapes=[pltpu.VMEM((BLK, 128), F32), pltpu.VMEM((BLK, 128), F32)],
        compiler_params=_cp("arbitrary"),
    )(sinks, qkv, qkv, qkv, qkv, qkv, lse, dmix, cos, sin_s, cos, sin_s)


def _expand_mat():
    return (_iota((128, SW), 1) // HD == _iota((128, SW), 0)).astype(F32)


def _expand_mat_t():
    return (_iota((SW, 128), 0) // HD == _iota((SW, 128), 1)).astype(F32)


def _conv_shifts(u, up):
    row = _iota(u.shape, 0)
    out = [u]
    for j in range(1, CONVK):
        out.append(jnp.where(row < j, pltpu.roll(up, j, 0), pltpu.roll(u, j, 0)))
    return out


def _ssd_parts(u, up, cw_ref, cb_ref, dtr, dtb, alog):
    sh = _conv_shifts(u, up)
    co = cb_ref[...] + cw_ref[3:4, :] * sh[0]
    for j in range(1, CONVK):
        co = co + cw_ref[3 - j:4 - j, :] * sh[j]
    sg = _sigmoid(co)
    xc = co * sg
    xx = dtr + dtb
    dt = jnp.maximum(xx, 0.0) + jnp.log(1.0 + jnp.exp(-jnp.abs(xx)))
    a_neg = -jnp.exp(alog)
    tril = _iota((BLK, BLK), 1) <= _iota((BLK, BLK), 0)
    cs = _dot_hi(tril.astype(F32), dt * a_neg)
    e_mat = _expand_mat()
    csx = _dot_hi(cs, e_mat)
    last = csx[BLK - 1:BLK, :]
    return dict(sh=sh, co=co, sg=sg, xc=xc, xx=xx, dt=dt, a_neg=a_neg, tril=tril, cs=cs, cs_t=cs.T,
                ecsx=jnp.exp(csx), dtex=jnp.exp(last - csx), cdx=jnp.exp(last), dtx=_dot_hi(dt, e_mat))


def _decay(parts, h):
    seg = parts["cs"][:, h:h + 1] - parts["cs_t"][h:h + 1, :]
    return jnp.exp(jnp.where(parts["tril"], seg, NEG))


def _group_cols(a, g):
    return a[:, 256 * g:256 * (g + 1)]


def _ssd_fwd(xbc, z, dtr, conv_w, conv_b, dtb, alog, dskx, ssm_w):
    T = xbc.shape[0]
    nc = T // BLK

    def body(u_ref, up_ref, z_ref, dtr_ref, cw_ref, cb_ref, dtb_ref, al_ref, dk_ref, sw_ref,
             yn_ref, yp_ref, st_ref, s_scr):
        n = pl.program_id(0)

        @pl.when(n == 0)
        def _():
            s_scr[...] = jnp.zeros_like(s_scr)

        u = u_ref[...]
        up = jnp.where(n > 0, up_ref[...], 0.0)
        pt = _ssd_parts(u, up, cw_ref, cb_ref, dtr_ref[...], dtb_ref[...], al_ref[...])
        xc = pt["xc"]
        xs = xc[:, :SW]
        bm = [xc[:, 512:640].astype(BF16), xc[:, 640:768].astype(BF16)]
        cm = [xc[:, 768:896].astype(BF16), xc[:, 896:1024].astype(BF16)]
        s_in = s_scr[...]
        st_ref[0] = s_in
        xdt = xs * pt["dtx"]
        xde = (xdt * pt["dtex"]).astype(BF16)
        lane = _iota((BLK, 128), 1)
        lo = lane < 64
        ys, s_new = [], []
        for g in range(2):
            cb = _dot_nt(cm[g], bm[g])
            yoff = _dot(cm[g], _group_cols(s_in, g).astype(BF16))
            s_new.append(_dot_tn(bm[g], _group_cols(xde, g)))
            for jj in range(2):
                j = 2 * g + jj
                chunk = xdt[:, 128 * j:128 * (j + 1)]
                g_ev = (cb * _decay(pt, 2 * j)).astype(BF16)
                g_od = (cb * _decay(pt, 2 * j + 1)).astype(BF16)
                yd = _dot(g_ev, jnp.where(lo, chunk, 0.0).astype(BF16)) + _dot(g_od, jnp.where(lo, 0.0, chunk).astype(BF16))
                ys.append(yd + yoff[:, 128 * jj:128 * (jj + 1)] * pt["ecsx"][:, 128 * j:128 * (j + 1)])
        y = jnp.concatenate(ys, axis=1) + xs * dk_ref[...]
        s_scr[...] = s_in * pt["cdx"] + jnp.concatenate(s_new, axis=1)
        yp_ref[...] = y
        zv = z_ref[...]
        yz = y * (zv * _sigmoid(zv))
        outs = []
        for g in range(2):
            yg = _group_cols(yz, g)
            outs.append(yg * lax.rsqrt(jnp.mean(yg * yg, axis=-1, keepdims=True) + EPS))
        yn_ref[...] = (jnp.concatenate(outs, axis=1) * sw_ref[...]).astype(BF16)

    prev = lambda n: jnp.maximum(n - 1, 0)
    full = lambda a: pl.BlockSpec(a.shape, lambda n: (0,) * a.ndim)
    return pl.pallas_call(
        body, name="ssd_fwd", grid=(nc,),
        in_specs=[pl.BlockSpec((BLK, CONVC), lambda n: (n, 0)), pl.BlockSpec((BLK, CONVC), lambda n: (prev(n), 0)),
                  pl.BlockSpec((BLK, SW), lambda n: (n, 0)), pl.BlockSpec((BLK, 128), lambda n: (n, 0)),
                  full(conv_w), full(conv_b), full(dtb), full(alog), full(dskx), full(ssm_w)],
        out_specs=[pl.BlockSpec((BLK, SW), lambda n: (n, 0)), pl.BlockSpec((BLK, SW), lambda n: (n, 0)),
                   pl.BlockSpec((1, NST, SW), lambda n: (n, 0, 0))],
        out_shape=[jax.ShapeDtypeStruct((T, SW), BF16), jax.ShapeDtypeStruct((T, SW), F32),
                   jax.ShapeDtypeStruct((nc, NST, SW), F32)],
        scratch_shapes=[pltpu.VMEM((NST, SW), F32)],
        compiler_params=_cp("arbitrary"),
    )(xbc, xbc, z, dtr, conv_w, conv_b, dtb, alog, dskx, ssm_w)


def _ssd_bwd(xbc, z, dtr, ypre, states, dmix, conv_w, conv_b, dtb, alog, dskx, ssm_w):
    T = xbc.shape[0]
    nc = T // BLK

    def body(u_ref, up_ref, z_ref, dtr_ref, yp_ref, st_ref, dyn_ref, cw_ref, cb_ref, dtb_ref, al_ref, dk_ref, sw_ref,
             out_ref, dcw_ref, dcb_ref, dsw_ref, dsk_ref, ddtb_ref, dav_ref, ds_scr, dco_scr, dskx_scr):
        i = pl.program_id(0)
        n = nc - 1 - i

        @pl.when(i == 0)
        def _():
            for r in (dcw_ref, dcb_ref, dsw_ref, dsk_ref, ddtb_ref, dav_ref, ds_scr, dco_scr, dskx_scr):
                r[...] = jnp.zeros_like(r)

        u = u_ref[...]
        up = jnp.where(n > 0, up_ref[...], 0.0)
        pt = _ssd_parts(u, up, cw_ref, cb_ref, dtr_ref[...], dtb_ref[...], al_ref[...])
        xc, dtx, ecsx, dtex, cdx = pt["xc"], pt["dtx"], pt["ecsx"], pt["dtex"], pt["cdx"]
        xs = xc[:, :SW]
        bm = [xc[:, 512:640].astype(BF16), xc[:, 640:768].astype(BF16)]
        cm = [xc[:, 768:896].astype(BF16), xc[:, 896:1024].astype(BF16)]
        s_in = st_ref[0]
        ds_out = ds_scr[...]
        e_t = _expand_mat_t()

        zv = z_ref[...]
        sz = _sigmoid(zv)
        silu_z = zv * sz
        ypre = yp_ref[...]
        yz = ypre * silu_z
        dyn = dyn_ref[...]
        sw = sw_ref[...]
        dyz, yns = [], []
        for g in range(2):
            yg = _group_cols(yz, g)
            r = lax.rsqrt(jnp.mean(yg * yg, axis=-1, keepdims=True) + EPS)
            yn = yg * r
            dg = _group_cols(dyn, g) * _group_cols(sw, g)
            dyz.append(r * (dg - yn * jnp.mean(dg * yn, axis=-1, keepdims=True)))
            yns.append(yn)
        dyz = jnp.concatenate(dyz, axis=1)
        dsw_ref[...] += jnp.sum(dyn * jnp.concatenate(yns, axis=1), axis=0, keepdims=True)
        dy = dyz * silu_z
        dz = dyz * ypre * (sz * (1.0 + zv * (1.0 - sz)))

        xdt = xs * dtx
        xdt_b = xdt.astype(BF16)
        edy = (ecsx * dy).astype(BF16)
        xde = (xdt * dtex).astype(BF16)
        lane = _iota((BLK, 128), 1)
        lo = lane < 64
        row8 = _iota((8, 128), 0)
        dcs = jnp.zeros((BLK, 128), F32)
        col_rows = jnp.zeros((8, 128), F32)
        dxdt, bds, yoff, dbs, dcs_g, ds_new = [], [], [], [], [], []
        for g in range(2):
            s_g = _group_cols(s_in, g).astype(BF16)
            dso_g = _group_cols(ds_out, g).astype(BF16)
            cb = _dot_nt(cm[g], bm[g])
            bds.append(_dot(bm[g], dso_g))
            yoff.append(_dot(cm[g], s_g))
            dcb_g = jnp.zeros((BLK, BLK), F32)
            for jj in range(2):
                j = 2 * g + jj
                dy_c = dy[:, 128 * j:128 * (j + 1)]
                xdt_c = xdt_b[:, 128 * j:128 * (j + 1)]
                acc = jnp.zeros((BLK, 128), F32)
                for par in range(2):
                    h = 2 * j + par
                    lm = _decay(pt, h)
                    gm = cb * lm
                    dy_m = (jnp.where(lo, dy_c, 0.0) if par == 0 else jnp.where(lo, 0.0, dy_c)).astype(BF16)
                    dg_h = _dot_nt(dy_m, xdt_c)
                    w_h = dg_h * gm
                    dcs = dcs + jnp.where(lane == h, jnp.sum(w_h, axis=1, keepdims=True), 0.0)
                    col_rows = col_rows + jnp.where(row8 == h, jnp.sum(w_h, axis=0, keepdims=True), 0.0)
                    dcb_g = dcb_g + dg_h * lm
                    acc = acc + _dot_tn(gm.astype(BF16), dy_m)
                dxdt.append(acc)
            dcb_b = dcb_g.astype(BF16)
            dcs_g.append(_dot(dcb_b, bm[g]) + _dot_nt(_group_cols(edy, g), s_g))
            dbs.append(_dot_tn(dcb_b, cm[g]) + _dot_nt(_group_cols(xde, g), dso_g))
            ds_new.append(_dot_tn(cm[g], _group_cols(edy, g)))
        bds = jnp.concatenate(bds, axis=1)
        yoff = jnp.concatenate(yoff, axis=1) * ecsx
        dxdt = jnp.concatenate(dxdt, axis=1) + dtex * bds
        ds_scr[...] = cdx * ds_out + jnp.concatenate(ds_new, axis=1)

        t_m = _dot_hi(dtex * xdt * bds, e_t)
        colsum_t = jnp.concatenate([col_rows, jnp.zeros((BLK - 8, 128), F32)], axis=0).T
        cd = jnp.exp(pt["cs"][BLK - 1:BLK, :])
        sds = jnp.sum(s_in * ds_out, axis=0, keepdims=True)
        last_row = jnp.sum(t_m, axis=0, keepdims=True) + cd * _dot_hi(jnp.broadcast_to(sds, (8, SW)), e_t)[0:1]
        dcs = dcs - colsum_t + _dot_hi(dy * yoff, e_t) - t_m
        dcs = dcs + jnp.where(_iota((BLK, 128), 0) == BLK - 1, last_row, 0.0)
        triu = (_iota((BLK, BLK), 1) >= _iota((BLK, BLK), 0)).astype(F32)
        da = _dot_hi(triu, dcs)
        dt = pt["dt"]
        ddt = da * pt["a_neg"] + _dot_hi(dxdt * xs, e_t)
        dav_ref[...] += jnp.sum(da * dt, axis=0, keepdims=True)
        ddtr = ddt * _sigmoid(pt["xx"])
        ddtb_ref[...] += jnp.sum(ddtr, axis=0, keepdims=True)
        dxs = dxdt * dtx + dy * dk_ref[...]
        dskx_scr[...] += jnp.sum(dy * xs, axis=0, keepdims=True)
        dxc = jnp.concatenate([dxs, dbs[0], dbs[1], dcs_g[0], dcs_g[1]], axis=1)
        co, sg = pt["co"], pt["sg"]
        dco = dxc * (sg * (1.0 + co * (1.0 - sg)))

        dcb_ref[...] += jnp.sum(dco, axis=0, keepdims=True)
        sh = pt["sh"]
        for j in range(CONVK):
            dcw_ref[3 - j:4 - j, :] += jnp.sum(dco * sh[j], axis=0, keepdims=True)
        dnext = dco_scr[...]
        rowc = _iota(dco.shape, 0)
        du = cw_ref[3:4, :] * dco
        for j in range(1, CONVK):
            up_j = jnp.where(rowc >= BLK - j, pltpu.roll(dnext, BLK - j, 0), pltpu.roll(dco, BLK - j, 0))
            du = du + cw_ref[3 - j:4 - j, :] * up_j
        dco_scr[...] = dco
        out_ref[:, 0:512] = dz.astype(BF16)
        out_ref[:, 512:1536] = du.astype(BF16)
        out_ref[:, 1536:1664] = ddtr.astype(BF16)

        @pl.when(i == nc - 1)
        def _():
            dsk_ref[...] = _dot_hi(jnp.broadcast_to(dskx_scr[...], (8, SW)), e_t)[0:1]

    rev = lambda i: nc - 1 - i
    prev = lambda i: jnp.maximum(nc - 2 - i, 0)
    full = lambda a: pl.BlockSpec(a.shape, lambda i: (0,) * a.ndim)
    acc = lambda r, c: pl.BlockSpec((r, c), lambda i: (0, 0))
    return pl.pallas_call(
        body, name="ssd_bwd", grid=(nc,),
        in_specs=[pl.BlockSpec((BLK, CONVC), lambda i: (rev(i), 0)), pl.BlockSpec((BLK, CONVC), lambda i: (prev(i), 0)),
                  pl.BlockSpec((BLK, SW), lambda i: (rev(i), 0)), pl.BlockSpec((BLK, 128), lambda i: (rev(i), 0)),
                  pl.BlockSpec((BLK, SW), lambda i: (rev(i), 0)), pl.BlockSpec((1, NST, SW), lambda i: (rev(i), 0, 0)),
                  pl.BlockSpec((BLK, SW), lambda i: (rev(i), 1)),
                  full(conv_w), full(conv_b), full(dtb), full(alog), full(dskx), full(ssm_w)],
        out_specs=[pl.BlockSpec((BLK, 1664), lambda i: (rev(i), 0)),
                   acc(CONVK, CONVC), acc(1, CONVC), acc(1, SW), acc(1, 128), acc(1, 128), acc(1, 128)],
        out_shape=[jax.ShapeDtypeStruct((T, 1664), BF16),
                   jax.ShapeDtypeStruct((CONVK, CONVC), F32), jax.ShapeDtypeStruct((1, CONVC), F32),
                   jax.ShapeDtypeStruct((1, SW), F32), jax.ShapeDtypeStruct((1, 128), F32),
                   jax.ShapeDtypeStruct((1, 128), F32), jax.ShapeDtypeStruct((1, 128), F32)],
        scratch_shapes=[pltpu.VMEM((NST, SW), F32), pltpu.VMEM((BLK, CONVC), F32), pltpu.VMEM((1, SW), F32)],
        compiler_params=_cp("arbitrary"),
    )(xbc, xbc, z, dtr, ypre, states, dmix, conv_w, conv_b, dtb, alog, dskx, ssm_w)


def _mix_ffn(x, attn, ynorm, tgt, mod6, norm2_w, final_w, w_out, w_gu, w_dn, tm):
    T = x.shape[0]
    nt = T // tm

    def body(x_ref, a_ref, y_ref, t_ref, mod_ref, n2_ref, fw_ref, wo_hbm, wgu_hbm, wdn_hbm,
             sq_ref, dmix_ref, dx1_ref, h2_ref, act_ref, df_ref, dgu_ref, do_ref, sm_ref,
             wo, wgu, wdn, sems):
        i = pl.program_id(0)

        @pl.when(i == 0)
        def _():
            cps = [pltpu.make_async_copy(s, d, sems.at[k]) for k, (s, d) in
                   enumerate(((wo_hbm, wo), (wgu_hbm, wgu), (wdn_hbm, wdn)))]
            for c in cps:
                c.start()
            for c in cps:
                c.wait()
            sq_ref[...] = jnp.zeros_like(sq_ref)
            sm_ref[...] = jnp.zeros_like(sm_ref)

        gate1, shift2, scale2, gate2 = mod_ref[2:3, :], mod_ref[3:4, :], mod_ref[4:5, :], mod_ref[5:6, :]
        n2w, fw = n2_ref[...], fw_ref[...]
        o = _dot(a_ref[...], wo[0:AW, :]) + _dot(y_ref[...], wo[AW:D, :])
        x1 = x_ref[...] + gate1 * o
        r2 = lax.rsqrt(jnp.mean(x1 * x1, axis=-1, keepdims=True) + EPS)
        xh2 = x1 * r2
        n2 = xh2 * n2w
        h2b = (n2 * (1.0 + scale2) + shift2).astype(BF16)
        h2_ref[...] = h2b
        f = jnp.zeros((tm, D), F32)
        saved = []
        for p in range(2):
            gp = _dot(h2b, wgu[p])
            upj = _dot(h2b, wgu[p + 2])
            sg = _sigmoid(gp)
            sl = gp * sg
            actb = (sl * upj).astype(BF16)
            act_ref[p] = actb
            f = f + _dot(actb, wdn[GU_SH * p:GU_SH * (p + 1), :])
            saved.append((gp, upj, sg, sl))
        x2 = x1 + gate2 * f
        r3 = lax.rsqrt(jnp.mean(x2 * x2, axis=-1, keepdims=True) + EPS)
        xh3 = x2 * r3
        err = xh3 * fw - t_ref[...]
        sq_ref[...] += jnp.sum(err * err, axis=0, keepdims=True)
        dy = err * (1.0 / D)
        dfw = jnp.sum(dy * xh3, axis=0, keepdims=True)
        dxh3 = dy * fw
        dx2 = r3 * (dxh3 - xh3 * jnp.mean(dxh3 * xh3, axis=-1, keepdims=True))
        dgate2 = jnp.sum(dx2 * f, axis=0, keepdims=True)
        dfb = (dx2 * gate2).astype(BF16)
        df_ref[...] = dfb
        dh2 = jnp.zeros((tm, D), F32)
        for p in range(2):
            gp, upj, sg, sl = saved[p]
            dact = _dot_nt(dfb, wdn[GU_SH * p:GU_SH * (p + 1), :])
            dg = (dact * upj * (sg * (1.0 + gp * (1.0 - sg)))).astype(BF16)
            du = (dact * sl).astype(BF16)
            dgu_ref[p] = dg
            dgu_ref[p + 2] = du
            dh2 = dh2 + _dot_nt(dg, wgu[p]) + _dot_nt(du, wgu[p + 2])
        dshift2 = jnp.sum(dh2, axis=0, keepdims=True)
        dscale2 = jnp.sum(dh2 * n2, axis=0, keepdims=True)
        dn2 = dh2 * (1.0 + scale2)
        dn2w = jnp.sum(dn2 * xh2, axis=0, keepdims=True)
        dxh2 = dn2 * n2w
        dx1 = dx2 + r2 * (dxh2 - xh2 * jnp.mean(dxh2 * xh2, axis=-1, keepdims=True))
        dx1_ref[...] = dx1
        dgate1 = jnp.sum(dx1 * o, axis=0, keepdims=True)
        dob = (dx1 * gate1).astype(BF16)
        do_ref[...] = dob
        dmix_ref[...] = _dot_nt(dob, wo[...])
        sm_ref[...] += jnp.concatenate(
            [dfw, dn2w, dshift2, dscale2, dgate2, dgate1, jnp.zeros((2, D), F32)], axis=0)

    row = lambda w: pl.BlockSpec((tm, w), lambda i: (i, 0))
    full = lambda a: pl.BlockSpec(a.shape, lambda i: (0,) * a.ndim)
    anyspec = pl.BlockSpec(memory_space=pl.ANY)
    return pl.pallas_call(
        body, name="mix_ffn", grid=(nt,),
        in_specs=[row(D), row(AW), row(SW), row(D), full(mod6), full(norm2_w), full(final_w), anyspec, anyspec, anyspec],
        out_specs=[pl.BlockSpec((1, D), lambda i: (0, 0)), row(D), row(D), row(D),
                   pl.BlockSpec((2, tm, GU_SH), lambda i: (0, i, 0)), row(D),
                   pl.BlockSpec((4, tm, GU_SH), lambda i: (0, i, 0)), row(D),
                   pl.BlockSpec((8, D), lambda i: (0, 0))],
        out_shape=[jax.ShapeDtypeStruct((1, D), F32), jax.ShapeDtypeStruct((T, D), F32), jax.ShapeDtypeStruct((T, D), F32),
                   jax.ShapeDtypeStruct((T, D), BF16), jax.ShapeDtypeStruct((2, T, GU_SH), BF16),
                   jax.ShapeDtypeStruct((T, D), BF16), jax.ShapeDtypeStruct((4, T, GU_SH), BF16),
                   jax.ShapeDtypeStruct((T, D), BF16), jax.ShapeDtypeStruct((8, D), F32)],
        scratch_shapes=[pltpu.VMEM((D, D), BF16), pltpu.VMEM((4, D, GU_SH), BF16), pltpu.VMEM((DFF, D), BF16),
                        pltpu.SemaphoreType.DMA((3,))],
        compiler_params=_cp("arbitrary"),
    )(x, attn, ynorm, tgt, mod6, norm2_w, final_w, w_out, w_gu, w_dn)


def _in_proj_bwd(x, dx1, dq, dk, dv, dzxd, mod6, norm1_w, w_pad, tm):
    T = x.shape[0]

    def body(x_ref, dx1_ref, dq_ref, dk_ref, dv_ref, dz_ref, mod_ref, nw_ref, w_hbm, gx_ref, sm_ref, w_vmem, sem):
        _load_resident(w_hbm, w_vmem, sem)

        @pl.when(pl.program_id(0) == 0)
        def _():
            sm_ref[...] = jnp.zeros_like(sm_ref)

        dh = (_dot_nt(dq_ref[...], w_vmem[:, 0:512]) + _dot_nt(dk_ref[...], w_vmem[:, 512:640])
              + _dot_nt(dv_ref[...], w_vmem[:, 640:768]) + _dot_nt(dz_ref[...], w_vmem[:, 768:IN_PAD]))
        xv = x_ref[...]
        nw = nw_ref[...]
        scale1 = mod_ref[1:2, :]
        r = lax.rsqrt(jnp.mean(xv * xv, axis=-1, keepdims=True) + EPS)
        xh = xv * r
        n1 = xh * nw
        dshift = jnp.sum(dh, axis=0, keepdims=True)
        dscale = jnp.sum(dh * n1, axis=0, keepdims=True)
        dn = dh * (1.0 + scale1)
        dnw = jnp.sum(dn * xh, axis=0, keepdims=True)
        dxh = dn * nw
        gx_ref[...] = dx1_ref[...] + r * (dxh - xh * jnp.mean(dxh * xh, axis=-1, keepdims=True))
        sm_ref[...] += jnp.concatenate([dnw, dshift, dscale, jnp.zeros((5, D), F32)], axis=0)

    row = lambda w: pl.BlockSpec((tm, w), lambda i: (i, 0))
    full = lambda a: pl.BlockSpec(a.shape, lambda i: (0,) * a.ndim)
    return pl.pallas_call(
        body, name="in_proj_bwd", grid=(T // tm,),
        in_specs=[row(D), row(D), row(512), row(128), row(128), row(1664), full(mod6), full(norm1_w),
                  pl.BlockSpec(memory_space=pl.ANY)],
        out_specs=[row(D), pl.BlockSpec((8, D), lambda i: (0, 0))],
        out_shape=[jax.ShapeDtypeStruct((T, D), F32), jax.ShapeDtypeStruct((8, D), F32)],
        scratch_shapes=[pltpu.VMEM((D, IN_PAD), BF16), pltpu.SemaphoreType.DMA],
        compiler_params=_cp("arbitrary"),
    )(x, dx1, dq, dk, dv, dzxd, mod6, norm1_w, w_pad)


def _tn_matmul(a3, b3, tt, name):
    ja, T, K = a3.shape
    jb, _, N = b3.shape
    J = max(ja, jb)

    def body(a_ref, b_ref, o_ref):
        t = pl.program_id(1)
        prod = _dot_tn(a_ref[0], b_ref[0])

        @pl.when(t == 0)
        def _():
            o_ref[0] = prod

        @pl.when(t > 0)
        def _():
            o_ref[0] += prod

    return pl.pallas_call(
        body, name=name, grid=(J, T // tt),
        in_specs=[pl.BlockSpec((1, tt, K), lambda j, t: (j if ja > 1 else 0, t, 0)),
                  pl.BlockSpec((1, tt, N), lambda j, t: (j if jb > 1 else 0, t, 0))],
        out_specs=pl.BlockSpec((1, K, N), lambda j, t: (j, 0, 0)),
        out_shape=jax.ShapeDtypeStruct((J, K, N), F32),
        compiler_params=_cp("parallel", "arbitrary"),
    )(a3, b3)


def _adam_math(w, g, m, v):
    m = B1 * m + (1.0 - B1) * g
    v = B2 * v + (1.0 - B2) * (g * g)
    m_hat = m / (1.0 - B1 ** STEP)
    v_hat = v / (1.0 - B2 ** STEP)
    delta = -LR * (m_hat / (jnp.sqrt(v_hat) + AEPS) + WD * w)
    return delta, m, v


def _adam_2d(w, g, m, v, rb, name):
    R, C = w.shape

    def body(w_ref, g_ref, m_ref, v_ref, d_ref, mo_ref, vo_ref):
        d, mn, vn = _adam_math(w_ref[...], g_ref[...], m_ref[...], v_ref[...])
        d_ref[...] = d
        mo_ref[...] = mn
        vo_ref[...] = vn

    spec = pl.BlockSpec((rb, C), lambda i: (i, 0))
    return pl.pallas_call(
        body, name=name, grid=(R // rb,), in_specs=[spec] * 4, out_specs=[spec] * 3,
        out_shape=[jax.ShapeDtypeStruct((R, C), F32)] * 3, compiler_params=_cp("parallel"),
    )(w, g, m, v)


def _adam_w_ada(sc_all, dmod_s, w, m, v, rb):
    R, C = w.shape

    def body(sc_ref, dm_ref, w_ref, m_ref, v_ref, g_ref, d_ref, mo_ref, vo_ref):
        g = lax.dot_general(sc_ref[...], dm_ref[...], (((0,), (0,)), ((), ())), precision=HI, preferred_element_type=F32)
        d, mn, vn = _adam_math(w_ref[...], g, m_ref[...], v_ref[...])
        g_ref[...] = g
        d_ref[...] = d
        mo_ref[...] = mn
        vo_ref[...] = vn

    spec = pl.BlockSpec((rb, C), lambda i: (i, 0))
    return pl.pallas_call(
        body, name="adam_w_ada", grid=(R // rb,),
        in_specs=[pl.BlockSpec((8, rb), lambda i: (0, i)), pl.BlockSpec((8, C), lambda i: (0, 0)), spec, spec, spec],
        out_specs=[spec] * 4, out_shape=[jax.ShapeDtypeStruct((R, C), F32)] * 4, compiler_params=_cp("parallel"),
    )(sc_all, dmod_s, w, m, v)


def _adam_small(grads, ws, ms, vs):
    k = len(ws)

    def body(*refs):
        g, w, m, v = refs[0:k], refs[k:2 * k], refs[2 * k:3 * k], refs[3 * k:4 * k]
        d_o, m_o, v_o = refs[4 * k:5 * k], refs[5 * k:6 * k], refs[6 * k:7 * k]
        for i in range(k):
            d, mn, vn = _adam_math(w[i][...], g[i][...], m[i][...], v[i][...])
            d_o[i][...] = d
            m_o[i][...] = mn
            v_o[i][...] = vn

    shapes = [jax.ShapeDtypeStruct(w.shape, F32) for w in ws]
    vm = pl.BlockSpec(memory_space=pltpu.VMEM)
    outs = pl.pallas_call(
        body, name="adam_small", in_specs=[vm] * (4 * k), out_specs=[vm] * (3 * k), out_shape=shapes * 3,
    )(*grads, *ws, *ms, *vs)
    return outs[0:k], outs[k:2 * k], outs[2 * k:3 * k]


def _pos():
    return lax.axis_index("x"), lax.axis_index("y"), lax.axis_index("c")


def _flip(v, bit):
    return 1 - v if bit else v


def _peer(k):
    x, y, c = _pos()
    return (_flip(x, (k >> 2) & 1), _flip(y, (k >> 1) & 1), _flip(c, k & 1))


def _logical(p):
    return 4 * p[0] + 2 * p[1] + p[2]


def _gather8(src_ref, dst_ref, send_sems, recv_sems):
    me = _logical(_pos())
    dst_ref[pl.ds(me, 1)] = src_ref[...][None]
    copies = []
    for k in range(1, 8):
        cp = pltpu.make_async_remote_copy(src_ref, dst_ref.at[me], send_sems.at[k - 1], recv_sems.at[k - 1],
                                          device_id=_peer(k), device_id_type=MESH)
        cp.start()
        copies.append(cp)
    for k in range(1, 8):
        pltpu.make_async_remote_copy(src_ref, dst_ref.at[_logical(_peer(k))], send_sems.at[k - 1], recv_sems.at[k - 1],
                                     device_id=_peer(k), device_id_type=MESH).wait_recv()
    for cp in copies:
        cp.wait_send()


def _rows_select(ref3, width):
    row = _iota((8, width), 0)
    out = jnp.zeros((8, width), F32)
    for i in range(8):
        out = jnp.where(row == i, ref3[i][:, 0:width], out)
    return out


def _mod_exchange(payload, w_ada_s, b_ada4):
    n_sh = w_ada_s.shape[1]

    def body(pay_ref, w_ref, b_ref, gat_ref, mod_ref, p3, sa, ra, sb, rb):
        x, y, c = _pos()
        me = _logical((x, y, c))
        my_s = 2 * x + y
        _gather8(pay_ref, gat_ref, sa, ra)
        cmat = _rows_select(gat_ref, D)
        prod = _dot_hi(cmat * _sigmoid(cmat), w_ref[...])
        for b in range(8):
            p3[b] = prod[b:b + 1, :]
        mod_ref[pl.ds(my_s, 1)] = p3[pl.ds(me, 1)] + b_ref[pl.ds(my_s, 1)]
        ks = (2, 4, 6)
        copies = []
        for i, k in enumerate(ks):
            pr = _peer(k)
            cp = pltpu.make_async_remote_copy(p3.at[_logical(pr)], mod_ref.at[my_s], sb.at[i], rb.at[i],
                                              device_id=pr, device_id_type=MESH)
            cp.start()
            copies.append(cp)
        for i, k in enumerate(ks):
            pr = _peer(k)
            s_src = 2 * pr[0] + pr[1]
            pltpu.make_async_remote_copy(p3.at[0], mod_ref.at[s_src], sb.at[i], rb.at[i],
                                         device_id=pr, device_id_type=MESH).wait_recv()
            mod_ref[pl.ds(s_src, 1)] = mod_ref[pl.ds(s_src, 1)] + b_ref[pl.ds(s_src, 1)]
        for cp in copies:
            cp.wait_send()

    vm = pl.BlockSpec(memory_space=pltpu.VMEM)
    return pl.pallas_call(
        body, name="mod_exchange", in_specs=[vm, vm, vm], out_specs=[vm, vm],
        out_shape=[jax.ShapeDtypeStruct((8, 1, payload.shape[1]), F32), jax.ShapeDtypeStruct((4, 1, n_sh), F32)],
        scratch_shapes=[pltpu.VMEM((8, 1, n_sh), F32), pltpu.SemaphoreType.DMA((7,)), pltpu.SemaphoreType.DMA((7,)),
                        pltpu.SemaphoreType.DMA((3,)), pltpu.SemaphoreType.DMA((3,))],
        compiler_params=pltpu.CompilerParams(vmem_limit_bytes=VMEM_LIMIT),
    )(payload, w_ada_s, b_ada4)


def _chips():
    x, y, _ = _pos()
    out = []
    for k in (1, 2, 3):
        px, py = _flip(x, (k >> 1) & 1), _flip(y, k & 1)
        out.append((px, py, 2 * px + py))
    return out


def _half_rows(ref, which):
    half = ref.shape[-2] // 2
    return pl.ds(pl.multiple_of(which * half, 8), half)


def _weight_gather(shards):
    nw = len(shards)

    def body(*refs):
        ins, outs = refs[:nw], refs[nw:2 * nw]
        send, recv, fsend, frecv, lsem = refs[2 * nw:]
        x, y, c = _pos()
        my_s = 2 * x + y
        sib = (x, y, 1 - c)
        chips = _chips()
        local = [pltpu.make_async_copy(ins[w], outs[w].at[my_s], lsem.at[w]) for w in range(nw)]
        for cp in local:
            cp.start()
        sends = []
        for w in range(nw):
            mine = _half_rows(ins[w], c)
            for k, (px, py, _) in enumerate(chips):
                cp = pltpu.make_async_remote_copy(ins[w].at[mine], outs[w].at[my_s, mine], send.at[3 * w + k],
                                                  recv.at[3 * w + k], device_id=(px, py, c), device_id_type=MESH)
                cp.start()
                sends.append(cp)
        for w in range(nw):
            mine = _half_rows(ins[w], c)
            for k, (px, py, ps) in enumerate(chips):
                got = outs[w].at[ps, mine]
                pltpu.make_async_remote_copy(got, got, send.at[3 * w + k], recv.at[3 * w + k],
                                             device_id=(px, py, c), device_id_type=MESH).wait_recv()
                cp = pltpu.make_async_remote_copy(got, got, fsend.at[3 * w + k], frecv.at[3 * w + k],
                                                  device_id=sib, device_id_type=MESH)
                cp.start()
                sends.append(cp)
        for w in range(nw):
            other = _half_rows(ins[w], 1 - c)
            for k, (px, py, ps) in enumerate(chips):
                got = outs[w].at[ps, other]
                pltpu.make_async_remote_copy(got, got, fsend.at[3 * w + k], frecv.at[3 * w + k],
                                             device_id=sib, device_id_type=MESH).wait_recv()
        for cp in sends:
            cp.wait_send()
        for cp in local:
            cp.wait()

    hbm = pl.BlockSpec(memory_space=pltpu.HBM)
    return pl.pallas_call(
        body, name="weight_gather", in_specs=[hbm] * nw, out_specs=[hbm] * nw,
        out_shape=[pltpu.HBM((4,) + s.shape, s.dtype) for s in shards],
        scratch_shapes=[pltpu.SemaphoreType.DMA((3 * nw,)), pltpu.SemaphoreType.DMA((3 * nw,)),
                        pltpu.SemaphoreType.DMA((3 * nw,)), pltpu.SemaphoreType.DMA((3 * nw,)),
                        pltpu.SemaphoreType.DMA((nw,))],
    )(*shards)


def _small_reduce(vec):
    n = vec.shape[1]

    def body(v_ref, tot_ref, gat_ref, sa, ra):
        _gather8(v_ref, gat_ref, sa, ra)
        tot = gat_ref[0]
        for i in range(1, 8):
            tot = tot + gat_ref[i]
        tot_ref[...] = tot

    vm = pl.BlockSpec(memory_space=pltpu.VMEM)
    return pl.pallas_call(
        body, name="small_reduce", in_specs=[vm], out_specs=[vm, vm],
        out_shape=[jax.ShapeDtypeStruct((1, n), F32), jax.ShapeDtypeStruct((8, 1, n), F32)],
        scratch_shapes=[pltpu.SemaphoreType.DMA((7,)), pltpu.SemaphoreType.DMA((7,))],
    )(vec)


def _sibling_swap(grads):
    nw = len(grads)

    def body(*refs):
        ins, outs = refs[:nw], refs[nw:2 * nw]
        send, recv = refs[2 * nw:]
        x, y, c = _pos()
        sib = (x, y, 1 - c)
        cps = []
        for w in range(nw):
            theirs = _half_rows(ins[w], 1 - c)
            cp = pltpu.make_async_remote_copy(ins[w].at[:, theirs], outs[w], send.at[w], recv.at[w],
                                              device_id=sib, device_id_type=MESH)
            cp.start()
            cps.append(cp)
        for cp in cps:
            cp.wait()

    hbm = pl.BlockSpec(memory_space=pltpu.HBM)
    return pl.pallas_call(
        body, name="grad_sibling_swap", in_specs=[hbm] * nw, out_specs=[hbm] * nw,
        out_shape=[pltpu.HBM((4, g.shape[1] // 2, g.shape[2]), F32) for g in grads],
        scratch_shapes=[pltpu.SemaphoreType.DMA((nw,)), pltpu.SemaphoreType.DMA((nw,))],
    )(*grads)


def _add_half(g, sib, c_arr, rb, name):
    _, R, C = g.shape
    half = R // 2
    nb = half // rb

    def body(c_ref, g_ref, s_ref, o_ref):
        o_ref[...] = g_ref[...] + s_ref[...]

    return pl.pallas_call(
        body, name=name,
        grid_spec=pltpu.PrefetchScalarGridSpec(
            num_scalar_prefetch=1, grid=(4, nb),
            in_specs=[pl.BlockSpec((1, rb, C), lambda s, i, c_ref: (s, c_ref[0] * nb + i, 0)),
                      pl.BlockSpec((1, rb, C), lambda s, i, c_ref: (s, i, 0))],
            out_specs=pl.BlockSpec((1, rb, C), lambda s, i, c_ref: (s, i, 0))),
        out_shape=jax.ShapeDtypeStruct((4, half, C), F32),
        compiler_params=_cp("parallel", "parallel"),
    )(c_arr, g, sib)


def _chip_scatter(parts):
    nw = len(parts)

    def body(*refs):
        ins, outs = refs[:nw], refs[nw:2 * nw]
        send, recv, lsem = refs[2 * nw:]
        x, y, c = _pos()
        my_s = 2 * x + y
        chips = _chips()
        local = [pltpu.make_async_copy(ins[w].at[my_s], outs[w].at[my_s], lsem.at[w]) for w in range(nw)]
        for cp in local:
            cp.start()
        cps = []
        for w in range(nw):
            for k, (px, py, ps) in enumerate(chips):
                cp = pltpu.make_async_remote_copy(ins[w].at[ps], outs[w].at[my_s], send.at[3 * w + k], recv.at[3 * w + k],
                                                  device_id=(px, py, c), device_id_type=MESH)
                cp.start()
                cps.append(cp)
        for w in range(nw):
            for k, (px, py, ps) in enumerate(chips):
                pltpu.make_async_remote_copy(ins[w].at[ps], outs[w].at[ps], send.at[3 * w + k], recv.at[3 * w + k],
                                             device_id=(px, py, c), device_id_type=MESH).wait_recv()
        for cp in cps:
            cp.wait_send()
        for cp in local:
            cp.wait()

    hbm = pl.BlockSpec(memory_space=pltpu.HBM)
    return pl.pallas_call(
        body, name="grad_chip_scatter", in_specs=[hbm] * nw, out_specs=[hbm] * nw,
        out_shape=[pltpu.HBM(p.shape, F32) for p in parts],
        scratch_shapes=[pltpu.SemaphoreType.DMA((3 * nw,)), pltpu.SemaphoreType.DMA((3 * nw,)),
                        pltpu.SemaphoreType.DMA((nw,))],
    )(*parts)


def _sum4(r, rb, name):
    _, H, C = r.shape

    def body(r_ref, o_ref):
        o_ref[...] = ((r_ref[0] + r_ref[1]) + r_ref[2]) + r_ref[3]

    return pl.pallas_call(
        body, name=name, grid=(H // rb,),
        in_specs=[pl.BlockSpec((4, rb, C), lambda i: (0, i, 0))], out_specs=pl.BlockSpec((rb, C), lambda i: (i, 0)),
        out_shape=jax.ShapeDtypeStruct((H, C), F32), compiler_params=_cp("parallel"),
    )(r)


def _sibling_join(halves):
    nw = len(halves)

    def body(*refs):
        ins, outs = refs[:nw], refs[nw:2 * nw]
        send, recv, lsem = refs[2 * nw:]
        x, y, c = _pos()
        sib = (x, y, 1 - c)
        cps, local = [], []
        for w in range(nw):
            mine = _half_rows(outs[w], c)
            lc = pltpu.make_async_copy(ins[w], outs[w].at[mine], lsem.at[w])
            lc.start()
            local.append(lc)
            cp = pltpu.make_async_remote_copy(ins[w], outs[w].at[mine], send.at[w], recv.at[w],
                                              device_id=sib, device_id_type=MESH)
            cp.start()
            cps.append(cp)
        for w in range(nw):
            other = _half_rows(outs[w], 1 - c)
            pltpu.make_async_remote_copy(ins[w], outs[w].at[other], send.at[w], recv.at[w],
                                         device_id=sib, device_id_type=MESH).wait_recv()
        for cp in cps:
            cp.wait_send()
        for lc in local:
            lc.wait()

    hbm = pl.BlockSpec(memory_space=pltpu.HBM)
    return pl.pallas_call(
        body, name="grad_sibling_join", in_specs=[hbm] * nw, out_specs=[hbm] * nw,
        out_shape=[pltpu.HBM((2 * h.shape[0], h.shape[1]), F32) for h in halves],
        scratch_shapes=[pltpu.SemaphoreType.DMA((nw,)), pltpu.SemaphoreType.DMA((nw,)), pltpu.SemaphoreType.DMA((nw,))],
    )(*halves)


def _pad_lanes(a, n):
    return jnp.pad(a, ((0, 0), (0, n - a.shape[1])))


def kernel(x, c, positions, w_ada, b_ada, norm1_w, w_in, conv_w, conv_b, dt_bias, a_log, d_skip, attn_sinks, ssm_norm_w, w_out, norm2_w, w_gate_up, w_down, final_norm_w, loss_target, m_w_ada, m_b_ada, m_norm1_w, m_w_in, m_conv_w, m_conv_b, m_dt_bias, m_a_log, m_d_skip, m_attn_sinks, m_ssm_norm_w, m_w_out, m_norm2_w, m_w_gate_up, m_w_down, m_final_norm_w, v_w_ada, v_b_ada, v_norm1_w, v_w_in, v_conv_w, v_conv_b, v_dt_bias, v_a_log, v_d_skip, v_attn_sinks, v_ssm_norm_w, v_w_out, v_norm2_w, v_w_gate_up, v_w_down, v_final_norm_w):
    T = x.shape[1]
    tm = min(256, T)
    xi, yi, ci = lax.axis_index("x"), lax.axis_index("y"), lax.axis_index("c")
    my_s = 2 * xi + yi
    xs = x[0]
    tgt = loss_target[0]

    payload = jnp.concatenate([c, conv_w[0].reshape(1, CONVK * 256)], axis=1)
    gat, mod4 = _mod_exchange(payload, w_ada[0], b_ada.reshape(4, 1, 1536))
    mod6 = mod4.reshape(6, D)
    c_all = gat[:, 0, 0:D]
    cw_dev = gat[:, 0, D:].reshape(4, 2, CONVK, 256)[:, 0]
    conv_full = cw_dev.transpose(1, 0, 2).reshape(CONVK, CONVC)

    g_in, g_out, g_gu, g_dn = _weight_gather(
        [w_in[0].astype(BF16), w_out[0].astype(BF16), w_gate_up[0].astype(BF16), w_down[0].astype(BF16)])
    w_pad = _pad_lanes(g_in.transpose(1, 0, 2).reshape(D, IN_PROJ), IN_PAD)
    w_out_f = g_out.reshape(D, D)
    w_dn_f = g_dn.reshape(DFF, D)

    inv_freq = (10000.0 ** (-jnp.arange(32, dtype=F32) / 32))
    inv_row = jnp.tile(inv_freq, 4).reshape(1, 128)
    cos, sin_s = _rope_tables(positions.reshape(T, 1), inv_row, tm)
    qkv, z, xbc, dtr, h1b = _in_proj_fwd(xs, cos, sin_s, mod6, norm1_w, w_pad, tm)
    sinks = attn_sinks
    attn, lse = _attn_fwd(qkv, sinks)
    dtb = _pad_lanes(dt_bias, 128)
    alog = _pad_lanes(a_log, 128)
    dskx = jnp.repeat(d_skip, HD, axis=1)
    ynorm, ypre, states = _ssd_fwd(xbc, z, dtr, conv_full, conv_b, dtb, alog, dskx, ssm_norm_w)

    fw2 = final_norm_w.reshape(1, D)
    sq, dmix, dx1, h2b, act, dfb, dgu, dob, sm_ffn = _mix_ffn(
        xs, attn, ynorm, tgt, mod6, norm2_w, fw2, w_out_f, g_gu, w_dn_f, tm)
    loss = lax.psum(0.5 / D * jnp.sum(sq), ("x", "y", "c"))

    dzxd, d_cw, d_cb, d_sw, d_sk, d_dtb, d_av = _ssd_bwd(
        xbc, z, dtr, ypre, states, dmix, conv_full, conv_b, dtb, alog, dskx, ssm_norm_w)
    dq, dk, dv, d_sinks = _attn_bwd(qkv, sinks, lse, dmix, cos, sin_s)
    grad_x, sm_in = _in_proj_bwd(xs, dx1, dq, dk, dv, dzxd, mod6, norm1_w, w_pad, tm)

    tt = min(512, T)
    h1_3 = h1b[None]
    gw_in = jnp.concatenate(
        [_tn_matmul(h1_3, dq[None], tt, "dw_in_q")[0], _tn_matmul(h1_3, dk[None], tt, "dw_in_k")[0],
         _tn_matmul(h1_3, dv[None], tt, "dw_in_v")[0], _tn_matmul(h1_3, dzxd[None], tt, "dw_in_z")[0]], axis=1)
    gw_in4 = gw_in[:, :IN_PROJ].reshape(D, 4, IN_PROJ // 4).transpose(1, 0, 2)
    gw_out4 = jnp.concatenate(
        [_tn_matmul(attn[None], dob[None], tt, "dw_out_a")[0], _tn_matmul(ynorm[None], dob[None], tt, "dw_out_y")[0]],
        axis=0).reshape(4, D // 4, D)
    gw_gu4 = _tn_matmul(h2b[None], dgu, tt, "dw_gate_up")
    gw_dn4 = _tn_matmul(act, dfb[None], tt, "dw_down").reshape(4, DFF // 4, D)

    big = [gw_in4, gw_out4, gw_gu4, gw_dn4]
    c_arr = ci.reshape(1).astype(jnp.int32)
    sibs = _sibling_swap(big)
    rbs = [128, 128, 128, 88]
    chip_sums = [_add_half(g, s, c_arr, rb, "grad_add_%d" % i) for i, (g, s, rb) in enumerate(zip(big, sibs, rbs))]
    slots = _chip_scatter(chip_sums)
    halves = [_sum4(r, rb, "grad_sum_%d" % i) for i, (r, rb) in enumerate(zip(slots, rbs))]
    g_in_s, g_out_s, g_gu_s, g_dn_s = _sibling_join(halves)

    a_neg = -jnp.exp(alog)
    pieces = [sm_in[1:2], sm_in[2:3], sm_ffn[5:6], sm_ffn[2:3], sm_ffn[3:4], sm_ffn[4:5],
              sm_in[0:1], sm_ffn[1:2], sm_ffn[0:1], d_cb, d_cw.reshape(1, CONVK * CONVC),
              _pad_lanes(d_sw, SW), d_dtb, d_av * a_neg, d_sk, d_sinks]
    vec = jnp.concatenate(pieces, axis=1)
    tot, allv = _small_reduce(vec)
    o = 0
    offs = []
    for p in pieces:
        offs.append(o)
        o += p.shape[1]
    seg = lambda i, n: tot[:, offs[i]:offs[i] + n]
    g_b_ada = tot[:, 0:6 * D]
    g_norm1, g_norm2, g_final, g_conv_b = seg(6, D), seg(7, D), seg(8, D), seg(9, D)
    g_conv_w = lax.dynamic_slice_in_dim(seg(10, CONVK * CONVC).reshape(CONVK, CONVC), my_s * 256, 256, axis=1)
    g_ssm_w, g_dtb, g_alog, g_dsk, g_sink = seg(11, SW), seg(12, 8), seg(13, 8), seg(14, 8), seg(15, 8)

    small_names = ["b_ada", "norm1_w", "conv_w", "conv_b", "dt_bias", "a_log", "d_skip", "attn_sinks", "ssm_norm_w",
                   "norm2_w", "final_norm_w"]
    small_g = [g_b_ada, g_norm1, g_conv_w, g_conv_b, g_dtb, g_alog, g_dsk, g_sink, g_ssm_w, g_norm2, g_final]
    as2d = lambda a: a.reshape(-1, a.shape[-1])
    small_w = [as2d(a) for a in (b_ada, norm1_w, conv_w, conv_b, dt_bias, a_log, d_skip, attn_sinks, ssm_norm_w,
                                 norm2_w, final_norm_w)]
    small_m = [as2d(a) for a in (m_b_ada, m_norm1_w, m_conv_w, m_conv_b, m_dt_bias, m_a_log, m_d_skip, m_attn_sinks,
                                 m_ssm_norm_w, m_norm2_w, m_final_norm_w)]
    small_v = [as2d(a) for a in (v_b_ada, v_norm1_w, v_conv_w, v_conv_b, v_dt_bias, v_a_log, v_d_skip, v_attn_sinks,
                                 v_ssm_norm_w, v_norm2_w, v_final_norm_w)]
    sd, smn, svn = _adam_small(small_g, small_w, small_m, small_v)

    sc_all = c_all * jax.nn.sigmoid(c_all)
    dmod_all = allv[:, 0, 0:6 * D]
    dmod_s = lax.dynamic_slice_in_dim(dmod_all, my_s * 1536, 1536, axis=1)
    g_ada, d_ada, m_ada, v_ada = _adam_w_ada(sc_all, dmod_s, w_ada[0], m_w_ada[0], v_w_ada[0], 256)
    d_in, m_in, v_in = _adam_2d(w_in[0], g_in_s, m_w_in[0], v_w_in[0], 256, "adam_w_in")
    d_out, m_out, v_out = _adam_2d(w_out[0], g_out_s, m_w_out[0], v_w_out[0], 256, "adam_w_out")
    d_gu, m_gu, v_gu = _adam_2d(w_gate_up[0], g_gu_s, m_w_gate_up[0], v_w_gate_up[0], 256, "adam_w_gate_up")
    d_dn, m_dn, v_dn = _adam_2d(w_down[0], g_dn_s, m_w_down[0], v_w_down[0], 352, "adam_w_down")

    order = ["w_ada", "b_ada", "norm1_w", "w_in", "conv_w", "conv_b", "dt_bias", "a_log", "d_skip", "attn_sinks",
             "ssm_norm_w", "w_out", "norm2_w", "w_gate_up", "w_down", "final_norm_w"]
    shapes = dict(w_ada=w_ada.shape, b_ada=b_ada.shape, norm1_w=norm1_w.shape, w_in=w_in.shape, conv_w=conv_w.shape,
                  conv_b=conv_b.shape, dt_bias=dt_bias.shape, a_log=a_log.shape, d_skip=d_skip.shape,
                  attn_sinks=attn_sinks.shape, ssm_norm_w=ssm_norm_w.shape, w_out=w_out.shape, norm2_w=norm2_w.shape,
                  w_gate_up=w_gate_up.shape, w_down=w_down.shape, final_norm_w=final_norm_w.shape)
    grads = dict(w_ada=g_ada, w_in=g_in_s, w_out=g_out_s, w_gate_up=g_gu_s, w_down=g_dn_s)
    deltas = dict(w_ada=d_ada, w_in=d_in, w_out=d_out, w_gate_up=d_gu, w_down=d_dn)
    new_m = dict(w_ada=m_ada, w_in=m_in, w_out=m_out, w_gate_up=m_gu, w_down=m_dn)
    new_v = dict(w_ada=v_ada, w_in=v_in, w_out=v_out, w_gate_up=v_gu, w_down=v_dn)
    for i, nme in enumerate(small_names):
        grads[nme], deltas[nme], new_m[nme], new_v[nme] = small_g[i], sd[i], smn[i], svn[i]
    outs = [loss, grad_x[None]]
    for table in (grads, deltas, new_m, new_v):
        outs += [table[nme].reshape(shapes[nme]) for nme in order]
    return tuple(outs)
```

```python
import functools
import math

import jax
import jax.numpy as jnp
from jax import lax
from jax.experimental import pallas as pl
from jax.experimental.pallas import tpu as pltpu

F32 = jnp.float32
BF16 = jnp.bfloat16
HI = lax.Precision.HIGHEST
MESH = pl.DeviceIdType.MESH

D = 1024
HD = 64
NQ = 8
AW = 512
KVW = 128
SW = 512
NST = 128
CONVK = 4
CONVC = 1024
BLK = 128
IN_PROJ = 2312
IN_PAD = 2432
DFF = 2816
GU_SH = 1408
EPS = 1e-6
NEG = -1e30
LR, B1, B2, AEPS, WD, STEP = 0.001, 0.9, 0.999, 1e-08, 0.01, 10
VMEM_LIMIT = 58 * 1024 * 1024


def _cp(*sem):
    return pltpu.CompilerParams(dimension_semantics=sem or None, vmem_limit_bytes=VMEM_LIMIT)


def _dot(a, b):
    return jnp.dot(a, b, preferred_element_type=F32)


def _dot_nt(a, b):
    return lax.dot_general(a, b, (((1,), (1,)), ((), ())), preferred_element_type=F32)


def _dot_tn(a, b):
    return lax.dot_general(a, b, (((0,), (0,)), ((), ())), preferred_element_type=F32)


def _dot_hi(a, b):
    return jnp.dot(a, b, precision=HI, preferred_element_type=F32)


def _sigmoid(x):
    return 1.0 / (1.0 + jnp.exp(-x))


def _iota(shape, dim):
    return lax.broadcasted_iota(jnp.int32, shape, dim)


def _load_resident(hbm_ref, vmem_ref, sem):
    @pl.when(pl.program_id(0) == 0)
    def _():
        cp = pltpu.make_async_copy(hbm_ref, vmem_ref, sem)
        cp.start()
        cp.wait()


def _swap32(t):
    lane = _iota(t.shape, 1)
    return jnp.where((lane & 63) < 32, pltpu.roll(t, 96, 1), pltpu.roll(t, 32, 1))


def _rope_fwd(t, cos, sin_s):
    return t * cos + _swap32(t) * sin_s


def _rope_bwd(t, cos, sin_s):
    return t * cos - _swap32(t) * sin_s


def _rope_tables(pos_col, inv_freq_row, tm):
    T = pos_col.shape[0]

    def body(p_ref, f_ref, cos_ref, sin_ref):
        ang = p_ref[...].astype(F32) * f_ref[...]
        lane = _iota((tm, 128), 1)
        s = jnp.sin(ang)
        cos_ref[...] = jnp.cos(ang)
        sin_ref[...] = jnp.where((lane & 63) < 32, -s, s)

    return pl.pallas_call(
        body, name="rope_tables", grid=(T // tm,),
        in_specs=[pl.BlockSpec((tm, 1), lambda i: (i, 0)), pl.BlockSpec((1, 128), lambda i: (0, 0))],
        out_specs=[pl.BlockSpec((tm, 128), lambda i: (i, 0))] * 2,
        out_shape=[jax.ShapeDtypeStruct((T, 128), F32)] * 2,
        compiler_params=_cp("parallel"),
    )(pos_col, inv_freq_row)


def _in_proj_fwd(x, cos, sin_s, mod6, norm1_w, w_pad, tm):
    T = x.shape[0]

    def body(x_ref, cos_ref, sin_ref, mod_ref, nw_ref, w_hbm, qkv_ref, z_ref, xbc_ref, dt_ref, h_ref, w_vmem, sem):
        _load_resident(w_hbm, w_vmem, sem)
        xv = x_ref[...]
        r = lax.rsqrt(jnp.mean(xv * xv, axis=-1, keepdims=True) + EPS)
        h = (xv * r * nw_ref[...]) * (1.0 + mod_ref[1:2, :]) + mod_ref[0:1, :]
        hb = h.astype(BF16)
        h_ref[...] = hb
        proj = _dot(hb, w_vmem[...])
        cs, sn = cos_ref[...], sin_ref[...]
        for j in range(5):
            qkv_ref[:, 128 * j:128 * (j + 1)] = _rope_fwd(proj[:, 128 * j:128 * (j + 1)], cs, sn).astype(BF16)
        qkv_ref[:, 640:768] = proj[:, 640:768].astype(BF16)
        z_ref[...] = proj[:, 768:1280]
        xbc_ref[...] = proj[:, 1280:2304]
        dt_ref[...] = proj[:, 2304:2432]

    row = lambda w: pl.BlockSpec((tm, w), lambda i: (i, 0))
    full = lambda a: pl.BlockSpec(a.shape, lambda i: (0,) * a.ndim)
    return pl.pallas_call(
        body, name="in_proj_fwd", grid=(T // tm,),
        in_specs=[row(D), row(128), row(128), full(mod6), full(norm1_w), pl.BlockSpec(memory_space=pl.ANY)],
        out_specs=[row(768), row(512), row(1024), row(128), row(D)],
        out_shape=[jax.ShapeDtypeStruct((T, 768), BF16), jax.ShapeDtypeStruct((T, 512), F32),
                   jax.ShapeDtypeStruct((T, 1024), F32), jax.ShapeDtypeStruct((T, 128), F32),
                   jax.ShapeDtypeStruct((T, D), BF16)],
        scratch_shapes=[pltpu.VMEM((D, IN_PAD), BF16), pltpu.SemaphoreType.DMA],
        compiler_params=_cp("arbitrary"),
    )(x, cos, sin_s, mod6, norm1_w, w_pad)


def _head_variants(pair, j):
    lane = _iota(pair.shape, 1)
    lo = lane < 64
    kv = j // 2
    ev = jnp.where(lo, pair, 0.0)
    od = jnp.where(lo, 0.0, pair)
    if kv == 0:
        od = pltpu.roll(od, 64, 1)
    else:
        ev = pltpu.roll(ev, 64, 1)
    return ev.astype(BF16), od.astype(BF16)


def _kv_variants(vcat):
    lane = _iota(vcat.shape, 1)
    lo = lane < 64
    v0 = jnp.where(lo, vcat, 0.0)
    v1 = jnp.where(lo, 0.0, vcat)
    out = {
        (0, 0): v0, (0, 1): pltpu.roll(v0, 64, 1),
        (1, 0): pltpu.roll(v1, 64, 1), (1, 1): v1,
    }
    return {k: v.astype(BF16) for k, v in out.items()}


def _attn_mask(n):
    i = _iota((BLK, 2 * BLK), 0)
    j = _iota((BLK, 2 * BLK), 1)
    return (j > i) & (j <= i + BLK) & ((n > 0) | (j >= BLK))


def _attn_fwd(qkv, sinks):
    T = qkv.shape[0]
    nb = T // BLK

    def body(sink_ref, q_ref, kc_ref, kp_ref, vc_ref, vp_ref, o_ref, lse_ref):
        n = pl.program_id(0)
        valid = _attn_mask(n)
        kcat = jnp.concatenate([kp_ref[...], kc_ref[...]], axis=0)
        vvar = _kv_variants(jnp.concatenate([vp_ref[...], vc_ref[...]], axis=0).astype(F32))
        lane = _iota((BLK, 128), 1)
        lse_acc = jnp.zeros((BLK, 128), F32)
        for j in range(4):
            qv = _head_variants(q_ref[:, 128 * j:128 * (j + 1)].astype(F32), j)
            acc = jnp.zeros((BLK, 128), F32)
            for par in range(2):
                h = 2 * j + par
                sink = sink_ref[0, h]
                s = jnp.where(valid, _dot_nt(qv[par], kcat) * 0.125, NEG)
                m = jnp.maximum(jnp.max(s, axis=1, keepdims=True), sink)
                p = jnp.exp(s - m)
                den = jnp.sum(p, axis=1, keepdims=True) + jnp.exp(sink - m)
                probs = (p * (1.0 / den)).astype(BF16)
                acc = acc + _dot(probs, vvar[(j // 2, par)])
                lse_acc = jnp.where(lane == h, m + jnp.log(den), lse_acc)
            o_ref[:, 128 * j:128 * (j + 1)] = acc.astype(BF16)
        lse_ref[...] = lse_acc

    prev = lambda n: jnp.maximum(n - 1, 0)
    return pl.pallas_call(
        body, name="attn_fwd", grid=(nb,),
        in_specs=[pl.BlockSpec(memory_space=pltpu.SMEM),
                  pl.BlockSpec((BLK, 512), lambda n: (n, 0)),
                  pl.BlockSpec((BLK, 128), lambda n: (n, 4)),
                  pl.BlockSpec((BLK, 128), lambda n: (prev(n), 4)),
                  pl.BlockSpec((BLK, 128), lambda n: (n, 5)),
                  pl.BlockSpec((BLK, 128), lambda n: (prev(n), 5))],
        out_specs=[pl.BlockSpec((BLK, 512), lambda n: (n, 0)), pl.BlockSpec((BLK, 128), lambda n: (n, 0))],
        out_shape=[jax.ShapeDtypeStruct((T, 512), BF16), jax.ShapeDtypeStruct((T, 128), F32)],
        compiler_params=_cp("parallel"),
    )(sinks, qkv, qkv, qkv, qkv, qkv)


def _attn_bwd(qkv, sinks, lse, dmix, cos, sin_s):
    T = qkv.shape[0]
    nb = T // BLK

    def body(sink_ref, q_ref, kc_ref, kp_ref, vc_ref, vp_ref, lse_ref, do_ref, cq_ref, sq_ref, ck_ref, sk_ref,
             dq_ref, dk_ref, dv_ref, ds_ref, dk_car, dv_car):
        n = pl.program_id(0)
        lane = _iota((BLK, 128), 1)

        @pl.when(n == 0)
        def _():
            ds_ref[...] = jnp.zeros_like(ds_ref)
            dk_car[...] = jnp.zeros_like(dk_car)
            dv_car[...] = jnp.zeros_like(dv_car)

        @pl.when(n < nb)
        def _():
            valid = _attn_mask(n)
            kcat = jnp.concatenate([kp_ref[...], kc_ref[...]], axis=0)
            vcat = jnp.concatenate([vp_ref[...], vc_ref[...]], axis=0)
            kvar = _kv_variants(kcat.astype(F32))
            lse_v = lse_ref[...]
            dkc = jnp.zeros((2 * BLK, 128), F32)
            dvc = jnp.zeros((2 * BLK, 128), F32)
            dsk = jnp.zeros((1, 128), F32)
            for j in range(4):
                qv = _head_variants(q_ref[:, 128 * j:128 * (j + 1)].astype(F32), j)
                dov = _head_variants(do_ref[:, 128 * j:128 * (j + 1)], j)
                dq_acc = jnp.zeros((BLK, 128), F32)
                for par in range(2):
                    h = 2 * j + par
                    sink = sink_ref[0, h]
                    lse_h = jnp.sum(jnp.where(lane == h, lse_v, 0.0), axis=1, keepdims=True)
                    s = jnp.where(valid, _dot_nt(qv[par], kcat) * 0.125, NEG)
                    p = jnp.exp(s - lse_h)
                    dp = _dot_nt(dov[par], vcat)
                    delta = jnp.sum(p * dp, axis=1, keepdims=True)
                    dsc = (p * (dp - delta) * 0.125).astype(BF16)
                    dq_acc = dq_acc + _dot(dsc, kvar[(j // 2, par)])
                    dkc = dkc + _dot_tn(dsc, qv[par])
                    dvc = dvc + _dot_tn(p.astype(BF16), dov[par])
                    psink = jnp.exp(sink - lse_h)
                    dsk = dsk + jnp.where(lane[0:1] == h, -jnp.sum(psink * delta), 0.0)
                dq_ref[:, 128 * j:128 * (j + 1)] = _rope_bwd(dq_acc, cq_ref[...], sq_ref[...]).astype(BF16)
            ds_ref[...] += dsk
            dk_ref[...] = _rope_bwd(dk_car[...] + dkc[:BLK], ck_ref[...], sk_ref[...]).astype(BF16)
            dv_ref[...] = (dv_car[...] + dvc[:BLK]).astype(BF16)
            dk_car[...] = dkc[BLK:]
            dv_car[...] = dvc[BLK:]

        @pl.when(n == nb)
        def _():
            dk_ref[...] = _rope_bwd(dk_car[...], ck_ref[...], sk_ref[...]).astype(BF16)
            dv_ref[...] = dv_car[...].astype(BF16)

    cur = lambda n: jnp.minimum(n, nb - 1)
    prev = lambda n: jnp.maximum(cur(n) - 1, 0)
    outb = lambda n: jnp.maximum(n - 1, 0)
    return pl.pallas_call(
        body, name="attn_bwd", grid=(nb + 1,),
        in_specs=[pl.BlockSpec(memory_space=pltpu.SMEM),
                  pl.BlockSpec((BLK, 512), lambda n: (cur(n), 0)),
                  pl.BlockSpec((BLK, 128), lambda n: (cur(n), 4)),
                  pl.BlockSpec((BLK, 128), lambda n: (prev(n), 4)),
                  pl.BlockSpec((BLK, 128), lambda n: (cur(n), 5)),
                  pl.BlockSpec((BLK, 128), lambda n: (prev(n), 5)),
                  pl.BlockSpec((BLK, 128), lambda n: (cur(n), 0)),
                  pl.BlockSpec((BLK, 512), lambda n: (cur(n), 0)),
                  pl.BlockSpec((BLK, 128), lambda n: (cur(n), 0)),
                  pl.BlockSpec((BLK, 128), lambda n: (cur(n), 0)),
                  pl.BlockSpec((BLK, 128), lambda n: (outb(n), 0)),
                  pl.BlockSpec((BLK, 128), lambda n: (outb(n), 0))],
        out_specs=[pl.BlockSpec((BLK, 512), lambda n: (cur(n), 0)),
                   pl.BlockSpec((BLK, 128), lambda n: (outb(n), 0)),
                   pl.BlockSpec((BLK, 128), lambda n: (outb(n), 0)),
                   pl.BlockSpec((1, 128), lambda n: (0, 0))],
        out_shape=[jax.ShapeDtypeStruct((T, 512), BF16), jax.ShapeDtypeStruct((T, 128), BF16),
                   jax.ShapeDtypeStruct((T, 128), BF16), jax.ShapeDtypeStruct((1, 128), F32)],
        scratch_shapes=[pltpu.VMEM((BLK, 128), F32), pltpu.VMEM((BLK, 128), F32)],
        compiler_params=_cp("arbitrary"),
    )(sinks, qkv, qkv, qkv, qkv, qkv, lse, dmix, cos, sin_s, cos, sin_s)


def _expand_mat():
    return (_iota((128, SW), 1) // HD == _iota((128, SW), 0)).astype(F32)


def _expand_mat_t():
    return (_iota((SW, 128), 0) // HD == _iota((SW, 128), 1)).astype(F32)


def _conv_shifts(u, up):
    row = _iota(u.shape, 0)
    out = [u]
    for j in range(1, CONVK):
        out.append(jnp.where(row < j, pltpu.roll(up, j, 0), pltpu.roll(u, j, 0)))
    return out


def _ssd_parts(u, up, cw_ref, cb_ref, dtr, dtb, alog):
    sh = _conv_shifts(u, up)
    co = cb_ref[...] + cw_ref[3:4, :] * sh[0]
    for j in range(1, CONVK):
        co = co + cw_ref[3 - j:4 - j, :] * sh[j]
    sg = _sigmoid(co)
    xc = co * sg
    xx = dtr + dtb
    dt = jnp.maximum(xx, 0.0) + jnp.log(1.0 + jnp.exp(-jnp.abs(xx)))
    a_neg = -jnp.exp(alog)
    tril = _iota((BLK, BLK), 1) <= _iota((BLK, BLK), 0)
    cs = _dot_hi(tril.astype(F32), dt * a_neg)
    e_mat = _expand_mat()
    csx = _dot_hi(cs, e_mat)
    last = csx[BLK - 1:BLK, :]
    return dict(sh=sh, co=co, sg=sg, xc=xc, xx=xx, dt=dt, a_neg=a_neg, tril=tril, cs=cs, cs_t=cs.T,
                ecsx=jnp.exp(csx), dtex=jnp.exp(last - csx), cdx=jnp.exp(last), dtx=_dot_hi(dt, e_mat))


def _decay(parts, h):
    seg = parts["cs"][:, h:h + 1] - parts["cs_t"][h:h + 1, :]
    return jnp.exp(jnp.where(parts["tril"], seg, NEG))


def _group_cols(a, g):
    return a[:, 256 * g:256 * (g + 1)]


def _ssd_fwd(xbc, z, dtr, conv_w, conv_b, dtb, alog, dskx, ssm_w):
    T = xbc.shape[0]
    nc = T // BLK

    def body(u_ref, up_ref, z_ref, dtr_ref, cw_ref, cb_ref, dtb_ref, al_ref, dk_ref, sw_ref,
             yn_ref, yp_ref, st_ref, s_scr):
        n = pl.program_id(0)

        @pl.when(n == 0)
        def _():
            s_scr[...] = jnp.zeros_like(s_scr)

        u = u_ref[...]
        up = jnp.where(n > 0, up_ref[...], 0.0)
        pt = _ssd_parts(u, up, cw_ref, cb_ref, dtr_ref[...], dtb_ref[...], al_ref[...])
        xc = pt["xc"]
        xs = xc[:, :SW]
        bm = [xc[:, 512:640].astype(BF16), xc[:, 640:768].astype(BF16)]
        cm = [xc[:, 768:896].astype(BF16), xc[:, 896:1024].astype(BF16)]
        s_in = s_scr[...]
        st_ref[0] = s_in
        xdt = xs * pt["dtx"]
        xde = (xdt * pt["dtex"]).astype(BF16)
        lane = _iota((BLK, 128), 1)
        lo = lane < 64
        ys, s_new = [], []
        for g in range(2):
            cb = _dot_nt(cm[g], bm[g])
            yoff = _dot(cm[g], _group_cols(s_in, g).astype(BF16))
            s_new.append(_dot_tn(bm[g], _group_cols(xde, g)))
            for jj in range(2):
                j = 2 * g + jj
                chunk = xdt[:, 128 * j:128 * (j + 1)]
                g_ev = (cb * _decay(pt, 2 * j)).astype(BF16)
                g_od = (cb * _decay(pt, 2 * j + 1)).astype(BF16)
                yd = _dot(g_ev, jnp.where(lo, chunk, 0.0).astype(BF16)) + _dot(g_od, jnp.where(lo, 0.0, chunk).astype(BF16))
                ys.append(yd + yoff[:, 128 * jj:128 * (jj + 1)] * pt["ecsx"][:, 128 * j:128 * (j + 1)])
        y = jnp.concatenate(ys, axis=1) + xs * dk_ref[...]
        s_scr[...] = s_in * pt["cdx"] + jnp.concatenate(s_new, axis=1)
        yp_ref[...] = y
        zv = z_ref[...]
        yz = y * (zv * _sigmoid(zv))
        outs = []
        for g in range(2):
            yg = _group_cols(yz, g)
            outs.append(yg * lax.rsqrt(jnp.mean(yg * yg, axis=-1, keepdims=True) + EPS))
        yn_ref[...] = (jnp.concatenate(outs, axis=1) * sw_ref[...]).astype(BF16)

    prev = lambda n: jnp.maximum(n - 1, 0)
    full = lambda a: pl.BlockSpec(a.shape, lambda n: (0,) * a.ndim)
    return pl.pallas_call(
        body, name="ssd_fwd", grid=(nc,),
        in_specs=[pl.BlockSpec((BLK, CONVC), lambda n: (n, 0)), pl.BlockSpec((BLK, CONVC), lambda n: (prev(n), 0)),
                  pl.BlockSpec((BLK, SW), lambda n: (n, 0)), pl.BlockSpec((BLK, 128), lambda n: (n, 0)),
                  full(conv_w), full(conv_b), full(dtb), full(alog), full(dskx), full(ssm_w)],
        out_specs=[pl.BlockSpec((BLK, SW), lambda n: (n, 0)), pl.BlockSpec((BLK, SW), lambda n: (n, 0)),
                   pl.BlockSpec((1, NST, SW), lambda n: (n, 0, 0))],
        out_shape=[jax.ShapeDtypeStruct((T, SW), BF16), jax.ShapeDtypeStruct((T, SW), F32),
                   jax.ShapeDtypeStruct((nc, NST, SW), F32)],
        scratch_shapes=[pltpu.VMEM((NST, SW), F32)],
        compiler_params=_cp("arbitrary"),
    )(xbc, xbc, z, dtr, conv_w, conv_b, dtb, alog, dskx, ssm_w)


def _ssd_bwd(xbc, z, dtr, ypre, states, dmix, conv_w, conv_b, dtb, alog, dskx, ssm_w):
    T = xbc.shape[0]
    nc = T // BLK

    def body(u_ref, up_ref, z_ref, dtr_ref, yp_ref, st_ref, dyn_ref, cw_ref, cb_ref, dtb_ref, al_ref, dk_ref, sw_ref,
             out_ref, dcw_ref, dcb_ref, dsw_ref, dsk_ref, ddtb_ref, dav_ref, ds_scr, dco_scr, dskx_scr):
        i = pl.program_id(0)
        n = nc - 1 - i

        @pl.when(i == 0)
        def _():
            for r in (dcw_ref, dcb_ref, dsw_ref, dsk_ref, ddtb_ref, dav_ref, ds_scr, dco_scr, dskx_scr):
                r[...] = jnp.zeros_like(r)

        u = u_ref[...]
        up = jnp.where(n > 0, up_ref[...], 0.0)
        pt = _ssd_parts(u, up, cw_ref, cb_ref, dtr_ref[...], dtb_ref[...], al_ref[...])
        xc, dtx, ecsx, dtex, cdx = pt["xc"], pt["dtx"], pt["ecsx"], pt["dtex"], pt["cdx"]
        xs = xc[:, :SW]
        bm = [xc[:, 512:640].astype(BF16), xc[:, 640:768].astype(BF16)]
        cm = [xc[:, 768:896].astype(BF16), xc[:, 896:1024].astype(BF16)]
        s_in = st_ref[0]
        ds_out = ds_scr[...]
        e_t = _expand_mat_t()

        zv = z_ref[...]
        sz = _sigmoid(zv)
        silu_z = zv * sz
        ypre = yp_ref[...]
        yz = ypre * silu_z
        dyn = dyn_ref[...]
        sw = sw_ref[...]
        dyz, yns = [], []
        for g in range(2):
            yg = _group_cols(yz, g)
            r = lax.rsqrt(jnp.mean(yg * yg, axis=-1, keepdims=True) + EPS)
            yn = yg * r
            dg = _group_cols(dyn, g) * _group_cols(sw, g)
            dyz.append(r * (dg - yn * jnp.mean(dg * yn, axis=-1, keepdims=True)))
            yns.append(yn)
        dyz = jnp.concatenate(dyz, axis=1)
        dsw_ref[...] += jnp.sum(dyn * jnp.concatenate(yns, axis=1), axis=0, keepdims=True)
        dy = dyz * silu_z
        dz = dyz * ypre * (sz * (1.0 + zv * (1.0 - sz)))

        xdt = xs * dtx
        xdt_b = xdt.astype(BF16)
        edy = (ecsx * dy).astype(BF16)
        xde = (xdt * dtex).astype(BF16)
        lane = _iota((BLK, 128), 1)
        lo = lane < 64
        row8 = _iota((8, 128), 0)
        dcs = jnp.zeros((BLK, 128), F32)
        col_rows = jnp.zeros((8, 128), F32)
        dxdt, bds, yoff, dbs, dcs_g, ds_new = [], [], [], [], [], []
        for g in range(2):
            s_g = _group_cols(s_in, g).astype(BF16)
            dso_g = _group_cols(ds_out, g).astype(BF16)
            cb = _dot_nt(cm[g], bm[g])
            bds.append(_dot(bm[g], dso_g))
            yoff.append(_dot(cm[g], s_g))
            dcb_g = jnp.zeros((BLK, BLK), F32)
            for jj in range(2):
                j = 2 * g + jj
                dy_c = dy[:, 128 * j:128 * (j + 1)]
                xdt_c = xdt_b[:, 128 * j:128 * (j + 1)]
                acc = jnp.zeros((BLK, 128), F32)
                for par in range(2):
                    h = 2 * j + par
                    lm = _decay(pt, h)
                    gm = cb * lm
                    dy_m = (jnp.where(lo, dy_c, 0.0) if par == 0 else jnp.where(lo, 0.0, dy_c)).astype(BF16)
                    dg_h = _dot_nt(dy_m, xdt_c)
                    w_h = dg_h * gm
                    dcs = dcs + jnp.where(lane == h, jnp.sum(w_h, axis=1, keepdims=True), 0.0)
                    col_rows = col_rows + jnp.where(row8 == h, jnp.sum(w_h, axis=0, keepdims=True), 0.0)
                    dcb_g = dcb_g + dg_h * lm
                    acc = acc + _dot_tn(gm.astype(BF16), dy_m)
                dxdt.append(acc)
            dcb_b = dcb_g.astype(BF16)
            dcs_g.append(_dot(dcb_b, bm[g]) + _dot_nt(_group_cols(edy, g), s_g))
            dbs.append(_dot_tn(dcb_b, cm[g]) + _dot_nt(_group_cols(xde, g), dso_g))
            ds_new.append(_dot_tn(cm[g], _group_cols(edy, g)))
        bds = jnp.concatenate(bds, axis=1)
        yoff = jnp.concatenate(yoff, axis=1) * ecsx
        dxdt = jnp.concatenate(dxdt, axis=1) + dtex * bds
        ds_scr[...] = cdx * ds_out + jnp.concatenate(ds_new, axis=1)

        t_m = _dot_hi(dtex * xdt * bds, e_t)
        colsum_t = jnp.concatenate([col_rows, jnp.zeros((BLK - 8, 128), F32)], axis=0).T
        cd = jnp.exp(pt["cs"][BLK - 1:BLK, :])
        sds = jnp.sum(s_in * ds_out, axis=0, keepdims=True)
        last_row = jnp.sum(t_m, axis=0, keepdims=True) + cd * _dot_hi(jnp.broadcast_to(sds, (8, SW)), e_t)[0:1]
        dcs = dcs - colsum_t + _dot_hi(dy * yoff, e_t) - t_m
        dcs = dcs + jnp.where(_iota((BLK, 128), 0) == BLK - 1, last_row, 0.0)
        triu = (_iota((BLK, BLK), 1) >= _iota((BLK, BLK), 0)).astype(F32)
        da = _dot_hi(triu, dcs)
        dt = pt["dt"]
        ddt = da * pt["a_neg"] + _dot_hi(dxdt * xs, e_t)
        dav_ref[...] += jnp.sum(da * dt, axis=0, keepdims=True)
        ddtr = ddt * _sigmoid(pt["xx"])
        ddtb_ref[...] += jnp.sum(ddtr, axis=0, keepdims=True)
        dxs = dxdt * dtx + dy * dk_ref[...]
        dskx_scr[...] += jnp.sum(dy * xs, axis=0, keepdims=True)
        dxc = jnp.concatenate([dxs, dbs[0], dbs[1], dcs_g[0], dcs_g[1]], axis=1)
        co, sg = pt["co"], pt["sg"]
        dco = dxc * (sg * (1.0 + co * (1.0 - sg)))

        dcb_ref[...] += jnp.sum(dco, axis=0, keepdims=True)
        sh = pt["sh"]
        for j in range(CONVK):
            dcw_ref[3 - j:4 - j, :] += jnp.sum(dco * sh[j], axis=0, keepdims=True)
        dnext = dco_scr[...]
        rowc = _iota(dco.shape, 0)
        du = cw_ref[3:4, :] * dco
        for j in range(1, CONVK):
            up_j = jnp.where(rowc >= BLK - j, pltpu.roll(dnext, BLK - j, 0), pltpu.roll(dco, BLK - j, 0))
            du = du + cw_ref[3 - j:4 - j, :] * up_j
        dco_scr[...] = dco
        out_ref[:, 0:512] = dz.astype(BF16)
        out_ref[:, 512:1536] = du.astype(BF16)
        out_ref[:, 1536:1664] = ddtr.astype(BF16)

        @pl.when(i == nc - 1)
        def _():
            dsk_ref[...] = _dot_hi(jnp.broadcast_to(dskx_scr[...], (8, SW)), e_t)[0:1]

    rev = lambda i: nc - 1 - i
    prev = lambda i: jnp.maximum(nc - 2 - i, 0)
    full = lambda a: pl.BlockSpec(a.shape, lambda i: (0,) * a.ndim)
    acc = lambda r, c: pl.BlockSpec((r, c), lambda i: (0, 0))
    return pl.pallas_call(
        body, name="ssd_bwd", grid=(nc,),
        in_specs=[pl.BlockSpec((BLK, CONVC), lambda i: (rev(i), 0)), pl.BlockSpec((BLK, CONVC), lambda i: (prev(i), 0)),
                  pl.BlockSpec((BLK, SW), lambda i: (rev(i), 0)), pl.BlockSpec((BLK, 128), lambda i: (rev(i), 0)),
                  pl.BlockSpec((BLK, SW), lambda i: (rev(i), 0)), pl.BlockSpec((1, NST, SW), lambda i: (rev(i), 0, 0)),
                  pl.BlockSpec((BLK, SW), lambda i: (rev(i), 1)),
                  full(conv_w), full(conv_b), full(dtb), full(alog), full(dskx), full(ssm_w)],
        out_specs=[pl.BlockSpec((BLK, 1664), lambda i: (rev(i), 0)),
                   acc(CONVK, CONVC), acc(1, CONVC), acc(1, SW), acc(1, 128), acc(1, 128), acc(1, 128)],
        out_shape=[jax.ShapeDtypeStruct((T, 1664), BF16),
                   jax.ShapeDtypeStruct((CONVK, CONVC), F32), jax.ShapeDtypeStruct((1, CONVC), F32),
                   jax.ShapeDtypeStruct((1, SW), F32), jax.ShapeDtypeStruct((1, 128), F32),
                   jax.ShapeDtypeStruct((1, 128), F32), jax.ShapeDtypeStruct((1, 128), F32)],
        scratch_shapes=[pltpu.VMEM((NST, SW), F32), pltpu.VMEM((BLK, CONVC), F32), pltpu.VMEM((1, SW), F32)],
        compiler_params=_cp("arbitrary"),
    )(xbc, xbc, z, dtr, ypre, states, dmix, conv_w, conv_b, dtb, alog, dskx, ssm_w)


def _mix_ffn(x, attn, ynorm, tgt, mod6, norm2_w, final_w, w_out, w_gu, w_dn, tm):
    T = x.shape[0]
    nt = T // tm

    def body(x_ref, a_ref, y_ref, t_ref, mod_ref, n2_ref, fw_ref, wo_hbm, wgu_hbm, wdn_hbm,
             sq_ref, dmix_ref, dx1_ref, h2_ref, act_ref, df_ref, dgu_ref, do_ref, sm_ref,
             wo, wgu, wdn, sems):
        i = pl.program_id(0)

        @pl.when(i == 0)
        def _():
            cps = [pltpu.make_async_copy(s, d, sems.at[k]) for k, (s, d) in
                   enumerate(((wo_hbm, wo), (wgu_hbm, wgu), (wdn_hbm, wdn)))]
            for c in cps:
                c.start()
            for c in cps:
                c.wait()
            sq_ref[...] = jnp.zeros_like(sq_ref)
            sm_ref[...] = jnp.zeros_like(sm_ref)

        gate1, shift2, scale2, gate2 = mod_ref[2:3, :], mod_ref[3:4, :], mod_ref[4:5, :], mod_ref[5:6, :]
        n2w, fw = n2_ref[...], fw_ref[...]
        o = _dot(a_ref[...], wo[0:AW, :]) + _dot(y_ref[...], wo[AW:D, :])
        x1 = x_ref[...] + gate1 * o
        r2 = lax.rsqrt(jnp.mean(x1 * x1, axis=-1, keepdims=True) + EPS)
        xh2 = x1 * r2
        n2 = xh2 * n2w
        h2b = (n2 * (1.0 + scale2) + shift2).astype(BF16)
        h2_ref[...] = h2b
        f = jnp.zeros((tm, D), F32)
        saved = []
        for p in range(2):
            gp = _dot(h2b, wgu[p])
            upj = _dot(h2b, wgu[p + 2])
            sg = _sigmoid(gp)
            sl = gp * sg
            actb = (sl * upj).astype(BF16)
            act_ref[p] = actb
            f = f + _dot(actb, wdn[GU_SH * p:GU_SH * (p + 1), :])
            saved.append((gp, upj, sg, sl))
        x2 = x1 + gate2 * f
        r3 = lax.rsqrt(jnp.mean(x2 * x2, axis=-1, keepdims=True) + EPS)
        xh3 = x2 * r3
        err = xh3 * fw - t_ref[...]
        sq_ref[...] += jnp.sum(err * err, axis=0, keepdims=True)
        dy = err * (1.0 / D)
        dfw = jnp.sum(dy * xh3, axis=0, keepdims=True)
        dxh3 = dy * fw
        dx2 = r3 * (dxh3 - xh3 * jnp.mean(dxh3 * xh3, axis=-1, keepdims=True))
        dgate2 = jnp.sum(dx2 * f, axis=0, keepdims=True)
        dfb = (dx2 * gate2).astype(BF16)
        df_ref[...] = dfb
        dh2 = jnp.zeros((tm, D), F32)
        for p in range(2):
            gp, upj, sg, sl = saved[p]
            dact = _dot_nt(dfb, wdn[GU_SH * p:GU_SH * (p + 1), :])
            dg = (dact * upj * (sg * (1.0 + gp * (1.0 - sg)))).astype(BF16)
            du = (dact * sl).astype(BF16)
            dgu_ref[p] = dg
            dgu_ref[p + 2] = du
            dh2 = dh2 + _dot_nt(dg, wgu[p]) + _dot_nt(du, wgu[p + 2])
        dshift2 = jnp.sum(dh2, axis=0, keepdims=True)
        dscale2 = jnp.sum(dh2 * n2, axis=0, keepdims=True)
        dn2 = dh2 * (1.0 + scale2)
        dn2w = jnp.sum(dn2 * xh2, axis=0, keepdims=True)
        dxh2 = dn2 * n2w
        dx1 = dx2 + r2 * (dxh2 - xh2 * jnp.mean(dxh2 * xh2, axis=-1, keepdims=True))
        dx1_ref[...] = dx1
        dgate1 = jnp.sum(dx1 * o, axis=0, keepdims=True)
        dob = (dx1 * gate1).astype(BF16)
        do_ref[...] = dob
        dmix_ref[...] = _dot_nt(dob, wo[...])
        sm_ref[...] += jnp.concatenate(
            [dfw, dn2w, dshift2, dscale2, dgate2, dgate1, jnp.zeros((2, D), F32)], axis=0)

    row = lambda w: pl.BlockSpec((tm, w), lambda i: (i, 0))
    full = lambda a: pl.BlockSpec(a.shape, lambda i: (0,) * a.ndim)
    anyspec = pl.BlockSpec(memory_space=pl.ANY)
    return pl.pallas_call(
        body, name="mix_ffn", grid=(nt,),
        in_specs=[row(D), row(AW), row(SW), row(D), full(mod6), full(norm2_w), full(final_w), anyspec, anyspec, anyspec],
        out_specs=[pl.BlockSpec((1, D), lambda i: (0, 0)), row(D), row(D), row(D),
                   pl.BlockSpec((2, tm, GU_SH), lambda i: (0, i, 0)), row(D),
                   pl.BlockSpec((4, tm, GU_SH), lambda i: (0, i, 0)), row(D),
                   pl.BlockSpec((8, D), lambda i: (0, 0))],
        out_shape=[jax.ShapeDtypeStruct((1, D), F32), jax.ShapeDtypeStruct((T, D), F32), jax.ShapeDtypeStruct((T, D), F32),
                   jax.ShapeDtypeStruct((T, D), BF16), jax.ShapeDtypeStruct((2, T, GU_SH), BF16),
                   jax.ShapeDtypeStruct((T, D), BF16), jax.ShapeDtypeStruct((4, T, GU_SH), BF16),
                   jax.ShapeDtypeStruct((T, D), BF16), jax.ShapeDtypeStruct((8, D), F32)],
        scratch_shapes=[pltpu.VMEM((D, D), BF16), pltpu.VMEM((4, D, GU_SH), BF16), pltpu.VMEM((DFF, D), BF16),
                        pltpu.SemaphoreType.DMA((3,))],
        compiler_params=_cp("arbitrary"),
    )(x, attn, ynorm, tgt, mod6, norm2_w, final_w, w_out, w_gu, w_dn)


def _in_proj_bwd(x, dx1, dq, dk, dv, dzxd, mod6, norm1_w, w_pad, tm):
    T = x.shape[0]

    def body(x_ref, dx1_ref, dq_ref, dk_ref, dv_ref, dz_ref, mod_ref, nw_ref, w_hbm, gx_ref, sm_ref, w_vmem, sem):
        _load_resident(w_hbm, w_vmem, sem)

        @pl.when(pl.program_id(0) == 0)
        def _():
            sm_ref[...] = jnp.zeros_like(sm_ref)

        dh = (_dot_nt(dq_ref[...], w_vmem[:, 0:512]) + _dot_nt(dk_ref[...], w_vmem[:, 512:640])
              + _dot_nt(dv_ref[...], w_vmem[:, 640:768]) + _dot_nt(dz_ref[...], w_vmem[:, 768:IN_PAD]))
        xv = x_ref[...]
        nw = nw_ref[...]
        scale1 = mod_ref[1:2, :]
        r = lax.rsqrt(jnp.mean(xv * xv, axis=-1, keepdims=True) + EPS)
        xh = xv * r
        n1 = xh * nw
        dshift = jnp.sum(dh, axis=0, keepdims=True)
        dscale = jnp.sum(dh * n1, axis=0, keepdims=True)
        dn = dh * (1.0 + scale1)
        dnw = jnp.sum(dn * xh, axis=0, keepdims=True)
        dxh = dn * nw
        gx_ref[...] = dx1_ref[...] + r * (dxh - xh * jnp.mean(dxh * xh, axis=-1, keepdims=True))
        sm_ref[...] += jnp.concatenate([dnw, dshift, dscale, jnp.zeros((5, D), F32)], axis=0)

    row = lambda w: pl.BlockSpec((tm, w), lambda i: (i, 0))
    full = lambda a: pl.BlockSpec(a.shape, lambda i: (0,) * a.ndim)
    return pl.pallas_call(
        body, name="in_proj_bwd", grid=(T // tm,),
        in_specs=[row(D), row(D), row(512), row(128), row(128), row(1664), full(mod6), full(norm1_w),
                  pl.BlockSpec(memory_space=pl.ANY)],
        out_specs=[row(D), pl.BlockSpec((8, D), lambda i: (0, 0))],
        out_shape=[jax.ShapeDtypeStruct((T, D), F32), jax.ShapeDtypeStruct((8, D), F32)],
        scratch_shapes=[pltpu.VMEM((D, IN_PAD), BF16), pltpu.SemaphoreType.DMA],
        compiler_params=_cp("arbitrary"),
    )(x, dx1, dq, dk, dv, dzxd, mod6, norm1_w, w_pad)


def _tn_matmul(a3, b3, tt, name):
    ja, T, K = a3.shape
    jb, _, N = b3.shape
    J = max(ja, jb)

    def body(a_ref, b_ref, o_ref):
        t = pl.program_id(1)
        prod = _dot_tn(a_ref[0], b_ref[0])

        @pl.when(t == 0)
        def _():
            o_ref[0] = prod

        @pl.when(t > 0)
        def _():
            o_ref[0] += prod

    return pl.pallas_call(
        body, name=name, grid=(J, T // tt),
        in_specs=[pl.BlockSpec((1, tt, K), lambda j, t: (j if ja > 1 else 0, t, 0)),
                  pl.BlockSpec((1, tt, N), lambda j, t: (j if jb > 1 else 0, t, 0))],
        out_specs=pl.BlockSpec((1, K, N), lambda j, t: (j, 0, 0)),
        out_shape=jax.ShapeDtypeStruct((J, K, N), F32),
        compiler_params=_cp("parallel", "arbitrary"),
    )(a3, b3)


def _adam_math(w, g, m, v):
    m = B1 * m + (1.0 - B1) * g
    v = B2 * v + (1.0 - B2) * (g * g)
    m_hat = m / (1.0 - B1 ** STEP)
    v_hat = v / (1.0 - B2 ** STEP)
    delta = -LR * (m_hat / (jnp.sqrt(v_hat) + AEPS) + WD * w)
    return delta, m, v


def _adam_2d(w, g, m, v, rb, name):
    R, C = w.shape

    def body(w_ref, g_ref, m_ref, v_ref, d_ref, mo_ref, vo_ref):
        d, mn, vn = _adam_math(w_ref[...], g_ref[...], m_ref[...], v_ref[...])
        d_ref[...] = d
        mo_ref[...] = mn
        vo_ref[...] = vn

    spec = pl.BlockSpec((rb, C), lambda i: (i, 0))
    return pl.pallas_call(
        body, name=name, grid=(R // rb,), in_specs=[spec] * 4, out_specs=[spec] * 3,
        out_shape=[jax.ShapeDtypeStruct((R, C), F32)] * 3, compiler_params=_cp("parallel"),
    )(w, g, m, v)


def _adam_w_ada(sc_all, dmod_s, w, m, v, rb):
    R, C = w.shape

    def body(sc_ref, dm_ref, w_ref, m_ref, v_ref, g_ref, d_ref, mo_ref, vo_ref):
        g = lax.dot_general(sc_ref[...], dm_ref[...], (((0,), (0,)), ((), ())), precision=HI, preferred_element_type=F32)
        d, mn, vn = _adam_math(w_ref[...], g, m_ref[...], v_ref[...])
        g_ref[...] = g
        d_ref[...] = d
        mo_ref[...] = mn
        vo_ref[...] = vn

    spec = pl.BlockSpec((rb, C), lambda i: (i, 0))
    return pl.pallas_call(
        body, name="adam_w_ada", grid=(R // rb,),
        in_specs=[pl.BlockSpec((8, rb), lambda i: (0, i)), pl.BlockSpec((8, C), lambda i: (0, 0)), spec, spec, spec],
        out_specs=[spec] * 4, out_shape=[jax.ShapeDtypeStruct((R, C), F32)] * 4, compiler_params=_cp("parallel"),
    )(sc_all, dmod_s, w, m, v)


def _adam_small(grads, ws, ms, vs):
    k = len(ws)

    def body(*refs):
        g, w, m, v = refs[0:k], refs[k:2 * k], refs[2 * k:3 * k], refs[3 * k:4 * k]
        d_o, m_o, v_o = refs[4 * k:5 * k], refs[5 * k:6 * k], refs[6 * k:7 * k]
        for i in range(k):
            d, mn, vn = _adam_math(w[i][...], g[i][...], m[i][...], v[i][...])
            d_o[i][...] = d
            m_o[i][...] = mn
            v_o[i][...] = vn

    shapes = [jax.ShapeDtypeStruct(w.shape, F32) for w in ws]
    vm = pl.BlockSpec(memory_space=pltpu.VMEM)
    outs = pl.pallas_call(
        body, name="adam_small", in_specs=[vm] * (4 * k), out_specs=[vm] * (3 * k), out_shape=shapes * 3,
    )(*grads, *ws, *ms, *vs)
    return outs[0:k], outs[k:2 * k], outs[2 * k:3 * k]


def _pos():
    return lax.axis_index("x"), lax.axis_index("y"), lax.axis_index("c")


def _flip(v, bit):
    return 1 - v if bit else v


def _peer(k):
    x, y, c = _pos()
    return (_flip(x, (k >> 2) & 1), _flip(y, (k >> 1) & 1), _flip(c, k & 1))


def _logical(p):
    return 4 * p[0] + 2 * p[1] + p[2]


def _gather8(src_ref, dst_ref, send_sems, recv_sems):
    me = _logical(_pos())
    dst_ref[pl.ds(me, 1)] = src_ref[...][None]
    copies = []
    for k in range(1, 8):
        cp = pltpu.make_async_remote_copy(src_ref, dst_ref.at[me], send_sems.at[k - 1], recv_sems.at[k - 1],
                                          device_id=_peer(k), device_id_type=MESH)
        cp.start()
        copies.append(cp)
    for k in range(1, 8):
        pltpu.make_async_remote_copy(src_ref, dst_ref.at[_logical(_peer(k))], send_sems.at[k - 1], recv_sems.at[k - 1],
                                     device_id=_peer(k), device_id_type=MESH).wait_recv()
    for cp in copies:
        cp.wait_send()


def _rows_select(ref3, width):
    row = _iota((8, width), 0)
    out = jnp.zeros((8, width), F32)
    for i in range(8):
        out = jnp.where(row == i, ref3[i][:, 0:width], out)
    return out


def _mod_exchange(payload, w_ada_s, b_ada4):
    n_sh = w_ada_s.shape[1]

    def body(pay_ref, w_ref, b_ref, gat_ref, mod_ref, p3, sa, ra, sb, rb):
        x, y, c = _pos()
        me = _logical((x, y, c))
        my_s = 2 * x + y
        _gather8(pay_ref, gat_ref, sa, ra)
        cmat = _rows_select(gat_ref, D)
        prod = _dot_hi(cmat * _sigmoid(cmat), w_ref[...])
        for b in range(8):
            p3[b] = prod[b:b + 1, :]
        mod_ref[pl.ds(my_s, 1)] = p3[pl.ds(me, 1)] + b_ref[pl.ds(my_s, 1)]
        ks = (2, 4, 6)
        copies = []
        for i, k in enumerate(ks):
            pr = _peer(k)
            cp = pltpu.make_async_remote_copy(p3.at[_logical(pr)], mod_ref.at[my_s], sb.at[i], rb.at[i],
                                              device_id=pr, device_id_type=MESH)
            cp.start()
            copies.append(cp)
        for i, k in enumerate(ks):
            pr = _peer(k)
            s_src = 2 * pr[0] + pr[1]
            pltpu.make_async_remote_copy(p3.at[0], mod_ref.at[s_src], sb.at[i], rb.at[i],
                                         device_id=pr, device_id_type=MESH).wait_recv()
            mod_ref[pl.ds(s_src, 1)] = mod_ref[pl.ds(s_src, 1)] + b_ref[pl.ds(s_src, 1)]
        for cp in copies:
            cp.wait_send()

    vm = pl.BlockSpec(memory_space=pltpu.VMEM)
    return pl.pallas_call(
        body, name="mod_exchange", in_specs=[vm, vm, vm], out_specs=[vm, vm],
        out_shape=[jax.ShapeDtypeStruct((8, 1, payload.shape[1]), F32), jax.ShapeDtypeStruct((4, 1, n_sh), F32)],
        scratch_shapes=[pltpu.VMEM((8, 1, n_sh), F32), pltpu.SemaphoreType.DMA((7,)), pltpu.SemaphoreType.DMA((7,)),
                        pltpu.SemaphoreType.DMA((3,)), pltpu.SemaphoreType.DMA((3,))],
        compiler_params=pltpu.CompilerParams(vmem_limit_bytes=VMEM_LIMIT),
    )(payload, w_ada_s, b_ada4)


def _chips():
    x, y, _ = _pos()
    out = []
    for k in (1, 2, 3):
        px, py = _flip(x, (k >> 1) & 1), _flip(y, k & 1)
        out.append((px, py, 2 * px + py))
    return out


def _half_rows(ref, which):
    half = ref.shape[-2] // 2
    return pl.ds(pl.multiple_of(which * half, 8), half)


def _weight_gather(shards):
    nw = len(shards)

    def body(*refs):
        ins, outs = refs[:nw], refs[nw:2 * nw]
        send, recv, fsend, frecv, lsem = refs[2 * nw:]
        x, y, c = _pos()
        my_s = 2 * x + y
        sib = (x, y, 1 - c)
        chips = _chips()
        local = [pltpu.make_async_copy(ins[w], outs[w].at[my_s], lsem.at[w]) for w in range(nw)]
        for cp in local:
            cp.start()
        sends = []
        for w in range(nw):
            mine = _half_rows(ins[w], c)
            for k, (px, py, _) in enumerate(chips):
                cp = pltpu.make_async_remote_copy(ins[w].at[mine], outs[w].at[my_s, mine], send.at[3 * w + k],
                                                  recv.at[3 * w + k], device_id=(px, py, c), device_id_type=MESH)
                cp.start()
                sends.append(cp)
        for w in range(nw):
            mine = _half_rows(ins[w], c)
            for k, (px, py, ps) in enumerate(chips):
                got = outs[w].at[ps, mine]
                pltpu.make_async_remote_copy(got, got, send.at[3 * w + k], recv.at[3 * w + k],
                                             device_id=(px, py, c), device_id_type=MESH).wait_recv()
                cp = pltpu.make_async_remote_copy(got, got, fsend.at[3 * w + k], frecv.at[3 * w + k],
                                                  device_id=sib, device_id_type=MESH)
                cp.start()
                sends.append(cp)
        for w in range(nw):
            other = _half_rows(ins[w], 1 - c)
            for k, (px, py, ps) in enumerate(chips):
                got = outs[w].at[ps, other]
                pltpu.make_async_remote_copy(got, got, fsend.at[3 * w + k], frecv.at[3 * w + k],
                                             device_id=sib, device_id_type=MESH).wait_recv()
        for cp in sends:
            cp.wait_send()
        for cp in local:
            cp.wait()

    hbm = pl.BlockSpec(memory_space=pltpu.HBM)
    return pl.pallas_call(
        body, name="weight_gather", in_specs=[hbm] * nw, out_specs=[hbm] * nw,
        out_shape=[pltpu.HBM((4,) + s.shape, s.dtype) for s in shards],
        scratch_shapes=[pltpu.SemaphoreType.DMA((3 * nw,)), pltpu.SemaphoreType.DMA((3 * nw,)),
                        pltpu.SemaphoreType.DMA((3 * nw,)), pltpu.SemaphoreType.DMA((3 * nw,)),
                        pltpu.SemaphoreType.DMA((nw,))],
    )(*shards)


def _small_reduce(vec):
    n = vec.shape[1]

    def body(v_ref, tot_ref, gat_ref, sa, ra):
        _gather8(v_ref, gat_ref, sa, ra)
        tot = gat_ref[0]
        for i in range(1, 8):
            tot = tot + gat_ref[i]
        tot_ref[...] = tot

    vm = pl.BlockSpec(memory_space=pltpu.VMEM)
    return pl.pallas_call(
        body, name="small_reduce", in_specs=[vm], out_specs=[vm, vm],
        out_shape=[jax.ShapeDtypeStruct((1, n), F32), jax.ShapeDtypeStruct((8, 1, n), F32)],
        scratch_shapes=[pltpu.SemaphoreType.DMA((7,)), pltpu.SemaphoreType.DMA((7,))],
    )(vec)


def _sibling_swap(grads):
    nw = len(grads)

    def body(*refs):
        ins, outs = refs[:nw], refs[nw:2 * nw]
        send, recv = refs[2 * nw:]
        x, y, c = _pos()
        sib = (x, y, 1 - c)
        cps = []
        for w in range(nw):
            theirs = _half_rows(ins[w], 1 - c)
            cp = pltpu.make_async_remote_copy(ins[w].at[:, theirs], outs[w], send.at[w], recv.at[w],
                                              device_id=sib, device_id_type=MESH)
            cp.start()
            cps.append(cp)
        for cp in cps:
            cp.wait()

    hbm = pl.BlockSpec(memory_space=pltpu.HBM)
    return pl.pallas_call(
        body, name="grad_sibling_swap", in_specs=[hbm] * nw, out_specs=[hbm] * nw,
        out_shape=[pltpu.HBM((4, g.shape[1] // 2, g.shape[2]), F32) for g in grads],
        scratch_shapes=[pltpu.SemaphoreType.DMA((nw,)), pltpu.SemaphoreType.DMA((nw,))],
    )(*grads)


def _add_half(g, sib, c_arr, rb, name):
    _, R, C = g.shape
    half = R // 2
    nb = half // rb

    def body(c_ref, g_ref, s_ref, o_ref):
        o_ref[...] = g_ref[...] + s_ref[...]

    return pl.pallas_call(
        body, name=name,
        grid_spec=pltpu.PrefetchScalarGridSpec(
            num_scalar_prefetch=1, grid=(4, nb),
            in_specs=[pl.BlockSpec((1, rb, C), lambda s, i, c_ref: (s, c_ref[0] * nb + i, 0)),
                      pl.BlockSpec((1, rb, C), lambda s, i, c_ref: (s, i, 0))],
            out_specs=pl.BlockSpec((1, rb, C), lambda s, i, c_ref: (s, i, 0))),
        out_shape=jax.ShapeDtypeStruct((4, half, C), F32),
        compiler_params=_cp("parallel", "parallel"),
    )(c_arr, g, sib)


def _chip_scatter(parts):
    nw = len(parts)

    def body(*refs):
        ins, outs = refs[:nw], refs[nw:2 * nw]
        send, recv, lsem = refs[2 * nw:]
        x, y, c = _pos()
        my_s = 2 * x + y
        chips = _chips()
        local = [pltpu.make_async_copy(ins[w].at[my_s], outs[w].at[my_s], lsem.at[w]) for w in range(nw)]
        for cp in local:
            cp.start()
        cps = []
        for w in range(nw):
            for k, (px, py, ps) in enumerate(chips):
                cp = pltpu.make_async_remote_copy(ins[w].at[ps], outs[w].at[my_s], send.at[3 * w + k], recv.at[3 * w + k],
                                                  device_id=(px, py, c), device_id_type=MESH)
                cp.start()
                cps.append(cp)
        for w in range(nw):
            for k, (px, py, ps) in enumerate(chips):
                pltpu.make_async_remote_copy(ins[w].at[ps], outs[w].at[ps], send.at[3 * w + k], recv.at[3 * w + k],
                                             device_id=(px, py, c), device_id_type=MESH).wait_recv()
        for cp in cps:
            cp.wait_send()
        for cp in local:
            cp.wait()

    hbm = pl.BlockSpec(memory_space=pltpu.HBM)
    return pl.pallas_call(
        body, name="grad_chip_scatter", in_specs=[hbm] * nw, out_specs=[hbm] * nw,
        out_shape=[pltpu.HBM(p.shape, F32) for p in parts],
        scratch_shapes=[pltpu.SemaphoreType.DMA((3 * nw,)), pltpu.SemaphoreType.DMA((3 * nw,)),
                        pltpu.SemaphoreType.DMA((nw,))],
    )(*parts)


def _sum4(r, rb, name):
    _, H, C = r.shape

    def body(r_ref, o_ref):
        o_ref[...] = ((r_ref[0] + r_ref[1]) + r_ref[2]) + r_ref[3]

    return pl.pallas_call(
        body, name=name, grid=(H // rb,),
        in_specs=[pl.BlockSpec((4, rb, C), lambda i: (0, i, 0))], out_specs=pl.BlockSpec((rb, C), lambda i: (i, 0)),
        out_shape=jax.ShapeDtypeStruct((H, C), F32), compiler_params=_cp("parallel"),
    )(r)


def _sibling_join(halves):
    nw = len(halves)

    def body(*refs):
        ins, outs = refs[:nw], refs[nw:2 * nw]
        send, recv, lsem = refs[2 * nw:]
        x, y, c = _pos()
        sib = (x, y, 1 - c)
        cps, local = [], []
        for w in range(nw):
            mine = _half_rows(outs[w], c)
            lc = pltpu.make_async_copy(ins[w], outs[w].at[mine], lsem.at[w])
            lc.start()
            local.append(lc)
            cp = pltpu.make_async_remote_copy(ins[w], outs[w].at[mine], send.at[w], recv.at[w],
                                              device_id=sib, device_id_type=MESH)
            cp.start()
            cps.append(cp)
        for w in range(nw):
            other = _half_rows(outs[w], 1 - c)
            pltpu.make_async_remote_copy(ins[w], outs[w].at[other], send.at[w], recv.at[w],
                                         device_id=sib, device_id_type=MESH).wait_recv()
        for cp in cps:
            cp.wait_send()
        for lc in local:
            lc.wait()

    hbm = pl.BlockSpec(memory_space=pltpu.HBM)
    return pl.pallas_call(
        body, name="grad_sibling_join", in_specs=[hbm] * nw, out_specs=[hbm] * nw,
        out_shape=[pltpu.HBM((2 * h.shape[0], h.shape[1]), F32) for h in halves],
        scratch_shapes=[pltpu.SemaphoreType.DMA((nw,)), pltpu.SemaphoreType.DMA((nw,)), pltpu.SemaphoreType.DMA((nw,))],
    )(*halves)


HBM_SPEC = pl.BlockSpec(memory_space=pltpu.HBM)
SEM_SPEC = pl.BlockSpec(memory_space=pltpu.SEMAPHORE)
EFFECT = pltpu.SideEffectType.DATAFLOW_SIDE_EFFECTING


def _split_start(name, bufs, n_sem, plan):
    nb = len(bufs)

    def body(*refs):
        ins, send, recv, token = refs[:nb], refs[nb], refs[nb + 1], refs[-1]
        for i, (src, dst, dev, _) in enumerate(plan(ins)):
            pltpu.make_async_remote_copy(src, dst, send.at[i], recv.at[i], device_id=dev, device_id_type=MESH).start()
        token[...] = jnp.zeros_like(token)

    outs = pl.pallas_call(
        body, name=name,
        out_shape=(pltpu.SemaphoreType.DMA((n_sem,)), pltpu.SemaphoreType.DMA((n_sem,)),
                   *[pltpu.HBM(b.shape, b.dtype) for b in bufs], jax.ShapeDtypeStruct((8, 128), F32)),
        in_specs=[HBM_SPEC] * nb,
        out_specs=(SEM_SPEC, SEM_SPEC, *([HBM_SPEC] * nb), pl.BlockSpec(memory_space=pltpu.VMEM)),
        input_output_aliases={i: 2 + i for i in range(nb)},
        compiler_params=pltpu.CompilerParams(has_side_effects=EFFECT),
    )(*[pltpu.with_memory_space_constraint(b, pltpu.HBM) for b in bufs])
    return outs[0], outs[1], list(outs[2:2 + nb]), outs[-1]


def _split_wait(name, send, recv, bufs, after, plan):
    nb = len(bufs)

    def body(*refs):
        ins, send_s, recv_s = refs[:nb], refs[nb], refs[nb + 1]
        for i, (src, dst, dev, mine) in enumerate(plan(ins)):
            pltpu.make_async_remote_copy(src, dst, send_s.at[i], recv_s.at[i], device_id=dev,
                                         device_id_type=MESH).wait_send()
            pltpu.make_async_remote_copy(src, mine, send_s.at[i], recv_s.at[i], device_id=dev,
                                         device_id_type=MESH).wait_recv()

    outs = pl.pallas_call(
        body, name=name, out_shape=[pltpu.HBM(b.shape, b.dtype) for b in bufs],
        in_specs=[HBM_SPEC] * nb + [SEM_SPEC, SEM_SPEC, pl.BlockSpec(memory_space=pl.ANY)],
        out_specs=[HBM_SPEC] * nb, input_output_aliases={i: i for i in range(nb)},
        compiler_params=pltpu.CompilerParams(has_side_effects=EFFECT),
    )(*bufs, send, recv, after)
    return list(outs)


def _plan_gather_ici(nw):
    def plan(refs):
        x, y, c = _pos()
        my_s = 2 * x + y
        out = []
        for w in range(nw):
            mine = _half_rows(refs[w], c)
            for px, py, ps in _chips():
                out.append((refs[w].at[mine], refs[nw + w].at[my_s, mine], (px, py, c), refs[nw + w].at[ps, mine]))
        return out
    return plan


def _plan_gather_fwd(nw):
    def plan(refs):
        x, y, c = _pos()
        out = []
        for w in range(nw):
            mine, other = _half_rows(refs[w], c), _half_rows(refs[w], 1 - c)
            for px, py, ps in _chips():
                got = refs[w].at[ps, mine]
                out.append((got, got, (x, y, 1 - c), refs[w].at[ps, other]))
        return out
    return plan


def _plan_swap(nw):
    def plan(refs):
        x, y, c = _pos()
        return [(refs[w].at[:, _half_rows(refs[w], 1 - c)], refs[nw + w], (x, y, 1 - c), refs[nw + w])
                for w in range(nw)]
    return plan


def _plan_scatter(nw):
    def plan(refs):
        x, y, c = _pos()
        my_s = 2 * x + y
        out = []
        for w in range(nw):
            for px, py, ps in _chips():
                out.append((refs[w].at[ps], refs[nw + w].at[my_s], (px, py, c), refs[nw + w].at[ps]))
        return out
    return plan


def _hbm_empty(shape, dtype):
    return pltpu.with_memory_space_constraint(lax.empty(shape, dtype), pltpu.HBM)


def _put_slot(land, own, slot):
    return lax.dynamic_update_slice(land, own[None], (slot,) + (0,) * own.ndim)


def _pad_lanes(a, n):
    return jnp.pad(a, ((0, 0), (0, n - a.shape[1])))


def kernel(x, c, positions, w_ada, b_ada, norm1_w, w_in, conv_w, conv_b, dt_bias, a_log, d_skip, attn_sinks, ssm_norm_w, w_out, norm2_w, w_gate_up, w_down, final_norm_w, loss_target, m_w_ada, m_b_ada, m_norm1_w, m_w_in, m_conv_w, m_conv_b, m_dt_bias, m_a_log, m_d_skip, m_attn_sinks, m_ssm_norm_w, m_w_out, m_norm2_w, m_w_gate_up, m_w_down, m_final_norm_w, v_w_ada, v_b_ada, v_norm1_w, v_w_in, v_conv_w, v_conv_b, v_dt_bias, v_a_log, v_d_skip, v_attn_sinks, v_ssm_norm_w, v_w_out, v_norm2_w, v_w_gate_up, v_w_down, v_final_norm_w):
    T = x.shape[1]
    tm = min(256, T)
    xi, yi, ci = lax.axis_index("x"), lax.axis_index("y"), lax.axis_index("c")
    my_s = 2 * xi + yi
    xs = x[0]
    tgt = loss_target[0]

    payload = jnp.concatenate([c, conv_w[0].reshape(1, CONVK * 256)], axis=1)
    gat, mod4 = _mod_exchange(payload, w_ada[0], b_ada.reshape(4, 1, 1536))
    mod6 = mod4.reshape(6, D)
    c_all = gat[:, 0, 0:D]
    cw_dev = gat[:, 0, D:].reshape(4, 2, CONVK, 256)[:, 0]
    conv_full = cw_dev.transpose(1, 0, 2).reshape(CONVK, CONVC)

    (g_in,) = _weight_gather([w_in[0].astype(BF16)])
    w_pad = _pad_lanes(g_in.transpose(1, 0, 2).reshape(D, IN_PROJ), IN_PAD)
    late = [w_out[0].astype(BF16), w_gate_up[0].astype(BF16), w_down[0].astype(BF16)]
    lands = [_hbm_empty((4,) + s.shape, BF16) for s in late]
    s_a, r_a, bufs, tok = _split_start("wgather_ici_start", late + lands, 9, _plan_gather_ici(3))

    inv_freq = (10000.0 ** (-jnp.arange(32, dtype=F32) / 32))
    inv_row = jnp.tile(inv_freq, 4).reshape(1, 128)
    cos, sin_s = _rope_tables(positions.reshape(T, 1), inv_row, tm)
    qkv, z, xbc, dtr, h1b = _in_proj_fwd(xs, cos, sin_s, mod6 + tok[0, 0], norm1_w, w_pad, tm)
    sinks = attn_sinks
    attn, lse = _attn_fwd(qkv, sinks)
    bufs = _split_wait("wgather_ici_wait", s_a, r_a, bufs, attn, _plan_gather_ici(3))
    s_b, r_b, lands, tok = _split_start("wgather_fwd_start", bufs[3:], 9, _plan_gather_fwd(3))
    dtb = _pad_lanes(dt_bias, 128)
    alog = _pad_lanes(a_log, 128)
    dskx = jnp.repeat(d_skip, HD, axis=1)
    ynorm, ypre, states = _ssd_fwd(xbc, z, dtr, conv_full, conv_b, dtb + tok[0, 0], alog, dskx, ssm_norm_w)
    lands = _split_wait("wgather_fwd_wait", s_b, r_b, lands, ynorm, _plan_gather_fwd(3))
    g_out, g_gu, g_dn = [_put_slot(l, s, my_s) for l, s in zip(lands, late)]
    w_out_f = g_out.reshape(D, D)
    w_dn_f = g_dn.reshape(DFF, D)

    fw2 = final_norm_w.reshape(1, D)
    sq, dmix, dx1, h2b, act, dfb, dgu, dob, sm_ffn = _mix_ffn(
        xs, attn, ynorm, tgt, mod6, norm2_w, fw2, w_out_f, g_gu, w_dn_f, tm)
    loss = lax.psum(0.5 / D * jnp.sum(sq), ("x", "y", "c"))

    tt = min(512, T)
    c_arr = ci.reshape(1).astype(jnp.int32)
    gw_dn4 = _tn_matmul(act, dfb[None], tt, "dw_down").reshape(4, DFF // 4, D)
    gw_gu4 = _tn_matmul(h2b[None], dgu, tt, "dw_gate_up")
    gw_out4 = jnp.concatenate(
        [_tn_matmul(attn[None], dob[None], tt, "dw_out_a")[0], _tn_matmul(ynorm[None], dob[None], tt, "dw_out_y")[0]],
        axis=0).reshape(4, D // 4, D)
    big1 = [gw_out4, gw_gu4, gw_dn4]
    rbs1 = [128, 128, 88]
    sib1 = [_hbm_empty((4, g.shape[1] // 2, g.shape[2]), F32) for g in big1]
    s_c, r_c, bufs, tok = _split_start("gswap_start", big1 + sib1, 3, _plan_swap(3))

    dzxd, d_cw, d_cb, d_sw, d_sk, d_dtb, d_av = _ssd_bwd(
        xbc, z, dtr, ypre, states, dmix, conv_full, conv_b, dtb + tok[0, 0], alog, dskx, ssm_norm_w)
    bufs = _split_wait("gswap_wait", s_c, r_c, bufs, dzxd, _plan_swap(3))
    sums1 = [_add_half(g, s, c_arr, rb, "grad_add_%d" % i)
             for i, (g, s, rb) in enumerate(zip(bufs[:3], bufs[3:], rbs1))]
    land1 = [_hbm_empty(p.shape, F32) for p in sums1]
    s_d, r_d, bufs, tok = _split_start("gscatter_start", sums1 + land1, 9, _plan_scatter(3))
    dq, dk, dv, d_sinks = _attn_bwd(qkv, sinks + tok[0:1, 0:8], lse, dmix, cos, sin_s)
    grad_x, sm_in = _in_proj_bwd(xs, dx1, dq, dk, dv, dzxd, mod6, norm1_w, w_pad, tm)
    h1_3 = h1b[None]
    gw_in = jnp.concatenate(
        [_tn_matmul(h1_3, dq[None], tt, "dw_in_q")[0], _tn_matmul(h1_3, dk[None], tt, "dw_in_k")[0],
         _tn_matmul(h1_3, dv[None], tt, "dw_in_v")[0], _tn_matmul(h1_3, dzxd[None], tt, "dw_in_z")[0]], axis=1)
    gw_in4 = gw_in[:, :IN_PROJ].reshape(D, 4, IN_PROJ // 4).transpose(1, 0, 2)
    bufs = _split_wait("gscatter_wait", s_d, r_d, bufs, gw_in4, _plan_scatter(3))
    slots1 = [_put_slot(l, lax.dynamic_index_in_dim(p, my_s, 0, keepdims=False), my_s)
              for p, l in zip(bufs[:3], bufs[3:])]
    halves1 = [_sum4(r, rb, "grad_sum_%d" % i) for i, (r, rb) in enumerate(zip(slots1, rbs1))]

    (sib0,) = _sibling_swap([gw_in4])
    (slot0,) = _chip_scatter([_add_half(gw_in4, sib0, c_arr, 128, "grad_add_in")])
    half0 = _sum4(slot0, 128, "grad_sum_in")
    g_in_s, g_out_s, g_gu_s, g_dn_s = _sibling_join([half0] + halves1)

    a_neg = -jnp.exp(alog)
    pieces = [sm_in[1:2], sm_in[2:3], sm_ffn[5:6], sm_ffn[2:3], sm_ffn[3:4], sm_ffn[4:5],
              sm_in[0:1], sm_ffn[1:2], sm_ffn[0:1], d_cb, d_cw.reshape(1, CONVK * CONVC),
              _pad_lanes(d_sw, SW), d_dtb, d_av * a_neg, d_sk, d_sinks]
    vec = jnp.concatenate(pieces, axis=1)
    tot, allv = _small_reduce(vec)
    o = 0
    offs = []
    for p in pieces:
        offs.append(o)
        o += p.shape[1]
    seg = lambda i, n: tot[:, offs[i]:offs[i] + n]
    g_b_ada = tot[:, 0:6 * D]
    g_norm1, g_norm2, g_final, g_conv_b = seg(6, D), seg(7, D), seg(8, D), seg(9, D)
    g_conv_w = lax.dynamic_slice_in_dim(seg(10, CONVK * CONVC).reshape(CONVK, CONVC), my_s * 256, 256, axis=1)
    g_ssm_w, g_dtb, g_alog, g_dsk, g_sink = seg(11, SW), seg(12, 8), seg(13, 8), seg(14, 8), seg(15, 8)

    small_names = ["b_ada", "norm1_w", "conv_w", "conv_b", "dt_bias", "a_log", "d_skip", "attn_sinks", "ssm_norm_w",
                   "norm2_w", "final_norm_w"]
    small_g = [g_b_ada, g_norm1, g_conv_w, g_conv_b, g_dtb, g_alog, g_dsk, g_sink, g_ssm_w, g_norm2, g_final]
    as2d = lambda a: a.reshape(-1, a.shape[-1])
    small_w = [as2d(a) for a in (b_ada, norm1_w, conv_w, conv_b, dt_bias, a_log, d_skip, attn_sinks, ssm_norm_w,
                                 norm2_w, final_norm_w)]
    small_m = [as2d(a) for a in (m_b_ada, m_norm1_w, m_conv_w, m_conv_b, m_dt_bias, m_a_log, m_d_skip, m_attn_sinks,
                                 m_ssm_norm_w, m_norm2_w, m_final_norm_w)]
    small_v = [as2d(a) for a in (v_b_ada, v_norm1_w, v_conv_w, v_conv_b, v_dt_bias, v_a_log, v_d_skip, v_attn_sinks,
                                 v_ssm_norm_w, v_norm2_w, v_final_norm_w)]
    sd, smn, svn = _adam_small(small_g, small_w, small_m, small_v)

    sc_all = c_all * jax.nn.sigmoid(c_all)
    dmod_all = allv[:, 0, 0:6 * D]
    dmod_s = lax.dynamic_slice_in_dim(dmod_all, my_s * 1536, 1536, axis=1)
    g_ada, d_ada, m_ada, v_ada = _adam_w_ada(sc_all, dmod_s, w_ada[0], m_w_ada[0], v_w_ada[0], 256)
    d_in, m_in, v_in = _adam_2d(w_in[0], g_in_s, m_w_in[0], v_w_in[0], 256, "adam_w_in")
    d_out, m_out, v_out = _adam_2d(w_out[0], g_out_s, m_w_out[0], v_w_out[0], 256, "adam_w_out")
    d_gu, m_gu, v_gu = _adam_2d(w_gate_up[0], g_gu_s, m_w_gate_up[0], v_w_gate_up[0], 256, "adam_w_gate_up")
    d_dn, m_dn, v_dn = _adam_2d(w_down[0], g_dn_s, m_w_down[0], v_w_down[0], 352, "adam_w_down")

    order = ["w_ada", "b_ada", "norm1_w", "w_in", "conv_w", "conv_b", "dt_bias", "a_log", "d_skip", "attn_sinks",
             "ssm_norm_w", "w_out", "norm2_w", "w_gate_up", "w_down", "final_norm_w"]
    shapes = dict(w_ada=w_ada.shape, b_ada=b_ada.shape, norm1_w=norm1_w.shape, w_in=w_in.shape, conv_w=conv_w.shape,
                  conv_b=conv_b.shape, dt_bias=dt_bias.shape, a_log=a_log.shape, d_skip=d_skip.shape,
                  attn_sinks=attn_sinks.shape, ssm_norm_w=ssm_norm_w.shape, w_out=w_out.shape, norm2_w=norm2_w.shape,
                  w_gate_up=w_gate_up.shape, w_down=w_down.shape, final_norm_w=final_norm_w.shape)
    grads = dict(w_ada=g_ada, w_in=g_in_s, w_out=g_out_s, w_gate_up=g_gu_s, w_down=g_dn_s)
    deltas = dict(w_ada=d_ada, w_in=d_in, w_out=d_out, w_gate_up=d_gu, w_down=d_dn)
    new_m = dict(w_ada=m_ada, w_in=m_in, w_out=m_out, w_gate_up=m_gu, w_down=m_dn)
    new_v = dict(w_ada=v_ada, w_in=v_in, w_out=v_out, w_gate_up=v_gu, w_down=v_dn)
    for i, nme in enumerate(small_names):
        grads[nme], deltas[nme], new_m[nme], new_v[nme] = small_g[i], sd[i], smn[i], svn[i]
    outs = [loss, grad_x[None]]
    for table in (grads, deltas, new_m, new_v):
        outs += [table[nme].reshape(shapes[nme]) for nme in order]
    return tuple(outs)
```

```python
import functools
import math

import jax
import jax.numpy as jnp
from jax import lax
from jax.experimental import pallas as pl
from jax.experimental.pallas import tpu as pltpu

F32 = jnp.float32
BF16 = jnp.bfloat16
HI = lax.Precision.HIGHEST
MESH = pl.DeviceIdType.MESH

D = 1024
HD = 64
NQ = 8
AW = 512
KVW = 128
SW = 512
NST = 128
CONVK = 4
CONVC = 1024
BLK = 128
IN_PROJ = 2312
IN_PAD = 2432
DFF = 2816
GU_SH = 1408
EPS = 1e-6
NEG = -1e30
LR, B1, B2, AEPS, WD, STEP = 0.001, 0.9, 0.999, 1e-08, 0.01, 10
VMEM_LIMIT = 58 * 1024 * 1024


def _cp(*sem):
    return pltpu.CompilerParams(dimension_semantics=sem or None, vmem_limit_bytes=VMEM_LIMIT)


def _dot(a, b):
    return jnp.dot(a, b, preferred_element_type=F32)


def _dot_nt(a, b):
    return lax.dot_general(a, b, (((1,), (1,)), ((), ())), preferred_element_type=F32)


def _dot_tn(a, b):
    return lax.dot_general(a, b, (((0,), (0,)), ((), ())), preferred_element_type=F32)


def _dot_hi(a, b):
    return jnp.dot(a, b, precision=HI, preferred_element_type=F32)


def _sigmoid(x):
    return 1.0 / (1.0 + jnp.exp(-x))


def _iota(shape, dim):
    return lax.broadcasted_iota(jnp.int32, shape, dim)


def _load_resident(hbm_ref, vmem_ref, sem):
    @pl.when(pl.program_id(0) == 0)
    def _():
        cp = pltpu.make_async_copy(hbm_ref, vmem_ref, sem)
        cp.start()
        cp.wait()


def _swap32(t):
    lane = _iota(t.shape, 1)
    return jnp.where((lane & 63) < 32, pltpu.roll(t, 96, 1), pltpu.roll(t, 32, 1))


def _rope_fwd(t, cos, sin_s):
    return t * cos + _swap32(t) * sin_s


def _rope_bwd(t, cos, sin_s):
    return t * cos - _swap32(t) * sin_s


def _rope_tables(pos_col, inv_freq_row, tm):
    T = pos_col.shape[0]

    def body(p_ref, f_ref, cos_ref, sin_ref):
        ang = p_ref[...].astype(F32) * f_ref[...]
        lane = _iota((tm, 128), 1)
        s = jnp.sin(ang)
        cos_ref[...] = jnp.cos(ang)
        sin_ref[...] = jnp.where((lane & 63) < 32, -s, s)

    return pl.pallas_call(
        body, name="rope_tables", grid=(T // tm,),
        in_specs=[pl.BlockSpec((tm, 1), lambda i: (i, 0)), pl.BlockSpec((1, 128), lambda i: (0, 0))],
        out_specs=[pl.BlockSpec((tm, 128), lambda i: (i, 0))] * 2,
        out_shape=[jax.ShapeDtypeStruct((T, 128), F32)] * 2,
        compiler_params=_cp("parallel"),
    )(pos_col, inv_freq_row)


def _in_proj_fwd(x, cos, sin_s, mod6, norm1_w, w_pad, tm):
    T = x.shape[0]

    def body(x_ref, cos_ref, sin_ref, mod_ref, nw_ref, w_hbm, qkv_ref, z_ref, xbc_ref, dt_ref, h_ref, w_vmem, sem):
        _load_resident(w_hbm, w_vmem, sem)
        xv = x_ref[...]
        r = lax.rsqrt(jnp.mean(xv * xv, axis=-1, keepdims=True) + EPS)
        h = (xv * r * nw_ref[...]) * (1.0 + mod_ref[1:2, :]) + mod_ref[0:1, :]
        hb = h.astype(BF16)
        h_ref[...] = hb
        proj = _dot(hb, w_vmem[...])
        cs, sn = cos_ref[...], sin_ref[...]
        for j in range(5):
            qkv_ref[:, 128 * j:128 * (j + 1)] = _rope_fwd(proj[:, 128 * j:128 * (j + 1)], cs, sn).astype(BF16)
        qkv_ref[:, 640:768] = proj[:, 640:768].astype(BF16)
        z_ref[...] = proj[:, 768:1280]
        xbc_ref[...] = proj[:, 1280:2304]
        dt_ref[...] = proj[:, 2304:2432]

    row = lambda w: pl.BlockSpec((tm, w), lambda i: (i, 0))
    full = lambda a: pl.BlockSpec(a.shape, lambda i: (0,) * a.ndim)
    return pl.pallas_call(
        body, name="in_proj_fwd", grid=(T // tm,),
        in_specs=[row(D), row(128), row(128), full(mod6), full(norm1_w), pl.BlockSpec(memory_space=pl.ANY)],
        out_specs=[row(768), row(512), row(1024), row(128), row(D)],
        out_shape=[jax.ShapeDtypeStruct((T, 768), BF16), jax.ShapeDtypeStruct((T, 512), F32),
                   jax.ShapeDtypeStruct((T, 1024), F32), jax.ShapeDtypeStruct((T, 128), F32),
                   jax.ShapeDtypeStruct((T, D), BF16)],
        scratch_shapes=[pltpu.VMEM((D, IN_PAD), BF16), pltpu.SemaphoreType.DMA],
        compiler_params=_cp("arbitrary"),
    )(x, cos, sin_s, mod6, norm1_w, w_pad)


def _head_variants(pair, j):
    lane = _iota(pair.shape, 1)
    lo = lane < 64
    kv = j // 2
    ev = jnp.where(lo, pair, 0.0)
    od = jnp.where(lo, 0.0, pair)
    if kv == 0:
        od = pltpu.roll(od, 64, 1)
    else:
        ev = pltpu.roll(ev, 64, 1)
    return ev.astype(BF16), od.astype(BF16)


def _kv_variants(vcat):
    lane = _iota(vcat.shape, 1)
    lo = lane < 64
    v0 = jnp.where(lo, vcat, 0.0)
    v1 = jnp.where(lo, 0.0, vcat)
    out = {
        (0, 0): v0, (0, 1): pltpu.roll(v0, 64, 1),
        (1, 0): pltpu.roll(v1, 64, 1), (1, 1): v1,
    }
    return {k: v.astype(BF16) for k, v in out.items()}


def _attn_mask(n):
    i = _iota((BLK, 2 * BLK), 0)
    j = _iota((BLK, 2 * BLK), 1)
    return (j > i) & (j <= i + BLK) & ((n > 0) | (j >= BLK))


def _attn_fwd(qkv, sinks):
    T = qkv.shape[0]
    nb = T // BLK

    def body(sink_ref, q_ref, kc_ref, kp_ref, vc_ref, vp_ref, o_ref, lse_ref):
        n = pl.program_id(0)
        valid = _attn_mask(n)
        kcat = jnp.concatenate([kp_ref[...], kc_ref[...]], axis=0)
        vvar = _kv_variants(jnp.concatenate([vp_ref[...], vc_ref[...]], axis=0).astype(F32))
        lane = _iota((BLK, 128), 1)
        lse_acc = jnp.zeros((BLK, 128), F32)
        for j in range(4):
            qv = _head_variants(q_ref[:, 128 * j:128 * (j + 1)].astype(F32), j)
            acc = jnp.zeros((BLK, 128), F32)
            for par in range(2):
                h = 2 * j + par
                sink = sink_ref[0, h]
                s = jnp.where(valid, _dot_nt(qv[par], kcat) * 0.125, NEG)
                m = jnp.maximum(jnp.max(s, axis=1, keepdims=True), sink)
                p = jnp.exp(s - m)
                den = jnp.sum(p, axis=1, keepdims=True) + jnp.exp(sink - m)
                probs = (p * (1.0 / den)).astype(BF16)
                acc = acc + _dot(probs, vvar[(j // 2, par)])
                lse_acc = jnp.where(lane == h, m + jnp.log(den), lse_acc)
            o_ref[:, 128 * j:128 * (j + 1)] = acc.astype(BF16)
        lse_ref[...] = lse_acc

    prev = lambda n: jnp.maximum(n - 1, 0)
    return pl.pallas_call(
        body, name="attn_fwd", grid=(nb,),
        in_specs=[pl.BlockSpec(memory_space=pltpu.SMEM),
                  pl.BlockSpec((BLK, 512), lambda n: (n, 0)),
                  pl.BlockSpec((BLK, 128), lambda n: (n, 4)),
                  pl.BlockSpec((BLK, 128), lambda n: (prev(n), 4)),
                  pl.BlockSpec((BLK, 128), lambda n: (n, 5)),
                  pl.BlockSpec((BLK, 128), lambda n: (prev(n), 5))],
        out_specs=[pl.BlockSpec((BLK, 512), lambda n: (n, 0)), pl.BlockSpec((BLK, 128), lambda n: (n, 0))],
        out_shape=[jax.ShapeDtypeStruct((T, 512), BF16), jax.ShapeDtypeStruct((T, 128), F32)],
        compiler_params=_cp("parallel"),
    )(sinks, qkv, qkv, qkv, qkv, qkv)


def _attn_bwd(qkv, sinks, lse, dmix, cos, sin_s):
    T = qkv.shape[0]
    nb = T // BLK

    def body(sink_ref, q_ref, kc_ref, kp_ref, vc_ref, vp_ref, lse_ref, do_ref, cq_ref, sq_ref, ck_ref, sk_ref,
             dq_ref, dk_ref, dv_ref, ds_ref, dk_car, dv_car):
        n = pl.program_id(0)
        lane = _iota((BLK, 128), 1)

        @pl.when(n == 0)
        def _():
            ds_ref[...] = jnp.zeros_like(ds_ref)
            dk_car[...] = jnp.zeros_like(dk_car)
            dv_car[...] = jnp.zeros_like(dv_car)

        @pl.when(n < nb)
        def _():
            valid = _attn_mask(n)
            kcat = jnp.concatenate([kp_ref[...], kc_ref[...]], axis=0)
            vcat = jnp.concatenate([vp_ref[...], vc_ref[...]], axis=0)
            kvar = _kv_variants(kcat.astype(F32))
            lse_v = lse_ref[...]
            dkc = jnp.zeros((2 * BLK, 128), F32)
            dvc = jnp.zeros((2 * BLK, 128), F32)
            dsk = jnp.zeros((1, 128), F32)
            for j in range(4):
                qv = _head_variants(q_ref[:, 128 * j:128 * (j + 1)].astype(F32), j)
                dov = _head_variants(do_ref[:, 128 * j:128 * (j + 1)], j)
                dq_acc = jnp.zeros((BLK, 128), F32)
                for par in range(2):
                    h = 2 * j + par
                    sink = sink_ref[0, h]
                    lse_h = jnp.sum(jnp.where(lane == h, lse_v, 0.0), axis=1, keepdims=True)
                    s = jnp.where(valid, _dot_nt(qv[par], kcat) * 0.125, NEG)
                    p = jnp.exp(s - lse_h)
                    dp = _dot_nt(dov[par], vcat)
                    delta = jnp.sum(p * dp, axis=1, keepdims=True)
                    dsc = (p * (dp - delta) * 0.125).astype(BF16)
                    dq_acc = dq_acc + _dot(dsc, kvar[(j // 2, par)])
                    dkc = dkc + _dot_tn(dsc, qv[par])
                    dvc = dvc + _dot_tn(p.astype(BF16), dov[par])
                    psink = jnp.exp(sink - lse_h)
                    dsk = dsk + jnp.where(lane[0:1] == h, -jnp.sum(psink * delta), 0.0)
                dq_ref[:, 128 * j:128 * (j + 1)] = _rope_bwd(dq_acc, cq_ref[...], sq_ref[...]).astype(BF16)
            ds_ref[...] += dsk
            dk_ref[...] = _rope_bwd(dk_car[...] + dkc[:BLK], ck_ref[...], sk_ref[...]).astype(BF16)
            dv_ref[...] = (dv_car[...] + dvc[:BLK]).astype(BF16)
            dk_car[...] = dkc[BLK:]
            dv_car[...] = dvc[BLK:]

        @pl.when(n == nb)
        def _():
            dk_ref[...] = _rope_bwd(dk_car[...], ck_ref[...], sk_ref[...]).astype(BF16)
            dv_ref[...] = dv_car[...].astype(BF16)

    cur = lambda n: jnp.minimum(n, nb - 1)
    prev = lambda n: jnp.maximum(cur(n) - 1, 0)
    outb = lambda n: jnp.maximum(n - 1, 0)
    return pl.pallas_call(
        body, name="attn_bwd", grid=(nb + 1,),
        in_specs=[pl.BlockSpec(memory_space=pltpu.SMEM),
                  pl.BlockSpec((BLK, 512), lambda n: (cur(n), 0)),
                  pl.BlockSpec((BLK, 128), lambda n: (cur(n), 4)),
                  pl.BlockSpec((BLK, 128), lambda n: (prev(n), 4)),
                  pl.BlockSpec((BLK, 128), lambda n: (cur(n), 5)),
                  pl.BlockSpec((BLK, 128), lambda n: (prev(n), 5)),
                  pl.BlockSpec((BLK, 128), lambda n: (cur(n), 0)),
                  pl.BlockSpec((BLK, 512), lambda n: (cur(n), 0)),
                  pl.BlockSpec((BLK, 128), lambda n: (cur(n), 0)),
                  pl.BlockSpec((BLK, 128), lambda n: (cur(n), 0)),
                  pl.BlockSpec((BLK, 128), lambda n: (outb(n), 0)),
                  pl.BlockSpec((BLK, 128), lambda n: (outb(n), 0))],
        out_specs=[pl.BlockSpec((BLK, 512), lambda n: (cur(n), 0)),
                   pl.BlockSpec((BLK, 128), lambda n: (outb(n), 0)),
                   pl.BlockSpec((BLK, 128), lambda n: (outb(n), 0)),
                   pl.BlockSpec((1, 128), lambda n: (0, 0))],
        out_shape=[jax.ShapeDtypeStruct((T, 512), BF16), jax.ShapeDtypeStruct((T, 128), BF16),
                   jax.ShapeDtypeStruct((T, 128), BF16), jax.ShapeDtypeStruct((1, 128), F32)],
        scratch_shapes=[pltpu.VMEM((BLK, 128), F32), pltpu.VMEM((BLK, 128), F32)],
        compiler_params=_cp("arbitrary"),
    )(sinks, qkv, qkv, qkv, qkv, qkv, lse, dmix, cos, sin_s, cos, sin_s)


def _expand_mat():
    return (_iota((128, SW), 1) // HD == _iota((128, SW), 0)).astype(F32)


def _expand_mat_t():
    return (_iota((SW, 128), 0) // HD == _iota((SW, 128), 1)).astype(F32)


def _conv_shifts(u, up):
    row = _iota(u.shape, 0)
    out = [u]
    for j in range(1, CONVK):
        out.append(jnp.where(row < j, pltpu.roll(up, j, 0), pltpu.roll(u, j, 0)))
    return out


def _ssd_parts(u, up, cw_ref, cb_ref, dtr, dtb, alog):
    sh = _conv_shifts(u, up)
    co = cb_ref[...] + cw_ref[3:4, :] * sh[0]
    for j in range(1, CONVK):
        co = co + cw_ref[3 - j:4 - j, :] * sh[j]
    sg = _sigmoid(co)
    xc = co * sg
    xx = dtr + dtb
    dt = jnp.maximum(xx, 0.0) + jnp.log(1.0 + jnp.exp(-jnp.abs(xx)))
    a_neg = -jnp.exp(alog)
    tril = _iota((BLK, BLK), 1) <= _iota((BLK, BLK), 0)
    cs = _dot_hi(tril.astype(F32), dt * a_neg)
    e_mat = _expand_mat()
    csx = _dot_hi(cs, e_mat)
    last = csx[BLK - 1:BLK, :]
    return dict(sh=sh, co=co, sg=sg, xc=xc, xx=xx, dt=dt, a_neg=a_neg, tril=tril, cs=cs, cs_t=cs.T,
                ecsx=jnp.exp(csx), dtex=jnp.exp(last - csx), cdx=jnp.exp(last), dtx=_dot_hi(dt, e_mat))


def _decay(parts, h):
    seg = parts["cs"][:, h:h + 1] - parts["cs_t"][h:h + 1, :]
    return jnp.exp(jnp.where(parts["tril"], seg, NEG))


def _group_cols(a, g):
    return a[:, 256 * g:256 * (g + 1)]


def _ssd_fwd(xbc, z, dtr, conv_w, conv_b, dtb, alog, dskx, ssm_w):
    T = xbc.shape[0]
    nc = T // BLK

    def body(u_ref, up_ref, z_ref, dtr_ref, cw_ref, cb_ref, dtb_ref, al_ref, dk_ref, sw_ref,
             yn_ref, yp_ref, st_ref, s_scr):
        n = pl.program_id(0)

        @pl.when(n == 0)
        def _():
            s_scr[...] = jnp.zeros_like(s_scr)

        u = u_ref[...]
        up = jnp.where(n > 0, up_ref[...], 0.0)
        pt = _ssd_parts(u, up, cw_ref, cb_ref, dtr_ref[...], dtb_ref[...], al_ref[...])
        xc = pt["xc"]
        xs = xc[:, :SW]
        bm = [xc[:, 512:640].astype(BF16), xc[:, 640:768].astype(BF16)]
        cm = [xc[:, 768:896].astype(BF16), xc[:, 896:1024].astype(BF16)]
        s_in = s_scr[...]
        st_ref[0] = s_in
        xdt = xs * pt["dtx"]
        xde = (xdt * pt["dtex"]).astype(BF16)
        lane = _iota((BLK, 128), 1)
        lo = lane < 64
        ys, s_new = [], []
        for g in range(2):
            cb = _dot_nt(cm[g], bm[g])
            yoff = _dot(cm[g], _group_cols(s_in, g).astype(BF16))
            s_new.append(_dot_tn(bm[g], _group_cols(xde, g)))
            for jj in range(2):
                j = 2 * g + jj
                chunk = xdt[:, 128 * j:128 * (j + 1)]
                g_ev = (cb * _decay(pt, 2 * j)).astype(BF16)
                g_od = (cb * _decay(pt, 2 * j + 1)).astype(BF16)
                yd = _dot(g_ev, jnp.where(lo, chunk, 0.0).astype(BF16)) + _dot(g_od, jnp.where(lo, 0.0, chunk).astype(BF16))
                ys.append(yd + yoff[:, 128 * jj:128 * (jj + 1)] * pt["ecsx"][:, 128 * j:128 * (j + 1)])
        y = jnp.concatenate(ys, axis=1) + xs * dk_ref[...]
        s_scr[...] = s_in * pt["cdx"] + jnp.concatenate(s_new, axis=1)
        yp_ref[...] = y
        zv = z_ref[...]
        yz = y * (zv * _sigmoid(zv))
        outs = []
        for g in range(2):
            yg = _group_cols(yz, g)
            outs.append(yg * lax.rsqrt(jnp.mean(yg * yg, axis=-1, keepdims=True) + EPS))
        yn_ref[...] = (jnp.concatenate(outs, axis=1) * sw_ref[...]).astype(BF16)

    prev = lambda n: jnp.maximum(n - 1, 0)
    full = lambda a: pl.BlockSpec(a.shape, lambda n: (0,) * a.ndim)
    return pl.pallas_call(
        body, name="ssd_fwd", grid=(nc,),
        in_specs=[pl.BlockSpec((BLK, CONVC), lambda n: (n, 0)), pl.BlockSpec((BLK, CONVC), lambda n: (prev(n), 0)),
                  pl.BlockSpec((BLK, SW), lambda n: (n, 0)), pl.BlockSpec((BLK, 128), lambda n: (n, 0)),
                  full(conv_w), full(conv_b), full(dtb), full(alog), full(dskx), full(ssm_w)],
        out_specs=[pl.BlockSpec((BLK, SW), lambda n: (n, 0)), pl.BlockSpec((BLK, SW), lambda n: (n, 0)),
                   pl.BlockSpec((1, NST, SW), lambda n: (n, 0, 0))],
        out_shape=[jax.ShapeDtypeStruct((T, SW), BF16), jax.ShapeDtypeStruct((T, SW), F32),
                   jax.ShapeDtypeStruct((nc, NST, SW), F32)],
        scratch_shapes=[pltpu.VMEM((NST, SW), F32)],
        compiler_params=_cp("arbitrary"),
    )(xbc, xbc, z, dtr, conv_w, conv_b, dtb, alog, dskx, ssm_w)


def _ssd_bwd(xbc, z, dtr, ypre, states, dmix, conv_w, conv_b, dtb, alog, dskx, ssm_w):
    T = xbc.shape[0]
    nc = T // BLK

    def body(u_ref, up_ref, z_ref, dtr_ref, yp_ref, st_ref, dyn_ref, cw_ref, cb_ref, dtb_ref, al_ref, dk_ref, sw_ref,
             out_ref, dcw_ref, dcb_ref, dsw_ref, dsk_ref, ddtb_ref, dav_ref, ds_scr, dco_scr, dskx_scr):
        i = pl.program_id(0)
        n = nc - 1 - i

        @pl.when(i == 0)
        def _():
            for r in (dcw_ref, dcb_ref, dsw_ref, dsk_ref, ddtb_ref, dav_ref, ds_scr, dco_scr, dskx_scr):
                r[...] = jnp.zeros_like(r)

        u = u_ref[...]
        up = jnp.where(n > 0, up_ref[...], 0.0)
        pt = _ssd_parts(u, up, cw_ref, cb_ref, dtr_ref[...], dtb_ref[...], al_ref[...])
        xc, dtx, ecsx, dtex, cdx = pt["xc"], pt["dtx"], pt["ecsx"], pt["dtex"], pt["cdx"]
        xs = xc[:, :SW]
        bm = [xc[:, 512:640].astype(BF16), xc[:, 640:768].astype(BF16)]
        cm = [xc[:, 768:896].astype(BF16), xc[:, 896:1024].astype(BF16)]
        s_in = st_ref[0]
        ds_out = ds_scr[...]
        e_t = _expand_mat_t()

        zv = z_ref[...]
        sz = _sigmoid(zv)
        silu_z = zv * sz
        ypre = yp_ref[...]
        yz = ypre * silu_z
        dyn = dyn_ref[...]
        sw = sw_ref[...]
        dyz, yns = [], []
        for g in range(2):
            yg = _group_cols(yz, g)
            r = lax.rsqrt(jnp.mean(yg * yg, axis=-1, keepdims=True) + EPS)
            yn = yg * r
            dg = _group_cols(dyn, g) * _group_cols(sw, g)
            dyz.append(r * (dg - yn * jnp.mean(dg * yn, axis=-1, keepdims=True)))
            yns.append(yn)
        dyz = jnp.concatenate(dyz, axis=1)
        dsw_ref[...] += jnp.sum(dyn * jnp.concatenate(yns, axis=1), axis=0, keepdims=True)
        dy = dyz * silu_z
        dz = dyz * ypre * (sz * (1.0 + zv * (1.0 - sz)))

        xdt = xs * dtx
        xdt_b = xdt.astype(BF16)
        edy = (ecsx * dy).astype(BF16)
        xde = (xdt * dtex).astype(BF16)
        lane = _iota((BLK, 128), 1)
        lo = lane < 64
        row8 = _iota((8, 128), 0)
        dcs = jnp.zeros((BLK, 128), F32)
        col_rows = jnp.zeros((8, 128), F32)
        dxdt, bds, yoff, dbs, dcs_g, ds_new = [], [], [], [], [], []
        for g in range(2):
            s_g = _group_cols(s_in, g).astype(BF16)
            dso_g = _group_cols(ds_out, g).astype(BF16)
            cb = _dot_nt(cm[g], bm[g])
            bds.append(_dot(bm[g], dso_g))
            yoff.append(_dot(cm[g], s_g))
            dcb_g = jnp.zeros((BLK, BLK), F32)
            for jj in range(2):
                j = 2 * g + jj
                dy_c = dy[:, 128 * j:128 * (j + 1)]
                xdt_c = xdt_b[:, 128 * j:128 * (j + 1)]
                acc = jnp.zeros((BLK, 128), F32)
                for par in range(2):
                    h = 2 * j + par
                    lm = _decay(pt, h)
                    gm = cb * lm
                    dy_m = (jnp.where(lo, dy_c, 0.0) if par == 0 else jnp.where(lo, 0.0, dy_c)).astype(BF16)
                    dg_h = _dot_nt(dy_m, xdt_c)
                    w_h = dg_h * gm
                    dcs = dcs + jnp.where(lane == h, jnp.sum(w_h, axis=1, keepdims=True), 0.0)
                    col_rows = col_rows + jnp.where(row8 == h, jnp.sum(w_h, axis=0, keepdims=True), 0.0)
                    dcb_g = dcb_g + dg_h * lm
                    acc = acc + _dot_tn(gm.astype(BF16), dy_m)
                dxdt.append(acc)
            dcb_b = dcb_g.astype(BF16)
            dcs_g.append(_dot(dcb_b, bm[g]) + _dot_nt(_group_cols(edy, g), s_g))
            dbs.append(_dot_tn(dcb_b, cm[g]) + _dot_nt(_group_cols(xde, g), dso_g))
            ds_new.append(_dot_tn(cm[g], _group_cols(edy, g)))
        bds = jnp.concatenate(bds, axis=1)
        yoff = jnp.concatenate(yoff, axis=1) * ecsx
        dxdt = jnp.concatenate(dxdt, axis=1) + dtex * bds
        ds_scr[...] = cdx * ds_out + jnp.concatenate(ds_new, axis=1)

        t_m = _dot_hi(dtex * xdt * bds, e_t)
        colsum_t = jnp.concatenate([col_rows, jnp.zeros((BLK - 8, 128), F32)], axis=0).T
        cd = jnp.exp(pt["cs"][BLK - 1:BLK, :])
        sds = jnp.sum(s_in * ds_out, axis=0, keepdims=True)
        last_row = jnp.sum(t_m, axis=0, keepdims=True) + cd * _dot_hi(jnp.broadcast_to(sds, (8, SW)), e_t)[0:1]
        dcs = dcs - colsum_t + _dot_hi(dy * yoff, e_t) - t_m
        dcs = dcs + jnp.where(_iota((BLK, 128), 0) == BLK - 1, last_row, 0.0)
        triu = (_iota((BLK, BLK), 1) >= _iota((BLK, BLK), 0)).astype(F32)
        da = _dot_hi(triu, dcs)
        dt = pt["dt"]
        ddt = da * pt["a_neg"] + _dot_hi(dxdt * xs, e_t)
        dav_ref[...] += jnp.sum(da * dt, axis=0, keepdims=True)
        ddtr = ddt * _sigmoid(pt["xx"])
        ddtb_ref[...] += jnp.sum(ddtr, axis=0, keepdims=True)
        dxs = dxdt * dtx + dy * dk_ref[...]
        dskx_scr[...] += jnp.sum(dy * xs, axis=0, keepdims=True)
        dxc = jnp.concatenate([dxs, dbs[0], dbs[1], dcs_g[0], dcs_g[1]], axis=1)
        co, sg = pt["co"], pt["sg"]
        dco = dxc * (sg * (1.0 + co * (1.0 - sg)))

        dcb_ref[...] += jnp.sum(dco, axis=0, keepdims=True)
        sh = pt["sh"]
        for j in range(CONVK):
            dcw_ref[3 - j:4 - j, :] += jnp.sum(dco * sh[j], axis=0, keepdims=True)
        dnext = dco_scr[...]
        rowc = _iota(dco.shape, 0)
        du = cw_ref[3:4, :] * dco
        for j in range(1, CONVK):
            up_j = jnp.where(rowc >= BLK - j, pltpu.roll(dnext, BLK - j, 0), pltpu.roll(dco, BLK - j, 0))
            du = du + cw_ref[3 - j:4 - j, :] * up_j
        dco_scr[...] = dco
        out_ref[:, 0:512] = dz.astype(BF16)
        out_ref[:, 512:1536] = du.astype(BF16)
        out_ref[:, 1536:1664] = ddtr.astype(BF16)

        @pl.when(i == nc - 1)
        def _():
            dsk_ref[...] = _dot_hi(jnp.broadcast_to(dskx_scr[...], (8, SW)), e_t)[0:1]

    rev = lambda i: nc - 1 - i
    prev = lambda i: jnp.maximum(nc - 2 - i, 0)
    full = lambda a: pl.BlockSpec(a.shape, lambda i: (0,) * a.ndim)
    acc = lambda r, c: pl.BlockSpec((r, c), lambda i: (0, 0))
    return pl.pallas_call(
        body, name="ssd_bwd", grid=(nc,),
        in_specs=[pl.BlockSpec((BLK, CONVC), lambda i: (rev(i), 0)), pl.BlockSpec((BLK, CONVC), lambda i: (prev(i), 0)),
                  pl.BlockSpec((BLK, SW), lambda i: (rev(i), 0)), pl.BlockSpec((BLK, 128), lambda i: (rev(i), 0)),
                  pl.BlockSpec((BLK, SW), lambda i: (rev(i), 0)), pl.BlockSpec((1, NST, SW), lambda i: (rev(i), 0, 0)),
                  pl.BlockSpec((BLK, SW), lambda i: (rev(i), 1)),
                  full(conv_w), full(conv_b), full(dtb), full(alog), full(dskx), full(ssm_w)],
        out_specs=[pl.BlockSpec((BLK, 1664), lambda i: (rev(i), 0)),
                   acc(CONVK, CONVC), acc(1, CONVC), acc(1, SW), acc(1, 128), acc(1, 128), acc(1, 128)],
        out_shape=[jax.ShapeDtypeStruct((T, 1664), BF16),
                   jax.ShapeDtypeStruct((CONVK, CONVC), F32), jax.ShapeDtypeStruct((1, CONVC), F32),
                   jax.ShapeDtypeStruct((1, SW), F32), jax.ShapeDtypeStruct((1, 128), F32),
                   jax.ShapeDtypeStruct((1, 128), F32), jax.ShapeDtypeStruct((1, 128), F32)],
        scratch_shapes=[pltpu.VMEM((NST, SW), F32), pltpu.VMEM((BLK, CONVC), F32), pltpu.VMEM((1, SW), F32)],
        compiler_params=_cp("arbitrary"),
    )(xbc, xbc, z, dtr, ypre, states, dmix, conv_w, conv_b, dtb, alog, dskx, ssm_w)


def _mix_ffn(x, attn, ynorm, tgt, mod6, norm2_w, final_w, w_out, w_gu, w_dn, tm):
    T = x.shape[0]
    nt = T // tm

    def body(x_ref, a_ref, y_ref, t_ref, mod_ref, n2_ref, fw_ref, wo_hbm, wgu_hbm, wdn_hbm,
             sq_ref, dmix_ref, dx1_ref, h2_ref, act_ref, df_ref, dgu_ref, do_ref, sm_ref,
             wo, wgu, wdn, sems):
        i = pl.program_id(0)

        @pl.when(i == 0)
        def _():
            cps = [pltpu.make_async_copy(s, d, sems.at[k]) for k, (s, d) in
                   enumerate(((wo_hbm, wo), (wgu_hbm, wgu), (wdn_hbm, wdn)))]
            for c in cps:
                c.start()
            for c in cps:
                c.wait()
            sq_ref[...] = jnp.zeros_like(sq_ref)
            sm_ref[...] = jnp.zeros_like(sm_ref)

        gate1, shift2, scale2, gate2 = mod_ref[2:3, :], mod_ref[3:4, :], mod_ref[4:5, :], mod_ref[5:6, :]
        n2w, fw = n2_ref[...], fw_ref[...]
        o = _dot(a_ref[...], wo[0:AW, :]) + _dot(y_ref[...], wo[AW:D, :])
        x1 = x_ref[...] + gate1 * o
        r2 = lax.rsqrt(jnp.mean(x1 * x1, axis=-1, keepdims=True) + EPS)
        xh2 = x1 * r2
        n2 = xh2 * n2w
        h2b = (n2 * (1.0 + scale2) + shift2).astype(BF16)
        h2_ref[...] = h2b
        f = jnp.zeros((tm, D), F32)
        saved = []
        for p in range(2):
            gp = _dot(h2b, wgu[p])
            upj = _dot(h2b, wgu[p + 2])
            sg = _sigmoid(gp)
            sl = gp * sg
            actb = (sl * upj).astype(BF16)
            act_ref[p] = actb
            f = f + _dot(actb, wdn[GU_SH * p:GU_SH * (p + 1), :])
            saved.append((gp, upj, sg, sl))
        x2 = x1 + gate2 * f
        r3 = lax.rsqrt(jnp.mean(x2 * x2, axis=-1, keepdims=True) + EPS)
        xh3 = x2 * r3
        err = xh3 * fw - t_ref[...]
        sq_ref[...] += jnp.sum(err * err, axis=0, keepdims=True)
        dy = err * (1.0 / D)
        dfw = jnp.sum(dy * xh3, axis=0, keepdims=True)
        dxh3 = dy * fw
        dx2 = r3 * (dxh3 - xh3 * jnp.mean(dxh3 * xh3, axis=-1, keepdims=True))
        dgate2 = jnp.sum(dx2 * f, axis=0, keepdims=True)
        dfb = (dx2 * gate2).astype(BF16)
        df_ref[...] = dfb
        dh2 = jnp.zeros((tm, D), F32)
        for p in range(2):
            gp, upj, sg, sl = saved[p]
            dact = _dot_nt(dfb, wdn[GU_SH * p:GU_SH * (p + 1), :])
            dg = (dact * upj * (sg * (1.0 + gp * (1.0 - sg)))).astype(BF16)
            du = (dact * sl).astype(BF16)
            dgu_ref[p] = dg
            dgu_ref[p + 2] = du
            dh2 = dh2 + _dot_nt(dg, wgu[p]) + _dot_nt(du, wgu[p + 2])
        dshift2 = jnp.sum(dh2, axis=0, keepdims=True)
        dscale2 = jnp.sum(dh2 * n2, axis=0, keepdims=True)
        dn2 = dh2 * (1.0 + scale2)
        dn2w = jnp.sum(dn2 * xh2, axis=0, keepdims=True)
        dxh2 = dn2 * n2w
        dx1 = dx2 + r2 * (dxh2 - xh2 * jnp.mean(dxh2 * xh2, axis=-1, keepdims=True))
        dx1_ref[...] = dx1
        dgate1 = jnp.sum(dx1 * o, axis=0, keepdims=True)
        dob = (dx1 * gate1).astype(BF16)
        do_ref[...] = dob
        dmix_ref[...] = _dot_nt(dob, wo[...])
        sm_ref[...] += jnp.concatenate(
            [dfw, dn2w, dshift2, dscale2, dgate2, dgate1, jnp.zeros((2, D), F32)], axis=0)

    row = lambda w: pl.BlockSpec((tm, w), lambda i: (i, 0))
    full = lambda a: pl.BlockSpec(a.shape, lambda i: (0,) * a.ndim)
    anyspec = pl.BlockSpec(memory_space=pl.ANY)
    return pl.pallas_call(
        body, name="mix_ffn", grid=(nt,),
        in_specs=[row(D), row(AW), row(SW), row(D), full(mod6), full(norm2_w), full(final_w), anyspec, anyspec, anyspec],
        out_specs=[pl.BlockSpec((1, D), lambda i: (0, 0)), row(D), row(D), row(D),
                   pl.BlockSpec((2, tm, GU_SH), lambda i: (0, i, 0)), row(D),
                   pl.BlockSpec((4, tm, GU_SH), lambda i: (0, i, 0)), row(D),
                   pl.BlockSpec((8, D), lambda i: (0, 0))],
        out_shape=[jax.ShapeDtypeStruct((1, D), F32), jax.ShapeDtypeStruct((T, D), F32), jax.ShapeDtypeStruct((T, D), F32),
                   jax.ShapeDtypeStruct((T, D), BF16), jax.ShapeDtypeStruct((2, T, GU_SH), BF16),
                   jax.ShapeDtypeStruct((T, D), BF16), jax.ShapeDtypeStruct((4, T, GU_SH), BF16),
                   jax.ShapeDtypeStruct((T, D), BF16), jax.ShapeDtypeStruct((8, D), F32)],
        scratch_shapes=[pltpu.VMEM((D, D), BF16), pltpu.VMEM((4, D, GU_SH), BF16), pltpu.VMEM((DFF, D), BF16),
                        pltpu.SemaphoreType.DMA((3,))],
        compiler_params=_cp("arbitrary"),
    )(x, attn, ynorm, tgt, mod6, norm2_w, final_w, w_out, w_gu, w_dn)


def _in_proj_bwd(x, dx1, dq, dk, dv, dzxd, mod6, norm1_w, w_pad, tm):
    T = x.shape[0]

    def body(x_ref, dx1_ref, dq_ref, dk_ref, dv_ref, dz_ref, mod_ref, nw_ref, w_hbm, gx_ref, sm_ref, w_vmem, sem):
        _load_resident(w_hbm, w_vmem, sem)

        @pl.when(pl.program_id(0) == 0)
        def _():
            sm_ref[...] = jnp.zeros_like(sm_ref)

        dh = (_dot_nt(dq_ref[...], w_vmem[:, 0:512]) + _dot_nt(dk_ref[...], w_vmem[:, 512:640])
              + _dot_nt(dv_ref[...], w_vmem[:, 640:768]) + _dot_nt(dz_ref[...], w_vmem[:, 768:IN_PAD]))
        xv = x_ref[...]
        nw = nw_ref[...]
        scale1 = mod_ref[1:2, :]
        r = lax.rsqrt(jnp.mean(xv * xv, axis=-1, keepdims=True) + EPS)
        xh = xv * r
        n1 = xh * nw
        dshift = jnp.sum(dh, axis=0, keepdims=True)
        dscale = jnp.sum(dh * n1, axis=0, keepdims=True)
        dn = dh * (1.0 + scale1)
        dnw = jnp.sum(dn * xh, axis=0, keepdims=True)
        dxh = dn * nw
        gx_ref[...] = dx1_ref[...] + r * (dxh - xh * jnp.mean(dxh * xh, axis=-1, keepdims=True))
        sm_ref[...] += jnp.concatenate([dnw, dshift, dscale, jnp.zeros((5, D), F32)], axis=0)

    row = lambda w: pl.BlockSpec((tm, w), lambda i: (i, 0))
    full = lambda a: pl.BlockSpec(a.shape, lambda i: (0,) * a.ndim)
    return pl.pallas_call(
        body, name="in_proj_bwd", grid=(T // tm,),
        in_specs=[row(D), row(D), row(512), row(128), row(128), row(1664), full(mod6), full(norm1_w),
                  pl.BlockSpec(memory_space=pl.ANY)],
        out_specs=[row(D), pl.BlockSpec((8, D), lambda i: (0, 0))],
        out_shape=[jax.ShapeDtypeStruct((T, D), F32), jax.ShapeDtypeStruct((8, D), F32)],
        scratch_shapes=[pltpu.VMEM((D, IN_PAD), BF16), pltpu.SemaphoreType.DMA],
        compiler_params=_cp("arbitrary"),
    )(x, dx1, dq, dk, dv, dzxd, mod6, norm1_w, w_pad)


def _tn_matmul(a3, b3, tt, name):
    ja, T, K = a3.shape
    jb, _, N = b3.shape
    J = max(ja, jb)

    def body(a_ref, b_ref, o_ref):
        t = pl.program_id(1)
        prod = _dot_tn(a_ref[0], b_ref[0])

        @pl.when(t == 0)
        def _():
            o_ref[0] = prod

        @pl.when(t > 0)
        def _():
            o_ref[0] += prod

    return pl.pallas_call(
        body, name=name, grid=(J, T // tt),
        in_specs=[pl.BlockSpec((1, tt, K), lambda j, t: (j if ja > 1 else 0, t, 0)),
                  pl.BlockSpec((1, tt, N), lambda j, t: (j if jb > 1 else 0, t, 0))],
        out_specs=pl.BlockSpec((1, K, N), lambda j, t: (j, 0, 0)),
        out_shape=jax.ShapeDtypeStruct((J, K, N), F32),
        compiler_params=_cp("parallel", "arbitrary"),
    )(a3, b3)


def _adam_math(w, g, m, v):
    m = B1 * m + (1.0 - B1) * g
    v = B2 * v + (1.0 - B2) * (g * g)
    m_hat = m / (1.0 - B1 ** STEP)
    v_hat = v / (1.0 - B2 ** STEP)
    delta = -LR * (m_hat / (jnp.sqrt(v_hat) + AEPS) + WD * w)
    return delta, m, v


def _adam_2d(w, g, m, v, rb, name):
    R, C = w.shape

    def body(w_ref, g_ref, m_ref, v_ref, d_ref, mo_ref, vo_ref):
        d, mn, vn = _adam_math(w_ref[...], g_ref[...], m_ref[...], v_ref[...])
        d_ref[...] = d
        mo_ref[...] = mn
        vo_ref[...] = vn

    spec = pl.BlockSpec((rb, C), lambda i: (i, 0))
    return pl.pallas_call(
        body, name=name, grid=(R // rb,), in_specs=[spec] * 4, out_specs=[spec] * 3,
        out_shape=[jax.ShapeDtypeStruct((R, C), F32)] * 3, compiler_params=_cp("parallel"),
    )(w, g, m, v)


def _adam_w_ada(sc_all, dmod_s, w, m, v, rb):
    R, C = w.shape

    def body(sc_ref, dm_ref, w_ref, m_ref, v_ref, g_ref, d_ref, mo_ref, vo_ref):
        g = lax.dot_general(sc_ref[...], dm_ref[...], (((0,), (0,)), ((), ())), precision=HI, preferred_element_type=F32)
        d, mn, vn = _adam_math(w_ref[...], g, m_ref[...], v_ref[...])
        g_ref[...] = g
        d_ref[...] = d
        mo_ref[...] = mn
        vo_ref[...] = vn

    spec = pl.BlockSpec((rb, C), lambda i: (i, 0))
    return pl.pallas_call(
        body, name="adam_w_ada", grid=(R // rb,),
        in_specs=[pl.BlockSpec((8, rb), lambda i: (0, i)), pl.BlockSpec((8, C), lambda i: (0, 0)), spec, spec, spec],
        out_specs=[spec] * 4, out_shape=[jax.ShapeDtypeStruct((R, C), F32)] * 4, compiler_params=_cp("parallel"),
    )(sc_all, dmod_s, w, m, v)


def _adam_small(grads, ws, ms, vs):
    k = len(ws)

    def body(*refs):
        g, w, m, v = refs[0:k], refs[k:2 * k], refs[2 * k:3 * k], refs[3 * k:4 * k]
        d_o, m_o, v_o = refs[4 * k:5 * k], refs[5 * k:6 * k], refs[6 * k:7 * k]
        for i in range(k):
            d, mn, vn = _adam_math(w[i][...], g[i][...], m[i][...], v[i][...])
            d_o[i][...] = d
            m_o[i][...] = mn
            v_o[i][...] = vn

    shapes = [jax.ShapeDtypeStruct(w.shape, F32) for w in ws]
    vm = pl.BlockSpec(memory_space=pltpu.VMEM)
    outs = pl.pallas_call(
        body, name="adam_small", in_specs=[vm] * (4 * k), out_specs=[vm] * (3 * k), out_shape=shapes * 3,
    )(*grads, *ws, *ms, *vs)
    return outs[0:k], outs[k:2 * k], outs[2 * k:3 * k]


def _pos():
    return lax.axis_index("x"), lax.axis_index("y"), lax.axis_index("c")


def _flip(v, bit):
    return 1 - v if bit else v


def _peer(k):
    x, y, c = _pos()
    return (_flip(x, (k >> 2) & 1), _flip(y, (k >> 1) & 1), _flip(c, k & 1))


def _logical(p):
    return 4 * p[0] + 2 * p[1] + p[2]


def _gather8(src_ref, dst_ref, send_sems, recv_sems):
    me = _logical(_pos())
    dst_ref[pl.ds(me, 1)] = src_ref[...][None]
    copies = []
    for k in range(1, 8):
        cp = pltpu.make_async_remote_copy(src_ref, dst_ref.at[me], send_sems.at[k - 1], recv_sems.at[k - 1],
                                          device_id=_peer(k), device_id_type=MESH)
        cp.start()
        copies.append(cp)
    for k in range(1, 8):
        pltpu.make_async_remote_copy(src_ref, dst_ref.at[_logical(_peer(k))], send_sems.at[k - 1], recv_sems.at[k - 1],
                                     device_id=_peer(k), device_id_type=MESH).wait_recv()
    for cp in copies:
        cp.wait_send()


def _rows_select(ref3, width):
    row = _iota((8, width), 0)
    out = jnp.zeros((8, width), F32)
    for i in range(8):
        out = jnp.where(row == i, ref3[i][:, 0:width], out)
    return out


def _mod_exchange(payload, w_ada_s, b_ada4):
    n_sh = w_ada_s.shape[1]

    def body(pay_ref, w_ref, b_ref, gat_ref, mod_ref, p3, sa, ra, sb, rb):
        x, y, c = _pos()
        me = _logical((x, y, c))
        my_s = 2 * x + y
        _gather8(pay_ref, gat_ref, sa, ra)
        cmat = _rows_select(gat_ref, D)
        prod = _dot_hi(cmat * _sigmoid(cmat), w_ref[...])
        for b in range(8):
            p3[b] = prod[b:b + 1, :]
        mod_ref[pl.ds(my_s, 1)] = p3[pl.ds(me, 1)] + b_ref[pl.ds(my_s, 1)]
        ks = (2, 4, 6)
        copies = []
        for i, k in enumerate(ks):
            pr = _peer(k)
            cp = pltpu.make_async_remote_copy(p3.at[_logical(pr)], mod_ref.at[my_s], sb.at[i], rb.at[i],
                                              device_id=pr, device_id_type=MESH)
            cp.start()
            copies.append(cp)
        for i, k in enumerate(ks):
            pr = _peer(k)
            s_src = 2 * pr[0] + pr[1]
            pltpu.make_async_remote_copy(p3.at[0], mod_ref.at[s_src], sb.at[i], rb.at[i],
                                         device_id=pr, device_id_type=MESH).wait_recv()
            mod_ref[pl.ds(s_src, 1)] = mod_ref[pl.ds(s_src, 1)] + b_ref[pl.ds(s_src, 1)]
        for cp in copies:
            cp.wait_send()

    vm = pl.BlockSpec(memory_space=pltpu.VMEM)
    return pl.pallas_call(
        body, name="mod_exchange", in_specs=[vm, vm, vm], out_specs=[vm, vm],
        out_shape=[jax.ShapeDtypeStruct((8, 1, payload.shape[1]), F32), jax.ShapeDtypeStruct((4, 1, n_sh), F32)],
        scratch_shapes=[pltpu.VMEM((8, 1, n_sh), F32), pltpu.SemaphoreType.DMA((7,)), pltpu.SemaphoreType.DMA((7,)),
                        pltpu.SemaphoreType.DMA((3,)), pltpu.SemaphoreType.DMA((3,))],
        compiler_params=pltpu.CompilerParams(vmem_limit_bytes=VMEM_LIMIT),
    )(payload, w_ada_s, b_ada4)


def _chips():
    x, y, _ = _pos()
    out = []
    for k in (1, 2, 3):
        px, py = _flip(x, (k >> 1) & 1), _flip(y, k & 1)
        out.append((px, py, 2 * px + py))
    return out


def _half_rows(ref, which):
    half = ref.shape[-2] // 2
    return pl.ds(pl.multiple_of(which * half, 8), half)


def _weight_gather(shards):
    nw = len(shards)

    def body(*refs):
        ins, outs, token = refs[:nw], refs[nw:2 * nw], refs[2 * nw]
        send, recv, fsend, frecv, lsem = refs[2 * nw + 1:]
        token[...] = jnp.zeros_like(token)
        x, y, c = _pos()
        my_s = 2 * x + y
        sib = (x, y, 1 - c)
        chips = _chips()
        local = [pltpu.make_async_copy(ins[w], outs[w].at[my_s], lsem.at[w]) for w in range(nw)]
        for cp in local:
            cp.start()
        sends = []
        for w in range(nw):
            mine = _half_rows(ins[w], c)
            for k, (px, py, _) in enumerate(chips):
                cp = pltpu.make_async_remote_copy(ins[w].at[mine], outs[w].at[my_s, mine], send.at[3 * w + k],
                                                  recv.at[3 * w + k], device_id=(px, py, c), device_id_type=MESH)
                cp.start()
                sends.append(cp)
        for w in range(nw):
            mine = _half_rows(ins[w], c)
            for k, (px, py, ps) in enumerate(chips):
                got = outs[w].at[ps, mine]
                pltpu.make_async_remote_copy(got, got, send.at[3 * w + k], recv.at[3 * w + k],
                                             device_id=(px, py, c), device_id_type=MESH).wait_recv()
                cp = pltpu.make_async_remote_copy(got, got, fsend.at[3 * w + k], frecv.at[3 * w + k],
                                                  device_id=sib, device_id_type=MESH)
                cp.start()
                sends.append(cp)
        for w in range(nw):
            other = _half_rows(ins[w], 1 - c)
            for k, (px, py, ps) in enumerate(chips):
                got = outs[w].at[ps, other]
                pltpu.make_async_remote_copy(got, got, fsend.at[3 * w + k], frecv.at[3 * w + k],
                                             device_id=sib, device_id_type=MESH).wait_recv()
        for cp in sends:
            cp.wait_send()
        for cp in local:
            cp.wait()

    hbm = pl.BlockSpec(memory_space=pltpu.HBM)
    return pl.pallas_call(
        body, name="weight_gather", in_specs=[hbm] * nw,
        out_specs=[hbm] * nw + [pl.BlockSpec(memory_space=pltpu.VMEM)],
        out_shape=[pltpu.HBM((4,) + s.shape, s.dtype) for s in shards] + [jax.ShapeDtypeStruct((8, 128), F32)],
        scratch_shapes=[pltpu.SemaphoreType.DMA((3 * nw,)), pltpu.SemaphoreType.DMA((3 * nw,)),
                        pltpu.SemaphoreType.DMA((3 * nw,)), pltpu.SemaphoreType.DMA((3 * nw,)),
                        pltpu.SemaphoreType.DMA((nw,))],
    )(*shards)


def _small_reduce(vec):
    n = vec.shape[1]

    def body(v_ref, tot_ref, gat_ref, sa, ra):
        _gather8(v_ref, gat_ref, sa, ra)
        tot = gat_ref[0]
        for i in range(1, 8):
            tot = tot + gat_ref[i]
        tot_ref[...] = tot

    vm = pl.BlockSpec(memory_space=pltpu.VMEM)
    return pl.pallas_call(
        body, name="small_reduce", in_specs=[vm], out_specs=[vm, vm],
        out_shape=[jax.ShapeDtypeStruct((1, n), F32), jax.ShapeDtypeStruct((8, 1, n), F32)],
        scratch_shapes=[pltpu.SemaphoreType.DMA((7,)), pltpu.SemaphoreType.DMA((7,))],
    )(vec)


def _sibling_swap(grads):
    nw = len(grads)

    def body(*refs):
        ins, outs = refs[:nw], refs[nw:2 * nw]
        send, recv = refs[2 * nw:]
        x, y, c = _pos()
        sib = (x, y, 1 - c)
        cps = []
        for w in range(nw):
            theirs = _half_rows(ins[w], 1 - c)
            cp = pltpu.make_async_remote_copy(ins[w].at[:, theirs], outs[w], send.at[w], recv.at[w],
                                              device_id=sib, device_id_type=MESH)
            cp.start()
            cps.append(cp)
        for cp in cps:
            cp.wait()

    hbm = pl.BlockSpec(memory_space=pltpu.HBM)
    return pl.pallas_call(
        body, name="grad_sibling_swap", in_specs=[hbm] * nw, out_specs=[hbm] * nw,
        out_shape=[pltpu.HBM((4, g.shape[1] // 2, g.shape[2]), F32) for g in grads],
        scratch_shapes=[pltpu.SemaphoreType.DMA((nw,)), pltpu.SemaphoreType.DMA((nw,))],
    )(*grads)


def _add_half(g, sib, c_arr, rb, name):
    _, R, C = g.shape
    half = R // 2
    nb = half // rb

    def body(c_ref, g_ref, s_ref, o_ref):
        o_ref[...] = g_ref[...] + s_ref[...]

    return pl.pallas_call(
        body, name=name,
        grid_spec=pltpu.PrefetchScalarGridSpec(
            num_scalar_prefetch=1, grid=(4, nb),
            in_specs=[pl.BlockSpec((1, rb, C), lambda s, i, c_ref: (s, c_ref[0] * nb + i, 0)),
                      pl.BlockSpec((1, rb, C), lambda s, i, c_ref: (s, i, 0))],
            out_specs=pl.BlockSpec((1, rb, C), lambda s, i, c_ref: (s, i, 0))),
        out_shape=jax.ShapeDtypeStruct((4, half, C), F32),
        compiler_params=_cp("parallel", "parallel"),
    )(c_arr, g, sib)


def _chip_scatter(parts):
    nw = len(parts)

    def body(*refs):
        ins, outs = refs[:nw], refs[nw:2 * nw]
        send, recv, lsem = refs[2 * nw:]
        x, y, c = _pos()
        my_s = 2 * x + y
        chips = _chips()
        local = [pltpu.make_async_copy(ins[w].at[my_s], outs[w].at[my_s], lsem.at[w]) for w in range(nw)]
        for cp in local:
            cp.start()
        cps = []
        for w in range(nw):
            for k, (px, py, ps) in enumerate(chips):
                cp = pltpu.make_async_remote_copy(ins[w].at[ps], outs[w].at[my_s], send.at[3 * w + k], recv.at[3 * w + k],
                                                  device_id=(px, py, c), device_id_type=MESH)
                cp.start()
                cps.append(cp)
        for w in range(nw):
            for k, (px, py, ps) in enumerate(chips):
                pltpu.make_async_remote_copy(ins[w].at[ps], outs[w].at[ps], send.at[3 * w + k], recv.at[3 * w + k],
                                             device_id=(px, py, c), device_id_type=MESH).wait_recv()
        for cp in cps:
            cp.wait_send()
        for cp in local:
            cp.wait()

    hbm = pl.BlockSpec(memory_space=pltpu.HBM)
    return pl.pallas_call(
        body, name="grad_chip_scatter", in_specs=[hbm] * nw, out_specs=[hbm] * nw,
        out_shape=[pltpu.HBM(p.shape, F32) for p in parts],
        scratch_shapes=[pltpu.SemaphoreType.DMA((3 * nw,)), pltpu.SemaphoreType.DMA((3 * nw,)),
                        pltpu.SemaphoreType.DMA((nw,))],
    )(*parts)


def _sum4(r, rb, name):
    _, H, C = r.shape

    def body(r_ref, o_ref):
        o_ref[...] = ((r_ref[0] + r_ref[1]) + r_ref[2]) + r_ref[3]

    return pl.pallas_call(
        body, name=name, grid=(H // rb,),
        in_specs=[pl.BlockSpec((4, rb, C), lambda i: (0, i, 0))], out_specs=pl.BlockSpec((rb, C), lambda i: (i, 0)),
        out_shape=jax.ShapeDtypeStruct((H, C), F32), compiler_params=_cp("parallel"),
    )(r)


def _sibling_join(halves):
    nw = len(halves)

    def body(*refs):
        ins, outs = refs[:nw], refs[nw:2 * nw]
        send, recv, lsem = refs[2 * nw:]
        x, y, c = _pos()
        sib = (x, y, 1 - c)
        cps, local = [], []
        for w in range(nw):
            mine = _half_rows(outs[w], c)
            lc = pltpu.make_async_copy(ins[w], outs[w].at[mine], lsem.at[w])
            lc.start()
            local.append(lc)
            cp = pltpu.make_async_remote_copy(ins[w], outs[w].at[mine], send.at[w], recv.at[w],
                                              device_id=sib, device_id_type=MESH)
            cp.start()
            cps.append(cp)
        for w in range(nw):
            other = _half_rows(outs[w], 1 - c)
            pltpu.make_async_remote_copy(ins[w], outs[w].at[other], send.at[w], recv.at[w],
                                         device_id=sib, device_id_type=MESH).wait_recv()
        for cp in cps:
            cp.wait_send()
        for lc in local:
            lc.wait()

    hbm = pl.BlockSpec(memory_space=pltpu.HBM)
    return pl.pallas_call(
        body, name="grad_sibling_join", in_specs=[hbm] * nw, out_specs=[hbm] * nw,
        out_shape=[pltpu.HBM((2 * h.shape[0], h.shape[1]), F32) for h in halves],
        scratch_shapes=[pltpu.SemaphoreType.DMA((nw,)), pltpu.SemaphoreType.DMA((nw,)), pltpu.SemaphoreType.DMA((nw,))],
    )(*halves)


HBM_SPEC = pl.BlockSpec(memory_space=pltpu.HBM)
SEM_SPEC = pl.BlockSpec(memory_space=pltpu.SEMAPHORE)
EFFECT = pltpu.SideEffectType.DATAFLOW_SIDE_EFFECTING


def _split_start(name, bufs, n_sem, plan):
    nb = len(bufs)

    def body(*refs):
        ins, send, recv, token = refs[:nb], refs[nb], refs[nb + 1], refs[-1]
        for i, (src, dst, dev, _) in enumerate(plan(ins)):
            pltpu.make_async_remote_copy(src, dst, send.at[i], recv.at[i], device_id=dev, device_id_type=MESH).start()
        token[...] = jnp.zeros_like(token)

    outs = pl.pallas_call(
        body, name=name,
        out_shape=(pltpu.SemaphoreType.DMA((n_sem,)), pltpu.SemaphoreType.DMA((n_sem,)),
                   *[pltpu.HBM(b.shape, b.dtype) for b in bufs], jax.ShapeDtypeStruct((8, 128), F32)),
        in_specs=[HBM_SPEC] * nb,
        out_specs=(SEM_SPEC, SEM_SPEC, *([HBM_SPEC] * nb), pl.BlockSpec(memory_space=pltpu.VMEM)),
        input_output_aliases={i: 2 + i for i in range(nb)},
        compiler_params=pltpu.CompilerParams(has_side_effects=EFFECT),
    )(*[pltpu.with_memory_space_constraint(b, pltpu.HBM) for b in bufs])
    return outs[0], outs[1], list(outs[2:2 + nb]), outs[-1]


def _split_wait(name, send, recv, bufs, after, plan):
    nb = len(bufs)

    def body(*refs):
        ins, send_s, recv_s = refs[:nb], refs[nb], refs[nb + 1]
        for i, (src, dst, dev, mine) in enumerate(plan(ins)):
            pltpu.make_async_remote_copy(src, dst, send_s.at[i], recv_s.at[i], device_id=dev,
                                         device_id_type=MESH).wait_send()
            pltpu.make_async_remote_copy(src, mine, send_s.at[i], recv_s.at[i], device_id=dev,
                                         device_id_type=MESH).wait_recv()

    outs = pl.pallas_call(
        body, name=name, out_shape=[pltpu.HBM(b.shape, b.dtype) for b in bufs],
        in_specs=[HBM_SPEC] * nb + [SEM_SPEC, SEM_SPEC, pl.BlockSpec(memory_space=pl.ANY)],
        out_specs=[HBM_SPEC] * nb, input_output_aliases={i: i for i in range(nb)},
        compiler_params=pltpu.CompilerParams(has_side_effects=EFFECT),
    )(*bufs, send, recv, after)
    return list(outs)


def _plan_gather_ici(nw):
    def plan(refs):
        x, y, c = _pos()
        my_s = 2 * x + y
        out = []
        for w in range(nw):
            mine = _half_rows(refs[w], c)
            for px, py, ps in _chips():
                out.append((refs[w].at[mine], refs[nw + w].at[my_s, mine], (px, py, c), refs[nw + w].at[ps, mine]))
        return out
    return plan


def _plan_gather_fwd(nw):
    def plan(refs):
        x, y, c = _pos()
        out = []
        for w in range(nw):
            mine, other = _half_rows(refs[w], c), _half_rows(refs[w], 1 - c)
            for px, py, ps in _chips():
                got = refs[w].at[ps, mine]
                out.append((got, got, (x, y, 1 - c), refs[w].at[ps, other]))
        return out
    return plan


def _plan_swap(nw):
    def plan(refs):
        x, y, c = _pos()
        return [(refs[w].at[:, _half_rows(refs[w], 1 - c)], refs[nw + w], (x, y, 1 - c), refs[nw + w])
                for w in range(nw)]
    return plan


def _plan_scatter(nw):
    def plan(refs):
        x, y, c = _pos()
        my_s = 2 * x + y
        out = []
        for w in range(nw):
            for px, py, ps in _chips():
                out.append((refs[w].at[ps], refs[nw + w].at[my_s], (px, py, c), refs[nw + w].at[ps]))
        return out
    return plan


def _plan_join(nw):
    def plan(refs):
        x, y, c = _pos()
        out = []
        for w in range(nw):
            land = refs[nw + w]
            out.append((refs[w], land.at[_half_rows(land, c)], (x, y, 1 - c), land.at[_half_rows(land, 1 - c)]))
        return out
    return plan


def _hbm_empty(shape, dtype):
    return pltpu.with_memory_space_constraint(lax.empty(shape, dtype), pltpu.HBM)


def _put_slot(land, own, slot):
    return lax.dynamic_update_slice(land, own[None], (slot,) + (0,) * own.ndim)


def _pad_lanes(a, n):
    return jnp.pad(a, ((0, 0), (0, n - a.shape[1])))


def kernel(x, c, positions, w_ada, b_ada, norm1_w, w_in, conv_w, conv_b, dt_bias, a_log, d_skip, attn_sinks, ssm_norm_w, w_out, norm2_w, w_gate_up, w_down, final_norm_w, loss_target, m_w_ada, m_b_ada, m_norm1_w, m_w_in, m_conv_w, m_conv_b, m_dt_bias, m_a_log, m_d_skip, m_attn_sinks, m_ssm_norm_w, m_w_out, m_norm2_w, m_w_gate_up, m_w_down, m_final_norm_w, v_w_ada, v_b_ada, v_norm1_w, v_w_in, v_conv_w, v_conv_b, v_dt_bias, v_a_log, v_d_skip, v_attn_sinks, v_ssm_norm_w, v_w_out, v_norm2_w, v_w_gate_up, v_w_down, v_final_norm_w):
    T = x.shape[1]
    tm = min(256, T)
    xi, yi, ci = lax.axis_index("x"), lax.axis_index("y"), lax.axis_index("c")
    my_s = 2 * xi + yi
    xs = x[0]
    tgt = loss_target[0]

    payload = jnp.concatenate([c, conv_w[0].reshape(1, CONVK * 256)], axis=1)
    gat, mod4 = _mod_exchange(payload, w_ada[0], b_ada.reshape(4, 1, 1536))
    mod6 = mod4.reshape(6, D)
    c_all = gat[:, 0, 0:D]
    cw_dev = gat[:, 0, D:].reshape(4, 2, CONVK, 256)[:, 0]
    conv_full = cw_dev.transpose(1, 0, 2).reshape(CONVK, CONVC)

    g_in, tok = _weight_gather([w_in[0].astype(BF16)])
    w_pad = _pad_lanes(g_in.transpose(1, 0, 2).reshape(D, IN_PROJ), IN_PAD)
    late = [(w_out[0] + tok[0, 0]).astype(BF16), w_gate_up[0].astype(BF16), w_down[0].astype(BF16)]
    lands = [_hbm_empty((4,) + s.shape, BF16) for s in late]
    s_a, r_a, bufs, tok = _split_start("wgather_ici_start", late + lands, 9, _plan_gather_ici(3))

    inv_freq = (10000.0 ** (-jnp.arange(32, dtype=F32) / 32))
    inv_row = jnp.tile(inv_freq, 4).reshape(1, 128)
    cos, sin_s = _rope_tables(positions.reshape(T, 1), inv_row, tm)
    qkv, z, xbc, dtr, h1b = _in_proj_fwd(xs, cos, sin_s, mod6 + tok[0, 0], norm1_w, w_pad, tm)
    sinks = attn_sinks
    attn, lse = _attn_fwd(qkv, sinks)
    bufs = _split_wait("wgather_ici_wait", s_a, r_a, bufs, attn, _plan_gather_ici(3))
    s_b, r_b, lands, tok = _split_start("wgather_fwd_start", bufs[3:], 9, _plan_gather_fwd(3))
    dtb = _pad_lanes(dt_bias, 128)
    alog = _pad_lanes(a_log, 128)
    dskx = jnp.repeat(d_skip, HD, axis=1)
    ynorm, ypre, states = _ssd_fwd(xbc, z, dtr, conv_full, conv_b, dtb + tok[0, 0], alog, dskx, ssm_norm_w)
    lands = _split_wait("wgather_fwd_wait", s_b, r_b, lands, ynorm, _plan_gather_fwd(3))
    g_out, g_gu, g_dn = [_put_slot(l, s, my_s) for l, s in zip(lands, late)]
    w_out_f = g_out.reshape(D, D)
    w_dn_f = g_dn.reshape(DFF, D)

    fw2 = final_norm_w.reshape(1, D)
    sq, dmix, dx1, h2b, act, dfb, dgu, dob, sm_ffn = _mix_ffn(
        xs, attn, ynorm, tgt, mod6, norm2_w, fw2, w_out_f, g_gu, w_dn_f, tm)
    loss = lax.psum(0.5 / D * jnp.sum(sq), ("x", "y", "c"))

    tt = min(512, T)
    c_arr = ci.reshape(1).astype(jnp.int32)
    gw_dn4 = _tn_matmul(act, dfb[None], tt, "dw_down").reshape(4, DFF // 4, D)
    gw_gu4 = _tn_matmul(h2b[None], dgu, tt, "dw_gate_up")
    gw_out4 = jnp.concatenate(
        [_tn_matmul(attn[None], dob[None], tt, "dw_out_a")[0], _tn_matmul(ynorm[None], dob[None], tt, "dw_out_y")[0]],
        axis=0).reshape(4, D // 4, D)
    big1 = [gw_out4, gw_gu4, gw_dn4]
    rbs1 = [128, 128, 88]
    sib1 = [_hbm_empty((4, g.shape[1] // 2, g.shape[2]), F32) for g in big1]
    s_c, r_c, bufs, tok = _split_start("gswap_start", big1 + sib1, 3, _plan_swap(3))

    dzxd, d_cw, d_cb, d_sw, d_sk, d_dtb, d_av = _ssd_bwd(
        xbc, z, dtr, ypre, states, dmix, conv_full, conv_b, dtb + tok[0, 0], alog, dskx, ssm_norm_w)
    bufs = _split_wait("gswap_wait", s_c, r_c, bufs, dzxd, _plan_swap(3))
    sums1 = [_add_half(g, s, c_arr, rb, "grad_add_%d" % i)
             for i, (g, s, rb) in enumerate(zip(bufs[:3], bufs[3:], rbs1))]
    land1 = [_hbm_empty(p.shape, F32) for p in sums1]
    s_d, r_d, bufs, tok = _split_start("gscatter_start", sums1 + land1, 9, _plan_scatter(3))
    dq, dk, dv, d_sinks = _attn_bwd(qkv, sinks + tok[0:1, 0:8], lse, dmix, cos, sin_s)
    grad_x, sm_in = _in_proj_bwd(xs, dx1, dq, dk, dv, dzxd, mod6, norm1_w, w_pad, tm)
    h1_3 = h1b[None]
    gw_in = jnp.concatenate(
        [_tn_matmul(h1_3, dq[None], tt, "dw_in_q")[0], _tn_matmul(h1_3, dk[None], tt, "dw_in_k")[0],
         _tn_matmul(h1_3, dv[None], tt, "dw_in_v")[0], _tn_matmul(h1_3, dzxd[None], tt, "dw_in_z")[0]], axis=1)
    gw_in4 = gw_in[:, :IN_PROJ].reshape(D, 4, IN_PROJ // 4).transpose(1, 0, 2)
    bufs = _split_wait("gscatter_wait", s_d, r_d, bufs, gw_in4, _plan_scatter(3))
    slots1 = [_put_slot(l, lax.dynamic_index_in_dim(p, my_s, 0, keepdims=False), my_s)
              for p, l in zip(bufs[:3], bufs[3:])]
    halves1 = [_sum4(r, rb, "grad_sum_%d" % i) for i, (r, rb) in enumerate(zip(slots1, rbs1))]
    full1 = [_hbm_empty((2 * h.shape[0], h.shape[1]), F32) for h in halves1]
    s_e, r_e, bufs, tok = _split_start("gjoin_start", halves1 + full1, 3, _plan_join(3))

    (sib0,) = _sibling_swap([gw_in4])
    (slot0,) = _chip_scatter([_add_half(gw_in4, sib0, c_arr, 128, "grad_add_in")])
    half0 = _sum4(slot0, 128, "grad_sum_in")
    bufs = _split_wait("gjoin_wait", s_e, r_e, bufs, half0, _plan_join(3))
    g_out_s, g_gu_s, g_dn_s = [lax.dynamic_update_slice(f, h, (ci * h.shape[0], 0)) for h, f in zip(bufs[:3], bufs[3:])]
    (g_in_s,) = _sibling_join([half0])

    a_neg = -jnp.exp(alog)
    pieces = [sm_in[1:2], sm_in[2:3], sm_ffn[5:6], sm_ffn[2:3], sm_ffn[3:4], sm_ffn[4:5],
              sm_in[0:1], sm_ffn[1:2], sm_ffn[0:1], d_cb, d_cw.reshape(1, CONVK * CONVC),
              _pad_lanes(d_sw, SW), d_dtb, d_av * a_neg, d_sk, d_sinks]
    vec = jnp.concatenate(pieces, axis=1)
    tot, allv = _small_reduce(vec)
    o = 0
    offs = []
    for p in pieces:
        offs.append(o)
        o += p.shape[1]
    seg = lambda i, n: tot[:, offs[i]:offs[i] + n]
    g_b_ada = tot[:, 0:6 * D]
    g_norm1, g_norm2, g_final, g_conv_b = seg(6, D), seg(7, D), seg(8, D), seg(9, D)
    g_conv_w = lax.dynamic_slice_in_dim(seg(10, CONVK * CONVC).reshape(CONVK, CONVC), my_s * 256, 256, axis=1)
    g_ssm_w, g_dtb, g_alog, g_dsk, g_sink = seg(11, SW), seg(12, 8), seg(13, 8), seg(14, 8), seg(15, 8)

    small_names = ["b_ada", "norm1_w", "conv_w", "conv_b", "dt_bias", "a_log", "d_skip", "attn_sinks", "ssm_norm_w",
                   "norm2_w", "final_norm_w"]
    small_g = [g_b_ada, g_norm1, g_conv_w, g_conv_b, g_dtb, g_alog, g_dsk, g_sink, g_ssm_w, g_norm2, g_final]
    as2d = lambda a: a.reshape(-1, a.shape[-1])
    small_w = [as2d(a) for a in (b_ada, norm1_w, conv_w, conv_b, dt_bias, a_log, d_skip, attn_sinks, ssm_norm_w,
                                 norm2_w, final_norm_w)]
    small_m = [as2d(a) for a in (m_b_ada, m_norm1_w, m_conv_w, m_conv_b, m_dt_bias, m_a_log, m_d_skip, m_attn_sinks,
                                 m_ssm_norm_w, m_norm2_w, m_final_norm_w)]
    small_v = [as2d(a) for a in (v_b_ada, v_norm1_w, v_conv_w, v_conv_b, v_dt_bias, v_a_log, v_d_skip, v_attn_sinks,
                                 v_ssm_norm_w, v_norm2_w, v_final_norm_w)]
    sd, smn, svn = _adam_small(small_g, small_w, small_m, small_v)

    sc_all = c_all * jax.nn.sigmoid(c_all)
    dmod_all = allv[:, 0, 0:6 * D]
    dmod_s = lax.dynamic_slice_in_dim(dmod_all, my_s * 1536, 1536, axis=1)
    g_ada, d_ada, m_ada, v_ada = _adam_w_ada(sc_all, dmod_s, w_ada[0], m_w_ada[0], v_w_ada[0], 256)
    d_in, m_in, v_in = _adam_2d(w_in[0], g_in_s, m_w_in[0], v_w_in[0], 256, "adam_w_in")
    d_out, m_out, v_out = _adam_2d(w_out[0], g_out_s, m_w_out[0], v_w_out[0], 256, "adam_w_out")
    d_gu, m_gu, v_gu = _adam_2d(w_gate_up[0], g_gu_s, m_w_gate_up[0], v_w_gate_up[0], 256, "adam_w_gate_up")
    d_dn, m_dn, v_dn = _adam_2d(w_down[0], g_dn_s, m_w_down[0], v_w_down[0], 352, "adam_w_down")

    order = ["w_ada", "b_ada", "norm1_w", "w_in", "conv_w", "conv_b", "dt_bias", "a_log", "d_skip", "attn_sinks",
             "ssm_norm_w", "w_out", "norm2_w", "w_gate_up", "w_down", "final_norm_w"]
    shapes = dict(w_ada=w_ada.shape, b_ada=b_ada.shape, norm1_w=norm1_w.shape, w_in=w_in.shape, conv_w=conv_w.shape,
                  conv_b=conv_b.shape, dt_bias=dt_bias.shape, a_log=a_log.shape, d_skip=d_skip.shape,
                  attn_sinks=attn_sinks.shape, ssm_norm_w=ssm_norm_w.shape, w_out=w_out.shape, norm2_w=norm2_w.shape,
                  w_gate_up=w_gate_up.shape, w_down=w_down.shape, final_norm_w=final_norm_w.shape)
    grads = dict(w_ada=g_ada, w_in=g_in_s, w_out=g_out_s, w_gate_up=g_gu_s, w_down=g_dn_s)
    deltas = dict(w_ada=d_ada, w_in=d_in, w_out=d_out, w_gate_up=d_gu, w_down=d_dn)
    new_m = dict(w_ada=m_ada, w_in=m_in, w_out=m_out, w_gate_up=m_gu, w_down=m_dn)
    new_v = dict(w_ada=v_ada, w_in=v_in, w_out=v_out, w_gate_up=v_gu, w_down=v_dn)
    for i, nme in enumerate(small_names):
        grads[nme], deltas[nme], new_m[nme], new_v[nme] = small_g[i], sd[i], smn[i], svn[i]
    outs = [loss, grad_x[None]]
    for table in (grads, deltas, new_m, new_v):
        outs += [table[nme].reshape(shapes[nme]) for nme in order]
    return tuple(outs)
```

```python
import functools
import math

import jax
import jax.numpy as jnp
from jax import lax
from jax.experimental import pallas as pl
from jax.experimental.pallas import tpu as pltpu

F32 = jnp.float32
BF16 = jnp.bfloat16
HI = lax.Precision.HIGHEST
MESH = pl.DeviceIdType.MESH

D = 1024
HD = 64
NQ = 8
AW = 512
KVW = 128
SW = 512
NST = 128
CONVK = 4
CONVC = 1024
BLK = 128
IN_PROJ = 2312
IN_PAD = 2432
DFF = 2816
GU_SH = 1408
EPS = 1e-6
NEG = -1e30
LR, B1, B2, AEPS, WD, STEP = 0.001, 0.9, 0.999, 1e-08, 0.01, 10
VMEM_LIMIT = 58 * 1024 * 1024


def _cp(*sem):
    return pltpu.CompilerParams(dimension_semantics=sem or None, vmem_limit_bytes=VMEM_LIMIT)


def _dot(a, b):
    return jnp.dot(a, b, preferred_element_type=F32)


def _dot_nt(a, b):
    return lax.dot_general(a, b, (((1,), (1,)), ((), ())), preferred_element_type=F32)


def _dot_tn(a, b):
    return lax.dot_general(a, b, (((0,), (0,)), ((), ())), preferred_element_type=F32)


def _dot_hi(a, b):
    return jnp.dot(a, b, precision=HI, preferred_element_type=F32)


def _sigmoid(x):
    return 1.0 / (1.0 + jnp.exp(-x))


def _iota(shape, dim):
    return lax.broadcasted_iota(jnp.int32, shape, dim)


def _load_resident(hbm_ref, vmem_ref, sem):
    @pl.when(pl.program_id(0) == 0)
    def _():
        cp = pltpu.make_async_copy(hbm_ref, vmem_ref, sem)
        cp.start()
        cp.wait()


def _swap32(t):
    lane = _iota(t.shape, 1)
    return jnp.where((lane & 63) < 32, pltpu.roll(t, 96, 1), pltpu.roll(t, 32, 1))


def _rope_fwd(t, cos, sin_s):
    return t * cos + _swap32(t) * sin_s


def _rope_bwd(t, cos, sin_s):
    return t * cos - _swap32(t) * sin_s


def _rope_tables(pos_col, inv_freq_row, tm):
    T = pos_col.shape[0]

    def body(p_ref, f_ref, cos_ref, sin_ref):
        ang = p_ref[...].astype(F32) * f_ref[...]
        lane = _iota((tm, 128), 1)
        s = jnp.sin(ang)
        cos_ref[...] = jnp.cos(ang)
        sin_ref[...] = jnp.where((lane & 63) < 32, -s, s)

    return pl.pallas_call(
        body, name="rope_tables", grid=(T // tm,),
        in_specs=[pl.BlockSpec((tm, 1), lambda i: (i, 0)), pl.BlockSpec((1, 128), lambda i: (0, 0))],
        out_specs=[pl.BlockSpec((tm, 128), lambda i: (i, 0))] * 2,
        out_shape=[jax.ShapeDtypeStruct((T, 128), F32)] * 2,
        compiler_params=_cp("parallel"),
    )(pos_col, inv_freq_row)


def _in_proj_fwd(x, cos, sin_s, mod6, norm1_w, w_pad, tm):
    T = x.shape[0]

    def body(x_ref, cos_ref, sin_ref, mod_ref, nw_ref, w_hbm, qkv_ref, z_ref, xbc_ref, dt_ref, h_ref, w_vmem, sem):
        _load_resident(w_hbm, w_vmem, sem)
        xv = x_ref[...]
        r = lax.rsqrt(jnp.mean(xv * xv, axis=-1, keepdims=True) + EPS)
        h = (xv * r * nw_ref[...]) * (1.0 + mod_ref[1:2, :]) + mod_ref[0:1, :]
        hb = h.astype(BF16)
        h_ref[...] = hb
        proj = _dot(hb, w_vmem[...])
        cs, sn = cos_ref[...], sin_ref[...]
        for j in range(5):
            qkv_ref[:, 128 * j:128 * (j + 1)] = _rope_fwd(proj[:, 128 * j:128 * (j + 1)], cs, sn).astype(BF16)
        qkv_ref[:, 640:768] = proj[:, 640:768].astype(BF16)
        z_ref[...] = proj[:, 768:1280]
        xbc_ref[...] = proj[:, 1280:2304]
        dt_ref[...] = proj[:, 2304:2432]

    row = lambda w: pl.BlockSpec((tm, w), lambda i: (i, 0))
    full = lambda a: pl.BlockSpec(a.shape, lambda i: (0,) * a.ndim)
    return pl.pallas_call(
        body, name="in_proj_fwd", grid=(T // tm,),
        in_specs=[row(D), row(128), row(128), full(mod6), full(norm1_w), pl.BlockSpec(memory_space=pl.ANY)],
        out_specs=[row(768), row(512), row(1024), row(128), row(D)],
        out_shape=[jax.ShapeDtypeStruct((T, 768), BF16), jax.ShapeDtypeStruct((T, 512), F32),
                   jax.ShapeDtypeStruct((T, 1024), F32), jax.ShapeDtypeStruct((T, 128), F32),
                   jax.ShapeDtypeStruct((T, D), BF16)],
        scratch_shapes=[pltpu.VMEM((D, IN_PAD), BF16), pltpu.SemaphoreType.DMA],
        compiler_params=_cp("arbitrary"),
    )(x, cos, sin_s, mod6, norm1_w, w_pad)


def _head_variants(pair, j):
    lane = _iota(pair.shape, 1)
    lo = lane < 64
    kv = j // 2
    ev = jnp.where(lo, pair, 0.0)
    od = jnp.where(lo, 0.0, pair)
    if kv == 0:
        od = pltpu.roll(od, 64, 1)
    else:
        ev = pltpu.roll(ev, 64, 1)
    return ev.astype(BF16), od.astype(BF16)


def _kv_variants(vcat):
    lane = _iota(vcat.shape, 1)
    lo = lane < 64
    v0 = jnp.where(lo, vcat, 0.0)
    v1 = jnp.where(lo, 0.0, vcat)
    out = {
        (0, 0): v0, (0, 1): pltpu.roll(v0, 64, 1),
        (1, 0): pltpu.roll(v1, 64, 1), (1, 1): v1,
    }
    return {k: v.astype(BF16) for k, v in out.items()}


def _attn_mask(n):
    i = _iota((BLK, 2 * BLK), 0)
    j = _iota((BLK, 2 * BLK), 1)
    return (j > i) & (j <= i + BLK) & ((n > 0) | (j >= BLK))


def _attn_fwd(qkv, sinks):
    T = qkv.shape[0]
    nb = T // BLK

    def body(sink_ref, q_ref, kc_ref, kp_ref, vc_ref, vp_ref, o_ref, lse_ref):
        n = pl.program_id(0)
        valid = _attn_mask(n)
        kcat = jnp.concatenate([kp_ref[...], kc_ref[...]], axis=0)
        vvar = _kv_variants(jnp.concatenate([vp_ref[...], vc_ref[...]], axis=0).astype(F32))
        lane = _iota((BLK, 128), 1)
        lse_acc = jnp.zeros((BLK, 128), F32)
        for j in range(4):
            qv = _head_variants(q_ref[:, 128 * j:128 * (j + 1)].astype(F32), j)
            acc = jnp.zeros((BLK, 128), F32)
            for par in range(2):
                h = 2 * j + par
                sink = sink_ref[0, h]
                s = jnp.where(valid, _dot_nt(qv[par], kcat) * 0.125, NEG)
                m = jnp.maximum(jnp.max(s, axis=1, keepdims=True), sink)
                p = jnp.exp(s - m)
                den = jnp.sum(p, axis=1, keepdims=True) + jnp.exp(sink - m)
                probs = (p * (1.0 / den)).astype(BF16)
                acc = acc + _dot(probs, vvar[(j // 2, par)])
                lse_acc = jnp.where(lane == h, m + jnp.log(den), lse_acc)
            o_ref[:, 128 * j:128 * (j + 1)] = acc.astype(BF16)
        lse_ref[...] = lse_acc

    prev = lambda n: jnp.maximum(n - 1, 0)
    return pl.pallas_call(
        body, name="attn_fwd", grid=(nb,),
        in_specs=[pl.BlockSpec(memory_space=pltpu.SMEM),
                  pl.BlockSpec((BLK, 512), lambda n: (n, 0)),
                  pl.BlockSpec((BLK, 128), lambda n: (n, 4)),
                  pl.BlockSpec((BLK, 128), lambda n: (prev(n), 4)),
                  pl.BlockSpec((BLK, 128), lambda n: (n, 5)),
                  pl.BlockSpec((BLK, 128), lambda n: (prev(n), 5))],
        out_specs=[pl.BlockSpec((BLK, 512), lambda n: (n, 0)), pl.BlockSpec((BLK, 128), lambda n: (n, 0))],
        out_shape=[jax.ShapeDtypeStruct((T, 512), BF16), jax.ShapeDtypeStruct((T, 128), F32)],
        compiler_params=_cp("parallel"),
    )(sinks, qkv, qkv, qkv, qkv, qkv)


def _attn_bwd(qkv, sinks, lse, dmix, cos, sin_s):
    T = qkv.shape[0]
    nb = T // BLK

    def body(sink_ref, q_ref, kc_ref, kp_ref, vc_ref, vp_ref, lse_ref, do_ref, cq_ref, sq_ref, ck_ref, sk_ref,
             out_ref, ds_ref, dq_car, dk_car, dv_car):
        n = pl.program_id(0)
        lane = _iota((BLK, 128), 1)

        @pl.when(n == 0)
        def _():
            ds_ref[...] = jnp.zeros_like(ds_ref)
            dq_car[...] = jnp.zeros_like(dq_car)
            dk_car[...] = jnp.zeros_like(dk_car)
            dv_car[...] = jnp.zeros_like(dv_car)

        @pl.when(n < nb)
        def _():
            valid = _attn_mask(n)
            kcat = jnp.concatenate([kp_ref[...], kc_ref[...]], axis=0)
            vcat = jnp.concatenate([vp_ref[...], vc_ref[...]], axis=0)
            kvar = _kv_variants(kcat.astype(F32))
            lse_v = lse_ref[...]
            dkc = jnp.zeros((2 * BLK, 128), F32)
            dvc = jnp.zeros((2 * BLK, 128), F32)
            dsk = jnp.zeros((1, 128), F32)
            out_ref[:, 0:512] = dq_car[...]
            for j in range(4):
                qv = _head_variants(q_ref[:, 128 * j:128 * (j + 1)].astype(F32), j)
                dov = _head_variants(do_ref[:, 128 * j:128 * (j + 1)], j)
                dq_acc = jnp.zeros((BLK, 128), F32)
                for par in range(2):
                    h = 2 * j + par
                    sink = sink_ref[0, h]
                    lse_h = jnp.sum(jnp.where(lane == h, lse_v, 0.0), axis=1, keepdims=True)
                    s = jnp.where(valid, _dot_nt(qv[par], kcat) * 0.125, NEG)
                    p = jnp.exp(s - lse_h)
                    dp = _dot_nt(dov[par], vcat)
                    delta = jnp.sum(p * dp, axis=1, keepdims=True)
                    dsc = (p * (dp - delta) * 0.125).astype(BF16)
                    dq_acc = dq_acc + _dot(dsc, kvar[(j // 2, par)])
                    dkc = dkc + _dot_tn(dsc, qv[par])
                    dvc = dvc + _dot_tn(p.astype(BF16), dov[par])
                    psink = jnp.exp(sink - lse_h)
                    dsk = dsk + jnp.where(lane[0:1] == h, -jnp.sum(psink * delta), 0.0)
                dq_car[:, 128 * j:128 * (j + 1)] = _rope_bwd(dq_acc, cq_ref[...], sq_ref[...]).astype(BF16)
            ds_ref[...] += dsk
            out_ref[:, 512:640] = _rope_bwd(dk_car[...] + dkc[:BLK], ck_ref[...], sk_ref[...]).astype(BF16)
            out_ref[:, 640:768] = (dv_car[...] + dvc[:BLK]).astype(BF16)
            dk_car[...] = dkc[BLK:]
            dv_car[...] = dvc[BLK:]

        @pl.when(n == nb)
        def _():
            out_ref[:, 0:512] = dq_car[...]
            out_ref[:, 512:640] = _rope_bwd(dk_car[...], ck_ref[...], sk_ref[...]).astype(BF16)
            out_ref[:, 640:768] = dv_car[...].astype(BF16)

    cur = lambda n: jnp.minimum(n, nb - 1)
    prev = lambda n: jnp.maximum(cur(n) - 1, 0)
    outb = lambda n: jnp.maximum(n - 1, 0)
    return pl.pallas_call(
        body, name="attn_bwd", grid=(nb + 1,),
        in_specs=[pl.BlockSpec(memory_space=pltpu.SMEM),
                  pl.BlockSpec((BLK, 512), lambda n: (cur(n), 0)),
                  pl.BlockSpec((BLK, 128), lambda n: (cur(n), 4)),
                  pl.BlockSpec((BLK, 128), lambda n: (prev(n), 4)),
                  pl.BlockSpec((BLK, 128), lambda n: (cur(n), 5)),
                  pl.BlockSpec((BLK, 128), lambda n: (prev(n), 5)),
                  pl.BlockSpec((BLK, 128), lambda n: (cur(n), 0)),
                  pl.BlockSpec((BLK, 512), lambda n: (cur(n), 0)),
                  pl.BlockSpec((BLK, 128), lambda n: (cur(n), 0)),
                  pl.BlockSpec((BLK, 128), lambda n: (cur(n), 0)),
                  pl.BlockSpec((BLK, 128), lambda n: (outb(n), 0)),
                  pl.BlockSpec((BLK, 128), lambda n: (outb(n), 0))],
        out_specs=[pl.BlockSpec((BLK, 768), lambda n: (outb(n), 0)), pl.BlockSpec((1, 128), lambda n: (0, 0))],
        out_shape=[jax.ShapeDtypeStruct((T, 768), BF16), jax.ShapeDtypeStruct((1, 128), F32)],
        scratch_shapes=[pltpu.VMEM((BLK, 512), BF16), pltpu.VMEM((BLK, 128), F32), pltpu.VMEM((BLK, 128), F32)],
        compiler_params=_cp("arbitrary"),
    )(sinks, qkv, qkv, qkv, qkv, qkv, lse, dmix, cos, sin_s, cos, sin_s)


def _expand_mat():
    return (_iota((128, SW), 1) // HD == _iota((128, SW), 0)).astype(F32)


def _expand_mat_t():
    return (_iota((SW, 128), 0) // HD == _iota((SW, 128), 1)).astype(F32)


def _conv_shifts(u, up):
    row = _iota(u.shape, 0)
    out = [u]
    for j in range(1, CONVK):
        out.append(jnp.where(row < j, pltpu.roll(up, j, 0), pltpu.roll(u, j, 0)))
    return out


def _ssd_parts(u, up, cw_ref, cb_ref, dtr, dtb, alog):
    sh = _conv_shifts(u, up)
    co = cb_ref[...] + cw_ref[3:4, :] * sh[0]
    for j in range(1, CONVK):
        co = co + cw_ref[3 - j:4 - j, :] * sh[j]
    sg = _sigmoid(co)
    xc = co * sg
    xx = dtr + dtb
    dt = jnp.maximum(xx, 0.0) + jnp.log(1.0 + jnp.exp(-jnp.abs(xx)))
    a_neg = -jnp.exp(alog)
    tril = _iota((BLK, BLK), 1) <= _iota((BLK, BLK), 0)
    cs = _dot_hi(tril.astype(F32), dt * a_neg)
    e_mat = _expand_mat()
    csx = _dot_hi(cs, e_mat)
    last = csx[BLK - 1:BLK, :]
    return dict(sh=sh, co=co, sg=sg, xc=xc, xx=xx, dt=dt, a_neg=a_neg, tril=tril, cs=cs, cs_t=cs.T,
                ecsx=jnp.exp(csx), dtex=jnp.exp(last - csx), cdx=jnp.exp(last), dtx=_dot_hi(dt, e_mat))


def _decay(parts, h):
    seg = parts["cs"][:, h:h + 1] - parts["cs_t"][h:h + 1, :]
    return jnp.exp(jnp.where(parts["tril"], seg, NEG))


def _group_cols(a, g):
    return a[:, 256 * g:256 * (g + 1)]


def _ssd_fwd(xbc, z, dtr, conv_w, conv_b, dtb, alog, dskx, ssm_w):
    T = xbc.shape[0]
    nc = T // BLK

    def body(u_ref, up_ref, z_ref, dtr_ref, cw_ref, cb_ref, dtb_ref, al_ref, dk_ref, sw_ref,
             yn_ref, yp_ref, st_ref, s_scr):
        n = pl.program_id(0)

        @pl.when(n == 0)
        def _():
            s_scr[...] = jnp.zeros_like(s_scr)

        u = u_ref[...]
        up = jnp.where(n > 0, up_ref[...], 0.0)
        pt = _ssd_parts(u, up, cw_ref, cb_ref, dtr_ref[...], dtb_ref[...], al_ref[...])
        xc = pt["xc"]
        xs = xc[:, :SW]
        bm = [xc[:, 512:640].astype(BF16), xc[:, 640:768].astype(BF16)]
        cm = [xc[:, 768:896].astype(BF16), xc[:, 896:1024].astype(BF16)]
        s_in = s_scr[...]
        st_ref[0] = s_in
        xdt = xs * pt["dtx"]
        xde = (xdt * pt["dtex"]).astype(BF16)
        lane = _iota((BLK, 128), 1)
        lo = lane < 64
        ys, s_new = [], []
        for g in range(2):
            cb = _dot_nt(cm[g], bm[g])
            yoff = _dot(cm[g], _group_cols(s_in, g).astype(BF16))
            s_new.append(_dot_tn(bm[g], _group_cols(xde, g)))
            for jj in range(2):
                j = 2 * g + jj
                chunk = xdt[:, 128 * j:128 * (j + 1)]
                g_ev = (cb * _decay(pt, 2 * j)).astype(BF16)
                g_od = (cb * _decay(pt, 2 * j + 1)).astype(BF16)
                yd = _dot(g_ev, jnp.where(lo, chunk, 0.0).astype(BF16)) + _dot(g_od, jnp.where(lo, 0.0, chunk).astype(BF16))
                ys.append(yd + yoff[:, 128 * jj:128 * (jj + 1)] * pt["ecsx"][:, 128 * j:128 * (j + 1)])
        y = jnp.concatenate(ys, axis=1) + xs * dk_ref[...]
        s_scr[...] = s_in * pt["cdx"] + jnp.concatenate(s_new, axis=1)
        yp_ref[...] = y
        zv = z_ref[...]
        yz = y * (zv * _sigmoid(zv))
        outs = []
        for g in range(2):
            yg = _group_cols(yz, g)
            outs.append(yg * lax.rsqrt(jnp.mean(yg * yg, axis=-1, keepdims=True) + EPS))
        yn_ref[...] = (jnp.concatenate(outs, axis=1) * sw_ref[...]).astype(BF16)

    prev = lambda n: jnp.maximum(n - 1, 0)
    full = lambda a: pl.BlockSpec(a.shape, lambda n: (0,) * a.ndim)
    return pl.pallas_call(
        body, name="ssd_fwd", grid=(nc,),
        in_specs=[pl.BlockSpec((BLK, CONVC), lambda n: (n, 0)), pl.BlockSpec((BLK, CONVC), lambda n: (prev(n), 0)),
                  pl.BlockSpec((BLK, SW), lambda n: (n, 0)), pl.BlockSpec((BLK, 128), lambda n: (n, 0)),
                  full(conv_w), full(conv_b), full(dtb), full(alog), full(dskx), full(ssm_w)],
        out_specs=[pl.BlockSpec((BLK, SW), lambda n: (n, 0)), pl.BlockSpec((BLK, SW), lambda n: (n, 0)),
                   pl.BlockSpec((1, NST, SW), lambda n: (n, 0, 0))],
        out_shape=[jax.ShapeDtypeStruct((T, SW), BF16), jax.ShapeDtypeStruct((T, SW), F32),
                   jax.ShapeDtypeStruct((nc, NST, SW), F32)],
        scratch_shapes=[pltpu.VMEM((NST, SW), F32)],
        compiler_params=_cp("arbitrary"),
    )(xbc, xbc, z, dtr, conv_w, conv_b, dtb, alog, dskx, ssm_w)


def _ssd_bwd(xbc, z, dtr, ypre, states, dmix, conv_w, conv_b, dtb, alog, dskx, ssm_w):
    T = xbc.shape[0]
    nc = T // BLK

    def body(u_ref, up_ref, z_ref, dtr_ref, yp_ref, st_ref, dyn_ref, cw_ref, cb_ref, dtb_ref, al_ref, dk_ref, sw_ref,
             out_ref, dcw_ref, dcb_ref, dsw_ref, dsk_ref, ddtb_ref, dav_ref, ds_scr, dco_scr, dskx_scr):
        i = pl.program_id(0)
        n = nc - 1 - i

        @pl.when(i == 0)
        def _():
            for r in (dcw_ref, dcb_ref, dsw_ref, dsk_ref, ddtb_ref, dav_ref, ds_scr, dco_scr, dskx_scr):
                r[...] = jnp.zeros_like(r)

        u = u_ref[...]
        up = jnp.where(n > 0, up_ref[...], 0.0)
        pt = _ssd_parts(u, up, cw_ref, cb_ref, dtr_ref[...], dtb_ref[...], al_ref[...])
        xc, dtx, ecsx, dtex, cdx = pt["xc"], pt["dtx"], pt["ecsx"], pt["dtex"], pt["cdx"]
        xs = xc[:, :SW]
        bm = [xc[:, 512:640].astype(BF16), xc[:, 640:768].astype(BF16)]
        cm = [xc[:, 768:896].astype(BF16), xc[:, 896:1024].astype(BF16)]
        s_in = st_ref[0]
        ds_out = ds_scr[...]
        e_t = _expand_mat_t()

        zv = z_ref[...]
        sz = _sigmoid(zv)
        silu_z = zv * sz
        ypre = yp_ref[...]
        yz = ypre * silu_z
        dyn = dyn_ref[...]
        sw = sw_ref[...]
        dyz, yns = [], []
        for g in range(2):
            yg = _group_cols(yz, g)
            r = lax.rsqrt(jnp.mean(yg * yg, axis=-1, keepdims=True) + EPS)
            yn = yg * r
            dg = _group_cols(dyn, g) * _group_cols(sw, g)
            dyz.append(r * (dg - yn * jnp.mean(dg * yn, axis=-1, keepdims=True)))
            yns.append(yn)
        dyz = jnp.concatenate(dyz, axis=1)
        dsw_ref[...] += jnp.sum(dyn * jnp.concatenate(yns, axis=1), axis=0, keepdims=True)
        dy = dyz * silu_z
        dz = dyz * ypre * (sz * (1.0 + zv * (1.0 - sz)))

        xdt = xs * dtx
        xdt_b = xdt.astype(BF16)
        edy = (ecsx * dy).astype(BF16)
        xde = (xdt * dtex).astype(BF16)
        lane = _iota((BLK, 128), 1)
        lo = lane < 64
        row8 = _iota((8, 128), 0)
        dcs = jnp.zeros((BLK, 128), F32)
        col_rows = jnp.zeros((8, 128), F32)
        dxdt, bds, yoff, dbs, dcs_g, ds_new = [], [], [], [], [], []
        for g in range(2):
            s_g = _group_cols(s_in, g).astype(BF16)
            dso_g = _group_cols(ds_out, g).astype(BF16)
            cb = _dot_nt(cm[g], bm[g])
            bds.append(_dot(bm[g], dso_g))
            yoff.append(_dot(cm[g], s_g))
            dcb_g = jnp.zeros((BLK, BLK), F32)
            for jj in range(2):
                j = 2 * g + jj
                dy_c = dy[:, 128 * j:128 * (j + 1)]
                xdt_c = xdt_b[:, 128 * j:128 * (j + 1)]
                acc = jnp.zeros((BLK, 128), F32)
                for par in range(2):
                    h = 2 * j + par
                    lm = _decay(pt, h)
                    gm = cb * lm
                    dy_m = (jnp.where(lo, dy_c, 0.0) if par == 0 else jnp.where(lo, 0.0, dy_c)).astype(BF16)
                    dg_h = _dot_nt(dy_m, xdt_c)
                    w_h = dg_h * gm
                    dcs = dcs + jnp.where(lane == h, jnp.sum(w_h, axis=1, keepdims=True), 0.0)
                    col_rows = col_rows + jnp.where(row8 == h, jnp.sum(w_h, axis=0, keepdims=True), 0.0)
                    dcb_g = dcb_g + dg_h * lm
                    acc = acc + _dot_tn(gm.astype(BF16), dy_m)
                dxdt.append(acc)
            dcb_b = dcb_g.astype(BF16)
            dcs_g.append(_dot(dcb_b, bm[g]) + _dot_nt(_group_cols(edy, g), s_g))
            dbs.append(_dot_tn(dcb_b, cm[g]) + _dot_nt(_group_cols(xde, g), dso_g))
            ds_new.append(_dot_tn(cm[g], _group_cols(edy, g)))
        bds = jnp.concatenate(bds, axis=1)
        yoff = jnp.concatenate(yoff, axis=1) * ecsx
        dxdt = jnp.concatenate(dxdt, axis=1) + dtex * bds
        ds_scr[...] = cdx * ds_out + jnp.concatenate(ds_new, axis=1)

        t_m = _dot_hi(dtex * xdt * bds, e_t)
        colsum_t = jnp.concatenate([col_rows, jnp.zeros((BLK - 8, 128), F32)], axis=0).T
        cd = jnp.exp(pt["cs"][BLK - 1:BLK, :])
        sds = jnp.sum(s_in * ds_out, axis=0, keepdims=True)
        last_row = jnp.sum(t_m, axis=0, keepdims=True) + cd * _dot_hi(jnp.broadcast_to(sds, (8, SW)), e_t)[0:1]
        dcs = dcs - colsum_t + _dot_hi(dy * yoff, e_t) - t_m
        dcs = dcs + jnp.where(_iota((BLK, 128), 0) == BLK - 1, last_row, 0.0)
        triu = (_iota((BLK, BLK), 1) >= _iota((BLK, BLK), 0)).astype(F32)
        da = _dot_hi(triu, dcs)
        dt = pt["dt"]
        ddt = da * pt["a_neg"] + _dot_hi(dxdt * xs, e_t)
        dav_ref[...] += jnp.sum(da * dt, axis=0, keepdims=True)
        ddtr = ddt * _sigmoid(pt["xx"])
        ddtb_ref[...] += jnp.sum(ddtr, axis=0, keepdims=True)
        dxs = dxdt * dtx + dy * dk_ref[...]
        dskx_scr[...] += jnp.sum(dy * xs, axis=0, keepdims=True)
        dxc = jnp.concatenate([dxs, dbs[0], dbs[1], dcs_g[0], dcs_g[1]], axis=1)
        co, sg = pt["co"], pt["sg"]
        dco = dxc * (sg * (1.0 + co * (1.0 - sg)))

        dcb_ref[...] += jnp.sum(dco, axis=0, keepdims=True)
        sh = pt["sh"]
        for j in range(CONVK):
            dcw_ref[3 - j:4 - j, :] += jnp.sum(dco * sh[j], axis=0, keepdims=True)
        dnext = dco_scr[...]
        rowc = _iota(dco.shape, 0)
        du = cw_ref[3:4, :] * dco
        for j in range(1, CONVK):
            up_j = jnp.where(rowc >= BLK - j, pltpu.roll(dnext, BLK - j, 0), pltpu.roll(dco, BLK - j, 0))
            du = du + cw_ref[3 - j:4 - j, :] * up_j
        dco_scr[...] = dco
        out_ref[:, 0:512] = dz.astype(BF16)
        out_ref[:, 512:1536] = du.astype(BF16)
        out_ref[:, 1536:1664] = ddtr.astype(BF16)

        @pl.when(i == nc - 1)
        def _():
            dsk_ref[...] = _dot_hi(jnp.broadcast_to(dskx_scr[...], (8, SW)), e_t)[0:1]

    rev = lambda i: nc - 1 - i
    prev = lambda i: jnp.maximum(nc - 2 - i, 0)
    full = lambda a: pl.BlockSpec(a.shape, lambda i: (0,) * a.ndim)
    acc = lambda r, c: pl.BlockSpec((r, c), lambda i: (0, 0))
    return pl.pallas_call(
        body, name="ssd_bwd", grid=(nc,),
        in_specs=[pl.BlockSpec((BLK, CONVC), lambda i: (rev(i), 0)), pl.BlockSpec((BLK, CONVC), lambda i: (prev(i), 0)),
                  pl.BlockSpec((BLK, SW), lambda i: (rev(i), 0)), pl.BlockSpec((BLK, 128), lambda i: (rev(i), 0)),
                  pl.BlockSpec((BLK, SW), lambda i: (rev(i), 0)), pl.BlockSpec((1, NST, SW), lambda i: (rev(i), 0, 0)),
                  pl.BlockSpec((BLK, SW), lambda i: (rev(i), 1)),
                  full(conv_w), full(conv_b), full(dtb), full(alog), full(dskx), full(ssm_w)],
        out_specs=[pl.BlockSpec((BLK, 1664), lambda i: (rev(i), 0)),
                   acc(CONVK, CONVC), acc(1, CONVC), acc(1, SW), acc(1, 128), acc(1, 128), acc(1, 128)],
        out_shape=[jax.ShapeDtypeStruct((T, 1664), BF16),
                   jax.ShapeDtypeStruct((CONVK, CONVC), F32), jax.ShapeDtypeStruct((1, CONVC), F32),
                   jax.ShapeDtypeStruct((1, SW), F32), jax.ShapeDtypeStruct((1, 128), F32),
                   jax.ShapeDtypeStruct((1, 128), F32), jax.ShapeDtypeStruct((1, 128), F32)],
        scratch_shapes=[pltpu.VMEM((NST, SW), F32), pltpu.VMEM((BLK, CONVC), F32), pltpu.VMEM((1, SW), F32)],
        compiler_params=_cp("arbitrary"),
    )(xbc, xbc, z, dtr, ypre, states, dmix, conv_w, conv_b, dtb, alog, dskx, ssm_w)


def _mix_ffn(x, attn, ynorm, tgt, mod6, norm2_w, final_w, w_out, w_gu, w_dn, tm):
    T = x.shape[0]
    nt = T // tm

    def body(x_ref, a_ref, y_ref, t_ref, mod_ref, n2_ref, fw_ref, wo_hbm, wgu_hbm, wdn_hbm,
             sq_ref, dmix_ref, dx1_ref, h2_ref, act_ref, df_ref, dgu_ref, do_ref, sm_ref,
             wo, wgu, wdn, sems):
        i = pl.program_id(0)

        @pl.when(i == 0)
        def _():
            cps = [pltpu.make_async_copy(s, d, sems.at[k]) for k, (s, d) in
                   enumerate(((wo_hbm, wo), (wgu_hbm, wgu), (wdn_hbm, wdn)))]
            for c in cps:
                c.start()
            for c in cps:
                c.wait()
            sq_ref[...] = jnp.zeros_like(sq_ref)
            sm_ref[...] = jnp.zeros_like(sm_ref)

        gate1, shift2, scale2, gate2 = mod_ref[2:3, :], mod_ref[3:4, :], mod_ref[4:5, :], mod_ref[5:6, :]
        n2w, fw = n2_ref[...], fw_ref[...]
        o = _dot(a_ref[...], wo[0:AW, :]) + _dot(y_ref[...], wo[AW:D, :])
        x1 = x_ref[...] + gate1 * o
        r2 = lax.rsqrt(jnp.mean(x1 * x1, axis=-1, keepdims=True) + EPS)
        xh2 = x1 * r2
        n2 = xh2 * n2w
        h2b = (n2 * (1.0 + scale2) + shift2).astype(BF16)
        h2_ref[...] = h2b
        f = jnp.zeros((tm, D), F32)
        saved = []
        for p in range(2):
            gp = _dot(h2b, wgu[p])
            upj = _dot(h2b, wgu[p + 2])
            sg = _sigmoid(gp)
            sl = gp * sg
            actb = (sl * upj).astype(BF16)
            act_ref[p] = actb
            f = f + _dot(actb, wdn[GU_SH * p:GU_SH * (p + 1), :])
            saved.append((gp, upj, sg, sl))
        x2 = x1 + gate2 * f
        r3 = lax.rsqrt(jnp.mean(x2 * x2, axis=-1, keepdims=True) + EPS)
        xh3 = x2 * r3
        err = xh3 * fw - t_ref[...]
        sq_ref[...] += jnp.sum(err * err, axis=0, keepdims=True)
        dy = err * (1.0 / D)
        dfw = jnp.sum(dy * xh3, axis=0, keepdims=True)
        dxh3 = dy * fw
        dx2 = r3 * (dxh3 - xh3 * jnp.mean(dxh3 * xh3, axis=-1, keepdims=True))
        dgate2 = jnp.sum(dx2 * f, axis=0, keepdims=True)
        dfb = (dx2 * gate2).astype(BF16)
        df_ref[...] = dfb
        dh2 = jnp.zeros((tm, D), F32)
        for p in range(2):
            gp, upj, sg, sl = saved[p]
            dact = _dot_nt(dfb, wdn[GU_SH * p:GU_SH * (p + 1), :])
            dg = (dact * upj * (sg * (1.0 + gp * (1.0 - sg)))).astype(BF16)
            du = (dact * sl).astype(BF16)
            dgu_ref[p] = dg
            dgu_ref[p + 2] = du
            dh2 = dh2 + _dot_nt(dg, wgu[p]) + _dot_nt(du, wgu[p + 2])
        dshift2 = jnp.sum(dh2, axis=0, keepdims=True)
        dscale2 = jnp.sum(dh2 * n2, axis=0, keepdims=True)
        dn2 = dh2 * (1.0 + scale2)
        dn2w = jnp.sum(dn2 * xh2, axis=0, keepdims=True)
        dxh2 = dn2 * n2w
        dx1 = dx2 + r2 * (dxh2 - xh2 * jnp.mean(dxh2 * xh2, axis=-1, keepdims=True))
        dx1_ref[...] = dx1
        dgate1 = jnp.sum(dx1 * o, axis=0, keepdims=True)
        dob = (dx1 * gate1).astype(BF16)
        do_ref[...] = dob
        dmix_ref[...] = _dot_nt(dob, wo[...])
        sm_ref[...] += jnp.concatenate(
            [dfw, dn2w, dshift2, dscale2, dgate2, dgate1, jnp.zeros((2, D), F32)], axis=0)

    row = lambda w: pl.BlockSpec((tm, w), lambda i: (i, 0))
    full = lambda a: pl.BlockSpec(a.shape, lambda i: (0,) * a.ndim)
    anyspec = pl.BlockSpec(memory_space=pl.ANY)
    return pl.pallas_call(
        body, name="mix_ffn", grid=(nt,),
        in_specs=[row(D), row(AW), row(SW), row(D), full(mod6), full(norm2_w), full(final_w), anyspec, anyspec, anyspec],
        out_specs=[pl.BlockSpec((1, D), lambda i: (0, 0)), row(D), row(D), row(D),
                   pl.BlockSpec((2, tm, GU_SH), lambda i: (0, i, 0)), row(D),
                   pl.BlockSpec((4, tm, GU_SH), lambda i: (0, i, 0)), row(D),
                   pl.BlockSpec((8, D), lambda i: (0, 0))],
        out_shape=[jax.ShapeDtypeStruct((1, D), F32), jax.ShapeDtypeStruct((T, D), F32), jax.ShapeDtypeStruct((T, D), F32),
                   jax.ShapeDtypeStruct((T, D), BF16), jax.ShapeDtypeStruct((2, T, GU_SH), BF16),
                   jax.ShapeDtypeStruct((T, D), BF16), jax.ShapeDtypeStruct((4, T, GU_SH), BF16),
                   jax.ShapeDtypeStruct((T, D), BF16), jax.ShapeDtypeStruct((8, D), F32)],
        scratch_shapes=[pltpu.VMEM((D, D), BF16), pltpu.VMEM((4, D, GU_SH), BF16), pltpu.VMEM((DFF, D), BF16),
                        pltpu.SemaphoreType.DMA((3,))],
        compiler_params=_cp("arbitrary"),
    )(x, attn, ynorm, tgt, mod6, norm2_w, final_w, w_out, w_gu, w_dn)


def _in_proj_bwd(x, dx1, dqkv, dzxd, mod6, norm1_w, w_pad, tm):
    T = x.shape[0]

    def body(x_ref, dx1_ref, dq_ref, dz_ref, mod_ref, nw_ref, w_hbm, gx_ref, sm_ref, w_vmem, sem):
        _load_resident(w_hbm, w_vmem, sem)

        @pl.when(pl.program_id(0) == 0)
        def _():
            sm_ref[...] = jnp.zeros_like(sm_ref)

        dh = _dot_nt(dq_ref[...], w_vmem[:, 0:768]) + _dot_nt(dz_ref[...], w_vmem[:, 768:IN_PAD])
        xv = x_ref[...]
        nw = nw_ref[...]
        scale1 = mod_ref[1:2, :]
        r = lax.rsqrt(jnp.mean(xv * xv, axis=-1, keepdims=True) + EPS)
        xh = xv * r
        n1 = xh * nw
        dshift = jnp.sum(dh, axis=0, keepdims=True)
        dscale = jnp.sum(dh * n1, axis=0, keepdims=True)
        dn = dh * (1.0 + scale1)
        dnw = jnp.sum(dn * xh, axis=0, keepdims=True)
        dxh = dn * nw
        gx_ref[...] = dx1_ref[...] + r * (dxh - xh * jnp.mean(dxh * xh, axis=-1, keepdims=True))
        sm_ref[...] += jnp.concatenate([dnw, dshift, dscale, jnp.zeros((5, D), F32)], axis=0)

    row = lambda w: pl.BlockSpec((tm, w), lambda i: (i, 0))
    full = lambda a: pl.BlockSpec(a.shape, lambda i: (0,) * a.ndim)
    return pl.pallas_call(
        body, name="in_proj_bwd", grid=(T // tm,),
        in_specs=[row(D), row(D), row(768), row(1664), full(mod6), full(norm1_w), pl.BlockSpec(memory_space=pl.ANY)],
        out_specs=[row(D), pl.BlockSpec((8, D), lambda i: (0, 0))],
        out_shape=[jax.ShapeDtypeStruct((T, D), F32), jax.ShapeDtypeStruct((8, D), F32)],
        scratch_shapes=[pltpu.VMEM((D, IN_PAD), BF16), pltpu.SemaphoreType.DMA],
        compiler_params=_cp("arbitrary"),
    )(x, dx1, dqkv, dzxd, mod6, norm1_w, w_pad)


def _tn_matmul(a3, b3, tt, name, dep):
    ja, T, K = a3.shape
    jb, _, N = b3.shape
    J = max(ja, jb)

    def body(a_ref, b_ref, dep_ref, o_ref):
        t = pl.program_id(1)
        prod = _dot_tn(a_ref[0], b_ref[0])

        @pl.when(t == 0)
        def _():
            o_ref[0] = prod

        @pl.when(t > 0)
        def _():
            o_ref[0] += prod

    return pl.pallas_call(
        body, name=name, grid=(J, T // tt),
        in_specs=[pl.BlockSpec((1, tt, K), lambda j, t: (j if ja > 1 else 0, t, 0)),
                  pl.BlockSpec((1, tt, N), lambda j, t: (j if jb > 1 else 0, t, 0)),
                  pl.BlockSpec((8, 128), lambda j, t: (0, 0))],
        out_specs=pl.BlockSpec((1, K, N), lambda j, t: (j, 0, 0)),
        out_shape=jax.ShapeDtypeStruct((J, K, N), F32),
        compiler_params=_cp("parallel", "arbitrary"),
    )(a3, b3, dep)


def _adam_math(w, g, m, v):
    m = B1 * m + (1.0 - B1) * g
    v = B2 * v + (1.0 - B2) * (g * g)
    m_hat = m / (1.0 - B1 ** STEP)
    v_hat = v / (1.0 - B2 ** STEP)
    delta = -LR * (m_hat / (jnp.sqrt(v_hat) + AEPS) + WD * w)
    return delta, m, v


def _adam_2d(w, g, m, v, rb, name):
    R, C = w.shape

    def body(w_ref, g_ref, m_ref, v_ref, d_ref, mo_ref, vo_ref):
        d, mn, vn = _adam_math(w_ref[...], g_ref[...], m_ref[...], v_ref[...])
        d_ref[...] = d
        mo_ref[...] = mn
        vo_ref[...] = vn

    spec = pl.BlockSpec((rb, C), lambda i: (i, 0))
    return pl.pallas_call(
        body, name=name, grid=(R // rb,), in_specs=[spec] * 4, out_specs=[spec] * 3,
        out_shape=[jax.ShapeDtypeStruct((R, C), F32)] * 3, compiler_params=_cp("parallel"),
    )(w, g, m, v)


def _adam_w_ada(sc_all, dmod_s, w, m, v, rb):
    R, C = w.shape

    def body(sc_ref, dm_ref, w_ref, m_ref, v_ref, g_ref, d_ref, mo_ref, vo_ref):
        g = lax.dot_general(sc_ref[...], dm_ref[...], (((0,), (0,)), ((), ())), precision=HI, preferred_element_type=F32)
        d, mn, vn = _adam_math(w_ref[...], g, m_ref[...], v_ref[...])
        g_ref[...] = g
        d_ref[...] = d
        mo_ref[...] = mn
        vo_ref[...] = vn

    spec = pl.BlockSpec((rb, C), lambda i: (i, 0))
    return pl.pallas_call(
        body, name="adam_w_ada", grid=(R // rb,),
        in_specs=[pl.BlockSpec((8, rb), lambda i: (0, i)), pl.BlockSpec((8, C), lambda i: (0, 0)), spec, spec, spec],
        out_specs=[spec] * 4, out_shape=[jax.ShapeDtypeStruct((R, C), F32)] * 4, compiler_params=_cp("parallel"),
    )(sc_all, dmod_s, w, m, v)


def _adam_small(grads, ws, ms, vs):
    k = len(ws)

    def body(*refs):
        g, w, m, v = refs[0:k], refs[k:2 * k], refs[2 * k:3 * k], refs[3 * k:4 * k]
        d_o, m_o, v_o = refs[4 * k:5 * k], refs[5 * k:6 * k], refs[6 * k:7 * k]
        for i in range(k):
            d, mn, vn = _adam_math(w[i][...], g[i][...], m[i][...], v[i][...])
            d_o[i][...] = d
            m_o[i][...] = mn
            v_o[i][...] = vn

    shapes = [jax.ShapeDtypeStruct(w.shape, F32) for w in ws]
    vm = pl.BlockSpec(memory_space=pltpu.VMEM)
    outs = pl.pallas_call(
        body, name="adam_small", in_specs=[vm] * (4 * k), out_specs=[vm] * (3 * k), out_shape=shapes * 3,
    )(*grads, *ws, *ms, *vs)
    return outs[0:k], outs[k:2 * k], outs[2 * k:3 * k]


def _pos():
    return lax.axis_index("x"), lax.axis_index("y"), lax.axis_index("c")


def _flip(v, bit):
    return 1 - v if bit else v


def _peer(k):
    x, y, c = _pos()
    return (_flip(x, (k >> 2) & 1), _flip(y, (k >> 1) & 1), _flip(c, k & 1))


def _logical(p):
    return 4 * p[0] + 2 * p[1] + p[2]


def _gather8(src_ref, dst_ref, send_sems, recv_sems):
    me = _logical(_pos())
    dst_ref[pl.ds(me, 1)] = src_ref[...][None]
    copies = []
    for k in range(1, 8):
        cp = pltpu.make_async_remote_copy(src_ref, dst_ref.at[me], send_sems.at[k - 1], recv_sems.at[k - 1],
                                          device_id=_peer(k), device_id_type=MESH)
        cp.start()
        copies.append(cp)
    for k in range(1, 8):
        pltpu.make_async_remote_copy(src_ref, dst_ref.at[_logical(_peer(k))], send_sems.at[k - 1], recv_sems.at[k - 1],
                                     device_id=_peer(k), device_id_type=MESH).wait_recv()
    for cp in copies:
        cp.wait_send()


def _rows_select(ref3, width):
    row = _iota((8, width), 0)
    out = jnp.zeros((8, width), F32)
    for i in range(8):
        out = jnp.where(row == i, ref3[i][:, 0:width], out)
    return out


def _mod_exchange(payload, w_ada_s, b_ada4):
    n_sh = w_ada_s.shape[1]

    def body(pay_ref, w_ref, b_ref, gat_ref, mod_ref, token, p3, sa, ra, sb, rb):
        token[...] = jnp.zeros_like(token)
        x, y, c = _pos()
        me = _logical((x, y, c))
        my_s = 2 * x + y
        _gather8(pay_ref, gat_ref, sa, ra)
        cmat = _rows_select(gat_ref, D)
        prod = _dot_hi(cmat * _sigmoid(cmat), w_ref[...])
        for b in range(8):
            p3[b] = prod[b:b + 1, :]
        mod_ref[pl.ds(my_s, 1)] = p3[pl.ds(me, 1)] + b_ref[pl.ds(my_s, 1)]
        ks = (2, 4, 6)
        copies = []
        for i, k in enumerate(ks):
            pr = _peer(k)
            cp = pltpu.make_async_remote_copy(p3.at[_logical(pr)], mod_ref.at[my_s], sb.at[i], rb.at[i],
                                              device_id=pr, device_id_type=MESH)
            cp.start()
            copies.append(cp)
        for i, k in enumerate(ks):
            pr = _peer(k)
            s_src = 2 * pr[0] + pr[1]
            pltpu.make_async_remote_copy(p3.at[0], mod_ref.at[s_src], sb.at[i], rb.at[i],
                                         device_id=pr, device_id_type=MESH).wait_recv()
            mod_ref[pl.ds(s_src, 1)] = mod_ref[pl.ds(s_src, 1)] + b_ref[pl.ds(s_src, 1)]
        for cp in copies:
            cp.wait_send()

    vm = pl.BlockSpec(memory_space=pltpu.VMEM)
    return pl.pallas_call(
        body, name="mod_exchange", in_specs=[vm, vm, vm], out_specs=[vm, vm, vm],
        out_shape=[jax.ShapeDtypeStruct((8, 1, payload.shape[1]), F32), jax.ShapeDtypeStruct((4, 1, n_sh), F32),
                   jax.ShapeDtypeStruct((8, 128), F32)],
        scratch_shapes=[pltpu.VMEM((8, 1, n_sh), F32), pltpu.SemaphoreType.DMA((7,)), pltpu.SemaphoreType.DMA((7,)),
                        pltpu.SemaphoreType.DMA((3,)), pltpu.SemaphoreType.DMA((3,))],
        compiler_params=pltpu.CompilerParams(vmem_limit_bytes=VMEM_LIMIT),
    )(payload, w_ada_s, b_ada4)


def _chips():
    x, y, _ = _pos()
    out = []
    for k in (1, 2, 3):
        px, py = _flip(x, (k >> 1) & 1), _flip(y, k & 1)
        out.append((px, py, 2 * px + py))
    return out


def _half_rows(ref, which):
    half = ref.shape[-2] // 2
    return pl.ds(pl.multiple_of(which * half, 8), half)


def _weight_gather(shards):
    nw = len(shards)

    def body(*refs):
        ins, outs, token = refs[:nw], refs[nw:2 * nw], refs[2 * nw]
        send, recv, fsend, frecv = refs[2 * nw + 1:]
        token[...] = jnp.zeros_like(token)
        x, y, c = _pos()
        my_s = 2 * x + y
        sib = (x, y, 1 - c)
        chips = _chips()
        sends = []
        for w in range(nw):
            mine = _half_rows(ins[w], c)
            for k, (px, py, _) in enumerate(chips):
                cp = pltpu.make_async_remote_copy(ins[w].at[mine], outs[w].at[my_s, mine], send.at[3 * w + k],
                                                  recv.at[3 * w + k], device_id=(px, py, c), device_id_type=MESH)
                cp.start()
                sends.append(cp)
        for w in range(nw):
            mine = _half_rows(ins[w], c)
            for k, (px, py, ps) in enumerate(chips):
                got = outs[w].at[ps, mine]
                pltpu.make_async_remote_copy(got, got, send.at[3 * w + k], recv.at[3 * w + k],
                                             device_id=(px, py, c), device_id_type=MESH).wait_recv()
                cp = pltpu.make_async_remote_copy(got, got, fsend.at[3 * w + k], frecv.at[3 * w + k],
                                                  device_id=sib, device_id_type=MESH)
                cp.start()
                sends.append(cp)
        for w in range(nw):
            other = _half_rows(ins[w], 1 - c)
            for k, (px, py, ps) in enumerate(chips):
                got = outs[w].at[ps, other]
                pltpu.make_async_remote_copy(got, got, fsend.at[3 * w + k], frecv.at[3 * w + k],
                                             device_id=sib, device_id_type=MESH).wait_recv()
        for cp in sends:
            cp.wait_send()

    hbm = pl.BlockSpec(memory_space=pltpu.HBM)
    return pl.pallas_call(
        body, name="weight_gather", in_specs=[hbm] * nw,
        out_specs=[hbm] * nw + [pl.BlockSpec(memory_space=pltpu.VMEM)],
        out_shape=[pltpu.HBM((4,) + s.shape, s.dtype) for s in shards] + [jax.ShapeDtypeStruct((8, 128), F32)],
        scratch_shapes=[pltpu.SemaphoreType.DMA((3 * nw,)), pltpu.SemaphoreType.DMA((3 * nw,)),
                        pltpu.SemaphoreType.DMA((3 * nw,)), pltpu.SemaphoreType.DMA((3 * nw,))],
    )(*shards)


def _small_reduce(vec):
    n = vec.shape[1]

    def body(v_ref, tot_ref, gat_ref, sa, ra):
        _gather8(v_ref, gat_ref, sa, ra)
        tot = gat_ref[0]
        for i in range(1, 8):
            tot = tot + gat_ref[i]
        tot_ref[...] = tot

    vm = pl.BlockSpec(memory_space=pltpu.VMEM)
    return pl.pallas_call(
        body, name="small_reduce", in_specs=[vm], out_specs=[vm, vm],
        out_shape=[jax.ShapeDtypeStruct((1, n), F32), jax.ShapeDtypeStruct((8, 1, n), F32)],
        scratch_shapes=[pltpu.SemaphoreType.DMA((7,)), pltpu.SemaphoreType.DMA((7,))],
    )(vec)


def _add_half(g, sib, c_arr, rb, name):
    _, R, C = g.shape
    half = R // 2
    nb = half // rb

    def body(c_ref, g_ref, s_ref, o_ref):
        o_ref[...] = (g_ref[...] + s_ref[...]).astype(BF16)

    return pl.pallas_call(
        body, name=name,
        grid_spec=pltpu.PrefetchScalarGridSpec(
            num_scalar_prefetch=1, grid=(4, nb),
            in_specs=[pl.BlockSpec((1, rb, C), lambda s, i, c_ref: (s, c_ref[0] * nb + i, 0)),
                      pl.BlockSpec((1, rb, C), lambda s, i, c_ref: (s, i, 0))],
            out_specs=pl.BlockSpec((1, rb, C), lambda s, i, c_ref: (s, i, 0))),
        out_shape=jax.ShapeDtypeStruct((4, half, C), BF16),
        compiler_params=_cp("parallel", "parallel"),
    )(c_arr, g, sib)


def _sum4(r, rb, name):
    _, H, C = r.shape

    def body(r_ref, o_ref):
        o_ref[...] = ((r_ref[0].astype(F32) + r_ref[1].astype(F32)) + r_ref[2].astype(F32)) + r_ref[3].astype(F32)

    return pl.pallas_call(
        body, name=name, grid=(H // rb,),
        in_specs=[pl.BlockSpec((4, rb, C), lambda i: (0, i, 0))], out_specs=pl.BlockSpec((rb, C), lambda i: (i, 0)),
        out_shape=jax.ShapeDtypeStruct((H, C), F32), compiler_params=_cp("parallel"),
    )(r)


HBM_SPEC = pl.BlockSpec(memory_space=pltpu.HBM)
SEM_SPEC = pl.BlockSpec(memory_space=pltpu.SEMAPHORE)
EFFECT = pltpu.SideEffectType.DATAFLOW_SIDE_EFFECTING


def _split_start(name, bufs, n_sem, plan):
    nb = len(bufs)

    def body(*refs):
        ins, send, recv, token = refs[:nb], refs[nb], refs[nb + 1], refs[-1]
        for i, (src, dst, dev, _) in enumerate(plan(ins)):
            pltpu.make_async_remote_copy(src, dst, send.at[i], recv.at[i], device_id=dev, device_id_type=MESH).start()
        token[...] = jnp.zeros_like(token)

    outs = pl.pallas_call(
        body, name=name,
        out_shape=(pltpu.SemaphoreType.DMA((n_sem,)), pltpu.SemaphoreType.DMA((n_sem,)),
                   *[pltpu.HBM(b.shape, b.dtype) for b in bufs], jax.ShapeDtypeStruct((8, 128), F32)),
        in_specs=[HBM_SPEC] * nb,
        out_specs=(SEM_SPEC, SEM_SPEC, *([HBM_SPEC] * nb), pl.BlockSpec(memory_space=pltpu.VMEM)),
        input_output_aliases={i: 2 + i for i in range(nb)},
        compiler_params=pltpu.CompilerParams(has_side_effects=EFFECT),
    )(*[pltpu.with_memory_space_constraint(b, pltpu.HBM) for b in bufs])
    return outs[0], outs[1], list(outs[2:2 + nb]), outs[-1]


def _split_wait(name, send, recv, bufs, after, plan):
    nb = len(bufs)

    def body(*refs):
        ins, send_s, recv_s = refs[:nb], refs[nb], refs[nb + 1]
        for i, (src, dst, dev, mine) in enumerate(plan(ins)):
            pltpu.make_async_remote_copy(src, dst, send_s.at[i], recv_s.at[i], device_id=dev,
                                         device_id_type=MESH).wait_send()
            pltpu.make_async_remote_copy(src, mine, send_s.at[i], recv_s.at[i], device_id=dev,
                                         device_id_type=MESH).wait_recv()

    outs = pl.pallas_call(
        body, name=name, out_shape=[pltpu.HBM(b.shape, b.dtype) for b in bufs],
        in_specs=[HBM_SPEC] * nb + [SEM_SPEC, SEM_SPEC, pl.BlockSpec(memory_space=pl.ANY)],
        out_specs=[HBM_SPEC] * nb, input_output_aliases={i: i for i in range(nb)},
        compiler_params=pltpu.CompilerParams(has_side_effects=EFFECT),
    )(*bufs, send, recv, after)
    return list(outs)


def _plan_gather_ici(nw):
    def plan(refs):
        x, y, c = _pos()
        my_s = 2 * x + y
        out = []
        for w in range(nw):
            mine = _half_rows(refs[w], c)
            for px, py, ps in _chips():
                out.append((refs[w].at[mine], refs[nw + w].at[my_s, mine], (px, py, c), refs[nw + w].at[ps, mine]))
        return out
    return plan


def _plan_gather_fwd(nw):
    def plan(refs):
        x, y, c = _pos()
        out = []
        for w in range(nw):
            mine, other = _half_rows(refs[w], c), _half_rows(refs[w], 1 - c)
            for px, py, ps in _chips():
                got = refs[w].at[ps, mine]
                out.append((got, got, (x, y, 1 - c), refs[w].at[ps, other]))
        return out
    return plan


def _plan_swap(nw):
    def plan(refs):
        x, y, c = _pos()
        return [(refs[w].at[:, _half_rows(refs[w], 1 - c)], refs[nw + w], (x, y, 1 - c), refs[nw + w])
                for w in range(nw)]
    return plan


def _plan_scatter(nw):
    def plan(refs):
        x, y, c = _pos()
        my_s = 2 * x + y
        out = []
        for w in range(nw):
            for px, py, ps in _chips():
                out.append((refs[w].at[ps], refs[nw + w].at[my_s], (px, py, c), refs[nw + w].at[ps]))
        return out
    return plan


def _plan_join(nw):
    def plan(refs):
        x, y, c = _pos()
        out = []
        for w in range(nw):
            land = refs[nw + w]
            out.append((refs[w], land.at[_half_rows(land, c)], (x, y, 1 - c), land.at[_half_rows(land, 1 - c)]))
        return out
    return plan


def _hbm_empty(shape, dtype):
    return pltpu.with_memory_space_constraint(lax.empty(shape, dtype), pltpu.HBM)


def _put_slot(land, own, slot):
    return lax.dynamic_update_slice(land, own[None], (slot,) + (0,) * own.ndim)


def _pad_lanes(a, n):
    return jnp.pad(a, ((0, 0), (0, n - a.shape[1])))


def kernel(x, c, positions, w_ada, b_ada, norm1_w, w_in, conv_w, conv_b, dt_bias, a_log, d_skip, attn_sinks, ssm_norm_w, w_out, norm2_w, w_gate_up, w_down, final_norm_w, loss_target, m_w_ada, m_b_ada, m_norm1_w, m_w_in, m_conv_w, m_conv_b, m_dt_bias, m_a_log, m_d_skip, m_attn_sinks, m_ssm_norm_w, m_w_out, m_norm2_w, m_w_gate_up, m_w_down, m_final_norm_w, v_w_ada, v_b_ada, v_norm1_w, v_w_in, v_conv_w, v_conv_b, v_dt_bias, v_a_log, v_d_skip, v_attn_sinks, v_ssm_norm_w, v_w_out, v_norm2_w, v_w_gate_up, v_w_down, v_final_norm_w):
    T = x.shape[1]
    tm = min(256, T)
    xi, yi, ci = lax.axis_index("x"), lax.axis_index("y"), lax.axis_index("c")
    my_s = 2 * xi + yi
    xs = x[0]
    tgt = loss_target[0]

    payload = jnp.concatenate([c, conv_w[0].reshape(1, CONVK * 256)], axis=1)
    gat, mod4, tok = _mod_exchange(payload, w_ada[0], b_ada.reshape(4, 1, 1536))
    mod6 = mod4.reshape(6, D)
    c_all = gat[:, 0, 0:D]
    cw_dev = gat[:, 0, D:].reshape(4, 2, CONVK, 256)[:, 0]
    conv_full = cw_dev.transpose(1, 0, 2).reshape(CONVK, CONVC)

    w_in_b = (w_in[0] + tok[0, 0]).astype(BF16)
    g_in, tok = _weight_gather([w_in_b])
    g_in = _put_slot(g_in, w_in_b, my_s)
    w_pad = jnp.concatenate([g_in[0], g_in[1], g_in[2], g_in[3], jnp.zeros((D, IN_PAD - IN_PROJ), BF16)], axis=1)
    late = [(w_out[0] + tok[0, 0]).astype(BF16), w_gate_up[0].astype(BF16), w_down[0].astype(BF16)]
    lands = [_hbm_empty((4,) + s.shape, BF16) for s in late]
    s_a, r_a, bufs, tok = _split_start("wgather_ici_start", late + lands, 9, _plan_gather_ici(3))

    inv_freq = (10000.0 ** (-jnp.arange(32, dtype=F32) / 32))
    inv_row = jnp.tile(inv_freq, 4).reshape(1, 128)
    cos, sin_s = _rope_tables(positions.reshape(T, 1), inv_row, tm)
    qkv, z, xbc, dtr, h1b = _in_proj_fwd(xs, cos, sin_s, mod6 + tok[0, 0], norm1_w, w_pad, tm)
    sinks = attn_sinks
    attn, lse = _attn_fwd(qkv, sinks)
    bufs = _split_wait("wgather_ici_wait", s_a, r_a, bufs, attn, _plan_gather_ici(3))
    s_b, r_b, lands, tok = _split_start("wgather_fwd_start", bufs[3:], 9, _plan_gather_fwd(3))
    dtb = _pad_lanes(dt_bias, 128)
    alog = _pad_lanes(a_log, 128)
    dskx = jnp.repeat(d_skip, HD, axis=1)
    ynorm, ypre, states = _ssd_fwd(xbc, z, dtr, conv_full, conv_b, dtb + tok[0, 0], alog, dskx, ssm_norm_w)
    lands = _split_wait("wgather_fwd_wait", s_b, r_b, lands, ynorm, _plan_gather_fwd(3))
    g_out, g_gu, g_dn = [_put_slot(l, s, my_s) for l, s in zip(lands, late)]
    w_out_f = g_out.reshape(D, D)
    w_dn_f = g_dn.reshape(DFF, D)

    fw2 = final_norm_w.reshape(1, D)
    sq, dmix, dx1, h2b, act, dfb, dgu, dob, sm_ffn = _mix_ffn(
        xs, attn, ynorm, tgt, mod6, norm2_w, fw2, w_out_f, g_gu, w_dn_f, tm)
    loss = lax.psum(0.5 / D * jnp.sum(sq), ("x", "y", "c"))

    tt = min(512, T)
    c_arr = ci.reshape(1).astype(jnp.int32)
    tok0 = jnp.zeros((8, 128), F32)
    gw_dn4 = _tn_matmul(act, dfb[None], tt, "dw_down", tok0).reshape(4, DFF // 4, D)
    gw_gu4 = _tn_matmul(h2b[None], dgu, tt, "dw_gate_up", tok0)
    gw_out4 = jnp.concatenate(
        [_tn_matmul(attn[None], dob[None], tt, "dw_out_a", tok0)[0],
         _tn_matmul(ynorm[None], dob[None], tt, "dw_out_y", tok0)[0]], axis=0).reshape(4, D // 4, D)
    big1 = [gw_out4, gw_gu4, gw_dn4]
    rbs1 = [128, 128, 176]
    sib1 = [_hbm_empty((4, g.shape[1] // 2, g.shape[2]), F32) for g in big1]
    s_c, r_c, bufs, tok = _split_start("gswap_start", big1 + sib1, 3, _plan_swap(3))

    dzxd, d_cw, d_cb, d_sw, d_sk, d_dtb, d_av = _ssd_bwd(
        xbc, z, dtr, ypre, states, dmix, conv_full, conv_b, dtb + tok[0, 0], alog, dskx, ssm_norm_w)
    bufs = _split_wait("gswap_wait", s_c, r_c, bufs, dzxd, _plan_swap(3))
    sums1 = [_add_half(g, s, c_arr, rb, "grad_add_%d" % i)
             for i, (g, s, rb) in enumerate(zip(bufs[:3], bufs[3:], rbs1))]
    land1 = [_hbm_empty(p.shape, BF16) for p in sums1]
    s_d, r_d, bufs, tok = _split_start("gscatter_start", sums1 + land1, 9, _plan_scatter(3))
    dqkv, d_sinks = _attn_bwd(qkv, sinks + tok[0:1, 0:8], lse, dmix, cos, sin_s)
    bufs = _split_wait("gscatter_wait", s_d, r_d, bufs, dqkv, _plan_scatter(3))
    slots1 = [_put_slot(l, lax.dynamic_index_in_dim(p, my_s, 0, keepdims=False), my_s)
              for p, l in zip(bufs[:3], bufs[3:])]
    halves1 = [_sum4(r, rb, "grad_sum_%d" % i) for i, (r, rb) in enumerate(zip(slots1, rbs1))]
    full1 = [_hbm_empty((2 * h.shape[0], h.shape[1]), F32) for h in halves1]
    s_e, r_e, bufs, tok = _split_start("gjoin_start", halves1 + full1, 3, _plan_join(3))
    h1_3 = h1b[None]
    gw_in = jnp.concatenate([_tn_matmul(h1_3, dqkv[None], tt, "dw_in_qkv", tok)[0],
                             _tn_matmul(h1_3, dzxd[None], tt, "dw_in_zxd", tok)[0]], axis=1)
    gw_in4 = jnp.stack([gw_in[:, 578 * s:578 * (s + 1)] for s in range(4)])
    bufs = _split_wait("gjoin_wait", s_e, r_e, bufs, gw_in4, _plan_join(3))
    g_out_s, g_gu_s, g_dn_s = [lax.dynamic_update_slice(f, h, (ci * h.shape[0], 0)) for h, f in zip(bufs[:3], bufs[3:])]

    sib0 = _hbm_empty((4, D // 2, IN_PROJ // 4), F32)
    s_f, r_f, bufs, tok = _split_start("gswap_in_start", [gw_in4, sib0], 1, _plan_swap(1))
    bufs = _split_wait("gswap_in_wait", s_f, r_f, bufs, tok, _plan_swap(1))
    sum0 = _add_half(bufs[0], bufs[1], c_arr, 128, "grad_add_in")
    s_g, r_g, bufs, tok = _split_start("gscatter_in_start", [sum0, _hbm_empty(sum0.shape, BF16)], 3, _plan_scatter(1))
    grad_x, sm_in = _in_proj_bwd(xs, dx1, dqkv, dzxd, mod6 + tok[0, 0], norm1_w, w_pad, tm)
    bufs = _split_wait("gscatter_in_wait", s_g, r_g, bufs, grad_x, _plan_scatter(1))
    slot0 = _put_slot(bufs[1], lax.dynamic_index_in_dim(bufs[0], my_s, 0, keepdims=False), my_s)
    half0 = _sum4(slot0, 128, "grad_sum_in")
    s_h, r_h, bufs, tok = _split_start("gjoin_in_start", [half0, _hbm_empty((D, IN_PROJ // 4), F32)], 1, _plan_join(1))
    bufs = _split_wait("gjoin_in_wait", s_h, r_h, bufs, tok, _plan_join(1))
    g_in_s = lax.dynamic_update_slice(bufs[1], bufs[0], (ci * (D // 2), 0))

    a_neg = -jnp.exp(alog)
    pieces = [sm_in[1:2], sm_in[2:3], sm_ffn[5:6], sm_ffn[2:3], sm_ffn[3:4], sm_ffn[4:5],
              sm_in[0:1], sm_ffn[1:2], sm_ffn[0:1], d_cb, d_cw.reshape(1, CONVK * CONVC),
              _pad_lanes(d_sw, SW), d_dtb, d_av * a_neg, d_sk, d_sinks]
    vec = jnp.concatenate(pieces, axis=1)
    tot, allv = _small_reduce(vec)
    o = 0
    offs = []
    for p in pieces:
        offs.append(o)
        o += p.shape[1]
    seg = lambda i, n: tot[:, offs[i]:offs[i] + n]
    g_b_ada = tot[:, 0:6 * D]
    g_norm1, g_norm2, g_final, g_conv_b = seg(6, D), seg(7, D), seg(8, D), seg(9, D)
    g_conv_w = lax.dynamic_slice_in_dim(seg(10, CONVK * CONVC).reshape(CONVK, CONVC), my_s * 256, 256, axis=1)
    g_ssm_w, g_dtb, g_alog, g_dsk, g_sink = seg(11, SW), seg(12, 8), seg(13, 8), seg(14, 8), seg(15, 8)

    small_names = ["b_ada", "norm1_w", "conv_w", "conv_b", "dt_bias", "a_log", "d_skip", "attn_sinks", "ssm_norm_w",
                   "norm2_w", "final_norm_w"]
    small_g = [g_b_ada, g_norm1, g_conv_w, g_conv_b, g_dtb, g_alog, g_dsk, g_sink, g_ssm_w, g_norm2, g_final]
    as2d = lambda a: a.reshape(-1, a.shape[-1])
    small_w = [as2d(a) for a in (b_ada, norm1_w, conv_w, conv_b, dt_bias, a_log, d_skip, attn_sinks, ssm_norm_w,
                                 norm2_w, final_norm_w)]
    small_m = [as2d(a) for a in (m_b_ada, m_norm1_w, m_conv_w, m_conv_b, m_dt_bias, m_a_log, m_d_skip, m_attn_sinks,
                                 m_ssm_norm_w, m_norm2_w, m_final_norm_w)]
    small_v = [as2d(a) for a in (v_b_ada, v_norm1_w, v_conv_w, v_conv_b, v_dt_bias, v_a_log, v_d_skip, v_attn_sinks,
                                 v_ssm_norm_w, v_norm2_w, v_final_norm_w)]
    sd, smn, svn = _adam_small(small_g, small_w, small_m, small_v)

    sc_all = c_all * jax.nn.sigmoid(c_all)
    dmod_all = allv[:, 0, 0:6 * D]
    dmod_s = lax.dynamic_slice_in_dim(dmod_all, my_s * 1536, 1536, axis=1)
    g_ada, d_ada, m_ada, v_ada = _adam_w_ada(sc_all, dmod_s, w_ada[0], m_w_ada[0], v_w_ada[0], 256)
    d_in, m_in, v_in = _adam_2d(w_in[0], g_in_s, m_w_in[0], v_w_in[0], 256, "adam_w_in")
    d_out, m_out, v_out = _adam_2d(w_out[0], g_out_s, m_w_out[0], v_w_out[0], 256, "adam_w_out")
    d_gu, m_gu, v_gu = _adam_2d(w_gate_up[0], g_gu_s, m_w_gate_up[0], v_w_gate_up[0], 256, "adam_w_gate_up")
    d_dn, m_dn, v_dn = _adam_2d(w_down[0], g_dn_s, m_w_down[0], v_w_down[0], 352, "adam_w_down")

    order = ["w_ada", "b_ada", "norm1_w", "w_in", "conv_w", "conv_b", "dt_bias", "a_log", "d_skip", "attn_sinks",
             "ssm_norm_w", "w_out", "norm2_w", "w_gate_up", "w_down", "final_norm_w"]
    shapes = dict(w_ada=w_ada.shape, b_ada=b_ada.shape, norm1_w=norm1_w.shape, w_in=w_in.shape, conv_w=conv_w.shape,
                  conv_b=conv_b.shape, dt_bias=dt_bias.shape, a_log=a_log.shape, d_skip=d_skip.shape,
                  attn_sinks=attn_sinks.shape, ssm_norm_w=ssm_norm_w.shape, w_out=w_out.shape, norm2_w=norm2_w.shape,
                  w_gate_up=w_gate_up.shape, w_down=w_down.shape, final_norm_w=final_norm_w.shape)
    grads = dict(w_ada=g_ada, w_in=g_in_s, w_out=g_out_s, w_gate_up=g_gu_s, w_down=g_dn_s)
    deltas = dict(w_ada=d_ada, w_in=d_in, w_out=d_out, w_gate_up=d_gu, w_down=d_dn)
    new_m = dict(w_ada=m_ada, w_in=m_in, w_out=m_out, w_gate_up=m_gu, w_down=m_dn)
    new_v = dict(w_ada=v_ada, w_in=v_in, w_out=v_out, w_gate_up=v_gu, w_down=v_dn)
    for i, nme in enumerate(small_names):
        grads[nme], deltas[nme], new_m[nme], new_v[nme] = small_g[i], sd[i], smn[i], svn[i]
    outs = [loss, grad_x[None]]
    for table in (grads, deltas, new_m, new_v):
        outs += [table[nme].reshape(shapes[nme]) for nme in order]
    return tuple(outs)
```

```python
import functools
import math

import jax
import jax.numpy as jnp
from jax import lax
from jax.experimental import pallas as pl
from jax.experimental.pallas import tpu as pltpu

F32 = jnp.float32
BF16 = jnp.bfloat16
HI = lax.Precision.HIGHEST
MESH = pl.DeviceIdType.MESH

D = 1024
HD = 64
NQ = 8
AW = 512
KVW = 128
SW = 512
NST = 128
CONVK = 4
CONVC = 1024
BLK = 128
IN_PROJ = 2312
IN_PAD = 2432
DFF = 2816
GU_SH = 1408
EPS = 1e-6
NEG = -1e30
LR, B1, B2, AEPS, WD, STEP = 0.001, 0.9, 0.999, 1e-08, 0.01, 10
VMEM_LIMIT = 58 * 1024 * 1024


def _cp(*sem):
    return pltpu.CompilerParams(dimension_semantics=sem or None, vmem_limit_bytes=VMEM_LIMIT)


def _dot(a, b):
    return jnp.dot(a, b, preferred_element_type=F32)


def _dot_nt(a, b):
    return lax.dot_general(a, b, (((1,), (1,)), ((), ())), preferred_element_type=F32)


def _dot_tn(a, b):
    return lax.dot_general(a, b, (((0,), (0,)), ((), ())), preferred_element_type=F32)


def _dot_hi(a, b):
    return jnp.dot(a, b, precision=HI, preferred_element_type=F32)


def _sigmoid(x):
    return 1.0 / (1.0 + jnp.exp(-x))


def _iota(shape, dim):
    return lax.broadcasted_iota(jnp.int32, shape, dim)


def _load_resident(hbm_ref, vmem_ref, sem):
    @pl.when(pl.program_id(0) == 0)
    def _():
        cp = pltpu.make_async_copy(hbm_ref, vmem_ref, sem)
        cp.start()
        cp.wait()


def _swap32(t):
    lane = _iota(t.shape, 1)
    return jnp.where((lane & 63) < 32, pltpu.roll(t, 96, 1), pltpu.roll(t, 32, 1))


def _rope_fwd(t, cos, sin_s):
    return t * cos + _swap32(t) * sin_s


def _rope_bwd(t, cos, sin_s):
    return t * cos - _swap32(t) * sin_s


def _rope_tables(pos_col, inv_freq_row, tm):
    T = pos_col.shape[0]

    def body(p_ref, f_ref, cos_ref, sin_ref):
        ang = p_ref[...].astype(F32) * f_ref[...]
        lane = _iota((tm, 128), 1)
        s = jnp.sin(ang)
        cos_ref[...] = jnp.cos(ang)
        sin_ref[...] = jnp.where((lane & 63) < 32, -s, s)

    return pl.pallas_call(
        body, name="rope_tables", grid=(T // tm,),
        in_specs=[pl.BlockSpec((tm, 1), lambda i: (i, 0)), pl.BlockSpec((1, 128), lambda i: (0, 0))],
        out_specs=[pl.BlockSpec((tm, 128), lambda i: (i, 0))] * 2,
        out_shape=[jax.ShapeDtypeStruct((T, 128), F32)] * 2,
        compiler_params=_cp("parallel"),
    )(pos_col, inv_freq_row)


def _in_proj_fwd(x, cos, sin_s, mod6, norm1_w, w_pad, tm):
    T = x.shape[0]

    def body(x_ref, cos_ref, sin_ref, mod_ref, nw_ref, w_hbm, qkv_ref, z_ref, xbc_ref, dt_ref, h_ref, w_vmem, sem):
        _load_resident(w_hbm, w_vmem, sem)
        xv = x_ref[...]
        r = lax.rsqrt(jnp.mean(xv * xv, axis=-1, keepdims=True) + EPS)
        h = (xv * r * nw_ref[...]) * (1.0 + mod_ref[1:2, :]) + mod_ref[0:1, :]
        hb = h.astype(BF16)
        h_ref[...] = hb
        proj = _dot(hb, w_vmem[...])
        cs, sn = cos_ref[...], sin_ref[...]
        for j in range(5):
            qkv_ref[:, 128 * j:128 * (j + 1)] = _rope_fwd(proj[:, 128 * j:128 * (j + 1)], cs, sn).astype(BF16)
        qkv_ref[:, 640:768] = proj[:, 640:768].astype(BF16)
        z_ref[...] = proj[:, 768:1280]
        xbc_ref[...] = proj[:, 1280:2304]
        dt_ref[...] = proj[:, 2304:2432]

    row = lambda w: pl.BlockSpec((tm, w), lambda i: (i, 0))
    full = lambda a: pl.BlockSpec(a.shape, lambda i: (0,) * a.ndim)
    return pl.pallas_call(
        body, name="in_proj_fwd", grid=(T // tm,),
        in_specs=[row(D), row(128), row(128), full(mod6), full(norm1_w), pl.BlockSpec(memory_space=pl.ANY)],
        out_specs=[row(768), row(512), row(1024), row(128), row(D)],
        out_shape=[jax.ShapeDtypeStruct((T, 768), BF16), jax.ShapeDtypeStruct((T, 512), F32),
                   jax.ShapeDtypeStruct((T, 1024), F32), jax.ShapeDtypeStruct((T, 128), F32),
                   jax.ShapeDtypeStruct((T, D), BF16)],
        scratch_shapes=[pltpu.VMEM((D, IN_PAD), BF16), pltpu.SemaphoreType.DMA],
        compiler_params=_cp("arbitrary"),
    )(x, cos, sin_s, mod6, norm1_w, w_pad)


def _head_variants(pair, j):
    lane = _iota(pair.shape, 1)
    lo = lane < 64
    kv = j // 2
    ev = jnp.where(lo, pair, 0.0)
    od = jnp.where(lo, 0.0, pair)
    if kv == 0:
        od = pltpu.roll(od, 64, 1)
    else:
        ev = pltpu.roll(ev, 64, 1)
    return ev.astype(BF16), od.astype(BF16)


def _kv_variants(vcat):
    lane = _iota(vcat.shape, 1)
    lo = lane < 64
    v0 = jnp.where(lo, vcat, 0.0)
    v1 = jnp.where(lo, 0.0, vcat)
    out = {
        (0, 0): v0, (0, 1): pltpu.roll(v0, 64, 1),
        (1, 0): pltpu.roll(v1, 64, 1), (1, 1): v1,
    }
    return {k: v.astype(BF16) for k, v in out.items()}


def _attn_mask(n):
    i = _iota((BLK, 2 * BLK), 0)
    j = _iota((BLK, 2 * BLK), 1)
    return (j > i) & (j <= i + BLK) & ((n > 0) | (j >= BLK))


def _attn_fwd(qkv, sinks):
    T = qkv.shape[0]
    nb = T // BLK

    def body(sink_ref, q_ref, kc_ref, kp_ref, vc_ref, vp_ref, o_ref, lse_ref):
        n = pl.program_id(0)
        kcat = jnp.concatenate([kp_ref[...], kc_ref[...]], axis=0)
        vvar = _kv_variants(jnp.concatenate([vp_ref[...], vc_ref[...]], axis=0).astype(F32))
        q_all = jnp.concatenate(
            [v for j in range(4) for v in _head_variants(q_ref[:, 128 * j:128 * (j + 1)].astype(F32), j)], axis=0)
        rows = NQ * BLK
        head = _iota((rows, 1), 0) // BLK
        sink = jnp.zeros((rows, 1), F32)
        for h in range(NQ):
            sink = jnp.where(head == h, sink_ref[0, h], sink)
        i = _iota((rows, 2 * BLK), 0) & (BLK - 1)
        j = _iota((rows, 2 * BLK), 1)
        valid = (j > i) & (j <= i + BLK) & ((n > 0) | (j >= BLK))
        s = jnp.where(valid, _dot_nt(q_all, kcat) * 0.125, NEG)
        m = jnp.maximum(jnp.max(s, axis=1, keepdims=True), sink)
        p = jnp.exp(s - m)
        den = jnp.sum(p, axis=1, keepdims=True) + jnp.exp(sink - m)
        probs = (p * (1.0 / den)).astype(BF16)
        lse = m + jnp.log(den)
        lane = _iota((BLK, 128), 1)
        lse_acc = jnp.zeros((BLK, 128), F32)
        for jj in range(4):
            acc = (_dot(probs[2 * jj * BLK:(2 * jj + 1) * BLK], vvar[(jj // 2, 0)])
                   + _dot(probs[(2 * jj + 1) * BLK:(2 * jj + 2) * BLK], vvar[(jj // 2, 1)]))
            o_ref[:, 128 * jj:128 * (jj + 1)] = acc.astype(BF16)
        for h in range(NQ):
            lse_acc = jnp.where(lane == h, lse[h * BLK:(h + 1) * BLK], lse_acc)
        lse_ref[...] = lse_acc

    prev = lambda n: jnp.maximum(n - 1, 0)
    return pl.pallas_call(
        body, name="attn_fwd", grid=(nb,),
        in_specs=[pl.BlockSpec(memory_space=pltpu.SMEM),
                  pl.BlockSpec((BLK, 512), lambda n: (n, 0)),
                  pl.BlockSpec((BLK, 128), lambda n: (n, 4)),
                  pl.BlockSpec((BLK, 128), lambda n: (prev(n), 4)),
                  pl.BlockSpec((BLK, 128), lambda n: (n, 5)),
                  pl.BlockSpec((BLK, 128), lambda n: (prev(n), 5))],
        out_specs=[pl.BlockSpec((BLK, 512), lambda n: (n, 0)), pl.BlockSpec((BLK, 128), lambda n: (n, 0))],
        out_shape=[jax.ShapeDtypeStruct((T, 512), BF16), jax.ShapeDtypeStruct((T, 128), F32)],
        compiler_params=_cp("parallel"),
    )(sinks, qkv, qkv, qkv, qkv, qkv)


def _attn_bwd(qkv, sinks, lse, dmix, cos, sin_s):
    T = qkv.shape[0]
    nb = T // BLK

    def body(sink_ref, q_ref, kc_ref, kp_ref, vc_ref, vp_ref, lse_ref, do_ref, cq_ref, sq_ref, ck_ref, sk_ref,
             out_ref, ds_ref, dq_car, dk_car, dv_car):
        n = pl.program_id(0)
        lane = _iota((BLK, 128), 1)

        @pl.when(n == 0)
        def _():
            ds_ref[...] = jnp.zeros_like(ds_ref)
            dq_car[...] = jnp.zeros_like(dq_car)
            dk_car[...] = jnp.zeros_like(dk_car)
            dv_car[...] = jnp.zeros_like(dv_car)

        @pl.when(n < nb)
        def _():
            kcat = jnp.concatenate([kp_ref[...], kc_ref[...]], axis=0)
            vcat = jnp.concatenate([vp_ref[...], vc_ref[...]], axis=0)
            kvar = _kv_variants(kcat.astype(F32))
            lse_v = lse_ref[...]
            q_all = jnp.concatenate(
                [v for j in range(4) for v in _head_variants(q_ref[:, 128 * j:128 * (j + 1)].astype(F32), j)], axis=0)
            do_all = jnp.concatenate(
                [v for j in range(4) for v in _head_variants(do_ref[:, 128 * j:128 * (j + 1)], j)], axis=0)
            rows = NQ * BLK
            head = _iota((rows, 1), 0) // BLK
            sink = jnp.zeros((rows, 1), F32)
            for h in range(NQ):
                sink = jnp.where(head == h, sink_ref[0, h], sink)
            lse_col = jnp.concatenate(
                [jnp.sum(jnp.where(lane == h, lse_v, 0.0), axis=1, keepdims=True) for h in range(NQ)], axis=0)
            i = _iota((rows, 2 * BLK), 0) & (BLK - 1)
            jc = _iota((rows, 2 * BLK), 1)
            valid = (jc > i) & (jc <= i + BLK) & ((n > 0) | (jc >= BLK))
            s = jnp.where(valid, _dot_nt(q_all, kcat) * 0.125, NEG)
            p = jnp.exp(s - lse_col)
            dp = _dot_nt(do_all, vcat)
            delta = jnp.sum(p * dp, axis=1, keepdims=True)
            dsc = (p * (dp - delta) * 0.125).astype(BF16)
            dkc = _dot_tn(dsc, q_all)
            dvc = _dot_tn(p.astype(BF16), do_all)
            out_ref[:, 0:512] = dq_car[...]
            for jj in range(4):
                dq_acc = (_dot(dsc[2 * jj * BLK:(2 * jj + 1) * BLK], kvar[(jj // 2, 0)])
                          + _dot(dsc[(2 * jj + 1) * BLK:(2 * jj + 2) * BLK], kvar[(jj // 2, 1)]))
                dq_car[:, 128 * jj:128 * (jj + 1)] = _rope_bwd(dq_acc, cq_ref[...], sq_ref[...]).astype(BF16)
            dsink = jnp.exp(sink - lse_col) * delta
            dsk = jnp.zeros((1, 128), F32)
            for h in range(NQ):
                dsk = dsk + jnp.where(lane[0:1] == h, -jnp.sum(dsink[h * BLK:(h + 1) * BLK]), 0.0)
            ds_ref[...] += dsk
            out_ref[:, 512:640] = _rope_bwd(dk_car[...] + dkc[:BLK], ck_ref[...], sk_ref[...]).astype(BF16)
            out_ref[:, 640:768] = (dv_car[...] + dvc[:BLK]).astype(BF16)
            dk_car[...] = dkc[BLK:]
            dv_car[...] = dvc[BLK:]

        @pl.when(n == nb)
        def _():
            out_ref[:, 0:512] = dq_car[...]
            out_ref[:, 512:640] = _rope_bwd(dk_car[...], ck_ref[...], sk_ref[...]).astype(BF16)
            out_ref[:, 640:768] = dv_car[...].astype(BF16)

    cur = lambda n: jnp.minimum(n, nb - 1)
    prev = lambda n: jnp.maximum(cur(n) - 1, 0)
    outb = lambda n: jnp.maximum(n - 1, 0)
    return pl.pallas_call(
        body, name="attn_bwd", grid=(nb + 1,),
        in_specs=[pl.BlockSpec(memory_space=pltpu.SMEM),
                  pl.BlockSpec((BLK, 512), lambda n: (cur(n), 0)),
                  pl.BlockSpec((BLK, 128), lambda n: (cur(n), 4)),
                  pl.BlockSpec((BLK, 128), lambda n: (prev(n), 4)),
                  pl.BlockSpec((BLK, 128), lambda n: (cur(n), 5)),
                  pl.BlockSpec((BLK, 128), lambda n: (prev(n), 5)),
                  pl.BlockSpec((BLK, 128), lambda n: (cur(n), 0)),
                  pl.BlockSpec((BLK, 512), lambda n: (cur(n), 0)),
                  pl.BlockSpec((BLK, 128), lambda n: (cur(n), 0)),
                  pl.BlockSpec((BLK, 128), lambda n: (cur(n), 0)),
                  pl.BlockSpec((BLK, 128), lambda n: (outb(n), 0)),
                  pl.BlockSpec((BLK, 128), lambda n: (outb(n), 0))],
        out_specs=[pl.BlockSpec((BLK, 768), lambda n: (outb(n), 0)), pl.BlockSpec((1, 128), lambda n: (0, 0))],
        out_shape=[jax.ShapeDtypeStruct((T, 768), BF16), jax.ShapeDtypeStruct((1, 128), F32)],
        scratch_shapes=[pltpu.VMEM((BLK, 512), BF16), pltpu.VMEM((BLK, 128), F32), pltpu.VMEM((BLK, 128), F32)],
        compiler_params=_cp("arbitrary"),
    )(sinks, qkv, qkv, qkv, qkv, qkv, lse, dmix, cos, sin_s, cos, sin_s)


def _ssd_mats():
    e = jnp.arange(SW)[None, :] // HD == jnp.arange(128)[:, None]
    tri = jnp.arange(BLK)[None, :] <= jnp.arange(BLK)[:, None]
    return (jnp.tile(e, (3, 1)).astype(BF16), jnp.tile(e.T, (2, 1)).astype(BF16),
            jnp.tile(tri, (1, 3)).astype(BF16), jnp.tile(tri.T, (1, 3)).astype(BF16))


def _pieces(x, n, axis):
    out, r = [], x
    for i in range(n):
        p = r.astype(BF16)
        out.append(p)
        if i + 1 < n:
            r = r - p.astype(F32)
    return jnp.concatenate(out, axis=axis)


def _expand(x, e3):
    return _dot(_pieces(x, 3, 1), e3)


def _head_sums(x, et2):
    return _dot(_pieces(x, 2, 1), et2)


def _run_sum(tri3, x):
    return _dot(tri3, _pieces(x, 3, 0))


def _shift_down(u, tail, j):
    rolled = pltpu.roll(u, j, 0)
    first = jnp.where(_iota(tail.shape, 0) < j, pltpu.roll(tail, j, 0), rolled[0:8])
    return jnp.concatenate([first, rolled[8:]], axis=0)


def _shift_up(d, head, j):
    rolled = pltpu.roll(d, BLK - j, 0)
    last = jnp.where(_iota(head.shape, 0) >= 8 - j, pltpu.roll(head, 8 - j, 0), rolled[BLK - 8:])
    return jnp.concatenate([rolled[:BLK - 8], last], axis=0)


def _ssd_parts(dtr, dtb, alog, e3, tril3):
    xx = dtr + dtb
    dt = jnp.maximum(xx, 0.0) + jnp.log(1.0 + jnp.exp(-jnp.abs(xx)))
    a_neg = -jnp.exp(alog)
    tril = _iota((BLK, BLK), 1) <= _iota((BLK, BLK), 0)
    cs = _run_sum(tril3, dt * a_neg)
    csx = _expand(cs, e3)
    last = csx[BLK - 1:BLK, :]
    return dict(xx=xx, dt=dt, a_neg=a_neg, tril=tril, cs=cs, cs_t=cs.T,
                ecsx=jnp.exp(csx), dtex=jnp.exp(last - csx), cdx=jnp.exp(last), dtx=_expand(dt, e3))


def _decay(parts, h):
    seg = parts["cs"][:, h:h + 1] - parts["cs_t"][h:h + 1, :]
    return jnp.exp(jnp.where(parts["tril"], seg, NEG))


def _group_cols(a, g):
    return a[:, 256 * g:256 * (g + 1)]


def _ssd_fwd(xbc, z, dtr, conv_w, conv_b, dtb, alog, dskx, ssm_w, mats):
    T = xbc.shape[0]
    nc = T // BLK

    def body(u_ref, tail_ref, z_ref, dtr_ref, cw_ref, cb_ref, dtb_ref, al_ref, dk_ref, sw_ref, e3_ref, tril3_ref,
             yn_ref, yp_ref, st_ref, co_ref, s_scr):
        n = pl.program_id(0)

        @pl.when(n == 0)
        def _():
            s_scr[...] = jnp.zeros_like(s_scr)

        u = u_ref[...]
        tail = jnp.where(n > 0, tail_ref[...], 0.0)
        co = cb_ref[...] + cw_ref[3:4, :] * u
        for j in range(1, CONVK):
            co = co + cw_ref[3 - j:4 - j, :] * _shift_down(u, tail, j)
        co_ref[...] = co
        xc = co * _sigmoid(co)
        pt = _ssd_parts(dtr_ref[...], dtb_ref[...], al_ref[...], e3_ref[...], tril3_ref[...])
        xs = xc[:, :SW]
        bm = [xc[:, 512:640].astype(BF16), xc[:, 640:768].astype(BF16)]
        cm = [xc[:, 768:896].astype(BF16), xc[:, 896:1024].astype(BF16)]
        s_in = s_scr[...]
        st_ref[0] = s_in
        xdt = xs * pt["dtx"]
        xde = (xdt * pt["dtex"]).astype(BF16)
        lane = _iota((BLK, 128), 1)
        lo = lane < 64
        ys, s_new = [], []
        for g in range(2):
            cb = _dot_nt(cm[g], bm[g])
            yoff = _dot(cm[g], _group_cols(s_in, g).astype(BF16))
            s_new.append(_dot_tn(bm[g], _group_cols(xde, g)))
            for jj in range(2):
                j = 2 * g + jj
                chunk = xdt[:, 128 * j:128 * (j + 1)]
                g_ev = (cb * _decay(pt, 2 * j)).astype(BF16)
                g_od = (cb * _decay(pt, 2 * j + 1)).astype(BF16)
                yd = _dot(g_ev, jnp.where(lo, chunk, 0.0).astype(BF16)) + _dot(g_od, jnp.where(lo, 0.0, chunk).astype(BF16))
                ys.append(yd + yoff[:, 128 * jj:128 * (jj + 1)] * pt["ecsx"][:, 128 * j:128 * (j + 1)])
        y = jnp.concatenate(ys, axis=1) + xs * dk_ref[...]
        s_scr[...] = s_in * pt["cdx"] + jnp.concatenate(s_new, axis=1)
        yp_ref[...] = y
        zv = z_ref[...]
        yz = y * (zv * _sigmoid(zv))
        outs = []
        for g in range(2):
            yg = _group_cols(yz, g)
            outs.append(yg * lax.rsqrt(jnp.mean(yg * yg, axis=-1, keepdims=True) + EPS))
        yn_ref[...] = (jnp.concatenate(outs, axis=1) * sw_ref[...]).astype(BF16)

    e3, _, tril3, _ = mats
    tail8 = lambda n: jnp.maximum(n * (BLK // 8) - 1, 0)
    full = lambda a: pl.BlockSpec(a.shape, lambda n: (0,) * a.ndim)
    return pl.pallas_call(
        body, name="ssd_fwd", grid=(nc,),
        in_specs=[pl.BlockSpec((BLK, CONVC), lambda n: (n, 0)), pl.BlockSpec((8, CONVC), lambda n: (tail8(n), 0)),
                  pl.BlockSpec((BLK, SW), lambda n: (n, 0)), pl.BlockSpec((BLK, 128), lambda n: (n, 0)),
                  full(conv_w), full(conv_b), full(dtb), full(alog), full(dskx), full(ssm_w), full(e3), full(tril3)],
        out_specs=[pl.BlockSpec((BLK, SW), lambda n: (n, 0)), pl.BlockSpec((BLK, SW), lambda n: (n, 0)),
                   pl.BlockSpec((1, NST, SW), lambda n: (n, 0, 0)), pl.BlockSpec((BLK, CONVC), lambda n: (n, 0))],
        out_shape=[jax.ShapeDtypeStruct((T, SW), BF16), jax.ShapeDtypeStruct((T, SW), F32),
                   jax.ShapeDtypeStruct((nc, NST, SW), F32), jax.ShapeDtypeStruct((T, CONVC), F32)],
        scratch_shapes=[pltpu.VMEM((NST, SW), F32)],
        compiler_params=_cp("arbitrary"),
    )(xbc, xbc, z, dtr, conv_w, conv_b, dtb, alog, dskx, ssm_w, e3, tril3)


def _ssd_bwd(xbc, co_all, z, dtr, ypre, states, dmix, conv_w, dtb, alog, dskx, ssm_w, mats):
    T = xbc.shape[0]
    nc = T // BLK

    def body(u_ref, co_ref, z_ref, dtr_ref, yp_ref, st_ref, dyn_ref, cw_ref, dtb_ref, al_ref, dk_ref, sw_ref,
             e3_ref, et2_ref, tril3_ref, triu3_ref,
             out_ref, dcw_ref, dcb_ref, dsw_ref, dsk_ref, ddtb_ref, dav_ref, ds_scr, dco_scr, dskx_scr):
        i = pl.program_id(0)

        @pl.when(i == 0)
        def _():
            for r in (dcw_ref, dcb_ref, dsw_ref, dsk_ref, ddtb_ref, dav_ref, ds_scr, dco_scr, dskx_scr):
                r[...] = jnp.zeros_like(r)

        co = co_ref[...]
        sg = _sigmoid(co)
        xc = co * sg
        pt = _ssd_parts(dtr_ref[...], dtb_ref[...], al_ref[...], e3_ref[...], tril3_ref[...])
        dtx, ecsx, dtex, cdx = pt["dtx"], pt["ecsx"], pt["dtex"], pt["cdx"]
        xs = xc[:, :SW]
        bm = [xc[:, 512:640].astype(BF16), xc[:, 640:768].astype(BF16)]
        cm = [xc[:, 768:896].astype(BF16), xc[:, 896:1024].astype(BF16)]
        s_in = st_ref[0]
        ds_out = ds_scr[...]
        e_t = et2_ref[...]

        zv = z_ref[...]
        sz = _sigmoid(zv)
        silu_z = zv * sz
        ypre = yp_ref[...]
        yz = ypre * silu_z
        dyn = dyn_ref[...]
        sw = sw_ref[...]
        dyz, yns = [], []
        for g in range(2):
            yg = _group_cols(yz, g)
            r = lax.rsqrt(jnp.mean(yg * yg, axis=-1, keepdims=True) + EPS)
            yn = yg * r
            dg = _group_cols(dyn, g) * _group_cols(sw, g)
            dyz.append(r * (dg - yn * jnp.mean(dg * yn, axis=-1, keepdims=True)))
            yns.append(yn)
        dyz = jnp.concatenate(dyz, axis=1)
        dsw_ref[...] += jnp.sum(dyn * jnp.concatenate(yns, axis=1), axis=0, keepdims=True)
        dy = dyz * silu_z
        dz = dyz * ypre * (sz * (1.0 + zv * (1.0 - sz)))

        xdt = xs * dtx
        xdt_b = xdt.astype(BF16)
        edy = (ecsx * dy).astype(BF16)
        xde = (xdt * dtex).astype(BF16)
        lane = _iota((BLK, 128), 1)
        lo = lane < 64
        row8 = _iota((8, 128), 0)
        dcs = jnp.zeros((BLK, 128), F32)
        col_rows = jnp.zeros((8, 128), F32)
        dxdt, bds, yoff, dbs, dcs_g, ds_new = [], [], [], [], [], []
        for g in range(2):
            s_g = _group_cols(s_in, g).astype(BF16)
            dso_g = _group_cols(ds_out, g).astype(BF16)
            cb = _dot_nt(cm[g], bm[g])
            bds.append(_dot(bm[g], dso_g))
            yoff.append(_dot(cm[g], s_g))
            dcb_g = jnp.zeros((BLK, BLK), F32)
            for jj in range(2):
                j = 2 * g + jj
                dy_c = dy[:, 128 * j:128 * (j + 1)]
                xdt_c = xdt_b[:, 128 * j:128 * (j + 1)]
                acc = jnp.zeros((BLK, 128), F32)
                for par in range(2):
                    h = 2 * j + par
                    lm = _decay(pt, h)
                    gm = cb * lm
                    dy_m = (jnp.where(lo, dy_c, 0.0) if par == 0 else jnp.where(lo, 0.0, dy_c)).astype(BF16)
                    dg_h = _dot_nt(dy_m, xdt_c)
                    w_h = dg_h * gm
                    dcs = dcs + jnp.where(lane == h, jnp.sum(w_h, axis=1, keepdims=True), 0.0)
                    col_rows = col_rows + jnp.where(row8 == h, jnp.sum(w_h, axis=0, keepdims=True), 0.0)
                    dcb_g = dcb_g + dg_h * lm
                    acc = acc + _dot_tn(gm.astype(BF16), dy_m)
                dxdt.append(acc)
            dcb_b = dcb_g.astype(BF16)
            dcs_g.append(_dot(dcb_b, bm[g]) + _dot_nt(_group_cols(edy, g), s_g))
            dbs.append(_dot_tn(dcb_b, cm[g]) + _dot_nt(_group_cols(xde, g), dso_g))
            ds_new.append(_dot_tn(cm[g], _group_cols(edy, g)))
        bds = jnp.concatenate(bds, axis=1)
        yoff = jnp.concatenate(yoff, axis=1) * ecsx
        dxdt = jnp.concatenate(dxdt, axis=1) + dtex * bds
        ds_scr[...] = cdx * ds_out + jnp.concatenate(ds_new, axis=1)

        t_m = _head_sums(dtex * xdt * bds, e_t)
        colsum_t = jnp.concatenate([col_rows, jnp.zeros((BLK - 8, 128), F32)], axis=0).T
        cd = jnp.exp(pt["cs"][BLK - 1:BLK, :])
        sds = jnp.sum(s_in * ds_out, axis=0, keepdims=True)
        last_row = jnp.sum(t_m, axis=0, keepdims=True) + cd * _head_sums(jnp.broadcast_to(sds, (8, SW)), e_t)[0:1]
        dcs = dcs - colsum_t + _head_sums(dy * yoff, e_t) - t_m
        dcs = dcs + jnp.where(_iota((BLK, 128), 0) == BLK - 1, last_row, 0.0)
        da = _run_sum(triu3_ref[...], dcs)
        dt = pt["dt"]
        ddt = da * pt["a_neg"] + _head_sums(dxdt * xs, e_t)
        dav_ref[...] += jnp.sum(da * dt, axis=0, keepdims=True)
        ddtr = ddt * _sigmoid(pt["xx"])
        ddtb_ref[...] += jnp.sum(ddtr, axis=0, keepdims=True)
        dxs = dxdt * dtx + dy * dk_ref[...]
        dskx_scr[...] += jnp.sum(dy * xs, axis=0, keepdims=True)
        dxc = jnp.concatenate([dxs, dbs[0], dbs[1], dcs_g[0], dcs_g[1]], axis=1)
        dco = dxc * (sg * (1.0 + co * (1.0 - sg)))

        dcb_ref[...] += jnp.sum(dco, axis=0, keepdims=True)
        u = u_ref[...]
        head = dco_scr[...]
        du = jnp.zeros_like(dco)
        for j in range(CONVK):
            up_j = dco if j == 0 else _shift_up(dco, head, j)
            dcw_ref[3 - j:4 - j, :] += jnp.sum(up_j * u, axis=0, keepdims=True)
            du = du + cw_ref[3 - j:4 - j, :] * up_j
        dco_scr[...] = dco[0:8]
        out_ref[:, 0:512] = dz.astype(BF16)
        out_ref[:, 512:1536] = du.astype(BF16)
        out_ref[:, 1536:1664] = ddtr.astype(BF16)

        @pl.when(i == nc - 1)
        def _():
            dsk_ref[...] = _head_sums(jnp.broadcast_to(dskx_scr[...], (8, SW)), e_t)[0:1]

    e3, et2, tril3, triu3 = mats
    rev = lambda i: nc - 1 - i
    full = lambda a: pl.BlockSpec(a.shape, lambda i: (0,) * a.ndim)
    acc = lambda r, c: pl.BlockSpec((r, c), lambda i: (0, 0))
    return pl.pallas_call(
        body, name="ssd_bwd", grid=(nc,),
        in_specs=[pl.BlockSpec((BLK, CONVC), lambda i: (rev(i), 0)), pl.BlockSpec((BLK, CONVC), lambda i: (rev(i), 0)),
                  pl.BlockSpec((BLK, SW), lambda i: (rev(i), 0)), pl.BlockSpec((BLK, 128), lambda i: (rev(i), 0)),
                  pl.BlockSpec((BLK, SW), lambda i: (rev(i), 0)), pl.BlockSpec((1, NST, SW), lambda i: (rev(i), 0, 0)),
                  pl.BlockSpec((BLK, SW), lambda i: (rev(i), 1)),
                  full(conv_w), full(dtb), full(alog), full(dskx), full(ssm_w),
                  full(e3), full(et2), full(tril3), full(triu3)],
        out_specs=[pl.BlockSpec((BLK, 1664), lambda i: (rev(i), 0)),
                   acc(CONVK, CONVC), acc(1, CONVC), acc(1, SW), acc(1, 128), acc(1, 128), acc(1, 128)],
        out_shape=[jax.ShapeDtypeStruct((T, 1664), BF16),
                   jax.ShapeDtypeStruct((CONVK, CONVC), F32), jax.ShapeDtypeStruct((1, CONVC), F32),
                   jax.ShapeDtypeStruct((1, SW), F32), jax.ShapeDtypeStruct((1, 128), F32),
                   jax.ShapeDtypeStruct((1, 128), F32), jax.ShapeDtypeStruct((1, 128), F32)],
        scratch_shapes=[pltpu.VMEM((NST, SW), F32), pltpu.VMEM((8, CONVC), F32), pltpu.VMEM((1, SW), F32)],
        compiler_params=_cp("arbitrary"),
    )(xbc, co_all, z, dtr, ypre, states, dmix, conv_w, dtb, alog, dskx, ssm_w, e3, et2, tril3, triu3)


def _mix_ffn(x, attn, ynorm, tgt, mod6, norm2_w, final_w, w_out, w_gu, w_dn, tm):
    T = x.shape[0]
    nt = T // tm

    def body(x_ref, a_ref, y_ref, t_ref, mod_ref, n2_ref, fw_ref, wo_hbm, wgu_hbm, wdn_hbm,
             sq_ref, dmix_ref, dx1_ref, h2_ref, act_ref, df_ref, dgu_ref, do_ref, sm_ref,
             wo, wgu, wdn, sems):
        i = pl.program_id(0)

        @pl.when(i == 0)
        def _():
            cps = [pltpu.make_async_copy(s, d, sems.at[k]) for k, (s, d) in
                   enumerate(((wo_hbm, wo), (wgu_hbm, wgu), (wdn_hbm, wdn)))]
            for c in cps:
                c.start()
            for c in cps:
                c.wait()
            sq_ref[...] = jnp.zeros_like(sq_ref)
            sm_ref[...] = jnp.zeros_like(sm_ref)

        gate1, shift2, scale2, gate2 = mod_ref[2:3, :], mod_ref[3:4, :], mod_ref[4:5, :], mod_ref[5:6, :]
        n2w, fw = n2_ref[...], fw_ref[...]
        o = _dot(a_ref[...], wo[0:AW, :]) + _dot(y_ref[...], wo[AW:D, :])
        x1 = x_ref[...] + gate1 * o
        r2 = lax.rsqrt(jnp.mean(x1 * x1, axis=-1, keepdims=True) + EPS)
        xh2 = x1 * r2
        n2 = xh2 * n2w
        h2b = (n2 * (1.0 + scale2) + shift2).astype(BF16)
        h2_ref[...] = h2b
        f = jnp.zeros((tm, D), F32)
        saved = []
        for p in range(2):
            gp = _dot(h2b, wgu[p])
            upj = _dot(h2b, wgu[p + 2])
            sg = _sigmoid(gp)
            sl = gp * sg
            actb = (sl * upj).astype(BF16)
            act_ref[p] = actb
            f = f + _dot(actb, wdn[GU_SH * p:GU_SH * (p + 1), :])
            saved.append((gp, upj, sg, sl))
        x2 = x1 + gate2 * f
        r3 = lax.rsqrt(jnp.mean(x2 * x2, axis=-1, keepdims=True) + EPS)
        xh3 = x2 * r3
        err = xh3 * fw - t_ref[...]
        sq_ref[...] += jnp.sum(err * err, axis=0, keepdims=True)
        dy = err * (1.0 / D)
        dfw = jnp.sum(dy * xh3, axis=0, keepdims=True)
        dxh3 = dy * fw
        dx2 = r3 * (dxh3 - xh3 * jnp.mean(dxh3 * xh3, axis=-1, keepdims=True))
        dgate2 = jnp.sum(dx2 * f, axis=0, keepdims=True)
        dfb = (dx2 * gate2).astype(BF16)
        df_ref[...] = dfb
        dh2 = jnp.zeros((tm, D), F32)
        for p in range(2):
            gp, upj, sg, sl = saved[p]
            dact = _dot_nt(dfb, wdn[GU_SH * p:GU_SH * (p + 1), :])
            dg = (dact * upj * (sg * (1.0 + gp * (1.0 - sg)))).astype(BF16)
            du = (dact * sl).astype(BF16)
            dgu_ref[p] = dg
            dgu_ref[p + 2] = du
            dh2 = dh2 + _dot_nt(dg, wgu[p]) + _dot_nt(du, wgu[p + 2])
        dshift2 = jnp.sum(dh2, axis=0, keepdims=True)
        dscale2 = jnp.sum(dh2 * n2, axis=0, keepdims=True)
        dn2 = dh2 * (1.0 + scale2)
        dn2w = jnp.sum(dn2 * xh2, axis=0, keepdims=True)
        dxh2 = dn2 * n2w
        dx1 = dx2 + r2 * (dxh2 - xh2 * jnp.mean(dxh2 * xh2, axis=-1, keepdims=True))
        dx1_ref[...] = dx1
        dgate1 = jnp.sum(dx1 * o, axis=0, keepdims=True)
        dob = (dx1 * gate1).astype(BF16)
        do_ref[...] = dob
        dmix_ref[...] = _dot_nt(dob, wo[...])
        sm_ref[...] += jnp.concatenate(
            [dfw, dn2w, dshift2, dscale2, dgate2, dgate1, jnp.zeros((2, D), F32)], axis=0)

    row = lambda w: pl.BlockSpec((tm, w), lambda i: (i, 0))
    full = lambda a: pl.BlockSpec(a.shape, lambda i: (0,) * a.ndim)
    anyspec = pl.BlockSpec(memory_space=pl.ANY)
    return pl.pallas_call(
        body, name="mix_ffn", grid=(nt,),
        in_specs=[row(D), row(AW), row(SW), row(D), full(mod6), full(norm2_w), full(final_w), anyspec, anyspec, anyspec],
        out_specs=[pl.BlockSpec((1, D), lambda i: (0, 0)), row(D), row(D), row(D),
                   pl.BlockSpec((2, tm, GU_SH), lambda i: (0, i, 0)), row(D),
                   pl.BlockSpec((4, tm, GU_SH), lambda i: (0, i, 0)), row(D),
                   pl.BlockSpec((8, D), lambda i: (0, 0))],
        out_shape=[jax.ShapeDtypeStruct((1, D), F32), jax.ShapeDtypeStruct((T, D), F32), jax.ShapeDtypeStruct((T, D), F32),
                   jax.ShapeDtypeStruct((T, D), BF16), jax.ShapeDtypeStruct((2, T, GU_SH), BF16),
                   jax.ShapeDtypeStruct((T, D), BF16), jax.ShapeDtypeStruct((4, T, GU_SH), BF16),
                   jax.ShapeDtypeStruct((T, D), BF16), jax.ShapeDtypeStruct((8, D), F32)],
        scratch_shapes=[pltpu.VMEM((D, D), BF16), pltpu.VMEM((4, D, GU_SH), BF16), pltpu.VMEM((DFF, D), BF16),
                        pltpu.SemaphoreType.DMA((3,))],
        compiler_params=_cp("arbitrary"),
    )(x, attn, ynorm, tgt, mod6, norm2_w, final_w, w_out, w_gu, w_dn)


def _in_proj_bwd(x, dx1, dqkv, dzxd, mod6, norm1_w, w_pad, tm):
    T = x.shape[0]

    def body(x_ref, dx1_ref, dq_ref, dz_ref, mod_ref, nw_ref, w_hbm, gx_ref, sm_ref, w_vmem, sem):
        _load_resident(w_hbm, w_vmem, sem)

        @pl.when(pl.program_id(0) == 0)
        def _():
            sm_ref[...] = jnp.zeros_like(sm_ref)

        dh = _dot_nt(dq_ref[...], w_vmem[:, 0:768]) + _dot_nt(dz_ref[...], w_vmem[:, 768:IN_PAD])
        xv = x_ref[...]
        nw = nw_ref[...]
        scale1 = mod_ref[1:2, :]
        r = lax.rsqrt(jnp.mean(xv * xv, axis=-1, keepdims=True) + EPS)
        xh = xv * r
        n1 = xh * nw
        dshift = jnp.sum(dh, axis=0, keepdims=True)
        dscale = jnp.sum(dh * n1, axis=0, keepdims=True)
        dn = dh * (1.0 + scale1)
        dnw = jnp.sum(dn * xh, axis=0, keepdims=True)
        dxh = dn * nw
        gx_ref[...] = dx1_ref[...] + r * (dxh - xh * jnp.mean(dxh * xh, axis=-1, keepdims=True))
        sm_ref[...] += jnp.concatenate([dnw, dshift, dscale, jnp.zeros((5, D), F32)], axis=0)

    row = lambda w: pl.BlockSpec((tm, w), lambda i: (i, 0))
    full = lambda a: pl.BlockSpec(a.shape, lambda i: (0,) * a.ndim)
    return pl.pallas_call(
        body, name="in_proj_bwd", grid=(T // tm,),
        in_specs=[row(D), row(D), row(768), row(1664), full(mod6), full(norm1_w), pl.BlockSpec(memory_space=pl.ANY)],
        out_specs=[row(D), pl.BlockSpec((8, D), lambda i: (0, 0))],
        out_shape=[jax.ShapeDtypeStruct((T, D), F32), jax.ShapeDtypeStruct((8, D), F32)],
        scratch_shapes=[pltpu.VMEM((D, IN_PAD), BF16), pltpu.SemaphoreType.DMA],
        compiler_params=_cp("arbitrary"),
    )(x, dx1, dqkv, dzxd, mod6, norm1_w, w_pad)


def _tn_matmul(a3, b3, tt, name, dep):
    ja, T, K = a3.shape
    jb, _, N = b3.shape
    J = max(ja, jb)

    def body(a_ref, b_ref, dep_ref, o_ref):
        t = pl.program_id(1)
        prod = _dot_tn(a_ref[0], b_ref[0])

        @pl.when(t == 0)
        def _():
            o_ref[0] = prod

        @pl.when(t > 0)
        def _():
            o_ref[0] += prod

    return pl.pallas_call(
        body, name=name, grid=(J, T // tt),
        in_specs=[pl.BlockSpec((1, tt, K), lambda j, t: (j if ja > 1 else 0, t, 0)),
                  pl.BlockSpec((1, tt, N), lambda j, t: (j if jb > 1 else 0, t, 0)),
                  pl.BlockSpec((8, 128), lambda j, t: (0, 0))],
        out_specs=pl.BlockSpec((1, K, N), lambda j, t: (j, 0, 0)),
        out_shape=jax.ShapeDtypeStruct((J, K, N), F32),
        compiler_params=_cp("parallel", "arbitrary"),
    )(a3, b3, dep)


def _adam_math(w, g, m, v):
    m = B1 * m + (1.0 - B1) * g
    v = B2 * v + (1.0 - B2) * (g * g)
    m_hat = m / (1.0 - B1 ** STEP)
    v_hat = v / (1.0 - B2 ** STEP)
    delta = -LR * (m_hat / (jnp.sqrt(v_hat) + AEPS) + WD * w)
    return delta, m, v


def _adam_2d(w, g, m, v, rb, name):
    R, C = w.shape

    def body(w_ref, g_ref, m_ref, v_ref, d_ref, mo_ref, vo_ref):
        d, mn, vn = _adam_math(w_ref[...], g_ref[...], m_ref[...], v_ref[...])
        d_ref[...] = d
        mo_ref[...] = mn
        vo_ref[...] = vn

    spec = pl.BlockSpec((rb, C), lambda i: (i, 0))
    return pl.pallas_call(
        body, name=name, grid=(R // rb,), in_specs=[spec] * 4, out_specs=[spec] * 3,
        out_shape=[jax.ShapeDtypeStruct((R, C), F32)] * 3, compiler_params=_cp("parallel"),
    )(w, g, m, v)


def _adam_w_ada(sc_all, dmod_s, w, m, v, rb):
    R, C = w.shape

    def body(sc_ref, dm_ref, w_ref, m_ref, v_ref, g_ref, d_ref, mo_ref, vo_ref):
        g = lax.dot_general(sc_ref[...], dm_ref[...], (((0,), (0,)), ((), ())), precision=HI, preferred_element_type=F32)
        d, mn, vn = _adam_math(w_ref[...], g, m_ref[...], v_ref[...])
        g_ref[...] = g
        d_ref[...] = d
        mo_ref[...] = mn
        vo_ref[...] = vn

    spec = pl.BlockSpec((rb, C), lambda i: (i, 0))
    return pl.pallas_call(
        body, name="adam_w_ada", grid=(R // rb,),
        in_specs=[pl.BlockSpec((8, rb), lambda i: (0, i)), pl.BlockSpec((8, C), lambda i: (0, 0)), spec, spec, spec],
        out_specs=[spec] * 4, out_shape=[jax.ShapeDtypeStruct((R, C), F32)] * 4, compiler_params=_cp("parallel"),
    )(sc_all, dmod_s, w, m, v)


def _adam_small(grads, ws, ms, vs):
    k = len(ws)

    def body(*refs):
        g, w, m, v = refs[0:k], refs[k:2 * k], refs[2 * k:3 * k], refs[3 * k:4 * k]
        d_o, m_o, v_o = refs[4 * k:5 * k], refs[5 * k:6 * k], refs[6 * k:7 * k]
        for i in range(k):
            d, mn, vn = _adam_math(w[i][...], g[i][...], m[i][...], v[i][...])
            d_o[i][...] = d
            m_o[i][...] = mn
            v_o[i][...] = vn

    shapes = [jax.ShapeDtypeStruct(w.shape, F32) for w in ws]
    vm = pl.BlockSpec(memory_space=pltpu.VMEM)
    outs = pl.pallas_call(
        body, name="adam_small", in_specs=[vm] * (4 * k), out_specs=[vm] * (3 * k), out_shape=shapes * 3,
    )(*grads, *ws, *ms, *vs)
    return outs[0:k], outs[k:2 * k], outs[2 * k:3 * k]


def _pos():
    return lax.axis_index("x"), lax.axis_index("y"), lax.axis_index("c")


def _flip(v, bit):
    return 1 - v if bit else v


def _peer(k):
    x, y, c = _pos()
    return (_flip(x, (k >> 2) & 1), _flip(y, (k >> 1) & 1), _flip(c, k & 1))


def _logical(p):
    return 4 * p[0] + 2 * p[1] + p[2]


def _gather8(src_ref, dst_ref, send_sems, recv_sems):
    me = _logical(_pos())
    dst_ref[pl.ds(me, 1)] = src_ref[...][None]
    copies = []
    for k in range(1, 8):
        cp = pltpu.make_async_remote_copy(src_ref, dst_ref.at[me], send_sems.at[k - 1], recv_sems.at[k - 1],
                                          device_id=_peer(k), device_id_type=MESH)
        cp.start()
        copies.append(cp)
    for k in range(1, 8):
        pltpu.make_async_remote_copy(src_ref, dst_ref.at[_logical(_peer(k))], send_sems.at[k - 1], recv_sems.at[k - 1],
                                     device_id=_peer(k), device_id_type=MESH).wait_recv()
    for cp in copies:
        cp.wait_send()


def _rows_select(ref3, width):
    row = _iota((8, width), 0)
    out = jnp.zeros((8, width), F32)
    for i in range(8):
        out = jnp.where(row == i, ref3[i][:, 0:width], out)
    return out


def _mod_exchange(payload, w_ada_s, b_ada4):
    n_sh = w_ada_s.shape[1]

    def body(pay_ref, w_ref, b_ref, gat_ref, mod_ref, token, p3, sa, ra, sb, rb):
        token[...] = jnp.zeros_like(token)
        x, y, c = _pos()
        me = _logical((x, y, c))
        my_s = 2 * x + y
        _gather8(pay_ref, gat_ref, sa, ra)
        cmat = _rows_select(gat_ref, D)
        prod = _dot_hi(cmat * _sigmoid(cmat), w_ref[...])
        for b in range(8):
            p3[b] = prod[b:b + 1, :]
        mod_ref[pl.ds(my_s, 1)] = p3[pl.ds(me, 1)] + b_ref[pl.ds(my_s, 1)]
        ks = (2, 4, 6)
        copies = []
        for i, k in enumerate(ks):
            pr = _peer(k)
            cp = pltpu.make_async_remote_copy(p3.at[_logical(pr)], mod_ref.at[my_s], sb.at[i], rb.at[i],
                                              device_id=pr, device_id_type=MESH)
            cp.start()
            copies.append(cp)
        for i, k in enumerate(ks):
            pr = _peer(k)
            s_src = 2 * pr[0] + pr[1]
            pltpu.make_async_remote_copy(p3.at[0], mod_ref.at[s_src], sb.at[i], rb.at[i],
                                         device_id=pr, device_id_type=MESH).wait_recv()
            mod_ref[pl.ds(s_src, 1)] = mod_ref[pl.ds(s_src, 1)] + b_ref[pl.ds(s_src, 1)]
        for cp in copies:
            cp.wait_send()

    vm = pl.BlockSpec(memory_space=pltpu.VMEM)
    return pl.pallas_call(
        body, name="mod_exchange", in_specs=[vm, vm, vm], out_specs=[vm, vm, vm],
        out_shape=[jax.ShapeDtypeStruct((8, 1, payload.shape[1]), F32), jax.ShapeDtypeStruct((4, 1, n_sh), F32),
                   jax.ShapeDtypeStruct((8, 128), F32)],
        scratch_shapes=[pltpu.VMEM((8, 1, n_sh), F32), pltpu.SemaphoreType.DMA((7,)), pltpu.SemaphoreType.DMA((7,)),
                        pltpu.SemaphoreType.DMA((3,)), pltpu.SemaphoreType.DMA((3,))],
        compiler_params=pltpu.CompilerParams(vmem_limit_bytes=VMEM_LIMIT),
    )(payload, w_ada_s, b_ada4)


def _chips():
    x, y, _ = _pos()
    out = []
    for k in (1, 2, 3):
        px, py = _flip(x, (k >> 1) & 1), _flip(y, k & 1)
        out.append((px, py, 2 * px + py))
    return out


def _half_rows(ref, which):
    half = ref.shape[-2] // 2
    return pl.ds(pl.multiple_of(which * half, 8), half)


def _weight_gather(shards):
    nw = len(shards)

    def body(*refs):
        ins, outs, token = refs[:nw], refs[nw:2 * nw], refs[2 * nw]
        send, recv, fsend, frecv = refs[2 * nw + 1:]
        token[...] = jnp.zeros_like(token)
        x, y, c = _pos()
        my_s = 2 * x + y
        sib = (x, y, 1 - c)
        chips = _chips()
        sends = []
        for w in range(nw):
            mine = _half_rows(ins[w], c)
            for k, (px, py, _) in enumerate(chips):
                cp = pltpu.make_async_remote_copy(ins[w].at[mine], outs[w].at[my_s, mine], send.at[3 * w + k],
                                                  recv.at[3 * w + k], device_id=(px, py, c), device_id_type=MESH)
                cp.start()
                sends.append(cp)
        for w in range(nw):
            mine = _half_rows(ins[w], c)
            for k, (px, py, ps) in enumerate(chips):
                got = outs[w].at[ps, mine]
                pltpu.make_async_remote_copy(got, got, send.at[3 * w + k], recv.at[3 * w + k],
                                             device_id=(px, py, c), device_id_type=MESH).wait_recv()
                cp = pltpu.make_async_remote_copy(got, got, fsend.at[3 * w + k], frecv.at[3 * w + k],
                                                  device_id=sib, device_id_type=MESH)
                cp.start()
                sends.append(cp)
        for w in range(nw):
            other = _half_rows(ins[w], 1 - c)
            for k, (px, py, ps) in enumerate(chips):
                got = outs[w].at[ps, other]
                pltpu.make_async_remote_copy(got, got, fsend.at[3 * w + k], frecv.at[3 * w + k],
                                             device_id=sib, device_id_type=MESH).wait_recv()
        for cp in sends:
            cp.wait_send()

    hbm = pl.BlockSpec(memory_space=pltpu.HBM)
    return pl.pallas_call(
        body, name="weight_gather", in_specs=[hbm] * nw,
        out_specs=[hbm] * nw + [pl.BlockSpec(memory_space=pltpu.VMEM)],
        out_shape=[pltpu.HBM((4,) + s.shape, s.dtype) for s in shards] + [jax.ShapeDtypeStruct((8, 128), F32)],
        scratch_shapes=[pltpu.SemaphoreType.DMA((3 * nw,)), pltpu.SemaphoreType.DMA((3 * nw,)),
                        pltpu.SemaphoreType.DMA((3 * nw,)), pltpu.SemaphoreType.DMA((3 * nw,))],
    )(*shards)


def _small_reduce(vec):
    n = vec.shape[1]

    def body(v_ref, tot_ref, gat_ref, sa, ra):
        _gather8(v_ref, gat_ref, sa, ra)
        tot = gat_ref[0]
        for i in range(1, 8):
            tot = tot + gat_ref[i]
        tot_ref[...] = tot

    vm = pl.BlockSpec(memory_space=pltpu.VMEM)
    return pl.pallas_call(
        body, name="small_reduce", in_specs=[vm], out_specs=[vm, vm],
        out_shape=[jax.ShapeDtypeStruct((1, n), F32), jax.ShapeDtypeStruct((8, 1, n), F32)],
        scratch_shapes=[pltpu.SemaphoreType.DMA((7,)), pltpu.SemaphoreType.DMA((7,))],
    )(vec)


def _add_half(g, sib, c_arr, rb, name):
    _, R, C = g.shape
    half = R // 2
    nb = half // rb

    def body(c_ref, g_ref, s_ref, o_ref):
        o_ref[...] = (g_ref[...] + s_ref[...]).astype(BF16)

    return pl.pallas_call(
        body, name=name,
        grid_spec=pltpu.PrefetchScalarGridSpec(
            num_scalar_prefetch=1, grid=(4, nb),
            in_specs=[pl.BlockSpec((1, rb, C), lambda s, i, c_ref: (s, c_ref[0] * nb + i, 0)),
                      pl.BlockSpec((1, rb, C), lambda s, i, c_ref: (s, i, 0))],
            out_specs=pl.BlockSpec((1, rb, C), lambda s, i, c_ref: (s, i, 0))),
        out_shape=jax.ShapeDtypeStruct((4, half, C), BF16),
        compiler_params=_cp("parallel", "parallel"),
    )(c_arr, g, sib)


def _sum4(r, rb, name):
    _, H, C = r.shape

    def body(r_ref, o_ref):
        o_ref[...] = ((r_ref[0].astype(F32) + r_ref[1].astype(F32)) + r_ref[2].astype(F32)) + r_ref[3].astype(F32)

    return pl.pallas_call(
        body, name=name, grid=(H // rb,),
        in_specs=[pl.BlockSpec((4, rb, C), lambda i: (0, i, 0))], out_specs=pl.BlockSpec((rb, C), lambda i: (i, 0)),
        out_shape=jax.ShapeDtypeStruct((H, C), F32), compiler_params=_cp("parallel"),
    )(r)


HBM_SPEC = pl.BlockSpec(memory_space=pltpu.HBM)
SEM_SPEC = pl.BlockSpec(memory_space=pltpu.SEMAPHORE)
EFFECT = pltpu.SideEffectType.DATAFLOW_SIDE_EFFECTING


def _split_start(name, bufs, n_sem, plan):
    nb = len(bufs)

    def body(*refs):
        ins, send, recv, token = refs[:nb], refs[nb], refs[nb + 1], refs[-1]
        for i, (src, dst, dev, _) in enumerate(plan(ins)):
            pltpu.make_async_remote_copy(src, dst, send.at[i], recv.at[i], device_id=dev, device_id_type=MESH).start()
        token[...] = jnp.zeros_like(token)

    outs = pl.pallas_call(
        body, name=name,
        out_shape=(pltpu.SemaphoreType.DMA((n_sem,)), pltpu.SemaphoreType.DMA((n_sem,)),
                   *[pltpu.HBM(b.shape, b.dtype) for b in bufs], jax.ShapeDtypeStruct((8, 128), F32)),
        in_specs=[HBM_SPEC] * nb,
        out_specs=(SEM_SPEC, SEM_SPEC, *([HBM_SPEC] * nb), pl.BlockSpec(memory_space=pltpu.VMEM)),
        input_output_aliases={i: 2 + i for i in range(nb)},
        compiler_params=pltpu.CompilerParams(has_side_effects=EFFECT),
    )(*[pltpu.with_memory_space_constraint(b, pltpu.HBM) for b in bufs])
    return outs[0], outs[1], list(outs[2:2 + nb]), outs[-1]


def _split_wait(name, send, recv, bufs, after, plan):
    nb = len(bufs)

    def body(*refs):
        ins, send_s, recv_s = refs[:nb], refs[nb], refs[nb + 1]
        for i, (src, dst, dev, mine) in enumerate(plan(ins)):
            pltpu.make_async_remote_copy(src, dst, send_s.at[i], recv_s.at[i], device_id=dev,
                                         device_id_type=MESH).wait_send()
            pltpu.make_async_remote_copy(src, mine, send_s.at[i], recv_s.at[i], device_id=dev,
                                         device_id_type=MESH).wait_recv()

    outs = pl.pallas_call(
        body, name=name, out_shape=[pltpu.HBM(b.shape, b.dtype) for b in bufs],
        in_specs=[HBM_SPEC] * nb + [SEM_SPEC, SEM_SPEC, pl.BlockSpec(memory_space=pl.ANY)],
        out_specs=[HBM_SPEC] * nb, input_output_aliases={i: i for i in range(nb)},
        compiler_params=pltpu.CompilerParams(has_side_effects=EFFECT),
    )(*bufs, send, recv, after)
    return list(outs)


def _plan_gather_ici(nw):
    def plan(refs):
        x, y, c = _pos()
        my_s = 2 * x + y
        out = []
        for w in range(nw):
            mine = _half_rows(refs[w], c)
            for px, py, ps in _chips():
                out.append((refs[w].at[mine], refs[nw + w].at[my_s, mine], (px, py, c), refs[nw + w].at[ps, mine]))
        return out
    return plan


def _plan_gather_fwd(nw):
    def plan(refs):
        x, y, c = _pos()
        out = []
        for w in range(nw):
            mine, other = _half_rows(refs[w], c), _half_rows(refs[w], 1 - c)
            for px, py, ps in _chips():
                got = refs[w].at[ps, mine]
                out.append((got, got, (x, y, 1 - c), refs[w].at[ps, other]))
        return out
    return plan


def _plan_swap(nw):
    def plan(refs):
        x, y, c = _pos()
        return [(refs[w].at[:, _half_rows(refs[w], 1 - c)], refs[nw + w], (x, y, 1 - c), refs[nw + w])
                for w in range(nw)]
    return plan


def _plan_scatter(nw):
    def plan(refs):
        x, y, c = _pos()
        my_s = 2 * x + y
        out = []
        for w in range(nw):
            for px, py, ps in _chips():
                out.append((refs[w].at[ps], refs[nw + w].at[my_s], (px, py, c), refs[nw + w].at[ps]))
        return out
    return plan


def _plan_join(nw):
    def plan(refs):
        x, y, c = _pos()
        out = []
        for w in range(nw):
            land = refs[nw + w]
            out.append((refs[w], land.at[_half_rows(land, c)], (x, y, 1 - c), land.at[_half_rows(land, 1 - c)]))
        return out
    return plan


def _hbm_empty(shape, dtype):
    return pltpu.with_memory_space_constraint(lax.empty(shape, dtype), pltpu.HBM)


def _put_slot(land, own, slot):
    return lax.dynamic_update_slice(land, own[None], (slot,) + (0,) * own.ndim)


def _pad_lanes(a, n):
    return jnp.pad(a, ((0, 0), (0, n - a.shape[1])))


def kernel(x, c, positions, w_ada, b_ada, norm1_w, w_in, conv_w, conv_b, dt_bias, a_log, d_skip, attn_sinks, ssm_norm_w, w_out, norm2_w, w_gate_up, w_down, final_norm_w, loss_target, m_w_ada, m_b_ada, m_norm1_w, m_w_in, m_conv_w, m_conv_b, m_dt_bias, m_a_log, m_d_skip, m_attn_sinks, m_ssm_norm_w, m_w_out, m_norm2_w, m_w_gate_up, m_w_down, m_final_norm_w, v_w_ada, v_b_ada, v_norm1_w, v_w_in, v_conv_w, v_conv_b, v_dt_bias, v_a_log, v_d_skip, v_attn_sinks, v_ssm_norm_w, v_w_out, v_norm2_w, v_w_gate_up, v_w_down, v_final_norm_w):
    T = x.shape[1]
    tm = min(256, T)
    xi, yi, ci = lax.axis_index("x"), lax.axis_index("y"), lax.axis_index("c")
    my_s = 2 * xi + yi
    xs = x[0]
    tgt = loss_target[0]

    payload = jnp.concatenate([c, conv_w[0].reshape(1, CONVK * 256)], axis=1)
    gat, mod4, tok = _mod_exchange(payload, w_ada[0], b_ada.reshape(4, 1, 1536))
    mod6 = mod4.reshape(6, D)
    c_all = gat[:, 0, 0:D]
    cw_dev = gat[:, 0, D:].reshape(4, 2, CONVK, 256)[:, 0]
    conv_full = cw_dev.transpose(1, 0, 2).reshape(CONVK, CONVC)

    w_in_b = (w_in[0] + tok[0, 0]).astype(BF16)
    g_in, tok = _weight_gather([w_in_b])
    g_in = _put_slot(g_in, w_in_b, my_s)
    w_pad = jnp.concatenate([g_in[0], g_in[1], g_in[2], g_in[3], jnp.zeros((D, IN_PAD - IN_PROJ), BF16)], axis=1)
    late = [(w_out[0] + tok[0, 0]).astype(BF16), w_gate_up[0].astype(BF16), w_down[0].astype(BF16)]
    lands = [_hbm_empty((4,) + s.shape, BF16) for s in late]
    s_a, r_a, bufs, tok = _split_start("wgather_ici_start", late + lands, 9, _plan_gather_ici(3))

    inv_freq = (10000.0 ** (-jnp.arange(32, dtype=F32) / 32))
    inv_row = jnp.tile(inv_freq, 4).reshape(1, 128)
    cos, sin_s = _rope_tables(positions.reshape(T, 1), inv_row, tm)
    qkv, z, xbc, dtr, h1b = _in_proj_fwd(xs, cos, sin_s, mod6 + tok[0, 0], norm1_w, w_pad, tm)
    sinks = attn_sinks
    attn, lse = _attn_fwd(qkv, sinks)
    bufs = _split_wait("wgather_ici_wait", s_a, r_a, bufs, attn, _plan_gather_ici(3))
    s_b, r_b, lands, tok = _split_start("wgather_fwd_start", bufs[3:], 9, _plan_gather_fwd(3))
    dtb = _pad_lanes(dt_bias, 128)
    alog = _pad_lanes(a_log, 128)
    dskx = jnp.repeat(d_skip, HD, axis=1)
    mats = _ssd_mats()
    ynorm, ypre, states, conv_pre = _ssd_fwd(xbc, z, dtr, conv_full, conv_b, dtb + tok[0, 0], alog, dskx, ssm_norm_w,
                                             mats)
    lands = _split_wait("wgather_fwd_wait", s_b, r_b, lands, ynorm, _plan_gather_fwd(3))
    g_out, g_gu, g_dn = [_put_slot(l, s, my_s) for l, s in zip(lands, late)]
    w_out_f = g_out.reshape(D, D)
    w_dn_f = g_dn.reshape(DFF, D)

    fw2 = final_norm_w.reshape(1, D)
    sq, dmix, dx1, h2b, act, dfb, dgu, dob, sm_ffn = _mix_ffn(
        xs, attn, ynorm, tgt, mod6, norm2_w, fw2, w_out_f, g_gu, w_dn_f, tm)
    loss = lax.psum(0.5 / D * jnp.sum(sq), ("x", "y", "c"))

    tt = min(2048, T)
    c_arr = ci.reshape(1).astype(jnp.int32)
    tok0 = jnp.zeros((8, 128), F32)
    gw_dn4 = _tn_matmul(act, dfb[None], tt, "dw_down", tok0).reshape(4, DFF // 4, D)
    gw_gu4 = _tn_matmul(h2b[None], dgu, tt, "dw_gate_up", tok0)
    gw_out4 = jnp.concatenate(
        [_tn_matmul(attn[None], dob[None], tt, "dw_out_a", tok0)[0],
         _tn_matmul(ynorm[None], dob[None], tt, "dw_out_y", tok0)[0]], axis=0).reshape(4, D // 4, D)
    big1 = [gw_out4, gw_gu4, gw_dn4]
    rbs1 = [128, 128, 176]
    sib1 = [_hbm_empty((4, g.shape[1] // 2, g.shape[2]), F32) for g in big1]
    s_c, r_c, bufs, tok = _split_start("gswap_start", big1 + sib1, 3, _plan_swap(3))

    dzxd, d_cw, d_cb, d_sw, d_sk, d_dtb, d_av = _ssd_bwd(
        xbc, conv_pre, z, dtr, ypre, states, dmix, conv_full, dtb + tok[0, 0], alog, dskx, ssm_norm_w, mats)
    bufs = _split_wait("gswap_wait", s_c, r_c, bufs, dzxd, _plan_swap(3))
    sums1 = [_add_half(g, s, c_arr, rb, "grad_add_%d" % i)
             for i, (g, s, rb) in enumerate(zip(bufs[:3], bufs[3:], rbs1))]
    land1 = [_hbm_empty(p.shape, BF16) for p in sums1]
    s_d, r_d, bufs, tok = _split_start("gscatter_start", sums1 + land1, 9, _plan_scatter(3))
    dqkv, d_sinks = _attn_bwd(qkv, sinks + tok[0:1, 0:8], lse, dmix, cos, sin_s)
    bufs = _split_wait("gscatter_wait", s_d, r_d, bufs, dqkv, _plan_scatter(3))
    slots1 = [_put_slot(l, lax.dynamic_index_in_dim(p, my_s, 0, keepdims=False), my_s)
              for p, l in zip(bufs[:3], bufs[3:])]
    halves1 = [_sum4(r, rb, "grad_sum_%d" % i) for i, (r, rb) in enumerate(zip(slots1, rbs1))]
    full1 = [_hbm_empty((2 * h.shape[0], h.shape[1]), F32) for h in halves1]
    s_e, r_e, bufs, tok = _split_start("gjoin_start", halves1 + full1, 3, _plan_join(3))
    h1_3 = h1b[None]
    gw_in = jnp.concatenate([_tn_matmul(h1_3, dqkv[None], tt, "dw_in_qkv", tok)[0],
                             _tn_matmul(h1_3, dzxd[None], tt, "dw_in_zxd", tok)[0]], axis=1)
    gw_in4 = jnp.stack([gw_in[:, 578 * s:578 * (s + 1)] for s in range(4)])
    bufs = _split_wait("gjoin_wait", s_e, r_e, bufs, gw_in4, _plan_join(3))
    g_out_s, g_gu_s, g_dn_s = [lax.dynamic_update_slice(f, h, (ci * h.shape[0], 0)) for h, f in zip(bufs[:3], bufs[3:])]

    sib0 = _hbm_empty((4, D // 2, IN_PROJ // 4), F32)
    s_f, r_f, bufs, tok = _split_start("gswap_in_start", [gw_in4, sib0], 1, _plan_swap(1))
    bufs = _split_wait("gswap_in_wait", s_f, r_f, bufs, tok, _plan_swap(1))
    sum0 = _add_half(bufs[0], bufs[1], c_arr, 128, "grad_add_in")
    s_g, r_g, bufs, tok = _split_start("gscatter_in_start", [sum0, _hbm_empty(sum0.shape, BF16)], 3, _plan_scatter(1))
    grad_x, sm_in = _in_proj_bwd(xs, dx1, dqkv, dzxd, mod6 + tok[0, 0], norm1_w, w_pad, tm)
    bufs = _split_wait("gscatter_in_wait", s_g, r_g, bufs, grad_x, _plan_scatter(1))
    slot0 = _put_slot(bufs[1], lax.dynamic_index_in_dim(bufs[0], my_s, 0, keepdims=False), my_s)
    half0 = _sum4(slot0, 128, "grad_sum_in")
    s_h, r_h, bufs, tok = _split_start("gjoin_in_start", [half0, _hbm_empty((D, IN_PROJ // 4), F32)], 1, _plan_join(1))
    bufs = _split_wait("gjoin_in_wait", s_h, r_h, bufs, tok, _plan_join(1))
    g_in_s = lax.dynamic_update_slice(bufs[1], bufs[0], (ci * (D // 2), 0))

    a_neg = -jnp.exp(alog)
    pieces = [sm_in[1:2], sm_in[2:3], sm_ffn[5:6], sm_ffn[2:3], sm_ffn[3:4], sm_ffn[4:5],
              sm_in[0:1], sm_ffn[1:2], sm_ffn[0:1], d_cb, d_cw.reshape(1, CONVK * CONVC),
              _pad_lanes(d_sw, SW), d_dtb, d_av * a_neg, d_sk, d_sinks]
    vec = jnp.concatenate(pieces, axis=1)
    tot, allv = _small_reduce(vec)
    o = 0
    offs = []
    for p in pieces:
        offs.append(o)
        o += p.shape[1]
    seg = lambda i, n: tot[:, offs[i]:offs[i] + n]
    g_b_ada = tot[:, 0:6 * D]
    g_norm1, g_norm2, g_final, g_conv_b = seg(6, D), seg(7, D), seg(8, D), seg(9, D)
    g_conv_w = lax.dynamic_slice_in_dim(seg(10, CONVK * CONVC).reshape(CONVK, CONVC), my_s * 256, 256, axis=1)
    g_ssm_w, g_dtb, g_alog, g_dsk, g_sink = seg(11, SW), seg(12, 8), seg(13, 8), seg(14, 8), seg(15, 8)

    small_names = ["b_ada", "norm1_w", "conv_w", "conv_b", "dt_bias", "a_log", "d_skip", "attn_sinks", "ssm_norm_w",
                   "norm2_w", "final_norm_w"]
    small_g = [g_b_ada, g_norm1, g_conv_w, g_conv_b, g_dtb, g_alog, g_dsk, g_sink, g_ssm_w, g_norm2, g_final]
    as2d = lambda a: a.reshape(-1, a.shape[-1])
    small_w = [as2d(a) for a in (b_ada, norm1_w, conv_w, conv_b, dt_bias, a_log, d_skip, attn_sinks, ssm_norm_w,
                                 norm2_w, final_norm_w)]
    small_m = [as2d(a) for a in (m_b_ada, m_norm1_w, m_conv_w, m_conv_b, m_dt_bias, m_a_log, m_d_skip, m_attn_sinks,
                                 m_ssm_norm_w, m_norm2_w, m_final_norm_w)]
    small_v = [as2d(a) for a in (v_b_ada, v_norm1_w, v_conv_w, v_conv_b, v_dt_bias, v_a_log, v_d_skip, v_attn_sinks,
                                 v_ssm_norm_w, v_norm2_w, v_final_norm_w)]
    sd, smn, svn = _adam_small(small_g, small_w, small_m, small_v)

    sc_all = c_all * jax.nn.sigmoid(c_all)
    dmod_all = allv[:, 0, 0:6 * D]
    dmod_s = lax.dynamic_slice_in_dim(dmod_all, my_s * 1536, 1536, axis=1)
    g_ada, d_ada, m_ada, v_ada = _adam_w_ada(sc_all, dmod_s, w_ada[0], m_w_ada[0], v_w_ada[0], 256)
    d_in, m_in, v_in = _adam_2d(w_in[0], g_in_s, m_w_in[0], v_w_in[0], 256, "adam_w_in")
    d_out, m_out, v_out = _adam_2d(w_out[0], g_out_s, m_w_out[0], v_w_out[0], 256, "adam_w_out")
    d_gu, m_gu, v_gu = _adam_2d(w_gate_up[0], g_gu_s, m_w_gate_up[0], v_w_gate_up[0], 256, "adam_w_gate_up")
    d_dn, m_dn, v_dn = _adam_2d(w_down[0], g_dn_s, m_w_down[0], v_w_down[0], 352, "adam_w_down")

    order = ["w_ada", "b_ada", "norm1_w", "w_in", "conv_w", "conv_b", "dt_bias", "a_log", "d_skip", "attn_sinks",
             "ssm_norm_w", "w_out", "norm2_w", "w_gate_up", "w_down", "final_norm_w"]
    shapes = dict(w_ada=w_ada.shape, b_ada=b_ada.shape, norm1_w=norm1_w.shape, w_in=w_in.shape, conv_w=conv_w.shape,
                  conv_b=conv_b.shape, dt_bias=dt_bias.shape, a_log=a_log.shape, d_skip=d_skip.shape,
                  attn_sinks=attn_sinks.shape, ssm_norm_w=ssm_norm_w.shape, w_out=w_out.shape, norm2_w=norm2_w.shape,
                  w_gate_up=w_gate_up.shape, w_down=w_down.shape, final_norm_w=final_norm_w.shape)
    grads = dict(w_ada=g_ada, w_in=g_in_s, w_out=g_out_s, w_gate_up=g_gu_s, w_down=g_dn_s)
    deltas = dict(w_ada=d_ada, w_in=d_in, w_out=d_out, w_gate_up=d_gu, w_down=d_dn)
    new_m = dict(w_ada=m_ada, w_in=m_in, w_out=m_out, w_gate_up=m_gu, w_down=m_dn)
    new_v = dict(w_ada=v_ada, w_in=v_in, w_out=v_out, w_gate_up=v_gu, w_down=v_dn)
    for i, nme in enumerate(small_names):
        grads[nme], deltas[nme], new_m[nme], new_v[nme] = small_g[i], sd[i], smn[i], svn[i]
    outs = [loss, grad_x[None]]
    for table in (grads, deltas, new_m, new_v):
        outs += [table[nme].reshape(shapes[nme]) for nme in order]
    return tuple(outs)
```

```python
import functools
import math

import jax
import jax.numpy as jnp
from jax import lax
from jax.experimental import pallas as pl
from jax.experimental.pallas import tpu as pltpu

F32 = jnp.float32
BF16 = jnp.bfloat16
HI = lax.Precision.HIGHEST
MESH = pl.DeviceIdType.MESH

D = 1024
HD = 64
NQ = 8
AW = 512
KVW = 128
SW = 512
NST = 128
CONVK = 4
CONVC = 1024
BLK = 128
IN_PROJ = 2312
IN_PAD = 2432
DFF = 2816
GU_SH = 1408
EPS = 1e-6
NEG = -1e30
LR, B1, B2, AEPS, WD, STEP = 0.001, 0.9, 0.999, 1e-08, 0.01, 10
VMEM_LIMIT = 58 * 1024 * 1024


def _cp(*sem):
    return pltpu.CompilerParams(dimension_semantics=sem or None, vmem_limit_bytes=VMEM_LIMIT)


def _dot(a, b):
    return jnp.dot(a, b, preferred_element_type=F32)


def _dot_nt(a, b):
    return lax.dot_general(a, b, (((1,), (1,)), ((), ())), preferred_element_type=F32)


def _dot_tn(a, b):
    return lax.dot_general(a, b, (((0,), (0,)), ((), ())), preferred_element_type=F32)


def _dot_hi(a, b):
    return jnp.dot(a, b, precision=HI, preferred_element_type=F32)


def _sigmoid(x):
    return 1.0 / (1.0 + jnp.exp(-x))


def _iota(shape, dim):
    return lax.broadcasted_iota(jnp.int32, shape, dim)


def _load_resident(hbm_ref, vmem_ref, sem):
    @pl.when(pl.program_id(0) == 0)
    def _():
        cp = pltpu.make_async_copy(hbm_ref, vmem_ref, sem)
        cp.start()
        cp.wait()


def _swap32(t):
    lane = _iota(t.shape, 1)
    return jnp.where((lane & 63) < 32, pltpu.roll(t, 96, 1), pltpu.roll(t, 32, 1))


def _rope_fwd(t, cos, sin_s):
    return t * cos + _swap32(t) * sin_s


def _rope_bwd(t, cos, sin_s):
    return t * cos - _swap32(t) * sin_s


def _rope_tables(pos_col, inv_freq_row, tm):
    T = pos_col.shape[0]

    def body(p_ref, f_ref, cos_ref, sin_ref):
        ang = p_ref[...].astype(F32) * f_ref[...]
        lane = _iota((tm, 128), 1)
        s = jnp.sin(ang)
        cos_ref[...] = jnp.cos(ang)
        sin_ref[...] = jnp.where((lane & 63) < 32, -s, s)

    return pl.pallas_call(
        body, name="rope_tables", grid=(T // tm,),
        in_specs=[pl.BlockSpec((tm, 1), lambda i: (i, 0)), pl.BlockSpec((1, 128), lambda i: (0, 0))],
        out_specs=[pl.BlockSpec((tm, 128), lambda i: (i, 0))] * 2,
        out_shape=[jax.ShapeDtypeStruct((T, 128), F32)] * 2,
        compiler_params=_cp("parallel"),
    )(pos_col, inv_freq_row)


def _in_proj_fwd(x, cos, sin_s, mod6, norm1_w, w_pad, tm):
    T = x.shape[0]

    def body(x_ref, cos_ref, sin_ref, mod_ref, nw_ref, w_hbm, qkv_ref, z_ref, xbc_ref, dt_ref, h_ref, w_vmem, sem):
        _load_resident(w_hbm, w_vmem, sem)
        xv = x_ref[...]
        r = lax.rsqrt(jnp.mean(xv * xv, axis=-1, keepdims=True) + EPS)
        h = (xv * r * nw_ref[...]) * (1.0 + mod_ref[1:2, :]) + mod_ref[0:1, :]
        hb = h.astype(BF16)
        h_ref[...] = hb
        proj = _dot(hb, w_vmem[...])
        cs, sn = cos_ref[...], sin_ref[...]
        for j in range(5):
            qkv_ref[:, 128 * j:128 * (j + 1)] = _rope_fwd(proj[:, 128 * j:128 * (j + 1)], cs, sn).astype(BF16)
        qkv_ref[:, 640:768] = proj[:, 640:768].astype(BF16)
        z_ref[...] = proj[:, 768:1280]
        xbc_ref[...] = proj[:, 1280:2304]
        dt_ref[...] = proj[:, 2304:2432]

    row = lambda w: pl.BlockSpec((tm, w), lambda i: (i, 0))
    full = lambda a: pl.BlockSpec(a.shape, lambda i: (0,) * a.ndim)
    return pl.pallas_call(
        body, name="in_proj_fwd", grid=(T // tm,),
        in_specs=[row(D), row(128), row(128), full(mod6), full(norm1_w), pl.BlockSpec(memory_space=pl.ANY)],
        out_specs=[row(768), row(512), row(1024), row(128), row(D)],
        out_shape=[jax.ShapeDtypeStruct((T, 768), BF16), jax.ShapeDtypeStruct((T, 512), F32),
                   jax.ShapeDtypeStruct((T, 1024), F32), jax.ShapeDtypeStruct((T, 128), F32),
                   jax.ShapeDtypeStruct((T, D), BF16)],
        scratch_shapes=[pltpu.VMEM((D, IN_PAD), BF16), pltpu.SemaphoreType.DMA],
        compiler_params=_cp("arbitrary"),
    )(x, cos, sin_s, mod6, norm1_w, w_pad)


def _head_variants(pair, j):
    lane = _iota(pair.shape, 1)
    lo = lane < 64
    kv = j // 2
    ev = jnp.where(lo, pair, 0.0)
    od = jnp.where(lo, 0.0, pair)
    if kv == 0:
        od = pltpu.roll(od, 64, 1)
    else:
        ev = pltpu.roll(ev, 64, 1)
    return ev.astype(BF16), od.astype(BF16)


def _kv_variants(vcat):
    lane = _iota(vcat.shape, 1)
    lo = lane < 64
    v0 = jnp.where(lo, vcat, 0.0)
    v1 = jnp.where(lo, 0.0, vcat)
    out = {
        (0, 0): v0, (0, 1): pltpu.roll(v0, 64, 1),
        (1, 0): pltpu.roll(v1, 64, 1), (1, 1): v1,
    }
    return {k: v.astype(BF16) for k, v in out.items()}


def _attn_mask(n):
    i = _iota((BLK, 2 * BLK), 0)
    j = _iota((BLK, 2 * BLK), 1)
    return (j > i) & (j <= i + BLK) & ((n > 0) | (j >= BLK))


def _attn_fwd(qkv, sinks):
    T = qkv.shape[0]
    nb = T // BLK

    def body(sink_ref, q_ref, kc_ref, kp_ref, vc_ref, vp_ref, o_ref, lse_ref):
        n = pl.program_id(0)
        kcat = jnp.concatenate([kp_ref[...], kc_ref[...]], axis=0)
        vvar = _kv_variants(jnp.concatenate([vp_ref[...], vc_ref[...]], axis=0).astype(F32))
        q_all = jnp.concatenate(
            [v for j in range(4) for v in _head_variants(q_ref[:, 128 * j:128 * (j + 1)].astype(F32), j)], axis=0)
        rows = NQ * BLK
        head = _iota((rows, 1), 0) // BLK
        sink = jnp.zeros((rows, 1), F32)
        for h in range(NQ):
            sink = jnp.where(head == h, sink_ref[0, h], sink)
        i = _iota((rows, 2 * BLK), 0) & (BLK - 1)
        j = _iota((rows, 2 * BLK), 1)
        valid = (j > i) & (j <= i + BLK) & ((n > 0) | (j >= BLK))
        s = jnp.where(valid, _dot_nt(q_all, kcat) * 0.125, NEG)
        m = jnp.maximum(jnp.max(s, axis=1, keepdims=True), sink)
        p = jnp.exp(s - m)
        den = jnp.sum(p, axis=1, keepdims=True) + jnp.exp(sink - m)
        probs = (p * (1.0 / den)).astype(BF16)
        lse = m + jnp.log(den)
        lane = _iota((BLK, 128), 1)
        lse_acc = jnp.zeros((BLK, 128), F32)
        for jj in range(4):
            acc = (_dot(probs[2 * jj * BLK:(2 * jj + 1) * BLK], vvar[(jj // 2, 0)])
                   + _dot(probs[(2 * jj + 1) * BLK:(2 * jj + 2) * BLK], vvar[(jj // 2, 1)]))
            o_ref[:, 128 * jj:128 * (jj + 1)] = acc.astype(BF16)
        for h in range(NQ):
            lse_acc = jnp.where(lane == h, lse[h * BLK:(h + 1) * BLK], lse_acc)
        lse_ref[...] = lse_acc

    prev = lambda n: jnp.maximum(n - 1, 0)
    return pl.pallas_call(
        body, name="attn_fwd", grid=(nb,),
        in_specs=[pl.BlockSpec(memory_space=pltpu.SMEM),
                  pl.BlockSpec((BLK, 512), lambda n: (n, 0)),
                  pl.BlockSpec((BLK, 128), lambda n: (n, 4)),
                  pl.BlockSpec((BLK, 128), lambda n: (prev(n), 4)),
                  pl.BlockSpec((BLK, 128), lambda n: (n, 5)),
                  pl.BlockSpec((BLK, 128), lambda n: (prev(n), 5))],
        out_specs=[pl.BlockSpec((BLK, 512), lambda n: (n, 0)), pl.BlockSpec((BLK, 128), lambda n: (n, 0))],
        out_shape=[jax.ShapeDtypeStruct((T, 512), BF16), jax.ShapeDtypeStruct((T, 128), F32)],
        compiler_params=_cp("parallel"),
    )(sinks, qkv, qkv, qkv, qkv, qkv)


def _attn_bwd(qkv, sinks, lse, dmix, cos, sin_s):
    T = qkv.shape[0]
    nb = T // BLK

    def body(sink_ref, q_ref, kc_ref, kp_ref, vc_ref, vp_ref, lse_ref, do_ref, cq_ref, sq_ref, ck_ref, sk_ref,
             out_ref, ds_ref, dq_car, dk_car, dv_car):
        n = pl.program_id(0)
        lane = _iota((BLK, 128), 1)

        @pl.when(n == 0)
        def _():
            ds_ref[...] = jnp.zeros_like(ds_ref)
            dq_car[...] = jnp.zeros_like(dq_car)
            dk_car[...] = jnp.zeros_like(dk_car)
            dv_car[...] = jnp.zeros_like(dv_car)

        @pl.when(n < nb)
        def _():
            kcat = jnp.concatenate([kp_ref[...], kc_ref[...]], axis=0)
            vcat = jnp.concatenate([vp_ref[...], vc_ref[...]], axis=0)
            kvar = _kv_variants(kcat.astype(F32))
            lse_v = lse_ref[...]
            q_all = jnp.concatenate(
                [v for j in range(4) for v in _head_variants(q_ref[:, 128 * j:128 * (j + 1)].astype(F32), j)], axis=0)
            do_all = jnp.concatenate(
                [v for j in range(4) for v in _head_variants(do_ref[:, 128 * j:128 * (j + 1)], j)], axis=0)
            rows = NQ * BLK
            head = _iota((rows, 1), 0) // BLK
            sink = jnp.zeros((rows, 1), F32)
            for h in range(NQ):
                sink = jnp.where(head == h, sink_ref[0, h], sink)
            lse_col = jnp.concatenate(
                [jnp.sum(jnp.where(lane == h, lse_v, 0.0), axis=1, keepdims=True) for h in range(NQ)], axis=0)
            i = _iota((rows, 2 * BLK), 0) & (BLK - 1)
            jc = _iota((rows, 2 * BLK), 1)
            valid = (jc > i) & (jc <= i + BLK) & ((n > 0) | (jc >= BLK))
            s = jnp.where(valid, _dot_nt(q_all, kcat) * 0.125, NEG)
            p = jnp.exp(s - lse_col)
            dp = _dot_nt(do_all, vcat)
            delta = jnp.sum(p * dp, axis=1, keepdims=True)
            dsc = (p * (dp - delta) * 0.125).astype(BF16)
            dkc = _dot_tn(dsc, q_all)
            dvc = _dot_tn(p.astype(BF16), do_all)
            out_ref[:, 0:512] = dq_car[...]
            for jj in range(4):
                dq_acc = (_dot(dsc[2 * jj * BLK:(2 * jj + 1) * BLK], kvar[(jj // 2, 0)])
                          + _dot(dsc[(2 * jj + 1) * BLK:(2 * jj + 2) * BLK], kvar[(jj // 2, 1)]))
                dq_car[:, 128 * jj:128 * (jj + 1)] = _rope_bwd(dq_acc, cq_ref[...], sq_ref[...]).astype(BF16)
            dsink = jnp.exp(sink - lse_col) * delta
            dsk = jnp.zeros((1, 128), F32)
            for h in range(NQ):
                dsk = dsk + jnp.where(lane[0:1] == h, -jnp.sum(dsink[h * BLK:(h + 1) * BLK]), 0.0)
            ds_ref[...] += dsk
            out_ref[:, 512:640] = _rope_bwd(dk_car[...] + dkc[:BLK], ck_ref[...], sk_ref[...]).astype(BF16)
            out_ref[:, 640:768] = (dv_car[...] + dvc[:BLK]).astype(BF16)
            dk_car[...] = dkc[BLK:]
            dv_car[...] = dvc[BLK:]

        @pl.when(n == nb)
        def _():
            out_ref[:, 0:512] = dq_car[...]
            out_ref[:, 512:640] = _rope_bwd(dk_car[...], ck_ref[...], sk_ref[...]).astype(BF16)
            out_ref[:, 640:768] = dv_car[...].astype(BF16)

    cur = lambda n: jnp.minimum(n, nb - 1)
    prev = lambda n: jnp.maximum(cur(n) - 1, 0)
    outb = lambda n: jnp.maximum(n - 1, 0)
    return pl.pallas_call(
        body, name="attn_bwd", grid=(nb + 1,),
        in_specs=[pl.BlockSpec(memory_space=pltpu.SMEM),
                  pl.BlockSpec((BLK, 512), lambda n: (cur(n), 0)),
                  pl.BlockSpec((BLK, 128), lambda n: (cur(n), 4)),
                  pl.BlockSpec((BLK, 128), lambda n: (prev(n), 4)),
                  pl.BlockSpec((BLK, 128), lambda n: (cur(n), 5)),
                  pl.BlockSpec((BLK, 128), lambda n: (prev(n), 5)),
                  pl.BlockSpec((BLK, 128), lambda n: (cur(n), 0)),
                  pl.BlockSpec((BLK, 512), lambda n: (cur(n), 0)),
                  pl.BlockSpec((BLK, 128), lambda n: (cur(n), 0)),
                  pl.BlockSpec((BLK, 128), lambda n: (cur(n), 0)),
                  pl.BlockSpec((BLK, 128), lambda n: (outb(n), 0)),
                  pl.BlockSpec((BLK, 128), lambda n: (outb(n), 0))],
        out_specs=[pl.BlockSpec((BLK, 768), lambda n: (outb(n), 0)), pl.BlockSpec((1, 128), lambda n: (0, 0))],
        out_shape=[jax.ShapeDtypeStruct((T, 768), BF16), jax.ShapeDtypeStruct((1, 128), F32)],
        scratch_shapes=[pltpu.VMEM((BLK, 512), BF16), pltpu.VMEM((BLK, 128), F32), pltpu.VMEM((BLK, 128), F32)],
        compiler_params=_cp("arbitrary"),
    )(sinks, qkv, qkv, qkv, qkv, qkv, lse, dmix, cos, sin_s, cos, sin_s)


def _ssd_mats():
    e = jnp.arange(SW)[None, :] // HD == jnp.arange(128)[:, None]
    tri = jnp.arange(BLK)[None, :] <= jnp.arange(BLK)[:, None]
    return (jnp.tile(e, (3, 1)).astype(BF16), jnp.tile(e.T, (2, 1)).astype(BF16),
            jnp.tile(tri, (1, 3)).astype(BF16), jnp.tile(tri.T, (1, 3)).astype(BF16))


def _pieces(x, n, axis):
    out, r = [], x
    for i in range(n):
        p = r.astype(BF16)
        out.append(p)
        if i + 1 < n:
            r = r - p.astype(F32)
    return jnp.concatenate(out, axis=axis)


def _expand(x, e3):
    return _dot(_pieces(x, 3, 1), e3)


def _head_sums(x, et2):
    return _dot(_pieces(x, 2, 1), et2)


def _run_sum(tri3, x):
    return _dot(tri3, _pieces(x, 3, 0))


def _shift_down(u, tail, j):
    rolled = pltpu.roll(u, j, 0)
    first = jnp.where(_iota(tail.shape, 0) < j, pltpu.roll(tail, j, 0), rolled[0:8])
    return jnp.concatenate([first, rolled[8:]], axis=0)


def _shift_up(d, head, j):
    rolled = pltpu.roll(d, BLK - j, 0)
    last = jnp.where(_iota(head.shape, 0) >= 8 - j, pltpu.roll(head, 8 - j, 0), rolled[BLK - 8:])
    return jnp.concatenate([rolled[:BLK - 8], last], axis=0)


def _ssd_parts(dtr, dtb, alog, e3, tril3):
    xx = dtr + dtb
    dt = jnp.maximum(xx, 0.0) + jnp.log(1.0 + jnp.exp(-jnp.abs(xx)))
    a_neg = -jnp.exp(alog)
    tril = _iota((BLK, BLK), 1) <= _iota((BLK, BLK), 0)
    cs = _run_sum(tril3, dt * a_neg)
    csx = _expand(cs, e3)
    last = csx[BLK - 1:BLK, :]
    return dict(xx=xx, dt=dt, a_neg=a_neg, tril=tril, cs=cs, cs_t=cs.T,
                ecsx=jnp.exp(csx), dtex=jnp.exp(last - csx), cdx=jnp.exp(last), dtx=_expand(dt, e3))


def _decay(parts, h):
    seg = parts["cs"][:, h:h + 1] - parts["cs_t"][h:h + 1, :]
    return jnp.exp(jnp.where(parts["tril"], seg, NEG))


def _group_cols(a, g):
    return a[:, 256 * g:256 * (g + 1)]


def _ssd_fwd(xbc, z, dtr, conv_w, conv_b, dtb, alog, dskx, ssm_w, mats):
    T = xbc.shape[0]
    nc = T // BLK

    def body(u_ref, tail_ref, z_ref, dtr_ref, cw_ref, cb_ref, dtb_ref, al_ref, dk_ref, sw_ref, e3_ref, tril3_ref,
             yn_ref, yp_ref, st_ref, co_ref, s_scr):
        n = pl.program_id(0)

        @pl.when(n == 0)
        def _():
            s_scr[...] = jnp.zeros_like(s_scr)

        u = u_ref[...]
        tail = jnp.where(n > 0, tail_ref[...], 0.0)
        co = cb_ref[...] + cw_ref[3:4, :] * u
        for j in range(1, CONVK):
            co = co + cw_ref[3 - j:4 - j, :] * _shift_down(u, tail, j)
        co_ref[...] = co
        xc = co * _sigmoid(co)
        pt = _ssd_parts(dtr_ref[...], dtb_ref[...], al_ref[...], e3_ref[...], tril3_ref[...])
        xs = xc[:, :SW]
        bm = [xc[:, 512:640].astype(BF16), xc[:, 640:768].astype(BF16)]
        cm = [xc[:, 768:896].astype(BF16), xc[:, 896:1024].astype(BF16)]
        s_in = s_scr[...]
        st_ref[0] = s_in
        xdt = xs * pt["dtx"]
        xde = (xdt * pt["dtex"]).astype(BF16)
        lane = _iota((BLK, 128), 1)
        lo = lane < 64
        ys, s_new = [], []
        for g in range(2):
            cb = _dot_nt(cm[g], bm[g])
            yoff = _dot(cm[g], _group_cols(s_in, g).astype(BF16))
            s_new.append(_dot_tn(bm[g], _group_cols(xde, g)))
            for jj in range(2):
                j = 2 * g + jj
                chunk = xdt[:, 128 * j:128 * (j + 1)]
                g_ev = (cb * _decay(pt, 2 * j)).astype(BF16)
                g_od = (cb * _decay(pt, 2 * j + 1)).astype(BF16)
                yd = _dot(g_ev, jnp.where(lo, chunk, 0.0).astype(BF16)) + _dot(g_od, jnp.where(lo, 0.0, chunk).astype(BF16))
                ys.append(yd + yoff[:, 128 * jj:128 * (jj + 1)] * pt["ecsx"][:, 128 * j:128 * (j + 1)])
        y = jnp.concatenate(ys, axis=1) + xs * dk_ref[...]
        s_scr[...] = s_in * pt["cdx"] + jnp.concatenate(s_new, axis=1)
        yp_ref[...] = y
        zv = z_ref[...]
        yz = y * (zv * _sigmoid(zv))
        outs = []
        for g in range(2):
            yg = _group_cols(yz, g)
            outs.append(yg * lax.rsqrt(jnp.mean(yg * yg, axis=-1, keepdims=True) + EPS))
        yn_ref[...] = (jnp.concatenate(outs, axis=1) * sw_ref[...]).astype(BF16)

    e3, _, tril3, _ = mats
    tail8 = lambda n: jnp.maximum(n * (BLK // 8) - 1, 0)
    full = lambda a: pl.BlockSpec(a.shape, lambda n: (0,) * a.ndim)
    return pl.pallas_call(
        body, name="ssd_fwd", grid=(nc,),
        in_specs=[pl.BlockSpec((BLK, CONVC), lambda n: (n, 0)), pl.BlockSpec((8, CONVC), lambda n: (tail8(n), 0)),
                  pl.BlockSpec((BLK, SW), lambda n: (n, 0)), pl.BlockSpec((BLK, 128), lambda n: (n, 0)),
                  full(conv_w), full(conv_b), full(dtb), full(alog), full(dskx), full(ssm_w), full(e3), full(tril3)],
        out_specs=[pl.BlockSpec((BLK, SW), lambda n: (n, 0)), pl.BlockSpec((BLK, SW), lambda n: (n, 0)),
                   pl.BlockSpec((1, NST, SW), lambda n: (n, 0, 0)), pl.BlockSpec((BLK, CONVC), lambda n: (n, 0))],
        out_shape=[jax.ShapeDtypeStruct((T, SW), BF16), jax.ShapeDtypeStruct((T, SW), F32),
                   jax.ShapeDtypeStruct((nc, NST, SW), F32), jax.ShapeDtypeStruct((T, CONVC), F32)],
        scratch_shapes=[pltpu.VMEM((NST, SW), F32)],
        compiler_params=_cp("arbitrary"),
    )(xbc, xbc, z, dtr, conv_w, conv_b, dtb, alog, dskx, ssm_w, e3, tril3)


def _ssd_bwd(xbc, co_all, z, dtr, ypre, states, dmix, conv_w, dtb, alog, dskx, ssm_w, mats):
    T = xbc.shape[0]
    nc = T // BLK

    def body(u_ref, co_ref, z_ref, dtr_ref, yp_ref, st_ref, dyn_ref, cw_ref, dtb_ref, al_ref, dk_ref, sw_ref,
             e3_ref, et2_ref, tril3_ref, triu3_ref,
             out_ref, dcw_ref, dcb_ref, dsw_ref, dsk_ref, ddtb_ref, dav_ref, ds_scr, dco_scr, dskx_scr):
        i = pl.program_id(0)

        @pl.when(i == 0)
        def _():
            for r in (dcw_ref, dcb_ref, dsw_ref, dsk_ref, ddtb_ref, dav_ref, ds_scr, dco_scr, dskx_scr):
                r[...] = jnp.zeros_like(r)

        co = co_ref[...]
        sg = _sigmoid(co)
        xc = co * sg
        pt = _ssd_parts(dtr_ref[...], dtb_ref[...], al_ref[...], e3_ref[...], tril3_ref[...])
        dtx, ecsx, dtex, cdx = pt["dtx"], pt["ecsx"], pt["dtex"], pt["cdx"]
        xs = xc[:, :SW]
        bm = [xc[:, 512:640].astype(BF16), xc[:, 640:768].astype(BF16)]
        cm = [xc[:, 768:896].astype(BF16), xc[:, 896:1024].astype(BF16)]
        s_in = st_ref[0]
        ds_out = ds_scr[...]
        e_t = et2_ref[...]

        zv = z_ref[...]
        sz = _sigmoid(zv)
        silu_z = zv * sz
        ypre = yp_ref[...]
        yz = ypre * silu_z
        dyn = dyn_ref[...]
        sw = sw_ref[...]
        dyz, yns = [], []
        for g in range(2):
            yg = _group_cols(yz, g)
            r = lax.rsqrt(jnp.mean(yg * yg, axis=-1, keepdims=True) + EPS)
            yn = yg * r
            dg = _group_cols(dyn, g) * _group_cols(sw, g)
            dyz.append(r * (dg - yn * jnp.mean(dg * yn, axis=-1, keepdims=True)))
            yns.append(yn)
        dyz = jnp.concatenate(dyz, axis=1)
        dsw_ref[...] += jnp.sum(dyn * jnp.concatenate(yns, axis=1), axis=0, keepdims=True)
        dy = dyz * silu_z
        dz = dyz * ypre * (sz * (1.0 + zv * (1.0 - sz)))

        xdt = xs * dtx
        xdt_b = xdt.astype(BF16)
        edy = (ecsx * dy).astype(BF16)
        xde = (xdt * dtex).astype(BF16)
        lane = _iota((BLK, 128), 1)
        lo = lane < 64
        row8 = _iota((8, 128), 0)
        dcs = jnp.zeros((BLK, 128), F32)
        col_rows = jnp.zeros((8, 128), F32)
        dxdt, bds, yoff, dbs, dcs_g, ds_new = [], [], [], [], [], []
        for g in range(2):
            s_g = _group_cols(s_in, g).astype(BF16)
            dso_g = _group_cols(ds_out, g).astype(BF16)
            cb = _dot_nt(cm[g], bm[g])
            bds.append(_dot(bm[g], dso_g))
            yoff.append(_dot(cm[g], s_g))
            dcb_g = jnp.zeros((BLK, BLK), F32)
            for jj in range(2):
                j = 2 * g + jj
                dy_c = dy[:, 128 * j:128 * (j + 1)]
                xdt_c = xdt_b[:, 128 * j:128 * (j + 1)]
                acc = jnp.zeros((BLK, 128), F32)
                for par in range(2):
                    h = 2 * j + par
                    lm = _decay(pt, h)
                    gm = cb * lm
                    dy_m = (jnp.where(lo, dy_c, 0.0) if par == 0 else jnp.where(lo, 0.0, dy_c)).astype(BF16)
                    dg_h = _dot_nt(dy_m, xdt_c)
                    w_h = dg_h * gm
                    dcs = dcs + jnp.where(lane == h, jnp.sum(w_h, axis=1, keepdims=True), 0.0)
                    col_rows = col_rows + jnp.where(row8 == h, jnp.sum(w_h, axis=0, keepdims=True), 0.0)
                    dcb_g = dcb_g + dg_h * lm
                    acc = acc + _dot_tn(gm.astype(BF16), dy_m)
                dxdt.append(acc)
            dcb_b = dcb_g.astype(BF16)
            dcs_g.append(_dot(dcb_b, bm[g]) + _dot_nt(_group_cols(edy, g), s_g))
            dbs.append(_dot_tn(dcb_b, cm[g]) + _dot_nt(_group_cols(xde, g), dso_g))
            ds_new.append(_dot_tn(cm[g], _group_cols(edy, g)))
        bds = jnp.concatenate(bds, axis=1)
        yoff = jnp.concatenate(yoff, axis=1) * ecsx
        dxdt = jnp.concatenate(dxdt, axis=1) + dtex * bds
        ds_scr[...] = cdx * ds_out + jnp.concatenate(ds_new, axis=1)

        t_m = _head_sums(dtex * xdt * bds, e_t)
        colsum_t = jnp.concatenate([col_rows, jnp.zeros((BLK - 8, 128), F32)], axis=0).T
        cd = jnp.exp(pt["cs"][BLK - 1:BLK, :])
        sds = jnp.sum(s_in * ds_out, axis=0, keepdims=True)
        last_row = jnp.sum(t_m, axis=0, keepdims=True) + cd * _head_sums(jnp.broadcast_to(sds, (8, SW)), e_t)[0:1]
        dcs = dcs - colsum_t + _head_sums(dy * yoff, e_t) - t_m
        dcs = dcs + jnp.where(_iota((BLK, 128), 0) == BLK - 1, last_row, 0.0)
        da = _run_sum(triu3_ref[...], dcs)
        dt = pt["dt"]
        ddt = da * pt["a_neg"] + _head_sums(dxdt * xs, e_t)
        dav_ref[...] += jnp.sum(da * dt, axis=0, keepdims=True)
        ddtr = ddt * _sigmoid(pt["xx"])
        ddtb_ref[...] += jnp.sum(ddtr, axis=0, keepdims=True)
        dxs = dxdt * dtx + dy * dk_ref[...]
        dskx_scr[...] += jnp.sum(dy * xs, axis=0, keepdims=True)
        dxc = jnp.concatenate([dxs, dbs[0], dbs[1], dcs_g[0], dcs_g[1]], axis=1)
        dco = dxc * (sg * (1.0 + co * (1.0 - sg)))

        dcb_ref[...] += jnp.sum(dco, axis=0, keepdims=True)
        u = u_ref[...]
        head = dco_scr[...]
        du = jnp.zeros_like(dco)
        for j in range(CONVK):
            up_j = dco if j == 0 else _shift_up(dco, head, j)
            dcw_ref[3 - j:4 - j, :] += jnp.sum(up_j * u, axis=0, keepdims=True)
            du = du + cw_ref[3 - j:4 - j, :] * up_j
        dco_scr[...] = dco[0:8]
        out_ref[:, 0:512] = dz.astype(BF16)
        out_ref[:, 512:1536] = du.astype(BF16)
        out_ref[:, 1536:1664] = ddtr.astype(BF16)

        @pl.when(i == nc - 1)
        def _():
            dsk_ref[...] = _head_sums(jnp.broadcast_to(dskx_scr[...], (8, SW)), e_t)[0:1]

    e3, et2, tril3, triu3 = mats
    rev = lambda i: nc - 1 - i
    full = lambda a: pl.BlockSpec(a.shape, lambda i: (0,) * a.ndim)
    acc = lambda r, c: pl.BlockSpec((r, c), lambda i: (0, 0))
    return pl.pallas_call(
        body, name="ssd_bwd", grid=(nc,),
        in_specs=[pl.BlockSpec((BLK, CONVC), lambda i: (rev(i), 0)), pl.BlockSpec((BLK, CONVC), lambda i: (rev(i), 0)),
                  pl.BlockSpec((BLK, SW), lambda i: (rev(i), 0)), pl.BlockSpec((BLK, 128), lambda i: (rev(i), 0)),
                  pl.BlockSpec((BLK, SW), lambda i: (rev(i), 0)), pl.BlockSpec((1, NST, SW), lambda i: (rev(i), 0, 0)),
                  pl.BlockSpec((BLK, SW), lambda i: (rev(i), 1)),
                  full(conv_w), full(dtb), full(alog), full(dskx), full(ssm_w),
                  full(e3), full(et2), full(tril3), full(triu3)],
        out_specs=[pl.BlockSpec((BLK, 1664), lambda i: (rev(i), 0)),
                   acc(CONVK, CONVC), acc(1, CONVC), acc(1, SW), acc(1, 128), acc(1, 128), acc(1, 128)],
        out_shape=[jax.ShapeDtypeStruct((T, 1664), BF16),
                   jax.ShapeDtypeStruct((CONVK, CONVC), F32), jax.ShapeDtypeStruct((1, CONVC), F32),
                   jax.ShapeDtypeStruct((1, SW), F32), jax.ShapeDtypeStruct((1, 128), F32),
                   jax.ShapeDtypeStruct((1, 128), F32), jax.ShapeDtypeStruct((1, 128), F32)],
        scratch_shapes=[pltpu.VMEM((NST, SW), F32), pltpu.VMEM((8, CONVC), F32), pltpu.VMEM((1, SW), F32)],
        compiler_params=_cp("arbitrary"),
    )(xbc, co_all, z, dtr, ypre, states, dmix, conv_w, dtb, alog, dskx, ssm_w, e3, et2, tril3, triu3)


def _mix_ffn(x, attn, ynorm, tgt, mod6, norm2_w, final_w, w_out, w_gu, w_dn, tm):
    T = x.shape[0]
    nt = T // tm

    def body(x_ref, a_ref, y_ref, t_ref, mod_ref, n2_ref, fw_ref, wo_hbm, wgu_hbm, wdn_hbm,
             sq_ref, dmix_ref, dx1_ref, h2_ref, act_ref, df_ref, dgu_ref, do_ref, sm_ref,
             wo, wgu, wdn, sems):
        i = pl.program_id(0)

        @pl.when(i == 0)
        def _():
            cps = [pltpu.make_async_copy(s, d, sems.at[k]) for k, (s, d) in
                   enumerate(((wo_hbm, wo), (wgu_hbm, wgu), (wdn_hbm, wdn)))]
            for c in cps:
                c.start()
            for c in cps:
                c.wait()
            sq_ref[...] = jnp.zeros_like(sq_ref)
            sm_ref[...] = jnp.zeros_like(sm_ref)

        gate1, shift2, scale2, gate2 = mod_ref[2:3, :], mod_ref[3:4, :], mod_ref[4:5, :], mod_ref[5:6, :]
        n2w, fw = n2_ref[...], fw_ref[...]
        o = _dot(a_ref[...], wo[0:AW, :]) + _dot(y_ref[...], wo[AW:D, :])
        x1 = x_ref[...] + gate1 * o
        r2 = lax.rsqrt(jnp.mean(x1 * x1, axis=-1, keepdims=True) + EPS)
        xh2 = x1 * r2
        n2 = xh2 * n2w
        h2b = (n2 * (1.0 + scale2) + shift2).astype(BF16)
        h2_ref[...] = h2b
        f = jnp.zeros((tm, D), F32)
        saved = []
        for p in range(2):
            gp = _dot(h2b, wgu[p])
            upj = _dot(h2b, wgu[p + 2])
            sg = _sigmoid(gp)
            sl = gp * sg
            actb = (sl * upj).astype(BF16)
            act_ref[p] = actb
            f = f + _dot(actb, wdn[GU_SH * p:GU_SH * (p + 1), :])
            saved.append((gp, upj, sg, sl))
        x2 = x1 + gate2 * f
        r3 = lax.rsqrt(jnp.mean(x2 * x2, axis=-1, keepdims=True) + EPS)
        xh3 = x2 * r3
        err = xh3 * fw - t_ref[...]
        sq_ref[...] += jnp.sum(err * err, axis=0, keepdims=True)
        dy = err * (1.0 / D)
        dfw = jnp.sum(dy * xh3, axis=0, keepdims=True)
        dxh3 = dy * fw
        dx2 = r3 * (dxh3 - xh3 * jnp.mean(dxh3 * xh3, axis=-1, keepdims=True))
        dgate2 = jnp.sum(dx2 * f, axis=0, keepdims=True)
        dfb = (dx2 * gate2).astype(BF16)
        df_ref[...] = dfb
        dh2 = jnp.zeros((tm, D), F32)
        for p in range(2):
            gp, upj, sg, sl = saved[p]
            dact = _dot_nt(dfb, wdn[GU_SH * p:GU_SH * (p + 1), :])
            dg = (dact * upj * (sg * (1.0 + gp * (1.0 - sg)))).astype(BF16)
            du = (dact * sl).astype(BF16)
            dgu_ref[p] = dg
            dgu_ref[p + 2] = du
            dh2 = dh2 + _dot_nt(dg, wgu[p]) + _dot_nt(du, wgu[p + 2])
        dshift2 = jnp.sum(dh2, axis=0, keepdims=True)
        dscale2 = jnp.sum(dh2 * n2, axis=0, keepdims=True)
        dn2 = dh2 * (1.0 + scale2)
        dn2w = jnp.sum(dn2 * xh2, axis=0, keepdims=True)
        dxh2 = dn2 * n2w
        dx1 = dx2 + r2 * (dxh2 - xh2 * jnp.mean(dxh2 * xh2, axis=-1, keepdims=True))
        dx1_ref[...] = dx1
        dgate1 = jnp.sum(dx1 * o, axis=0, keepdims=True)
        dob = (dx1 * gate1).astype(BF16)
        do_ref[...] = dob
        dmix_ref[...] = _dot_nt(dob, wo[...])
        sm_ref[...] += jnp.concatenate(
            [dfw, dn2w, dshift2, dscale2, dgate2, dgate1, jnp.zeros((2, D), F32)], axis=0)

    row = lambda w: pl.BlockSpec((tm, w), lambda i: (i, 0))
    full = lambda a: pl.BlockSpec(a.shape, lambda i: (0,) * a.ndim)
    anyspec = pl.BlockSpec(memory_space=pl.ANY)
    return pl.pallas_call(
        body, name="mix_ffn", grid=(nt,),
        in_specs=[row(D), row(AW), row(SW), row(D), full(mod6), full(norm2_w), full(final_w), anyspec, anyspec, anyspec],
        out_specs=[pl.BlockSpec((1, D), lambda i: (0, 0)), row(D), row(D), row(D),
                   pl.BlockSpec((2, tm, GU_SH), lambda i: (0, i, 0)), row(D),
                   pl.BlockSpec((4, tm, GU_SH), lambda i: (0, i, 0)), row(D),
                   pl.BlockSpec((8, D), lambda i: (0, 0))],
        out_shape=[jax.ShapeDtypeStruct((1, D), F32), jax.ShapeDtypeStruct((T, D), F32), jax.ShapeDtypeStruct((T, D), F32),
                   jax.ShapeDtypeStruct((T, D), BF16), jax.ShapeDtypeStruct((2, T, GU_SH), BF16),
                   jax.ShapeDtypeStruct((T, D), BF16), jax.ShapeDtypeStruct((4, T, GU_SH), BF16),
                   jax.ShapeDtypeStruct((T, D), BF16), jax.ShapeDtypeStruct((8, D), F32)],
        scratch_shapes=[pltpu.VMEM((D, D), BF16), pltpu.VMEM((4, D, GU_SH), BF16), pltpu.VMEM((DFF, D), BF16),
                        pltpu.SemaphoreType.DMA((3,))],
        compiler_params=_cp("arbitrary"),
    )(x, attn, ynorm, tgt, mod6, norm2_w, final_w, w_out, w_gu, w_dn)


def _in_proj_bwd(x, dx1, dqkv, dzxd, mod6, norm1_w, w_pad, tm):
    T = x.shape[0]

    def body(x_ref, dx1_ref, dq_ref, dz_ref, mod_ref, nw_ref, w_hbm, gx_ref, sm_ref, w_vmem, sem):
        _load_resident(w_hbm, w_vmem, sem)

        @pl.when(pl.program_id(0) == 0)
        def _():
            sm_ref[...] = jnp.zeros_like(sm_ref)

        dh = _dot_nt(dq_ref[...], w_vmem[:, 0:768]) + _dot_nt(dz_ref[...], w_vmem[:, 768:IN_PAD])
        xv = x_ref[...]
        nw = nw_ref[...]
        scale1 = mod_ref[1:2, :]
        r = lax.rsqrt(jnp.mean(xv * xv, axis=-1, keepdims=True) + EPS)
        xh = xv * r
        n1 = xh * nw
        dshift = jnp.sum(dh, axis=0, keepdims=True)
        dscale = jnp.sum(dh * n1, axis=0, keepdims=True)
        dn = dh * (1.0 + scale1)
        dnw = jnp.sum(dn * xh, axis=0, keepdims=True)
        dxh = dn * nw
        gx_ref[...] = dx1_ref[...] + r * (dxh - xh * jnp.mean(dxh * xh, axis=-1, keepdims=True))
        sm_ref[...] += jnp.concatenate([dnw, dshift, dscale, jnp.zeros((5, D), F32)], axis=0)

    row = lambda w: pl.BlockSpec((tm, w), lambda i: (i, 0))
    full = lambda a: pl.BlockSpec(a.shape, lambda i: (0,) * a.ndim)
    return pl.pallas_call(
        body, name="in_proj_bwd", grid=(T // tm,),
        in_specs=[row(D), row(D), row(768), row(1664), full(mod6), full(norm1_w), pl.BlockSpec(memory_space=pl.ANY)],
        out_specs=[row(D), pl.BlockSpec((8, D), lambda i: (0, 0))],
        out_shape=[jax.ShapeDtypeStruct((T, D), F32), jax.ShapeDtypeStruct((8, D), F32)],
        scratch_shapes=[pltpu.VMEM((D, IN_PAD), BF16), pltpu.SemaphoreType.DMA],
        compiler_params=_cp("arbitrary"),
    )(x, dx1, dqkv, dzxd, mod6, norm1_w, w_pad)


def _tn_matmul(a3, b3, tt, name, dep):
    ja, T, K = a3.shape
    jb, _, N = b3.shape
    J = max(ja, jb)

    def body(a_ref, b_ref, dep_ref, o_ref):
        t = pl.program_id(1)
        prod = _dot_tn(a_ref[0], b_ref[0])

        @pl.when(t == 0)
        def _():
            o_ref[0] = prod

        @pl.when(t > 0)
        def _():
            o_ref[0] += prod

    return pl.pallas_call(
        body, name=name, grid=(J, T // tt),
        in_specs=[pl.BlockSpec((1, tt, K), lambda j, t: (j if ja > 1 else 0, t, 0)),
                  pl.BlockSpec((1, tt, N), lambda j, t: (j if jb > 1 else 0, t, 0)),
                  pl.BlockSpec((8, 128), lambda j, t: (0, 0))],
        out_specs=pl.BlockSpec((1, K, N), lambda j, t: (j, 0, 0)),
        out_shape=jax.ShapeDtypeStruct((J, K, N), F32),
        compiler_params=_cp("parallel", "arbitrary"),
    )(a3, b3, dep)


def _adam_math(w, g, m, v):
    m = B1 * m + (1.0 - B1) * g
    v = B2 * v + (1.0 - B2) * (g * g)
    m_hat = m / (1.0 - B1 ** STEP)
    v_hat = v / (1.0 - B2 ** STEP)
    delta = -LR * (m_hat / (jnp.sqrt(v_hat) + AEPS) + WD * w)
    return delta, m, v


def _adam_2d(w, g, m, v, rb, name):
    R, C = w.shape

    def body(w_ref, g_ref, m_ref, v_ref, go_ref, d_ref, mo_ref, vo_ref):
        g = g_ref[...]
        d, mn, vn = _adam_math(w_ref[...], g, m_ref[...], v_ref[...])
        go_ref[...] = g
        d_ref[...] = d
        mo_ref[...] = mn
        vo_ref[...] = vn

    spec = pl.BlockSpec((rb, C), lambda i: (i, 0))
    return pl.pallas_call(
        body, name=name, grid=(R // rb,), in_specs=[spec] * 4, out_specs=[spec] * 4,
        out_shape=[jax.ShapeDtypeStruct((R, C), F32)] * 4, compiler_params=_cp("parallel"),
    )(w, g, m, v)


def _adam_w_ada(sc_all, dmod_s, w, m, v, rb):
    R, C = w.shape

    def body(sc_ref, dm_ref, w_ref, m_ref, v_ref, g_ref, d_ref, mo_ref, vo_ref):
        g = lax.dot_general(sc_ref[...], dm_ref[...], (((0,), (0,)), ((), ())), precision=HI, preferred_element_type=F32)
        d, mn, vn = _adam_math(w_ref[...], g, m_ref[...], v_ref[...])
        g_ref[...] = g
        d_ref[...] = d
        mo_ref[...] = mn
        vo_ref[...] = vn

    spec = pl.BlockSpec((rb, C), lambda i: (i, 0))
    return pl.pallas_call(
        body, name="adam_w_ada", grid=(R // rb,),
        in_specs=[pl.BlockSpec((8, rb), lambda i: (0, i)), pl.BlockSpec((8, C), lambda i: (0, 0)), spec, spec, spec],
        out_specs=[spec] * 4, out_shape=[jax.ShapeDtypeStruct((R, C), F32)] * 4, compiler_params=_cp("parallel"),
    )(sc_all, dmod_s, w, m, v)


def _adam_small(grads, ws, ms, vs):
    k = len(ws)

    def body(*refs):
        g, w, m, v = refs[0:k], refs[k:2 * k], refs[2 * k:3 * k], refs[3 * k:4 * k]
        g_o, d_o, m_o, v_o = refs[4 * k:5 * k], refs[5 * k:6 * k], refs[6 * k:7 * k], refs[7 * k:8 * k]
        for i in range(k):
            gi = g[i][...]
            d, mn, vn = _adam_math(w[i][...], gi, m[i][...], v[i][...])
            g_o[i][...] = gi
            d_o[i][...] = d
            m_o[i][...] = mn
            v_o[i][...] = vn

    shapes = [jax.ShapeDtypeStruct(w.shape, F32) for w in ws]
    vm = pl.BlockSpec(memory_space=pltpu.VMEM)
    outs = pl.pallas_call(
        body, name="adam_small", in_specs=[vm] * (4 * k), out_specs=[vm] * (4 * k), out_shape=shapes * 4,
    )(*grads, *ws, *ms, *vs)
    return outs[0:k], outs[k:2 * k], outs[2 * k:3 * k], outs[3 * k:4 * k]


def _pos():
    return lax.axis_index("x"), lax.axis_index("y"), lax.axis_index("c")


def _flip(v, bit):
    return 1 - v if bit else v


def _peer(k):
    x, y, c = _pos()
    return (_flip(x, (k >> 2) & 1), _flip(y, (k >> 1) & 1), _flip(c, k & 1))


def _logical(p):
    return 4 * p[0] + 2 * p[1] + p[2]


def _gather8(src_ref, dst_ref, send_sems, recv_sems):
    me = _logical(_pos())
    dst_ref[pl.ds(me, 1)] = src_ref[...][None]
    copies = []
    for k in range(1, 8):
        cp = pltpu.make_async_remote_copy(src_ref, dst_ref.at[me], send_sems.at[k - 1], recv_sems.at[k - 1],
                                          device_id=_peer(k), device_id_type=MESH)
        cp.start()
        copies.append(cp)
    for k in range(1, 8):
        pltpu.make_async_remote_copy(src_ref, dst_ref.at[_logical(_peer(k))], send_sems.at[k - 1], recv_sems.at[k - 1],
                                     device_id=_peer(k), device_id_type=MESH).wait_recv()
    for cp in copies:
        cp.wait_send()


def _rows_select(ref3, width):
    row = _iota((8, width), 0)
    out = jnp.zeros((8, width), F32)
    for i in range(8):
        out = jnp.where(row == i, ref3[i][:, 0:width], out)
    return out


def _mod_exchange(payload, w_ada_s, b_ada4):
    n_sh = w_ada_s.shape[1]

    def body(pay_ref, w_ref, b_ref, gat_ref, mod_ref, token, p3, sa, ra, sb, rb):
        token[...] = jnp.zeros_like(token)
        x, y, c = _pos()
        me = _logical((x, y, c))
        my_s = 2 * x + y
        _gather8(pay_ref, gat_ref, sa, ra)
        cmat = _rows_select(gat_ref, D)
        prod = _dot_hi(cmat * _sigmoid(cmat), w_ref[...])
        for b in range(8):
            p3[b] = prod[b:b + 1, :]
        mod_ref[pl.ds(my_s, 1)] = p3[pl.ds(me, 1)] + b_ref[pl.ds(my_s, 1)]
        ks = (2, 4, 6)
        copies = []
        for i, k in enumerate(ks):
            pr = _peer(k)
            cp = pltpu.make_async_remote_copy(p3.at[_logical(pr)], mod_ref.at[my_s], sb.at[i], rb.at[i],
                                              device_id=pr, device_id_type=MESH)
            cp.start()
            copies.append(cp)
        for i, k in enumerate(ks):
            pr = _peer(k)
            s_src = 2 * pr[0] + pr[1]
            pltpu.make_async_remote_copy(p3.at[0], mod_ref.at[s_src], sb.at[i], rb.at[i],
                                         device_id=pr, device_id_type=MESH).wait_recv()
            mod_ref[pl.ds(s_src, 1)] = mod_ref[pl.ds(s_src, 1)] + b_ref[pl.ds(s_src, 1)]
        for cp in copies:
            cp.wait_send()

    vm = pl.BlockSpec(memory_space=pltpu.VMEM)
    return pl.pallas_call(
        body, name="mod_exchange", in_specs=[vm, vm, vm], out_specs=[vm, vm, vm],
        out_shape=[jax.ShapeDtypeStruct((8, 1, payload.shape[1]), F32), jax.ShapeDtypeStruct((4, 1, n_sh), F32),
                   jax.ShapeDtypeStruct((8, 128), F32)],
        scratch_shapes=[pltpu.VMEM((8, 1, n_sh), F32), pltpu.SemaphoreType.DMA((7,)), pltpu.SemaphoreType.DMA((7,)),
                        pltpu.SemaphoreType.DMA((3,)), pltpu.SemaphoreType.DMA((3,))],
        compiler_params=pltpu.CompilerParams(vmem_limit_bytes=VMEM_LIMIT),
    )(payload, w_ada_s, b_ada4)


def _chips():
    x, y, _ = _pos()
    out = []
    for k in (1, 2, 3):
        px, py = _flip(x, (k >> 1) & 1), _flip(y, k & 1)
        out.append((px, py, 2 * px + py))
    return out


def _half_rows(ref, which):
    half = ref.shape[-2] // 2
    return pl.ds(pl.multiple_of(which * half, 8), half)


def _weight_gather(shards):
    nw = len(shards)

    def body(*refs):
        ins, outs, token = refs[:nw], refs[nw:2 * nw], refs[2 * nw]
        send, recv, fsend, frecv = refs[2 * nw + 1:]
        token[...] = jnp.zeros_like(token)
        x, y, c = _pos()
        my_s = 2 * x + y
        sib = (x, y, 1 - c)
        chips = _chips()
        sends = []
        for w in range(nw):
            mine = _half_rows(ins[w], c)
            for k, (px, py, _) in enumerate(chips):
                cp = pltpu.make_async_remote_copy(ins[w].at[mine], outs[w].at[my_s, mine], send.at[3 * w + k],
                                                  recv.at[3 * w + k], device_id=(px, py, c), device_id_type=MESH)
                cp.start()
                sends.append(cp)
        for w in range(nw):
            mine = _half_rows(ins[w], c)
            for k, (px, py, ps) in enumerate(chips):
                got = outs[w].at[ps, mine]
                pltpu.make_async_remote_copy(got, got, send.at[3 * w + k], recv.at[3 * w + k],
                                             device_id=(px, py, c), device_id_type=MESH).wait_recv()
                cp = pltpu.make_async_remote_copy(got, got, fsend.at[3 * w + k], frecv.at[3 * w + k],
                                                  device_id=sib, device_id_type=MESH)
                cp.start()
                sends.append(cp)
        for w in range(nw):
            other = _half_rows(ins[w], 1 - c)
            for k, (px, py, ps) in enumerate(chips):
                got = outs[w].at[ps, other]
                pltpu.make_async_remote_copy(got, got, fsend.at[3 * w + k], frecv.at[3 * w + k],
                                             device_id=sib, device_id_type=MESH).wait_recv()
        for cp in sends:
            cp.wait_send()

    hbm = pl.BlockSpec(memory_space=pltpu.HBM)
    return pl.pallas_call(
        body, name="weight_gather", in_specs=[hbm] * nw,
        out_specs=[hbm] * nw + [pl.BlockSpec(memory_space=pltpu.VMEM)],
        out_shape=[pltpu.HBM((4,) + s.shape, s.dtype) for s in shards] + [jax.ShapeDtypeStruct((8, 128), F32)],
        scratch_shapes=[pltpu.SemaphoreType.DMA((3 * nw,)), pltpu.SemaphoreType.DMA((3 * nw,)),
                        pltpu.SemaphoreType.DMA((3 * nw,)), pltpu.SemaphoreType.DMA((3 * nw,))],
    )(*shards)


def _small_reduce(vec):
    n = vec.shape[1]

    def body(v_ref, tot_ref, gat_ref, sa, ra):
        _gather8(v_ref, gat_ref, sa, ra)
        tot = gat_ref[0]
        for i in range(1, 8):
            tot = tot + gat_ref[i]
        tot_ref[...] = tot

    vm = pl.BlockSpec(memory_space=pltpu.VMEM)
    return pl.pallas_call(
        body, name="small_reduce", in_specs=[vm], out_specs=[vm, vm],
        out_shape=[jax.ShapeDtypeStruct((1, n), F32), jax.ShapeDtypeStruct((8, 1, n), F32)],
        scratch_shapes=[pltpu.SemaphoreType.DMA((7,)), pltpu.SemaphoreType.DMA((7,))],
    )(vec)


def _add_half(g, sib, c_arr, rb, name):
    _, R, C = g.shape
    half = R // 2
    nb = half // rb

    def body(c_ref, g_ref, s_ref, o_ref):
        o_ref[...] = (g_ref[...] + s_ref[...]).astype(BF16)

    return pl.pallas_call(
        body, name=name,
        grid_spec=pltpu.PrefetchScalarGridSpec(
            num_scalar_prefetch=1, grid=(4, nb),
            in_specs=[pl.BlockSpec((1, rb, C), lambda s, i, c_ref: (s, c_ref[0] * nb + i, 0)),
                      pl.BlockSpec((1, rb, C), lambda s, i, c_ref: (s, i, 0))],
            out_specs=pl.BlockSpec((1, rb, C), lambda s, i, c_ref: (s, i, 0))),
        out_shape=jax.ShapeDtypeStruct((4, half, C), BF16),
        compiler_params=_cp("parallel", "parallel"),
    )(c_arr, g, sib)


def _sum4(r, rb, name):
    _, H, C = r.shape

    def body(r_ref, o_ref):
        o_ref[...] = ((r_ref[0].astype(F32) + r_ref[1].astype(F32)) + r_ref[2].astype(F32)) + r_ref[3].astype(F32)

    return pl.pallas_call(
        body, name=name, grid=(H // rb,),
        in_specs=[pl.BlockSpec((4, rb, C), lambda i: (0, i, 0))], out_specs=pl.BlockSpec((rb, C), lambda i: (i, 0)),
        out_shape=jax.ShapeDtypeStruct((H, C), F32), compiler_params=_cp("parallel"),
    )(r)


HBM_SPEC = pl.BlockSpec(memory_space=pltpu.HBM)
SEM_SPEC = pl.BlockSpec(memory_space=pltpu.SEMAPHORE)
EFFECT = pltpu.SideEffectType.DATAFLOW_SIDE_EFFECTING


def _split_start(name, bufs, n_sem, plan):
    nb = len(bufs)

    def body(*refs):
        ins, send, recv, token = refs[:nb], refs[nb], refs[nb + 1], refs[-1]
        for i, (src, dst, dev, _) in enumerate(plan(ins)):
            pltpu.make_async_remote_copy(src, dst, send.at[i], recv.at[i], device_id=dev, device_id_type=MESH).start()
        token[...] = jnp.zeros_like(token)

    outs = pl.pallas_call(
        body, name=name,
        out_shape=(pltpu.SemaphoreType.DMA((n_sem,)), pltpu.SemaphoreType.DMA((n_sem,)),
                   *[pltpu.HBM(b.shape, b.dtype) for b in bufs], jax.ShapeDtypeStruct((8, 128), F32)),
        in_specs=[HBM_SPEC] * nb,
        out_specs=(SEM_SPEC, SEM_SPEC, *([HBM_SPEC] * nb), pl.BlockSpec(memory_space=pltpu.VMEM)),
        input_output_aliases={i: 2 + i for i in range(nb)},
        compiler_params=pltpu.CompilerParams(has_side_effects=EFFECT),
    )(*[pltpu.with_memory_space_constraint(b, pltpu.HBM) for b in bufs])
    return outs[0], outs[1], list(outs[2:2 + nb]), outs[-1]


def _split_wait(name, send, recv, bufs, after, plan):
    nb = len(bufs)

    def body(*refs):
        ins, send_s, recv_s = refs[:nb], refs[nb], refs[nb + 1]
        for i, (src, dst, dev, mine) in enumerate(plan(ins)):
            pltpu.make_async_remote_copy(src, dst, send_s.at[i], recv_s.at[i], device_id=dev,
                                         device_id_type=MESH).wait_send()
            pltpu.make_async_remote_copy(src, mine, send_s.at[i], recv_s.at[i], device_id=dev,
                                         device_id_type=MESH).wait_recv()

    outs = pl.pallas_call(
        body, name=name, out_shape=[pltpu.HBM(b.shape, b.dtype) for b in bufs],
        in_specs=[HBM_SPEC] * nb + [SEM_SPEC, SEM_SPEC, pl.BlockSpec(memory_space=pl.ANY)],
        out_specs=[HBM_SPEC] * nb, input_output_aliases={i: i for i in range(nb)},
        compiler_params=pltpu.CompilerParams(has_side_effects=EFFECT),
    )(*bufs, send, recv, after)
    return list(outs)


def _plan_gather_ici(nw):
    def plan(refs):
        x, y, c = _pos()
        my_s = 2 * x + y
        out = []
        for w in range(nw):
            mine = _half_rows(refs[w], c)
            for px, py, ps in _chips():
                out.append((refs[w].at[mine], refs[nw + w].at[my_s, mine], (px, py, c), refs[nw + w].at[ps, mine]))
        return out
    return plan


def _plan_gather_fwd(nw):
    def plan(refs):
        x, y, c = _pos()
        out = []
        for w in range(nw):
            mine, other = _half_rows(refs[w], c), _half_rows(refs[w], 1 - c)
            for px, py, ps in _chips():
                got = refs[w].at[ps, mine]
                out.append((got, got, (x, y, 1 - c), refs[w].at[ps, other]))
        return out
    return plan


def _plan_swap(nw):
    def plan(refs):
        x, y, c = _pos()
        return [(refs[w].at[:, _half_rows(refs[w], 1 - c)], refs[nw + w], (x, y, 1 - c), refs[nw + w])
                for w in range(nw)]
    return plan


def _plan_scatter(nw):
    def plan(refs):
        x, y, c = _pos()
        my_s = 2 * x + y
        out = []
        for w in range(nw):
            for px, py, ps in _chips():
                out.append((refs[w].at[ps], refs[nw + w].at[my_s], (px, py, c), refs[nw + w].at[ps]))
        return out
    return plan


def _plan_join(nw):
    def plan(refs):
        x, y, c = _pos()
        out = []
        for w in range(nw):
            land = refs[nw + w]
            out.append((refs[w], land.at[_half_rows(land, c)], (x, y, 1 - c), land.at[_half_rows(land, 1 - c)]))
        return out
    return plan


def _hbm_empty(shape, dtype):
    return pltpu.with_memory_space_constraint(lax.empty(shape, dtype), pltpu.HBM)


def _put_slot(land, own, slot):
    return lax.dynamic_update_slice(land, own[None], (slot,) + (0,) * own.ndim)


def _pad_lanes(a, n):
    return jnp.pad(a, ((0, 0), (0, n - a.shape[1])))


def kernel(x, c, positions, w_ada, b_ada, norm1_w, w_in, conv_w, conv_b, dt_bias, a_log, d_skip, attn_sinks, ssm_norm_w, w_out, norm2_w, w_gate_up, w_down, final_norm_w, loss_target, m_w_ada, m_b_ada, m_norm1_w, m_w_in, m_conv_w, m_conv_b, m_dt_bias, m_a_log, m_d_skip, m_attn_sinks, m_ssm_norm_w, m_w_out, m_norm2_w, m_w_gate_up, m_w_down, m_final_norm_w, v_w_ada, v_b_ada, v_norm1_w, v_w_in, v_conv_w, v_conv_b, v_dt_bias, v_a_log, v_d_skip, v_attn_sinks, v_ssm_norm_w, v_w_out, v_norm2_w, v_w_gate_up, v_w_down, v_final_norm_w):
    T = x.shape[1]
    tm = min(256, T)
    xi, yi, ci = lax.axis_index("x"), lax.axis_index("y"), lax.axis_index("c")
    my_s = 2 * xi + yi
    xs = x[0]
    tgt = loss_target[0]

    w_in_b = w_in[0].astype(BF16)
    s_i, r_i, bufs, tok = _split_start("wgather_in_ici_start", [w_in_b, _hbm_empty((4,) + w_in_b.shape, BF16)], 3,
                                       _plan_gather_ici(1))

    payload = jnp.concatenate([c + tok[0:1, 0:1], conv_w[0].reshape(1, CONVK * 256)], axis=1)
    gat, mod4, tok = _mod_exchange(payload, w_ada[0], b_ada.reshape(4, 1, 1536))
    mod6 = mod4.reshape(6, D)
    c_all = gat[:, 0, 0:D]
    cw_dev = gat[:, 0, D:].reshape(4, 2, CONVK, 256)[:, 0]
    conv_full = cw_dev.transpose(1, 0, 2).reshape(CONVK, CONVC)
    inv_freq = (10000.0 ** (-jnp.arange(32, dtype=F32) / 32))
    inv_row = jnp.tile(inv_freq, 4).reshape(1, 128)
    cos, sin_s = _rope_tables(positions.reshape(T, 1), inv_row + tok[0:1, :], tm)
    bufs = _split_wait("wgather_in_ici_wait", s_i, r_i, bufs, cos, _plan_gather_ici(1))
    s_j, r_j, bufs, tok = _split_start("wgather_in_fwd_start", bufs[1:], 3, _plan_gather_fwd(1))
    bufs = _split_wait("wgather_in_fwd_wait", s_j, r_j, bufs, tok, _plan_gather_fwd(1))
    g_in = _put_slot(bufs[0], w_in_b, my_s)
    w_pad = jnp.concatenate([g_in[0], g_in[1], g_in[2], g_in[3], jnp.zeros((D, IN_PAD - IN_PROJ), BF16)], axis=1)

    late = [(w_out[0] + tok[0, 0]).astype(BF16), w_gate_up[0].astype(BF16), w_down[0].astype(BF16)]
    lands = [_hbm_empty((4,) + s.shape, BF16) for s in late]
    s_a, r_a, bufs, tok = _split_start("wgather_ici_start", late + lands, 9, _plan_gather_ici(3))

    qkv, z, xbc, dtr, h1b = _in_proj_fwd(xs, cos, sin_s, mod6 + tok[0, 0], norm1_w, w_pad, min(512, T))
    sinks = attn_sinks
    attn, lse = _attn_fwd(qkv, sinks)
    bufs = _split_wait("wgather_ici_wait", s_a, r_a, bufs, attn, _plan_gather_ici(3))
    s_b, r_b, lands, tok = _split_start("wgather_fwd_start", bufs[3:], 9, _plan_gather_fwd(3))
    dtb = _pad_lanes(dt_bias, 128)
    alog = _pad_lanes(a_log, 128)
    dskx = jnp.repeat(d_skip, HD, axis=1)
    mats = _ssd_mats()
    ynorm, ypre, states, conv_pre = _ssd_fwd(xbc, z, dtr, conv_full, conv_b, dtb + tok[0, 0], alog, dskx, ssm_norm_w,
                                             mats)
    lands = _split_wait("wgather_fwd_wait", s_b, r_b, lands, ynorm, _plan_gather_fwd(3))
    g_out, g_gu, g_dn = [_put_slot(l, s, my_s) for l, s in zip(lands, late)]
    w_out_f = g_out.reshape(D, D)
    w_dn_f = g_dn.reshape(DFF, D)

    fw2 = final_norm_w.reshape(1, D)
    sq, dmix, dx1, h2b, act, dfb, dgu, dob, sm_ffn = _mix_ffn(
        xs, attn, ynorm, tgt, mod6, norm2_w, fw2, w_out_f, g_gu, w_dn_f, tm)

    tt = min(2048, T)
    c_arr = ci.reshape(1).astype(jnp.int32)
    tok0 = jnp.zeros((8, 128), F32)
    gw_dn4 = _tn_matmul(act, dfb[None], tt, "dw_down", tok0).reshape(4, DFF // 4, D)
    gw_gu4 = _tn_matmul(h2b[None], dgu, tt, "dw_gate_up", tok0)
    gw_out4 = jnp.concatenate(
        [_tn_matmul(attn[None], dob[None], tt, "dw_out_a", tok0)[0],
         _tn_matmul(ynorm[None], dob[None], tt, "dw_out_y", tok0)[0]], axis=0).reshape(4, D // 4, D)
    big1 = [gw_out4, gw_gu4, gw_dn4]
    rbs1 = [128, 128, 176]
    sib1 = [_hbm_empty((4, g.shape[1] // 2, g.shape[2]), F32) for g in big1]
    s_c, r_c, bufs, tok = _split_start("gswap_start", big1 + sib1, 3, _plan_swap(3))

    dzxd, d_cw, d_cb, d_sw, d_sk, d_dtb, d_av = _ssd_bwd(
        xbc, conv_pre, z, dtr, ypre, states, dmix, conv_full, dtb + tok[0, 0], alog, dskx, ssm_norm_w, mats)
    bufs = _split_wait("gswap_wait", s_c, r_c, bufs, dzxd, _plan_swap(3))
    sums1 = [_add_half(g, s, c_arr, rb, "grad_add_%d" % i)
             for i, (g, s, rb) in enumerate(zip(bufs[:3], bufs[3:], rbs1))]
    land1 = [_hbm_empty(p.shape, BF16) for p in sums1]
    s_d, r_d, bufs, tok = _split_start("gscatter_start", sums1 + land1, 9, _plan_scatter(3))
    dqkv, d_sinks = _attn_bwd(qkv, sinks + tok[0:1, 0:8], lse, dmix, cos, sin_s)
    bufs = _split_wait("gscatter_wait", s_d, r_d, bufs, dqkv, _plan_scatter(3))
    slots1 = [_put_slot(l, lax.dynamic_index_in_dim(p, my_s, 0, keepdims=False), my_s)
              for p, l in zip(bufs[:3], bufs[3:])]
    halves1 = [_sum4(r, rb, "grad_sum_%d" % i) for i, (r, rb) in enumerate(zip(slots1, rbs1))]
    full1 = [_hbm_empty((2 * h.shape[0], h.shape[1]), F32) for h in halves1]
    s_e, r_e, bufs, tok = _split_start("gjoin_start", halves1 + full1, 3, _plan_join(3))
    h1_3 = h1b[None]
    gw_in = jnp.concatenate([_tn_matmul(h1_3, dqkv[None], tt, "dw_in_qkv", tok)[0],
                             _tn_matmul(h1_3, dzxd[None], tt, "dw_in_zxd", tok)[0]], axis=1)
    gw_in4 = jnp.stack([gw_in[:, 578 * s:578 * (s + 1)] for s in range(4)])
    bufs = _split_wait("gjoin_wait", s_e, r_e, bufs, gw_in4, _plan_join(3))
    g_out_s, g_gu_s, g_dn_s = [lax.dynamic_update_slice(f, h, (ci * h.shape[0], 0)) for h, f in zip(bufs[:3], bufs[3:])]

    sib0 = _hbm_empty((4, D // 2, IN_PROJ // 4), F32)
    s_f, r_f, bufs, tok = _split_start("gswap_in_start", [gw_in4, sib0], 1, _plan_swap(1))
    bufs = _split_wait("gswap_in_wait", s_f, r_f, bufs, tok, _plan_swap(1))
    sum0 = _add_half(bufs[0], bufs[1], c_arr, 128, "grad_add_in")
    s_g, r_g, bufs, tok = _split_start("gscatter_in_start", [sum0, _hbm_empty(sum0.shape, BF16)], 3, _plan_scatter(1))
    grad_x, sm_in = _in_proj_bwd(xs, dx1, dqkv, dzxd, mod6 + tok[0, 0], norm1_w, w_pad, min(512, T))
    bufs = _split_wait("gscatter_in_wait", s_g, r_g, bufs, grad_x, _plan_scatter(1))
    slot0 = _put_slot(bufs[1], lax.dynamic_index_in_dim(bufs[0], my_s, 0, keepdims=False), my_s)
    half0 = _sum4(slot0, 128, "grad_sum_in")
    s_h, r_h, bufs, tok = _split_start("gjoin_in_start", [half0, _hbm_empty((D, IN_PROJ // 4), F32)], 1, _plan_join(1))
    bufs = _split_wait("gjoin_in_wait", s_h, r_h, bufs, tok, _plan_join(1))
    g_in_s = lax.dynamic_update_slice(bufs[1], bufs[0], (ci * (D // 2), 0))

    a_neg = -jnp.exp(alog)
    pieces = [sm_in[1:2], sm_in[2:3], sm_ffn[5:6], sm_ffn[2:3], sm_ffn[3:4], sm_ffn[4:5],
              sm_in[0:1], sm_ffn[1:2], sm_ffn[0:1], d_cb, d_cw.reshape(1, CONVK * CONVC),
              _pad_lanes(d_sw, SW), d_dtb, d_av * a_neg, d_sk, d_sinks,
              _pad_lanes((0.5 / D * jnp.sum(sq)).reshape(1, 1), 128)]
    vec = jnp.concatenate(pieces, axis=1)
    tot, allv = _small_reduce(vec)
    o = 0
    offs = []
    for p in pieces:
        offs.append(o)
        o += p.shape[1]
    seg = lambda i, n: tot[:, offs[i]:offs[i] + n]
    g_b_ada = tot[:, 0:6 * D]
    g_norm1, g_norm2, g_final, g_conv_b = seg(6, D), seg(7, D), seg(8, D), seg(9, D)
    g_conv_w = lax.dynamic_slice_in_dim(seg(10, CONVK * CONVC).reshape(CONVK, CONVC), my_s * 256, 256, axis=1)
    g_ssm_w, g_dtb, g_alog, g_dsk, g_sink = seg(11, SW), seg(12, 8), seg(13, 8), seg(14, 8), seg(15, 8)
    loss = tot[0, offs[16]]

    small_names = ["b_ada", "norm1_w", "conv_w", "conv_b", "dt_bias", "a_log", "d_skip", "attn_sinks", "ssm_norm_w",
                   "norm2_w", "final_norm_w"]
    small_g = [g_b_ada, g_norm1, g_conv_w, g_conv_b, g_dtb, g_alog, g_dsk, g_sink, g_ssm_w, g_norm2, g_final]
    as2d = lambda a: a.reshape(-1, a.shape[-1])
    small_w = [as2d(a) for a in (b_ada, norm1_w, conv_w, conv_b, dt_bias, a_log, d_skip, attn_sinks, ssm_norm_w,
                                 norm2_w, final_norm_w)]
    small_m = [as2d(a) for a in (m_b_ada, m_norm1_w, m_conv_w, m_conv_b, m_dt_bias, m_a_log, m_d_skip, m_attn_sinks,
                                 m_ssm_norm_w, m_norm2_w, m_final_norm_w)]
    small_v = [as2d(a) for a in (v_b_ada, v_norm1_w, v_conv_w, v_conv_b, v_dt_bias, v_a_log, v_d_skip, v_attn_sinks,
                                 v_ssm_norm_w, v_norm2_w, v_final_norm_w)]
    small_g, sd, smn, svn = _adam_small(small_g, small_w, small_m, small_v)

    sc_all = c_all * jax.nn.sigmoid(c_all)
    dmod_all = allv[:, 0, 0:6 * D]
    dmod_s = lax.dynamic_slice_in_dim(dmod_all, my_s * 1536, 1536, axis=1)
    g_ada, d_ada, m_ada, v_ada = _adam_w_ada(sc_all, dmod_s, w_ada[0], m_w_ada[0], v_w_ada[0], 256)
    g_in_s, d_in, m_in, v_in = _adam_2d(w_in[0], g_in_s, m_w_in[0], v_w_in[0], 256, "adam_w_in")
    g_out_s, d_out, m_out, v_out = _adam_2d(w_out[0], g_out_s, m_w_out[0], v_w_out[0], 256, "adam_w_out")
    g_gu_s, d_gu, m_gu, v_gu = _adam_2d(w_gate_up[0], g_gu_s, m_w_gate_up[0], v_w_gate_up[0], 256, "adam_w_gate_up")
    g_dn_s, d_dn, m_dn, v_dn = _adam_2d(w_down[0], g_dn_s, m_w_down[0], v_w_down[0], 352, "adam_w_down")

    order = ["w_ada", "b_ada", "norm1_w", "w_in", "conv_w", "conv_b", "dt_bias", "a_log", "d_skip", "attn_sinks",
             "ssm_norm_w", "w_out", "norm2_w", "w_gate_up", "w_down", "final_norm_w"]
    shapes = dict(w_ada=w_ada.shape, b_ada=b_ada.shape, norm1_w=norm1_w.shape, w_in=w_in.shape, conv_w=conv_w.shape,
                  conv_b=conv_b.shape, dt_bias=dt_bias.shape, a_log=a_log.shape, d_skip=d_skip.shape,
                  attn_sinks=attn_sinks.shape, ssm_norm_w=ssm_norm_w.shape, w_out=w_out.shape, norm2_w=norm2_w.shape,
                  w_gate_up=w_gate_up.shape, w_down=w_down.shape, final_norm_w=final_norm_w.shape)
    grads = dict(w_ada=g_ada, w_in=g_in_s, w_out=g_out_s, w_gate_up=g_gu_s, w_down=g_dn_s)
    deltas = dict(w_ada=d_ada, w_in=d_in, w_out=d_out, w_gate_up=d_gu, w_down=d_dn)
    new_m = dict(w_ada=m_ada, w_in=m_in, w_out=m_out, w_gate_up=m_gu, w_down=m_dn)
    new_v = dict(w_ada=v_ada, w_in=v_in, w_out=v_out, w_gate_up=v_gu, w_down=v_dn)
    for i, nme in enumerate(small_names):
        grads[nme], deltas[nme], new_m[nme], new_v[nme] = small_g[i], sd[i], smn[i], svn[i]
    outs = [loss, grad_x[None]]
    for table in (grads, deltas, new_m, new_v):
        outs += [table[nme].reshape(shapes[nme]) for nme in order]
    return tuple(outs)
```

```python
import functools
import math

import jax
import jax.numpy as jnp
from jax import lax
from jax.experimental import pallas as pl
from jax.experimental.pallas import tpu as pltpu

F32 = jnp.float32
BF16 = jnp.bfloat16
HI = lax.Precision.HIGHEST
MESH = pl.DeviceIdType.MESH

D = 1024
HD = 64
NQ = 8
AW = 512
KVW = 128
SW = 512
NST = 128
CONVK = 4
CONVC = 1024
BLK = 128
IN_PROJ = 2312
IN_PAD = 2432
IN_SH = IN_PROJ // 4
IN_SH_PAD = 608
DFF = 2816
GU_SH = 1408
EPS = 1e-6
NEG = -1e30
LR, B1, B2, AEPS, WD, STEP = 0.001, 0.9, 0.999, 1e-08, 0.01, 10
VMEM_LIMIT = 58 * 1024 * 1024


def _cp(*sem):
    return pltpu.CompilerParams(dimension_semantics=sem or None, vmem_limit_bytes=VMEM_LIMIT)


def _dot(a, b):
    return jnp.dot(a, b, preferred_element_type=F32)


def _dot_nt(a, b):
    return lax.dot_general(a, b, (((1,), (1,)), ((), ())), preferred_element_type=F32)


def _dot_tn(a, b):
    return lax.dot_general(a, b, (((0,), (0,)), ((), ())), preferred_element_type=F32)


def _dot_hi(a, b):
    return jnp.dot(a, b, precision=HI, preferred_element_type=F32)


def _sigmoid(x):
    return 1.0 / (1.0 + jnp.exp(-x))


def _iota(shape, dim):
    return lax.broadcasted_iota(jnp.int32, shape, dim)


def _load_resident(hbm_ref, vmem_ref, sem):
    @pl.when(pl.program_id(0) == 0)
    def _():
        cp = pltpu.make_async_copy(hbm_ref, vmem_ref, sem)
        cp.start()
        cp.wait()


def _swap32(t):
    lane = _iota(t.shape, 1)
    return jnp.where((lane & 63) < 32, pltpu.roll(t, 96, 1), pltpu.roll(t, 32, 1))


def _rope_fwd(t, cos, sin_s):
    return t * cos + _swap32(t) * sin_s


def _rope_bwd(t, cos, sin_s):
    return t * cos - _swap32(t) * sin_s


def _rope_tables(pos_col, inv_freq_row, tm):
    T = pos_col.shape[0]

    def body(p_ref, f_ref, cos_ref, sin_ref):
        ang = p_ref[...].astype(F32) * f_ref[...]
        lane = _iota((tm, 128), 1)
        s = jnp.sin(ang)
        cos_ref[...] = jnp.cos(ang)
        sin_ref[...] = jnp.where((lane & 63) < 32, -s, s)

    return pl.pallas_call(
        body, name="rope_tables", grid=(T // tm,),
        in_specs=[pl.BlockSpec((tm, 1), lambda i: (i, 0)), pl.BlockSpec((1, 128), lambda i: (0, 0))],
        out_specs=[pl.BlockSpec((tm, 128), lambda i: (i, 0))] * 2,
        out_shape=[jax.ShapeDtypeStruct((T, 128), F32)] * 2,
        compiler_params=_cp("parallel"),
    )(pos_col, inv_freq_row)


def _in_proj_fwd(x, cos, sin_s, mod6, norm1_w, w_pad, tm):
    T = x.shape[0]

    def body(x_ref, cos_ref, sin_ref, mod_ref, nw_ref, w_hbm, qkv_ref, z_ref, xbc_ref, dt_ref, h_ref, w_vmem, sem):
        _load_resident(w_hbm, w_vmem, sem)
        xv = x_ref[...]
        r = lax.rsqrt(jnp.mean(xv * xv, axis=-1, keepdims=True) + EPS)
        h = (xv * r * nw_ref[...]) * (1.0 + mod_ref[1:2, :]) + mod_ref[0:1, :]
        hb = h.astype(BF16)
        h_ref[...] = hb
        proj = _dot(hb, w_vmem[...])
        cs, sn = cos_ref[...], sin_ref[...]
        for j in range(5):
            qkv_ref[:, 128 * j:128 * (j + 1)] = _rope_fwd(proj[:, 128 * j:128 * (j + 1)], cs, sn).astype(BF16)
        qkv_ref[:, 640:768] = proj[:, 640:768].astype(BF16)
        z_ref[...] = proj[:, 768:1280]
        xbc_ref[...] = proj[:, 1280:2304]
        dt_ref[...] = proj[:, 2304:2432]

    row = lambda w: pl.BlockSpec((tm, w), lambda i: (i, 0))
    full = lambda a: pl.BlockSpec(a.shape, lambda i: (0,) * a.ndim)
    return pl.pallas_call(
        body, name="in_proj_fwd", grid=(T // tm,),
        in_specs=[row(D), row(128), row(128), full(mod6), full(norm1_w), pl.BlockSpec(memory_space=pl.ANY)],
        out_specs=[row(768), row(512), row(1024), row(128), row(D)],
        out_shape=[jax.ShapeDtypeStruct((T, 768), BF16), jax.ShapeDtypeStruct((T, 512), F32),
                   jax.ShapeDtypeStruct((T, 1024), F32), jax.ShapeDtypeStruct((T, 128), F32),
                   jax.ShapeDtypeStruct((T, D), BF16)],
        scratch_shapes=[pltpu.VMEM((D, IN_PAD), BF16), pltpu.SemaphoreType.DMA],
        compiler_params=_cp("arbitrary"),
    )(x, cos, sin_s, mod6, norm1_w, w_pad)


def _head_variants(pair, j):
    lane = _iota(pair.shape, 1)
    lo = lane < 64
    kv = j // 2
    ev = jnp.where(lo, pair, 0.0)
    od = jnp.where(lo, 0.0, pair)
    if kv == 0:
        od = pltpu.roll(od, 64, 1)
    else:
        ev = pltpu.roll(ev, 64, 1)
    return ev.astype(BF16), od.astype(BF16)


def _kv_variants(vcat):
    lane = _iota(vcat.shape, 1)
    lo = lane < 64
    v0 = jnp.where(lo, vcat, 0.0)
    v1 = jnp.where(lo, 0.0, vcat)
    out = {
        (0, 0): v0, (0, 1): pltpu.roll(v0, 64, 1),
        (1, 0): pltpu.roll(v1, 64, 1), (1, 1): v1,
    }
    return {k: v.astype(BF16) for k, v in out.items()}


def _fold_masks(n):
    upper = _iota((BLK, BLK), 1) > _iota((BLK, BLK), 0)
    return upper, upper & (n == 0)


def _attn_fwd(qkv, sinks):
    T = qkv.shape[0]
    nb = T // BLK

    def body(sink_ref, q_ref, kc_ref, kp_ref, vc_ref, vp_ref, o_ref, lse_ref):
        n = pl.program_id(0)
        vpv = _kv_variants(vp_ref[...].astype(F32))
        vcv = _kv_variants(vc_ref[...].astype(F32))
        q_all = jnp.concatenate(
            [v for j in range(4) for v in _head_variants(q_ref[:, 128 * j:128 * (j + 1)].astype(F32), j)], axis=0)
        s_prev = _dot_nt(q_all, kp_ref[...])
        s_cur = _dot_nt(q_all, kc_ref[...])
        upper, dead = _fold_masks(n)
        lane = _iota((BLK, 128), 1)
        lse_acc = jnp.zeros((BLK, 128), F32)
        for jj in range(4):
            acc = jnp.zeros((BLK, 128), F32)
            for par in range(2):
                h = 2 * jj + par
                rows = slice(h * BLK, (h + 1) * BLK)
                sink = sink_ref[0, h]
                s = jnp.where(dead, NEG, jnp.where(upper, s_prev[rows], s_cur[rows]) * 0.125)
                m = jnp.maximum(jnp.max(s, axis=1, keepdims=True), sink)
                p = jnp.exp(s - m)
                den = jnp.sum(p, axis=1, keepdims=True) + jnp.exp(sink - m)
                pn = p * (1.0 / den)
                acc = (acc + _dot(jnp.where(upper, pn, 0.0).astype(BF16), vpv[(jj // 2, par)])
                       + _dot(jnp.where(upper, 0.0, pn).astype(BF16), vcv[(jj // 2, par)]))
                lse_acc = jnp.where(lane == h, m + jnp.log(den), lse_acc)
            o_ref[:, 128 * jj:128 * (jj + 1)] = acc.astype(BF16)
        lse_ref[...] = lse_acc

    prev = lambda n: jnp.maximum(n - 1, 0)
    return pl.pallas_call(
        body, name="attn_fwd", grid=(nb,),
        in_specs=[pl.BlockSpec(memory_space=pltpu.SMEM),
                  pl.BlockSpec((BLK, 512), lambda n: (n, 0)),
                  pl.BlockSpec((BLK, 128), lambda n: (n, 4)),
                  pl.BlockSpec((BLK, 128), lambda n: (prev(n), 4)),
                  pl.BlockSpec((BLK, 128), lambda n: (n, 5)),
                  pl.BlockSpec((BLK, 128), lambda n: (prev(n), 5))],
        out_specs=[pl.BlockSpec((BLK, 512), lambda n: (n, 0)), pl.BlockSpec((BLK, 128), lambda n: (n, 0))],
        out_shape=[jax.ShapeDtypeStruct((T, 512), BF16), jax.ShapeDtypeStruct((T, 128), F32)],
        compiler_params=_cp("parallel"),
    )(sinks, qkv, qkv, qkv, qkv, qkv)


def _attn_bwd(qkv, sinks, lse, dmix, cos, sin_s):
    T = qkv.shape[0]
    nb = T // BLK

    def body(sink_ref, q_ref, kc_ref, kp_ref, vc_ref, vp_ref, lse_ref, do_ref, cq_ref, sq_ref, ck_ref, sk_ref,
             out_ref, ds_ref, dq_car, dk_car, dv_car):
        n = pl.program_id(0)
        lane = _iota((BLK, 128), 1)

        @pl.when(n == 0)
        def _():
            ds_ref[...] = jnp.zeros_like(ds_ref)
            dq_car[...] = jnp.zeros_like(dq_car)
            dk_car[...] = jnp.zeros_like(dk_car)
            dv_car[...] = jnp.zeros_like(dv_car)

        @pl.when(n < nb)
        def _():
            kp, kc, vp, vc = kp_ref[...], kc_ref[...], vp_ref[...], vc_ref[...]
            kpv = _kv_variants(kp.astype(F32))
            kcv = _kv_variants(kc.astype(F32))
            lse_v = lse_ref[...]
            q_all = jnp.concatenate(
                [v for j in range(4) for v in _head_variants(q_ref[:, 128 * j:128 * (j + 1)].astype(F32), j)], axis=0)
            do_all = jnp.concatenate(
                [v for j in range(4) for v in _head_variants(do_ref[:, 128 * j:128 * (j + 1)], j)], axis=0)
            s_prev, s_cur = _dot_nt(q_all, kp), _dot_nt(q_all, kc)
            dp_prev, dp_cur = _dot_nt(do_all, vp), _dot_nt(do_all, vc)
            upper, dead = _fold_masks(n)
            out_ref[:, 0:512] = dq_car[...]
            dsk = jnp.zeros((1, 128), F32)
            ds_u, ds_l, p_u, p_l = [], [], [], []
            for jj in range(4):
                dq_acc = jnp.zeros((BLK, 128), F32)
                for par in range(2):
                    h = 2 * jj + par
                    rows = slice(h * BLK, (h + 1) * BLK)
                    lse_h = jnp.sum(jnp.where(lane == h, lse_v, 0.0), axis=1, keepdims=True)
                    s = jnp.where(dead, NEG, jnp.where(upper, s_prev[rows], s_cur[rows]) * 0.125)
                    p = jnp.exp(s - lse_h)
                    dp = jnp.where(upper, dp_prev[rows], dp_cur[rows])
                    delta = jnp.sum(p * dp, axis=1, keepdims=True)
                    ds = p * (dp - delta) * 0.125
                    dsu, dsl = jnp.where(upper, ds, 0.0).astype(BF16), jnp.where(upper, 0.0, ds).astype(BF16)
                    dq_acc = dq_acc + _dot(dsu, kpv[(jj // 2, par)]) + _dot(dsl, kcv[(jj // 2, par)])
                    ds_u.append(dsu)
                    ds_l.append(dsl)
                    p_u.append(jnp.where(upper, p, 0.0).astype(BF16))
                    p_l.append(jnp.where(upper, 0.0, p).astype(BF16))
                    dsk = dsk + jnp.where(lane[0:1] == h, -jnp.sum(jnp.exp(sink_ref[0, h] - lse_h) * delta), 0.0)
                dq_car[:, 128 * jj:128 * (jj + 1)] = _rope_bwd(dq_acc, cq_ref[...], sq_ref[...]).astype(BF16)
            stack = lambda parts: jnp.concatenate(parts, axis=0)
            dk_prev, dk_cur = _dot_tn(stack(ds_u), q_all), _dot_tn(stack(ds_l), q_all)
            dv_prev, dv_cur = _dot_tn(stack(p_u), do_all), _dot_tn(stack(p_l), do_all)
            ds_ref[...] += dsk
            out_ref[:, 512:640] = _rope_bwd(dk_car[...] + dk_prev, ck_ref[...], sk_ref[...]).astype(BF16)
            out_ref[:, 640:768] = (dv_car[...] + dv_prev).astype(BF16)
            dk_car[...] = dk_cur
            dv_car[...] = dv_cur

        @pl.when(n == nb)
        def _():
            out_ref[:, 0:512] = dq_car[...]
            out_ref[:, 512:640] = _rope_bwd(dk_car[...], ck_ref[...], sk_ref[...]).astype(BF16)
            out_ref[:, 640:768] = dv_car[...].astype(BF16)

    cur = lambda n: jnp.minimum(n, nb - 1)
    prev = lambda n: jnp.maximum(cur(n) - 1, 0)
    outb = lambda n: jnp.maximum(n - 1, 0)
    return pl.pallas_call(
        body, name="attn_bwd", grid=(nb + 1,),
        in_specs=[pl.BlockSpec(memory_space=pltpu.SMEM),
                  pl.BlockSpec((BLK, 512), lambda n: (cur(n), 0)),
                  pl.BlockSpec((BLK, 128), lambda n: (cur(n), 4)),
                  pl.BlockSpec((BLK, 128), lambda n: (prev(n), 4)),
                  pl.BlockSpec((BLK, 128), lambda n: (cur(n), 5)),
                  pl.BlockSpec((BLK, 128), lambda n: (prev(n), 5)),
                  pl.BlockSpec((BLK, 128), lambda n: (cur(n), 0)),
                  pl.BlockSpec((BLK, 512), lambda n: (cur(n), 0)),
                  pl.BlockSpec((BLK, 128), lambda n: (cur(n), 0)),
                  pl.BlockSpec((BLK, 128), lambda n: (cur(n), 0)),
                  pl.BlockSpec((BLK, 128), lambda n: (outb(n), 0)),
                  pl.BlockSpec((BLK, 128), lambda n: (outb(n), 0))],
        out_specs=[pl.BlockSpec((BLK, 768), lambda n: (outb(n), 0)), pl.BlockSpec((1, 128), lambda n: (0, 0))],
        out_shape=[jax.ShapeDtypeStruct((T, 768), BF16), jax.ShapeDtypeStruct((1, 128), F32)],
        scratch_shapes=[pltpu.VMEM((BLK, 512), BF16), pltpu.VMEM((BLK, 128), F32), pltpu.VMEM((BLK, 128), F32)],
        compiler_params=_cp("arbitrary"),
    )(sinks, qkv, qkv, qkv, qkv, qkv, lse, dmix, cos, sin_s, cos, sin_s)


def _ssd_mats():
    e = jnp.arange(SW)[None, :] // HD == jnp.arange(128)[:, None]
    tri = jnp.arange(BLK)[None, :] <= jnp.arange(BLK)[:, None]
    return (jnp.tile(e, (3, 1)).astype(BF16), jnp.tile(e.T, (2, 1)).astype(BF16),
            jnp.tile(tri, (1, 3)).astype(BF16), jnp.tile(tri.T, (1, 3)).astype(BF16))


def _pieces(x, n, axis):
    out, r = [], x
    for i in range(n):
        p = r.astype(BF16)
        out.append(p)
        if i + 1 < n:
            r = r - p.astype(F32)
    return jnp.concatenate(out, axis=axis)


def _expand(x, e3):
    return _dot(_pieces(x, 3, 1), e3)


def _head_sums(x, et2):
    return _dot(_pieces(x, 2, 1), et2)


def _run_sum(tri3, x):
    return _dot(tri3, _pieces(x, 3, 0))


def _shift_down(u, tail, j):
    rolled = pltpu.roll(u, j, 0)
    first = jnp.where(_iota(tail.shape, 0) < j, pltpu.roll(tail, j, 0), rolled[0:8])
    return jnp.concatenate([first, rolled[8:]], axis=0)


def _shift_up(d, head, j):
    rolled = pltpu.roll(d, BLK - j, 0)
    last = jnp.where(_iota(head.shape, 0) >= 8 - j, pltpu.roll(head, 8 - j, 0), rolled[BLK - 8:])
    return jnp.concatenate([rolled[:BLK - 8], last], axis=0)


def _ssd_parts(dtr, dtb, alog, e3, tril3):
    xx = dtr + dtb
    dt = jnp.maximum(xx, 0.0) + jnp.log(1.0 + jnp.exp(-jnp.abs(xx)))
    a_neg = -jnp.exp(alog)
    tril = _iota((BLK, BLK), 1) <= _iota((BLK, BLK), 0)
    cs = _run_sum(tril3, dt * a_neg)
    csx = _expand(cs, e3)
    last = csx[BLK - 1:BLK, :]
    return dict(xx=xx, dt=dt, a_neg=a_neg, tril=tril, cs=cs, cs_t=cs.T,
                ecsx=jnp.exp(csx), dtex=jnp.exp(last - csx), cdx=jnp.exp(last), dtx=_expand(dt, e3))


def _decay(parts, h):
    seg = parts["cs"][:, h:h + 1] - parts["cs_t"][h:h + 1, :]
    return jnp.exp(jnp.where(parts["tril"], seg, NEG))


def _group_cols(a, g):
    return a[:, 256 * g:256 * (g + 1)]


def _ssd_fwd(xbc, z, dtr, conv_w, conv_b, dtb, alog, dskx, ssm_w, mats):
    T = xbc.shape[0]
    nc = T // BLK

    def body(u_ref, tail_ref, z_ref, dtr_ref, cw_ref, cb_ref, dtb_ref, al_ref, dk_ref, sw_ref, e3_ref, tril3_ref,
             yn_ref, yp_ref, st_ref, co_ref, s_scr):
        n = pl.program_id(0)

        @pl.when(n == 0)
        def _():
            s_scr[...] = jnp.zeros_like(s_scr)

        u = u_ref[...]
        tail = jnp.where(n > 0, tail_ref[...], 0.0)
        co = cb_ref[...] + cw_ref[3:4, :] * u
        for j in range(1, CONVK):
            co = co + cw_ref[3 - j:4 - j, :] * _shift_down(u, tail, j)
        co_ref[...] = co
        xc = co * _sigmoid(co)
        pt = _ssd_parts(dtr_ref[...], dtb_ref[...], al_ref[...], e3_ref[...], tril3_ref[...])
        xs = xc[:, :SW]
        bm = [xc[:, 512:640].astype(BF16), xc[:, 640:768].astype(BF16)]
        cm = [xc[:, 768:896].astype(BF16), xc[:, 896:1024].astype(BF16)]
        s_in = s_scr[...]
        st_ref[0] = s_in
        xdt = xs * pt["dtx"]
        xde = (xdt * pt["dtex"]).astype(BF16)
        lane = _iota((BLK, 128), 1)
        lo = lane < 64
        ys, s_new = [], []
        for g in range(2):
            cb = _dot_nt(cm[g], bm[g])
            yoff = _dot(cm[g], _group_cols(s_in, g).astype(BF16))
            s_new.append(_dot_tn(bm[g], _group_cols(xde, g)))
            for jj in range(2):
                j = 2 * g + jj
                chunk = xdt[:, 128 * j:128 * (j + 1)]
                g_ev = (cb * _decay(pt, 2 * j)).astype(BF16)
                g_od = (cb * _decay(pt, 2 * j + 1)).astype(BF16)
                yd = _dot(g_ev, jnp.where(lo, chunk, 0.0).astype(BF16)) + _dot(g_od, jnp.where(lo, 0.0, chunk).astype(BF16))
                ys.append(yd + yoff[:, 128 * jj:128 * (jj + 1)] * pt["ecsx"][:, 128 * j:128 * (j + 1)])
        y = jnp.concatenate(ys, axis=1) + xs * dk_ref[...]
        s_scr[...] = s_in * pt["cdx"] + jnp.concatenate(s_new, axis=1)
        yp_ref[...] = y
        zv = z_ref[...]
        yz = y * (zv * _sigmoid(zv))
        outs = []
        for g in range(2):
            yg = _group_cols(yz, g)
            outs.append(yg * lax.rsqrt(jnp.mean(yg * yg, axis=-1, keepdims=True) + EPS))
        yn_ref[...] = (jnp.concatenate(outs, axis=1) * sw_ref[...]).astype(BF16)

    e3, _, tril3, _ = mats
    tail8 = lambda n: jnp.maximum(n * (BLK // 8) - 1, 0)
    full = lambda a: pl.BlockSpec(a.shape, lambda n: (0,) * a.ndim)
    return pl.pallas_call(
        body, name="ssd_fwd", grid=(nc,),
        in_specs=[pl.BlockSpec((BLK, CONVC), lambda n: (n, 0)), pl.BlockSpec((8, CONVC), lambda n: (tail8(n), 0)),
                  pl.BlockSpec((BLK, SW), lambda n: (n, 0)), pl.BlockSpec((BLK, 128), lambda n: (n, 0)),
                  full(conv_w), full(conv_b), full(dtb), full(alog), full(dskx), full(ssm_w), full(e3), full(tril3)],
        out_specs=[pl.BlockSpec((BLK, SW), lambda n: (n, 0)), pl.BlockSpec((BLK, SW), lambda n: (n, 0)),
                   pl.BlockSpec((1, NST, SW), lambda n: (n, 0, 0)), pl.BlockSpec((BLK, CONVC), lambda n: (n, 0))],
        out_shape=[jax.ShapeDtypeStruct((T, SW), BF16), jax.ShapeDtypeStruct((T, SW), F32),
                   jax.ShapeDtypeStruct((nc, NST, SW), F32), jax.ShapeDtypeStruct((T, CONVC), F32)],
        scratch_shapes=[pltpu.VMEM((NST, SW), F32)],
        compiler_params=_cp("arbitrary"),
    )(xbc, xbc, z, dtr, conv_w, conv_b, dtb, alog, dskx, ssm_w, e3, tril3)


def _ssd_bwd(xbc, co_all, z, dtr, ypre, states, dmix, conv_w, dtb, alog, dskx, ssm_w, mats):
    T = xbc.shape[0]
    nc = T // BLK

    def body(u_ref, co_ref, z_ref, dtr_ref, yp_ref, st_ref, dyn_ref, cw_ref, dtb_ref, al_ref, dk_ref, sw_ref,
             e3_ref, et2_ref, tril3_ref, triu3_ref,
             out_ref, dcw_ref, dcb_ref, dsw_ref, dsk_ref, ddtb_ref, dav_ref, ds_scr, dco_scr, dskx_scr):
        i = pl.program_id(0)

        @pl.when(i == 0)
        def _():
            for r in (dcw_ref, dcb_ref, dsw_ref, dsk_ref, ddtb_ref, dav_ref, ds_scr, dco_scr, dskx_scr):
                r[...] = jnp.zeros_like(r)

        co = co_ref[...]
        sg = _sigmoid(co)
        xc = co * sg
        pt = _ssd_parts(dtr_ref[...], dtb_ref[...], al_ref[...], e3_ref[...], tril3_ref[...])
        dtx, ecsx, dtex, cdx = pt["dtx"], pt["ecsx"], pt["dtex"], pt["cdx"]
        xs = xc[:, :SW]
        bm = [xc[:, 512:640].astype(BF16), xc[:, 640:768].astype(BF16)]
        cm = [xc[:, 768:896].astype(BF16), xc[:, 896:1024].astype(BF16)]
        s_in = st_ref[0]
        ds_out = ds_scr[...]
        e_t = et2_ref[...]

        zv = z_ref[...]
        sz = _sigmoid(zv)
        silu_z = zv * sz
        ypre = yp_ref[...]
        yz = ypre * silu_z
        dyn = dyn_ref[...]
        sw = sw_ref[...]
        dyz, yns = [], []
        for g in range(2):
            yg = _group_cols(yz, g)
            r = lax.rsqrt(jnp.mean(yg * yg, axis=-1, keepdims=True) + EPS)
            yn = yg * r
            dg = _group_cols(dyn, g) * _group_cols(sw, g)
            dyz.append(r * (dg - yn * jnp.mean(dg * yn, axis=-1, keepdims=True)))
            yns.append(yn)
        dyz = jnp.concatenate(dyz, axis=1)
        dsw_ref[...] += jnp.sum(dyn * jnp.concatenate(yns, axis=1), axis=0, keepdims=True)
        dy = dyz * silu_z
        dz = dyz * ypre * (sz * (1.0 + zv * (1.0 - sz)))

        xdt = xs * dtx
        xdt_b = xdt.astype(BF16)
        edy = (ecsx * dy).astype(BF16)
        xde = (xdt * dtex).astype(BF16)
        lane = _iota((BLK, 128), 1)
        lo = lane < 64
        row8 = _iota((8, 128), 0)
        dcs = jnp.zeros((BLK, 128), F32)
        col_rows = jnp.zeros((8, 128), F32)
        dxdt, bds, yoff, dbs, dcs_g, ds_new = [], [], [], [], [], []
        for g in range(2):
            s_g = _group_cols(s_in, g).astype(BF16)
            dso_g = _group_cols(ds_out, g).astype(BF16)
            cb = _dot_nt(cm[g], bm[g])
            bds.append(_dot(bm[g], dso_g))
            yoff.append(_dot(cm[g], s_g))
            dcb_g = jnp.zeros((BLK, BLK), F32)
            for jj in range(2):
                j = 2 * g + jj
                dy_c = dy[:, 128 * j:128 * (j + 1)]
                xdt_c = xdt_b[:, 128 * j:128 * (j + 1)]
                acc = jnp.zeros((BLK, 128), F32)
                for par in range(2):
                    h = 2 * j + par
                    lm = _decay(pt, h)
                    gm = cb * lm
                    dy_m = (jnp.where(lo, dy_c, 0.0) if par == 0 else jnp.where(lo, 0.0, dy_c)).astype(BF16)
                    dg_h = _dot_nt(dy_m, xdt_c)
                    w_h = dg_h * gm
                    dcs = dcs + jnp.where(lane == h, jnp.sum(w_h, axis=1, keepdims=True), 0.0)
                    col_rows = col_rows + jnp.where(row8 == h, jnp.sum(w_h, axis=0, keepdims=True), 0.0)
                    dcb_g = dcb_g + dg_h * lm
                    acc = acc + _dot_tn(gm.astype(BF16), dy_m)
                dxdt.append(acc)
            dcb_b = dcb_g.astype(BF16)
            dcs_g.append(_dot(dcb_b, bm[g]) + _dot_nt(_group_cols(edy, g), s_g))
            dbs.append(_dot_tn(dcb_b, cm[g]) + _dot_nt(_group_cols(xde, g), dso_g))
            ds_new.append(_dot_tn(cm[g], _group_cols(edy, g)))
        bds = jnp.concatenate(bds, axis=1)
        yoff = jnp.concatenate(yoff, axis=1) * ecsx
        dxdt = jnp.concatenate(dxdt, axis=1) + dtex * bds
        ds_scr[...] = cdx * ds_out + jnp.concatenate(ds_new, axis=1)

        t_m = _head_sums(dtex * xdt * bds, e_t)
        colsum_t = jnp.concatenate([col_rows, jnp.zeros((BLK - 8, 128), F32)], axis=0).T
        cd = jnp.exp(pt["cs"][BLK - 1:BLK, :])
        sds = jnp.sum(s_in * ds_out, axis=0, keepdims=True)
        last_row = jnp.sum(t_m, axis=0, keepdims=True) + cd * _head_sums(jnp.broadcast_to(sds, (8, SW)), e_t)[0:1]
        dcs = dcs - colsum_t + _head_sums(dy * yoff, e_t) - t_m
        dcs = dcs + jnp.where(_iota((BLK, 128), 0) == BLK - 1, last_row, 0.0)
        da = _run_sum(triu3_ref[...], dcs)
        dt = pt["dt"]
        ddt = da * pt["a_neg"] + _head_sums(dxdt * xs, e_t)
        dav_ref[...] += jnp.sum(da * dt, axis=0, keepdims=True)
        ddtr = ddt * _sigmoid(pt["xx"])
        ddtb_ref[...] += jnp.sum(ddtr, axis=0, keepdims=True)
        dxs = dxdt * dtx + dy * dk_ref[...]
        dskx_scr[...] += jnp.sum(dy * xs, axis=0, keepdims=True)
        dxc = jnp.concatenate([dxs, dbs[0], dbs[1], dcs_g[0], dcs_g[1]], axis=1)
        dco = dxc * (sg * (1.0 + co * (1.0 - sg)))

        dcb_ref[...] += jnp.sum(dco, axis=0, keepdims=True)
        u = u_ref[...]
        head = dco_scr[...]
        du = jnp.zeros_like(dco)
        for j in range(CONVK):
            up_j = dco if j == 0 else _shift_up(dco, head, j)
            dcw_ref[3 - j:4 - j, :] += jnp.sum(up_j * u, axis=0, keepdims=True)
            du = du + cw_ref[3 - j:4 - j, :] * up_j
        dco_scr[...] = dco[0:8]
        out_ref[:, 0:512] = dz.astype(BF16)
        out_ref[:, 512:1536] = du.astype(BF16)
        out_ref[:, 1536:1664] = ddtr.astype(BF16)

        @pl.when(i == nc - 1)
        def _():
            dsk_ref[...] = _head_sums(jnp.broadcast_to(dskx_scr[...], (8, SW)), e_t)[0:1]

    e3, et2, tril3, triu3 = mats
    rev = lambda i: nc - 1 - i
    full = lambda a: pl.BlockSpec(a.shape, lambda i: (0,) * a.ndim)
    acc = lambda r, c: pl.BlockSpec((r, c), lambda i: (0, 0))
    return pl.pallas_call(
        body, name="ssd_bwd", grid=(nc,),
        in_specs=[pl.BlockSpec((BLK, CONVC), lambda i: (rev(i), 0)), pl.BlockSpec((BLK, CONVC), lambda i: (rev(i), 0)),
                  pl.BlockSpec((BLK, SW), lambda i: (rev(i), 0)), pl.BlockSpec((BLK, 128), lambda i: (rev(i), 0)),
                  pl.BlockSpec((BLK, SW), lambda i: (rev(i), 0)), pl.BlockSpec((1, NST, SW), lambda i: (rev(i), 0, 0)),
                  pl.BlockSpec((BLK, SW), lambda i: (rev(i), 1)),
                  full(conv_w), full(dtb), full(alog), full(dskx), full(ssm_w),
                  full(e3), full(et2), full(tril3), full(triu3)],
        out_specs=[pl.BlockSpec((BLK, 1664), lambda i: (rev(i), 0)),
                   acc(CONVK, CONVC), acc(1, CONVC), acc(1, SW), acc(1, 128), acc(1, 128), acc(1, 128)],
        out_shape=[jax.ShapeDtypeStruct((T, 1664), BF16),
                   jax.ShapeDtypeStruct((CONVK, CONVC), F32), jax.ShapeDtypeStruct((1, CONVC), F32),
                   jax.ShapeDtypeStruct((1, SW), F32), jax.ShapeDtypeStruct((1, 128), F32),
                   jax.ShapeDtypeStruct((1, 128), F32), jax.ShapeDtypeStruct((1, 128), F32)],
        scratch_shapes=[pltpu.VMEM((NST, SW), F32), pltpu.VMEM((8, CONVC), F32), pltpu.VMEM((1, SW), F32)],
        compiler_params=_cp("arbitrary"),
    )(xbc, co_all, z, dtr, ypre, states, dmix, conv_w, dtb, alog, dskx, ssm_w, e3, et2, tril3, triu3)


def _mix_ffn(x, attn, ynorm, tgt, mod6, norm2_w, final_w, w_out, w_gu, w_dn, tm):
    T = x.shape[0]
    nt = T // tm

    def body(x_ref, a_ref, y_ref, t_ref, mod_ref, n2_ref, fw_ref, wo_hbm, wgu_hbm, wdn_hbm,
             sq_ref, dmix_ref, dx1_ref, h2_ref, act_ref, df_ref, dgu_ref, do_ref, sm_ref,
             wo, wgu, wdn, sems):
        i = pl.program_id(0)

        @pl.when(i == 0)
        def _():
            cps = [pltpu.make_async_copy(s, d, sems.at[k]) for k, (s, d) in
                   enumerate(((wo_hbm, wo), (wgu_hbm, wgu), (wdn_hbm, wdn)))]
            for c in cps:
                c.start()
            for c in cps:
                c.wait()
            sq_ref[...] = jnp.zeros_like(sq_ref)
            sm_ref[...] = jnp.zeros_like(sm_ref)

        gate1, shift2, scale2, gate2 = mod_ref[2:3, :], mod_ref[3:4, :], mod_ref[4:5, :], mod_ref[5:6, :]
        n2w, fw = n2_ref[...], fw_ref[...]
        o = _dot(a_ref[...], wo[0:AW, :]) + _dot(y_ref[...], wo[AW:D, :])
        x1 = x_ref[...] + gate1 * o
        r2 = lax.rsqrt(jnp.mean(x1 * x1, axis=-1, keepdims=True) + EPS)
        xh2 = x1 * r2
        n2 = xh2 * n2w
        h2b = (n2 * (1.0 + scale2) + shift2).astype(BF16)
        h2_ref[...] = h2b
        f = jnp.zeros((tm, D), F32)
        saved = []
        for p in range(2):
            gp = _dot(h2b, wgu[p])
            upj = _dot(h2b, wgu[p + 2])
            sg = _sigmoid(gp)
            sl = gp * sg
            actb = (sl * upj).astype(BF16)
            act_ref[p] = actb
            f = f + _dot(actb, wdn[GU_SH * p:GU_SH * (p + 1), :])
            saved.append((gp, upj, sg, sl))
        x2 = x1 + gate2 * f
        r3 = lax.rsqrt(jnp.mean(x2 * x2, axis=-1, keepdims=True) + EPS)
        xh3 = x2 * r3
        err = xh3 * fw - t_ref[...]
        sq_ref[...] += jnp.sum(err * err, axis=0, keepdims=True)
        dy = err * (1.0 / D)
        dfw = jnp.sum(dy * xh3, axis=0, keepdims=True)
        dxh3 = dy * fw
        dx2 = r3 * (dxh3 - xh3 * jnp.mean(dxh3 * xh3, axis=-1, keepdims=True))
        dgate2 = jnp.sum(dx2 * f, axis=0, keepdims=True)
        dfb = (dx2 * gate2).astype(BF16)
        df_ref[...] = dfb
        dh2 = jnp.zeros((tm, D), F32)
        for p in range(2):
            gp, upj, sg, sl = saved[p]
            dact = _dot_nt(dfb, wdn[GU_SH * p:GU_SH * (p + 1), :])
            dg = (dact * upj * (sg * (1.0 + gp * (1.0 - sg)))).astype(BF16)
            du = (dact * sl).astype(BF16)
            dgu_ref[p] = dg
            dgu_ref[p + 2] = du
            dh2 = dh2 + _dot_nt(dg, wgu[p]) + _dot_nt(du, wgu[p + 2])
        dshift2 = jnp.sum(dh2, axis=0, keepdims=True)
        dscale2 = jnp.sum(dh2 * n2, axis=0, keepdims=True)
        dn2 = dh2 * (1.0 + scale2)
        dn2w = jnp.sum(dn2 * xh2, axis=0, keepdims=True)
        dxh2 = dn2 * n2w
        dx1 = dx2 + r2 * (dxh2 - xh2 * jnp.mean(dxh2 * xh2, axis=-1, keepdims=True))
        dx1_ref[...] = dx1
        dgate1 = jnp.sum(dx1 * o, axis=0, keepdims=True)
        dob = (dx1 * gate1).astype(BF16)
        do_ref[...] = dob
        dmix_ref[...] = _dot_nt(dob, wo[...])
        sm_ref[...] += jnp.concatenate(
            [dfw, dn2w, dshift2, dscale2, dgate2, dgate1, jnp.zeros((2, D), F32)], axis=0)

    row = lambda w: pl.BlockSpec((tm, w), lambda i: (i, 0))
    full = lambda a: pl.BlockSpec(a.shape, lambda i: (0,) * a.ndim)
    anyspec = pl.BlockSpec(memory_space=pl.ANY)
    return pl.pallas_call(
        body, name="mix_ffn", grid=(nt,),
        in_specs=[row(D), row(AW), row(SW), row(D), full(mod6), full(norm2_w), full(final_w), anyspec, anyspec, anyspec],
        out_specs=[pl.BlockSpec((1, D), lambda i: (0, 0)), row(D), row(D), row(D),
                   pl.BlockSpec((2, tm, GU_SH), lambda i: (0, i, 0)), row(D),
                   pl.BlockSpec((4, tm, GU_SH), lambda i: (0, i, 0)), row(D),
                   pl.BlockSpec((8, D), lambda i: (0, 0))],
        out_shape=[jax.ShapeDtypeStruct((1, D), F32), jax.ShapeDtypeStruct((T, D), F32), jax.ShapeDtypeStruct((T, D), F32),
                   jax.ShapeDtypeStruct((T, D), BF16), jax.ShapeDtypeStruct((2, T, GU_SH), BF16),
                   jax.ShapeDtypeStruct((T, D), BF16), jax.ShapeDtypeStruct((4, T, GU_SH), BF16),
                   jax.ShapeDtypeStruct((T, D), BF16), jax.ShapeDtypeStruct((8, D), F32)],
        scratch_shapes=[pltpu.VMEM((D, D), BF16), pltpu.VMEM((4, D, GU_SH), BF16), pltpu.VMEM((DFF, D), BF16),
                        pltpu.SemaphoreType.DMA((3,))],
        compiler_params=_cp("arbitrary"),
    )(x, attn, ynorm, tgt, mod6, norm2_w, final_w, w_out, w_gu, w_dn)


def _in_proj_bwd(x, dx1, dqkv, dzxd, mod6, norm1_w, w_pad, tm):
    T = x.shape[0]

    def body(x_ref, dx1_ref, dq_ref, dz_ref, mod_ref, nw_ref, w_hbm, gx_ref, sm_ref, w_vmem, sem):
        _load_resident(w_hbm, w_vmem, sem)

        @pl.when(pl.program_id(0) == 0)
        def _():
            sm_ref[...] = jnp.zeros_like(sm_ref)

        dh = _dot_nt(dq_ref[...], w_vmem[:, 0:768]) + _dot_nt(dz_ref[...], w_vmem[:, 768:IN_PAD])
        xv = x_ref[...]
        nw = nw_ref[...]
        scale1 = mod_ref[1:2, :]
        r = lax.rsqrt(jnp.mean(xv * xv, axis=-1, keepdims=True) + EPS)
        xh = xv * r
        n1 = xh * nw
        dshift = jnp.sum(dh, axis=0, keepdims=True)
        dscale = jnp.sum(dh * n1, axis=0, keepdims=True)
        dn = dh * (1.0 + scale1)
        dnw = jnp.sum(dn * xh, axis=0, keepdims=True)
        dxh = dn * nw
        gx_ref[...] = dx1_ref[...] + r * (dxh - xh * jnp.mean(dxh * xh, axis=-1, keepdims=True))
        sm_ref[...] += jnp.concatenate([dnw, dshift, dscale, jnp.zeros((5, D), F32)], axis=0)

    row = lambda w: pl.BlockSpec((tm, w), lambda i: (i, 0))
    full = lambda a: pl.BlockSpec(a.shape, lambda i: (0,) * a.ndim)
    return pl.pallas_call(
        body, name="in_proj_bwd", grid=(T // tm,),
        in_specs=[row(D), row(D), row(768), row(1664), full(mod6), full(norm1_w), pl.BlockSpec(memory_space=pl.ANY)],
        out_specs=[row(D), pl.BlockSpec((8, D), lambda i: (0, 0))],
        out_shape=[jax.ShapeDtypeStruct((T, D), F32), jax.ShapeDtypeStruct((8, D), F32)],
        scratch_shapes=[pltpu.VMEM((D, IN_PAD), BF16), pltpu.SemaphoreType.DMA],
        compiler_params=_cp("arbitrary"),
    )(x, dx1, dqkv, dzxd, mod6, norm1_w, w_pad)


def _tn_matmul(a3, b3, tt, name, dep):
    ja, T, K = a3.shape
    jb, _, N = b3.shape
    J = max(ja, jb)

    def body(a_ref, b_ref, dep_ref, o_ref):
        t = pl.program_id(1)
        prod = _dot_tn(a_ref[0], b_ref[0])

        @pl.when(t == 0)
        def _():
            o_ref[0] = prod

        @pl.when(t > 0)
        def _():
            o_ref[0] += prod

    return pl.pallas_call(
        body, name=name, grid=(J, T // tt),
        in_specs=[pl.BlockSpec((1, tt, K), lambda j, t: (j if ja > 1 else 0, t, 0)),
                  pl.BlockSpec((1, tt, N), lambda j, t: (j if jb > 1 else 0, t, 0)),
                  pl.BlockSpec((8, 128), lambda j, t: (0, 0))],
        out_specs=pl.BlockSpec((1, K, N), lambda j, t: (j, 0, 0)),
        out_shape=jax.ShapeDtypeStruct((J, K, N), F32),
        compiler_params=_cp("parallel", "arbitrary"),
    )(a3, b3, dep)


def _adam_math(w, g, m, v):
    m = B1 * m + (1.0 - B1) * g
    v = B2 * v + (1.0 - B2) * (g * g)
    m_hat = m / (1.0 - B1 ** STEP)
    v_hat = v / (1.0 - B2 ** STEP)
    delta = -LR * (m_hat / (jnp.sqrt(v_hat) + AEPS) + WD * w)
    return delta, m, v


def _adam_2d(w, g, m, v, rb, name):
    R, C = w.shape

    def body(w_ref, g_ref, m_ref, v_ref, go_ref, d_ref, mo_ref, vo_ref):
        g = g_ref[...]
        d, mn, vn = _adam_math(w_ref[...], g, m_ref[...], v_ref[...])
        go_ref[...] = g
        d_ref[...] = d
        mo_ref[...] = mn
        vo_ref[...] = vn

    spec = pl.BlockSpec((rb, C), lambda i: (i, 0))
    return pl.pallas_call(
        body, name=name, grid=(R // rb,), in_specs=[spec] * 4, out_specs=[spec] * 4,
        out_shape=[jax.ShapeDtypeStruct((R, C), F32)] * 4, compiler_params=_cp("parallel"),
    )(w, g, m, v)


def _adam_w_ada(sc_all, dmod_s, w, m, v, rb):
    R, C = w.shape

    def body(sc_ref, dm_ref, w_ref, m_ref, v_ref, g_ref, d_ref, mo_ref, vo_ref):
        g = lax.dot_general(sc_ref[...], dm_ref[...], (((0,), (0,)), ((), ())), precision=HI, preferred_element_type=F32)
        d, mn, vn = _adam_math(w_ref[...], g, m_ref[...], v_ref[...])
        g_ref[...] = g
        d_ref[...] = d
        mo_ref[...] = mn
        vo_ref[...] = vn

    spec = pl.BlockSpec((rb, C), lambda i: (i, 0))
    return pl.pallas_call(
        body, name="adam_w_ada", grid=(R // rb,),
        in_specs=[pl.BlockSpec((8, rb), lambda i: (0, i)), pl.BlockSpec((8, C), lambda i: (0, 0)), spec, spec, spec],
        out_specs=[spec] * 4, out_shape=[jax.ShapeDtypeStruct((R, C), F32)] * 4, compiler_params=_cp("parallel"),
    )(sc_all, dmod_s, w, m, v)


def _adam_small(grads, ws, ms, vs):
    k = len(ws)

    def body(*refs):
        g, w, m, v = refs[0:k], refs[k:2 * k], refs[2 * k:3 * k], refs[3 * k:4 * k]
        g_o, d_o, m_o, v_o = refs[4 * k:5 * k], refs[5 * k:6 * k], refs[6 * k:7 * k], refs[7 * k:8 * k]
        for i in range(k):
            gi = g[i][...]
            d, mn, vn = _adam_math(w[i][...], gi, m[i][...], v[i][...])
            g_o[i][...] = gi
            d_o[i][...] = d
            m_o[i][...] = mn
            v_o[i][...] = vn

    shapes = [jax.ShapeDtypeStruct(w.shape, F32) for w in ws]
    vm = pl.BlockSpec(memory_space=pltpu.VMEM)
    outs = pl.pallas_call(
        body, name="adam_small", in_specs=[vm] * (4 * k), out_specs=[vm] * (4 * k), out_shape=shapes * 4,
    )(*grads, *ws, *ms, *vs)
    return outs[0:k], outs[k:2 * k], outs[2 * k:3 * k], outs[3 * k:4 * k]


def _pos():
    return lax.axis_index("x"), lax.axis_index("y"), lax.axis_index("c")


def _flip(v, bit):
    return 1 - v if bit else v


def _peer(k):
    x, y, c = _pos()
    return (_flip(x, (k >> 2) & 1), _flip(y, (k >> 1) & 1), _flip(c, k & 1))


def _logical(p):
    return 4 * p[0] + 2 * p[1] + p[2]


def _gather8(src_ref, dst_ref, send_sems, recv_sems):
    me = _logical(_pos())
    dst_ref[pl.ds(me, 1)] = src_ref[...][None]
    copies = []
    for k in range(1, 8):
        cp = pltpu.make_async_remote_copy(src_ref, dst_ref.at[me], send_sems.at[k - 1], recv_sems.at[k - 1],
                                          device_id=_peer(k), device_id_type=MESH)
        cp.start()
        copies.append(cp)
    for k in range(1, 8):
        pltpu.make_async_remote_copy(src_ref, dst_ref.at[_logical(_peer(k))], send_sems.at[k - 1], recv_sems.at[k - 1],
                                     device_id=_peer(k), device_id_type=MESH).wait_recv()
    for cp in copies:
        cp.wait_send()


def _rows_select(ref3, width):
    row = _iota((8, width), 0)
    out = jnp.zeros((8, width), F32)
    for i in range(8):
        out = jnp.where(row == i, ref3[i][:, 0:width], out)
    return out


def _mod_exchange(payload, w_ada_s, b_ada4):
    n_sh = w_ada_s.shape[1]

    def body(pay_ref, w_ref, b_ref, gat_ref, mod_ref, token, p3, sa, ra, sb, rb):
        token[...] = jnp.zeros_like(token)
        x, y, c = _pos()
        me = _logical((x, y, c))
        my_s = 2 * x + y
        _gather8(pay_ref, gat_ref, sa, ra)
        cmat = _rows_select(gat_ref, D)
        prod = _dot_hi(cmat * _sigmoid(cmat), w_ref[...])
        for b in range(8):
            p3[b] = prod[b:b + 1, :]
        mod_ref[pl.ds(my_s, 1)] = p3[pl.ds(me, 1)] + b_ref[pl.ds(my_s, 1)]
        ks = (2, 4, 6)
        copies = []
        for i, k in enumerate(ks):
            pr = _peer(k)
            cp = pltpu.make_async_remote_copy(p3.at[_logical(pr)], mod_ref.at[my_s], sb.at[i], rb.at[i],
                                              device_id=pr, device_id_type=MESH)
            cp.start()
            copies.append(cp)
        for i, k in enumerate(ks):
            pr = _peer(k)
            s_src = 2 * pr[0] + pr[1]
            pltpu.make_async_remote_copy(p3.at[0], mod_ref.at[s_src], sb.at[i], rb.at[i],
                                         device_id=pr, device_id_type=MESH).wait_recv()
            mod_ref[pl.ds(s_src, 1)] = mod_ref[pl.ds(s_src, 1)] + b_ref[pl.ds(s_src, 1)]
        for cp in copies:
            cp.wait_send()

    vm = pl.BlockSpec(memory_space=pltpu.VMEM)
    return pl.pallas_call(
        body, name="mod_exchange", in_specs=[vm, vm, vm], out_specs=[vm, vm, vm],
        out_shape=[jax.ShapeDtypeStruct((8, 1, payload.shape[1]), F32), jax.ShapeDtypeStruct((4, 1, n_sh), F32),
                   jax.ShapeDtypeStruct((8, 128), F32)],
        scratch_shapes=[pltpu.VMEM((8, 1, n_sh), F32), pltpu.SemaphoreType.DMA((7,)), pltpu.SemaphoreType.DMA((7,)),
                        pltpu.SemaphoreType.DMA((3,)), pltpu.SemaphoreType.DMA((3,))],
        compiler_params=pltpu.CompilerParams(vmem_limit_bytes=VMEM_LIMIT),
    )(payload, w_ada_s, b_ada4)


def _chips():
    x, y, _ = _pos()
    out = []
    for k in (1, 2, 3):
        px, py = _flip(x, (k >> 1) & 1), _flip(y, k & 1)
        out.append((px, py, 2 * px + py))
    return out


def _half_rows(ref, which):
    half = ref.shape[-2] // 2
    return pl.ds(pl.multiple_of(which * half, 8), half)


def _weight_gather(shards):
    nw = len(shards)

    def body(*refs):
        ins, outs, token = refs[:nw], refs[nw:2 * nw], refs[2 * nw]
        send, recv, fsend, frecv = refs[2 * nw + 1:]
        token[...] = jnp.zeros_like(token)
        x, y, c = _pos()
        my_s = 2 * x + y
        sib = (x, y, 1 - c)
        chips = _chips()
        sends = []
        for w in range(nw):
            mine = _half_rows(ins[w], c)
            for k, (px, py, _) in enumerate(chips):
                cp = pltpu.make_async_remote_copy(ins[w].at[mine], outs[w].at[my_s, mine], send.at[3 * w + k],
                                                  recv.at[3 * w + k], device_id=(px, py, c), device_id_type=MESH)
                cp.start()
                sends.append(cp)
        for w in range(nw):
            mine = _half_rows(ins[w], c)
            for k, (px, py, ps) in enumerate(chips):
                got = outs[w].at[ps, mine]
                pltpu.make_async_remote_copy(got, got, send.at[3 * w + k], recv.at[3 * w + k],
                                             device_id=(px, py, c), device_id_type=MESH).wait_recv()
                cp = pltpu.make_async_remote_copy(got, got, fsend.at[3 * w + k], frecv.at[3 * w + k],
                                                  device_id=sib, device_id_type=MESH)
                cp.start()
                sends.append(cp)
        for w in range(nw):
            other = _half_rows(ins[w], 1 - c)
            for k, (px, py, ps) in enumerate(chips):
                got = outs[w].at[ps, other]
                pltpu.make_async_remote_copy(got, got, fsend.at[3 * w + k], frecv.at[3 * w + k],
                                             device_id=sib, device_id_type=MESH).wait_recv()
        for cp in sends:
            cp.wait_send()

    hbm = pl.BlockSpec(memory_space=pltpu.HBM)
    return pl.pallas_call(
        body, name="weight_gather", in_specs=[hbm] * nw,
        out_specs=[hbm] * nw + [pl.BlockSpec(memory_space=pltpu.VMEM)],
        out_shape=[pltpu.HBM((4,) + s.shape, s.dtype) for s in shards] + [jax.ShapeDtypeStruct((8, 128), F32)],
        scratch_shapes=[pltpu.SemaphoreType.DMA((3 * nw,)), pltpu.SemaphoreType.DMA((3 * nw,)),
                        pltpu.SemaphoreType.DMA((3 * nw,)), pltpu.SemaphoreType.DMA((3 * nw,))],
    )(*shards)


def _small_reduce(vec):
    n = vec.shape[1]

    def body(v_ref, tot_ref, gat_ref, sa, ra):
        _gather8(v_ref, gat_ref, sa, ra)
        tot = gat_ref[0]
        for i in range(1, 8):
            tot = tot + gat_ref[i]
        tot_ref[...] = tot

    vm = pl.BlockSpec(memory_space=pltpu.VMEM)
    return pl.pallas_call(
        body, name="small_reduce", in_specs=[vm], out_specs=[vm, vm],
        out_shape=[jax.ShapeDtypeStruct((1, n), F32), jax.ShapeDtypeStruct((8, 1, n), F32)],
        scratch_shapes=[pltpu.SemaphoreType.DMA((7,)), pltpu.SemaphoreType.DMA((7,))],
    )(vec)


def _add_half(g, sib, c_arr, rb, name):
    _, R, C = g.shape
    half = R // 2
    nb = half // rb

    def body(c_ref, g_ref, s_ref, o_ref):
        o_ref[...] = (g_ref[...] + s_ref[...]).astype(BF16)

    return pl.pallas_call(
        body, name=name,
        grid_spec=pltpu.PrefetchScalarGridSpec(
            num_scalar_prefetch=1, grid=(4, nb),
            in_specs=[pl.BlockSpec((1, rb, C), lambda s, i, c_ref: (s, c_ref[0] * nb + i, 0)),
                      pl.BlockSpec((1, rb, C), lambda s, i, c_ref: (s, i, 0))],
            out_specs=pl.BlockSpec((1, rb, C), lambda s, i, c_ref: (s, i, 0))),
        out_shape=jax.ShapeDtypeStruct((4, half, C), BF16),
        compiler_params=_cp("parallel", "parallel"),
    )(c_arr, g, sib)


def _sum4(r, rb, name):
    _, H, C = r.shape

    def body(r_ref, o_ref):
        o_ref[...] = ((r_ref[0].astype(F32) + r_ref[1].astype(F32)) + r_ref[2].astype(F32)) + r_ref[3].astype(F32)

    return pl.pallas_call(
        body, name=name, grid=(H // rb,),
        in_specs=[pl.BlockSpec((4, rb, C), lambda i: (0, i, 0))], out_specs=pl.BlockSpec((rb, C), lambda i: (i, 0)),
        out_shape=jax.ShapeDtypeStruct((H, C), F32), compiler_params=_cp("parallel"),
    )(r)


HBM_SPEC = pl.BlockSpec(memory_space=pltpu.HBM)
SEM_SPEC = pl.BlockSpec(memory_space=pltpu.SEMAPHORE)
EFFECT = pltpu.SideEffectType.DATAFLOW_SIDE_EFFECTING


def _split_start(name, bufs, n_sem, plan):
    nb = len(bufs)

    def body(*refs):
        ins, send, recv, token = refs[:nb], refs[nb], refs[nb + 1], refs[-1]
        for i, (src, dst, dev, _) in enumerate(plan(ins)):
            pltpu.make_async_remote_copy(src, dst, send.at[i], recv.at[i], device_id=dev, device_id_type=MESH).start()
        token[...] = jnp.zeros_like(token)

    outs = pl.pallas_call(
        body, name=name,
        out_shape=(pltpu.SemaphoreType.DMA((n_sem,)), pltpu.SemaphoreType.DMA((n_sem,)),
                   *[pltpu.HBM(b.shape, b.dtype) for b in bufs], jax.ShapeDtypeStruct((8, 128), F32)),
        in_specs=[HBM_SPEC] * nb,
        out_specs=(SEM_SPEC, SEM_SPEC, *([HBM_SPEC] * nb), pl.BlockSpec(memory_space=pltpu.VMEM)),
        input_output_aliases={i: 2 + i for i in range(nb)},
        compiler_params=pltpu.CompilerParams(has_side_effects=EFFECT),
    )(*[pltpu.with_memory_space_constraint(b, pltpu.HBM) for b in bufs])
    return outs[0], outs[1], list(outs[2:2 + nb]), outs[-1]


def _split_wait(name, send, recv, bufs, after, plan):
    nb = len(bufs)

    def body(*refs):
        ins, send_s, recv_s = refs[:nb], refs[nb], refs[nb + 1]
        for i, (src, dst, dev, mine) in enumerate(plan(ins)):
            pltpu.make_async_remote_copy(src, dst, send_s.at[i], recv_s.at[i], device_id=dev,
                                         device_id_type=MESH).wait_send()
            pltpu.make_async_remote_copy(src, mine, send_s.at[i], recv_s.at[i], device_id=dev,
                                         device_id_type=MESH).wait_recv()

    outs = pl.pallas_call(
        body, name=name, out_shape=[pltpu.HBM(b.shape, b.dtype) for b in bufs],
        in_specs=[HBM_SPEC] * nb + [SEM_SPEC, SEM_SPEC, pl.BlockSpec(memory_space=pl.ANY)],
        out_specs=[HBM_SPEC] * nb, input_output_aliases={i: i for i in range(nb)},
        compiler_params=pltpu.CompilerParams(has_side_effects=EFFECT),
    )(*bufs, send, recv, after)
    return list(outs)


def _plan_gather_ici(nw):
    def plan(refs):
        x, y, c = _pos()
        my_s = 2 * x + y
        out = []
        for w in range(nw):
            mine = _half_rows(refs[w], c)
            for px, py, ps in _chips():
                out.append((refs[w].at[mine], refs[nw + w].at[my_s, mine], (px, py, c), refs[nw + w].at[ps, mine]))
        return out
    return plan


def _plan_gather_fwd(nw):
    def plan(refs):
        x, y, c = _pos()
        out = []
        for w in range(nw):
            mine, other = _half_rows(refs[w], c), _half_rows(refs[w], 1 - c)
            for px, py, ps in _chips():
                got = refs[w].at[ps, mine]
                out.append((got, got, (x, y, 1 - c), refs[w].at[ps, other]))
        return out
    return plan


def _plan_swap(nw):
    def plan(refs):
        x, y, c = _pos()
        return [(refs[w].at[:, _half_rows(refs[w], 1 - c)], refs[nw + w], (x, y, 1 - c), refs[nw + w])
                for w in range(nw)]
    return plan


def _plan_scatter(nw):
    def plan(refs):
        x, y, c = _pos()
        my_s = 2 * x + y
        out = []
        for w in range(nw):
            for px, py, ps in _chips():
                out.append((refs[w].at[ps], refs[nw + w].at[my_s], (px, py, c), refs[nw + w].at[ps]))
        return out
    return plan


def _plan_join(nw):
    def plan(refs):
        x, y, c = _pos()
        out = []
        for w in range(nw):
            land = refs[nw + w]
            out.append((refs[w], land.at[_half_rows(land, c)], (x, y, 1 - c), land.at[_half_rows(land, 1 - c)]))
        return out
    return plan


def _hbm_empty(shape, dtype):
    return pltpu.with_memory_space_constraint(lax.empty(shape, dtype), pltpu.HBM)


def _put_slot(land, own, slot):
    return lax.dynamic_update_slice(land, own[None], (slot,) + (0,) * own.ndim)


def _pad_lanes(a, n):
    return jnp.pad(a, ((0, 0), (0, n - a.shape[1])))


def kernel(x, c, positions, w_ada, b_ada, norm1_w, w_in, conv_w, conv_b, dt_bias, a_log, d_skip, attn_sinks, ssm_norm_w, w_out, norm2_w, w_gate_up, w_down, final_norm_w, loss_target, m_w_ada, m_b_ada, m_norm1_w, m_w_in, m_conv_w, m_conv_b, m_dt_bias, m_a_log, m_d_skip, m_attn_sinks, m_ssm_norm_w, m_w_out, m_norm2_w, m_w_gate_up, m_w_down, m_final_norm_w, v_w_ada, v_b_ada, v_norm1_w, v_w_in, v_conv_w, v_conv_b, v_dt_bias, v_a_log, v_d_skip, v_attn_sinks, v_ssm_norm_w, v_w_out, v_norm2_w, v_w_gate_up, v_w_down, v_final_norm_w):
    T = x.shape[1]
    tm = min(256, T)
    xi, yi, ci = lax.axis_index("x"), lax.axis_index("y"), lax.axis_index("c")
    my_s = 2 * xi + yi
    xs = x[0]
    tgt = loss_target[0]

    payload = jnp.concatenate([c, conv_w[0].reshape(1, CONVK * 256)], axis=1)
    gat, mod4, tok = _mod_exchange(payload, w_ada[0], b_ada.reshape(4, 1, 1536))
    mod6 = mod4.reshape(6, D)
    c_all = gat[:, 0, 0:D]
    cw_dev = gat[:, 0, D:].reshape(4, 2, CONVK, 256)[:, 0]
    conv_full = cw_dev.transpose(1, 0, 2).reshape(CONVK, CONVC)

    w_in_b = (w_in[0] + tok[0, 0]).astype(BF16)
    s_i, r_i, bufs, tok = _split_start("wgather_in_ici_start", [w_in_b, _hbm_empty((4,) + w_in_b.shape, BF16)], 3,
                                       _plan_gather_ici(1))
    inv_freq = (10000.0 ** (-jnp.arange(32, dtype=F32) / 32))
    inv_row = jnp.tile(inv_freq, 4).reshape(1, 128)
    cos, sin_s = _rope_tables(positions.reshape(T, 1), inv_row + tok[0:1, :], tm)
    bufs = _split_wait("wgather_in_ici_wait", s_i, r_i, bufs, cos, _plan_gather_ici(1))
    s_j, r_j, bufs, tok = _split_start("wgather_in_fwd_start", bufs[1:], 3, _plan_gather_fwd(1))
    bufs = _split_wait("wgather_in_fwd_wait", s_j, r_j, bufs, tok, _plan_gather_fwd(1))
    g_in = _put_slot(bufs[0], w_in_b, my_s)
    w_pad = jnp.concatenate([g_in[0], g_in[1], g_in[2], g_in[3], jnp.zeros((D, IN_PAD - IN_PROJ), BF16)], axis=1)

    late = [(w_out[0] + tok[0, 0]).astype(BF16), w_gate_up[0].astype(BF16), w_down[0].astype(BF16)]
    lands = [_hbm_empty((4,) + s.shape, BF16) for s in late]
    s_a, r_a, bufs, tok = _split_start("wgather_ici_start", late + lands, 9, _plan_gather_ici(3))

    qkv, z, xbc, dtr, h1b = _in_proj_fwd(xs, cos, sin_s, mod6 + tok[0, 0], norm1_w, w_pad, min(512, T))
    sinks = attn_sinks
    attn, lse = _attn_fwd(qkv, sinks)
    bufs = _split_wait("wgather_ici_wait", s_a, r_a, bufs, attn, _plan_gather_ici(3))
    s_b, r_b, lands, tok = _split_start("wgather_fwd_start", bufs[3:], 9, _plan_gather_fwd(3))
    dtb = _pad_lanes(dt_bias, 128)
    alog = _pad_lanes(a_log, 128)
    dskx = jnp.repeat(d_skip, HD, axis=1)
    mats = _ssd_mats()
    ynorm, ypre, states, conv_pre = _ssd_fwd(xbc, z, dtr, conv_full, conv_b, dtb + tok[0, 0], alog, dskx, ssm_norm_w,
                                             mats)
    lands = _split_wait("wgather_fwd_wait", s_b, r_b, lands, ynorm, _plan_gather_fwd(3))
    g_out, g_gu, g_dn = [_put_slot(l, s, my_s) for l, s in zip(lands, late)]
    w_out_f = g_out.reshape(D, D)
    w_dn_f = g_dn.reshape(DFF, D)

    fw2 = final_norm_w.reshape(1, D)
    sq, dmix, dx1, h2b, act, dfb, dgu, dob, sm_ffn = _mix_ffn(
        xs, attn, ynorm, tgt, mod6, norm2_w, fw2, w_out_f, g_gu, w_dn_f, tm)

    tt = min(2048, T)
    c_arr = ci.reshape(1).astype(jnp.int32)
    tok0 = jnp.zeros((8, 128), F32)
    gw_dn4 = _tn_matmul(act, dfb[None], tt, "dw_down", tok0).reshape(4, DFF // 4, D)
    gw_gu4 = _tn_matmul(h2b[None], dgu, tt, "dw_gate_up", tok0)
    gw_out4 = jnp.concatenate(
        [_tn_matmul(attn[None], dob[None], tt, "dw_out_a", tok0)[0],
         _tn_matmul(ynorm[None], dob[None], tt, "dw_out_y", tok0)[0]], axis=0).reshape(4, D // 4, D)
    big1 = [gw_out4, gw_gu4, gw_dn4]
    rbs1 = [128, 128, 176]
    sib1 = [_hbm_empty((4, g.shape[1] // 2, g.shape[2]), F32) for g in big1]
    s_c, r_c, bufs, tok = _split_start("gswap_start", big1 + sib1, 3, _plan_swap(3))

    dzxd, d_cw, d_cb, d_sw, d_sk, d_dtb, d_av = _ssd_bwd(
        xbc, conv_pre, z, dtr, ypre, states, dmix, conv_full, dtb + tok[0, 0], alog, dskx, ssm_norm_w, mats)
    bufs = _split_wait("gswap_wait", s_c, r_c, bufs, dzxd, _plan_swap(3))
    sums1 = [_add_half(g, s, c_arr, rb, "grad_add_%d" % i)
             for i, (g, s, rb) in enumerate(zip(bufs[:3], bufs[3:], rbs1))]
    land1 = [_hbm_empty(p.shape, BF16) for p in sums1]
    s_d, r_d, bufs, tok = _split_start("gscatter_start", sums1 + land1, 9, _plan_scatter(3))
    dqkv, d_sinks = _attn_bwd(qkv, sinks + tok[0:1, 0:8], lse, dmix, cos, sin_s)
    bufs = _split_wait("gscatter_wait", s_d, r_d, bufs, dqkv, _plan_scatter(3))
    slots1 = [_put_slot(l, lax.dynamic_index_in_dim(p, my_s, 0, keepdims=False), my_s)
              for p, l in zip(bufs[:3], bufs[3:])]
    halves1 = [_sum4(r, rb, "grad_sum_%d" % i) for i, (r, rb) in enumerate(zip(slots1, rbs1))]
    full1 = [_hbm_empty((2 * h.shape[0], h.shape[1]), F32) for h in halves1]
    s_e, r_e, bufs, tok = _split_start("gjoin_start", halves1 + full1, 3, _plan_join(3))
    h1_3 = h1b[None]
    gw_in = jnp.concatenate([_tn_matmul(h1_3, dqkv[None], tt, "dw_in_qkv", tok)[0],
                             _tn_matmul(h1_3, dzxd[None], tt, "dw_in_zxd", tok)[0]], axis=1)
    gw_in4 = jnp.stack([gw_in[:, IN_SH * s:IN_SH * (s + 1)] for s in range(4)])
    bufs = _split_wait("gjoin_wait", s_e, r_e, bufs, gw_in4, _plan_join(3))
    g_out_s, g_gu_s, g_dn_s = [lax.dynamic_update_slice(f, h, (ci * h.shape[0], 0)) for h, f in zip(bufs[:3], bufs[3:])]

    sib0 = _hbm_empty((4, D // 2, IN_SH), F32)
    s_f, r_f, bufs, tok = _split_start("gswap_in_start", [gw_in4, sib0], 1, _plan_swap(1))
    bufs = _split_wait("gswap_in_wait", s_f, r_f, bufs, tok, _plan_swap(1))
    sum0 = _add_half(bufs[0], bufs[1], c_arr, 128, "grad_add_in")
    s_g, r_g, bufs, tok = _split_start("gscatter_in_start", [sum0, _hbm_empty(sum0.shape, BF16)], 3, _plan_scatter(1))
    grad_x, sm_in = _in_proj_bwd(xs, dx1, dqkv, dzxd, mod6 + tok[0, 0], norm1_w, w_pad, min(512, T))
    bufs = _split_wait("gscatter_in_wait", s_g, r_g, bufs, grad_x, _plan_scatter(1))
    slot0 = _put_slot(bufs[1], lax.dynamic_index_in_dim(bufs[0], my_s, 0, keepdims=False), my_s)
    half0 = _sum4(slot0, 128, "grad_sum_in")
    s_h, r_h, bufs, tok = _split_start("gjoin_in_start", [half0, _hbm_empty((D, IN_SH), F32)], 1, _plan_join(1))
    bufs = _split_wait("gjoin_in_wait", s_h, r_h, bufs, tok, _plan_join(1))
    g_in_s = lax.dynamic_update_slice(bufs[1], bufs[0], (ci * (D // 2), 0))

    a_neg = -jnp.exp(alog)
    pieces = [sm_in[1:2], sm_in[2:3], sm_ffn[5:6], sm_ffn[2:3], sm_ffn[3:4], sm_ffn[4:5],
              sm_in[0:1], sm_ffn[1:2], sm_ffn[0:1], d_cb, d_cw.reshape(1, CONVK * CONVC),
              _pad_lanes(d_sw, SW), d_dtb, d_av * a_neg, d_sk, d_sinks,
              _pad_lanes((0.5 / D * jnp.sum(sq)).reshape(1, 1), 128)]
    vec = jnp.concatenate(pieces, axis=1)
    tot, allv = _small_reduce(vec)
    o = 0
    offs = []
    for p in pieces:
        offs.append(o)
        o += p.shape[1]
    seg = lambda i, n: tot[:, offs[i]:offs[i] + n]
    g_b_ada = tot[:, 0:6 * D]
    g_norm1, g_norm2, g_final, g_conv_b = seg(6, D), seg(7, D), seg(8, D), seg(9, D)
    g_conv_w = lax.dynamic_slice_in_dim(seg(10, CONVK * CONVC).reshape(CONVK, CONVC), my_s * 256, 256, axis=1)
    g_ssm_w, g_dtb, g_alog, g_dsk, g_sink = seg(11, SW), seg(12, 8), seg(13, 8), seg(14, 8), seg(15, 8)
    loss = tot[0, offs[16]]

    small_names = ["b_ada", "norm1_w", "conv_w", "conv_b", "dt_bias", "a_log", "d_skip", "attn_sinks", "ssm_norm_w",
                   "norm2_w", "final_norm_w"]
    small_g = [g_b_ada, g_norm1, g_conv_w, g_conv_b, g_dtb, g_alog, g_dsk, g_sink, g_ssm_w, g_norm2, g_final]
    as2d = lambda a: a.reshape(-1, a.shape[-1])
    small_w = [as2d(a) for a in (b_ada, norm1_w, conv_w, conv_b, dt_bias, a_log, d_skip, attn_sinks, ssm_norm_w,
                                 norm2_w, final_norm_w)]
    small_m = [as2d(a) for a in (m_b_ada, m_norm1_w, m_conv_w, m_conv_b, m_dt_bias, m_a_log, m_d_skip, m_attn_sinks,
                                 m_ssm_norm_w, m_norm2_w, m_final_norm_w)]
    small_v = [as2d(a) for a in (v_b_ada, v_norm1_w, v_conv_w, v_conv_b, v_dt_bias, v_a_log, v_d_skip, v_attn_sinks,
                                 v_ssm_norm_w, v_norm2_w, v_final_norm_w)]
    small_g, sd, smn, svn = _adam_small(small_g, small_w, small_m, small_v)

    sc_all = c_all * jax.nn.sigmoid(c_all)
    dmod_all = allv[:, 0, 0:6 * D]
    dmod_s = lax.dynamic_slice_in_dim(dmod_all, my_s * 1536, 1536, axis=1)
    g_ada, d_ada, m_ada, v_ada = _adam_w_ada(sc_all, dmod_s, w_ada[0], m_w_ada[0], v_w_ada[0], 256)
    g_in_s, d_in, m_in, v_in = _adam_2d(w_in[0], g_in_s, m_w_in[0], v_w_in[0], 256, "adam_w_in")
    g_out_s, d_out, m_out, v_out = _adam_2d(w_out[0], g_out_s, m_w_out[0], v_w_out[0], 256, "adam_w_out")
    g_gu_s, d_gu, m_gu, v_gu = _adam_2d(w_gate_up[0], g_gu_s, m_w_gate_up[0], v_w_gate_up[0], 256, "adam_w_gate_up")
    g_dn_s, d_dn, m_dn, v_dn = _adam_2d(w_down[0], g_dn_s, m_w_down[0], v_w_down[0], 352, "adam_w_down")

    order = ["w_ada", "b_ada", "norm1_w", "w_in", "conv_w", "conv_b", "dt_bias", "a_log", "d_skip", "attn_sinks",
             "ssm_norm_w", "w_out", "norm2_w", "w_gate_up", "w_down", "final_norm_w"]
    shapes = dict(w_ada=w_ada.shape, b_ada=b_ada.shape, norm1_w=norm1_w.shape, w_in=w_in.shape, conv_w=conv_w.shape,
                  conv_b=conv_b.shape, dt_bias=dt_bias.shape, a_log=a_log.shape, d_skip=d_skip.shape,
                  attn_sinks=attn_sinks.shape, ssm_norm_w=ssm_norm_w.shape, w_out=w_out.shape, norm2_w=norm2_w.shape,
                  w_gate_up=w_gate_up.shape, w_down=w_down.shape, final_norm_w=final_norm_w.shape)
    grads = dict(w_ada=g_ada, w_in=g_in_s, w_out=g_out_s, w_gate_up=g_gu_s, w_down=g_dn_s)
    deltas = dict(w_ada=d_ada, w_in=d_in, w_out=d_out, w_gate_up=d_gu, w_down=d_dn)
    new_m = dict(w_ada=m_ada, w_in=m_in, w_out=m_out, w_gate_up=m_gu, w_down=m_dn)
    new_v = dict(w_ada=v_ada, w_in=v_in, w_out=v_out, w_gate_up=v_gu, w_down=v_dn)
    for i, nme in enumerate(small_names):
        grads[nme], deltas[nme], new_m[nme], new_v[nme] = small_g[i], sd[i], smn[i], svn[i]
    outs = [loss, grad_x[None]]
    for table in (grads, deltas, new_m, new_v):
        outs += [table[nme].reshape(shapes[nme]) for nme in order]
    return tuple(outs)
```

```python
import functools
import math

import jax
import jax.numpy as jnp
from jax import lax
from jax.experimental import pallas as pl
from jax.experimental.pallas import tpu as pltpu

F32 = jnp.float32
BF16 = jnp.bfloat16
HI = lax.Precision.HIGHEST
MESH = pl.DeviceIdType.MESH

D = 1024
HD = 64
NQ = 8
AW = 512
KVW = 128
SW = 512
NST = 128
CONVK = 4
CONVC = 1024
BLK = 128
IN_PROJ = 2312
IN_PAD = 2432
IN_SH = IN_PROJ // 4
IN_SH_PAD = 608
DFF = 2816
GU_SH = 1408
FF_SPLITS = ((0, 1536), (1536, 2816))
EPS = 1e-6
NEG = -1e30
LR, B1, B2, AEPS, WD, STEP = 0.001, 0.9, 0.999, 1e-08, 0.01, 10
VMEM_LIMIT = 58 * 1024 * 1024


def _cp(*sem):
    return pltpu.CompilerParams(dimension_semantics=sem or None, vmem_limit_bytes=VMEM_LIMIT)


def _dot(a, b):
    return jnp.dot(a, b, preferred_element_type=F32)


def _dot_nt(a, b):
    return lax.dot_general(a, b, (((1,), (1,)), ((), ())), preferred_element_type=F32)


def _dot_tn(a, b):
    return lax.dot_general(a, b, (((0,), (0,)), ((), ())), preferred_element_type=F32)


def _dot_hi(a, b):
    return jnp.dot(a, b, precision=HI, preferred_element_type=F32)


def _sigmoid(x):
    return 1.0 / (1.0 + jnp.exp(-x))


def _iota(shape, dim):
    return lax.broadcasted_iota(jnp.int32, shape, dim)


def _load_resident(hbm_ref, vmem_ref, sem):
    @pl.when(pl.program_id(0) == 0)
    def _():
        cp = pltpu.make_async_copy(hbm_ref, vmem_ref, sem)
        cp.start()
        cp.wait()


def _swap32(t):
    lane = _iota(t.shape, 1)
    return jnp.where((lane & 63) < 32, pltpu.roll(t, 96, 1), pltpu.roll(t, 32, 1))


def _rope_fwd(t, cos, sin_s):
    return t * cos + _swap32(t) * sin_s


def _rope_bwd(t, cos, sin_s):
    return t * cos - _swap32(t) * sin_s


def _rope_tables(pos_col, inv_freq_row, tm):
    T = pos_col.shape[0]

    def body(p_ref, f_ref, cos_ref, sin_ref):
        ang = p_ref[...].astype(F32) * f_ref[...]
        lane = _iota((tm, 128), 1)
        s = jnp.sin(ang)
        cos_ref[...] = jnp.cos(ang)
        sin_ref[...] = jnp.where((lane & 63) < 32, -s, s)

    return pl.pallas_call(
        body, name="rope_tables", grid=(T // tm,),
        in_specs=[pl.BlockSpec((tm, 1), lambda i: (i, 0)), pl.BlockSpec((1, 128), lambda i: (0, 0))],
        out_specs=[pl.BlockSpec((tm, 128), lambda i: (i, 0))] * 2,
        out_shape=[jax.ShapeDtypeStruct((T, 128), F32)] * 2,
        compiler_params=_cp("parallel"),
    )(pos_col, inv_freq_row)


def _in_proj_fwd(x, cos, sin_s, mod6, norm1_w, w_pad, tm):
    T = x.shape[0]

    def body(x_ref, cos_ref, sin_ref, mod_ref, nw_ref, w_hbm, qkv_ref, z_ref, xbc_ref, dt_ref, h_ref, w_vmem, sem):
        _load_resident(w_hbm, w_vmem, sem)
        xv = x_ref[...]
        r = lax.rsqrt(jnp.mean(xv * xv, axis=-1, keepdims=True) + EPS)
        h = (xv * r * nw_ref[...]) * (1.0 + mod_ref[1:2, :]) + mod_ref[0:1, :]
        hb = h.astype(BF16)
        h_ref[...] = hb
        proj = _dot(hb, w_vmem[...])
        cs, sn = cos_ref[...], sin_ref[...]
        for j in range(5):
            qkv_ref[:, 128 * j:128 * (j + 1)] = _rope_fwd(proj[:, 128 * j:128 * (j + 1)], cs, sn).astype(BF16)
        qkv_ref[:, 640:768] = proj[:, 640:768].astype(BF16)
        z_ref[...] = proj[:, 768:1280]
        xbc_ref[...] = proj[:, 1280:2304]
        dt_ref[...] = proj[:, 2304:2432]

    row = lambda w: pl.BlockSpec((tm, w), lambda i: (i, 0))
    full = lambda a: pl.BlockSpec(a.shape, lambda i: (0,) * a.ndim)
    return pl.pallas_call(
        body, name="in_proj_fwd", grid=(T // tm,),
        in_specs=[row(D), row(128), row(128), full(mod6), full(norm1_w), pl.BlockSpec(memory_space=pl.ANY)],
        out_specs=[row(768), row(512), row(1024), row(128), row(D)],
        out_shape=[jax.ShapeDtypeStruct((T, 768), BF16), jax.ShapeDtypeStruct((T, 512), F32),
                   jax.ShapeDtypeStruct((T, 1024), F32), jax.ShapeDtypeStruct((T, 128), F32),
                   jax.ShapeDtypeStruct((T, D), BF16)],
        scratch_shapes=[pltpu.VMEM((D, IN_PAD), BF16), pltpu.SemaphoreType.DMA],
        compiler_params=_cp("arbitrary"),
    )(x, cos, sin_s, mod6, norm1_w, w_pad)


def _head_variants(pair, j):
    lane = _iota(pair.shape, 1)
    lo = lane < 64
    kv = j // 2
    ev = jnp.where(lo, pair, 0.0)
    od = jnp.where(lo, 0.0, pair)
    if kv == 0:
        od = pltpu.roll(od, 64, 1)
    else:
        ev = pltpu.roll(ev, 64, 1)
    return ev.astype(BF16), od.astype(BF16)


def _kv_variants(vcat):
    lane = _iota(vcat.shape, 1)
    lo = lane < 64
    v0 = jnp.where(lo, vcat, 0.0)
    v1 = jnp.where(lo, 0.0, vcat)
    out = {
        (0, 0): v0, (0, 1): pltpu.roll(v0, 64, 1),
        (1, 0): pltpu.roll(v1, 64, 1), (1, 1): v1,
    }
    return {k: v.astype(BF16) for k, v in out.items()}


def _fold_masks(n):
    upper = _iota((BLK, BLK), 1) > _iota((BLK, BLK), 0)
    return upper, upper & (n == 0)


def _attn_fwd(qkv, sinks):
    T = qkv.shape[0]
    nb = T // BLK

    def body(sink_ref, q_ref, kc_ref, kp_ref, vc_ref, vp_ref, o_ref, lse_ref):
        n = pl.program_id(0)
        vpv = _kv_variants(vp_ref[...].astype(F32))
        vcv = _kv_variants(vc_ref[...].astype(F32))
        q_all = jnp.concatenate(
            [v for j in range(4) for v in _head_variants(q_ref[:, 128 * j:128 * (j + 1)].astype(F32), j)], axis=0)
        s_prev = _dot_nt(q_all, kp_ref[...])
        s_cur = _dot_nt(q_all, kc_ref[...])
        upper, dead = _fold_masks(n)
        lane = _iota((BLK, 128), 1)
        lse_acc = jnp.zeros((BLK, 128), F32)
        for jj in range(4):
            acc = jnp.zeros((BLK, 128), F32)
            for par in range(2):
                h = 2 * jj + par
                rows = slice(h * BLK, (h + 1) * BLK)
                sink = sink_ref[0, h]
                s = jnp.where(dead, NEG, jnp.where(upper, s_prev[rows], s_cur[rows]) * 0.125)
                m = jnp.maximum(jnp.max(s, axis=1, keepdims=True), sink)
                p = jnp.exp(s - m)
                den = jnp.sum(p, axis=1, keepdims=True) + jnp.exp(sink - m)
                pn = p * (1.0 / den)
                acc = (acc + _dot(jnp.where(upper, pn, 0.0).astype(BF16), vpv[(jj // 2, par)])
                       + _dot(jnp.where(upper, 0.0, pn).astype(BF16), vcv[(jj // 2, par)]))
                lse_acc = jnp.where(lane == h, m + jnp.log(den), lse_acc)
            o_ref[:, 128 * jj:128 * (jj + 1)] = acc.astype(BF16)
        lse_ref[...] = lse_acc

    prev = lambda n: jnp.maximum(n - 1, 0)
    return pl.pallas_call(
        body, name="attn_fwd", grid=(nb,),
        in_specs=[pl.BlockSpec(memory_space=pltpu.SMEM),
                  pl.BlockSpec((BLK, 512), lambda n: (n, 0)),
                  pl.BlockSpec((BLK, 128), lambda n: (n, 4)),
                  pl.BlockSpec((BLK, 128), lambda n: (prev(n), 4)),
                  pl.BlockSpec((BLK, 128), lambda n: (n, 5)),
                  pl.BlockSpec((BLK, 128), lambda n: (prev(n), 5))],
        out_specs=[pl.BlockSpec((BLK, 512), lambda n: (n, 0)), pl.BlockSpec((BLK, 128), lambda n: (n, 0))],
        out_shape=[jax.ShapeDtypeStruct((T, 512), BF16), jax.ShapeDtypeStruct((T, 128), F32)],
        compiler_params=_cp("parallel"),
    )(sinks, qkv, qkv, qkv, qkv, qkv)


def _attn_bwd(qkv, sinks, lse, dmix, cos, sin_s):
    T = qkv.shape[0]
    nb = T // BLK

    def body(sink_ref, q_ref, kc_ref, kp_ref, vc_ref, vp_ref, lse_ref, do_ref, cq_ref, sq_ref, ck_ref, sk_ref,
             out_ref, ds_ref, dq_car, dk_car, dv_car):
        n = pl.program_id(0)
        lane = _iota((BLK, 128), 1)

        @pl.when(n == 0)
        def _():
            ds_ref[...] = jnp.zeros_like(ds_ref)
            dq_car[...] = jnp.zeros_like(dq_car)
            dk_car[...] = jnp.zeros_like(dk_car)
            dv_car[...] = jnp.zeros_like(dv_car)

        @pl.when(n < nb)
        def _():
            kp, kc, vp, vc = kp_ref[...], kc_ref[...], vp_ref[...], vc_ref[...]
            kpv = _kv_variants(kp.astype(F32))
            kcv = _kv_variants(kc.astype(F32))
            lse_v = lse_ref[...]
            q_all = jnp.concatenate(
                [v for j in range(4) for v in _head_variants(q_ref[:, 128 * j:128 * (j + 1)].astype(F32), j)], axis=0)
            do_all = jnp.concatenate(
                [v for j in range(4) for v in _head_variants(do_ref[:, 128 * j:128 * (j + 1)], j)], axis=0)
            s_prev, s_cur = _dot_nt(q_all, kp), _dot_nt(q_all, kc)
            dp_prev, dp_cur = _dot_nt(do_all, vp), _dot_nt(do_all, vc)
            upper, dead = _fold_masks(n)
            out_ref[:, 0:512] = dq_car[...]
            dsk = jnp.zeros((1, 128), F32)
            ds_u, ds_l, p_u, p_l = [], [], [], []
            for jj in range(4):
                dq_acc = jnp.zeros((BLK, 128), F32)
                for par in range(2):
                    h = 2 * jj + par
                    rows = slice(h * BLK, (h + 1) * BLK)
                    lse_h = jnp.sum(jnp.where(lane == h, lse_v, 0.0), axis=1, keepdims=True)
                    s = jnp.where(dead, NEG, jnp.where(upper, s_prev[rows], s_cur[rows]) * 0.125)
                    p = jnp.exp(s - lse_h)
                    dp = jnp.where(upper, dp_prev[rows], dp_cur[rows])
                    delta = jnp.sum(p * dp, axis=1, keepdims=True)
                    ds = p * (dp - delta) * 0.125
                    dsu, dsl = jnp.where(upper, ds, 0.0).astype(BF16), jnp.where(upper, 0.0, ds).astype(BF16)
                    dq_acc = dq_acc + _dot(dsu, kpv[(jj // 2, par)]) + _dot(dsl, kcv[(jj // 2, par)])
                    ds_u.append(dsu)
                    ds_l.append(dsl)
                    p_u.append(jnp.where(upper, p, 0.0).astype(BF16))
                    p_l.append(jnp.where(upper, 0.0, p).astype(BF16))
                    dsk = dsk + jnp.where(lane[0:1] == h, -jnp.sum(jnp.exp(sink_ref[0, h] - lse_h) * delta), 0.0)
                dq_car[:, 128 * jj:128 * (jj + 1)] = _rope_bwd(dq_acc, cq_ref[...], sq_ref[...]).astype(BF16)
            stack = lambda parts: jnp.concatenate(parts, axis=0)
            dk_prev, dk_cur = _dot_tn(stack(ds_u), q_all), _dot_tn(stack(ds_l), q_all)
            dv_prev, dv_cur = _dot_tn(stack(p_u), do_all), _dot_tn(stack(p_l), do_all)
            ds_ref[...] += dsk
            out_ref[:, 512:640] = _rope_bwd(dk_car[...] + dk_prev, ck_ref[...], sk_ref[...]).astype(BF16)
            out_ref[:, 640:768] = (dv_car[...] + dv_prev).astype(BF16)
            dk_car[...] = dk_cur
            dv_car[...] = dv_cur

        @pl.when(n == nb)
        def _():
            out_ref[:, 0:512] = dq_car[...]
            out_ref[:, 512:640] = _rope_bwd(dk_car[...], ck_ref[...], sk_ref[...]).astype(BF16)
            out_ref[:, 640:768] = dv_car[...].astype(BF16)

    cur = lambda n: jnp.minimum(n, nb - 1)
    prev = lambda n: jnp.maximum(cur(n) - 1, 0)
    outb = lambda n: jnp.maximum(n - 1, 0)
    return pl.pallas_call(
        body, name="attn_bwd", grid=(nb + 1,),
        in_specs=[pl.BlockSpec(memory_space=pltpu.SMEM),
                  pl.BlockSpec((BLK, 512), lambda n: (cur(n), 0)),
                  pl.BlockSpec((BLK, 128), lambda n: (cur(n), 4)),
                  pl.BlockSpec((BLK, 128), lambda n: (prev(n), 4)),
                  pl.BlockSpec((BLK, 128), lambda n: (cur(n), 5)),
                  pl.BlockSpec((BLK, 128), lambda n: (prev(n), 5)),
                  pl.BlockSpec((BLK, 128), lambda n: (cur(n), 0)),
                  pl.BlockSpec((BLK, 512), lambda n: (cur(n), 0)),
                  pl.BlockSpec((BLK, 128), lambda n: (cur(n), 0)),
                  pl.BlockSpec((BLK, 128), lambda n: (cur(n), 0)),
                  pl.BlockSpec((BLK, 128), lambda n: (outb(n), 0)),
                  pl.BlockSpec((BLK, 128), lambda n: (outb(n), 0))],
        out_specs=[pl.BlockSpec((BLK, 768), lambda n: (outb(n), 0)), pl.BlockSpec((1, 128), lambda n: (0, 0))],
        out_shape=[jax.ShapeDtypeStruct((T, 768), BF16), jax.ShapeDtypeStruct((1, 128), F32)],
        scratch_shapes=[pltpu.VMEM((BLK, 512), BF16), pltpu.VMEM((BLK, 128), F32), pltpu.VMEM((BLK, 128), F32)],
        compiler_params=_cp("arbitrary"),
    )(sinks, qkv, qkv, qkv, qkv, qkv, lse, dmix, cos, sin_s, cos, sin_s)


def _ssd_mats():
    e = jnp.arange(SW)[None, :] // HD == jnp.arange(128)[:, None]
    tri = jnp.arange(BLK)[None, :] <= jnp.arange(BLK)[:, None]
    return (jnp.tile(e, (3, 1)).astype(BF16), jnp.tile(e.T, (2, 1)).astype(BF16),
            jnp.tile(tri, (1, 3)).astype(BF16), jnp.tile(tri.T, (1, 3)).astype(BF16))


def _pieces(x, n, axis):
    out, r = [], x
    for i in range(n):
        p = r.astype(BF16)
        out.append(p)
        if i + 1 < n:
            r = r - p.astype(F32)
    return jnp.concatenate(out, axis=axis)


def _expand(x, e3):
    return _dot(_pieces(x, 3, 1), e3)


def _head_sums(x, et2):
    return _dot(_pieces(x, 2, 1), et2)


def _run_sum(tri3, x):
    return _dot(tri3, _pieces(x, 3, 0))


def _shift_down(u, tail, j):
    rolled = pltpu.roll(u, j, 0)
    first = jnp.where(_iota(tail.shape, 0) < j, pltpu.roll(tail, j, 0), rolled[0:8])
    return jnp.concatenate([first, rolled[8:]], axis=0)


def _shift_up(d, head, j):
    rolled = pltpu.roll(d, BLK - j, 0)
    last = jnp.where(_iota(head.shape, 0) >= 8 - j, pltpu.roll(head, 8 - j, 0), rolled[BLK - 8:])
    return jnp.concatenate([rolled[:BLK - 8], last], axis=0)


def _ssd_parts(dtr, dtb, alog, e3, tril3):
    xx = dtr + dtb
    dt = jnp.maximum(xx, 0.0) + jnp.log(1.0 + jnp.exp(-jnp.abs(xx)))
    a_neg = -jnp.exp(alog)
    tril = _iota((BLK, BLK), 1) <= _iota((BLK, BLK), 0)
    cs = _run_sum(tril3, dt * a_neg)
    csx = _expand(cs, e3)
    last = csx[BLK - 1:BLK, :]
    return dict(xx=xx, dt=dt, a_neg=a_neg, tril=tril, cs=cs, cs_t=cs.T,
                ecsx=jnp.exp(csx), dtex=jnp.exp(last - csx), cdx=jnp.exp(last), dtx=_expand(dt, e3))


def _decay(parts, h):
    seg = parts["cs"][:, h:h + 1] - parts["cs_t"][h:h + 1, :]
    return jnp.exp(jnp.where(parts["tril"], seg, NEG))


def _group_cols(a, g):
    return a[:, 256 * g:256 * (g + 1)]


def _ssd_fwd(xbc, z, dtr, conv_w, conv_b, dtb, alog, dskx, ssm_w, mats):
    T = xbc.shape[0]
    nc = T // BLK

    def body(u_ref, tail_ref, z_ref, dtr_ref, cw_ref, cb_ref, dtb_ref, al_ref, dk_ref, sw_ref, e3_ref, tril3_ref,
             yn_ref, yp_ref, st_ref, co_ref, s_scr):
        n = pl.program_id(0)

        @pl.when(n == 0)
        def _():
            s_scr[...] = jnp.zeros_like(s_scr)

        u = u_ref[...]
        tail = jnp.where(n > 0, tail_ref[...], 0.0)
        co = cb_ref[...] + cw_ref[3:4, :] * u
        for j in range(1, CONVK):
            co = co + cw_ref[3 - j:4 - j, :] * _shift_down(u, tail, j)
        co_ref[...] = co
        xc = co * _sigmoid(co)
        pt = _ssd_parts(dtr_ref[...], dtb_ref[...], al_ref[...], e3_ref[...], tril3_ref[...])
        xs = xc[:, :SW]
        bm = [xc[:, 512:640].astype(BF16), xc[:, 640:768].astype(BF16)]
        cm = [xc[:, 768:896].astype(BF16), xc[:, 896:1024].astype(BF16)]
        s_in = s_scr[...]
        st_ref[0] = s_in
        xdt = xs * pt["dtx"]
        xde = (xdt * pt["dtex"]).astype(BF16)
        lane = _iota((BLK, 128), 1)
        lo = lane < 64
        ys, s_new = [], []
        for g in range(2):
            cb = _dot_nt(cm[g], bm[g])
            yoff = _dot(cm[g], _group_cols(s_in, g).astype(BF16))
            s_new.append(_dot_tn(bm[g], _group_cols(xde, g)))
            for jj in range(2):
                j = 2 * g + jj
                chunk = xdt[:, 128 * j:128 * (j + 1)]
                g_ev = (cb * _decay(pt, 2 * j)).astype(BF16)
                g_od = (cb * _decay(pt, 2 * j + 1)).astype(BF16)
                yd = _dot(g_ev, jnp.where(lo, chunk, 0.0).astype(BF16)) + _dot(g_od, jnp.where(lo, 0.0, chunk).astype(BF16))
                ys.append(yd + yoff[:, 128 * jj:128 * (jj + 1)] * pt["ecsx"][:, 128 * j:128 * (j + 1)])
        y = jnp.concatenate(ys, axis=1) + xs * dk_ref[...]
        s_scr[...] = s_in * pt["cdx"] + jnp.concatenate(s_new, axis=1)
        yp_ref[...] = y
        zv = z_ref[...]
        yz = y * (zv * _sigmoid(zv))
        outs = []
        for g in range(2):
            yg = _group_cols(yz, g)
            outs.append(yg * lax.rsqrt(jnp.mean(yg * yg, axis=-1, keepdims=True) + EPS))
        yn_ref[...] = (jnp.concatenate(outs, axis=1) * sw_ref[...]).astype(BF16)

    e3, _, tril3, _ = mats
    tail8 = lambda n: jnp.maximum(n * (BLK // 8) - 1, 0)
    full = lambda a: pl.BlockSpec(a.shape, lambda n: (0,) * a.ndim)
    return pl.pallas_call(
        body, name="ssd_fwd", grid=(nc,),
        in_specs=[pl.BlockSpec((BLK, CONVC), lambda n: (n, 0)), pl.BlockSpec((8, CONVC), lambda n: (tail8(n), 0)),
                  pl.BlockSpec((BLK, SW), lambda n: (n, 0)), pl.BlockSpec((BLK, 128), lambda n: (n, 0)),
                  full(conv_w), full(conv_b), full(dtb), full(alog), full(dskx), full(ssm_w), full(e3), full(tril3)],
        out_specs=[pl.BlockSpec((BLK, SW), lambda n: (n, 0)), pl.BlockSpec((BLK, SW), lambda n: (n, 0)),
                   pl.BlockSpec((1, NST, SW), lambda n: (n, 0, 0)), pl.BlockSpec((BLK, CONVC), lambda n: (n, 0))],
        out_shape=[jax.ShapeDtypeStruct((T, SW), BF16), jax.ShapeDtypeStruct((T, SW), F32),
                   jax.ShapeDtypeStruct((nc, NST, SW), F32), jax.ShapeDtypeStruct((T, CONVC), F32)],
        scratch_shapes=[pltpu.VMEM((NST, SW), F32)],
        compiler_params=_cp("arbitrary"),
    )(xbc, xbc, z, dtr, conv_w, conv_b, dtb, alog, dskx, ssm_w, e3, tril3)


def _ssd_bwd(xbc, co_all, z, dtr, ypre, states, dmix, conv_w, dtb, alog, dskx, ssm_w, mats):
    T = xbc.shape[0]
    nc = T // BLK

    def body(u_ref, co_ref, z_ref, dtr_ref, yp_ref, st_ref, dyn_ref, cw_ref, dtb_ref, al_ref, dk_ref, sw_ref,
             e3_ref, et2_ref, tril3_ref, triu3_ref,
             out_ref, dcw_ref, dcb_ref, dsw_ref, dsk_ref, ddtb_ref, dav_ref, ds_scr, dco_scr, dskx_scr):
        i = pl.program_id(0)

        @pl.when(i == 0)
        def _():
            for r in (dcw_ref, dcb_ref, dsw_ref, dsk_ref, ddtb_ref, dav_ref, ds_scr, dco_scr, dskx_scr):
                r[...] = jnp.zeros_like(r)

        co = co_ref[...]
        sg = _sigmoid(co)
        xc = co * sg
        pt = _ssd_parts(dtr_ref[...], dtb_ref[...], al_ref[...], e3_ref[...], tril3_ref[...])
        dtx, ecsx, dtex, cdx = pt["dtx"], pt["ecsx"], pt["dtex"], pt["cdx"]
        xs = xc[:, :SW]
        bm = [xc[:, 512:640].astype(BF16), xc[:, 640:768].astype(BF16)]
        cm = [xc[:, 768:896].astype(BF16), xc[:, 896:1024].astype(BF16)]
        s_in = st_ref[0]
        ds_out = ds_scr[...]
        e_t = et2_ref[...]

        zv = z_ref[...]
        sz = _sigmoid(zv)
        silu_z = zv * sz
        ypre = yp_ref[...]
        yz = ypre * silu_z
        dyn = dyn_ref[...]
        sw = sw_ref[...]
        dyz, yns = [], []
        for g in range(2):
            yg = _group_cols(yz, g)
            r = lax.rsqrt(jnp.mean(yg * yg, axis=-1, keepdims=True) + EPS)
            yn = yg * r
            dg = _group_cols(dyn, g) * _group_cols(sw, g)
            dyz.append(r * (dg - yn * jnp.mean(dg * yn, axis=-1, keepdims=True)))
            yns.append(yn)
        dyz = jnp.concatenate(dyz, axis=1)
        dsw_ref[...] += jnp.sum(dyn * jnp.concatenate(yns, axis=1), axis=0, keepdims=True)
        dy = dyz * silu_z
        dz = dyz * ypre * (sz * (1.0 + zv * (1.0 - sz)))

        xdt = xs * dtx
        xdt_b = xdt.astype(BF16)
        edy = (ecsx * dy).astype(BF16)
        xde = (xdt * dtex).astype(BF16)
        lane = _iota((BLK, 128), 1)
        lo = lane < 64
        row8 = _iota((8, 128), 0)
        dcs = jnp.zeros((BLK, 128), F32)
        col_rows = jnp.zeros((8, 128), F32)
        dxdt, bds, yoff, dbs, dcs_g, ds_new = [], [], [], [], [], []
        for g in range(2):
            s_g = _group_cols(s_in, g).astype(BF16)
            dso_g = _group_cols(ds_out, g).astype(BF16)
            cb = _dot_nt(cm[g], bm[g])
            bds.append(_dot(bm[g], dso_g))
            yoff.append(_dot(cm[g], s_g))
            dcb_g = jnp.zeros((BLK, BLK), F32)
            for jj in range(2):
                j = 2 * g + jj
                dy_c = dy[:, 128 * j:128 * (j + 1)]
                xdt_c = xdt_b[:, 128 * j:128 * (j + 1)]
                acc = jnp.zeros((BLK, 128), F32)
                for par in range(2):
                    h = 2 * j + par
                    lm = _decay(pt, h)
                    gm = cb * lm
                    dy_m = (jnp.where(lo, dy_c, 0.0) if par == 0 else jnp.where(lo, 0.0, dy_c)).astype(BF16)
                    dg_h = _dot_nt(dy_m, xdt_c)
                    w_h = dg_h * gm
                    dcs = dcs + jnp.where(lane == h, jnp.sum(w_h, axis=1, keepdims=True), 0.0)
                    col_rows = col_rows + jnp.where(row8 == h, jnp.sum(w_h, axis=0, keepdims=True), 0.0)
                    dcb_g = dcb_g + dg_h * lm
                    acc = acc + _dot_tn(gm.astype(BF16), dy_m)
                dxdt.append(acc)
            dcb_b = dcb_g.astype(BF16)
            dcs_g.append(_dot(dcb_b, bm[g]) + _dot_nt(_group_cols(edy, g), s_g))
            dbs.append(_dot_tn(dcb_b, cm[g]) + _dot_nt(_group_cols(xde, g), dso_g))
            ds_new.append(_dot_tn(cm[g], _group_cols(edy, g)))
        bds = jnp.concatenate(bds, axis=1)
        yoff = jnp.concatenate(yoff, axis=1) * ecsx
        dxdt = jnp.concatenate(dxdt, axis=1) + dtex * bds
        ds_scr[...] = cdx * ds_out + jnp.concatenate(ds_new, axis=1)

        t_m = _head_sums(dtex * xdt * bds, e_t)
        colsum_t = jnp.concatenate([col_rows, jnp.zeros((BLK - 8, 128), F32)], axis=0).T
        cd = jnp.exp(pt["cs"][BLK - 1:BLK, :])
        sds = jnp.sum(s_in * ds_out, axis=0, keepdims=True)
        last_row = jnp.sum(t_m, axis=0, keepdims=True) + cd * _head_sums(jnp.broadcast_to(sds, (8, SW)), e_t)[0:1]
        dcs = dcs - colsum_t + _head_sums(dy * yoff, e_t) - t_m
        dcs = dcs + jnp.where(_iota((BLK, 128), 0) == BLK - 1, last_row, 0.0)
        da = _run_sum(triu3_ref[...], dcs)
        dt = pt["dt"]
        ddt = da * pt["a_neg"] + _head_sums(dxdt * xs, e_t)
        dav_ref[...] += jnp.sum(da * dt, axis=0, keepdims=True)
        ddtr = ddt * _sigmoid(pt["xx"])
        ddtb_ref[...] += jnp.sum(ddtr, axis=0, keepdims=True)
        dxs = dxdt * dtx + dy * dk_ref[...]
        dskx_scr[...] += jnp.sum(dy * xs, axis=0, keepdims=True)
        dxc = jnp.concatenate([dxs, dbs[0], dbs[1], dcs_g[0], dcs_g[1]], axis=1)
        dco = dxc * (sg * (1.0 + co * (1.0 - sg)))

        dcb_ref[...] += jnp.sum(dco, axis=0, keepdims=True)
        u = u_ref[...]
        head = dco_scr[...]
        du = jnp.zeros_like(dco)
        for j in range(CONVK):
            up_j = dco if j == 0 else _shift_up(dco, head, j)
            dcw_ref[3 - j:4 - j, :] += jnp.sum(up_j * u, axis=0, keepdims=True)
            du = du + cw_ref[3 - j:4 - j, :] * up_j
        dco_scr[...] = dco[0:8]
        out_ref[:, 0:512] = dz.astype(BF16)
        out_ref[:, 512:1536] = du.astype(BF16)
        out_ref[:, 1536:1664] = ddtr.astype(BF16)

        @pl.when(i == nc - 1)
        def _():
            dsk_ref[...] = _head_sums(jnp.broadcast_to(dskx_scr[...], (8, SW)), e_t)[0:1]

    e3, et2, tril3, triu3 = mats
    rev = lambda i: nc - 1 - i
    full = lambda a: pl.BlockSpec(a.shape, lambda i: (0,) * a.ndim)
    acc = lambda r, c: pl.BlockSpec((r, c), lambda i: (0, 0))
    return pl.pallas_call(
        body, name="ssd_bwd", grid=(nc,),
        in_specs=[pl.BlockSpec((BLK, CONVC), lambda i: (rev(i), 0)), pl.BlockSpec((BLK, CONVC), lambda i: (rev(i), 0)),
                  pl.BlockSpec((BLK, SW), lambda i: (rev(i), 0)), pl.BlockSpec((BLK, 128), lambda i: (rev(i), 0)),
                  pl.BlockSpec((BLK, SW), lambda i: (rev(i), 0)), pl.BlockSpec((1, NST, SW), lambda i: (rev(i), 0, 0)),
                  pl.BlockSpec((BLK, SW), lambda i: (rev(i), 1)),
                  full(conv_w), full(dtb), full(alog), full(dskx), full(ssm_w),
                  full(e3), full(et2), full(tril3), full(triu3)],
        out_specs=[pl.BlockSpec((BLK, 1664), lambda i: (rev(i), 0)),
                   acc(CONVK, CONVC), acc(1, CONVC), acc(1, SW), acc(1, 128), acc(1, 128), acc(1, 128)],
        out_shape=[jax.ShapeDtypeStruct((T, 1664), BF16),
                   jax.ShapeDtypeStruct((CONVK, CONVC), F32), jax.ShapeDtypeStruct((1, CONVC), F32),
                   jax.ShapeDtypeStruct((1, SW), F32), jax.ShapeDtypeStruct((1, 128), F32),
                   jax.ShapeDtypeStruct((1, 128), F32), jax.ShapeDtypeStruct((1, 128), F32)],
        scratch_shapes=[pltpu.VMEM((NST, SW), F32), pltpu.VMEM((8, CONVC), F32), pltpu.VMEM((1, SW), F32)],
        compiler_params=_cp("arbitrary"),
    )(xbc, co_all, z, dtr, ypre, states, dmix, conv_w, dtb, alog, dskx, ssm_w, e3, et2, tril3, triu3)


def _mix_ffn(x, attn, ynorm, tgt, mod6, norm2_w, final_w, w_out, w_gu, w_dn, tm):
    T = x.shape[0]
    nt = T // tm

    def body(x_ref, a_ref, y_ref, t_ref, mod_ref, n2_ref, fw_ref, wo_hbm, wgu_hbm, wdn_hbm,
             sq_ref, dmix_ref, dx1_ref, h2_ref, act_ref, df_ref, dgu_ref, do_ref, sm_ref,
             wo, wgu, wdn, sems):
        i = pl.program_id(0)

        @pl.when(i == 0)
        def _():
            cps = [pltpu.make_async_copy(s, d, sems.at[k]) for k, (s, d) in
                   enumerate(((wo_hbm, wo), (wgu_hbm, wgu), (wdn_hbm, wdn)))]
            for c in cps:
                c.start()
            for c in cps:
                c.wait()
            sq_ref[...] = jnp.zeros_like(sq_ref)
            sm_ref[...] = jnp.zeros_like(sm_ref)

        gate1, shift2, scale2, gate2 = mod_ref[2:3, :], mod_ref[3:4, :], mod_ref[4:5, :], mod_ref[5:6, :]
        n2w, fw = n2_ref[...], fw_ref[...]
        o = _dot(a_ref[...], wo[0:AW, :]) + _dot(y_ref[...], wo[AW:D, :])
        x1 = x_ref[...] + gate1 * o
        r2 = lax.rsqrt(jnp.mean(x1 * x1, axis=-1, keepdims=True) + EPS)
        xh2 = x1 * r2
        n2 = xh2 * n2w
        h2b = (n2 * (1.0 + scale2) + shift2).astype(BF16)
        h2_ref[...] = h2b
        f = jnp.zeros((tm, D), F32)
        saved = []
        for a, b in FF_SPLITS:
            gp = _dot(h2b, wgu[:, a:b])
            upj = _dot(h2b, wgu[:, DFF + a:DFF + b])
            sg = _sigmoid(gp)
            sl = gp * sg
            actb = (sl * upj).astype(BF16)
            act_ref[:, a:b] = actb
            f = f + _dot(actb, wdn[a:b, :])
            saved.append((gp, upj, sg, sl))
        x2 = x1 + gate2 * f
        r3 = lax.rsqrt(jnp.mean(x2 * x2, axis=-1, keepdims=True) + EPS)
        xh3 = x2 * r3
        err = xh3 * fw - t_ref[...]
        sq_ref[...] += jnp.sum(err * err, axis=0, keepdims=True)
        dy = err * (1.0 / D)
        dfw = jnp.sum(dy * xh3, axis=0, keepdims=True)
        dxh3 = dy * fw
        dx2 = r3 * (dxh3 - xh3 * jnp.mean(dxh3 * xh3, axis=-1, keepdims=True))
        dgate2 = jnp.sum(dx2 * f, axis=0, keepdims=True)
        dfb = (dx2 * gate2).astype(BF16)
        df_ref[...] = dfb
        dh2 = jnp.zeros((tm, D), F32)
        for (a, b), (gp, upj, sg, sl) in zip(FF_SPLITS, saved):
            dact = _dot_nt(dfb, wdn[a:b, :])
            dg = (dact * upj * (sg * (1.0 + gp * (1.0 - sg)))).astype(BF16)
            du = (dact * sl).astype(BF16)
            dgu_ref[:, a:b] = dg
            dgu_ref[:, DFF + a:DFF + b] = du
            dh2 = dh2 + _dot_nt(dg, wgu[:, a:b]) + _dot_nt(du, wgu[:, DFF + a:DFF + b])
        dshift2 = jnp.sum(dh2, axis=0, keepdims=True)
        dscale2 = jnp.sum(dh2 * n2, axis=0, keepdims=True)
        dn2 = dh2 * (1.0 + scale2)
        dn2w = jnp.sum(dn2 * xh2, axis=0, keepdims=True)
        dxh2 = dn2 * n2w
        dx1 = dx2 + r2 * (dxh2 - xh2 * jnp.mean(dxh2 * xh2, axis=-1, keepdims=True))
        dx1_ref[...] = dx1
        dgate1 = jnp.sum(dx1 * o, axis=0, keepdims=True)
        dob = (dx1 * gate1).astype(BF16)
        do_ref[...] = dob
        dmix_ref[...] = _dot_nt(dob, wo[...])
        sm_ref[...] += jnp.concatenate(
            [dfw, dn2w, dshift2, dscale2, dgate2, dgate1, jnp.zeros((2, D), F32)], axis=0)

    row = lambda w: pl.BlockSpec((tm, w), lambda i: (i, 0))
    full = lambda a: pl.BlockSpec(a.shape, lambda i: (0,) * a.ndim)
    anyspec = pl.BlockSpec(memory_space=pl.ANY)
    return pl.pallas_call(
        body, name="mix_ffn", grid=(nt,),
        in_specs=[row(D), row(AW), row(SW), row(D), full(mod6), full(norm2_w), full(final_w), anyspec, anyspec, anyspec],
        out_specs=[pl.BlockSpec((1, D), lambda i: (0, 0)), row(D), row(D), row(D),
                   row(DFF), row(D), row(2 * DFF), row(D), pl.BlockSpec((8, D), lambda i: (0, 0))],
        out_shape=[jax.ShapeDtypeStruct((1, D), F32), jax.ShapeDtypeStruct((T, D), F32), jax.ShapeDtypeStruct((T, D), F32),
                   jax.ShapeDtypeStruct((T, D), BF16), jax.ShapeDtypeStruct((T, DFF), BF16),
                   jax.ShapeDtypeStruct((T, D), BF16), jax.ShapeDtypeStruct((T, 2 * DFF), BF16),
                   jax.ShapeDtypeStruct((T, D), BF16), jax.ShapeDtypeStruct((8, D), F32)],
        scratch_shapes=[pltpu.VMEM((D, D), BF16), pltpu.VMEM((D, 2 * DFF), BF16), pltpu.VMEM((DFF, D), BF16),
                        pltpu.SemaphoreType.DMA((3,))],
        compiler_params=_cp("arbitrary"),
    )(x, attn, ynorm, tgt, mod6, norm2_w, final_w, w_out, w_gu, w_dn)


def _in_proj_bwd(x, dx1, dqkv, dzxd, mod6, norm1_w, w_pad, tm):
    T = x.shape[0]

    def body(x_ref, dx1_ref, dq_ref, dz_ref, mod_ref, nw_ref, w_hbm, gx_ref, sm_ref, w_vmem, sem):
        _load_resident(w_hbm, w_vmem, sem)

        @pl.when(pl.program_id(0) == 0)
        def _():
            sm_ref[...] = jnp.zeros_like(sm_ref)

        nw = nw_ref[...]
        scale1 = mod_ref[1:2, :]
        sums = jnp.zeros((8, D), F32)
        for rows in (slice(0, tm // 2), slice(tm // 2, tm)):
            dh = _dot_nt(dq_ref[rows, :], w_vmem[:, 0:768]) + _dot_nt(dz_ref[rows, :], w_vmem[:, 768:IN_PAD])
            xv = x_ref[rows, :]
            r = lax.rsqrt(jnp.mean(xv * xv, axis=-1, keepdims=True) + EPS)
            xh = xv * r
            n1 = xh * nw
            dshift = jnp.sum(dh, axis=0, keepdims=True)
            dscale = jnp.sum(dh * n1, axis=0, keepdims=True)
            dn = dh * (1.0 + scale1)
            dnw = jnp.sum(dn * xh, axis=0, keepdims=True)
            dxh = dn * nw
            gx_ref[rows, :] = dx1_ref[rows, :] + r * (dxh - xh * jnp.mean(dxh * xh, axis=-1, keepdims=True))
            sums = sums + jnp.concatenate([dnw, dshift, dscale, jnp.zeros((5, D), F32)], axis=0)
        sm_ref[...] += sums

    row = lambda w: pl.BlockSpec((tm, w), lambda i: (i, 0))
    full = lambda a: pl.BlockSpec(a.shape, lambda i: (0,) * a.ndim)
    return pl.pallas_call(
        body, name="in_proj_bwd", grid=(T // tm,),
        in_specs=[row(D), row(D), row(768), row(1664), full(mod6), full(norm1_w), pl.BlockSpec(memory_space=pl.ANY)],
        out_specs=[row(D), pl.BlockSpec((8, D), lambda i: (0, 0))],
        out_shape=[jax.ShapeDtypeStruct((T, D), F32), jax.ShapeDtypeStruct((8, D), F32)],
        scratch_shapes=[pltpu.VMEM((D, IN_PAD), BF16), pltpu.SemaphoreType.DMA],
        compiler_params=_cp("arbitrary"),
    )(x, dx1, dqkv, dzxd, mod6, norm1_w, w_pad)


def _tn_matmul(a, b, K, N, tt, name, dep):
    T = a.shape[0]
    ja, jb = a.shape[1] // K, b.shape[1] // N
    J = max(ja, jb)

    def body(a_ref, b_ref, dep_ref, o_ref):
        t = pl.program_id(1)
        prod = _dot_tn(a_ref[...], b_ref[...])

        @pl.when(t == 0)
        def _():
            o_ref[0] = prod

        @pl.when(t > 0)
        def _():
            o_ref[0] += prod

    return pl.pallas_call(
        body, name=name, grid=(J, T // tt),
        in_specs=[pl.BlockSpec((tt, K), lambda j, t: (t, j if ja > 1 else 0)),
                  pl.BlockSpec((tt, N), lambda j, t: (t, j if jb > 1 else 0)),
                  pl.BlockSpec((8, 128), lambda j, t: (0, 0))],
        out_specs=pl.BlockSpec((1, K, N), lambda j, t: (j, 0, 0)),
        out_shape=jax.ShapeDtypeStruct((J, K, N), F32),
        compiler_params=_cp("parallel", "arbitrary"),
    )(a, b, dep)


def _adam_math(w, g, m, v):
    m = B1 * m + (1.0 - B1) * g
    v = B2 * v + (1.0 - B2) * (g * g)
    m_hat = m / (1.0 - B1 ** STEP)
    v_hat = v / (1.0 - B2 ** STEP)
    delta = -LR * (m_hat / (jnp.sqrt(v_hat) + AEPS) + WD * w)
    return delta, m, v


def _adam_2d(w, mine, land, m, v, c_arr, rb, name):
    R, C = w.shape
    nbh = R // 2 // rb

    def body(c_ref, w_ref, mine_ref, land_ref, m_ref, v_ref, go_ref, d_ref, mo_ref, vo_ref):
        g = jnp.where(pl.program_id(0) // nbh == c_ref[0], mine_ref[...], land_ref[...])
        d, mn, vn = _adam_math(w_ref[...], g, m_ref[...], v_ref[...])
        go_ref[...] = g
        d_ref[...] = d
        mo_ref[...] = mn
        vo_ref[...] = vn

    spec = pl.BlockSpec((rb, C), lambda i, c_ref: (i, 0))
    mine_spec = pl.BlockSpec((rb, C), lambda i, c_ref: (jnp.clip(i - c_ref[0] * nbh, 0, nbh - 1), 0))
    return pl.pallas_call(
        body, name=name,
        grid_spec=pltpu.PrefetchScalarGridSpec(
            num_scalar_prefetch=1, grid=(R // rb,), in_specs=[spec, mine_spec, spec, spec, spec], out_specs=[spec] * 4),
        out_shape=[jax.ShapeDtypeStruct((R, C), F32)] * 4, compiler_params=_cp("parallel"),
    )(c_arr, w, mine, land, m, v)


def _adam_w_ada(sc_all, dmod_s, w, m, v, rb):
    R, C = w.shape

    def body(sc_ref, dm_ref, w_ref, m_ref, v_ref, g_ref, d_ref, mo_ref, vo_ref):
        g = lax.dot_general(sc_ref[...], dm_ref[...], (((0,), (0,)), ((), ())), precision=HI, preferred_element_type=F32)
        d, mn, vn = _adam_math(w_ref[...], g, m_ref[...], v_ref[...])
        g_ref[...] = g
        d_ref[...] = d
        mo_ref[...] = mn
        vo_ref[...] = vn

    spec = pl.BlockSpec((rb, C), lambda i: (i, 0))
    return pl.pallas_call(
        body, name="adam_w_ada", grid=(R // rb,),
        in_specs=[pl.BlockSpec((8, rb), lambda i: (0, i)), pl.BlockSpec((8, C), lambda i: (0, 0)), spec, spec, spec],
        out_specs=[spec] * 4, out_shape=[jax.ShapeDtypeStruct((R, C), F32)] * 4, compiler_params=_cp("parallel"),
    )(sc_all, dmod_s, w, m, v)


def _adam_small(grads, ws, ms, vs):
    k = len(ws)

    def body(*refs):
        g, w, m, v = refs[0:k], refs[k:2 * k], refs[2 * k:3 * k], refs[3 * k:4 * k]
        g_o, d_o, m_o, v_o = refs[4 * k:5 * k], refs[5 * k:6 * k], refs[6 * k:7 * k], refs[7 * k:8 * k]
        for i in range(k):
            gi = g[i][...]
            d, mn, vn = _adam_math(w[i][...], gi, m[i][...], v[i][...])
            g_o[i][...] = gi
            d_o[i][...] = d
            m_o[i][...] = mn
            v_o[i][...] = vn

    shapes = [jax.ShapeDtypeStruct(w.shape, F32) for w in ws]
    vm = pl.BlockSpec(memory_space=pltpu.VMEM)
    outs = pl.pallas_call(
        body, name="adam_small", in_specs=[vm] * (4 * k), out_specs=[vm] * (4 * k), out_shape=shapes * 4,
    )(*grads, *ws, *ms, *vs)
    return outs[0:k], outs[k:2 * k], outs[2 * k:3 * k], outs[3 * k:4 * k]


def _pos():
    return lax.axis_index("x"), lax.axis_index("y"), lax.axis_index("c")


def _flip(v, bit):
    return 1 - v if bit else v


def _peer(k):
    x, y, c = _pos()
    return (_flip(x, (k >> 2) & 1), _flip(y, (k >> 1) & 1), _flip(c, k & 1))


def _logical(p):
    return 4 * p[0] + 2 * p[1] + p[2]


def _gather8(src_ref, dst_ref, send_sems, recv_sems):
    me = _logical(_pos())
    dst_ref[pl.ds(me, 1)] = src_ref[...][None]
    copies = []
    for k in range(1, 8):
        cp = pltpu.make_async_remote_copy(src_ref, dst_ref.at[me], send_sems.at[k - 1], recv_sems.at[k - 1],
                                          device_id=_peer(k), device_id_type=MESH)
        cp.start()
        copies.append(cp)
    for k in range(1, 8):
        pltpu.make_async_remote_copy(src_ref, dst_ref.at[_logical(_peer(k))], send_sems.at[k - 1], recv_sems.at[k - 1],
                                     device_id=_peer(k), device_id_type=MESH).wait_recv()
    for cp in copies:
        cp.wait_send()


def _rows_select(ref3, width):
    row = _iota((8, width), 0)
    out = jnp.zeros((8, width), F32)
    for i in range(8):
        out = jnp.where(row == i, ref3[i][:, 0:width], out)
    return out


def _mod_exchange(payload, w_ada_s, b_ada4):
    n_sh = w_ada_s.shape[1]

    def body(pay_ref, w_ref, b_ref, gat_ref, mod_ref, token, p3, sa, ra, sb, rb):
        token[...] = jnp.zeros_like(token)
        x, y, c = _pos()
        me = _logical((x, y, c))
        my_s = 2 * x + y
        _gather8(pay_ref, gat_ref, sa, ra)
        cmat = _rows_select(gat_ref, D)
        prod = _dot_hi(cmat * _sigmoid(cmat), w_ref[...])
        for b in range(8):
            p3[b] = prod[b:b + 1, :]
        mod_ref[pl.ds(my_s, 1)] = p3[pl.ds(me, 1)] + b_ref[pl.ds(my_s, 1)]
        ks = (2, 4, 6)
        copies = []
        for i, k in enumerate(ks):
            pr = _peer(k)
            cp = pltpu.make_async_remote_copy(p3.at[_logical(pr)], mod_ref.at[my_s], sb.at[i], rb.at[i],
                                              device_id=pr, device_id_type=MESH)
            cp.start()
            copies.append(cp)
        for i, k in enumerate(ks):
            pr = _peer(k)
            s_src = 2 * pr[0] + pr[1]
            pltpu.make_async_remote_copy(p3.at[0], mod_ref.at[s_src], sb.at[i], rb.at[i],
                                         device_id=pr, device_id_type=MESH).wait_recv()
            mod_ref[pl.ds(s_src, 1)] = mod_ref[pl.ds(s_src, 1)] + b_ref[pl.ds(s_src, 1)]
        for cp in copies:
            cp.wait_send()

    vm = pl.BlockSpec(memory_space=pltpu.VMEM)
    return pl.pallas_call(
        body, name="mod_exchange", in_specs=[vm, vm, vm], out_specs=[vm, vm, vm],
        out_shape=[jax.ShapeDtypeStruct((8, 1, payload.shape[1]), F32), jax.ShapeDtypeStruct((4, 1, n_sh), F32),
                   jax.ShapeDtypeStruct((8, 128), F32)],
        scratch_shapes=[pltpu.VMEM((8, 1, n_sh), F32), pltpu.SemaphoreType.DMA((7,)), pltpu.SemaphoreType.DMA((7,)),
                        pltpu.SemaphoreType.DMA((3,)), pltpu.SemaphoreType.DMA((3,))],
        compiler_params=pltpu.CompilerParams(vmem_limit_bytes=VMEM_LIMIT),
    )(payload, w_ada_s, b_ada4)


def _chips():
    x, y, _ = _pos()
    out = []
    for k in (1, 2, 3):
        px, py = _flip(x, (k >> 1) & 1), _flip(y, k & 1)
        out.append((px, py, 2 * px + py))
    return out


def _half_rows(ref, which):
    half = ref.shape[-2] // 2
    return pl.ds(pl.multiple_of(which * half, 8), half)


def _weight_gather(shards):
    nw = len(shards)

    def body(*refs):
        ins, outs, token = refs[:nw], refs[nw:2 * nw], refs[2 * nw]
        send, recv, fsend, frecv = refs[2 * nw + 1:]
        token[...] = jnp.zeros_like(token)
        x, y, c = _pos()
        my_s = 2 * x + y
        sib = (x, y, 1 - c)
        chips = _chips()
        sends = []
        for w in range(nw):
            mine = _half_rows(ins[w], c)
            for k, (px, py, _) in enumerate(chips):
                cp = pltpu.make_async_remote_copy(ins[w].at[mine], outs[w].at[my_s, mine], send.at[3 * w + k],
                                                  recv.at[3 * w + k], device_id=(px, py, c), device_id_type=MESH)
                cp.start()
                sends.append(cp)
        for w in range(nw):
            mine = _half_rows(ins[w], c)
            for k, (px, py, ps) in enumerate(chips):
                got = outs[w].at[ps, mine]
                pltpu.make_async_remote_copy(got, got, send.at[3 * w + k], recv.at[3 * w + k],
                                             device_id=(px, py, c), device_id_type=MESH).wait_recv()
                cp = pltpu.make_async_remote_copy(got, got, fsend.at[3 * w + k], frecv.at[3 * w + k],
                                                  device_id=sib, device_id_type=MESH)
                cp.start()
                sends.append(cp)
        for w in range(nw):
            other = _half_rows(ins[w], 1 - c)
            for k, (px, py, ps) in enumerate(chips):
                got = outs[w].at[ps, other]
                pltpu.make_async_remote_copy(got, got, fsend.at[3 * w + k], frecv.at[3 * w + k],
                                             device_id=sib, device_id_type=MESH).wait_recv()
        for cp in sends:
            cp.wait_send()

    hbm = pl.BlockSpec(memory_space=pltpu.HBM)
    return pl.pallas_call(
        body, name="weight_gather", in_specs=[hbm] * nw,
        out_specs=[hbm] * nw + [pl.BlockSpec(memory_space=pltpu.VMEM)],
        out_shape=[pltpu.HBM((4,) + s.shape, s.dtype) for s in shards] + [jax.ShapeDtypeStruct((8, 128), F32)],
        scratch_shapes=[pltpu.SemaphoreType.DMA((3 * nw,)), pltpu.SemaphoreType.DMA((3 * nw,)),
                        pltpu.SemaphoreType.DMA((3 * nw,)), pltpu.SemaphoreType.DMA((3 * nw,))],
    )(*shards)


def _small_reduce(vec):
    n = vec.shape[1]

    def body(v_ref, tot_ref, gat_ref, sa, ra):
        _gather8(v_ref, gat_ref, sa, ra)
        tot = gat_ref[0]
        for i in range(1, 8):
            tot = tot + gat_ref[i]
        tot_ref[...] = tot

    vm = pl.BlockSpec(memory_space=pltpu.VMEM)
    return pl.pallas_call(
        body, name="small_reduce", in_specs=[vm], out_specs=[vm, vm],
        out_shape=[jax.ShapeDtypeStruct((1, n), F32), jax.ShapeDtypeStruct((8, 1, n), F32)],
        scratch_shapes=[pltpu.SemaphoreType.DMA((7,)), pltpu.SemaphoreType.DMA((7,))],
    )(vec)


def _add_half(g, sib, c_arr, rb, name):
    _, R, C = g.shape
    half = R // 2
    nb = half // rb

    def body(c_ref, g_ref, s_ref, o_ref):
        o_ref[...] = (g_ref[...] + s_ref[...]).astype(BF16)

    return pl.pallas_call(
        body, name=name,
        grid_spec=pltpu.PrefetchScalarGridSpec(
            num_scalar_prefetch=1, grid=(4, nb),
            in_specs=[pl.BlockSpec((1, rb, C), lambda s, i, c_ref: (s, c_ref[0] * nb + i, 0)),
                      pl.BlockSpec((1, rb, C), lambda s, i, c_ref: (s, i, 0))],
            out_specs=pl.BlockSpec((1, rb, C), lambda s, i, c_ref: (s, i, 0))),
        out_shape=jax.ShapeDtypeStruct((4, half, C), BF16),
        compiler_params=_cp("parallel", "parallel"),
    )(c_arr, g, sib)


def _sum4(parts, land, s_arr, rb, name):
    _, H, C = land.shape

    def body(s_ref, own_ref, r_ref, o_ref):
        own = own_ref[0].astype(F32)
        tot = jnp.zeros((rb, C), F32)
        for j in range(4):
            tot = tot + jnp.where(s_ref[0] == j, own, r_ref[j].astype(F32))
        o_ref[...] = tot

    return pl.pallas_call(
        body, name=name,
        grid_spec=pltpu.PrefetchScalarGridSpec(
            num_scalar_prefetch=1, grid=(H // rb,),
            in_specs=[pl.BlockSpec((1, rb, C), lambda i, s_ref: (s_ref[0], i, 0)),
                      pl.BlockSpec((4, rb, C), lambda i, s_ref: (0, i, 0))],
            out_specs=pl.BlockSpec((rb, C), lambda i, s_ref: (i, 0))),
        out_shape=jax.ShapeDtypeStruct((H, C), F32), compiler_params=_cp("parallel"),
    )(s_arr, parts, land)


HBM_SPEC = pl.BlockSpec(memory_space=pltpu.HBM)
SEM_SPEC = pl.BlockSpec(memory_space=pltpu.SEMAPHORE)
EFFECT = pltpu.SideEffectType.DATAFLOW_SIDE_EFFECTING


def _split_start(name, bufs, n_sem, plan):
    nb = len(bufs)

    def body(*refs):
        ins, send, recv, token = refs[:nb], refs[nb], refs[nb + 1], refs[-1]
        for i, (src, dst, dev, _) in enumerate(plan(ins)):
            pltpu.make_async_remote_copy(src, dst, send.at[i], recv.at[i], device_id=dev, device_id_type=MESH).start()
        token[...] = jnp.zeros_like(token)

    outs = pl.pallas_call(
        body, name=name,
        out_shape=(pltpu.SemaphoreType.DMA((n_sem,)), pltpu.SemaphoreType.DMA((n_sem,)),
                   *[pltpu.HBM(b.shape, b.dtype) for b in bufs], jax.ShapeDtypeStruct((8, 128), F32)),
        in_specs=[HBM_SPEC] * nb,
        out_specs=(SEM_SPEC, SEM_SPEC, *([HBM_SPEC] * nb), pl.BlockSpec(memory_space=pltpu.VMEM)),
        input_output_aliases={i: 2 + i for i in range(nb)},
        compiler_params=pltpu.CompilerParams(has_side_effects=EFFECT),
    )(*[pltpu.with_memory_space_constraint(b, pltpu.HBM) for b in bufs])
    return outs[0], outs[1], list(outs[2:2 + nb]), outs[-1]


def _split_wait(name, send, recv, bufs, after, plan):
    nb = len(bufs)

    def body(*refs):
        ins, send_s, recv_s = refs[:nb], refs[nb], refs[nb + 1]
        for i, (src, dst, dev, mine) in enumerate(plan(ins)):
            pltpu.make_async_remote_copy(src, dst, send_s.at[i], recv_s.at[i], device_id=dev,
                                         device_id_type=MESH).wait_send()
            pltpu.make_async_remote_copy(src, mine, send_s.at[i], recv_s.at[i], device_id=dev,
                                         device_id_type=MESH).wait_recv()

    outs = pl.pallas_call(
        body, name=name, out_shape=[pltpu.HBM(b.shape, b.dtype) for b in bufs],
        in_specs=[HBM_SPEC] * nb + [SEM_SPEC, SEM_SPEC, pl.BlockSpec(memory_space=pl.ANY)],
        out_specs=[HBM_SPEC] * nb, input_output_aliases={i: i for i in range(nb)},
        compiler_params=pltpu.CompilerParams(has_side_effects=EFFECT),
    )(*bufs, send, recv, after)
    return list(outs)


def _slot(land, s, rows, cols):
    if cols is None:
        return land.at[s, rows]
    return land.at[rows, pl.ds(pl.multiple_of(s * cols, 128), cols)]


def _plan_gather_ici(cols):
    nw = len(cols)

    def plan(refs):
        x, y, c = _pos()
        my_s = 2 * x + y
        out = []
        for w in range(nw):
            mine = _half_rows(refs[w], c)
            for px, py, ps in _chips():
                out.append((refs[w].at[mine], _slot(refs[nw + w], my_s, mine, cols[w]), (px, py, c),
                            _slot(refs[nw + w], ps, mine, cols[w])))
        return out
    return plan


def _plan_gather_fwd(cols, rows):
    def plan(refs):
        x, y, c = _pos()
        out = []
        for w in range(len(cols)):
            half = rows[w] // 2
            mine = pl.ds(pl.multiple_of(c * half, 8), half)
            other = pl.ds(pl.multiple_of((1 - c) * half, 8), half)
            for px, py, ps in _chips():
                got = _slot(refs[w], ps, mine, cols[w])
                out.append((got, got, (x, y, 1 - c), _slot(refs[w], ps, other, cols[w])))
        return out
    return plan


def _plan_swap(nw):
    def plan(refs):
        x, y, c = _pos()
        return [(refs[w].at[:, _half_rows(refs[w], 1 - c)], refs[nw + w], (x, y, 1 - c), refs[nw + w])
                for w in range(nw)]
    return plan


def _plan_scatter(nw):
    def plan(refs):
        x, y, c = _pos()
        my_s = 2 * x + y
        out = []
        for w in range(nw):
            for px, py, ps in _chips():
                out.append((refs[w].at[ps], refs[nw + w].at[my_s], (px, py, c), refs[nw + w].at[ps]))
        return out
    return plan


def _plan_join(nw):
    def plan(refs):
        x, y, c = _pos()
        out = []
        for w in range(nw):
            land = refs[nw + w]
            out.append((refs[w], land.at[_half_rows(land, c)], (x, y, 1 - c), land.at[_half_rows(land, 1 - c)]))
        return out
    return plan


def _hbm_empty(shape, dtype):
    return pltpu.with_memory_space_constraint(lax.empty(shape, dtype), pltpu.HBM)


def _put_slot(land, own, slot):
    return lax.dynamic_update_slice(land, own[None], (slot,) + (0,) * own.ndim)


def _pad_lanes(a, n):
    return jnp.pad(a, ((0, 0), (0, n - a.shape[1])))


def kernel(x, c, positions, w_ada, b_ada, norm1_w, w_in, conv_w, conv_b, dt_bias, a_log, d_skip, attn_sinks, ssm_norm_w, w_out, norm2_w, w_gate_up, w_down, final_norm_w, loss_target, m_w_ada, m_b_ada, m_norm1_w, m_w_in, m_conv_w, m_conv_b, m_dt_bias, m_a_log, m_d_skip, m_attn_sinks, m_ssm_norm_w, m_w_out, m_norm2_w, m_w_gate_up, m_w_down, m_final_norm_w, v_w_ada, v_b_ada, v_norm1_w, v_w_in, v_conv_w, v_conv_b, v_dt_bias, v_a_log, v_d_skip, v_attn_sinks, v_ssm_norm_w, v_w_out, v_norm2_w, v_w_gate_up, v_w_down, v_final_norm_w):
    T = x.shape[1]
    tm = min(256, T)
    xi, yi, ci = lax.axis_index("x"), lax.axis_index("y"), lax.axis_index("c")
    my_s = 2 * xi + yi
    xs = x[0]
    tgt = loss_target[0]

    payload = jnp.concatenate([c, conv_w[0].reshape(1, CONVK * 256)], axis=1)
    gat, mod4, tok = _mod_exchange(payload, w_ada[0], b_ada.reshape(4, 1, 1536))
    mod6 = mod4.reshape(6, D)
    c_all = gat[:, 0, 0:D]
    cw_dev = gat[:, 0, D:].reshape(4, 2, CONVK, 256)[:, 0]
    conv_full = cw_dev.transpose(1, 0, 2).reshape(CONVK, CONVC)

    w_in_b = (w_in[0] + tok[0, 0]).astype(BF16)
    s_i, r_i, bufs, tok = _split_start("wgather_in_ici_start", [w_in_b, _hbm_empty((4,) + w_in_b.shape, BF16)], 3,
                                       _plan_gather_ici([None]))
    inv_freq = (10000.0 ** (-jnp.arange(32, dtype=F32) / 32))
    inv_row = jnp.tile(inv_freq, 4).reshape(1, 128)
    cos, sin_s = _rope_tables(positions.reshape(T, 1), inv_row + tok[0:1, :], tm)
    bufs = _split_wait("wgather_in_ici_wait", s_i, r_i, bufs, cos, _plan_gather_ici([None]))
    s_j, r_j, bufs, tok = _split_start("wgather_in_fwd_start", bufs[1:], 3, _plan_gather_fwd([None], [D]))
    bufs = _split_wait("wgather_in_fwd_wait", s_j, r_j, bufs, tok, _plan_gather_fwd([None], [D]))
    g_in = _put_slot(bufs[0], w_in_b, my_s)
    w_pad = jnp.concatenate([g_in[0], g_in[1], g_in[2], g_in[3], jnp.zeros((D, IN_PAD - IN_PROJ), BF16)], axis=1)

    late = [(w_out[0] + tok[0, 0]).astype(BF16), w_gate_up[0].astype(BF16), w_down[0].astype(BF16)]
    lands = [_hbm_empty((4, D // 4, D), BF16), _hbm_empty((D, 2 * DFF), BF16), _hbm_empty((4, DFF // 4, D), BF16)]
    cols3, rows3 = [None, GU_SH, None], [D // 4, D, DFF // 4]
    s_a, r_a, bufs, tok = _split_start("wgather_ici_start", late + lands, 9, _plan_gather_ici(cols3))

    qkv, z, xbc, dtr, h1b = _in_proj_fwd(xs, cos, sin_s, mod6 + tok[0, 0], norm1_w, w_pad, min(512, T))
    sinks = attn_sinks
    attn, lse = _attn_fwd(qkv, sinks)
    bufs = _split_wait("wgather_ici_wait", s_a, r_a, bufs, attn, _plan_gather_ici(cols3))
    s_b, r_b, lands, tok = _split_start("wgather_fwd_start", bufs[3:], 9, _plan_gather_fwd(cols3, rows3))
    dtb = _pad_lanes(dt_bias, 128)
    alog = _pad_lanes(a_log, 128)
    dskx = jnp.repeat(d_skip, HD, axis=1)
    mats = _ssd_mats()
    ynorm, ypre, states, conv_pre = _ssd_fwd(xbc, z, dtr, conv_full, conv_b, dtb + tok[0, 0], alog, dskx, ssm_norm_w,
                                             mats)
    lands = _split_wait("wgather_fwd_wait", s_b, r_b, lands, ynorm, _plan_gather_fwd(cols3, rows3))
    w_out_f = _put_slot(lands[0], late[0], my_s).reshape(D, D)
    g_gu = lax.dynamic_update_slice(lands[1], late[1], (0, my_s * GU_SH))
    w_dn_f = _put_slot(lands[2], late[2], my_s).reshape(DFF, D)

    fw2 = final_norm_w.reshape(1, D)
    sq, dmix, dx1, h2b, act, dfb, dgu, dob, sm_ffn = _mix_ffn(
        xs, attn, ynorm, tgt, mod6, norm2_w, fw2, w_out_f, g_gu, w_dn_f, tm)

    tt = min(2048, T)
    c_arr = ci.reshape(1).astype(jnp.int32)
    tok0 = jnp.zeros((8, 128), F32)
    gw_dn4 = _tn_matmul(act, dfb, GU_SH, D, tt, "dw_down", tok0).reshape(4, DFF // 4, D)
    gw_gu4 = _tn_matmul(h2b, dgu, D, GU_SH, tt, "dw_gate_up", tok0)
    gw_out4 = jnp.concatenate(
        [_tn_matmul(attn, dob, AW, D, tt, "dw_out_a", tok0)[0],
         _tn_matmul(ynorm, dob, SW, D, tt, "dw_out_y", tok0)[0]], axis=0).reshape(4, D // 4, D)
    big1 = [gw_out4, gw_gu4, gw_dn4]
    rbs1 = [128, 128, 176]
    sib1 = [_hbm_empty((4, g.shape[1] // 2, g.shape[2]), F32) for g in big1]
    s_c, r_c, bufs, tok = _split_start("gswap_start", big1 + sib1, 3, _plan_swap(3))

    dzxd, d_cw, d_cb, d_sw, d_sk, d_dtb, d_av = _ssd_bwd(
        xbc, conv_pre, z, dtr, ypre, states, dmix, conv_full, dtb + tok[0, 0], alog, dskx, ssm_norm_w, mats)
    bufs = _split_wait("gswap_wait", s_c, r_c, bufs, dzxd, _plan_swap(3))
    sums1 = [_add_half(g, s, c_arr, rb, "grad_add_%d" % i)
             for i, (g, s, rb) in enumerate(zip(bufs[:3], bufs[3:], rbs1))]
    land1 = [_hbm_empty(p.shape, BF16) for p in sums1]
    s_d, r_d, bufs, tok = _split_start("gscatter_start", sums1 + land1, 9, _plan_scatter(3))
    dqkv, d_sinks = _attn_bwd(qkv, sinks + tok[0:1, 0:8], lse, dmix, cos, sin_s)
    bufs = _split_wait("gscatter_wait", s_d, r_d, bufs, dqkv, _plan_scatter(3))
    s_arr = my_s.reshape(1).astype(jnp.int32)
    halves1 = [_sum4(p, l, s_arr, rb, "grad_sum_%d" % i)
               for i, (p, l, rb) in enumerate(zip(bufs[:3], bufs[3:], rbs1))]
    full1 = [_hbm_empty((2 * h.shape[0], h.shape[1]), F32) for h in halves1]
    s_e, r_e, bufs, tok = _split_start("gjoin_start", halves1 + full1, 3, _plan_join(3))
    gw_in = jnp.concatenate([_tn_matmul(h1b, dqkv, D, 768, tt, "dw_in_qkv", tok)[0],
                             _tn_matmul(h1b, dzxd, D, 1664, tt, "dw_in_zxd", tok)[0]], axis=1)
    gw_in4 = jnp.stack([gw_in[:, IN_SH * s:IN_SH * (s + 1)] for s in range(4)])
    joined1 = _split_wait("gjoin_wait", s_e, r_e, bufs, gw_in4, _plan_join(3))

    sib0 = _hbm_empty((4, D // 2, IN_SH), F32)
    s_f, r_f, bufs, tok = _split_start("gswap_in_start", [gw_in4, sib0], 1, _plan_swap(1))
    bufs = _split_wait("gswap_in_wait", s_f, r_f, bufs, tok, _plan_swap(1))
    sum0 = _add_half(bufs[0], bufs[1], c_arr, 128, "grad_add_in")
    s_g, r_g, bufs, tok = _split_start("gscatter_in_start", [sum0, _hbm_empty(sum0.shape, BF16)], 3, _plan_scatter(1))
    grad_x, sm_in = _in_proj_bwd(xs, dx1, dqkv, dzxd, mod6 + tok[0, 0], norm1_w, w_pad, min(512, T))
    bufs = _split_wait("gscatter_in_wait", s_g, r_g, bufs, grad_x, _plan_scatter(1))
    half0 = _sum4(bufs[0], bufs[1], s_arr, 128, "grad_sum_in")
    s_h, r_h, bufs, tok = _split_start("gjoin_in_start", [half0, _hbm_empty((D, IN_SH), F32)], 1, _plan_join(1))
    joined0 = _split_wait("gjoin_in_wait", s_h, r_h, bufs, tok, _plan_join(1))

    a_neg = -jnp.exp(alog)
    pieces = [sm_in[1:2], sm_in[2:3], sm_ffn[5:6], sm_ffn[2:3], sm_ffn[3:4], sm_ffn[4:5],
              sm_in[0:1], sm_ffn[1:2], sm_ffn[0:1], d_cb, d_cw.reshape(1, CONVK * CONVC),
              _pad_lanes(d_sw, SW), d_dtb, d_av * a_neg, d_sk, d_sinks,
              _pad_lanes((0.5 / D * jnp.sum(sq)).reshape(1, 1), 128)]
    vec = jnp.concatenate(pieces, axis=1)
    tot, allv = _small_reduce(vec)
    o = 0
    offs = []
    for p in pieces:
        offs.append(o)
        o += p.shape[1]
    seg = lambda i, n: tot[:, offs[i]:offs[i] + n]
    g_b_ada = tot[:, 0:6 * D]
    g_norm1, g_norm2, g_final, g_conv_b = seg(6, D), seg(7, D), seg(8, D), seg(9, D)
    g_conv_w = lax.dynamic_slice_in_dim(seg(10, CONVK * CONVC).reshape(CONVK, CONVC), my_s * 256, 256, axis=1)
    g_ssm_w, g_dtb, g_alog, g_dsk, g_sink = seg(11, SW), seg(12, 8), seg(13, 8), seg(14, 8), seg(15, 8)
    loss = tot[0, offs[16]]

    small_names = ["b_ada", "norm1_w", "conv_w", "conv_b", "dt_bias", "a_log", "d_skip", "attn_sinks", "ssm_norm_w",
                   "norm2_w", "final_norm_w"]
    small_g = [g_b_ada, g_norm1, g_conv_w, g_conv_b, g_dtb, g_alog, g_dsk, g_sink, g_ssm_w, g_norm2, g_final]
    as2d = lambda a: a.reshape(-1, a.shape[-1])
    small_w = [as2d(a) for a in (b_ada, norm1_w, conv_w, conv_b, dt_bias, a_log, d_skip, attn_sinks, ssm_norm_w,
                                 norm2_w, final_norm_w)]
    small_m = [as2d(a) for a in (m_b_ada, m_norm1_w, m_conv_w, m_conv_b, m_dt_bias, m_a_log, m_d_skip, m_attn_sinks,
                                 m_ssm_norm_w, m_norm2_w, m_final_norm_w)]
    small_v = [as2d(a) for a in (v_b_ada, v_norm1_w, v_conv_w, v_conv_b, v_dt_bias, v_a_log, v_d_skip, v_attn_sinks,
                                 v_ssm_norm_w, v_norm2_w, v_final_norm_w)]
    small_g, sd, smn, svn = _adam_small(small_g, small_w, small_m, small_v)

    sc_all = c_all * jax.nn.sigmoid(c_all)
    dmod_all = allv[:, 0, 0:6 * D]
    dmod_s = lax.dynamic_slice_in_dim(dmod_all, my_s * 1536, 1536, axis=1)
    g_ada, d_ada, m_ada, v_ada = _adam_w_ada(sc_all, dmod_s, w_ada[0], m_w_ada[0], v_w_ada[0], 256)
    g_in_s, d_in, m_in, v_in = _adam_2d(w_in[0], joined0[0], joined0[1], m_w_in[0], v_w_in[0], c_arr, 256, "adam_w_in")
    g_out_s, d_out, m_out, v_out = _adam_2d(w_out[0], joined1[0], joined1[3], m_w_out[0], v_w_out[0], c_arr, 128,
                                            "adam_w_out")
    g_gu_s, d_gu, m_gu, v_gu = _adam_2d(w_gate_up[0], joined1[1], joined1[4], m_w_gate_up[0], v_w_gate_up[0], c_arr,
                                        256, "adam_w_gate_up")
    g_dn_s, d_dn, m_dn, v_dn = _adam_2d(w_down[0], joined1[2], joined1[5], m_w_down[0], v_w_down[0], c_arr, 352,
                                        "adam_w_down")

    order = ["w_ada", "b_ada", "norm1_w", "w_in", "conv_w", "conv_b", "dt_bias", "a_log", "d_skip", "attn_sinks",
             "ssm_norm_w", "w_out", "norm2_w", "w_gate_up", "w_down", "final_norm_w"]
    shapes = dict(w_ada=w_ada.shape, b_ada=b_ada.shape, norm1_w=norm1_w.shape, w_in=w_in.shape, conv_w=conv_w.shape,
                  conv_b=conv_b.shape, dt_bias=dt_bias.shape, a_log=a_log.shape, d_skip=d_skip.shape,
                  attn_sinks=attn_sinks.shape, ssm_norm_w=ssm_norm_w.shape, w_out=w_out.shape, norm2_w=norm2_w.shape,
                  w_gate_up=w_gate_up.shape, w_down=w_down.shape, final_norm_w=final_norm_w.shape)
    grads = dict(w_ada=g_ada, w_in=g_in_s, w_out=g_out_s, w_gate_up=g_gu_s, w_down=g_dn_s)
    deltas = dict(w_ada=d_ada, w_in=d_in, w_out=d_out, w_gate_up=d_gu, w_down=d_dn)
    new_m = dict(w_ada=m_ada, w_in=m_in, w_out=m_out, w_gate_up=m_gu, w_down=m_dn)
    new_v = dict(w_ada=v_ada, w_in=v_in, w_out=v_out, w_gate_up=v_gu, w_down=v_dn)
    for i, nme in enumerate(small_names):
        grads[nme], deltas[nme], new_m[nme], new_v[nme] = small_g[i], sd[i], smn[i], svn[i]
    outs = [loss, grad_x[None]]
    for table in (grads, deltas, new_m, new_v):
        outs += [table[nme].reshape(shapes[nme]) for nme in order]
    return tuple(outs)
```

```python
import functools
import math

import jax
import jax.numpy as jnp
from jax import lax
from jax.experimental import pallas as pl
from jax.experimental.pallas import tpu as pltpu

F32 = jnp.float32
BF16 = jnp.bfloat16
HI = lax.Precision.HIGHEST
MESH = pl.DeviceIdType.MESH

D = 1024
HD = 64
NQ = 8
AW = 512
KVW = 128
SW = 512
NST = 128
CONVK = 4
CONVC = 1024
BLK = 128
IN_PROJ = 2312
IN_PAD = 2432
IN_SH = IN_PROJ // 4
IN_SH_PAD = 608
DFF = 2816
GU_SH = 1408
FF_SPLITS = ((0, 1536), (1536, 2816))
EPS = 1e-6
NEG = -1e30
LR, B1, B2, AEPS, WD, STEP = 0.001, 0.9, 0.999, 1e-08, 0.01, 10
VMEM_LIMIT = 58 * 1024 * 1024


def _cp(*sem):
    return pltpu.CompilerParams(dimension_semantics=sem or None, vmem_limit_bytes=VMEM_LIMIT)


def _dot(a, b):
    return jnp.dot(a, b, preferred_element_type=F32)


def _dot_nt(a, b):
    return lax.dot_general(a, b, (((1,), (1,)), ((), ())), preferred_element_type=F32)


def _dot_tn(a, b):
    return lax.dot_general(a, b, (((0,), (0,)), ((), ())), preferred_element_type=F32)


def _dot_hi(a, b):
    return jnp.dot(a, b, precision=HI, preferred_element_type=F32)


def _sigmoid(x):
    return 1.0 / (1.0 + jnp.exp(-x))


def _iota(shape, dim):
    return lax.broadcasted_iota(jnp.int32, shape, dim)


def _load_resident(hbm_ref, vmem_ref, sem):
    @pl.when(pl.program_id(0) == 0)
    def _():
        cp = pltpu.make_async_copy(hbm_ref, vmem_ref, sem)
        cp.start()
        cp.wait()


def _swap32(t):
    lane = _iota(t.shape, 1)
    return jnp.where((lane & 63) < 32, pltpu.roll(t, 96, 1), pltpu.roll(t, 32, 1))


def _rope_fwd(t, cos, sin_s):
    return t * cos + _swap32(t) * sin_s


def _rope_bwd(t, cos, sin_s):
    return t * cos - _swap32(t) * sin_s


def _rope_tables(pos_col, inv_freq_row, tm):
    T = pos_col.shape[0]

    def body(p_ref, f_ref, cos_ref, sin_ref):
        ang = p_ref[...].astype(F32) * f_ref[...]
        lane = _iota((tm, 128), 1)
        s = jnp.sin(ang)
        cos_ref[...] = jnp.cos(ang)
        sin_ref[...] = jnp.where((lane & 63) < 32, -s, s)

    return pl.pallas_call(
        body, name="rope_tables", grid=(T // tm,),
        in_specs=[pl.BlockSpec((tm, 1), lambda i: (i, 0)), pl.BlockSpec((1, 128), lambda i: (0, 0))],
        out_specs=[pl.BlockSpec((tm, 128), lambda i: (i, 0))] * 2,
        out_shape=[jax.ShapeDtypeStruct((T, 128), F32)] * 2,
        compiler_params=_cp("parallel"),
    )(pos_col, inv_freq_row)


def _in_proj_fwd(x, cos, sin_s, mod6, norm1_w, w_pad, tm):
    T = x.shape[0]

    def body(x_ref, cos_ref, sin_ref, mod_ref, nw_ref, w_hbm, qkv_ref, z_ref, xbc_ref, dt_ref, h_ref, w_vmem, sem):
        _load_resident(w_hbm, w_vmem, sem)
        xv = x_ref[...]
        r = lax.rsqrt(jnp.mean(xv * xv, axis=-1, keepdims=True) + EPS)
        h = (xv * r * nw_ref[...]) * (1.0 + mod_ref[1:2, :]) + mod_ref[0:1, :]
        hb = h.astype(BF16)
        h_ref[...] = hb
        proj = _dot(hb, w_vmem[...])
        cs, sn = cos_ref[...], sin_ref[...]
        for j in range(5):
            qkv_ref[:, 128 * j:128 * (j + 1)] = _rope_fwd(proj[:, 128 * j:128 * (j + 1)], cs, sn).astype(BF16)
        qkv_ref[:, 640:768] = proj[:, 640:768].astype(BF16)
        z_ref[...] = proj[:, 768:1280]
        xbc_ref[...] = proj[:, 1280:2304]
        dt_ref[...] = proj[:, 2304:2432]

    row = lambda w: pl.BlockSpec((tm, w), lambda i: (i, 0))
    full = lambda a: pl.BlockSpec(a.shape, lambda i: (0,) * a.ndim)
    return pl.pallas_call(
        body, name="in_proj_fwd", grid=(T // tm,),
        in_specs=[row(D), row(128), row(128), full(mod6), full(norm1_w), pl.BlockSpec(memory_space=pl.ANY)],
        out_specs=[row(768), row(512), row(1024), row(128), row(D)],
        out_shape=[jax.ShapeDtypeStruct((T, 768), BF16), jax.ShapeDtypeStruct((T, 512), F32),
                   jax.ShapeDtypeStruct((T, 1024), F32), jax.ShapeDtypeStruct((T, 128), F32),
                   jax.ShapeDtypeStruct((T, D), BF16)],
        scratch_shapes=[pltpu.VMEM((D, IN_PAD), BF16), pltpu.SemaphoreType.DMA],
        compiler_params=_cp("arbitrary"),
    )(x, cos, sin_s, mod6, norm1_w, w_pad)


def _head_variants(pair, j):
    lane = _iota(pair.shape, 1)
    lo = lane < 64
    kv = j // 2
    ev = jnp.where(lo, pair, 0.0)
    od = jnp.where(lo, 0.0, pair)
    if kv == 0:
        od = pltpu.roll(od, 64, 1)
    else:
        ev = pltpu.roll(ev, 64, 1)
    return ev.astype(BF16), od.astype(BF16)


def _kv_variants(vcat):
    lane = _iota(vcat.shape, 1)
    lo = lane < 64
    v0 = jnp.where(lo, vcat, 0.0)
    v1 = jnp.where(lo, 0.0, vcat)
    out = {
        (0, 0): v0, (0, 1): pltpu.roll(v0, 64, 1),
        (1, 0): pltpu.roll(v1, 64, 1), (1, 1): v1,
    }
    return {k: v.astype(BF16) for k, v in out.items()}


def _fold_masks(n):
    upper = _iota((BLK, BLK), 1) > _iota((BLK, BLK), 0)
    return upper, upper & (n == 0)


def _attn_fwd(qkv, sinks):
    T = qkv.shape[0]
    nb = T // BLK

    def body(sink_ref, q_ref, kc_ref, kp_ref, vc_ref, vp_ref, o_ref, lse_ref):
        n = pl.program_id(0)
        vpv = _kv_variants(vp_ref[...].astype(F32))
        vcv = _kv_variants(vc_ref[...].astype(F32))
        q_all = jnp.concatenate(
            [v for j in range(4) for v in _head_variants(q_ref[:, 128 * j:128 * (j + 1)].astype(F32), j)], axis=0)
        s_prev = _dot_nt(q_all, kp_ref[...])
        s_cur = _dot_nt(q_all, kc_ref[...])
        upper, dead = _fold_masks(n)
        lane = _iota((BLK, 128), 1)
        lse_acc = jnp.zeros((BLK, 128), F32)
        for jj in range(4):
            acc = jnp.zeros((BLK, 128), F32)
            for par in range(2):
                h = 2 * jj + par
                rows = slice(h * BLK, (h + 1) * BLK)
                sink = sink_ref[0, h]
                s = jnp.where(dead, NEG, jnp.where(upper, s_prev[rows], s_cur[rows]) * 0.125)
                m = jnp.maximum(jnp.max(s, axis=1, keepdims=True), sink)
                p = jnp.exp(s - m)
                den = jnp.sum(p, axis=1, keepdims=True) + jnp.exp(sink - m)
                pn = p * (1.0 / den)
                acc = (acc + _dot(jnp.where(upper, pn, 0.0).astype(BF16), vpv[(jj // 2, par)])
                       + _dot(jnp.where(upper, 0.0, pn).astype(BF16), vcv[(jj // 2, par)]))
                lse_acc = jnp.where(lane == h, m + jnp.log(den), lse_acc)
            o_ref[:, 128 * jj:128 * (jj + 1)] = acc.astype(BF16)
        lse_ref[...] = lse_acc

    prev = lambda n: jnp.maximum(n - 1, 0)
    return pl.pallas_call(
        body, name="attn_fwd", grid=(nb,),
        in_specs=[pl.BlockSpec(memory_space=pltpu.SMEM),
                  pl.BlockSpec((BLK, 512), lambda n: (n, 0)),
                  pl.BlockSpec((BLK, 128), lambda n: (n, 4)),
                  pl.BlockSpec((BLK, 128), lambda n: (prev(n), 4)),
                  pl.BlockSpec((BLK, 128), lambda n: (n, 5)),
                  pl.BlockSpec((BLK, 128), lambda n: (prev(n), 5))],
        out_specs=[pl.BlockSpec((BLK, 512), lambda n: (n, 0)), pl.BlockSpec((BLK, 128), lambda n: (n, 0))],
        out_shape=[jax.ShapeDtypeStruct((T, 512), BF16), jax.ShapeDtypeStruct((T, 128), F32)],
        compiler_params=_cp("parallel"),
    )(sinks, qkv, qkv, qkv, qkv, qkv)


def _attn_bwd(qkv, sinks, lse, dmix, cos, sin_s):
    T = qkv.shape[0]
    nb = T // BLK

    def body(sink_ref, q_ref, kc_ref, kp_ref, vc_ref, vp_ref, lse_ref, do_ref, cq_ref, sq_ref, ck_ref, sk_ref,
             out_ref, ds_ref, dq_car, dk_car, dv_car):
        n = pl.program_id(0)
        lane = _iota((BLK, 128), 1)

        @pl.when(n == 0)
        def _():
            ds_ref[...] = jnp.zeros_like(ds_ref)
            dq_car[...] = jnp.zeros_like(dq_car)
            dk_car[...] = jnp.zeros_like(dk_car)
            dv_car[...] = jnp.zeros_like(dv_car)

        @pl.when(n < nb)
        def _():
            kp, kc, vp, vc = kp_ref[...], kc_ref[...], vp_ref[...], vc_ref[...]
            kpv = _kv_variants(kp.astype(F32))
            kcv = _kv_variants(kc.astype(F32))
            lse_v = lse_ref[...]
            q_all = jnp.concatenate(
                [v for j in range(4) for v in _head_variants(q_ref[:, 128 * j:128 * (j + 1)].astype(F32), j)], axis=0)
            do_all = jnp.concatenate(
                [v for j in range(4) for v in _head_variants(do_ref[:, 128 * j:128 * (j + 1)], j)], axis=0)
            s_prev, s_cur = _dot_nt(q_all, kp), _dot_nt(q_all, kc)
            dp_prev, dp_cur = _dot_nt(do_all, vp), _dot_nt(do_all, vc)
            upper, dead = _fold_masks(n)
            out_ref[:, 0:512] = dq_car[...]
            dsk = jnp.zeros((1, 128), F32)
            ds_u, ds_l, p_u, p_l = [], [], [], []
            for jj in range(4):
                dq_acc = jnp.zeros((BLK, 128), F32)
                for par in range(2):
                    h = 2 * jj + par
                    rows = slice(h * BLK, (h + 1) * BLK)
                    lse_h = jnp.sum(jnp.where(lane == h, lse_v, 0.0), axis=1, keepdims=True)
                    s = jnp.where(dead, NEG, jnp.where(upper, s_prev[rows], s_cur[rows]) * 0.125)
                    p = jnp.exp(s - lse_h)
                    dp = jnp.where(upper, dp_prev[rows], dp_cur[rows])
                    delta = jnp.sum(p * dp, axis=1, keepdims=True)
                    ds = p * (dp - delta) * 0.125
                    dsu, dsl = jnp.where(upper, ds, 0.0).astype(BF16), jnp.where(upper, 0.0, ds).astype(BF16)
                    dq_acc = dq_acc + _dot(dsu, kpv[(jj // 2, par)]) + _dot(dsl, kcv[(jj // 2, par)])
                    ds_u.append(dsu)
                    ds_l.append(dsl)
                    p_u.append(jnp.where(upper, p, 0.0).astype(BF16))
                    p_l.append(jnp.where(upper, 0.0, p).astype(BF16))
                    dsk = dsk + jnp.where(lane[0:1] == h, -jnp.sum(jnp.exp(sink_ref[0, h] - lse_h) * delta), 0.0)
                dq_car[:, 128 * jj:128 * (jj + 1)] = _rope_bwd(dq_acc, cq_ref[...], sq_ref[...]).astype(BF16)
            stack = lambda parts: jnp.concatenate(parts, axis=0)
            dk_prev, dk_cur = _dot_tn(stack(ds_u), q_all), _dot_tn(stack(ds_l), q_all)
            dv_prev, dv_cur = _dot_tn(stack(p_u), do_all), _dot_tn(stack(p_l), do_all)
            ds_ref[...] += dsk
            out_ref[:, 512:640] = _rope_bwd(dk_car[...] + dk_prev, ck_ref[...], sk_ref[...]).astype(BF16)
            out_ref[:, 640:768] = (dv_car[...] + dv_prev).astype(BF16)
            dk_car[...] = dk_cur
            dv_car[...] = dv_cur

        @pl.when(n == nb)
        def _():
            out_ref[:, 0:512] = dq_car[...]
            out_ref[:, 512:640] = _rope_bwd(dk_car[...], ck_ref[...], sk_ref[...]).astype(BF16)
            out_ref[:, 640:768] = dv_car[...].astype(BF16)

    cur = lambda n: jnp.minimum(n, nb - 1)
    prev = lambda n: jnp.maximum(cur(n) - 1, 0)
    outb = lambda n: jnp.maximum(n - 1, 0)
    return pl.pallas_call(
        body, name="attn_bwd", grid=(nb + 1,),
        in_specs=[pl.BlockSpec(memory_space=pltpu.SMEM),
                  pl.BlockSpec((BLK, 512), lambda n: (cur(n), 0)),
                  pl.BlockSpec((BLK, 128), lambda n: (cur(n), 4)),
                  pl.BlockSpec((BLK, 128), lambda n: (prev(n), 4)),
                  pl.BlockSpec((BLK, 128), lambda n: (cur(n), 5)),
                  pl.BlockSpec((BLK, 128), lambda n: (prev(n), 5)),
                  pl.BlockSpec((BLK, 128), lambda n: (cur(n), 0)),
                  pl.BlockSpec((BLK, 512), lambda n: (cur(n), 0)),
                  pl.BlockSpec((BLK, 128), lambda n: (cur(n), 0)),
                  pl.BlockSpec((BLK, 128), lambda n: (cur(n), 0)),
                  pl.BlockSpec((BLK, 128), lambda n: (outb(n), 0)),
                  pl.BlockSpec((BLK, 128), lambda n: (outb(n), 0))],
        out_specs=[pl.BlockSpec((BLK, 768), lambda n: (outb(n), 0)), pl.BlockSpec((1, 128), lambda n: (0, 0))],
        out_shape=[jax.ShapeDtypeStruct((T, 768), BF16), jax.ShapeDtypeStruct((1, 128), F32)],
        scratch_shapes=[pltpu.VMEM((BLK, 512), BF16), pltpu.VMEM((BLK, 128), F32), pltpu.VMEM((BLK, 128), F32)],
        compiler_params=_cp("arbitrary"),
    )(sinks, qkv, qkv, qkv, qkv, qkv, lse, dmix, cos, sin_s, cos, sin_s)


def _ssd_mats():
    e = jnp.arange(SW)[None, :] // HD == jnp.arange(128)[:, None]
    tri = jnp.arange(BLK)[None, :] <= jnp.arange(BLK)[:, None]
    return (jnp.tile(e, (3, 1)).astype(BF16), jnp.tile(e.T, (2, 1)).astype(BF16),
            jnp.tile(tri, (1, 3)).astype(BF16), jnp.tile(tri.T, (1, 3)).astype(BF16))


def _pieces(x, n, axis):
    out, r = [], x
    for i in range(n):
        p = r.astype(BF16)
        out.append(p)
        if i + 1 < n:
            r = r - p.astype(F32)
    return jnp.concatenate(out, axis=axis)


def _expand(x, e3):
    return _dot(_pieces(x, 3, 1), e3)


def _head_sums(x, et2):
    return _dot(_pieces(x, 2, 1), et2)


def _run_sum(tri3, x):
    return _dot(tri3, _pieces(x, 3, 0))


def _shift_down(u, tail, j):
    rolled = pltpu.roll(u, j, 0)
    first = jnp.where(_iota(tail.shape, 0) < j, pltpu.roll(tail, j, 0), rolled[0:8])
    return jnp.concatenate([first, rolled[8:]], axis=0)


def _shift_up(d, head, j):
    rolled = pltpu.roll(d, BLK - j, 0)
    last = jnp.where(_iota(head.shape, 0) >= 8 - j, pltpu.roll(head, 8 - j, 0), rolled[BLK - 8:])
    return jnp.concatenate([rolled[:BLK - 8], last], axis=0)


def _ssd_parts(dtr, dtb, alog, e3, tril3):
    xx = dtr + dtb
    dt = jnp.maximum(xx, 0.0) + jnp.log(1.0 + jnp.exp(-jnp.abs(xx)))
    a_neg = -jnp.exp(alog)
    tril = _iota((BLK, BLK), 1) <= _iota((BLK, BLK), 0)
    cs = _run_sum(tril3, dt * a_neg)
    csx = _expand(cs, e3)
    last = csx[BLK - 1:BLK, :]
    return dict(xx=xx, dt=dt, a_neg=a_neg, tril=tril, cs=cs, cs_t=cs.T,
                ecsx=jnp.exp(csx), dtex=jnp.exp(last - csx), cdx=jnp.exp(last), dtx=_expand(dt, e3))


def _decay(parts, h):
    seg = parts["cs"][:, h:h + 1] - parts["cs_t"][h:h + 1, :]
    return jnp.exp(jnp.where(parts["tril"], seg, NEG))


def _group_cols(a, g):
    return a[:, 256 * g:256 * (g + 1)]


def _ssd_fwd(xbc, z, dtr, conv_w, conv_b, dtb, alog, dskx, ssm_w, mats):
    T = xbc.shape[0]
    nc = T // BLK

    def body(u_ref, tail_ref, z_ref, dtr_ref, cw_ref, cb_ref, dtb_ref, al_ref, dk_ref, sw_ref, e3_ref, tril3_ref,
             yn_ref, yp_ref, st_ref, co_ref, s_scr):
        n = pl.program_id(0)

        @pl.when(n == 0)
        def _():
            s_scr[...] = jnp.zeros_like(s_scr)

        u = u_ref[...]
        tail = jnp.where(n > 0, tail_ref[...], 0.0)
        co = cb_ref[...] + cw_ref[3:4, :] * u
        for j in range(1, CONVK):
            co = co + cw_ref[3 - j:4 - j, :] * _shift_down(u, tail, j)
        co_ref[...] = co
        xc = co * _sigmoid(co)
        pt = _ssd_parts(dtr_ref[...], dtb_ref[...], al_ref[...], e3_ref[...], tril3_ref[...])
        xs = xc[:, :SW]
        bm = [xc[:, 512:640].astype(BF16), xc[:, 640:768].astype(BF16)]
        cm = [xc[:, 768:896].astype(BF16), xc[:, 896:1024].astype(BF16)]
        s_in = s_scr[...]
        st_ref[0] = s_in
        xdt = xs * pt["dtx"]
        xde = (xdt * pt["dtex"]).astype(BF16)
        lane = _iota((BLK, 128), 1)
        lo = lane < 64
        ys, s_new = [], []
        for g in range(2):
            cb = _dot_nt(cm[g], bm[g])
            yoff = _dot(cm[g], _group_cols(s_in, g).astype(BF16))
            s_new.append(_dot_tn(bm[g], _group_cols(xde, g)))
            for jj in range(2):
                j = 2 * g + jj
                chunk = xdt[:, 128 * j:128 * (j + 1)]
                g_ev = (cb * _decay(pt, 2 * j)).astype(BF16)
                g_od = (cb * _decay(pt, 2 * j + 1)).astype(BF16)
                yd = _dot(g_ev, jnp.where(lo, chunk, 0.0).astype(BF16)) + _dot(g_od, jnp.where(lo, 0.0, chunk).astype(BF16))
                ys.append(yd + yoff[:, 128 * jj:128 * (jj + 1)] * pt["ecsx"][:, 128 * j:128 * (j + 1)])
        y = jnp.concatenate(ys, axis=1) + xs * dk_ref[...]
        s_scr[...] = s_in * pt["cdx"] + jnp.concatenate(s_new, axis=1)
        yp_ref[...] = y
        zv = z_ref[...]
        yz = y * (zv * _sigmoid(zv))
        outs = []
        for g in range(2):
            yg = _group_cols(yz, g)
            outs.append(yg * lax.rsqrt(jnp.mean(yg * yg, axis=-1, keepdims=True) + EPS))
        yn_ref[...] = (jnp.concatenate(outs, axis=1) * sw_ref[...]).astype(BF16)

    e3, _, tril3, _ = mats
    tail8 = lambda n: jnp.maximum(n * (BLK // 8) - 1, 0)
    full = lambda a: pl.BlockSpec(a.shape, lambda n: (0,) * a.ndim)
    return pl.pallas_call(
        body, name="ssd_fwd", grid=(nc,),
        in_specs=[pl.BlockSpec((BLK, CONVC), lambda n: (n, 0)), pl.BlockSpec((8, CONVC), lambda n: (tail8(n), 0)),
                  pl.BlockSpec((BLK, SW), lambda n: (n, 0)), pl.BlockSpec((BLK, 128), lambda n: (n, 0)),
                  full(conv_w), full(conv_b), full(dtb), full(alog), full(dskx), full(ssm_w), full(e3), full(tril3)],
        out_specs=[pl.BlockSpec((BLK, SW), lambda n: (n, 0)), pl.BlockSpec((BLK, SW), lambda n: (n, 0)),
                   pl.BlockSpec((1, NST, SW), lambda n: (n, 0, 0)), pl.BlockSpec((BLK, CONVC), lambda n: (n, 0))],
        out_shape=[jax.ShapeDtypeStruct((T, SW), BF16), jax.ShapeDtypeStruct((T, SW), F32),
                   jax.ShapeDtypeStruct((nc, NST, SW), F32), jax.ShapeDtypeStruct((T, CONVC), F32)],
        scratch_shapes=[pltpu.VMEM((NST, SW), F32)],
        compiler_params=_cp("arbitrary"),
    )(xbc, xbc, z, dtr, conv_w, conv_b, dtb, alog, dskx, ssm_w, e3, tril3)


def _ssd_bwd(xbc, co_all, z, dtr, ypre, states, dmix, conv_w, dtb, alog, dskx, ssm_w, mats):
    T = xbc.shape[0]
    nc = T // BLK

    def body(u_ref, co_ref, z_ref, dtr_ref, yp_ref, st_ref, dyn_ref, cw_ref, dtb_ref, al_ref, dk_ref, sw_ref,
             e3_ref, et2_ref, tril3_ref, triu3_ref,
             out_ref, dcw_ref, dcb_ref, dsw_ref, dsk_ref, ddtb_ref, dav_ref, ds_scr, dco_scr, dskx_scr):
        i = pl.program_id(0)

        @pl.when(i == 0)
        def _():
            for r in (dcw_ref, dcb_ref, dsw_ref, dsk_ref, ddtb_ref, dav_ref, ds_scr, dco_scr, dskx_scr):
                r[...] = jnp.zeros_like(r)

        co = co_ref[...]
        sg = _sigmoid(co)
        xc = co * sg
        pt = _ssd_parts(dtr_ref[...], dtb_ref[...], al_ref[...], e3_ref[...], tril3_ref[...])
        dtx, ecsx, dtex, cdx = pt["dtx"], pt["ecsx"], pt["dtex"], pt["cdx"]
        xs = xc[:, :SW]
        bm = [xc[:, 512:640].astype(BF16), xc[:, 640:768].astype(BF16)]
        cm = [xc[:, 768:896].astype(BF16), xc[:, 896:1024].astype(BF16)]
        s_in = st_ref[0]
        ds_out = ds_scr[...]
        e_t = et2_ref[...]

        zv = z_ref[...]
        sz = _sigmoid(zv)
        silu_z = zv * sz
        ypre = yp_ref[...]
        yz = ypre * silu_z
        dyn = dyn_ref[...]
        sw = sw_ref[...]
        dyz, yns = [], []
        for g in range(2):
            yg = _group_cols(yz, g)
            r = lax.rsqrt(jnp.mean(yg * yg, axis=-1, keepdims=True) + EPS)
            yn = yg * r
            dg = _group_cols(dyn, g) * _group_cols(sw, g)
            dyz.append(r * (dg - yn * jnp.mean(dg * yn, axis=-1, keepdims=True)))
            yns.append(yn)
        dyz = jnp.concatenate(dyz, axis=1)
        dsw_ref[...] += jnp.sum(dyn * jnp.concatenate(yns, axis=1), axis=0, keepdims=True)
        dy = dyz * silu_z
        dz = dyz * ypre * (sz * (1.0 + zv * (1.0 - sz)))

        xdt = xs * dtx
        xdt_b = xdt.astype(BF16)
        edy = (ecsx * dy).astype(BF16)
        xde = (xdt * dtex).astype(BF16)
        lane = _iota((BLK, 128), 1)
        lo = lane < 64
        row8 = _iota((8, 128), 0)
        dcs = jnp.zeros((BLK, 128), F32)
        col_rows = jnp.zeros((8, 128), F32)
        dxdt, bds, yoff, dbs, dcs_g, ds_new = [], [], [], [], [], []
        for g in range(2):
            s_g = _group_cols(s_in, g).astype(BF16)
            dso_g = _group_cols(ds_out, g).astype(BF16)
            cb = _dot_nt(cm[g], bm[g])
            bds.append(_dot(bm[g], dso_g))
            yoff.append(_dot(cm[g], s_g))
            dcb_g = jnp.zeros((BLK, BLK), F32)
            for jj in range(2):
                j = 2 * g + jj
                dy_c = dy[:, 128 * j:128 * (j + 1)]
                xdt_c = xdt_b[:, 128 * j:128 * (j + 1)]
                acc = jnp.zeros((BLK, 128), F32)
                for par in range(2):
                    h = 2 * j + par
                    lm = _decay(pt, h)
                    gm = cb * lm
                    dy_m = (jnp.where(lo, dy_c, 0.0) if par == 0 else jnp.where(lo, 0.0, dy_c)).astype(BF16)
                    dg_h = _dot_nt(dy_m, xdt_c)
                    w_h = dg_h * gm
                    dcs = dcs + jnp.where(lane == h, jnp.sum(w_h, axis=1, keepdims=True), 0.0)
                    col_rows = col_rows + jnp.where(row8 == h, jnp.sum(w_h, axis=0, keepdims=True), 0.0)
                    dcb_g = dcb_g + dg_h * lm
                    acc = acc + _dot_tn(gm.astype(BF16), dy_m)
                dxdt.append(acc)
            dcb_b = dcb_g.astype(BF16)
            dcs_g.append(_dot(dcb_b, bm[g]) + _dot_nt(_group_cols(edy, g), s_g))
            dbs.append(_dot_tn(dcb_b, cm[g]) + _dot_nt(_group_cols(xde, g), dso_g))
            ds_new.append(_dot_tn(cm[g], _group_cols(edy, g)))
        bds = jnp.concatenate(bds, axis=1)
        yoff = jnp.concatenate(yoff, axis=1) * ecsx
        dxdt = jnp.concatenate(dxdt, axis=1) + dtex * bds
        ds_scr[...] = cdx * ds_out + jnp.concatenate(ds_new, axis=1)

        t_m = _head_sums(dtex * xdt * bds, e_t)
        colsum_t = jnp.concatenate([col_rows, jnp.zeros((BLK - 8, 128), F32)], axis=0).T
        cd = jnp.exp(pt["cs"][BLK - 1:BLK, :])
        sds = jnp.sum(s_in * ds_out, axis=0, keepdims=True)
        last_row = jnp.sum(t_m, axis=0, keepdims=True) + cd * _head_sums(jnp.broadcast_to(sds, (8, SW)), e_t)[0:1]
        dcs = dcs - colsum_t + _head_sums(dy * yoff, e_t) - t_m
        dcs = dcs + jnp.where(_iota((BLK, 128), 0) == BLK - 1, last_row, 0.0)
        da = _run_sum(triu3_ref[...], dcs)
        dt = pt["dt"]
        ddt = da * pt["a_neg"] + _head_sums(dxdt * xs, e_t)
        dav_ref[...] += jnp.sum(da * dt, axis=0, keepdims=True)
        ddtr = ddt * _sigmoid(pt["xx"])
        ddtb_ref[...] += jnp.sum(ddtr, axis=0, keepdims=True)
        dxs = dxdt * dtx + dy * dk_ref[...]
        dskx_scr[...] += jnp.sum(dy * xs, axis=0, keepdims=True)
        dxc = jnp.concatenate([dxs, dbs[0], dbs[1], dcs_g[0], dcs_g[1]], axis=1)
        dco = dxc * (sg * (1.0 + co * (1.0 - sg)))

        dcb_ref[...] += jnp.sum(dco, axis=0, keepdims=True)
        u = u_ref[...]
        head = dco_scr[...]
        du = jnp.zeros_like(dco)
        for j in range(CONVK):
            up_j = dco if j == 0 else _shift_up(dco, head, j)
            dcw_ref[3 - j:4 - j, :] += jnp.sum(up_j * u, axis=0, keepdims=True)
            du = du + cw_ref[3 - j:4 - j, :] * up_j
        dco_scr[...] = dco[0:8]
        out_ref[:, 0:512] = dz.astype(BF16)
        out_ref[:, 512:1536] = du.astype(BF16)
        out_ref[:, 1536:1664] = ddtr.astype(BF16)

        @pl.when(i == nc - 1)
        def _():
            dsk_ref[...] = _head_sums(jnp.broadcast_to(dskx_scr[...], (8, SW)), e_t)[0:1]

    e3, et2, tril3, triu3 = mats
    rev = lambda i: nc - 1 - i
    full = lambda a: pl.BlockSpec(a.shape, lambda i: (0,) * a.ndim)
    acc = lambda r, c: pl.BlockSpec((r, c), lambda i: (0, 0))
    return pl.pallas_call(
        body, name="ssd_bwd", grid=(nc,),
        in_specs=[pl.BlockSpec((BLK, CONVC), lambda i: (rev(i), 0)), pl.BlockSpec((BLK, CONVC), lambda i: (rev(i), 0)),
                  pl.BlockSpec((BLK, SW), lambda i: (rev(i), 0)), pl.BlockSpec((BLK, 128), lambda i: (rev(i), 0)),
                  pl.BlockSpec((BLK, SW), lambda i: (rev(i), 0)), pl.BlockSpec((1, NST, SW), lambda i: (rev(i), 0, 0)),
                  pl.BlockSpec((BLK, SW), lambda i: (rev(i), 1)),
                  full(conv_w), full(dtb), full(alog), full(dskx), full(ssm_w),
                  full(e3), full(et2), full(tril3), full(triu3)],
        out_specs=[pl.BlockSpec((BLK, 1664), lambda i: (rev(i), 0)),
                   acc(CONVK, CONVC), acc(1, CONVC), acc(1, SW), acc(1, 128), acc(1, 128), acc(1, 128)],
        out_shape=[jax.ShapeDtypeStruct((T, 1664), BF16),
                   jax.ShapeDtypeStruct((CONVK, CONVC), F32), jax.ShapeDtypeStruct((1, CONVC), F32),
                   jax.ShapeDtypeStruct((1, SW), F32), jax.ShapeDtypeStruct((1, 128), F32),
                   jax.ShapeDtypeStruct((1, 128), F32), jax.ShapeDtypeStruct((1, 128), F32)],
        scratch_shapes=[pltpu.VMEM((NST, SW), F32), pltpu.VMEM((8, CONVC), F32), pltpu.VMEM((1, SW), F32)],
        compiler_params=_cp("arbitrary"),
    )(xbc, co_all, z, dtr, ypre, states, dmix, conv_w, dtb, alog, dskx, ssm_w, e3, et2, tril3, triu3)


def _mix_ffn(x, attn, ynorm, tgt, mod6, norm2_w, final_w, w_out, w_gu, w_gu_own, s_arr, w_dn, tm):
    T = x.shape[0]
    nt = T // tm

    def body(x_ref, a_ref, y_ref, t_ref, mod_ref, n2_ref, fw_ref, wo_hbm, wgu_hbm, own_hbm, s_ref, wdn_hbm,
             sq_ref, dmix_ref, dx1_ref, h2_ref, act_ref, df_ref, dgu_ref, do_ref, sm_ref,
             wo, wgu, wdn, sems):
        i = pl.program_id(0)

        @pl.when(i == 0)
        def _():
            cps = [pltpu.make_async_copy(s, d, sems.at[k]) for k, (s, d) in
                   enumerate(((wo_hbm, wo), (wgu_hbm, wgu), (wdn_hbm, wdn)))]
            for c in cps:
                c.start()
            for c in cps:
                c.wait()
            own = pltpu.make_async_copy(
                own_hbm, wgu.at[:, pl.ds(pl.multiple_of(s_ref[0] * GU_SH, 128), GU_SH)], sems.at[3])
            own.start()
            own.wait()
            sq_ref[...] = jnp.zeros_like(sq_ref)
            sm_ref[...] = jnp.zeros_like(sm_ref)

        gate1, shift2, scale2, gate2 = mod_ref[2:3, :], mod_ref[3:4, :], mod_ref[4:5, :], mod_ref[5:6, :]
        n2w, fw = n2_ref[...], fw_ref[...]
        o = _dot(a_ref[...], wo[0:AW, :]) + _dot(y_ref[...], wo[AW:D, :])
        x1 = x_ref[...] + gate1 * o
        r2 = lax.rsqrt(jnp.mean(x1 * x1, axis=-1, keepdims=True) + EPS)
        xh2 = x1 * r2
        n2 = xh2 * n2w
        h2b = (n2 * (1.0 + scale2) + shift2).astype(BF16)
        h2_ref[...] = h2b
        f = jnp.zeros((tm, D), F32)
        saved = []
        for a, b in FF_SPLITS:
            gp = _dot(h2b, wgu[:, a:b])
            upj = _dot(h2b, wgu[:, DFF + a:DFF + b])
            sg = _sigmoid(gp)
            sl = gp * sg
            actb = (sl * upj).astype(BF16)
            act_ref[:, a:b] = actb
            f = f + _dot(actb, wdn[a:b, :])
            saved.append((gp, upj, sg, sl))
        x2 = x1 + gate2 * f
        r3 = lax.rsqrt(jnp.mean(x2 * x2, axis=-1, keepdims=True) + EPS)
        xh3 = x2 * r3
        err = xh3 * fw - t_ref[...]
        sq_ref[...] += jnp.sum(err * err, axis=0, keepdims=True)
        dy = err * (1.0 / D)
        dfw = jnp.sum(dy * xh3, axis=0, keepdims=True)
        dxh3 = dy * fw
        dx2 = r3 * (dxh3 - xh3 * jnp.mean(dxh3 * xh3, axis=-1, keepdims=True))
        dgate2 = jnp.sum(dx2 * f, axis=0, keepdims=True)
        dfb = (dx2 * gate2).astype(BF16)
        df_ref[...] = dfb
        dh2 = jnp.zeros((tm, D), F32)
        for (a, b), (gp, upj, sg, sl) in zip(FF_SPLITS, saved):
            dact = _dot_nt(dfb, wdn[a:b, :])
            dg = (dact * upj * (sg * (1.0 + gp * (1.0 - sg)))).astype(BF16)
            du = (dact * sl).astype(BF16)
            dgu_ref[:, a:b] = dg
            dgu_ref[:, DFF + a:DFF + b] = du
            dh2 = dh2 + _dot_nt(dg, wgu[:, a:b]) + _dot_nt(du, wgu[:, DFF + a:DFF + b])
        dshift2 = jnp.sum(dh2, axis=0, keepdims=True)
        dscale2 = jnp.sum(dh2 * n2, axis=0, keepdims=True)
        dn2 = dh2 * (1.0 + scale2)
        dn2w = jnp.sum(dn2 * xh2, axis=0, keepdims=True)
        dxh2 = dn2 * n2w
        dx1 = dx2 + r2 * (dxh2 - xh2 * jnp.mean(dxh2 * xh2, axis=-1, keepdims=True))
        dx1_ref[...] = dx1
        dgate1 = jnp.sum(dx1 * o, axis=0, keepdims=True)
        dob = (dx1 * gate1).astype(BF16)
        do_ref[...] = dob
        dmix_ref[...] = _dot_nt(dob, wo[...])
        sm_ref[...] += jnp.concatenate(
            [dfw, dn2w, dshift2, dscale2, dgate2, dgate1, jnp.zeros((2, D), F32)], axis=0)

    row = lambda w: pl.BlockSpec((tm, w), lambda i: (i, 0))
    full = lambda a: pl.BlockSpec(a.shape, lambda i: (0,) * a.ndim)
    anyspec = pl.BlockSpec(memory_space=pl.ANY)
    return pl.pallas_call(
        body, name="mix_ffn", grid=(nt,),
        in_specs=[row(D), row(AW), row(SW), row(D), full(mod6), full(norm2_w), full(final_w), anyspec, anyspec, anyspec,
                  pl.BlockSpec(memory_space=pltpu.SMEM), anyspec],
        out_specs=[pl.BlockSpec((1, D), lambda i: (0, 0)), row(D), row(D), row(D),
                   row(DFF), row(D), row(2 * DFF), row(D), pl.BlockSpec((8, D), lambda i: (0, 0))],
        out_shape=[jax.ShapeDtypeStruct((1, D), F32), jax.ShapeDtypeStruct((T, D), F32), jax.ShapeDtypeStruct((T, D), F32),
                   jax.ShapeDtypeStruct((T, D), BF16), jax.ShapeDtypeStruct((T, DFF), BF16),
                   jax.ShapeDtypeStruct((T, D), BF16), jax.ShapeDtypeStruct((T, 2 * DFF), BF16),
                   jax.ShapeDtypeStruct((T, D), BF16), jax.ShapeDtypeStruct((8, D), F32)],
        scratch_shapes=[pltpu.VMEM((D, D), BF16), pltpu.VMEM((D, 2 * DFF), BF16), pltpu.VMEM((DFF, D), BF16),
                        pltpu.SemaphoreType.DMA((4,))],
        compiler_params=_cp("arbitrary"),
    )(x, attn, ynorm, tgt, mod6, norm2_w, final_w, w_out, w_gu, w_gu_own, s_arr, w_dn)


def _in_proj_bwd(x, dx1, dqkv, dzxd, mod6, norm1_w, w_pad, tm):
    T = x.shape[0]

    def body(x_ref, dx1_ref, dq_ref, dz_ref, mod_ref, nw_ref, w_hbm, gx_ref, sm_ref, w_vmem, sem):
        _load_resident(w_hbm, w_vmem, sem)

        @pl.when(pl.program_id(0) == 0)
        def _():
            sm_ref[...] = jnp.zeros_like(sm_ref)

        nw = nw_ref[...]
        scale1 = mod_ref[1:2, :]
        sums = jnp.zeros((8, D), F32)
        for rows in (slice(0, tm // 2), slice(tm // 2, tm)):
            dh = _dot_nt(dq_ref[rows, :], w_vmem[:, 0:768]) + _dot_nt(dz_ref[rows, :], w_vmem[:, 768:IN_PAD])
            xv = x_ref[rows, :]
            r = lax.rsqrt(jnp.mean(xv * xv, axis=-1, keepdims=True) + EPS)
            xh = xv * r
            n1 = xh * nw
            dshift = jnp.sum(dh, axis=0, keepdims=True)
            dscale = jnp.sum(dh * n1, axis=0, keepdims=True)
            dn = dh * (1.0 + scale1)
            dnw = jnp.sum(dn * xh, axis=0, keepdims=True)
            dxh = dn * nw
            gx_ref[rows, :] = dx1_ref[rows, :] + r * (dxh - xh * jnp.mean(dxh * xh, axis=-1, keepdims=True))
            sums = sums + jnp.concatenate([dnw, dshift, dscale, jnp.zeros((5, D), F32)], axis=0)
        sm_ref[...] += sums

    row = lambda w: pl.BlockSpec((tm, w), lambda i: (i, 0))
    full = lambda a: pl.BlockSpec(a.shape, lambda i: (0,) * a.ndim)
    return pl.pallas_call(
        body, name="in_proj_bwd", grid=(T // tm,),
        in_specs=[row(D), row(D), row(768), row(1664), full(mod6), full(norm1_w), pl.BlockSpec(memory_space=pl.ANY)],
        out_specs=[row(D), pl.BlockSpec((8, D), lambda i: (0, 0))],
        out_shape=[jax.ShapeDtypeStruct((T, D), F32), jax.ShapeDtypeStruct((8, D), F32)],
        scratch_shapes=[pltpu.VMEM((D, IN_PAD), BF16), pltpu.SemaphoreType.DMA],
        compiler_params=_cp("arbitrary"),
    )(x, dx1, dqkv, dzxd, mod6, norm1_w, w_pad)


def _tn_matmul(a, b, K, N, tt, name, dep):
    T = a.shape[0]
    ja, jb = a.shape[1] // K, b.shape[1] // N
    J = max(ja, jb)

    def body(a_ref, b_ref, dep_ref, o_ref):
        t = pl.program_id(1)
        prod = _dot_tn(a_ref[...], b_ref[...])

        @pl.when(t == 0)
        def _():
            o_ref[0] = prod

        @pl.when(t > 0)
        def _():
            o_ref[0] += prod

    return pl.pallas_call(
        body, name=name, grid=(J, T // tt),
        in_specs=[pl.BlockSpec((tt, K), lambda j, t: (t, j if ja > 1 else 0)),
                  pl.BlockSpec((tt, N), lambda j, t: (t, j if jb > 1 else 0)),
                  pl.BlockSpec((8, 128), lambda j, t: (0, 0))],
        out_specs=pl.BlockSpec((1, K, N), lambda j, t: (j, 0, 0)),
        out_shape=jax.ShapeDtypeStruct((J, K, N), F32),
        compiler_params=_cp("parallel", "arbitrary"),
    )(a, b, dep)


def _adam_math(w, g, m, v):
    m = B1 * m + (1.0 - B1) * g
    v = B2 * v + (1.0 - B2) * (g * g)
    m_hat = m / (1.0 - B1 ** STEP)
    v_hat = v / (1.0 - B2 ** STEP)
    delta = -LR * (m_hat / (jnp.sqrt(v_hat) + AEPS) + WD * w)
    return delta, m, v


def _adam_2d(w, mine, land, m, v, c_arr, rb, name):
    R, C = w.shape
    nbh = R // 2 // rb

    def body(c_ref, w_ref, mine_ref, land_ref, m_ref, v_ref, go_ref, d_ref, mo_ref, vo_ref):
        g = jnp.where(pl.program_id(0) // nbh == c_ref[0], mine_ref[...], land_ref[...])
        d, mn, vn = _adam_math(w_ref[...], g, m_ref[...], v_ref[...])
        go_ref[...] = g
        d_ref[...] = d
        mo_ref[...] = mn
        vo_ref[...] = vn

    spec = pl.BlockSpec((rb, C), lambda i, c_ref: (i, 0))
    mine_spec = pl.BlockSpec((rb, C), lambda i, c_ref: (jnp.clip(i - c_ref[0] * nbh, 0, nbh - 1), 0))
    return pl.pallas_call(
        body, name=name,
        grid_spec=pltpu.PrefetchScalarGridSpec(
            num_scalar_prefetch=1, grid=(R // rb,), in_specs=[spec, mine_spec, spec, spec, spec], out_specs=[spec] * 4),
        out_shape=[jax.ShapeDtypeStruct((R, C), F32)] * 4, compiler_params=_cp("parallel"),
    )(c_arr, w, mine, land, m, v)


def _adam_w_ada(sc_all, dmod_s, w, m, v, rb):
    R, C = w.shape

    def body(sc_ref, dm_ref, w_ref, m_ref, v_ref, g_ref, d_ref, mo_ref, vo_ref):
        g = lax.dot_general(sc_ref[...], dm_ref[...], (((0,), (0,)), ((), ())), precision=HI, preferred_element_type=F32)
        d, mn, vn = _adam_math(w_ref[...], g, m_ref[...], v_ref[...])
        g_ref[...] = g
        d_ref[...] = d
        mo_ref[...] = mn
        vo_ref[...] = vn

    spec = pl.BlockSpec((rb, C), lambda i: (i, 0))
    return pl.pallas_call(
        body, name="adam_w_ada", grid=(R // rb,),
        in_specs=[pl.BlockSpec((8, rb), lambda i: (0, i)), pl.BlockSpec((8, C), lambda i: (0, 0)), spec, spec, spec],
        out_specs=[spec] * 4, out_shape=[jax.ShapeDtypeStruct((R, C), F32)] * 4, compiler_params=_cp("parallel"),
    )(sc_all, dmod_s, w, m, v)


def _adam_small(grads, ws, ms, vs):
    k = len(ws)

    def body(*refs):
        g, w, m, v = refs[0:k], refs[k:2 * k], refs[2 * k:3 * k], refs[3 * k:4 * k]
        g_o, d_o, m_o, v_o = refs[4 * k:5 * k], refs[5 * k:6 * k], refs[6 * k:7 * k], refs[7 * k:8 * k]
        for i in range(k):
            gi = g[i][...]
            d, mn, vn = _adam_math(w[i][...], gi, m[i][...], v[i][...])
            g_o[i][...] = gi
            d_o[i][...] = d
            m_o[i][...] = mn
            v_o[i][...] = vn

    shapes = [jax.ShapeDtypeStruct(w.shape, F32) for w in ws]
    vm = pl.BlockSpec(memory_space=pltpu.VMEM)
    outs = pl.pallas_call(
        body, name="adam_small", in_specs=[vm] * (4 * k), out_specs=[vm] * (4 * k), out_shape=shapes * 4,
    )(*grads, *ws, *ms, *vs)
    return outs[0:k], outs[k:2 * k], outs[2 * k:3 * k], outs[3 * k:4 * k]


def _pos():
    return lax.axis_index("x"), lax.axis_index("y"), lax.axis_index("c")


def _flip(v, bit):
    return 1 - v if bit else v


def _peer(k):
    x, y, c = _pos()
    return (_flip(x, (k >> 2) & 1), _flip(y, (k >> 1) & 1), _flip(c, k & 1))


def _logical(p):
    return 4 * p[0] + 2 * p[1] + p[2]


def _gather8(src_ref, dst_ref, send_sems, recv_sems):
    me = _logical(_pos())
    dst_ref[pl.ds(me, 1)] = src_ref[...][None]
    copies = []
    for k in range(1, 8):
        cp = pltpu.make_async_remote_copy(src_ref, dst_ref.at[me], send_sems.at[k - 1], recv_sems.at[k - 1],
                                          device_id=_peer(k), device_id_type=MESH)
        cp.start()
        copies.append(cp)
    for k in range(1, 8):
        pltpu.make_async_remote_copy(src_ref, dst_ref.at[_logical(_peer(k))], send_sems.at[k - 1], recv_sems.at[k - 1],
                                     device_id=_peer(k), device_id_type=MESH).wait_recv()
    for cp in copies:
        cp.wait_send()


def _rows_select(ref3, width):
    row = _iota((8, width), 0)
    out = jnp.zeros((8, width), F32)
    for i in range(8):
        out = jnp.where(row == i, ref3[i][:, 0:width], out)
    return out


def _mod_exchange(payload, w_ada_s, b_ada4):
    n_sh = w_ada_s.shape[1]

    def body(pay_ref, w_ref, b_ref, gat_ref, mod_ref, token, p3, sa, ra, sb, rb):
        token[...] = jnp.zeros_like(token)
        x, y, c = _pos()
        me = _logical((x, y, c))
        my_s = 2 * x + y
        _gather8(pay_ref, gat_ref, sa, ra)
        cmat = _rows_select(gat_ref, D)
        prod = _dot_hi(cmat * _sigmoid(cmat), w_ref[...])
        for b in range(8):
            p3[b] = prod[b:b + 1, :]
        mod_ref[pl.ds(my_s, 1)] = p3[pl.ds(me, 1)] + b_ref[pl.ds(my_s, 1)]
        ks = (2, 4, 6)
        copies = []
        for i, k in enumerate(ks):
            pr = _peer(k)
            cp = pltpu.make_async_remote_copy(p3.at[_logical(pr)], mod_ref.at[my_s], sb.at[i], rb.at[i],
                                              device_id=pr, device_id_type=MESH)
            cp.start()
            copies.append(cp)
        for i, k in enumerate(ks):
            pr = _peer(k)
            s_src = 2 * pr[0] + pr[1]
            pltpu.make_async_remote_copy(p3.at[0], mod_ref.at[s_src], sb.at[i], rb.at[i],
                                         device_id=pr, device_id_type=MESH).wait_recv()
            mod_ref[pl.ds(s_src, 1)] = mod_ref[pl.ds(s_src, 1)] + b_ref[pl.ds(s_src, 1)]
        for cp in copies:
            cp.wait_send()

    vm = pl.BlockSpec(memory_space=pltpu.VMEM)
    return pl.pallas_call(
        body, name="mod_exchange", in_specs=[vm, vm, vm], out_specs=[vm, vm, vm],
        out_shape=[jax.ShapeDtypeStruct((8, 1, payload.shape[1]), F32), jax.ShapeDtypeStruct((4, 1, n_sh), F32),
                   jax.ShapeDtypeStruct((8, 128), F32)],
        scratch_shapes=[pltpu.VMEM((8, 1, n_sh), F32), pltpu.SemaphoreType.DMA((7,)), pltpu.SemaphoreType.DMA((7,)),
                        pltpu.SemaphoreType.DMA((3,)), pltpu.SemaphoreType.DMA((3,))],
        compiler_params=pltpu.CompilerParams(vmem_limit_bytes=VMEM_LIMIT),
    )(payload, w_ada_s, b_ada4)


def _chips():
    x, y, _ = _pos()
    out = []
    for k in (1, 2, 3):
        px, py = _flip(x, (k >> 1) & 1), _flip(y, k & 1)
        out.append((px, py, 2 * px + py))
    return out


def _half_rows(ref, which):
    half = ref.shape[-2] // 2
    return pl.ds(pl.multiple_of(which * half, 8), half)


def _weight_gather(shards):
    nw = len(shards)

    def body(*refs):
        ins, outs, token = refs[:nw], refs[nw:2 * nw], refs[2 * nw]
        send, recv, fsend, frecv = refs[2 * nw + 1:]
        token[...] = jnp.zeros_like(token)
        x, y, c = _pos()
        my_s = 2 * x + y
        sib = (x, y, 1 - c)
        chips = _chips()
        sends = []
        for w in range(nw):
            mine = _half_rows(ins[w], c)
            for k, (px, py, _) in enumerate(chips):
                cp = pltpu.make_async_remote_copy(ins[w].at[mine], outs[w].at[my_s, mine], send.at[3 * w + k],
                                                  recv.at[3 * w + k], device_id=(px, py, c), device_id_type=MESH)
                cp.start()
                sends.append(cp)
        for w in range(nw):
            mine = _half_rows(ins[w], c)
            for k, (px, py, ps) in enumerate(chips):
                got = outs[w].at[ps, mine]
                pltpu.make_async_remote_copy(got, got, send.at[3 * w + k], recv.at[3 * w + k],
                                             device_id=(px, py, c), device_id_type=MESH).wait_recv()
                cp = pltpu.make_async_remote_copy(got, got, fsend.at[3 * w + k], frecv.at[3 * w + k],
                                                  device_id=sib, device_id_type=MESH)
                cp.start()
                sends.append(cp)
        for w in range(nw):
            other = _half_rows(ins[w], 1 - c)
            for k, (px, py, ps) in enumerate(chips):
                got = outs[w].at[ps, other]
                pltpu.make_async_remote_copy(got, got, fsend.at[3 * w + k], frecv.at[3 * w + k],
                                             device_id=sib, device_id_type=MESH).wait_recv()
        for cp in sends:
            cp.wait_send()

    hbm = pl.BlockSpec(memory_space=pltpu.HBM)
    return pl.pallas_call(
        body, name="weight_gather", in_specs=[hbm] * nw,
        out_specs=[hbm] * nw + [pl.BlockSpec(memory_space=pltpu.VMEM)],
        out_shape=[pltpu.HBM((4,) + s.shape, s.dtype) for s in shards] + [jax.ShapeDtypeStruct((8, 128), F32)],
        scratch_shapes=[pltpu.SemaphoreType.DMA((3 * nw,)), pltpu.SemaphoreType.DMA((3 * nw,)),
                        pltpu.SemaphoreType.DMA((3 * nw,)), pltpu.SemaphoreType.DMA((3 * nw,))],
    )(*shards)


def _small_reduce(vec):
    n = vec.shape[1]

    def body(v_ref, tot_ref, gat_ref, sa, ra):
        _gather8(v_ref, gat_ref, sa, ra)
        tot = gat_ref[0]
        for i in range(1, 8):
            tot = tot + gat_ref[i]
        tot_ref[...] = tot

    vm = pl.BlockSpec(memory_space=pltpu.VMEM)
    return pl.pallas_call(
        body, name="small_reduce", in_specs=[vm], out_specs=[vm, vm],
        out_shape=[jax.ShapeDtypeStruct((1, n), F32), jax.ShapeDtypeStruct((8, 1, n), F32)],
        scratch_shapes=[pltpu.SemaphoreType.DMA((7,)), pltpu.SemaphoreType.DMA((7,))],
    )(vec)


def _add_half(g, sib, c_arr, rb, name):
    _, R, C = g.shape
    half = R // 2
    nb = half // rb

    def body(c_ref, g_ref, s_ref, o_ref):
        o_ref[...] = (g_ref[...] + s_ref[...]).astype(BF16)

    return pl.pallas_call(
        body, name=name,
        grid_spec=pltpu.PrefetchScalarGridSpec(
            num_scalar_prefetch=1, grid=(4, nb),
            in_specs=[pl.BlockSpec((1, rb, C), lambda s, i, c_ref: (s, c_ref[0] * nb + i, 0)),
                      pl.BlockSpec((1, rb, C), lambda s, i, c_ref: (s, i, 0))],
            out_specs=pl.BlockSpec((1, rb, C), lambda s, i, c_ref: (s, i, 0))),
        out_shape=jax.ShapeDtypeStruct((4, half, C), BF16),
        compiler_params=_cp("parallel", "parallel"),
    )(c_arr, g, sib)


def _sum4(parts, land, s_arr, rb, name):
    _, H, C = land.shape

    def body(s_ref, own_ref, r_ref, o_ref):
        own = own_ref[0].astype(F32)
        tot = jnp.zeros((rb, C), F32)
        for j in range(4):
            tot = tot + jnp.where(s_ref[0] == j, own, r_ref[j].astype(F32))
        o_ref[...] = tot

    return pl.pallas_call(
        body, name=name,
        grid_spec=pltpu.PrefetchScalarGridSpec(
            num_scalar_prefetch=1, grid=(H // rb,),
            in_specs=[pl.BlockSpec((1, rb, C), lambda i, s_ref: (s_ref[0], i, 0)),
                      pl.BlockSpec((4, rb, C), lambda i, s_ref: (0, i, 0))],
            out_specs=pl.BlockSpec((rb, C), lambda i, s_ref: (i, 0))),
        out_shape=jax.ShapeDtypeStruct((H, C), F32), compiler_params=_cp("parallel"),
    )(s_arr, parts, land)


HBM_SPEC = pl.BlockSpec(memory_space=pltpu.HBM)
SEM_SPEC = pl.BlockSpec(memory_space=pltpu.SEMAPHORE)
EFFECT = pltpu.SideEffectType.DATAFLOW_SIDE_EFFECTING


def _split_start(name, bufs, n_sem, plan):
    nb = len(bufs)

    def body(*refs):
        ins, send, recv, token = refs[:nb], refs[nb], refs[nb + 1], refs[-1]
        for i, (src, dst, dev, _) in enumerate(plan(ins)):
            pltpu.make_async_remote_copy(src, dst, send.at[i], recv.at[i], device_id=dev, device_id_type=MESH).start()
        token[...] = jnp.zeros_like(token)

    outs = pl.pallas_call(
        body, name=name,
        out_shape=(pltpu.SemaphoreType.DMA((n_sem,)), pltpu.SemaphoreType.DMA((n_sem,)),
                   *[pltpu.HBM(b.shape, b.dtype) for b in bufs], jax.ShapeDtypeStruct((8, 128), F32)),
        in_specs=[HBM_SPEC] * nb,
        out_specs=(SEM_SPEC, SEM_SPEC, *([HBM_SPEC] * nb), pl.BlockSpec(memory_space=pltpu.VMEM)),
        input_output_aliases={i: 2 + i for i in range(nb)},
        compiler_params=pltpu.CompilerParams(has_side_effects=EFFECT),
    )(*[pltpu.with_memory_space_constraint(b, pltpu.HBM) for b in bufs])
    return outs[0], outs[1], list(outs[2:2 + nb]), outs[-1]


def _split_wait(name, send, recv, bufs, after, plan):
    nb = len(bufs)

    def body(*refs):
        ins, send_s, recv_s = refs[:nb], refs[nb], refs[nb + 1]
        for i, (src, dst, dev, mine) in enumerate(plan(ins)):
            pltpu.make_async_remote_copy(src, dst, send_s.at[i], recv_s.at[i], device_id=dev,
                                         device_id_type=MESH).wait_send()
            pltpu.make_async_remote_copy(src, mine, send_s.at[i], recv_s.at[i], device_id=dev,
                                         device_id_type=MESH).wait_recv()

    outs = pl.pallas_call(
        body, name=name, out_shape=[pltpu.HBM(b.shape, b.dtype) for b in bufs],
        in_specs=[HBM_SPEC] * nb + [SEM_SPEC, SEM_SPEC, pl.BlockSpec(memory_space=pl.ANY)],
        out_specs=[HBM_SPEC] * nb, input_output_aliases={i: i for i in range(nb)},
        compiler_params=pltpu.CompilerParams(has_side_effects=EFFECT),
    )(*bufs, send, recv, after)
    return list(outs)


def _slot(land, s, rows, cols):
    if cols is None:
        return land.at[s, rows]
    return land.at[rows, pl.ds(pl.multiple_of(s * cols, 128), cols)]


def _plan_gather_ici(cols):
    nw = len(cols)

    def plan(refs):
        x, y, c = _pos()
        my_s = 2 * x + y
        out = []
        for w in range(nw):
            mine = _half_rows(refs[w], c)
            for px, py, ps in _chips():
                out.append((refs[w].at[mine], _slot(refs[nw + w], my_s, mine, cols[w]), (px, py, c),
                            _slot(refs[nw + w], ps, mine, cols[w])))
        return out
    return plan


def _plan_gather_fwd(cols, rows):
    def plan(refs):
        x, y, c = _pos()
        out = []
        for w in range(len(cols)):
            half = rows[w] // 2
            mine = pl.ds(pl.multiple_of(c * half, 8), half)
            other = pl.ds(pl.multiple_of((1 - c) * half, 8), half)
            for px, py, ps in _chips():
                got = _slot(refs[w], ps, mine, cols[w])
                out.append((got, got, (x, y, 1 - c), _slot(refs[w], ps, other, cols[w])))
        return out
    return plan


def _plan_swap(nw):
    def plan(refs):
        x, y, c = _pos()
        return [(refs[w].at[:, _half_rows(refs[w], 1 - c)], refs[nw + w], (x, y, 1 - c), refs[nw + w])
                for w in range(nw)]
    return plan


def _plan_scatter(nw):
    def plan(refs):
        x, y, c = _pos()
        my_s = 2 * x + y
        out = []
        for w in range(nw):
            for px, py, ps in _chips():
                out.append((refs[w].at[ps], refs[nw + w].at[my_s], (px, py, c), refs[nw + w].at[ps]))
        return out
    return plan


def _plan_join(nw):
    def plan(refs):
        x, y, c = _pos()
        out = []
        for w in range(nw):
            land = refs[nw + w]
            out.append((refs[w], land.at[_half_rows(land, c)], (x, y, 1 - c), land.at[_half_rows(land, 1 - c)]))
        return out
    return plan


def _hbm_empty(shape, dtype):
    return pltpu.with_memory_space_constraint(lax.empty(shape, dtype), pltpu.HBM)


def _put_slot(land, own, slot):
    return lax.dynamic_update_slice(land, own[None], (slot,) + (0,) * own.ndim)


def _pad_lanes(a, n):
    return jnp.pad(a, ((0, 0), (0, n - a.shape[1])))


def kernel(x, c, positions, w_ada, b_ada, norm1_w, w_in, conv_w, conv_b, dt_bias, a_log, d_skip, attn_sinks, ssm_norm_w, w_out, norm2_w, w_gate_up, w_down, final_norm_w, loss_target, m_w_ada, m_b_ada, m_norm1_w, m_w_in, m_conv_w, m_conv_b, m_dt_bias, m_a_log, m_d_skip, m_attn_sinks, m_ssm_norm_w, m_w_out, m_norm2_w, m_w_gate_up, m_w_down, m_final_norm_w, v_w_ada, v_b_ada, v_norm1_w, v_w_in, v_conv_w, v_conv_b, v_dt_bias, v_a_log, v_d_skip, v_attn_sinks, v_ssm_norm_w, v_w_out, v_norm2_w, v_w_gate_up, v_w_down, v_final_norm_w):
    T = x.shape[1]
    tm = min(256, T)
    xi, yi, ci = lax.axis_index("x"), lax.axis_index("y"), lax.axis_index("c")
    my_s = 2 * xi + yi
    xs = x[0]
    tgt = loss_target[0]

    payload = jnp.concatenate([c, conv_w[0].reshape(1, CONVK * 256)], axis=1)
    gat, mod4, tok = _mod_exchange(payload, w_ada[0], b_ada.reshape(4, 1, 1536))
    mod6 = mod4.reshape(6, D)
    c_all = gat[:, 0, 0:D]
    cw_dev = gat[:, 0, D:].reshape(4, 2, CONVK, 256)[:, 0]
    conv_full = cw_dev.transpose(1, 0, 2).reshape(CONVK, CONVC)

    w_in_b = (w_in[0] + tok[0, 0]).astype(BF16)
    s_i, r_i, bufs, tok = _split_start("wgather_in_ici_start", [w_in_b, _hbm_empty((4,) + w_in_b.shape, BF16)], 3,
                                       _plan_gather_ici([None]))
    inv_freq = (10000.0 ** (-jnp.arange(32, dtype=F32) / 32))
    inv_row = jnp.tile(inv_freq, 4).reshape(1, 128)
    cos, sin_s = _rope_tables(positions.reshape(T, 1), inv_row + tok[0:1, :], tm)
    bufs = _split_wait("wgather_in_ici_wait", s_i, r_i, bufs, cos, _plan_gather_ici([None]))
    s_j, r_j, bufs, tok = _split_start("wgather_in_fwd_start", bufs[1:], 3, _plan_gather_fwd([None], [D]))
    bufs = _split_wait("wgather_in_fwd_wait", s_j, r_j, bufs, tok, _plan_gather_fwd([None], [D]))
    g_in = _put_slot(bufs[0], w_in_b, my_s)
    w_pad = jnp.concatenate([g_in[0], g_in[1], g_in[2], g_in[3], jnp.zeros((D, IN_PAD - IN_PROJ), BF16)], axis=1)

    late = [(w_out[0] + tok[0, 0]).astype(BF16), w_gate_up[0].astype(BF16), w_down[0].astype(BF16)]
    lands = [_hbm_empty((4, D // 4, D), BF16), _hbm_empty((D, 2 * DFF), BF16), _hbm_empty((4, DFF // 4, D), BF16)]
    cols3, rows3 = [None, GU_SH, None], [D // 4, D, DFF // 4]
    s_a, r_a, bufs, tok = _split_start("wgather_ici_start", late + lands, 9, _plan_gather_ici(cols3))

    qkv, z, xbc, dtr, h1b = _in_proj_fwd(xs, cos, sin_s, mod6 + tok[0, 0], norm1_w, w_pad, min(512, T))
    sinks = attn_sinks
    attn, lse = _attn_fwd(qkv, sinks)
    bufs = _split_wait("wgather_ici_wait", s_a, r_a, bufs, attn, _plan_gather_ici(cols3))
    s_b, r_b, lands, tok = _split_start("wgather_fwd_start", bufs[3:], 9, _plan_gather_fwd(cols3, rows3))
    dtb = _pad_lanes(dt_bias, 128)
    alog = _pad_lanes(a_log, 128)
    dskx = jnp.repeat(d_skip, HD, axis=1)
    mats = _ssd_mats()
    ynorm, ypre, states, conv_pre = _ssd_fwd(xbc, z, dtr, conv_full, conv_b, dtb + tok[0, 0], alog, dskx, ssm_norm_w,
                                             mats)
    lands = _split_wait("wgather_fwd_wait", s_b, r_b, lands, ynorm, _plan_gather_fwd(cols3, rows3))
    w_out_f = _put_slot(lands[0], late[0], my_s).reshape(D, D)
    w_dn_f = _put_slot(lands[2], late[2], my_s).reshape(DFF, D)
    s_arr = my_s.reshape(1).astype(jnp.int32)

    fw2 = final_norm_w.reshape(1, D)
    sq, dmix, dx1, h2b, act, dfb, dgu, dob, sm_ffn = _mix_ffn(
        xs, attn, ynorm, tgt, mod6, norm2_w, fw2, w_out_f, lands[1], late[1], s_arr, w_dn_f, tm)

    tt = min(2048, T)
    c_arr = ci.reshape(1).astype(jnp.int32)
    tok0 = jnp.zeros((8, 128), F32)
    gw_dn4 = _tn_matmul(act, dfb, GU_SH, D, tt, "dw_down", tok0).reshape(4, DFF // 4, D)
    gw_gu4 = _tn_matmul(h2b, dgu, D, GU_SH, tt, "dw_gate_up", tok0)
    gw_out4 = jnp.concatenate(
        [_tn_matmul(attn, dob, AW, D, tt, "dw_out_a", tok0)[0],
         _tn_matmul(ynorm, dob, SW, D, tt, "dw_out_y", tok0)[0]], axis=0).reshape(4, D // 4, D)
    big1 = [gw_out4, gw_gu4, gw_dn4]
    rbs1 = [128, 128, 176]
    sib1 = [_hbm_empty((4, g.shape[1] // 2, g.shape[2]), F32) for g in big1]
    s_c, r_c, bufs, tok = _split_start("gswap_start", big1 + sib1, 3, _plan_swap(3))

    dzxd, d_cw, d_cb, d_sw, d_sk, d_dtb, d_av = _ssd_bwd(
        xbc, conv_pre, z, dtr, ypre, states, dmix, conv_full, dtb + tok[0, 0], alog, dskx, ssm_norm_w, mats)
    bufs = _split_wait("gswap_wait", s_c, r_c, bufs, dzxd, _plan_swap(3))
    sums1 = [_add_half(g, s, c_arr, rb, "grad_add_%d" % i)
             for i, (g, s, rb) in enumerate(zip(bufs[:3], bufs[3:], rbs1))]
    land1 = [_hbm_empty(p.shape, BF16) for p in sums1]
    s_d, r_d, bufs, tok = _split_start("gscatter_start", sums1 + land1, 9, _plan_scatter(3))
    dqkv, d_sinks = _attn_bwd(qkv, sinks + tok[0:1, 0:8], lse, dmix, cos, sin_s)
    bufs = _split_wait("gscatter_wait", s_d, r_d, bufs, dqkv, _plan_scatter(3))
    halves1 = [_sum4(p, l, s_arr, rb, "grad_sum_%d" % i)
               for i, (p, l, rb) in enumerate(zip(bufs[:3], bufs[3:], rbs1))]
    full1 = [_hbm_empty((2 * h.shape[0], h.shape[1]), F32) for h in halves1]
    s_e, r_e, bufs, tok = _split_start("gjoin_start", halves1 + full1, 3, _plan_join(3))
    gq = _tn_matmul(h1b, dqkv, D, 768, tt, "dw_in_qkv", tok)[0]
    gz = _tn_matmul(h1b, dzxd, D, 1664, tt, "dw_in_zxd", tok)[0]
    gw_in4 = jnp.stack([gq[:, :IN_SH], jnp.concatenate([gq[:, IN_SH:], gz[:, :2 * IN_SH - 768]], axis=1),
                        gz[:, 2 * IN_SH - 768:3 * IN_SH - 768], gz[:, 3 * IN_SH - 768:4 * IN_SH - 768]])
    joined1 = _split_wait("gjoin_wait", s_e, r_e, bufs, gw_in4, _plan_join(3))

    sib0 = _hbm_empty((4, D // 2, IN_SH), F32)
    s_f, r_f, bufs, tok = _split_start("gswap_in_start", [gw_in4, sib0], 1, _plan_swap(1))
    bufs = _split_wait("gswap_in_wait", s_f, r_f, bufs, tok, _plan_swap(1))
    sum0 = _add_half(bufs[0], bufs[1], c_arr, 128, "grad_add_in")
    s_g, r_g, bufs, tok = _split_start("gscatter_in_start", [sum0, _hbm_empty(sum0.shape, BF16)], 3, _plan_scatter(1))
    grad_x, sm_in = _in_proj_bwd(xs, dx1, dqkv, dzxd, mod6 + tok[0, 0], norm1_w, w_pad, min(512, T))
    bufs = _split_wait("gscatter_in_wait", s_g, r_g, bufs, grad_x, _plan_scatter(1))
    half0 = _sum4(bufs[0], bufs[1], s_arr, 128, "grad_sum_in")
    s_h, r_h, bufs, tok = _split_start("gjoin_in_start", [half0, _hbm_empty((D, IN_SH), F32)], 1, _plan_join(1))
    joined0 = _split_wait("gjoin_in_wait", s_h, r_h, bufs, tok, _plan_join(1))

    a_neg = -jnp.exp(alog)
    pieces = [sm_in[1:2], sm_in[2:3], sm_ffn[5:6], sm_ffn[2:3], sm_ffn[3:4], sm_ffn[4:5],
              sm_in[0:1], sm_ffn[1:2], sm_ffn[0:1], d_cb, d_cw.reshape(1, CONVK * CONVC),
              _pad_lanes(d_sw, SW), d_dtb, d_av * a_neg, d_sk, d_sinks,
              _pad_lanes((0.5 / D * jnp.sum(sq)).reshape(1, 1), 128)]
    vec = jnp.concatenate(pieces, axis=1)
    tot, allv = _small_reduce(vec)
    o = 0
    offs = []
    for p in pieces:
        offs.append(o)
        o += p.shape[1]
    seg = lambda i, n: tot[:, offs[i]:offs[i] + n]
    g_b_ada = tot[:, 0:6 * D]
    g_norm1, g_norm2, g_final, g_conv_b = seg(6, D), seg(7, D), seg(8, D), seg(9, D)
    g_conv_w = lax.dynamic_slice_in_dim(seg(10, CONVK * CONVC).reshape(CONVK, CONVC), my_s * 256, 256, axis=1)
    g_ssm_w, g_dtb, g_alog, g_dsk, g_sink = seg(11, SW), seg(12, 8), seg(13, 8), seg(14, 8), seg(15, 8)
    loss = tot[0, offs[16]]

    small_names = ["b_ada", "norm1_w", "conv_w", "conv_b", "dt_bias", "a_log", "d_skip", "attn_sinks", "ssm_norm_w",
                   "norm2_w", "final_norm_w"]
    small_g = [g_b_ada, g_norm1, g_conv_w, g_conv_b, g_dtb, g_alog, g_dsk, g_sink, g_ssm_w, g_norm2, g_final]
    as2d = lambda a: a.reshape(-1, a.shape[-1])
    small_w = [as2d(a) for a in (b_ada, norm1_w, conv_w, conv_b, dt_bias, a_log, d_skip, attn_sinks, ssm_norm_w,
                                 norm2_w, final_norm_w)]
    small_m = [as2d(a) for a in (m_b_ada, m_norm1_w, m_conv_w, m_conv_b, m_dt_bias, m_a_log, m_d_skip, m_attn_sinks,
                                 m_ssm_norm_w, m_norm2_w, m_final_norm_w)]
    small_v = [as2d(a) for a in (v_b_ada, v_norm1_w, v_conv_w, v_conv_b, v_dt_bias, v_a_log, v_d_skip, v_attn_sinks,
                                 v_ssm_norm_w, v_norm2_w, v_final_norm_w)]
    small_g, sd, smn, svn = _adam_small(small_g, small_w, small_m, small_v)

    sc_all = c_all * jax.nn.sigmoid(c_all)
    dmod_all = allv[:, 0, 0:6 * D]
    dmod_s = lax.dynamic_slice_in_dim(dmod_all, my_s * 1536, 1536, axis=1)
    g_ada, d_ada, m_ada, v_ada = _adam_w_ada(sc_all, dmod_s, w_ada[0], m_w_ada[0], v_w_ada[0], 256)
    g_in_s, d_in, m_in, v_in = _adam_2d(w_in[0], joined0[0], joined0[1], m_w_in[0], v_w_in[0], c_arr, 256, "adam_w_in")
    g_out_s, d_out, m_out, v_out = _adam_2d(w_out[0], joined1[0], joined1[3], m_w_out[0], v_w_out[0], c_arr, 128,
                                            "adam_w_out")
    g_gu_s, d_gu, m_gu, v_gu = _adam_2d(w_gate_up[0], joined1[1], joined1[4], m_w_gate_up[0], v_w_gate_up[0], c_arr,
                                        256, "adam_w_gate_up")
    g_dn_s, d_dn, m_dn, v_dn = _adam_2d(w_down[0], joined1[2], joined1[5], m_w_down[0], v_w_down[0], c_arr, 352,
                                        "adam_w_down")

    order = ["w_ada", "b_ada", "norm1_w", "w_in", "conv_w", "conv_b", "dt_bias", "a_log", "d_skip", "attn_sinks",
             "ssm_norm_w", "w_out", "norm2_w", "w_gate_up", "w_down", "final_norm_w"]
    shapes = dict(w_ada=w_ada.shape, b_ada=b_ada.shape, norm1_w=norm1_w.shape, w_in=w_in.shape, conv_w=conv_w.shape,
                  conv_b=conv_b.shape, dt_bias=dt_bias.shape, a_log=a_log.shape, d_skip=d_skip.shape,
                  attn_sinks=attn_sinks.shape, ssm_norm_w=ssm_norm_w.shape, w_out=w_out.shape, norm2_w=norm2_w.shape,
                  w_gate_up=w_gate_up.shape, w_down=w_down.shape, final_norm_w=final_norm_w.shape)
    grads = dict(w_ada=g_ada, w_in=g_in_s, w_out=g_out_s, w_gate_up=g_gu_s, w_down=g_dn_s)
    deltas = dict(w_ada=d_ada, w_in=d_in, w_out=d_out, w_gate_up=d_gu, w_down=d_dn)
    new_m = dict(w_ada=m_ada, w_in=m_in, w_out=m_out, w_gate_up=m_gu, w_down=m_dn)
    new_v = dict(w_ada=v_ada, w_in=v_in, w_out=v_out, w_gate_up=v_gu, w_down=v_dn)
    for i, nme in enumerate(small_names):
        grads[nme], deltas[nme], new_m[nme], new_v[nme] = small_g[i], sd[i], smn[i], svn[i]
    outs = [loss, grad_x[None]]
    for table in (grads, deltas, new_m, new_v):
        outs += [table[nme].reshape(shapes[nme]) for nme in order]
    return tuple(outs)
```

```python
import functools
import math

import jax
import jax.numpy as jnp
from jax import lax
from jax.experimental import pallas as pl
from jax.experimental.pallas import tpu as pltpu

F32 = jnp.float32
BF16 = jnp.bfloat16
HI = lax.Precision.HIGHEST
MESH = pl.DeviceIdType.MESH

D = 1024
HD = 64
NQ = 8
AW = 512
KVW = 128
SW = 512
NST = 128
CONVK = 4
CONVC = 1024
BLK = 128
IN_PROJ = 2312
IN_PAD = 2432
IN_SH = IN_PROJ // 4
IN_SH_PAD = 608
DFF = 2816
GU_SH = 1408
FF_SPLITS = ((0, 1536), (1536, 2816))
EPS = 1e-6
NEG = -1e30
LR, B1, B2, AEPS, WD, STEP = 0.001, 0.9, 0.999, 1e-08, 0.01, 10
VMEM_LIMIT = 58 * 1024 * 1024


def _cp(*sem):
    return pltpu.CompilerParams(dimension_semantics=sem or None, vmem_limit_bytes=VMEM_LIMIT)


def _dot(a, b):
    return jnp.dot(a, b, preferred_element_type=F32)


def _dot_nt(a, b):
    return lax.dot_general(a, b, (((1,), (1,)), ((), ())), preferred_element_type=F32)


def _dot_tn(a, b):
    return lax.dot_general(a, b, (((0,), (0,)), ((), ())), preferred_element_type=F32)


def _dot_hi(a, b):
    return jnp.dot(a, b, precision=HI, preferred_element_type=F32)


def _sigmoid(x):
    return 1.0 / (1.0 + jnp.exp(-x))


def _iota(shape, dim):
    return lax.broadcasted_iota(jnp.int32, shape, dim)


def _load_resident(hbm_ref, vmem_ref, sem):
    @pl.when(pl.program_id(0) == 0)
    def _():
        cp = pltpu.make_async_copy(hbm_ref, vmem_ref, sem)
        cp.start()
        cp.wait()


def _swap32(t):
    lane = _iota(t.shape, 1)
    return jnp.where((lane & 63) < 32, pltpu.roll(t, 96, 1), pltpu.roll(t, 32, 1))


def _rope_fwd(t, cos, sin_s):
    return t * cos + _swap32(t) * sin_s


def _rope_bwd(t, cos, sin_s):
    return t * cos - _swap32(t) * sin_s


def _rope_tables(pos_row, inv_freq_col, tm):
    T = pos_row.shape[1]
    lane, row = jnp.arange(128)[None, :], jnp.arange(96)[:, None]
    pick = (lane % 32) == (row % 32)
    sel_cos = pick.astype(BF16)
    sel_sin = jnp.where(pick, jnp.where(lane % 64 < 32, -1.0, 1.0), 0.0).astype(BF16)

    def body(p_ref, f_ref, sc_ref, ss_ref, cos_ref, sin_ref):
        ang = f_ref[...] * p_ref[...].astype(F32)
        cos_ref[...] = _dot_tn(_pieces(jnp.cos(ang), 3, 0), sc_ref[...])
        sin_ref[...] = _dot_tn(_pieces(jnp.sin(ang), 3, 0), ss_ref[...])

    full = lambda a: pl.BlockSpec(a.shape, lambda i: (0,) * a.ndim)
    return pl.pallas_call(
        body, name="rope_tables", grid=(T // tm,),
        in_specs=[pl.BlockSpec((1, tm), lambda i: (0, i)), full(inv_freq_col), full(sel_cos), full(sel_sin)],
        out_specs=[pl.BlockSpec((tm, 128), lambda i: (i, 0))] * 2,
        out_shape=[jax.ShapeDtypeStruct((T, 128), F32)] * 2,
        compiler_params=_cp("parallel"),
    )(pos_row, inv_freq_col, sel_cos, sel_sin)


def _in_proj_fwd(x, cos, sin_s, mod6, norm1_w, w_pad, tm):
    T = x.shape[0]

    def body(x_ref, cos_ref, sin_ref, mod_ref, nw_ref, w_hbm, qkv_ref, z_ref, xbc_ref, dt_ref, h_ref, w_vmem, sem):
        _load_resident(w_hbm, w_vmem, sem)
        xv = x_ref[...]
        r = lax.rsqrt(jnp.mean(xv * xv, axis=-1, keepdims=True) + EPS)
        h = (xv * r * nw_ref[...]) * (1.0 + mod_ref[1:2, :]) + mod_ref[0:1, :]
        hb = h.astype(BF16)
        h_ref[...] = hb
        proj = _dot(hb, w_vmem[...])
        cs, sn = cos_ref[...], sin_ref[...]
        for j in range(5):
            qkv_ref[:, 128 * j:128 * (j + 1)] = _rope_fwd(proj[:, 128 * j:128 * (j + 1)], cs, sn).astype(BF16)
        qkv_ref[:, 640:768] = proj[:, 640:768].astype(BF16)
        z_ref[...] = proj[:, 768:1280]
        xbc_ref[...] = proj[:, 1280:2304]
        dt_ref[...] = proj[:, 2304:2432]

    row = lambda w: pl.BlockSpec((tm, w), lambda i: (i, 0))
    full = lambda a: pl.BlockSpec(a.shape, lambda i: (0,) * a.ndim)
    return pl.pallas_call(
        body, name="in_proj_fwd", grid=(T // tm,),
        in_specs=[row(D), row(128), row(128), full(mod6), full(norm1_w), pl.BlockSpec(memory_space=pl.ANY)],
        out_specs=[row(768), row(512), row(1024), row(128), row(D)],
        out_shape=[jax.ShapeDtypeStruct((T, 768), BF16), jax.ShapeDtypeStruct((T, 512), F32),
                   jax.ShapeDtypeStruct((T, 1024), F32), jax.ShapeDtypeStruct((T, 128), F32),
                   jax.ShapeDtypeStruct((T, D), BF16)],
        scratch_shapes=[pltpu.VMEM((D, IN_PAD), BF16), pltpu.SemaphoreType.DMA],
        compiler_params=_cp("arbitrary"),
    )(x, cos, sin_s, mod6, norm1_w, w_pad)


def _head_variants(pair, j):
    lane = _iota(pair.shape, 1)
    lo = lane < 64
    kv = j // 2
    ev = jnp.where(lo, pair, 0.0)
    od = jnp.where(lo, 0.0, pair)
    if kv == 0:
        od = pltpu.roll(od, 64, 1)
    else:
        ev = pltpu.roll(ev, 64, 1)
    return ev.astype(BF16), od.astype(BF16)


def _kv_variants(vcat):
    lane = _iota(vcat.shape, 1)
    lo = lane < 64
    v0 = jnp.where(lo, vcat, 0.0)
    v1 = jnp.where(lo, 0.0, vcat)
    out = {
        (0, 0): v0, (0, 1): pltpu.roll(v0, 64, 1),
        (1, 0): pltpu.roll(v1, 64, 1), (1, 1): v1,
    }
    return {k: v.astype(BF16) for k, v in out.items()}


def _fold_masks(n):
    upper = _iota((BLK, BLK), 1) > _iota((BLK, BLK), 0)
    return upper, upper & (n == 0)


def _attn_fwd(qkv, sinks):
    T = qkv.shape[0]
    nb = T // BLK

    def body(sink_ref, q_ref, kc_ref, kp_ref, vc_ref, vp_ref, o_ref, lse_ref):
        n = pl.program_id(0)
        vpv = _kv_variants(vp_ref[...].astype(F32))
        vcv = _kv_variants(vc_ref[...].astype(F32))
        q_all = jnp.concatenate(
            [v for j in range(4) for v in _head_variants(q_ref[:, 128 * j:128 * (j + 1)].astype(F32), j)], axis=0)
        s_prev = _dot_nt(q_all, kp_ref[...])
        s_cur = _dot_nt(q_all, kc_ref[...])
        upper, dead = _fold_masks(n)
        lane = _iota((BLK, 128), 1)
        lse_acc = jnp.zeros((BLK, 128), F32)
        for jj in range(4):
            acc = jnp.zeros((BLK, 128), F32)
            for par in range(2):
                h = 2 * jj + par
                rows = slice(h * BLK, (h + 1) * BLK)
                sink = sink_ref[0, h]
                s = jnp.where(dead, NEG, jnp.where(upper, s_prev[rows], s_cur[rows]) * 0.125)
                m = jnp.maximum(jnp.max(s, axis=1, keepdims=True), sink)
                p = jnp.exp(s - m)
                den = jnp.sum(p, axis=1, keepdims=True) + jnp.exp(sink - m)
                pn = p * (1.0 / den)
                acc = (acc + _dot(jnp.where(upper, pn, 0.0).astype(BF16), vpv[(jj // 2, par)])
                       + _dot(jnp.where(upper, 0.0, pn).astype(BF16), vcv[(jj // 2, par)]))
                lse_acc = jnp.where(lane == h, m + jnp.log(den), lse_acc)
            o_ref[:, 128 * jj:128 * (jj + 1)] = acc.astype(BF16)
        lse_ref[...] = lse_acc

    prev = lambda n: jnp.maximum(n - 1, 0)
    return pl.pallas_call(
        body, name="attn_fwd", grid=(nb,),
        in_specs=[pl.BlockSpec(memory_space=pltpu.SMEM),
                  pl.BlockSpec((BLK, 512), lambda n: (n, 0)),
                  pl.BlockSpec((BLK, 128), lambda n: (n, 4)),
                  pl.BlockSpec((BLK, 128), lambda n: (prev(n), 4)),
                  pl.BlockSpec((BLK, 128), lambda n: (n, 5)),
                  pl.BlockSpec((BLK, 128), lambda n: (prev(n), 5))],
        out_specs=[pl.BlockSpec((BLK, 512), lambda n: (n, 0)), pl.BlockSpec((BLK, 128), lambda n: (n, 0))],
        out_shape=[jax.ShapeDtypeStruct((T, 512), BF16), jax.ShapeDtypeStruct((T, 128), F32)],
        compiler_params=_cp("parallel"),
    )(sinks, qkv, qkv, qkv, qkv, qkv)


def _attn_bwd(qkv, sinks, lse, dmix, cos, sin_s):
    T = qkv.shape[0]
    nb = T // BLK

    def body(sink_ref, q_ref, kc_ref, kp_ref, vc_ref, vp_ref, lse_ref, do_ref, cq_ref, sq_ref, ck_ref, sk_ref,
             out_ref, ds_ref, dq_car, dk_car, dv_car):
        n = pl.program_id(0)
        lane = _iota((BLK, 128), 1)

        @pl.when(n == 0)
        def _():
            ds_ref[...] = jnp.zeros_like(ds_ref)
            dq_car[...] = jnp.zeros_like(dq_car)
            dk_car[...] = jnp.zeros_like(dk_car)
            dv_car[...] = jnp.zeros_like(dv_car)

        @pl.when(n < nb)
        def _():
            kp, kc, vp, vc = kp_ref[...], kc_ref[...], vp_ref[...], vc_ref[...]
            kpv = _kv_variants(kp.astype(F32))
            kcv = _kv_variants(kc.astype(F32))
            lse_v = lse_ref[...]
            q_all = jnp.concatenate(
                [v for j in range(4) for v in _head_variants(q_ref[:, 128 * j:128 * (j + 1)].astype(F32), j)], axis=0)
            do_all = jnp.concatenate(
                [v for j in range(4) for v in _head_variants(do_ref[:, 128 * j:128 * (j + 1)], j)], axis=0)
            s_prev, s_cur = _dot_nt(q_all, kp), _dot_nt(q_all, kc)
            dp_prev, dp_cur = _dot_nt(do_all, vp), _dot_nt(do_all, vc)
            upper, dead = _fold_masks(n)
            out_ref[:, 0:512] = dq_car[...]
            dsk = jnp.zeros((1, 128), F32)
            ds_u, ds_l, p_u, p_l = [], [], [], []
            for jj in range(4):
                dq_acc = jnp.zeros((BLK, 128), F32)
                for par in range(2):
                    h = 2 * jj + par
                    rows = slice(h * BLK, (h + 1) * BLK)
                    lse_h = jnp.sum(jnp.where(lane == h, lse_v, 0.0), axis=1, keepdims=True)
                    s = jnp.where(dead, NEG, jnp.where(upper, s_prev[rows], s_cur[rows]) * 0.125)
                    p = jnp.exp(s - lse_h)
                    dp = jnp.where(upper, dp_prev[rows], dp_cur[rows])
                    delta = jnp.sum(p * dp, axis=1, keepdims=True)
                    ds = p * (dp - delta) * 0.125
                    dsu, dsl = jnp.where(upper, ds, 0.0).astype(BF16), jnp.where(upper, 0.0, ds).astype(BF16)
                    dq_acc = dq_acc + _dot(dsu, kpv[(jj // 2, par)]) + _dot(dsl, kcv[(jj // 2, par)])
                    ds_u.append(dsu)
                    ds_l.append(dsl)
                    p_u.append(jnp.where(upper, p, 0.0).astype(BF16))
                    p_l.append(jnp.where(upper, 0.0, p).astype(BF16))
                    dsk = dsk + jnp.where(lane[0:1] == h, -jnp.sum(jnp.exp(sink_ref[0, h] - lse_h) * delta), 0.0)
                dq_car[:, 128 * jj:128 * (jj + 1)] = _rope_bwd(dq_acc, cq_ref[...], sq_ref[...]).astype(BF16)
            stack = lambda parts: jnp.concatenate(parts, axis=0)
            dk_prev, dk_cur = _dot_tn(stack(ds_u), q_all), _dot_tn(stack(ds_l), q_all)
            dv_prev, dv_cur = _dot_tn(stack(p_u), do_all), _dot_tn(stack(p_l), do_all)
            ds_ref[...] += dsk
            out_ref[:, 512:640] = _rope_bwd(dk_car[...] + dk_prev, ck_ref[...], sk_ref[...]).astype(BF16)
            out_ref[:, 640:768] = (dv_car[...] + dv_prev).astype(BF16)
            dk_car[...] = dk_cur
            dv_car[...] = dv_cur

        @pl.when(n == nb)
        def _():
            out_ref[:, 0:512] = dq_car[...]
            out_ref[:, 512:640] = _rope_bwd(dk_car[...], ck_ref[...], sk_ref[...]).astype(BF16)
            out_ref[:, 640:768] = dv_car[...].astype(BF16)

    cur = lambda n: jnp.minimum(n, nb - 1)
    prev = lambda n: jnp.maximum(cur(n) - 1, 0)
    outb = lambda n: jnp.maximum(n - 1, 0)
    return pl.pallas_call(
        body, name="attn_bwd", grid=(nb + 1,),
        in_specs=[pl.BlockSpec(memory_space=pltpu.SMEM),
                  pl.BlockSpec((BLK, 512), lambda n: (cur(n), 0)),
                  pl.BlockSpec((BLK, 128), lambda n: (cur(n), 4)),
                  pl.BlockSpec((BLK, 128), lambda n: (prev(n), 4)),
                  pl.BlockSpec((BLK, 128), lambda n: (cur(n), 5)),
                  pl.BlockSpec((BLK, 128), lambda n: (prev(n), 5)),
                  pl.BlockSpec((BLK, 128), lambda n: (cur(n), 0)),
                  pl.BlockSpec((BLK, 512), lambda n: (cur(n), 0)),
                  pl.BlockSpec((BLK, 128), lambda n: (cur(n), 0)),
                  pl.BlockSpec((BLK, 128), lambda n: (cur(n), 0)),
                  pl.BlockSpec((BLK, 128), lambda n: (outb(n), 0)),
                  pl.BlockSpec((BLK, 128), lambda n: (outb(n), 0))],
        out_specs=[pl.BlockSpec((BLK, 768), lambda n: (outb(n), 0)), pl.BlockSpec((1, 128), lambda n: (0, 0))],
        out_shape=[jax.ShapeDtypeStruct((T, 768), BF16), jax.ShapeDtypeStruct((1, 128), F32)],
        scratch_shapes=[pltpu.VMEM((BLK, 512), BF16), pltpu.VMEM((BLK, 128), F32), pltpu.VMEM((BLK, 128), F32)],
        compiler_params=_cp("arbitrary"),
    )(sinks, qkv, qkv, qkv, qkv, qkv, lse, dmix, cos, sin_s, cos, sin_s)


def _ssd_mats():
    e = jnp.arange(SW)[None, :] // HD == jnp.arange(128)[:, None]
    tri = jnp.arange(BLK)[None, :] <= jnp.arange(BLK)[:, None]
    return (jnp.tile(e, (3, 1)).astype(BF16), jnp.tile(e.T, (2, 1)).astype(BF16),
            jnp.tile(tri, (1, 3)).astype(BF16), jnp.tile(tri.T, (1, 3)).astype(BF16))


def _pieces(x, n, axis):
    out, r = [], x
    for i in range(n):
        p = r.astype(BF16)
        out.append(p)
        if i + 1 < n:
            r = r - p.astype(F32)
    return jnp.concatenate(out, axis=axis)


def _expand(x, e3):
    return _dot(_pieces(x, 3, 1), e3)


def _head_sums(x, et2):
    return _dot(_pieces(x, 2, 1), et2)


def _run_sum(tri3, x):
    return _dot(tri3, _pieces(x, 3, 0))


def _shift_down(u, tail, j):
    rolled = pltpu.roll(u, j, 0)
    first = jnp.where(_iota(tail.shape, 0) < j, pltpu.roll(tail, j, 0), rolled[0:8])
    return jnp.concatenate([first, rolled[8:]], axis=0)


def _shift_up(d, head, j):
    rolled = pltpu.roll(d, BLK - j, 0)
    last = jnp.where(_iota(head.shape, 0) >= 8 - j, pltpu.roll(head, 8 - j, 0), rolled[BLK - 8:])
    return jnp.concatenate([rolled[:BLK - 8], last], axis=0)


def _ssd_parts(dtr, dtb, alog, e3, tril3):
    xx = dtr + dtb
    dt = jnp.maximum(xx, 0.0) + jnp.log(1.0 + jnp.exp(-jnp.abs(xx)))
    a_neg = -jnp.exp(alog)
    tril = _iota((BLK, BLK), 1) <= _iota((BLK, BLK), 0)
    cs = _run_sum(tril3, dt * a_neg)
    csx = _expand(cs, e3)
    last = csx[BLK - 1:BLK, :]
    return dict(xx=xx, dt=dt, a_neg=a_neg, tril=tril, cs=cs, cs_t=cs.T,
                ecsx=jnp.exp(csx), dtex=jnp.exp(last - csx), cdx=jnp.exp(last), dtx=_expand(dt, e3))


def _decay(parts, h):
    seg = parts["cs"][:, h:h + 1] - parts["cs_t"][h:h + 1, :]
    return jnp.exp(jnp.where(parts["tril"], seg, NEG))


def _group_cols(a, g):
    return a[:, 256 * g:256 * (g + 1)]


def _ssd_fwd(xbc, z, dtr, conv_w, conv_b, dtb, alog, dskx, ssm_w, mats):
    T = xbc.shape[0]
    nc = T // BLK

    def body(u_ref, tail_ref, z_ref, dtr_ref, cw_ref, cb_ref, dtb_ref, al_ref, dk_ref, sw_ref, e3_ref, tril3_ref,
             yn_ref, yp_ref, st_ref, co_ref, s_scr):
        n = pl.program_id(0)

        @pl.when(n == 0)
        def _():
            s_scr[...] = jnp.zeros_like(s_scr)

        u = u_ref[...]
        tail = jnp.where(n > 0, tail_ref[...], 0.0)
        co = cb_ref[...] + cw_ref[3:4, :] * u
        for j in range(1, CONVK):
            co = co + cw_ref[3 - j:4 - j, :] * _shift_down(u, tail, j)
        co_ref[...] = co
        xc = co * _sigmoid(co)
        pt = _ssd_parts(dtr_ref[...], dtb_ref[...], al_ref[...], e3_ref[...], tril3_ref[...])
        xs = xc[:, :SW]
        bm = [xc[:, 512:640].astype(BF16), xc[:, 640:768].astype(BF16)]
        cm = [xc[:, 768:896].astype(BF16), xc[:, 896:1024].astype(BF16)]
        s_in = s_scr[...]
        st_ref[0] = s_in
        xdt = xs * pt["dtx"]
        xde = (xdt * pt["dtex"]).astype(BF16)
        lane = _iota((BLK, 128), 1)
        lo = lane < 64
        ys, s_new = [], []
        for g in range(2):
            cb = _dot_nt(cm[g], bm[g])
            yoff = _dot(cm[g], _group_cols(s_in, g).astype(BF16))
            s_new.append(_dot_tn(bm[g], _group_cols(xde, g)))
            for jj in range(2):
                j = 2 * g + jj
                chunk = xdt[:, 128 * j:128 * (j + 1)]
                g_ev = (cb * _decay(pt, 2 * j)).astype(BF16)
                g_od = (cb * _decay(pt, 2 * j + 1)).astype(BF16)
                yd = _dot(g_ev, jnp.where(lo, chunk, 0.0).astype(BF16)) + _dot(g_od, jnp.where(lo, 0.0, chunk).astype(BF16))
                ys.append(yd + yoff[:, 128 * jj:128 * (jj + 1)] * pt["ecsx"][:, 128 * j:128 * (j + 1)])
        y = jnp.concatenate(ys, axis=1) + xs * dk_ref[...]
        s_scr[...] = s_in * pt["cdx"] + jnp.concatenate(s_new, axis=1)
        yp_ref[...] = y
        zv = z_ref[...]
        yz = y * (zv * _sigmoid(zv))
        outs = []
        for g in range(2):
            yg = _group_cols(yz, g)
            outs.append(yg * lax.rsqrt(jnp.mean(yg * yg, axis=-1, keepdims=True) + EPS))
        yn_ref[...] = (jnp.concatenate(outs, axis=1) * sw_ref[...]).astype(BF16)

    e3, _, tril3, _ = mats
    tail8 = lambda n: jnp.maximum(n * (BLK // 8) - 1, 0)
    full = lambda a: pl.BlockSpec(a.shape, lambda n: (0,) * a.ndim)
    return pl.pallas_call(
        body, name="ssd_fwd", grid=(nc,),
        in_specs=[pl.BlockSpec((BLK, CONVC), lambda n: (n, 0)), pl.BlockSpec((8, CONVC), lambda n: (tail8(n), 0)),
                  pl.BlockSpec((BLK, SW), lambda n: (n, 0)), pl.BlockSpec((BLK, 128), lambda n: (n, 0)),
                  full(conv_w), full(conv_b), full(dtb), full(alog), full(dskx), full(ssm_w), full(e3), full(tril3)],
        out_specs=[pl.BlockSpec((BLK, SW), lambda n: (n, 0)), pl.BlockSpec((BLK, SW), lambda n: (n, 0)),
                   pl.BlockSpec((1, NST, SW), lambda n: (n, 0, 0)), pl.BlockSpec((BLK, CONVC), lambda n: (n, 0))],
        out_shape=[jax.ShapeDtypeStruct((T, SW), BF16), jax.ShapeDtypeStruct((T, SW), F32),
                   jax.ShapeDtypeStruct((nc, NST, SW), F32), jax.ShapeDtypeStruct((T, CONVC), F32)],
        scratch_shapes=[pltpu.VMEM((NST, SW), F32)],
        compiler_params=_cp("arbitrary"),
    )(xbc, xbc, z, dtr, conv_w, conv_b, dtb, alog, dskx, ssm_w, e3, tril3)


def _ssd_bwd(xbc, co_all, z, dtr, ypre, states, dmix, conv_w, dtb, alog, dskx, ssm_w, mats):
    T = xbc.shape[0]
    nc = T // BLK

    def body(u_ref, co_ref, z_ref, dtr_ref, yp_ref, st_ref, dyn_ref, cw_ref, dtb_ref, al_ref, dk_ref, sw_ref,
             e3_ref, et2_ref, tril3_ref, triu3_ref,
             out_ref, dcw_ref, dcb_ref, dsw_ref, dsk_ref, ddtb_ref, dav_ref, ds_scr, dco_scr, dskx_scr):
        i = pl.program_id(0)

        @pl.when(i == 0)
        def _():
            for r in (dcw_ref, dcb_ref, dsw_ref, dsk_ref, ddtb_ref, dav_ref, ds_scr, dco_scr, dskx_scr):
                r[...] = jnp.zeros_like(r)

        co = co_ref[...]
        sg = _sigmoid(co)
        xc = co * sg
        pt = _ssd_parts(dtr_ref[...], dtb_ref[...], al_ref[...], e3_ref[...], tril3_ref[...])
        dtx, ecsx, dtex, cdx = pt["dtx"], pt["ecsx"], pt["dtex"], pt["cdx"]
        xs = xc[:, :SW]
        bm = [xc[:, 512:640].astype(BF16), xc[:, 640:768].astype(BF16)]
        cm = [xc[:, 768:896].astype(BF16), xc[:, 896:1024].astype(BF16)]
        s_in = st_ref[0]
        ds_out = ds_scr[...]
        e_t = et2_ref[...]

        zv = z_ref[...]
        sz = _sigmoid(zv)
        silu_z = zv * sz
        ypre = yp_ref[...]
        yz = ypre * silu_z
        dyn = dyn_ref[...]
        sw = sw_ref[...]
        dyz, yns = [], []
        for g in range(2):
            yg = _group_cols(yz, g)
            r = lax.rsqrt(jnp.mean(yg * yg, axis=-1, keepdims=True) + EPS)
            yn = yg * r
            dg = _group_cols(dyn, g) * _group_cols(sw, g)
            dyz.append(r * (dg - yn * jnp.mean(dg * yn, axis=-1, keepdims=True)))
            yns.append(yn)
        dyz = jnp.concatenate(dyz, axis=1)
        dsw_ref[...] += jnp.sum(dyn * jnp.concatenate(yns, axis=1), axis=0, keepdims=True)
        dy = dyz * silu_z
        dz = dyz * ypre * (sz * (1.0 + zv * (1.0 - sz)))

        xdt = xs * dtx
        xdt_b = xdt.astype(BF16)
        edy = (ecsx * dy).astype(BF16)
        xde = (xdt * dtex).astype(BF16)
        lane = _iota((BLK, 128), 1)
        lo = lane < 64
        row8 = _iota((8, 128), 0)
        dcs = jnp.zeros((BLK, 128), F32)
        col_rows = jnp.zeros((8, 128), F32)
        dxdt, bds, yoff, dbs, dcs_g, ds_new = [], [], [], [], [], []
        for g in range(2):
            s_g = _group_cols(s_in, g).astype(BF16)
            dso_g = _group_cols(ds_out, g).astype(BF16)
            cb = _dot_nt(cm[g], bm[g])
            bds.append(_dot(bm[g], dso_g))
            yoff.append(_dot(cm[g], s_g))
            dcb_g = jnp.zeros((BLK, BLK), F32)
            for jj in range(2):
                j = 2 * g + jj
                dy_c = dy[:, 128 * j:128 * (j + 1)]
                xdt_c = xdt_b[:, 128 * j:128 * (j + 1)]
                acc = jnp.zeros((BLK, 128), F32)
                for par in range(2):
                    h = 2 * j + par
                    lm = _decay(pt, h)
                    gm = cb * lm
                    dy_m = (jnp.where(lo, dy_c, 0.0) if par == 0 else jnp.where(lo, 0.0, dy_c)).astype(BF16)
                    dg_h = _dot_nt(dy_m, xdt_c)
                    w_h = dg_h * gm
                    dcs = dcs + jnp.where(lane == h, jnp.sum(w_h, axis=1, keepdims=True), 0.0)
                    col_rows = col_rows + jnp.where(row8 == h, jnp.sum(w_h, axis=0, keepdims=True), 0.0)
                    dcb_g = dcb_g + dg_h * lm
                    acc = acc + _dot_tn(gm.astype(BF16), dy_m)
                dxdt.append(acc)
            dcb_b = dcb_g.astype(BF16)
            dcs_g.append(_dot(dcb_b, bm[g]) + _dot_nt(_group_cols(edy, g), s_g))
            dbs.append(_dot_tn(dcb_b, cm[g]) + _dot_nt(_group_cols(xde, g), dso_g))
            ds_new.append(_dot_tn(cm[g], _group_cols(edy, g)))
        bds = jnp.concatenate(bds, axis=1)
        yoff = jnp.concatenate(yoff, axis=1) * ecsx
        dxdt = jnp.concatenate(dxdt, axis=1) + dtex * bds
        ds_scr[...] = cdx * ds_out + jnp.concatenate(ds_new, axis=1)

        t_m = _head_sums(dtex * xdt * bds, e_t)
        colsum_t = jnp.concatenate([col_rows, jnp.zeros((BLK - 8, 128), F32)], axis=0).T
        cd = jnp.exp(pt["cs"][BLK - 1:BLK, :])
        sds = jnp.sum(s_in * ds_out, axis=0, keepdims=True)
        last_row = jnp.sum(t_m, axis=0, keepdims=True) + cd * _head_sums(jnp.broadcast_to(sds, (8, SW)), e_t)[0:1]
        dcs = dcs - colsum_t + _head_sums(dy * yoff, e_t) - t_m
        dcs = dcs + jnp.where(_iota((BLK, 128), 0) == BLK - 1, last_row, 0.0)
        da = _run_sum(triu3_ref[...], dcs)
        dt = pt["dt"]
        ddt = da * pt["a_neg"] + _head_sums(dxdt * xs, e_t)
        dav_ref[...] += jnp.sum(da * dt, axis=0, keepdims=True)
        ddtr = ddt * _sigmoid(pt["xx"])
        ddtb_ref[...] += jnp.sum(ddtr, axis=0, keepdims=True)
        dxs = dxdt * dtx + dy * dk_ref[...]
        dskx_scr[...] += jnp.sum(dy * xs, axis=0, keepdims=True)
        dxc = jnp.concatenate([dxs, dbs[0], dbs[1], dcs_g[0], dcs_g[1]], axis=1)
        dco = dxc * (sg * (1.0 + co * (1.0 - sg)))

        dcb_ref[...] += jnp.sum(dco, axis=0, keepdims=True)
        u = u_ref[...]
        head = dco_scr[...]
        du = jnp.zeros_like(dco)
        for j in range(CONVK):
            up_j = dco if j == 0 else _shift_up(dco, head, j)
            dcw_ref[3 - j:4 - j, :] += jnp.sum(up_j * u, axis=0, keepdims=True)
            du = du + cw_ref[3 - j:4 - j, :] * up_j
        dco_scr[...] = dco[0:8]
        out_ref[:, 0:512] = dz.astype(BF16)
        out_ref[:, 512:1536] = du.astype(BF16)
        out_ref[:, 1536:1664] = ddtr.astype(BF16)

        @pl.when(i == nc - 1)
        def _():
            dsk_ref[...] = _head_sums(jnp.broadcast_to(dskx_scr[...], (8, SW)), e_t)[0:1]

    e3, et2, tril3, triu3 = mats
    rev = lambda i: nc - 1 - i
    full = lambda a: pl.BlockSpec(a.shape, lambda i: (0,) * a.ndim)
    acc = lambda r, c: pl.BlockSpec((r, c), lambda i: (0, 0))
    return pl.pallas_call(
        body, name="ssd_bwd", grid=(nc,),
        in_specs=[pl.BlockSpec((BLK, CONVC), lambda i: (rev(i), 0)), pl.BlockSpec((BLK, CONVC), lambda i: (rev(i), 0)),
                  pl.BlockSpec((BLK, SW), lambda i: (rev(i), 0)), pl.BlockSpec((BLK, 128), lambda i: (rev(i), 0)),
                  pl.BlockSpec((BLK, SW), lambda i: (rev(i), 0)), pl.BlockSpec((1, NST, SW), lambda i: (rev(i), 0, 0)),
                  pl.BlockSpec((BLK, SW), lambda i: (rev(i), 1)),
                  full(conv_w), full(dtb), full(alog), full(dskx), full(ssm_w),
                  full(e3), full(et2), full(tril3), full(triu3)],
        out_specs=[pl.BlockSpec((BLK, 1664), lambda i: (rev(i), 0)),
                   acc(CONVK, CONVC), acc(1, CONVC), acc(1, SW), acc(1, 128), acc(1, 128), acc(1, 128)],
        out_shape=[jax.ShapeDtypeStruct((T, 1664), BF16),
                   jax.ShapeDtypeStruct((CONVK, CONVC), F32), jax.ShapeDtypeStruct((1, CONVC), F32),
                   jax.ShapeDtypeStruct((1, SW), F32), jax.ShapeDtypeStruct((1, 128), F32),
                   jax.ShapeDtypeStruct((1, 128), F32), jax.ShapeDtypeStruct((1, 128), F32)],
        scratch_shapes=[pltpu.VMEM((NST, SW), F32), pltpu.VMEM((8, CONVC), F32), pltpu.VMEM((1, SW), F32)],
        compiler_params=_cp("arbitrary"),
    )(xbc, co_all, z, dtr, ypre, states, dmix, conv_w, dtb, alog, dskx, ssm_w, e3, et2, tril3, triu3)


def _mix_ffn(x, attn, ynorm, tgt, mod6, norm2_w, final_w, w_out, w_gu, w_gu_own, s_arr, w_dn, tm):
    T = x.shape[0]
    nt = T // tm

    def body(x_ref, a_ref, y_ref, t_ref, mod_ref, n2_ref, fw_ref, wo_hbm, wgu_hbm, own_hbm, s_ref, wdn_hbm,
             sq_ref, dmix_ref, dx1_ref, h2_ref, act_ref, df_ref, dgu_ref, do_ref, sm_ref,
             wo, wgu, wdn, sems):
        i = pl.program_id(0)

        @pl.when(i == 0)
        def _():
            cps = [pltpu.make_async_copy(s, d, sems.at[k]) for k, (s, d) in
                   enumerate(((wo_hbm, wo), (wgu_hbm, wgu), (wdn_hbm, wdn)))]
            for c in cps:
                c.start()
            for c in cps:
                c.wait()
            own = pltpu.make_async_copy(
                own_hbm, wgu.at[:, pl.ds(pl.multiple_of(s_ref[0] * GU_SH, 128), GU_SH)], sems.at[3])
            own.start()
            own.wait()
            sq_ref[...] = jnp.zeros_like(sq_ref)
            sm_ref[...] = jnp.zeros_like(sm_ref)

        gate1, shift2, scale2, gate2 = mod_ref[2:3, :], mod_ref[3:4, :], mod_ref[4:5, :], mod_ref[5:6, :]
        n2w, fw = n2_ref[...], fw_ref[...]
        o = _dot(a_ref[...], wo[0:AW, :]) + _dot(y_ref[...], wo[AW:D, :])
        x1 = x_ref[...] + gate1 * o
        r2 = lax.rsqrt(jnp.mean(x1 * x1, axis=-1, keepdims=True) + EPS)
        xh2 = x1 * r2
        n2 = xh2 * n2w
        h2b = (n2 * (1.0 + scale2) + shift2).astype(BF16)
        h2_ref[...] = h2b
        f = jnp.zeros((tm, D), F32)
        saved = []
        for a, b in FF_SPLITS:
            gp = _dot(h2b, wgu[:, a:b])
            upj = _dot(h2b, wgu[:, DFF + a:DFF + b])
            sg = _sigmoid(gp)
            sl = gp * sg
            actb = (sl * upj).astype(BF16)
            act_ref[:, a:b] = actb
            f = f + _dot(actb, wdn[a:b, :])
            saved.append((gp, upj, sg, sl))
        x2 = x1 + gate2 * f
        r3 = lax.rsqrt(jnp.mean(x2 * x2, axis=-1, keepdims=True) + EPS)
        xh3 = x2 * r3
        err = xh3 * fw - t_ref[...]
        sq_ref[...] += jnp.sum(err * err, axis=0, keepdims=True)
        dy = err * (1.0 / D)
        dfw = jnp.sum(dy * xh3, axis=0, keepdims=True)
        dxh3 = dy * fw
        dx2 = r3 * (dxh3 - xh3 * jnp.mean(dxh3 * xh3, axis=-1, keepdims=True))
        dgate2 = jnp.sum(dx2 * f, axis=0, keepdims=True)
        dfb = (dx2 * gate2).astype(BF16)
        df_ref[...] = dfb
        dh2 = jnp.zeros((tm, D), F32)
        for (a, b), (gp, upj, sg, sl) in zip(FF_SPLITS, saved):
            dact = _dot_nt(dfb, wdn[a:b, :])
            dg = (dact * upj * (sg * (1.0 + gp * (1.0 - sg)))).astype(BF16)
            du = (dact * sl).astype(BF16)
            dgu_ref[:, a:b] = dg
            dgu_ref[:, DFF + a:DFF + b] = du
            dh2 = dh2 + _dot_nt(dg, wgu[:, a:b]) + _dot_nt(du, wgu[:, DFF + a:DFF + b])
        dshift2 = jnp.sum(dh2, axis=0, keepdims=True)
        dscale2 = jnp.sum(dh2 * n2, axis=0, keepdims=True)
        dn2 = dh2 * (1.0 + scale2)
        dn2w = jnp.sum(dn2 * xh2, axis=0, keepdims=True)
        dxh2 = dn2 * n2w
        dx1 = dx2 + r2 * (dxh2 - xh2 * jnp.mean(dxh2 * xh2, axis=-1, keepdims=True))
        dx1_ref[...] = dx1
        dgate1 = jnp.sum(dx1 * o, axis=0, keepdims=True)
        dob = (dx1 * gate1).astype(BF16)
        do_ref[...] = dob
        dmix_ref[...] = _dot_nt(dob, wo[...])
        sm_ref[...] += jnp.concatenate(
            [dfw, dn2w, dshift2, dscale2, dgate2, dgate1, jnp.zeros((2, D), F32)], axis=0)

    row = lambda w: pl.BlockSpec((tm, w), lambda i: (i, 0))
    full = lambda a: pl.BlockSpec(a.shape, lambda i: (0,) * a.ndim)
    anyspec = pl.BlockSpec(memory_space=pl.ANY)
    return pl.pallas_call(
        body, name="mix_ffn", grid=(nt,),
        in_specs=[row(D), row(AW), row(SW), row(D), full(mod6), full(norm2_w), full(final_w), anyspec, anyspec, anyspec,
                  pl.BlockSpec(memory_space=pltpu.SMEM), anyspec],
        out_specs=[pl.BlockSpec((1, D), lambda i: (0, 0)), row(D), row(D), row(D),
                   row(DFF), row(D), row(2 * DFF), row(D), pl.BlockSpec((8, D), lambda i: (0, 0))],
        out_shape=[jax.ShapeDtypeStruct((1, D), F32), jax.ShapeDtypeStruct((T, D), F32), jax.ShapeDtypeStruct((T, D), F32),
                   jax.ShapeDtypeStruct((T, D), BF16), jax.ShapeDtypeStruct((T, DFF), BF16),
                   jax.ShapeDtypeStruct((T, D), BF16), jax.ShapeDtypeStruct((T, 2 * DFF), BF16),
                   jax.ShapeDtypeStruct((T, D), BF16), jax.ShapeDtypeStruct((8, D), F32)],
        scratch_shapes=[pltpu.VMEM((D, D), BF16), pltpu.VMEM((D, 2 * DFF), BF16), pltpu.VMEM((DFF, D), BF16),
                        pltpu.SemaphoreType.DMA((4,))],
        compiler_params=_cp("arbitrary"),
    )(x, attn, ynorm, tgt, mod6, norm2_w, final_w, w_out, w_gu, w_gu_own, s_arr, w_dn)


def _in_proj_bwd(x, dx1, dqkv, dzxd, mod6, norm1_w, w_pad, tm):
    T = x.shape[0]

    def body(x_ref, dx1_ref, dq_ref, dz_ref, mod_ref, nw_ref, w_hbm, gx_ref, sm_ref, w_vmem, sem):
        _load_resident(w_hbm, w_vmem, sem)

        @pl.when(pl.program_id(0) == 0)
        def _():
            sm_ref[...] = jnp.zeros_like(sm_ref)

        nw = nw_ref[...]
        scale1 = mod_ref[1:2, :]
        sums = jnp.zeros((8, D), F32)
        for rows in (slice(0, tm // 2), slice(tm // 2, tm)):
            dh = _dot_nt(dq_ref[rows, :], w_vmem[:, 0:768]) + _dot_nt(dz_ref[rows, :], w_vmem[:, 768:IN_PAD])
            xv = x_ref[rows, :]
            r = lax.rsqrt(jnp.mean(xv * xv, axis=-1, keepdims=True) + EPS)
            xh = xv * r
            n1 = xh * nw
            dshift = jnp.sum(dh, axis=0, keepdims=True)
            dscale = jnp.sum(dh * n1, axis=0, keepdims=True)
            dn = dh * (1.0 + scale1)
            dnw = jnp.sum(dn * xh, axis=0, keepdims=True)
            dxh = dn * nw
            gx_ref[rows, :] = dx1_ref[rows, :] + r * (dxh - xh * jnp.mean(dxh * xh, axis=-1, keepdims=True))
            sums = sums + jnp.concatenate([dnw, dshift, dscale, jnp.zeros((5, D), F32)], axis=0)
        sm_ref[...] += sums

    row = lambda w: pl.BlockSpec((tm, w), lambda i: (i, 0))
    full = lambda a: pl.BlockSpec(a.shape, lambda i: (0,) * a.ndim)
    return pl.pallas_call(
        body, name="in_proj_bwd", grid=(T // tm,),
        in_specs=[row(D), row(D), row(768), row(1664), full(mod6), full(norm1_w), pl.BlockSpec(memory_space=pl.ANY)],
        out_specs=[row(D), pl.BlockSpec((8, D), lambda i: (0, 0))],
        out_shape=[jax.ShapeDtypeStruct((T, D), F32), jax.ShapeDtypeStruct((8, D), F32)],
        scratch_shapes=[pltpu.VMEM((D, IN_PAD), BF16), pltpu.SemaphoreType.DMA],
        compiler_params=_cp("arbitrary"),
    )(x, dx1, dqkv, dzxd, mod6, norm1_w, w_pad)


def _tn_matmul(a, b, K, N, tt, name, dep):
    T = a.shape[0]
    ja, jb = a.shape[1] // K, b.shape[1] // N
    J = max(ja, jb)

    def body(a_ref, b_ref, dep_ref, o_ref):
        t = pl.program_id(1)
        prod = _dot_tn(a_ref[...], b_ref[...])

        @pl.when(t == 0)
        def _():
            o_ref[0] = prod

        @pl.when(t > 0)
        def _():
            o_ref[0] += prod

    return pl.pallas_call(
        body, name=name, grid=(J, T // tt),
        in_specs=[pl.BlockSpec((tt, K), lambda j, t: (t, j if ja > 1 else 0)),
                  pl.BlockSpec((tt, N), lambda j, t: (t, j if jb > 1 else 0)),
                  pl.BlockSpec((8, 128), lambda j, t: (0, 0))],
        out_specs=pl.BlockSpec((1, K, N), lambda j, t: (j, 0, 0)),
        out_shape=jax.ShapeDtypeStruct((J, K, N), F32),
        compiler_params=_cp("parallel", "arbitrary"),
    )(a, b, dep)


def _adam_math(w, g, m, v):
    m = B1 * m + (1.0 - B1) * g
    v = B2 * v + (1.0 - B2) * (g * g)
    m_hat = m / (1.0 - B1 ** STEP)
    v_hat = v / (1.0 - B2 ** STEP)
    delta = -LR * (m_hat / (jnp.sqrt(v_hat) + AEPS) + WD * w)
    return delta, m, v


def _adam_2d(w, mine, land, m, v, c_arr, rb, name):
    R, C = w.shape
    nbh = R // 2 // rb

    def body(c_ref, w_ref, mine_ref, land_ref, m_ref, v_ref, go_ref, d_ref, mo_ref, vo_ref):
        g = jnp.where(pl.program_id(0) // nbh == c_ref[0], mine_ref[...], land_ref[...])
        d, mn, vn = _adam_math(w_ref[...], g, m_ref[...], v_ref[...])
        go_ref[...] = g
        d_ref[...] = d
        mo_ref[...] = mn
        vo_ref[...] = vn

    spec = pl.BlockSpec((rb, C), lambda i, c_ref: (i, 0))
    mine_spec = pl.BlockSpec((rb, C), lambda i, c_ref: (jnp.clip(i - c_ref[0] * nbh, 0, nbh - 1), 0))
    return pl.pallas_call(
        body, name=name,
        grid_spec=pltpu.PrefetchScalarGridSpec(
            num_scalar_prefetch=1, grid=(R // rb,), in_specs=[spec, mine_spec, spec, spec, spec], out_specs=[spec] * 4),
        out_shape=[jax.ShapeDtypeStruct((R, C), F32)] * 4, compiler_params=_cp("parallel"),
    )(c_arr, w, mine, land, m, v)


def _adam_w_in(w3, mine, land, m3, v3, c_arr):
    n = w3.shape[0]

    def body(c_ref, w_hbm, mine_ref, land_ref, m_hbm, v_hbm, g_hbm, d_hbm, mo_hbm, vo_hbm, bufs, sems):
        ins = [pltpu.make_async_copy(src.at[:, 0], bufs.at[k], sems.at[k]) for k, src in enumerate((w_hbm, m_hbm, v_hbm))]
        for cp in ins:
            cp.start()
        half = D // 2
        top = jnp.where(c_ref[0] == 0, mine_ref[...], land_ref[0:half, :])
        bot = jnp.where(c_ref[0] == 1, mine_ref[...], land_ref[half:D, :])
        g = jnp.concatenate([top, bot], axis=0)
        eye = (_iota((D, D), 0) == _iota((D, D), 1)).astype(BF16)
        g_t = jnp.zeros((n, D), F32)
        r = g
        for i in range(3):
            p = r.astype(BF16)
            g_t = g_t + _dot_tn(p, eye)
            if i < 2:
                r = r - p.astype(F32)
        for cp in ins:
            cp.wait()
        d, mn, vn = _adam_math(bufs[0], g_t, bufs[1], bufs[2])
        for k, val in enumerate((g_t, d, mn, vn)):
            bufs[3 + k] = val
        outs = [pltpu.make_async_copy(bufs.at[3 + k], dst.at[:, 0], sems.at[3 + k])
                for k, dst in enumerate((g_hbm, d_hbm, mo_hbm, vo_hbm))]
        for cp in outs:
            cp.start()
        for cp in outs:
            cp.wait()

    anyspec = pl.BlockSpec(memory_space=pl.ANY)
    vm = pl.BlockSpec(memory_space=pltpu.VMEM)
    return pl.pallas_call(
        body, name="adam_w_in",
        in_specs=[pl.BlockSpec(memory_space=pltpu.SMEM), anyspec, vm, vm, anyspec, anyspec], out_specs=[anyspec] * 4,
        out_shape=[jax.ShapeDtypeStruct(w3.shape, F32)] * 4,
        scratch_shapes=[pltpu.VMEM((7, n, D), F32), pltpu.SemaphoreType.DMA((7,))],
        compiler_params=pltpu.CompilerParams(vmem_limit_bytes=VMEM_LIMIT),
    )(c_arr, w3, mine, land, m3, v3)


def _adam_w_ada(sc_all, dmod_s, w, m, v, rb):
    R, C = w.shape

    def body(sc_ref, dm_ref, w_ref, m_ref, v_ref, g_ref, d_ref, mo_ref, vo_ref):
        g = lax.dot_general(sc_ref[...], dm_ref[...], (((0,), (0,)), ((), ())), precision=HI, preferred_element_type=F32)
        d, mn, vn = _adam_math(w_ref[...], g, m_ref[...], v_ref[...])
        g_ref[...] = g
        d_ref[...] = d
        mo_ref[...] = mn
        vo_ref[...] = vn

    spec = pl.BlockSpec((rb, C), lambda i: (i, 0))
    return pl.pallas_call(
        body, name="adam_w_ada", grid=(R // rb,),
        in_specs=[pl.BlockSpec((8, rb), lambda i: (0, i)), pl.BlockSpec((8, C), lambda i: (0, 0)), spec, spec, spec],
        out_specs=[spec] * 4, out_shape=[jax.ShapeDtypeStruct((R, C), F32)] * 4, compiler_params=_cp("parallel"),
    )(sc_all, dmod_s, w, m, v)


def _adam_small(grads, ws, ms, vs):
    k = len(ws)

    def body(*refs):
        g, w, m, v = refs[0:k], refs[k:2 * k], refs[2 * k:3 * k], refs[3 * k:4 * k]
        g_o, d_o, m_o, v_o = refs[4 * k:5 * k], refs[5 * k:6 * k], refs[6 * k:7 * k], refs[7 * k:8 * k]
        for i in range(k):
            gi = g[i][...]
            d, mn, vn = _adam_math(w[i][...], gi, m[i][...], v[i][...])
            g_o[i][...] = gi
            d_o[i][...] = d
            m_o[i][...] = mn
            v_o[i][...] = vn

    shapes = [jax.ShapeDtypeStruct(w.shape, F32) for w in ws]
    vm = pl.BlockSpec(memory_space=pltpu.VMEM)
    outs = pl.pallas_call(
        body, name="adam_small", in_specs=[vm] * (4 * k), out_specs=[vm] * (4 * k), out_shape=shapes * 4,
    )(*grads, *ws, *ms, *vs)
    return outs[0:k], outs[k:2 * k], outs[2 * k:3 * k], outs[3 * k:4 * k]


def _pos():
    return lax.axis_index("x"), lax.axis_index("y"), lax.axis_index("c")


def _flip(v, bit):
    return 1 - v if bit else v


def _peer(k):
    x, y, c = _pos()
    return (_flip(x, (k >> 2) & 1), _flip(y, (k >> 1) & 1), _flip(c, k & 1))


def _logical(p):
    return 4 * p[0] + 2 * p[1] + p[2]


def _gather8(src_ref, dst_ref, send_sems, recv_sems):
    me = _logical(_pos())
    dst_ref[pl.ds(me, 1)] = src_ref[...][None]
    copies = []
    for k in range(1, 8):
        cp = pltpu.make_async_remote_copy(src_ref, dst_ref.at[me], send_sems.at[k - 1], recv_sems.at[k - 1],
                                          device_id=_peer(k), device_id_type=MESH)
        cp.start()
        copies.append(cp)
    for k in range(1, 8):
        pltpu.make_async_remote_copy(src_ref, dst_ref.at[_logical(_peer(k))], send_sems.at[k - 1], recv_sems.at[k - 1],
                                     device_id=_peer(k), device_id_type=MESH).wait_recv()
    for cp in copies:
        cp.wait_send()


def _rows_select(ref3, width):
    row = _iota((8, width), 0)
    out = jnp.zeros((8, width), F32)
    for i in range(8):
        out = jnp.where(row == i, ref3[i][:, 0:width], out)
    return out


def _mod_exchange(payload, w_ada_s, b_ada4):
    n_sh = w_ada_s.shape[1]

    def body(pay_ref, w_ref, b_ref, gat_ref, mod_ref, token, p3, sa, ra, sb, rb):
        token[...] = jnp.zeros_like(token)
        x, y, c = _pos()
        me = _logical((x, y, c))
        my_s = 2 * x + y
        _gather8(pay_ref, gat_ref, sa, ra)
        cmat = _rows_select(gat_ref, D)
        prod = _dot_hi(cmat * _sigmoid(cmat), w_ref[...])
        for b in range(8):
            p3[b] = prod[b:b + 1, :]
        mod_ref[pl.ds(my_s, 1)] = p3[pl.ds(me, 1)] + b_ref[pl.ds(my_s, 1)]
        ks = (2, 4, 6)
        copies = []
        for i, k in enumerate(ks):
            pr = _peer(k)
            cp = pltpu.make_async_remote_copy(p3.at[_logical(pr)], mod_ref.at[my_s], sb.at[i], rb.at[i],
                                              device_id=pr, device_id_type=MESH)
            cp.start()
            copies.append(cp)
        for i, k in enumerate(ks):
            pr = _peer(k)
            s_src = 2 * pr[0] + pr[1]
            pltpu.make_async_remote_copy(p3.at[0], mod_ref.at[s_src], sb.at[i], rb.at[i],
                                         device_id=pr, device_id_type=MESH).wait_recv()
            mod_ref[pl.ds(s_src, 1)] = mod_ref[pl.ds(s_src, 1)] + b_ref[pl.ds(s_src, 1)]
        for cp in copies:
            cp.wait_send()

    vm = pl.BlockSpec(memory_space=pltpu.VMEM)
    return pl.pallas_call(
        body, name="mod_exchange", in_specs=[vm, vm, vm], out_specs=[vm, vm, vm],
        out_shape=[jax.ShapeDtypeStruct((8, 1, payload.shape[1]), F32), jax.ShapeDtypeStruct((4, 1, n_sh), F32),
                   jax.ShapeDtypeStruct((8, 128), F32)],
        scratch_shapes=[pltpu.VMEM((8, 1, n_sh), F32), pltpu.SemaphoreType.DMA((7,)), pltpu.SemaphoreType.DMA((7,)),
                        pltpu.SemaphoreType.DMA((3,)), pltpu.SemaphoreType.DMA((3,))],
        compiler_params=pltpu.CompilerParams(vmem_limit_bytes=VMEM_LIMIT),
    )(payload, w_ada_s, b_ada4)


def _chips():
    x, y, _ = _pos()
    out = []
    for k in (1, 2, 3):
        px, py = _flip(x, (k >> 1) & 1), _flip(y, k & 1)
        out.append((px, py, 2 * px + py))
    return out


def _half_rows(ref, which):
    half = ref.shape[-2] // 2
    return pl.ds(pl.multiple_of(which * half, 8), half)


def _weight_gather(shards):
    nw = len(shards)

    def body(*refs):
        ins, outs, token = refs[:nw], refs[nw:2 * nw], refs[2 * nw]
        send, recv, fsend, frecv = refs[2 * nw + 1:]
        token[...] = jnp.zeros_like(token)
        x, y, c = _pos()
        my_s = 2 * x + y
        sib = (x, y, 1 - c)
        chips = _chips()
        sends = []
        for w in range(nw):
            mine = _half_rows(ins[w], c)
            for k, (px, py, _) in enumerate(chips):
                cp = pltpu.make_async_remote_copy(ins[w].at[mine], outs[w].at[my_s, mine], send.at[3 * w + k],
                                                  recv.at[3 * w + k], device_id=(px, py, c), device_id_type=MESH)
                cp.start()
                sends.append(cp)
        for w in range(nw):
            mine = _half_rows(ins[w], c)
            for k, (px, py, ps) in enumerate(chips):
                got = outs[w].at[ps, mine]
                pltpu.make_async_remote_copy(got, got, send.at[3 * w + k], recv.at[3 * w + k],
                                             device_id=(px, py, c), device_id_type=MESH).wait_recv()
                cp = pltpu.make_async_remote_copy(got, got, fsend.at[3 * w + k], frecv.at[3 * w + k],
                                                  device_id=sib, device_id_type=MESH)
                cp.start()
                sends.append(cp)
        for w in range(nw):
            other = _half_rows(ins[w], 1 - c)
            for k, (px, py, ps) in enumerate(chips):
                got = outs[w].at[ps, other]
                pltpu.make_async_remote_copy(got, got, fsend.at[3 * w + k], frecv.at[3 * w + k],
                                             device_id=sib, device_id_type=MESH).wait_recv()
        for cp in sends:
            cp.wait_send()

    hbm = pl.BlockSpec(memory_space=pltpu.HBM)
    return pl.pallas_call(
        body, name="weight_gather", in_specs=[hbm] * nw,
        out_specs=[hbm] * nw + [pl.BlockSpec(memory_space=pltpu.VMEM)],
        out_shape=[pltpu.HBM((4,) + s.shape, s.dtype) for s in shards] + [jax.ShapeDtypeStruct((8, 128), F32)],
        scratch_shapes=[pltpu.SemaphoreType.DMA((3 * nw,)), pltpu.SemaphoreType.DMA((3 * nw,)),
                        pltpu.SemaphoreType.DMA((3 * nw,)), pltpu.SemaphoreType.DMA((3 * nw,))],
    )(*shards)


def _small_reduce(vec):
    n = vec.shape[1]

    def body(v_ref, tot_ref, gat_ref, sa, ra):
        _gather8(v_ref, gat_ref, sa, ra)
        tot = gat_ref[0]
        for i in range(1, 8):
            tot = tot + gat_ref[i]
        tot_ref[...] = tot

    vm = pl.BlockSpec(memory_space=pltpu.VMEM)
    return pl.pallas_call(
        body, name="small_reduce", in_specs=[vm], out_specs=[vm, vm],
        out_shape=[jax.ShapeDtypeStruct((1, n), F32), jax.ShapeDtypeStruct((8, 1, n), F32)],
        scratch_shapes=[pltpu.SemaphoreType.DMA((7,)), pltpu.SemaphoreType.DMA((7,))],
    )(vec)


def _add_half(g, sib, c_arr, rb, name):
    _, R, C = g.shape
    half = R // 2
    nb = half // rb

    def body(c_ref, g_ref, s_ref, o_ref):
        o_ref[...] = (g_ref[...] + s_ref[...]).astype(BF16)

    return pl.pallas_call(
        body, name=name,
        grid_spec=pltpu.PrefetchScalarGridSpec(
            num_scalar_prefetch=1, grid=(4, nb),
            in_specs=[pl.BlockSpec((1, rb, C), lambda s, i, c_ref: (s, c_ref[0] * nb + i, 0)),
                      pl.BlockSpec((1, rb, C), lambda s, i, c_ref: (s, i, 0))],
            out_specs=pl.BlockSpec((1, rb, C), lambda s, i, c_ref: (s, i, 0))),
        out_shape=jax.ShapeDtypeStruct((4, half, C), BF16),
        compiler_params=_cp("parallel", "parallel"),
    )(c_arr, g, sib)


def _sum4(parts, land, s_arr, rb, name):
    _, H, C = land.shape

    def body(s_ref, own_ref, r_ref, o_ref):
        own = own_ref[0].astype(F32)
        tot = jnp.zeros((rb, C), F32)
        for j in range(4):
            tot = tot + jnp.where(s_ref[0] == j, own, r_ref[j].astype(F32))
        o_ref[...] = tot

    return pl.pallas_call(
        body, name=name,
        grid_spec=pltpu.PrefetchScalarGridSpec(
            num_scalar_prefetch=1, grid=(H // rb,),
            in_specs=[pl.BlockSpec((1, rb, C), lambda i, s_ref: (s_ref[0], i, 0)),
                      pl.BlockSpec((4, rb, C), lambda i, s_ref: (0, i, 0))],
            out_specs=pl.BlockSpec((rb, C), lambda i, s_ref: (i, 0))),
        out_shape=jax.ShapeDtypeStruct((H, C), F32), compiler_params=_cp("parallel"),
    )(s_arr, parts, land)


HBM_SPEC = pl.BlockSpec(memory_space=pltpu.HBM)
SEM_SPEC = pl.BlockSpec(memory_space=pltpu.SEMAPHORE)
EFFECT = pltpu.SideEffectType.DATAFLOW_SIDE_EFFECTING


def _split_start(name, bufs, n_sem, plan):
    nb = len(bufs)

    def body(*refs):
        ins, send, recv, token = refs[:nb], refs[nb], refs[nb + 1], refs[-1]
        for i, (src, dst, dev, _) in enumerate(plan(ins)):
            pltpu.make_async_remote_copy(src, dst, send.at[i], recv.at[i], device_id=dev, device_id_type=MESH).start()
        token[...] = jnp.zeros_like(token)

    outs = pl.pallas_call(
        body, name=name,
        out_shape=(pltpu.SemaphoreType.DMA((n_sem,)), pltpu.SemaphoreType.DMA((n_sem,)),
                   *[pltpu.HBM(b.shape, b.dtype) for b in bufs], jax.ShapeDtypeStruct((8, 128), F32)),
        in_specs=[HBM_SPEC] * nb,
        out_specs=(SEM_SPEC, SEM_SPEC, *([HBM_SPEC] * nb), pl.BlockSpec(memory_space=pltpu.VMEM)),
        input_output_aliases={i: 2 + i for i in range(nb)},
        compiler_params=pltpu.CompilerParams(has_side_effects=EFFECT),
    )(*[pltpu.with_memory_space_constraint(b, pltpu.HBM) for b in bufs])
    return outs[0], outs[1], list(outs[2:2 + nb]), outs[-1]


def _split_wait(name, send, recv, bufs, after, plan):
    nb = len(bufs)

    def body(*refs):
        ins, send_s, recv_s = refs[:nb], refs[nb], refs[nb + 1]
        for i, (src, dst, dev, mine) in enumerate(plan(ins)):
            pltpu.make_async_remote_copy(src, dst, send_s.at[i], recv_s.at[i], device_id=dev,
                                         device_id_type=MESH).wait_send()
            pltpu.make_async_remote_copy(src, mine, send_s.at[i], recv_s.at[i], device_id=dev,
                                         device_id_type=MESH).wait_recv()

    outs = pl.pallas_call(
        body, name=name, out_shape=[pltpu.HBM(b.shape, b.dtype) for b in bufs],
        in_specs=[HBM_SPEC] * nb + [SEM_SPEC, SEM_SPEC, pl.BlockSpec(memory_space=pl.ANY)],
        out_specs=[HBM_SPEC] * nb, input_output_aliases={i: i for i in range(nb)},
        compiler_params=pltpu.CompilerParams(has_side_effects=EFFECT),
    )(*bufs, send, recv, after)
    return list(outs)


def _slot(land, s, rows, cols):
    if cols is None:
        return land.at[s, rows]
    return land.at[rows, pl.ds(pl.multiple_of(s * cols, 128), cols)]


def _plan_gather_ici(cols):
    nw = len(cols)

    def plan(refs):
        x, y, c = _pos()
        my_s = 2 * x + y
        out = []
        for w in range(nw):
            mine = _half_rows(refs[w], c)
            for px, py, ps in _chips():
                out.append((refs[w].at[mine], _slot(refs[nw + w], my_s, mine, cols[w]), (px, py, c),
                            _slot(refs[nw + w], ps, mine, cols[w])))
        return out
    return plan


def _plan_gather_fwd(cols, rows):
    def plan(refs):
        x, y, c = _pos()
        out = []
        for w in range(len(cols)):
            half = rows[w] // 2
            mine = pl.ds(pl.multiple_of(c * half, 8), half)
            other = pl.ds(pl.multiple_of((1 - c) * half, 8), half)
            for px, py, ps in _chips():
                got = _slot(refs[w], ps, mine, cols[w])
                out.append((got, got, (x, y, 1 - c), _slot(refs[w], ps, other, cols[w])))
        return out
    return plan


def _plan_swap(nw):
    def plan(refs):
        x, y, c = _pos()
        return [(refs[w].at[:, _half_rows(refs[w], 1 - c)], refs[nw + w], (x, y, 1 - c), refs[nw + w])
                for w in range(nw)]
    return plan


def _plan_scatter(nw):
    def plan(refs):
        x, y, c = _pos()
        my_s = 2 * x + y
        out = []
        for w in range(nw):
            for px, py, ps in _chips():
                out.append((refs[w].at[ps], refs[nw + w].at[my_s], (px, py, c), refs[nw + w].at[ps]))
        return out
    return plan


def _plan_join(nw):
    def plan(refs):
        x, y, c = _pos()
        out = []
        for w in range(nw):
            land = refs[nw + w]
            out.append((refs[w], land.at[_half_rows(land, c)], (x, y, 1 - c), land.at[_half_rows(land, 1 - c)]))
        return out
    return plan


def _hbm_empty(shape, dtype):
    return pltpu.with_memory_space_constraint(lax.empty(shape, dtype), pltpu.HBM)


def _put_slot(land, own, slot):
    return lax.dynamic_update_slice(land, own[None], (slot,) + (0,) * own.ndim)


def _pad_lanes(a, n):
    return jnp.pad(a, ((0, 0), (0, n - a.shape[1])))


def kernel(x, c, positions, w_ada, b_ada, norm1_w, w_in, conv_w, conv_b, dt_bias, a_log, d_skip, attn_sinks, ssm_norm_w, w_out, norm2_w, w_gate_up, w_down, final_norm_w, loss_target, m_w_ada, m_b_ada, m_norm1_w, m_w_in, m_conv_w, m_conv_b, m_dt_bias, m_a_log, m_d_skip, m_attn_sinks, m_ssm_norm_w, m_w_out, m_norm2_w, m_w_gate_up, m_w_down, m_final_norm_w, v_w_ada, v_b_ada, v_norm1_w, v_w_in, v_conv_w, v_conv_b, v_dt_bias, v_a_log, v_d_skip, v_attn_sinks, v_ssm_norm_w, v_w_out, v_norm2_w, v_w_gate_up, v_w_down, v_final_norm_w):
    T = x.shape[1]
    tm = min(256, T)
    xi, yi, ci = lax.axis_index("x"), lax.axis_index("y"), lax.axis_index("c")
    my_s = 2 * xi + yi
    xs = x[0]
    tgt = loss_target[0]

    payload = jnp.concatenate([c, conv_w[0].reshape(1, CONVK * 256)], axis=1)
    gat, mod4, tok = _mod_exchange(payload, w_ada[0], b_ada.reshape(4, 1, 1536))
    mod6 = mod4.reshape(6, D)
    c_all = gat[:, 0, 0:D]
    cw_dev = gat[:, 0, D:].reshape(4, 2, CONVK, 256)[:, 0]
    conv_full = cw_dev.transpose(1, 0, 2).reshape(CONVK, CONVC)

    w_in_b = (w_in[0] + tok[0, 0]).astype(BF16)
    s_i, r_i, bufs, tok = _split_start("wgather_in_ici_start", [w_in_b, _hbm_empty((4,) + w_in_b.shape, BF16)], 3,
                                       _plan_gather_ici([None]))
    inv_freq = (10000.0 ** (-jnp.arange(32, dtype=F32) / 32))
    cos, sin_s = _rope_tables(positions, inv_freq.reshape(32, 1) + tok[0:1, 0:1], min(512, T))
    bufs = _split_wait("wgather_in_ici_wait", s_i, r_i, bufs, cos, _plan_gather_ici([None]))
    s_j, r_j, bufs, tok = _split_start("wgather_in_fwd_start", bufs[1:], 3, _plan_gather_fwd([None], [D]))
    bufs = _split_wait("wgather_in_fwd_wait", s_j, r_j, bufs, tok, _plan_gather_fwd([None], [D]))
    g_in = _put_slot(bufs[0], w_in_b, my_s)
    w_pad = jnp.concatenate([g_in[0], g_in[1], g_in[2], g_in[3], jnp.zeros((D, IN_PAD - IN_PROJ), BF16)], axis=1)

    late = [(w_out[0] + tok[0, 0]).astype(BF16), w_gate_up[0].astype(BF16), w_down[0].astype(BF16)]
    lands = [_hbm_empty((4, D // 4, D), BF16), _hbm_empty((D, 2 * DFF), BF16), _hbm_empty((4, DFF // 4, D), BF16)]
    cols3, rows3 = [None, GU_SH, None], [D // 4, D, DFF // 4]
    s_a, r_a, bufs, tok = _split_start("wgather_ici_start", late + lands, 9, _plan_gather_ici(cols3))

    qkv, z, xbc, dtr, h1b = _in_proj_fwd(xs, cos, sin_s, mod6 + tok[0, 0], norm1_w, w_pad, min(512, T))
    sinks = attn_sinks
    attn, lse = _attn_fwd(qkv, sinks)
    bufs = _split_wait("wgather_ici_wait", s_a, r_a, bufs, attn, _plan_gather_ici(cols3))
    s_b, r_b, lands, tok = _split_start("wgather_fwd_start", bufs[3:], 9, _plan_gather_fwd(cols3, rows3))
    dtb = _pad_lanes(dt_bias, 128)
    alog = _pad_lanes(a_log, 128)
    dskx = jnp.repeat(d_skip, HD, axis=1)
    mats = _ssd_mats()
    ynorm, ypre, states, conv_pre = _ssd_fwd(xbc, z, dtr, conv_full, conv_b, dtb + tok[0, 0], alog, dskx, ssm_norm_w,
                                             mats)
    lands = _split_wait("wgather_fwd_wait", s_b, r_b, lands, ynorm, _plan_gather_fwd(cols3, rows3))
    w_out_f = _put_slot(lands[0], late[0], my_s).reshape(D, D)
    w_dn_f = _put_slot(lands[2], late[2], my_s).reshape(DFF, D)
    s_arr = my_s.reshape(1).astype(jnp.int32)

    fw2 = final_norm_w.reshape(1, D)
    sq, dmix, dx1, h2b, act, dfb, dgu, dob, sm_ffn = _mix_ffn(
        xs, attn, ynorm, tgt, mod6, norm2_w, fw2, w_out_f, lands[1], late[1], s_arr, w_dn_f, tm)

    tt = min(2048, T)
    c_arr = ci.reshape(1).astype(jnp.int32)
    tok0 = jnp.zeros((8, 128), F32)
    gw_dn4 = _tn_matmul(act, dfb, GU_SH, D, tt, "dw_down", tok0).reshape(4, DFF // 4, D)
    gw_gu4 = _tn_matmul(h2b, dgu, D, GU_SH, tt, "dw_gate_up", tok0)
    gw_out4 = jnp.concatenate(
        [_tn_matmul(attn, dob, AW, D, tt, "dw_out_a", tok0)[0],
         _tn_matmul(ynorm, dob, SW, D, tt, "dw_out_y", tok0)[0]], axis=0).reshape(4, D // 4, D)
    big1 = [gw_out4, gw_gu4, gw_dn4]
    rbs1 = [128, 128, 176]
    sib1 = [_hbm_empty((4, g.shape[1] // 2, g.shape[2]), F32) for g in big1]
    s_c, r_c, bufs, tok = _split_start("gswap_start", big1 + sib1, 3, _plan_swap(3))

    dzxd, d_cw, d_cb, d_sw, d_sk, d_dtb, d_av = _ssd_bwd(
        xbc, conv_pre, z, dtr, ypre, states, dmix, conv_full, dtb + tok[0, 0], alog, dskx, ssm_norm_w, mats)
    bufs = _split_wait("gswap_wait", s_c, r_c, bufs, dzxd, _plan_swap(3))
    sums1 = [_add_half(g, s, c_arr, rb, "grad_add_%d" % i)
             for i, (g, s, rb) in enumerate(zip(bufs[:3], bufs[3:], rbs1))]
    land1 = [_hbm_empty(p.shape, BF16) for p in sums1]
    s_d, r_d, bufs, tok = _split_start("gscatter_start", sums1 + land1, 9, _plan_scatter(3))
    dqkv, d_sinks = _attn_bwd(qkv, sinks + tok[0:1, 0:8], lse, dmix, cos, sin_s)
    bufs = _split_wait("gscatter_wait", s_d, r_d, bufs, dqkv, _plan_scatter(3))
    halves1 = [_sum4(p, l, s_arr, rb, "grad_sum_%d" % i)
               for i, (p, l, rb) in enumerate(zip(bufs[:3], bufs[3:], rbs1))]
    full1 = [_hbm_empty((2 * h.shape[0], h.shape[1]), F32) for h in halves1]
    s_e, r_e, bufs, tok = _split_start("gjoin_start", halves1 + full1, 3, _plan_join(3))
    gq = _tn_matmul(h1b, dqkv, D, 768, tt, "dw_in_qkv", tok)[0]
    gz = _tn_matmul(h1b, dzxd, D, 1664, tt, "dw_in_zxd", tok)[0]
    gw_in4 = jnp.stack([gq[:, :IN_SH], jnp.concatenate([gq[:, IN_SH:], gz[:, :2 * IN_SH - 768]], axis=1),
                        gz[:, 2 * IN_SH - 768:3 * IN_SH - 768], gz[:, 3 * IN_SH - 768:4 * IN_SH - 768]])
    joined1 = _split_wait("gjoin_wait", s_e, r_e, bufs, gw_in4, _plan_join(3))

    sib0 = _hbm_empty((4, D // 2, IN_SH), F32)
    s_f, r_f, bufs, tok = _split_start("gswap_in_start", [gw_in4, sib0], 1, _plan_swap(1))
    bufs = _split_wait("gswap_in_wait", s_f, r_f, bufs, tok, _plan_swap(1))
    sum0 = _add_half(bufs[0], bufs[1], c_arr, 128, "grad_add_in")
    s_g, r_g, bufs, tok = _split_start("gscatter_in_start", [sum0, _hbm_empty(sum0.shape, BF16)], 3, _plan_scatter(1))
    grad_x, sm_in = _in_proj_bwd(xs, dx1, dqkv, dzxd, mod6 + tok[0, 0], norm1_w, w_pad, min(512, T))
    bufs = _split_wait("gscatter_in_wait", s_g, r_g, bufs, grad_x, _plan_scatter(1))
    half0 = _sum4(bufs[0], bufs[1], s_arr, 128, "grad_sum_in")
    s_h, r_h, bufs, tok = _split_start("gjoin_in_start", [half0, _hbm_empty((D, IN_SH), F32)], 1, _plan_join(1))
    joined0 = _split_wait("gjoin_in_wait", s_h, r_h, bufs, tok, _plan_join(1))

    a_neg = -jnp.exp(alog)
    pieces = [sm_in[1:2], sm_in[2:3], sm_ffn[5:6], sm_ffn[2:3], sm_ffn[3:4], sm_ffn[4:5],
              sm_in[0:1], sm_ffn[1:2], sm_ffn[0:1], d_cb, d_cw.reshape(1, CONVK * CONVC),
              _pad_lanes(d_sw, SW), d_dtb, d_av * a_neg, d_sk, d_sinks,
              _pad_lanes((0.5 / D * jnp.sum(sq)).reshape(1, 1), 128)]
    vec = jnp.concatenate(pieces, axis=1)
    tot, allv = _small_reduce(vec)
    o = 0
    offs = []
    for p in pieces:
        offs.append(o)
        o += p.shape[1]
    seg = lambda i, n: tot[:, offs[i]:offs[i] + n]
    g_b_ada = tot[:, 0:6 * D]
    g_norm1, g_norm2, g_final, g_conv_b = seg(6, D), seg(7, D), seg(8, D), seg(9, D)
    g_conv_w = lax.dynamic_slice_in_dim(seg(10, CONVK * CONVC).reshape(CONVK, CONVC), my_s * 256, 256, axis=1)
    g_ssm_w, g_dtb, g_alog, g_dsk, g_sink = seg(11, SW), seg(12, 8), seg(13, 8), seg(14, 8), seg(15, 8)
    loss = tot[0, offs[16]]

    small_names = ["b_ada", "norm1_w", "conv_w", "conv_b", "dt_bias", "a_log", "d_skip", "attn_sinks", "ssm_norm_w",
                   "norm2_w", "final_norm_w"]
    small_g = [g_b_ada, g_norm1, g_conv_w, g_conv_b, g_dtb, g_alog, g_dsk, g_sink, g_ssm_w, g_norm2, g_final]
    as2d = lambda a: a.reshape(-1, a.shape[-1])
    small_w = [as2d(a) for a in (b_ada, norm1_w, conv_w, conv_b, dt_bias, a_log, d_skip, attn_sinks, ssm_norm_w,
                                 norm2_w, final_norm_w)]
    small_m = [as2d(a) for a in (m_b_ada, m_norm1_w, m_conv_w, m_conv_b, m_dt_bias, m_a_log, m_d_skip, m_attn_sinks,
                                 m_ssm_norm_w, m_norm2_w, m_final_norm_w)]
    small_v = [as2d(a) for a in (v_b_ada, v_norm1_w, v_conv_w, v_conv_b, v_dt_bias, v_a_log, v_d_skip, v_attn_sinks,
                                 v_ssm_norm_w, v_norm2_w, v_final_norm_w)]
    small_g, sd, smn, svn = _adam_small(small_g, small_w, small_m, small_v)

    sc_all = c_all * jax.nn.sigmoid(c_all)
    dmod_all = allv[:, 0, 0:6 * D]
    dmod_s = lax.dynamic_slice_in_dim(dmod_all, my_s * 1536, 1536, axis=1)
    g_ada, d_ada, m_ada, v_ada = _adam_w_ada(sc_all, dmod_s, w_ada[0], m_w_ada[0], v_w_ada[0], 256)
    native = lambda a: a.transpose(2, 0, 1)
    g_in_s, d_in, m_in, v_in = [a.transpose(1, 2, 0) for a in _adam_w_in(
        native(w_in), joined0[0], joined0[1], native(m_w_in), native(v_w_in), c_arr)]
    g_out_s, d_out, m_out, v_out = _adam_2d(w_out[0], joined1[0], joined1[3], m_w_out[0], v_w_out[0], c_arr, 128,
                                            "adam_w_out")
    g_gu_s, d_gu, m_gu, v_gu = _adam_2d(w_gate_up[0], joined1[1], joined1[4], m_w_gate_up[0], v_w_gate_up[0], c_arr,
                                        256, "adam_w_gate_up")
    g_dn_s, d_dn, m_dn, v_dn = _adam_2d(w_down[0], joined1[2], joined1[5], m_w_down[0], v_w_down[0], c_arr, 352,
                                        "adam_w_down")

    order = ["w_ada", "b_ada", "norm1_w", "w_in", "conv_w", "conv_b", "dt_bias", "a_log", "d_skip", "attn_sinks",
             "ssm_norm_w", "w_out", "norm2_w", "w_gate_up", "w_down", "final_norm_w"]
    shapes = dict(w_ada=w_ada.shape, b_ada=b_ada.shape, norm1_w=norm1_w.shape, w_in=w_in.shape, conv_w=conv_w.shape,
                  conv_b=conv_b.shape, dt_bias=dt_bias.shape, a_log=a_log.shape, d_skip=d_skip.shape,
                  attn_sinks=attn_sinks.shape, ssm_norm_w=ssm_norm_w.shape, w_out=w_out.shape, norm2_w=norm2_w.shape,
                  w_gate_up=w_gate_up.shape, w_down=w_down.shape, final_norm_w=final_norm_w.shape)
    grads = dict(w_ada=g_ada, w_in=g_in_s, w_out=g_out_s, w_gate_up=g_gu_s, w_down=g_dn_s)
    deltas = dict(w_ada=d_ada, w_in=d_in, w_out=d_out, w_gate_up=d_gu, w_down=d_dn)
    new_m = dict(w_ada=m_ada, w_in=m_in, w_out=m_out, w_gate_up=m_gu, w_down=m_dn)
    new_v = dict(w_ada=v_ada, w_in=v_in, w_out=v_out, w_gate_up=v_gu, w_down=v_dn)
    for i, nme in enumerate(small_names):
        grads[nme], deltas[nme], new_m[nme], new_v[nme] = small_g[i], sd[i], smn[i], svn[i]
    outs = [loss, grad_x[None]]
    for table in (grads, deltas, new_m, new_v):
        outs += [table[nme].reshape(shapes[nme]) for nme in order]
    return tuple(outs)
```

```python
import functools
import math

import jax
import jax.numpy as jnp
from jax import lax
from jax.experimental import pallas as pl
from jax.experimental.pallas import tpu as pltpu

F32 = jnp.float32
BF16 = jnp.bfloat16
HI = lax.Precision.HIGHEST
MESH = pl.DeviceIdType.MESH

D = 1024
HD = 64
AW = 512
SW = 512
NST = 128
CONVK = 4
CONVC = 1024
BLK = 128
IN_PROJ = 2312
IN_PAD = 2432
IN_SH = IN_PROJ // 4
DFF = 2816
GU_SH = 1408
FF_SPLITS = ((0, 1536), (1536, 2816))
EPS = 1e-6
NEG = -1e30
LR, B1, B2, AEPS, WD, STEP = 0.001, 0.9, 0.999, 1e-08, 0.01, 10
VMEM_LIMIT = 58 * 1024 * 1024


def _cp(*sem):
    return pltpu.CompilerParams(dimension_semantics=sem or None, vmem_limit_bytes=VMEM_LIMIT)


def _dot(a, b):
    return jnp.dot(a, b, preferred_element_type=F32)


def _dot_nt(a, b):
    return lax.dot_general(a, b, (((1,), (1,)), ((), ())), preferred_element_type=F32)


def _dot_tn(a, b):
    return lax.dot_general(a, b, (((0,), (0,)), ((), ())), preferred_element_type=F32)


def _dot_hi(a, b):
    return jnp.dot(a, b, precision=HI, preferred_element_type=F32)


def _sigmoid(x):
    return 1.0 / (1.0 + jnp.exp(-x))


def _iota(shape, dim):
    return lax.broadcasted_iota(jnp.int32, shape, dim)


def _load_resident(hbm_ref, vmem_ref, sem):
    @pl.when(pl.program_id(0) == 0)
    def _():
        cp = pltpu.make_async_copy(hbm_ref, vmem_ref, sem)
        cp.start()
        cp.wait()


def _swap32(t):
    lane = _iota(t.shape, 1)
    return jnp.where((lane & 63) < 32, pltpu.roll(t, 96, 1), pltpu.roll(t, 32, 1))


def _rope_fwd(t, cos, sin_s):
    return t * cos + _swap32(t) * sin_s


def _rope_bwd(t, cos, sin_s):
    return t * cos - _swap32(t) * sin_s


def _rope_tables(pos_row, inv_freq_col, tm):
    T = pos_row.shape[1]
    lane, row = jnp.arange(128)[None, :], jnp.arange(96)[:, None]
    pick = (lane % 32) == (row % 32)
    sel_cos = pick.astype(BF16)
    sel_sin = jnp.where(pick, jnp.where(lane % 64 < 32, -1.0, 1.0), 0.0).astype(BF16)

    def body(p_ref, f_ref, sc_ref, ss_ref, cos_ref, sin_ref):
        ang = f_ref[...] * p_ref[...].astype(F32)
        cos_ref[...] = _dot_tn(_pieces(jnp.cos(ang), 3, 0), sc_ref[...])
        sin_ref[...] = _dot_tn(_pieces(jnp.sin(ang), 3, 0), ss_ref[...])

    full = lambda a: pl.BlockSpec(a.shape, lambda i: (0,) * a.ndim)
    return pl.pallas_call(
        body, name="rope_tables", grid=(T // tm,),
        in_specs=[pl.BlockSpec((1, tm), lambda i: (0, i)), full(inv_freq_col), full(sel_cos), full(sel_sin)],
        out_specs=[pl.BlockSpec((tm, 128), lambda i: (i, 0))] * 2,
        out_shape=[jax.ShapeDtypeStruct((T, 128), F32)] * 2,
        compiler_params=_cp("parallel"),
    )(pos_row, inv_freq_col, sel_cos, sel_sin)


def _in_proj_fwd(x, cos, sin_s, mod6, norm1_w, w_pad, tm):
    T = x.shape[0]

    def body(x_ref, cos_ref, sin_ref, mod_ref, nw_ref, w_hbm, qkv_ref, z_ref, xbc_ref, dt_ref, h_ref, w_vmem, sem):
        _load_resident(w_hbm, w_vmem, sem)
        xv = x_ref[...]
        r = lax.rsqrt(jnp.mean(xv * xv, axis=-1, keepdims=True) + EPS)
        h = (xv * r * nw_ref[...]) * (1.0 + mod_ref[1:2, :]) + mod_ref[0:1, :]
        hb = h.astype(BF16)
        h_ref[...] = hb
        proj = _dot(hb, w_vmem[...])
        cs, sn = cos_ref[...], sin_ref[...]
        for j in range(5):
            qkv_ref[:, 128 * j:128 * (j + 1)] = _rope_fwd(proj[:, 128 * j:128 * (j + 1)], cs, sn).astype(BF16)
        qkv_ref[:, 640:768] = proj[:, 640:768].astype(BF16)
        z_ref[...] = proj[:, 768:1280]
        xbc_ref[...] = proj[:, 1280:2304]
        dt_ref[...] = proj[:, 2304:2432]

    row = lambda w: pl.BlockSpec((tm, w), lambda i: (i, 0))
    full = lambda a: pl.BlockSpec(a.shape, lambda i: (0,) * a.ndim)
    return pl.pallas_call(
        body, name="in_proj_fwd", grid=(T // tm,),
        in_specs=[row(D), row(128), row(128), full(mod6), full(norm1_w), pl.BlockSpec(memory_space=pl.ANY)],
        out_specs=[row(768), row(512), row(1024), row(128), row(D)],
        out_shape=[jax.ShapeDtypeStruct((T, 768), BF16), jax.ShapeDtypeStruct((T, 512), F32),
                   jax.ShapeDtypeStruct((T, 1024), F32), jax.ShapeDtypeStruct((T, 128), F32),
                   jax.ShapeDtypeStruct((T, D), BF16)],
        scratch_shapes=[pltpu.VMEM((D, IN_PAD), BF16), pltpu.SemaphoreType.DMA],
        compiler_params=_cp("arbitrary"),
    )(x, cos, sin_s, mod6, norm1_w, w_pad)


def _head_variants(pair, j):
    lane = _iota(pair.shape, 1)
    lo = lane < 64
    kv = j // 2
    ev = jnp.where(lo, pair, 0.0)
    od = jnp.where(lo, 0.0, pair)
    if kv == 0:
        od = pltpu.roll(od, 64, 1)
    else:
        ev = pltpu.roll(ev, 64, 1)
    return ev.astype(BF16), od.astype(BF16)


def _kv_variants(vcat):
    lane = _iota(vcat.shape, 1)
    lo = lane < 64
    v0 = jnp.where(lo, vcat, 0.0)
    v1 = jnp.where(lo, 0.0, vcat)
    out = {
        (0, 0): v0, (0, 1): pltpu.roll(v0, 64, 1),
        (1, 0): pltpu.roll(v1, 64, 1), (1, 1): v1,
    }
    return {k: v.astype(BF16) for k, v in out.items()}


def _fold_masks(n):
    upper = _iota((BLK, BLK), 1) > _iota((BLK, BLK), 0)
    return upper, upper & (n == 0)


def _attn_fwd(qkv, sinks):
    T = qkv.shape[0]
    nb = T // BLK

    def body(sink_ref, q_ref, kc_ref, kp_ref, vc_ref, vp_ref, o_ref, lse_ref):
        n = pl.program_id(0)
        vpv = _kv_variants(vp_ref[...].astype(F32))
        vcv = _kv_variants(vc_ref[...].astype(F32))
        q_all = jnp.concatenate(
            [v for j in range(4) for v in _head_variants(q_ref[:, 128 * j:128 * (j + 1)].astype(F32), j)], axis=0)
        s_prev = _dot_nt(q_all, kp_ref[...])
        s_cur = _dot_nt(q_all, kc_ref[...])
        upper, dead = _fold_masks(n)
        lane = _iota((BLK, 128), 1)
        lse_acc = jnp.zeros((BLK, 128), F32)
        for jj in range(4):
            acc = jnp.zeros((BLK, 128), F32)
            for par in range(2):
                h = 2 * jj + par
                rows = slice(h * BLK, (h + 1) * BLK)
                sink = sink_ref[0, h]
                s = jnp.where(dead, NEG, jnp.where(upper, s_prev[rows], s_cur[rows]) * 0.125)
                m = jnp.maximum(jnp.max(s, axis=1, keepdims=True), sink)
                p = jnp.exp(s - m)
                den = jnp.sum(p, axis=1, keepdims=True) + jnp.exp(sink - m)
                pn = p * (1.0 / den)
                acc = (acc + _dot(jnp.where(upper, pn, 0.0).astype(BF16), vpv[(jj // 2, par)])
                       + _dot(jnp.where(upper, 0.0, pn).astype(BF16), vcv[(jj // 2, par)]))
                lse_acc = jnp.where(lane == h, m + jnp.log(den), lse_acc)
            o_ref[:, 128 * jj:128 * (jj + 1)] = acc.astype(BF16)
        lse_ref[...] = lse_acc

    prev = lambda n: jnp.maximum(n - 1, 0)
    return pl.pallas_call(
        body, name="attn_fwd", grid=(nb,),
        in_specs=[pl.BlockSpec(memory_space=pltpu.SMEM),
                  pl.BlockSpec((BLK, 512), lambda n: (n, 0)),
                  pl.BlockSpec((BLK, 128), lambda n: (n, 4)),
                  pl.BlockSpec((BLK, 128), lambda n: (prev(n), 4)),
                  pl.BlockSpec((BLK, 128), lambda n: (n, 5)),
                  pl.BlockSpec((BLK, 128), lambda n: (prev(n), 5))],
        out_specs=[pl.BlockSpec((BLK, 512), lambda n: (n, 0)), pl.BlockSpec((BLK, 128), lambda n: (n, 0))],
        out_shape=[jax.ShapeDtypeStruct((T, 512), BF16), jax.ShapeDtypeStruct((T, 128), F32)],
        compiler_params=_cp("parallel"),
    )(sinks, qkv, qkv, qkv, qkv, qkv)


def _attn_bwd(qkv, sinks, lse, dmix, cos, sin_s):
    T = qkv.shape[0]
    nb = T // BLK

    def body(sink_ref, q_ref, kc_ref, kp_ref, vc_ref, vp_ref, lse_ref, do_ref, cq_ref, sq_ref, ck_ref, sk_ref,
             out_ref, ds_ref, dq_car, dk_car, dv_car):
        n = pl.program_id(0)
        lane = _iota((BLK, 128), 1)

        @pl.when(n == 0)
        def _():
            ds_ref[...] = jnp.zeros_like(ds_ref)
            dq_car[...] = jnp.zeros_like(dq_car)
            dk_car[...] = jnp.zeros_like(dk_car)
            dv_car[...] = jnp.zeros_like(dv_car)

        @pl.when(n < nb)
        def _():
            kp, kc, vp, vc = kp_ref[...], kc_ref[...], vp_ref[...], vc_ref[...]
            kpv = _kv_variants(kp.astype(F32))
            kcv = _kv_variants(kc.astype(F32))
            lse_v = lse_ref[...]
            q_all = jnp.concatenate(
                [v for j in range(4) for v in _head_variants(q_ref[:, 128 * j:128 * (j + 1)].astype(F32), j)], axis=0)
            do_all = jnp.concatenate(
                [v for j in range(4) for v in _head_variants(do_ref[:, 128 * j:128 * (j + 1)], j)], axis=0)
            s_prev, s_cur = _dot_nt(q_all, kp), _dot_nt(q_all, kc)
            dp_prev, dp_cur = _dot_nt(do_all, vp), _dot_nt(do_all, vc)
            upper, dead = _fold_masks(n)
            out_ref[:, 0:512] = dq_car[...]
            dsk = jnp.zeros((1, 128), F32)
            ds_u, ds_l, p_u, p_l = [], [], [], []
            for jj in range(4):
                dq_acc = jnp.zeros((BLK, 128), F32)
                for par in range(2):
                    h = 2 * jj + par
                    rows = slice(h * BLK, (h + 1) * BLK)
                    lse_h = jnp.sum(jnp.where(lane == h, lse_v, 0.0), axis=1, keepdims=True)
                    s = jnp.where(dead, NEG, jnp.where(upper, s_prev[rows], s_cur[rows]) * 0.125)
                    p = jnp.exp(s - lse_h)
                    dp = jnp.where(upper, dp_prev[rows], dp_cur[rows])
                    delta = jnp.sum(p * dp, axis=1, keepdims=True)
                    ds = p * (dp - delta) * 0.125
                    dsu, dsl = jnp.where(upper, ds, 0.0).astype(BF16), jnp.where(upper, 0.0, ds).astype(BF16)
                    dq_acc = dq_acc + _dot(dsu, kpv[(jj // 2, par)]) + _dot(dsl, kcv[(jj // 2, par)])
                    ds_u.append(dsu)
                    ds_l.append(dsl)
                    p_u.append(jnp.where(upper, p, 0.0).astype(BF16))
                    p_l.append(jnp.where(upper, 0.0, p).astype(BF16))
                    dsk = dsk + jnp.where(lane[0:1] == h, -jnp.sum(jnp.exp(sink_ref[0, h] - lse_h) * delta), 0.0)
                dq_car[:, 128 * jj:128 * (jj + 1)] = _rope_bwd(dq_acc, cq_ref[...], sq_ref[...]).astype(BF16)
            stack = lambda parts: jnp.concatenate(parts, axis=0)
            dk_prev, dk_cur = _dot_tn(stack(ds_u), q_all), _dot_tn(stack(ds_l), q_all)
            dv_prev, dv_cur = _dot_tn(stack(p_u), do_all), _dot_tn(stack(p_l), do_all)
            ds_ref[...] += dsk
            out_ref[:, 512:640] = _rope_bwd(dk_car[...] + dk_prev, ck_ref[...], sk_ref[...]).astype(BF16)
            out_ref[:, 640:768] = (dv_car[...] + dv_prev).astype(BF16)
            dk_car[...] = dk_cur
            dv_car[...] = dv_cur

        @pl.when(n == nb)
        def _():
            out_ref[:, 0:512] = dq_car[...]
            out_ref[:, 512:640] = _rope_bwd(dk_car[...], ck_ref[...], sk_ref[...]).astype(BF16)
            out_ref[:, 640:768] = dv_car[...].astype(BF16)

    cur = lambda n: jnp.minimum(n, nb - 1)
    prev = lambda n: jnp.maximum(cur(n) - 1, 0)
    outb = lambda n: jnp.maximum(n - 1, 0)
    return pl.pallas_call(
        body, name="attn_bwd", grid=(nb + 1,),
        in_specs=[pl.BlockSpec(memory_space=pltpu.SMEM),
                  pl.BlockSpec((BLK, 512), lambda n: (cur(n), 0)),
                  pl.BlockSpec((BLK, 128), lambda n: (cur(n), 4)),
                  pl.BlockSpec((BLK, 128), lambda n: (prev(n), 4)),
                  pl.BlockSpec((BLK, 128), lambda n: (cur(n), 5)),
                  pl.BlockSpec((BLK, 128), lambda n: (prev(n), 5)),
                  pl.BlockSpec((BLK, 128), lambda n: (cur(n), 0)),
                  pl.BlockSpec((BLK, 512), lambda n: (cur(n), 0)),
                  pl.BlockSpec((BLK, 128), lambda n: (cur(n), 0)),
                  pl.BlockSpec((BLK, 128), lambda n: (cur(n), 0)),
                  pl.BlockSpec((BLK, 128), lambda n: (outb(n), 0)),
                  pl.BlockSpec((BLK, 128), lambda n: (outb(n), 0))],
        out_specs=[pl.BlockSpec((BLK, 768), lambda n: (outb(n), 0)), pl.BlockSpec((1, 128), lambda n: (0, 0))],
        out_shape=[jax.ShapeDtypeStruct((T, 768), BF16), jax.ShapeDtypeStruct((1, 128), F32)],
        scratch_shapes=[pltpu.VMEM((BLK, 512), BF16), pltpu.VMEM((BLK, 128), F32), pltpu.VMEM((BLK, 128), F32)],
        compiler_params=_cp("arbitrary"),
    )(sinks, qkv, qkv, qkv, qkv, qkv, lse, dmix, cos, sin_s, cos, sin_s)


def _ssd_mats():
    e = jnp.arange(SW)[None, :] // HD == jnp.arange(128)[:, None]
    tri = jnp.arange(BLK)[None, :] <= jnp.arange(BLK)[:, None]
    return (jnp.tile(e, (3, 1)).astype(BF16), jnp.tile(e.T, (2, 1)).astype(BF16),
            jnp.tile(tri, (1, 3)).astype(BF16), jnp.tile(tri.T, (1, 3)).astype(BF16))


def _pieces(x, n, axis):
    out, r = [], x
    for i in range(n):
        p = r.astype(BF16)
        out.append(p)
        if i + 1 < n:
            r = r - p.astype(F32)
    return jnp.concatenate(out, axis=axis)


def _expand(x, e3):
    return _dot(_pieces(x, 3, 1), e3)


def _head_sums(x, et2):
    return _dot(_pieces(x, 2, 1), et2)


def _run_sum(tri3, x):
    return _dot(tri3, _pieces(x, 3, 0))


def _shift_down(u, tail, j):
    rolled = pltpu.roll(u, j, 0)
    first = jnp.where(_iota(tail.shape, 0) < j, pltpu.roll(tail, j, 0), rolled[0:8])
    return jnp.concatenate([first, rolled[8:]], axis=0)


def _shift_up(d, head, j):
    rolled = pltpu.roll(d, BLK - j, 0)
    last = jnp.where(_iota(head.shape, 0) >= 8 - j, pltpu.roll(head, 8 - j, 0), rolled[BLK - 8:])
    return jnp.concatenate([rolled[:BLK - 8], last], axis=0)


def _ssd_parts(dtr, dtb, alog, e3, tril3):
    xx = dtr + dtb
    dt = jnp.maximum(xx, 0.0) + jnp.log(1.0 + jnp.exp(-jnp.abs(xx)))
    a_neg = -jnp.exp(alog)
    tril = _iota((BLK, BLK), 1) <= _iota((BLK, BLK), 0)
    cs = _run_sum(tril3, dt * a_neg)
    csx = _expand(cs, e3)
    last = csx[BLK - 1:BLK, :]
    return dict(xx=xx, dt=dt, a_neg=a_neg, tril=tril, cs=cs, cs_t=cs.T,
                ecsx=jnp.exp(csx), dtex=jnp.exp(last - csx), cdx=jnp.exp(last), dtx=_expand(dt, e3))


def _decay(parts, h):
    seg = parts["cs"][:, h:h + 1] - parts["cs_t"][h:h + 1, :]
    return jnp.exp(jnp.where(parts["tril"], seg, NEG))


def _group_cols(a, g):
    return a[:, 256 * g:256 * (g + 1)]


def _ssd_fwd(xbc, z, dtr, conv_w, conv_b, dtb, alog, dskx, ssm_w, mats):
    T = xbc.shape[0]
    nc = T // BLK

    def body(u_ref, tail_ref, z_ref, dtr_ref, cw_ref, cb_ref, dtb_ref, al_ref, dk_ref, sw_ref, e3_ref, tril3_ref,
             yn_ref, yp_ref, st_ref, co_ref, s_scr):
        n = pl.program_id(0)

        @pl.when(n == 0)
        def _():
            s_scr[...] = jnp.zeros_like(s_scr)

        u = u_ref[...]
        tail = jnp.where(n > 0, tail_ref[...], 0.0)
        co = cb_ref[...] + cw_ref[3:4, :] * u
        for j in range(1, CONVK):
            co = co + cw_ref[3 - j:4 - j, :] * _shift_down(u, tail, j)
        co_ref[...] = co
        xc = co * _sigmoid(co)
        pt = _ssd_parts(dtr_ref[...], dtb_ref[...], al_ref[...], e3_ref[...], tril3_ref[...])
        xs = xc[:, :SW]
        bm = [xc[:, 512:640].astype(BF16), xc[:, 640:768].astype(BF16)]
        cm = [xc[:, 768:896].astype(BF16), xc[:, 896:1024].astype(BF16)]
        s_in = s_scr[...]
        st_ref[0] = s_in
        xdt = xs * pt["dtx"]
        xde = (xdt * pt["dtex"]).astype(BF16)
        lane = _iota((BLK, 128), 1)
        lo = lane < 64
        ys, s_new = [], []
        for g in range(2):
            cb = _dot_nt(cm[g], bm[g])
            yoff = _dot(cm[g], _group_cols(s_in, g).astype(BF16))
            s_new.append(_dot_tn(bm[g], _group_cols(xde, g)))
            for jj in range(2):
                j = 2 * g + jj
                chunk = xdt[:, 128 * j:128 * (j + 1)]
                g_ev = (cb * _decay(pt, 2 * j)).astype(BF16)
                g_od = (cb * _decay(pt, 2 * j + 1)).astype(BF16)
                yd = _dot(g_ev, jnp.where(lo, chunk, 0.0).astype(BF16)) + _dot(g_od, jnp.where(lo, 0.0, chunk).astype(BF16))
                ys.append(yd + yoff[:, 128 * jj:128 * (jj + 1)] * pt["ecsx"][:, 128 * j:128 * (j + 1)])
        y = jnp.concatenate(ys, axis=1) + xs * dk_ref[...]
        s_scr[...] = s_in * pt["cdx"] + jnp.concatenate(s_new, axis=1)
        yp_ref[...] = y
        zv = z_ref[...]
        yz = y * (zv * _sigmoid(zv))
        outs = []
        for g in range(2):
            yg = _group_cols(yz, g)
            outs.append(yg * lax.rsqrt(jnp.mean(yg * yg, axis=-1, keepdims=True) + EPS))
        yn_ref[...] = (jnp.concatenate(outs, axis=1) * sw_ref[...]).astype(BF16)

    e3, _, tril3, _ = mats
    tail8 = lambda n: jnp.maximum(n * (BLK // 8) - 1, 0)
    full = lambda a: pl.BlockSpec(a.shape, lambda n: (0,) * a.ndim)
    return pl.pallas_call(
        body, name="ssd_fwd", grid=(nc,),
        in_specs=[pl.BlockSpec((BLK, CONVC), lambda n: (n, 0)), pl.BlockSpec((8, CONVC), lambda n: (tail8(n), 0)),
                  pl.BlockSpec((BLK, SW), lambda n: (n, 0)), pl.BlockSpec((BLK, 128), lambda n: (n, 0)),
                  full(conv_w), full(conv_b), full(dtb), full(alog), full(dskx), full(ssm_w), full(e3), full(tril3)],
        out_specs=[pl.BlockSpec((BLK, SW), lambda n: (n, 0)), pl.BlockSpec((BLK, SW), lambda n: (n, 0)),
                   pl.BlockSpec((1, NST, SW), lambda n: (n, 0, 0)), pl.BlockSpec((BLK, CONVC), lambda n: (n, 0))],
        out_shape=[jax.ShapeDtypeStruct((T, SW), BF16), jax.ShapeDtypeStruct((T, SW), F32),
                   jax.ShapeDtypeStruct((nc, NST, SW), F32), jax.ShapeDtypeStruct((T, CONVC), F32)],
        scratch_shapes=[pltpu.VMEM((NST, SW), F32)],
        compiler_params=_cp("arbitrary"),
    )(xbc, xbc, z, dtr, conv_w, conv_b, dtb, alog, dskx, ssm_w, e3, tril3)


def _ssd_bwd(xbc, co_all, z, dtr, ypre, states, dmix, conv_w, dtb, alog, dskx, ssm_w, mats):
    T = xbc.shape[0]
    nc = T // BLK

    def body(u_ref, co_ref, z_ref, dtr_ref, yp_ref, st_ref, dyn_ref, cw_ref, dtb_ref, al_ref, dk_ref, sw_ref,
             e3_ref, et2_ref, tril3_ref, triu3_ref,
             out_ref, dcw_ref, dcb_ref, dsw_ref, dsk_ref, ddtb_ref, dav_ref, ds_scr, dco_scr, dskx_scr):
        i = pl.program_id(0)

        @pl.when(i == 0)
        def _():
            for r in (dcw_ref, dcb_ref, dsw_ref, dsk_ref, ddtb_ref, dav_ref, ds_scr, dco_scr, dskx_scr):
                r[...] = jnp.zeros_like(r)

        co = co_ref[...]
        sg = _sigmoid(co)
        xc = co * sg
        pt = _ssd_parts(dtr_ref[...], dtb_ref[...], al_ref[...], e3_ref[...], tril3_ref[...])
        dtx, ecsx, dtex, cdx = pt["dtx"], pt["ecsx"], pt["dtex"], pt["cdx"]
        xs = xc[:, :SW]
        bm = [xc[:, 512:640].astype(BF16), xc[:, 640:768].astype(BF16)]
        cm = [xc[:, 768:896].astype(BF16), xc[:, 896:1024].astype(BF16)]
        s_in = st_ref[0]
        ds_out = ds_scr[...]
        e_t = et2_ref[...]

        zv = z_ref[...]
        sz = _sigmoid(zv)
        silu_z = zv * sz
        ypre = yp_ref[...]
        yz = ypre * silu_z
        dyn = dyn_ref[...]
        sw = sw_ref[...]
        dyz, yns = [], []
        for g in range(2):
            yg = _group_cols(yz, g)
            r = lax.rsqrt(jnp.mean(yg * yg, axis=-1, keepdims=True) + EPS)
            yn = yg * r
            dg = _group_cols(dyn, g) * _group_cols(sw, g)
            dyz.append(r * (dg - yn * jnp.mean(dg * yn, axis=-1, keepdims=True)))
            yns.append(yn)
        dyz = jnp.concatenate(dyz, axis=1)
        dsw_ref[...] += jnp.sum(dyn * jnp.concatenate(yns, axis=1), axis=0, keepdims=True)
        dy = dyz * silu_z
        dz = dyz * ypre * (sz * (1.0 + zv * (1.0 - sz)))

        xdt = xs * dtx
        xdt_b = xdt.astype(BF16)
        edy = (ecsx * dy).astype(BF16)
        xde = (xdt * dtex).astype(BF16)
        lane = _iota((BLK, 128), 1)
        lo = lane < 64
        row8 = _iota((8, 128), 0)
        dcs = jnp.zeros((BLK, 128), F32)
        col_rows = jnp.zeros((8, 128), F32)
        dxdt, bds, yoff, dbs, dcs_g, ds_new = [], [], [], [], [], []
        for g in range(2):
            s_g = _group_cols(s_in, g).astype(BF16)
            dso_g = _group_cols(ds_out, g).astype(BF16)
            cb = _dot_nt(cm[g], bm[g])
            bds.append(_dot(bm[g], dso_g))
            yoff.append(_dot(cm[g], s_g))
            dcb_g = jnp.zeros((BLK, BLK), F32)
            for jj in range(2):
                j = 2 * g + jj
                dy_c = dy[:, 128 * j:128 * (j + 1)]
                xdt_c = xdt_b[:, 128 * j:128 * (j + 1)]
                acc = jnp.zeros((BLK, 128), F32)
                for par in range(2):
                    h = 2 * j + par
                    lm = _decay(pt, h)
                    gm = cb * lm
                    dy_m = (jnp.where(lo, dy_c, 0.0) if par == 0 else jnp.where(lo, 0.0, dy_c)).astype(BF16)
                    dg_h = _dot_nt(dy_m, xdt_c)
                    w_h = dg_h * gm
                    dcs = dcs + jnp.where(lane == h, jnp.sum(w_h, axis=1, keepdims=True), 0.0)
                    col_rows = col_rows + jnp.where(row8 == h, jnp.sum(w_h, axis=0, keepdims=True), 0.0)
                    dcb_g = dcb_g + dg_h * lm
                    acc = acc + _dot_tn(gm.astype(BF16), dy_m)
                dxdt.append(acc)
            dcb_b = dcb_g.astype(BF16)
            dcs_g.append(_dot(dcb_b, bm[g]) + _dot_nt(_group_cols(edy, g), s_g))
            dbs.append(_dot_tn(dcb_b, cm[g]) + _dot_nt(_group_cols(xde, g), dso_g))
            ds_new.append(_dot_tn(cm[g], _group_cols(edy, g)))
        bds = jnp.concatenate(bds, axis=1)
        yoff = jnp.concatenate(yoff, axis=1) * ecsx
        dxdt = jnp.concatenate(dxdt, axis=1) + dtex * bds
        ds_scr[...] = cdx * ds_out + jnp.concatenate(ds_new, axis=1)

        t_m = _head_sums(dtex * xdt * bds, e_t)
        colsum_t = jnp.concatenate([col_rows, jnp.zeros((BLK - 8, 128), F32)], axis=0).T
        cd = jnp.exp(pt["cs"][BLK - 1:BLK, :])
        sds = jnp.sum(s_in * ds_out, axis=0, keepdims=True)
        last_row = jnp.sum(t_m, axis=0, keepdims=True) + cd * _head_sums(jnp.broadcast_to(sds, (8, SW)), e_t)[0:1]
        dcs = dcs - colsum_t + _head_sums(dy * yoff, e_t) - t_m
        dcs = dcs + jnp.where(_iota((BLK, 128), 0) == BLK - 1, last_row, 0.0)
        da = _run_sum(triu3_ref[...], dcs)
        dt = pt["dt"]
        ddt = da * pt["a_neg"] + _head_sums(dxdt * xs, e_t)
        dav_ref[...] += jnp.sum(da * dt, axis=0, keepdims=True)
        ddtr = ddt * _sigmoid(pt["xx"])
        ddtb_ref[...] += jnp.sum(ddtr, axis=0, keepdims=True)
        dxs = dxdt * dtx + dy * dk_ref[...]
        dskx_scr[...] += jnp.sum(dy * xs, axis=0, keepdims=True)
        dxc = jnp.concatenate([dxs, dbs[0], dbs[1], dcs_g[0], dcs_g[1]], axis=1)
        dco = dxc * (sg * (1.0 + co * (1.0 - sg)))

        dcb_ref[...] += jnp.sum(dco, axis=0, keepdims=True)
        u = u_ref[...]
        head = dco_scr[...]
        du = jnp.zeros_like(dco)
        for j in range(CONVK):
            up_j = dco if j == 0 else _shift_up(dco, head, j)
            dcw_ref[3 - j:4 - j, :] += jnp.sum(up_j * u, axis=0, keepdims=True)
            du = du + cw_ref[3 - j:4 - j, :] * up_j
        dco_scr[...] = dco[0:8]
        out_ref[:, 0:512] = dz.astype(BF16)
        out_ref[:, 512:1536] = du.astype(BF16)
        out_ref[:, 1536:1664] = ddtr.astype(BF16)

        @pl.when(i == nc - 1)
        def _():
            dsk_ref[...] = _head_sums(jnp.broadcast_to(dskx_scr[...], (8, SW)), e_t)[0:1]

    e3, et2, tril3, triu3 = mats
    rev = lambda i: nc - 1 - i
    full = lambda a: pl.BlockSpec(a.shape, lambda i: (0,) * a.ndim)
    acc = lambda r, c: pl.BlockSpec((r, c), lambda i: (0, 0))
    return pl.pallas_call(
        body, name="ssd_bwd", grid=(nc,),
        in_specs=[pl.BlockSpec((BLK, CONVC), lambda i: (rev(i), 0)), pl.BlockSpec((BLK, CONVC), lambda i: (rev(i), 0)),
                  pl.BlockSpec((BLK, SW), lambda i: (rev(i), 0)), pl.BlockSpec((BLK, 128), lambda i: (rev(i), 0)),
                  pl.BlockSpec((BLK, SW), lambda i: (rev(i), 0)), pl.BlockSpec((1, NST, SW), lambda i: (rev(i), 0, 0)),
                  pl.BlockSpec((BLK, SW), lambda i: (rev(i), 1)),
                  full(conv_w), full(dtb), full(alog), full(dskx), full(ssm_w),
                  full(e3), full(et2), full(tril3), full(triu3)],
        out_specs=[pl.BlockSpec((BLK, 1664), lambda i: (rev(i), 0)),
                   acc(CONVK, CONVC), acc(1, CONVC), acc(1, SW), acc(1, 128), acc(1, 128), acc(1, 128)],
        out_shape=[jax.ShapeDtypeStruct((T, 1664), BF16),
                   jax.ShapeDtypeStruct((CONVK, CONVC), F32), jax.ShapeDtypeStruct((1, CONVC), F32),
                   jax.ShapeDtypeStruct((1, SW), F32), jax.ShapeDtypeStruct((1, 128), F32),
                   jax.ShapeDtypeStruct((1, 128), F32), jax.ShapeDtypeStruct((1, 128), F32)],
        scratch_shapes=[pltpu.VMEM((NST, SW), F32), pltpu.VMEM((8, CONVC), F32), pltpu.VMEM((1, SW), F32)],
        compiler_params=_cp("arbitrary"),
    )(xbc, co_all, z, dtr, ypre, states, dmix, conv_w, dtb, alog, dskx, ssm_w, e3, et2, tril3, triu3)


def _mix_ffn(x, attn, ynorm, tgt, mod6, norm2_w, final_w, w_out, w_gu, w_gu_own, s_arr, w_dn, tm):
    T = x.shape[0]
    nt = T // tm

    def body(x_ref, a_ref, y_ref, t_ref, mod_ref, n2_ref, fw_ref, wo_hbm, wgu_hbm, own_hbm, s_ref, wdn_hbm,
             sq_ref, dmix_ref, dx1_ref, h2_ref, act_ref, df_ref, dgu_ref, do_ref, sm_ref,
             wo, wgu, wdn, sems):
        i = pl.program_id(0)

        @pl.when(i == 0)
        def _():
            cps = [pltpu.make_async_copy(s, d, sems.at[k]) for k, (s, d) in
                   enumerate(((wo_hbm, wo), (wgu_hbm, wgu), (wdn_hbm, wdn)))]
            for c in cps:
                c.start()
            for c in cps:
                c.wait()
            own = pltpu.make_async_copy(
                own_hbm, wgu.at[:, pl.ds(pl.multiple_of(s_ref[0] * GU_SH, 128), GU_SH)], sems.at[3])
            own.start()
            own.wait()
            sq_ref[...] = jnp.zeros_like(sq_ref)
            sm_ref[...] = jnp.zeros_like(sm_ref)

        gate1, shift2, scale2, gate2 = mod_ref[2:3, :], mod_ref[3:4, :], mod_ref[4:5, :], mod_ref[5:6, :]
        n2w, fw = n2_ref[...], fw_ref[...]
        o = _dot(a_ref[...], wo[0:AW, :]) + _dot(y_ref[...], wo[AW:D, :])
        x1 = x_ref[...] + gate1 * o
        r2 = lax.rsqrt(jnp.mean(x1 * x1, axis=-1, keepdims=True) + EPS)
        xh2 = x1 * r2
        n2 = xh2 * n2w
        h2b = (n2 * (1.0 + scale2) + shift2).astype(BF16)
        h2_ref[...] = h2b
        f = jnp.zeros((tm, D), F32)
        saved = []
        for a, b in FF_SPLITS:
            gp = _dot(h2b, wgu[:, a:b])
            upj = _dot(h2b, wgu[:, DFF + a:DFF + b])
            sg = _sigmoid(gp)
            sl = gp * sg
            actb = (sl * upj).astype(BF16)
            act_ref[:, a:b] = actb
            f = f + _dot(actb, wdn[a:b, :])
            saved.append((gp, upj, sg, sl))
        x2 = x1 + gate2 * f
        r3 = lax.rsqrt(jnp.mean(x2 * x2, axis=-1, keepdims=True) + EPS)
        xh3 = x2 * r3
        err = xh3 * fw - t_ref[...]
        sq_ref[...] += jnp.sum(err * err, axis=0, keepdims=True)
        dy = err * (1.0 / D)
        dfw = jnp.sum(dy * xh3, axis=0, keepdims=True)
        dxh3 = dy * fw
        dx2 = r3 * (dxh3 - xh3 * jnp.mean(dxh3 * xh3, axis=-1, keepdims=True))
        dgate2 = jnp.sum(dx2 * f, axis=0, keepdims=True)
        dfb = (dx2 * gate2).astype(BF16)
        df_ref[...] = dfb
        dh2 = jnp.zeros((tm, D), F32)
        for (a, b), (gp, upj, sg, sl) in zip(FF_SPLITS, saved):
            dact = _dot_nt(dfb, wdn[a:b, :])
            dg = (dact * upj * (sg * (1.0 + gp * (1.0 - sg)))).astype(BF16)
            du = (dact * sl).astype(BF16)
            dgu_ref[:, a:b] = dg
            dgu_ref[:, DFF + a:DFF + b] = du
            dh2 = dh2 + _dot_nt(dg, wgu[:, a:b]) + _dot_nt(du, wgu[:, DFF + a:DFF + b])
        dshift2 = jnp.sum(dh2, axis=0, keepdims=True)
        dscale2 = jnp.sum(dh2 * n2, axis=0, keepdims=True)
        dn2 = dh2 * (1.0 + scale2)
        dn2w = jnp.sum(dn2 * xh2, axis=0, keepdims=True)
        dxh2 = dn2 * n2w
        dx1 = dx2 + r2 * (dxh2 - xh2 * jnp.mean(dxh2 * xh2, axis=-1, keepdims=True))
        dx1_ref[...] = dx1
        dgate1 = jnp.sum(dx1 * o, axis=0, keepdims=True)
        dob = (dx1 * gate1).astype(BF16)
        do_ref[...] = dob
        dmix_ref[...] = _dot_nt(dob, wo[...])
        sm_ref[...] += jnp.concatenate(
            [dfw, dn2w, dshift2, dscale2, dgate2, dgate1, jnp.zeros((2, D), F32)], axis=0)

    row = lambda w: pl.BlockSpec((tm, w), lambda i: (i, 0))
    full = lambda a: pl.BlockSpec(a.shape, lambda i: (0,) * a.ndim)
    anyspec = pl.BlockSpec(memory_space=pl.ANY)
    return pl.pallas_call(
        body, name="mix_ffn", grid=(nt,),
        in_specs=[row(D), row(AW), row(SW), row(D), full(mod6), full(norm2_w), full(final_w), anyspec, anyspec, anyspec,
                  pl.BlockSpec(memory_space=pltpu.SMEM), anyspec],
        out_specs=[pl.BlockSpec((1, D), lambda i: (0, 0)), row(D), row(D), row(D),
                   row(DFF), row(D), row(2 * DFF), row(D), pl.BlockSpec((8, D), lambda i: (0, 0))],
        out_shape=[jax.ShapeDtypeStruct((1, D), F32), jax.ShapeDtypeStruct((T, D), F32), jax.ShapeDtypeStruct((T, D), F32),
                   jax.ShapeDtypeStruct((T, D), BF16), jax.ShapeDtypeStruct((T, DFF), BF16),
                   jax.ShapeDtypeStruct((T, D), BF16), jax.ShapeDtypeStruct((T, 2 * DFF), BF16),
                   jax.ShapeDtypeStruct((T, D), BF16), jax.ShapeDtypeStruct((8, D), F32)],
        scratch_shapes=[pltpu.VMEM((D, D), BF16), pltpu.VMEM((D, 2 * DFF), BF16), pltpu.VMEM((DFF, D), BF16),
                        pltpu.SemaphoreType.DMA((4,))],
        compiler_params=_cp("arbitrary"),
    )(x, attn, ynorm, tgt, mod6, norm2_w, final_w, w_out, w_gu, w_gu_own, s_arr, w_dn)


def _in_proj_bwd(x, dx1, dqkv, dzxd, mod6, norm1_w, w_pad, tm):
    T = x.shape[0]

    def body(x_ref, dx1_ref, dq_ref, dz_ref, mod_ref, nw_ref, w_hbm, gx_ref, sm_ref, w_vmem, sem):
        _load_resident(w_hbm, w_vmem, sem)

        @pl.when(pl.program_id(0) == 0)
        def _():
            sm_ref[...] = jnp.zeros_like(sm_ref)

        nw = nw_ref[...]
        scale1 = mod_ref[1:2, :]
        sums = jnp.zeros((8, D), F32)
        for rows in (slice(0, tm // 2), slice(tm // 2, tm)):
            dh = _dot_nt(dq_ref[rows, :], w_vmem[:, 0:768]) + _dot_nt(dz_ref[rows, :], w_vmem[:, 768:IN_PAD])
            xv = x_ref[rows, :]
            r = lax.rsqrt(jnp.mean(xv * xv, axis=-1, keepdims=True) + EPS)
            xh = xv * r
            n1 = xh * nw
            dshift = jnp.sum(dh, axis=0, keepdims=True)
            dscale = jnp.sum(dh * n1, axis=0, keepdims=True)
            dn = dh * (1.0 + scale1)
            dnw = jnp.sum(dn * xh, axis=0, keepdims=True)
            dxh = dn * nw
            gx_ref[rows, :] = dx1_ref[rows, :] + r * (dxh - xh * jnp.mean(dxh * xh, axis=-1, keepdims=True))
            sums = sums + jnp.concatenate([dnw, dshift, dscale, jnp.zeros((5, D), F32)], axis=0)
        sm_ref[...] += sums

    row = lambda w: pl.BlockSpec((tm, w), lambda i: (i, 0))
    full = lambda a: pl.BlockSpec(a.shape, lambda i: (0,) * a.ndim)
    return pl.pallas_call(
        body, name="in_proj_bwd", grid=(T // tm,),
        in_specs=[row(D), row(D), row(768), row(1664), full(mod6), full(norm1_w), pl.BlockSpec(memory_space=pl.ANY)],
        out_specs=[row(D), pl.BlockSpec((8, D), lambda i: (0, 0))],
        out_shape=[jax.ShapeDtypeStruct((T, D), F32), jax.ShapeDtypeStruct((8, D), F32)],
        scratch_shapes=[pltpu.VMEM((D, IN_PAD), BF16), pltpu.SemaphoreType.DMA],
        compiler_params=_cp("arbitrary"),
    )(x, dx1, dqkv, dzxd, mod6, norm1_w, w_pad)


def _tn_matmul(a, b, K, N, tt, name, dep):
    T = a.shape[0]
    ja, jb = a.shape[1] // K, b.shape[1] // N
    J = max(ja, jb)

    def body(a_ref, b_ref, dep_ref, o_ref):
        t = pl.program_id(1)
        prod = _dot_tn(a_ref[...], b_ref[...])

        @pl.when(t == 0)
        def _():
            o_ref[0] = prod

        @pl.when(t > 0)
        def _():
            o_ref[0] += prod

    return pl.pallas_call(
        body, name=name, grid=(J, T // tt),
        in_specs=[pl.BlockSpec((tt, K), lambda j, t: (t, j if ja > 1 else 0)),
                  pl.BlockSpec((tt, N), lambda j, t: (t, j if jb > 1 else 0)),
                  pl.BlockSpec((8, 128), lambda j, t: (0, 0))],
        out_specs=pl.BlockSpec((1, K, N), lambda j, t: (j, 0, 0)),
        out_shape=jax.ShapeDtypeStruct((J, K, N), F32),
        compiler_params=_cp("parallel", "arbitrary"),
    )(a, b, dep)


def _adam_math(w, g, m, v):
    m = B1 * m + (1.0 - B1) * g
    v = B2 * v + (1.0 - B2) * (g * g)
    m_hat = m / (1.0 - B1 ** STEP)
    v_hat = v / (1.0 - B2 ** STEP)
    delta = -LR * (m_hat / (jnp.sqrt(v_hat) + AEPS) + WD * w)
    return delta, m, v


def _adam_2d(w, mine, land, m, v, c_arr, rb, name):
    R, C = w.shape
    nbh = R // 2 // rb

    def body(c_ref, w_ref, mine_ref, land_ref, m_ref, v_ref, go_ref, d_ref, mo_ref, vo_ref):
        g = jnp.where(pl.program_id(0) // nbh == c_ref[0], mine_ref[...], land_ref[...])
        d, mn, vn = _adam_math(w_ref[...], g, m_ref[...], v_ref[...])
        go_ref[...] = g
        d_ref[...] = d
        mo_ref[...] = mn
        vo_ref[...] = vn

    spec = pl.BlockSpec((rb, C), lambda i, c_ref: (i, 0))
    mine_spec = pl.BlockSpec((rb, C), lambda i, c_ref: (jnp.clip(i - c_ref[0] * nbh, 0, nbh - 1), 0))
    return pl.pallas_call(
        body, name=name,
        grid_spec=pltpu.PrefetchScalarGridSpec(
            num_scalar_prefetch=1, grid=(R // rb,), in_specs=[spec, mine_spec, spec, spec, spec], out_specs=[spec] * 4),
        out_shape=[jax.ShapeDtypeStruct((R, C), F32)] * 4, compiler_params=_cp("parallel"),
    )(c_arr, w, mine, land, m, v)


def _adam_w_in(w3, mine, land, m3, v3, c_arr):
    n = w3.shape[0]

    def body(c_ref, w_hbm, mine_ref, land_ref, m_hbm, v_hbm, g_hbm, d_hbm, mo_hbm, vo_hbm, bufs, sems):
        ins = [pltpu.make_async_copy(src.at[:, 0], bufs.at[k], sems.at[k]) for k, src in enumerate((w_hbm, m_hbm, v_hbm))]
        for cp in ins:
            cp.start()
        half = D // 2
        top = jnp.where(c_ref[0] == 0, mine_ref[...], land_ref[0:half, :])
        bot = jnp.where(c_ref[0] == 1, mine_ref[...], land_ref[half:D, :])
        g = jnp.concatenate([top, bot], axis=0)
        eye = (_iota((D, D), 0) == _iota((D, D), 1)).astype(BF16)
        g_t = jnp.zeros((n, D), F32)
        r = g
        for i in range(3):
            p = r.astype(BF16)
            g_t = g_t + _dot_tn(p, eye)
            if i < 2:
                r = r - p.astype(F32)
        for cp in ins:
            cp.wait()
        d, mn, vn = _adam_math(bufs[0], g_t, bufs[1], bufs[2])
        for k, val in enumerate((g_t, d, mn, vn)):
            bufs[3 + k] = val
        outs = [pltpu.make_async_copy(bufs.at[3 + k], dst.at[:, 0], sems.at[3 + k])
                for k, dst in enumerate((g_hbm, d_hbm, mo_hbm, vo_hbm))]
        for cp in outs:
            cp.start()
        for cp in outs:
            cp.wait()

    anyspec = pl.BlockSpec(memory_space=pl.ANY)
    vm = pl.BlockSpec(memory_space=pltpu.VMEM)
    return pl.pallas_call(
        body, name="adam_w_in",
        in_specs=[pl.BlockSpec(memory_space=pltpu.SMEM), anyspec, vm, vm, anyspec, anyspec], out_specs=[anyspec] * 4,
        out_shape=[jax.ShapeDtypeStruct(w3.shape, F32)] * 4,
        scratch_shapes=[pltpu.VMEM((7, n, D), F32), pltpu.SemaphoreType.DMA((7,))],
        compiler_params=pltpu.CompilerParams(vmem_limit_bytes=VMEM_LIMIT),
    )(c_arr, w3, mine, land, m3, v3)


def _adam_w_ada(sc_all, dmod_s, w, m, v, rb):
    R, C = w.shape

    def body(sc_ref, dm_ref, w_ref, m_ref, v_ref, g_ref, d_ref, mo_ref, vo_ref):
        g = lax.dot_general(sc_ref[...], dm_ref[...], (((0,), (0,)), ((), ())), precision=HI, preferred_element_type=F32)
        d, mn, vn = _adam_math(w_ref[...], g, m_ref[...], v_ref[...])
        g_ref[...] = g
        d_ref[...] = d
        mo_ref[...] = mn
        vo_ref[...] = vn

    spec = pl.BlockSpec((rb, C), lambda i: (i, 0))
    return pl.pallas_call(
        body, name="adam_w_ada", grid=(R // rb,),
        in_specs=[pl.BlockSpec((8, rb), lambda i: (0, i)), pl.BlockSpec((8, C), lambda i: (0, 0)), spec, spec, spec],
        out_specs=[spec] * 4, out_shape=[jax.ShapeDtypeStruct((R, C), F32)] * 4, compiler_params=_cp("parallel"),
    )(sc_all, dmod_s, w, m, v)


def _adam_small(grads, ws, ms, vs):
    k = len(ws)

    def body(*refs):
        g, w, m, v = refs[0:k], refs[k:2 * k], refs[2 * k:3 * k], refs[3 * k:4 * k]
        g_o, d_o, m_o, v_o = refs[4 * k:5 * k], refs[5 * k:6 * k], refs[6 * k:7 * k], refs[7 * k:8 * k]
        for i in range(k):
            gi = g[i][...]
            d, mn, vn = _adam_math(w[i][...], gi, m[i][...], v[i][...])
            g_o[i][...] = gi
            d_o[i][...] = d
            m_o[i][...] = mn
            v_o[i][...] = vn

    shapes = [jax.ShapeDtypeStruct(w.shape, F32) for w in ws]
    vm = pl.BlockSpec(memory_space=pltpu.VMEM)
    outs = pl.pallas_call(
        body, name="adam_small", in_specs=[vm] * (4 * k), out_specs=[vm] * (4 * k), out_shape=shapes * 4,
    )(*grads, *ws, *ms, *vs)
    return outs[0:k], outs[k:2 * k], outs[2 * k:3 * k], outs[3 * k:4 * k]


def _pos():
    return lax.axis_index("x"), lax.axis_index("y"), lax.axis_index("c")


def _flip(v, bit):
    return 1 - v if bit else v


def _peer(k):
    x, y, c = _pos()
    return (_flip(x, (k >> 2) & 1), _flip(y, (k >> 1) & 1), _flip(c, k & 1))


def _logical(p):
    return 4 * p[0] + 2 * p[1] + p[2]


def _gather8(src_ref, dst_ref, send_sems, recv_sems):
    me = _logical(_pos())
    dst_ref[pl.ds(me, 1)] = src_ref[...][None]
    copies = []
    for k in range(1, 8):
        cp = pltpu.make_async_remote_copy(src_ref, dst_ref.at[me], send_sems.at[k - 1], recv_sems.at[k - 1],
                                          device_id=_peer(k), device_id_type=MESH)
        cp.start()
        copies.append(cp)
    for k in range(1, 8):
        pltpu.make_async_remote_copy(src_ref, dst_ref.at[_logical(_peer(k))], send_sems.at[k - 1], recv_sems.at[k - 1],
                                     device_id=_peer(k), device_id_type=MESH).wait_recv()
    for cp in copies:
        cp.wait_send()


def _rows_select(ref3, width):
    row = _iota((8, width), 0)
    out = jnp.zeros((8, width), F32)
    for i in range(8):
        out = jnp.where(row == i, ref3[i][:, 0:width], out)
    return out


def _mod_exchange(payload, w_ada_s, b_ada4):
    n_sh = w_ada_s.shape[1]

    def body(pay_ref, w_ref, b_ref, gat_ref, mod_ref, token, p3, sa, ra, sb, rb):
        token[...] = jnp.zeros_like(token)
        x, y, c = _pos()
        me = _logical((x, y, c))
        my_s = 2 * x + y
        _gather8(pay_ref, gat_ref, sa, ra)
        cmat = _rows_select(gat_ref, D)
        prod = _dot_hi(cmat * _sigmoid(cmat), w_ref[...])
        for b in range(8):
            p3[b] = prod[b:b + 1, :]
        mod_ref[pl.ds(my_s, 1)] = p3[pl.ds(me, 1)] + b_ref[pl.ds(my_s, 1)]
        ks = (2, 4, 6)
        copies = []
        for i, k in enumerate(ks):
            pr = _peer(k)
            cp = pltpu.make_async_remote_copy(p3.at[_logical(pr)], mod_ref.at[my_s], sb.at[i], rb.at[i],
                                              device_id=pr, device_id_type=MESH)
            cp.start()
            copies.append(cp)
        for i, k in enumerate(ks):
            pr = _peer(k)
            s_src = 2 * pr[0] + pr[1]
            pltpu.make_async_remote_copy(p3.at[0], mod_ref.at[s_src], sb.at[i], rb.at[i],
                                         device_id=pr, device_id_type=MESH).wait_recv()
            mod_ref[pl.ds(s_src, 1)] = mod_ref[pl.ds(s_src, 1)] + b_ref[pl.ds(s_src, 1)]
        for cp in copies:
            cp.wait_send()

    vm = pl.BlockSpec(memory_space=pltpu.VMEM)
    return pl.pallas_call(
        body, name="mod_exchange", in_specs=[vm, vm, vm], out_specs=[vm, vm, vm],
        out_shape=[jax.ShapeDtypeStruct((8, 1, payload.shape[1]), F32), jax.ShapeDtypeStruct((4, 1, n_sh), F32),
                   jax.ShapeDtypeStruct((8, 128), F32)],
        scratch_shapes=[pltpu.VMEM((8, 1, n_sh), F32), pltpu.SemaphoreType.DMA((7,)), pltpu.SemaphoreType.DMA((7,)),
                        pltpu.SemaphoreType.DMA((3,)), pltpu.SemaphoreType.DMA((3,))],
        compiler_params=pltpu.CompilerParams(vmem_limit_bytes=VMEM_LIMIT),
    )(payload, w_ada_s, b_ada4)


def _chips():
    x, y, _ = _pos()
    out = []
    for k in (1, 2, 3):
        px, py = _flip(x, (k >> 1) & 1), _flip(y, k & 1)
        out.append((px, py, 2 * px + py))
    return out


def _half_rows(ref, which):
    half = ref.shape[-2] // 2
    return pl.ds(pl.multiple_of(which * half, 8), half)


def _small_reduce(vec):
    n = vec.shape[1]

    def body(v_ref, tot_ref, gat_ref, sa, ra):
        _gather8(v_ref, gat_ref, sa, ra)
        tot = gat_ref[0]
        for i in range(1, 8):
            tot = tot + gat_ref[i]
        tot_ref[...] = tot

    vm = pl.BlockSpec(memory_space=pltpu.VMEM)
    return pl.pallas_call(
        body, name="small_reduce", in_specs=[vm], out_specs=[vm, vm],
        out_shape=[jax.ShapeDtypeStruct((1, n), F32), jax.ShapeDtypeStruct((8, 1, n), F32)],
        scratch_shapes=[pltpu.SemaphoreType.DMA((7,)), pltpu.SemaphoreType.DMA((7,))],
    )(vec)


def _add_half(g, sib, c_arr, rb, name):
    _, R, C = g.shape
    half = R // 2
    nb = half // rb

    def body(c_ref, g_ref, s_ref, o_ref):
        o_ref[...] = (g_ref[...] + s_ref[...]).astype(BF16)

    return pl.pallas_call(
        body, name=name,
        grid_spec=pltpu.PrefetchScalarGridSpec(
            num_scalar_prefetch=1, grid=(4, nb),
            in_specs=[pl.BlockSpec((1, rb, C), lambda s, i, c_ref: (s, c_ref[0] * nb + i, 0)),
                      pl.BlockSpec((1, rb, C), lambda s, i, c_ref: (s, i, 0))],
            out_specs=pl.BlockSpec((1, rb, C), lambda s, i, c_ref: (s, i, 0))),
        out_shape=jax.ShapeDtypeStruct((4, half, C), BF16),
        compiler_params=_cp("parallel", "parallel"),
    )(c_arr, g, sib)


def _sum4(parts, land, s_arr, rb, name):
    _, H, C = land.shape

    def body(s_ref, own_ref, r_ref, o_ref):
        own = own_ref[0].astype(F32)
        tot = jnp.zeros((rb, C), F32)
        for j in range(4):
            tot = tot + jnp.where(s_ref[0] == j, own, r_ref[j].astype(F32))
        o_ref[...] = tot

    return pl.pallas_call(
        body, name=name,
        grid_spec=pltpu.PrefetchScalarGridSpec(
            num_scalar_prefetch=1, grid=(H // rb,),
            in_specs=[pl.BlockSpec((1, rb, C), lambda i, s_ref: (s_ref[0], i, 0)),
                      pl.BlockSpec((4, rb, C), lambda i, s_ref: (0, i, 0))],
            out_specs=pl.BlockSpec((rb, C), lambda i, s_ref: (i, 0))),
        out_shape=jax.ShapeDtypeStruct((H, C), F32), compiler_params=_cp("parallel"),
    )(s_arr, parts, land)


HBM_SPEC = pl.BlockSpec(memory_space=pltpu.HBM)
SEM_SPEC = pl.BlockSpec(memory_space=pltpu.SEMAPHORE)
EFFECT = pltpu.SideEffectType.DATAFLOW_SIDE_EFFECTING


def _split_start(name, bufs, n_sem, plan):
    nb = len(bufs)

    def body(*refs):
        ins, send, recv, token = refs[:nb], refs[nb], refs[nb + 1], refs[-1]
        for i, (src, dst, dev, _) in enumerate(plan(ins)):
            pltpu.make_async_remote_copy(src, dst, send.at[i], recv.at[i], device_id=dev, device_id_type=MESH).start()
        token[...] = jnp.zeros_like(token)

    outs = pl.pallas_call(
        body, name=name,
        out_shape=(pltpu.SemaphoreType.DMA((n_sem,)), pltpu.SemaphoreType.DMA((n_sem,)),
                   *[pltpu.HBM(b.shape, b.dtype) for b in bufs], jax.ShapeDtypeStruct((8, 128), F32)),
        in_specs=[HBM_SPEC] * nb,
        out_specs=(SEM_SPEC, SEM_SPEC, *([HBM_SPEC] * nb), pl.BlockSpec(memory_space=pltpu.VMEM)),
        input_output_aliases={i: 2 + i for i in range(nb)},
        compiler_params=pltpu.CompilerParams(has_side_effects=EFFECT),
    )(*[pltpu.with_memory_space_constraint(b, pltpu.HBM) for b in bufs])
    return outs[0], outs[1], list(outs[2:2 + nb]), outs[-1]


def _split_wait(name, send, recv, bufs, after, plan):
    nb = len(bufs)

    def body(*refs):
        ins, send_s, recv_s = refs[:nb], refs[nb], refs[nb + 1]
        for i, (src, dst, dev, mine) in enumerate(plan(ins)):
            pltpu.make_async_remote_copy(src, dst, send_s.at[i], recv_s.at[i], device_id=dev,
                                         device_id_type=MESH).wait_send()
            pltpu.make_async_remote_copy(src, mine, send_s.at[i], recv_s.at[i], device_id=dev,
                                         device_id_type=MESH).wait_recv()

    outs = pl.pallas_call(
        body, name=name, out_shape=[pltpu.HBM(b.shape, b.dtype) for b in bufs],
        in_specs=[HBM_SPEC] * nb + [SEM_SPEC, SEM_SPEC, pl.BlockSpec(memory_space=pl.ANY)],
        out_specs=[HBM_SPEC] * nb, input_output_aliases={i: i for i in range(nb)},
        compiler_params=pltpu.CompilerParams(has_side_effects=EFFECT),
    )(*bufs, send, recv, after)
    return list(outs)


def _copies_now(name, bufs, n_sem, plan):
    nb = len(bufs)

    def body(*refs):
        ins, token, send, recv = refs[:nb], refs[2 * nb], refs[-2], refs[-1]
        token[...] = jnp.zeros_like(token)
        todo = plan(ins)
        for i, (src, dst, dev, _) in enumerate(todo):
            pltpu.make_async_remote_copy(src, dst, send.at[i], recv.at[i], device_id=dev, device_id_type=MESH).start()
        for i, (src, dst, dev, mine) in enumerate(todo):
            pltpu.make_async_remote_copy(src, mine, send.at[i], recv.at[i], device_id=dev, device_id_type=MESH).wait_recv()
        for i, (src, dst, dev, _) in enumerate(todo):
            pltpu.make_async_remote_copy(src, dst, send.at[i], recv.at[i], device_id=dev, device_id_type=MESH).wait_send()

    outs = pl.pallas_call(
        body, name=name,
        out_shape=[pltpu.HBM(b.shape, b.dtype) for b in bufs] + [jax.ShapeDtypeStruct((8, 128), F32)],
        in_specs=[HBM_SPEC] * nb, out_specs=[HBM_SPEC] * nb + [pl.BlockSpec(memory_space=pltpu.VMEM)],
        input_output_aliases={i: i for i in range(nb)},
        scratch_shapes=[pltpu.SemaphoreType.DMA((n_sem,)), pltpu.SemaphoreType.DMA((n_sem,))],
    )(*[pltpu.with_memory_space_constraint(b, pltpu.HBM) for b in bufs])
    return list(outs[:nb]), outs[nb]


def _slot(land, s, rows, cols):
    if cols is None:
        return land.at[s, rows]
    return land.at[rows, pl.ds(pl.multiple_of(s * cols, 128), cols)]


def _plan_gather_ici(cols):
    nw = len(cols)

    def plan(refs):
        x, y, c = _pos()
        my_s = 2 * x + y
        out = []
        for w in range(nw):
            mine = _half_rows(refs[w], c)
            for px, py, ps in _chips():
                out.append((refs[w].at[mine], _slot(refs[nw + w], my_s, mine, cols[w]), (px, py, c),
                            _slot(refs[nw + w], ps, mine, cols[w])))
        return out
    return plan


def _plan_gather_fwd(cols, rows):
    def plan(refs):
        x, y, c = _pos()
        out = []
        for w in range(len(cols)):
            half = rows[w] // 2
            mine = pl.ds(pl.multiple_of(c * half, 8), half)
            other = pl.ds(pl.multiple_of((1 - c) * half, 8), half)
            for px, py, ps in _chips():
                got = _slot(refs[w], ps, mine, cols[w])
                out.append((got, got, (x, y, 1 - c), _slot(refs[w], ps, other, cols[w])))
        return out
    return plan


def _plan_swap(nw):
    def plan(refs):
        x, y, c = _pos()
        return [(refs[w].at[:, _half_rows(refs[w], 1 - c)], refs[nw + w], (x, y, 1 - c), refs[nw + w])
                for w in range(nw)]
    return plan


def _plan_scatter(nw):
    def plan(refs):
        x, y, c = _pos()
        my_s = 2 * x + y
        out = []
        for w in range(nw):
            for px, py, ps in _chips():
                out.append((refs[w].at[ps], refs[nw + w].at[my_s], (px, py, c), refs[nw + w].at[ps]))
        return out
    return plan


def _plan_join(nw):
    def plan(refs):
        x, y, c = _pos()
        out = []
        for w in range(nw):
            land = refs[nw + w]
            out.append((refs[w], land.at[_half_rows(land, c)], (x, y, 1 - c), land.at[_half_rows(land, 1 - c)]))
        return out
    return plan


def _hbm_empty(shape, dtype):
    return pltpu.with_memory_space_constraint(lax.empty(shape, dtype), pltpu.HBM)


def _put_slot(land, own, slot):
    return lax.dynamic_update_slice(land, own[None], (slot,) + (0,) * own.ndim)


def _pad_lanes(a, n):
    return jnp.pad(a, ((0, 0), (0, n - a.shape[1])))


def kernel(x, c, positions, w_ada, b_ada, norm1_w, w_in, conv_w, conv_b, dt_bias, a_log, d_skip, attn_sinks, ssm_norm_w, w_out, norm2_w, w_gate_up, w_down, final_norm_w, loss_target, m_w_ada, m_b_ada, m_norm1_w, m_w_in, m_conv_w, m_conv_b, m_dt_bias, m_a_log, m_d_skip, m_attn_sinks, m_ssm_norm_w, m_w_out, m_norm2_w, m_w_gate_up, m_w_down, m_final_norm_w, v_w_ada, v_b_ada, v_norm1_w, v_w_in, v_conv_w, v_conv_b, v_dt_bias, v_a_log, v_d_skip, v_attn_sinks, v_ssm_norm_w, v_w_out, v_norm2_w, v_w_gate_up, v_w_down, v_final_norm_w):
    T = x.shape[1]
    tm = min(256, T)
    xi, yi, ci = lax.axis_index("x"), lax.axis_index("y"), lax.axis_index("c")
    my_s = 2 * xi + yi
    xs = x[0]
    tgt = loss_target[0]

    payload = jnp.concatenate([c, conv_w[0].reshape(1, CONVK * 256)], axis=1)
    gat, mod4, tok = _mod_exchange(payload, w_ada[0], b_ada.reshape(4, 1, 1536))
    mod6 = mod4.reshape(6, D)
    c_all = gat[:, 0, 0:D]
    cw_dev = gat[:, 0, D:].reshape(4, 2, CONVK, 256)[:, 0]
    conv_full = cw_dev.transpose(1, 0, 2).reshape(CONVK, CONVC)

    w_in_b = (w_in[0] + tok[0, 0]).astype(BF16)
    s_i, r_i, bufs, tok = _split_start("wgather_in_ici_start", [w_in_b, _hbm_empty((4,) + w_in_b.shape, BF16)], 3,
                                       _plan_gather_ici([None]))
    inv_freq = (10000.0 ** (-jnp.arange(32, dtype=F32) / 32))
    cos, sin_s = _rope_tables(positions, inv_freq.reshape(32, 1) + tok[0:1, 0:1], min(512, T))
    bufs = _split_wait("wgather_in_ici_wait", s_i, r_i, bufs, cos, _plan_gather_ici([None]))
    bufs, tok = _copies_now("wgather_in_fwd", bufs[1:], 3, _plan_gather_fwd([None], [D]))
    g_in = _put_slot(bufs[0], w_in_b, my_s)
    w_pad = jnp.concatenate([g_in[0], g_in[1], g_in[2], g_in[3], jnp.zeros((D, IN_PAD - IN_PROJ), BF16)], axis=1)

    late = [(w_out[0] + tok[0, 0]).astype(BF16), w_gate_up[0].astype(BF16), w_down[0].astype(BF16)]
    lands = [_hbm_empty((4, D // 4, D), BF16), _hbm_empty((D, 2 * DFF), BF16), _hbm_empty((4, DFF // 4, D), BF16)]
    cols3, rows3 = [None, GU_SH, None], [D // 4, D, DFF // 4]
    s_a, r_a, bufs, tok = _split_start("wgather_ici_start", late + lands, 9, _plan_gather_ici(cols3))

    qkv, z, xbc, dtr, h1b = _in_proj_fwd(xs, cos, sin_s, mod6 + tok[0, 0], norm1_w, w_pad, min(512, T))
    sinks = attn_sinks
    attn, lse = _attn_fwd(qkv, sinks)
    bufs = _split_wait("wgather_ici_wait", s_a, r_a, bufs, attn, _plan_gather_ici(cols3))
    s_b, r_b, lands, tok = _split_start("wgather_fwd_start", bufs[3:], 9, _plan_gather_fwd(cols3, rows3))
    dtb = _pad_lanes(dt_bias, 128)
    alog = _pad_lanes(a_log, 128)
    dskx = jnp.repeat(d_skip, HD, axis=1)
    mats = _ssd_mats()
    ynorm, ypre, states, conv_pre = _ssd_fwd(xbc, z, dtr, conv_full, conv_b, dtb + tok[0, 0], alog, dskx, ssm_norm_w,
                                             mats)
    lands = _split_wait("wgather_fwd_wait", s_b, r_b, lands, ynorm, _plan_gather_fwd(cols3, rows3))
    w_out_f = _put_slot(lands[0], late[0], my_s).reshape(D, D)
    w_dn_f = _put_slot(lands[2], late[2], my_s).reshape(DFF, D)
    s_arr = my_s.reshape(1).astype(jnp.int32)

    fw2 = final_norm_w.reshape(1, D)
    sq, dmix, dx1, h2b, act, dfb, dgu, dob, sm_ffn = _mix_ffn(
        xs, attn, ynorm, tgt, mod6, norm2_w, fw2, w_out_f, lands[1], late[1], s_arr, w_dn_f, tm)

    tt = min(2048, T)
    c_arr = ci.reshape(1).astype(jnp.int32)
    tok0 = jnp.zeros((8, 128), F32)
    gw_dn4 = _tn_matmul(act, dfb, GU_SH, D, tt, "dw_down", tok0).reshape(4, DFF // 4, D)
    gw_gu4 = _tn_matmul(h2b, dgu, D, GU_SH, tt, "dw_gate_up", tok0)
    gw_out4 = jnp.concatenate(
        [_tn_matmul(attn, dob, AW, D, tt, "dw_out_a", tok0)[0],
         _tn_matmul(ynorm, dob, SW, D, tt, "dw_out_y", tok0)[0]], axis=0).reshape(4, D // 4, D)
    big1 = [gw_out4, gw_gu4, gw_dn4]
    rbs1 = [128, 128, 176]
    sib1 = [_hbm_empty((4, g.shape[1] // 2, g.shape[2]), F32) for g in big1]
    s_c, r_c, bufs, tok = _split_start("gswap_start", big1 + sib1, 3, _plan_swap(3))

    dzxd, d_cw, d_cb, d_sw, d_sk, d_dtb, d_av = _ssd_bwd(
        xbc, conv_pre, z, dtr, ypre, states, dmix, conv_full, dtb + tok[0, 0], alog, dskx, ssm_norm_w, mats)
    bufs = _split_wait("gswap_wait", s_c, r_c, bufs, dzxd, _plan_swap(3))
    sums1 = [_add_half(g, s, c_arr, rb, "grad_add_%d" % i)
             for i, (g, s, rb) in enumerate(zip(bufs[:3], bufs[3:], rbs1))]
    land1 = [_hbm_empty(p.shape, BF16) for p in sums1]
    s_d, r_d, bufs, tok = _split_start("gscatter_start", sums1 + land1, 9, _plan_scatter(3))
    dqkv, d_sinks = _attn_bwd(qkv, sinks + tok[0:1, 0:8], lse, dmix, cos, sin_s)
    bufs = _split_wait("gscatter_wait", s_d, r_d, bufs, dqkv, _plan_scatter(3))
    halves1 = [_sum4(p, l, s_arr, rb, "grad_sum_%d" % i)
               for i, (p, l, rb) in enumerate(zip(bufs[:3], bufs[3:], rbs1))]
    full1 = [_hbm_empty((2 * h.shape[0], h.shape[1]), F32) for h in halves1]
    s_e, r_e, bufs, tok = _split_start("gjoin_start", halves1 + full1, 3, _plan_join(3))
    gq = _tn_matmul(h1b, dqkv, D, 768, tt, "dw_in_qkv", tok)[0]
    gz = _tn_matmul(h1b, dzxd, D, 1664, tt, "dw_in_zxd", tok)[0]
    gw_in4 = jnp.stack([gq[:, :IN_SH], jnp.concatenate([gq[:, IN_SH:], gz[:, :2 * IN_SH - 768]], axis=1),
                        gz[:, 2 * IN_SH - 768:3 * IN_SH - 768], gz[:, 3 * IN_SH - 768:4 * IN_SH - 768]])
    joined1 = _split_wait("gjoin_wait", s_e, r_e, bufs, gw_in4, _plan_join(3))

    sib0 = _hbm_empty((4, D // 2, IN_SH), F32)
    bufs, _ = _copies_now("gswap_in", [gw_in4, sib0], 1, _plan_swap(1))
    sum0 = _add_half(bufs[0], bufs[1], c_arr, 128, "grad_add_in")
    s_g, r_g, bufs, tok = _split_start("gscatter_in_start", [sum0, _hbm_empty(sum0.shape, BF16)], 3, _plan_scatter(1))
    grad_x, sm_in = _in_proj_bwd(xs, dx1, dqkv, dzxd, mod6 + tok[0, 0], norm1_w, w_pad, min(512, T))
    bufs = _split_wait("gscatter_in_wait", s_g, r_g, bufs, grad_x, _plan_scatter(1))
    half0 = _sum4(bufs[0], bufs[1], s_arr, 128, "grad_sum_in")
    joined0, _ = _copies_now("gjoin_in", [half0, _hbm_empty((D, IN_SH), F32)], 1, _plan_join(1))

    a_neg = -jnp.exp(alog)
    pieces = [sm_in[1:2], sm_in[2:3], sm_ffn[5:6], sm_ffn[2:3], sm_ffn[3:4], sm_ffn[4:5],
              sm_in[0:1], sm_ffn[1:2], sm_ffn[0:1], d_cb, d_cw.reshape(1, CONVK * CONVC),
              _pad_lanes(d_sw, SW), d_dtb, d_av * a_neg, d_sk, d_sinks,
              _pad_lanes((0.5 / D * jnp.sum(sq)).reshape(1, 1), 128)]
    vec = jnp.concatenate(pieces, axis=1)
    tot, allv = _small_reduce(vec)
    o = 0
    offs = []
    for p in pieces:
        offs.append(o)
        o += p.shape[1]
    seg = lambda i, n: tot[:, offs[i]:offs[i] + n]
    g_b_ada = tot[:, 0:6 * D]
    g_norm1, g_norm2, g_final, g_conv_b = seg(6, D), seg(7, D), seg(8, D), seg(9, D)
    g_conv_w = lax.dynamic_slice_in_dim(seg(10, CONVK * CONVC).reshape(CONVK, CONVC), my_s * 256, 256, axis=1)
    g_ssm_w, g_dtb, g_alog, g_dsk, g_sink = seg(11, SW), seg(12, 8), seg(13, 8), seg(14, 8), seg(15, 8)
    loss = tot[0, offs[16]]

    small_names = ["b_ada", "norm1_w", "conv_w", "conv_b", "dt_bias", "a_log", "d_skip", "attn_sinks", "ssm_norm_w",
                   "norm2_w", "final_norm_w"]
    small_g = [g_b_ada, g_norm1, g_conv_w, g_conv_b, g_dtb, g_alog, g_dsk, g_sink, g_ssm_w, g_norm2, g_final]
    as2d = lambda a: a.reshape(-1, a.shape[-1])
    small_w = [as2d(a) for a in (b_ada, norm1_w, conv_w, conv_b, dt_bias, a_log, d_skip, attn_sinks, ssm_norm_w,
                                 norm2_w, final_norm_w)]
    small_m = [as2d(a) for a in (m_b_ada, m_norm1_w, m_conv_w, m_conv_b, m_dt_bias, m_a_log, m_d_skip, m_attn_sinks,
                                 m_ssm_norm_w, m_norm2_w, m_final_norm_w)]
    small_v = [as2d(a) for a in (v_b_ada, v_norm1_w, v_conv_w, v_conv_b, v_dt_bias, v_a_log, v_d_skip, v_attn_sinks,
                                 v_ssm_norm_w, v_norm2_w, v_final_norm_w)]
    small_g, sd, smn, svn = _adam_small(small_g, small_w, small_m, small_v)

    sc_all = c_all * jax.nn.sigmoid(c_all)
    dmod_all = allv[:, 0, 0:6 * D]
    dmod_s = lax.dynamic_slice_in_dim(dmod_all, my_s * 1536, 1536, axis=1)
    g_ada, d_ada, m_ada, v_ada = _adam_w_ada(sc_all, dmod_s, w_ada[0], m_w_ada[0], v_w_ada[0], 256)
    native = lambda a: a.transpose(2, 0, 1)
    g_in_s, d_in, m_in, v_in = [a.transpose(1, 2, 0) for a in _adam_w_in(
        native(w_in), joined0[0], joined0[1], native(m_w_in), native(v_w_in), c_arr)]
    g_out_s, d_out, m_out, v_out = _adam_2d(w_out[0], joined1[0], joined1[3], m_w_out[0], v_w_out[0], c_arr, 128,
                                            "adam_w_out")
    g_gu_s, d_gu, m_gu, v_gu = _adam_2d(w_gate_up[0], joined1[1], joined1[4], m_w_gate_up[0], v_w_gate_up[0], c_arr,
                                        256, "adam_w_gate_up")
    g_dn_s, d_dn, m_dn, v_dn = _adam_2d(w_down[0], joined1[2], joined1[5], m_w_down[0], v_w_down[0], c_arr, 352,
                                        "adam_w_down")

    order = ["w_ada", "b_ada", "norm1_w", "w_in", "conv_w", "conv_b", "dt_bias", "a_log", "d_skip", "attn_sinks",
             "ssm_norm_w", "w_out", "norm2_w", "w_gate_up", "w_down", "final_norm_w"]
    shapes = dict(w_ada=w_ada.shape, b_ada=b_ada.shape, norm1_w=norm1_w.shape, w_in=w_in.shape, conv_w=conv_w.shape,
                  conv_b=conv_b.shape, dt_bias=dt_bias.shape, a_log=a_log.shape, d_skip=d_skip.shape,
                  attn_sinks=attn_sinks.shape, ssm_norm_w=ssm_norm_w.shape, w_out=w_out.shape, norm2_w=norm2_w.shape,
                  w_gate_up=w_gate_up.shape, w_down=w_down.shape, final_norm_w=final_norm_w.shape)
    grads = dict(w_ada=g_ada, w_in=g_in_s, w_out=g_out_s, w_gate_up=g_gu_s, w_down=g_dn_s)
    deltas = dict(w_ada=d_ada, w_in=d_in, w_out=d_out, w_gate_up=d_gu, w_down=d_dn)
    new_m = dict(w_ada=m_ada, w_in=m_in, w_out=m_out, w_gate_up=m_gu, w_down=m_dn)
    new_v = dict(w_ada=v_ada, w_in=v_in, w_out=v_out, w_gate_up=v_gu, w_down=v_dn)
    for i, nme in enumerate(small_names):
        grads[nme], deltas[nme], new_m[nme], new_v[nme] = small_g[i], sd[i], smn[i], svn[i]
    outs = [loss, grad_x[None]]
    for table in (grads, deltas, new_m, new_v):
        outs += [table[nme].reshape(shapes[nme]) for nme in order]
    return tuple(outs)
```

```python
import functools
import math

import jax
import jax.numpy as jnp
from jax import lax
from jax.experimental import pallas as pl
from jax.experimental.pallas import tpu as pltpu

F32 = jnp.float32
BF16 = jnp.bfloat16
HI = lax.Precision.HIGHEST
MESH = pl.DeviceIdType.MESH

D = 1024
HD = 64
AW = 512
SW = 512
NST = 128
CONVK = 4
CONVC = 1024
BLK = 128
IN_PROJ = 2312
IN_PAD = 2432
IN_SH = IN_PROJ // 4
DFF = 2816
GU_SH = 1408
FF_SPLITS = ((0, 1536), (1536, 2816))
EPS = 1e-6
NEG = -1e30
LR, B1, B2, AEPS, WD, STEP = 0.001, 0.9, 0.999, 1e-08, 0.01, 10
VMEM_LIMIT = 58 * 1024 * 1024


def _cp(*sem):
    return pltpu.CompilerParams(dimension_semantics=sem or None, vmem_limit_bytes=VMEM_LIMIT)


def _dot(a, b):
    return jnp.dot(a, b, preferred_element_type=F32)


def _dot_nt(a, b):
    return lax.dot_general(a, b, (((1,), (1,)), ((), ())), preferred_element_type=F32)


def _dot_tn(a, b):
    return lax.dot_general(a, b, (((0,), (0,)), ((), ())), preferred_element_type=F32)


def _dot_hi(a, b):
    return jnp.dot(a, b, precision=HI, preferred_element_type=F32)


def _sigmoid(x):
    return 1.0 / (1.0 + jnp.exp(-x))


def _iota(shape, dim):
    return lax.broadcasted_iota(jnp.int32, shape, dim)


def _load_resident(hbm_ref, vmem_ref, sem):
    @pl.when(pl.program_id(0) == 0)
    def _():
        cp = pltpu.make_async_copy(hbm_ref, vmem_ref, sem)
        cp.start()
        cp.wait()


def _swap32(t):
    lane = _iota(t.shape, 1)
    return jnp.where((lane & 63) < 32, pltpu.roll(t, 96, 1), pltpu.roll(t, 32, 1))


def _rope_fwd(t, cos, sin_s):
    return t * cos + _swap32(t) * sin_s


def _rope_bwd(t, cos, sin_s):
    return t * cos - _swap32(t) * sin_s


def _rope_tables(pos_row, inv_freq_col, tm):
    T = pos_row.shape[1]
    lane, row = jnp.arange(128)[None, :], jnp.arange(96)[:, None]
    pick = (lane % 32) == (row % 32)
    sel_cos = pick.astype(BF16)
    sel_sin = jnp.where(pick, jnp.where(lane % 64 < 32, -1.0, 1.0), 0.0).astype(BF16)

    def body(p_ref, f_ref, sc_ref, ss_ref, cos_ref, sin_ref):
        ang = f_ref[...] * p_ref[...].astype(F32)
        cos_ref[...] = _dot_tn(_pieces(jnp.cos(ang), 3, 0), sc_ref[...])
        sin_ref[...] = _dot_tn(_pieces(jnp.sin(ang), 3, 0), ss_ref[...])

    full = lambda a: pl.BlockSpec(a.shape, lambda i: (0,) * a.ndim)
    return pl.pallas_call(
        body, name="rope_tables", grid=(T // tm,),
        in_specs=[pl.BlockSpec((1, tm), lambda i: (0, i)), full(inv_freq_col), full(sel_cos), full(sel_sin)],
        out_specs=[pl.BlockSpec((tm, 128), lambda i: (i, 0))] * 2,
        out_shape=[jax.ShapeDtypeStruct((T, 128), F32)] * 2,
        compiler_params=_cp("parallel"),
    )(pos_row, inv_freq_col, sel_cos, sel_sin)


def _in_proj_fwd(x, cos, sin_s, mod6, norm1_w, w_pad, tm):
    T = x.shape[0]

    def body(x_ref, cos_ref, sin_ref, mod_ref, nw_ref, w_hbm, qkv_ref, z_ref, xbc_ref, dt_ref, h_ref, w_vmem, sem):
        _load_resident(w_hbm, w_vmem, sem)
        xv = x_ref[...]
        r = lax.rsqrt(jnp.mean(xv * xv, axis=-1, keepdims=True) + EPS)
        h = (xv * r * nw_ref[...]) * (1.0 + mod_ref[1:2, :]) + mod_ref[0:1, :]
        hb = h.astype(BF16)
        h_ref[...] = hb
        proj = _dot(hb, w_vmem[...])
        cs, sn = cos_ref[...], sin_ref[...]
        for j in range(5):
            qkv_ref[:, 128 * j:128 * (j + 1)] = _rope_fwd(proj[:, 128 * j:128 * (j + 1)], cs, sn).astype(BF16)
        qkv_ref[:, 640:768] = proj[:, 640:768].astype(BF16)
        z_ref[...] = proj[:, 768:1280]
        xbc_ref[...] = proj[:, 1280:2304]
        dt_ref[...] = proj[:, 2304:2432]

    row = lambda w: pl.BlockSpec((tm, w), lambda i: (i, 0))
    full = lambda a: pl.BlockSpec(a.shape, lambda i: (0,) * a.ndim)
    return pl.pallas_call(
        body, name="in_proj_fwd", grid=(T // tm,),
        in_specs=[row(D), row(128), row(128), full(mod6), full(norm1_w), pl.BlockSpec(memory_space=pl.ANY)],
        out_specs=[row(768), row(512), row(1024), row(128), row(D)],
        out_shape=[jax.ShapeDtypeStruct((T, 768), BF16), jax.ShapeDtypeStruct((T, 512), F32),
                   jax.ShapeDtypeStruct((T, 1024), F32), jax.ShapeDtypeStruct((T, 128), F32),
                   jax.ShapeDtypeStruct((T, D), BF16)],
        scratch_shapes=[pltpu.VMEM((D, IN_PAD), BF16), pltpu.SemaphoreType.DMA],
        compiler_params=_cp("arbitrary"),
    )(x, cos, sin_s, mod6, norm1_w, w_pad)


def _head_variants(pair, j):
    lane = _iota(pair.shape, 1)
    lo = lane < 64
    kv = j // 2
    ev = jnp.where(lo, pair, 0.0)
    od = jnp.where(lo, 0.0, pair)
    if kv == 0:
        od = pltpu.roll(od, 64, 1)
    else:
        ev = pltpu.roll(ev, 64, 1)
    return ev.astype(BF16), od.astype(BF16)


def _kv_variants(vcat):
    lane = _iota(vcat.shape, 1)
    lo = lane < 64
    v0 = jnp.where(lo, vcat, 0.0)
    v1 = jnp.where(lo, 0.0, vcat)
    out = {
        (0, 0): v0, (0, 1): pltpu.roll(v0, 64, 1),
        (1, 0): pltpu.roll(v1, 64, 1), (1, 1): v1,
    }
    return {k: v.astype(BF16) for k, v in out.items()}


def _fold_masks(n):
    upper = _iota((BLK, BLK), 1) > _iota((BLK, BLK), 0)
    return upper, upper & (n == 0)


def _attn_fwd(qkv, sinks):
    T = qkv.shape[0]
    nb = T // BLK

    def body(sink_ref, q_ref, kc_ref, kp_ref, vc_ref, vp_ref, o_ref, lse_ref):
        n = pl.program_id(0)
        vpv = _kv_variants(vp_ref[...].astype(F32))
        vcv = _kv_variants(vc_ref[...].astype(F32))
        q_all = jnp.concatenate(
            [v for j in range(4) for v in _head_variants(q_ref[:, 128 * j:128 * (j + 1)].astype(F32), j)], axis=0)
        s_prev = _dot_nt(q_all, kp_ref[...])
        s_cur = _dot_nt(q_all, kc_ref[...])
        upper, dead = _fold_masks(n)
        lane = _iota((BLK, 128), 1)
        lse_acc = jnp.zeros((BLK, 128), F32)
        for jj in range(4):
            acc = jnp.zeros((BLK, 128), F32)
            for par in range(2):
                h = 2 * jj + par
                rows = slice(h * BLK, (h + 1) * BLK)
                sink = sink_ref[0, h]
                s = jnp.where(dead, NEG, jnp.where(upper, s_prev[rows], s_cur[rows]) * 0.125)
                m = jnp.maximum(jnp.max(s, axis=1, keepdims=True), sink)
                p = jnp.exp(s - m)
                den = jnp.sum(p, axis=1, keepdims=True) + jnp.exp(sink - m)
                pn = p * (1.0 / den)
                acc = (acc + _dot(jnp.where(upper, pn, 0.0).astype(BF16), vpv[(jj // 2, par)])
                       + _dot(jnp.where(upper, 0.0, pn).astype(BF16), vcv[(jj // 2, par)]))
                lse_acc = jnp.where(lane == h, m + jnp.log(den), lse_acc)
            o_ref[:, 128 * jj:128 * (jj + 1)] = acc.astype(BF16)
        lse_ref[...] = lse_acc

    prev = lambda n: jnp.maximum(n - 1, 0)
    return pl.pallas_call(
        body, name="attn_fwd", grid=(nb,),
        in_specs=[pl.BlockSpec(memory_space=pltpu.SMEM),
                  pl.BlockSpec((BLK, 512), lambda n: (n, 0)),
                  pl.BlockSpec((BLK, 128), lambda n: (n, 4)),
                  pl.BlockSpec((BLK, 128), lambda n: (prev(n), 4)),
                  pl.BlockSpec((BLK, 128), lambda n: (n, 5)),
                  pl.BlockSpec((BLK, 128), lambda n: (prev(n), 5))],
        out_specs=[pl.BlockSpec((BLK, 512), lambda n: (n, 0)), pl.BlockSpec((BLK, 128), lambda n: (n, 0))],
        out_shape=[jax.ShapeDtypeStruct((T, 512), BF16), jax.ShapeDtypeStruct((T, 128), F32)],
        compiler_params=_cp("parallel"),
    )(sinks, qkv, qkv, qkv, qkv, qkv)


def _attn_bwd(qkv, sinks, lse, dmix, cos, sin_s):
    T = qkv.shape[0]
    nb = T // BLK

    def body(sink_ref, q_ref, kc_ref, kp_ref, vc_ref, vp_ref, lse_ref, do_ref, cq_ref, sq_ref, ck_ref, sk_ref,
             out_ref, ds_ref, dq_car, dk_car, dv_car):
        n = pl.program_id(0)
        lane = _iota((BLK, 128), 1)

        @pl.when(n == 0)
        def _():
            ds_ref[...] = jnp.zeros_like(ds_ref)
            dq_car[...] = jnp.zeros_like(dq_car)
            dk_car[...] = jnp.zeros_like(dk_car)
            dv_car[...] = jnp.zeros_like(dv_car)

        @pl.when(n < nb)
        def _():
            kp, kc, vp, vc = kp_ref[...], kc_ref[...], vp_ref[...], vc_ref[...]
            kpv = _kv_variants(kp.astype(F32))
            kcv = _kv_variants(kc.astype(F32))
            lse_v = lse_ref[...]
            q_all = jnp.concatenate(
                [v for j in range(4) for v in _head_variants(q_ref[:, 128 * j:128 * (j + 1)].astype(F32), j)], axis=0)
            do_all = jnp.concatenate(
                [v for j in range(4) for v in _head_variants(do_ref[:, 128 * j:128 * (j + 1)], j)], axis=0)
            s_prev, s_cur = _dot_nt(q_all, kp), _dot_nt(q_all, kc)
            dp_prev, dp_cur = _dot_nt(do_all, vp), _dot_nt(do_all, vc)
            upper, dead = _fold_masks(n)
            out_ref[:, 0:512] = dq_car[...]
            dsk = jnp.zeros((1, 128), F32)
            ds_u, ds_l, p_u, p_l = [], [], [], []
            for jj in range(4):
                dq_acc = jnp.zeros((BLK, 128), F32)
                for par in range(2):
                    h = 2 * jj + par
                    rows = slice(h * BLK, (h + 1) * BLK)
                    lse_h = jnp.sum(jnp.where(lane == h, lse_v, 0.0), axis=1, keepdims=True)
                    s = jnp.where(dead, NEG, jnp.where(upper, s_prev[rows], s_cur[rows]) * 0.125)
                    p = jnp.exp(s - lse_h)
                    dp = jnp.where(upper, dp_prev[rows], dp_cur[rows])
                    delta = jnp.sum(p * dp, axis=1, keepdims=True)
                    ds = p * (dp - delta) * 0.125
                    dsu, dsl = jnp.where(upper, ds, 0.0).astype(BF16), jnp.where(upper, 0.0, ds).astype(BF16)
                    dq_acc = dq_acc + _dot(dsu, kpv[(jj // 2, par)]) + _dot(dsl, kcv[(jj // 2, par)])
                    ds_u.append(dsu)
                    ds_l.append(dsl)
                    p_u.append(jnp.where(upper, p, 0.0).astype(BF16))
                    p_l.append(jnp.where(upper, 0.0, p).astype(BF16))
                    dsk = dsk + jnp.where(lane[0:1] == h, -jnp.sum(jnp.exp(sink_ref[0, h] - lse_h) * delta), 0.0)
                dq_car[:, 128 * jj:128 * (jj + 1)] = _rope_bwd(dq_acc, cq_ref[...], sq_ref[...]).astype(BF16)
            stack = lambda parts: jnp.concatenate(parts, axis=0)
            dk_prev, dk_cur = _dot_tn(stack(ds_u), q_all), _dot_tn(stack(ds_l), q_all)
            dv_prev, dv_cur = _dot_tn(stack(p_u), do_all), _dot_tn(stack(p_l), do_all)
            ds_ref[...] += dsk
            out_ref[:, 512:640] = _rope_bwd(dk_car[...] + dk_prev, ck_ref[...], sk_ref[...]).astype(BF16)
            out_ref[:, 640:768] = (dv_car[...] + dv_prev).astype(BF16)
            dk_car[...] = dk_cur
            dv_car[...] = dv_cur

        @pl.when(n == nb)
        def _():
            out_ref[:, 0:512] = dq_car[...]
            out_ref[:, 512:640] = _rope_bwd(dk_car[...], ck_ref[...], sk_ref[...]).astype(BF16)
            out_ref[:, 640:768] = dv_car[...].astype(BF16)

    cur = lambda n: jnp.minimum(n, nb - 1)
    prev = lambda n: jnp.maximum(cur(n) - 1, 0)
    outb = lambda n: jnp.maximum(n - 1, 0)
    return pl.pallas_call(
        body, name="attn_bwd", grid=(nb + 1,),
        in_specs=[pl.BlockSpec(memory_space=pltpu.SMEM),
                  pl.BlockSpec((BLK, 512), lambda n: (cur(n), 0)),
                  pl.BlockSpec((BLK, 128), lambda n: (cur(n), 4)),
                  pl.BlockSpec((BLK, 128), lambda n: (prev(n), 4)),
                  pl.BlockSpec((BLK, 128), lambda n: (cur(n), 5)),
                  pl.BlockSpec((BLK, 128), lambda n: (prev(n), 5)),
                  pl.BlockSpec((BLK, 128), lambda n: (cur(n), 0)),
                  pl.BlockSpec((BLK, 512), lambda n: (cur(n), 0)),
                  pl.BlockSpec((BLK, 128), lambda n: (cur(n), 0)),
                  pl.BlockSpec((BLK, 128), lambda n: (cur(n), 0)),
                  pl.BlockSpec((BLK, 128), lambda n: (outb(n), 0)),
                  pl.BlockSpec((BLK, 128), lambda n: (outb(n), 0))],
        out_specs=[pl.BlockSpec((BLK, 768), lambda n: (outb(n), 0)), pl.BlockSpec((1, 128), lambda n: (0, 0))],
        out_shape=[jax.ShapeDtypeStruct((T, 768), BF16), jax.ShapeDtypeStruct((1, 128), F32)],
        scratch_shapes=[pltpu.VMEM((BLK, 512), BF16), pltpu.VMEM((BLK, 128), F32), pltpu.VMEM((BLK, 128), F32)],
        compiler_params=_cp("arbitrary"),
    )(sinks, qkv, qkv, qkv, qkv, qkv, lse, dmix, cos, sin_s, cos, sin_s)


def _ssd_mats():
    e = jnp.arange(SW)[None, :] // HD == jnp.arange(128)[:, None]
    tri = jnp.arange(BLK)[None, :] <= jnp.arange(BLK)[:, None]
    return (jnp.tile(e, (3, 1)).astype(BF16), jnp.tile(e.T, (2, 1)).astype(BF16),
            jnp.tile(tri, (1, 3)).astype(BF16), jnp.tile(tri.T, (1, 3)).astype(BF16))


def _pieces(x, n, axis):
    out, r = [], x
    for i in range(n):
        p = r.astype(BF16)
        out.append(p)
        if i + 1 < n:
            r = r - p.astype(F32)
    return jnp.concatenate(out, axis=axis)


def _expand(x, e3):
    return _dot(_pieces(x, 3, 1), e3)


def _head_sums(x, et2):
    return _dot(_pieces(x, 2, 1), et2)


def _run_sum(tri3, x):
    return _dot(tri3, _pieces(x, 3, 0))


def _shift_down(u, tail, j):
    rolled = pltpu.roll(u, j, 0)
    first = jnp.where(_iota(tail.shape, 0) < j, pltpu.roll(tail, j, 0), rolled[0:8])
    return jnp.concatenate([first, rolled[8:]], axis=0)


def _shift_up(d, head, j):
    rolled = pltpu.roll(d, BLK - j, 0)
    last = jnp.where(_iota(head.shape, 0) >= 8 - j, pltpu.roll(head, 8 - j, 0), rolled[BLK - 8:])
    return jnp.concatenate([rolled[:BLK - 8], last], axis=0)


def _ssd_parts(dtr, dtb, alog, e3, tril3):
    xx = dtr + dtb
    dt = jnp.maximum(xx, 0.0) + jnp.log(1.0 + jnp.exp(-jnp.abs(xx)))
    a_neg = -jnp.exp(alog)
    tril = _iota((BLK, BLK), 1) <= _iota((BLK, BLK), 0)
    cs = _run_sum(tril3, dt * a_neg)
    csx = _expand(cs, e3)
    last = csx[BLK - 1:BLK, :]
    return dict(xx=xx, dt=dt, a_neg=a_neg, tril=tril, cs=cs, cs_t=cs.T,
                ecsx=jnp.exp(csx), dtex=jnp.exp(last - csx), cdx=jnp.exp(last), dtx=_expand(dt, e3))


def _decay(parts, h):
    seg = parts["cs"][:, h:h + 1] - parts["cs_t"][h:h + 1, :]
    return jnp.exp(jnp.where(parts["tril"], seg, NEG))


def _group_cols(a, g):
    return a[:, 256 * g:256 * (g + 1)]


def _ssd_fwd(xbc, z, dtr, conv_w, conv_b, dtb, alog, dskx, ssm_w, mats):
    T = xbc.shape[0]
    nc = T // BLK

    def body(u_ref, tail_ref, z_ref, dtr_ref, cw_ref, cb_ref, dtb_ref, al_ref, dk_ref, sw_ref, e3_ref, tril3_ref,
             yn_ref, yp_ref, st_ref, co_ref, s_scr):
        n = pl.program_id(0)

        @pl.when(n == 0)
        def _():
            s_scr[...] = jnp.zeros_like(s_scr)

        u = u_ref[...]
        tail = jnp.where(n > 0, tail_ref[...], 0.0)
        co = cb_ref[...] + cw_ref[3:4, :] * u
        for j in range(1, CONVK):
            co = co + cw_ref[3 - j:4 - j, :] * _shift_down(u, tail, j)
        co_ref[...] = co
        xc = co * _sigmoid(co)
        pt = _ssd_parts(dtr_ref[...], dtb_ref[...], al_ref[...], e3_ref[...], tril3_ref[...])
        xs = xc[:, :SW]
        bm = [xc[:, 512:640].astype(BF16), xc[:, 640:768].astype(BF16)]
        cm = [xc[:, 768:896].astype(BF16), xc[:, 896:1024].astype(BF16)]
        s_in = s_scr[...]
        st_ref[0] = s_in
        xdt = xs * pt["dtx"]
        xde = (xdt * pt["dtex"]).astype(BF16)
        lane = _iota((BLK, 128), 1)
        lo = lane < 64
        ys, s_new = [], []
        for g in range(2):
            cb = _dot_nt(cm[g], bm[g])
            yoff = _dot(cm[g], _group_cols(s_in, g).astype(BF16))
            s_new.append(_dot_tn(bm[g], _group_cols(xde, g)))
            for jj in range(2):
                j = 2 * g + jj
                chunk = xdt[:, 128 * j:128 * (j + 1)]
                g_ev = (cb * _decay(pt, 2 * j)).astype(BF16)
                g_od = (cb * _decay(pt, 2 * j + 1)).astype(BF16)
                yd = _dot(g_ev, jnp.where(lo, chunk, 0.0).astype(BF16)) + _dot(g_od, jnp.where(lo, 0.0, chunk).astype(BF16))
                ys.append(yd + yoff[:, 128 * jj:128 * (jj + 1)] * pt["ecsx"][:, 128 * j:128 * (j + 1)])
        y = jnp.concatenate(ys, axis=1) + xs * dk_ref[...]
        s_scr[...] = s_in * pt["cdx"] + jnp.concatenate(s_new, axis=1)
        yp_ref[...] = y
        zv = z_ref[...]
        yz = y * (zv * _sigmoid(zv))
        outs = []
        for g in range(2):
            yg = _group_cols(yz, g)
            outs.append(yg * lax.rsqrt(jnp.mean(yg * yg, axis=-1, keepdims=True) + EPS))
        yn_ref[...] = (jnp.concatenate(outs, axis=1) * sw_ref[...]).astype(BF16)

    e3, _, tril3, _ = mats
    tail8 = lambda n: jnp.maximum(n * (BLK // 8) - 1, 0)
    full = lambda a: pl.BlockSpec(a.shape, lambda n: (0,) * a.ndim)
    return pl.pallas_call(
        body, name="ssd_fwd", grid=(nc,),
        in_specs=[pl.BlockSpec((BLK, CONVC), lambda n: (n, 0)), pl.BlockSpec((8, CONVC), lambda n: (tail8(n), 0)),
                  pl.BlockSpec((BLK, SW), lambda n: (n, 0)), pl.BlockSpec((BLK, 128), lambda n: (n, 0)),
                  full(conv_w), full(conv_b), full(dtb), full(alog), full(dskx), full(ssm_w), full(e3), full(tril3)],
        out_specs=[pl.BlockSpec((BLK, SW), lambda n: (n, 0)), pl.BlockSpec((BLK, SW), lambda n: (n, 0)),
                   pl.BlockSpec((1, NST, SW), lambda n: (n, 0, 0)), pl.BlockSpec((BLK, CONVC), lambda n: (n, 0))],
        out_shape=[jax.ShapeDtypeStruct((T, SW), BF16), jax.ShapeDtypeStruct((T, SW), F32),
                   jax.ShapeDtypeStruct((nc, NST, SW), F32), jax.ShapeDtypeStruct((T, CONVC), F32)],
        scratch_shapes=[pltpu.VMEM((NST, SW), F32)],
        compiler_params=_cp("arbitrary"),
    )(xbc, xbc, z, dtr, conv_w, conv_b, dtb, alog, dskx, ssm_w, e3, tril3)


def _ssd_bwd(xbc, co_all, z, dtr, ypre, states, dmix, conv_w, dtb, alog, dskx, ssm_w, mats):
    T = xbc.shape[0]
    nc = T // BLK

    def body(u_ref, co_ref, z_ref, dtr_ref, yp_ref, st_ref, dyn_ref, cw_ref, dtb_ref, al_ref, dk_ref, sw_ref,
             e3_ref, et2_ref, tril3_ref, triu3_ref,
             out_ref, dcw_ref, dcb_ref, dsw_ref, dsk_ref, ddtb_ref, dav_ref, ds_scr, dco_scr, dskx_scr):
        i = pl.program_id(0)

        @pl.when(i == 0)
        def _():
            for r in (dcw_ref, dcb_ref, dsw_ref, dsk_ref, ddtb_ref, dav_ref, ds_scr, dco_scr, dskx_scr):
                r[...] = jnp.zeros_like(r)

        co = co_ref[...]
        sg = _sigmoid(co)
        xc = co * sg
        pt = _ssd_parts(dtr_ref[...], dtb_ref[...], al_ref[...], e3_ref[...], tril3_ref[...])
        dtx, ecsx, dtex, cdx = pt["dtx"], pt["ecsx"], pt["dtex"], pt["cdx"]
        xs = xc[:, :SW]
        bm = [xc[:, 512:640].astype(BF16), xc[:, 640:768].astype(BF16)]
        cm = [xc[:, 768:896].astype(BF16), xc[:, 896:1024].astype(BF16)]
        s_in = st_ref[0]
        ds_out = ds_scr[...]
        e_t = et2_ref[...]

        zv = z_ref[...]
        sz = _sigmoid(zv)
        silu_z = zv * sz
        ypre = yp_ref[...]
        yz = ypre * silu_z
        dyn = dyn_ref[...]
        sw = sw_ref[...]
        dyz, yns = [], []
        for g in range(2):
            yg = _group_cols(yz, g)
            r = lax.rsqrt(jnp.mean(yg * yg, axis=-1, keepdims=True) + EPS)
            yn = yg * r
            dg = _group_cols(dyn, g) * _group_cols(sw, g)
            dyz.append(r * (dg - yn * jnp.mean(dg * yn, axis=-1, keepdims=True)))
            yns.append(yn)
        dyz = jnp.concatenate(dyz, axis=1)
        dsw_ref[...] += jnp.sum(dyn * jnp.concatenate(yns, axis=1), axis=0, keepdims=True)
        dy = dyz * silu_z
        dz = dyz * ypre * (sz * (1.0 + zv * (1.0 - sz)))

        xdt = xs * dtx
        xdt_b = xdt.astype(BF16)
        edy = (ecsx * dy).astype(BF16)
        xde = (xdt * dtex).astype(BF16)
        lane = _iota((BLK, 128), 1)
        lo = lane < 64
        row8 = _iota((8, 128), 0)
        dcs = jnp.zeros((BLK, 128), F32)
        col_rows = jnp.zeros((8, 128), F32)
        dxdt, bds, yoff, dbs, dcs_g, ds_new = [], [], [], [], [], []
        for g in range(2):
            s_g = _group_cols(s_in, g).astype(BF16)
            dso_g = _group_cols(ds_out, g).astype(BF16)
            cb = _dot_nt(cm[g], bm[g])
            bds.append(_dot(bm[g], dso_g))
            yoff.append(_dot(cm[g], s_g))
            dcb_g = jnp.zeros((BLK, BLK), F32)
            for jj in range(2):
                j = 2 * g + jj
                dy_c = dy[:, 128 * j:128 * (j + 1)]
                xdt_c = xdt_b[:, 128 * j:128 * (j + 1)]
                acc = jnp.zeros((BLK, 128), F32)
                for par in range(2):
                    h = 2 * j + par
                    lm = _decay(pt, h)
                    gm = cb * lm
                    dy_m = (jnp.where(lo, dy_c, 0.0) if par == 0 else jnp.where(lo, 0.0, dy_c)).astype(BF16)
                    dg_h = _dot_nt(dy_m, xdt_c)
                    w_h = dg_h * gm
                    dcs = dcs + jnp.where(lane == h, jnp.sum(w_h, axis=1, keepdims=True), 0.0)
                    col_rows = col_rows + jnp.where(row8 == h, jnp.sum(w_h, axis=0, keepdims=True), 0.0)
                    dcb_g = dcb_g + dg_h * lm
                    acc = acc + _dot_tn(gm.astype(BF16), dy_m)
                dxdt.append(acc)
            dcb_b = dcb_g.astype(BF16)
            dcs_g.append(_dot(dcb_b, bm[g]) + _dot_nt(_group_cols(edy, g), s_g))
            dbs.append(_dot_tn(dcb_b, cm[g]) + _dot_nt(_group_cols(xde, g), dso_g))
            ds_new.append(_dot_tn(cm[g], _group_cols(edy, g)))
        bds = jnp.concatenate(bds, axis=1)
        yoff = jnp.concatenate(yoff, axis=1) * ecsx
        dxdt = jnp.concatenate(dxdt, axis=1) + dtex * bds
        ds_scr[...] = cdx * ds_out + jnp.concatenate(ds_new, axis=1)

        t_m = _head_sums(dtex * xdt * bds, e_t)
        colsum_t = jnp.concatenate([col_rows, jnp.zeros((BLK - 8, 128), F32)], axis=0).T
        cd = jnp.exp(pt["cs"][BLK - 1:BLK, :])
        sds = jnp.sum(s_in * ds_out, axis=0, keepdims=True)
        last_row = jnp.sum(t_m, axis=0, keepdims=True) + cd * _head_sums(jnp.broadcast_to(sds, (8, SW)), e_t)[0:1]
        dcs = dcs - colsum_t + _head_sums(dy * yoff, e_t) - t_m
        dcs = dcs + jnp.where(_iota((BLK, 128), 0) == BLK - 1, last_row, 0.0)
        da = _run_sum(triu3_ref[...], dcs)
        dt = pt["dt"]
        ddt = da * pt["a_neg"] + _head_sums(dxdt * xs, e_t)
        dav_ref[...] += jnp.sum(da * dt, axis=0, keepdims=True)
        ddtr = ddt * _sigmoid(pt["xx"])
        ddtb_ref[...] += jnp.sum(ddtr, axis=0, keepdims=True)
        dxs = dxdt * dtx + dy * dk_ref[...]
        dskx_scr[...] += jnp.sum(dy * xs, axis=0, keepdims=True)
        dxc = jnp.concatenate([dxs, dbs[0], dbs[1], dcs_g[0], dcs_g[1]], axis=1)
        dco = dxc * (sg * (1.0 + co * (1.0 - sg)))

        dcb_ref[...] += jnp.sum(dco, axis=0, keepdims=True)
        u = u_ref[...]
        head = dco_scr[...]
        du = jnp.zeros_like(dco)
        for j in range(CONVK):
            up_j = dco if j == 0 else _shift_up(dco, head, j)
            dcw_ref[3 - j:4 - j, :] += jnp.sum(up_j * u, axis=0, keepdims=True)
            du = du + cw_ref[3 - j:4 - j, :] * up_j
        dco_scr[...] = dco[0:8]
        out_ref[:, 0:512] = dz.astype(BF16)
        out_ref[:, 512:1536] = du.astype(BF16)
        out_ref[:, 1536:1664] = ddtr.astype(BF16)

        @pl.when(i == nc - 1)
        def _():
            dsk_ref[...] = _head_sums(jnp.broadcast_to(dskx_scr[...], (8, SW)), e_t)[0:1]

    e3, et2, tril3, triu3 = mats
    rev = lambda i: nc - 1 - i
    full = lambda a: pl.BlockSpec(a.shape, lambda i: (0,) * a.ndim)
    acc = lambda r, c: pl.BlockSpec((r, c), lambda i: (0, 0))
    return pl.pallas_call(
        body, name="ssd_bwd", grid=(nc,),
        in_specs=[pl.BlockSpec((BLK, CONVC), lambda i: (rev(i), 0)), pl.BlockSpec((BLK, CONVC), lambda i: (rev(i), 0)),
                  pl.BlockSpec((BLK, SW), lambda i: (rev(i), 0)), pl.BlockSpec((BLK, 128), lambda i: (rev(i), 0)),
                  pl.BlockSpec((BLK, SW), lambda i: (rev(i), 0)), pl.BlockSpec((1, NST, SW), lambda i: (rev(i), 0, 0)),
                  pl.BlockSpec((BLK, SW), lambda i: (rev(i), 1)),
                  full(conv_w), full(dtb), full(alog), full(dskx), full(ssm_w),
                  full(e3), full(et2), full(tril3), full(triu3)],
        out_specs=[pl.BlockSpec((BLK, 1664), lambda i: (rev(i), 0)),
                   acc(CONVK, CONVC), acc(1, CONVC), acc(1, SW), acc(1, 128), acc(1, 128), acc(1, 128)],
        out_shape=[jax.ShapeDtypeStruct((T, 1664), BF16),
                   jax.ShapeDtypeStruct((CONVK, CONVC), F32), jax.ShapeDtypeStruct((1, CONVC), F32),
                   jax.ShapeDtypeStruct((1, SW), F32), jax.ShapeDtypeStruct((1, 128), F32),
                   jax.ShapeDtypeStruct((1, 128), F32), jax.ShapeDtypeStruct((1, 128), F32)],
        scratch_shapes=[pltpu.VMEM((NST, SW), F32), pltpu.VMEM((8, CONVC), F32), pltpu.VMEM((1, SW), F32)],
        compiler_params=_cp("arbitrary"),
    )(xbc, co_all, z, dtr, ypre, states, dmix, conv_w, dtb, alog, dskx, ssm_w, e3, et2, tril3, triu3)


def _mix_ffn(x, attn, ynorm, tgt, mod6, norm2_w, final_w, w_out, w_gu, w_gu_own, s_arr, w_dn, tm):
    T = x.shape[0]
    nt = T // tm

    def body(x_ref, a_ref, y_ref, t_ref, mod_ref, n2_ref, fw_ref, wo_hbm, wgu_hbm, own_hbm, s_ref, wdn_hbm,
             sq_ref, dmix_ref, dx1_ref, h2_ref, act_ref, df_ref, dgu_ref, do_ref, sm_ref,
             wo, wgu, wdn, sems):
        i = pl.program_id(0)

        @pl.when(i == 0)
        def _():
            cps = [pltpu.make_async_copy(s, d, sems.at[k]) for k, (s, d) in
                   enumerate(((wo_hbm, wo), (wgu_hbm, wgu), (wdn_hbm, wdn)))]
            for c in cps:
                c.start()
            for c in cps:
                c.wait()
            own = pltpu.make_async_copy(
                own_hbm, wgu.at[:, pl.ds(pl.multiple_of(s_ref[0] * GU_SH, 128), GU_SH)], sems.at[3])
            own.start()
            own.wait()
            sq_ref[...] = jnp.zeros_like(sq_ref)
            sm_ref[...] = jnp.zeros_like(sm_ref)

        gate1, shift2, scale2, gate2 = mod_ref[2:3, :], mod_ref[3:4, :], mod_ref[4:5, :], mod_ref[5:6, :]
        n2w, fw = n2_ref[...], fw_ref[...]
        o = _dot(a_ref[...], wo[0:AW, :]) + _dot(y_ref[...], wo[AW:D, :])
        x1 = x_ref[...] + gate1 * o
        r2 = lax.rsqrt(jnp.mean(x1 * x1, axis=-1, keepdims=True) + EPS)
        xh2 = x1 * r2
        n2 = xh2 * n2w
        h2b = (n2 * (1.0 + scale2) + shift2).astype(BF16)
        h2_ref[...] = h2b
        f = jnp.zeros((tm, D), F32)
        saved = []
        for a, b in FF_SPLITS:
            gp = _dot(h2b, wgu[:, a:b])
            upj = _dot(h2b, wgu[:, DFF + a:DFF + b])
            sg = _sigmoid(gp)
            sl = gp * sg
            actb = (sl * upj).astype(BF16)
            act_ref[:, a:b] = actb
            f = f + _dot(actb, wdn[a:b, :])
            saved.append((gp, upj, sg, sl))
        x2 = x1 + gate2 * f
        r3 = lax.rsqrt(jnp.mean(x2 * x2, axis=-1, keepdims=True) + EPS)
        xh3 = x2 * r3
        err = xh3 * fw - t_ref[...]
        sq_ref[...] += jnp.sum(err * err, axis=0, keepdims=True)
        dy = err * (1.0 / D)
        dfw = jnp.sum(dy * xh3, axis=0, keepdims=True)
        dxh3 = dy * fw
        dx2 = r3 * (dxh3 - xh3 * jnp.mean(dxh3 * xh3, axis=-1, keepdims=True))
        dgate2 = jnp.sum(dx2 * f, axis=0, keepdims=True)
        dfb = (dx2 * gate2).astype(BF16)
        df_ref[...] = dfb
        dh2 = jnp.zeros((tm, D), F32)
        for (a, b), (gp, upj, sg, sl) in zip(FF_SPLITS, saved):
            dact = _dot_nt(dfb, wdn[a:b, :])
            dg = (dact * upj * (sg * (1.0 + gp * (1.0 - sg)))).astype(BF16)
            du = (dact * sl).astype(BF16)
            dgu_ref[:, a:b] = dg
            dgu_ref[:, DFF + a:DFF + b] = du
            dh2 = dh2 + _dot_nt(dg, wgu[:, a:b]) + _dot_nt(du, wgu[:, DFF + a:DFF + b])
        dshift2 = jnp.sum(dh2, axis=0, keepdims=True)
        dscale2 = jnp.sum(dh2 * n2, axis=0, keepdims=True)
        dn2 = dh2 * (1.0 + scale2)
        dn2w = jnp.sum(dn2 * xh2, axis=0, keepdims=True)
        dxh2 = dn2 * n2w
        dx1 = dx2 + r2 * (dxh2 - xh2 * jnp.mean(dxh2 * xh2, axis=-1, keepdims=True))
        dx1_ref[...] = dx1
        dgate1 = jnp.sum(dx1 * o, axis=0, keepdims=True)
        dob = (dx1 * gate1).astype(BF16)
        do_ref[...] = dob
        dmix_ref[...] = _dot_nt(dob, wo[...])
        sm_ref[...] += jnp.concatenate(
            [dfw, dn2w, dshift2, dscale2, dgate2, dgate1, jnp.zeros((2, D), F32)], axis=0)

    row = lambda w: pl.BlockSpec((tm, w), lambda i: (i, 0))
    full = lambda a: pl.BlockSpec(a.shape, lambda i: (0,) * a.ndim)
    anyspec = pl.BlockSpec(memory_space=pl.ANY)
    return pl.pallas_call(
        body, name="mix_ffn", grid=(nt,),
        in_specs=[row(D), row(AW), row(SW), row(D), full(mod6), full(norm2_w), full(final_w), anyspec, anyspec, anyspec,
                  pl.BlockSpec(memory_space=pltpu.SMEM), anyspec],
        out_specs=[pl.BlockSpec((1, D), lambda i: (0, 0)), row(D), row(D), row(D),
                   row(DFF), row(D), row(2 * DFF), row(D), pl.BlockSpec((8, D), lambda i: (0, 0))],
        out_shape=[jax.ShapeDtypeStruct((1, D), F32), jax.ShapeDtypeStruct((T, D), F32), jax.ShapeDtypeStruct((T, D), F32),
                   jax.ShapeDtypeStruct((T, D), BF16), jax.ShapeDtypeStruct((T, DFF), BF16),
                   jax.ShapeDtypeStruct((T, D), BF16), jax.ShapeDtypeStruct((T, 2 * DFF), BF16),
                   jax.ShapeDtypeStruct((T, D), BF16), jax.ShapeDtypeStruct((8, D), F32)],
        scratch_shapes=[pltpu.VMEM((D, D), BF16), pltpu.VMEM((D, 2 * DFF), BF16), pltpu.VMEM((DFF, D), BF16),
                        pltpu.SemaphoreType.DMA((4,))],
        compiler_params=_cp("arbitrary"),
    )(x, attn, ynorm, tgt, mod6, norm2_w, final_w, w_out, w_gu, w_gu_own, s_arr, w_dn)


def _in_proj_bwd(x, dx1, dqkv, dzxd, mod6, norm1_w, w_pad, tm):
    T = x.shape[0]

    def body(x_ref, dx1_ref, dq_ref, dz_ref, mod_ref, nw_ref, w_hbm, gx_ref, sm_ref, w_vmem, sem):
        _load_resident(w_hbm, w_vmem, sem)

        @pl.when(pl.program_id(0) == 0)
        def _():
            sm_ref[...] = jnp.zeros_like(sm_ref)

        nw = nw_ref[...]
        scale1 = mod_ref[1:2, :]
        sums = jnp.zeros((8, D), F32)
        for rows in (slice(0, tm // 2), slice(tm // 2, tm)):
            dh = _dot_nt(dq_ref[rows, :], w_vmem[:, 0:768]) + _dot_nt(dz_ref[rows, :], w_vmem[:, 768:IN_PAD])
            xv = x_ref[rows, :]
            r = lax.rsqrt(jnp.mean(xv * xv, axis=-1, keepdims=True) + EPS)
            xh = xv * r
            n1 = xh * nw
            dshift = jnp.sum(dh, axis=0, keepdims=True)
            dscale = jnp.sum(dh * n1, axis=0, keepdims=True)
            dn = dh * (1.0 + scale1)
            dnw = jnp.sum(dn * xh, axis=0, keepdims=True)
            dxh = dn * nw
            gx_ref[rows, :] = dx1_ref[rows, :] + r * (dxh - xh * jnp.mean(dxh * xh, axis=-1, keepdims=True))
            sums = sums + jnp.concatenate([dnw, dshift, dscale, jnp.zeros((5, D), F32)], axis=0)
        sm_ref[...] += sums

    row = lambda w: pl.BlockSpec((tm, w), lambda i: (i, 0))
    full = lambda a: pl.BlockSpec(a.shape, lambda i: (0,) * a.ndim)
    return pl.pallas_call(
        body, name="in_proj_bwd", grid=(T // tm,),
        in_specs=[row(D), row(D), row(768), row(1664), full(mod6), full(norm1_w), pl.BlockSpec(memory_space=pl.ANY)],
        out_specs=[row(D), pl.BlockSpec((8, D), lambda i: (0, 0))],
        out_shape=[jax.ShapeDtypeStruct((T, D), F32), jax.ShapeDtypeStruct((8, D), F32)],
        scratch_shapes=[pltpu.VMEM((D, IN_PAD), BF16), pltpu.SemaphoreType.DMA],
        compiler_params=_cp("arbitrary"),
    )(x, dx1, dqkv, dzxd, mod6, norm1_w, w_pad)


def _tn_matmul(a, b, K, N, tt, name, dep):
    T = a.shape[0]
    ja, jb = a.shape[1] // K, b.shape[1] // N
    J = max(ja, jb)

    def body(a_ref, b_ref, dep_ref, o_ref):
        t = pl.program_id(1)
        prod = _dot_tn(a_ref[...], b_ref[...])

        @pl.when(t == 0)
        def _():
            o_ref[0] = prod

        @pl.when(t > 0)
        def _():
            o_ref[0] += prod

    return pl.pallas_call(
        body, name=name, grid=(J, T // tt),
        in_specs=[pl.BlockSpec((tt, K), lambda j, t: (t, j if ja > 1 else 0)),
                  pl.BlockSpec((tt, N), lambda j, t: (t, j if jb > 1 else 0)),
                  pl.BlockSpec((8, 128), lambda j, t: (0, 0))],
        out_specs=pl.BlockSpec((1, K, N), lambda j, t: (j, 0, 0)),
        out_shape=jax.ShapeDtypeStruct((J, K, N), F32),
        compiler_params=_cp("parallel", "arbitrary"),
    )(a, b, dep)


def _adam_math(w, g, m, v):
    m = B1 * m + (1.0 - B1) * g
    v = B2 * v + (1.0 - B2) * (g * g)
    m_hat = m / (1.0 - B1 ** STEP)
    v_hat = v / (1.0 - B2 ** STEP)
    delta = -LR * (m_hat / (jnp.sqrt(v_hat) + AEPS) + WD * w)
    return delta, m, v


def _adam_2d(w, mine, land, m, v, c_arr, rb, name):
    R, C = w.shape
    nbh = R // 2 // rb

    def body(c_ref, w_ref, mine_ref, land_ref, m_ref, v_ref, go_ref, d_ref, mo_ref, vo_ref):
        g = jnp.where(pl.program_id(0) // nbh == c_ref[0], mine_ref[...], land_ref[...])
        d, mn, vn = _adam_math(w_ref[...], g, m_ref[...], v_ref[...])
        go_ref[...] = g
        d_ref[...] = d
        mo_ref[...] = mn
        vo_ref[...] = vn

    spec = pl.BlockSpec((rb, C), lambda i, c_ref: (i, 0))
    mine_spec = pl.BlockSpec((rb, C), lambda i, c_ref: (jnp.clip(i - c_ref[0] * nbh, 0, nbh - 1), 0))
    return pl.pallas_call(
        body, name=name,
        grid_spec=pltpu.PrefetchScalarGridSpec(
            num_scalar_prefetch=1, grid=(R // rb,), in_specs=[spec, mine_spec, spec, spec, spec], out_specs=[spec] * 4),
        out_shape=[jax.ShapeDtypeStruct((R, C), F32)] * 4, compiler_params=_cp("parallel"),
    )(c_arr, w, mine, land, m, v)


def _adam_w_in(w3, mine, land, m3, v3, c_arr):
    n = w3.shape[0]

    def body(c_ref, w_hbm, mine_ref, land_ref, m_hbm, v_hbm, g_hbm, d_hbm, mo_hbm, vo_hbm, bufs, sems):
        ins = [pltpu.make_async_copy(src.at[:, 0], bufs.at[k], sems.at[k]) for k, src in enumerate((w_hbm, m_hbm, v_hbm))]
        for cp in ins:
            cp.start()
        half = D // 2
        top = jnp.where(c_ref[0] == 0, mine_ref[...], land_ref[0:half, :])
        bot = jnp.where(c_ref[0] == 1, mine_ref[...], land_ref[half:D, :])
        g = jnp.concatenate([top, bot], axis=0)
        eye = (_iota((D, D), 0) == _iota((D, D), 1)).astype(BF16)
        g_t = jnp.zeros((n, D), F32)
        r = g
        for i in range(3):
            p = r.astype(BF16)
            g_t = g_t + _dot_tn(p, eye)
            if i < 2:
                r = r - p.astype(F32)
        for cp in ins:
            cp.wait()
        d, mn, vn = _adam_math(bufs[0], g_t, bufs[1], bufs[2])
        for k, val in enumerate((g_t, d, mn, vn)):
            bufs[3 + k] = val
        outs = [pltpu.make_async_copy(bufs.at[3 + k], dst.at[:, 0], sems.at[3 + k])
                for k, dst in enumerate((g_hbm, d_hbm, mo_hbm, vo_hbm))]
        for cp in outs:
            cp.start()
        for cp in outs:
            cp.wait()

    anyspec = pl.BlockSpec(memory_space=pl.ANY)
    vm = pl.BlockSpec(memory_space=pltpu.VMEM)
    return pl.pallas_call(
        body, name="adam_w_in",
        in_specs=[pl.BlockSpec(memory_space=pltpu.SMEM), anyspec, vm, vm, anyspec, anyspec], out_specs=[anyspec] * 4,
        out_shape=[jax.ShapeDtypeStruct(w3.shape, F32)] * 4,
        scratch_shapes=[pltpu.VMEM((7, n, D), F32), pltpu.SemaphoreType.DMA((7,))],
        compiler_params=pltpu.CompilerParams(vmem_limit_bytes=VMEM_LIMIT),
    )(c_arr, w3, mine, land, m3, v3)


def _adam_w_ada(sc_all, dmod_s, w, m, v, rb):
    R, C = w.shape

    def body(sc_ref, dm_ref, w_ref, m_ref, v_ref, g_ref, d_ref, mo_ref, vo_ref):
        g = lax.dot_general(sc_ref[...], dm_ref[...], (((0,), (0,)), ((), ())), precision=HI, preferred_element_type=F32)
        d, mn, vn = _adam_math(w_ref[...], g, m_ref[...], v_ref[...])
        g_ref[...] = g
        d_ref[...] = d
        mo_ref[...] = mn
        vo_ref[...] = vn

    spec = pl.BlockSpec((rb, C), lambda i: (i, 0))
    return pl.pallas_call(
        body, name="adam_w_ada", grid=(R // rb,),
        in_specs=[pl.BlockSpec((8, rb), lambda i: (0, i)), pl.BlockSpec((8, C), lambda i: (0, 0)), spec, spec, spec],
        out_specs=[spec] * 4, out_shape=[jax.ShapeDtypeStruct((R, C), F32)] * 4, compiler_params=_cp("parallel"),
    )(sc_all, dmod_s, w, m, v)


def _adam_small(grads, ws, ms, vs):
    k = len(ws)

    def body(*refs):
        g, w, m, v = refs[0:k], refs[k:2 * k], refs[2 * k:3 * k], refs[3 * k:4 * k]
        g_o, d_o, m_o, v_o = refs[4 * k:5 * k], refs[5 * k:6 * k], refs[6 * k:7 * k], refs[7 * k:8 * k]
        for i in range(k):
            gi = g[i][...]
            d, mn, vn = _adam_math(w[i][...], gi, m[i][...], v[i][...])
            g_o[i][...] = gi
            d_o[i][...] = d
            m_o[i][...] = mn
            v_o[i][...] = vn

    shapes = [jax.ShapeDtypeStruct(w.shape, F32) for w in ws]
    vm = pl.BlockSpec(memory_space=pltpu.VMEM)
    outs = pl.pallas_call(
        body, name="adam_small", in_specs=[vm] * (4 * k), out_specs=[vm] * (4 * k), out_shape=shapes * 4,
    )(*grads, *ws, *ms, *vs)
    return outs[0:k], outs[k:2 * k], outs[2 * k:3 * k], outs[3 * k:4 * k]


def _pos():
    return lax.axis_index("x"), lax.axis_index("y"), lax.axis_index("c")


def _flip(v, bit):
    return 1 - v if bit else v


def _peer(k):
    x, y, c = _pos()
    return (_flip(x, (k >> 2) & 1), _flip(y, (k >> 1) & 1), _flip(c, k & 1))


def _logical(p):
    return 4 * p[0] + 2 * p[1] + p[2]


def _gather8(src_ref, dst_ref, send_sems, recv_sems):
    me = _logical(_pos())
    dst_ref[pl.ds(me, 1)] = src_ref[...][None]
    copies = []
    for k in range(1, 8):
        cp = pltpu.make_async_remote_copy(src_ref, dst_ref.at[me], send_sems.at[k - 1], recv_sems.at[k - 1],
                                          device_id=_peer(k), device_id_type=MESH)
        cp.start()
        copies.append(cp)
    for k in range(1, 8):
        pltpu.make_async_remote_copy(src_ref, dst_ref.at[_logical(_peer(k))], send_sems.at[k - 1], recv_sems.at[k - 1],
                                     device_id=_peer(k), device_id_type=MESH).wait_recv()
    for cp in copies:
        cp.wait_send()


def _rows_select(ref3, width):
    row = _iota((8, width), 0)
    out = jnp.zeros((8, width), F32)
    for i in range(8):
        out = jnp.where(row == i, ref3[i][:, 0:width], out)
    return out


def _mod_exchange(payload, w_ada_s, b_ada4):
    n_sh = w_ada_s.shape[1]

    def body(pay_ref, w_ref, b_ref, gat_ref, mod_ref, token, p3, sa, ra, sb, rb):
        token[...] = jnp.zeros_like(token)
        x, y, c = _pos()
        me = _logical((x, y, c))
        my_s = 2 * x + y
        _gather8(pay_ref, gat_ref, sa, ra)
        cmat = _rows_select(gat_ref, D)
        prod = _dot_hi(cmat * _sigmoid(cmat), w_ref[...])
        for b in range(8):
            p3[b] = prod[b:b + 1, :]
        mod_ref[pl.ds(my_s, 1)] = p3[pl.ds(me, 1)] + b_ref[pl.ds(my_s, 1)]
        ks = (2, 4, 6)
        copies = []
        for i, k in enumerate(ks):
            pr = _peer(k)
            cp = pltpu.make_async_remote_copy(p3.at[_logical(pr)], mod_ref.at[my_s], sb.at[i], rb.at[i],
                                              device_id=pr, device_id_type=MESH)
            cp.start()
            copies.append(cp)
        for i, k in enumerate(ks):
            pr = _peer(k)
            s_src = 2 * pr[0] + pr[1]
            pltpu.make_async_remote_copy(p3.at[0], mod_ref.at[s_src], sb.at[i], rb.at[i],
                                         device_id=pr, device_id_type=MESH).wait_recv()
            mod_ref[pl.ds(s_src, 1)] = mod_ref[pl.ds(s_src, 1)] + b_ref[pl.ds(s_src, 1)]
        for cp in copies:
            cp.wait_send()

    vm = pl.BlockSpec(memory_space=pltpu.VMEM)
    return pl.pallas_call(
        body, name="mod_exchange", in_specs=[vm, vm, vm], out_specs=[vm, vm, vm],
        out_shape=[jax.ShapeDtypeStruct((8, 1, payload.shape[1]), F32), jax.ShapeDtypeStruct((4, 1, n_sh), F32),
                   jax.ShapeDtypeStruct((8, 128), F32)],
        scratch_shapes=[pltpu.VMEM((8, 1, n_sh), F32), pltpu.SemaphoreType.DMA((7,)), pltpu.SemaphoreType.DMA((7,)),
                        pltpu.SemaphoreType.DMA((3,)), pltpu.SemaphoreType.DMA((3,))],
        compiler_params=pltpu.CompilerParams(vmem_limit_bytes=VMEM_LIMIT),
    )(payload, w_ada_s, b_ada4)


def _chips():
    x, y, _ = _pos()
    out = []
    for k in (1, 2, 3):
        px, py = _flip(x, (k >> 1) & 1), _flip(y, k & 1)
        out.append((px, py, 2 * px + py))
    return out


def _half_rows(ref, which):
    half = ref.shape[-2] // 2
    return pl.ds(pl.multiple_of(which * half, 8), half)


def _small_reduce(vec):
    n = vec.shape[1]

    def body(v_ref, tot_ref, gat_ref, sa, ra):
        _gather8(v_ref, gat_ref, sa, ra)
        tot = gat_ref[0]
        for i in range(1, 8):
            tot = tot + gat_ref[i]
        tot_ref[...] = tot

    vm = pl.BlockSpec(memory_space=pltpu.VMEM)
    return pl.pallas_call(
        body, name="small_reduce", in_specs=[vm], out_specs=[vm, vm],
        out_shape=[jax.ShapeDtypeStruct((1, n), F32), jax.ShapeDtypeStruct((8, 1, n), F32)],
        scratch_shapes=[pltpu.SemaphoreType.DMA((7,)), pltpu.SemaphoreType.DMA((7,))],
    )(vec)


def _add_half(g, sib, c_arr, rb, name):
    _, R, C = g.shape
    half = R // 2
    nb = half // rb

    def body(c_ref, g_ref, s_ref, o_ref):
        o_ref[...] = (g_ref[...] + s_ref[...]).astype(BF16)

    return pl.pallas_call(
        body, name=name,
        grid_spec=pltpu.PrefetchScalarGridSpec(
            num_scalar_prefetch=1, grid=(4, nb),
            in_specs=[pl.BlockSpec((1, rb, C), lambda s, i, c_ref: (s, c_ref[0] * nb + i, 0)),
                      pl.BlockSpec((1, rb, C), lambda s, i, c_ref: (s, i, 0))],
            out_specs=pl.BlockSpec((1, rb, C), lambda s, i, c_ref: (s, i, 0))),
        out_shape=jax.ShapeDtypeStruct((4, half, C), BF16),
        compiler_params=_cp("parallel", "parallel"),
    )(c_arr, g, sib)


def _sum4(parts, land, s_arr, rb, name):
    _, H, C = land.shape

    def body(s_ref, own_ref, r_ref, o_ref):
        own = own_ref[0].astype(F32)
        tot = jnp.zeros((rb, C), F32)
        for j in range(4):
            tot = tot + jnp.where(s_ref[0] == j, own, r_ref[j].astype(F32))
        o_ref[...] = tot

    return pl.pallas_call(
        body, name=name,
        grid_spec=pltpu.PrefetchScalarGridSpec(
            num_scalar_prefetch=1, grid=(H // rb,),
            in_specs=[pl.BlockSpec((1, rb, C), lambda i, s_ref: (s_ref[0], i, 0)),
                      pl.BlockSpec((4, rb, C), lambda i, s_ref: (0, i, 0))],
            out_specs=pl.BlockSpec((rb, C), lambda i, s_ref: (i, 0))),
        out_shape=jax.ShapeDtypeStruct((H, C), F32), compiler_params=_cp("parallel"),
    )(s_arr, parts, land)


HBM_SPEC = pl.BlockSpec(memory_space=pltpu.HBM)
SEM_SPEC = pl.BlockSpec(memory_space=pltpu.SEMAPHORE)
EFFECT = pltpu.SideEffectType.DATAFLOW_SIDE_EFFECTING


def _split_start(name, bufs, n_sem, plan):
    nb = len(bufs)

    def body(*refs):
        ins, send, recv, token = refs[:nb], refs[nb], refs[nb + 1], refs[-1]
        for i, (src, dst, dev, _) in enumerate(plan(ins)):
            pltpu.make_async_remote_copy(src, dst, send.at[i], recv.at[i], device_id=dev, device_id_type=MESH).start()
        token[...] = jnp.zeros_like(token)

    outs = pl.pallas_call(
        body, name=name,
        out_shape=(pltpu.SemaphoreType.DMA((n_sem,)), pltpu.SemaphoreType.DMA((n_sem,)),
                   *[pltpu.HBM(b.shape, b.dtype) for b in bufs], jax.ShapeDtypeStruct((8, 128), F32)),
        in_specs=[HBM_SPEC] * nb,
        out_specs=(SEM_SPEC, SEM_SPEC, *([HBM_SPEC] * nb), pl.BlockSpec(memory_space=pltpu.VMEM)),
        input_output_aliases={i: 2 + i for i in range(nb)},
        compiler_params=pltpu.CompilerParams(has_side_effects=EFFECT),
    )(*[pltpu.with_memory_space_constraint(b, pltpu.HBM) for b in bufs])
    return outs[0], outs[1], list(outs[2:2 + nb]), outs[-1]


def _split_wait(name, send, recv, bufs, after, plan):
    nb = len(bufs)

    def body(*refs):
        ins, send_s, recv_s = refs[:nb], refs[nb], refs[nb + 1]
        for i, (src, dst, dev, mine) in enumerate(plan(ins)):
            pltpu.make_async_remote_copy(src, dst, send_s.at[i], recv_s.at[i], device_id=dev,
                                         device_id_type=MESH).wait_send()
            pltpu.make_async_remote_copy(src, mine, send_s.at[i], recv_s.at[i], device_id=dev,
                                         device_id_type=MESH).wait_recv()

    outs = pl.pallas_call(
        body, name=name, out_shape=[pltpu.HBM(b.shape, b.dtype) for b in bufs],
        in_specs=[HBM_SPEC] * nb + [SEM_SPEC, SEM_SPEC, pl.BlockSpec(memory_space=pl.ANY)],
        out_specs=[HBM_SPEC] * nb, input_output_aliases={i: i for i in range(nb)},
        compiler_params=pltpu.CompilerParams(has_side_effects=EFFECT),
    )(*bufs, send, recv, after)
    return list(outs)


def _copies_now(name, bufs, n_sem, plan):
    nb = len(bufs)

    def body(*refs):
        ins, token, send, recv = refs[:nb], refs[2 * nb], refs[-2], refs[-1]
        token[...] = jnp.zeros_like(token)
        todo = plan(ins)
        for i, (src, dst, dev, _) in enumerate(todo):
            pltpu.make_async_remote_copy(src, dst, send.at[i], recv.at[i], device_id=dev, device_id_type=MESH).start()
        for i, (src, dst, dev, mine) in enumerate(todo):
            pltpu.make_async_remote_copy(src, mine, send.at[i], recv.at[i], device_id=dev, device_id_type=MESH).wait_recv()
        for i, (src, dst, dev, _) in enumerate(todo):
            pltpu.make_async_remote_copy(src, dst, send.at[i], recv.at[i], device_id=dev, device_id_type=MESH).wait_send()

    outs = pl.pallas_call(
        body, name=name,
        out_shape=[pltpu.HBM(b.shape, b.dtype) for b in bufs] + [jax.ShapeDtypeStruct((8, 128), F32)],
        in_specs=[HBM_SPEC] * nb, out_specs=[HBM_SPEC] * nb + [pl.BlockSpec(memory_space=pltpu.VMEM)],
        input_output_aliases={i: i for i in range(nb)},
        scratch_shapes=[pltpu.SemaphoreType.DMA((n_sem,)), pltpu.SemaphoreType.DMA((n_sem,))],
    )(*[pltpu.with_memory_space_constraint(b, pltpu.HBM) for b in bufs])
    return list(outs[:nb]), outs[nb]


def _slot(land, s, rows, cols):
    if cols is None:
        return land.at[s, rows]
    return land.at[rows, pl.ds(pl.multiple_of(s * cols, 128), cols)]


def _plan_gather_ici(cols):
    nw = len(cols)

    def plan(refs):
        x, y, c = _pos()
        my_s = 2 * x + y
        out = []
        for w in range(nw):
            mine = _half_rows(refs[w], c)
            for px, py, ps in _chips():
                out.append((refs[w].at[mine], _slot(refs[nw + w], my_s, mine, cols[w]), (px, py, c),
                            _slot(refs[nw + w], ps, mine, cols[w])))
        return out
    return plan


def _plan_gather_fwd(cols, rows):
    def plan(refs):
        x, y, c = _pos()
        out = []
        for w in range(len(cols)):
            half = rows[w] // 2
            mine = pl.ds(pl.multiple_of(c * half, 8), half)
            other = pl.ds(pl.multiple_of((1 - c) * half, 8), half)
            for px, py, ps in _chips():
                got = _slot(refs[w], ps, mine, cols[w])
                out.append((got, got, (x, y, 1 - c), _slot(refs[w], ps, other, cols[w])))
        return out
    return plan


def _plan_swap(nw):
    def plan(refs):
        x, y, c = _pos()
        return [(refs[w].at[:, _half_rows(refs[w], 1 - c)], refs[nw + w], (x, y, 1 - c), refs[nw + w])
                for w in range(nw)]
    return plan


def _plan_scatter(nw):
    def plan(refs):
        x, y, c = _pos()
        my_s = 2 * x + y
        out = []
        for w in range(nw):
            for px, py, ps in _chips():
                out.append((refs[w].at[ps], refs[nw + w].at[my_s], (px, py, c), refs[nw + w].at[ps]))
        return out
    return plan


def _plan_join(nw):
    def plan(refs):
        x, y, c = _pos()
        out = []
        for w in range(nw):
            land = refs[nw + w]
            out.append((refs[w], land.at[_half_rows(land, c)], (x, y, 1 - c), land.at[_half_rows(land, 1 - c)]))
        return out
    return plan


def _hbm_empty(shape, dtype):
    return pltpu.with_memory_space_constraint(lax.empty(shape, dtype), pltpu.HBM)


def _put_slot(land, own, slot):
    return lax.dynamic_update_slice(land, own[None], (slot,) + (0,) * own.ndim)


def _pad_lanes(a, n):
    return jnp.pad(a, ((0, 0), (0, n - a.shape[1])))


def kernel(x, c, positions, w_ada, b_ada, norm1_w, w_in, conv_w, conv_b, dt_bias, a_log, d_skip, attn_sinks, ssm_norm_w, w_out, norm2_w, w_gate_up, w_down, final_norm_w, loss_target, m_w_ada, m_b_ada, m_norm1_w, m_w_in, m_conv_w, m_conv_b, m_dt_bias, m_a_log, m_d_skip, m_attn_sinks, m_ssm_norm_w, m_w_out, m_norm2_w, m_w_gate_up, m_w_down, m_final_norm_w, v_w_ada, v_b_ada, v_norm1_w, v_w_in, v_conv_w, v_conv_b, v_dt_bias, v_a_log, v_d_skip, v_attn_sinks, v_ssm_norm_w, v_w_out, v_norm2_w, v_w_gate_up, v_w_down, v_final_norm_w):
    T = x.shape[1]
    tm = min(256, T)
    xi, yi, ci = lax.axis_index("x"), lax.axis_index("y"), lax.axis_index("c")
    my_s = 2 * xi + yi
    xs = x[0]
    tgt = loss_target[0]

    payload = jnp.concatenate([c, conv_w[0].reshape(1, CONVK * 256)], axis=1)
    gat, mod4, tok = _mod_exchange(payload, w_ada[0], b_ada.reshape(4, 1, 1536))
    mod6 = mod4.reshape(6, D)
    c_all = gat[:, 0, 0:D]
    cw_dev = gat[:, 0, D:].reshape(4, 2, CONVK, 256)[:, 0]
    conv_full = cw_dev.transpose(1, 0, 2).reshape(CONVK, CONVC)

    w_in_b = (w_in[0] + tok[0, 0]).astype(BF16)
    s_i, r_i, bufs, tok = _split_start("wgather_in_ici_start", [w_in_b, _hbm_empty((4,) + w_in_b.shape, BF16)], 3,
                                       _plan_gather_ici([None]))
    inv_freq = (10000.0 ** (-jnp.arange(32, dtype=F32) / 32))
    cos, sin_s = _rope_tables(positions, inv_freq.reshape(32, 1) + tok[0:1, 0:1], min(512, T))
    bufs = _split_wait("wgather_in_ici_wait", s_i, r_i, bufs, cos, _plan_gather_ici([None]))
    bufs, tok = _copies_now("wgather_in_fwd", bufs[1:], 3, _plan_gather_fwd([None], [D]))
    g_in = _put_slot(bufs[0], w_in_b, my_s)
    w_pad = jnp.concatenate([g_in[0], g_in[1], g_in[2], g_in[3], jnp.zeros((D, IN_PAD - IN_PROJ), BF16)], axis=1)

    late = [(w_out[0] + tok[0, 0]).astype(BF16), w_gate_up[0].astype(BF16), w_down[0].astype(BF16)]
    lands = [_hbm_empty((4, D // 4, D), BF16), _hbm_empty((D, 2 * DFF), BF16), _hbm_empty((4, DFF // 4, D), BF16)]
    cols3, rows3 = [None, GU_SH, None], [D // 4, D, DFF // 4]
    s_a, r_a, bufs, tok = _split_start("wgather_ici_start", late + lands, 9, _plan_gather_ici(cols3))

    qkv, z, xbc, dtr, h1b = _in_proj_fwd(xs, cos, sin_s, mod6 + tok[0, 0], norm1_w, w_pad, min(512, T))
    sinks = attn_sinks
    attn, lse = _attn_fwd(qkv, sinks)
    bufs = _split_wait("wgather_ici_wait", s_a, r_a, bufs, attn, _plan_gather_ici(cols3))
    s_b, r_b, lands, tok = _split_start("wgather_fwd_start", bufs[3:], 9, _plan_gather_fwd(cols3, rows3))
    dtb = _pad_lanes(dt_bias, 128)
    alog = _pad_lanes(a_log, 128)
    dskx = jnp.repeat(d_skip, HD, axis=1)
    mats = _ssd_mats()
    ynorm, ypre, states, conv_pre = _ssd_fwd(xbc, z, dtr, conv_full, conv_b, dtb + tok[0, 0], alog, dskx, ssm_norm_w,
                                             mats)
    lands = _split_wait("wgather_fwd_wait", s_b, r_b, lands, ynorm, _plan_gather_fwd(cols3, rows3))
    w_out_f = _put_slot(lands[0], late[0], my_s).reshape(D, D)
    w_dn_f = _put_slot(lands[2], late[2], my_s).reshape(DFF, D)
    s_arr = my_s.reshape(1).astype(jnp.int32)

    fw2 = final_norm_w.reshape(1, D)
    sq, dmix, dx1, h2b, act, dfb, dgu, dob, sm_ffn = _mix_ffn(
        xs, attn, ynorm, tgt, mod6, norm2_w, fw2, w_out_f, lands[1], late[1], s_arr, w_dn_f, tm)

    tt = min(2048, T)
    c_arr = ci.reshape(1).astype(jnp.int32)
    tok0 = jnp.zeros((8, 128), F32)
    gw_dn4 = _tn_matmul(act, dfb, GU_SH, D, tt, "dw_down", tok0).reshape(4, DFF // 4, D)
    gw_gu4 = _tn_matmul(h2b, dgu, D, GU_SH, tt, "dw_gate_up", tok0)
    gw_out4 = jnp.concatenate(
        [_tn_matmul(attn, dob, AW, D, tt, "dw_out_a", tok0)[0],
         _tn_matmul(ynorm, dob, SW, D, tt, "dw_out_y", tok0)[0]], axis=0).reshape(4, D // 4, D)
    big1 = [gw_out4, gw_gu4, gw_dn4]
    rbs1 = [128, 512, 352]
    sib1 = [_hbm_empty((4, g.shape[1] // 2, g.shape[2]), F32) for g in big1]
    s_c, r_c, bufs, tok = _split_start("gswap_start", big1 + sib1, 3, _plan_swap(3))

    dzxd, d_cw, d_cb, d_sw, d_sk, d_dtb, d_av = _ssd_bwd(
        xbc, conv_pre, z, dtr, ypre, states, dmix, conv_full, dtb + tok[0, 0], alog, dskx, ssm_norm_w, mats)
    bufs = _split_wait("gswap_wait", s_c, r_c, bufs, dzxd, _plan_swap(3))
    sums1 = [_add_half(g, s, c_arr, rb, "grad_add_%d" % i)
             for i, (g, s, rb) in enumerate(zip(bufs[:3], bufs[3:], rbs1))]
    land1 = [_hbm_empty(p.shape, BF16) for p in sums1]
    s_d, r_d, bufs, tok = _split_start("gscatter_start", sums1 + land1, 9, _plan_scatter(3))
    dqkv, d_sinks = _attn_bwd(qkv, sinks + tok[0:1, 0:8], lse, dmix, cos, sin_s)
    bufs = _split_wait("gscatter_wait", s_d, r_d, bufs, dqkv, _plan_scatter(3))
    halves1 = [_sum4(p, l, s_arr, rb, "grad_sum_%d" % i)
               for i, (p, l, rb) in enumerate(zip(bufs[:3], bufs[3:], rbs1))]
    full1 = [_hbm_empty((2 * h.shape[0], h.shape[1]), F32) for h in halves1]
    s_e, r_e, bufs, tok = _split_start("gjoin_start", halves1 + full1, 3, _plan_join(3))
    gq = _tn_matmul(h1b, dqkv, D, 768, tt, "dw_in_qkv", tok)[0]
    gz = _tn_matmul(h1b, dzxd, D, 1664, tt, "dw_in_zxd", tok)[0]
    gw_in4 = jnp.stack([gq[:, :IN_SH], jnp.concatenate([gq[:, IN_SH:], gz[:, :2 * IN_SH - 768]], axis=1),
                        gz[:, 2 * IN_SH - 768:3 * IN_SH - 768], gz[:, 3 * IN_SH - 768:4 * IN_SH - 768]])
    joined1 = _split_wait("gjoin_wait", s_e, r_e, bufs, gw_in4, _plan_join(3))

    sib0 = _hbm_empty((4, D // 2, IN_SH), F32)
    bufs, _ = _copies_now("gswap_in", [gw_in4, sib0], 1, _plan_swap(1))
    sum0 = _add_half(bufs[0], bufs[1], c_arr, 512, "grad_add_in")
    s_g, r_g, bufs, tok = _split_start("gscatter_in_start", [sum0, _hbm_empty(sum0.shape, BF16)], 3, _plan_scatter(1))
    grad_x, sm_in = _in_proj_bwd(xs, dx1, dqkv, dzxd, mod6 + tok[0, 0], norm1_w, w_pad, min(512, T))
    bufs = _split_wait("gscatter_in_wait", s_g, r_g, bufs, grad_x, _plan_scatter(1))
    half0 = _sum4(bufs[0], bufs[1], s_arr, 512, "grad_sum_in")
    joined0, _ = _copies_now("gjoin_in", [half0, _hbm_empty((D, IN_SH), F32)], 1, _plan_join(1))

    a_neg = -jnp.exp(alog)
    pieces = [sm_in[1:2], sm_in[2:3], sm_ffn[5:6], sm_ffn[2:3], sm_ffn[3:4], sm_ffn[4:5],
              sm_in[0:1], sm_ffn[1:2], sm_ffn[0:1], d_cb, d_cw.reshape(1, CONVK * CONVC),
              _pad_lanes(d_sw, SW), d_dtb, d_av * a_neg, d_sk, d_sinks,
              _pad_lanes((0.5 / D * jnp.sum(sq)).reshape(1, 1), 128)]
    vec = jnp.concatenate(pieces, axis=1)
    tot, allv = _small_reduce(vec)
    o = 0
    offs = []
    for p in pieces:
        offs.append(o)
        o += p.shape[1]
    seg = lambda i, n: tot[:, offs[i]:offs[i] + n]
    g_b_ada = tot[:, 0:6 * D]
    g_norm1, g_norm2, g_final, g_conv_b = seg(6, D), seg(7, D), seg(8, D), seg(9, D)
    g_conv_w = lax.dynamic_slice_in_dim(seg(10, CONVK * CONVC).reshape(CONVK, CONVC), my_s * 256, 256, axis=1)
    g_ssm_w, g_dtb, g_alog, g_dsk, g_sink = seg(11, SW), seg(12, 8), seg(13, 8), seg(14, 8), seg(15, 8)
    loss = tot[0, offs[16]]

    small_names = ["b_ada", "norm1_w", "conv_w", "conv_b", "dt_bias", "a_log", "d_skip", "attn_sinks", "ssm_norm_w",
                   "norm2_w", "final_norm_w"]
    small_g = [g_b_ada, g_norm1, g_conv_w, g_conv_b, g_dtb, g_alog, g_dsk, g_sink, g_ssm_w, g_norm2, g_final]
    as2d = lambda a: a.reshape(-1, a.shape[-1])
    small_w = [as2d(a) for a in (b_ada, norm1_w, conv_w, conv_b, dt_bias, a_log, d_skip, attn_sinks, ssm_norm_w,
                                 norm2_w, final_norm_w)]
    small_m = [as2d(a) for a in (m_b_ada, m_norm1_w, m_conv_w, m_conv_b, m_dt_bias, m_a_log, m_d_skip, m_attn_sinks,
                                 m_ssm_norm_w, m_norm2_w, m_final_norm_w)]
    small_v = [as2d(a) for a in (v_b_ada, v_norm1_w, v_conv_w, v_conv_b, v_dt_bias, v_a_log, v_d_skip, v_attn_sinks,
                                 v_ssm_norm_w, v_norm2_w, v_final_norm_w)]
    small_g, sd, smn, svn = _adam_small(small_g, small_w, small_m, small_v)

    sc_all = c_all * jax.nn.sigmoid(c_all)
    dmod_all = allv[:, 0, 0:6 * D]
    dmod_s = lax.dynamic_slice_in_dim(dmod_all, my_s * 1536, 1536, axis=1)
    g_ada, d_ada, m_ada, v_ada = _adam_w_ada(sc_all, dmod_s, w_ada[0], m_w_ada[0], v_w_ada[0], 256)
    native = lambda a: a.transpose(2, 0, 1)
    g_in_s, d_in, m_in, v_in = [a.transpose(1, 2, 0) for a in _adam_w_in(
        native(w_in), joined0[0], joined0[1], native(m_w_in), native(v_w_in), c_arr)]
    g_out_s, d_out, m_out, v_out = _adam_2d(w_out[0], joined1[0], joined1[3], m_w_out[0], v_w_out[0], c_arr, 128,
                                            "adam_w_out")
    g_gu_s, d_gu, m_gu, v_gu = _adam_2d(w_gate_up[0], joined1[1], joined1[4], m_w_gate_up[0], v_w_gate_up[0], c_arr,
                                        256, "adam_w_gate_up")
    g_dn_s, d_dn, m_dn, v_dn = _adam_2d(w_down[0], joined1[2], joined1[5], m_w_down[0], v_w_down[0], c_arr, 352,
                                        "adam_w_down")

    order = ["w_ada", "b_ada", "norm1_w", "w_in", "conv_w", "conv_b", "dt_bias", "a_log", "d_skip", "attn_sinks",
             "ssm_norm_w", "w_out", "norm2_w", "w_gate_up", "w_down", "final_norm_w"]
    shapes = dict(w_ada=w_ada.shape, b_ada=b_ada.shape, norm1_w=norm1_w.shape, w_in=w_in.shape, conv_w=conv_w.shape,
                  conv_b=conv_b.shape, dt_bias=dt_bias.shape, a_log=a_log.shape, d_skip=d_skip.shape,
                  attn_sinks=attn_sinks.shape, ssm_norm_w=ssm_norm_w.shape, w_out=w_out.shape, norm2_w=norm2_w.shape,
                  w_gate_up=w_gate_up.shape, w_down=w_down.shape, final_norm_w=final_norm_w.shape)
    grads = dict(w_ada=g_ada, w_in=g_in_s, w_out=g_out_s, w_gate_up=g_gu_s, w_down=g_dn_s)
    deltas = dict(w_ada=d_ada, w_in=d_in, w_out=d_out, w_gate_up=d_gu, w_down=d_dn)
    new_m = dict(w_ada=m_ada, w_in=m_in, w_out=m_out, w_gate_up=m_gu, w_down=m_dn)
    new_v = dict(w_ada=v_ada, w_in=v_in, w_out=v_out, w_gate_up=v_gu, w_down=v_dn)
    for i, nme in enumerate(small_names):
        grads[nme], deltas[nme], new_m[nme], new_v[nme] = small_g[i], sd[i], smn[i], svn[i]
    outs = [loss, grad_x[None]]
    for table in (grads, deltas, new_m, new_v):
        outs += [table[nme].reshape(shapes[nme]) for nme in order]
    return tuple(outs)
```

```python
import functools
import math

import jax
import jax.numpy as jnp
from jax import lax
from jax.experimental import pallas as pl
from jax.experimental.pallas import tpu as pltpu

F32 = jnp.float32
BF16 = jnp.bfloat16
HI = lax.Precision.HIGHEST
MESH = pl.DeviceIdType.MESH

D = 1024
HD = 64
AW = 512
SW = 512
NST = 128
CONVK = 4
CONVC = 1024
BLK = 128
CPS = 4
IN_PROJ = 2312
IN_PAD = 2432
IN_SH = IN_PROJ // 4
DFF = 2816
GU_SH = 1408
FF_SPLITS = ((0, 1536), (1536, 2816))
EPS = 1e-6
NEG = -1e30
LR, B1, B2, AEPS, WD, STEP = 0.001, 0.9, 0.999, 1e-08, 0.01, 10
VMEM_LIMIT = 58 * 1024 * 1024


def _cp(*sem):
    return pltpu.CompilerParams(dimension_semantics=sem or None, vmem_limit_bytes=VMEM_LIMIT)


def _dot(a, b):
    return jnp.dot(a, b, preferred_element_type=F32)


def _dot_nt(a, b):
    return lax.dot_general(a, b, (((1,), (1,)), ((), ())), preferred_element_type=F32)


def _dot_tn(a, b):
    return lax.dot_general(a, b, (((0,), (0,)), ((), ())), preferred_element_type=F32)


def _dot_hi(a, b):
    return jnp.dot(a, b, precision=HI, preferred_element_type=F32)


def _sigmoid(x):
    return 1.0 / (1.0 + jnp.exp(-x))


def _iota(shape, dim):
    return lax.broadcasted_iota(jnp.int32, shape, dim)


def _load_resident(hbm_ref, vmem_ref, sem):
    @pl.when(pl.program_id(0) == 0)
    def _():
        cp = pltpu.make_async_copy(hbm_ref, vmem_ref, sem)
        cp.start()
        cp.wait()


def _swap32(t):
    lane = _iota(t.shape, 1)
    return jnp.where((lane & 63) < 32, pltpu.roll(t, 96, 1), pltpu.roll(t, 32, 1))


def _rope_fwd(t, cos, sin_s):
    return t * cos + _swap32(t) * sin_s


def _rope_bwd(t, cos, sin_s):
    return t * cos - _swap32(t) * sin_s


def _rope_tables(pos_row, inv_freq_col, tm):
    T = pos_row.shape[1]
    lane, row = jnp.arange(128)[None, :], jnp.arange(96)[:, None]
    pick = (lane % 32) == (row % 32)
    sel_cos = pick.astype(BF16)
    sel_sin = jnp.where(pick, jnp.where(lane % 64 < 32, -1.0, 1.0), 0.0).astype(BF16)

    def body(p_ref, f_ref, sc_ref, ss_ref, cos_ref, sin_ref):
        ang = f_ref[...] * p_ref[...].astype(F32)
        cos_ref[...] = _dot_tn(_pieces(jnp.cos(ang), 3, 0), sc_ref[...])
        sin_ref[...] = _dot_tn(_pieces(jnp.sin(ang), 3, 0), ss_ref[...])

    full = lambda a: pl.BlockSpec(a.shape, lambda i: (0,) * a.ndim)
    return pl.pallas_call(
        body, name="rope_tables", grid=(T // tm,),
        in_specs=[pl.BlockSpec((1, tm), lambda i: (0, i)), full(inv_freq_col), full(sel_cos), full(sel_sin)],
        out_specs=[pl.BlockSpec((tm, 128), lambda i: (i, 0))] * 2,
        out_shape=[jax.ShapeDtypeStruct((T, 128), F32)] * 2,
        compiler_params=_cp("parallel"),
    )(pos_row, inv_freq_col, sel_cos, sel_sin)


def _in_proj_fwd(x, cos, sin_s, mod6, norm1_w, w_pad, tm):
    T = x.shape[0]

    def body(x_ref, cos_ref, sin_ref, mod_ref, nw_ref, w_hbm, qkv_ref, z_ref, xbc_ref, dt_ref, h_ref, w_vmem, sem):
        _load_resident(w_hbm, w_vmem, sem)
        xv = x_ref[...]
        r = lax.rsqrt(jnp.mean(xv * xv, axis=-1, keepdims=True) + EPS)
        h = (xv * r * nw_ref[...]) * (1.0 + mod_ref[1:2, :]) + mod_ref[0:1, :]
        hb = h.astype(BF16)
        h_ref[...] = hb
        proj = _dot(hb, w_vmem[...])
        cs, sn = cos_ref[...], sin_ref[...]
        for j in range(5):
            qkv_ref[:, 128 * j:128 * (j + 1)] = _rope_fwd(proj[:, 128 * j:128 * (j + 1)], cs, sn).astype(BF16)
        qkv_ref[:, 640:768] = proj[:, 640:768].astype(BF16)
        z_ref[...] = proj[:, 768:1280]
        xbc_ref[...] = proj[:, 1280:2304]
        dt_ref[...] = proj[:, 2304:2432]

    row = lambda w: pl.BlockSpec((tm, w), lambda i: (i, 0))
    full = lambda a: pl.BlockSpec(a.shape, lambda i: (0,) * a.ndim)
    return pl.pallas_call(
        body, name="in_proj_fwd", grid=(T // tm,),
        in_specs=[row(D), row(128), row(128), full(mod6), full(norm1_w), pl.BlockSpec(memory_space=pl.ANY)],
        out_specs=[row(768), row(512), row(1024), row(128), row(D)],
        out_shape=[jax.ShapeDtypeStruct((T, 768), BF16), jax.ShapeDtypeStruct((T, 512), F32),
                   jax.ShapeDtypeStruct((T, 1024), F32), jax.ShapeDtypeStruct((T, 128), F32),
                   jax.ShapeDtypeStruct((T, D), BF16)],
        scratch_shapes=[pltpu.VMEM((D, IN_PAD), BF16), pltpu.SemaphoreType.DMA],
        compiler_params=_cp("arbitrary"),
    )(x, cos, sin_s, mod6, norm1_w, w_pad)


def _head_variants(pair, j):
    lane = _iota(pair.shape, 1)
    lo = lane < 64
    kv = j // 2
    ev = jnp.where(lo, pair, 0.0)
    od = jnp.where(lo, 0.0, pair)
    if kv == 0:
        od = pltpu.roll(od, 64, 1)
    else:
        ev = pltpu.roll(ev, 64, 1)
    return ev.astype(BF16), od.astype(BF16)


def _kv_variants(vcat):
    lane = _iota(vcat.shape, 1)
    lo = lane < 64
    v0 = jnp.where(lo, vcat, 0.0)
    v1 = jnp.where(lo, 0.0, vcat)
    out = {
        (0, 0): v0, (0, 1): pltpu.roll(v0, 64, 1),
        (1, 0): pltpu.roll(v1, 64, 1), (1, 1): v1,
    }
    return {k: v.astype(BF16) for k, v in out.items()}


def _fold_masks(n):
    upper = _iota((BLK, BLK), 1) > _iota((BLK, BLK), 0)
    return upper, upper & (n == 0)


def _attn_fwd(qkv, sinks):
    T = qkv.shape[0]
    nsteps = T // (CPS * BLK)

    def body(sink_ref, q_ref, kc_ref, kp_ref, vc_ref, vp_ref, o_ref, lse_ref):
        for sub in range(CPS):
            rows, before = slice(BLK * sub, BLK * (sub + 1)), slice(BLK * (sub - 1), BLK * sub)
            block(pl.program_id(0) * CPS + sub, sink_ref, q_ref.at[rows, :], kc_ref.at[rows, :],
                  kp_ref if sub == 0 else kc_ref.at[before, :], vc_ref.at[rows, :],
                  vp_ref if sub == 0 else vc_ref.at[before, :], o_ref.at[rows, :], lse_ref.at[rows, :])

    def block(n, sink_ref, q_ref, kc_ref, kp_ref, vc_ref, vp_ref, o_ref, lse_ref):
        vpv = _kv_variants(vp_ref[...].astype(F32))
        vcv = _kv_variants(vc_ref[...].astype(F32))
        q_all = jnp.concatenate(
            [v for j in range(4) for v in _head_variants(q_ref[:, 128 * j:128 * (j + 1)].astype(F32), j)], axis=0)
        s_prev = _dot_nt(q_all, kp_ref[...])
        s_cur = _dot_nt(q_all, kc_ref[...])
        upper, dead = _fold_masks(n)
        lane = _iota((BLK, 128), 1)
        lse_acc = jnp.zeros((BLK, 128), F32)
        for jj in range(4):
            acc = jnp.zeros((BLK, 128), F32)
            for par in range(2):
                h = 2 * jj + par
                rows = slice(h * BLK, (h + 1) * BLK)
                sink = sink_ref[0, h]
                s = jnp.where(dead, NEG, jnp.where(upper, s_prev[rows], s_cur[rows]) * 0.125)
                m = jnp.maximum(jnp.max(s, axis=1, keepdims=True), sink)
                p = jnp.exp(s - m)
                den = jnp.sum(p, axis=1, keepdims=True) + jnp.exp(sink - m)
                pn = p * (1.0 / den)
                acc = (acc + _dot(jnp.where(upper, pn, 0.0).astype(BF16), vpv[(jj // 2, par)])
                       + _dot(jnp.where(upper, 0.0, pn).astype(BF16), vcv[(jj // 2, par)]))
                lse_acc = jnp.where(lane == h, m + jnp.log(den), lse_acc)
            o_ref[:, 128 * jj:128 * (jj + 1)] = acc.astype(BF16)
        lse_ref[...] = lse_acc

    RB = CPS * BLK
    prev = lambda n: jnp.maximum(n * CPS - 1, 0)
    return pl.pallas_call(
        body, name="attn_fwd", grid=(nsteps,),
        in_specs=[pl.BlockSpec(memory_space=pltpu.SMEM),
                  pl.BlockSpec((RB, 512), lambda n: (n, 0)),
                  pl.BlockSpec((RB, 128), lambda n: (n, 4)),
                  pl.BlockSpec((BLK, 128), lambda n: (prev(n), 4)),
                  pl.BlockSpec((RB, 128), lambda n: (n, 5)),
                  pl.BlockSpec((BLK, 128), lambda n: (prev(n), 5))],
        out_specs=[pl.BlockSpec((RB, 512), lambda n: (n, 0)), pl.BlockSpec((RB, 128), lambda n: (n, 0))],
        out_shape=[jax.ShapeDtypeStruct((T, 512), BF16), jax.ShapeDtypeStruct((T, 128), F32)],
        compiler_params=_cp("parallel"),
    )(sinks, qkv, qkv, qkv, qkv, qkv)


def _attn_bwd(qkv, sinks, lse, dmix, cos, sin_s):
    T = qkv.shape[0]
    nb = T // BLK

    def body(sink_ref, q_ref, kc_ref, kp_ref, vc_ref, vp_ref, lse_ref, do_ref, cq_ref, sq_ref, ck_ref, sk_ref,
             out_ref, ds_ref, dq_car, dk_car, dv_car):
        n = pl.program_id(0)
        lane = _iota((BLK, 128), 1)

        @pl.when(n == 0)
        def _():
            ds_ref[...] = jnp.zeros_like(ds_ref)
            dq_car[...] = jnp.zeros_like(dq_car)
            dk_car[...] = jnp.zeros_like(dk_car)
            dv_car[...] = jnp.zeros_like(dv_car)

        @pl.when(n < nb)
        def _():
            kp, kc, vp, vc = kp_ref[...], kc_ref[...], vp_ref[...], vc_ref[...]
            kpv = _kv_variants(kp.astype(F32))
            kcv = _kv_variants(kc.astype(F32))
            lse_v = lse_ref[...]
            q_all = jnp.concatenate(
                [v for j in range(4) for v in _head_variants(q_ref[:, 128 * j:128 * (j + 1)].astype(F32), j)], axis=0)
            do_all = jnp.concatenate(
                [v for j in range(4) for v in _head_variants(do_ref[:, 128 * j:128 * (j + 1)], j)], axis=0)
            s_prev, s_cur = _dot_nt(q_all, kp), _dot_nt(q_all, kc)
            dp_prev, dp_cur = _dot_nt(do_all, vp), _dot_nt(do_all, vc)
            upper, dead = _fold_masks(n)
            out_ref[:, 0:512] = dq_car[...]
            dsk = jnp.zeros((1, 128), F32)
            ds_u, ds_l, p_u, p_l = [], [], [], []
            for jj in range(4):
                dq_acc = jnp.zeros((BLK, 128), F32)
                for par in range(2):
                    h = 2 * jj + par
                    rows = slice(h * BLK, (h + 1) * BLK)
                    lse_h = jnp.sum(jnp.where(lane == h, lse_v, 0.0), axis=1, keepdims=True)
                    s = jnp.where(dead, NEG, jnp.where(upper, s_prev[rows], s_cur[rows]) * 0.125)
                    p = jnp.exp(s - lse_h)
                    dp = jnp.where(upper, dp_prev[rows], dp_cur[rows])
                    delta = jnp.sum(p * dp, axis=1, keepdims=True)
                    ds = p * (dp - delta) * 0.125
                    dsu, dsl = jnp.where(upper, ds, 0.0).astype(BF16), jnp.where(upper, 0.0, ds).astype(BF16)
                    dq_acc = dq_acc + _dot(dsu, kpv[(jj // 2, par)]) + _dot(dsl, kcv[(jj // 2, par)])
                    ds_u.append(dsu)
                    ds_l.append(dsl)
                    p_u.append(jnp.where(upper, p, 0.0).astype(BF16))
                    p_l.append(jnp.where(upper, 0.0, p).astype(BF16))
                    dsk = dsk + jnp.where(lane[0:1] == h, -jnp.sum(jnp.exp(sink_ref[0, h] - lse_h) * delta), 0.0)
                dq_car[:, 128 * jj:128 * (jj + 1)] = _rope_bwd(dq_acc, cq_ref[...], sq_ref[...]).astype(BF16)
            stack = lambda parts: jnp.concatenate(parts, axis=0)
            dk_prev, dk_cur = _dot_tn(stack(ds_u), q_all), _dot_tn(stack(ds_l), q_all)
            dv_prev, dv_cur = _dot_tn(stack(p_u), do_all), _dot_tn(stack(p_l), do_all)
            ds_ref[...] += dsk
            out_ref[:, 512:640] = _rope_bwd(dk_car[...] + dk_prev, ck_ref[...], sk_ref[...]).astype(BF16)
            out_ref[:, 640:768] = (dv_car[...] + dv_prev).astype(BF16)
            dk_car[...] = dk_cur
            dv_car[...] = dv_cur

        @pl.when(n == nb)
        def _():
            out_ref[:, 0:512] = dq_car[...]
            out_ref[:, 512:640] = _rope_bwd(dk_car[...], ck_ref[...], sk_ref[...]).astype(BF16)
            out_ref[:, 640:768] = dv_car[...].astype(BF16)

    cur = lambda n: jnp.minimum(n, nb - 1)
    prev = lambda n: jnp.maximum(cur(n) - 1, 0)
    outb = lambda n: jnp.maximum(n - 1, 0)
    return pl.pallas_call(
        body, name="attn_bwd", grid=(nb + 1,),
        in_specs=[pl.BlockSpec(memory_space=pltpu.SMEM),
                  pl.BlockSpec((BLK, 512), lambda n: (cur(n), 0)),
                  pl.BlockSpec((BLK, 128), lambda n: (cur(n), 4)),
                  pl.BlockSpec((BLK, 128), lambda n: (prev(n), 4)),
                  pl.BlockSpec((BLK, 128), lambda n: (cur(n), 5)),
                  pl.BlockSpec((BLK, 128), lambda n: (prev(n), 5)),
                  pl.BlockSpec((BLK, 128), lambda n: (cur(n), 0)),
                  pl.BlockSpec((BLK, 512), lambda n: (cur(n), 0)),
                  pl.BlockSpec((BLK, 128), lambda n: (cur(n), 0)),
                  pl.BlockSpec((BLK, 128), lambda n: (cur(n), 0)),
                  pl.BlockSpec((BLK, 128), lambda n: (outb(n), 0)),
                  pl.BlockSpec((BLK, 128), lambda n: (outb(n), 0))],
        out_specs=[pl.BlockSpec((BLK, 768), lambda n: (outb(n), 0)), pl.BlockSpec((1, 128), lambda n: (0, 0))],
        out_shape=[jax.ShapeDtypeStruct((T, 768), BF16), jax.ShapeDtypeStruct((1, 128), F32)],
        scratch_shapes=[pltpu.VMEM((BLK, 512), BF16), pltpu.VMEM((BLK, 128), F32), pltpu.VMEM((BLK, 128), F32)],
        compiler_params=_cp("arbitrary"),
    )(sinks, qkv, qkv, qkv, qkv, qkv, lse, dmix, cos, sin_s, cos, sin_s)


def _ssd_mats():
    e = jnp.arange(SW)[None, :] // HD == jnp.arange(128)[:, None]
    tri = jnp.arange(BLK)[None, :] <= jnp.arange(BLK)[:, None]
    return (jnp.tile(e, (3, 1)).astype(BF16), jnp.tile(e.T, (2, 1)).astype(BF16),
            jnp.tile(tri, (1, 3)).astype(BF16), jnp.tile(tri.T, (1, 3)).astype(BF16))


def _pieces(x, n, axis):
    out, r = [], x
    for i in range(n):
        p = r.astype(BF16)
        out.append(p)
        if i + 1 < n:
            r = r - p.astype(F32)
    return jnp.concatenate(out, axis=axis)


def _expand(x, e3):
    return _dot(_pieces(x, 3, 1), e3)


def _head_sums(x, et2):
    return _dot(_pieces(x, 2, 1), et2)


def _run_sum(tri3, x):
    return _dot(tri3, _pieces(x, 3, 0))


def _shift_down(u, tail, j):
    rolled = pltpu.roll(u, j, 0)
    first = jnp.where(_iota(tail.shape, 0) < j, pltpu.roll(tail, j, 0), rolled[0:8])
    return jnp.concatenate([first, rolled[8:]], axis=0)


def _shift_up(d, head, j):
    rolled = pltpu.roll(d, BLK - j, 0)
    last = jnp.where(_iota(head.shape, 0) >= 8 - j, pltpu.roll(head, 8 - j, 0), rolled[BLK - 8:])
    return jnp.concatenate([rolled[:BLK - 8], last], axis=0)


def _ssd_parts(dtr, dtb, alog, e3, tril3):
    xx = dtr + dtb
    dt = jnp.maximum(xx, 0.0) + jnp.log(1.0 + jnp.exp(-jnp.abs(xx)))
    a_neg = -jnp.exp(alog)
    tril = _iota((BLK, BLK), 1) <= _iota((BLK, BLK), 0)
    cs = _run_sum(tril3, dt * a_neg)
    csx = _expand(cs, e3)
    last = csx[BLK - 1:BLK, :]
    return dict(xx=xx, dt=dt, a_neg=a_neg, tril=tril, cs=cs, cs_t=cs.T,
                ecsx=jnp.exp(csx), dtex=jnp.exp(last - csx), cdx=jnp.exp(last), dtx=_expand(dt, e3))


def _decay(parts, h):
    seg = parts["cs"][:, h:h + 1] - parts["cs_t"][h:h + 1, :]
    return jnp.exp(jnp.where(parts["tril"], seg, NEG))


def _group_cols(a, g):
    return a[:, 256 * g:256 * (g + 1)]


def _ssd_fwd(xbc, z, dtr, conv_w, conv_b, dtb, alog, dskx, ssm_w, mats):
    T = xbc.shape[0]
    nc = T // BLK

    def body(u_ref, tail_ref, z_ref, dtr_ref, cw_ref, cb_ref, dtb_ref, al_ref, dk_ref, sw_ref, e3_ref, tril3_ref,
             yn_ref, yp_ref, st_ref, co_ref, s_scr):
        n = pl.program_id(0)

        @pl.when(n == 0)
        def _():
            s_scr[...] = jnp.zeros_like(s_scr)

        lane = _iota((BLK, 128), 1)
        lo = lane < 64
        for sub in range(CPS):
            rows = slice(BLK * sub, BLK * (sub + 1))
            u = u_ref[rows, :]
            tail = jnp.where(n > 0, tail_ref[...], 0.0) if sub == 0 else u_ref[BLK * sub - 8:BLK * sub, :]
            co = cb_ref[...] + cw_ref[3:4, :] * u
            for j in range(1, CONVK):
                co = co + cw_ref[3 - j:4 - j, :] * _shift_down(u, tail, j)
            co_ref[rows, :] = co
            xc = co * _sigmoid(co)
            pt = _ssd_parts(dtr_ref[rows, :], dtb_ref[...], al_ref[...], e3_ref[...], tril3_ref[...])
            xs = xc[:, :SW]
            bm = [xc[:, 512:640].astype(BF16), xc[:, 640:768].astype(BF16)]
            cm = [xc[:, 768:896].astype(BF16), xc[:, 896:1024].astype(BF16)]
            s_in = s_scr[...]
            st_ref[sub] = s_in
            xdt = xs * pt["dtx"]
            xde = (xdt * pt["dtex"]).astype(BF16)
            ys, s_new = [], []
            for g in range(2):
                cb = _dot_nt(cm[g], bm[g])
                yoff = _dot(cm[g], _group_cols(s_in, g).astype(BF16))
                s_new.append(_dot_tn(bm[g], _group_cols(xde, g)))
                for jj in range(2):
                    j = 2 * g + jj
                    chunk = xdt[:, 128 * j:128 * (j + 1)]
                    g_ev = (cb * _decay(pt, 2 * j)).astype(BF16)
                    g_od = (cb * _decay(pt, 2 * j + 1)).astype(BF16)
                    yd = (_dot(g_ev, jnp.where(lo, chunk, 0.0).astype(BF16))
                          + _dot(g_od, jnp.where(lo, 0.0, chunk).astype(BF16)))
                    ys.append(yd + yoff[:, 128 * jj:128 * (jj + 1)] * pt["ecsx"][:, 128 * j:128 * (j + 1)])
            y = jnp.concatenate(ys, axis=1) + xs * dk_ref[...]
            s_scr[...] = s_in * pt["cdx"] + jnp.concatenate(s_new, axis=1)
            yp_ref[rows, :] = y
            zv = z_ref[rows, :]
            yz = y * (zv * _sigmoid(zv))
            outs = []
            for g in range(2):
                yg = _group_cols(yz, g)
                outs.append(yg * lax.rsqrt(jnp.mean(yg * yg, axis=-1, keepdims=True) + EPS))
            yn_ref[rows, :] = (jnp.concatenate(outs, axis=1) * sw_ref[...]).astype(BF16)

    e3, _, tril3, _ = mats
    RB = CPS * BLK
    tail8 = lambda n: jnp.maximum(n * (RB // 8) - 1, 0)
    full = lambda a: pl.BlockSpec(a.shape, lambda n: (0,) * a.ndim)
    return pl.pallas_call(
        body, name="ssd_fwd", grid=(nc // CPS,),
        in_specs=[pl.BlockSpec((RB, CONVC), lambda n: (n, 0)), pl.BlockSpec((8, CONVC), lambda n: (tail8(n), 0)),
                  pl.BlockSpec((RB, SW), lambda n: (n, 0)), pl.BlockSpec((RB, 128), lambda n: (n, 0)),
                  full(conv_w), full(conv_b), full(dtb), full(alog), full(dskx), full(ssm_w), full(e3), full(tril3)],
        out_specs=[pl.BlockSpec((RB, SW), lambda n: (n, 0)), pl.BlockSpec((RB, SW), lambda n: (n, 0)),
                   pl.BlockSpec((CPS, NST, SW), lambda n: (n, 0, 0)), pl.BlockSpec((RB, CONVC), lambda n: (n, 0))],
        out_shape=[jax.ShapeDtypeStruct((T, SW), BF16), jax.ShapeDtypeStruct((T, SW), F32),
                   jax.ShapeDtypeStruct((nc, NST, SW), F32), jax.ShapeDtypeStruct((T, CONVC), F32)],
        scratch_shapes=[pltpu.VMEM((NST, SW), F32)],
        compiler_params=_cp("arbitrary"),
    )(xbc, xbc, z, dtr, conv_w, conv_b, dtb, alog, dskx, ssm_w, e3, tril3)


def _ssd_bwd(xbc, co_all, z, dtr, ypre, states, dmix, conv_w, dtb, alog, dskx, ssm_w, mats):
    T = xbc.shape[0]
    nsteps = T // (CPS * BLK)

    def body(*refs):
        per_chunk, consts, out_ref, carried = refs[:7], refs[7:16], refs[16], refs[17:]
        i = pl.program_id(0)

        @pl.when(i == 0)
        def _():
            for r in carried:
                r[...] = jnp.zeros_like(r)

        for sub in reversed(range(CPS)):
            rows = slice(BLK * sub, BLK * (sub + 1))
            views = [r.at[sub:sub + 1] if k == 5 else r.at[rows, :] for k, r in enumerate(per_chunk)]
            chunk(*views, *consts, out_ref.at[rows, :], *carried)

        @pl.when(i == nsteps - 1)
        def _():
            dsk_ref, dskx_scr = carried[3], carried[8]
            dsk_ref[...] = _head_sums(jnp.broadcast_to(dskx_scr[...], (8, SW)), consts[6][...])[0:1]

    def chunk(u_ref, co_ref, z_ref, dtr_ref, yp_ref, st_ref, dyn_ref, cw_ref, dtb_ref, al_ref, dk_ref, sw_ref,
              e3_ref, et2_ref, tril3_ref, triu3_ref,
              out_ref, dcw_ref, dcb_ref, dsw_ref, dsk_ref, ddtb_ref, dav_ref, ds_scr, dco_scr, dskx_scr):
        co = co_ref[...]
        sg = _sigmoid(co)
        xc = co * sg
        pt = _ssd_parts(dtr_ref[...], dtb_ref[...], al_ref[...], e3_ref[...], tril3_ref[...])
        dtx, ecsx, dtex, cdx = pt["dtx"], pt["ecsx"], pt["dtex"], pt["cdx"]
        xs = xc[:, :SW]
        bm = [xc[:, 512:640].astype(BF16), xc[:, 640:768].astype(BF16)]
        cm = [xc[:, 768:896].astype(BF16), xc[:, 896:1024].astype(BF16)]
        s_in = st_ref[0]
        ds_out = ds_scr[...]
        e_t = et2_ref[...]

        zv = z_ref[...]
        sz = _sigmoid(zv)
        silu_z = zv * sz
        ypre = yp_ref[...]
        yz = ypre * silu_z
        dyn = dyn_ref[...]
        sw = sw_ref[...]
        dyz, yns = [], []
        for g in range(2):
            yg = _group_cols(yz, g)
            r = lax.rsqrt(jnp.mean(yg * yg, axis=-1, keepdims=True) + EPS)
            yn = yg * r
            dg = _group_cols(dyn, g) * _group_cols(sw, g)
            dyz.append(r * (dg - yn * jnp.mean(dg * yn, axis=-1, keepdims=True)))
            yns.append(yn)
        dyz = jnp.concatenate(dyz, axis=1)
        dsw_ref[...] += jnp.sum(dyn * jnp.concatenate(yns, axis=1), axis=0, keepdims=True)
        dy = dyz * silu_z
        dz = dyz * ypre * (sz * (1.0 + zv * (1.0 - sz)))

        xdt = xs * dtx
        xdt_b = xdt.astype(BF16)
        edy = (ecsx * dy).astype(BF16)
        xde = (xdt * dtex).astype(BF16)
        lane = _iota((BLK, 128), 1)
        lo = lane < 64
        row8 = _iota((8, 128), 0)
        dcs = jnp.zeros((BLK, 128), F32)
        col_rows = jnp.zeros((8, 128), F32)
        dxdt, bds, yoff, dbs, dcs_g, ds_new = [], [], [], [], [], []
        for g in range(2):
            s_g = _group_cols(s_in, g).astype(BF16)
            dso_g = _group_cols(ds_out, g).astype(BF16)
            cb = _dot_nt(cm[g], bm[g])
            bds.append(_dot(bm[g], dso_g))
            yoff.append(_dot(cm[g], s_g))
            dcb_g = jnp.zeros((BLK, BLK), F32)
            for jj in range(2):
                j = 2 * g + jj
                dy_c = dy[:, 128 * j:128 * (j + 1)]
                xdt_c = xdt_b[:, 128 * j:128 * (j + 1)]
                acc = jnp.zeros((BLK, 128), F32)
                for par in range(2):
                    h = 2 * j + par
                    lm = _decay(pt, h)
                    gm = cb * lm
                    dy_m = (jnp.where(lo, dy_c, 0.0) if par == 0 else jnp.where(lo, 0.0, dy_c)).astype(BF16)
                    dg_h = _dot_nt(dy_m, xdt_c)
                    w_h = dg_h * gm
                    dcs = dcs + jnp.where(lane == h, jnp.sum(w_h, axis=1, keepdims=True), 0.0)
                    col_rows = col_rows + jnp.where(row8 == h, jnp.sum(w_h, axis=0, keepdims=True), 0.0)
                    dcb_g = dcb_g + dg_h * lm
                    acc = acc + _dot_tn(gm.astype(BF16), dy_m)
                dxdt.append(acc)
            dcb_b = dcb_g.astype(BF16)
            dcs_g.append(_dot(dcb_b, bm[g]) + _dot_nt(_group_cols(edy, g), s_g))
            dbs.append(_dot_tn(dcb_b, cm[g]) + _dot_nt(_group_cols(xde, g), dso_g))
            ds_new.append(_dot_tn(cm[g], _group_cols(edy, g)))
        bds = jnp.concatenate(bds, axis=1)
        yoff = jnp.concatenate(yoff, axis=1) * ecsx
        dxdt = jnp.concatenate(dxdt, axis=1) + dtex * bds
        ds_scr[...] = cdx * ds_out + jnp.concatenate(ds_new, axis=1)

        t_m = _head_sums(dtex * xdt * bds, e_t)
        colsum_t = jnp.concatenate([col_rows, jnp.zeros((BLK - 8, 128), F32)], axis=0).T
        cd = jnp.exp(pt["cs"][BLK - 1:BLK, :])
        sds = jnp.sum(s_in * ds_out, axis=0, keepdims=True)
        last_row = jnp.sum(t_m, axis=0, keepdims=True) + cd * _head_sums(jnp.broadcast_to(sds, (8, SW)), e_t)[0:1]
        dcs = dcs - colsum_t + _head_sums(dy * yoff, e_t) - t_m
        dcs = dcs + jnp.where(_iota((BLK, 128), 0) == BLK - 1, last_row, 0.0)
        da = _run_sum(triu3_ref[...], dcs)
        dt = pt["dt"]
        ddt = da * pt["a_neg"] + _head_sums(dxdt * xs, e_t)
        dav_ref[...] += jnp.sum(da * dt, axis=0, keepdims=True)
        ddtr = ddt * _sigmoid(pt["xx"])
        ddtb_ref[...] += jnp.sum(ddtr, axis=0, keepdims=True)
        dxs = dxdt * dtx + dy * dk_ref[...]
        dskx_scr[...] += jnp.sum(dy * xs, axis=0, keepdims=True)
        dxc = jnp.concatenate([dxs, dbs[0], dbs[1], dcs_g[0], dcs_g[1]], axis=1)
        dco = dxc * (sg * (1.0 + co * (1.0 - sg)))

        dcb_ref[...] += jnp.sum(dco, axis=0, keepdims=True)
        u = u_ref[...]
        head = dco_scr[...]
        du = jnp.zeros_like(dco)
        for j in range(CONVK):
            up_j = dco if j == 0 else _shift_up(dco, head, j)
            dcw_ref[3 - j:4 - j, :] += jnp.sum(up_j * u, axis=0, keepdims=True)
            du = du + cw_ref[3 - j:4 - j, :] * up_j
        dco_scr[...] = dco[0:8]
        out_ref[:, 0:512] = dz.astype(BF16)
        out_ref[:, 512:1536] = du.astype(BF16)
        out_ref[:, 1536:1664] = ddtr.astype(BF16)

    e3, et2, tril3, triu3 = mats
    RB = CPS * BLK
    rev = lambda i: nsteps - 1 - i
    full = lambda a: pl.BlockSpec(a.shape, lambda i: (0,) * a.ndim)
    acc = lambda r, c: pl.BlockSpec((r, c), lambda i: (0, 0))
    return pl.pallas_call(
        body, name="ssd_bwd", grid=(nsteps,),
        in_specs=[pl.BlockSpec((RB, CONVC), lambda i: (rev(i), 0)), pl.BlockSpec((RB, CONVC), lambda i: (rev(i), 0)),
                  pl.BlockSpec((RB, SW), lambda i: (rev(i), 0)), pl.BlockSpec((RB, 128), lambda i: (rev(i), 0)),
                  pl.BlockSpec((RB, SW), lambda i: (rev(i), 0)), pl.BlockSpec((CPS, NST, SW), lambda i: (rev(i), 0, 0)),
                  pl.BlockSpec((RB, SW), lambda i: (rev(i), 1)),
                  full(conv_w), full(dtb), full(alog), full(dskx), full(ssm_w),
                  full(e3), full(et2), full(tril3), full(triu3)],
        out_specs=[pl.BlockSpec((RB, 1664), lambda i: (rev(i), 0)),
                   acc(CONVK, CONVC), acc(1, CONVC), acc(1, SW), acc(1, 128), acc(1, 128), acc(1, 128)],
        out_shape=[jax.ShapeDtypeStruct((T, 1664), BF16),
                   jax.ShapeDtypeStruct((CONVK, CONVC), F32), jax.ShapeDtypeStruct((1, CONVC), F32),
                   jax.ShapeDtypeStruct((1, SW), F32), jax.ShapeDtypeStruct((1, 128), F32),
                   jax.ShapeDtypeStruct((1, 128), F32), jax.ShapeDtypeStruct((1, 128), F32)],
        scratch_shapes=[pltpu.VMEM((NST, SW), F32), pltpu.VMEM((8, CONVC), F32), pltpu.VMEM((1, SW), F32)],
        compiler_params=_cp("arbitrary"),
    )(xbc, co_all, z, dtr, ypre, states, dmix, conv_w, dtb, alog, dskx, ssm_w, e3, et2, tril3, triu3)


def _mix_ffn(x, attn, ynorm, tgt, mod6, norm2_w, final_w, w_out, w_gu, w_gu_own, s_arr, w_dn, tm):
    T = x.shape[0]
    nt = T // tm

    def body(x_ref, a_ref, y_ref, t_ref, mod_ref, n2_ref, fw_ref, wo_hbm, wgu_hbm, own_hbm, s_ref, wdn_hbm,
             sq_ref, dmix_ref, dx1_ref, h2_ref, act_ref, df_ref, dgu_ref, do_ref, sm_ref,
             wo, wgu, wdn, sems):
        i = pl.program_id(0)

        @pl.when(i == 0)
        def _():
            cps = [pltpu.make_async_copy(s, d, sems.at[k]) for k, (s, d) in
                   enumerate(((wo_hbm, wo), (wgu_hbm, wgu), (wdn_hbm, wdn)))]
            for c in cps:
                c.start()
            for c in cps:
                c.wait()
            own = pltpu.make_async_copy(
                own_hbm, wgu.at[:, pl.ds(pl.multiple_of(s_ref[0] * GU_SH, 128), GU_SH)], sems.at[3])
            own.start()
            own.wait()
            sq_ref[...] = jnp.zeros_like(sq_ref)
            sm_ref[...] = jnp.zeros_like(sm_ref)

        gate1, shift2, scale2, gate2 = mod_ref[2:3, :], mod_ref[3:4, :], mod_ref[4:5, :], mod_ref[5:6, :]
        n2w, fw = n2_ref[...], fw_ref[...]
        o = _dot(a_ref[...], wo[0:AW, :]) + _dot(y_ref[...], wo[AW:D, :])
        x1 = x_ref[...] + gate1 * o
        r2 = lax.rsqrt(jnp.mean(x1 * x1, axis=-1, keepdims=True) + EPS)
        xh2 = x1 * r2
        n2 = xh2 * n2w
        h2b = (n2 * (1.0 + scale2) + shift2).astype(BF16)
        h2_ref[...] = h2b
        f = jnp.zeros((tm, D), F32)
        saved = []
        for a, b in FF_SPLITS:
            gp = _dot(h2b, wgu[:, a:b])
            upj = _dot(h2b, wgu[:, DFF + a:DFF + b])
            sg = _sigmoid(gp)
            sl = gp * sg
            actb = (sl * upj).astype(BF16)
            act_ref[:, a:b] = actb
            f = f + _dot(actb, wdn[a:b, :])
            saved.append((gp, upj, sg, sl))
        x2 = x1 + gate2 * f
        r3 = lax.rsqrt(jnp.mean(x2 * x2, axis=-1, keepdims=True) + EPS)
        xh3 = x2 * r3
        err = xh3 * fw - t_ref[...]
        sq_ref[...] += jnp.sum(err * err, axis=0, keepdims=True)
        dy = err * (1.0 / D)
        dfw = jnp.sum(dy * xh3, axis=0, keepdims=True)
        dxh3 = dy * fw
        dx2 = r3 * (dxh3 - xh3 * jnp.mean(dxh3 * xh3, axis=-1, keepdims=True))
        dgate2 = jnp.sum(dx2 * f, axis=0, keepdims=True)
        dfb = (dx2 * gate2).astype(BF16)
        df_ref[...] = dfb
        dh2 = jnp.zeros((tm, D), F32)
        for (a, b), (gp, upj, sg, sl) in zip(FF_SPLITS, saved):
            dact = _dot_nt(dfb, wdn[a:b, :])
            dg = (dact * upj * (sg * (1.0 + gp * (1.0 - sg)))).astype(BF16)
            du = (dact * sl).astype(BF16)
            dgu_ref[:, a:b] = dg
            dgu_ref[:, DFF + a:DFF + b] = du
            dh2 = dh2 + _dot_nt(dg, wgu[:, a:b]) + _dot_nt(du, wgu[:, DFF + a:DFF + b])
        dshift2 = jnp.sum(dh2, axis=0, keepdims=True)
        dscale2 = jnp.sum(dh2 * n2, axis=0, keepdims=True)
        dn2 = dh2 * (1.0 + scale2)
        dn2w = jnp.sum(dn2 * xh2, axis=0, keepdims=True)
        dxh2 = dn2 * n2w
        dx1 = dx2 + r2 * (dxh2 - xh2 * jnp.mean(dxh2 * xh2, axis=-1, keepdims=True))
        dx1_ref[...] = dx1
        dgate1 = jnp.sum(dx1 * o, axis=0, keepdims=True)
        dob = (dx1 * gate1).astype(BF16)
        do_ref[...] = dob
        dmix_ref[...] = _dot_nt(dob, wo[...])
        sm_ref[...] += jnp.concatenate(
            [dfw, dn2w, dshift2, dscale2, dgate2, dgate1, jnp.zeros((2, D), F32)], axis=0)

    row = lambda w: pl.BlockSpec((tm, w), lambda i: (i, 0))
    full = lambda a: pl.BlockSpec(a.shape, lambda i: (0,) * a.ndim)
    anyspec = pl.BlockSpec(memory_space=pl.ANY)
    return pl.pallas_call(
        body, name="mix_ffn", grid=(nt,),
        in_specs=[row(D), row(AW), row(SW), row(D), full(mod6), full(norm2_w), full(final_w), anyspec, anyspec, anyspec,
                  pl.BlockSpec(memory_space=pltpu.SMEM), anyspec],
        out_specs=[pl.BlockSpec((1, D), lambda i: (0, 0)), row(D), row(D), row(D),
                   row(DFF), row(D), row(2 * DFF), row(D), pl.BlockSpec((8, D), lambda i: (0, 0))],
        out_shape=[jax.ShapeDtypeStruct((1, D), F32), jax.ShapeDtypeStruct((T, D), F32), jax.ShapeDtypeStruct((T, D), F32),
                   jax.ShapeDtypeStruct((T, D), BF16), jax.ShapeDtypeStruct((T, DFF), BF16),
                   jax.ShapeDtypeStruct((T, D), BF16), jax.ShapeDtypeStruct((T, 2 * DFF), BF16),
                   jax.ShapeDtypeStruct((T, D), BF16), jax.ShapeDtypeStruct((8, D), F32)],
        scratch_shapes=[pltpu.VMEM((D, D), BF16), pltpu.VMEM((D, 2 * DFF), BF16), pltpu.VMEM((DFF, D), BF16),
                        pltpu.SemaphoreType.DMA((4,))],
        compiler_params=_cp("arbitrary"),
    )(x, attn, ynorm, tgt, mod6, norm2_w, final_w, w_out, w_gu, w_gu_own, s_arr, w_dn)


def _in_proj_bwd(x, dx1, dqkv, dzxd, mod6, norm1_w, w_pad, tm):
    T = x.shape[0]

    def body(x_ref, dx1_ref, dq_ref, dz_ref, mod_ref, nw_ref, w_hbm, gx_ref, sm_ref, w_vmem, sem):
        _load_resident(w_hbm, w_vmem, sem)

        @pl.when(pl.program_id(0) == 0)
        def _():
            sm_ref[...] = jnp.zeros_like(sm_ref)

        nw = nw_ref[...]
        scale1 = mod_ref[1:2, :]
        sums = jnp.zeros((8, D), F32)
        for rows in (slice(0, tm // 2), slice(tm // 2, tm)):
            dh = _dot_nt(dq_ref[rows, :], w_vmem[:, 0:768]) + _dot_nt(dz_ref[rows, :], w_vmem[:, 768:IN_PAD])
            xv = x_ref[rows, :]
            r = lax.rsqrt(jnp.mean(xv * xv, axis=-1, keepdims=True) + EPS)
            xh = xv * r
            n1 = xh * nw
            dshift = jnp.sum(dh, axis=0, keepdims=True)
            dscale = jnp.sum(dh * n1, axis=0, keepdims=True)
            dn = dh * (1.0 + scale1)
            dnw = jnp.sum(dn * xh, axis=0, keepdims=True)
            dxh = dn * nw
            gx_ref[rows, :] = dx1_ref[rows, :] + r * (dxh - xh * jnp.mean(dxh * xh, axis=-1, keepdims=True))
            sums = sums + jnp.concatenate([dnw, dshift, dscale, jnp.zeros((5, D), F32)], axis=0)
        sm_ref[...] += sums

    row = lambda w: pl.BlockSpec((tm, w), lambda i: (i, 0))
    full = lambda a: pl.BlockSpec(a.shape, lambda i: (0,) * a.ndim)
    return pl.pallas_call(
        body, name="in_proj_bwd", grid=(T // tm,),
        in_specs=[row(D), row(D), row(768), row(1664), full(mod6), full(norm1_w), pl.BlockSpec(memory_space=pl.ANY)],
        out_specs=[row(D), pl.BlockSpec((8, D), lambda i: (0, 0))],
        out_shape=[jax.ShapeDtypeStruct((T, D), F32), jax.ShapeDtypeStruct((8, D), F32)],
        scratch_shapes=[pltpu.VMEM((D, IN_PAD), BF16), pltpu.SemaphoreType.DMA],
        compiler_params=_cp("arbitrary"),
    )(x, dx1, dqkv, dzxd, mod6, norm1_w, w_pad)


def _tn_matmul(a, b, K, N, tt, name, dep):
    T = a.shape[0]
    ja, jb = a.shape[1] // K, b.shape[1] // N
    J = max(ja, jb)

    def body(a_ref, b_ref, dep_ref, o_ref):
        t = pl.program_id(1)
        prod = _dot_tn(a_ref[...], b_ref[...])

        @pl.when(t == 0)
        def _():
            o_ref[0] = prod

        @pl.when(t > 0)
        def _():
            o_ref[0] += prod

    return pl.pallas_call(
        body, name=name, grid=(J, T // tt),
        in_specs=[pl.BlockSpec((tt, K), lambda j, t: (t, j if ja > 1 else 0)),
                  pl.BlockSpec((tt, N), lambda j, t: (t, j if jb > 1 else 0)),
                  pl.BlockSpec((8, 128), lambda j, t: (0, 0))],
        out_specs=pl.BlockSpec((1, K, N), lambda j, t: (j, 0, 0)),
        out_shape=jax.ShapeDtypeStruct((J, K, N), F32),
        compiler_params=_cp("parallel", "arbitrary"),
    )(a, b, dep)


def _adam_math(w, g, m, v):
    m = B1 * m + (1.0 - B1) * g
    v = B2 * v + (1.0 - B2) * (g * g)
    m_hat = m / (1.0 - B1 ** STEP)
    v_hat = v / (1.0 - B2 ** STEP)
    delta = -LR * (m_hat / (jnp.sqrt(v_hat) + AEPS) + WD * w)
    return delta, m, v


def _adam_2d(w, mine, land, m, v, c_arr, rb, name):
    R, C = w.shape
    nbh = R // 2 // rb

    def body(c_ref, w_ref, mine_ref, land_ref, m_ref, v_ref, go_ref, d_ref, mo_ref, vo_ref):
        g = jnp.where(pl.program_id(0) // nbh == c_ref[0], mine_ref[...], land_ref[...])
        d, mn, vn = _adam_math(w_ref[...], g, m_ref[...], v_ref[...])
        go_ref[...] = g
        d_ref[...] = d
        mo_ref[...] = mn
        vo_ref[...] = vn

    spec = pl.BlockSpec((rb, C), lambda i, c_ref: (i, 0))
    mine_spec = pl.BlockSpec((rb, C), lambda i, c_ref: (jnp.clip(i - c_ref[0] * nbh, 0, nbh - 1), 0))
    return pl.pallas_call(
        body, name=name,
        grid_spec=pltpu.PrefetchScalarGridSpec(
            num_scalar_prefetch=1, grid=(R // rb,), in_specs=[spec, mine_spec, spec, spec, spec], out_specs=[spec] * 4),
        out_shape=[jax.ShapeDtypeStruct((R, C), F32)] * 4, compiler_params=_cp("parallel"),
    )(c_arr, w, mine, land, m, v)


def _adam_w_in(w3, mine, land, m3, v3, c_arr):
    n = w3.shape[0]

    def body(c_ref, w_hbm, mine_ref, land_ref, m_hbm, v_hbm, g_hbm, d_hbm, mo_hbm, vo_hbm, bufs, sems):
        ins = [pltpu.make_async_copy(src.at[:, 0], bufs.at[k], sems.at[k]) for k, src in enumerate((w_hbm, m_hbm, v_hbm))]
        for cp in ins:
            cp.start()
        half = D // 2
        top = jnp.where(c_ref[0] == 0, mine_ref[...], land_ref[0:half, :])
        bot = jnp.where(c_ref[0] == 1, mine_ref[...], land_ref[half:D, :])
        g = jnp.concatenate([top, bot], axis=0)
        eye = (_iota((D, D), 0) == _iota((D, D), 1)).astype(BF16)
        g_t = jnp.zeros((n, D), F32)
        r = g
        for i in range(3):
            p = r.astype(BF16)
            g_t = g_t + _dot_tn(p, eye)
            if i < 2:
                r = r - p.astype(F32)
        for cp in ins:
            cp.wait()
        d, mn, vn = _adam_math(bufs[0], g_t, bufs[1], bufs[2])
        for k, val in enumerate((g_t, d, mn, vn)):
            bufs[3 + k] = val
        outs = [pltpu.make_async_copy(bufs.at[3 + k], dst.at[:, 0], sems.at[3 + k])
                for k, dst in enumerate((g_hbm, d_hbm, mo_hbm, vo_hbm))]
        for cp in outs:
            cp.start()
        for cp in outs:
            cp.wait()

    anyspec = pl.BlockSpec(memory_space=pl.ANY)
    vm = pl.BlockSpec(memory_space=pltpu.VMEM)
    return pl.pallas_call(
        body, name="adam_w_in",
        in_specs=[pl.BlockSpec(memory_space=pltpu.SMEM), anyspec, vm, vm, anyspec, anyspec], out_specs=[anyspec] * 4,
        out_shape=[jax.ShapeDtypeStruct(w3.shape, F32)] * 4,
        scratch_shapes=[pltpu.VMEM((7, n, D), F32), pltpu.SemaphoreType.DMA((7,))],
        compiler_params=pltpu.CompilerParams(vmem_limit_bytes=VMEM_LIMIT),
    )(c_arr, w3, mine, land, m3, v3)


def _adam_w_ada(sc_all, dmod_s, w, m, v, rb):
    R, C = w.shape

    def body(sc_ref, dm_ref, w_ref, m_ref, v_ref, g_ref, d_ref, mo_ref, vo_ref):
        g = lax.dot_general(sc_ref[...], dm_ref[...], (((0,), (0,)), ((), ())), precision=HI, preferred_element_type=F32)
        d, mn, vn = _adam_math(w_ref[...], g, m_ref[...], v_ref[...])
        g_ref[...] = g
        d_ref[...] = d
        mo_ref[...] = mn
        vo_ref[...] = vn

    spec = pl.BlockSpec((rb, C), lambda i: (i, 0))
    return pl.pallas_call(
        body, name="adam_w_ada", grid=(R // rb,),
        in_specs=[pl.BlockSpec((8, rb), lambda i: (0, i)), pl.BlockSpec((8, C), lambda i: (0, 0)), spec, spec, spec],
        out_specs=[spec] * 4, out_shape=[jax.ShapeDtypeStruct((R, C), F32)] * 4, compiler_params=_cp("parallel"),
    )(sc_all, dmod_s, w, m, v)


def _adam_small(grads, ws, ms, vs):
    k = len(ws)

    def body(*refs):
        g, w, m, v = refs[0:k], refs[k:2 * k], refs[2 * k:3 * k], refs[3 * k:4 * k]
        g_o, d_o, m_o, v_o = refs[4 * k:5 * k], refs[5 * k:6 * k], refs[6 * k:7 * k], refs[7 * k:8 * k]
        for i in range(k):
            gi = g[i][...]
            d, mn, vn = _adam_math(w[i][...], gi, m[i][...], v[i][...])
            g_o[i][...] = gi
            d_o[i][...] = d
            m_o[i][...] = mn
            v_o[i][...] = vn

    shapes = [jax.ShapeDtypeStruct(w.shape, F32) for w in ws]
    vm = pl.BlockSpec(memory_space=pltpu.VMEM)
    outs = pl.pallas_call(
        body, name="adam_small", in_specs=[vm] * (4 * k), out_specs=[vm] * (4 * k), out_shape=shapes * 4,
    )(*grads, *ws, *ms, *vs)
    return outs[0:k], outs[k:2 * k], outs[2 * k:3 * k], outs[3 * k:4 * k]


def _pos():
    return lax.axis_index("x"), lax.axis_index("y"), lax.axis_index("c")


def _flip(v, bit):
    return 1 - v if bit else v


def _peer(k):
    x, y, c = _pos()
    return (_flip(x, (k >> 2) & 1), _flip(y, (k >> 1) & 1), _flip(c, k & 1))


def _logical(p):
    return 4 * p[0] + 2 * p[1] + p[2]


def _gather8(src_ref, dst_ref, send_sems, recv_sems):
    me = _logical(_pos())
    dst_ref[pl.ds(me, 1)] = src_ref[...][None]
    copies = []
    for k in range(1, 8):
        cp = pltpu.make_async_remote_copy(src_ref, dst_ref.at[me], send_sems.at[k - 1], recv_sems.at[k - 1],
                                          device_id=_peer(k), device_id_type=MESH)
        cp.start()
        copies.append(cp)
    for k in range(1, 8):
        pltpu.make_async_remote_copy(src_ref, dst_ref.at[_logical(_peer(k))], send_sems.at[k - 1], recv_sems.at[k - 1],
                                     device_id=_peer(k), device_id_type=MESH).wait_recv()
    for cp in copies:
        cp.wait_send()


def _rows_select(ref3, width):
    row = _iota((8, width), 0)
    out = jnp.zeros((8, width), F32)
    for i in range(8):
        out = jnp.where(row == i, ref3[i][:, 0:width], out)
    return out


def _mod_exchange(payload, w_ada_s, b_ada4):
    n_sh = w_ada_s.shape[1]

    def body(pay_ref, w_ref, b_ref, gat_ref, mod_ref, token, p3, sa, ra, sb, rb):
        token[...] = jnp.zeros_like(token)
        x, y, c = _pos()
        me = _logical((x, y, c))
        my_s = 2 * x + y
        _gather8(pay_ref, gat_ref, sa, ra)
        cmat = _rows_select(gat_ref, D)
        prod = _dot_hi(cmat * _sigmoid(cmat), w_ref[...])
        for b in range(8):
            p3[b] = prod[b:b + 1, :]
        mod_ref[pl.ds(my_s, 1)] = p3[pl.ds(me, 1)] + b_ref[pl.ds(my_s, 1)]
        ks = (2, 4, 6)
        copies = []
        for i, k in enumerate(ks):
            pr = _peer(k)
            cp = pltpu.make_async_remote_copy(p3.at[_logical(pr)], mod_ref.at[my_s], sb.at[i], rb.at[i],
                                              device_id=pr, device_id_type=MESH)
            cp.start()
            copies.append(cp)
        for i, k in enumerate(ks):
            pr = _peer(k)
            s_src = 2 * pr[0] + pr[1]
            pltpu.make_async_remote_copy(p3.at[0], mod_ref.at[s_src], sb.at[i], rb.at[i],
                                         device_id=pr, device_id_type=MESH).wait_recv()
            mod_ref[pl.ds(s_src, 1)] = mod_ref[pl.ds(s_src, 1)] + b_ref[pl.ds(s_src, 1)]
        for cp in copies:
            cp.wait_send()

    vm = pl.BlockSpec(memory_space=pltpu.VMEM)
    return pl.pallas_call(
        body, name="mod_exchange", in_specs=[vm, vm, vm], out_specs=[vm, vm, vm],
        out_shape=[jax.ShapeDtypeStruct((8, 1, payload.shape[1]), F32), jax.ShapeDtypeStruct((4, 1, n_sh), F32),
                   jax.ShapeDtypeStruct((8, 128), F32)],
        scratch_shapes=[pltpu.VMEM((8, 1, n_sh), F32), pltpu.SemaphoreType.DMA((7,)), pltpu.SemaphoreType.DMA((7,)),
                        pltpu.SemaphoreType.DMA((3,)), pltpu.SemaphoreType.DMA((3,))],
        compiler_params=pltpu.CompilerParams(vmem_limit_bytes=VMEM_LIMIT),
    )(payload, w_ada_s, b_ada4)


def _chips():
    x, y, _ = _pos()
    out = []
    for k in (1, 2, 3):
        px, py = _flip(x, (k >> 1) & 1), _flip(y, k & 1)
        out.append((px, py, 2 * px + py))
    return out


def _half_rows(ref, which):
    half = ref.shape[-2] // 2
    return pl.ds(pl.multiple_of(which * half, 8), half)


def _small_reduce(vec):
    n = vec.shape[1]

    def body(v_ref, tot_ref, gat_ref, sa, ra):
        _gather8(v_ref, gat_ref, sa, ra)
        tot = gat_ref[0]
        for i in range(1, 8):
            tot = tot + gat_ref[i]
        tot_ref[...] = tot

    vm = pl.BlockSpec(memory_space=pltpu.VMEM)
    return pl.pallas_call(
        body, name="small_reduce", in_specs=[vm], out_specs=[vm, vm],
        out_shape=[jax.ShapeDtypeStruct((1, n), F32), jax.ShapeDtypeStruct((8, 1, n), F32)],
        scratch_shapes=[pltpu.SemaphoreType.DMA((7,)), pltpu.SemaphoreType.DMA((7,))],
    )(vec)


def _add_half(g, sib, c_arr, rb, name):
    _, R, C = g.shape
    half = R // 2
    nb = half // rb

    def body(c_ref, g_ref, s_ref, o_ref):
        o_ref[...] = (g_ref[...] + s_ref[...]).astype(BF16)

    return pl.pallas_call(
        body, name=name,
        grid_spec=pltpu.PrefetchScalarGridSpec(
            num_scalar_prefetch=1, grid=(4, nb),
            in_specs=[pl.BlockSpec((1, rb, C), lambda s, i, c_ref: (s, c_ref[0] * nb + i, 0)),
                      pl.BlockSpec((1, rb, C), lambda s, i, c_ref: (s, i, 0))],
            out_specs=pl.BlockSpec((1, rb, C), lambda s, i, c_ref: (s, i, 0))),
        out_shape=jax.ShapeDtypeStruct((4, half, C), BF16),
        compiler_params=_cp("parallel", "parallel"),
    )(c_arr, g, sib)


def _sum4(parts, land, s_arr, rb, name):
    _, H, C = land.shape

    def body(s_ref, own_ref, r_ref, o_ref):
        own = own_ref[0].astype(F32)
        tot = jnp.zeros((rb, C), F32)
        for j in range(4):
            tot = tot + jnp.where(s_ref[0] == j, own, r_ref[j].astype(F32))
        o_ref[...] = tot

    return pl.pallas_call(
        body, name=name,
        grid_spec=pltpu.PrefetchScalarGridSpec(
            num_scalar_prefetch=1, grid=(H // rb,),
            in_specs=[pl.BlockSpec((1, rb, C), lambda i, s_ref: (s_ref[0], i, 0)),
                      pl.BlockSpec((4, rb, C), lambda i, s_ref: (0, i, 0))],
            out_specs=pl.BlockSpec((rb, C), lambda i, s_ref: (i, 0))),
        out_shape=jax.ShapeDtypeStruct((H, C), F32), compiler_params=_cp("parallel"),
    )(s_arr, parts, land)


HBM_SPEC = pl.BlockSpec(memory_space=pltpu.HBM)
SEM_SPEC = pl.BlockSpec(memory_space=pltpu.SEMAPHORE)
EFFECT = pltpu.SideEffectType.DATAFLOW_SIDE_EFFECTING


def _split_start(name, bufs, n_sem, plan):
    nb = len(bufs)

    def body(*refs):
        ins, send, recv, token = refs[:nb], refs[nb], refs[nb + 1], refs[-1]
        for i, (src, dst, dev, _) in enumerate(plan(ins)):
            pltpu.make_async_remote_copy(src, dst, send.at[i], recv.at[i], device_id=dev, device_id_type=MESH).start()
        token[...] = jnp.zeros_like(token)

    outs = pl.pallas_call(
        body, name=name,
        out_shape=(pltpu.SemaphoreType.DMA((n_sem,)), pltpu.SemaphoreType.DMA((n_sem,)),
                   *[pltpu.HBM(b.shape, b.dtype) for b in bufs], jax.ShapeDtypeStruct((8, 128), F32)),
        in_specs=[HBM_SPEC] * nb,
        out_specs=(SEM_SPEC, SEM_SPEC, *([HBM_SPEC] * nb), pl.BlockSpec(memory_space=pltpu.VMEM)),
        input_output_aliases={i: 2 + i for i in range(nb)},
        compiler_params=pltpu.CompilerParams(has_side_effects=EFFECT),
    )(*[pltpu.with_memory_space_constraint(b, pltpu.HBM) for b in bufs])
    return outs[0], outs[1], list(outs[2:2 + nb]), outs[-1]


def _split_wait(name, send, recv, bufs, after, plan):
    nb = len(bufs)

    def body(*refs):
        ins, send_s, recv_s = refs[:nb], refs[nb], refs[nb + 1]
        for i, (src, dst, dev, mine) in enumerate(plan(ins)):
            pltpu.make_async_remote_copy(src, dst, send_s.at[i], recv_s.at[i], device_id=dev,
                                         device_id_type=MESH).wait_send()
            pltpu.make_async_remote_copy(src, mine, send_s.at[i], recv_s.at[i], device_id=dev,
                                         device_id_type=MESH).wait_recv()

    outs = pl.pallas_call(
        body, name=name, out_shape=[pltpu.HBM(b.shape, b.dtype) for b in bufs],
        in_specs=[HBM_SPEC] * nb + [SEM_SPEC, SEM_SPEC, pl.BlockSpec(memory_space=pl.ANY)],
        out_specs=[HBM_SPEC] * nb, input_output_aliases={i: i for i in range(nb)},
        compiler_params=pltpu.CompilerParams(has_side_effects=EFFECT),
    )(*bufs, send, recv, after)
    return list(outs)


def _copies_now(name, bufs, n_sem, plan):
    nb = len(bufs)

    def body(*refs):
        ins, token, send, recv = refs[:nb], refs[2 * nb], refs[-2], refs[-1]
        token[...] = jnp.zeros_like(token)
        todo = plan(ins)
        for i, (src, dst, dev, _) in enumerate(todo):
            pltpu.make_async_remote_copy(src, dst, send.at[i], recv.at[i], device_id=dev, device_id_type=MESH).start()
        for i, (src, dst, dev, mine) in enumerate(todo):
            pltpu.make_async_remote_copy(src, mine, send.at[i], recv.at[i], device_id=dev, device_id_type=MESH).wait_recv()
        for i, (src, dst, dev, _) in enumerate(todo):
            pltpu.make_async_remote_copy(src, dst, send.at[i], recv.at[i], device_id=dev, device_id_type=MESH).wait_send()

    outs = pl.pallas_call(
        body, name=name,
        out_shape=[pltpu.HBM(b.shape, b.dtype) for b in bufs] + [jax.ShapeDtypeStruct((8, 128), F32)],
        in_specs=[HBM_SPEC] * nb, out_specs=[HBM_SPEC] * nb + [pl.BlockSpec(memory_space=pltpu.VMEM)],
        input_output_aliases={i: i for i in range(nb)},
        scratch_shapes=[pltpu.SemaphoreType.DMA((n_sem,)), pltpu.SemaphoreType.DMA((n_sem,))],
    )(*[pltpu.with_memory_space_constraint(b, pltpu.HBM) for b in bufs])
    return list(outs[:nb]), outs[nb]


def _slot(land, s, rows, cols):
    if cols is None:
        return land.at[s, rows]
    return land.at[rows, pl.ds(pl.multiple_of(s * cols, 128), cols)]


def _plan_gather_ici(cols):
    nw = len(cols)

    def plan(refs):
        x, y, c = _pos()
        my_s = 2 * x + y
        out = []
        for w in range(nw):
            mine = _half_rows(refs[w], c)
            for px, py, ps in _chips():
                out.append((refs[w].at[mine], _slot(refs[nw + w], my_s, mine, cols[w]), (px, py, c),
                            _slot(refs[nw + w], ps, mine, cols[w])))
        return out
    return plan


def _plan_gather_fwd(cols, rows):
    def plan(refs):
        x, y, c = _pos()
        out = []
        for w in range(len(cols)):
            half = rows[w] // 2
            mine = pl.ds(pl.multiple_of(c * half, 8), half)
            other = pl.ds(pl.multiple_of((1 - c) * half, 8), half)
            for px, py, ps in _chips():
                got = _slot(refs[w], ps, mine, cols[w])
                out.append((got, got, (x, y, 1 - c), _slot(refs[w], ps, other, cols[w])))
        return out
    return plan


def _plan_swap(nw):
    def plan(refs):
        x, y, c = _pos()
        return [(refs[w].at[:, _half_rows(refs[w], 1 - c)], refs[nw + w], (x, y, 1 - c), refs[nw + w])
                for w in range(nw)]
    return plan


def _plan_scatter(nw):
    def plan(refs):
        x, y, c = _pos()
        my_s = 2 * x + y
        out = []
        for w in range(nw):
            for px, py, ps in _chips():
                out.append((refs[w].at[ps], refs[nw + w].at[my_s], (px, py, c), refs[nw + w].at[ps]))
        return out
    return plan


def _plan_join(nw):
    def plan(refs):
        x, y, c = _pos()
        out = []
        for w in range(nw):
            land = refs[nw + w]
            out.append((refs[w], land.at[_half_rows(land, c)], (x, y, 1 - c), land.at[_half_rows(land, 1 - c)]))
        return out
    return plan


def _hbm_empty(shape, dtype):
    return pltpu.with_memory_space_constraint(lax.empty(shape, dtype), pltpu.HBM)


def _put_slot(land, own, slot):
    return lax.dynamic_update_slice(land, own[None], (slot,) + (0,) * own.ndim)


def _pad_lanes(a, n):
    return jnp.pad(a, ((0, 0), (0, n - a.shape[1])))


def kernel(x, c, positions, w_ada, b_ada, norm1_w, w_in, conv_w, conv_b, dt_bias, a_log, d_skip, attn_sinks, ssm_norm_w, w_out, norm2_w, w_gate_up, w_down, final_norm_w, loss_target, m_w_ada, m_b_ada, m_norm1_w, m_w_in, m_conv_w, m_conv_b, m_dt_bias, m_a_log, m_d_skip, m_attn_sinks, m_ssm_norm_w, m_w_out, m_norm2_w, m_w_gate_up, m_w_down, m_final_norm_w, v_w_ada, v_b_ada, v_norm1_w, v_w_in, v_conv_w, v_conv_b, v_dt_bias, v_a_log, v_d_skip, v_attn_sinks, v_ssm_norm_w, v_w_out, v_norm2_w, v_w_gate_up, v_w_down, v_final_norm_w):
    T = x.shape[1]
    tm = min(256, T)
    xi, yi, ci = lax.axis_index("x"), lax.axis_index("y"), lax.axis_index("c")
    my_s = 2 * xi + yi
    xs = x[0]
    tgt = loss_target[0]

    payload = jnp.concatenate([c, conv_w[0].reshape(1, CONVK * 256)], axis=1)
    gat, mod4, tok = _mod_exchange(payload, w_ada[0], b_ada.reshape(4, 1, 1536))
    mod6 = mod4.reshape(6, D)
    c_all = gat[:, 0, 0:D]
    cw_dev = gat[:, 0, D:].reshape(4, 2, CONVK, 256)[:, 0]
    conv_full = cw_dev.transpose(1, 0, 2).reshape(CONVK, CONVC)

    w_in_b = (w_in[0] + tok[0, 0]).astype(BF16)
    s_i, r_i, bufs, tok = _split_start("wgather_in_ici_start", [w_in_b, _hbm_empty((4,) + w_in_b.shape, BF16)], 3,
                                       _plan_gather_ici([None]))
    inv_freq = (10000.0 ** (-jnp.arange(32, dtype=F32) / 32))
    cos, sin_s = _rope_tables(positions, inv_freq.reshape(32, 1) + tok[0:1, 0:1], min(512, T))
    bufs = _split_wait("wgather_in_ici_wait", s_i, r_i, bufs, cos, _plan_gather_ici([None]))
    bufs, tok = _copies_now("wgather_in_fwd", bufs[1:], 3, _plan_gather_fwd([None], [D]))
    g_in = _put_slot(bufs[0], w_in_b, my_s)
    w_pad = jnp.concatenate([g_in[0], g_in[1], g_in[2], g_in[3], jnp.zeros((D, IN_PAD - IN_PROJ), BF16)], axis=1)

    late = [(w_out[0] + tok[0, 0]).astype(BF16), w_gate_up[0].astype(BF16), w_down[0].astype(BF16)]
    lands = [_hbm_empty((4, D // 4, D), BF16), _hbm_empty((D, 2 * DFF), BF16), _hbm_empty((4, DFF // 4, D), BF16)]
    cols3, rows3 = [None, GU_SH, None], [D // 4, D, DFF // 4]
    s_a, r_a, bufs, tok = _split_start("wgather_ici_start", late + lands, 9, _plan_gather_ici(cols3))

    qkv, z, xbc, dtr, h1b = _in_proj_fwd(xs, cos, sin_s, mod6 + tok[0, 0], norm1_w, w_pad, min(512, T))
    sinks = attn_sinks
    attn, lse = _attn_fwd(qkv, sinks)
    bufs = _split_wait("wgather_ici_wait", s_a, r_a, bufs, attn, _plan_gather_ici(cols3))
    s_b, r_b, lands, tok = _split_start("wgather_fwd_start", bufs[3:], 9, _plan_gather_fwd(cols3, rows3))
    dtb = _pad_lanes(dt_bias, 128)
    alog = _pad_lanes(a_log, 128)
    dskx = jnp.repeat(d_skip, HD, axis=1)
    mats = _ssd_mats()
    ynorm, ypre, states, conv_pre = _ssd_fwd(xbc, z, dtr, conv_full, conv_b, dtb + tok[0, 0], alog, dskx, ssm_norm_w,
                                             mats)
    lands = _split_wait("wgather_fwd_wait", s_b, r_b, lands, ynorm, _plan_gather_fwd(cols3, rows3))
    w_out_f = _put_slot(lands[0], late[0], my_s).reshape(D, D)
    w_dn_f = _put_slot(lands[2], late[2], my_s).reshape(DFF, D)
    s_arr = my_s.reshape(1).astype(jnp.int32)

    fw2 = final_norm_w.reshape(1, D)
    sq, dmix, dx1, h2b, act, dfb, dgu, dob, sm_ffn = _mix_ffn(
        xs, attn, ynorm, tgt, mod6, norm2_w, fw2, w_out_f, lands[1], late[1], s_arr, w_dn_f, tm)

    tt = min(2048, T)
    c_arr = ci.reshape(1).astype(jnp.int32)
    tok0 = jnp.zeros((8, 128), F32)
    gw_dn4 = _tn_matmul(act, dfb, GU_SH, D, tt, "dw_down", tok0).reshape(4, DFF // 4, D)
    gw_gu4 = _tn_matmul(h2b, dgu, D, GU_SH, tt, "dw_gate_up", tok0)
    gw_out4 = jnp.concatenate(
        [_tn_matmul(attn, dob, AW, D, tt, "dw_out_a", tok0)[0],
         _tn_matmul(ynorm, dob, SW, D, tt, "dw_out_y", tok0)[0]], axis=0).reshape(4, D // 4, D)
    big1 = [gw_out4, gw_gu4, gw_dn4]
    rbs1 = [128, 512, 352]
    sib1 = [_hbm_empty((4, g.shape[1] // 2, g.shape[2]), F32) for g in big1]
    s_c, r_c, bufs, tok = _split_start("gswap_start", big1 + sib1, 3, _plan_swap(3))

    dzxd, d_cw, d_cb, d_sw, d_sk, d_dtb, d_av = _ssd_bwd(
        xbc, conv_pre, z, dtr, ypre, states, dmix, conv_full, dtb + tok[0, 0], alog, dskx, ssm_norm_w, mats)
    bufs = _split_wait("gswap_wait", s_c, r_c, bufs, dzxd, _plan_swap(3))
    sums1 = [_add_half(g, s, c_arr, rb, "grad_add_%d" % i)
             for i, (g, s, rb) in enumerate(zip(bufs[:3], bufs[3:], rbs1))]
    land1 = [_hbm_empty(p.shape, BF16) for p in sums1]
    s_d, r_d, bufs, tok = _split_start("gscatter_start", sums1 + land1, 9, _plan_scatter(3))
    dqkv, d_sinks = _attn_bwd(qkv, sinks + tok[0:1, 0:8], lse, dmix, cos, sin_s)
    bufs = _split_wait("gscatter_wait", s_d, r_d, bufs, dqkv, _plan_scatter(3))
    halves1 = [_sum4(p, l, s_arr, rb, "grad_sum_%d" % i)
               for i, (p, l, rb) in enumerate(zip(bufs[:3], bufs[3:], rbs1))]
    full1 = [_hbm_empty((2 * h.shape[0], h.shape[1]), F32) for h in halves1]
    s_e, r_e, bufs, tok = _split_start("gjoin_start", halves1 + full1, 3, _plan_join(3))
    gq = _tn_matmul(h1b, dqkv, D, 768, tt, "dw_in_qkv", tok)[0]
    gz = _tn_matmul(h1b, dzxd, D, 1664, tt, "dw_in_zxd", tok)[0]
    gw_in4 = jnp.stack([gq[:, :IN_SH], jnp.concatenate([gq[:, IN_SH:], gz[:, :2 * IN_SH - 768]], axis=1),
                        gz[:, 2 * IN_SH - 768:3 * IN_SH - 768], gz[:, 3 * IN_SH - 768:4 * IN_SH - 768]])
    joined1 = _split_wait("gjoin_wait", s_e, r_e, bufs, gw_in4, _plan_join(3))

    sib0 = _hbm_empty((4, D // 2, IN_SH), F32)
    bufs, _ = _copies_now("gswap_in", [gw_in4, sib0], 1, _plan_swap(1))
    sum0 = _add_half(bufs[0], bufs[1], c_arr, 512, "grad_add_in")
    s_g, r_g, bufs, tok = _split_start("gscatter_in_start", [sum0, _hbm_empty(sum0.shape, BF16)], 3, _plan_scatter(1))
    grad_x, sm_in = _in_proj_bwd(xs, dx1, dqkv, dzxd, mod6 + tok[0, 0], norm1_w, w_pad, min(512, T))
    bufs = _split_wait("gscatter_in_wait", s_g, r_g, bufs, grad_x, _plan_scatter(1))
    half0 = _sum4(bufs[0], bufs[1], s_arr, 512, "grad_sum_in")
    joined0, _ = _copies_now("gjoin_in", [half0, _hbm_empty((D, IN_SH), F32)], 1, _plan_join(1))

    a_neg = -jnp.exp(alog)
    pieces = [sm_in[1:2], sm_in[2:3], sm_ffn[5:6], sm_ffn[2:3], sm_ffn[3:4], sm_ffn[4:5],
              sm_in[0:1], sm_ffn[1:2], sm_ffn[0:1], d_cb, d_cw.reshape(1, CONVK * CONVC),
              _pad_lanes(d_sw, SW), d_dtb, d_av * a_neg, d_sk, d_sinks,
              _pad_lanes((0.5 / D * jnp.sum(sq)).reshape(1, 1), 128)]
    vec = jnp.concatenate(pieces, axis=1)
    tot, allv = _small_reduce(vec)
    o = 0
    offs = []
    for p in pieces:
        offs.append(o)
        o += p.shape[1]
    seg = lambda i, n: tot[:, offs[i]:offs[i] + n]
    g_b_ada = tot[:, 0:6 * D]
    g_norm1, g_norm2, g_final, g_conv_b = seg(6, D), seg(7, D), seg(8, D), seg(9, D)
    g_conv_w = lax.dynamic_slice_in_dim(seg(10, CONVK * CONVC).reshape(CONVK, CONVC), my_s * 256, 256, axis=1)
    g_ssm_w, g_dtb, g_alog, g_dsk, g_sink = seg(11, SW), seg(12, 8), seg(13, 8), seg(14, 8), seg(15, 8)
    loss = tot[0, offs[16]]

    small_names = ["b_ada", "norm1_w", "conv_w", "conv_b", "dt_bias", "a_log", "d_skip", "attn_sinks", "ssm_norm_w",
                   "norm2_w", "final_norm_w"]
    small_g = [g_b_ada, g_norm1, g_conv_w, g_conv_b, g_dtb, g_alog, g_dsk, g_sink, g_ssm_w, g_norm2, g_final]
    as2d = lambda a: a.reshape(-1, a.shape[-1])
    small_w = [as2d(a) for a in (b_ada, norm1_w, conv_w, conv_b, dt_bias, a_log, d_skip, attn_sinks, ssm_norm_w,
                                 norm2_w, final_norm_w)]
    small_m = [as2d(a) for a in (m_b_ada, m_norm1_w, m_conv_w, m_conv_b, m_dt_bias, m_a_log, m_d_skip, m_attn_sinks,
                                 m_ssm_norm_w, m_norm2_w, m_final_norm_w)]
    small_v = [as2d(a) for a in (v_b_ada, v_norm1_w, v_conv_w, v_conv_b, v_dt_bias, v_a_log, v_d_skip, v_attn_sinks,
                                 v_ssm_norm_w, v_norm2_w, v_final_norm_w)]
    small_g, sd, smn, svn = _adam_small(small_g, small_w, small_m, small_v)

    sc_all = c_all * jax.nn.sigmoid(c_all)
    dmod_all = allv[:, 0, 0:6 * D]
    dmod_s = lax.dynamic_slice_in_dim(dmod_all, my_s * 1536, 1536, axis=1)
    g_ada, d_ada, m_ada, v_ada = _adam_w_ada(sc_all, dmod_s, w_ada[0], m_w_ada[0], v_w_ada[0], 256)
    native = lambda a: a.transpose(2, 0, 1)
    g_in_s, d_in, m_in, v_in = [a.transpose(1, 2, 0) for a in _adam_w_in(
        native(w_in), joined0[0], joined0[1], native(m_w_in), native(v_w_in), c_arr)]
    g_out_s, d_out, m_out, v_out = _adam_2d(w_out[0], joined1[0], joined1[3], m_w_out[0], v_w_out[0], c_arr, 128,
                                            "adam_w_out")
    g_gu_s, d_gu, m_gu, v_gu = _adam_2d(w_gate_up[0], joined1[1], joined1[4], m_w_gate_up[0], v_w_gate_up[0], c_arr,
                                        256, "adam_w_gate_up")
    g_dn_s, d_dn, m_dn, v_dn = _adam_2d(w_down[0], joined1[2], joined1[5], m_w_down[0], v_w_down[0], c_arr, 352,
                                        "adam_w_down")

    order = ["w_ada", "b_ada", "norm1_w", "w_in", "conv_w", "conv_b", "dt_bias", "a_log", "d_skip", "attn_sinks",
             "ssm_norm_w", "w_out", "norm2_w", "w_gate_up", "w_down", "final_norm_w"]
    shapes = dict(w_ada=w_ada.shape, b_ada=b_ada.shape, norm1_w=norm1_w.shape, w_in=w_in.shape, conv_w=conv_w.shape,
                  conv_b=conv_b.shape, dt_bias=dt_bias.shape, a_log=a_log.shape, d_skip=d_skip.shape,
                  attn_sinks=attn_sinks.shape, ssm_norm_w=ssm_norm_w.shape, w_out=w_out.shape, norm2_w=norm2_w.shape,
                  w_gate_up=w_gate_up.shape, w_down=w_down.shape, final_norm_w=final_norm_w.shape)
    grads = dict(w_ada=g_ada, w_in=g_in_s, w_out=g_out_s, w_gate_up=g_gu_s, w_down=g_dn_s)
    deltas = dict(w_ada=d_ada, w_in=d_in, w_out=d_out, w_gate_up=d_gu, w_down=d_dn)
    new_m = dict(w_ada=m_ada, w_in=m_in, w_out=m_out, w_gate_up=m_gu, w_down=m_dn)
    new_v = dict(w_ada=v_ada, w_in=v_in, w_out=v_out, w_gate_up=v_gu, w_down=v_dn)
    for i, nme in enumerate(small_names):
        grads[nme], deltas[nme], new_m[nme], new_v[nme] = small_g[i], sd[i], smn[i], svn[i]
    outs = [loss, grad_x[None]]
    for table in (grads, deltas, new_m, new_v):
        outs += [table[nme].reshape(shapes[nme]) for nme in order]
    return tuple(outs)
```

```python
import functools
import math

import jax
import jax.numpy as jnp
from jax import lax
from jax.experimental import pallas as pl
from jax.experimental.pallas import tpu as pltpu

F32 = jnp.float32
BF16 = jnp.bfloat16
HI = lax.Precision.HIGHEST
MESH = pl.DeviceIdType.MESH

D = 1024
HD = 64
AW = 512
SW = 512
NST = 128
CONVK = 4
CONVC = 1024
BLK = 128
CPS = 4
IN_PROJ = 2312
IN_PAD = 2432
IN_SH = IN_PROJ // 4
DFF = 2816
GU_SH = 1408
FF_SPLITS = ((0, 1536), (1536, 2816))
EPS = 1e-6
NEG = -1e30
LR, B1, B2, AEPS, WD, STEP = 0.001, 0.9, 0.999, 1e-08, 0.01, 10
VMEM_LIMIT = 58 * 1024 * 1024


def _cp(*sem):
    return pltpu.CompilerParams(dimension_semantics=sem or None, vmem_limit_bytes=VMEM_LIMIT)


def _dot(a, b):
    return jnp.dot(a, b, preferred_element_type=F32)


def _dot_nt(a, b):
    return lax.dot_general(a, b, (((1,), (1,)), ((), ())), preferred_element_type=F32)


def _dot_tn(a, b):
    return lax.dot_general(a, b, (((0,), (0,)), ((), ())), preferred_element_type=F32)


def _dot_hi(a, b):
    return jnp.dot(a, b, precision=HI, preferred_element_type=F32)


def _sigmoid(x):
    return 1.0 / (1.0 + jnp.exp(-x))


def _iota(shape, dim):
    return lax.broadcasted_iota(jnp.int32, shape, dim)


def _load_resident(hbm_ref, vmem_ref, sem):
    @pl.when(pl.program_id(0) == 0)
    def _():
        cp = pltpu.make_async_copy(hbm_ref, vmem_ref, sem)
        cp.start()
        cp.wait()


def _swap32(t):
    lane = _iota(t.shape, 1)
    return jnp.where((lane & 63) < 32, pltpu.roll(t, 96, 1), pltpu.roll(t, 32, 1))


def _rope_fwd(t, cos, sin_s):
    return t * cos + _swap32(t) * sin_s


def _rope_bwd(t, cos, sin_s):
    return t * cos - _swap32(t) * sin_s


DEP_SPEC = pl.BlockSpec((8, 128), lambda *_: (0, 0))


def _rope_tables(pos_row, inv_freq_col, tm, dep):
    T = pos_row.shape[1]
    lane, row = jnp.arange(128)[None, :], jnp.arange(96)[:, None]
    pick = (lane % 32) == (row % 32)
    sel_cos = pick.astype(BF16)
    sel_sin = jnp.where(pick, jnp.where(lane % 64 < 32, -1.0, 1.0), 0.0).astype(BF16)

    def body(p_ref, f_ref, sc_ref, ss_ref, dep_ref, cos_ref, sin_ref):
        ang = f_ref[...] * p_ref[...].astype(F32)
        cos_ref[...] = _dot_tn(_pieces(jnp.cos(ang), 3, 0), sc_ref[...])
        sin_ref[...] = _dot_tn(_pieces(jnp.sin(ang), 3, 0), ss_ref[...])

    full = lambda a: pl.BlockSpec(a.shape, lambda i: (0,) * a.ndim)
    return pl.pallas_call(
        body, name="rope_tables", grid=(T // tm,),
        in_specs=[pl.BlockSpec((1, tm), lambda i: (0, i)), full(inv_freq_col), full(sel_cos), full(sel_sin), DEP_SPEC],
        out_specs=[pl.BlockSpec((tm, 128), lambda i: (i, 0))] * 2,
        out_shape=[jax.ShapeDtypeStruct((T, 128), F32)] * 2,
        compiler_params=_cp("parallel"),
    )(pos_row, inv_freq_col, sel_cos, sel_sin, dep)


def _in_proj_fwd(x, cos, sin_s, mod6, norm1_w, w_pad, tm, dep):
    T = x.shape[0]

    def body(x_ref, cos_ref, sin_ref, mod_ref, nw_ref, w_hbm, dep_ref, qkv_ref, z_ref, xbc_ref, dt_ref, h_ref, w_vmem,
             sem):
        _load_resident(w_hbm, w_vmem, sem)
        xv = x_ref[...]
        r = lax.rsqrt(jnp.mean(xv * xv, axis=-1, keepdims=True) + EPS)
        h = (xv * r * nw_ref[...]) * (1.0 + mod_ref[1:2, :]) + mod_ref[0:1, :]
        hb = h.astype(BF16)
        h_ref[...] = hb
        proj = _dot(hb, w_vmem[...])
        cs, sn = cos_ref[...], sin_ref[...]
        for j in range(5):
            qkv_ref[:, 128 * j:128 * (j + 1)] = _rope_fwd(proj[:, 128 * j:128 * (j + 1)], cs, sn).astype(BF16)
        qkv_ref[:, 640:768] = proj[:, 640:768].astype(BF16)
        z_ref[...] = proj[:, 768:1280]
        xbc_ref[...] = proj[:, 1280:2304]
        dt_ref[...] = proj[:, 2304:2432]

    row = lambda w: pl.BlockSpec((tm, w), lambda i: (i, 0))
    full = lambda a: pl.BlockSpec(a.shape, lambda i: (0,) * a.ndim)
    return pl.pallas_call(
        body, name="in_proj_fwd", grid=(T // tm,),
        in_specs=[row(D), row(128), row(128), full(mod6), full(norm1_w), pl.BlockSpec(memory_space=pl.ANY), DEP_SPEC],
        out_specs=[row(768), row(512), row(1024), row(128), row(D)],
        out_shape=[jax.ShapeDtypeStruct((T, 768), BF16), jax.ShapeDtypeStruct((T, 512), F32),
                   jax.ShapeDtypeStruct((T, 1024), F32), jax.ShapeDtypeStruct((T, 128), F32),
                   jax.ShapeDtypeStruct((T, D), BF16)],
        scratch_shapes=[pltpu.VMEM((D, IN_PAD), BF16), pltpu.SemaphoreType.DMA],
        compiler_params=_cp("arbitrary"),
    )(x, cos, sin_s, mod6, norm1_w, w_pad, dep)


def _head_variants(pair, j):
    lane = _iota(pair.shape, 1)
    lo = lane < 64
    kv = j // 2
    ev = jnp.where(lo, pair, 0.0)
    od = jnp.where(lo, 0.0, pair)
    if kv == 0:
        od = pltpu.roll(od, 64, 1)
    else:
        ev = pltpu.roll(ev, 64, 1)
    return ev.astype(BF16), od.astype(BF16)


def _kv_variants(vcat):
    lane = _iota(vcat.shape, 1)
    lo = lane < 64
    v0 = jnp.where(lo, vcat, 0.0)
    v1 = jnp.where(lo, 0.0, vcat)
    out = {
        (0, 0): v0, (0, 1): pltpu.roll(v0, 64, 1),
        (1, 0): pltpu.roll(v1, 64, 1), (1, 1): v1,
    }
    return {k: v.astype(BF16) for k, v in out.items()}


def _fold_masks(n):
    upper = _iota((BLK, BLK), 1) > _iota((BLK, BLK), 0)
    return upper, upper & (n == 0)


def _attn_fwd(qkv, sinks):
    T = qkv.shape[0]
    nsteps = T // (CPS * BLK)

    def body(sink_ref, q_ref, kc_ref, kp_ref, vc_ref, vp_ref, o_ref, lse_ref):
        for sub in range(CPS):
            rows, before = slice(BLK * sub, BLK * (sub + 1)), slice(BLK * (sub - 1), BLK * sub)
            block(pl.program_id(0) * CPS + sub, sink_ref, q_ref.at[rows, :], kc_ref.at[rows, :],
                  kp_ref if sub == 0 else kc_ref.at[before, :], vc_ref.at[rows, :],
                  vp_ref if sub == 0 else vc_ref.at[before, :], o_ref.at[rows, :], lse_ref.at[rows, :])

    def block(n, sink_ref, q_ref, kc_ref, kp_ref, vc_ref, vp_ref, o_ref, lse_ref):
        vpv = _kv_variants(vp_ref[...].astype(F32))
        vcv = _kv_variants(vc_ref[...].astype(F32))
        q_all = jnp.concatenate(
            [v for j in range(4) for v in _head_variants(q_ref[:, 128 * j:128 * (j + 1)].astype(F32), j)], axis=0)
        s_prev = _dot_nt(q_all, kp_ref[...])
        s_cur = _dot_nt(q_all, kc_ref[...])
        upper, dead = _fold_masks(n)
        lane = _iota((BLK, 128), 1)
        lse_acc = jnp.zeros((BLK, 128), F32)
        for jj in range(4):
            acc = jnp.zeros((BLK, 128), F32)
            for par in range(2):
                h = 2 * jj + par
                rows = slice(h * BLK, (h + 1) * BLK)
                sink = sink_ref[0, h]
                s = jnp.where(dead, NEG, jnp.where(upper, s_prev[rows], s_cur[rows]) * 0.125)
                m = jnp.maximum(jnp.max(s, axis=1, keepdims=True), sink)
                p = jnp.exp(s - m)
                den = jnp.sum(p, axis=1, keepdims=True) + jnp.exp(sink - m)
                pn = p * (1.0 / den)
                acc = (acc + _dot(jnp.where(upper, pn, 0.0).astype(BF16), vpv[(jj // 2, par)])
                       + _dot(jnp.where(upper, 0.0, pn).astype(BF16), vcv[(jj // 2, par)]))
                lse_acc = jnp.where(lane == h, m + jnp.log(den), lse_acc)
            o_ref[:, 128 * jj:128 * (jj + 1)] = acc.astype(BF16)
        lse_ref[...] = lse_acc

    RB = CPS * BLK
    prev = lambda n: jnp.maximum(n * CPS - 1, 0)
    return pl.pallas_call(
        body, name="attn_fwd", grid=(nsteps,),
        in_specs=[pl.BlockSpec(memory_space=pltpu.SMEM),
                  pl.BlockSpec((RB, 512), lambda n: (n, 0)),
                  pl.BlockSpec((RB, 128), lambda n: (n, 4)),
                  pl.BlockSpec((BLK, 128), lambda n: (prev(n), 4)),
                  pl.BlockSpec((RB, 128), lambda n: (n, 5)),
                  pl.BlockSpec((BLK, 128), lambda n: (prev(n), 5))],
        out_specs=[pl.BlockSpec((RB, 512), lambda n: (n, 0)), pl.BlockSpec((RB, 128), lambda n: (n, 0))],
        out_shape=[jax.ShapeDtypeStruct((T, 512), BF16), jax.ShapeDtypeStruct((T, 128), F32)],
        compiler_params=_cp("parallel"),
    )(sinks, qkv, qkv, qkv, qkv, qkv)


def _attn_bwd(qkv, sinks, lse, dmix, cos, sin_s, dep):
    T = qkv.shape[0]
    nb = T // BLK

    def body(sink_ref, q_ref, kc_ref, kp_ref, vc_ref, vp_ref, lse_ref, do_ref, cq_ref, sq_ref, ck_ref, sk_ref,
             dep_ref, out_ref, ds_ref, dq_car, dk_car, dv_car):
        n = pl.program_id(0)
        lane = _iota((BLK, 128), 1)

        @pl.when(n == 0)
        def _():
            ds_ref[...] = jnp.zeros_like(ds_ref)
            dq_car[...] = jnp.zeros_like(dq_car)
            dk_car[...] = jnp.zeros_like(dk_car)
            dv_car[...] = jnp.zeros_like(dv_car)

        @pl.when(n < nb)
        def _():
            kp, kc, vp, vc = kp_ref[...], kc_ref[...], vp_ref[...], vc_ref[...]
            kpv = _kv_variants(kp.astype(F32))
            kcv = _kv_variants(kc.astype(F32))
            lse_v = lse_ref[...]
            q_all = jnp.concatenate(
                [v for j in range(4) for v in _head_variants(q_ref[:, 128 * j:128 * (j + 1)].astype(F32), j)], axis=0)
            do_all = jnp.concatenate(
                [v for j in range(4) for v in _head_variants(do_ref[:, 128 * j:128 * (j + 1)], j)], axis=0)
            s_prev, s_cur = _dot_nt(q_all, kp), _dot_nt(q_all, kc)
            dp_prev, dp_cur = _dot_nt(do_all, vp), _dot_nt(do_all, vc)
            upper, dead = _fold_masks(n)
            out_ref[:, 0:512] = dq_car[...]
            dsk = jnp.zeros((1, 128), F32)
            ds_u, ds_l, p_u, p_l = [], [], [], []
            for jj in range(4):
                dq_acc = jnp.zeros((BLK, 128), F32)
                for par in range(2):
                    h = 2 * jj + par
                    rows = slice(h * BLK, (h + 1) * BLK)
                    lse_h = jnp.sum(jnp.where(lane == h, lse_v, 0.0), axis=1, keepdims=True)
                    s = jnp.where(dead, NEG, jnp.where(upper, s_prev[rows], s_cur[rows]) * 0.125)
                    p = jnp.exp(s - lse_h)
                    dp = jnp.where(upper, dp_prev[rows], dp_cur[rows])
                    delta = jnp.sum(p * dp, axis=1, keepdims=True)
                    ds = p * (dp - delta) * 0.125
                    dsu, dsl = jnp.where(upper, ds, 0.0).astype(BF16), jnp.where(upper, 0.0, ds).astype(BF16)
                    dq_acc = dq_acc + _dot(dsu, kpv[(jj // 2, par)]) + _dot(dsl, kcv[(jj // 2, par)])
                    ds_u.append(dsu)
                    ds_l.append(dsl)
                    p_u.append(jnp.where(upper, p, 0.0).astype(BF16))
                    p_l.append(jnp.where(upper, 0.0, p).astype(BF16))
                    dsk = dsk + jnp.where(lane[0:1] == h, -jnp.sum(jnp.exp(sink_ref[0, h] - lse_h) * delta), 0.0)
                dq_car[:, 128 * jj:128 * (jj + 1)] = _rope_bwd(dq_acc, cq_ref[...], sq_ref[...]).astype(BF16)
            stack = lambda parts: jnp.concatenate(parts, axis=0)
            dk_prev, dk_cur = _dot_tn(stack(ds_u), q_all), _dot_tn(stack(ds_l), q_all)
            dv_prev, dv_cur = _dot_tn(stack(p_u), do_all), _dot_tn(stack(p_l), do_all)
            ds_ref[...] += dsk
            out_ref[:, 512:640] = _rope_bwd(dk_car[...] + dk_prev, ck_ref[...], sk_ref[...]).astype(BF16)
            out_ref[:, 640:768] = (dv_car[...] + dv_prev).astype(BF16)
            dk_car[...] = dk_cur
            dv_car[...] = dv_cur

        @pl.when(n == nb)
        def _():
            out_ref[:, 0:512] = dq_car[...]
            out_ref[:, 512:640] = _rope_bwd(dk_car[...], ck_ref[...], sk_ref[...]).astype(BF16)
            out_ref[:, 640:768] = dv_car[...].astype(BF16)

    cur = lambda n: jnp.minimum(n, nb - 1)
    prev = lambda n: jnp.maximum(cur(n) - 1, 0)
    outb = lambda n: jnp.maximum(n - 1, 0)
    return pl.pallas_call(
        body, name="attn_bwd", grid=(nb + 1,),
        in_specs=[pl.BlockSpec(memory_space=pltpu.SMEM),
                  pl.BlockSpec((BLK, 512), lambda n: (cur(n), 0)),
                  pl.BlockSpec((BLK, 128), lambda n: (cur(n), 4)),
                  pl.BlockSpec((BLK, 128), lambda n: (prev(n), 4)),
                  pl.BlockSpec((BLK, 128), lambda n: (cur(n), 5)),
                  pl.BlockSpec((BLK, 128), lambda n: (prev(n), 5)),
                  pl.BlockSpec((BLK, 128), lambda n: (cur(n), 0)),
                  pl.BlockSpec((BLK, 512), lambda n: (cur(n), 0)),
                  pl.BlockSpec((BLK, 128), lambda n: (cur(n), 0)),
                  pl.BlockSpec((BLK, 128), lambda n: (cur(n), 0)),
                  pl.BlockSpec((BLK, 128), lambda n: (outb(n), 0)),
                  pl.BlockSpec((BLK, 128), lambda n: (outb(n), 0)), DEP_SPEC],
        out_specs=[pl.BlockSpec((BLK, 768), lambda n: (outb(n), 0)), pl.BlockSpec((1, 128), lambda n: (0, 0))],
        out_shape=[jax.ShapeDtypeStruct((T, 768), BF16), jax.ShapeDtypeStruct((1, 128), F32)],
        scratch_shapes=[pltpu.VMEM((BLK, 512), BF16), pltpu.VMEM((BLK, 128), F32), pltpu.VMEM((BLK, 128), F32)],
        compiler_params=_cp("arbitrary"),
    )(sinks, qkv, qkv, qkv, qkv, qkv, lse, dmix, cos, sin_s, cos, sin_s, dep)


def _ssd_mats():
    e = jnp.arange(SW)[None, :] // HD == jnp.arange(128)[:, None]
    tri = jnp.arange(BLK)[None, :] <= jnp.arange(BLK)[:, None]
    return (jnp.tile(e, (3, 1)).astype(BF16), jnp.tile(e.T, (2, 1)).astype(BF16),
            jnp.tile(tri, (1, 3)).astype(BF16), jnp.tile(tri.T, (1, 3)).astype(BF16))


def _pieces(x, n, axis):
    out, r = [], x
    for i in range(n):
        p = r.astype(BF16)
        out.append(p)
        if i + 1 < n:
            r = r - p.astype(F32)
    return jnp.concatenate(out, axis=axis)


def _expand(x, e3):
    return _dot(_pieces(x, 3, 1), e3)


def _head_sums(x, et2):
    return _dot(_pieces(x, 2, 1), et2)


def _run_sum(tri3, x):
    return _dot(tri3, _pieces(x, 3, 0))


def _shift_down(u, tail, j):
    rolled = pltpu.roll(u, j, 0)
    first = jnp.where(_iota(tail.shape, 0) < j, pltpu.roll(tail, j, 0), rolled[0:8])
    return jnp.concatenate([first, rolled[8:]], axis=0)


def _shift_up(d, head, j):
    rolled = pltpu.roll(d, BLK - j, 0)
    last = jnp.where(_iota(head.shape, 0) >= 8 - j, pltpu.roll(head, 8 - j, 0), rolled[BLK - 8:])
    return jnp.concatenate([rolled[:BLK - 8], last], axis=0)


def _ssd_parts(dtr, dtb, alog, e3, tril3):
    xx = dtr + dtb
    dt = jnp.maximum(xx, 0.0) + jnp.log(1.0 + jnp.exp(-jnp.abs(xx)))
    a_neg = -jnp.exp(alog)
    tril = _iota((BLK, BLK), 1) <= _iota((BLK, BLK), 0)
    cs = _run_sum(tril3, dt * a_neg)
    csx = _expand(cs, e3)
    last = csx[BLK - 1:BLK, :]
    return dict(xx=xx, dt=dt, a_neg=a_neg, tril=tril, cs=cs, cs_t=cs.T,
                ecsx=jnp.exp(csx), dtex=jnp.exp(last - csx), cdx=jnp.exp(last), dtx=_expand(dt, e3))


def _decay(parts, h):
    seg = parts["cs"][:, h:h + 1] - parts["cs_t"][h:h + 1, :]
    return jnp.exp(jnp.where(parts["tril"], seg, NEG))


def _group_cols(a, g):
    return a[:, 256 * g:256 * (g + 1)]


def _ssd_fwd(xbc, z, dtr, conv_w, conv_b, dtb, alog, dskx, ssm_w, mats, dep):
    T = xbc.shape[0]
    nc = T // BLK

    def body(u_ref, tail_ref, z_ref, dtr_ref, cw_ref, cb_ref, dtb_ref, al_ref, dk_ref, sw_ref, e3_ref, tril3_ref,
             dep_ref, yn_ref, yp_ref, st_ref, co_ref, s_scr):
        n = pl.program_id(0)

        @pl.when(n == 0)
        def _():
            s_scr[...] = jnp.zeros_like(s_scr)

        lane = _iota((BLK, 128), 1)
        lo = lane < 64
        for sub in range(CPS):
            rows = slice(BLK * sub, BLK * (sub + 1))
            u = u_ref[rows, :]
            tail = jnp.where(n > 0, tail_ref[...], 0.0) if sub == 0 else u_ref[BLK * sub - 8:BLK * sub, :]
            co = cb_ref[...] + cw_ref[3:4, :] * u
            for j in range(1, CONVK):
                co = co + cw_ref[3 - j:4 - j, :] * _shift_down(u, tail, j)
            co_ref[rows, :] = co
            xc = co * _sigmoid(co)
            pt = _ssd_parts(dtr_ref[rows, :], dtb_ref[...], al_ref[...], e3_ref[...], tril3_ref[...])
            xs = xc[:, :SW]
            bm = [xc[:, 512:640].astype(BF16), xc[:, 640:768].astype(BF16)]
            cm = [xc[:, 768:896].astype(BF16), xc[:, 896:1024].astype(BF16)]
            s_in = s_scr[...]
            st_ref[sub] = s_in
            xdt = xs * pt["dtx"]
            xde = (xdt * pt["dtex"]).astype(BF16)
            ys, s_new = [], []
            for g in range(2):
                cb = _dot_nt(cm[g], bm[g])
                yoff = _dot(cm[g], _group_cols(s_in, g).astype(BF16))
                s_new.append(_dot_tn(bm[g], _group_cols(xde, g)))
                for jj in range(2):
                    j = 2 * g + jj
                    chunk = xdt[:, 128 * j:128 * (j + 1)]
                    g_ev = (cb * _decay(pt, 2 * j)).astype(BF16)
                    g_od = (cb * _decay(pt, 2 * j + 1)).astype(BF16)
                    yd = (_dot(g_ev, jnp.where(lo, chunk, 0.0).astype(BF16))
                          + _dot(g_od, jnp.where(lo, 0.0, chunk).astype(BF16)))
                    ys.append(yd + yoff[:, 128 * jj:128 * (jj + 1)] * pt["ecsx"][:, 128 * j:128 * (j + 1)])
            y = jnp.concatenate(ys, axis=1) + xs * dk_ref[...]
            s_scr[...] = s_in * pt["cdx"] + jnp.concatenate(s_new, axis=1)
            yp_ref[rows, :] = y
            zv = z_ref[rows, :]
            yz = y * (zv * _sigmoid(zv))
            outs = []
            for g in range(2):
                yg = _group_cols(yz, g)
                outs.append(yg * lax.rsqrt(jnp.mean(yg * yg, axis=-1, keepdims=True) + EPS))
            yn_ref[rows, :] = (jnp.concatenate(outs, axis=1) * sw_ref[...]).astype(BF16)

    e3, _, tril3, _ = mats
    RB = CPS * BLK
    tail8 = lambda n: jnp.maximum(n * (RB // 8) - 1, 0)
    full = lambda a: pl.BlockSpec(a.shape, lambda n: (0,) * a.ndim)
    return pl.pallas_call(
        body, name="ssd_fwd", grid=(nc // CPS,),
        in_specs=[pl.BlockSpec((RB, CONVC), lambda n: (n, 0)), pl.BlockSpec((8, CONVC), lambda n: (tail8(n), 0)),
                  pl.BlockSpec((RB, SW), lambda n: (n, 0)), pl.BlockSpec((RB, 128), lambda n: (n, 0)),
                  full(conv_w), full(conv_b), full(dtb), full(alog), full(dskx), full(ssm_w), full(e3), full(tril3),
                  DEP_SPEC],
        out_specs=[pl.BlockSpec((RB, SW), lambda n: (n, 0)), pl.BlockSpec((RB, SW), lambda n: (n, 0)),
                   pl.BlockSpec((CPS, NST, SW), lambda n: (n, 0, 0)), pl.BlockSpec((RB, CONVC), lambda n: (n, 0))],
        out_shape=[jax.ShapeDtypeStruct((T, SW), BF16), jax.ShapeDtypeStruct((T, SW), F32),
                   jax.ShapeDtypeStruct((nc, NST, SW), F32), jax.ShapeDtypeStruct((T, CONVC), F32)],
        scratch_shapes=[pltpu.VMEM((NST, SW), F32)],
        compiler_params=_cp("arbitrary"),
    )(xbc, xbc, z, dtr, conv_w, conv_b, dtb, alog, dskx, ssm_w, e3, tril3, dep)


def _ssd_bwd(xbc, co_all, z, dtr, ypre, states, dmix, conv_w, dtb, alog, dskx, ssm_w, mats, dep):
    T = xbc.shape[0]
    nsteps = T // (CPS * BLK)

    def body(*refs):
        per_chunk, consts, out_ref, carried = refs[:7], refs[7:16], refs[17], refs[18:]
        i = pl.program_id(0)

        @pl.when(i == 0)
        def _():
            for r in carried:
                r[...] = jnp.zeros_like(r)

        for sub in reversed(range(CPS)):
            rows = slice(BLK * sub, BLK * (sub + 1))
            views = [r.at[sub:sub + 1] if k == 5 else r.at[rows, :] for k, r in enumerate(per_chunk)]
            chunk(*views, *consts, out_ref.at[rows, :], *carried)

        @pl.when(i == nsteps - 1)
        def _():
            dsk_ref, dskx_scr = carried[3], carried[8]
            dsk_ref[...] = _head_sums(jnp.broadcast_to(dskx_scr[...], (8, SW)), consts[6][...])[0:1]

    def chunk(u_ref, co_ref, z_ref, dtr_ref, yp_ref, st_ref, dyn_ref, cw_ref, dtb_ref, al_ref, dk_ref, sw_ref,
              e3_ref, et2_ref, tril3_ref, triu3_ref,
              out_ref, dcw_ref, dcb_ref, dsw_ref, dsk_ref, ddtb_ref, dav_ref, ds_scr, dco_scr, dskx_scr):
        co = co_ref[...]
        sg = _sigmoid(co)
        xc = co * sg
        pt = _ssd_parts(dtr_ref[...], dtb_ref[...], al_ref[...], e3_ref[...], tril3_ref[...])
        dtx, ecsx, dtex, cdx = pt["dtx"], pt["ecsx"], pt["dtex"], pt["cdx"]
        xs = xc[:, :SW]
        bm = [xc[:, 512:640].astype(BF16), xc[:, 640:768].astype(BF16)]
        cm = [xc[:, 768:896].astype(BF16), xc[:, 896:1024].astype(BF16)]
        s_in = st_ref[0]
        ds_out = ds_scr[...]
        e_t = et2_ref[...]

        zv = z_ref[...]
        sz = _sigmoid(zv)
        silu_z = zv * sz
        ypre = yp_ref[...]
        yz = ypre * silu_z
        dyn = dyn_ref[...]
        sw = sw_ref[...]
        dyz, yns = [], []
        for g in range(2):
            yg = _group_cols(yz, g)
            r = lax.rsqrt(jnp.mean(yg * yg, axis=-1, keepdims=True) + EPS)
            yn = yg * r
            dg = _group_cols(dyn, g) * _group_cols(sw, g)
            dyz.append(r * (dg - yn * jnp.mean(dg * yn, axis=-1, keepdims=True)))
            yns.append(yn)
        dyz = jnp.concatenate(dyz, axis=1)
        dsw_ref[...] += jnp.sum(dyn * jnp.concatenate(yns, axis=1), axis=0, keepdims=True)
        dy = dyz * silu_z
        dz = dyz * ypre * (sz * (1.0 + zv * (1.0 - sz)))

        xdt = xs * dtx
        xdt_b = xdt.astype(BF16)
        edy = (ecsx * dy).astype(BF16)
        xde = (xdt * dtex).astype(BF16)
        lane = _iota((BLK, 128), 1)
        lo = lane < 64
        row8 = _iota((8, 128), 0)
        dcs = jnp.zeros((BLK, 128), F32)
        col_rows = jnp.zeros((8, 128), F32)
        dxdt, bds, yoff, dbs, dcs_g, ds_new = [], [], [], [], [], []
        for g in range(2):
            s_g = _group_cols(s_in, g).astype(BF16)
            dso_g = _group_cols(ds_out, g).astype(BF16)
            cb = _dot_nt(cm[g], bm[g])
            bds.append(_dot(bm[g], dso_g))
            yoff.append(_dot(cm[g], s_g))
            dcb_g = jnp.zeros((BLK, BLK), F32)
            for jj in range(2):
                j = 2 * g + jj
                dy_c = dy[:, 128 * j:128 * (j + 1)]
                xdt_c = xdt_b[:, 128 * j:128 * (j + 1)]
                acc = jnp.zeros((BLK, 128), F32)
                for par in range(2):
                    h = 2 * j + par
                    lm = _decay(pt, h)
                    gm = cb * lm
                    dy_m = (jnp.where(lo, dy_c, 0.0) if par == 0 else jnp.where(lo, 0.0, dy_c)).astype(BF16)
                    dg_h = _dot_nt(dy_m, xdt_c)
                    w_h = dg_h * gm
                    dcs = dcs + jnp.where(lane == h, jnp.sum(w_h, axis=1, keepdims=True), 0.0)
                    col_rows = col_rows + jnp.where(row8 == h, jnp.sum(w_h, axis=0, keepdims=True), 0.0)
                    dcb_g = dcb_g + dg_h * lm
                    acc = acc + _dot_tn(gm.astype(BF16), dy_m)
                dxdt.append(acc)
            dcb_b = dcb_g.astype(BF16)
            dcs_g.append(_dot(dcb_b, bm[g]) + _dot_nt(_group_cols(edy, g), s_g))
            dbs.append(_dot_tn(dcb_b, cm[g]) + _dot_nt(_group_cols(xde, g), dso_g))
            ds_new.append(_dot_tn(cm[g], _group_cols(edy, g)))
        bds = jnp.concatenate(bds, axis=1)
        yoff = jnp.concatenate(yoff, axis=1) * ecsx
        dxdt = jnp.concatenate(dxdt, axis=1) + dtex * bds
        ds_scr[...] = cdx * ds_out + jnp.concatenate(ds_new, axis=1)

        t_m = _head_sums(dtex * xdt * bds, e_t)
        colsum_t = jnp.concatenate([col_rows, jnp.zeros((BLK - 8, 128), F32)], axis=0).T
        cd = jnp.exp(pt["cs"][BLK - 1:BLK, :])
        sds = jnp.sum(s_in * ds_out, axis=0, keepdims=True)
        last_row = jnp.sum(t_m, axis=0, keepdims=True) + cd * _head_sums(jnp.broadcast_to(sds, (8, SW)), e_t)[0:1]
        dcs = dcs - colsum_t + _head_sums(dy * yoff, e_t) - t_m
        dcs = dcs + jnp.where(_iota((BLK, 128), 0) == BLK - 1, last_row, 0.0)
        da = _run_sum(triu3_ref[...], dcs)
        dt = pt["dt"]
        ddt = da * pt["a_neg"] + _head_sums(dxdt * xs, e_t)
        dav_ref[...] += jnp.sum(da * dt, axis=0, keepdims=True)
        ddtr = ddt * _sigmoid(pt["xx"])
        ddtb_ref[...] += jnp.sum(ddtr, axis=0, keepdims=True)
        dxs = dxdt * dtx + dy * dk_ref[...]
        dskx_scr[...] += jnp.sum(dy * xs, axis=0, keepdims=True)
        dxc = jnp.concatenate([dxs, dbs[0], dbs[1], dcs_g[0], dcs_g[1]], axis=1)
        dco = dxc * (sg * (1.0 + co * (1.0 - sg)))

        dcb_ref[...] += jnp.sum(dco, axis=0, keepdims=True)
        u = u_ref[...]
        head = dco_scr[...]
        du = jnp.zeros_like(dco)
        for j in range(CONVK):
            up_j = dco if j == 0 else _shift_up(dco, head, j)
            dcw_ref[3 - j:4 - j, :] += jnp.sum(up_j * u, axis=0, keepdims=True)
            du = du + cw_ref[3 - j:4 - j, :] * up_j
        dco_scr[...] = dco[0:8]
        out_ref[:, 0:512] = dz.astype(BF16)
        out_ref[:, 512:1536] = du.astype(BF16)
        out_ref[:, 1536:1664] = ddtr.astype(BF16)

    e3, et2, tril3, triu3 = mats
    RB = CPS * BLK
    rev = lambda i: nsteps - 1 - i
    full = lambda a: pl.BlockSpec(a.shape, lambda i: (0,) * a.ndim)
    acc = lambda r, c: pl.BlockSpec((r, c), lambda i: (0, 0))
    return pl.pallas_call(
        body, name="ssd_bwd", grid=(nsteps,),
        in_specs=[pl.BlockSpec((RB, CONVC), lambda i: (rev(i), 0)), pl.BlockSpec((RB, CONVC), lambda i: (rev(i), 0)),
                  pl.BlockSpec((RB, SW), lambda i: (rev(i), 0)), pl.BlockSpec((RB, 128), lambda i: (rev(i), 0)),
                  pl.BlockSpec((RB, SW), lambda i: (rev(i), 0)), pl.BlockSpec((CPS, NST, SW), lambda i: (rev(i), 0, 0)),
                  pl.BlockSpec((RB, SW), lambda i: (rev(i), 1)),
                  full(conv_w), full(dtb), full(alog), full(dskx), full(ssm_w),
                  full(e3), full(et2), full(tril3), full(triu3), DEP_SPEC],
        out_specs=[pl.BlockSpec((RB, 1664), lambda i: (rev(i), 0)),
                   acc(CONVK, CONVC), acc(1, CONVC), acc(1, SW), acc(1, 128), acc(1, 128), acc(1, 128)],
        out_shape=[jax.ShapeDtypeStruct((T, 1664), BF16),
                   jax.ShapeDtypeStruct((CONVK, CONVC), F32), jax.ShapeDtypeStruct((1, CONVC), F32),
                   jax.ShapeDtypeStruct((1, SW), F32), jax.ShapeDtypeStruct((1, 128), F32),
                   jax.ShapeDtypeStruct((1, 128), F32), jax.ShapeDtypeStruct((1, 128), F32)],
        scratch_shapes=[pltpu.VMEM((NST, SW), F32), pltpu.VMEM((8, CONVC), F32), pltpu.VMEM((1, SW), F32)],
        compiler_params=_cp("arbitrary"),
    )(xbc, co_all, z, dtr, ypre, states, dmix, conv_w, dtb, alog, dskx, ssm_w, e3, et2, tril3, triu3, dep)


def _mix_ffn(x, attn, ynorm, tgt, mod6, norm2_w, final_w, w_out, w_gu, w_gu_own, s_arr, w_dn, tm):
    T = x.shape[0]
    nt = T // tm

    def body(x_ref, a_ref, y_ref, t_ref, mod_ref, n2_ref, fw_ref, wo_hbm, wgu_hbm, own_hbm, s_ref, wdn_hbm,
             sq_ref, dmix_ref, dx1_ref, h2_ref, act_ref, df_ref, dgu_ref, do_ref, sm_ref,
             wo, wgu, wdn, sems):
        i = pl.program_id(0)

        @pl.when(i == 0)
        def _():
            cps = [pltpu.make_async_copy(s, d, sems.at[k]) for k, (s, d) in
                   enumerate(((wo_hbm, wo), (wgu_hbm, wgu), (wdn_hbm, wdn)))]
            for c in cps:
                c.start()
            for c in cps:
                c.wait()
            own = pltpu.make_async_copy(
                own_hbm, wgu.at[:, pl.ds(pl.multiple_of(s_ref[0] * GU_SH, 128), GU_SH)], sems.at[3])
            own.start()
            own.wait()
            sq_ref[...] = jnp.zeros_like(sq_ref)
            sm_ref[...] = jnp.zeros_like(sm_ref)

        gate1, shift2, scale2, gate2 = mod_ref[2:3, :], mod_ref[3:4, :], mod_ref[4:5, :], mod_ref[5:6, :]
        n2w, fw = n2_ref[...], fw_ref[...]
        o = _dot(a_ref[...], wo[0:AW, :]) + _dot(y_ref[...], wo[AW:D, :])
        x1 = x_ref[...] + gate1 * o
        r2 = lax.rsqrt(jnp.mean(x1 * x1, axis=-1, keepdims=True) + EPS)
        xh2 = x1 * r2
        n2 = xh2 * n2w
        h2b = (n2 * (1.0 + scale2) + shift2).astype(BF16)
        h2_ref[...] = h2b
        f = jnp.zeros((tm, D), F32)
        saved = []
        for a, b in FF_SPLITS:
            gp = _dot(h2b, wgu[:, a:b])
            upj = _dot(h2b, wgu[:, DFF + a:DFF + b])
            sg = _sigmoid(gp)
            sl = gp * sg
            actb = (sl * upj).astype(BF16)
            act_ref[:, a:b] = actb
            f = f + _dot(actb, wdn[a:b, :])
            saved.append((gp, upj, sg, sl))
        x2 = x1 + gate2 * f
        r3 = lax.rsqrt(jnp.mean(x2 * x2, axis=-1, keepdims=True) + EPS)
        xh3 = x2 * r3
        err = xh3 * fw - t_ref[...]
        sq_ref[...] += jnp.sum(err * err, axis=0, keepdims=True)
        dy = err * (1.0 / D)
        dfw = jnp.sum(dy * xh3, axis=0, keepdims=True)
        dxh3 = dy * fw
        dx2 = r3 * (dxh3 - xh3 * jnp.mean(dxh3 * xh3, axis=-1, keepdims=True))
        dgate2 = jnp.sum(dx2 * f, axis=0, keepdims=True)
        dfb = (dx2 * gate2).astype(BF16)
        df_ref[...] = dfb
        dh2 = jnp.zeros((tm, D), F32)
        for (a, b), (gp, upj, sg, sl) in zip(FF_SPLITS, saved):
            dact = _dot_nt(dfb, wdn[a:b, :])
            dg = (dact * upj * (sg * (1.0 + gp * (1.0 - sg)))).astype(BF16)
            du = (dact * sl).astype(BF16)
            dgu_ref[:, a:b] = dg
            dgu_ref[:, DFF + a:DFF + b] = du
            dh2 = dh2 + _dot_nt(dg, wgu[:, a:b]) + _dot_nt(du, wgu[:, DFF + a:DFF + b])
        dshift2 = jnp.sum(dh2, axis=0, keepdims=True)
        dscale2 = jnp.sum(dh2 * n2, axis=0, keepdims=True)
        dn2 = dh2 * (1.0 + scale2)
        dn2w = jnp.sum(dn2 * xh2, axis=0, keepdims=True)
        dxh2 = dn2 * n2w
        dx1 = dx2 + r2 * (dxh2 - xh2 * jnp.mean(dxh2 * xh2, axis=-1, keepdims=True))
        dx1_ref[...] = dx1
        dgate1 = jnp.sum(dx1 * o, axis=0, keepdims=True)
        dob = (dx1 * gate1).astype(BF16)
        do_ref[...] = dob
        dmix_ref[...] = _dot_nt(dob, wo[...])
        sm_ref[...] += jnp.concatenate(
            [dfw, dn2w, dshift2, dscale2, dgate2, dgate1, jnp.zeros((2, D), F32)], axis=0)

    row = lambda w: pl.BlockSpec((tm, w), lambda i: (i, 0))
    full = lambda a: pl.BlockSpec(a.shape, lambda i: (0,) * a.ndim)
    anyspec = pl.BlockSpec(memory_space=pl.ANY)
    return pl.pallas_call(
        body, name="mix_ffn", grid=(nt,),
        in_specs=[row(D), row(AW), row(SW), row(D), full(mod6), full(norm2_w), full(final_w), anyspec, anyspec, anyspec,
                  pl.BlockSpec(memory_space=pltpu.SMEM), anyspec],
        out_specs=[pl.BlockSpec((1, D), lambda i: (0, 0)), row(D), row(D), row(D),
                   row(DFF), row(D), row(2 * DFF), row(D), pl.BlockSpec((8, D), lambda i: (0, 0))],
        out_shape=[jax.ShapeDtypeStruct((1, D), F32), jax.ShapeDtypeStruct((T, D), F32), jax.ShapeDtypeStruct((T, D), F32),
                   jax.ShapeDtypeStruct((T, D), BF16), jax.ShapeDtypeStruct((T, DFF), BF16),
                   jax.ShapeDtypeStruct((T, D), BF16), jax.ShapeDtypeStruct((T, 2 * DFF), BF16),
                   jax.ShapeDtypeStruct((T, D), BF16), jax.ShapeDtypeStruct((8, D), F32)],
        scratch_shapes=[pltpu.VMEM((D, D), BF16), pltpu.VMEM((D, 2 * DFF), BF16), pltpu.VMEM((DFF, D), BF16),
                        pltpu.SemaphoreType.DMA((4,))],
        compiler_params=_cp("arbitrary"),
    )(x, attn, ynorm, tgt, mod6, norm2_w, final_w, w_out, w_gu, w_gu_own, s_arr, w_dn)


def _in_proj_bwd(x, dx1, dqkv, dzxd, mod6, norm1_w, w_pad, tm, dep):
    T = x.shape[0]

    def body(x_ref, dx1_ref, dq_ref, dz_ref, mod_ref, nw_ref, w_hbm, dep_ref, gx_ref, sm_ref, w_vmem, sem):
        _load_resident(w_hbm, w_vmem, sem)

        @pl.when(pl.program_id(0) == 0)
        def _():
            sm_ref[...] = jnp.zeros_like(sm_ref)

        nw = nw_ref[...]
        scale1 = mod_ref[1:2, :]
        sums = jnp.zeros((8, D), F32)
        for rows in (slice(0, tm // 2), slice(tm // 2, tm)):
            dh = _dot_nt(dq_ref[rows, :], w_vmem[:, 0:768]) + _dot_nt(dz_ref[rows, :], w_vmem[:, 768:IN_PAD])
            xv = x_ref[rows, :]
            r = lax.rsqrt(jnp.mean(xv * xv, axis=-1, keepdims=True) + EPS)
            xh = xv * r
            n1 = xh * nw
            dshift = jnp.sum(dh, axis=0, keepdims=True)
            dscale = jnp.sum(dh * n1, axis=0, keepdims=True)
            dn = dh * (1.0 + scale1)
            dnw = jnp.sum(dn * xh, axis=0, keepdims=True)
            dxh = dn * nw
            gx_ref[rows, :] = dx1_ref[rows, :] + r * (dxh - xh * jnp.mean(dxh * xh, axis=-1, keepdims=True))
            sums = sums + jnp.concatenate([dnw, dshift, dscale, jnp.zeros((5, D), F32)], axis=0)
        sm_ref[...] += sums

    row = lambda w: pl.BlockSpec((tm, w), lambda i: (i, 0))
    full = lambda a: pl.BlockSpec(a.shape, lambda i: (0,) * a.ndim)
    return pl.pallas_call(
        body, name="in_proj_bwd", grid=(T // tm,),
        in_specs=[row(D), row(D), row(768), row(1664), full(mod6), full(norm1_w), pl.BlockSpec(memory_space=pl.ANY),
                  DEP_SPEC],
        out_specs=[row(D), pl.BlockSpec((8, D), lambda i: (0, 0))],
        out_shape=[jax.ShapeDtypeStruct((T, D), F32), jax.ShapeDtypeStruct((8, D), F32)],
        scratch_shapes=[pltpu.VMEM((D, IN_PAD), BF16), pltpu.SemaphoreType.DMA],
        compiler_params=_cp("arbitrary"),
    )(x, dx1, dqkv, dzxd, mod6, norm1_w, w_pad, dep)


def _tn_matmul(a, b, K, N, tt, name, dep):
    T = a.shape[0]
    ja, jb = a.shape[1] // K, b.shape[1] // N
    J = max(ja, jb)

    def body(a_ref, b_ref, dep_ref, o_ref):
        t = pl.program_id(1)
        prod = _dot_tn(a_ref[...], b_ref[...])

        @pl.when(t == 0)
        def _():
            o_ref[0] = prod

        @pl.when(t > 0)
        def _():
            o_ref[0] += prod

    return pl.pallas_call(
        body, name=name, grid=(J, T // tt),
        in_specs=[pl.BlockSpec((tt, K), lambda j, t: (t, j if ja > 1 else 0)),
                  pl.BlockSpec((tt, N), lambda j, t: (t, j if jb > 1 else 0)),
                  pl.BlockSpec((8, 128), lambda j, t: (0, 0))],
        out_specs=pl.BlockSpec((1, K, N), lambda j, t: (j, 0, 0)),
        out_shape=jax.ShapeDtypeStruct((J, K, N), F32),
        compiler_params=_cp("parallel", "arbitrary"),
    )(a, b, dep)


def _adam_math(w, g, m, v):
    m = B1 * m + (1.0 - B1) * g
    v = B2 * v + (1.0 - B2) * (g * g)
    m_hat = m / (1.0 - B1 ** STEP)
    v_hat = v / (1.0 - B2 ** STEP)
    delta = -LR * (m_hat / (jnp.sqrt(v_hat) + AEPS) + WD * w)
    return delta, m, v


def _adam_2d(w, mine, land, m, v, c_arr, rb, name):
    R, C = w.shape
    nbh = R // 2 // rb

    def body(c_ref, w_ref, mine_ref, land_ref, m_ref, v_ref, go_ref, d_ref, mo_ref, vo_ref):
        g = jnp.where(pl.program_id(0) // nbh == c_ref[0], mine_ref[...], land_ref[...])
        d, mn, vn = _adam_math(w_ref[...], g, m_ref[...], v_ref[...])
        go_ref[...] = g
        d_ref[...] = d
        mo_ref[...] = mn
        vo_ref[...] = vn

    spec = pl.BlockSpec((rb, C), lambda i, c_ref: (i, 0))
    mine_spec = pl.BlockSpec((rb, C), lambda i, c_ref: (jnp.clip(i - c_ref[0] * nbh, 0, nbh - 1), 0))
    return pl.pallas_call(
        body, name=name,
        grid_spec=pltpu.PrefetchScalarGridSpec(
            num_scalar_prefetch=1, grid=(R // rb,), in_specs=[spec, mine_spec, spec, spec, spec], out_specs=[spec] * 4),
        out_shape=[jax.ShapeDtypeStruct((R, C), F32)] * 4, compiler_params=_cp("parallel"),
    )(c_arr, w, mine, land, m, v)


def _adam_w_in(w3, mine, land, m3, v3, c_arr):
    n = w3.shape[0]

    def body(c_ref, w_hbm, mine_ref, land_ref, m_hbm, v_hbm, g_hbm, d_hbm, mo_hbm, vo_hbm, bufs, sems):
        ins = [pltpu.make_async_copy(src.at[:, 0], bufs.at[k], sems.at[k]) for k, src in enumerate((w_hbm, m_hbm, v_hbm))]
        for cp in ins:
            cp.start()
        half = D // 2
        top = jnp.where(c_ref[0] == 0, mine_ref[...], land_ref[0:half, :])
        bot = jnp.where(c_ref[0] == 1, mine_ref[...], land_ref[half:D, :])
        g = jnp.concatenate([top, bot], axis=0)
        eye = (_iota((D, D), 0) == _iota((D, D), 1)).astype(BF16)
        g_t = jnp.zeros((n, D), F32)
        r = g
        for i in range(3):
            p = r.astype(BF16)
            g_t = g_t + _dot_tn(p, eye)
            if i < 2:
                r = r - p.astype(F32)
        for cp in ins:
            cp.wait()
        d, mn, vn = _adam_math(bufs[0], g_t, bufs[1], bufs[2])
        for k, val in enumerate((g_t, d, mn, vn)):
            bufs[3 + k] = val
        outs = [pltpu.make_async_copy(bufs.at[3 + k], dst.at[:, 0], sems.at[3 + k])
                for k, dst in enumerate((g_hbm, d_hbm, mo_hbm, vo_hbm))]
        for cp in outs:
            cp.start()
        for cp in outs:
            cp.wait()

    anyspec = pl.BlockSpec(memory_space=pl.ANY)
    vm = pl.BlockSpec(memory_space=pltpu.VMEM)
    return pl.pallas_call(
        body, name="adam_w_in",
        in_specs=[pl.BlockSpec(memory_space=pltpu.SMEM), anyspec, vm, vm, anyspec, anyspec], out_specs=[anyspec] * 4,
        out_shape=[jax.ShapeDtypeStruct(w3.shape, F32)] * 4,
        scratch_shapes=[pltpu.VMEM((7, n, D), F32), pltpu.SemaphoreType.DMA((7,))],
        compiler_params=pltpu.CompilerParams(vmem_limit_bytes=VMEM_LIMIT),
    )(c_arr, w3, mine, land, m3, v3)


def _adam_w_ada(sc_all, dmod_s, w, m, v, rb):
    R, C = w.shape

    def body(sc_ref, dm_ref, w_ref, m_ref, v_ref, g_ref, d_ref, mo_ref, vo_ref):
        g = lax.dot_general(sc_ref[...], dm_ref[...], (((0,), (0,)), ((), ())), precision=HI, preferred_element_type=F32)
        d, mn, vn = _adam_math(w_ref[...], g, m_ref[...], v_ref[...])
        g_ref[...] = g
        d_ref[...] = d
        mo_ref[...] = mn
        vo_ref[...] = vn

    spec = pl.BlockSpec((rb, C), lambda i: (i, 0))
    return pl.pallas_call(
        body, name="adam_w_ada", grid=(R // rb,),
        in_specs=[pl.BlockSpec((8, rb), lambda i: (0, i)), pl.BlockSpec((8, C), lambda i: (0, 0)), spec, spec, spec],
        out_specs=[spec] * 4, out_shape=[jax.ShapeDtypeStruct((R, C), F32)] * 4, compiler_params=_cp("parallel"),
    )(sc_all, dmod_s, w, m, v)


def _adam_small(grads, ws, ms, vs):
    k = len(ws)

    def body(*refs):
        g, w, m, v = refs[0:k], refs[k:2 * k], refs[2 * k:3 * k], refs[3 * k:4 * k]
        g_o, d_o, m_o, v_o = refs[4 * k:5 * k], refs[5 * k:6 * k], refs[6 * k:7 * k], refs[7 * k:8 * k]
        for i in range(k):
            gi = g[i][...]
            d, mn, vn = _adam_math(w[i][...], gi, m[i][...], v[i][...])
            g_o[i][...] = gi
            d_o[i][...] = d
            m_o[i][...] = mn
            v_o[i][...] = vn

    shapes = [jax.ShapeDtypeStruct(w.shape, F32) for w in ws]
    vm = pl.BlockSpec(memory_space=pltpu.VMEM)
    outs = pl.pallas_call(
        body, name="adam_small", in_specs=[vm] * (4 * k), out_specs=[vm] * (4 * k), out_shape=shapes * 4,
    )(*grads, *ws, *ms, *vs)
    return outs[0:k], outs[k:2 * k], outs[2 * k:3 * k], outs[3 * k:4 * k]


def _pos():
    return lax.axis_index("x"), lax.axis_index("y"), lax.axis_index("c")


def _flip(v, bit):
    return 1 - v if bit else v


def _peer(k):
    x, y, c = _pos()
    return (_flip(x, (k >> 2) & 1), _flip(y, (k >> 1) & 1), _flip(c, k & 1))


def _logical(p):
    return 4 * p[0] + 2 * p[1] + p[2]


def _gather8(src_ref, dst_ref, send_sems, recv_sems):
    me = _logical(_pos())
    dst_ref[pl.ds(me, 1)] = src_ref[...][None]
    copies = []
    for k in range(1, 8):
        cp = pltpu.make_async_remote_copy(src_ref, dst_ref.at[me], send_sems.at[k - 1], recv_sems.at[k - 1],
                                          device_id=_peer(k), device_id_type=MESH)
        cp.start()
        copies.append(cp)
    for k in range(1, 8):
        pltpu.make_async_remote_copy(src_ref, dst_ref.at[_logical(_peer(k))], send_sems.at[k - 1], recv_sems.at[k - 1],
                                     device_id=_peer(k), device_id_type=MESH).wait_recv()
    for cp in copies:
        cp.wait_send()


def _rows_select(ref3, width):
    row = _iota((8, width), 0)
    out = jnp.zeros((8, width), F32)
    for i in range(8):
        out = jnp.where(row == i, ref3[i][:, 0:width], out)
    return out


def _mod_exchange(payload, w_ada_s, b_ada4):
    n_sh = w_ada_s.shape[1]

    def body(pay_ref, w_ref, b_ref, gat_ref, mod_ref, token, p3, sa, ra, sb, rb):
        token[...] = jnp.zeros_like(token)
        x, y, c = _pos()
        me = _logical((x, y, c))
        my_s = 2 * x + y
        _gather8(pay_ref, gat_ref, sa, ra)
        cmat = _rows_select(gat_ref, D)
        prod = _dot_hi(cmat * _sigmoid(cmat), w_ref[...])
        for b in range(8):
            p3[b] = prod[b:b + 1, :]
        mod_ref[pl.ds(my_s, 1)] = p3[pl.ds(me, 1)] + b_ref[pl.ds(my_s, 1)]
        ks = (2, 4, 6)
        copies = []
        for i, k in enumerate(ks):
            pr = _peer(k)
            cp = pltpu.make_async_remote_copy(p3.at[_logical(pr)], mod_ref.at[my_s], sb.at[i], rb.at[i],
                                              device_id=pr, device_id_type=MESH)
            cp.start()
            copies.append(cp)
        for i, k in enumerate(ks):
            pr = _peer(k)
            s_src = 2 * pr[0] + pr[1]
            pltpu.make_async_remote_copy(p3.at[0], mod_ref.at[s_src], sb.at[i], rb.at[i],
                                         device_id=pr, device_id_type=MESH).wait_recv()
            mod_ref[pl.ds(s_src, 1)] = mod_ref[pl.ds(s_src, 1)] + b_ref[pl.ds(s_src, 1)]
        for cp in copies:
            cp.wait_send()

    vm = pl.BlockSpec(memory_space=pltpu.VMEM)
    return pl.pallas_call(
        body, name="mod_exchange", in_specs=[vm, vm, vm], out_specs=[vm, vm, vm],
        out_shape=[jax.ShapeDtypeStruct((8, 1, payload.shape[1]), F32), jax.ShapeDtypeStruct((4, 1, n_sh), F32),
                   jax.ShapeDtypeStruct((8, 128), F32)],
        scratch_shapes=[pltpu.VMEM((8, 1, n_sh), F32), pltpu.SemaphoreType.DMA((7,)), pltpu.SemaphoreType.DMA((7,)),
                        pltpu.SemaphoreType.DMA((3,)), pltpu.SemaphoreType.DMA((3,))],
        compiler_params=pltpu.CompilerParams(vmem_limit_bytes=VMEM_LIMIT),
    )(payload, w_ada_s, b_ada4)


def _chips():
    x, y, _ = _pos()
    out = []
    for k in (1, 2, 3):
        px, py = _flip(x, (k >> 1) & 1), _flip(y, k & 1)
        out.append((px, py, 2 * px + py))
    return out


def _half_rows(ref, which):
    half = ref.shape[-2] // 2
    return pl.ds(pl.multiple_of(which * half, 8), half)


def _small_reduce(vec):
    n = vec.shape[1]

    def body(v_ref, tot_ref, gat_ref, sa, ra):
        _gather8(v_ref, gat_ref, sa, ra)
        tot = gat_ref[0]
        for i in range(1, 8):
            tot = tot + gat_ref[i]
        tot_ref[...] = tot

    vm = pl.BlockSpec(memory_space=pltpu.VMEM)
    return pl.pallas_call(
        body, name="small_reduce", in_specs=[vm], out_specs=[vm, vm],
        out_shape=[jax.ShapeDtypeStruct((1, n), F32), jax.ShapeDtypeStruct((8, 1, n), F32)],
        scratch_shapes=[pltpu.SemaphoreType.DMA((7,)), pltpu.SemaphoreType.DMA((7,))],
    )(vec)


def _add_half(g, sib, c_arr, rb, name):
    _, R, C = g.shape
    half = R // 2
    nb = half // rb

    def body(c_ref, g_ref, s_ref, o_ref):
        o_ref[...] = (g_ref[...] + s_ref[...]).astype(BF16)

    return pl.pallas_call(
        body, name=name,
        grid_spec=pltpu.PrefetchScalarGridSpec(
            num_scalar_prefetch=1, grid=(4, nb),
            in_specs=[pl.BlockSpec((1, rb, C), lambda s, i, c_ref: (s, c_ref[0] * nb + i, 0)),
                      pl.BlockSpec((1, rb, C), lambda s, i, c_ref: (s, i, 0))],
            out_specs=pl.BlockSpec((1, rb, C), lambda s, i, c_ref: (s, i, 0))),
        out_shape=jax.ShapeDtypeStruct((4, half, C), BF16),
        compiler_params=_cp("parallel", "parallel"),
    )(c_arr, g, sib)


def _sum4(parts, land, s_arr, rb, name):
    _, H, C = land.shape

    def body(s_ref, own_ref, r_ref, o_ref):
        own = own_ref[0].astype(F32)
        tot = jnp.zeros((rb, C), F32)
        for j in range(4):
            tot = tot + jnp.where(s_ref[0] == j, own, r_ref[j].astype(F32))
        o_ref[...] = tot

    return pl.pallas_call(
        body, name=name,
        grid_spec=pltpu.PrefetchScalarGridSpec(
            num_scalar_prefetch=1, grid=(H // rb,),
            in_specs=[pl.BlockSpec((1, rb, C), lambda i, s_ref: (s_ref[0], i, 0)),
                      pl.BlockSpec((4, rb, C), lambda i, s_ref: (0, i, 0))],
            out_specs=pl.BlockSpec((rb, C), lambda i, s_ref: (i, 0))),
        out_shape=jax.ShapeDtypeStruct((H, C), F32), compiler_params=_cp("parallel"),
    )(s_arr, parts, land)


HBM_SPEC = pl.BlockSpec(memory_space=pltpu.HBM)
SEM_SPEC = pl.BlockSpec(memory_space=pltpu.SEMAPHORE)
EFFECT = pltpu.SideEffectType.DATAFLOW_SIDE_EFFECTING


def _split_start(name, bufs, n_sem, plan, dep):
    nb = len(bufs)

    def body(*refs):
        ins, send, recv, token = refs[:nb], refs[nb + 1], refs[nb + 2], refs[-1]
        for i, (src, dst, dev, _) in enumerate(plan(ins)):
            pltpu.make_async_remote_copy(src, dst, send.at[i], recv.at[i], device_id=dev, device_id_type=MESH).start()
        token[...] = jnp.zeros_like(token)

    outs = pl.pallas_call(
        body, name=name,
        out_shape=(pltpu.SemaphoreType.DMA((n_sem,)), pltpu.SemaphoreType.DMA((n_sem,)),
                   *[pltpu.HBM(b.shape, b.dtype) for b in bufs], jax.ShapeDtypeStruct((8, 128), F32)),
        in_specs=[HBM_SPEC] * nb + [pl.BlockSpec(memory_space=pl.ANY)],
        out_specs=(SEM_SPEC, SEM_SPEC, *([HBM_SPEC] * nb), pl.BlockSpec(memory_space=pltpu.VMEM)),
        input_output_aliases={i: 2 + i for i in range(nb)},
        compiler_params=pltpu.CompilerParams(has_side_effects=EFFECT),
    )(*[pltpu.with_memory_space_constraint(b, pltpu.HBM) for b in bufs], dep)
    return outs[0], outs[1], list(outs[2:2 + nb]), outs[-1]


def _split_wait(name, send, recv, bufs, after, plan):
    nb = len(bufs)

    def body(*refs):
        ins, send_s, recv_s = refs[:nb], refs[nb], refs[nb + 1]
        for i, (src, dst, dev, mine) in enumerate(plan(ins)):
            pltpu.make_async_remote_copy(src, dst, send_s.at[i], recv_s.at[i], device_id=dev,
                                         device_id_type=MESH).wait_send()
            pltpu.make_async_remote_copy(src, mine, send_s.at[i], recv_s.at[i], device_id=dev,
                                         device_id_type=MESH).wait_recv()

    outs = pl.pallas_call(
        body, name=name, out_shape=[pltpu.HBM(b.shape, b.dtype) for b in bufs],
        in_specs=[HBM_SPEC] * nb + [SEM_SPEC, SEM_SPEC, pl.BlockSpec(memory_space=pl.ANY)],
        out_specs=[HBM_SPEC] * nb, input_output_aliases={i: i for i in range(nb)},
        compiler_params=pltpu.CompilerParams(has_side_effects=EFFECT),
    )(*bufs, send, recv, after)
    return list(outs)


def _copies_now(name, bufs, n_sem, plan):
    nb = len(bufs)

    def body(*refs):
        ins, token, send, recv = refs[:nb], refs[2 * nb], refs[-2], refs[-1]
        token[...] = jnp.zeros_like(token)
        todo = plan(ins)
        for i, (src, dst, dev, _) in enumerate(todo):
            pltpu.make_async_remote_copy(src, dst, send.at[i], recv.at[i], device_id=dev, device_id_type=MESH).start()
        for i, (src, dst, dev, mine) in enumerate(todo):
            pltpu.make_async_remote_copy(src, mine, send.at[i], recv.at[i], device_id=dev, device_id_type=MESH).wait_recv()
        for i, (src, dst, dev, _) in enumerate(todo):
            pltpu.make_async_remote_copy(src, dst, send.at[i], recv.at[i], device_id=dev, device_id_type=MESH).wait_send()

    outs = pl.pallas_call(
        body, name=name,
        out_shape=[pltpu.HBM(b.shape, b.dtype) for b in bufs] + [jax.ShapeDtypeStruct((8, 128), F32)],
        in_specs=[HBM_SPEC] * nb, out_specs=[HBM_SPEC] * nb + [pl.BlockSpec(memory_space=pltpu.VMEM)],
        input_output_aliases={i: i for i in range(nb)},
        scratch_shapes=[pltpu.SemaphoreType.DMA((n_sem,)), pltpu.SemaphoreType.DMA((n_sem,))],
    )(*[pltpu.with_memory_space_constraint(b, pltpu.HBM) for b in bufs])
    return list(outs[:nb]), outs[nb]


def _slot(land, s, rows, cols):
    if cols is None:
        return land.at[s, rows]
    return land.at[rows, pl.ds(pl.multiple_of(s * cols, 128), cols)]


def _plan_gather_ici(cols):
    nw = len(cols)

    def plan(refs):
        x, y, c = _pos()
        my_s = 2 * x + y
        out = []
        for w in range(nw):
            mine = _half_rows(refs[w], c)
            for px, py, ps in _chips():
                out.append((refs[w].at[mine], _slot(refs[nw + w], my_s, mine, cols[w]), (px, py, c),
                            _slot(refs[nw + w], ps, mine, cols[w])))
        return out
    return plan


def _plan_gather_fwd(cols, rows):
    def plan(refs):
        x, y, c = _pos()
        out = []
        for w in range(len(cols)):
            half = rows[w] // 2
            mine = pl.ds(pl.multiple_of(c * half, 8), half)
            other = pl.ds(pl.multiple_of((1 - c) * half, 8), half)
            for px, py, ps in _chips():
                got = _slot(refs[w], ps, mine, cols[w])
                out.append((got, got, (x, y, 1 - c), _slot(refs[w], ps, other, cols[w])))
        return out
    return plan


def _plan_swap(nw):
    def plan(refs):
        x, y, c = _pos()
        return [(refs[w].at[:, _half_rows(refs[w], 1 - c)], refs[nw + w], (x, y, 1 - c), refs[nw + w])
                for w in range(nw)]
    return plan


def _plan_scatter(nw):
    def plan(refs):
        x, y, c = _pos()
        my_s = 2 * x + y
        out = []
        for w in range(nw):
            for px, py, ps in _chips():
                out.append((refs[w].at[ps], refs[nw + w].at[my_s], (px, py, c), refs[nw + w].at[ps]))
        return out
    return plan


def _plan_join(nw):
    def plan(refs):
        x, y, c = _pos()
        out = []
        for w in range(nw):
            land = refs[nw + w]
            out.append((refs[w], land.at[_half_rows(land, c)], (x, y, 1 - c), land.at[_half_rows(land, 1 - c)]))
        return out
    return plan


def _hbm_empty(shape, dtype):
    return pltpu.with_memory_space_constraint(lax.empty(shape, dtype), pltpu.HBM)


def _put_slot(land, own, slot):
    return lax.dynamic_update_slice(land, own[None], (slot,) + (0,) * own.ndim)


def _pad_lanes(a, n):
    return jnp.pad(a, ((0, 0), (0, n - a.shape[1])))


def kernel(x, c, positions, w_ada, b_ada, norm1_w, w_in, conv_w, conv_b, dt_bias, a_log, d_skip, attn_sinks, ssm_norm_w, w_out, norm2_w, w_gate_up, w_down, final_norm_w, loss_target, m_w_ada, m_b_ada, m_norm1_w, m_w_in, m_conv_w, m_conv_b, m_dt_bias, m_a_log, m_d_skip, m_attn_sinks, m_ssm_norm_w, m_w_out, m_norm2_w, m_w_gate_up, m_w_down, m_final_norm_w, v_w_ada, v_b_ada, v_norm1_w, v_w_in, v_conv_w, v_conv_b, v_dt_bias, v_a_log, v_d_skip, v_attn_sinks, v_ssm_norm_w, v_w_out, v_norm2_w, v_w_gate_up, v_w_down, v_final_norm_w):
    T = x.shape[1]
    tm = min(256, T)
    xi, yi, ci = lax.axis_index("x"), lax.axis_index("y"), lax.axis_index("c")
    my_s = 2 * xi + yi
    xs = x[0]
    tgt = loss_target[0]

    payload = jnp.concatenate([c, conv_w[0].reshape(1, CONVK * 256)], axis=1)
    gat, mod4, tok = _mod_exchange(payload, w_ada[0], b_ada.reshape(4, 1, 1536))
    mod6 = mod4.reshape(6, D)
    c_all = gat[:, 0, 0:D]
    cw_dev = gat[:, 0, D:].reshape(4, 2, CONVK, 256)[:, 0]
    conv_full = cw_dev.transpose(1, 0, 2).reshape(CONVK, CONVC)

    w_in_b = w_in[0].astype(BF16)
    s_i, r_i, bufs, tok = _split_start("wgather_in_ici_start", [w_in_b, _hbm_empty((4,) + w_in_b.shape, BF16)], 3,
                                       _plan_gather_ici([None]), tok)
    inv_freq = (10000.0 ** (-jnp.arange(32, dtype=F32) / 32))
    cos, sin_s = _rope_tables(positions, inv_freq.reshape(32, 1), min(512, T), tok)
    bufs = _split_wait("wgather_in_ici_wait", s_i, r_i, bufs, cos, _plan_gather_ici([None]))
    bufs, tok = _copies_now("wgather_in_fwd", bufs[1:], 3, _plan_gather_fwd([None], [D]))
    g_in = _put_slot(bufs[0], w_in_b, my_s)
    w_pad = jnp.concatenate([g_in[0], g_in[1], g_in[2], g_in[3], jnp.zeros((D, IN_PAD - IN_PROJ), BF16)], axis=1)

    late = [w_out[0].astype(BF16), w_gate_up[0].astype(BF16), w_down[0].astype(BF16)]
    lands = [_hbm_empty((4, D // 4, D), BF16), _hbm_empty((D, 2 * DFF), BF16), _hbm_empty((4, DFF // 4, D), BF16)]
    cols3, rows3 = [None, GU_SH, None], [D // 4, D, DFF // 4]
    s_a, r_a, bufs, tok = _split_start("wgather_ici_start", late + lands, 9, _plan_gather_ici(cols3), tok)

    qkv, z, xbc, dtr, h1b = _in_proj_fwd(xs, cos, sin_s, mod6, norm1_w, w_pad, min(512, T), tok)
    sinks = attn_sinks
    attn, lse = _attn_fwd(qkv, sinks)
    bufs = _split_wait("wgather_ici_wait", s_a, r_a, bufs, attn, _plan_gather_ici(cols3))
    s_b, r_b, lands, tok = _split_start("wgather_fwd_start", bufs[3:], 9, _plan_gather_fwd(cols3, rows3), attn)
    dtb = _pad_lanes(dt_bias, 128)
    alog = _pad_lanes(a_log, 128)
    dskx = jnp.repeat(d_skip, HD, axis=1)
    mats = _ssd_mats()
    ynorm, ypre, states, conv_pre = _ssd_fwd(xbc, z, dtr, conv_full, conv_b, dtb, alog, dskx, ssm_norm_w, mats, tok)
    lands = _split_wait("wgather_fwd_wait", s_b, r_b, lands, ynorm, _plan_gather_fwd(cols3, rows3))
    w_out_f = _put_slot(lands[0], late[0], my_s).reshape(D, D)
    w_dn_f = _put_slot(lands[2], late[2], my_s).reshape(DFF, D)
    s_arr = my_s.reshape(1).astype(jnp.int32)

    fw2 = final_norm_w.reshape(1, D)
    sq, dmix, dx1, h2b, act, dfb, dgu, dob, sm_ffn = _mix_ffn(
        xs, attn, ynorm, tgt, mod6, norm2_w, fw2, w_out_f, lands[1], late[1], s_arr, w_dn_f, tm)

    tt = min(2048, T)
    c_arr = ci.reshape(1).astype(jnp.int32)
    tok0 = jnp.zeros((8, 128), F32)
    gw_dn4 = _tn_matmul(act, dfb, GU_SH, D, tt, "dw_down", tok0).reshape(4, DFF // 4, D)
    gw_gu4 = _tn_matmul(h2b, dgu, D, GU_SH, tt, "dw_gate_up", tok0)
    gw_out4 = jnp.concatenate(
        [_tn_matmul(attn, dob, AW, D, tt, "dw_out_a", tok0)[0],
         _tn_matmul(ynorm, dob, SW, D, tt, "dw_out_y", tok0)[0]], axis=0).reshape(4, D // 4, D)
    big1 = [gw_out4, gw_gu4, gw_dn4]
    rbs1 = [128, 512, 352]
    sib1 = [_hbm_empty((4, g.shape[1] // 2, g.shape[2]), F32) for g in big1]
    s_c, r_c, bufs, tok = _split_start("gswap_start", big1 + sib1, 3, _plan_swap(3), tok0)

    dzxd, d_cw, d_cb, d_sw, d_sk, d_dtb, d_av = _ssd_bwd(
        xbc, conv_pre, z, dtr, ypre, states, dmix, conv_full, dtb, alog, dskx, ssm_norm_w, mats, tok)
    bufs = _split_wait("gswap_wait", s_c, r_c, bufs, dzxd, _plan_swap(3))
    sums1 = [_add_half(g, s, c_arr, rb, "grad_add_%d" % i)
             for i, (g, s, rb) in enumerate(zip(bufs[:3], bufs[3:], rbs1))]
    land1 = [_hbm_empty(p.shape, BF16) for p in sums1]
    s_d, r_d, bufs, tok = _split_start("gscatter_start", sums1 + land1, 9, _plan_scatter(3), tok0)
    dqkv, d_sinks = _attn_bwd(qkv, sinks, lse, dmix, cos, sin_s, tok)
    bufs = _split_wait("gscatter_wait", s_d, r_d, bufs, dqkv, _plan_scatter(3))
    halves1 = [_sum4(p, l, s_arr, rb, "grad_sum_%d" % i)
               for i, (p, l, rb) in enumerate(zip(bufs[:3], bufs[3:], rbs1))]
    full1 = [_hbm_empty((2 * h.shape[0], h.shape[1]), F32) for h in halves1]
    s_e, r_e, bufs, tok = _split_start("gjoin_start", halves1 + full1, 3, _plan_join(3), tok0)
    gq = _tn_matmul(h1b, dqkv, D, 768, tt, "dw_in_qkv", tok)[0]
    gz = _tn_matmul(h1b, dzxd, D, 1664, tt, "dw_in_zxd", tok)[0]
    gw_in4 = jnp.stack([gq[:, :IN_SH], jnp.concatenate([gq[:, IN_SH:], gz[:, :2 * IN_SH - 768]], axis=1),
                        gz[:, 2 * IN_SH - 768:3 * IN_SH - 768], gz[:, 3 * IN_SH - 768:4 * IN_SH - 768]])
    joined1 = _split_wait("gjoin_wait", s_e, r_e, bufs, gw_in4, _plan_join(3))

    sib0 = _hbm_empty((4, D // 2, IN_SH), F32)
    bufs, _ = _copies_now("gswap_in", [gw_in4, sib0], 1, _plan_swap(1))
    sum0 = _add_half(bufs[0], bufs[1], c_arr, 512, "grad_add_in")
    s_g, r_g, bufs, tok = _split_start("gscatter_in_start", [sum0, _hbm_empty(sum0.shape, BF16)], 3, _plan_scatter(1),
                                       tok0)
    grad_x, sm_in = _in_proj_bwd(xs, dx1, dqkv, dzxd, mod6, norm1_w, w_pad, min(512, T), tok)
    bufs = _split_wait("gscatter_in_wait", s_g, r_g, bufs, grad_x, _plan_scatter(1))
    half0 = _sum4(bufs[0], bufs[1], s_arr, 512, "grad_sum_in")
    joined0, _ = _copies_now("gjoin_in", [half0, _hbm_empty((D, IN_SH), F32)], 1, _plan_join(1))

    a_neg = -jnp.exp(alog)
    pieces = [sm_in[1:2], sm_in[2:3], sm_ffn[5:6], sm_ffn[2:3], sm_ffn[3:4], sm_ffn[4:5],
              sm_in[0:1], sm_ffn[1:2], sm_ffn[0:1], d_cb, d_cw.reshape(1, CONVK * CONVC),
              _pad_lanes(d_sw, SW), d_dtb, d_av * a_neg, d_sk, d_sinks,
              _pad_lanes((0.5 / D * jnp.sum(sq)).reshape(1, 1), 128)]
    vec = jnp.concatenate(pieces, axis=1)
    tot, allv = _small_reduce(vec)
    o = 0
    offs = []
    for p in pieces:
        offs.append(o)
        o += p.shape[1]
    seg = lambda i, n: tot[:, offs[i]:offs[i] + n]
    g_b_ada = tot[:, 0:6 * D]
    g_norm1, g_norm2, g_final, g_conv_b = seg(6, D), seg(7, D), seg(8, D), seg(9, D)
    g_conv_w = lax.dynamic_slice_in_dim(seg(10, CONVK * CONVC).reshape(CONVK, CONVC), my_s * 256, 256, axis=1)
    g_ssm_w, g_dtb, g_alog, g_dsk, g_sink = seg(11, SW), seg(12, 8), seg(13, 8), seg(14, 8), seg(15, 8)
    loss = tot[0, offs[16]]

    small_names = ["b_ada", "norm1_w", "conv_w", "conv_b", "dt_bias", "a_log", "d_skip", "attn_sinks", "ssm_norm_w",
                   "norm2_w", "final_norm_w"]
    small_g = [g_b_ada, g_norm1, g_conv_w, g_conv_b, g_dtb, g_alog, g_dsk, g_sink, g_ssm_w, g_norm2, g_final]
    as2d = lambda a: a.reshape(-1, a.shape[-1])
    small_w = [as2d(a) for a in (b_ada, norm1_w, conv_w, conv_b, dt_bias, a_log, d_skip, attn_sinks, ssm_norm_w,
                                 norm2_w, final_norm_w)]
    small_m = [as2d(a) for a in (m_b_ada, m_norm1_w, m_conv_w, m_conv_b, m_dt_bias, m_a_log, m_d_skip, m_attn_sinks,
                                 m_ssm_norm_w, m_norm2_w, m_final_norm_w)]
    small_v = [as2d(a) for a in (v_b_ada, v_norm1_w, v_conv_w, v_conv_b, v_dt_bias, v_a_log, v_d_skip, v_attn_sinks,
                                 v_ssm_norm_w, v_norm2_w, v_final_norm_w)]
    small_g, sd, smn, svn = _adam_small(small_g, small_w, small_m, small_v)

    sc_all = c_all * jax.nn.sigmoid(c_all)
    dmod_all = allv[:, 0, 0:6 * D]
    dmod_s = lax.dynamic_slice_in_dim(dmod_all, my_s * 1536, 1536, axis=1)
    g_ada, d_ada, m_ada, v_ada = _adam_w_ada(sc_all, dmod_s, w_ada[0], m_w_ada[0], v_w_ada[0], 256)
    native = lambda a: a.transpose(2, 0, 1)
    g_in_s, d_in, m_in, v_in = [a.transpose(1, 2, 0) for a in _adam_w_in(
        native(w_in), joined0[0], joined0[1], native(m_w_in), native(v_w_in), c_arr)]
    g_out_s, d_out, m_out, v_out = _adam_2d(w_out[0], joined1[0], joined1[3], m_w_out[0], v_w_out[0], c_arr, 128,
                                            "adam_w_out")
    g_gu_s, d_gu, m_gu, v_gu = _adam_2d(w_gate_up[0], joined1[1], joined1[4], m_w_gate_up[0], v_w_gate_up[0], c_arr,
                                        256, "adam_w_gate_up")
    g_dn_s, d_dn, m_dn, v_dn = _adam_2d(w_down[0], joined1[2], joined1[5], m_w_down[0], v_w_down[0], c_arr, 352,
                                        "adam_w_down")

    order = ["w_ada", "b_ada", "norm1_w", "w_in", "conv_w", "conv_b", "dt_bias", "a_log", "d_skip", "attn_sinks",
             "ssm_norm_w", "w_out", "norm2_w", "w_gate_up", "w_down", "final_norm_w"]
    shapes = dict(w_ada=w_ada.shape, b_ada=b_ada.shape, norm1_w=norm1_w.shape, w_in=w_in.shape, conv_w=conv_w.shape,
                  conv_b=conv_b.shape, dt_bias=dt_bias.shape, a_log=a_log.shape, d_skip=d_skip.shape,
                  attn_sinks=attn_sinks.shape, ssm_norm_w=ssm_norm_w.shape, w_out=w_out.shape, norm2_w=norm2_w.shape,
                  w_gate_up=w_gate_up.shape, w_down=w_down.shape, final_norm_w=final_norm_w.shape)
    grads = dict(w_ada=g_ada, w_in=g_in_s, w_out=g_out_s, w_gate_up=g_gu_s, w_down=g_dn_s)
    deltas = dict(w_ada=d_ada, w_in=d_in, w_out=d_out, w_gate_up=d_gu, w_down=d_dn)
    new_m = dict(w_ada=m_ada, w_in=m_in, w_out=m_out, w_gate_up=m_gu, w_down=m_dn)
    new_v = dict(w_ada=v_ada, w_in=v_in, w_out=v_out, w_gate_up=v_gu, w_down=v_dn)
    for i, nme in enumerate(small_names):
        grads[nme], deltas[nme], new_m[nme], new_v[nme] = small_g[i], sd[i], smn[i], svn[i]
    outs = [loss, grad_x[None]]
    for table in (grads, deltas, new_m, new_v):
        outs += [table[nme].reshape(shapes[nme]) for nme in order]
    return tuple(outs)
```

```python
import functools
import math

import jax
import jax.numpy as jnp
from jax import lax
from jax.experimental import pallas as pl
from jax.experimental.pallas import tpu as pltpu

F32 = jnp.float32
BF16 = jnp.bfloat16
HI = lax.Precision.HIGHEST
MESH = pl.DeviceIdType.MESH

D = 1024
HD = 64
AW = 512
SW = 512
NST = 128
CONVK = 4
CONVC = 1024
BLK = 128
CPS = 4
IN_PROJ = 2312
IN_PAD = 2432
IN_SH = IN_PROJ // 4
DFF = 2816
GU_SH = 1408
FF_SPLITS = ((0, 1536), (1536, 2816))
EPS = 1e-6
NEG = -1e30
LR, B1, B2, AEPS, WD, STEP = 0.001, 0.9, 0.999, 1e-08, 0.01, 10
VMEM_LIMIT = 58 * 1024 * 1024


def _cp(*sem):
    return pltpu.CompilerParams(dimension_semantics=sem or None, vmem_limit_bytes=VMEM_LIMIT)


def _dot(a, b):
    return jnp.dot(a, b, preferred_element_type=F32)


def _dot_nt(a, b):
    return lax.dot_general(a, b, (((1,), (1,)), ((), ())), preferred_element_type=F32)


def _dot_tn(a, b):
    return lax.dot_general(a, b, (((0,), (0,)), ((), ())), preferred_element_type=F32)


def _dot_hi(a, b):
    return jnp.dot(a, b, precision=HI, preferred_element_type=F32)


def _sigmoid(x):
    return 1.0 / (1.0 + jnp.exp(-x))


def _iota(shape, dim):
    return lax.broadcasted_iota(jnp.int32, shape, dim)


def _load_resident(hbm_ref, vmem_ref, sem):
    @pl.when(pl.program_id(0) == 0)
    def _():
        cp = pltpu.make_async_copy(hbm_ref, vmem_ref, sem)
        cp.start()
        cp.wait()


def _swap32(t):
    lane = _iota(t.shape, 1)
    return jnp.where((lane & 63) < 32, pltpu.roll(t, 96, 1), pltpu.roll(t, 32, 1))


def _rope_fwd(t, cos, sin_s):
    return t * cos + _swap32(t) * sin_s


def _rope_bwd(t, cos, sin_s):
    return t * cos - _swap32(t) * sin_s


DEP_SPEC = pl.BlockSpec((8, 128), lambda *_: (0, 0))


def _rope_tables(pos_row, inv_freq_col, tm, dep):
    T = pos_row.shape[1]
    lane, row = jnp.arange(128)[None, :], jnp.arange(96)[:, None]
    pick = (lane % 32) == (row % 32)
    sel_cos = pick.astype(BF16)
    sel_sin = jnp.where(pick, jnp.where(lane % 64 < 32, -1.0, 1.0), 0.0).astype(BF16)

    def body(p_ref, f_ref, sc_ref, ss_ref, dep_ref, cos_ref, sin_ref):
        ang = f_ref[...] * p_ref[...].astype(F32)
        cos_ref[...] = _dot_tn(_pieces(jnp.cos(ang), 3, 0), sc_ref[...])
        sin_ref[...] = _dot_tn(_pieces(jnp.sin(ang), 3, 0), ss_ref[...])

    full = lambda a: pl.BlockSpec(a.shape, lambda i: (0,) * a.ndim)
    return pl.pallas_call(
        body, name="rope_tables", grid=(T // tm,),
        in_specs=[pl.BlockSpec((1, tm), lambda i: (0, i)), full(inv_freq_col), full(sel_cos), full(sel_sin), DEP_SPEC],
        out_specs=[pl.BlockSpec((tm, 128), lambda i: (i, 0))] * 2,
        out_shape=[jax.ShapeDtypeStruct((T, 128), F32)] * 2,
        compiler_params=_cp("parallel"),
    )(pos_row, inv_freq_col, sel_cos, sel_sin, dep)


def _in_proj_fwd(x, cos, sin_s, mod6, norm1_w, w_pad, tm, dep):
    T = x.shape[0]

    def body(x_ref, cos_ref, sin_ref, mod_ref, nw_ref, w_hbm, dep_ref, qkv_ref, z_ref, xbc_ref, dt_ref, h_ref, w_vmem,
             sem):
        _load_resident(w_hbm, w_vmem, sem)
        xv = x_ref[...]
        r = lax.rsqrt(jnp.mean(xv * xv, axis=-1, keepdims=True) + EPS)
        h = (xv * r * nw_ref[...]) * (1.0 + mod_ref[1:2, :]) + mod_ref[0:1, :]
        hb = h.astype(BF16)
        h_ref[...] = hb
        proj = _dot(hb, w_vmem[...])
        cs, sn = cos_ref[...], sin_ref[...]
        for j in range(5):
            qkv_ref[:, 128 * j:128 * (j + 1)] = _rope_fwd(proj[:, 128 * j:128 * (j + 1)], cs, sn).astype(BF16)
        qkv_ref[:, 640:768] = proj[:, 640:768].astype(BF16)
        z_ref[...] = proj[:, 768:1280]
        xbc_ref[...] = proj[:, 1280:2304]
        dt_ref[...] = proj[:, 2304:2432]

    row = lambda w: pl.BlockSpec((tm, w), lambda i: (i, 0))
    full = lambda a: pl.BlockSpec(a.shape, lambda i: (0,) * a.ndim)
    return pl.pallas_call(
        body, name="in_proj_fwd", grid=(T // tm,),
        in_specs=[row(D), row(128), row(128), full(mod6), full(norm1_w), pl.BlockSpec(memory_space=pl.ANY), DEP_SPEC],
        out_specs=[row(768), row(512), row(1024), row(128), row(D)],
        out_shape=[jax.ShapeDtypeStruct((T, 768), BF16), jax.ShapeDtypeStruct((T, 512), F32),
                   jax.ShapeDtypeStruct((T, 1024), F32), jax.ShapeDtypeStruct((T, 128), F32),
                   jax.ShapeDtypeStruct((T, D), BF16)],
        scratch_shapes=[pltpu.VMEM((D, IN_PAD), BF16), pltpu.SemaphoreType.DMA],
        compiler_params=_cp("arbitrary"),
    )(x, cos, sin_s, mod6, norm1_w, w_pad, dep)


def _head_variants(pair, j):
    lane = _iota(pair.shape, 1)
    lo = lane < 64
    kv = j // 2
    ev = jnp.where(lo, pair, 0.0)
    od = jnp.where(lo, 0.0, pair)
    if kv == 0:
        od = pltpu.roll(od, 64, 1)
    else:
        ev = pltpu.roll(ev, 64, 1)
    return ev.astype(BF16), od.astype(BF16)


def _kv_variants(vcat):
    lane = _iota(vcat.shape, 1)
    lo = lane < 64
    v0 = jnp.where(lo, vcat, 0.0)
    v1 = jnp.where(lo, 0.0, vcat)
    out = {
        (0, 0): v0, (0, 1): pltpu.roll(v0, 64, 1),
        (1, 0): pltpu.roll(v1, 64, 1), (1, 1): v1,
    }
    return {k: v.astype(BF16) for k, v in out.items()}


def _fold_masks(n):
    upper = _iota((BLK, BLK), 1) > _iota((BLK, BLK), 0)
    return upper, upper & (n == 0)


def _attn_fwd(qkv, sinks):
    T = qkv.shape[0]
    nsteps = T // (CPS * BLK)

    def body(sink_ref, q_ref, kc_ref, kp_ref, vc_ref, vp_ref, o_ref, lse_ref):
        for sub in range(CPS):
            rows, before = slice(BLK * sub, BLK * (sub + 1)), slice(BLK * (sub - 1), BLK * sub)
            block(pl.program_id(0) * CPS + sub, sink_ref, q_ref.at[rows, :], kc_ref.at[rows, :],
                  kp_ref if sub == 0 else kc_ref.at[before, :], vc_ref.at[rows, :],
                  vp_ref if sub == 0 else vc_ref.at[before, :], o_ref.at[rows, :], lse_ref.at[rows, :])

    def block(n, sink_ref, q_ref, kc_ref, kp_ref, vc_ref, vp_ref, o_ref, lse_ref):
        vpv = _kv_variants(vp_ref[...].astype(F32))
        vcv = _kv_variants(vc_ref[...].astype(F32))
        q_all = jnp.concatenate(
            [v for j in range(4) for v in _head_variants(q_ref[:, 128 * j:128 * (j + 1)].astype(F32), j)], axis=0)
        s_prev = _dot_nt(q_all, kp_ref[...])
        s_cur = _dot_nt(q_all, kc_ref[...])
        upper, dead = _fold_masks(n)
        lane = _iota((BLK, 128), 1)
        lse_acc = jnp.zeros((BLK, 128), F32)
        for jj in range(4):
            acc = jnp.zeros((BLK, 128), F32)
            for par in range(2):
                h = 2 * jj + par
                rows = slice(h * BLK, (h + 1) * BLK)
                sink = sink_ref[0, h]
                s = jnp.where(dead, NEG, jnp.where(upper, s_prev[rows], s_cur[rows]) * 0.125)
                m = jnp.maximum(jnp.max(s, axis=1, keepdims=True), sink)
                p = jnp.exp(s - m)
                den = jnp.sum(p, axis=1, keepdims=True) + jnp.exp(sink - m)
                pn = p * (1.0 / den)
                acc = (acc + _dot(jnp.where(upper, pn, 0.0).astype(BF16), vpv[(jj // 2, par)])
                       + _dot(jnp.where(upper, 0.0, pn).astype(BF16), vcv[(jj // 2, par)]))
                lse_acc = jnp.where(lane == h, m + jnp.log(den), lse_acc)
            o_ref[:, 128 * jj:128 * (jj + 1)] = acc.astype(BF16)
        lse_ref[...] = lse_acc

    RB = CPS * BLK
    prev = lambda n: jnp.maximum(n * CPS - 1, 0)
    return pl.pallas_call(
        body, name="attn_fwd", grid=(nsteps,),
        in_specs=[pl.BlockSpec(memory_space=pltpu.SMEM),
                  pl.BlockSpec((RB, 512), lambda n: (n, 0)),
                  pl.BlockSpec((RB, 128), lambda n: (n, 4)),
                  pl.BlockSpec((BLK, 128), lambda n: (prev(n), 4)),
                  pl.BlockSpec((RB, 128), lambda n: (n, 5)),
                  pl.BlockSpec((BLK, 128), lambda n: (prev(n), 5))],
        out_specs=[pl.BlockSpec((RB, 512), lambda n: (n, 0)), pl.BlockSpec((RB, 128), lambda n: (n, 0))],
        out_shape=[jax.ShapeDtypeStruct((T, 512), BF16), jax.ShapeDtypeStruct((T, 128), F32)],
        compiler_params=_cp("parallel"),
    )(sinks, qkv, qkv, qkv, qkv, qkv)


def _attn_bwd(qkv, sinks, lse, dmix, cos, sin_s, dep):
    T = qkv.shape[0]
    nb = T // BLK

    def body(sink_ref, q_ref, kc_ref, kp_ref, vc_ref, vp_ref, lse_ref, do_ref, cq_ref, sq_ref, ck_ref, sk_ref,
             dep_ref, out_ref, ds_ref, dq_car, dk_car, dv_car):
        n = pl.program_id(0)
        lane = _iota((BLK, 128), 1)

        @pl.when(n == 0)
        def _():
            ds_ref[...] = jnp.zeros_like(ds_ref)
            dq_car[...] = jnp.zeros_like(dq_car)
            dk_car[...] = jnp.zeros_like(dk_car)
            dv_car[...] = jnp.zeros_like(dv_car)

        @pl.when(n < nb)
        def _():
            kp, kc, vp, vc = kp_ref[...], kc_ref[...], vp_ref[...], vc_ref[...]
            kpv = _kv_variants(kp.astype(F32))
            kcv = _kv_variants(kc.astype(F32))
            lse_v = lse_ref[...]
            q_all = jnp.concatenate(
                [v for j in range(4) for v in _head_variants(q_ref[:, 128 * j:128 * (j + 1)].astype(F32), j)], axis=0)
            do_all = jnp.concatenate(
                [v for j in range(4) for v in _head_variants(do_ref[:, 128 * j:128 * (j + 1)], j)], axis=0)
            s_prev, s_cur = _dot_nt(q_all, kp), _dot_nt(q_all, kc)
            dp_prev, dp_cur = _dot_nt(do_all, vp), _dot_nt(do_all, vc)
            upper, dead = _fold_masks(n)
            out_ref[:, 0:512] = dq_car[...]
            dsk = jnp.zeros((1, 128), F32)
            ds_u, ds_l, p_u, p_l = [], [], [], []
            for jj in range(4):
                dq_acc = jnp.zeros((BLK, 128), F32)
                for par in range(2):
                    h = 2 * jj + par
                    rows = slice(h * BLK, (h + 1) * BLK)
                    lse_h = jnp.sum(jnp.where(lane == h, lse_v, 0.0), axis=1, keepdims=True)
                    s = jnp.where(dead, NEG, jnp.where(upper, s_prev[rows], s_cur[rows]) * 0.125)
                    p = jnp.exp(s - lse_h)
                    dp = jnp.where(upper, dp_prev[rows], dp_cur[rows])
                    delta = jnp.sum(p * dp, axis=1, keepdims=True)
                    ds = p * (dp - delta) * 0.125
                    dsu, dsl = jnp.where(upper, ds, 0.0).astype(BF16), jnp.where(upper, 0.0, ds).astype(BF16)
                    dq_acc = dq_acc + _dot(dsu, kpv[(jj // 2, par)]) + _dot(dsl, kcv[(jj // 2, par)])
                    ds_u.append(dsu)
                    ds_l.append(dsl)
                    p_u.append(jnp.where(upper, p, 0.0).astype(BF16))
                    p_l.append(jnp.where(upper, 0.0, p).astype(BF16))
                    dsk = dsk + jnp.where(lane[0:1] == h, -jnp.sum(jnp.exp(sink_ref[0, h] - lse_h) * delta), 0.0)
                dq_car[:, 128 * jj:128 * (jj + 1)] = _rope_bwd(dq_acc, cq_ref[...], sq_ref[...]).astype(BF16)
            stack = lambda parts: jnp.concatenate(parts, axis=0)
            dk_prev, dk_cur = _dot_tn(stack(ds_u), q_all), _dot_tn(stack(ds_l), q_all)
            dv_prev, dv_cur = _dot_tn(stack(p_u), do_all), _dot_tn(stack(p_l), do_all)
            ds_ref[...] += dsk
            out_ref[:, 512:640] = _rope_bwd(dk_car[...] + dk_prev, ck_ref[...], sk_ref[...]).astype(BF16)
            out_ref[:, 640:768] = (dv_car[...] + dv_prev).astype(BF16)
            dk_car[...] = dk_cur
            dv_car[...] = dv_cur

        @pl.when(n == nb)
        def _():
            out_ref[:, 0:512] = dq_car[...]
            out_ref[:, 512:640] = _rope_bwd(dk_car[...], ck_ref[...], sk_ref[...]).astype(BF16)
            out_ref[:, 640:768] = dv_car[...].astype(BF16)

    cur = lambda n: jnp.minimum(n, nb - 1)
    prev = lambda n: jnp.maximum(cur(n) - 1, 0)
    outb = lambda n: jnp.maximum(n - 1, 0)
    return pl.pallas_call(
        body, name="attn_bwd", grid=(nb + 1,),
        in_specs=[pl.BlockSpec(memory_space=pltpu.SMEM),
                  pl.BlockSpec((BLK, 512), lambda n: (cur(n), 0)),
                  pl.BlockSpec((BLK, 128), lambda n: (cur(n), 4)),
                  pl.BlockSpec((BLK, 128), lambda n: (prev(n), 4)),
                  pl.BlockSpec((BLK, 128), lambda n: (cur(n), 5)),
                  pl.BlockSpec((BLK, 128), lambda n: (prev(n), 5)),
                  pl.BlockSpec((BLK, 128), lambda n: (cur(n), 0)),
                  pl.BlockSpec((BLK, 512), lambda n: (cur(n), 0)),
                  pl.BlockSpec((BLK, 128), lambda n: (cur(n), 0)),
                  pl.BlockSpec((BLK, 128), lambda n: (cur(n), 0)),
                  pl.BlockSpec((BLK, 128), lambda n: (outb(n), 0)),
                  pl.BlockSpec((BLK, 128), lambda n: (outb(n), 0)), DEP_SPEC],
        out_specs=[pl.BlockSpec((BLK, 768), lambda n: (outb(n), 0)), pl.BlockSpec((1, 128), lambda n: (0, 0))],
        out_shape=[jax.ShapeDtypeStruct((T, 768), BF16), jax.ShapeDtypeStruct((1, 128), F32)],
        scratch_shapes=[pltpu.VMEM((BLK, 512), BF16), pltpu.VMEM((BLK, 128), F32), pltpu.VMEM((BLK, 128), F32)],
        compiler_params=_cp("arbitrary"),
    )(sinks, qkv, qkv, qkv, qkv, qkv, lse, dmix, cos, sin_s, cos, sin_s, dep)


def _ssd_mats():
    e = jnp.arange(SW)[None, :] // HD == jnp.arange(128)[:, None]
    tri = jnp.arange(BLK)[None, :] <= jnp.arange(BLK)[:, None]
    return (jnp.tile(e, (3, 1)).astype(BF16), jnp.tile(e.T, (2, 1)).astype(BF16),
            jnp.tile(tri, (1, 3)).astype(BF16), jnp.tile(tri.T, (1, 3)).astype(BF16))


def _pieces(x, n, axis):
    out, r = [], x
    for i in range(n):
        p = r.astype(BF16)
        out.append(p)
        if i + 1 < n:
            r = r - p.astype(F32)
    return jnp.concatenate(out, axis=axis)


def _expand(x, e3):
    return _dot(_pieces(x, 3, 1), e3)


def _head_sums(x, et2):
    return _dot(_pieces(x, 2, 1), et2)


def _run_sum(tri3, x):
    return _dot(tri3, _pieces(x, 3, 0))


def _shift_down(u, tail, j):
    rolled = pltpu.roll(u, j, 0)
    first = jnp.where(_iota(tail.shape, 0) < j, pltpu.roll(tail, j, 0), rolled[0:8])
    return jnp.concatenate([first, rolled[8:]], axis=0)


def _shift_up(d, head, j):
    rolled = pltpu.roll(d, BLK - j, 0)
    last = jnp.where(_iota(head.shape, 0) >= 8 - j, pltpu.roll(head, 8 - j, 0), rolled[BLK - 8:])
    return jnp.concatenate([rolled[:BLK - 8], last], axis=0)


def _ssd_parts(dtr, dtb, alog, e3, tril3):
    xx = dtr + dtb
    dt = jnp.maximum(xx, 0.0) + jnp.log(1.0 + jnp.exp(-jnp.abs(xx)))
    a_neg = -jnp.exp(alog)
    tril = _iota((BLK, BLK), 1) <= _iota((BLK, BLK), 0)
    cs = _run_sum(tril3, dt * a_neg)
    csx = _expand(cs, e3)
    last = csx[BLK - 1:BLK, :]
    return dict(xx=xx, dt=dt, a_neg=a_neg, tril=tril, cs=cs, cs_t=cs.T,
                ecsx=jnp.exp(csx), dtex=jnp.exp(last - csx), cdx=jnp.exp(last), dtx=_expand(dt, e3))


def _decay(parts, h):
    seg = parts["cs"][:, h:h + 1] - parts["cs_t"][h:h + 1, :]
    return jnp.exp(jnp.where(parts["tril"], seg, NEG))


def _group_cols(a, g):
    return a[:, 256 * g:256 * (g + 1)]


def _ssd_fwd(xbc, z, dtr, conv_w, conv_b, dtb, alog, dskx, ssm_w, mats, dep):
    T = xbc.shape[0]
    nc = T // BLK

    def body(u_ref, tail_ref, z_ref, dtr_ref, cw_ref, cb_ref, dtb_ref, al_ref, dk_ref, sw_ref, e3_ref, tril3_ref,
             dep_ref, yn_ref, yp_ref, st_ref, co_ref, s_scr):
        n = pl.program_id(0)

        @pl.when(n == 0)
        def _():
            s_scr[...] = jnp.zeros_like(s_scr)

        lane = _iota((BLK, 128), 1)
        lo = lane < 64
        for sub in range(CPS):
            rows = slice(BLK * sub, BLK * (sub + 1))
            u = u_ref[rows, :]
            tail = jnp.where(n > 0, tail_ref[...], 0.0) if sub == 0 else u_ref[BLK * sub - 8:BLK * sub, :]
            co = cb_ref[...] + cw_ref[3:4, :] * u
            for j in range(1, CONVK):
                co = co + cw_ref[3 - j:4 - j, :] * _shift_down(u, tail, j)
            co_ref[rows, :] = co
            xc = co * _sigmoid(co)
            pt = _ssd_parts(dtr_ref[rows, :], dtb_ref[...], al_ref[...], e3_ref[...], tril3_ref[...])
            xs = xc[:, :SW]
            bm = [xc[:, 512:640].astype(BF16), xc[:, 640:768].astype(BF16)]
            cm = [xc[:, 768:896].astype(BF16), xc[:, 896:1024].astype(BF16)]
            s_in = s_scr[...]
            st_ref[sub] = s_in
            xdt = xs * pt["dtx"]
            xde = (xdt * pt["dtex"]).astype(BF16)
            ys, s_new = [], []
            for g in range(2):
                cb = _dot_nt(cm[g], bm[g])
                yoff = _dot(cm[g], _group_cols(s_in, g).astype(BF16))
                s_new.append(_dot_tn(bm[g], _group_cols(xde, g)))
                for jj in range(2):
                    j = 2 * g + jj
                    chunk = xdt[:, 128 * j:128 * (j + 1)]
                    g_ev = (cb * _decay(pt, 2 * j)).astype(BF16)
                    g_od = (cb * _decay(pt, 2 * j + 1)).astype(BF16)
                    yd = (_dot(g_ev, jnp.where(lo, chunk, 0.0).astype(BF16))
                          + _dot(g_od, jnp.where(lo, 0.0, chunk).astype(BF16)))
                    ys.append(yd + yoff[:, 128 * jj:128 * (jj + 1)] * pt["ecsx"][:, 128 * j:128 * (j + 1)])
            y = jnp.concatenate(ys, axis=1) + xs * dk_ref[...]
            s_scr[...] = s_in * pt["cdx"] + jnp.concatenate(s_new, axis=1)
            yp_ref[rows, :] = y
            zv = z_ref[rows, :]
            yz = y * (zv * _sigmoid(zv))
            outs = []
            for g in range(2):
                yg = _group_cols(yz, g)
                outs.append(yg * lax.rsqrt(jnp.mean(yg * yg, axis=-1, keepdims=True) + EPS))
            yn_ref[rows, :] = (jnp.concatenate(outs, axis=1) * sw_ref[...]).astype(BF16)

    e3, _, tril3, _ = mats
    RB = CPS * BLK
    tail8 = lambda n: jnp.maximum(n * (RB // 8) - 1, 0)
    full = lambda a: pl.BlockSpec(a.shape, lambda n: (0,) * a.ndim)
    return pl.pallas_call(
        body, name="ssd_fwd", grid=(nc // CPS,),
        in_specs=[pl.BlockSpec((RB, CONVC), lambda n: (n, 0)), pl.BlockSpec((8, CONVC), lambda n: (tail8(n), 0)),
                  pl.BlockSpec((RB, SW), lambda n: (n, 0)), pl.BlockSpec((RB, 128), lambda n: (n, 0)),
                  full(conv_w), full(conv_b), full(dtb), full(alog), full(dskx), full(ssm_w), full(e3), full(tril3),
                  DEP_SPEC],
        out_specs=[pl.BlockSpec((RB, SW), lambda n: (n, 0)), pl.BlockSpec((RB, SW), lambda n: (n, 0)),
                   pl.BlockSpec((CPS, NST, SW), lambda n: (n, 0, 0)), pl.BlockSpec((RB, CONVC), lambda n: (n, 0))],
        out_shape=[jax.ShapeDtypeStruct((T, SW), BF16), jax.ShapeDtypeStruct((T, SW), F32),
                   jax.ShapeDtypeStruct((nc, NST, SW), F32), jax.ShapeDtypeStruct((T, CONVC), F32)],
        scratch_shapes=[pltpu.VMEM((NST, SW), F32)],
        compiler_params=_cp("arbitrary"),
    )(xbc, xbc, z, dtr, conv_w, conv_b, dtb, alog, dskx, ssm_w, e3, tril3, dep)


def _ssd_bwd(xbc, co_all, z, dtr, ypre, states, dmix, conv_w, dtb, alog, dskx, ssm_w, mats, dep):
    T = xbc.shape[0]
    nsteps = T // (CPS * BLK)

    def body(*refs):
        per_chunk, consts, out_ref, carried = refs[:7], refs[7:16], refs[17], refs[18:]
        i = pl.program_id(0)

        @pl.when(i == 0)
        def _():
            for r in carried:
                r[...] = jnp.zeros_like(r)

        for sub in reversed(range(CPS)):
            rows = slice(BLK * sub, BLK * (sub + 1))
            views = [r.at[sub:sub + 1] if k == 5 else r.at[rows, :] for k, r in enumerate(per_chunk)]
            chunk(*views, *consts, out_ref.at[rows, :], *carried)

        @pl.when(i == nsteps - 1)
        def _():
            dsk_ref, dskx_scr = carried[3], carried[8]
            dsk_ref[...] = _head_sums(jnp.broadcast_to(dskx_scr[...], (8, SW)), consts[6][...])[0:1]

    def chunk(u_ref, co_ref, z_ref, dtr_ref, yp_ref, st_ref, dyn_ref, cw_ref, dtb_ref, al_ref, dk_ref, sw_ref,
              e3_ref, et2_ref, tril3_ref, triu3_ref,
              out_ref, dcw_ref, dcb_ref, dsw_ref, dsk_ref, ddtb_ref, dav_ref, ds_scr, dco_scr, dskx_scr):
        co = co_ref[...]
        sg = _sigmoid(co)
        xc = co * sg
        pt = _ssd_parts(dtr_ref[...], dtb_ref[...], al_ref[...], e3_ref[...], tril3_ref[...])
        dtx, ecsx, dtex, cdx = pt["dtx"], pt["ecsx"], pt["dtex"], pt["cdx"]
        xs = xc[:, :SW]
        bm = [xc[:, 512:640].astype(BF16), xc[:, 640:768].astype(BF16)]
        cm = [xc[:, 768:896].astype(BF16), xc[:, 896:1024].astype(BF16)]
        s_in = st_ref[0]
        ds_out = ds_scr[...]
        e_t = et2_ref[...]

        zv = z_ref[...]
        sz = _sigmoid(zv)
        silu_z = zv * sz
        ypre = yp_ref[...]
        yz = ypre * silu_z
        dyn = dyn_ref[...]
        sw = sw_ref[...]
        dyz, yns = [], []
        for g in range(2):
            yg = _group_cols(yz, g)
            r = lax.rsqrt(jnp.mean(yg * yg, axis=-1, keepdims=True) + EPS)
            yn = yg * r
            dg = _group_cols(dyn, g) * _group_cols(sw, g)
            dyz.append(r * (dg - yn * jnp.mean(dg * yn, axis=-1, keepdims=True)))
            yns.append(yn)
        dyz = jnp.concatenate(dyz, axis=1)
        dsw_ref[...] += jnp.sum(dyn * jnp.concatenate(yns, axis=1), axis=0, keepdims=True)
        dy = dyz * silu_z
        dz = dyz * ypre * (sz * (1.0 + zv * (1.0 - sz)))

        xdt = xs * dtx
        xdt_b = xdt.astype(BF16)
        edy = (ecsx * dy).astype(BF16)
        xde = (xdt * dtex).astype(BF16)
        lane = _iota((BLK, 128), 1)
        lo = lane < 64
        row8 = _iota((8, 128), 0)
        dcs = jnp.zeros((BLK, 128), F32)
        col_rows = jnp.zeros((8, 128), F32)
        dxdt, bds, yoff, dbs, dcs_g, ds_new = [], [], [], [], [], []
        for g in range(2):
            s_g = _group_cols(s_in, g).astype(BF16)
            dso_g = _group_cols(ds_out, g).astype(BF16)
            cb = _dot_nt(cm[g], bm[g])
            bds.append(_dot(bm[g], dso_g))
            yoff.append(_dot(cm[g], s_g))
            dcb_g = jnp.zeros((BLK, BLK), F32)
            for jj in range(2):
                j = 2 * g + jj
                dy_c = dy[:, 128 * j:128 * (j + 1)]
                xdt_c = xdt_b[:, 128 * j:128 * (j + 1)]
                acc = jnp.zeros((BLK, 128), F32)
                for par in range(2):
                    h = 2 * j + par
                    lm = _decay(pt, h)
                    gm = cb * lm
                    dy_m = (jnp.where(lo, dy_c, 0.0) if par == 0 else jnp.where(lo, 0.0, dy_c)).astype(BF16)
                    dg_h = _dot_nt(dy_m, xdt_c)
                    w_h = dg_h * gm
                    dcs = dcs + jnp.where(lane == h, jnp.sum(w_h, axis=1, keepdims=True), 0.0)
                    col_rows = col_rows + jnp.where(row8 == h, jnp.sum(w_h, axis=0, keepdims=True), 0.0)
                    dcb_g = dcb_g + dg_h * lm
                    acc = acc + _dot_tn(gm.astype(BF16), dy_m)
                dxdt.append(acc)
            dcb_b = dcb_g.astype(BF16)
            dcs_g.append(_dot(dcb_b, bm[g]) + _dot_nt(_group_cols(edy, g), s_g))
            dbs.append(_dot_tn(dcb_b, cm[g]) + _dot_nt(_group_cols(xde, g), dso_g))
            ds_new.append(_dot_tn(cm[g], _group_cols(edy, g)))
        bds = jnp.concatenate(bds, axis=1)
        yoff = jnp.concatenate(yoff, axis=1) * ecsx
        dxdt = jnp.concatenate(dxdt, axis=1) + dtex * bds
        ds_scr[...] = cdx * ds_out + jnp.concatenate(ds_new, axis=1)

        t_m = _head_sums(dtex * xdt * bds, e_t)
        colsum_t = jnp.concatenate([col_rows, jnp.zeros((BLK - 8, 128), F32)], axis=0).T
        cd = jnp.exp(pt["cs"][BLK - 1:BLK, :])
        sds = jnp.sum(s_in * ds_out, axis=0, keepdims=True)
        last_row = jnp.sum(t_m, axis=0, keepdims=True) + cd * _head_sums(jnp.broadcast_to(sds, (8, SW)), e_t)[0:1]
        dcs = dcs - colsum_t + _head_sums(dy * yoff, e_t) - t_m
        dcs = dcs + jnp.where(_iota((BLK, 128), 0) == BLK - 1, last_row, 0.0)
        da = _run_sum(triu3_ref[...], dcs)
        dt = pt["dt"]
        ddt = da * pt["a_neg"] + _head_sums(dxdt * xs, e_t)
        dav_ref[...] += jnp.sum(da * dt, axis=0, keepdims=True)
        ddtr = ddt * _sigmoid(pt["xx"])
        ddtb_ref[...] += jnp.sum(ddtr, axis=0, keepdims=True)
        dxs = dxdt * dtx + dy * dk_ref[...]
        dskx_scr[...] += jnp.sum(dy * xs, axis=0, keepdims=True)
        dxc = jnp.concatenate([dxs, dbs[0], dbs[1], dcs_g[0], dcs_g[1]], axis=1)
        dco = dxc * (sg * (1.0 + co * (1.0 - sg)))

        dcb_ref[...] += jnp.sum(dco, axis=0, keepdims=True)
        u = u_ref[...]
        head = dco_scr[...]
        du = jnp.zeros_like(dco)
        for j in range(CONVK):
            up_j = dco if j == 0 else _shift_up(dco, head, j)
            dcw_ref[3 - j:4 - j, :] += jnp.sum(up_j * u, axis=0, keepdims=True)
            du = du + cw_ref[3 - j:4 - j, :] * up_j
        dco_scr[...] = dco[0:8]
        out_ref[:, 0:512] = dz.astype(BF16)
        out_ref[:, 512:1536] = du.astype(BF16)
        out_ref[:, 1536:1664] = ddtr.astype(BF16)

    e3, et2, tril3, triu3 = mats
    RB = CPS * BLK
    rev = lambda i: nsteps - 1 - i
    full = lambda a: pl.BlockSpec(a.shape, lambda i: (0,) * a.ndim)
    acc = lambda r, c: pl.BlockSpec((r, c), lambda i: (0, 0))
    return pl.pallas_call(
        body, name="ssd_bwd", grid=(nsteps,),
        in_specs=[pl.BlockSpec((RB, CONVC), lambda i: (rev(i), 0)), pl.BlockSpec((RB, CONVC), lambda i: (rev(i), 0)),
                  pl.BlockSpec((RB, SW), lambda i: (rev(i), 0)), pl.BlockSpec((RB, 128), lambda i: (rev(i), 0)),
                  pl.BlockSpec((RB, SW), lambda i: (rev(i), 0)), pl.BlockSpec((CPS, NST, SW), lambda i: (rev(i), 0, 0)),
                  pl.BlockSpec((RB, SW), lambda i: (rev(i), 1)),
                  full(conv_w), full(dtb), full(alog), full(dskx), full(ssm_w),
                  full(e3), full(et2), full(tril3), full(triu3), DEP_SPEC],
        out_specs=[pl.BlockSpec((RB, 1664), lambda i: (rev(i), 0)),
                   acc(CONVK, CONVC), acc(1, CONVC), acc(1, SW), acc(1, 128), acc(1, 128), acc(1, 128)],
        out_shape=[jax.ShapeDtypeStruct((T, 1664), BF16),
                   jax.ShapeDtypeStruct((CONVK, CONVC), F32), jax.ShapeDtypeStruct((1, CONVC), F32),
                   jax.ShapeDtypeStruct((1, SW), F32), jax.ShapeDtypeStruct((1, 128), F32),
                   jax.ShapeDtypeStruct((1, 128), F32), jax.ShapeDtypeStruct((1, 128), F32)],
        scratch_shapes=[pltpu.VMEM((NST, SW), F32), pltpu.VMEM((8, CONVC), F32), pltpu.VMEM((1, SW), F32)],
        compiler_params=_cp("arbitrary"),
    )(xbc, co_all, z, dtr, ypre, states, dmix, conv_w, dtb, alog, dskx, ssm_w, e3, et2, tril3, triu3, dep)


def _mix_ffn(x, attn, ynorm, tgt, mod6, norm2_w, final_w, w_out, w_gu, w_gu_own, s_arr, w_dn, tm):
    T = x.shape[0]
    nt = T // tm

    def body(x_ref, a_ref, y_ref, t_ref, mod_ref, n2_ref, fw_ref, wo_hbm, wgu_hbm, own_hbm, s_ref, wdn_hbm,
             sq_ref, dmix_ref, dx1_ref, h2_ref, act_ref, df_ref, dgu_ref, do_ref, sm_ref,
             wo, wgu, wdn, sems):
        i = pl.program_id(0)

        @pl.when(i == 0)
        def _():
            cps = [pltpu.make_async_copy(s, d, sems.at[k]) for k, (s, d) in
                   enumerate(((wo_hbm, wo), (wgu_hbm, wgu), (wdn_hbm, wdn)))]
            for c in cps:
                c.start()
            for c in cps:
                c.wait()
            own = pltpu.make_async_copy(
                own_hbm, wgu.at[:, pl.ds(pl.multiple_of(s_ref[0] * GU_SH, 128), GU_SH)], sems.at[3])
            own.start()
            own.wait()
            sq_ref[...] = jnp.zeros_like(sq_ref)
            sm_ref[...] = jnp.zeros_like(sm_ref)

        gate1, shift2, scale2, gate2 = mod_ref[2:3, :], mod_ref[3:4, :], mod_ref[4:5, :], mod_ref[5:6, :]
        n2w, fw = n2_ref[...], fw_ref[...]
        o = _dot(a_ref[...], wo[0:AW, :]) + _dot(y_ref[...], wo[AW:D, :])
        x1 = x_ref[...] + gate1 * o
        r2 = lax.rsqrt(jnp.mean(x1 * x1, axis=-1, keepdims=True) + EPS)
        xh2 = x1 * r2
        n2 = xh2 * n2w
        h2b = (n2 * (1.0 + scale2) + shift2).astype(BF16)
        h2_ref[...] = h2b
        f = jnp.zeros((tm, D), F32)
        saved = []
        for a, b in FF_SPLITS:
            gp = _dot(h2b, wgu[:, a:b])
            upj = _dot(h2b, wgu[:, DFF + a:DFF + b])
            sg = _sigmoid(gp)
            sl = gp * sg
            actb = (sl * upj).astype(BF16)
            act_ref[:, a:b] = actb
            f = f + _dot(actb, wdn[a:b, :])
            saved.append((gp, upj, sg, sl))
        x2 = x1 + gate2 * f
        r3 = lax.rsqrt(jnp.mean(x2 * x2, axis=-1, keepdims=True) + EPS)
        xh3 = x2 * r3
        err = xh3 * fw - t_ref[...]
        sq_ref[...] += jnp.sum(err * err, axis=0, keepdims=True)
        dy = err * (1.0 / D)
        dfw = jnp.sum(dy * xh3, axis=0, keepdims=True)
        dxh3 = dy * fw
        dx2 = r3 * (dxh3 - xh3 * jnp.mean(dxh3 * xh3, axis=-1, keepdims=True))
        dgate2 = jnp.sum(dx2 * f, axis=0, keepdims=True)
        dfb = (dx2 * gate2).astype(BF16)
        df_ref[...] = dfb
        dh2 = jnp.zeros((tm, D), F32)
        for (a, b), (gp, upj, sg, sl) in zip(FF_SPLITS, saved):
            dact = _dot_nt(dfb, wdn[a:b, :])
            dg = (dact * upj * (sg * (1.0 + gp * (1.0 - sg)))).astype(BF16)
            du = (dact * sl).astype(BF16)
            dgu_ref[:, a:b] = dg
            dgu_ref[:, DFF + a:DFF + b] = du
            dh2 = dh2 + _dot_nt(dg, wgu[:, a:b]) + _dot_nt(du, wgu[:, DFF + a:DFF + b])
        dshift2 = jnp.sum(dh2, axis=0, keepdims=True)
        dscale2 = jnp.sum(dh2 * n2, axis=0, keepdims=True)
        dn2 = dh2 * (1.0 + scale2)
        dn2w = jnp.sum(dn2 * xh2, axis=0, keepdims=True)
        dxh2 = dn2 * n2w
        dx1 = dx2 + r2 * (dxh2 - xh2 * jnp.mean(dxh2 * xh2, axis=-1, keepdims=True))
        dx1_ref[...] = dx1
        dgate1 = jnp.sum(dx1 * o, axis=0, keepdims=True)
        dob = (dx1 * gate1).astype(BF16)
        do_ref[...] = dob
        dmix_ref[...] = _dot_nt(dob, wo[...])
        sm_ref[...] += jnp.concatenate(
            [dfw, dn2w, dshift2, dscale2, dgate2, dgate1, jnp.zeros((2, D), F32)], axis=0)

    row = lambda w: pl.BlockSpec((tm, w), lambda i: (i, 0))
    full = lambda a: pl.BlockSpec(a.shape, lambda i: (0,) * a.ndim)
    anyspec = pl.BlockSpec(memory_space=pl.ANY)
    return pl.pallas_call(
        body, name="mix_ffn", grid=(nt,),
        in_specs=[row(D), row(AW), row(SW), row(D), full(mod6), full(norm2_w), full(final_w), anyspec, anyspec, anyspec,
                  pl.BlockSpec(memory_space=pltpu.SMEM), anyspec],
        out_specs=[pl.BlockSpec((1, D), lambda i: (0, 0)), row(D), row(D), row(D),
                   row(DFF), row(D), row(2 * DFF), row(D), pl.BlockSpec((8, D), lambda i: (0, 0))],
        out_shape=[jax.ShapeDtypeStruct((1, D), F32), jax.ShapeDtypeStruct((T, D), F32), jax.ShapeDtypeStruct((T, D), F32),
                   jax.ShapeDtypeStruct((T, D), BF16), jax.ShapeDtypeStruct((T, DFF), BF16),
                   jax.ShapeDtypeStruct((T, D), BF16), jax.ShapeDtypeStruct((T, 2 * DFF), BF16),
                   jax.ShapeDtypeStruct((T, D), BF16), jax.ShapeDtypeStruct((8, D), F32)],
        scratch_shapes=[pltpu.VMEM((D, D), BF16), pltpu.VMEM((D, 2 * DFF), BF16), pltpu.VMEM((DFF, D), BF16),
                        pltpu.SemaphoreType.DMA((4,))],
        compiler_params=_cp("arbitrary"),
    )(x, attn, ynorm, tgt, mod6, norm2_w, final_w, w_out, w_gu, w_gu_own, s_arr, w_dn)


def _in_proj_bwd(x, dx1, dqkv, dzxd, mod6, norm1_w, w_pad, tm, dep):
    T = x.shape[0]

    def body(x_ref, dx1_ref, dq_ref, dz_ref, mod_ref, nw_ref, w_hbm, dep_ref, gx_ref, sm_ref, w_vmem, sem):
        _load_resident(w_hbm, w_vmem, sem)

        @pl.when(pl.program_id(0) == 0)
        def _():
            sm_ref[...] = jnp.zeros_like(sm_ref)

        nw = nw_ref[...]
        scale1 = mod_ref[1:2, :]
        sums = jnp.zeros((8, D), F32)
        for rows in (slice(0, tm // 2), slice(tm // 2, tm)):
            dh = _dot_nt(dq_ref[rows, :], w_vmem[:, 0:768]) + _dot_nt(dz_ref[rows, :], w_vmem[:, 768:IN_PAD])
            xv = x_ref[rows, :]
            r = lax.rsqrt(jnp.mean(xv * xv, axis=-1, keepdims=True) + EPS)
            xh = xv * r
            n1 = xh * nw
            dshift = jnp.sum(dh, axis=0, keepdims=True)
            dscale = jnp.sum(dh * n1, axis=0, keepdims=True)
            dn = dh * (1.0 + scale1)
            dnw = jnp.sum(dn * xh, axis=0, keepdims=True)
            dxh = dn * nw
            gx_ref[rows, :] = dx1_ref[rows, :] + r * (dxh - xh * jnp.mean(dxh * xh, axis=-1, keepdims=True))
            sums = sums + jnp.concatenate([dnw, dshift, dscale, jnp.zeros((5, D), F32)], axis=0)
        sm_ref[...] += sums

    row = lambda w: pl.BlockSpec((tm, w), lambda i: (i, 0))
    full = lambda a: pl.BlockSpec(a.shape, lambda i: (0,) * a.ndim)
    return pl.pallas_call(
        body, name="in_proj_bwd", grid=(T // tm,),
        in_specs=[row(D), row(D), row(768), row(1664), full(mod6), full(norm1_w), pl.BlockSpec(memory_space=pl.ANY),
                  DEP_SPEC],
        out_specs=[row(D), pl.BlockSpec((8, D), lambda i: (0, 0))],
        out_shape=[jax.ShapeDtypeStruct((T, D), F32), jax.ShapeDtypeStruct((8, D), F32)],
        scratch_shapes=[pltpu.VMEM((D, IN_PAD), BF16), pltpu.SemaphoreType.DMA],
        compiler_params=_cp("arbitrary"),
    )(x, dx1, dqkv, dzxd, mod6, norm1_w, w_pad, dep)


def _tn_matmul(a, b, K, N, tt, name, dep):
    T = a.shape[0]
    ja, jb = a.shape[1] // K, b.shape[1] // N
    J = max(ja, jb)

    def body(a_ref, b_ref, dep_ref, o_ref):
        t = pl.program_id(1)
        prod = _dot_tn(a_ref[...], b_ref[...])

        @pl.when(t == 0)
        def _():
            o_ref[0] = prod

        @pl.when(t > 0)
        def _():
            o_ref[0] += prod

    return pl.pallas_call(
        body, name=name, grid=(J, T // tt),
        in_specs=[pl.BlockSpec((tt, K), lambda j, t: (t, j if ja > 1 else 0)),
                  pl.BlockSpec((tt, N), lambda j, t: (t, j if jb > 1 else 0)),
                  pl.BlockSpec((8, 128), lambda j, t: (0, 0))],
        out_specs=pl.BlockSpec((1, K, N), lambda j, t: (j, 0, 0)),
        out_shape=jax.ShapeDtypeStruct((J, K, N), F32),
        compiler_params=_cp("parallel", "arbitrary"),
    )(a, b, dep)


def _adam_math(w, g, m, v):
    m = B1 * m + (1.0 - B1) * g
    v = B2 * v + (1.0 - B2) * (g * g)
    m_hat = m / (1.0 - B1 ** STEP)
    v_hat = v / (1.0 - B2 ** STEP)
    delta = -LR * (m_hat / (jnp.sqrt(v_hat) + AEPS) + WD * w)
    return delta, m, v


def _adam_2d(w, mine, land, m, v, c_arr, rb, name):
    R, C = w.shape
    nbh = R // 2 // rb

    def body(c_ref, w_ref, mine_ref, land_ref, m_ref, v_ref, go_ref, d_ref, mo_ref, vo_ref):
        g = jnp.where(pl.program_id(0) // nbh == c_ref[0], mine_ref[...], land_ref[...])
        d, mn, vn = _adam_math(w_ref[...], g, m_ref[...], v_ref[...])
        go_ref[...] = g
        d_ref[...] = d
        mo_ref[...] = mn
        vo_ref[...] = vn

    spec = pl.BlockSpec((rb, C), lambda i, c_ref: (i, 0))
    mine_spec = pl.BlockSpec((rb, C), lambda i, c_ref: (jnp.clip(i - c_ref[0] * nbh, 0, nbh - 1), 0))
    return pl.pallas_call(
        body, name=name,
        grid_spec=pltpu.PrefetchScalarGridSpec(
            num_scalar_prefetch=1, grid=(R // rb,), in_specs=[spec, mine_spec, spec, spec, spec], out_specs=[spec] * 4),
        out_shape=[jax.ShapeDtypeStruct((R, C), F32)] * 4, compiler_params=_cp("parallel"),
    )(c_arr, w, mine, land, m, v)


def _adam_w_in(w3, mine, land, m3, v3, c_arr):
    n = w3.shape[0]

    def body(c_ref, w_hbm, mine_ref, land_ref, m_hbm, v_hbm, g_hbm, d_hbm, mo_hbm, vo_hbm, bufs, sems):
        ins = [pltpu.make_async_copy(src.at[:, 0], bufs.at[k], sems.at[k]) for k, src in enumerate((w_hbm, m_hbm, v_hbm))]
        for cp in ins:
            cp.start()
        half = D // 2
        top = jnp.where(c_ref[0] == 0, mine_ref[...], land_ref[0:half, :])
        bot = jnp.where(c_ref[0] == 1, mine_ref[...], land_ref[half:D, :])
        g = jnp.concatenate([top, bot], axis=0)
        eye = (_iota((D, D), 0) == _iota((D, D), 1)).astype(BF16)
        g_t = jnp.zeros((n, D), F32)
        r = g
        for i in range(3):
            p = r.astype(BF16)
            g_t = g_t + _dot_tn(p, eye)
            if i < 2:
                r = r - p.astype(F32)
        for cp in ins:
            cp.wait()
        d, mn, vn = _adam_math(bufs[0], g_t, bufs[1], bufs[2])
        for k, val in enumerate((g_t, d, mn, vn)):
            bufs[3 + k] = val
        outs = [pltpu.make_async_copy(bufs.at[3 + k], dst.at[:, 0], sems.at[3 + k])
                for k, dst in enumerate((g_hbm, d_hbm, mo_hbm, vo_hbm))]
        for cp in outs:
            cp.start()
        for cp in outs:
            cp.wait()

    anyspec = pl.BlockSpec(memory_space=pl.ANY)
    vm = pl.BlockSpec(memory_space=pltpu.VMEM)
    return pl.pallas_call(
        body, name="adam_w_in",
        in_specs=[pl.BlockSpec(memory_space=pltpu.SMEM), anyspec, vm, vm, anyspec, anyspec], out_specs=[anyspec] * 4,
        out_shape=[jax.ShapeDtypeStruct(w3.shape, F32)] * 4,
        scratch_shapes=[pltpu.VMEM((7, n, D), F32), pltpu.SemaphoreType.DMA((7,))],
        compiler_params=pltpu.CompilerParams(vmem_limit_bytes=VMEM_LIMIT),
    )(c_arr, w3, mine, land, m3, v3)


def _adam_w_ada(gat, allv, s_arr, w, m, v, rb):
    R, C = w.shape

    def body(s_ref, c_ref, dm_ref, w_ref, m_ref, v_ref, g_ref, d_ref, mo_ref, vo_ref):
        cm = _rows_select(c_ref, rb)
        g = lax.dot_general(cm * _sigmoid(cm), _rows_select(dm_ref, C), (((0,), (0,)), ((), ())), precision=HI,
                            preferred_element_type=F32)
        d, mn, vn = _adam_math(w_ref[...], g, m_ref[...], v_ref[...])
        g_ref[...] = g
        d_ref[...] = d
        mo_ref[...] = mn
        vo_ref[...] = vn

    spec = pl.BlockSpec((rb, C), lambda i, s_ref: (i, 0))
    return pl.pallas_call(
        body, name="adam_w_ada",
        grid_spec=pltpu.PrefetchScalarGridSpec(
            num_scalar_prefetch=1, grid=(R // rb,),
            in_specs=[pl.BlockSpec((8, 1, rb), lambda i, s_ref: (0, 0, i)),
                      pl.BlockSpec((8, 1, C), lambda i, s_ref: (0, 0, s_ref[0])), spec, spec, spec],
            out_specs=[spec] * 4),
        out_shape=[jax.ShapeDtypeStruct((R, C), F32)] * 4, compiler_params=_cp("parallel"),
    )(s_arr, gat, allv, w, m, v)


def _adam_small(tot, segs, ws, ms, vs):
    k = len(ws)
    extra = [sg for sg in segs if not isinstance(sg, tuple)]
    ne = len(extra)

    def body(*refs):
        tot_ref, g_x = refs[0], list(refs[1:1 + ne])
        w, m, v = [refs[1 + ne + j * k:1 + ne + (j + 1) * k] for j in range(3)]
        g_o, d_o, m_o, v_o = [refs[1 + ne + (3 + j) * k:1 + ne + (4 + j) * k] for j in range(4)]
        for i in range(k):
            gi = tot_ref[:, segs[i][0]:segs[i][0] + segs[i][1]] if isinstance(segs[i], tuple) else g_x.pop(0)[...]
            d, mn, vn = _adam_math(w[i][...], gi, m[i][...], v[i][...])
            g_o[i][...] = gi
            d_o[i][...] = d
            m_o[i][...] = mn
            v_o[i][...] = vn

    shapes = [jax.ShapeDtypeStruct(w.shape, F32) for w in ws]
    vm = pl.BlockSpec(memory_space=pltpu.VMEM)
    outs = pl.pallas_call(
        body, name="adam_small", in_specs=[vm] * (1 + ne + 3 * k), out_specs=[vm] * (4 * k), out_shape=shapes * 4,
    )(tot, *extra, *ws, *ms, *vs)
    return outs[0:k], outs[k:2 * k], outs[2 * k:3 * k], outs[3 * k:4 * k]


def _pos():
    return lax.axis_index("x"), lax.axis_index("y"), lax.axis_index("c")


def _flip(v, bit):
    return 1 - v if bit else v


def _peer(k):
    x, y, c = _pos()
    return (_flip(x, (k >> 2) & 1), _flip(y, (k >> 1) & 1), _flip(c, k & 1))


def _logical(p):
    return 4 * p[0] + 2 * p[1] + p[2]


def _gather8(src_ref, dst_ref, send_sems, recv_sems):
    me = _logical(_pos())
    dst_ref[pl.ds(me, 1)] = src_ref[...][None]
    copies = []
    for k in range(1, 8):
        cp = pltpu.make_async_remote_copy(src_ref, dst_ref.at[me], send_sems.at[k - 1], recv_sems.at[k - 1],
                                          device_id=_peer(k), device_id_type=MESH)
        cp.start()
        copies.append(cp)
    for k in range(1, 8):
        pltpu.make_async_remote_copy(src_ref, dst_ref.at[_logical(_peer(k))], send_sems.at[k - 1], recv_sems.at[k - 1],
                                     device_id=_peer(k), device_id_type=MESH).wait_recv()
    for cp in copies:
        cp.wait_send()


def _rows_select(ref3, width):
    row = _iota((8, width), 0)
    out = jnp.zeros((8, width), F32)
    for i in range(8):
        out = jnp.where(row == i, ref3[i][:, 0:width], out)
    return out


def _mod_exchange(payload, w_ada_s, b_ada4):
    n_sh = w_ada_s.shape[1]

    def body(pay_ref, w_ref, b_ref, gat_ref, mod_ref, token, p3, sa, ra, sb, rb):
        token[...] = jnp.zeros_like(token)
        x, y, c = _pos()
        me = _logical((x, y, c))
        my_s = 2 * x + y
        _gather8(pay_ref, gat_ref, sa, ra)
        cmat = _rows_select(gat_ref, D)
        prod = _dot_hi(cmat * _sigmoid(cmat), w_ref[...])
        for b in range(8):
            p3[b] = prod[b:b + 1, :]
        mod_ref[pl.ds(my_s, 1)] = p3[pl.ds(me, 1)] + b_ref[pl.ds(my_s, 1)]
        ks = (2, 4, 6)
        copies = []
        for i, k in enumerate(ks):
            pr = _peer(k)
            cp = pltpu.make_async_remote_copy(p3.at[_logical(pr)], mod_ref.at[my_s], sb.at[i], rb.at[i],
                                              device_id=pr, device_id_type=MESH)
            cp.start()
            copies.append(cp)
        for i, k in enumerate(ks):
            pr = _peer(k)
            s_src = 2 * pr[0] + pr[1]
            pltpu.make_async_remote_copy(p3.at[0], mod_ref.at[s_src], sb.at[i], rb.at[i],
                                         device_id=pr, device_id_type=MESH).wait_recv()
            mod_ref[pl.ds(s_src, 1)] = mod_ref[pl.ds(s_src, 1)] + b_ref[pl.ds(s_src, 1)]
        for cp in copies:
            cp.wait_send()

    vm = pl.BlockSpec(memory_space=pltpu.VMEM)
    return pl.pallas_call(
        body, name="mod_exchange", in_specs=[vm, vm, vm], out_specs=[vm, vm, vm],
        out_shape=[jax.ShapeDtypeStruct((8, 1, payload.shape[1]), F32), jax.ShapeDtypeStruct((4, 1, n_sh), F32),
                   jax.ShapeDtypeStruct((8, 128), F32)],
        scratch_shapes=[pltpu.VMEM((8, 1, n_sh), F32), pltpu.SemaphoreType.DMA((7,)), pltpu.SemaphoreType.DMA((7,)),
                        pltpu.SemaphoreType.DMA((3,)), pltpu.SemaphoreType.DMA((3,))],
        compiler_params=pltpu.CompilerParams(vmem_limit_bytes=VMEM_LIMIT),
    )(payload, w_ada_s, b_ada4)


def _chips():
    x, y, _ = _pos()
    out = []
    for k in (1, 2, 3):
        px, py = _flip(x, (k >> 1) & 1), _flip(y, k & 1)
        out.append((px, py, 2 * px + py))
    return out


def _half_rows(ref, which):
    half = ref.shape[-2] // 2
    return pl.ds(pl.multiple_of(which * half, 8), half)


def _small_reduce(vec):
    n = vec.shape[1]

    def body(v_ref, tot_ref, gat_ref, sa, ra):
        _gather8(v_ref, gat_ref, sa, ra)
        tot = gat_ref[0]
        for i in range(1, 8):
            tot = tot + gat_ref[i]
        tot_ref[...] = tot

    vm = pl.BlockSpec(memory_space=pltpu.VMEM)
    return pl.pallas_call(
        body, name="small_reduce", in_specs=[vm], out_specs=[vm, vm],
        out_shape=[jax.ShapeDtypeStruct((1, n), F32), jax.ShapeDtypeStruct((8, 1, n), F32)],
        scratch_shapes=[pltpu.SemaphoreType.DMA((7,)), pltpu.SemaphoreType.DMA((7,))],
    )(vec)


def _add_half(g, sib, c_arr, rb, name):
    _, R, C = g.shape
    half = R // 2
    nb = half // rb

    def body(c_ref, g_ref, s_ref, o_ref):
        o_ref[...] = (g_ref[...] + s_ref[...]).astype(BF16)

    return pl.pallas_call(
        body, name=name,
        grid_spec=pltpu.PrefetchScalarGridSpec(
            num_scalar_prefetch=1, grid=(4, nb),
            in_specs=[pl.BlockSpec((1, rb, C), lambda s, i, c_ref: (s, c_ref[0] * nb + i, 0)),
                      pl.BlockSpec((1, rb, C), lambda s, i, c_ref: (s, i, 0))],
            out_specs=pl.BlockSpec((1, rb, C), lambda s, i, c_ref: (s, i, 0))),
        out_shape=jax.ShapeDtypeStruct((4, half, C), BF16),
        compiler_params=_cp("parallel", "parallel"),
    )(c_arr, g, sib)


def _sum4(parts, land, s_arr, rb, name):
    _, H, C = land.shape

    def body(s_ref, own_ref, r_ref, o_ref):
        own = own_ref[0].astype(F32)
        tot = jnp.zeros((rb, C), F32)
        for j in range(4):
            tot = tot + jnp.where(s_ref[0] == j, own, r_ref[j].astype(F32))
        o_ref[...] = tot

    return pl.pallas_call(
        body, name=name,
        grid_spec=pltpu.PrefetchScalarGridSpec(
            num_scalar_prefetch=1, grid=(H // rb,),
            in_specs=[pl.BlockSpec((1, rb, C), lambda i, s_ref: (s_ref[0], i, 0)),
                      pl.BlockSpec((4, rb, C), lambda i, s_ref: (0, i, 0))],
            out_specs=pl.BlockSpec((rb, C), lambda i, s_ref: (i, 0))),
        out_shape=jax.ShapeDtypeStruct((H, C), F32), compiler_params=_cp("parallel"),
    )(s_arr, parts, land)


HBM_SPEC = pl.BlockSpec(memory_space=pltpu.HBM)
SEM_SPEC = pl.BlockSpec(memory_space=pltpu.SEMAPHORE)
EFFECT = pltpu.SideEffectType.DATAFLOW_SIDE_EFFECTING


def _split_start(name, bufs, n_sem, plan, dep):
    nb = len(bufs)

    def body(*refs):
        ins, send, recv, token = refs[:nb], refs[nb + 1], refs[nb + 2], refs[-1]
        for i, (src, dst, dev, _) in enumerate(plan(ins)):
            pltpu.make_async_remote_copy(src, dst, send.at[i], recv.at[i], device_id=dev, device_id_type=MESH).start()
        token[...] = jnp.zeros_like(token)

    outs = pl.pallas_call(
        body, name=name,
        out_shape=(pltpu.SemaphoreType.DMA((n_sem,)), pltpu.SemaphoreType.DMA((n_sem,)),
                   *[pltpu.HBM(b.shape, b.dtype) for b in bufs], jax.ShapeDtypeStruct((8, 128), F32)),
        in_specs=[HBM_SPEC] * nb + [pl.BlockSpec(memory_space=pl.ANY)],
        out_specs=(SEM_SPEC, SEM_SPEC, *([HBM_SPEC] * nb), pl.BlockSpec(memory_space=pltpu.VMEM)),
        input_output_aliases={i: 2 + i for i in range(nb)},
        compiler_params=pltpu.CompilerParams(has_side_effects=EFFECT),
    )(*[pltpu.with_memory_space_constraint(b, pltpu.HBM) for b in bufs], dep)
    return outs[0], outs[1], list(outs[2:2 + nb]), outs[-1]


def _split_wait(name, send, recv, bufs, after, plan):
    nb = len(bufs)

    def body(*refs):
        ins, send_s, recv_s = refs[:nb], refs[nb], refs[nb + 1]
        for i, (src, dst, dev, mine) in enumerate(plan(ins)):
            pltpu.make_async_remote_copy(src, dst, send_s.at[i], recv_s.at[i], device_id=dev,
                                         device_id_type=MESH).wait_send()
            pltpu.make_async_remote_copy(src, mine, send_s.at[i], recv_s.at[i], device_id=dev,
                                         device_id_type=MESH).wait_recv()

    outs = pl.pallas_call(
        body, name=name, out_shape=[pltpu.HBM(b.shape, b.dtype) for b in bufs],
        in_specs=[HBM_SPEC] * nb + [SEM_SPEC, SEM_SPEC, pl.BlockSpec(memory_space=pl.ANY)],
        out_specs=[HBM_SPEC] * nb, input_output_aliases={i: i for i in range(nb)},
        compiler_params=pltpu.CompilerParams(has_side_effects=EFFECT),
    )(*bufs, send, recv, after)
    return list(outs)


def _copies_now(name, bufs, n_sem, plan):
    nb = len(bufs)

    def body(*refs):
        ins, token, send, recv = refs[:nb], refs[2 * nb], refs[-2], refs[-1]
        token[...] = jnp.zeros_like(token)
        todo = plan(ins)
        for i, (src, dst, dev, _) in enumerate(todo):
            pltpu.make_async_remote_copy(src, dst, send.at[i], recv.at[i], device_id=dev, device_id_type=MESH).start()
        for i, (src, dst, dev, mine) in enumerate(todo):
            pltpu.make_async_remote_copy(src, mine, send.at[i], recv.at[i], device_id=dev, device_id_type=MESH).wait_recv()
        for i, (src, dst, dev, _) in enumerate(todo):
            pltpu.make_async_remote_copy(src, dst, send.at[i], recv.at[i], device_id=dev, device_id_type=MESH).wait_send()

    outs = pl.pallas_call(
        body, name=name,
        out_shape=[pltpu.HBM(b.shape, b.dtype) for b in bufs] + [jax.ShapeDtypeStruct((8, 128), F32)],
        in_specs=[HBM_SPEC] * nb, out_specs=[HBM_SPEC] * nb + [pl.BlockSpec(memory_space=pltpu.VMEM)],
        input_output_aliases={i: i for i in range(nb)},
        scratch_shapes=[pltpu.SemaphoreType.DMA((n_sem,)), pltpu.SemaphoreType.DMA((n_sem,))],
    )(*[pltpu.with_memory_space_constraint(b, pltpu.HBM) for b in bufs])
    return list(outs[:nb]), outs[nb]


def _slot(land, s, rows, cols):
    if cols is None:
        return land.at[s, rows]
    return land.at[rows, pl.ds(pl.multiple_of(s * cols, 128), cols)]


def _plan_gather_ici(cols):
    nw = len(cols)

    def plan(refs):
        x, y, c = _pos()
        my_s = 2 * x + y
        out = []
        for w in range(nw):
            mine = _half_rows(refs[w], c)
            for px, py, ps in _chips():
                out.append((refs[w].at[mine], _slot(refs[nw + w], my_s, mine, cols[w]), (px, py, c),
                            _slot(refs[nw + w], ps, mine, cols[w])))
        return out
    return plan


def _plan_gather_fwd(cols, rows):
    def plan(refs):
        x, y, c = _pos()
        out = []
        for w in range(len(cols)):
            half = rows[w] // 2
            mine = pl.ds(pl.multiple_of(c * half, 8), half)
            other = pl.ds(pl.multiple_of((1 - c) * half, 8), half)
            for px, py, ps in _chips():
                got = _slot(refs[w], ps, mine, cols[w])
                out.append((got, got, (x, y, 1 - c), _slot(refs[w], ps, other, cols[w])))
        return out
    return plan


def _plan_swap(nw):
    def plan(refs):
        x, y, c = _pos()
        return [(refs[w].at[:, _half_rows(refs[w], 1 - c)], refs[nw + w], (x, y, 1 - c), refs[nw + w])
                for w in range(nw)]
    return plan


def _plan_scatter(nw):
    def plan(refs):
        x, y, c = _pos()
        my_s = 2 * x + y
        out = []
        for w in range(nw):
            for px, py, ps in _chips():
                out.append((refs[w].at[ps], refs[nw + w].at[my_s], (px, py, c), refs[nw + w].at[ps]))
        return out
    return plan


def _plan_join(nw):
    def plan(refs):
        x, y, c = _pos()
        out = []
        for w in range(nw):
            land = refs[nw + w]
            out.append((refs[w], land.at[_half_rows(land, c)], (x, y, 1 - c), land.at[_half_rows(land, 1 - c)]))
        return out
    return plan


def _hbm_empty(shape, dtype):
    return pltpu.with_memory_space_constraint(lax.empty(shape, dtype), pltpu.HBM)


def _put_slot(land, own, slot):
    return lax.dynamic_update_slice(land, own[None], (slot,) + (0,) * own.ndim)


def _pad_lanes(a, n):
    return jnp.pad(a, ((0, 0), (0, n - a.shape[1])))


def kernel(x, c, positions, w_ada, b_ada, norm1_w, w_in, conv_w, conv_b, dt_bias, a_log, d_skip, attn_sinks, ssm_norm_w, w_out, norm2_w, w_gate_up, w_down, final_norm_w, loss_target, m_w_ada, m_b_ada, m_norm1_w, m_w_in, m_conv_w, m_conv_b, m_dt_bias, m_a_log, m_d_skip, m_attn_sinks, m_ssm_norm_w, m_w_out, m_norm2_w, m_w_gate_up, m_w_down, m_final_norm_w, v_w_ada, v_b_ada, v_norm1_w, v_w_in, v_conv_w, v_conv_b, v_dt_bias, v_a_log, v_d_skip, v_attn_sinks, v_ssm_norm_w, v_w_out, v_norm2_w, v_w_gate_up, v_w_down, v_final_norm_w):
    T = x.shape[1]
    tm = min(256, T)
    xi, yi, ci = lax.axis_index("x"), lax.axis_index("y"), lax.axis_index("c")
    my_s = 2 * xi + yi
    xs = x[0]
    tgt = loss_target[0]

    payload = jnp.concatenate([c, conv_w[0].reshape(1, CONVK * 256)], axis=1)
    gat, mod4, tok = _mod_exchange(payload, w_ada[0], b_ada.reshape(4, 1, 1536))
    mod6 = mod4.reshape(6, D)
    cw_dev = gat[:, 0, D:].reshape(4, 2, CONVK, 256)[:, 0]
    conv_full = cw_dev.transpose(1, 0, 2).reshape(CONVK, CONVC)

    w_in_b = w_in[0].astype(BF16)
    s_i, r_i, bufs, tok = _split_start("wgather_in_ici_start", [w_in_b, _hbm_empty((4,) + w_in_b.shape, BF16)], 3,
                                       _plan_gather_ici([None]), tok)
    inv_freq = (10000.0 ** (-jnp.arange(32, dtype=F32) / 32))
    cos, sin_s = _rope_tables(positions, inv_freq.reshape(32, 1), min(512, T), tok)
    bufs = _split_wait("wgather_in_ici_wait", s_i, r_i, bufs, cos, _plan_gather_ici([None]))
    bufs, tok = _copies_now("wgather_in_fwd", bufs[1:], 3, _plan_gather_fwd([None], [D]))
    g_in = _put_slot(bufs[0], w_in_b, my_s)
    w_pad = jnp.concatenate([g_in[0], g_in[1], g_in[2], g_in[3], jnp.zeros((D, IN_PAD - IN_PROJ), BF16)], axis=1)

    late = [w_out[0].astype(BF16), w_gate_up[0].astype(BF16), w_down[0].astype(BF16)]
    lands = [_hbm_empty((4, D // 4, D), BF16), _hbm_empty((D, 2 * DFF), BF16), _hbm_empty((4, DFF // 4, D), BF16)]
    cols3, rows3 = [None, GU_SH, None], [D // 4, D, DFF // 4]
    s_a, r_a, bufs, tok = _split_start("wgather_ici_start", late + lands, 9, _plan_gather_ici(cols3), tok)

    qkv, z, xbc, dtr, h1b = _in_proj_fwd(xs, cos, sin_s, mod6, norm1_w, w_pad, min(512, T), tok)
    sinks = attn_sinks
    attn, lse = _attn_fwd(qkv, sinks)
    bufs = _split_wait("wgather_ici_wait", s_a, r_a, bufs, attn, _plan_gather_ici(cols3))
    s_b, r_b, lands, tok = _split_start("wgather_fwd_start", bufs[3:], 9, _plan_gather_fwd(cols3, rows3), attn)
    dtb = _pad_lanes(dt_bias, 128)
    alog = _pad_lanes(a_log, 128)
    dskx = jnp.repeat(d_skip, HD, axis=1)
    mats = _ssd_mats()
    ynorm, ypre, states, conv_pre = _ssd_fwd(xbc, z, dtr, conv_full, conv_b, dtb, alog, dskx, ssm_norm_w, mats, tok)
    lands = _split_wait("wgather_fwd_wait", s_b, r_b, lands, ynorm, _plan_gather_fwd(cols3, rows3))
    w_out_f = _put_slot(lands[0], late[0], my_s).reshape(D, D)
    w_dn_f = _put_slot(lands[2], late[2], my_s).reshape(DFF, D)
    s_arr = my_s.reshape(1).astype(jnp.int32)

    fw2 = final_norm_w.reshape(1, D)
    sq, dmix, dx1, h2b, act, dfb, dgu, dob, sm_ffn = _mix_ffn(
        xs, attn, ynorm, tgt, mod6, norm2_w, fw2, w_out_f, lands[1], late[1], s_arr, w_dn_f, tm)

    tt = min(2048, T)
    c_arr = ci.reshape(1).astype(jnp.int32)
    tok0 = jnp.zeros((8, 128), F32)
    gw_dn4 = _tn_matmul(act, dfb, GU_SH, D, tt, "dw_down", tok0).reshape(4, DFF // 4, D)
    gw_gu4 = _tn_matmul(h2b, dgu, D, GU_SH, tt, "dw_gate_up", tok0)
    gw_out4 = jnp.concatenate(
        [_tn_matmul(attn, dob, AW, D, tt, "dw_out_a", tok0)[0],
         _tn_matmul(ynorm, dob, SW, D, tt, "dw_out_y", tok0)[0]], axis=0).reshape(4, D // 4, D)
    big1 = [gw_out4, gw_gu4, gw_dn4]
    rbs1 = [128, 512, 352]
    sib1 = [_hbm_empty((4, g.shape[1] // 2, g.shape[2]), F32) for g in big1]
    s_c, r_c, bufs, tok = _split_start("gswap_start", big1 + sib1, 3, _plan_swap(3), tok0)

    dzxd, d_cw, d_cb, d_sw, d_sk, d_dtb, d_av = _ssd_bwd(
        xbc, conv_pre, z, dtr, ypre, states, dmix, conv_full, dtb, alog, dskx, ssm_norm_w, mats, tok)
    bufs = _split_wait("gswap_wait", s_c, r_c, bufs, dzxd, _plan_swap(3))
    sums1 = [_add_half(g, s, c_arr, rb, "grad_add_%d" % i)
             for i, (g, s, rb) in enumerate(zip(bufs[:3], bufs[3:], rbs1))]
    land1 = [_hbm_empty(p.shape, BF16) for p in sums1]
    s_d, r_d, bufs, tok = _split_start("gscatter_start", sums1 + land1, 9, _plan_scatter(3), tok0)
    dqkv, d_sinks = _attn_bwd(qkv, sinks, lse, dmix, cos, sin_s, tok)
    bufs = _split_wait("gscatter_wait", s_d, r_d, bufs, dqkv, _plan_scatter(3))
    halves1 = [_sum4(p, l, s_arr, rb, "grad_sum_%d" % i)
               for i, (p, l, rb) in enumerate(zip(bufs[:3], bufs[3:], rbs1))]
    full1 = [_hbm_empty((2 * h.shape[0], h.shape[1]), F32) for h in halves1]
    s_e, r_e, bufs, tok = _split_start("gjoin_start", halves1 + full1, 3, _plan_join(3), tok0)
    gq = _tn_matmul(h1b, dqkv, D, 768, tt, "dw_in_qkv", tok)[0]
    gz = _tn_matmul(h1b, dzxd, D, 1664, tt, "dw_in_zxd", tok)[0]
    gw_in4 = jnp.stack([gq[:, :IN_SH], jnp.concatenate([gq[:, IN_SH:], gz[:, :2 * IN_SH - 768]], axis=1),
                        gz[:, 2 * IN_SH - 768:3 * IN_SH - 768], gz[:, 3 * IN_SH - 768:4 * IN_SH - 768]])
    joined1 = _split_wait("gjoin_wait", s_e, r_e, bufs, gw_in4, _plan_join(3))

    sib0 = _hbm_empty((4, D // 2, IN_SH), F32)
    bufs, _ = _copies_now("gswap_in", [gw_in4, sib0], 1, _plan_swap(1))
    sum0 = _add_half(bufs[0], bufs[1], c_arr, 512, "grad_add_in")
    s_g, r_g, bufs, tok = _split_start("gscatter_in_start", [sum0, _hbm_empty(sum0.shape, BF16)], 3, _plan_scatter(1),
                                       tok0)
    grad_x, sm_in = _in_proj_bwd(xs, dx1, dqkv, dzxd, mod6, norm1_w, w_pad, min(512, T), tok)
    bufs = _split_wait("gscatter_in_wait", s_g, r_g, bufs, grad_x, _plan_scatter(1))
    half0 = _sum4(bufs[0], bufs[1], s_arr, 512, "grad_sum_in")
    joined0, _ = _copies_now("gjoin_in", [half0, _hbm_empty((D, IN_SH), F32)], 1, _plan_join(1))

    a_neg = -jnp.exp(alog)
    pieces = [sm_in[1:2], sm_in[2:3], sm_ffn[5:6], sm_ffn[2:3], sm_ffn[3:4], sm_ffn[4:5],
              sm_in[0:1], sm_ffn[1:2], sm_ffn[0:1], d_cb, d_cw.reshape(1, CONVK * CONVC),
              _pad_lanes(d_sw, SW), d_dtb, d_av * a_neg, d_sk, d_sinks,
              _pad_lanes((0.5 / D * jnp.sum(sq)).reshape(1, 1), 128)]
    vec = jnp.concatenate(pieces, axis=1)
    tot, allv = _small_reduce(vec)
    o = 0
    offs = []
    for p in pieces:
        offs.append(o)
        o += p.shape[1]
    seg = lambda i, n: (offs[i], n)
    g_conv_w = lax.dynamic_slice_in_dim(
        tot[:, offs[10]:offs[10] + CONVK * CONVC].reshape(CONVK, CONVC), my_s * 256, 256, axis=1)
    loss = tot[0, offs[16]]

    small_names = ["b_ada", "norm1_w", "conv_w", "conv_b", "dt_bias", "a_log", "d_skip", "attn_sinks", "ssm_norm_w",
                   "norm2_w", "final_norm_w"]
    small_g = [(0, 6 * D), seg(6, D), g_conv_w, seg(9, D), seg(12, 8), seg(13, 8), seg(14, 8), seg(15, 8),
               seg(11, SW), seg(7, D), seg(8, D)]
    as2d = lambda a: a.reshape(-1, a.shape[-1])
    small_w = [as2d(a) for a in (b_ada, norm1_w, conv_w, conv_b, dt_bias, a_log, d_skip, attn_sinks, ssm_norm_w,
                                 norm2_w, final_norm_w)]
    small_m = [as2d(a) for a in (m_b_ada, m_norm1_w, m_conv_w, m_conv_b, m_dt_bias, m_a_log, m_d_skip, m_attn_sinks,
                                 m_ssm_norm_w, m_norm2_w, m_final_norm_w)]
    small_v = [as2d(a) for a in (v_b_ada, v_norm1_w, v_conv_w, v_conv_b, v_dt_bias, v_a_log, v_d_skip, v_attn_sinks,
                                 v_ssm_norm_w, v_norm2_w, v_final_norm_w)]
    small_g, sd, smn, svn = _adam_small(tot, small_g, small_w, small_m, small_v)
    g_ada, d_ada, m_ada, v_ada = _adam_w_ada(gat, allv, s_arr, w_ada[0], m_w_ada[0], v_w_ada[0], 256)
    native = lambda a: a.transpose(2, 0, 1)
    g_in_s, d_in, m_in, v_in = [a.transpose(1, 2, 0) for a in _adam_w_in(
        native(w_in), joined0[0], joined0[1], native(m_w_in), native(v_w_in), c_arr)]
    g_out_s, d_out, m_out, v_out = _adam_2d(w_out[0], joined1[0], joined1[3], m_w_out[0], v_w_out[0], c_arr, 128,
                                            "adam_w_out")
    g_gu_s, d_gu, m_gu, v_gu = _adam_2d(w_gate_up[0], joined1[1], joined1[4], m_w_gate_up[0], v_w_gate_up[0], c_arr,
                                        256, "adam_w_gate_up")
    g_dn_s, d_dn, m_dn, v_dn = _adam_2d(w_down[0], joined1[2], joined1[5], m_w_down[0], v_w_down[0], c_arr, 352,
                                        "adam_w_down")

    order = ["w_ada", "b_ada", "norm1_w", "w_in", "conv_w", "conv_b", "dt_bias", "a_log", "d_skip", "attn_sinks",
             "ssm_norm_w", "w_out", "norm2_w", "w_gate_up", "w_down", "final_norm_w"]
    shapes = dict(w_ada=w_ada.shape, b_ada=b_ada.shape, norm1_w=norm1_w.shape, w_in=w_in.shape, conv_w=conv_w.shape,
                  conv_b=conv_b.shape, dt_bias=dt_bias.shape, a_log=a_log.shape, d_skip=d_skip.shape,
                  attn_sinks=attn_sinks.shape, ssm_norm_w=ssm_norm_w.shape, w_out=w_out.shape, norm2_w=norm2_w.shape,
                  w_gate_up=w_gate_up.shape, w_down=w_down.shape, final_norm_w=final_norm_w.shape)
    grads = dict(w_ada=g_ada, w_in=g_in_s, w_out=g_out_s, w_gate_up=g_gu_s, w_down=g_dn_s)
    deltas = dict(w_ada=d_ada, w_in=d_in, w_out=d_out, w_gate_up=d_gu, w_down=d_dn)
    new_m = dict(w_ada=m_ada, w_in=m_in, w_out=m_out, w_gate_up=m_gu, w_down=m_dn)
    new_v = dict(w_ada=v_ada, w_in=v_in, w_out=v_out, w_gate_up=v_gu, w_down=v_dn)
    for i, nme in enumerate(small_names):
        grads[nme], deltas[nme], new_m[nme], new_v[nme] = small_g[i], sd[i], smn[i], svn[i]
    outs = [loss, grad_x[None]]
    for table in (grads, deltas, new_m, new_v):
        outs += [table[nme].reshape(shapes[nme]) for nme in order]
    return tuple(outs)
```

```python
import functools
import math

import jax
import jax.numpy as jnp
from jax import lax
from jax.experimental import pallas as pl
from jax.experimental.pallas import tpu as pltpu

F32 = jnp.float32
BF16 = jnp.bfloat16
HI = lax.Precision.HIGHEST
MESH = pl.DeviceIdType.MESH

D = 1024
HD = 64
AW = 512
SW = 512
NST = 128
CONVK = 4
CONVC = 1024
BLK = 128
CPS = 4
SSD_FWD_CPS = 8
ATTN_BPS = 8
IN_PROJ = 2312
IN_PAD = 2432
IN_SH = IN_PROJ // 4
DFF = 2816
GU_SH = 1408
FF_SPLITS = ((0, 1536), (1536, 2816))
EPS = 1e-6
NEG = -1e30
LR, B1, B2, AEPS, WD, STEP = 0.001, 0.9, 0.999, 1e-08, 0.01, 10
VMEM_LIMIT = 58 * 1024 * 1024


def _cp(*sem):
    return pltpu.CompilerParams(dimension_semantics=sem or None, vmem_limit_bytes=VMEM_LIMIT)


def _dot(a, b):
    return jnp.dot(a, b, preferred_element_type=F32)


def _dot_nt(a, b):
    return lax.dot_general(a, b, (((1,), (1,)), ((), ())), preferred_element_type=F32)


def _dot_tn(a, b):
    return lax.dot_general(a, b, (((0,), (0,)), ((), ())), preferred_element_type=F32)


def _dot_hi(a, b):
    return jnp.dot(a, b, precision=HI, preferred_element_type=F32)


def _sigmoid(x):
    return 1.0 / (1.0 + jnp.exp(-x))


def _iota(shape, dim):
    return lax.broadcasted_iota(jnp.int32, shape, dim)


def _load_resident(hbm_ref, vmem_ref, sem):
    @pl.when(pl.program_id(0) == 0)
    def _():
        cp = pltpu.make_async_copy(hbm_ref, vmem_ref, sem)
        cp.start()
        cp.wait()


def _swap32(t):
    lane = _iota(t.shape, 1)
    return jnp.where((lane & 63) < 32, pltpu.roll(t, 96, 1), pltpu.roll(t, 32, 1))


def _rope_fwd(t, cos, sin_s):
    return t * cos + _swap32(t) * sin_s


def _rope_bwd(t, cos, sin_s):
    return t * cos - _swap32(t) * sin_s


DEP_SPEC = pl.BlockSpec((8, 128), lambda *_: (0, 0))


def _rope_tables(pos_row, inv_freq_col, tm, dep):
    T = pos_row.shape[1]
    lane, row = jnp.arange(128)[None, :], jnp.arange(96)[:, None]
    pick = (lane % 32) == (row % 32)
    sel_cos = pick.astype(BF16)
    sel_sin = jnp.where(pick, jnp.where(lane % 64 < 32, -1.0, 1.0), 0.0).astype(BF16)

    def body(p_ref, f_ref, sc_ref, ss_ref, dep_ref, cos_ref, sin_ref):
        ang = f_ref[...] * p_ref[...].astype(F32)
        cos_ref[...] = _dot_tn(_pieces(jnp.cos(ang), 3, 0), sc_ref[...])
        sin_ref[...] = _dot_tn(_pieces(jnp.sin(ang), 3, 0), ss_ref[...])

    full = lambda a: pl.BlockSpec(a.shape, lambda i: (0,) * a.ndim)
    return pl.pallas_call(
        body, name="rope_tables", grid=(T // tm,),
        in_specs=[pl.BlockSpec((1, tm), lambda i: (0, i)), full(inv_freq_col), full(sel_cos), full(sel_sin), DEP_SPEC],
        out_specs=[pl.BlockSpec((tm, 128), lambda i: (i, 0))] * 2,
        out_shape=[jax.ShapeDtypeStruct((T, 128), F32)] * 2,
        compiler_params=_cp("parallel"),
    )(pos_row, inv_freq_col, sel_cos, sel_sin, dep)


def _in_proj_fwd(x, cos, sin_s, mod6, norm1_w, w_pad, tm, dep):
    T = x.shape[0]

    def body(x_ref, cos_ref, sin_ref, mod_ref, nw_ref, w_hbm, dep_ref, qkv_ref, z_ref, xbc_ref, dt_ref, h_ref, w_vmem,
             sem):
        _load_resident(w_hbm, w_vmem, sem)
        xv = x_ref[...]
        r = lax.rsqrt(jnp.mean(xv * xv, axis=-1, keepdims=True) + EPS)
        h = (xv * r * nw_ref[...]) * (1.0 + mod_ref[1:2, :]) + mod_ref[0:1, :]
        hb = h.astype(BF16)
        h_ref[...] = hb
        proj = _dot(hb, w_vmem[...])
        cs, sn = cos_ref[...], sin_ref[...]
        for j in range(5):
            qkv_ref[:, 128 * j:128 * (j + 1)] = _rope_fwd(proj[:, 128 * j:128 * (j + 1)], cs, sn).astype(BF16)
        qkv_ref[:, 640:768] = proj[:, 640:768].astype(BF16)
        z_ref[...] = proj[:, 768:1280]
        xbc_ref[...] = proj[:, 1280:2304]
        dt_ref[...] = proj[:, 2304:2432]

    row = lambda w: pl.BlockSpec((tm, w), lambda i: (i, 0))
    full = lambda a: pl.BlockSpec(a.shape, lambda i: (0,) * a.ndim)
    return pl.pallas_call(
        body, name="in_proj_fwd", grid=(T // tm,),
        in_specs=[row(D), row(128), row(128), full(mod6), full(norm1_w), pl.BlockSpec(memory_space=pl.ANY), DEP_SPEC],
        out_specs=[row(768), row(512), row(1024), row(128), row(D)],
        out_shape=[jax.ShapeDtypeStruct((T, 768), BF16), jax.ShapeDtypeStruct((T, 512), F32),
                   jax.ShapeDtypeStruct((T, 1024), F32), jax.ShapeDtypeStruct((T, 128), F32),
                   jax.ShapeDtypeStruct((T, D), BF16)],
        scratch_shapes=[pltpu.VMEM((D, IN_PAD), BF16), pltpu.SemaphoreType.DMA],
        compiler_params=_cp("arbitrary"),
    )(x, cos, sin_s, mod6, norm1_w, w_pad, dep)


def _head_variants(pair, j):
    lane = _iota(pair.shape, 1)
    lo = lane < 64
    kv = j // 2
    ev = jnp.where(lo, pair, 0.0)
    od = jnp.where(lo, 0.0, pair)
    if kv == 0:
        od = pltpu.roll(od, 64, 1)
    else:
        ev = pltpu.roll(ev, 64, 1)
    return ev.astype(BF16), od.astype(BF16)


def _kv_variants(vcat):
    lane = _iota(vcat.shape, 1)
    lo = lane < 64
    v0 = jnp.where(lo, vcat, 0.0)
    v1 = jnp.where(lo, 0.0, vcat)
    out = {
        (0, 0): v0, (0, 1): pltpu.roll(v0, 64, 1),
        (1, 0): pltpu.roll(v1, 64, 1), (1, 1): v1,
    }
    return {k: v.astype(BF16) for k, v in out.items()}


def _fold_masks(n):
    upper = _iota((BLK, BLK), 1) > _iota((BLK, BLK), 0)
    return upper, upper & (n == 0)


def _attn_fwd(qkv, sinks):
    CPS = ATTN_BPS
    T = qkv.shape[0]
    nsteps = T // (CPS * BLK)

    def body(sink_ref, q_ref, kc_ref, kp_ref, vc_ref, vp_ref, o_ref, lse_ref):
        for sub in range(CPS):
            rows, before = slice(BLK * sub, BLK * (sub + 1)), slice(BLK * (sub - 1), BLK * sub)
            block(pl.program_id(0) * CPS + sub, sink_ref, q_ref.at[rows, :], kc_ref.at[rows, :],
                  kp_ref if sub == 0 else kc_ref.at[before, :], vc_ref.at[rows, :],
                  vp_ref if sub == 0 else vc_ref.at[before, :], o_ref.at[rows, :], lse_ref.at[rows, :])

    def block(n, sink_ref, q_ref, kc_ref, kp_ref, vc_ref, vp_ref, o_ref, lse_ref):
        vpv = _kv_variants(vp_ref[...].astype(F32))
        vcv = _kv_variants(vc_ref[...].astype(F32))
        q_all = jnp.concatenate(
            [v for j in range(4) for v in _head_variants(q_ref[:, 128 * j:128 * (j + 1)].astype(F32), j)], axis=0)
        s_prev = _dot_nt(q_all, kp_ref[...])
        s_cur = _dot_nt(q_all, kc_ref[...])
        upper, dead = _fold_masks(n)
        lane = _iota((BLK, 128), 1)
        lse_acc = jnp.zeros((BLK, 128), F32)
        for jj in range(4):
            acc = jnp.zeros((BLK, 128), F32)
            for par in range(2):
                h = 2 * jj + par
                rows = slice(h * BLK, (h + 1) * BLK)
                sink = sink_ref[0, h]
                s = jnp.where(dead, NEG, jnp.where(upper, s_prev[rows], s_cur[rows]) * 0.125)
                m = jnp.maximum(jnp.max(s, axis=1, keepdims=True), sink)
                p = jnp.exp(s - m)
                den = jnp.sum(p, axis=1, keepdims=True) + jnp.exp(sink - m)
                pn = p * (1.0 / den)
                acc = (acc + _dot(jnp.where(upper, pn, 0.0).astype(BF16), vpv[(jj // 2, par)])
                       + _dot(jnp.where(upper, 0.0, pn).astype(BF16), vcv[(jj // 2, par)]))
                lse_acc = jnp.where(lane == h, m + jnp.log(den), lse_acc)
            o_ref[:, 128 * jj:128 * (jj + 1)] = acc.astype(BF16)
        lse_ref[...] = lse_acc

    RB = CPS * BLK
    prev = lambda n: jnp.maximum(n * CPS - 1, 0)
    return pl.pallas_call(
        body, name="attn_fwd", grid=(nsteps,),
        in_specs=[pl.BlockSpec(memory_space=pltpu.SMEM),
                  pl.BlockSpec((RB, 512), lambda n: (n, 0)),
                  pl.BlockSpec((RB, 128), lambda n: (n, 4)),
                  pl.BlockSpec((BLK, 128), lambda n: (prev(n), 4)),
                  pl.BlockSpec((RB, 128), lambda n: (n, 5)),
                  pl.BlockSpec((BLK, 128), lambda n: (prev(n), 5))],
        out_specs=[pl.BlockSpec((RB, 512), lambda n: (n, 0)), pl.BlockSpec((RB, 128), lambda n: (n, 0))],
        out_shape=[jax.ShapeDtypeStruct((T, 512), BF16), jax.ShapeDtypeStruct((T, 128), F32)],
        compiler_params=_cp("parallel"),
    )(sinks, qkv, qkv, qkv, qkv, qkv)


def _attn_bwd(qkv, sinks, lse, dmix, cos, sin_s, dep):
    T = qkv.shape[0]
    nb = T // BLK

    def body(sink_ref, q_ref, kc_ref, kp_ref, vc_ref, vp_ref, lse_ref, do_ref, cq_ref, sq_ref, ck_ref, sk_ref,
             dep_ref, out_ref, ds_ref, dq_car, dk_car, dv_car):
        n = pl.program_id(0)
        lane = _iota((BLK, 128), 1)

        @pl.when(n == 0)
        def _():
            ds_ref[...] = jnp.zeros_like(ds_ref)
            dq_car[...] = jnp.zeros_like(dq_car)
            dk_car[...] = jnp.zeros_like(dk_car)
            dv_car[...] = jnp.zeros_like(dv_car)

        @pl.when(n < nb)
        def _():
            kp, kc, vp, vc = kp_ref[...], kc_ref[...], vp_ref[...], vc_ref[...]
            kpv = _kv_variants(kp.astype(F32))
            kcv = _kv_variants(kc.astype(F32))
            lse_v = lse_ref[...]
            q_all = jnp.concatenate(
                [v for j in range(4) for v in _head_variants(q_ref[:, 128 * j:128 * (j + 1)].astype(F32), j)], axis=0)
            do_all = jnp.concatenate(
                [v for j in range(4) for v in _head_variants(do_ref[:, 128 * j:128 * (j + 1)], j)], axis=0)
            s_prev, s_cur = _dot_nt(q_all, kp), _dot_nt(q_all, kc)
            dp_prev, dp_cur = _dot_nt(do_all, vp), _dot_nt(do_all, vc)
            upper, dead = _fold_masks(n)
            out_ref[:, 0:512] = dq_car[...]
            dsk = jnp.zeros((1, 128), F32)
            ds_u, ds_l, p_u, p_l = [], [], [], []
            for jj in range(4):
                dq_acc = jnp.zeros((BLK, 128), F32)
                for par in range(2):
                    h = 2 * jj + par
                    rows = slice(h * BLK, (h + 1) * BLK)
                    lse_h = jnp.sum(jnp.where(lane == h, lse_v, 0.0), axis=1, keepdims=True)
                    s = jnp.where(dead, NEG, jnp.where(upper, s_prev[rows], s_cur[rows]) * 0.125)
                    p = jnp.exp(s - lse_h)
                    dp = jnp.where(upper, dp_prev[rows], dp_cur[rows])
                    delta = jnp.sum(p * dp, axis=1, keepdims=True)
                    ds = p * (dp - delta) * 0.125
                    dsu, dsl = jnp.where(upper, ds, 0.0).astype(BF16), jnp.where(upper, 0.0, ds).astype(BF16)
                    dq_acc = dq_acc + _dot(dsu, kpv[(jj // 2, par)]) + _dot(dsl, kcv[(jj // 2, par)])
                    ds_u.append(dsu)
                    ds_l.append(dsl)
                    p_u.append(jnp.where(upper, p, 0.0).astype(BF16))
                    p_l.append(jnp.where(upper, 0.0, p).astype(BF16))
                    dsk = dsk + jnp.where(lane[0:1] == h, -jnp.sum(jnp.exp(sink_ref[0, h] - lse_h) * delta), 0.0)
                dq_car[:, 128 * jj:128 * (jj + 1)] = _rope_bwd(dq_acc, cq_ref[...], sq_ref[...]).astype(BF16)
            stack = lambda parts: jnp.concatenate(parts, axis=0)
            dk_prev, dk_cur = _dot_tn(stack(ds_u), q_all), _dot_tn(stack(ds_l), q_all)
            dv_prev, dv_cur = _dot_tn(stack(p_u), do_all), _dot_tn(stack(p_l), do_all)
            ds_ref[...] += dsk
            out_ref[:, 512:640] = _rope_bwd(dk_car[...] + dk_prev, ck_ref[...], sk_ref[...]).astype(BF16)
            out_ref[:, 640:768] = (dv_car[...] + dv_prev).astype(BF16)
            dk_car[...] = dk_cur
            dv_car[...] = dv_cur

        @pl.when(n == nb)
        def _():
            out_ref[:, 0:512] = dq_car[...]
            out_ref[:, 512:640] = _rope_bwd(dk_car[...], ck_ref[...], sk_ref[...]).astype(BF16)
            out_ref[:, 640:768] = dv_car[...].astype(BF16)

    cur = lambda n: jnp.minimum(n, nb - 1)
    prev = lambda n: jnp.maximum(cur(n) - 1, 0)
    outb = lambda n: jnp.maximum(n - 1, 0)
    return pl.pallas_call(
        body, name="attn_bwd", grid=(nb + 1,),
        in_specs=[pl.BlockSpec(memory_space=pltpu.SMEM),
                  pl.BlockSpec((BLK, 512), lambda n: (cur(n), 0)),
                  pl.BlockSpec((BLK, 128), lambda n: (cur(n), 4)),
                  pl.BlockSpec((BLK, 128), lambda n: (prev(n), 4)),
                  pl.BlockSpec((BLK, 128), lambda n: (cur(n), 5)),
                  pl.BlockSpec((BLK, 128), lambda n: (prev(n), 5)),
                  pl.BlockSpec((BLK, 128), lambda n: (cur(n), 0)),
                  pl.BlockSpec((BLK, 512), lambda n: (cur(n), 0)),
                  pl.BlockSpec((BLK, 128), lambda n: (cur(n), 0)),
                  pl.BlockSpec((BLK, 128), lambda n: (cur(n), 0)),
                  pl.BlockSpec((BLK, 128), lambda n: (outb(n), 0)),
                  pl.BlockSpec((BLK, 128), lambda n: (outb(n), 0)), DEP_SPEC],
        out_specs=[pl.BlockSpec((BLK, 768), lambda n: (outb(n), 0)), pl.BlockSpec((1, 128), lambda n: (0, 0))],
        out_shape=[jax.ShapeDtypeStruct((T, 768), BF16), jax.ShapeDtypeStruct((1, 128), F32)],
        scratch_shapes=[pltpu.VMEM((BLK, 512), BF16), pltpu.VMEM((BLK, 128), F32), pltpu.VMEM((BLK, 128), F32)],
        compiler_params=_cp("arbitrary"),
    )(sinks, qkv, qkv, qkv, qkv, qkv, lse, dmix, cos, sin_s, cos, sin_s, dep)


def _ssd_mats():
    e = jnp.arange(SW)[None, :] // HD == jnp.arange(128)[:, None]
    tri = jnp.arange(BLK)[None, :] <= jnp.arange(BLK)[:, None]
    return (jnp.tile(e, (3, 1)).astype(BF16), jnp.tile(e.T, (2, 1)).astype(BF16),
            jnp.tile(tri, (1, 3)).astype(BF16), jnp.tile(tri.T, (1, 3)).astype(BF16))


def _pieces(x, n, axis):
    out, r = [], x
    for i in range(n):
        p = r.astype(BF16)
        out.append(p)
        if i + 1 < n:
            r = r - p.astype(F32)
    return jnp.concatenate(out, axis=axis)


def _expand(x, e3):
    return _dot(_pieces(x, 3, 1), e3)


def _head_sums(x, et2):
    return _dot(_pieces(x, 2, 1), et2)


def _run_sum(tri3, x):
    return _dot(tri3, _pieces(x, 3, 0))


def _shift_down(u, tail, j):
    rolled = pltpu.roll(u, j, 0)
    first = jnp.where(_iota(tail.shape, 0) < j, pltpu.roll(tail, j, 0), rolled[0:8])
    return jnp.concatenate([first, rolled[8:]], axis=0)


def _shift_up(d, head, j):
    rolled = pltpu.roll(d, BLK - j, 0)
    last = jnp.where(_iota(head.shape, 0) >= 8 - j, pltpu.roll(head, 8 - j, 0), rolled[BLK - 8:])
    return jnp.concatenate([rolled[:BLK - 8], last], axis=0)


def _ssd_parts(dtr, dtb, alog, e3, tril3):
    xx = dtr + dtb
    dt = jnp.maximum(xx, 0.0) + jnp.log(1.0 + jnp.exp(-jnp.abs(xx)))
    a_neg = -jnp.exp(alog)
    tril = _iota((BLK, BLK), 1) <= _iota((BLK, BLK), 0)
    cs = _run_sum(tril3, dt * a_neg)
    csx = _expand(cs, e3)
    last = csx[BLK - 1:BLK, :]
    return dict(xx=xx, dt=dt, a_neg=a_neg, tril=tril, cs=cs, cs_t=cs.T,
                ecsx=jnp.exp(csx), dtex=jnp.exp(last - csx), cdx=jnp.exp(last), dtx=_expand(dt, e3))


def _decay(parts, h):
    seg = parts["cs"][:, h:h + 1] - parts["cs_t"][h:h + 1, :]
    return jnp.exp(jnp.where(parts["tril"], seg, NEG))


def _group_cols(a, g):
    return a[:, 256 * g:256 * (g + 1)]


def _ssd_fwd(xbc, z, dtr, conv_w, conv_b, dtb, alog, dskx, ssm_w, mats, dep):
    CPS = SSD_FWD_CPS
    T = xbc.shape[0]
    nc = T // BLK

    def body(u_ref, tail_ref, z_ref, dtr_ref, cw_ref, cb_ref, dtb_ref, al_ref, dk_ref, sw_ref, e3_ref, tril3_ref,
             dep_ref, yn_ref, yp_ref, st_ref, co_ref, s_scr):
        n = pl.program_id(0)

        @pl.when(n == 0)
        def _():
            s_scr[...] = jnp.zeros_like(s_scr)

        lane = _iota((BLK, 128), 1)
        lo = lane < 64
        for sub in range(CPS):
            rows = slice(BLK * sub, BLK * (sub + 1))
            u = u_ref[rows, :]
            tail = jnp.where(n > 0, tail_ref[...], 0.0) if sub == 0 else u_ref[BLK * sub - 8:BLK * sub, :]
            co = cb_ref[...] + cw_ref[3:4, :] * u
            for j in range(1, CONVK):
                co = co + cw_ref[3 - j:4 - j, :] * _shift_down(u, tail, j)
            co_ref[rows, :] = co
            xc = co * _sigmoid(co)
            pt = _ssd_parts(dtr_ref[rows, :], dtb_ref[...], al_ref[...], e3_ref[...], tril3_ref[...])
            xs = xc[:, :SW]
            bm = [xc[:, 512:640].astype(BF16), xc[:, 640:768].astype(BF16)]
            cm = [xc[:, 768:896].astype(BF16), xc[:, 896:1024].astype(BF16)]
            s_in = s_scr[...]
            st_ref[sub] = s_in
            xdt = xs * pt["dtx"]
            xde = (xdt * pt["dtex"]).astype(BF16)
            ys, s_new = [], []
            for g in range(2):
                cb = _dot_nt(cm[g], bm[g])
                yoff = _dot(cm[g], _group_cols(s_in, g).astype(BF16))
                s_new.append(_dot_tn(bm[g], _group_cols(xde, g)))
                for jj in range(2):
                    j = 2 * g + jj
                    chunk = xdt[:, 128 * j:128 * (j + 1)]
                    g_ev = (cb * _decay(pt, 2 * j)).astype(BF16)
                    g_od = (cb * _decay(pt, 2 * j + 1)).astype(BF16)
                    yd = (_dot(g_ev, jnp.where(lo, chunk, 0.0).astype(BF16))
                          + _dot(g_od, jnp.where(lo, 0.0, chunk).astype(BF16)))
                    ys.append(yd + yoff[:, 128 * jj:128 * (jj + 1)] * pt["ecsx"][:, 128 * j:128 * (j + 1)])
            y = jnp.concatenate(ys, axis=1) + xs * dk_ref[...]
            s_scr[...] = s_in * pt["cdx"] + jnp.concatenate(s_new, axis=1)
            yp_ref[rows, :] = y
            zv = z_ref[rows, :]
            yz = y * (zv * _sigmoid(zv))
            outs = []
            for g in range(2):
                yg = _group_cols(yz, g)
                outs.append(yg * lax.rsqrt(jnp.mean(yg * yg, axis=-1, keepdims=True) + EPS))
            yn_ref[rows, :] = (jnp.concatenate(outs, axis=1) * sw_ref[...]).astype(BF16)

    e3, _, tril3, _ = mats
    RB = CPS * BLK
    tail8 = lambda n: jnp.maximum(n * (RB // 8) - 1, 0)
    full = lambda a: pl.BlockSpec(a.shape, lambda n: (0,) * a.ndim)
    return pl.pallas_call(
        body, name="ssd_fwd", grid=(nc // CPS,),
        in_specs=[pl.BlockSpec((RB, CONVC), lambda n: (n, 0)), pl.BlockSpec((8, CONVC), lambda n: (tail8(n), 0)),
                  pl.BlockSpec((RB, SW), lambda n: (n, 0)), pl.BlockSpec((RB, 128), lambda n: (n, 0)),
                  full(conv_w), full(conv_b), full(dtb), full(alog), full(dskx), full(ssm_w), full(e3), full(tril3),
                  DEP_SPEC],
        out_specs=[pl.BlockSpec((RB, SW), lambda n: (n, 0)), pl.BlockSpec((RB, SW), lambda n: (n, 0)),
                   pl.BlockSpec((CPS, NST, SW), lambda n: (n, 0, 0)), pl.BlockSpec((RB, CONVC), lambda n: (n, 0))],
        out_shape=[jax.ShapeDtypeStruct((T, SW), BF16), jax.ShapeDtypeStruct((T, SW), F32),
                   jax.ShapeDtypeStruct((nc, NST, SW), F32), jax.ShapeDtypeStruct((T, CONVC), F32)],
        scratch_shapes=[pltpu.VMEM((NST, SW), F32)],
        compiler_params=_cp("arbitrary"),
    )(xbc, xbc, z, dtr, conv_w, conv_b, dtb, alog, dskx, ssm_w, e3, tril3, dep)


def _ssd_bwd(xbc, co_all, z, dtr, ypre, states, dmix, conv_w, dtb, alog, dskx, ssm_w, mats, dep):
    T = xbc.shape[0]
    nsteps = T // (CPS * BLK)

    def body(*refs):
        per_chunk, consts, out_ref, carried = refs[:7], refs[7:16], refs[17], refs[18:]
        i = pl.program_id(0)

        @pl.when(i == 0)
        def _():
            for r in carried:
                r[...] = jnp.zeros_like(r)

        for sub in reversed(range(CPS)):
            rows = slice(BLK * sub, BLK * (sub + 1))
            views = [r.at[sub:sub + 1] if k == 5 else r.at[rows, :] for k, r in enumerate(per_chunk)]
            chunk(*views, *consts, out_ref.at[rows, :], *carried)

        @pl.when(i == nsteps - 1)
        def _():
            dsk_ref, dskx_scr = carried[3], carried[8]
            dsk_ref[...] = _head_sums(jnp.broadcast_to(dskx_scr[...], (8, SW)), consts[6][...])[0:1]

    def chunk(u_ref, co_ref, z_ref, dtr_ref, yp_ref, st_ref, dyn_ref, cw_ref, dtb_ref, al_ref, dk_ref, sw_ref,
              e3_ref, et2_ref, tril3_ref, triu3_ref,
              out_ref, dcw_ref, dcb_ref, dsw_ref, dsk_ref, ddtb_ref, dav_ref, ds_scr, dco_scr, dskx_scr):
        co = co_ref[...]
        sg = _sigmoid(co)
        xc = co * sg
        pt = _ssd_parts(dtr_ref[...], dtb_ref[...], al_ref[...], e3_ref[...], tril3_ref[...])
        dtx, ecsx, dtex, cdx = pt["dtx"], pt["ecsx"], pt["dtex"], pt["cdx"]
        xs = xc[:, :SW]
        bm = [xc[:, 512:640].astype(BF16), xc[:, 640:768].astype(BF16)]
        cm = [xc[:, 768:896].astype(BF16), xc[:, 896:1024].astype(BF16)]
        s_in = st_ref[0]
        ds_out = ds_scr[...]
        e_t = et2_ref[...]

        zv = z_ref[...]
        sz = _sigmoid(zv)
        silu_z = zv * sz
        ypre = yp_ref[...]
        yz = ypre * silu_z
        dyn = dyn_ref[...]
        sw = sw_ref[...]
        dyz, yns = [], []
        for g in range(2):
            yg = _group_cols(yz, g)
            r = lax.rsqrt(jnp.mean(yg * yg, axis=-1, keepdims=True) + EPS)
            yn = yg * r
            dg = _group_cols(dyn, g) * _group_cols(sw, g)
            dyz.append(r * (dg - yn * jnp.mean(dg * yn, axis=-1, keepdims=True)))
            yns.append(yn)
        dyz = jnp.concatenate(dyz, axis=1)
        dsw_ref[...] += jnp.sum(dyn * jnp.concatenate(yns, axis=1), axis=0, keepdims=True)
        dy = dyz * silu_z
        dz = dyz * ypre * (sz * (1.0 + zv * (1.0 - sz)))

        xdt = xs * dtx
        xdt_b = xdt.astype(BF16)
        edy = (ecsx * dy).astype(BF16)
        xde = (xdt * dtex).astype(BF16)
        lane = _iota((BLK, 128), 1)
        lo = lane < 64
        row8 = _iota((8, 128), 0)
        dcs = jnp.zeros((BLK, 128), F32)
        col_rows = jnp.zeros((8, 128), F32)
        dxdt, bds, yoff, dbs, dcs_g, ds_new = [], [], [], [], [], []
        for g in range(2):
            s_g = _group_cols(s_in, g).astype(BF16)
            dso_g = _group_cols(ds_out, g).astype(BF16)
            cb = _dot_nt(cm[g], bm[g])
            bds.append(_dot(bm[g], dso_g))
            yoff.append(_dot(cm[g], s_g))
            dcb_g = jnp.zeros((BLK, BLK), F32)
            for jj in range(2):
                j = 2 * g + jj
                dy_c = dy[:, 128 * j:128 * (j + 1)]
                xdt_c = xdt_b[:, 128 * j:128 * (j + 1)]
                acc = jnp.zeros((BLK, 128), F32)
                for par in range(2):
                    h = 2 * j + par
                    lm = _decay(pt, h)
                    gm = cb * lm
                    dy_m = (jnp.where(lo, dy_c, 0.0) if par == 0 else jnp.where(lo, 0.0, dy_c)).astype(BF16)
                    dg_h = _dot_nt(dy_m, xdt_c)
                    w_h = dg_h * gm
                    dcs = dcs + jnp.where(lane == h, jnp.sum(w_h, axis=1, keepdims=True), 0.0)
                    col_rows = col_rows + jnp.where(row8 == h, jnp.sum(w_h, axis=0, keepdims=True), 0.0)
                    dcb_g = dcb_g + dg_h * lm
                    acc = acc + _dot_tn(gm.astype(BF16), dy_m)
                dxdt.append(acc)
            dcb_b = dcb_g.astype(BF16)
            dcs_g.append(_dot(dcb_b, bm[g]) + _dot_nt(_group_cols(edy, g), s_g))
            dbs.append(_dot_tn(dcb_b, cm[g]) + _dot_nt(_group_cols(xde, g), dso_g))
            ds_new.append(_dot_tn(cm[g], _group_cols(edy, g)))
        bds = jnp.concatenate(bds, axis=1)
        yoff = jnp.concatenate(yoff, axis=1) * ecsx
        dxdt = jnp.concatenate(dxdt, axis=1) + dtex * bds
        ds_scr[...] = cdx * ds_out + jnp.concatenate(ds_new, axis=1)

        t_m = _head_sums(dtex * xdt * bds, e_t)
        colsum_t = jnp.concatenate([col_rows, jnp.zeros((BLK - 8, 128), F32)], axis=0).T
        cd = jnp.exp(pt["cs"][BLK - 1:BLK, :])
        sds = jnp.sum(s_in * ds_out, axis=0, keepdims=True)
        last_row = jnp.sum(t_m, axis=0, keepdims=True) + cd * _head_sums(jnp.broadcast_to(sds, (8, SW)), e_t)[0:1]
        dcs = dcs - colsum_t + _head_sums(dy * yoff, e_t) - t_m
        dcs = dcs + jnp.where(_iota((BLK, 128), 0) == BLK - 1, last_row, 0.0)
        da = _run_sum(triu3_ref[...], dcs)
        dt = pt["dt"]
        ddt = da * pt["a_neg"] + _head_sums(dxdt * xs, e_t)
        dav_ref[...] += jnp.sum(da * dt, axis=0, keepdims=True)
        ddtr = ddt * _sigmoid(pt["xx"])
        ddtb_ref[...] += jnp.sum(ddtr, axis=0, keepdims=True)
        dxs = dxdt * dtx + dy * dk_ref[...]
        dskx_scr[...] += jnp.sum(dy * xs, axis=0, keepdims=True)
        dxc = jnp.concatenate([dxs, dbs[0], dbs[1], dcs_g[0], dcs_g[1]], axis=1)
        dco = dxc * (sg * (1.0 + co * (1.0 - sg)))

        dcb_ref[...] += jnp.sum(dco, axis=0, keepdims=True)
        u = u_ref[...]
        head = dco_scr[...]
        du = jnp.zeros_like(dco)
        for j in range(CONVK):
            up_j = dco if j == 0 else _shift_up(dco, head, j)
            dcw_ref[3 - j:4 - j, :] += jnp.sum(up_j * u, axis=0, keepdims=True)
            du = du + cw_ref[3 - j:4 - j, :] * up_j
        dco_scr[...] = dco[0:8]
        out_ref[:, 0:512] = dz.astype(BF16)
        out_ref[:, 512:1536] = du.astype(BF16)
        out_ref[:, 1536:1664] = ddtr.astype(BF16)

    e3, et2, tril3, triu3 = mats
    RB = CPS * BLK
    rev = lambda i: nsteps - 1 - i
    full = lambda a: pl.BlockSpec(a.shape, lambda i: (0,) * a.ndim)
    acc = lambda r, c: pl.BlockSpec((r, c), lambda i: (0, 0))
    return pl.pallas_call(
        body, name="ssd_bwd", grid=(nsteps,),
        in_specs=[pl.BlockSpec((RB, CONVC), lambda i: (rev(i), 0)), pl.BlockSpec((RB, CONVC), lambda i: (rev(i), 0)),
                  pl.BlockSpec((RB, SW), lambda i: (rev(i), 0)), pl.BlockSpec((RB, 128), lambda i: (rev(i), 0)),
                  pl.BlockSpec((RB, SW), lambda i: (rev(i), 0)), pl.BlockSpec((CPS, NST, SW), lambda i: (rev(i), 0, 0)),
                  pl.BlockSpec((RB, SW), lambda i: (rev(i), 1)),
                  full(conv_w), full(dtb), full(alog), full(dskx), full(ssm_w),
                  full(e3), full(et2), full(tril3), full(triu3), DEP_SPEC],
        out_specs=[pl.BlockSpec((RB, 1664), lambda i: (rev(i), 0)),
                   acc(CONVK, CONVC), acc(1, CONVC), acc(1, SW), acc(1, 128), acc(1, 128), acc(1, 128)],
        out_shape=[jax.ShapeDtypeStruct((T, 1664), BF16),
                   jax.ShapeDtypeStruct((CONVK, CONVC), F32), jax.ShapeDtypeStruct((1, CONVC), F32),
                   jax.ShapeDtypeStruct((1, SW), F32), jax.ShapeDtypeStruct((1, 128), F32),
                   jax.ShapeDtypeStruct((1, 128), F32), jax.ShapeDtypeStruct((1, 128), F32)],
        scratch_shapes=[pltpu.VMEM((NST, SW), F32), pltpu.VMEM((8, CONVC), F32), pltpu.VMEM((1, SW), F32)],
        compiler_params=_cp("arbitrary"),
    )(xbc, co_all, z, dtr, ypre, states, dmix, conv_w, dtb, alog, dskx, ssm_w, e3, et2, tril3, triu3, dep)


def _mix_ffn(x, attn, ynorm, tgt, mod6, norm2_w, final_w, w_out, w_gu, w_gu_own, s_arr, w_dn, tm):
    T = x.shape[0]
    nt = T // tm

    def body(x_ref, a_ref, y_ref, t_ref, mod_ref, n2_ref, fw_ref, wo_hbm, wgu_hbm, own_hbm, s_ref, wdn_hbm,
             sq_ref, dmix_ref, dx1_ref, h2_ref, act_ref, df_ref, dgu_ref, do_ref, sm_ref,
             wo, wgu, wdn, sems):
        i = pl.program_id(0)

        @pl.when(i == 0)
        def _():
            cps = [pltpu.make_async_copy(s, d, sems.at[k]) for k, (s, d) in
                   enumerate(((wo_hbm, wo), (wgu_hbm, wgu), (wdn_hbm, wdn)))]
            for c in cps:
                c.start()
            for c in cps:
                c.wait()
            own = pltpu.make_async_copy(
                own_hbm, wgu.at[:, pl.ds(pl.multiple_of(s_ref[0] * GU_SH, 128), GU_SH)], sems.at[3])
            own.start()
            own.wait()
            sq_ref[...] = jnp.zeros_like(sq_ref)
            sm_ref[...] = jnp.zeros_like(sm_ref)

        gate1, shift2, scale2, gate2 = mod_ref[2:3, :], mod_ref[3:4, :], mod_ref[4:5, :], mod_ref[5:6, :]
        n2w, fw = n2_ref[...], fw_ref[...]
        o = _dot(a_ref[...], wo[0:AW, :]) + _dot(y_ref[...], wo[AW:D, :])
        x1 = x_ref[...] + gate1 * o
        r2 = lax.rsqrt(jnp.mean(x1 * x1, axis=-1, keepdims=True) + EPS)
        xh2 = x1 * r2
        n2 = xh2 * n2w
        h2b = (n2 * (1.0 + scale2) + shift2).astype(BF16)
        h2_ref[...] = h2b
        f = jnp.zeros((tm, D), F32)
        saved = []
        for a, b in FF_SPLITS:
            gp = _dot(h2b, wgu[:, a:b])
            upj = _dot(h2b, wgu[:, DFF + a:DFF + b])
            sg = _sigmoid(gp)
            sl = gp * sg
            actb = (sl * upj).astype(BF16)
            act_ref[:, a:b] = actb
            f = f + _dot(actb, wdn[a:b, :])
            saved.append((gp, upj, sg, sl))
        x2 = x1 + gate2 * f
        r3 = lax.rsqrt(jnp.mean(x2 * x2, axis=-1, keepdims=True) + EPS)
        xh3 = x2 * r3
        err = xh3 * fw - t_ref[...]
        sq_ref[...] += jnp.sum(err * err, axis=0, keepdims=True)
        dy = err * (1.0 / D)
        dfw = jnp.sum(dy * xh3, axis=0, keepdims=True)
        dxh3 = dy * fw
        dx2 = r3 * (dxh3 - xh3 * jnp.mean(dxh3 * xh3, axis=-1, keepdims=True))
        dgate2 = jnp.sum(dx2 * f, axis=0, keepdims=True)
        dfb = (dx2 * gate2).astype(BF16)
        df_ref[...] = dfb
        dh2 = jnp.zeros((tm, D), F32)
        for (a, b), (gp, upj, sg, sl) in zip(FF_SPLITS, saved):
            dact = _dot_nt(dfb, wdn[a:b, :])
            dg = (dact * upj * (sg * (1.0 + gp * (1.0 - sg)))).astype(BF16)
            du = (dact * sl).astype(BF16)
            dgu_ref[:, a:b] = dg
            dgu_ref[:, DFF + a:DFF + b] = du
            dh2 = dh2 + _dot_nt(dg, wgu[:, a:b]) + _dot_nt(du, wgu[:, DFF + a:DFF + b])
        dshift2 = jnp.sum(dh2, axis=0, keepdims=True)
        dscale2 = jnp.sum(dh2 * n2, axis=0, keepdims=True)
        dn2 = dh2 * (1.0 + scale2)
        dn2w = jnp.sum(dn2 * xh2, axis=0, keepdims=True)
        dxh2 = dn2 * n2w
        dx1 = dx2 + r2 * (dxh2 - xh2 * jnp.mean(dxh2 * xh2, axis=-1, keepdims=True))
        dx1_ref[...] = dx1
        dgate1 = jnp.sum(dx1 * o, axis=0, keepdims=True)
        dob = (dx1 * gate1).astype(BF16)
        do_ref[...] = dob
        dmix_ref[...] = _dot_nt(dob, wo[...])
        sm_ref[...] += jnp.concatenate(
            [dfw, dn2w, dshift2, dscale2, dgate2, dgate1, jnp.zeros((2, D), F32)], axis=0)

    row = lambda w: pl.BlockSpec((tm, w), lambda i: (i, 0))
    full = lambda a: pl.BlockSpec(a.shape, lambda i: (0,) * a.ndim)
    anyspec = pl.BlockSpec(memory_space=pl.ANY)
    return pl.pallas_call(
        body, name="mix_ffn", grid=(nt,),
        in_specs=[row(D), row(AW), row(SW), row(D), full(mod6), full(norm2_w), full(final_w), anyspec, anyspec, anyspec,
                  pl.BlockSpec(memory_space=pltpu.SMEM), anyspec],
        out_specs=[pl.BlockSpec((1, D), lambda i: (0, 0)), row(D), row(D), row(D),
                   row(DFF), row(D), row(2 * DFF), row(D), pl.BlockSpec((8, D), lambda i: (0, 0))],
        out_shape=[jax.ShapeDtypeStruct((1, D), F32), jax.ShapeDtypeStruct((T, D), F32), jax.ShapeDtypeStruct((T, D), F32),
                   jax.ShapeDtypeStruct((T, D), BF16), jax.ShapeDtypeStruct((T, DFF), BF16),
                   jax.ShapeDtypeStruct((T, D), BF16), jax.ShapeDtypeStruct((T, 2 * DFF), BF16),
                   jax.ShapeDtypeStruct((T, D), BF16), jax.ShapeDtypeStruct((8, D), F32)],
        scratch_shapes=[pltpu.VMEM((D, D), BF16), pltpu.VMEM((D, 2 * DFF), BF16), pltpu.VMEM((DFF, D), BF16),
                        pltpu.SemaphoreType.DMA((4,))],
        compiler_params=_cp("arbitrary"),
    )(x, attn, ynorm, tgt, mod6, norm2_w, final_w, w_out, w_gu, w_gu_own, s_arr, w_dn)


def _in_proj_bwd(x, dx1, dqkv, dzxd, mod6, norm1_w, w_pad, tm, dep):
    T = x.shape[0]

    def body(x_ref, dx1_ref, dq_ref, dz_ref, mod_ref, nw_ref, w_hbm, dep_ref, gx_ref, sm_ref, w_vmem, sem):
        _load_resident(w_hbm, w_vmem, sem)

        @pl.when(pl.program_id(0) == 0)
        def _():
            sm_ref[...] = jnp.zeros_like(sm_ref)

        nw = nw_ref[...]
        scale1 = mod_ref[1:2, :]
        sums = jnp.zeros((8, D), F32)
        for rows in (slice(0, tm // 2), slice(tm // 2, tm)):
            dh = _dot_nt(dq_ref[rows, :], w_vmem[:, 0:768]) + _dot_nt(dz_ref[rows, :], w_vmem[:, 768:IN_PAD])
            xv = x_ref[rows, :]
            r = lax.rsqrt(jnp.mean(xv * xv, axis=-1, keepdims=True) + EPS)
            xh = xv * r
            n1 = xh * nw
            dshift = jnp.sum(dh, axis=0, keepdims=True)
            dscale = jnp.sum(dh * n1, axis=0, keepdims=True)
            dn = dh * (1.0 + scale1)
            dnw = jnp.sum(dn * xh, axis=0, keepdims=True)
            dxh = dn * nw
            gx_ref[rows, :] = dx1_ref[rows, :] + r * (dxh - xh * jnp.mean(dxh * xh, axis=-1, keepdims=True))
            sums = sums + jnp.concatenate([dnw, dshift, dscale, jnp.zeros((5, D), F32)], axis=0)
        sm_ref[...] += sums

    row = lambda w: pl.BlockSpec((tm, w), lambda i: (i, 0))
    full = lambda a: pl.BlockSpec(a.shape, lambda i: (0,) * a.ndim)
    return pl.pallas_call(
        body, name="in_proj_bwd", grid=(T // tm,),
        in_specs=[row(D), row(D), row(768), row(1664), full(mod6), full(norm1_w), pl.BlockSpec(memory_space=pl.ANY),
                  DEP_SPEC],
        out_specs=[row(D), pl.BlockSpec((8, D), lambda i: (0, 0))],
        out_shape=[jax.ShapeDtypeStruct((T, D), F32), jax.ShapeDtypeStruct((8, D), F32)],
        scratch_shapes=[pltpu.VMEM((D, IN_PAD), BF16), pltpu.SemaphoreType.DMA],
        compiler_params=_cp("arbitrary"),
    )(x, dx1, dqkv, dzxd, mod6, norm1_w, w_pad, dep)


def _tn_matmul(a, b, K, N, tt, name, dep):
    T = a.shape[0]
    ja, jb = a.shape[1] // K, b.shape[1] // N
    J = max(ja, jb)

    def body(a_ref, b_ref, dep_ref, o_ref):
        t = pl.program_id(1)
        prod = _dot_tn(a_ref[...], b_ref[...])

        @pl.when(t == 0)
        def _():
            o_ref[0] = prod

        @pl.when(t > 0)
        def _():
            o_ref[0] += prod

    return pl.pallas_call(
        body, name=name, grid=(J, T // tt),
        in_specs=[pl.BlockSpec((tt, K), lambda j, t: (t, j if ja > 1 else 0)),
                  pl.BlockSpec((tt, N), lambda j, t: (t, j if jb > 1 else 0)),
                  pl.BlockSpec((8, 128), lambda j, t: (0, 0))],
        out_specs=pl.BlockSpec((1, K, N), lambda j, t: (j, 0, 0)),
        out_shape=jax.ShapeDtypeStruct((J, K, N), F32),
        compiler_params=_cp("parallel", "arbitrary"),
    )(a, b, dep)


def _adam_math(w, g, m, v):
    m = B1 * m + (1.0 - B1) * g
    v = B2 * v + (1.0 - B2) * (g * g)
    m_hat = m / (1.0 - B1 ** STEP)
    v_hat = v / (1.0 - B2 ** STEP)
    delta = -LR * (m_hat / (jnp.sqrt(v_hat) + AEPS) + WD * w)
    return delta, m, v


def _adam_2d(w, mine, land, m, v, c_arr, rb, name):
    R, C = w.shape
    nbh = R // 2 // rb

    def body(c_ref, w_ref, mine_ref, land_ref, m_ref, v_ref, go_ref, d_ref, mo_ref, vo_ref):
        g = jnp.where(pl.program_id(0) // nbh == c_ref[0], mine_ref[...], land_ref[...])
        d, mn, vn = _adam_math(w_ref[...], g, m_ref[...], v_ref[...])
        go_ref[...] = g
        d_ref[...] = d
        mo_ref[...] = mn
        vo_ref[...] = vn

    spec = pl.BlockSpec((rb, C), lambda i, c_ref: (i, 0))
    mine_spec = pl.BlockSpec((rb, C), lambda i, c_ref: (jnp.clip(i - c_ref[0] * nbh, 0, nbh - 1), 0))
    return pl.pallas_call(
        body, name=name,
        grid_spec=pltpu.PrefetchScalarGridSpec(
            num_scalar_prefetch=1, grid=(R // rb,), in_specs=[spec, mine_spec, spec, spec, spec], out_specs=[spec] * 4),
        out_shape=[jax.ShapeDtypeStruct((R, C), F32)] * 4, compiler_params=_cp("parallel"),
    )(c_arr, w, mine, land, m, v)


def _adam_w_in(w3, mine, land, m3, v3, c_arr):
    n = w3.shape[0]

    def body(c_ref, w_hbm, mine_ref, land_ref, m_hbm, v_hbm, g_hbm, d_hbm, mo_hbm, vo_hbm, bufs, sems):
        ins = [pltpu.make_async_copy(src.at[:, 0], bufs.at[k], sems.at[k]) for k, src in enumerate((w_hbm, m_hbm, v_hbm))]
        for cp in ins:
            cp.start()
        half = D // 2
        top = jnp.where(c_ref[0] == 0, mine_ref[...], land_ref[0:half, :])
        bot = jnp.where(c_ref[0] == 1, mine_ref[...], land_ref[half:D, :])
        g = jnp.concatenate([top, bot], axis=0)
        eye = (_iota((D, D), 0) == _iota((D, D), 1)).astype(BF16)
        g_t = jnp.zeros((n, D), F32)
        r = g
        for i in range(3):
            p = r.astype(BF16)
            g_t = g_t + _dot_tn(p, eye)
            if i < 2:
                r = r - p.astype(F32)
        for cp in ins:
            cp.wait()
        d, mn, vn = _adam_math(bufs[0], g_t, bufs[1], bufs[2])
        for k, val in enumerate((g_t, d, mn, vn)):
            bufs[3 + k] = val
        outs = [pltpu.make_async_copy(bufs.at[3 + k], dst.at[:, 0], sems.at[3 + k])
                for k, dst in enumerate((g_hbm, d_hbm, mo_hbm, vo_hbm))]
        for cp in outs:
            cp.start()
        for cp in outs:
            cp.wait()

    anyspec = pl.BlockSpec(memory_space=pl.ANY)
    vm = pl.BlockSpec(memory_space=pltpu.VMEM)
    return pl.pallas_call(
        body, name="adam_w_in",
        in_specs=[pl.BlockSpec(memory_space=pltpu.SMEM), anyspec, vm, vm, anyspec, anyspec], out_specs=[anyspec] * 4,
        out_shape=[jax.ShapeDtypeStruct(w3.shape, F32)] * 4,
        scratch_shapes=[pltpu.VMEM((7, n, D), F32), pltpu.SemaphoreType.DMA((7,))],
        compiler_params=pltpu.CompilerParams(vmem_limit_bytes=VMEM_LIMIT),
    )(c_arr, w3, mine, land, m3, v3)


def _adam_w_ada(gat, allv, s_arr, w, m, v, rb):
    R, C = w.shape

    def body(s_ref, c_ref, dm_ref, w_ref, m_ref, v_ref, g_ref, d_ref, mo_ref, vo_ref):
        cm = _rows_select(c_ref, rb)
        g = lax.dot_general(cm * _sigmoid(cm), _rows_select(dm_ref, C), (((0,), (0,)), ((), ())), precision=HI,
                            preferred_element_type=F32)
        d, mn, vn = _adam_math(w_ref[...], g, m_ref[...], v_ref[...])
        g_ref[...] = g
        d_ref[...] = d
        mo_ref[...] = mn
        vo_ref[...] = vn

    spec = pl.BlockSpec((rb, C), lambda i, s_ref: (i, 0))
    return pl.pallas_call(
        body, name="adam_w_ada",
        grid_spec=pltpu.PrefetchScalarGridSpec(
            num_scalar_prefetch=1, grid=(R // rb,),
            in_specs=[pl.BlockSpec((8, 1, rb), lambda i, s_ref: (0, 0, i)),
                      pl.BlockSpec((8, 1, C), lambda i, s_ref: (0, 0, s_ref[0])), spec, spec, spec],
            out_specs=[spec] * 4),
        out_shape=[jax.ShapeDtypeStruct((R, C), F32)] * 4, compiler_params=_cp("parallel"),
    )(s_arr, gat, allv, w, m, v)


def _adam_small(tot, segs, ws, ms, vs):
    k = len(ws)
    extra = [sg for sg in segs if not isinstance(sg, tuple)]
    ne = len(extra)

    def body(*refs):
        tot_ref, g_x = refs[0], list(refs[1:1 + ne])
        w, m, v = [refs[1 + ne + j * k:1 + ne + (j + 1) * k] for j in range(3)]
        g_o, d_o, m_o, v_o = [refs[1 + ne + (3 + j) * k:1 + ne + (4 + j) * k] for j in range(4)]
        for i in range(k):
            gi = tot_ref[:, segs[i][0]:segs[i][0] + segs[i][1]] if isinstance(segs[i], tuple) else g_x.pop(0)[...]
            d, mn, vn = _adam_math(w[i][...], gi, m[i][...], v[i][...])
            g_o[i][...] = gi
            d_o[i][...] = d
            m_o[i][...] = mn
            v_o[i][...] = vn

    shapes = [jax.ShapeDtypeStruct(w.shape, F32) for w in ws]
    vm = pl.BlockSpec(memory_space=pltpu.VMEM)
    outs = pl.pallas_call(
        body, name="adam_small", in_specs=[vm] * (1 + ne + 3 * k), out_specs=[vm] * (4 * k), out_shape=shapes * 4,
    )(tot, *extra, *ws, *ms, *vs)
    return outs[0:k], outs[k:2 * k], outs[2 * k:3 * k], outs[3 * k:4 * k]


def _pos():
    return lax.axis_index("x"), lax.axis_index("y"), lax.axis_index("c")


def _flip(v, bit):
    return 1 - v if bit else v


def _peer(k):
    x, y, c = _pos()
    return (_flip(x, (k >> 2) & 1), _flip(y, (k >> 1) & 1), _flip(c, k & 1))


def _logical(p):
    return 4 * p[0] + 2 * p[1] + p[2]


def _gather8(src_ref, dst_ref, send_sems, recv_sems):
    me = _logical(_pos())
    dst_ref[pl.ds(me, 1)] = src_ref[...][None]
    copies = []
    for k in range(1, 8):
        cp = pltpu.make_async_remote_copy(src_ref, dst_ref.at[me], send_sems.at[k - 1], recv_sems.at[k - 1],
                                          device_id=_peer(k), device_id_type=MESH)
        cp.start()
        copies.append(cp)
    for k in range(1, 8):
        pltpu.make_async_remote_copy(src_ref, dst_ref.at[_logical(_peer(k))], send_sems.at[k - 1], recv_sems.at[k - 1],
                                     device_id=_peer(k), device_id_type=MESH).wait_recv()
    for cp in copies:
        cp.wait_send()


def _rows_select(ref3, width):
    row = _iota((8, width), 0)
    out = jnp.zeros((8, width), F32)
    for i in range(8):
        out = jnp.where(row == i, ref3[i][:, 0:width], out)
    return out


def _mod_exchange(payload, w_ada_s, b_ada4):
    n_sh = w_ada_s.shape[1]

    def body(pay_ref, w_ref, b_ref, gat_ref, mod_ref, token, p3, sa, ra, sb, rb):
        token[...] = jnp.zeros_like(token)
        x, y, c = _pos()
        me = _logical((x, y, c))
        my_s = 2 * x + y
        _gather8(pay_ref, gat_ref, sa, ra)
        cmat = _rows_select(gat_ref, D)
        prod = _dot_hi(cmat * _sigmoid(cmat), w_ref[...])
        for b in range(8):
            p3[b] = prod[b:b + 1, :]
        mod_ref[pl.ds(my_s, 1)] = p3[pl.ds(me, 1)] + b_ref[pl.ds(my_s, 1)]
        ks = (2, 4, 6)
        copies = []
        for i, k in enumerate(ks):
            pr = _peer(k)
            cp = pltpu.make_async_remote_copy(p3.at[_logical(pr)], mod_ref.at[my_s], sb.at[i], rb.at[i],
                                              device_id=pr, device_id_type=MESH)
            cp.start()
            copies.append(cp)
        for i, k in enumerate(ks):
            pr = _peer(k)
            s_src = 2 * pr[0] + pr[1]
            pltpu.make_async_remote_copy(p3.at[0], mod_ref.at[s_src], sb.at[i], rb.at[i],
                                         device_id=pr, device_id_type=MESH).wait_recv()
            mod_ref[pl.ds(s_src, 1)] = mod_ref[pl.ds(s_src, 1)] + b_ref[pl.ds(s_src, 1)]
        for cp in copies:
            cp.wait_send()

    vm = pl.BlockSpec(memory_space=pltpu.VMEM)
    return pl.pallas_call(
        body, name="mod_exchange", in_specs=[vm, vm, vm], out_specs=[vm, vm, vm],
        out_shape=[jax.ShapeDtypeStruct((8, 1, payload.shape[1]), F32), jax.ShapeDtypeStruct((4, 1, n_sh), F32),
                   jax.ShapeDtypeStruct((8, 128), F32)],
        scratch_shapes=[pltpu.VMEM((8, 1, n_sh), F32), pltpu.SemaphoreType.DMA((7,)), pltpu.SemaphoreType.DMA((7,)),
                        pltpu.SemaphoreType.DMA((3,)), pltpu.SemaphoreType.DMA((3,))],
        compiler_params=pltpu.CompilerParams(vmem_limit_bytes=VMEM_LIMIT),
    )(payload, w_ada_s, b_ada4)


def _chips():
    x, y, _ = _pos()
    out = []
    for k in (1, 2, 3):
        px, py = _flip(x, (k >> 1) & 1), _flip(y, k & 1)
        out.append((px, py, 2 * px + py))
    return out


def _half_rows(ref, which):
    half = ref.shape[-2] // 2
    return pl.ds(pl.multiple_of(which * half, 8), half)


def _small_reduce(vec):
    n = vec.shape[1]

    def body(v_ref, tot_ref, gat_ref, sa, ra):
        _gather8(v_ref, gat_ref, sa, ra)
        tot = gat_ref[0]
        for i in range(1, 8):
            tot = tot + gat_ref[i]
        tot_ref[...] = tot

    vm = pl.BlockSpec(memory_space=pltpu.VMEM)
    return pl.pallas_call(
        body, name="small_reduce", in_specs=[vm], out_specs=[vm, vm],
        out_shape=[jax.ShapeDtypeStruct((1, n), F32), jax.ShapeDtypeStruct((8, 1, n), F32)],
        scratch_shapes=[pltpu.SemaphoreType.DMA((7,)), pltpu.SemaphoreType.DMA((7,))],
    )(vec)


def _add_half(g, sib, c_arr, rb, name):
    _, R, C = g.shape
    half = R // 2
    nb = half // rb

    def body(c_ref, g_ref, s_ref, o_ref):
        o_ref[...] = (g_ref[...] + s_ref[...]).astype(BF16)

    return pl.pallas_call(
        body, name=name,
        grid_spec=pltpu.PrefetchScalarGridSpec(
            num_scalar_prefetch=1, grid=(4, nb),
            in_specs=[pl.BlockSpec((1, rb, C), lambda s, i, c_ref: (s, c_ref[0] * nb + i, 0)),
                      pl.BlockSpec((1, rb, C), lambda s, i, c_ref: (s, i, 0))],
            out_specs=pl.BlockSpec((1, rb, C), lambda s, i, c_ref: (s, i, 0))),
        out_shape=jax.ShapeDtypeStruct((4, half, C), BF16),
        compiler_params=_cp("parallel", "parallel"),
    )(c_arr, g, sib)


def _sum4(parts, land, s_arr, rb, name):
    _, H, C = land.shape

    def body(s_ref, own_ref, r_ref, o_ref):
        own = own_ref[0].astype(F32)
        tot = jnp.zeros((rb, C), F32)
        for j in range(4):
            tot = tot + jnp.where(s_ref[0] == j, own, r_ref[j].astype(F32))
        o_ref[...] = tot

    return pl.pallas_call(
        body, name=name,
        grid_spec=pltpu.PrefetchScalarGridSpec(
            num_scalar_prefetch=1, grid=(H // rb,),
            in_specs=[pl.BlockSpec((1, rb, C), lambda i, s_ref: (s_ref[0], i, 0)),
                      pl.BlockSpec((4, rb, C), lambda i, s_ref: (0, i, 0))],
            out_specs=pl.BlockSpec((rb, C), lambda i, s_ref: (i, 0))),
        out_shape=jax.ShapeDtypeStruct((H, C), F32), compiler_params=_cp("parallel"),
    )(s_arr, parts, land)


HBM_SPEC = pl.BlockSpec(memory_space=pltpu.HBM)
SEM_SPEC = pl.BlockSpec(memory_space=pltpu.SEMAPHORE)
EFFECT = pltpu.SideEffectType.DATAFLOW_SIDE_EFFECTING


def _split_start(name, bufs, n_sem, plan, dep):
    nb = len(bufs)

    def body(*refs):
        ins, send, recv, token = refs[:nb], refs[nb + 1], refs[nb + 2], refs[-1]
        for i, (src, dst, dev, _) in enumerate(plan(ins)):
            pltpu.make_async_remote_copy(src, dst, send.at[i], recv.at[i], device_id=dev, device_id_type=MESH).start()
        token[...] = jnp.zeros_like(token)

    outs = pl.pallas_call(
        body, name=name,
        out_shape=(pltpu.SemaphoreType.DMA((n_sem,)), pltpu.SemaphoreType.DMA((n_sem,)),
                   *[pltpu.HBM(b.shape, b.dtype) for b in bufs], jax.ShapeDtypeStruct((8, 128), F32)),
        in_specs=[HBM_SPEC] * nb + [pl.BlockSpec(memory_space=pl.ANY)],
        out_specs=(SEM_SPEC, SEM_SPEC, *([HBM_SPEC] * nb), pl.BlockSpec(memory_space=pltpu.VMEM)),
        input_output_aliases={i: 2 + i for i in range(nb)},
        compiler_params=pltpu.CompilerParams(has_side_effects=EFFECT),
    )(*[pltpu.with_memory_space_constraint(b, pltpu.HBM) for b in bufs], dep)
    return outs[0], outs[1], list(outs[2:2 + nb]), outs[-1]


def _split_wait(name, send, recv, bufs, after, plan):
    nb = len(bufs)

    def body(*refs):
        ins, send_s, recv_s = refs[:nb], refs[nb], refs[nb + 1]
        for i, (src, dst, dev, mine) in enumerate(plan(ins)):
            pltpu.make_async_remote_copy(src, dst, send_s.at[i], recv_s.at[i], device_id=dev,
                                         device_id_type=MESH).wait_send()
            pltpu.make_async_remote_copy(src, mine, send_s.at[i], recv_s.at[i], device_id=dev,
                                         device_id_type=MESH).wait_recv()

    outs = pl.pallas_call(
        body, name=name, out_shape=[pltpu.HBM(b.shape, b.dtype) for b in bufs],
        in_specs=[HBM_SPEC] * nb + [SEM_SPEC, SEM_SPEC, pl.BlockSpec(memory_space=pl.ANY)],
        out_specs=[HBM_SPEC] * nb, input_output_aliases={i: i for i in range(nb)},
        compiler_params=pltpu.CompilerParams(has_side_effects=EFFECT),
    )(*bufs, send, recv, after)
    return list(outs)


def _copies_now(name, bufs, n_sem, plan):
    nb = len(bufs)

    def body(*refs):
        ins, token, send, recv = refs[:nb], refs[2 * nb], refs[-2], refs[-1]
        token[...] = jnp.zeros_like(token)
        todo = plan(ins)
        for i, (src, dst, dev, _) in enumerate(todo):
            pltpu.make_async_remote_copy(src, dst, send.at[i], recv.at[i], device_id=dev, device_id_type=MESH).start()
        for i, (src, dst, dev, mine) in enumerate(todo):
            pltpu.make_async_remote_copy(src, mine, send.at[i], recv.at[i], device_id=dev, device_id_type=MESH).wait_recv()
        for i, (src, dst, dev, _) in enumerate(todo):
            pltpu.make_async_remote_copy(src, dst, send.at[i], recv.at[i], device_id=dev, device_id_type=MESH).wait_send()

    outs = pl.pallas_call(
        body, name=name,
        out_shape=[pltpu.HBM(b.shape, b.dtype) for b in bufs] + [jax.ShapeDtypeStruct((8, 128), F32)],
        in_specs=[HBM_SPEC] * nb, out_specs=[HBM_SPEC] * nb + [pl.BlockSpec(memory_space=pltpu.VMEM)],
        input_output_aliases={i: i for i in range(nb)},
        scratch_shapes=[pltpu.SemaphoreType.DMA((n_sem,)), pltpu.SemaphoreType.DMA((n_sem,))],
    )(*[pltpu.with_memory_space_constraint(b, pltpu.HBM) for b in bufs])
    return list(outs[:nb]), outs[nb]


def _slot(land, s, rows, cols):
    if cols is None:
        return land.at[s, rows]
    return land.at[rows, pl.ds(pl.multiple_of(s * cols, 128), cols)]


def _plan_gather_ici(cols):
    nw = len(cols)

    def plan(refs):
        x, y, c = _pos()
        my_s = 2 * x + y
        out = []
        for w in range(nw):
            mine = _half_rows(refs[w], c)
            for px, py, ps in _chips():
                out.append((refs[w].at[mine], _slot(refs[nw + w], my_s, mine, cols[w]), (px, py, c),
                            _slot(refs[nw + w], ps, mine, cols[w])))
        return out
    return plan


def _plan_gather_fwd(cols, rows):
    def plan(refs):
        x, y, c = _pos()
        out = []
        for w in range(len(cols)):
            half = rows[w] // 2
            mine = pl.ds(pl.multiple_of(c * half, 8), half)
            other = pl.ds(pl.multiple_of((1 - c) * half, 8), half)
            for px, py, ps in _chips():
                got = _slot(refs[w], ps, mine, cols[w])
                out.append((got, got, (x, y, 1 - c), _slot(refs[w], ps, other, cols[w])))
        return out
    return plan


def _plan_swap(nw):
    def plan(refs):
        x, y, c = _pos()
        return [(refs[w].at[:, _half_rows(refs[w], 1 - c)], refs[nw + w], (x, y, 1 - c), refs[nw + w])
                for w in range(nw)]
    return plan


def _plan_scatter(nw):
    def plan(refs):
        x, y, c = _pos()
        my_s = 2 * x + y
        out = []
        for w in range(nw):
            for px, py, ps in _chips():
                out.append((refs[w].at[ps], refs[nw + w].at[my_s], (px, py, c), refs[nw + w].at[ps]))
        return out
    return plan


def _plan_join(nw):
    def plan(refs):
        x, y, c = _pos()
        out = []
        for w in range(nw):
            land = refs[nw + w]
            out.append((refs[w], land.at[_half_rows(land, c)], (x, y, 1 - c), land.at[_half_rows(land, 1 - c)]))
        return out
    return plan


def _hbm_empty(shape, dtype):
    return pltpu.with_memory_space_constraint(lax.empty(shape, dtype), pltpu.HBM)


def _put_slot(land, own, slot):
    return lax.dynamic_update_slice(land, own[None], (slot,) + (0,) * own.ndim)


def _pad_lanes(a, n):
    return jnp.pad(a, ((0, 0), (0, n - a.shape[1])))


def kernel(x, c, positions, w_ada, b_ada, norm1_w, w_in, conv_w, conv_b, dt_bias, a_log, d_skip, attn_sinks, ssm_norm_w, w_out, norm2_w, w_gate_up, w_down, final_norm_w, loss_target, m_w_ada, m_b_ada, m_norm1_w, m_w_in, m_conv_w, m_conv_b, m_dt_bias, m_a_log, m_d_skip, m_attn_sinks, m_ssm_norm_w, m_w_out, m_norm2_w, m_w_gate_up, m_w_down, m_final_norm_w, v_w_ada, v_b_ada, v_norm1_w, v_w_in, v_conv_w, v_conv_b, v_dt_bias, v_a_log, v_d_skip, v_attn_sinks, v_ssm_norm_w, v_w_out, v_norm2_w, v_w_gate_up, v_w_down, v_final_norm_w):
    T = x.shape[1]
    tm = min(256, T)
    xi, yi, ci = lax.axis_index("x"), lax.axis_index("y"), lax.axis_index("c")
    my_s = 2 * xi + yi
    xs = x[0]
    tgt = loss_target[0]

    payload = jnp.concatenate([c, conv_w[0].reshape(1, CONVK * 256)], axis=1)
    gat, mod4, tok = _mod_exchange(payload, w_ada[0], b_ada.reshape(4, 1, 1536))
    mod6 = mod4.reshape(6, D)
    cw_dev = gat[:, 0, D:].reshape(4, 2, CONVK, 256)[:, 0]
    conv_full = cw_dev.transpose(1, 0, 2).reshape(CONVK, CONVC)

    w_in_b = w_in[0].astype(BF16)
    s_i, r_i, bufs, tok = _split_start("wgather_in_ici_start", [w_in_b, _hbm_empty((4,) + w_in_b.shape, BF16)], 3,
                                       _plan_gather_ici([None]), tok)
    inv_freq = (10000.0 ** (-jnp.arange(32, dtype=F32) / 32))
    cos, sin_s = _rope_tables(positions, inv_freq.reshape(32, 1), min(512, T), tok)
    bufs = _split_wait("wgather_in_ici_wait", s_i, r_i, bufs, cos, _plan_gather_ici([None]))
    bufs, tok = _copies_now("wgather_in_fwd", bufs[1:], 3, _plan_gather_fwd([None], [D]))
    g_in = _put_slot(bufs[0], w_in_b, my_s)
    w_pad = jnp.concatenate([g_in[0], g_in[1], g_in[2], g_in[3], jnp.zeros((D, IN_PAD - IN_PROJ), BF16)], axis=1)

    late = [w_out[0].astype(BF16), w_gate_up[0].astype(BF16), w_down[0].astype(BF16)]
    lands = [_hbm_empty((4, D // 4, D), BF16), _hbm_empty((D, 2 * DFF), BF16), _hbm_empty((4, DFF // 4, D), BF16)]
    cols3, rows3 = [None, GU_SH, None], [D // 4, D, DFF // 4]
    s_a, r_a, bufs, tok = _split_start("wgather_ici_start", late + lands, 9, _plan_gather_ici(cols3), tok)

    qkv, z, xbc, dtr, h1b = _in_proj_fwd(xs, cos, sin_s, mod6, norm1_w, w_pad, min(512, T), tok)
    sinks = attn_sinks
    attn, lse = _attn_fwd(qkv, sinks)
    bufs = _split_wait("wgather_ici_wait", s_a, r_a, bufs, attn, _plan_gather_ici(cols3))
    s_b, r_b, lands, tok = _split_start("wgather_fwd_start", bufs[3:], 9, _plan_gather_fwd(cols3, rows3), attn)
    dtb = _pad_lanes(dt_bias, 128)
    alog = _pad_lanes(a_log, 128)
    dskx = jnp.repeat(d_skip, HD, axis=1)
    mats = _ssd_mats()
    ynorm, ypre, states, conv_pre = _ssd_fwd(xbc, z, dtr, conv_full, conv_b, dtb, alog, dskx, ssm_norm_w, mats, tok)
    lands = _split_wait("wgather_fwd_wait", s_b, r_b, lands, ynorm, _plan_gather_fwd(cols3, rows3))
    w_out_f = _put_slot(lands[0], late[0], my_s).reshape(D, D)
    w_dn_f = _put_slot(lands[2], late[2], my_s).reshape(DFF, D)
    s_arr = my_s.reshape(1).astype(jnp.int32)

    fw2 = final_norm_w.reshape(1, D)
    sq, dmix, dx1, h2b, act, dfb, dgu, dob, sm_ffn = _mix_ffn(
        xs, attn, ynorm, tgt, mod6, norm2_w, fw2, w_out_f, lands[1], late[1], s_arr, w_dn_f, tm)

    tt = min(2048, T)
    c_arr = ci.reshape(1).astype(jnp.int32)
    tok0 = jnp.zeros((8, 128), F32)
    gw_dn4 = _tn_matmul(act, dfb, GU_SH, D, tt, "dw_down", tok0).reshape(4, DFF // 4, D)
    gw_gu4 = _tn_matmul(h2b, dgu, D, GU_SH, tt, "dw_gate_up", tok0)
    gw_out4 = jnp.concatenate(
        [_tn_matmul(attn, dob, AW, D, tt, "dw_out_a", tok0)[0],
         _tn_matmul(ynorm, dob, SW, D, tt, "dw_out_y", tok0)[0]], axis=0).reshape(4, D // 4, D)
    big1 = [gw_out4, gw_gu4, gw_dn4]
    rbs1 = [128, 512, 352]
    sib1 = [_hbm_empty((4, g.shape[1] // 2, g.shape[2]), F32) for g in big1]
    s_c, r_c, bufs, tok = _split_start("gswap_start", big1 + sib1, 3, _plan_swap(3), tok0)

    dzxd, d_cw, d_cb, d_sw, d_sk, d_dtb, d_av = _ssd_bwd(
        xbc, conv_pre, z, dtr, ypre, states, dmix, conv_full, dtb, alog, dskx, ssm_norm_w, mats, tok)
    bufs = _split_wait("gswap_wait", s_c, r_c, bufs, dzxd, _plan_swap(3))
    sums1 = [_add_half(g, s, c_arr, rb, "grad_add_%d" % i)
             for i, (g, s, rb) in enumerate(zip(bufs[:3], bufs[3:], rbs1))]
    land1 = [_hbm_empty(p.shape, BF16) for p in sums1]
    s_d, r_d, bufs, tok = _split_start("gscatter_start", sums1 + land1, 9, _plan_scatter(3), tok0)
    dqkv, d_sinks = _attn_bwd(qkv, sinks, lse, dmix, cos, sin_s, tok)
    bufs = _split_wait("gscatter_wait", s_d, r_d, bufs, dqkv, _plan_scatter(3))
    halves1 = [_sum4(p, l, s_arr, rb, "grad_sum_%d" % i)
               for i, (p, l, rb) in enumerate(zip(bufs[:3], bufs[3:], rbs1))]
    full1 = [_hbm_empty((2 * h.shape[0], h.shape[1]), F32) for h in halves1]
    s_e, r_e, bufs, tok = _split_start("gjoin_start", halves1 + full1, 3, _plan_join(3), tok0)
    gq = _tn_matmul(h1b, dqkv, D, 768, tt, "dw_in_qkv", tok)[0]
    gz = _tn_matmul(h1b, dzxd, D, 1664, tt, "dw_in_zxd", tok)[0]
    gw_in4 = jnp.stack([gq[:, :IN_SH], jnp.concatenate([gq[:, IN_SH:], gz[:, :2 * IN_SH - 768]], axis=1),
                        gz[:, 2 * IN_SH - 768:3 * IN_SH - 768], gz[:, 3 * IN_SH - 768:4 * IN_SH - 768]])
    joined1 = _split_wait("gjoin_wait", s_e, r_e, bufs, gw_in4, _plan_join(3))

    sib0 = _hbm_empty((4, D // 2, IN_SH), F32)
    bufs, _ = _copies_now("gswap_in", [gw_in4, sib0], 1, _plan_swap(1))
    sum0 = _add_half(bufs[0], bufs[1], c_arr, 512, "grad_add_in")
    s_g, r_g, bufs, tok = _split_start("gscatter_in_start", [sum0, _hbm_empty(sum0.shape, BF16)], 3, _plan_scatter(1),
                                       tok0)
    grad_x, sm_in = _in_proj_bwd(xs, dx1, dqkv, dzxd, mod6, norm1_w, w_pad, min(512, T), tok)
    bufs = _split_wait("gscatter_in_wait", s_g, r_g, bufs, grad_x, _plan_scatter(1))
    half0 = _sum4(bufs[0], bufs[1], s_arr, 512, "grad_sum_in")
    joined0, _ = _copies_now("gjoin_in", [half0, _hbm_empty((D, IN_SH), F32)], 1, _plan_join(1))

    a_neg = -jnp.exp(alog)
    pieces = [sm_in[1:2], sm_in[2:3], sm_ffn[5:6], sm_ffn[2:3], sm_ffn[3:4], sm_ffn[4:5],
              sm_in[0:1], sm_ffn[1:2], sm_ffn[0:1], d_cb, d_cw.reshape(1, CONVK * CONVC),
              _pad_lanes(d_sw, SW), d_dtb, d_av * a_neg, d_sk, d_sinks,
              _pad_lanes((0.5 / D * jnp.sum(sq)).reshape(1, 1), 128)]
    vec = jnp.concatenate(pieces, axis=1)
    tot, allv = _small_reduce(vec)
    o = 0
    offs = []
    for p in pieces:
        offs.append(o)
        o += p.shape[1]
    seg = lambda i, n: (offs[i], n)
    g_conv_w = lax.dynamic_slice_in_dim(
        tot[:, offs[10]:offs[10] + CONVK * CONVC].reshape(CONVK, CONVC), my_s * 256, 256, axis=1)
    loss = tot[0, offs[16]]

    small_names = ["b_ada", "norm1_w", "conv_w", "conv_b", "dt_bias", "a_log", "d_skip", "attn_sinks", "ssm_norm_w",
                   "norm2_w", "final_norm_w"]
    small_g = [(0, 6 * D), seg(6, D), g_conv_w, seg(9, D), seg(12, 8), seg(13, 8), seg(14, 8), seg(15, 8),
               seg(11, SW), seg(7, D), seg(8, D)]
    as2d = lambda a: a.reshape(-1, a.shape[-1])
    small_w = [as2d(a) for a in (b_ada, norm1_w, conv_w, conv_b, dt_bias, a_log, d_skip, attn_sinks, ssm_norm_w,
                                 norm2_w, final_norm_w)]
    small_m = [as2d(a) for a in (m_b_ada, m_norm1_w, m_conv_w, m_conv_b, m_dt_bias, m_a_log, m_d_skip, m_attn_sinks,
                                 m_ssm_norm_w, m_norm2_w, m_final_norm_w)]
    small_v = [as2d(a) for a in (v_b_ada, v_norm1_w, v_conv_w, v_conv_b, v_dt_bias, v_a_log, v_d_skip, v_attn_sinks,
                                 v_ssm_norm_w, v_norm2_w, v_final_norm_w)]
    small_g, sd, smn, svn = _adam_small(tot, small_g, small_w, small_m, small_v)
    g_ada, d_ada, m_ada, v_ada = _adam_w_ada(gat, allv, s_arr, w_ada[0], m_w_ada[0], v_w_ada[0], 256)
    native = lambda a: a.transpose(2, 0, 1)
    g_in_s, d_in, m_in, v_in = [a.transpose(1, 2, 0) for a in _adam_w_in(
        native(w_in), joined0[0], joined0[1], native(m_w_in), native(v_w_in), c_arr)]
    g_out_s, d_out, m_out, v_out = _adam_2d(w_out[0], joined1[0], joined1[3], m_w_out[0], v_w_out[0], c_arr, 128,
                                            "adam_w_out")
    g_gu_s, d_gu, m_gu, v_gu = _adam_2d(w_gate_up[0], joined1[1], joined1[4], m_w_gate_up[0], v_w_gate_up[0], c_arr,
                                        256, "adam_w_gate_up")
    g_dn_s, d_dn, m_dn, v_dn = _adam_2d(w_down[0], joined1[2], joined1[5], m_w_down[0], v_w_down[0], c_arr, 352,
                                        "adam_w_down")

    order = ["w_ada", "b_ada", "norm1_w", "w_in", "conv_w", "conv_b", "dt_bias", "a_log", "d_skip", "attn_sinks",
             "ssm_norm_w", "w_out", "norm2_w", "w_gate_up", "w_down", "final_norm_w"]
    shapes = dict(w_ada=w_ada.shape, b_ada=b_ada.shape, norm1_w=norm1_w.shape, w_in=w_in.shape, conv_w=conv_w.shape,
                  conv_b=conv_b.shape, dt_bias=dt_bias.shape, a_log=a_log.shape, d_skip=d_skip.shape,
                  attn_sinks=attn_sinks.shape, ssm_norm_w=ssm_norm_w.shape, w_out=w_out.shape, norm2_w=norm2_w.shape,
                  w_gate_up=w_gate_up.shape, w_down=w_down.shape, final_norm_w=final_norm_w.shape)
    grads = dict(w_ada=g_ada, w_in=g_in_s, w_out=g_out_s, w_gate_up=g_gu_s, w_down=g_dn_s)
    deltas = dict(w_ada=d_ada, w_in=d_in, w_out=d_out, w_gate_up=d_gu, w_down=d_dn)
    new_m = dict(w_ada=m_ada, w_in=m_in, w_out=m_out, w_gate_up=m_gu, w_down=m_dn)
    new_v = dict(w_ada=v_ada, w_in=v_in, w_out=v_out, w_gate_up=v_gu, w_down=v_dn)
    for i, nme in enumerate(small_names):
        grads[nme], deltas[nme], new_m[nme], new_v[nme] = small_g[i], sd[i], smn[i], svn[i]
    outs = [loss, grad_x[None]]
    for table in (grads, deltas, new_m, new_v):
        outs += [table[nme].reshape(shapes[nme]) for nme in order]
    return tuple(outs)
```

```python
import functools
import math

import jax
import jax.numpy as jnp
from jax import lax
from jax.experimental import pallas as pl
from jax.experimental.pallas import tpu as pltpu

F32 = jnp.float32
BF16 = jnp.bfloat16
HI = lax.Precision.HIGHEST
MESH = pl.DeviceIdType.MESH

D = 1024
HD = 64
AW = 512
SW = 512
NST = 128
CONVK = 4
CONVC = 1024
BLK = 128
CPS = 4
SSD_FWD_CPS = 8
ATTN_BPS = 8
IN_PROJ = 2312
IN_PAD = 2432
IN_SH = IN_PROJ // 4
DFF = 2816
GU_SH = 1408
FF_SPLITS = ((0, 1536), (1536, 2816))
EPS = 1e-6
NEG = -1e30
LR, B1, B2, AEPS, WD, STEP = 0.001, 0.9, 0.999, 1e-08, 0.01, 10
VMEM_LIMIT = 58 * 1024 * 1024


def _cp(*sem):
    return pltpu.CompilerParams(dimension_semantics=sem or None, vmem_limit_bytes=VMEM_LIMIT)


def _dot(a, b):
    return jnp.dot(a, b, preferred_element_type=F32)


def _dot_nt(a, b):
    return lax.dot_general(a, b, (((1,), (1,)), ((), ())), preferred_element_type=F32)


def _dot_tn(a, b):
    return lax.dot_general(a, b, (((0,), (0,)), ((), ())), preferred_element_type=F32)


def _dot_hi(a, b):
    return jnp.dot(a, b, precision=HI, preferred_element_type=F32)


def _sigmoid(x):
    return 1.0 / (1.0 + jnp.exp(-x))


def _iota(shape, dim):
    return lax.broadcasted_iota(jnp.int32, shape, dim)


def _load_resident(hbm_ref, vmem_ref, sem):
    @pl.when(pl.program_id(0) == 0)
    def _():
        cp = pltpu.make_async_copy(hbm_ref, vmem_ref, sem)
        cp.start()
        cp.wait()


def _swap32(t):
    lane = _iota(t.shape, 1)
    return jnp.where((lane & 63) < 32, pltpu.roll(t, 96, 1), pltpu.roll(t, 32, 1))


def _rope_fwd(t, cos, sin_s):
    return t * cos + _swap32(t) * sin_s


def _rope_bwd(t, cos, sin_s):
    return t * cos - _swap32(t) * sin_s


DEP_SPEC = pl.BlockSpec((8, 128), lambda *_: (0, 0))


def _rope_tables(positions):
    inv_freq = 10000.0 ** (-jnp.arange(32, dtype=F32) / 32)
    ang = positions.astype(F32)[0][:, None] * inv_freq
    lane = jnp.arange(128)[None, :]
    return jnp.tile(jnp.cos(ang), (1, 4)), jnp.where(lane % 64 < 32, -1.0, 1.0) * jnp.tile(jnp.sin(ang), (1, 4))


def _in_proj_fwd(x, cos, sin_s, mod6, norm1_w, w_pad, tm, dep):
    T = x.shape[0]

    def body(x_ref, cos_ref, sin_ref, mod_ref, nw_ref, w_hbm, dep_ref, qkv_ref, z_ref, xbc_ref, dt_ref, h_ref, w_vmem,
             sem):
        _load_resident(w_hbm, w_vmem, sem)
        xv = x_ref[...]
        r = lax.rsqrt(jnp.mean(xv * xv, axis=-1, keepdims=True) + EPS)
        h = (xv * r * nw_ref[...]) * (1.0 + mod_ref[1:2, :]) + mod_ref[0:1, :]
        hb = h.astype(BF16)
        h_ref[...] = hb
        proj = _dot(hb, w_vmem[...])
        cs, sn = cos_ref[...], sin_ref[...]
        for j in range(5):
            qkv_ref[:, 128 * j:128 * (j + 1)] = _rope_fwd(proj[:, 128 * j:128 * (j + 1)], cs, sn).astype(BF16)
        qkv_ref[:, 640:768] = proj[:, 640:768].astype(BF16)
        z_ref[...] = proj[:, 768:1280]
        xbc_ref[...] = proj[:, 1280:2304]
        dt_ref[...] = proj[:, 2304:2432]

    row = lambda w: pl.BlockSpec((tm, w), lambda i: (i, 0))
    full = lambda a: pl.BlockSpec(a.shape, lambda i: (0,) * a.ndim)
    return pl.pallas_call(
        body, name="in_proj_fwd", grid=(T // tm,),
        in_specs=[row(D), row(128), row(128), full(mod6), full(norm1_w), pl.BlockSpec(memory_space=pl.ANY), DEP_SPEC],
        out_specs=[row(768), row(512), row(1024), row(128), row(D)],
        out_shape=[jax.ShapeDtypeStruct((T, 768), BF16), jax.ShapeDtypeStruct((T, 512), F32),
                   jax.ShapeDtypeStruct((T, 1024), F32), jax.ShapeDtypeStruct((T, 128), F32),
                   jax.ShapeDtypeStruct((T, D), BF16)],
        scratch_shapes=[pltpu.VMEM((D, IN_PAD), BF16), pltpu.SemaphoreType.DMA],
        compiler_params=_cp("arbitrary"),
    )(x, cos, sin_s, mod6, norm1_w, w_pad, dep)


def _head_variants(pair, j):
    lane = _iota(pair.shape, 1)
    lo = lane < 64
    kv = j // 2
    ev = jnp.where(lo, pair, 0.0)
    od = jnp.where(lo, 0.0, pair)
    if kv == 0:
        od = pltpu.roll(od, 64, 1)
    else:
        ev = pltpu.roll(ev, 64, 1)
    return ev.astype(BF16), od.astype(BF16)


def _kv_variants(vcat):
    lane = _iota(vcat.shape, 1)
    lo = lane < 64
    v0 = jnp.where(lo, vcat, 0.0)
    v1 = jnp.where(lo, 0.0, vcat)
    out = {
        (0, 0): v0, (0, 1): pltpu.roll(v0, 64, 1),
        (1, 0): pltpu.roll(v1, 64, 1), (1, 1): v1,
    }
    return {k: v.astype(BF16) for k, v in out.items()}


def _fold_masks(n):
    upper = _iota((BLK, BLK), 1) > _iota((BLK, BLK), 0)
    return upper, upper & (n == 0)


def _attn_fwd(qkv, sinks):
    CPS = ATTN_BPS
    T = qkv.shape[0]
    nsteps = T // (CPS * BLK)

    def body(sink_ref, q_ref, kc_ref, kp_ref, vc_ref, vp_ref, o_ref, lse_ref):
        for sub in range(CPS):
            rows, before = slice(BLK * sub, BLK * (sub + 1)), slice(BLK * (sub - 1), BLK * sub)
            block(pl.program_id(0) * CPS + sub, sink_ref, q_ref.at[rows, :], kc_ref.at[rows, :],
                  kp_ref if sub == 0 else kc_ref.at[before, :], vc_ref.at[rows, :],
                  vp_ref if sub == 0 else vc_ref.at[before, :], o_ref.at[rows, :], lse_ref.at[rows, :])

    def block(n, sink_ref, q_ref, kc_ref, kp_ref, vc_ref, vp_ref, o_ref, lse_ref):
        vpv = _kv_variants(vp_ref[...].astype(F32))
        vcv = _kv_variants(vc_ref[...].astype(F32))
        q_all = jnp.concatenate(
            [v for j in range(4) for v in _head_variants(q_ref[:, 128 * j:128 * (j + 1)].astype(F32), j)], axis=0)
        s_prev = _dot_nt(q_all, kp_ref[...])
        s_cur = _dot_nt(q_all, kc_ref[...])
        upper, dead = _fold_masks(n)
        lane = _iota((BLK, 128), 1)
        lse_acc = jnp.zeros((BLK, 128), F32)
        for jj in range(4):
            acc = jnp.zeros((BLK, 128), F32)
            for par in range(2):
                h = 2 * jj + par
                rows = slice(h * BLK, (h + 1) * BLK)
                sink = sink_ref[0, h]
                s = jnp.where(dead, NEG, jnp.where(upper, s_prev[rows], s_cur[rows]) * 0.125)
                m = jnp.maximum(jnp.max(s, axis=1, keepdims=True), sink)
                p = jnp.exp(s - m)
                den = jnp.sum(p, axis=1, keepdims=True) + jnp.exp(sink - m)
                pn = p * (1.0 / den)
                acc = (acc + _dot(jnp.where(upper, pn, 0.0).astype(BF16), vpv[(jj // 2, par)])
                       + _dot(jnp.where(upper, 0.0, pn).astype(BF16), vcv[(jj // 2, par)]))
                lse_acc = jnp.where(lane == h, m + jnp.log(den), lse_acc)
            o_ref[:, 128 * jj:128 * (jj + 1)] = acc.astype(BF16)
        lse_ref[...] = lse_acc

    RB = CPS * BLK
    prev = lambda n: jnp.maximum(n * CPS - 1, 0)
    return pl.pallas_call(
        body, name="attn_fwd", grid=(nsteps,),
        in_specs=[pl.BlockSpec(memory_space=pltpu.SMEM),
                  pl.BlockSpec((RB, 512), lambda n: (n, 0)),
                  pl.BlockSpec((RB, 128), lambda n: (n, 4)),
                  pl.BlockSpec((BLK, 128), lambda n: (prev(n), 4)),
                  pl.BlockSpec((RB, 128), lambda n: (n, 5)),
                  pl.BlockSpec((BLK, 128), lambda n: (prev(n), 5))],
        out_specs=[pl.BlockSpec((RB, 512), lambda n: (n, 0)), pl.BlockSpec((RB, 128), lambda n: (n, 0))],
        out_shape=[jax.ShapeDtypeStruct((T, 512), BF16), jax.ShapeDtypeStruct((T, 128), F32)],
        compiler_params=_cp("parallel"),
    )(sinks, qkv, qkv, qkv, qkv, qkv)


def _attn_bwd(qkv, sinks, lse, dmix, cos, sin_s, dep):
    T = qkv.shape[0]
    nb = T // BLK

    def body(sink_ref, q_ref, kc_ref, kp_ref, vc_ref, vp_ref, lse_ref, do_ref, cq_ref, sq_ref, ck_ref, sk_ref,
             dep_ref, out_ref, ds_ref, dq_car, dk_car, dv_car):
        n = pl.program_id(0)
        lane = _iota((BLK, 128), 1)

        @pl.when(n == 0)
        def _():
            ds_ref[...] = jnp.zeros_like(ds_ref)
            dq_car[...] = jnp.zeros_like(dq_car)
            dk_car[...] = jnp.zeros_like(dk_car)
            dv_car[...] = jnp.zeros_like(dv_car)

        @pl.when(n < nb)
        def _():
            kp, kc, vp, vc = kp_ref[...], kc_ref[...], vp_ref[...], vc_ref[...]
            kpv = _kv_variants(kp.astype(F32))
            kcv = _kv_variants(kc.astype(F32))
            lse_v = lse_ref[...]
            q_all = jnp.concatenate(
                [v for j in range(4) for v in _head_variants(q_ref[:, 128 * j:128 * (j + 1)].astype(F32), j)], axis=0)
            do_all = jnp.concatenate(
                [v for j in range(4) for v in _head_variants(do_ref[:, 128 * j:128 * (j + 1)], j)], axis=0)
            s_prev, s_cur = _dot_nt(q_all, kp), _dot_nt(q_all, kc)
            dp_prev, dp_cur = _dot_nt(do_all, vp), _dot_nt(do_all, vc)
            upper, dead = _fold_masks(n)
            out_ref[:, 0:512] = dq_car[...]
            dsk = jnp.zeros((1, 128), F32)
            ds_u, ds_l, p_u, p_l = [], [], [], []
            for jj in range(4):
                dq_acc = jnp.zeros((BLK, 128), F32)
                for par in range(2):
                    h = 2 * jj + par
                    rows = slice(h * BLK, (h + 1) * BLK)
                    lse_h = jnp.sum(jnp.where(lane == h, lse_v, 0.0), axis=1, keepdims=True)
                    s = jnp.where(dead, NEG, jnp.where(upper, s_prev[rows], s_cur[rows]) * 0.125)
                    p = jnp.exp(s - lse_h)
                    dp = jnp.where(upper, dp_prev[rows], dp_cur[rows])
                    delta = jnp.sum(p * dp, axis=1, keepdims=True)
                    ds = p * (dp - delta) * 0.125
                    dsu, dsl = jnp.where(upper, ds, 0.0).astype(BF16), jnp.where(upper, 0.0, ds).astype(BF16)
                    dq_acc = dq_acc + _dot(dsu, kpv[(jj // 2, par)]) + _dot(dsl, kcv[(jj // 2, par)])
                    ds_u.append(dsu)
                    ds_l.append(dsl)
                    p_u.append(jnp.where(upper, p, 0.0).astype(BF16))
                    p_l.append(jnp.where(upper, 0.0, p).astype(BF16))
                    dsk = dsk + jnp.where(lane[0:1] == h, -jnp.sum(jnp.exp(sink_ref[0, h] - lse_h) * delta), 0.0)
                dq_car[:, 128 * jj:128 * (jj + 1)] = _rope_bwd(dq_acc, cq_ref[...], sq_ref[...]).astype(BF16)
            stack = lambda parts: jnp.concatenate(parts, axis=0)
            dk_prev, dk_cur = _dot_tn(stack(ds_u), q_all), _dot_tn(stack(ds_l), q_all)
            dv_prev, dv_cur = _dot_tn(stack(p_u), do_all), _dot_tn(stack(p_l), do_all)
            ds_ref[...] += dsk
            out_ref[:, 512:640] = _rope_bwd(dk_car[...] + dk_prev, ck_ref[...], sk_ref[...]).astype(BF16)
            out_ref[:, 640:768] = (dv_car[...] + dv_prev).astype(BF16)
            dk_car[...] = dk_cur
            dv_car[...] = dv_cur

        @pl.when(n == nb)
        def _():
            out_ref[:, 0:512] = dq_car[...]
            out_ref[:, 512:640] = _rope_bwd(dk_car[...], ck_ref[...], sk_ref[...]).astype(BF16)
            out_ref[:, 640:768] = dv_car[...].astype(BF16)

    cur = lambda n: jnp.minimum(n, nb - 1)
    prev = lambda n: jnp.maximum(cur(n) - 1, 0)
    outb = lambda n: jnp.maximum(n - 1, 0)
    return pl.pallas_call(
        body, name="attn_bwd", grid=(nb + 1,),
        in_specs=[pl.BlockSpec(memory_space=pltpu.SMEM),
                  pl.BlockSpec((BLK, 512), lambda n: (cur(n), 0)),
                  pl.BlockSpec((BLK, 128), lambda n: (cur(n), 4)),
                  pl.BlockSpec((BLK, 128), lambda n: (prev(n), 4)),
                  pl.BlockSpec((BLK, 128), lambda n: (cur(n), 5)),
                  pl.BlockSpec((BLK, 128), lambda n: (prev(n), 5)),
                  pl.BlockSpec((BLK, 128), lambda n: (cur(n), 0)),
                  pl.BlockSpec((BLK, 512), lambda n: (cur(n), 0)),
                  pl.BlockSpec((BLK, 128), lambda n: (cur(n), 0)),
                  pl.BlockSpec((BLK, 128), lambda n: (cur(n), 0)),
                  pl.BlockSpec((BLK, 128), lambda n: (outb(n), 0)),
                  pl.BlockSpec((BLK, 128), lambda n: (outb(n), 0)), DEP_SPEC],
        out_specs=[pl.BlockSpec((BLK, 768), lambda n: (outb(n), 0)), pl.BlockSpec((1, 128), lambda n: (0, 0))],
        out_shape=[jax.ShapeDtypeStruct((T, 768), BF16), jax.ShapeDtypeStruct((1, 128), F32)],
        scratch_shapes=[pltpu.VMEM((BLK, 512), BF16), pltpu.VMEM((BLK, 128), F32), pltpu.VMEM((BLK, 128), F32)],
        compiler_params=_cp("arbitrary"),
    )(sinks, qkv, qkv, qkv, qkv, qkv, lse, dmix, cos, sin_s, cos, sin_s, dep)


def _ssd_mats():
    e = jnp.arange(SW)[None, :] // HD == jnp.arange(128)[:, None]
    tri = jnp.arange(BLK)[None, :] <= jnp.arange(BLK)[:, None]
    return (jnp.tile(e, (3, 1)).astype(BF16), jnp.tile(e.T, (2, 1)).astype(BF16),
            jnp.tile(tri, (1, 3)).astype(BF16), jnp.tile(tri.T, (1, 3)).astype(BF16))


def _pieces(x, n, axis):
    out, r = [], x
    for i in range(n):
        p = r.astype(BF16)
        out.append(p)
        if i + 1 < n:
            r = r - p.astype(F32)
    return jnp.concatenate(out, axis=axis)


def _expand(x, e3):
    return _dot(_pieces(x, 3, 1), e3)


def _head_sums(x, et2):
    return _dot(_pieces(x, 2, 1), et2)


def _run_sum(tri3, x):
    return _dot(tri3, _pieces(x, 3, 0))


def _shift_down(u, tail, j):
    rolled = pltpu.roll(u, j, 0)
    first = jnp.where(_iota(tail.shape, 0) < j, pltpu.roll(tail, j, 0), rolled[0:8])
    return jnp.concatenate([first, rolled[8:]], axis=0)


def _shift_up(d, head, j):
    rolled = pltpu.roll(d, BLK - j, 0)
    last = jnp.where(_iota(head.shape, 0) >= 8 - j, pltpu.roll(head, 8 - j, 0), rolled[BLK - 8:])
    return jnp.concatenate([rolled[:BLK - 8], last], axis=0)


def _ssd_parts(dtr, dtb, alog, e3, tril3):
    xx = dtr + dtb
    dt = jnp.maximum(xx, 0.0) + jnp.log(1.0 + jnp.exp(-jnp.abs(xx)))
    a_neg = -jnp.exp(alog)
    tril = _iota((BLK, BLK), 1) <= _iota((BLK, BLK), 0)
    cs = _run_sum(tril3, dt * a_neg)
    csx = _expand(cs, e3)
    last = csx[BLK - 1:BLK, :]
    return dict(xx=xx, dt=dt, a_neg=a_neg, tril=tril, cs=cs, cs_t=cs.T,
                ecsx=jnp.exp(csx), dtex=jnp.exp(last - csx), cdx=jnp.exp(last), dtx=_expand(dt, e3))


def _decay(parts, h):
    seg = parts["cs"][:, h:h + 1] - parts["cs_t"][h:h + 1, :]
    return jnp.exp(jnp.where(parts["tril"], seg, NEG))


def _group_cols(a, g):
    return a[:, 256 * g:256 * (g + 1)]


def _ssd_fwd(xbc, z, dtr, conv_w, conv_b, dtb, alog, dskx, ssm_w, mats, dep):
    CPS = SSD_FWD_CPS
    T = xbc.shape[0]
    nc = T // BLK

    def body(u_ref, tail_ref, z_ref, dtr_ref, cw_ref, cb_ref, dtb_ref, al_ref, dk_ref, sw_ref, e3_ref, tril3_ref,
             dep_ref, yn_ref, yp_ref, st_ref, co_ref, s_scr):
        n = pl.program_id(0)

        @pl.when(n == 0)
        def _():
            s_scr[...] = jnp.zeros_like(s_scr)

        lane = _iota((BLK, 128), 1)
        lo = lane < 64
        for sub in range(CPS):
            rows = slice(BLK * sub, BLK * (sub + 1))
            u = u_ref[rows, :]
            tail = jnp.where(n > 0, tail_ref[...], 0.0) if sub == 0 else u_ref[BLK * sub - 8:BLK * sub, :]
            co = cb_ref[...] + cw_ref[3:4, :] * u
            for j in range(1, CONVK):
                co = co + cw_ref[3 - j:4 - j, :] * _shift_down(u, tail, j)
            co_ref[rows, :] = co
            xc = co * _sigmoid(co)
            pt = _ssd_parts(dtr_ref[rows, :], dtb_ref[...], al_ref[...], e3_ref[...], tril3_ref[...])
            xs = xc[:, :SW]
            bm = [xc[:, 512:640].astype(BF16), xc[:, 640:768].astype(BF16)]
            cm = [xc[:, 768:896].astype(BF16), xc[:, 896:1024].astype(BF16)]
            s_in = s_scr[...]
            st_ref[sub] = s_in
            xdt = xs * pt["dtx"]
            xde = (xdt * pt["dtex"]).astype(BF16)
            ys, s_new = [], []
            for g in range(2):
                cb = _dot_nt(cm[g], bm[g])
                yoff = _dot(cm[g], _group_cols(s_in, g).astype(BF16))
                s_new.append(_dot_tn(bm[g], _group_cols(xde, g)))
                for jj in range(2):
                    j = 2 * g + jj
                    chunk = xdt[:, 128 * j:128 * (j + 1)]
                    g_ev = (cb * _decay(pt, 2 * j)).astype(BF16)
                    g_od = (cb * _decay(pt, 2 * j + 1)).astype(BF16)
                    yd = (_dot(g_ev, jnp.where(lo, chunk, 0.0).astype(BF16))
                          + _dot(g_od, jnp.where(lo, 0.0, chunk).astype(BF16)))
                    ys.append(yd + yoff[:, 128 * jj:128 * (jj + 1)] * pt["ecsx"][:, 128 * j:128 * (j + 1)])
            y = jnp.concatenate(ys, axis=1) + xs * dk_ref[...]
            s_scr[...] = s_in * pt["cdx"] + jnp.concatenate(s_new, axis=1)
            yp_ref[rows, :] = y
            zv = z_ref[rows, :]
            yz = y * (zv * _sigmoid(zv))
            outs = []
            for g in range(2):
                yg = _group_cols(yz, g)
                outs.append(yg * lax.rsqrt(jnp.mean(yg * yg, axis=-1, keepdims=True) + EPS))
            yn_ref[rows, :] = (jnp.concatenate(outs, axis=1) * sw_ref[...]).astype(BF16)

    e3, _, tril3, _ = mats
    RB = CPS * BLK
    tail8 = lambda n: jnp.maximum(n * (RB // 8) - 1, 0)
    full = lambda a: pl.BlockSpec(a.shape, lambda n: (0,) * a.ndim)
    return pl.pallas_call(
        body, name="ssd_fwd", grid=(nc // CPS,),
        in_specs=[pl.BlockSpec((RB, CONVC), lambda n: (n, 0)), pl.BlockSpec((8, CONVC), lambda n: (tail8(n), 0)),
                  pl.BlockSpec((RB, SW), lambda n: (n, 0)), pl.BlockSpec((RB, 128), lambda n: (n, 0)),
                  full(conv_w), full(conv_b), full(dtb), full(alog), full(dskx), full(ssm_w), full(e3), full(tril3),
                  DEP_SPEC],
        out_specs=[pl.BlockSpec((RB, SW), lambda n: (n, 0)), pl.BlockSpec((RB, SW), lambda n: (n, 0)),
                   pl.BlockSpec((CPS, NST, SW), lambda n: (n, 0, 0)), pl.BlockSpec((RB, CONVC), lambda n: (n, 0))],
        out_shape=[jax.ShapeDtypeStruct((T, SW), BF16), jax.ShapeDtypeStruct((T, SW), F32),
                   jax.ShapeDtypeStruct((nc, NST, SW), F32), jax.ShapeDtypeStruct((T, CONVC), F32)],
        scratch_shapes=[pltpu.VMEM((NST, SW), F32)],
        compiler_params=_cp("arbitrary"),
    )(xbc, xbc, z, dtr, conv_w, conv_b, dtb, alog, dskx, ssm_w, e3, tril3, dep)


def _ssd_bwd(xbc, co_all, z, dtr, ypre, states, dmix, conv_w, dtb, alog, dskx, ssm_w, mats, dep):
    T = xbc.shape[0]
    nsteps = T // (CPS * BLK)

    def body(*refs):
        per_chunk, consts, out_ref, carried = refs[:7], refs[7:16], refs[17], refs[18:]
        i = pl.program_id(0)

        @pl.when(i == 0)
        def _():
            for r in carried:
                r[...] = jnp.zeros_like(r)

        for sub in reversed(range(CPS)):
            rows = slice(BLK * sub, BLK * (sub + 1))
            views = [r.at[sub:sub + 1] if k == 5 else r.at[rows, :] for k, r in enumerate(per_chunk)]
            chunk(*views, *consts, out_ref.at[rows, :], *carried)

        @pl.when(i == nsteps - 1)
        def _():
            dsk_ref, dskx_scr = carried[3], carried[8]
            dsk_ref[...] = _head_sums(jnp.broadcast_to(dskx_scr[...], (8, SW)), consts[6][...])[0:1]

    def chunk(u_ref, co_ref, z_ref, dtr_ref, yp_ref, st_ref, dyn_ref, cw_ref, dtb_ref, al_ref, dk_ref, sw_ref,
              e3_ref, et2_ref, tril3_ref, triu3_ref,
              out_ref, dcw_ref, dcb_ref, dsw_ref, dsk_ref, ddtb_ref, dav_ref, ds_scr, dco_scr, dskx_scr):
        co = co_ref[...]
        sg = _sigmoid(co)
        xc = co * sg
        pt = _ssd_parts(dtr_ref[...], dtb_ref[...], al_ref[...], e3_ref[...], tril3_ref[...])
        dtx, ecsx, dtex, cdx = pt["dtx"], pt["ecsx"], pt["dtex"], pt["cdx"]
        xs = xc[:, :SW]
        bm = [xc[:, 512:640].astype(BF16), xc[:, 640:768].astype(BF16)]
        cm = [xc[:, 768:896].astype(BF16), xc[:, 896:1024].astype(BF16)]
        s_in = st_ref[0]
        ds_out = ds_scr[...]
        e_t = et2_ref[...]

        zv = z_ref[...]
        sz = _sigmoid(zv)
        silu_z = zv * sz
        ypre = yp_ref[...]
        yz = ypre * silu_z
        dyn = dyn_ref[...]
        sw = sw_ref[...]
        dyz, yns = [], []
        for g in range(2):
            yg = _group_cols(yz, g)
            r = lax.rsqrt(jnp.mean(yg * yg, axis=-1, keepdims=True) + EPS)
            yn = yg * r
            dg = _group_cols(dyn, g) * _group_cols(sw, g)
            dyz.append(r * (dg - yn * jnp.mean(dg * yn, axis=-1, keepdims=True)))
            yns.append(yn)
        dyz = jnp.concatenate(dyz, axis=1)
        dsw_ref[...] += jnp.sum(dyn * jnp.concatenate(yns, axis=1), axis=0, keepdims=True)
        dy = dyz * silu_z
        dz = dyz * ypre * (sz * (1.0 + zv * (1.0 - sz)))

        xdt = xs * dtx
        xdt_b = xdt.astype(BF16)
        edy = (ecsx * dy).astype(BF16)
        xde = (xdt * dtex).astype(BF16)
        lane = _iota((BLK, 128), 1)
        lo = lane < 64
        row8 = _iota((8, 128), 0)
        dcs = jnp.zeros((BLK, 128), F32)
        col_rows = jnp.zeros((8, 128), F32)
        dxdt, bds, yoff, dbs, dcs_g, ds_new = [], [], [], [], [], []
        for g in range(2):
            s_g = _group_cols(s_in, g).astype(BF16)
            dso_g = _group_cols(ds_out, g).astype(BF16)
            cb = _dot_nt(cm[g], bm[g])
            bds.append(_dot(bm[g], dso_g))
            yoff.append(_dot(cm[g], s_g))
            dcb_g = jnp.zeros((BLK, BLK), F32)
            for jj in range(2):
                j = 2 * g + jj
                dy_c = dy[:, 128 * j:128 * (j + 1)]
                xdt_c = xdt_b[:, 128 * j:128 * (j + 1)]
                acc = jnp.zeros((BLK, 128), F32)
                for par in range(2):
                    h = 2 * j + par
                    lm = _decay(pt, h)
                    gm = cb * lm
                    dy_m = (jnp.where(lo, dy_c, 0.0) if par == 0 else jnp.where(lo, 0.0, dy_c)).astype(BF16)
                    dg_h = _dot_nt(dy_m, xdt_c)
                    w_h = dg_h * gm
                    dcs = dcs + jnp.where(lane == h, jnp.sum(w_h, axis=1, keepdims=True), 0.0)
                    col_rows = col_rows + jnp.where(row8 == h, jnp.sum(w_h, axis=0, keepdims=True), 0.0)
                    dcb_g = dcb_g + dg_h * lm
                    acc = acc + _dot_tn(gm.astype(BF16), dy_m)
                dxdt.append(acc)
            dcb_b = dcb_g.astype(BF16)
            dcs_g.append(_dot(dcb_b, bm[g]) + _dot_nt(_group_cols(edy, g), s_g))
            dbs.append(_dot_tn(dcb_b, cm[g]) + _dot_nt(_group_cols(xde, g), dso_g))
            ds_new.append(_dot_tn(cm[g], _group_cols(edy, g)))
        bds = jnp.concatenate(bds, axis=1)
        yoff = jnp.concatenate(yoff, axis=1) * ecsx
        dxdt = jnp.concatenate(dxdt, axis=1) + dtex * bds
        ds_scr[...] = cdx * ds_out + jnp.concatenate(ds_new, axis=1)

        t_m = _head_sums(dtex * xdt * bds, e_t)
        colsum_t = jnp.concatenate([col_rows, jnp.zeros((BLK - 8, 128), F32)], axis=0).T
        cd = jnp.exp(pt["cs"][BLK - 1:BLK, :])
        sds = jnp.sum(s_in * ds_out, axis=0, keepdims=True)
        last_row = jnp.sum(t_m, axis=0, keepdims=True) + cd * _head_sums(jnp.broadcast_to(sds, (8, SW)), e_t)[0:1]
        dcs = dcs - colsum_t + _head_sums(dy * yoff, e_t) - t_m
        dcs = dcs + jnp.where(_iota((BLK, 128), 0) == BLK - 1, last_row, 0.0)
        da = _run_sum(triu3_ref[...], dcs)
        dt = pt["dt"]
        ddt = da * pt["a_neg"] + _head_sums(dxdt * xs, e_t)
        dav_ref[...] += jnp.sum(da * dt, axis=0, keepdims=True)
        ddtr = ddt * _sigmoid(pt["xx"])
        ddtb_ref[...] += jnp.sum(ddtr, axis=0, keepdims=True)
        dxs = dxdt * dtx + dy * dk_ref[...]
        dskx_scr[...] += jnp.sum(dy * xs, axis=0, keepdims=True)
        dxc = jnp.concatenate([dxs, dbs[0], dbs[1], dcs_g[0], dcs_g[1]], axis=1)
        dco = dxc * (sg * (1.0 + co * (1.0 - sg)))

        dcb_ref[...] += jnp.sum(dco, axis=0, keepdims=True)
        u = u_ref[...]
        head = dco_scr[...]
        du = jnp.zeros_like(dco)
        for j in range(CONVK):
            up_j = dco if j == 0 else _shift_up(dco, head, j)
            dcw_ref[3 - j:4 - j, :] += jnp.sum(up_j * u, axis=0, keepdims=True)
            du = du + cw_ref[3 - j:4 - j, :] * up_j
        dco_scr[...] = dco[0:8]
        out_ref[:, 0:512] = dz.astype(BF16)
        out_ref[:, 512:1536] = du.astype(BF16)
        out_ref[:, 1536:1664] = ddtr.astype(BF16)

    e3, et2, tril3, triu3 = mats
    RB = CPS * BLK
    rev = lambda i: nsteps - 1 - i
    full = lambda a: pl.BlockSpec(a.shape, lambda i: (0,) * a.ndim)
    acc = lambda r, c: pl.BlockSpec((r, c), lambda i: (0, 0))
    return pl.pallas_call(
        body, name="ssd_bwd", grid=(nsteps,),
        in_specs=[pl.BlockSpec((RB, CONVC), lambda i: (rev(i), 0)), pl.BlockSpec((RB, CONVC), lambda i: (rev(i), 0)),
                  pl.BlockSpec((RB, SW), lambda i: (rev(i), 0)), pl.BlockSpec((RB, 128), lambda i: (rev(i), 0)),
                  pl.BlockSpec((RB, SW), lambda i: (rev(i), 0)), pl.BlockSpec((CPS, NST, SW), lambda i: (rev(i), 0, 0)),
                  pl.BlockSpec((RB, SW), lambda i: (rev(i), 1)),
                  full(conv_w), full(dtb), full(alog), full(dskx), full(ssm_w),
                  full(e3), full(et2), full(tril3), full(triu3), DEP_SPEC],
        out_specs=[pl.BlockSpec((RB, 1664), lambda i: (rev(i), 0)),
                   acc(CONVK, CONVC), acc(1, CONVC), acc(1, SW), acc(1, 128), acc(1, 128), acc(1, 128)],
        out_shape=[jax.ShapeDtypeStruct((T, 1664), BF16),
                   jax.ShapeDtypeStruct((CONVK, CONVC), F32), jax.ShapeDtypeStruct((1, CONVC), F32),
                   jax.ShapeDtypeStruct((1, SW), F32), jax.ShapeDtypeStruct((1, 128), F32),
                   jax.ShapeDtypeStruct((1, 128), F32), jax.ShapeDtypeStruct((1, 128), F32)],
        scratch_shapes=[pltpu.VMEM((NST, SW), F32), pltpu.VMEM((8, CONVC), F32), pltpu.VMEM((1, SW), F32)],
        compiler_params=_cp("arbitrary"),
    )(xbc, co_all, z, dtr, ypre, states, dmix, conv_w, dtb, alog, dskx, ssm_w, e3, et2, tril3, triu3, dep)


def _mix_ffn(x, attn, ynorm, tgt, mod6, norm2_w, final_w, w_out, w_gu, w_gu_own, s_arr, w_dn, tm):
    T = x.shape[0]
    nt = T // tm

    def body(x_ref, a_ref, y_ref, t_ref, mod_ref, n2_ref, fw_ref, wo_hbm, wgu_hbm, own_hbm, s_ref, wdn_hbm,
             sq_ref, dmix_ref, dx1_ref, h2_ref, act_ref, df_ref, dgu_ref, do_ref, sm_ref,
             wo, wgu, wdn, sems):
        i = pl.program_id(0)

        @pl.when(i == 0)
        def _():
            cps = [pltpu.make_async_copy(s, d, sems.at[k]) for k, (s, d) in
                   enumerate(((wo_hbm, wo), (wgu_hbm, wgu), (wdn_hbm, wdn)))]
            for c in cps:
                c.start()
            for c in cps:
                c.wait()
            own = pltpu.make_async_copy(
                own_hbm, wgu.at[:, pl.ds(pl.multiple_of(s_ref[0] * GU_SH, 128), GU_SH)], sems.at[3])
            own.start()
            own.wait()
            sq_ref[...] = jnp.zeros_like(sq_ref)
            sm_ref[...] = jnp.zeros_like(sm_ref)

        gate1, shift2, scale2, gate2 = mod_ref[2:3, :], mod_ref[3:4, :], mod_ref[4:5, :], mod_ref[5:6, :]
        n2w, fw = n2_ref[...], fw_ref[...]
        o = _dot(a_ref[...], wo[0:AW, :]) + _dot(y_ref[...], wo[AW:D, :])
        x1 = x_ref[...] + gate1 * o
        r2 = lax.rsqrt(jnp.mean(x1 * x1, axis=-1, keepdims=True) + EPS)
        xh2 = x1 * r2
        n2 = xh2 * n2w
        h2b = (n2 * (1.0 + scale2) + shift2).astype(BF16)
        h2_ref[...] = h2b
        f = jnp.zeros((tm, D), F32)
        saved = []
        for a, b in FF_SPLITS:
            gp = _dot(h2b, wgu[:, a:b])
            upj = _dot(h2b, wgu[:, DFF + a:DFF + b])
            sg = _sigmoid(gp)
            sl = gp * sg
            actb = (sl * upj).astype(BF16)
            act_ref[:, a:b] = actb
            f = f + _dot(actb, wdn[a:b, :])
            saved.append((gp, upj, sg, sl))
        x2 = x1 + gate2 * f
        r3 = lax.rsqrt(jnp.mean(x2 * x2, axis=-1, keepdims=True) + EPS)
        xh3 = x2 * r3
        err = xh3 * fw - t_ref[...]
        sq_ref[...] += jnp.sum(err * err, axis=0, keepdims=True)
        dy = err * (1.0 / D)
        dfw = jnp.sum(dy * xh3, axis=0, keepdims=True)
        dxh3 = dy * fw
        dx2 = r3 * (dxh3 - xh3 * jnp.mean(dxh3 * xh3, axis=-1, keepdims=True))
        dgate2 = jnp.sum(dx2 * f, axis=0, keepdims=True)
        dfb = (dx2 * gate2).astype(BF16)
        df_ref[...] = dfb
        dh2 = jnp.zeros((tm, D), F32)
        for (a, b), (gp, upj, sg, sl) in zip(FF_SPLITS, saved):
            dact = _dot_nt(dfb, wdn[a:b, :])
            dg = (dact * upj * (sg * (1.0 + gp * (1.0 - sg)))).astype(BF16)
            du = (dact * sl).astype(BF16)
            dgu_ref[:, a:b] = dg
            dgu_ref[:, DFF + a:DFF + b] = du
            dh2 = dh2 + _dot_nt(dg, wgu[:, a:b]) + _dot_nt(du, wgu[:, DFF + a:DFF + b])
        dshift2 = jnp.sum(dh2, axis=0, keepdims=True)
        dscale2 = jnp.sum(dh2 * n2, axis=0, keepdims=True)
        dn2 = dh2 * (1.0 + scale2)
        dn2w = jnp.sum(dn2 * xh2, axis=0, keepdims=True)
        dxh2 = dn2 * n2w
        dx1 = dx2 + r2 * (dxh2 - xh2 * jnp.mean(dxh2 * xh2, axis=-1, keepdims=True))
        dx1_ref[...] = dx1
        dgate1 = jnp.sum(dx1 * o, axis=0, keepdims=True)
        dob = (dx1 * gate1).astype(BF16)
        do_ref[...] = dob
        dmix_ref[...] = _dot_nt(dob, wo[...])
        sm_ref[...] += jnp.concatenate(
            [dfw, dn2w, dshift2, dscale2, dgate2, dgate1, jnp.zeros((2, D), F32)], axis=0)

    row = lambda w: pl.BlockSpec((tm, w), lambda i: (i, 0))
    full = lambda a: pl.BlockSpec(a.shape, lambda i: (0,) * a.ndim)
    anyspec = pl.BlockSpec(memory_space=pl.ANY)
    return pl.pallas_call(
        body, name="mix_ffn", grid=(nt,),
        in_specs=[row(D), row(AW), row(SW), row(D), full(mod6), full(norm2_w), full(final_w), anyspec, anyspec, anyspec,
                  pl.BlockSpec(memory_space=pltpu.SMEM), anyspec],
        out_specs=[pl.BlockSpec((1, D), lambda i: (0, 0)), row(D), row(D), row(D),
                   row(DFF), row(D), row(2 * DFF), row(D), pl.BlockSpec((8, D), lambda i: (0, 0))],
        out_shape=[jax.ShapeDtypeStruct((1, D), F32), jax.ShapeDtypeStruct((T, D), F32), jax.ShapeDtypeStruct((T, D), F32),
                   jax.ShapeDtypeStruct((T, D), BF16), jax.ShapeDtypeStruct((T, DFF), BF16),
                   jax.ShapeDtypeStruct((T, D), BF16), jax.ShapeDtypeStruct((T, 2 * DFF), BF16),
                   jax.ShapeDtypeStruct((T, D), BF16), jax.ShapeDtypeStruct((8, D), F32)],
        scratch_shapes=[pltpu.VMEM((D, D), BF16), pltpu.VMEM((D, 2 * DFF), BF16), pltpu.VMEM((DFF, D), BF16),
                        pltpu.SemaphoreType.DMA((4,))],
        compiler_params=_cp("arbitrary"),
    )(x, attn, ynorm, tgt, mod6, norm2_w, final_w, w_out, w_gu, w_gu_own, s_arr, w_dn)


def _in_proj_bwd(x, dx1, dqkv, dzxd, mod6, norm1_w, w_pad, tm, dep):
    T = x.shape[0]

    def body(x_ref, dx1_ref, dq_ref, dz_ref, mod_ref, nw_ref, w_hbm, dep_ref, gx_ref, sm_ref, w_vmem, sem):
        _load_resident(w_hbm, w_vmem, sem)

        @pl.when(pl.program_id(0) == 0)
        def _():
            sm_ref[...] = jnp.zeros_like(sm_ref)

        nw = nw_ref[...]
        scale1 = mod_ref[1:2, :]
        sums = jnp.zeros((8, D), F32)
        for rows in (slice(0, tm // 2), slice(tm // 2, tm)):
            dh = _dot_nt(dq_ref[rows, :], w_vmem[:, 0:768]) + _dot_nt(dz_ref[rows, :], w_vmem[:, 768:IN_PAD])
            xv = x_ref[rows, :]
            r = lax.rsqrt(jnp.mean(xv * xv, axis=-1, keepdims=True) + EPS)
            xh = xv * r
            n1 = xh * nw
            dshift = jnp.sum(dh, axis=0, keepdims=True)
            dscale = jnp.sum(dh * n1, axis=0, keepdims=True)
            dn = dh * (1.0 + scale1)
            dnw = jnp.sum(dn * xh, axis=0, keepdims=True)
            dxh = dn * nw
            gx_ref[rows, :] = dx1_ref[rows, :] + r * (dxh - xh * jnp.mean(dxh * xh, axis=-1, keepdims=True))
            sums = sums + jnp.concatenate([dnw, dshift, dscale, jnp.zeros((5, D), F32)], axis=0)
        sm_ref[...] += sums

    row = lambda w: pl.BlockSpec((tm, w), lambda i: (i, 0))
    full = lambda a: pl.BlockSpec(a.shape, lambda i: (0,) * a.ndim)
    return pl.pallas_call(
        body, name="in_proj_bwd", grid=(T // tm,),
        in_specs=[row(D), row(D), row(768), row(1664), full(mod6), full(norm1_w), pl.BlockSpec(memory_space=pl.ANY),
                  DEP_SPEC],
        out_specs=[row(D), pl.BlockSpec((8, D), lambda i: (0, 0))],
        out_shape=[jax.ShapeDtypeStruct((T, D), F32), jax.ShapeDtypeStruct((8, D), F32)],
        scratch_shapes=[pltpu.VMEM((D, IN_PAD), BF16), pltpu.SemaphoreType.DMA],
        compiler_params=_cp("arbitrary"),
    )(x, dx1, dqkv, dzxd, mod6, norm1_w, w_pad, dep)


def _tn_matmul(a, b, K, N, tt, name, dep):
    T = a.shape[0]
    ja, jb = a.shape[1] // K, b.shape[1] // N
    J = max(ja, jb)

    def body(a_ref, b_ref, dep_ref, o_ref):
        t = pl.program_id(1)
        prod = _dot_tn(a_ref[...], b_ref[...])

        @pl.when(t == 0)
        def _():
            o_ref[0] = prod

        @pl.when(t > 0)
        def _():
            o_ref[0] += prod

    return pl.pallas_call(
        body, name=name, grid=(J, T // tt),
        in_specs=[pl.BlockSpec((tt, K), lambda j, t: (t, j if ja > 1 else 0)),
                  pl.BlockSpec((tt, N), lambda j, t: (t, j if jb > 1 else 0)),
                  pl.BlockSpec((8, 128), lambda j, t: (0, 0))],
        out_specs=pl.BlockSpec((1, K, N), lambda j, t: (j, 0, 0)),
        out_shape=jax.ShapeDtypeStruct((J, K, N), F32),
        compiler_params=_cp("parallel", "arbitrary"),
    )(a, b, dep)


def _adam_math(w, g, m, v):
    m = B1 * m + (1.0 - B1) * g
    v = B2 * v + (1.0 - B2) * (g * g)
    m_hat = m / (1.0 - B1 ** STEP)
    v_hat = v / (1.0 - B2 ** STEP)
    delta = -LR * (m_hat / (jnp.sqrt(v_hat) + AEPS) + WD * w)
    return delta, m, v


def _adam_2d(w, mine, land, m, v, c_arr, rb, name):
    R, C = w.shape
    nbh = R // 2 // rb

    def body(c_ref, w_ref, mine_ref, land_ref, m_ref, v_ref, go_ref, d_ref, mo_ref, vo_ref):
        g = jnp.where(pl.program_id(0) // nbh == c_ref[0], mine_ref[...], land_ref[...])
        d, mn, vn = _adam_math(w_ref[...], g, m_ref[...], v_ref[...])
        go_ref[...] = g
        d_ref[...] = d
        mo_ref[...] = mn
        vo_ref[...] = vn

    spec = pl.BlockSpec((rb, C), lambda i, c_ref: (i, 0))
    mine_spec = pl.BlockSpec((rb, C), lambda i, c_ref: (jnp.clip(i - c_ref[0] * nbh, 0, nbh - 1), 0))
    return pl.pallas_call(
        body, name=name,
        grid_spec=pltpu.PrefetchScalarGridSpec(
            num_scalar_prefetch=1, grid=(R // rb,), in_specs=[spec, mine_spec, spec, spec, spec], out_specs=[spec] * 4),
        out_shape=[jax.ShapeDtypeStruct((R, C), F32)] * 4, compiler_params=_cp("parallel"),
    )(c_arr, w, mine, land, m, v)


def _adam_w_in(w3, mine, land, m3, v3, c_arr):
    n = w3.shape[0]

    def body(c_ref, w_hbm, mine_ref, land_ref, m_hbm, v_hbm, g_hbm, d_hbm, mo_hbm, vo_hbm, bufs, sems):
        ins = [pltpu.make_async_copy(src.at[:, 0], bufs.at[k], sems.at[k]) for k, src in enumerate((w_hbm, m_hbm, v_hbm))]
        for cp in ins:
            cp.start()
        half = D // 2
        top = jnp.where(c_ref[0] == 0, mine_ref[...], land_ref[0:half, :])
        bot = jnp.where(c_ref[0] == 1, mine_ref[...], land_ref[half:D, :])
        g = jnp.concatenate([top, bot], axis=0)
        eye = (_iota((D, D), 0) == _iota((D, D), 1)).astype(BF16)
        g_t = jnp.zeros((n, D), F32)
        r = g
        for i in range(3):
            p = r.astype(BF16)
            g_t = g_t + _dot_tn(p, eye)
            if i < 2:
                r = r - p.astype(F32)
        for cp in ins:
            cp.wait()
        d, mn, vn = _adam_math(bufs[0], g_t, bufs[1], bufs[2])
        for k, val in enumerate((g_t, d, mn, vn)):
            bufs[3 + k] = val
        outs = [pltpu.make_async_copy(bufs.at[3 + k], dst.at[:, 0], sems.at[3 + k])
                for k, dst in enumerate((g_hbm, d_hbm, mo_hbm, vo_hbm))]
        for cp in outs:
            cp.start()
        for cp in outs:
            cp.wait()

    anyspec = pl.BlockSpec(memory_space=pl.ANY)
    vm = pl.BlockSpec(memory_space=pltpu.VMEM)
    return pl.pallas_call(
        body, name="adam_w_in",
        in_specs=[pl.BlockSpec(memory_space=pltpu.SMEM), anyspec, vm, vm, anyspec, anyspec], out_specs=[anyspec] * 4,
        out_shape=[jax.ShapeDtypeStruct(w3.shape, F32)] * 4,
        scratch_shapes=[pltpu.VMEM((7, n, D), F32), pltpu.SemaphoreType.DMA((7,))],
        compiler_params=pltpu.CompilerParams(vmem_limit_bytes=VMEM_LIMIT),
    )(c_arr, w3, mine, land, m3, v3)


def _adam_w_ada(gat, allv, s_arr, w, m, v, rb):
    R, C = w.shape

    def body(s_ref, c_ref, dm_ref, w_ref, m_ref, v_ref, g_ref, d_ref, mo_ref, vo_ref):
        cm = _rows_select(c_ref, rb)
        g = lax.dot_general(cm * _sigmoid(cm), _rows_select(dm_ref, C), (((0,), (0,)), ((), ())), precision=HI,
                            preferred_element_type=F32)
        d, mn, vn = _adam_math(w_ref[...], g, m_ref[...], v_ref[...])
        g_ref[...] = g
        d_ref[...] = d
        mo_ref[...] = mn
        vo_ref[...] = vn

    spec = pl.BlockSpec((rb, C), lambda i, s_ref: (i, 0))
    return pl.pallas_call(
        body, name="adam_w_ada",
        grid_spec=pltpu.PrefetchScalarGridSpec(
            num_scalar_prefetch=1, grid=(R // rb,),
            in_specs=[pl.BlockSpec((8, 1, rb), lambda i, s_ref: (0, 0, i)),
                      pl.BlockSpec((8, 1, C), lambda i, s_ref: (0, 0, s_ref[0])), spec, spec, spec],
            out_specs=[spec] * 4),
        out_shape=[jax.ShapeDtypeStruct((R, C), F32)] * 4, compiler_params=_cp("parallel"),
    )(s_arr, gat, allv, w, m, v)


def _adam_small(tot, segs, ws, ms, vs):
    k = len(ws)
    extra = [sg for sg in segs if not isinstance(sg, tuple)]
    ne = len(extra)

    def body(*refs):
        tot_ref, g_x = refs[0], list(refs[1:1 + ne])
        w, m, v = [refs[1 + ne + j * k:1 + ne + (j + 1) * k] for j in range(3)]
        g_o, d_o, m_o, v_o = [refs[1 + ne + (3 + j) * k:1 + ne + (4 + j) * k] for j in range(4)]
        for i in range(k):
            gi = tot_ref[:, segs[i][0]:segs[i][0] + segs[i][1]] if isinstance(segs[i], tuple) else g_x.pop(0)[...]
            d, mn, vn = _adam_math(w[i][...], gi, m[i][...], v[i][...])
            g_o[i][...] = gi
            d_o[i][...] = d
            m_o[i][...] = mn
            v_o[i][...] = vn

    shapes = [jax.ShapeDtypeStruct(w.shape, F32) for w in ws]
    vm = pl.BlockSpec(memory_space=pltpu.VMEM)
    outs = pl.pallas_call(
        body, name="adam_small", in_specs=[vm] * (1 + ne + 3 * k), out_specs=[vm] * (4 * k), out_shape=shapes * 4,
    )(tot, *extra, *ws, *ms, *vs)
    return outs[0:k], outs[k:2 * k], outs[2 * k:3 * k], outs[3 * k:4 * k]


def _pos():
    return lax.axis_index("x"), lax.axis_index("y"), lax.axis_index("c")


def _flip(v, bit):
    return 1 - v if bit else v


def _peer(k):
    x, y, c = _pos()
    return (_flip(x, (k >> 2) & 1), _flip(y, (k >> 1) & 1), _flip(c, k & 1))


def _logical(p):
    return 4 * p[0] + 2 * p[1] + p[2]


def _gather8(src_ref, dst_ref, send_sems, recv_sems):
    me = _logical(_pos())
    dst_ref[pl.ds(me, 1)] = src_ref[...][None]
    copies = []
    for k in range(1, 8):
        cp = pltpu.make_async_remote_copy(src_ref, dst_ref.at[me], send_sems.at[k - 1], recv_sems.at[k - 1],
                                          device_id=_peer(k), device_id_type=MESH)
        cp.start()
        copies.append(cp)
    for k in range(1, 8):
        pltpu.make_async_remote_copy(src_ref, dst_ref.at[_logical(_peer(k))], send_sems.at[k - 1], recv_sems.at[k - 1],
                                     device_id=_peer(k), device_id_type=MESH).wait_recv()
    for cp in copies:
        cp.wait_send()


def _rows_select(ref3, width):
    row = _iota((8, width), 0)
    out = jnp.zeros((8, width), F32)
    for i in range(8):
        out = jnp.where(row == i, ref3[i][:, 0:width], out)
    return out


def _mod_exchange(payload, w_ada_s, b_ada4):
    n_sh = w_ada_s.shape[1]

    def body(pay_ref, w_ref, b_ref, gat_ref, mod_ref, token, p3, sa, ra, sb, rb):
        token[...] = jnp.zeros_like(token)
        x, y, c = _pos()
        me = _logical((x, y, c))
        my_s = 2 * x + y
        _gather8(pay_ref, gat_ref, sa, ra)
        cmat = _rows_select(gat_ref, D)
        prod = _dot_hi(cmat * _sigmoid(cmat), w_ref[...])
        for b in range(8):
            p3[b] = prod[b:b + 1, :]
        mod_ref[pl.ds(my_s, 1)] = p3[pl.ds(me, 1)] + b_ref[pl.ds(my_s, 1)]
        ks = (2, 4, 6)
        copies = []
        for i, k in enumerate(ks):
            pr = _peer(k)
            cp = pltpu.make_async_remote_copy(p3.at[_logical(pr)], mod_ref.at[my_s], sb.at[i], rb.at[i],
                                              device_id=pr, device_id_type=MESH)
            cp.start()
            copies.append(cp)
        for i, k in enumerate(ks):
            pr = _peer(k)
            s_src = 2 * pr[0] + pr[1]
            pltpu.make_async_remote_copy(p3.at[0], mod_ref.at[s_src], sb.at[i], rb.at[i],
                                         device_id=pr, device_id_type=MESH).wait_recv()
            mod_ref[pl.ds(s_src, 1)] = mod_ref[pl.ds(s_src, 1)] + b_ref[pl.ds(s_src, 1)]
        for cp in copies:
            cp.wait_send()

    vm = pl.BlockSpec(memory_space=pltpu.VMEM)
    return pl.pallas_call(
        body, name="mod_exchange", in_specs=[vm, vm, vm], out_specs=[vm, vm, vm],
        out_shape=[jax.ShapeDtypeStruct((8, 1, payload.shape[1]), F32), jax.ShapeDtypeStruct((4, 1, n_sh), F32),
                   jax.ShapeDtypeStruct((8, 128), F32)],
        scratch_shapes=[pltpu.VMEM((8, 1, n_sh), F32), pltpu.SemaphoreType.DMA((7,)), pltpu.SemaphoreType.DMA((7,)),
                        pltpu.SemaphoreType.DMA((3,)), pltpu.SemaphoreType.DMA((3,))],
        compiler_params=pltpu.CompilerParams(vmem_limit_bytes=VMEM_LIMIT),
    )(payload, w_ada_s, b_ada4)


def _chips():
    x, y, _ = _pos()
    out = []
    for k in (1, 2, 3):
        px, py = _flip(x, (k >> 1) & 1), _flip(y, k & 1)
        out.append((px, py, 2 * px + py))
    return out


def _half_rows(ref, which):
    half = ref.shape[-2] // 2
    return pl.ds(pl.multiple_of(which * half, 8), half)


def _small_reduce(vec):
    n = vec.shape[1]

    def body(v_ref, tot_ref, gat_ref, sa, ra):
        _gather8(v_ref, gat_ref, sa, ra)
        tot = gat_ref[0]
        for i in range(1, 8):
            tot = tot + gat_ref[i]
        tot_ref[...] = tot

    vm = pl.BlockSpec(memory_space=pltpu.VMEM)
    return pl.pallas_call(
        body, name="small_reduce", in_specs=[vm], out_specs=[vm, vm],
        out_shape=[jax.ShapeDtypeStruct((1, n), F32), jax.ShapeDtypeStruct((8, 1, n), F32)],
        scratch_shapes=[pltpu.SemaphoreType.DMA((7,)), pltpu.SemaphoreType.DMA((7,))],
    )(vec)


def _add_half(g, sib, c_arr, rb, name):
    _, R, C = g.shape
    half = R // 2
    nb = half // rb

    def body(c_ref, g_ref, s_ref, o_ref):
        o_ref[...] = (g_ref[...] + s_ref[...]).astype(BF16)

    return pl.pallas_call(
        body, name=name,
        grid_spec=pltpu.PrefetchScalarGridSpec(
            num_scalar_prefetch=1, grid=(4, nb),
            in_specs=[pl.BlockSpec((1, rb, C), lambda s, i, c_ref: (s, c_ref[0] * nb + i, 0)),
                      pl.BlockSpec((1, rb, C), lambda s, i, c_ref: (s, i, 0))],
            out_specs=pl.BlockSpec((1, rb, C), lambda s, i, c_ref: (s, i, 0))),
        out_shape=jax.ShapeDtypeStruct((4, half, C), BF16),
        compiler_params=_cp("parallel", "parallel"),
    )(c_arr, g, sib)


def _sum4(parts, land, s_arr, rb, name):
    _, H, C = land.shape

    def body(s_ref, own_ref, r_ref, o_ref):
        own = own_ref[0].astype(F32)
        tot = jnp.zeros((rb, C), F32)
        for j in range(4):
            tot = tot + jnp.where(s_ref[0] == j, own, r_ref[j].astype(F32))
        o_ref[...] = tot

    return pl.pallas_call(
        body, name=name,
        grid_spec=pltpu.PrefetchScalarGridSpec(
            num_scalar_prefetch=1, grid=(H // rb,),
            in_specs=[pl.BlockSpec((1, rb, C), lambda i, s_ref: (s_ref[0], i, 0)),
                      pl.BlockSpec((4, rb, C), lambda i, s_ref: (0, i, 0))],
            out_specs=pl.BlockSpec((rb, C), lambda i, s_ref: (i, 0))),
        out_shape=jax.ShapeDtypeStruct((H, C), F32), compiler_params=_cp("parallel"),
    )(s_arr, parts, land)


HBM_SPEC = pl.BlockSpec(memory_space=pltpu.HBM)
SEM_SPEC = pl.BlockSpec(memory_space=pltpu.SEMAPHORE)
EFFECT = pltpu.SideEffectType.DATAFLOW_SIDE_EFFECTING


def _split_start(name, bufs, n_sem, plan, dep):
    nb = len(bufs)

    def body(*refs):
        ins, send, recv, token = refs[:nb], refs[nb + 1], refs[nb + 2], refs[-1]
        for i, (src, dst, dev, _) in enumerate(plan(ins)):
            pltpu.make_async_remote_copy(src, dst, send.at[i], recv.at[i], device_id=dev, device_id_type=MESH).start()
        token[...] = jnp.zeros_like(token)

    outs = pl.pallas_call(
        body, name=name,
        out_shape=(pltpu.SemaphoreType.DMA((n_sem,)), pltpu.SemaphoreType.DMA((n_sem,)),
                   *[pltpu.HBM(b.shape, b.dtype) for b in bufs], jax.ShapeDtypeStruct((8, 128), F32)),
        in_specs=[HBM_SPEC] * nb + [pl.BlockSpec(memory_space=pl.ANY)],
        out_specs=(SEM_SPEC, SEM_SPEC, *([HBM_SPEC] * nb), pl.BlockSpec(memory_space=pltpu.VMEM)),
        input_output_aliases={i: 2 + i for i in range(nb)},
        compiler_params=pltpu.CompilerParams(has_side_effects=EFFECT),
    )(*[pltpu.with_memory_space_constraint(b, pltpu.HBM) for b in bufs], dep)
    return outs[0], outs[1], list(outs[2:2 + nb]), outs[-1]


def _split_wait(name, send, recv, bufs, after, plan):
    nb = len(bufs)

    def body(*refs):
        ins, send_s, recv_s = refs[:nb], refs[nb], refs[nb + 1]
        for i, (src, dst, dev, mine) in enumerate(plan(ins)):
            pltpu.make_async_remote_copy(src, dst, send_s.at[i], recv_s.at[i], device_id=dev,
                                         device_id_type=MESH).wait_send()
            pltpu.make_async_remote_copy(src, mine, send_s.at[i], recv_s.at[i], device_id=dev,
                                         device_id_type=MESH).wait_recv()

    outs = pl.pallas_call(
        body, name=name, out_shape=[pltpu.HBM(b.shape, b.dtype) for b in bufs],
        in_specs=[HBM_SPEC] * nb + [SEM_SPEC, SEM_SPEC, pl.BlockSpec(memory_space=pl.ANY)],
        out_specs=[HBM_SPEC] * nb, input_output_aliases={i: i for i in range(nb)},
        compiler_params=pltpu.CompilerParams(has_side_effects=EFFECT),
    )(*bufs, send, recv, after)
    return list(outs)


def _copies_now(name, bufs, n_sem, plan):
    nb = len(bufs)

    def body(*refs):
        ins, token, send, recv = refs[:nb], refs[2 * nb], refs[-2], refs[-1]
        token[...] = jnp.zeros_like(token)
        todo = plan(ins)
        for i, (src, dst, dev, _) in enumerate(todo):
            pltpu.make_async_remote_copy(src, dst, send.at[i], recv.at[i], device_id=dev, device_id_type=MESH).start()
        for i, (src, dst, dev, mine) in enumerate(todo):
            pltpu.make_async_remote_copy(src, mine, send.at[i], recv.at[i], device_id=dev, device_id_type=MESH).wait_recv()
        for i, (src, dst, dev, _) in enumerate(todo):
            pltpu.make_async_remote_copy(src, dst, send.at[i], recv.at[i], device_id=dev, device_id_type=MESH).wait_send()

    outs = pl.pallas_call(
        body, name=name,
        out_shape=[pltpu.HBM(b.shape, b.dtype) for b in bufs] + [jax.ShapeDtypeStruct((8, 128), F32)],
        in_specs=[HBM_SPEC] * nb, out_specs=[HBM_SPEC] * nb + [pl.BlockSpec(memory_space=pltpu.VMEM)],
        input_output_aliases={i: i for i in range(nb)},
        scratch_shapes=[pltpu.SemaphoreType.DMA((n_sem,)), pltpu.SemaphoreType.DMA((n_sem,))],
    )(*[pltpu.with_memory_space_constraint(b, pltpu.HBM) for b in bufs])
    return list(outs[:nb]), outs[nb]


def _slot(land, s, rows, cols):
    if cols is None:
        return land.at[s, rows]
    return land.at[rows, pl.ds(pl.multiple_of(s * cols, 128), cols)]


def _plan_gather_ici(cols):
    nw = len(cols)

    def plan(refs):
        x, y, c = _pos()
        my_s = 2 * x + y
        out = []
        for w in range(nw):
            mine = _half_rows(refs[w], c)
            for px, py, ps in _chips():
                out.append((refs[w].at[mine], _slot(refs[nw + w], my_s, mine, cols[w]), (px, py, c),
                            _slot(refs[nw + w], ps, mine, cols[w])))
        return out
    return plan


def _plan_gather_fwd(cols, rows):
    def plan(refs):
        x, y, c = _pos()
        out = []
        for w in range(len(cols)):
            half = rows[w] // 2
            mine = pl.ds(pl.multiple_of(c * half, 8), half)
            other = pl.ds(pl.multiple_of((1 - c) * half, 8), half)
            for px, py, ps in _chips():
                got = _slot(refs[w], ps, mine, cols[w])
                out.append((got, got, (x, y, 1 - c), _slot(refs[w], ps, other, cols[w])))
        return out
    return plan


def _plan_swap(nw):
    def plan(refs):
        x, y, c = _pos()
        return [(refs[w].at[:, _half_rows(refs[w], 1 - c)], refs[nw + w], (x, y, 1 - c), refs[nw + w])
                for w in range(nw)]
    return plan


def _plan_scatter(nw):
    def plan(refs):
        x, y, c = _pos()
        my_s = 2 * x + y
        out = []
        for w in range(nw):
            for px, py, ps in _chips():
                out.append((refs[w].at[ps], refs[nw + w].at[my_s], (px, py, c), refs[nw + w].at[ps]))
        return out
    return plan


def _plan_join(nw):
    def plan(refs):
        x, y, c = _pos()
        out = []
        for w in range(nw):
            land = refs[nw + w]
            out.append((refs[w], land.at[_half_rows(land, c)], (x, y, 1 - c), land.at[_half_rows(land, 1 - c)]))
        return out
    return plan


def _hbm_empty(shape, dtype):
    return pltpu.with_memory_space_constraint(lax.empty(shape, dtype), pltpu.HBM)


def _put_slot(land, own, slot):
    return lax.dynamic_update_slice(land, own[None], (slot,) + (0,) * own.ndim)


def _pad_lanes(a, n):
    return jnp.pad(a, ((0, 0), (0, n - a.shape[1])))


def kernel(x, c, positions, w_ada, b_ada, norm1_w, w_in, conv_w, conv_b, dt_bias, a_log, d_skip, attn_sinks, ssm_norm_w, w_out, norm2_w, w_gate_up, w_down, final_norm_w, loss_target, m_w_ada, m_b_ada, m_norm1_w, m_w_in, m_conv_w, m_conv_b, m_dt_bias, m_a_log, m_d_skip, m_attn_sinks, m_ssm_norm_w, m_w_out, m_norm2_w, m_w_gate_up, m_w_down, m_final_norm_w, v_w_ada, v_b_ada, v_norm1_w, v_w_in, v_conv_w, v_conv_b, v_dt_bias, v_a_log, v_d_skip, v_attn_sinks, v_ssm_norm_w, v_w_out, v_norm2_w, v_w_gate_up, v_w_down, v_final_norm_w):
    T = x.shape[1]
    tm = min(256, T)
    xi, yi, ci = lax.axis_index("x"), lax.axis_index("y"), lax.axis_index("c")
    my_s = 2 * xi + yi
    xs = x[0]
    tgt = loss_target[0]

    payload = jnp.concatenate([c, conv_w[0].reshape(1, CONVK * 256)], axis=1)
    gat, mod4, tok = _mod_exchange(payload, w_ada[0], b_ada.reshape(4, 1, 1536))
    mod6 = mod4.reshape(6, D)
    cw_dev = gat[:, 0, D:].reshape(4, 2, CONVK, 256)[:, 0]
    conv_full = cw_dev.transpose(1, 0, 2).reshape(CONVK, CONVC)

    w_in_b = w_in[0].astype(BF16)
    s_i, r_i, bufs, tok = _split_start("wgather_in_ici_start", [w_in_b, _hbm_empty((4,) + w_in_b.shape, BF16)], 3,
                                       _plan_gather_ici([None]), tok)
    cos, sin_s = _rope_tables(positions)
    bufs = _split_wait("wgather_in_ici_wait", s_i, r_i, bufs, cos, _plan_gather_ici([None]))
    bufs, tok = _copies_now("wgather_in_fwd", bufs[1:], 3, _plan_gather_fwd([None], [D]))
    g_in = _put_slot(bufs[0], w_in_b, my_s)
    w_pad = jnp.concatenate([g_in[0], g_in[1], g_in[2], g_in[3], jnp.zeros((D, IN_PAD - IN_PROJ), BF16)], axis=1)

    late = [w_out[0].astype(BF16), w_gate_up[0].astype(BF16), w_down[0].astype(BF16)]
    lands = [_hbm_empty((4, D // 4, D), BF16), _hbm_empty((D, 2 * DFF), BF16), _hbm_empty((4, DFF // 4, D), BF16)]
    cols3, rows3 = [None, GU_SH, None], [D // 4, D, DFF // 4]
    s_a, r_a, bufs, tok = _split_start("wgather_ici_start", late + lands, 9, _plan_gather_ici(cols3), tok)

    qkv, z, xbc, dtr, h1b = _in_proj_fwd(xs, cos, sin_s, mod6, norm1_w, w_pad, min(512, T), tok)
    sinks = attn_sinks
    attn, lse = _attn_fwd(qkv, sinks)
    bufs = _split_wait("wgather_ici_wait", s_a, r_a, bufs, attn, _plan_gather_ici(cols3))
    s_b, r_b, lands, tok = _split_start("wgather_fwd_start", bufs[3:], 9, _plan_gather_fwd(cols3, rows3), attn)
    dtb = _pad_lanes(dt_bias, 128)
    alog = _pad_lanes(a_log, 128)
    dskx = jnp.repeat(d_skip, HD, axis=1)
    mats = _ssd_mats()
    ynorm, ypre, states, conv_pre = _ssd_fwd(xbc, z, dtr, conv_full, conv_b, dtb, alog, dskx, ssm_norm_w, mats, tok)
    lands = _split_wait("wgather_fwd_wait", s_b, r_b, lands, ynorm, _plan_gather_fwd(cols3, rows3))
    w_out_f = _put_slot(lands[0], late[0], my_s).reshape(D, D)
    w_dn_f = _put_slot(lands[2], late[2], my_s).reshape(DFF, D)
    s_arr = my_s.reshape(1).astype(jnp.int32)

    fw2 = final_norm_w.reshape(1, D)
    sq, dmix, dx1, h2b, act, dfb, dgu, dob, sm_ffn = _mix_ffn(
        xs, attn, ynorm, tgt, mod6, norm2_w, fw2, w_out_f, lands[1], late[1], s_arr, w_dn_f, tm)

    tt = min(2048, T)
    c_arr = ci.reshape(1).astype(jnp.int32)
    tok0 = jnp.zeros((8, 128), F32)
    gw_dn4 = _tn_matmul(act, dfb, GU_SH, D, tt, "dw_down", tok0).reshape(4, DFF // 4, D)
    gw_gu4 = _tn_matmul(h2b, dgu, D, GU_SH, tt, "dw_gate_up", tok0)
    gw_out4 = jnp.concatenate(
        [_tn_matmul(attn, dob, AW, D, tt, "dw_out_a", tok0)[0],
         _tn_matmul(ynorm, dob, SW, D, tt, "dw_out_y", tok0)[0]], axis=0).reshape(4, D // 4, D)
    big1 = [gw_out4, gw_gu4, gw_dn4]
    rbs1 = [128, 512, 352]
    sib1 = [_hbm_empty((4, g.shape[1] // 2, g.shape[2]), F32) for g in big1]
    s_c, r_c, bufs, tok = _split_start("gswap_start", big1 + sib1, 3, _plan_swap(3), tok0)

    dzxd, d_cw, d_cb, d_sw, d_sk, d_dtb, d_av = _ssd_bwd(
        xbc, conv_pre, z, dtr, ypre, states, dmix, conv_full, dtb, alog, dskx, ssm_norm_w, mats, tok)
    bufs = _split_wait("gswap_wait", s_c, r_c, bufs, dzxd, _plan_swap(3))
    sums1 = [_add_half(g, s, c_arr, rb, "grad_add_%d" % i)
             for i, (g, s, rb) in enumerate(zip(bufs[:3], bufs[3:], rbs1))]
    land1 = [_hbm_empty(p.shape, BF16) for p in sums1]
    s_d, r_d, bufs, tok = _split_start("gscatter_start", sums1 + land1, 9, _plan_scatter(3), tok0)
    dqkv, d_sinks = _attn_bwd(qkv, sinks, lse, dmix, cos, sin_s, tok)
    bufs = _split_wait("gscatter_wait", s_d, r_d, bufs, dqkv, _plan_scatter(3))
    halves1 = [_sum4(p, l, s_arr, rb, "grad_sum_%d" % i)
               for i, (p, l, rb) in enumerate(zip(bufs[:3], bufs[3:], rbs1))]
    full1 = [_hbm_empty((2 * h.shape[0], h.shape[1]), F32) for h in halves1]
    s_e, r_e, bufs, tok = _split_start("gjoin_start", halves1 + full1, 3, _plan_join(3), tok0)
    gq = _tn_matmul(h1b, dqkv, D, 768, tt, "dw_in_qkv", tok)[0]
    gz = _tn_matmul(h1b, dzxd, D, 1664, tt, "dw_in_zxd", tok)[0]
    gw_in4 = jnp.stack([gq[:, :IN_SH], jnp.concatenate([gq[:, IN_SH:], gz[:, :2 * IN_SH - 768]], axis=1),
                        gz[:, 2 * IN_SH - 768:3 * IN_SH - 768], gz[:, 3 * IN_SH - 768:4 * IN_SH - 768]])
    joined1 = _split_wait("gjoin_wait", s_e, r_e, bufs, gw_in4, _plan_join(3))

    sib0 = _hbm_empty((4, D // 2, IN_SH), F32)
    bufs, _ = _copies_now("gswap_in", [gw_in4, sib0], 1, _plan_swap(1))
    sum0 = _add_half(bufs[0], bufs[1], c_arr, 512, "grad_add_in")
    s_g, r_g, bufs, tok = _split_start("gscatter_in_start", [sum0, _hbm_empty(sum0.shape, BF16)], 3, _plan_scatter(1),
                                       tok0)
    grad_x, sm_in = _in_proj_bwd(xs, dx1, dqkv, dzxd, mod6, norm1_w, w_pad, min(512, T), tok)
    bufs = _split_wait("gscatter_in_wait", s_g, r_g, bufs, grad_x, _plan_scatter(1))
    half0 = _sum4(bufs[0], bufs[1], s_arr, 512, "grad_sum_in")
    joined0, _ = _copies_now("gjoin_in", [half0, _hbm_empty((D, IN_SH), F32)], 1, _plan_join(1))

    a_neg = -jnp.exp(alog)
    pieces = [sm_in[1:2], sm_in[2:3], sm_ffn[5:6], sm_ffn[2:3], sm_ffn[3:4], sm_ffn[4:5],
              sm_in[0:1], sm_ffn[1:2], sm_ffn[0:1], d_cb, d_cw.reshape(1, CONVK * CONVC),
              _pad_lanes(d_sw, SW), d_dtb, d_av * a_neg, d_sk, d_sinks,
              _pad_lanes((0.5 / D * jnp.sum(sq)).reshape(1, 1), 128)]
    vec = jnp.concatenate(pieces, axis=1)
    tot, allv = _small_reduce(vec)
    o = 0
    offs = []
    for p in pieces:
        offs.append(o)
        o += p.shape[1]
    seg = lambda i, n: (offs[i], n)
    g_conv_w = lax.dynamic_slice_in_dim(
        tot[:, offs[10]:offs[10] + CONVK * CONVC].reshape(CONVK, CONVC), my_s * 256, 256, axis=1)
    loss = tot[0, offs[16]]

    small_names = ["b_ada", "norm1_w", "conv_w", "conv_b", "dt_bias", "a_log", "d_skip", "attn_sinks", "ssm_norm_w",
                   "norm2_w", "final_norm_w"]
    small_g = [(0, 6 * D), seg(6, D), g_conv_w, seg(9, D), seg(12, 8), seg(13, 8), seg(14, 8), seg(15, 8),
               seg(11, SW), seg(7, D), seg(8, D)]
    as2d = lambda a: a.reshape(-1, a.shape[-1])
    small_w = [as2d(a) for a in (b_ada, norm1_w, conv_w, conv_b, dt_bias, a_log, d_skip, attn_sinks, ssm_norm_w,
                                 norm2_w, final_norm_w)]
    small_m = [as2d(a) for a in (m_b_ada, m_norm1_w, m_conv_w, m_conv_b, m_dt_bias, m_a_log, m_d_skip, m_attn_sinks,
                                 m_ssm_norm_w, m_norm2_w, m_final_norm_w)]
    small_v = [as2d(a) for a in (v_b_ada, v_norm1_w, v_conv_w, v_conv_b, v_dt_bias, v_a_log, v_d_skip, v_attn_sinks,
                                 v_ssm_norm_w, v_norm2_w, v_final_norm_w)]
    small_g, sd, smn, svn = _adam_small(tot, small_g, small_w, small_m, small_v)
    g_ada, d_ada, m_ada, v_ada = _adam_w_ada(gat, allv, s_arr, w_ada[0], m_w_ada[0], v_w_ada[0], 256)
    native = lambda a: a.transpose(2, 0, 1)
    g_in_s, d_in, m_in, v_in = [a.transpose(1, 2, 0) for a in _adam_w_in(
        native(w_in), joined0[0], joined0[1], native(m_w_in), native(v_w_in), c_arr)]
    g_out_s, d_out, m_out, v_out = _adam_2d(w_out[0], joined1[0], joined1[3], m_w_out[0], v_w_out[0], c_arr, 128,
                                            "adam_w_out")
    g_gu_s, d_gu, m_gu, v_gu = _adam_2d(w_gate_up[0], joined1[1], joined1[4], m_w_gate_up[0], v_w_gate_up[0], c_arr,
                                        256, "adam_w_gate_up")
    g_dn_s, d_dn, m_dn, v_dn = _adam_2d(w_down[0], joined1[2], joined1[5], m_w_down[0], v_w_down[0], c_arr, 352,
                                        "adam_w_down")

    order = ["w_ada", "b_ada", "norm1_w", "w_in", "conv_w", "conv_b", "dt_bias", "a_log", "d_skip", "attn_sinks",
             "ssm_norm_w", "w_out", "norm2_w", "w_gate_up", "w_down", "final_norm_w"]
    shapes = dict(w_ada=w_ada.shape, b_ada=b_ada.shape, norm1_w=norm1_w.shape, w_in=w_in.shape, conv_w=conv_w.shape,
                  conv_b=conv_b.shape, dt_bias=dt_bias.shape, a_log=a_log.shape, d_skip=d_skip.shape,
                  attn_sinks=attn_sinks.shape, ssm_norm_w=ssm_norm_w.shape, w_out=w_out.shape, norm2_w=norm2_w.shape,
                  w_gate_up=w_gate_up.shape, w_down=w_down.shape, final_norm_w=final_norm_w.shape)
    grads = dict(w_ada=g_ada, w_in=g_in_s, w_out=g_out_s, w_gate_up=g_gu_s, w_down=g_dn_s)
    deltas = dict(w_ada=d_ada, w_in=d_in, w_out=d_out, w_gate_up=d_gu, w_down=d_dn)
    new_m = dict(w_ada=m_ada, w_in=m_in, w_out=m_out, w_gate_up=m_gu, w_down=m_dn)
    new_v = dict(w_ada=v_ada, w_in=v_in, w_out=v_out, w_gate_up=v_gu, w_down=v_dn)
    for i, nme in enumerate(small_names):
        grads[nme], deltas[nme], new_m[nme], new_v[nme] = small_g[i], sd[i], smn[i], svn[i]
    outs = [loss, grad_x[None]]
    for table in (grads, deltas, new_m, new_v):
        outs += [table[nme].reshape(shapes[nme]) for nme in order]
    return tuple(outs)
```

```python
import functools
import math

import jax
import jax.numpy as jnp
from jax import lax
from jax.experimental import pallas as pl
from jax.experimental.pallas import tpu as pltpu

F32 = jnp.float32
BF16 = jnp.bfloat16
HI = lax.Precision.HIGHEST
MESH = pl.DeviceIdType.MESH

D = 1024
HD = 64
AW = 512
SW = 512
NST = 128
CONVK = 4
CONVC = 1024
BLK = 128
CPS = 4
SSD_FWD_CPS = 8
ATTN_BPS = 8
IN_PROJ = 2312
IN_PAD = 2432
IN_SH = IN_PROJ // 4
DFF = 2816
GU_SH = 1408
FF_SPLITS = ((0, 1536), (1536, 2816))
EPS = 1e-6
NEG = -1e30
LR, B1, B2, AEPS, WD, STEP = 0.001, 0.9, 0.999, 1e-08, 0.01, 10
VMEM_LIMIT = 58 * 1024 * 1024


def _cp(*sem):
    return pltpu.CompilerParams(dimension_semantics=sem or None, vmem_limit_bytes=VMEM_LIMIT)


def _dot(a, b):
    return jnp.dot(a, b, preferred_element_type=F32)


def _dot_nt(a, b):
    return lax.dot_general(a, b, (((1,), (1,)), ((), ())), preferred_element_type=F32)


def _dot_tn(a, b):
    return lax.dot_general(a, b, (((0,), (0,)), ((), ())), preferred_element_type=F32)


def _dot_hi(a, b):
    return jnp.dot(a, b, precision=HI, preferred_element_type=F32)


def _sigmoid(x):
    return 1.0 / (1.0 + jnp.exp(-x))


def _iota(shape, dim):
    return lax.broadcasted_iota(jnp.int32, shape, dim)


def _load_resident(hbm_ref, vmem_ref, sem):
    @pl.when(pl.program_id(0) == 0)
    def _():
        cp = pltpu.make_async_copy(hbm_ref, vmem_ref, sem)
        cp.start()
        cp.wait()


def _swap32(t):
    lane = _iota(t.shape, 1)
    return jnp.where((lane & 63) < 32, pltpu.roll(t, 96, 1), pltpu.roll(t, 32, 1))


def _rope_fwd(t, cos, sin_s):
    return t * cos + _swap32(t) * sin_s


def _rope_bwd(t, cos, sin_s):
    return t * cos - _swap32(t) * sin_s


DEP_SPEC = pl.BlockSpec((8, 128), lambda *_: (0, 0))


def _rope_tables(pos_row, inv_freq_col, tm, dep):
    T = pos_row.shape[1]
    lane, row = jnp.arange(128)[None, :], jnp.arange(96)[:, None]
    pick = (lane % 32) == (row % 32)
    sel_cos = pick.astype(BF16)
    sel_sin = jnp.where(pick, jnp.where(lane % 64 < 32, -1.0, 1.0), 0.0).astype(BF16)

    def body(p_ref, f_ref, sc_ref, ss_ref, dep_ref, cos_ref, sin_ref):
        ang = f_ref[...] * p_ref[...].astype(F32)
        cos_ref[...] = _dot_tn(_pieces(jnp.cos(ang), 3, 0), sc_ref[...])
        sin_ref[...] = _dot_tn(_pieces(jnp.sin(ang), 3, 0), ss_ref[...])

    full = lambda a: pl.BlockSpec(a.shape, lambda i: (0,) * a.ndim)
    return pl.pallas_call(
        body, name="rope_tables", grid=(T // tm,),
        in_specs=[pl.BlockSpec((1, tm), lambda i: (0, i)), full(inv_freq_col), full(sel_cos), full(sel_sin), DEP_SPEC],
        out_specs=[pl.BlockSpec((tm, 128), lambda i: (i, 0))] * 2,
        out_shape=[jax.ShapeDtypeStruct((T, 128), F32)] * 2,
        compiler_params=_cp("parallel"),
    )(pos_row, inv_freq_col, sel_cos, sel_sin, dep)


def _in_proj_fwd(x, cos, sin_s, mod6, norm1_w, w_pad, tm, dep):
    T = x.shape[0]

    def body(x_ref, cos_ref, sin_ref, mod_ref, nw_ref, w_hbm, dep_ref, qkv_ref, z_ref, xbc_ref, dt_ref, h_ref, w_vmem,
             sem):
        _load_resident(w_hbm, w_vmem, sem)
        xv = x_ref[...]
        r = lax.rsqrt(jnp.mean(xv * xv, axis=-1, keepdims=True) + EPS)
        h = (xv * r * nw_ref[...]) * (1.0 + mod_ref[1:2, :]) + mod_ref[0:1, :]
        hb = h.astype(BF16)
        h_ref[...] = hb
        proj = _dot(hb, w_vmem[...])
        cs, sn = cos_ref[...], sin_ref[...]
        for j in range(5):
            qkv_ref[:, 128 * j:128 * (j + 1)] = _rope_fwd(proj[:, 128 * j:128 * (j + 1)], cs, sn).astype(BF16)
        qkv_ref[:, 640:768] = proj[:, 640:768].astype(BF16)
        z_ref[...] = proj[:, 768:1280]
        xbc_ref[...] = proj[:, 1280:2304]
        dt_ref[...] = proj[:, 2304:2432]

    row = lambda w: pl.BlockSpec((tm, w), lambda i: (i, 0))
    full = lambda a: pl.BlockSpec(a.shape, lambda i: (0,) * a.ndim)
    return pl.pallas_call(
        body, name="in_proj_fwd", grid=(T // tm,),
        in_specs=[row(D), row(128), row(128), full(mod6), full(norm1_w), pl.BlockSpec(memory_space=pl.ANY), DEP_SPEC],
        out_specs=[row(768), row(512), row(1024), row(128), row(D)],
        out_shape=[jax.ShapeDtypeStruct((T, 768), BF16), jax.ShapeDtypeStruct((T, 512), F32),
                   jax.ShapeDtypeStruct((T, 1024), F32), jax.ShapeDtypeStruct((T, 128), F32),
                   jax.ShapeDtypeStruct((T, D), BF16)],
        scratch_shapes=[pltpu.VMEM((D, IN_PAD), BF16), pltpu.SemaphoreType.DMA],
        compiler_params=_cp("arbitrary"),
    )(x, cos, sin_s, mod6, norm1_w, w_pad, dep)


def _head_variants(pair, j):
    lane = _iota(pair.shape, 1)
    lo = lane < 64
    kv = j // 2
    ev = jnp.where(lo, pair, 0.0)
    od = jnp.where(lo, 0.0, pair)
    if kv == 0:
        od = pltpu.roll(od, 64, 1)
    else:
        ev = pltpu.roll(ev, 64, 1)
    return ev.astype(BF16), od.astype(BF16)


def _kv_variants(vcat):
    lane = _iota(vcat.shape, 1)
    lo = lane < 64
    v0 = jnp.where(lo, vcat, 0.0)
    v1 = jnp.where(lo, 0.0, vcat)
    out = {
        (0, 0): v0, (0, 1): pltpu.roll(v0, 64, 1),
        (1, 0): pltpu.roll(v1, 64, 1), (1, 1): v1,
    }
    return {k: v.astype(BF16) for k, v in out.items()}


def _fold_masks(n):
    upper = _iota((BLK, BLK), 1) > _iota((BLK, BLK), 0)
    return upper, upper & (n == 0)


def _attn_fwd(qkv, sinks):
    CPS = ATTN_BPS
    T = qkv.shape[0]
    nsteps = T // (CPS * BLK)

    def body(sink_ref, q_ref, kc_ref, kp_ref, vc_ref, vp_ref, o_ref, lse_ref):
        for sub in range(CPS):
            rows, before = slice(BLK * sub, BLK * (sub + 1)), slice(BLK * (sub - 1), BLK * sub)
            block(pl.program_id(0) * CPS + sub, sink_ref, q_ref.at[rows, :], kc_ref.at[rows, :],
                  kp_ref if sub == 0 else kc_ref.at[before, :], vc_ref.at[rows, :],
                  vp_ref if sub == 0 else vc_ref.at[before, :], o_ref.at[rows, :], lse_ref.at[rows, :])

    def block(n, sink_ref, q_ref, kc_ref, kp_ref, vc_ref, vp_ref, o_ref, lse_ref):
        vpv = _kv_variants(vp_ref[...].astype(F32))
        vcv = _kv_variants(vc_ref[...].astype(F32))
        q_all = jnp.concatenate(
            [v for j in range(4) for v in _head_variants(q_ref[:, 128 * j:128 * (j + 1)].astype(F32), j)], axis=0)
        s_prev = _dot_nt(q_all, kp_ref[...])
        s_cur = _dot_nt(q_all, kc_ref[...])
        upper, dead = _fold_masks(n)
        lane = _iota((BLK, 128), 1)
        lse_acc = jnp.zeros((BLK, 128), F32)
        for jj in range(4):
            acc = jnp.zeros((BLK, 128), F32)
            for par in range(2):
                h = 2 * jj + par
                rows = slice(h * BLK, (h + 1) * BLK)
                sink = sink_ref[0, h]
                s = jnp.where(dead, NEG, jnp.where(upper, s_prev[rows], s_cur[rows]) * 0.125)
                m = jnp.maximum(jnp.max(s, axis=1, keepdims=True), sink)
                p = jnp.exp(s - m)
                den = jnp.sum(p, axis=1, keepdims=True) + jnp.exp(sink - m)
                pn = p * (1.0 / den)
                acc = (acc + _dot(jnp.where(upper, pn, 0.0).astype(BF16), vpv[(jj // 2, par)])
                       + _dot(jnp.where(upper, 0.0, pn).astype(BF16), vcv[(jj // 2, par)]))
                lse_acc = jnp.where(lane == h, m + jnp.log(den), lse_acc)
            o_ref[:, 128 * jj:128 * (jj + 1)] = acc.astype(BF16)
        lse_ref[...] = lse_acc

    RB = CPS * BLK
    prev = lambda n: jnp.maximum(n * CPS - 1, 0)
    return pl.pallas_call(
        body, name="attn_fwd", grid=(nsteps,),
        in_specs=[pl.BlockSpec(memory_space=pltpu.SMEM),
                  pl.BlockSpec((RB, 512), lambda n: (n, 0)),
                  pl.BlockSpec((RB, 128), lambda n: (n, 4)),
                  pl.BlockSpec((BLK, 128), lambda n: (prev(n), 4)),
                  pl.BlockSpec((RB, 128), lambda n: (n, 5)),
                  pl.BlockSpec((BLK, 128), lambda n: (prev(n), 5))],
        out_specs=[pl.BlockSpec((RB, 512), lambda n: (n, 0)), pl.BlockSpec((RB, 128), lambda n: (n, 0))],
        out_shape=[jax.ShapeDtypeStruct((T, 512), BF16), jax.ShapeDtypeStruct((T, 128), F32)],
        compiler_params=_cp("parallel"),
    )(sinks, qkv, qkv, qkv, qkv, qkv)


def _attn_bwd(qkv, sinks, lse, dmix, cos, sin_s, dep):
    T = qkv.shape[0]
    nb = T // BLK

    def body(sink_ref, q_ref, kc_ref, kp_ref, vc_ref, vp_ref, lse_ref, do_ref, cq_ref, sq_ref, ck_ref, sk_ref,
             dep_ref, out_ref, ds_ref, dq_car, dk_car, dv_car):
        n = pl.program_id(0)
        lane = _iota((BLK, 128), 1)

        @pl.when(n == 0)
        def _():
            ds_ref[...] = jnp.zeros_like(ds_ref)
            dq_car[...] = jnp.zeros_like(dq_car)
            dk_car[...] = jnp.zeros_like(dk_car)
            dv_car[...] = jnp.zeros_like(dv_car)

        @pl.when(n < nb)
        def _():
            kp, kc, vp, vc = kp_ref[...], kc_ref[...], vp_ref[...], vc_ref[...]
            kpv = _kv_variants(kp.astype(F32))
            kcv = _kv_variants(kc.astype(F32))
            lse_v = lse_ref[...]
            q_all = jnp.concatenate(
                [v for j in range(4) for v in _head_variants(q_ref[:, 128 * j:128 * (j + 1)].astype(F32), j)], axis=0)
            do_all = jnp.concatenate(
                [v for j in range(4) for v in _head_variants(do_ref[:, 128 * j:128 * (j + 1)], j)], axis=0)
            s_prev, s_cur = _dot_nt(q_all, kp), _dot_nt(q_all, kc)
            dp_prev, dp_cur = _dot_nt(do_all, vp), _dot_nt(do_all, vc)
            upper, dead = _fold_masks(n)
            out_ref[:, 0:512] = dq_car[...]
            dsk = jnp.zeros((1, 128), F32)
            ds_u, ds_l, p_u, p_l = [], [], [], []
            for jj in range(4):
                dq_acc = jnp.zeros((BLK, 128), F32)
                for par in range(2):
                    h = 2 * jj + par
                    rows = slice(h * BLK, (h + 1) * BLK)
                    lse_h = jnp.sum(jnp.where(lane == h, lse_v, 0.0), axis=1, keepdims=True)
                    s = jnp.where(dead, NEG, jnp.where(upper, s_prev[rows], s_cur[rows]) * 0.125)
                    p = jnp.exp(s - lse_h)
                    dp = jnp.where(upper, dp_prev[rows], dp_cur[rows])
                    delta = jnp.sum(p * dp, axis=1, keepdims=True)
                    ds = p * (dp - delta) * 0.125
                    dsu, dsl = jnp.where(upper, ds, 0.0).astype(BF16), jnp.where(upper, 0.0, ds).astype(BF16)
                    dq_acc = dq_acc + _dot(dsu, kpv[(jj // 2, par)]) + _dot(dsl, kcv[(jj // 2, par)])
                    ds_u.append(dsu)
                    ds_l.append(dsl)
                    p_u.append(jnp.where(upper, p, 0.0).astype(BF16))
                    p_l.append(jnp.where(upper, 0.0, p).astype(BF16))
                    dsk = dsk + jnp.where(lane[0:1] == h, -jnp.sum(jnp.exp(sink_ref[0, h] - lse_h) * delta), 0.0)
                dq_car[:, 128 * jj:128 * (jj + 1)] = _rope_bwd(dq_acc, cq_ref[...], sq_ref[...]).astype(BF16)
            stack = lambda parts: jnp.concatenate(parts, axis=0)
            dk_prev, dk_cur = _dot_tn(stack(ds_u), q_all), _dot_tn(stack(ds_l), q_all)
            dv_prev, dv_cur = _dot_tn(stack(p_u), do_all), _dot_tn(stack(p_l), do_all)
            ds_ref[...] += dsk
            out_ref[:, 512:640] = _rope_bwd(dk_car[...] + dk_prev, ck_ref[...], sk_ref[...]).astype(BF16)
            out_ref[:, 640:768] = (dv_car[...] + dv_prev).astype(BF16)
            dk_car[...] = dk_cur
            dv_car[...] = dv_cur

        @pl.when(n == nb)
        def _():
            out_ref[:, 0:512] = dq_car[...]
            out_ref[:, 512:640] = _rope_bwd(dk_car[...], ck_ref[...], sk_ref[...]).astype(BF16)
            out_ref[:, 640:768] = dv_car[...].astype(BF16)

    cur = lambda n: jnp.minimum(n, nb - 1)
    prev = lambda n: jnp.maximum(cur(n) - 1, 0)
    outb = lambda n: jnp.maximum(n - 1, 0)
    return pl.pallas_call(
        body, name="attn_bwd", grid=(nb + 1,),
        in_specs=[pl.BlockSpec(memory_space=pltpu.SMEM),
                  pl.BlockSpec((BLK, 512), lambda n: (cur(n), 0)),
                  pl.BlockSpec((BLK, 128), lambda n: (cur(n), 4)),
                  pl.BlockSpec((BLK, 128), lambda n: (prev(n), 4)),
                  pl.BlockSpec((BLK, 128), lambda n: (cur(n), 5)),
                  pl.BlockSpec((BLK, 128), lambda n: (prev(n), 5)),
                  pl.BlockSpec((BLK, 128), lambda n: (cur(n), 0)),
                  pl.BlockSpec((BLK, 512), lambda n: (cur(n), 0)),
                  pl.BlockSpec((BLK, 128), lambda n: (cur(n), 0)),
                  pl.BlockSpec((BLK, 128), lambda n: (cur(n), 0)),
                  pl.BlockSpec((BLK, 128), lambda n: (outb(n), 0)),
                  pl.BlockSpec((BLK, 128), lambda n: (outb(n), 0)), DEP_SPEC],
        out_specs=[pl.BlockSpec((BLK, 768), lambda n: (outb(n), 0)), pl.BlockSpec((1, 128), lambda n: (0, 0))],
        out_shape=[jax.ShapeDtypeStruct((T, 768), BF16), jax.ShapeDtypeStruct((1, 128), F32)],
        scratch_shapes=[pltpu.VMEM((BLK, 512), BF16), pltpu.VMEM((BLK, 128), F32), pltpu.VMEM((BLK, 128), F32)],
        compiler_params=_cp("arbitrary"),
    )(sinks, qkv, qkv, qkv, qkv, qkv, lse, dmix, cos, sin_s, cos, sin_s, dep)


def _ssd_mats():
    e = jnp.arange(SW)[None, :] // HD == jnp.arange(128)[:, None]
    tri = jnp.arange(BLK)[None, :] <= jnp.arange(BLK)[:, None]
    return (jnp.tile(e, (3, 1)).astype(BF16), jnp.tile(e.T, (2, 1)).astype(BF16),
            jnp.tile(tri, (1, 3)).astype(BF16), jnp.tile(tri.T, (1, 3)).astype(BF16))


def _pieces(x, n, axis):
    out, r = [], x
    for i in range(n):
        p = r.astype(BF16)
        out.append(p)
        if i + 1 < n:
            r = r - p.astype(F32)
    return jnp.concatenate(out, axis=axis)


def _expand(x, e3):
    return _dot(_pieces(x, 3, 1), e3)


def _head_sums(x, et2):
    return _dot(_pieces(x, 2, 1), et2)


def _run_sum(tri3, x):
    return _dot(tri3, _pieces(x, 3, 0))


def _shift_down(u, tail, j):
    rolled = pltpu.roll(u, j, 0)
    first = jnp.where(_iota(tail.shape, 0) < j, pltpu.roll(tail, j, 0), rolled[0:8])
    return jnp.concatenate([first, rolled[8:]], axis=0)


def _shift_up(d, head, j):
    rolled = pltpu.roll(d, BLK - j, 0)
    last = jnp.where(_iota(head.shape, 0) >= 8 - j, pltpu.roll(head, 8 - j, 0), rolled[BLK - 8:])
    return jnp.concatenate([rolled[:BLK - 8], last], axis=0)


def _ssd_parts(dtr, dtb, alog, e3, tril3):
    xx = dtr + dtb
    dt = jnp.maximum(xx, 0.0) + jnp.log(1.0 + jnp.exp(-jnp.abs(xx)))
    a_neg = -jnp.exp(alog)
    tril = _iota((BLK, BLK), 1) <= _iota((BLK, BLK), 0)
    cs = _run_sum(tril3, dt * a_neg)
    csx = _expand(cs, e3)
    last = csx[BLK - 1:BLK, :]
    return dict(xx=xx, dt=dt, a_neg=a_neg, tril=tril, cs=cs, cs_t=cs.T,
                ecsx=jnp.exp(csx), dtex=jnp.exp(last - csx), cdx=jnp.exp(last), dtx=_expand(dt, e3))


def _decay(parts, h):
    seg = parts["cs"][:, h:h + 1] - parts["cs_t"][h:h + 1, :]
    return jnp.exp(jnp.where(parts["tril"], seg, NEG))


def _group_cols(a, g):
    return a[:, 256 * g:256 * (g + 1)]


def _ssd_fwd(xbc, z, dtr, conv_w, conv_b, dtb, alog, dskx, ssm_w, mats, dep):
    CPS = SSD_FWD_CPS
    T = xbc.shape[0]
    nc = T // BLK

    def body(u_ref, tail_ref, z_ref, dtr_ref, cw_ref, cb_ref, dtb_ref, al_ref, dk_ref, sw_ref, e3_ref, tril3_ref,
             dep_ref, yn_ref, yp_ref, st_ref, co_ref, s_scr):
        n = pl.program_id(0)

        @pl.when(n == 0)
        def _():
            s_scr[...] = jnp.zeros_like(s_scr)

        lane = _iota((BLK, 128), 1)
        lo = lane < 64
        for sub in range(CPS):
            rows = slice(BLK * sub, BLK * (sub + 1))
            u = u_ref[rows, :]
            tail = jnp.where(n > 0, tail_ref[...], 0.0) if sub == 0 else u_ref[BLK * sub - 8:BLK * sub, :]
            co = cb_ref[...] + cw_ref[3:4, :] * u
            for j in range(1, CONVK):
                co = co + cw_ref[3 - j:4 - j, :] * _shift_down(u, tail, j)
            co_ref[rows, :] = co
            xc = co * _sigmoid(co)
            pt = _ssd_parts(dtr_ref[rows, :], dtb_ref[...], al_ref[...], e3_ref[...], tril3_ref[...])
            xs = xc[:, :SW]
            bm = [xc[:, 512:640].astype(BF16), xc[:, 640:768].astype(BF16)]
            cm = [xc[:, 768:896].astype(BF16), xc[:, 896:1024].astype(BF16)]
            s_in = s_scr[...]
            st_ref[sub] = s_in
            xdt = xs * pt["dtx"]
            xde = (xdt * pt["dtex"]).astype(BF16)
            ys, s_new = [], []
            for g in range(2):
                cb = _dot_nt(cm[g], bm[g])
                yoff = _dot(cm[g], _group_cols(s_in, g).astype(BF16))
                s_new.append(_dot_tn(bm[g], _group_cols(xde, g)))
                for jj in range(2):
                    j = 2 * g + jj
                    chunk = xdt[:, 128 * j:128 * (j + 1)]
                    g_ev = (cb * _decay(pt, 2 * j)).astype(BF16)
                    g_od = (cb * _decay(pt, 2 * j + 1)).astype(BF16)
                    yd = (_dot(g_ev, jnp.where(lo, chunk, 0.0).astype(BF16))
                          + _dot(g_od, jnp.where(lo, 0.0, chunk).astype(BF16)))
                    ys.append(yd + yoff[:, 128 * jj:128 * (jj + 1)] * pt["ecsx"][:, 128 * j:128 * (j + 1)])
            y = jnp.concatenate(ys, axis=1) + xs * dk_ref[...]
            s_scr[...] = s_in * pt["cdx"] + jnp.concatenate(s_new, axis=1)
            yp_ref[rows, :] = y
            zv = z_ref[rows, :]
            yz = y * (zv * _sigmoid(zv))
            outs = []
            for g in range(2):
                yg = _group_cols(yz, g)
                outs.append(yg * lax.rsqrt(jnp.mean(yg * yg, axis=-1, keepdims=True) + EPS))
            yn_ref[rows, :] = (jnp.concatenate(outs, axis=1) * sw_ref[...]).astype(BF16)

    e3, _, tril3, _ = mats
    RB = CPS * BLK
    tail8 = lambda n: jnp.maximum(n * (RB // 8) - 1, 0)
    full = lambda a: pl.BlockSpec(a.shape, lambda n: (0,) * a.ndim)
    return pl.pallas_call(
        body, name="ssd_fwd", grid=(nc // CPS,),
        in_specs=[pl.BlockSpec((RB, CONVC), lambda n: (n, 0)), pl.BlockSpec((8, CONVC), lambda n: (tail8(n), 0)),
                  pl.BlockSpec((RB, SW), lambda n: (n, 0)), pl.BlockSpec((RB, 128), lambda n: (n, 0)),
                  full(conv_w), full(conv_b), full(dtb), full(alog), full(dskx), full(ssm_w), full(e3), full(tril3),
                  DEP_SPEC],
        out_specs=[pl.BlockSpec((RB, SW), lambda n: (n, 0)), pl.BlockSpec((RB, SW), lambda n: (n, 0)),
                   pl.BlockSpec((CPS, NST, SW), lambda n: (n, 0, 0)), pl.BlockSpec((RB, CONVC), lambda n: (n, 0))],
        out_shape=[jax.ShapeDtypeStruct((T, SW), BF16), jax.ShapeDtypeStruct((T, SW), F32),
                   jax.ShapeDtypeStruct((nc, NST, SW), F32), jax.ShapeDtypeStruct((T, CONVC), F32)],
        scratch_shapes=[pltpu.VMEM((NST, SW), F32)],
        compiler_params=_cp("arbitrary"),
    )(xbc, xbc, z, dtr, conv_w, conv_b, dtb, alog, dskx, ssm_w, e3, tril3, dep)


def _ssd_bwd(xbc, co_all, z, dtr, ypre, states, dmix, conv_w, dtb, alog, dskx, ssm_w, mats, dep):
    T = xbc.shape[0]
    nsteps = T // (CPS * BLK)

    def body(*refs):
        per_chunk, consts, out_ref, carried = refs[:7], refs[7:16], refs[17], refs[18:]
        i = pl.program_id(0)

        @pl.when(i == 0)
        def _():
            for r in carried:
                r[...] = jnp.zeros_like(r)

        for sub in reversed(range(CPS)):
            rows = slice(BLK * sub, BLK * (sub + 1))
            views = [r.at[sub:sub + 1] if k == 5 else r.at[rows, :] for k, r in enumerate(per_chunk)]
            chunk(*views, *consts, out_ref.at[rows, :], *carried)

        @pl.when(i == nsteps - 1)
        def _():
            dsk_ref, dskx_scr = carried[3], carried[8]
            dsk_ref[...] = _head_sums(jnp.broadcast_to(dskx_scr[...], (8, SW)), consts[6][...])[0:1]

    def chunk(u_ref, co_ref, z_ref, dtr_ref, yp_ref, st_ref, dyn_ref, cw_ref, dtb_ref, al_ref, dk_ref, sw_ref,
              e3_ref, et2_ref, tril3_ref, triu3_ref,
              out_ref, dcw_ref, dcb_ref, dsw_ref, dsk_ref, ddtb_ref, dav_ref, ds_scr, dco_scr, dskx_scr):
        co = co_ref[...]
        sg = _sigmoid(co)
        xc = co * sg
        pt = _ssd_parts(dtr_ref[...], dtb_ref[...], al_ref[...], e3_ref[...], tril3_ref[...])
        dtx, ecsx, dtex, cdx = pt["dtx"], pt["ecsx"], pt["dtex"], pt["cdx"]
        xs = xc[:, :SW]
        bm = [xc[:, 512:640].astype(BF16), xc[:, 640:768].astype(BF16)]
        cm = [xc[:, 768:896].astype(BF16), xc[:, 896:1024].astype(BF16)]
        s_in = st_ref[0]
        ds_out = ds_scr[...]
        e_t = et2_ref[...]

        zv = z_ref[...]
        sz = _sigmoid(zv)
        silu_z = zv * sz
        ypre = yp_ref[...]
        yz = ypre * silu_z
        dyn = dyn_ref[...]
        sw = sw_ref[...]
        dyz, yns = [], []
        for g in range(2):
            yg = _group_cols(yz, g)
            r = lax.rsqrt(jnp.mean(yg * yg, axis=-1, keepdims=True) + EPS)
            yn = yg * r
            dg = _group_cols(dyn, g) * _group_cols(sw, g)
            dyz.append(r * (dg - yn * jnp.mean(dg * yn, axis=-1, keepdims=True)))
            yns.append(yn)
        dyz = jnp.concatenate(dyz, axis=1)
        dsw_ref[...] += jnp.sum(dyn * jnp.concatenate(yns, axis=1), axis=0, keepdims=True)
        dy = dyz * silu_z
        dz = dyz * ypre * (sz * (1.0 + zv * (1.0 - sz)))

        xdt = xs * dtx
        xdt_b = xdt.astype(BF16)
        edy = (ecsx * dy).astype(BF16)
        xde = (xdt * dtex).astype(BF16)
        lane = _iota((BLK, 128), 1)
        lo = lane < 64
        row8 = _iota((8, 128), 0)
        dcs = jnp.zeros((BLK, 128), F32)
        col_rows = jnp.zeros((8, 128), F32)
        dxdt, bds, yoff, dbs, dcs_g, ds_new = [], [], [], [], [], []
        for g in range(2):
            s_g = _group_cols(s_in, g).astype(BF16)
            dso_g = _group_cols(ds_out, g).astype(BF16)
            cb = _dot_nt(cm[g], bm[g])
            bds.append(_dot(bm[g], dso_g))
            yoff.append(_dot(cm[g], s_g))
            dcb_g = jnp.zeros((BLK, BLK), F32)
            for jj in range(2):
                j = 2 * g + jj
                dy_c = dy[:, 128 * j:128 * (j + 1)]
                xdt_c = xdt_b[:, 128 * j:128 * (j + 1)]
                acc = jnp.zeros((BLK, 128), F32)
                for par in range(2):
                    h = 2 * j + par
                    lm = _decay(pt, h)
                    gm = cb * lm
                    dy_m = (jnp.where(lo, dy_c, 0.0) if par == 0 else jnp.where(lo, 0.0, dy_c)).astype(BF16)
                    dg_h = _dot_nt(dy_m, xdt_c)
                    w_h = dg_h * gm
                    dcs = dcs + jnp.where(lane == h, jnp.sum(w_h, axis=1, keepdims=True), 0.0)
                    col_rows = col_rows + jnp.where(row8 == h, jnp.sum(w_h, axis=0, keepdims=True), 0.0)
                    dcb_g = dcb_g + dg_h * lm
                    acc = acc + _dot_tn(gm.astype(BF16), dy_m)
                dxdt.append(acc)
            dcb_b = dcb_g.astype(BF16)
            dcs_g.append(_dot(dcb_b, bm[g]) + _dot_nt(_group_cols(edy, g), s_g))
            dbs.append(_dot_tn(dcb_b, cm[g]) + _dot_nt(_group_cols(xde, g), dso_g))
            ds_new.append(_dot_tn(cm[g], _group_cols(edy, g)))
        bds = jnp.concatenate(bds, axis=1)
        yoff = jnp.concatenate(yoff, axis=1) * ecsx
        dxdt = jnp.concatenate(dxdt, axis=1) + dtex * bds
        ds_scr[...] = cdx * ds_out + jnp.concatenate(ds_new, axis=1)

        t_m = _head_sums(dtex * xdt * bds, e_t)
        colsum_t = jnp.concatenate([col_rows, jnp.zeros((BLK - 8, 128), F32)], axis=0).T
        cd = jnp.exp(pt["cs"][BLK - 1:BLK, :])
        sds = jnp.sum(s_in * ds_out, axis=0, keepdims=True)
        last_row = jnp.sum(t_m, axis=0, keepdims=True) + cd * _head_sums(jnp.broadcast_to(sds, (8, SW)), e_t)[0:1]
        dcs = dcs - colsum_t + _head_sums(dy * yoff, e_t) - t_m
        dcs = dcs + jnp.where(_iota((BLK, 128), 0) == BLK - 1, last_row, 0.0)
        da = _run_sum(triu3_ref[...], dcs)
        dt = pt["dt"]
        ddt = da * pt["a_neg"] + _head_sums(dxdt * xs, e_t)
        dav_ref[...] += jnp.sum(da * dt, axis=0, keepdims=True)
        ddtr = ddt * _sigmoid(pt["xx"])
        ddtb_ref[...] += jnp.sum(ddtr, axis=0, keepdims=True)
        dxs = dxdt * dtx + dy * dk_ref[...]
        dskx_scr[...] += jnp.sum(dy * xs, axis=0, keepdims=True)
        dxc = jnp.concatenate([dxs, dbs[0], dbs[1], dcs_g[0], dcs_g[1]], axis=1)
        dco = dxc * (sg * (1.0 + co * (1.0 - sg)))

        dcb_ref[...] += jnp.sum(dco, axis=0, keepdims=True)
        u = u_ref[...]
        head = dco_scr[...]
        du = jnp.zeros_like(dco)
        for j in range(CONVK):
            up_j = dco if j == 0 else _shift_up(dco, head, j)
            dcw_ref[3 - j:4 - j, :] += jnp.sum(up_j * u, axis=0, keepdims=True)
            du = du + cw_ref[3 - j:4 - j, :] * up_j
        dco_scr[...] = dco[0:8]
        out_ref[:, 0:512] = dz.astype(BF16)
        out_ref[:, 512:1536] = du.astype(BF16)
        out_ref[:, 1536:1664] = ddtr.astype(BF16)

    e3, et2, tril3, triu3 = mats
    RB = CPS * BLK
    rev = lambda i: nsteps - 1 - i
    full = lambda a: pl.BlockSpec(a.shape, lambda i: (0,) * a.ndim)
    acc = lambda r, c: pl.BlockSpec((r, c), lambda i: (0, 0))
    return pl.pallas_call(
        body, name="ssd_bwd", grid=(nsteps,),
        in_specs=[pl.BlockSpec((RB, CONVC), lambda i: (rev(i), 0)), pl.BlockSpec((RB, CONVC), lambda i: (rev(i), 0)),
                  pl.BlockSpec((RB, SW), lambda i: (rev(i), 0)), pl.BlockSpec((RB, 128), lambda i: (rev(i), 0)),
                  pl.BlockSpec((RB, SW), lambda i: (rev(i), 0)), pl.BlockSpec((CPS, NST, SW), lambda i: (rev(i), 0, 0)),
                  pl.BlockSpec((RB, SW), lambda i: (rev(i), 1)),
                  full(conv_w), full(dtb), full(alog), full(dskx), full(ssm_w),
                  full(e3), full(et2), full(tril3), full(triu3), DEP_SPEC],
        out_specs=[pl.BlockSpec((RB, 1664), lambda i: (rev(i), 0)),
                   acc(CONVK, CONVC), acc(1, CONVC), acc(1, SW), acc(1, 128), acc(1, 128), acc(1, 128)],
        out_shape=[jax.ShapeDtypeStruct((T, 1664), BF16),
                   jax.ShapeDtypeStruct((CONVK, CONVC), F32), jax.ShapeDtypeStruct((1, CONVC), F32),
                   jax.ShapeDtypeStruct((1, SW), F32), jax.ShapeDtypeStruct((1, 128), F32),
                   jax.ShapeDtypeStruct((1, 128), F32), jax.ShapeDtypeStruct((1, 128), F32)],
        scratch_shapes=[pltpu.VMEM((NST, SW), F32), pltpu.VMEM((8, CONVC), F32), pltpu.VMEM((1, SW), F32)],
        compiler_params=_cp("arbitrary"),
    )(xbc, co_all, z, dtr, ypre, states, dmix, conv_w, dtb, alog, dskx, ssm_w, e3, et2, tril3, triu3, dep)


def _mix_ffn(x, attn, ynorm, tgt, mod6, norm2_w, final_w, w_out, w_gu, w_gu_own, s_arr, w_dn, tm):
    T = x.shape[0]
    nt = T // tm

    def body(x_ref, a_ref, y_ref, t_ref, mod_ref, n2_ref, fw_ref, wo_hbm, wgu_hbm, own_hbm, s_ref, wdn_hbm,
             sq_ref, dmix_ref, dx1_ref, h2_ref, act_ref, df_ref, dgu_ref, do_ref, sm_ref,
             wo, wgu, wdn, sems):
        i = pl.program_id(0)

        @pl.when(i == 0)
        def _():
            cps = [pltpu.make_async_copy(s, d, sems.at[k]) for k, (s, d) in
                   enumerate(((wo_hbm, wo), (wgu_hbm, wgu), (wdn_hbm, wdn)))]
            for c in cps:
                c.start()
            for c in cps:
                c.wait()
            own = pltpu.make_async_copy(
                own_hbm, wgu.at[:, pl.ds(pl.multiple_of(s_ref[0] * GU_SH, 128), GU_SH)], sems.at[3])
            own.start()
            own.wait()
            sq_ref[...] = jnp.zeros_like(sq_ref)
            sm_ref[...] = jnp.zeros_like(sm_ref)

        gate1, shift2, scale2, gate2 = mod_ref[2:3, :], mod_ref[3:4, :], mod_ref[4:5, :], mod_ref[5:6, :]
        n2w, fw = n2_ref[...], fw_ref[...]
        o = _dot(a_ref[...], wo[0:AW, :]) + _dot(y_ref[...], wo[AW:D, :])
        x1 = x_ref[...] + gate1 * o
        r2 = lax.rsqrt(jnp.mean(x1 * x1, axis=-1, keepdims=True) + EPS)
        xh2 = x1 * r2
        n2 = xh2 * n2w
        h2b = (n2 * (1.0 + scale2) + shift2).astype(BF16)
        h2_ref[...] = h2b
        f = jnp.zeros((tm, D), F32)
        saved = []
        for a, b in FF_SPLITS:
            gp = _dot(h2b, wgu[:, a:b])
            upj = _dot(h2b, wgu[:, DFF + a:DFF + b])
            sg = _sigmoid(gp)
            sl = gp * sg
            actb = (sl * upj).astype(BF16)
            act_ref[:, a:b] = actb
            f = f + _dot(actb, wdn[a:b, :])
            saved.append((gp, upj, sg, sl))
        x2 = x1 + gate2 * f
        r3 = lax.rsqrt(jnp.mean(x2 * x2, axis=-1, keepdims=True) + EPS)
        xh3 = x2 * r3
        err = xh3 * fw - t_ref[...]
        sq_ref[...] += jnp.sum(err * err, axis=0, keepdims=True)
        dy = err * (1.0 / D)
        dfw = jnp.sum(dy * xh3, axis=0, keepdims=True)
        dxh3 = dy * fw
        dx2 = r3 * (dxh3 - xh3 * jnp.mean(dxh3 * xh3, axis=-1, keepdims=True))
        dgate2 = jnp.sum(dx2 * f, axis=0, keepdims=True)
        dfb = (dx2 * gate2).astype(BF16)
        df_ref[...] = dfb
        dh2 = jnp.zeros((tm, D), F32)
        for (a, b), (gp, upj, sg, sl) in zip(FF_SPLITS, saved):
            dact = _dot_nt(dfb, wdn[a:b, :])
            dg = (dact * upj * (sg * (1.0 + gp * (1.0 - sg)))).astype(BF16)
            du = (dact * sl).astype(BF16)
            dgu_ref[:, a:b] = dg
            dgu_ref[:, DFF + a:DFF + b] = du
            dh2 = dh2 + _dot_nt(dg, wgu[:, a:b]) + _dot_nt(du, wgu[:, DFF + a:DFF + b])
        dshift2 = jnp.sum(dh2, axis=0, keepdims=True)
        dscale2 = jnp.sum(dh2 * n2, axis=0, keepdims=True)
        dn2 = dh2 * (1.0 + scale2)
        dn2w = jnp.sum(dn2 * xh2, axis=0, keepdims=True)
        dxh2 = dn2 * n2w
        dx1 = dx2 + r2 * (dxh2 - xh2 * jnp.mean(dxh2 * xh2, axis=-1, keepdims=True))
        dx1_ref[...] = dx1
        dgate1 = jnp.sum(dx1 * o, axis=0, keepdims=True)
        dob = (dx1 * gate1).astype(BF16)
        do_ref[...] = dob
        dmix_ref[...] = _dot_nt(dob, wo[...])
        sm_ref[...] += jnp.concatenate(
            [dfw, dn2w, dshift2, dscale2, dgate2, dgate1, jnp.zeros((2, D), F32)], axis=0)

    row = lambda w: pl.BlockSpec((tm, w), lambda i: (i, 0))
    full = lambda a: pl.BlockSpec(a.shape, lambda i: (0,) * a.ndim)
    anyspec = pl.BlockSpec(memory_space=pl.ANY)
    return pl.pallas_call(
        body, name="mix_ffn", grid=(nt,),
        in_specs=[row(D), row(AW), row(SW), row(D), full(mod6), full(norm2_w), full(final_w), anyspec, anyspec, anyspec,
                  pl.BlockSpec(memory_space=pltpu.SMEM), anyspec],
        out_specs=[pl.BlockSpec((1, D), lambda i: (0, 0)), row(D), row(D), row(D),
                   row(DFF), row(D), row(2 * DFF), row(D), pl.BlockSpec((8, D), lambda i: (0, 0))],
        out_shape=[jax.ShapeDtypeStruct((1, D), F32), jax.ShapeDtypeStruct((T, D), F32), jax.ShapeDtypeStruct((T, D), F32),
                   jax.ShapeDtypeStruct((T, D), BF16), jax.ShapeDtypeStruct((T, DFF), BF16),
                   jax.ShapeDtypeStruct((T, D), BF16), jax.ShapeDtypeStruct((T, 2 * DFF), BF16),
                   jax.ShapeDtypeStruct((T, D), BF16), jax.ShapeDtypeStruct((8, D), F32)],
        scratch_shapes=[pltpu.VMEM((D, D), BF16), pltpu.VMEM((D, 2 * DFF), BF16), pltpu.VMEM((DFF, D), BF16),
                        pltpu.SemaphoreType.DMA((4,))],
        compiler_params=_cp("arbitrary"),
    )(x, attn, ynorm, tgt, mod6, norm2_w, final_w, w_out, w_gu, w_gu_own, s_arr, w_dn)


def _in_proj_bwd(x, dx1, dqkv, dzxd, mod6, norm1_w, w_pad, tm, dep):
    T = x.shape[0]

    def body(x_ref, dx1_ref, dq_ref, dz_ref, mod_ref, nw_ref, w_hbm, dep_ref, gx_ref, sm_ref, w_vmem, sem):
        _load_resident(w_hbm, w_vmem, sem)

        @pl.when(pl.program_id(0) == 0)
        def _():
            sm_ref[...] = jnp.zeros_like(sm_ref)

        nw = nw_ref[...]
        scale1 = mod_ref[1:2, :]
        sums = jnp.zeros((8, D), F32)
        for rows in (slice(0, tm // 2), slice(tm // 2, tm)):
            dh = _dot_nt(dq_ref[rows, :], w_vmem[:, 0:768]) + _dot_nt(dz_ref[rows, :], w_vmem[:, 768:IN_PAD])
            xv = x_ref[rows, :]
            r = lax.rsqrt(jnp.mean(xv * xv, axis=-1, keepdims=True) + EPS)
            xh = xv * r
            n1 = xh * nw
            dshift = jnp.sum(dh, axis=0, keepdims=True)
            dscale = jnp.sum(dh * n1, axis=0, keepdims=True)
            dn = dh * (1.0 + scale1)
            dnw = jnp.sum(dn * xh, axis=0, keepdims=True)
            dxh = dn * nw
            gx_ref[rows, :] = dx1_ref[rows, :] + r * (dxh - xh * jnp.mean(dxh * xh, axis=-1, keepdims=True))
            sums = sums + jnp.concatenate([dnw, dshift, dscale, jnp.zeros((5, D), F32)], axis=0)
        sm_ref[...] += sums

    row = lambda w: pl.BlockSpec((tm, w), lambda i: (i, 0))
    full = lambda a: pl.BlockSpec(a.shape, lambda i: (0,) * a.ndim)
    return pl.pallas_call(
        body, name="in_proj_bwd", grid=(T // tm,),
        in_specs=[row(D), row(D), row(768), row(1664), full(mod6), full(norm1_w), pl.BlockSpec(memory_space=pl.ANY),
                  DEP_SPEC],
        out_specs=[row(D), pl.BlockSpec((8, D), lambda i: (0, 0))],
        out_shape=[jax.ShapeDtypeStruct((T, D), F32), jax.ShapeDtypeStruct((8, D), F32)],
        scratch_shapes=[pltpu.VMEM((D, IN_PAD), BF16), pltpu.SemaphoreType.DMA],
        compiler_params=_cp("arbitrary"),
    )(x, dx1, dqkv, dzxd, mod6, norm1_w, w_pad, dep)


def _tn_matmul(a, b, K, N, tt, name, dep):
    T = a.shape[0]
    ja, jb = a.shape[1] // K, b.shape[1] // N
    J = max(ja, jb)

    def body(a_ref, b_ref, dep_ref, o_ref):
        t = pl.program_id(1)
        prod = _dot_tn(a_ref[...], b_ref[...])

        @pl.when(t == 0)
        def _():
            o_ref[0] = prod

        @pl.when(t > 0)
        def _():
            o_ref[0] += prod

    return pl.pallas_call(
        body, name=name, grid=(J, T // tt),
        in_specs=[pl.BlockSpec((tt, K), lambda j, t: (t, j if ja > 1 else 0)),
                  pl.BlockSpec((tt, N), lambda j, t: (t, j if jb > 1 else 0)),
                  pl.BlockSpec((8, 128), lambda j, t: (0, 0))],
        out_specs=pl.BlockSpec((1, K, N), lambda j, t: (j, 0, 0)),
        out_shape=jax.ShapeDtypeStruct((J, K, N), F32),
        compiler_params=_cp("parallel", "arbitrary"),
    )(a, b, dep)


def _adam_math(w, g, m, v):
    m = B1 * m + (1.0 - B1) * g
    v = B2 * v + (1.0 - B2) * (g * g)
    m_hat = m / (1.0 - B1 ** STEP)
    v_hat = v / (1.0 - B2 ** STEP)
    delta = -LR * (m_hat / (jnp.sqrt(v_hat) + AEPS) + WD * w)
    return delta, m, v


def _adam_2d(w, mine, land, m, v, c_arr, rb, name, dep):
    R, C = w.shape
    nbh = R // 2 // rb

    def body(c_ref, w_ref, mine_ref, land_ref, m_ref, v_ref, dep_ref, go_ref, d_ref, mo_ref, vo_ref):
        g = jnp.where(pl.program_id(0) // nbh == c_ref[0], mine_ref[...], land_ref[...])
        d, mn, vn = _adam_math(w_ref[...], g, m_ref[...], v_ref[...])
        go_ref[...] = g
        d_ref[...] = d
        mo_ref[...] = mn
        vo_ref[...] = vn

    spec = pl.BlockSpec((rb, C), lambda i, c_ref: (i, 0))
    mine_spec = pl.BlockSpec((rb, C), lambda i, c_ref: (jnp.clip(i - c_ref[0] * nbh, 0, nbh - 1), 0))
    return pl.pallas_call(
        body, name=name,
        grid_spec=pltpu.PrefetchScalarGridSpec(
            num_scalar_prefetch=1, grid=(R // rb,), in_specs=[spec, mine_spec, spec, spec, spec, DEP_SPEC],
            out_specs=[spec] * 4),
        out_shape=[jax.ShapeDtypeStruct((R, C), F32)] * 4, compiler_params=_cp("parallel"),
    )(c_arr, w, mine, land, m, v, dep)


def _adam_w_in(w3, mine, land, m3, v3, c_arr):
    n = w3.shape[0]

    def body(c_ref, w_hbm, mine_ref, land_ref, m_hbm, v_hbm, g_hbm, d_hbm, mo_hbm, vo_hbm, bufs, sems):
        ins = [pltpu.make_async_copy(src.at[:, 0], bufs.at[k], sems.at[k]) for k, src in enumerate((w_hbm, m_hbm, v_hbm))]
        for cp in ins:
            cp.start()
        half = D // 2
        top = jnp.where(c_ref[0] == 0, mine_ref[...], land_ref[0:half, :])
        bot = jnp.where(c_ref[0] == 1, mine_ref[...], land_ref[half:D, :])
        g = jnp.concatenate([top, bot], axis=0)
        eye = (_iota((D, D), 0) == _iota((D, D), 1)).astype(BF16)
        g_t = jnp.zeros((n, D), F32)
        r = g
        for i in range(3):
            p = r.astype(BF16)
            g_t = g_t + _dot_tn(p, eye)
            if i < 2:
                r = r - p.astype(F32)
        for cp in ins:
            cp.wait()
        d, mn, vn = _adam_math(bufs[0], g_t, bufs[1], bufs[2])
        for k, val in enumerate((g_t, d, mn, vn)):
            bufs[3 + k] = val
        outs = [pltpu.make_async_copy(bufs.at[3 + k], dst.at[:, 0], sems.at[3 + k])
                for k, dst in enumerate((g_hbm, d_hbm, mo_hbm, vo_hbm))]
        for cp in outs:
            cp.start()
        for cp in outs:
            cp.wait()

    anyspec = pl.BlockSpec(memory_space=pl.ANY)
    vm = pl.BlockSpec(memory_space=pltpu.VMEM)
    return pl.pallas_call(
        body, name="adam_w_in",
        in_specs=[pl.BlockSpec(memory_space=pltpu.SMEM), anyspec, vm, vm, anyspec, anyspec], out_specs=[anyspec] * 4,
        out_shape=[jax.ShapeDtypeStruct(w3.shape, F32)] * 4,
        scratch_shapes=[pltpu.VMEM((7, n, D), F32), pltpu.SemaphoreType.DMA((7,))],
        compiler_params=pltpu.CompilerParams(vmem_limit_bytes=VMEM_LIMIT),
    )(c_arr, w3, mine, land, m3, v3)


def _adam_w_ada(gat, allv, s_arr, w, m, v, rb):
    R, C = w.shape

    def body(s_ref, c_ref, dm_ref, w_ref, m_ref, v_ref, g_ref, d_ref, mo_ref, vo_ref):
        cm = _rows_select(c_ref, rb)
        g = lax.dot_general(cm * _sigmoid(cm), _rows_select(dm_ref, C), (((0,), (0,)), ((), ())), precision=HI,
                            preferred_element_type=F32)
        d, mn, vn = _adam_math(w_ref[...], g, m_ref[...], v_ref[...])
        g_ref[...] = g
        d_ref[...] = d
        mo_ref[...] = mn
        vo_ref[...] = vn

    spec = pl.BlockSpec((rb, C), lambda i, s_ref: (i, 0))
    return pl.pallas_call(
        body, name="adam_w_ada",
        grid_spec=pltpu.PrefetchScalarGridSpec(
            num_scalar_prefetch=1, grid=(R // rb,),
            in_specs=[pl.BlockSpec((8, 1, rb), lambda i, s_ref: (0, 0, i)),
                      pl.BlockSpec((8, 1, C), lambda i, s_ref: (0, 0, s_ref[0])), spec, spec, spec],
            out_specs=[spec] * 4),
        out_shape=[jax.ShapeDtypeStruct((R, C), F32)] * 4, compiler_params=_cp("parallel"),
    )(s_arr, gat, allv, w, m, v)


def _adam_small(tot, segs, ws, ms, vs):
    k = len(ws)
    extra = [sg for sg in segs if not isinstance(sg, tuple)]
    ne = len(extra)

    def body(*refs):
        tot_ref, g_x = refs[0], list(refs[1:1 + ne])
        w, m, v = [refs[1 + ne + j * k:1 + ne + (j + 1) * k] for j in range(3)]
        g_o, d_o, m_o, v_o = [refs[1 + ne + (3 + j) * k:1 + ne + (4 + j) * k] for j in range(4)]
        for i in range(k):
            gi = tot_ref[:, segs[i][0]:segs[i][0] + segs[i][1]] if isinstance(segs[i], tuple) else g_x.pop(0)[...]
            d, mn, vn = _adam_math(w[i][...], gi, m[i][...], v[i][...])
            g_o[i][...] = gi
            d_o[i][...] = d
            m_o[i][...] = mn
            v_o[i][...] = vn

    shapes = [jax.ShapeDtypeStruct(w.shape, F32) for w in ws]
    vm = pl.BlockSpec(memory_space=pltpu.VMEM)
    outs = pl.pallas_call(
        body, name="adam_small", in_specs=[vm] * (1 + ne + 3 * k), out_specs=[vm] * (4 * k), out_shape=shapes * 4,
    )(tot, *extra, *ws, *ms, *vs)
    return outs[0:k], outs[k:2 * k], outs[2 * k:3 * k], outs[3 * k:4 * k]


def _pos():
    return lax.axis_index("x"), lax.axis_index("y"), lax.axis_index("c")


def _flip(v, bit):
    return 1 - v if bit else v


def _peer(k):
    x, y, c = _pos()
    return (_flip(x, (k >> 2) & 1), _flip(y, (k >> 1) & 1), _flip(c, k & 1))


def _logical(p):
    return 4 * p[0] + 2 * p[1] + p[2]


def _gather8(src_ref, dst_ref, send_sems, recv_sems):
    me = _logical(_pos())
    dst_ref[pl.ds(me, 1)] = src_ref[...][None]
    copies = []
    for k in range(1, 8):
        cp = pltpu.make_async_remote_copy(src_ref, dst_ref.at[me], send_sems.at[k - 1], recv_sems.at[k - 1],
                                          device_id=_peer(k), device_id_type=MESH)
        cp.start()
        copies.append(cp)
    for k in range(1, 8):
        pltpu.make_async_remote_copy(src_ref, dst_ref.at[_logical(_peer(k))], send_sems.at[k - 1], recv_sems.at[k - 1],
                                     device_id=_peer(k), device_id_type=MESH).wait_recv()
    for cp in copies:
        cp.wait_send()


def _rows_select(ref3, width):
    row = _iota((8, width), 0)
    out = jnp.zeros((8, width), F32)
    for i in range(8):
        out = jnp.where(row == i, ref3[i][:, 0:width], out)
    return out


def _mod_exchange(payload, w_ada_s, b_ada4):
    n_sh = w_ada_s.shape[1]

    def body(pay_ref, w_ref, b_ref, gat_ref, mod_ref, token, p3, sa, ra, sb, rb):
        token[...] = jnp.zeros_like(token)
        x, y, c = _pos()
        me = _logical((x, y, c))
        my_s = 2 * x + y
        _gather8(pay_ref, gat_ref, sa, ra)
        cmat = _rows_select(gat_ref, D)
        prod = _dot_hi(cmat * _sigmoid(cmat), w_ref[...])
        for b in range(8):
            p3[b] = prod[b:b + 1, :]
        mod_ref[pl.ds(my_s, 1)] = p3[pl.ds(me, 1)] + b_ref[pl.ds(my_s, 1)]
        ks = (2, 4, 6)
        copies = []
        for i, k in enumerate(ks):
            pr = _peer(k)
            cp = pltpu.make_async_remote_copy(p3.at[_logical(pr)], mod_ref.at[my_s], sb.at[i], rb.at[i],
                                              device_id=pr, device_id_type=MESH)
            cp.start()
            copies.append(cp)
        for i, k in enumerate(ks):
            pr = _peer(k)
            s_src = 2 * pr[0] + pr[1]
            pltpu.make_async_remote_copy(p3.at[0], mod_ref.at[s_src], sb.at[i], rb.at[i],
                                         device_id=pr, device_id_type=MESH).wait_recv()
            mod_ref[pl.ds(s_src, 1)] = mod_ref[pl.ds(s_src, 1)] + b_ref[pl.ds(s_src, 1)]
        for cp in copies:
            cp.wait_send()

    vm = pl.BlockSpec(memory_space=pltpu.VMEM)
    return pl.pallas_call(
        body, name="mod_exchange", in_specs=[vm, vm, vm], out_specs=[vm, vm, vm],
        out_shape=[jax.ShapeDtypeStruct((8, 1, payload.shape[1]), F32), jax.ShapeDtypeStruct((4, 1, n_sh), F32),
                   jax.ShapeDtypeStruct((8, 128), F32)],
        scratch_shapes=[pltpu.VMEM((8, 1, n_sh), F32), pltpu.SemaphoreType.DMA((7,)), pltpu.SemaphoreType.DMA((7,)),
                        pltpu.SemaphoreType.DMA((3,)), pltpu.SemaphoreType.DMA((3,))],
        compiler_params=pltpu.CompilerParams(vmem_limit_bytes=VMEM_LIMIT),
    )(payload, w_ada_s, b_ada4)


def _chips():
    x, y, _ = _pos()
    out = []
    for k in (1, 2, 3):
        px, py = _flip(x, (k >> 1) & 1), _flip(y, k & 1)
        out.append((px, py, 2 * px + py))
    return out


def _half_rows(ref, which):
    half = ref.shape[-2] // 2
    return pl.ds(pl.multiple_of(which * half, 8), half)


def _plan_small():
    def plan(refs):
        me = _logical(_pos())
        return [(refs[0], refs[1].at[me], _peer(k), refs[1].at[_logical(_peer(k))]) for k in range(1, 8)]
    return plan


def _small_sum(vec, land, me_arr):
    n = vec.shape[1]

    def body(me_ref, v_ref, land_ref, tot_ref, all_ref):
        tot = None
        for i in range(8):
            row = jnp.where(me_ref[0] == i, v_ref[...], land_ref[i])
            all_ref[i] = row
            tot = row if i == 0 else tot + row
        tot_ref[...] = tot

    return pl.pallas_call(
        body, name="small_sum",
        grid_spec=pltpu.PrefetchScalarGridSpec(
            num_scalar_prefetch=1, grid=(1,),
            in_specs=[pl.BlockSpec((1, n), lambda i, me_ref: (0, 0)), pl.BlockSpec((8, 1, n), lambda i, me_ref: (0, 0, 0))],
            out_specs=[pl.BlockSpec((1, n), lambda i, me_ref: (0, 0)),
                       pl.BlockSpec((8, 1, n), lambda i, me_ref: (0, 0, 0))]),
        out_shape=[jax.ShapeDtypeStruct((1, n), F32), jax.ShapeDtypeStruct((8, 1, n), F32)],
        compiler_params=_cp("arbitrary"),
    )(me_arr, vec, land)


def _add_half(g, sib, c_arr, rb, name):
    _, R, C = g.shape
    half = R // 2
    nb = half // rb

    def body(c_ref, g_ref, s_ref, o_ref):
        o_ref[...] = (g_ref[...] + s_ref[...]).astype(BF16)

    return pl.pallas_call(
        body, name=name,
        grid_spec=pltpu.PrefetchScalarGridSpec(
            num_scalar_prefetch=1, grid=(4, nb),
            in_specs=[pl.BlockSpec((1, rb, C), lambda s, i, c_ref: (s, c_ref[0] * nb + i, 0)),
                      pl.BlockSpec((1, rb, C), lambda s, i, c_ref: (s, i, 0))],
            out_specs=pl.BlockSpec((1, rb, C), lambda s, i, c_ref: (s, i, 0))),
        out_shape=jax.ShapeDtypeStruct((4, half, C), BF16),
        compiler_params=_cp("parallel", "parallel"),
    )(c_arr, g, sib)


def _sum4(parts, land, s_arr, rb, name):
    _, H, C = land.shape

    def body(s_ref, own_ref, r_ref, o_ref):
        own = own_ref[0].astype(F32)
        tot = jnp.zeros((rb, C), F32)
        for j in range(4):
            tot = tot + jnp.where(s_ref[0] == j, own, r_ref[j].astype(F32))
        o_ref[...] = tot

    return pl.pallas_call(
        body, name=name,
        grid_spec=pltpu.PrefetchScalarGridSpec(
            num_scalar_prefetch=1, grid=(H // rb,),
            in_specs=[pl.BlockSpec((1, rb, C), lambda i, s_ref: (s_ref[0], i, 0)),
                      pl.BlockSpec((4, rb, C), lambda i, s_ref: (0, i, 0))],
            out_specs=pl.BlockSpec((rb, C), lambda i, s_ref: (i, 0))),
        out_shape=jax.ShapeDtypeStruct((H, C), F32), compiler_params=_cp("parallel"),
    )(s_arr, parts, land)


HBM_SPEC = pl.BlockSpec(memory_space=pltpu.HBM)
SEM_SPEC = pl.BlockSpec(memory_space=pltpu.SEMAPHORE)
EFFECT = pltpu.SideEffectType.DATAFLOW_SIDE_EFFECTING


def _split_start(name, bufs, n_sem, plan, dep):
    nb = len(bufs)

    def body(*refs):
        ins, send, recv, token = refs[:nb], refs[nb + 1], refs[nb + 2], refs[-1]
        for i, (src, dst, dev, _) in enumerate(plan(ins)):
            pltpu.make_async_remote_copy(src, dst, send.at[i], recv.at[i], device_id=dev, device_id_type=MESH).start()
        token[...] = jnp.zeros_like(token)

    outs = pl.pallas_call(
        body, name=name,
        out_shape=(pltpu.SemaphoreType.DMA((n_sem,)), pltpu.SemaphoreType.DMA((n_sem,)),
                   *[pltpu.HBM(b.shape, b.dtype) for b in bufs], jax.ShapeDtypeStruct((8, 128), F32)),
        in_specs=[HBM_SPEC] * nb + [pl.BlockSpec(memory_space=pl.ANY)],
        out_specs=(SEM_SPEC, SEM_SPEC, *([HBM_SPEC] * nb), pl.BlockSpec(memory_space=pltpu.VMEM)),
        input_output_aliases={i: 2 + i for i in range(nb)},
        compiler_params=pltpu.CompilerParams(has_side_effects=EFFECT),
    )(*[pltpu.with_memory_space_constraint(b, pltpu.HBM) for b in bufs], dep)
    return outs[0], outs[1], list(outs[2:2 + nb]), outs[-1]


def _split_wait(name, send, recv, bufs, after, plan):
    nb = len(bufs)
    after = list(after) if isinstance(after, (list, tuple)) else [after]

    def body(*refs):
        ins, send_s, recv_s = refs[:nb], refs[nb], refs[nb + 1]
        for i, (src, dst, dev, mine) in enumerate(plan(ins)):
            pltpu.make_async_remote_copy(src, dst, send_s.at[i], recv_s.at[i], device_id=dev,
                                         device_id_type=MESH).wait_send()
            pltpu.make_async_remote_copy(src, mine, send_s.at[i], recv_s.at[i], device_id=dev,
                                         device_id_type=MESH).wait_recv()

    outs = pl.pallas_call(
        body, name=name, out_shape=[pltpu.HBM(b.shape, b.dtype) for b in bufs],
        in_specs=[HBM_SPEC] * nb + [SEM_SPEC, SEM_SPEC] + [pl.BlockSpec(memory_space=pl.ANY)] * len(after),
        out_specs=[HBM_SPEC] * nb, input_output_aliases={i: i for i in range(nb)},
        compiler_params=pltpu.CompilerParams(has_side_effects=EFFECT),
    )(*bufs, send, recv, *after)
    return list(outs)


def _copies_now(name, bufs, n_sem, plan):
    nb = len(bufs)

    def body(*refs):
        ins, token, send, recv = refs[:nb], refs[2 * nb], refs[-2], refs[-1]
        token[...] = jnp.zeros_like(token)
        todo = plan(ins)
        for i, (src, dst, dev, _) in enumerate(todo):
            pltpu.make_async_remote_copy(src, dst, send.at[i], recv.at[i], device_id=dev, device_id_type=MESH).start()
        for i, (src, dst, dev, mine) in enumerate(todo):
            pltpu.make_async_remote_copy(src, mine, send.at[i], recv.at[i], device_id=dev, device_id_type=MESH).wait_recv()
        for i, (src, dst, dev, _) in enumerate(todo):
            pltpu.make_async_remote_copy(src, dst, send.at[i], recv.at[i], device_id=dev, device_id_type=MESH).wait_send()

    outs = pl.pallas_call(
        body, name=name,
        out_shape=[pltpu.HBM(b.shape, b.dtype) for b in bufs] + [jax.ShapeDtypeStruct((8, 128), F32)],
        in_specs=[HBM_SPEC] * nb, out_specs=[HBM_SPEC] * nb + [pl.BlockSpec(memory_space=pltpu.VMEM)],
        input_output_aliases={i: i for i in range(nb)},
        scratch_shapes=[pltpu.SemaphoreType.DMA((n_sem,)), pltpu.SemaphoreType.DMA((n_sem,))],
    )(*[pltpu.with_memory_space_constraint(b, pltpu.HBM) for b in bufs])
    return list(outs[:nb]), outs[nb]


def _slot(land, s, rows, cols):
    if cols is None:
        return land.at[s, rows]
    return land.at[rows, pl.ds(pl.multiple_of(s * cols, 128), cols)]


def _plan_gather_ici(cols):
    nw = len(cols)

    def plan(refs):
        x, y, c = _pos()
        my_s = 2 * x + y
        out = []
        for w in range(nw):
            mine = _half_rows(refs[w], c)
            for px, py, ps in _chips():
                out.append((refs[w].at[mine], _slot(refs[nw + w], my_s, mine, cols[w]), (px, py, c),
                            _slot(refs[nw + w], ps, mine, cols[w])))
        return out
    return plan


def _plan_gather_fwd(cols, rows):
    def plan(refs):
        x, y, c = _pos()
        out = []
        for w in range(len(cols)):
            half = rows[w] // 2
            mine = pl.ds(pl.multiple_of(c * half, 8), half)
            other = pl.ds(pl.multiple_of((1 - c) * half, 8), half)
            for px, py, ps in _chips():
                got = _slot(refs[w], ps, mine, cols[w])
                out.append((got, got, (x, y, 1 - c), _slot(refs[w], ps, other, cols[w])))
        return out
    return plan


def _plan_swap(nw):
    def plan(refs):
        x, y, c = _pos()
        return [(refs[w].at[:, _half_rows(refs[w], 1 - c)], refs[nw + w], (x, y, 1 - c), refs[nw + w])
                for w in range(nw)]
    return plan


def _plan_scatter(nw):
    def plan(refs):
        x, y, c = _pos()
        my_s = 2 * x + y
        out = []
        for w in range(nw):
            for px, py, ps in _chips():
                out.append((refs[w].at[ps], refs[nw + w].at[my_s], (px, py, c), refs[nw + w].at[ps]))
        return out
    return plan


def _plan_join(nw):
    def plan(refs):
        x, y, c = _pos()
        out = []
        for w in range(nw):
            land = refs[nw + w]
            out.append((refs[w], land.at[_half_rows(land, c)], (x, y, 1 - c), land.at[_half_rows(land, 1 - c)]))
        return out
    return plan


def _hbm_empty(shape, dtype):
    return pltpu.with_memory_space_constraint(lax.empty(shape, dtype), pltpu.HBM)


def _put_slot(land, own, slot):
    return lax.dynamic_update_slice(land, own[None], (slot,) + (0,) * own.ndim)


def _pad_lanes(a, n):
    return jnp.pad(a, ((0, 0), (0, n - a.shape[1])))


def kernel(x, c, positions, w_ada, b_ada, norm1_w, w_in, conv_w, conv_b, dt_bias, a_log, d_skip, attn_sinks, ssm_norm_w, w_out, norm2_w, w_gate_up, w_down, final_norm_w, loss_target, m_w_ada, m_b_ada, m_norm1_w, m_w_in, m_conv_w, m_conv_b, m_dt_bias, m_a_log, m_d_skip, m_attn_sinks, m_ssm_norm_w, m_w_out, m_norm2_w, m_w_gate_up, m_w_down, m_final_norm_w, v_w_ada, v_b_ada, v_norm1_w, v_w_in, v_conv_w, v_conv_b, v_dt_bias, v_a_log, v_d_skip, v_attn_sinks, v_ssm_norm_w, v_w_out, v_norm2_w, v_w_gate_up, v_w_down, v_final_norm_w):
    T = x.shape[1]
    tm = min(256, T)
    xi, yi, ci = lax.axis_index("x"), lax.axis_index("y"), lax.axis_index("c")
    my_s = 2 * xi + yi
    xs = x[0]
    tgt = loss_target[0]

    payload = jnp.concatenate([c, conv_w[0].reshape(1, CONVK * 256)], axis=1)
    gat, mod4, tok = _mod_exchange(payload, w_ada[0], b_ada.reshape(4, 1, 1536))
    mod6 = mod4.reshape(6, D)
    cw_dev = gat[:, 0, D:].reshape(4, 2, CONVK, 256)[:, 0]
    conv_full = cw_dev.transpose(1, 0, 2).reshape(CONVK, CONVC)

    w_in_b = w_in[0].astype(BF16)
    s_i, r_i, bufs, tok = _split_start("wgather_in_ici_start", [w_in_b, _hbm_empty((4,) + w_in_b.shape, BF16)], 3,
                                       _plan_gather_ici([None]), tok)
    inv_freq = (10000.0 ** (-jnp.arange(32, dtype=F32) / 32))
    cos, sin_s = _rope_tables(positions, inv_freq.reshape(32, 1), min(512, T), tok)
    late = [w_out[0].astype(BF16), w_gate_up[0].astype(BF16), w_down[0].astype(BF16)]
    bufs = _split_wait("wgather_in_ici_wait", s_i, r_i, bufs, [cos] + late, _plan_gather_ici([None]))
    bufs, tok = _copies_now("wgather_in_fwd", bufs[1:], 3, _plan_gather_fwd([None], [D]))
    g_in = _put_slot(bufs[0], w_in_b, my_s)
    w_pad = jnp.concatenate([g_in[0], g_in[1], g_in[2], g_in[3], jnp.zeros((D, IN_PAD - IN_PROJ), BF16)], axis=1)

    lands = [_hbm_empty((4, D // 4, D), BF16), _hbm_empty((D, 2 * DFF), BF16), _hbm_empty((4, DFF // 4, D), BF16)]
    cols3, rows3 = [None, GU_SH, None], [D // 4, D, DFF // 4]
    s_a, r_a, bufs, tok = _split_start("wgather_ici_start", late + lands, 9, _plan_gather_ici(cols3), tok)

    qkv, z, xbc, dtr, h1b = _in_proj_fwd(xs, cos, sin_s, mod6, norm1_w, w_pad, min(512, T), tok)
    sinks = attn_sinks
    attn, lse = _attn_fwd(qkv, sinks)
    bufs = _split_wait("wgather_ici_wait", s_a, r_a, bufs, attn, _plan_gather_ici(cols3))
    s_b, r_b, lands, tok = _split_start("wgather_fwd_start", bufs[3:], 9, _plan_gather_fwd(cols3, rows3), attn)
    dtb = _pad_lanes(dt_bias, 128)
    alog = _pad_lanes(a_log, 128)
    dskx = jnp.repeat(d_skip, HD, axis=1)
    mats = _ssd_mats()
    ynorm, ypre, states, conv_pre = _ssd_fwd(xbc, z, dtr, conv_full, conv_b, dtb, alog, dskx, ssm_norm_w, mats, tok)
    lands = _split_wait("wgather_fwd_wait", s_b, r_b, lands, ynorm, _plan_gather_fwd(cols3, rows3))
    w_out_f = _put_slot(lands[0], late[0], my_s).reshape(D, D)
    w_dn_f = _put_slot(lands[2], late[2], my_s).reshape(DFF, D)
    s_arr = my_s.reshape(1).astype(jnp.int32)

    fw2 = final_norm_w.reshape(1, D)
    sq, dmix, dx1, h2b, act, dfb, dgu, dob, sm_ffn = _mix_ffn(
        xs, attn, ynorm, tgt, mod6, norm2_w, fw2, w_out_f, lands[1], late[1], s_arr, w_dn_f, tm)

    tt = min(2048, T)
    c_arr = ci.reshape(1).astype(jnp.int32)
    tok0 = jnp.zeros((8, 128), F32)
    gw_dn4 = _tn_matmul(act, dfb, GU_SH, D, tt, "dw_down", tok0).reshape(4, DFF // 4, D)
    gw_gu4 = _tn_matmul(h2b, dgu, D, GU_SH, tt, "dw_gate_up", tok0)
    gw_out4 = jnp.concatenate(
        [_tn_matmul(attn, dob, AW, D, tt, "dw_out_a", tok0)[0],
         _tn_matmul(ynorm, dob, SW, D, tt, "dw_out_y", tok0)[0]], axis=0).reshape(4, D // 4, D)
    big1 = [gw_out4, gw_gu4, gw_dn4]
    rbs1 = [128, 512, 352]
    sib1 = [_hbm_empty((4, g.shape[1] // 2, g.shape[2]), F32) for g in big1]
    s_c, r_c, bufs, tok = _split_start("gswap_start", big1 + sib1, 3, _plan_swap(3), tok0)

    dzxd, d_cw, d_cb, d_sw, d_sk, d_dtb, d_av = _ssd_bwd(
        xbc, conv_pre, z, dtr, ypre, states, dmix, conv_full, dtb, alog, dskx, ssm_norm_w, mats, tok)
    bufs = _split_wait("gswap_wait", s_c, r_c, bufs, dzxd, _plan_swap(3))
    sums1 = [_add_half(g, s, c_arr, rb, "grad_add_%d" % i)
             for i, (g, s, rb) in enumerate(zip(bufs[:3], bufs[3:], rbs1))]
    land1 = [_hbm_empty(p.shape, BF16) for p in sums1]
    s_d, r_d, bufs, tok = _split_start("gscatter_start", sums1 + land1, 9, _plan_scatter(3), tok0)
    dqkv, d_sinks = _attn_bwd(qkv, sinks, lse, dmix, cos, sin_s, tok)
    bufs = _split_wait("gscatter_wait", s_d, r_d, bufs, dqkv, _plan_scatter(3))
    halves1 = [_sum4(p, l, s_arr, rb, "grad_sum_%d" % i)
               for i, (p, l, rb) in enumerate(zip(bufs[:3], bufs[3:], rbs1))]
    full1 = [_hbm_empty((2 * h.shape[0], h.shape[1]), F32) for h in halves1]
    s_e, r_e, bufs, tok = _split_start("gjoin_start", halves1 + full1, 3, _plan_join(3), tok0)
    gq = _tn_matmul(h1b, dqkv, D, 768, tt, "dw_in_qkv", tok)[0]
    gz = _tn_matmul(h1b, dzxd, D, 1664, tt, "dw_in_zxd", tok)[0]
    gw_in4 = jnp.stack([gq[:, :IN_SH], jnp.concatenate([gq[:, IN_SH:], gz[:, :2 * IN_SH - 768]], axis=1),
                        gz[:, 2 * IN_SH - 768:3 * IN_SH - 768], gz[:, 3 * IN_SH - 768:4 * IN_SH - 768]])
    joined1 = _split_wait("gjoin_wait", s_e, r_e, bufs, gw_in4, _plan_join(3))

    sib0 = _hbm_empty((4, D // 2, IN_SH), F32)
    s_f, r_f, bufs, tok = _split_start("gswap_in_start", [gw_in4, sib0], 1, _plan_swap(1), tok0)
    g_dn_s, d_dn, m_dn, v_dn = _adam_2d(w_down[0], joined1[2], joined1[5], m_w_down[0], v_w_down[0], c_arr, 352,
                                        "adam_w_down", tok)
    g_gu_s, d_gu, m_gu, v_gu = _adam_2d(w_gate_up[0], joined1[1], joined1[4], m_w_gate_up[0], v_w_gate_up[0], c_arr,
                                        256, "adam_w_gate_up", tok)
    g_out_s, d_out, m_out, v_out = _adam_2d(w_out[0], joined1[0], joined1[3], m_w_out[0], v_w_out[0], c_arr, 128,
                                            "adam_w_out", tok)
    bufs = _split_wait("gswap_in_wait", s_f, r_f, bufs, [d_dn, d_gu, d_out], _plan_swap(1))
    sum0 = _add_half(bufs[0], bufs[1], c_arr, 512, "grad_add_in")
    s_g, r_g, bufs, tok = _split_start("gscatter_in_start", [sum0, _hbm_empty(sum0.shape, BF16)], 3, _plan_scatter(1),
                                       tok0)
    grad_x, sm_in = _in_proj_bwd(xs, dx1, dqkv, dzxd, mod6, norm1_w, w_pad, min(512, T), tok)

    a_neg = -jnp.exp(alog)
    pieces = [sm_in[1:2], sm_in[2:3], sm_ffn[5:6], sm_ffn[2:3], sm_ffn[3:4], sm_ffn[4:5],
              sm_in[0:1], sm_ffn[1:2], sm_ffn[0:1], d_cb, d_cw.reshape(1, CONVK * CONVC),
              _pad_lanes(d_sw, SW), d_dtb, d_av * a_neg, d_sk, d_sinks,
              _pad_lanes((0.5 / D * jnp.sum(sq)).reshape(1, 1), 128)]
    vec = jnp.concatenate(pieces, axis=1)
    s_h, r_h, rows8, tok_small = _split_start("small_start", [vec, _hbm_empty((8,) + vec.shape, F32)], 7,
                                              _plan_small(), tok0)

    bufs = _split_wait("gscatter_in_wait", s_g, r_g, bufs, [grad_x, tok_small], _plan_scatter(1))
    half0 = _sum4(bufs[0], bufs[1], s_arr, 512, "grad_sum_in")
    joined0, _ = _copies_now("gjoin_in", [half0, _hbm_empty((D, IN_SH), F32)], 1, _plan_join(1))
    native = lambda a: a.transpose(2, 0, 1)
    g_in_s, d_in, m_in, v_in = [a.transpose(1, 2, 0) for a in _adam_w_in(
        native(w_in), joined0[0], joined0[1], native(m_w_in), native(v_w_in), c_arr)]
    rows8 = _split_wait("small_wait", s_h, r_h, rows8, [d_in], _plan_small())
    tot, allv = _small_sum(vec, rows8[1], (4 * xi + 2 * yi + ci).reshape(1).astype(jnp.int32))
    o = 0
    offs = []
    for p in pieces:
        offs.append(o)
        o += p.shape[1]
    seg = lambda i, n: (offs[i], n)
    g_conv_w = lax.dynamic_slice_in_dim(
        tot[:, offs[10]:offs[10] + CONVK * CONVC].reshape(CONVK, CONVC), my_s * 256, 256, axis=1)
    loss = tot[0, offs[16]]

    small_names = ["b_ada", "norm1_w", "conv_w", "conv_b", "dt_bias", "a_log", "d_skip", "attn_sinks", "ssm_norm_w",
                   "norm2_w", "final_norm_w"]
    small_g = [(0, 6 * D), seg(6, D), g_conv_w, seg(9, D), seg(12, 8), seg(13, 8), seg(14, 8), seg(15, 8),
               seg(11, SW), seg(7, D), seg(8, D)]
    as2d = lambda a: a.reshape(-1, a.shape[-1])
    small_w = [as2d(a) for a in (b_ada, norm1_w, conv_w, conv_b, dt_bias, a_log, d_skip, attn_sinks, ssm_norm_w,
                                 norm2_w, final_norm_w)]
    small_m = [as2d(a) for a in (m_b_ada, m_norm1_w, m_conv_w, m_conv_b, m_dt_bias, m_a_log, m_d_skip, m_attn_sinks,
                                 m_ssm_norm_w, m_norm2_w, m_final_norm_w)]
    small_v = [as2d(a) for a in (v_b_ada, v_norm1_w, v_conv_w, v_conv_b, v_dt_bias, v_a_log, v_d_skip, v_attn_sinks,
                                 v_ssm_norm_w, v_norm2_w, v_final_norm_w)]
    small_g, sd, smn, svn = _adam_small(tot, small_g, small_w, small_m, small_v)
    g_ada, d_ada, m_ada, v_ada = _adam_w_ada(gat, allv, s_arr, w_ada[0], m_w_ada[0], v_w_ada[0], 256)

    order = ["w_ada", "b_ada", "norm1_w", "w_in", "conv_w", "conv_b", "dt_bias", "a_log", "d_skip", "attn_sinks",
             "ssm_norm_w", "w_out", "norm2_w", "w_gate_up", "w_down", "final_norm_w"]
    shapes = dict(w_ada=w_ada.shape, b_ada=b_ada.shape, norm1_w=norm1_w.shape, w_in=w_in.shape, conv_w=conv_w.shape,
                  conv_b=conv_b.shape, dt_bias=dt_bias.shape, a_log=a_log.shape, d_skip=d_skip.shape,
                  attn_sinks=attn_sinks.shape, ssm_norm_w=ssm_norm_w.shape, w_out=w_out.shape, norm2_w=norm2_w.shape,
                  w_gate_up=w_gate_up.shape, w_down=w_down.shape, final_norm_w=final_norm_w.shape)
    grads = dict(w_ada=g_ada, w_in=g_in_s, w_out=g_out_s, w_gate_up=g_gu_s, w_down=g_dn_s)
    deltas = dict(w_ada=d_ada, w_in=d_in, w_out=d_out, w_gate_up=d_gu, w_down=d_dn)
    new_m = dict(w_ada=m_ada, w_in=m_in, w_out=m_out, w_gate_up=m_gu, w_down=m_dn)
    new_v = dict(w_ada=v_ada, w_in=v_in, w_out=v_out, w_gate_up=v_gu, w_down=v_dn)
    for i, nme in enumerate(small_names):
        grads[nme], deltas[nme], new_m[nme], new_v[nme] = small_g[i], sd[i], smn[i], svn[i]
    outs = [loss, grad_x[None]]
    for table in (grads, deltas, new_m, new_v):
        outs += [table[nme].reshape(shapes[nme]) for nme in order]
    return tuple(outs)
```

```python
import functools
import math

import jax
import jax.numpy as jnp
from jax import lax
from jax.experimental import pallas as pl
from jax.experimental.pallas import tpu as pltpu

F32 = jnp.float32
BF16 = jnp.bfloat16
HI = lax.Precision.HIGHEST
MESH = pl.DeviceIdType.MESH

D = 1024
HD = 64
AW = 512
SW = 512
NST = 128
CONVK = 4
CONVC = 1024
BLK = 128
CPS = 4
SSD_FWD_CPS = 8
ATTN_BPS = 8
IN_PROJ = 2312
IN_PAD = 2432
IN_SH = IN_PROJ // 4
DFF = 2816
GU_SH = 1408
FF_SPLITS = ((0, 1536), (1536, 2816))
EPS = 1e-6
NEG = -1e30
LR, B1, B2, AEPS, WD, STEP = 0.001, 0.9, 0.999, 1e-08, 0.01, 10
VMEM_LIMIT = 58 * 1024 * 1024


def _cp(*sem):
    return pltpu.CompilerParams(dimension_semantics=sem or None, vmem_limit_bytes=VMEM_LIMIT)


def _dot(a, b):
    return jnp.dot(a, b, preferred_element_type=F32)


def _dot_nt(a, b):
    return lax.dot_general(a, b, (((1,), (1,)), ((), ())), preferred_element_type=F32)


def _dot_tn(a, b):
    return lax.dot_general(a, b, (((0,), (0,)), ((), ())), preferred_element_type=F32)


def _dot_hi(a, b):
    return jnp.dot(a, b, precision=HI, preferred_element_type=F32)


def _sigmoid(x):
    return 1.0 / (1.0 + jnp.exp(-x))


def _iota(shape, dim):
    return lax.broadcasted_iota(jnp.int32, shape, dim)


def _load_resident(hbm_ref, vmem_ref, sem):
    @pl.when(pl.program_id(0) == 0)
    def _():
        cp = pltpu.make_async_copy(hbm_ref, vmem_ref, sem)
        cp.start()
        cp.wait()


def _swap32(t):
    lane = _iota(t.shape, 1)
    return jnp.where((lane & 63) < 32, pltpu.roll(t, 96, 1), pltpu.roll(t, 32, 1))


def _rope_fwd(t, cos, sin_s):
    return t * cos + _swap32(t) * sin_s


def _rope_bwd(t, cos, sin_s):
    return t * cos - _swap32(t) * sin_s


DEP_SPEC = pl.BlockSpec((8, 128), lambda *_: (0, 0))


def _rope_tables(pos_row, inv_freq_col, tm, dep):
    T = pos_row.shape[1]
    lane, row = jnp.arange(128)[None, :], jnp.arange(96)[:, None]
    pick = (lane % 32) == (row % 32)
    sel_cos = pick.astype(BF16)
    sel_sin = jnp.where(pick, jnp.where(lane % 64 < 32, -1.0, 1.0), 0.0).astype(BF16)

    def body(p_ref, f_ref, sc_ref, ss_ref, dep_ref, cos_ref, sin_ref):
        ang = f_ref[...] * p_ref[...].astype(F32)
        cos_ref[...] = _dot_tn(_pieces(jnp.cos(ang), 3, 0), sc_ref[...])
        sin_ref[...] = _dot_tn(_pieces(jnp.sin(ang), 3, 0), ss_ref[...])

    full = lambda a: pl.BlockSpec(a.shape, lambda i: (0,) * a.ndim)
    return pl.pallas_call(
        body, name="rope_tables", grid=(T // tm,),
        in_specs=[pl.BlockSpec((1, tm), lambda i: (0, i)), full(inv_freq_col), full(sel_cos), full(sel_sin), DEP_SPEC],
        out_specs=[pl.BlockSpec((tm, 128), lambda i: (i, 0))] * 2,
        out_shape=[jax.ShapeDtypeStruct((T, 128), F32)] * 2,
        compiler_params=_cp("parallel"),
    )(pos_row, inv_freq_col, sel_cos, sel_sin, dep)


def _in_proj_fwd(x, cos, sin_s, mod6, norm1_w, w_pad, tm, dep):
    T = x.shape[0]

    def body(x_ref, cos_ref, sin_ref, mod_ref, nw_ref, w_hbm, dep_ref, qkv_ref, z_ref, xbc_ref, dt_ref, h_ref, w_vmem,
             sem):
        _load_resident(w_hbm, w_vmem, sem)
        xv = x_ref[...]
        r = lax.rsqrt(jnp.mean(xv * xv, axis=-1, keepdims=True) + EPS)
        h = (xv * r * nw_ref[...]) * (1.0 + mod_ref[1:2, :]) + mod_ref[0:1, :]
        hb = h.astype(BF16)
        h_ref[...] = hb
        proj = _dot(hb, w_vmem[...])
        cs, sn = cos_ref[...], sin_ref[...]
        for j in range(5):
            qkv_ref[:, 128 * j:128 * (j + 1)] = _rope_fwd(proj[:, 128 * j:128 * (j + 1)], cs, sn).astype(BF16)
        qkv_ref[:, 640:768] = proj[:, 640:768].astype(BF16)
        z_ref[...] = proj[:, 768:1280]
        xbc_ref[...] = proj[:, 1280:2304]
        dt_ref[...] = proj[:, 2304:2432]

    row = lambda w: pl.BlockSpec((tm, w), lambda i: (i, 0))
    full = lambda a: pl.BlockSpec(a.shape, lambda i: (0,) * a.ndim)
    return pl.pallas_call(
        body, name="in_proj_fwd", grid=(T // tm,),
        in_specs=[row(D), row(128), row(128), full(mod6), full(norm1_w), pl.BlockSpec(memory_space=pl.ANY), DEP_SPEC],
        out_specs=[row(768), row(512), row(1024), row(128), row(D)],
        out_shape=[jax.ShapeDtypeStruct((T, 768), BF16), jax.ShapeDtypeStruct((T, 512), F32),
                   jax.ShapeDtypeStruct((T, 1024), F32), jax.ShapeDtypeStruct((T, 128), F32),
                   jax.ShapeDtypeStruct((T, D), BF16)],
        scratch_shapes=[pltpu.VMEM((D, IN_PAD), BF16), pltpu.SemaphoreType.DMA],
        compiler_params=_cp("arbitrary"),
    )(x, cos, sin_s, mod6, norm1_w, w_pad, dep)


def _head_variants(pair, j):
    lane = _iota(pair.shape, 1)
    lo = lane < 64
    kv = j // 2
    ev = jnp.where(lo, pair, 0.0)
    od = jnp.where(lo, 0.0, pair)
    if kv == 0:
        od = pltpu.roll(od, 64, 1)
    else:
        ev = pltpu.roll(ev, 64, 1)
    return ev.astype(BF16), od.astype(BF16)


def _kv_variants(vcat):
    lane = _iota(vcat.shape, 1)
    lo = lane < 64
    v0 = jnp.where(lo, vcat, 0.0)
    v1 = jnp.where(lo, 0.0, vcat)
    out = {
        (0, 0): v0, (0, 1): pltpu.roll(v0, 64, 1),
        (1, 0): pltpu.roll(v1, 64, 1), (1, 1): v1,
    }
    return {k: v.astype(BF16) for k, v in out.items()}


def _fold_masks(n):
    upper = _iota((BLK, BLK), 1) > _iota((BLK, BLK), 0)
    return upper, upper & (n == 0)


def _attn_fwd(qkv, sinks):
    CPS = ATTN_BPS
    T = qkv.shape[0]
    nsteps = T // (CPS * BLK)

    def body(sink_ref, q_ref, kc_ref, kp_ref, vc_ref, vp_ref, o_ref, lse_ref):
        for sub in range(CPS):
            rows, before = slice(BLK * sub, BLK * (sub + 1)), slice(BLK * (sub - 1), BLK * sub)
            block(pl.program_id(0) * CPS + sub, sink_ref, q_ref.at[rows, :], kc_ref.at[rows, :],
                  kp_ref if sub == 0 else kc_ref.at[before, :], vc_ref.at[rows, :],
                  vp_ref if sub == 0 else vc_ref.at[before, :], o_ref.at[rows, :], lse_ref.at[rows, :])

    def block(n, sink_ref, q_ref, kc_ref, kp_ref, vc_ref, vp_ref, o_ref, lse_ref):
        vpv = _kv_variants(vp_ref[...].astype(F32))
        vcv = _kv_variants(vc_ref[...].astype(F32))
        q_all = jnp.concatenate(
            [v for j in range(4) for v in _head_variants(q_ref[:, 128 * j:128 * (j + 1)].astype(F32), j)], axis=0)
        s_prev = _dot_nt(q_all, kp_ref[...])
        s_cur = _dot_nt(q_all, kc_ref[...])
        upper, dead = _fold_masks(n)
        lane = _iota((BLK, 128), 1)
        lse_acc = jnp.zeros((BLK, 128), F32)
        for jj in range(4):
            acc = jnp.zeros((BLK, 128), F32)
            for par in range(2):
                h = 2 * jj + par
                rows = slice(h * BLK, (h + 1) * BLK)
                sink = sink_ref[0, h]
                s = jnp.where(dead, NEG, jnp.where(upper, s_prev[rows], s_cur[rows]) * 0.125)
                m = jnp.maximum(jnp.max(s, axis=1, keepdims=True), sink)
                p = jnp.exp(s - m)
                den = jnp.sum(p, axis=1, keepdims=True) + jnp.exp(sink - m)
                pn = p * (1.0 / den)
                acc = (acc + _dot(jnp.where(upper, pn, 0.0).astype(BF16), vpv[(jj // 2, par)])
                       + _dot(jnp.where(upper, 0.0, pn).astype(BF16), vcv[(jj // 2, par)]))
                lse_acc = jnp.where(lane == h, m + jnp.log(den), lse_acc)
            o_ref[:, 128 * jj:128 * (jj + 1)] = acc.astype(BF16)
        lse_ref[...] = lse_acc

    RB = CPS * BLK
    prev = lambda n: jnp.maximum(n * CPS - 1, 0)
    return pl.pallas_call(
        body, name="attn_fwd", grid=(nsteps,),
        in_specs=[pl.BlockSpec(memory_space=pltpu.SMEM),
                  pl.BlockSpec((RB, 512), lambda n: (n, 0)),
                  pl.BlockSpec((RB, 128), lambda n: (n, 4)),
                  pl.BlockSpec((BLK, 128), lambda n: (prev(n), 4)),
                  pl.BlockSpec((RB, 128), lambda n: (n, 5)),
                  pl.BlockSpec((BLK, 128), lambda n: (prev(n), 5))],
        out_specs=[pl.BlockSpec((RB, 512), lambda n: (n, 0)), pl.BlockSpec((RB, 128), lambda n: (n, 0))],
        out_shape=[jax.ShapeDtypeStruct((T, 512), BF16), jax.ShapeDtypeStruct((T, 128), F32)],
        compiler_params=_cp("parallel"),
    )(sinks, qkv, qkv, qkv, qkv, qkv)


def _attn_bwd(qkv, sinks, lse, dmix, cos, sin_s, dep):
    T = qkv.shape[0]
    nb = T // BLK

    def body(sink_ref, q_ref, kc_ref, kp_ref, vc_ref, vp_ref, lse_ref, do_ref, cq_ref, sq_ref, ck_ref, sk_ref,
             dep_ref, out_ref, ds_ref, dq_car, dk_car, dv_car):
        n = pl.program_id(0)
        lane = _iota((BLK, 128), 1)

        @pl.when(n == 0)
        def _():
            ds_ref[...] = jnp.zeros_like(ds_ref)
            dq_car[...] = jnp.zeros_like(dq_car)
            dk_car[...] = jnp.zeros_like(dk_car)
            dv_car[...] = jnp.zeros_like(dv_car)

        @pl.when(n < nb)
        def _():
            kp, kc, vp, vc = kp_ref[...], kc_ref[...], vp_ref[...], vc_ref[...]
            kpv = _kv_variants(kp.astype(F32))
            kcv = _kv_variants(kc.astype(F32))
            lse_v = lse_ref[...]
            q_all = jnp.concatenate(
                [v for j in range(4) for v in _head_variants(q_ref[:, 128 * j:128 * (j + 1)].astype(F32), j)], axis=0)
            do_all = jnp.concatenate(
                [v for j in range(4) for v in _head_variants(do_ref[:, 128 * j:128 * (j + 1)], j)], axis=0)
            s_prev, s_cur = _dot_nt(q_all, kp), _dot_nt(q_all, kc)
            dp_prev, dp_cur = _dot_nt(do_all, vp), _dot_nt(do_all, vc)
            upper, dead = _fold_masks(n)
            out_ref[:, 0:512] = dq_car[...]
            dsk = jnp.zeros((1, 128), F32)
            ds_u, ds_l, p_u, p_l = [], [], [], []
            for jj in range(4):
                dq_acc = jnp.zeros((BLK, 128), F32)
                for par in range(2):
                    h = 2 * jj + par
                    rows = slice(h * BLK, (h + 1) * BLK)
                    lse_h = jnp.sum(jnp.where(lane == h, lse_v, 0.0), axis=1, keepdims=True)
                    s = jnp.where(dead, NEG, jnp.where(upper, s_prev[rows], s_cur[rows]) * 0.125)
                    p = jnp.exp(s - lse_h)
                    dp = jnp.where(upper, dp_prev[rows], dp_cur[rows])
                    delta = jnp.sum(p * dp, axis=1, keepdims=True)
                    ds = p * (dp - delta) * 0.125
                    dsu, dsl = jnp.where(upper, ds, 0.0).astype(BF16), jnp.where(upper, 0.0, ds).astype(BF16)
                    dq_acc = dq_acc + _dot(dsu, kpv[(jj // 2, par)]) + _dot(dsl, kcv[(jj // 2, par)])
                    ds_u.append(dsu)
                    ds_l.append(dsl)
                    p_u.append(jnp.where(upper, p, 0.0).astype(BF16))
                    p_l.append(jnp.where(upper, 0.0, p).astype(BF16))
                    dsk = dsk + jnp.where(lane[0:1] == h, -jnp.sum(jnp.exp(sink_ref[0, h] - lse_h) * delta), 0.0)
                dq_car[:, 128 * jj:128 * (jj + 1)] = _rope_bwd(dq_acc, cq_ref[...], sq_ref[...]).astype(BF16)
            stack = lambda parts: jnp.concatenate(parts, axis=0)
            dk_prev, dk_cur = _dot_tn(stack(ds_u), q_all), _dot_tn(stack(ds_l), q_all)
            dv_prev, dv_cur = _dot_tn(stack(p_u), do_all), _dot_tn(stack(p_l), do_all)
            ds_ref[...] += dsk
            out_ref[:, 512:640] = _rope_bwd(dk_car[...] + dk_prev, ck_ref[...], sk_ref[...]).astype(BF16)
            out_ref[:, 640:768] = (dv_car[...] + dv_prev).astype(BF16)
            dk_car[...] = dk_cur
            dv_car[...] = dv_cur

        @pl.when(n == nb)
        def _():
            out_ref[:, 0:512] = dq_car[...]
            out_ref[:, 512:640] = _rope_bwd(dk_car[...], ck_ref[...], sk_ref[...]).astype(BF16)
            out_ref[:, 640:768] = dv_car[...].astype(BF16)

    cur = lambda n: jnp.minimum(n, nb - 1)
    prev = lambda n: jnp.maximum(cur(n) - 1, 0)
    outb = lambda n: jnp.maximum(n - 1, 0)
    return pl.pallas_call(
        body, name="attn_bwd", grid=(nb + 1,),
        in_specs=[pl.BlockSpec(memory_space=pltpu.SMEM),
                  pl.BlockSpec((BLK, 512), lambda n: (cur(n), 0)),
                  pl.BlockSpec((BLK, 128), lambda n: (cur(n), 4)),
                  pl.BlockSpec((BLK, 128), lambda n: (prev(n), 4)),
                  pl.BlockSpec((BLK, 128), lambda n: (cur(n), 5)),
                  pl.BlockSpec((BLK, 128), lambda n: (prev(n), 5)),
                  pl.BlockSpec((BLK, 128), lambda n: (cur(n), 0)),
                  pl.BlockSpec((BLK, 512), lambda n: (cur(n), 0)),
                  pl.BlockSpec((BLK, 128), lambda n: (cur(n), 0)),
                  pl.BlockSpec((BLK, 128), lambda n: (cur(n), 0)),
                  pl.BlockSpec((BLK, 128), lambda n: (outb(n), 0)),
                  pl.BlockSpec((BLK, 128), lambda n: (outb(n), 0)), DEP_SPEC],
        out_specs=[pl.BlockSpec((BLK, 768), lambda n: (outb(n), 0)), pl.BlockSpec((1, 128), lambda n: (0, 0))],
        out_shape=[jax.ShapeDtypeStruct((T, 768), BF16), jax.ShapeDtypeStruct((1, 128), F32)],
        scratch_shapes=[pltpu.VMEM((BLK, 512), BF16), pltpu.VMEM((BLK, 128), F32), pltpu.VMEM((BLK, 128), F32)],
        compiler_params=_cp("arbitrary"),
    )(sinks, qkv, qkv, qkv, qkv, qkv, lse, dmix, cos, sin_s, cos, sin_s, dep)


def _ssd_mats():
    e = jnp.arange(SW)[None, :] // HD == jnp.arange(128)[:, None]
    tri = jnp.arange(BLK)[None, :] <= jnp.arange(BLK)[:, None]
    return (jnp.tile(e, (3, 1)).astype(BF16), jnp.tile(e.T, (2, 1)).astype(BF16),
            jnp.tile(tri, (1, 3)).astype(BF16), jnp.tile(tri.T, (1, 3)).astype(BF16))


def _pieces(x, n, axis):
    out, r = [], x
    for i in range(n):
        p = r.astype(BF16)
        out.append(p)
        if i + 1 < n:
            r = r - p.astype(F32)
    return jnp.concatenate(out, axis=axis)


def _expand(x, e3):
    return _dot(_pieces(x, 3, 1), e3)


def _head_sums(x, et2):
    return _dot(_pieces(x, 2, 1), et2)


def _run_sum(tri3, x):
    return _dot(tri3, _pieces(x, 3, 0))


def _shift_down(u, tail, j):
    rolled = pltpu.roll(u, j, 0)
    first = jnp.where(_iota(tail.shape, 0) < j, pltpu.roll(tail, j, 0), rolled[0:8])
    return jnp.concatenate([first, rolled[8:]], axis=0)


def _shift_up(d, head, j):
    rolled = pltpu.roll(d, BLK - j, 0)
    last = jnp.where(_iota(head.shape, 0) >= 8 - j, pltpu.roll(head, 8 - j, 0), rolled[BLK - 8:])
    return jnp.concatenate([rolled[:BLK - 8], last], axis=0)


def _ssd_parts(dtr, dtb, alog, e3, tril3):
    xx = dtr + dtb
    dt = jnp.maximum(xx, 0.0) + jnp.log(1.0 + jnp.exp(-jnp.abs(xx)))
    a_neg = -jnp.exp(alog)
    tril = _iota((BLK, BLK), 1) <= _iota((BLK, BLK), 0)
    cs = _run_sum(tril3, dt * a_neg)
    csx = _expand(cs, e3)
    last = csx[BLK - 1:BLK, :]
    return dict(xx=xx, dt=dt, a_neg=a_neg, tril=tril, cs=cs, cs_t=cs.T,
                ecsx=jnp.exp(csx), dtex=jnp.exp(last - csx), cdx=jnp.exp(last), dtx=_expand(dt, e3))


def _decay(parts, h):
    seg = parts["cs"][:, h:h + 1] - parts["cs_t"][h:h + 1, :]
    return jnp.exp(jnp.where(parts["tril"], seg, NEG))


def _group_cols(a, g):
    return a[:, 256 * g:256 * (g + 1)]


def _ssd_fwd(xbc, z, dtr, conv_w, conv_b, dtb, alog, dskx, ssm_w, mats, dep):
    CPS = SSD_FWD_CPS
    T = xbc.shape[0]
    nc = T // BLK

    def body(u_ref, tail_ref, z_ref, dtr_ref, cw_ref, cb_ref, dtb_ref, al_ref, dk_ref, sw_ref, e3_ref, tril3_ref,
             dep_ref, yn_ref, yp_ref, st_ref, co_ref, s_scr):
        n = pl.program_id(0)

        @pl.when(n == 0)
        def _():
            s_scr[...] = jnp.zeros_like(s_scr)

        lane = _iota((BLK, 128), 1)
        lo = lane < 64
        for sub in range(CPS):
            rows = slice(BLK * sub, BLK * (sub + 1))
            u = u_ref[rows, :]
            tail = jnp.where(n > 0, tail_ref[...], 0.0) if sub == 0 else u_ref[BLK * sub - 8:BLK * sub, :]
            co = cb_ref[...] + cw_ref[3:4, :] * u
            for j in range(1, CONVK):
                co = co + cw_ref[3 - j:4 - j, :] * _shift_down(u, tail, j)
            co_ref[rows, :] = co
            xc = co * _sigmoid(co)
            pt = _ssd_parts(dtr_ref[rows, :], dtb_ref[...], al_ref[...], e3_ref[...], tril3_ref[...])
            xs = xc[:, :SW]
            bm = [xc[:, 512:640].astype(BF16), xc[:, 640:768].astype(BF16)]
            cm = [xc[:, 768:896].astype(BF16), xc[:, 896:1024].astype(BF16)]
            s_in = s_scr[...]
            st_ref[sub] = s_in
            xdt = xs * pt["dtx"]
            xde = (xdt * pt["dtex"]).astype(BF16)
            ys, s_new = [], []
            for g in range(2):
                cb = _dot_nt(cm[g], bm[g])
                yoff = _dot(cm[g], _group_cols(s_in, g).astype(BF16))
                s_new.append(_dot_tn(bm[g], _group_cols(xde, g)))
                for jj in range(2):
                    j = 2 * g + jj
                    chunk = xdt[:, 128 * j:128 * (j + 1)]
                    g_ev = (cb * _decay(pt, 2 * j)).astype(BF16)
                    g_od = (cb * _decay(pt, 2 * j + 1)).astype(BF16)
                    yd = (_dot(g_ev, jnp.where(lo, chunk, 0.0).astype(BF16))
                          + _dot(g_od, jnp.where(lo, 0.0, chunk).astype(BF16)))
                    ys.append(yd + yoff[:, 128 * jj:128 * (jj + 1)] * pt["ecsx"][:, 128 * j:128 * (j + 1)])
            y = jnp.concatenate(ys, axis=1) + xs * dk_ref[...]
            s_scr[...] = s_in * pt["cdx"] + jnp.concatenate(s_new, axis=1)
            yp_ref[rows, :] = y
            zv = z_ref[rows, :]
            yz = y * (zv * _sigmoid(zv))
            outs = []
            for g in range(2):
                yg = _group_cols(yz, g)
                outs.append(yg * lax.rsqrt(jnp.mean(yg * yg, axis=-1, keepdims=True) + EPS))
            yn_ref[rows, :] = (jnp.concatenate(outs, axis=1) * sw_ref[...]).astype(BF16)

    e3, _, tril3, _ = mats
    RB = CPS * BLK
    tail8 = lambda n: jnp.maximum(n * (RB // 8) - 1, 0)
    full = lambda a: pl.BlockSpec(a.shape, lambda n: (0,) * a.ndim)
    return pl.pallas_call(
        body, name="ssd_fwd", grid=(nc // CPS,),
        in_specs=[pl.BlockSpec((RB, CONVC), lambda n: (n, 0)), pl.BlockSpec((8, CONVC), lambda n: (tail8(n), 0)),
                  pl.BlockSpec((RB, SW), lambda n: (n, 0)), pl.BlockSpec((RB, 128), lambda n: (n, 0)),
                  full(conv_w), full(conv_b), full(dtb), full(alog), full(dskx), full(ssm_w), full(e3), full(tril3),
                  DEP_SPEC],
        out_specs=[pl.BlockSpec((RB, SW), lambda n: (n, 0)), pl.BlockSpec((RB, SW), lambda n: (n, 0)),
                   pl.BlockSpec((CPS, NST, SW), lambda n: (n, 0, 0)), pl.BlockSpec((RB, CONVC), lambda n: (n, 0))],
        out_shape=[jax.ShapeDtypeStruct((T, SW), BF16), jax.ShapeDtypeStruct((T, SW), F32),
                   jax.ShapeDtypeStruct((nc, NST, SW), F32), jax.ShapeDtypeStruct((T, CONVC), F32)],
        scratch_shapes=[pltpu.VMEM((NST, SW), F32)],
        compiler_params=_cp("arbitrary"),
    )(xbc, xbc, z, dtr, conv_w, conv_b, dtb, alog, dskx, ssm_w, e3, tril3, dep)


def _ssd_bwd(xbc, co_all, z, dtr, ypre, states, dmix, conv_w, dtb, alog, dskx, ssm_w, mats, dep):
    T = xbc.shape[0]
    nsteps = T // (CPS * BLK)

    def body(*refs):
        per_chunk, consts, out_ref, carried = refs[:7], refs[7:16], refs[17], refs[18:]
        i = pl.program_id(0)

        @pl.when(i == 0)
        def _():
            for r in carried:
                r[...] = jnp.zeros_like(r)

        for sub in reversed(range(CPS)):
            rows = slice(BLK * sub, BLK * (sub + 1))
            views = [r.at[sub:sub + 1] if k == 5 else r.at[rows, :] for k, r in enumerate(per_chunk)]
            chunk(*views, *consts, out_ref.at[rows, :], *carried)

        @pl.when(i == nsteps - 1)
        def _():
            dsk_ref, dskx_scr = carried[3], carried[8]
            dsk_ref[...] = _head_sums(jnp.broadcast_to(dskx_scr[...], (8, SW)), consts[6][...])[0:1]

    def chunk(u_ref, co_ref, z_ref, dtr_ref, yp_ref, st_ref, dyn_ref, cw_ref, dtb_ref, al_ref, dk_ref, sw_ref,
              e3_ref, et2_ref, tril3_ref, triu3_ref,
              out_ref, dcw_ref, dcb_ref, dsw_ref, dsk_ref, ddtb_ref, dav_ref, ds_scr, dco_scr, dskx_scr):
        co = co_ref[...]
        sg = _sigmoid(co)
        xc = co * sg
        pt = _ssd_parts(dtr_ref[...], dtb_ref[...], al_ref[...], e3_ref[...], tril3_ref[...])
        dtx, ecsx, dtex, cdx = pt["dtx"], pt["ecsx"], pt["dtex"], pt["cdx"]
        xs = xc[:, :SW]
        bm = [xc[:, 512:640].astype(BF16), xc[:, 640:768].astype(BF16)]
        cm = [xc[:, 768:896].astype(BF16), xc[:, 896:1024].astype(BF16)]
        s_in = st_ref[0]
        ds_out = ds_scr[...]
        e_t = et2_ref[...]

        zv = z_ref[...]
        sz = _sigmoid(zv)
        silu_z = zv * sz
        ypre = yp_ref[...]
        yz = ypre * silu_z
        dyn = dyn_ref[...]
        sw = sw_ref[...]
        dyz, yns = [], []
        for g in range(2):
            yg = _group_cols(yz, g)
            r = lax.rsqrt(jnp.mean(yg * yg, axis=-1, keepdims=True) + EPS)
            yn = yg * r
            dg = _group_cols(dyn, g) * _group_cols(sw, g)
            dyz.append(r * (dg - yn * jnp.mean(dg * yn, axis=-1, keepdims=True)))
            yns.append(yn)
        dyz = jnp.concatenate(dyz, axis=1)
        dsw_ref[...] += jnp.sum(dyn * jnp.concatenate(yns, axis=1), axis=0, keepdims=True)
        dy = dyz * silu_z
        dz = dyz * ypre * (sz * (1.0 + zv * (1.0 - sz)))

        xdt = xs * dtx
        xdt_b = xdt.astype(BF16)
        edy = (ecsx * dy).astype(BF16)
        xde = (xdt * dtex).astype(BF16)
        lane = _iota((BLK, 128), 1)
        lo = lane < 64
        row8 = _iota((8, 128), 0)
        dcs = jnp.zeros((BLK, 128), F32)
        col_rows = jnp.zeros((8, 128), F32)
        dxdt, bds, yoff, dbs, dcs_g, ds_new = [], [], [], [], [], []
        for g in range(2):
            s_g = _group_cols(s_in, g).astype(BF16)
            dso_g = _group_cols(ds_out, g).astype(BF16)
            cb = _dot_nt(cm[g], bm[g])
            bds.append(_dot(bm[g], dso_g))
            yoff.append(_dot(cm[g], s_g))
            dcb_g = jnp.zeros((BLK, BLK), F32)
            for jj in range(2):
                j = 2 * g + jj
                dy_c = dy[:, 128 * j:128 * (j + 1)]
                xdt_c = xdt_b[:, 128 * j:128 * (j + 1)]
                acc = jnp.zeros((BLK, 128), F32)
                for par in range(2):
                    h = 2 * j + par
                    lm = _decay(pt, h)
                    gm = cb * lm
                    dy_m = (jnp.where(lo, dy_c, 0.0) if par == 0 else jnp.where(lo, 0.0, dy_c)).astype(BF16)
                    dg_h = _dot_nt(dy_m, xdt_c)
                    w_h = dg_h * gm
                    dcs = dcs + jnp.where(lane == h, jnp.sum(w_h, axis=1, keepdims=True), 0.0)
                    col_rows = col_rows + jnp.where(row8 == h, jnp.sum(w_h, axis=0, keepdims=True), 0.0)
                    dcb_g = dcb_g + dg_h * lm
                    acc = acc + _dot_tn(gm.astype(BF16), dy_m)
                dxdt.append(acc)
            dcb_b = dcb_g.astype(BF16)
            dcs_g.append(_dot(dcb_b, bm[g]) + _dot_nt(_group_cols(edy, g), s_g))
            dbs.append(_dot_tn(dcb_b, cm[g]) + _dot_nt(_group_cols(xde, g), dso_g))
            ds_new.append(_dot_tn(cm[g], _group_cols(edy, g)))
        bds = jnp.concatenate(bds, axis=1)
        yoff = jnp.concatenate(yoff, axis=1) * ecsx
        dxdt = jnp.concatenate(dxdt, axis=1) + dtex * bds
        ds_scr[...] = cdx * ds_out + jnp.concatenate(ds_new, axis=1)

        t_m = _head_sums(dtex * xdt * bds, e_t)
        colsum_t = jnp.concatenate([col_rows, jnp.zeros((BLK - 8, 128), F32)], axis=0).T
        cd = jnp.exp(pt["cs"][BLK - 1:BLK, :])
        sds = jnp.sum(s_in * ds_out, axis=0, keepdims=True)
        last_row = jnp.sum(t_m, axis=0, keepdims=True) + cd * _head_sums(jnp.broadcast_to(sds, (8, SW)), e_t)[0:1]
        dcs = dcs - colsum_t + _head_sums(dy * yoff, e_t) - t_m
        dcs = dcs + jnp.where(_iota((BLK, 128), 0) == BLK - 1, last_row, 0.0)
        da = _run_sum(triu3_ref[...], dcs)
        dt = pt["dt"]
        ddt = da * pt["a_neg"] + _head_sums(dxdt * xs, e_t)
        dav_ref[...] += jnp.sum(da * dt, axis=0, keepdims=True)
        ddtr = ddt * _sigmoid(pt["xx"])
        ddtb_ref[...] += jnp.sum(ddtr, axis=0, keepdims=True)
        dxs = dxdt * dtx + dy * dk_ref[...]
        dskx_scr[...] += jnp.sum(dy * xs, axis=0, keepdims=True)
        dxc = jnp.concatenate([dxs, dbs[0], dbs[1], dcs_g[0], dcs_g[1]], axis=1)
        dco = dxc * (sg * (1.0 + co * (1.0 - sg)))

        dcb_ref[...] += jnp.sum(dco, axis=0, keepdims=True)
        u = u_ref[...]
        head = dco_scr[...]
        du = jnp.zeros_like(dco)
        for j in range(CONVK):
            up_j = dco if j == 0 else _shift_up(dco, head, j)
            dcw_ref[3 - j:4 - j, :] += jnp.sum(up_j * u, axis=0, keepdims=True)
            du = du + cw_ref[3 - j:4 - j, :] * up_j
        dco_scr[...] = dco[0:8]
        out_ref[:, 0:512] = dz.astype(BF16)
        out_ref[:, 512:1536] = du.astype(BF16)
        out_ref[:, 1536:1664] = ddtr.astype(BF16)

    e3, et2, tril3, triu3 = mats
    RB = CPS * BLK
    rev = lambda i: nsteps - 1 - i
    full = lambda a: pl.BlockSpec(a.shape, lambda i: (0,) * a.ndim)
    acc = lambda r, c: pl.BlockSpec((r, c), lambda i: (0, 0))
    return pl.pallas_call(
        body, name="ssd_bwd", grid=(nsteps,),
        in_specs=[pl.BlockSpec((RB, CONVC), lambda i: (rev(i), 0)), pl.BlockSpec((RB, CONVC), lambda i: (rev(i), 0)),
                  pl.BlockSpec((RB, SW), lambda i: (rev(i), 0)), pl.BlockSpec((RB, 128), lambda i: (rev(i), 0)),
                  pl.BlockSpec((RB, SW), lambda i: (rev(i), 0)), pl.BlockSpec((CPS, NST, SW), lambda i: (rev(i), 0, 0)),
                  pl.BlockSpec((RB, SW), lambda i: (rev(i), 1)),
                  full(conv_w), full(dtb), full(alog), full(dskx), full(ssm_w),
                  full(e3), full(et2), full(tril3), full(triu3), DEP_SPEC],
        out_specs=[pl.BlockSpec((RB, 1664), lambda i: (rev(i), 0)),
                   acc(CONVK, CONVC), acc(1, CONVC), acc(1, SW), acc(1, 128), acc(1, 128), acc(1, 128)],
        out_shape=[jax.ShapeDtypeStruct((T, 1664), BF16),
                   jax.ShapeDtypeStruct((CONVK, CONVC), F32), jax.ShapeDtypeStruct((1, CONVC), F32),
                   jax.ShapeDtypeStruct((1, SW), F32), jax.ShapeDtypeStruct((1, 128), F32),
                   jax.ShapeDtypeStruct((1, 128), F32), jax.ShapeDtypeStruct((1, 128), F32)],
        scratch_shapes=[pltpu.VMEM((NST, SW), F32), pltpu.VMEM((8, CONVC), F32), pltpu.VMEM((1, SW), F32)],
        compiler_params=_cp("arbitrary"),
    )(xbc, co_all, z, dtr, ypre, states, dmix, conv_w, dtb, alog, dskx, ssm_w, e3, et2, tril3, triu3, dep)


def _mix_ffn(x, attn, ynorm, tgt, mod6, norm2_w, final_w, w_out, w_gu, w_gu_own, s_arr, w_dn, tm):
    T = x.shape[0]
    nt = T // tm

    def body(x_ref, a_ref, y_ref, t_ref, mod_ref, n2_ref, fw_ref, wo_hbm, wgu_hbm, own_hbm, s_ref, wdn_hbm,
             sq_ref, dmix_ref, dx1_ref, h2_ref, act_ref, df_ref, dgu_ref, do_ref, sm_ref,
             wo, wgu, wdn, sems):
        i = pl.program_id(0)

        @pl.when(i == 0)
        def _():
            cps = [pltpu.make_async_copy(s, d, sems.at[k]) for k, (s, d) in
                   enumerate(((wo_hbm, wo), (wgu_hbm, wgu), (wdn_hbm, wdn)))]
            for c in cps:
                c.start()
            for c in cps:
                c.wait()
            own = pltpu.make_async_copy(
                own_hbm, wgu.at[:, pl.ds(pl.multiple_of(s_ref[0] * GU_SH, 128), GU_SH)], sems.at[3])
            own.start()
            own.wait()
            sq_ref[...] = jnp.zeros_like(sq_ref)
            sm_ref[...] = jnp.zeros_like(sm_ref)

        gate1, shift2, scale2, gate2 = mod_ref[2:3, :], mod_ref[3:4, :], mod_ref[4:5, :], mod_ref[5:6, :]
        n2w, fw = n2_ref[...], fw_ref[...]
        o = _dot(a_ref[...], wo[0:AW, :]) + _dot(y_ref[...], wo[AW:D, :])
        x1 = x_ref[...] + gate1 * o
        r2 = lax.rsqrt(jnp.mean(x1 * x1, axis=-1, keepdims=True) + EPS)
        xh2 = x1 * r2
        n2 = xh2 * n2w
        h2b = (n2 * (1.0 + scale2) + shift2).astype(BF16)
        h2_ref[...] = h2b
        f = jnp.zeros((tm, D), F32)
        saved = []
        for a, b in FF_SPLITS:
            gp = _dot(h2b, wgu[:, a:b])
            upj = _dot(h2b, wgu[:, DFF + a:DFF + b])
            sg = _sigmoid(gp)
            sl = gp * sg
            actb = (sl * upj).astype(BF16)
            act_ref[:, a:b] = actb
            f = f + _dot(actb, wdn[a:b, :])
            saved.append((gp, upj, sg, sl))
        x2 = x1 + gate2 * f
        r3 = lax.rsqrt(jnp.mean(x2 * x2, axis=-1, keepdims=True) + EPS)
        xh3 = x2 * r3
        err = xh3 * fw - t_ref[...]
        sq_ref[...] += jnp.sum(err * err, axis=0, keepdims=True)
        dy = err * (1.0 / D)
        dfw = jnp.sum(dy * xh3, axis=0, keepdims=True)
        dxh3 = dy * fw
        dx2 = r3 * (dxh3 - xh3 * jnp.mean(dxh3 * xh3, axis=-1, keepdims=True))
        dgate2 = jnp.sum(dx2 * f, axis=0, keepdims=True)
        dfb = (dx2 * gate2).astype(BF16)
        df_ref[...] = dfb
        dh2 = jnp.zeros((tm, D), F32)
        for (a, b), (gp, upj, sg, sl) in zip(FF_SPLITS, saved):
            dact = _dot_nt(dfb, wdn[a:b, :])
            dg = (dact * upj * (sg * (1.0 + gp * (1.0 - sg)))).astype(BF16)
            du = (dact * sl).astype(BF16)
            dgu_ref[:, a:b] = dg
            dgu_ref[:, DFF + a:DFF + b] = du
            dh2 = dh2 + _dot_nt(dg, wgu[:, a:b]) + _dot_nt(du, wgu[:, DFF + a:DFF + b])
        dshift2 = jnp.sum(dh2, axis=0, keepdims=True)
        dscale2 = jnp.sum(dh2 * n2, axis=0, keepdims=True)
        dn2 = dh2 * (1.0 + scale2)
        dn2w = jnp.sum(dn2 * xh2, axis=0, keepdims=True)
        dxh2 = dn2 * n2w
        dx1 = dx2 + r2 * (dxh2 - xh2 * jnp.mean(dxh2 * xh2, axis=-1, keepdims=True))
        dx1_ref[...] = dx1
        dgate1 = jnp.sum(dx1 * o, axis=0, keepdims=True)
        dob = (dx1 * gate1).astype(BF16)
        do_ref[...] = dob
        dmix_ref[...] = _dot_nt(dob, wo[...])
        sm_ref[...] += jnp.concatenate(
            [dfw, dn2w, dshift2, dscale2, dgate2, dgate1, jnp.zeros((2, D), F32)], axis=0)

    row = lambda w: pl.BlockSpec((tm, w), lambda i: (i, 0))
    full = lambda a: pl.BlockSpec(a.shape, lambda i: (0,) * a.ndim)
    anyspec = pl.BlockSpec(memory_space=pl.ANY)
    return pl.pallas_call(
        body, name="mix_ffn", grid=(nt,),
        in_specs=[row(D), row(AW), row(SW), row(D), full(mod6), full(norm2_w), full(final_w), anyspec, anyspec, anyspec,
                  pl.BlockSpec(memory_space=pltpu.SMEM), anyspec],
        out_specs=[pl.BlockSpec((1, D), lambda i: (0, 0)), row(D), row(D), row(D),
                   row(DFF), row(D), row(2 * DFF), row(D), pl.BlockSpec((8, D), lambda i: (0, 0))],
        out_shape=[jax.ShapeDtypeStruct((1, D), F32), jax.ShapeDtypeStruct((T, D), F32), jax.ShapeDtypeStruct((T, D), F32),
                   jax.ShapeDtypeStruct((T, D), BF16), jax.ShapeDtypeStruct((T, DFF), BF16),
                   jax.ShapeDtypeStruct((T, D), BF16), jax.ShapeDtypeStruct((T, 2 * DFF), BF16),
                   jax.ShapeDtypeStruct((T, D), BF16), jax.ShapeDtypeStruct((8, D), F32)],
        scratch_shapes=[pltpu.VMEM((D, D), BF16), pltpu.VMEM((D, 2 * DFF), BF16), pltpu.VMEM((DFF, D), BF16),
                        pltpu.SemaphoreType.DMA((4,))],
        compiler_params=_cp("arbitrary"),
    )(x, attn, ynorm, tgt, mod6, norm2_w, final_w, w_out, w_gu, w_gu_own, s_arr, w_dn)


def _in_proj_bwd(x, dx1, dqkv, dzxd, mod6, norm1_w, w_pad, tm, dep):
    T = x.shape[0]

    def body(x_ref, dx1_ref, dq_ref, dz_ref, mod_ref, nw_ref, w_hbm, dep_ref, gx_ref, sm_ref, w_vmem, sem):
        _load_resident(w_hbm, w_vmem, sem)

        @pl.when(pl.program_id(0) == 0)
        def _():
            sm_ref[...] = jnp.zeros_like(sm_ref)

        nw = nw_ref[...]
        scale1 = mod_ref[1:2, :]
        sums = jnp.zeros((8, D), F32)
        for rows in (slice(0, tm // 2), slice(tm // 2, tm)):
            dh = _dot_nt(dq_ref[rows, :], w_vmem[:, 0:768]) + _dot_nt(dz_ref[rows, :], w_vmem[:, 768:IN_PAD])
            xv = x_ref[rows, :]
            r = lax.rsqrt(jnp.mean(xv * xv, axis=-1, keepdims=True) + EPS)
            xh = xv * r
            n1 = xh * nw
            dshift = jnp.sum(dh, axis=0, keepdims=True)
            dscale = jnp.sum(dh * n1, axis=0, keepdims=True)
            dn = dh * (1.0 + scale1)
            dnw = jnp.sum(dn * xh, axis=0, keepdims=True)
            dxh = dn * nw
            gx_ref[rows, :] = dx1_ref[rows, :] + r * (dxh - xh * jnp.mean(dxh * xh, axis=-1, keepdims=True))
            sums = sums + jnp.concatenate([dnw, dshift, dscale, jnp.zeros((5, D), F32)], axis=0)
        sm_ref[...] += sums

    row = lambda w: pl.BlockSpec((tm, w), lambda i: (i, 0))
    full = lambda a: pl.BlockSpec(a.shape, lambda i: (0,) * a.ndim)
    return pl.pallas_call(
        body, name="in_proj_bwd", grid=(T // tm,),
        in_specs=[row(D), row(D), row(768), row(1664), full(mod6), full(norm1_w), pl.BlockSpec(memory_space=pl.ANY),
                  DEP_SPEC],
        out_specs=[row(D), pl.BlockSpec((8, D), lambda i: (0, 0))],
        out_shape=[jax.ShapeDtypeStruct((T, D), F32), jax.ShapeDtypeStruct((8, D), F32)],
        scratch_shapes=[pltpu.VMEM((D, IN_PAD), BF16), pltpu.SemaphoreType.DMA],
        compiler_params=_cp("arbitrary"),
    )(x, dx1, dqkv, dzxd, mod6, norm1_w, w_pad, dep)


def _tn_matmul(a, b, K, N, tt, name, dep):
    T = a.shape[0]
    ja, jb = a.shape[1] // K, b.shape[1] // N
    J = max(ja, jb)

    def body(a_ref, b_ref, dep_ref, o_ref):
        t = pl.program_id(1)
        prod = _dot_tn(a_ref[...], b_ref[...])

        @pl.when(t == 0)
        def _():
            o_ref[0] = prod

        @pl.when(t > 0)
        def _():
            o_ref[0] += prod

    return pl.pallas_call(
        body, name=name, grid=(J, T // tt),
        in_specs=[pl.BlockSpec((tt, K), lambda j, t: (t, j if ja > 1 else 0)),
                  pl.BlockSpec((tt, N), lambda j, t: (t, j if jb > 1 else 0)),
                  pl.BlockSpec((8, 128), lambda j, t: (0, 0))],
        out_specs=pl.BlockSpec((1, K, N), lambda j, t: (j, 0, 0)),
        out_shape=jax.ShapeDtypeStruct((J, K, N), F32),
        compiler_params=_cp("parallel", "arbitrary"),
    )(a, b, dep)


def _tn_matmul_rows(a0, a1, b, tt, name, dep):
    T, K = a0.shape
    N = b.shape[1]

    def body(a0_ref, a1_ref, b_ref, dep_ref, o_ref):
        t = pl.program_id(1)
        a = jnp.where(pl.program_id(0) == 0, a0_ref[...], a1_ref[...])
        prod = _dot_tn(a, b_ref[...])

        @pl.when(t == 0)
        def _():
            o_ref[...] = prod

        @pl.when(t > 0)
        def _():
            o_ref[...] += prod

    a_spec = pl.BlockSpec((tt, K), lambda j, t: (t, 0))
    return pl.pallas_call(
        body, name=name, grid=(2, T // tt),
        in_specs=[a_spec, a_spec, pl.BlockSpec((tt, N), lambda j, t: (t, 0)), pl.BlockSpec((8, 128), lambda j, t: (0, 0))],
        out_specs=pl.BlockSpec((K, N), lambda j, t: (j, 0)),
        out_shape=jax.ShapeDtypeStruct((2 * K, N), F32),
        compiler_params=_cp("parallel", "arbitrary"),
    )(a0, a1, b, dep)


def _adam_math(w, g, m, v):
    m = B1 * m + (1.0 - B1) * g
    v = B2 * v + (1.0 - B2) * (g * g)
    m_hat = m / (1.0 - B1 ** STEP)
    v_hat = v / (1.0 - B2 ** STEP)
    delta = -LR * (m_hat / (jnp.sqrt(v_hat) + AEPS) + WD * w)
    return delta, m, v


def _adam_2d(w, mine, land, m, v, c_arr, rb, name, dep):
    R, C = w.shape
    nbh = R // 2 // rb

    def body(c_ref, w_ref, mine_ref, land_ref, m_ref, v_ref, dep_ref, go_ref, d_ref, mo_ref, vo_ref):
        g = jnp.where(pl.program_id(0) // nbh == c_ref[0], mine_ref[...], land_ref[...])
        d, mn, vn = _adam_math(w_ref[...], g, m_ref[...], v_ref[...])
        go_ref[...] = g
        d_ref[...] = d
        mo_ref[...] = mn
        vo_ref[...] = vn

    spec = pl.BlockSpec((rb, C), lambda i, c_ref: (i, 0))
    mine_spec = pl.BlockSpec((rb, C), lambda i, c_ref: (jnp.clip(i - c_ref[0] * nbh, 0, nbh - 1), 0))
    return pl.pallas_call(
        body, name=name,
        grid_spec=pltpu.PrefetchScalarGridSpec(
            num_scalar_prefetch=1, grid=(R // rb,), in_specs=[spec, mine_spec, spec, spec, spec, DEP_SPEC],
            out_specs=[spec] * 4),
        out_shape=[jax.ShapeDtypeStruct((R, C), F32)] * 4, compiler_params=_cp("parallel"),
    )(c_arr, w, mine, land, m, v, dep)


def _adam_w_in(w3, mine, land, m3, v3, c_arr):
    n = w3.shape[0]

    def body(c_ref, w_hbm, mine_ref, land_ref, m_hbm, v_hbm, g_hbm, d_hbm, mo_hbm, vo_hbm, bufs, sems):
        ins = [pltpu.make_async_copy(src.at[:, 0], bufs.at[k], sems.at[k]) for k, src in enumerate((w_hbm, m_hbm, v_hbm))]
        for cp in ins:
            cp.start()
        half = D // 2
        top = jnp.where(c_ref[0] == 0, mine_ref[...], land_ref[0:half, :])
        bot = jnp.where(c_ref[0] == 1, mine_ref[...], land_ref[half:D, :])
        g = jnp.concatenate([top, bot], axis=0)
        eye = (_iota((D, D), 0) == _iota((D, D), 1)).astype(BF16)
        g_t = jnp.zeros((n, D), F32)
        r = g
        for i in range(3):
            p = r.astype(BF16)
            g_t = g_t + _dot_tn(p, eye)
            if i < 2:
                r = r - p.astype(F32)
        for cp in ins:
            cp.wait()
        d, mn, vn = _adam_math(bufs[0], g_t, bufs[1], bufs[2])
        for k, val in enumerate((g_t, d, mn, vn)):
            bufs[3 + k] = val
        outs = [pltpu.make_async_copy(bufs.at[3 + k], dst.at[:, 0], sems.at[3 + k])
                for k, dst in enumerate((g_hbm, d_hbm, mo_hbm, vo_hbm))]
        for cp in outs:
            cp.start()
        for cp in outs:
            cp.wait()

    anyspec = pl.BlockSpec(memory_space=pl.ANY)
    vm = pl.BlockSpec(memory_space=pltpu.VMEM)
    return pl.pallas_call(
        body, name="adam_w_in",
        in_specs=[pl.BlockSpec(memory_space=pltpu.SMEM), anyspec, vm, vm, anyspec, anyspec], out_specs=[anyspec] * 4,
        out_shape=[jax.ShapeDtypeStruct(w3.shape, F32)] * 4,
        scratch_shapes=[pltpu.VMEM((7, n, D), F32), pltpu.SemaphoreType.DMA((7,))],
        compiler_params=pltpu.CompilerParams(vmem_limit_bytes=VMEM_LIMIT),
    )(c_arr, w3, mine, land, m3, v3)


def _adam_w_ada(gat, allv, s_arr, w, m, v, rb):
    R, C = w.shape

    def body(s_ref, c_ref, dm_ref, w_ref, m_ref, v_ref, g_ref, d_ref, mo_ref, vo_ref):
        cm = _rows_select(c_ref, rb)
        g = lax.dot_general(cm * _sigmoid(cm), _rows_select(dm_ref, C), (((0,), (0,)), ((), ())), precision=HI,
                            preferred_element_type=F32)
        d, mn, vn = _adam_math(w_ref[...], g, m_ref[...], v_ref[...])
        g_ref[...] = g
        d_ref[...] = d
        mo_ref[...] = mn
        vo_ref[...] = vn

    spec = pl.BlockSpec((rb, C), lambda i, s_ref: (i, 0))
    return pl.pallas_call(
        body, name="adam_w_ada",
        grid_spec=pltpu.PrefetchScalarGridSpec(
            num_scalar_prefetch=1, grid=(R // rb,),
            in_specs=[pl.BlockSpec((8, 1, rb), lambda i, s_ref: (0, 0, i)),
                      pl.BlockSpec((8, 1, C), lambda i, s_ref: (0, 0, s_ref[0])), spec, spec, spec],
            out_specs=[spec] * 4),
        out_shape=[jax.ShapeDtypeStruct((R, C), F32)] * 4, compiler_params=_cp("parallel"),
    )(s_arr, gat, allv, w, m, v)


def _adam_small(tot, segs, ws, ms, vs):
    k = len(ws)
    extra = [sg for sg in segs if not isinstance(sg, tuple)]
    ne = len(extra)

    def body(*refs):
        tot_ref, g_x = refs[0], list(refs[1:1 + ne])
        w, m, v = [refs[1 + ne + j * k:1 + ne + (j + 1) * k] for j in range(3)]
        g_o, d_o, m_o, v_o = [refs[1 + ne + (3 + j) * k:1 + ne + (4 + j) * k] for j in range(4)]
        for i in range(k):
            gi = tot_ref[:, segs[i][0]:segs[i][0] + segs[i][1]] if isinstance(segs[i], tuple) else g_x.pop(0)[...]
            d, mn, vn = _adam_math(w[i][...], gi, m[i][...], v[i][...])
            g_o[i][...] = gi
            d_o[i][...] = d
            m_o[i][...] = mn
            v_o[i][...] = vn

    shapes = [jax.ShapeDtypeStruct(w.shape, F32) for w in ws]
    vm = pl.BlockSpec(memory_space=pltpu.VMEM)
    outs = pl.pallas_call(
        body, name="adam_small", in_specs=[vm] * (1 + ne + 3 * k), out_specs=[vm] * (4 * k), out_shape=shapes * 4,
    )(tot, *extra, *ws, *ms, *vs)
    return outs[0:k], outs[k:2 * k], outs[2 * k:3 * k], outs[3 * k:4 * k]


def _pos():
    return lax.axis_index("x"), lax.axis_index("y"), lax.axis_index("c")


def _flip(v, bit):
    return 1 - v if bit else v


def _peer(k):
    x, y, c = _pos()
    return (_flip(x, (k >> 2) & 1), _flip(y, (k >> 1) & 1), _flip(c, k & 1))


def _logical(p):
    return 4 * p[0] + 2 * p[1] + p[2]


def _gather8(src_ref, dst_ref, send_sems, recv_sems):
    me = _logical(_pos())
    dst_ref[pl.ds(me, 1)] = src_ref[...][None]
    copies = []
    for k in range(1, 8):
        cp = pltpu.make_async_remote_copy(src_ref, dst_ref.at[me], send_sems.at[k - 1], recv_sems.at[k - 1],
                                          device_id=_peer(k), device_id_type=MESH)
        cp.start()
        copies.append(cp)
    for k in range(1, 8):
        pltpu.make_async_remote_copy(src_ref, dst_ref.at[_logical(_peer(k))], send_sems.at[k - 1], recv_sems.at[k - 1],
                                     device_id=_peer(k), device_id_type=MESH).wait_recv()
    for cp in copies:
        cp.wait_send()


def _rows_select(ref3, width):
    row = _iota((8, width), 0)
    out = jnp.zeros((8, width), F32)
    for i in range(8):
        out = jnp.where(row == i, ref3[i][:, 0:width], out)
    return out


def _mod_exchange(payload, w_ada_s, b_ada4):
    n_sh = w_ada_s.shape[1]

    def body(pay_ref, w_ref, b_ref, gat_ref, mod_ref, token, p3, sa, ra, sb, rb):
        token[...] = jnp.zeros_like(token)
        x, y, c = _pos()
        me = _logical((x, y, c))
        my_s = 2 * x + y
        _gather8(pay_ref, gat_ref, sa, ra)
        cmat = _rows_select(gat_ref, D)
        prod = _dot_hi(cmat * _sigmoid(cmat), w_ref[...])
        for b in range(8):
            p3[b] = prod[b:b + 1, :]
        mod_ref[pl.ds(my_s, 1)] = p3[pl.ds(me, 1)] + b_ref[pl.ds(my_s, 1)]
        ks = (2, 4, 6)
        copies = []
        for i, k in enumerate(ks):
            pr = _peer(k)
            cp = pltpu.make_async_remote_copy(p3.at[_logical(pr)], mod_ref.at[my_s], sb.at[i], rb.at[i],
                                              device_id=pr, device_id_type=MESH)
            cp.start()
            copies.append(cp)
        for i, k in enumerate(ks):
            pr = _peer(k)
            s_src = 2 * pr[0] + pr[1]
            pltpu.make_async_remote_copy(p3.at[0], mod_ref.at[s_src], sb.at[i], rb.at[i],
                                         device_id=pr, device_id_type=MESH).wait_recv()
            mod_ref[pl.ds(s_src, 1)] = mod_ref[pl.ds(s_src, 1)] + b_ref[pl.ds(s_src, 1)]
        for cp in copies:
            cp.wait_send()

    vm = pl.BlockSpec(memory_space=pltpu.VMEM)
    return pl.pallas_call(
        body, name="mod_exchange", in_specs=[vm, vm, vm], out_specs=[vm, vm, vm],
        out_shape=[jax.ShapeDtypeStruct((8, 1, payload.shape[1]), F32), jax.ShapeDtypeStruct((4, 1, n_sh), F32),
                   jax.ShapeDtypeStruct((8, 128), F32)],
        scratch_shapes=[pltpu.VMEM((8, 1, n_sh), F32), pltpu.SemaphoreType.DMA((7,)), pltpu.SemaphoreType.DMA((7,)),
                        pltpu.SemaphoreType.DMA((3,)), pltpu.SemaphoreType.DMA((3,))],
        compiler_params=pltpu.CompilerParams(vmem_limit_bytes=VMEM_LIMIT),
    )(payload, w_ada_s, b_ada4)


def _chips():
    x, y, _ = _pos()
    out = []
    for k in (1, 2, 3):
        px, py = _flip(x, (k >> 1) & 1), _flip(y, k & 1)
        out.append((px, py, 2 * px + py))
    return out


def _half_rows(ref, which):
    half = ref.shape[-2] // 2
    return pl.ds(pl.multiple_of(which * half, 8), half)


def _plan_small():
    def plan(refs):
        me = _logical(_pos())
        return [(refs[0], refs[1].at[me], _peer(k), refs[1].at[_logical(_peer(k))]) for k in range(1, 8)]
    return plan


def _small_sum(vec, land, me_arr):
    n = vec.shape[1]

    def body(me_ref, v_ref, land_ref, tot_ref, all_ref):
        tot = None
        for i in range(8):
            row = jnp.where(me_ref[0] == i, v_ref[...], land_ref[i])
            all_ref[i] = row
            tot = row if i == 0 else tot + row
        tot_ref[...] = tot

    return pl.pallas_call(
        body, name="small_sum",
        grid_spec=pltpu.PrefetchScalarGridSpec(
            num_scalar_prefetch=1, grid=(1,),
            in_specs=[pl.BlockSpec((1, n), lambda i, me_ref: (0, 0)), pl.BlockSpec((8, 1, n), lambda i, me_ref: (0, 0, 0))],
            out_specs=[pl.BlockSpec((1, n), lambda i, me_ref: (0, 0)),
                       pl.BlockSpec((8, 1, n), lambda i, me_ref: (0, 0, 0))]),
        out_shape=[jax.ShapeDtypeStruct((1, n), F32), jax.ShapeDtypeStruct((8, 1, n), F32)],
        compiler_params=_cp("arbitrary"),
    )(me_arr, vec, land)


def _add_half(g, sib, c_arr, rb, name):
    _, R, C = g.shape
    half = R // 2
    nb = half // rb

    def body(c_ref, g_ref, s_ref, o_ref):
        o_ref[...] = (g_ref[...] + s_ref[...]).astype(BF16)

    return pl.pallas_call(
        body, name=name,
        grid_spec=pltpu.PrefetchScalarGridSpec(
            num_scalar_prefetch=1, grid=(4, nb),
            in_specs=[pl.BlockSpec((1, rb, C), lambda s, i, c_ref: (s, c_ref[0] * nb + i, 0)),
                      pl.BlockSpec((1, rb, C), lambda s, i, c_ref: (s, i, 0))],
            out_specs=pl.BlockSpec((1, rb, C), lambda s, i, c_ref: (s, i, 0))),
        out_shape=jax.ShapeDtypeStruct((4, half, C), BF16),
        compiler_params=_cp("parallel", "parallel"),
    )(c_arr, g, sib)


def _add_half_in(gq, gz, sibq, sibz, c_arr, rb):
    half = D // 2
    nq = gq.shape[1]
    wide = -(-IN_SH // 128) * 128

    def sel(rows, first, lo):
        return (_iota((rows, wide), 0) + (first - lo) == _iota((rows, wide), 1)).astype(BF16)

    def body(c_ref, gq_ref, gz_ref, sq_ref, sz_ref, o_ref):
        q = (gq_ref[...] + sq_ref[...]).astype(BF16)
        z = (gz_ref[...] + sz_ref[...]).astype(BF16)
        for s in range(4):
            lo, hi = s * IN_SH, (s + 1) * IN_SH
            acc = jnp.zeros((rb, wide), F32)
            if lo < nq:
                a0, a1 = lo // 128 * 128, min(nq, -(-min(hi, nq) // 128) * 128)
                acc = acc + _dot(q[:, a0:a1], sel(a1 - a0, a0, lo))
            if hi > nq:
                a0, a1 = (max(lo, nq) - nq) // 128 * 128, -(-(hi - nq) // 128) * 128
                acc = acc + _dot(z[:, a0:a1], sel(a1 - a0, nq + a0, lo))
            o_ref[s] = acc[:, :IN_SH].astype(BF16)

    nb = half // rb
    mine = lambda w: pl.BlockSpec((rb, w), lambda i, c_ref: (c_ref[0] * nb + i, 0))
    sib = lambda w: pl.BlockSpec((rb, w), lambda i, c_ref: (i, 0))
    return pl.pallas_call(
        body, name="grad_add_in",
        grid_spec=pltpu.PrefetchScalarGridSpec(
            num_scalar_prefetch=1, grid=(nb,),
            in_specs=[mine(nq), mine(gz.shape[1]), sib(nq), sib(gz.shape[1])],
            out_specs=pl.BlockSpec((4, rb, IN_SH), lambda i, c_ref: (0, i, 0))),
        out_shape=jax.ShapeDtypeStruct((4, half, IN_SH), BF16),
        compiler_params=_cp("parallel"),
    )(c_arr, gq, gz, sibq, sibz)


def _sum4(parts, land, s_arr, rb, name):
    _, H, C = land.shape

    def body(s_ref, own_ref, r_ref, o_ref):
        own = own_ref[0].astype(F32)
        tot = jnp.zeros((rb, C), F32)
        for j in range(4):
            tot = tot + jnp.where(s_ref[0] == j, own, r_ref[j].astype(F32))
        o_ref[...] = tot

    return pl.pallas_call(
        body, name=name,
        grid_spec=pltpu.PrefetchScalarGridSpec(
            num_scalar_prefetch=1, grid=(H // rb,),
            in_specs=[pl.BlockSpec((1, rb, C), lambda i, s_ref: (s_ref[0], i, 0)),
                      pl.BlockSpec((4, rb, C), lambda i, s_ref: (0, i, 0))],
            out_specs=pl.BlockSpec((rb, C), lambda i, s_ref: (i, 0))),
        out_shape=jax.ShapeDtypeStruct((H, C), F32), compiler_params=_cp("parallel"),
    )(s_arr, parts, land)


HBM_SPEC = pl.BlockSpec(memory_space=pltpu.HBM)
SEM_SPEC = pl.BlockSpec(memory_space=pltpu.SEMAPHORE)
EFFECT = pltpu.SideEffectType.DATAFLOW_SIDE_EFFECTING


def _split_start(name, bufs, n_sem, plan, dep):
    nb = len(bufs)

    def body(*refs):
        ins, send, recv, token = refs[:nb], refs[nb + 1], refs[nb + 2], refs[-1]
        for i, (src, dst, dev, _) in enumerate(plan(ins)):
            pltpu.make_async_remote_copy(src, dst, send.at[i], recv.at[i], device_id=dev, device_id_type=MESH).start()
        token[...] = jnp.zeros_like(token)

    outs = pl.pallas_call(
        body, name=name,
        out_shape=(pltpu.SemaphoreType.DMA((n_sem,)), pltpu.SemaphoreType.DMA((n_sem,)),
                   *[pltpu.HBM(b.shape, b.dtype) for b in bufs], jax.ShapeDtypeStruct((8, 128), F32)),
        in_specs=[HBM_SPEC] * nb + [pl.BlockSpec(memory_space=pl.ANY)],
        out_specs=(SEM_SPEC, SEM_SPEC, *([HBM_SPEC] * nb), pl.BlockSpec(memory_space=pltpu.VMEM)),
        input_output_aliases={i: 2 + i for i in range(nb)},
        compiler_params=pltpu.CompilerParams(has_side_effects=EFFECT),
    )(*[pltpu.with_memory_space_constraint(b, pltpu.HBM) for b in bufs], dep)
    return outs[0], outs[1], list(outs[2:2 + nb]), outs[-1]


def _split_wait(name, send, recv, bufs, after, plan):
    nb = len(bufs)
    after = list(after) if isinstance(after, (list, tuple)) else [after]

    def body(*refs):
        ins, send_s, recv_s = refs[:nb], refs[nb], refs[nb + 1]
        for i, (src, dst, dev, mine) in enumerate(plan(ins)):
            pltpu.make_async_remote_copy(src, dst, send_s.at[i], recv_s.at[i], device_id=dev,
                                         device_id_type=MESH).wait_send()
            pltpu.make_async_remote_copy(src, mine, send_s.at[i], recv_s.at[i], device_id=dev,
                                         device_id_type=MESH).wait_recv()

    outs = pl.pallas_call(
        body, name=name, out_shape=[pltpu.HBM(b.shape, b.dtype) for b in bufs],
        in_specs=[HBM_SPEC] * nb + [SEM_SPEC, SEM_SPEC] + [HBM_SPEC] * len(after),
        out_specs=[HBM_SPEC] * nb, input_output_aliases={i: i for i in range(nb)},
        compiler_params=pltpu.CompilerParams(has_side_effects=EFFECT),
    )(*bufs, send, recv, *[pltpu.with_memory_space_constraint(a, pltpu.HBM) for a in after])
    return list(outs)


def _copies_now(name, bufs, n_sem, plan):
    nb = len(bufs)

    def body(*refs):
        ins, token, send, recv = refs[:nb], refs[2 * nb], refs[-2], refs[-1]
        token[...] = jnp.zeros_like(token)
        todo = plan(ins)
        for i, (src, dst, dev, _) in enumerate(todo):
            pltpu.make_async_remote_copy(src, dst, send.at[i], recv.at[i], device_id=dev, device_id_type=MESH).start()
        for i, (src, dst, dev, mine) in enumerate(todo):
            pltpu.make_async_remote_copy(src, mine, send.at[i], recv.at[i], device_id=dev, device_id_type=MESH).wait_recv()
        for i, (src, dst, dev, _) in enumerate(todo):
            pltpu.make_async_remote_copy(src, dst, send.at[i], recv.at[i], device_id=dev, device_id_type=MESH).wait_send()

    outs = pl.pallas_call(
        body, name=name,
        out_shape=[pltpu.HBM(b.shape, b.dtype) for b in bufs] + [jax.ShapeDtypeStruct((8, 128), F32)],
        in_specs=[HBM_SPEC] * nb, out_specs=[HBM_SPEC] * nb + [pl.BlockSpec(memory_space=pltpu.VMEM)],
        input_output_aliases={i: i for i in range(nb)},
        scratch_shapes=[pltpu.SemaphoreType.DMA((n_sem,)), pltpu.SemaphoreType.DMA((n_sem,))],
    )(*[pltpu.with_memory_space_constraint(b, pltpu.HBM) for b in bufs])
    return list(outs[:nb]), outs[nb]


def _slot(land, s, rows, cols):
    if cols is None:
        return land.at[s, rows]
    return land.at[rows, pl.ds(pl.multiple_of(s * cols, 128), cols)]


def _plan_gather_ici(cols):
    nw = len(cols)

    def plan(refs):
        x, y, c = _pos()
        my_s = 2 * x + y
        out = []
        for w in range(nw):
            mine = _half_rows(refs[w], c)
            for px, py, ps in _chips():
                out.append((refs[w].at[mine], _slot(refs[nw + w], my_s, mine, cols[w]), (px, py, c),
                            _slot(refs[nw + w], ps, mine, cols[w])))
        return out
    return plan


def _plan_gather_fwd(cols, rows):
    def plan(refs):
        x, y, c = _pos()
        out = []
        for w in range(len(cols)):
            half = rows[w] // 2
            mine = pl.ds(pl.multiple_of(c * half, 8), half)
            other = pl.ds(pl.multiple_of((1 - c) * half, 8), half)
            for px, py, ps in _chips():
                got = _slot(refs[w], ps, mine, cols[w])
                out.append((got, got, (x, y, 1 - c), _slot(refs[w], ps, other, cols[w])))
        return out
    return plan


def _plan_swap(nw):
    def plan(refs):
        x, y, c = _pos()
        return [(refs[w].at[:, _half_rows(refs[w], 1 - c)], refs[nw + w], (x, y, 1 - c), refs[nw + w])
                for w in range(nw)]
    return plan


def _plan_swap_rows(nw):
    def plan(refs):
        x, y, c = _pos()
        return [(refs[w].at[_half_rows(refs[w], 1 - c)], refs[nw + w], (x, y, 1 - c), refs[nw + w])
                for w in range(nw)]
    return plan


def _plan_scatter(nw):
    def plan(refs):
        x, y, c = _pos()
        my_s = 2 * x + y
        out = []
        for w in range(nw):
            for px, py, ps in _chips():
                out.append((refs[w].at[ps], refs[nw + w].at[my_s], (px, py, c), refs[nw + w].at[ps]))
        return out
    return plan


def _plan_join(nw):
    def plan(refs):
        x, y, c = _pos()
        out = []
        for w in range(nw):
            land = refs[nw + w]
            out.append((refs[w], land.at[_half_rows(land, c)], (x, y, 1 - c), land.at[_half_rows(land, 1 - c)]))
        return out
    return plan


def _hbm_empty(shape, dtype):
    return pltpu.with_memory_space_constraint(lax.empty(shape, dtype), pltpu.HBM)


def _put_slot(land, own, slot):
    return lax.dynamic_update_slice(land, own[None], (slot,) + (0,) * own.ndim)


def _pad_lanes(a, n):
    return jnp.pad(a, ((0, 0), (0, n - a.shape[1])))


def kernel(x, c, positions, w_ada, b_ada, norm1_w, w_in, conv_w, conv_b, dt_bias, a_log, d_skip, attn_sinks, ssm_norm_w, w_out, norm2_w, w_gate_up, w_down, final_norm_w, loss_target, m_w_ada, m_b_ada, m_norm1_w, m_w_in, m_conv_w, m_conv_b, m_dt_bias, m_a_log, m_d_skip, m_attn_sinks, m_ssm_norm_w, m_w_out, m_norm2_w, m_w_gate_up, m_w_down, m_final_norm_w, v_w_ada, v_b_ada, v_norm1_w, v_w_in, v_conv_w, v_conv_b, v_dt_bias, v_a_log, v_d_skip, v_attn_sinks, v_ssm_norm_w, v_w_out, v_norm2_w, v_w_gate_up, v_w_down, v_final_norm_w):
    T = x.shape[1]
    tm = min(256, T)
    xi, yi, ci = lax.axis_index("x"), lax.axis_index("y"), lax.axis_index("c")
    my_s = 2 * xi + yi
    xs = x[0]
    tgt = loss_target[0]

    payload = jnp.concatenate([c, conv_w[0].reshape(1, CONVK * 256)], axis=1)
    gat, mod4, tok = _mod_exchange(payload, w_ada[0], b_ada.reshape(4, 1, 1536))
    mod6 = mod4.reshape(6, D)
    cw_dev = gat[:, 0, D:].reshape(4, 2, CONVK, 256)[:, 0]
    conv_full = cw_dev.transpose(1, 0, 2).reshape(CONVK, CONVC)

    w_in_b = w_in[0].astype(BF16)
    s_i, r_i, bufs, tok = _split_start("wgather_in_ici_start", [w_in_b, _hbm_empty((4,) + w_in_b.shape, BF16)], 3,
                                       _plan_gather_ici([None]), tok)
    inv_freq = (10000.0 ** (-jnp.arange(32, dtype=F32) / 32))
    cos, sin_s = _rope_tables(positions, inv_freq.reshape(32, 1), min(512, T), tok)
    late = [w_out[0].astype(BF16), w_gate_up[0].astype(BF16), w_down[0].astype(BF16)]
    bufs = _split_wait("wgather_in_ici_wait", s_i, r_i, bufs, [cos] + late, _plan_gather_ici([None]))
    own_in = bufs[0]
    bufs, tok = _copies_now("wgather_in_fwd", bufs[1:], 3, _plan_gather_fwd([None], [D]))
    g_in = _put_slot(bufs[0], own_in, my_s)
    w_pad = jnp.concatenate([g_in[0], g_in[1], g_in[2], g_in[3], jnp.zeros((D, IN_PAD - IN_PROJ), BF16)], axis=1)

    lands = [_hbm_empty((4, D // 4, D), BF16), _hbm_empty((D, 2 * DFF), BF16), _hbm_empty((4, DFF // 4, D), BF16)]
    cols3, rows3 = [None, GU_SH, None], [D // 4, D, DFF // 4]
    s_a, r_a, bufs, tok = _split_start("wgather_ici_start", late + lands, 9, _plan_gather_ici(cols3), tok)

    qkv, z, xbc, dtr, h1b = _in_proj_fwd(xs, cos, sin_s, mod6, norm1_w, w_pad, min(512, T), tok)
    sinks = attn_sinks
    attn, lse = _attn_fwd(qkv, sinks)
    bufs = _split_wait("wgather_ici_wait", s_a, r_a, bufs, attn, _plan_gather_ici(cols3))
    late = bufs[:3]
    s_b, r_b, lands, tok = _split_start("wgather_fwd_start", bufs[3:], 9, _plan_gather_fwd(cols3, rows3), attn)
    dtb = _pad_lanes(dt_bias, 128)
    alog = _pad_lanes(a_log, 128)
    dskx = jnp.repeat(d_skip, HD, axis=1)
    mats = _ssd_mats()
    ynorm, ypre, states, conv_pre = _ssd_fwd(xbc, z, dtr, conv_full, conv_b, dtb, alog, dskx, ssm_norm_w, mats, tok)
    lands = _split_wait("wgather_fwd_wait", s_b, r_b, lands, ynorm, _plan_gather_fwd(cols3, rows3))
    w_out_f = _put_slot(lands[0], late[0], my_s).reshape(D, D)
    w_dn_f = _put_slot(lands[2], late[2], my_s).reshape(DFF, D)
    s_arr = my_s.reshape(1).astype(jnp.int32)

    fw2 = final_norm_w.reshape(1, D)
    sq, dmix, dx1, h2b, act, dfb, dgu, dob, sm_ffn = _mix_ffn(
        xs, attn, ynorm, tgt, mod6, norm2_w, fw2, w_out_f, lands[1], late[1], s_arr, w_dn_f, tm)

    tt = min(2048, T)
    c_arr = ci.reshape(1).astype(jnp.int32)
    tok0 = jnp.zeros((8, 128), F32)
    gw_dn4 = _tn_matmul(act, dfb, GU_SH, D, tt, "dw_down", tok0).reshape(4, DFF // 4, D)
    gw_gu4 = _tn_matmul(h2b, dgu, D, GU_SH, tt, "dw_gate_up", tok0)
    gw_out4 = _tn_matmul_rows(attn, ynorm, dob, tt, "dw_out", tok0).reshape(4, D // 4, D)
    big1 = [gw_out4, gw_gu4, gw_dn4]
    rbs1 = [128, 512, 352]
    sib1 = [_hbm_empty((4, g.shape[1] // 2, g.shape[2]), F32) for g in big1]
    s_c, r_c, bufs, tok = _split_start("gswap_start", big1 + sib1, 3, _plan_swap(3), tok0)

    dzxd, d_cw, d_cb, d_sw, d_sk, d_dtb, d_av = _ssd_bwd(
        xbc, conv_pre, z, dtr, ypre, states, dmix, conv_full, dtb, alog, dskx, ssm_norm_w, mats, tok)
    bufs = _split_wait("gswap_wait", s_c, r_c, bufs, dzxd, _plan_swap(3))
    sums1 = [_add_half(g, s, c_arr, rb, "grad_add_%d" % i)
             for i, (g, s, rb) in enumerate(zip(bufs[:3], bufs[3:], rbs1))]
    land1 = [_hbm_empty(p.shape, BF16) for p in sums1]
    s_d, r_d, bufs, tok = _split_start("gscatter_start", sums1 + land1, 9, _plan_scatter(3), tok0)
    dqkv, d_sinks = _attn_bwd(qkv, sinks, lse, dmix, cos, sin_s, tok)
    bufs = _split_wait("gscatter_wait", s_d, r_d, bufs, dqkv, _plan_scatter(3))
    halves1 = [_sum4(p, l, s_arr, rb, "grad_sum_%d" % i)
               for i, (p, l, rb) in enumerate(zip(bufs[:3], bufs[3:], rbs1))]
    full1 = [_hbm_empty((2 * h.shape[0], h.shape[1]), F32) for h in halves1]
    s_e, r_e, bufs, tok = _split_start("gjoin_start", halves1 + full1, 3, _plan_join(3), tok0)
    gq = _tn_matmul(h1b, dqkv, D, 768, tt, "dw_in_qkv", tok)[0]
    gz = _tn_matmul(h1b, dzxd, D, IN_PAD - 768, tt, "dw_in_zxd", tok)[0]
    joined1 = _split_wait("gjoin_wait", s_e, r_e, bufs, [gq, gz], _plan_join(3))

    sibs = [_hbm_empty((D // 2, g.shape[1]), F32) for g in (gq, gz)]
    s_f, r_f, bufs, tok = _split_start("gswap_in_start", [gq, gz] + sibs, 2, _plan_swap_rows(2), tok0)
    g_dn_s, d_dn, m_dn, v_dn = _adam_2d(w_down[0], joined1[2], joined1[5], m_w_down[0], v_w_down[0], c_arr, 352,
                                        "adam_w_down", tok)
    g_gu_s, d_gu, m_gu, v_gu = _adam_2d(w_gate_up[0], joined1[1], joined1[4], m_w_gate_up[0], v_w_gate_up[0], c_arr,
                                        256, "adam_w_gate_up", tok)
    g_out_s, d_out, m_out, v_out = _adam_2d(w_out[0], joined1[0], joined1[3], m_w_out[0], v_w_out[0], c_arr, 128,
                                            "adam_w_out", tok)
    bufs = _split_wait("gswap_in_wait", s_f, r_f, bufs, [d_dn, d_gu, d_out], _plan_swap_rows(2))
    sum0 = _add_half_in(bufs[0], bufs[1], bufs[2], bufs[3], c_arr, min(256, D // 2))
    s_g, r_g, bufs, tok = _split_start("gscatter_in_start", [sum0, _hbm_empty(sum0.shape, BF16)], 3, _plan_scatter(1),
                                       tok0)
    grad_x, sm_in = _in_proj_bwd(xs, dx1, dqkv, dzxd, mod6, norm1_w, w_pad, min(512, T), tok)

    a_neg = -jnp.exp(alog)
    pieces = [sm_in[1:2], sm_in[2:3], sm_ffn[5:6], sm_ffn[2:3], sm_ffn[3:4], sm_ffn[4:5],
              sm_in[0:1], sm_ffn[1:2], sm_ffn[0:1], d_cb, d_cw.reshape(1, CONVK * CONVC),
              _pad_lanes(d_sw, SW), d_dtb, d_av * a_neg, d_sk, d_sinks,
              _pad_lanes((0.5 / D * jnp.sum(sq)).reshape(1, 1), 128)]
    vec = jnp.concatenate(pieces, axis=1)
    s_h, r_h, rows8, tok_small = _split_start("small_start", [vec, _hbm_empty((8,) + vec.shape, F32)], 7,
                                              _plan_small(), tok0)

    bufs = _split_wait("gscatter_in_wait", s_g, r_g, bufs, [grad_x, tok_small], _plan_scatter(1))
    half0 = _sum4(bufs[0], bufs[1], s_arr, 512, "grad_sum_in")
    joined0, _ = _copies_now("gjoin_in", [half0, _hbm_empty((D, IN_SH), F32)], 1, _plan_join(1))
    native = lambda a: a.transpose(2, 0, 1)
    adam_in = _adam_w_in(native(w_in), joined0[0], joined0[1], native(m_w_in), native(v_w_in), c_arr)
    g_in_s, d_in, m_in, v_in = [a.transpose(1, 2, 0) for a in adam_in]
    rows8 = _split_wait("small_wait", s_h, r_h, rows8, [adam_in[1]], _plan_small())
    tot, allv = _small_sum(rows8[0], rows8[1], (4 * xi + 2 * yi + ci).reshape(1).astype(jnp.int32))
    o = 0
    offs = []
    for p in pieces:
        offs.append(o)
        o += p.shape[1]
    seg = lambda i, n: (offs[i], n)
    g_conv_w = lax.dynamic_slice_in_dim(
        tot[:, offs[10]:offs[10] + CONVK * CONVC].reshape(CONVK, CONVC), my_s * 256, 256, axis=1)
    loss = tot[0, offs[16]]

    small_names = ["b_ada", "norm1_w", "conv_w", "conv_b", "dt_bias", "a_log", "d_skip", "attn_sinks", "ssm_norm_w",
                   "norm2_w", "final_norm_w"]
    small_g = [(0, 6 * D), seg(6, D), g_conv_w, seg(9, D), seg(12, 8), seg(13, 8), seg(14, 8), seg(15, 8),
               seg(11, SW), seg(7, D), seg(8, D)]
    as2d = lambda a: a.reshape(-1, a.shape[-1])
    small_w = [as2d(a) for a in (b_ada, norm1_w, conv_w, conv_b, dt_bias, a_log, d_skip, attn_sinks, ssm_norm_w,
                                 norm2_w, final_norm_w)]
    small_m = [as2d(a) for a in (m_b_ada, m_norm1_w, m_conv_w, m_conv_b, m_dt_bias, m_a_log, m_d_skip, m_attn_sinks,
                                 m_ssm_norm_w, m_norm2_w, m_final_norm_w)]
    small_v = [as2d(a) for a in (v_b_ada, v_norm1_w, v_conv_w, v_conv_b, v_dt_bias, v_a_log, v_d_skip, v_attn_sinks,
                                 v_ssm_norm_w, v_norm2_w, v_final_norm_w)]
    small_g, sd, smn, svn = _adam_small(tot, small_g, small_w, small_m, small_v)
    g_ada, d_ada, m_ada, v_ada = _adam_w_ada(gat, allv, s_arr, w_ada[0], m_w_ada[0], v_w_ada[0], 256)

    order = ["w_ada", "b_ada", "norm1_w", "w_in", "conv_w", "conv_b", "dt_bias", "a_log", "d_skip", "attn_sinks",
             "ssm_norm_w", "w_out", "norm2_w", "w_gate_up", "w_down", "final_norm_w"]
    shapes = dict(w_ada=w_ada.shape, b_ada=b_ada.shape, norm1_w=norm1_w.shape, w_in=w_in.shape, conv_w=conv_w.shape,
                  conv_b=conv_b.shape, dt_bias=dt_bias.shape, a_log=a_log.shape, d_skip=d_skip.shape,
                  attn_sinks=attn_sinks.shape, ssm_norm_w=ssm_norm_w.shape, w_out=w_out.shape, norm2_w=norm2_w.shape,
                  w_gate_up=w_gate_up.shape, w_down=w_down.shape, final_norm_w=final_norm_w.shape)
    grads = dict(w_ada=g_ada, w_in=g_in_s, w_out=g_out_s, w_gate_up=g_gu_s, w_down=g_dn_s)
    deltas = dict(w_ada=d_ada, w_in=d_in, w_out=d_out, w_gate_up=d_gu, w_down=d_dn)
    new_m = dict(w_ada=m_ada, w_in=m_in, w_out=m_out, w_gate_up=m_gu, w_down=m_dn)
    new_v = dict(w_ada=v_ada, w_in=v_in, w_out=v_out, w_gate_up=v_gu, w_down=v_dn)
    for i, nme in enumerate(small_names):
        grads[nme], deltas[nme], new_m[nme], new_v[nme] = small_g[i], sd[i], smn[i], svn[i]
    outs = [loss, grad_x[None]]
    for table in (grads, deltas, new_m, new_v):
        outs += [table[nme].reshape(shapes[nme]) for nme in order]
    return tuple(outs)
```

```python
import functools
import math

import jax
import jax.numpy as jnp
from jax import lax
from jax.experimental import pallas as pl
from jax.experimental.pallas import tpu as pltpu

F32 = jnp.float32
BF16 = jnp.bfloat16
HI = lax.Precision.HIGHEST
MESH = pl.DeviceIdType.MESH

D = 1024
HD = 64
AW = 512
SW = 512
NST = 128
CONVK = 4
CONVC = 1024
BLK = 128
CPS = 4
SSD_FWD_CPS = 8
ATTN_BPS = 8
IN_PROJ = 2312
IN_PAD = 2432
IN_SH = IN_PROJ // 4
DFF = 2816
GU_SH = 1408
FF_SPLITS = ((0, 1536), (1536, 2816))
EPS = 1e-6
NEG = -1e30
LR, B1, B2, AEPS, WD, STEP = 0.001, 0.9, 0.999, 1e-08, 0.01, 10
VMEM_LIMIT = 58 * 1024 * 1024


def _cp(*sem):
    return pltpu.CompilerParams(dimension_semantics=sem or None, vmem_limit_bytes=VMEM_LIMIT)


def _dot(a, b):
    return jnp.dot(a, b, preferred_element_type=F32)


def _dot_nt(a, b):
    return lax.dot_general(a, b, (((1,), (1,)), ((), ())), preferred_element_type=F32)


def _dot_tn(a, b):
    return lax.dot_general(a, b, (((0,), (0,)), ((), ())), preferred_element_type=F32)


def _dot_hi(a, b):
    return jnp.dot(a, b, precision=HI, preferred_element_type=F32)


def _sigmoid(x):
    return 1.0 / (1.0 + jnp.exp(-x))


def _iota(shape, dim):
    return lax.broadcasted_iota(jnp.int32, shape, dim)


def _load_resident(hbm_ref, vmem_ref, sem):
    @pl.when(pl.program_id(0) == 0)
    def _():
        cp = pltpu.make_async_copy(hbm_ref, vmem_ref, sem)
        cp.start()
        cp.wait()


def _swap32(t):
    lane = _iota(t.shape, 1)
    return jnp.where((lane & 63) < 32, pltpu.roll(t, 96, 1), pltpu.roll(t, 32, 1))


def _rope_fwd(t, cos, sin_s):
    return t * cos + _swap32(t) * sin_s


def _rope_bwd(t, cos, sin_s):
    return t * cos - _swap32(t) * sin_s


DEP_SPEC = pl.BlockSpec((8, 128), lambda *_: (0, 0))


def _rope_tables(pos_row, inv_freq_col, tm, dep):
    T = pos_row.shape[1]
    lane, row = jnp.arange(128)[None, :], jnp.arange(96)[:, None]
    pick = (lane % 32) == (row % 32)
    sel_cos = pick.astype(BF16)
    sel_sin = jnp.where(pick, jnp.where(lane % 64 < 32, -1.0, 1.0), 0.0).astype(BF16)

    def body(p_ref, f_ref, sc_ref, ss_ref, dep_ref, cos_ref, sin_ref):
        ang = f_ref[...] * p_ref[...].astype(F32)
        cos_ref[...] = _dot_tn(_pieces(jnp.cos(ang), 3, 0), sc_ref[...])
        sin_ref[...] = _dot_tn(_pieces(jnp.sin(ang), 3, 0), ss_ref[...])

    full = lambda a: pl.BlockSpec(a.shape, lambda i: (0,) * a.ndim)
    return pl.pallas_call(
        body, name="rope_tables", grid=(T // tm,),
        in_specs=[pl.BlockSpec((1, tm), lambda i: (0, i)), full(inv_freq_col), full(sel_cos), full(sel_sin), DEP_SPEC],
        out_specs=[pl.BlockSpec((tm, 128), lambda i: (i, 0))] * 2,
        out_shape=[jax.ShapeDtypeStruct((T, 128), F32)] * 2,
        compiler_params=_cp("parallel"),
    )(pos_row, inv_freq_col, sel_cos, sel_sin, dep)


def _in_proj_fwd(x, cos, sin_s, mod6, norm1_w, w_pad, tm, dep):
    T = x.shape[0]

    def body(x_ref, cos_ref, sin_ref, mod_ref, nw_ref, w_hbm, dep_ref, qkv_ref, z_ref, xbc_ref, dt_ref, h_ref, w_vmem,
             sem):
        _load_resident(w_hbm, w_vmem, sem)
        xv = x_ref[...]
        r = lax.rsqrt(jnp.mean(xv * xv, axis=-1, keepdims=True) + EPS)
        h = (xv * r * nw_ref[...]) * (1.0 + mod_ref[1:2, :]) + mod_ref[0:1, :]
        hb = h.astype(BF16)
        h_ref[...] = hb
        proj = _dot(hb, w_vmem[...])
        cs, sn = cos_ref[...], sin_ref[...]
        for j in range(5):
            qkv_ref[:, 128 * j:128 * (j + 1)] = _rope_fwd(proj[:, 128 * j:128 * (j + 1)], cs, sn).astype(BF16)
        qkv_ref[:, 640:768] = proj[:, 640:768].astype(BF16)
        z_ref[...] = proj[:, 768:1280]
        xbc_ref[...] = proj[:, 1280:2304]
        dt_ref[...] = proj[:, 2304:2432]

    row = lambda w: pl.BlockSpec((tm, w), lambda i: (i, 0))
    full = lambda a: pl.BlockSpec(a.shape, lambda i: (0,) * a.ndim)
    return pl.pallas_call(
        body, name="in_proj_fwd", grid=(T // tm,),
        in_specs=[row(D), row(128), row(128), full(mod6), full(norm1_w), pl.BlockSpec(memory_space=pl.ANY), DEP_SPEC],
        out_specs=[row(768), row(512), row(1024), row(128), row(D)],
        out_shape=[jax.ShapeDtypeStruct((T, 768), BF16), jax.ShapeDtypeStruct((T, 512), F32),
                   jax.ShapeDtypeStruct((T, 1024), F32), jax.ShapeDtypeStruct((T, 128), F32),
                   jax.ShapeDtypeStruct((T, D), BF16)],
        scratch_shapes=[pltpu.VMEM((D, IN_PAD), BF16), pltpu.SemaphoreType.DMA],
        compiler_params=_cp("arbitrary"),
    )(x, cos, sin_s, mod6, norm1_w, w_pad, dep)


def _head_variants(pair, j):
    lane = _iota(pair.shape, 1)
    lo = lane < 64
    kv = j // 2
    ev = jnp.where(lo, pair, 0.0)
    od = jnp.where(lo, 0.0, pair)
    if kv == 0:
        od = pltpu.roll(od, 64, 1)
    else:
        ev = pltpu.roll(ev, 64, 1)
    return ev.astype(BF16), od.astype(BF16)


def _kv_variants(vcat):
    lane = _iota(vcat.shape, 1)
    lo = lane < 64
    v0 = jnp.where(lo, vcat, 0.0)
    v1 = jnp.where(lo, 0.0, vcat)
    out = {
        (0, 0): v0, (0, 1): pltpu.roll(v0, 64, 1),
        (1, 0): pltpu.roll(v1, 64, 1), (1, 1): v1,
    }
    return {k: v.astype(BF16) for k, v in out.items()}


def _fold_masks(n):
    upper = _iota((BLK, BLK), 1) > _iota((BLK, BLK), 0)
    return upper, upper & (n == 0)


def _attn_fwd(qkv, sinks):
    CPS = ATTN_BPS
    T = qkv.shape[0]
    nsteps = T // (CPS * BLK)

    def body(sink_ref, q_ref, kc_ref, kp_ref, vc_ref, vp_ref, o_ref, lse_ref):
        for sub in range(CPS):
            rows, before = slice(BLK * sub, BLK * (sub + 1)), slice(BLK * (sub - 1), BLK * sub)
            block(pl.program_id(0) * CPS + sub, sink_ref, q_ref.at[rows, :], kc_ref.at[rows, :],
                  kp_ref if sub == 0 else kc_ref.at[before, :], vc_ref.at[rows, :],
                  vp_ref if sub == 0 else vc_ref.at[before, :], o_ref.at[rows, :], lse_ref.at[rows, :])

    def block(n, sink_ref, q_ref, kc_ref, kp_ref, vc_ref, vp_ref, o_ref, lse_ref):
        vpv = _kv_variants(vp_ref[...].astype(F32))
        vcv = _kv_variants(vc_ref[...].astype(F32))
        q_all = jnp.concatenate(
            [v for j in range(4) for v in _head_variants(q_ref[:, 128 * j:128 * (j + 1)].astype(F32), j)], axis=0)
        s_prev = _dot_nt(q_all, kp_ref[...])
        s_cur = _dot_nt(q_all, kc_ref[...])
        upper, dead = _fold_masks(n)
        lane = _iota((BLK, 128), 1)
        lse_acc = jnp.zeros((BLK, 128), F32)
        for jj in range(4):
            acc = jnp.zeros((BLK, 128), F32)
            for par in range(2):
                h = 2 * jj + par
                rows = slice(h * BLK, (h + 1) * BLK)
                sink = sink_ref[0, h]
                s = jnp.where(dead, NEG, jnp.where(upper, s_prev[rows], s_cur[rows]) * 0.125)
                m = jnp.maximum(jnp.max(s, axis=1, keepdims=True), sink)
                p = jnp.exp(s - m)
                den = jnp.sum(p, axis=1, keepdims=True) + jnp.exp(sink - m)
                pn = p * (1.0 / den)
                acc = (acc + _dot(jnp.where(upper, pn, 0.0).astype(BF16), vpv[(jj // 2, par)])
                       + _dot(jnp.where(upper, 0.0, pn).astype(BF16), vcv[(jj // 2, par)]))
                lse_acc = jnp.where(lane == h, m + jnp.log(den), lse_acc)
            o_ref[:, 128 * jj:128 * (jj + 1)] = acc.astype(BF16)
        lse_ref[...] = lse_acc

    RB = CPS * BLK
    prev = lambda n: jnp.maximum(n * CPS - 1, 0)
    return pl.pallas_call(
        body, name="attn_fwd", grid=(nsteps,),
        in_specs=[pl.BlockSpec(memory_space=pltpu.SMEM),
                  pl.BlockSpec((RB, 512), lambda n: (n, 0)),
                  pl.BlockSpec((RB, 128), lambda n: (n, 4)),
                  pl.BlockSpec((BLK, 128), lambda n: (prev(n), 4)),
                  pl.BlockSpec((RB, 128), lambda n: (n, 5)),
                  pl.BlockSpec((BLK, 128), lambda n: (prev(n), 5))],
        out_specs=[pl.BlockSpec((RB, 512), lambda n: (n, 0)), pl.BlockSpec((RB, 128), lambda n: (n, 0))],
        out_shape=[jax.ShapeDtypeStruct((T, 512), BF16), jax.ShapeDtypeStruct((T, 128), F32)],
        compiler_params=_cp("parallel"),
    )(sinks, qkv, qkv, qkv, qkv, qkv)


def _attn_bwd(qkv, sinks, lse, dmix, cos, sin_s, dep):
    T = qkv.shape[0]
    nb = T // BLK

    def body(sink_ref, q_ref, kc_ref, kp_ref, vc_ref, vp_ref, lse_ref, do_ref, cq_ref, sq_ref, ck_ref, sk_ref,
             dep_ref, out_ref, ds_ref, dq_car, dk_car, dv_car):
        n = pl.program_id(0)
        lane = _iota((BLK, 128), 1)

        @pl.when(n == 0)
        def _():
            ds_ref[...] = jnp.zeros_like(ds_ref)
            dq_car[...] = jnp.zeros_like(dq_car)
            dk_car[...] = jnp.zeros_like(dk_car)
            dv_car[...] = jnp.zeros_like(dv_car)

        @pl.when(n < nb)
        def _():
            kp, kc, vp, vc = kp_ref[...], kc_ref[...], vp_ref[...], vc_ref[...]
            kpv = _kv_variants(kp.astype(F32))
            kcv = _kv_variants(kc.astype(F32))
            lse_v = lse_ref[...]
            q_all = jnp.concatenate(
                [v for j in range(4) for v in _head_variants(q_ref[:, 128 * j:128 * (j + 1)].astype(F32), j)], axis=0)
            do_all = jnp.concatenate(
                [v for j in range(4) for v in _head_variants(do_ref[:, 128 * j:128 * (j + 1)], j)], axis=0)
            s_prev, s_cur = _dot_nt(q_all, kp), _dot_nt(q_all, kc)
            dp_prev, dp_cur = _dot_nt(do_all, vp), _dot_nt(do_all, vc)
            upper, dead = _fold_masks(n)
            out_ref[:, 0:512] = dq_car[...]
            dsk = jnp.zeros((1, 128), F32)
            ds_u, ds_l, p_u, p_l = [], [], [], []
            for jj in range(4):
                dq_acc = jnp.zeros((BLK, 128), F32)
                for par in range(2):
                    h = 2 * jj + par
                    rows = slice(h * BLK, (h + 1) * BLK)
                    lse_h = jnp.sum(jnp.where(lane == h, lse_v, 0.0), axis=1, keepdims=True)
                    s = jnp.where(dead, NEG, jnp.where(upper, s_prev[rows], s_cur[rows]) * 0.125)
                    p = jnp.exp(s - lse_h)
                    dp = jnp.where(upper, dp_prev[rows], dp_cur[rows])
                    delta = jnp.sum(p * dp, axis=1, keepdims=True)
                    ds = p * (dp - delta) * 0.125
                    dsu, dsl = jnp.where(upper, ds, 0.0).astype(BF16), jnp.where(upper, 0.0, ds).astype(BF16)
                    dq_acc = dq_acc + _dot(dsu, kpv[(jj // 2, par)]) + _dot(dsl, kcv[(jj // 2, par)])
                    ds_u.append(dsu)
                    ds_l.append(dsl)
                    p_u.append(jnp.where(upper, p, 0.0).astype(BF16))
                    p_l.append(jnp.where(upper, 0.0, p).astype(BF16))
                    dsk = dsk + jnp.where(lane[0:1] == h, -jnp.sum(jnp.exp(sink_ref[0, h] - lse_h) * delta), 0.0)
                dq_car[:, 128 * jj:128 * (jj + 1)] = _rope_bwd(dq_acc, cq_ref[...], sq_ref[...]).astype(BF16)
            stack = lambda parts: jnp.concatenate(parts, axis=0)
            dk_prev, dk_cur = _dot_tn(stack(ds_u), q_all), _dot_tn(stack(ds_l), q_all)
            dv_prev, dv_cur = _dot_tn(stack(p_u), do_all), _dot_tn(stack(p_l), do_all)
            ds_ref[...] += dsk
            out_ref[:, 512:640] = _rope_bwd(dk_car[...] + dk_prev, ck_ref[...], sk_ref[...]).astype(BF16)
            out_ref[:, 640:768] = (dv_car[...] + dv_prev).astype(BF16)
            dk_car[...] = dk_cur
            dv_car[...] = dv_cur

        @pl.when(n == nb)
        def _():
            out_ref[:, 0:512] = dq_car[...]
            out_ref[:, 512:640] = _rope_bwd(dk_car[...], ck_ref[...], sk_ref[...]).astype(BF16)
            out_ref[:, 640:768] = dv_car[...].astype(BF16)

    cur = lambda n: jnp.minimum(n, nb - 1)
    prev = lambda n: jnp.maximum(cur(n) - 1, 0)
    outb = lambda n: jnp.maximum(n - 1, 0)
    return pl.pallas_call(
        body, name="attn_bwd", grid=(nb + 1,),
        in_specs=[pl.BlockSpec(memory_space=pltpu.SMEM),
                  pl.BlockSpec((BLK, 512), lambda n: (cur(n), 0)),
                  pl.BlockSpec((BLK, 128), lambda n: (cur(n), 4)),
                  pl.BlockSpec((BLK, 128), lambda n: (prev(n), 4)),
                  pl.BlockSpec((BLK, 128), lambda n: (cur(n), 5)),
                  pl.BlockSpec((BLK, 128), lambda n: (prev(n), 5)),
                  pl.BlockSpec((BLK, 128), lambda n: (cur(n), 0)),
                  pl.BlockSpec((BLK, 512), lambda n: (cur(n), 0)),
                  pl.BlockSpec((BLK, 128), lambda n: (cur(n), 0)),
                  pl.BlockSpec((BLK, 128), lambda n: (cur(n), 0)),
                  pl.BlockSpec((BLK, 128), lambda n: (outb(n), 0)),
                  pl.BlockSpec((BLK, 128), lambda n: (outb(n), 0)), DEP_SPEC],
        out_specs=[pl.BlockSpec((BLK, 768), lambda n: (outb(n), 0)), pl.BlockSpec((1, 128), lambda n: (0, 0))],
        out_shape=[jax.ShapeDtypeStruct((T, 768), BF16), jax.ShapeDtypeStruct((1, 128), F32)],
        scratch_shapes=[pltpu.VMEM((BLK, 512), BF16), pltpu.VMEM((BLK, 128), F32), pltpu.VMEM((BLK, 128), F32)],
        compiler_params=_cp("arbitrary"),
    )(sinks, qkv, qkv, qkv, qkv, qkv, lse, dmix, cos, sin_s, cos, sin_s, dep)


def _ssd_mats():
    e = jnp.arange(SW)[None, :] // HD == jnp.arange(128)[:, None]
    tri = jnp.arange(BLK)[None, :] <= jnp.arange(BLK)[:, None]
    return (jnp.tile(e, (3, 1)).astype(BF16), jnp.tile(e.T, (2, 1)).astype(BF16),
            jnp.tile(tri, (1, 3)).astype(BF16), jnp.tile(tri.T, (1, 3)).astype(BF16))


def _pieces(x, n, axis):
    out, r = [], x
    for i in range(n):
        p = r.astype(BF16)
        out.append(p)
        if i + 1 < n:
            r = r - p.astype(F32)
    return jnp.concatenate(out, axis=axis)


def _expand(x, e3):
    return _dot(_pieces(x, 3, 1), e3)


def _head_sums(x, et2):
    return _dot(_pieces(x, 2, 1), et2)


def _run_sum(tri3, x):
    return _dot(tri3, _pieces(x, 3, 0))


def _shift_down(u, tail, j):
    rolled = pltpu.roll(u, j, 0)
    first = jnp.where(_iota(tail.shape, 0) < j, pltpu.roll(tail, j, 0), rolled[0:8])
    return jnp.concatenate([first, rolled[8:]], axis=0)


def _shift_up(d, head, j):
    rolled = pltpu.roll(d, BLK - j, 0)
    last = jnp.where(_iota(head.shape, 0) >= 8 - j, pltpu.roll(head, 8 - j, 0), rolled[BLK - 8:])
    return jnp.concatenate([rolled[:BLK - 8], last], axis=0)


def _ssd_parts(dtr, dtb, alog, e3, tril3):
    xx = dtr + dtb
    dt = jnp.maximum(xx, 0.0) + jnp.log(1.0 + jnp.exp(-jnp.abs(xx)))
    a_neg = -jnp.exp(alog)
    tril = _iota((BLK, BLK), 1) <= _iota((BLK, BLK), 0)
    cs = _run_sum(tril3, dt * a_neg)
    csx = _expand(cs, e3)
    last = csx[BLK - 1:BLK, :]
    return dict(xx=xx, dt=dt, a_neg=a_neg, tril=tril, cs=cs, cs_t=cs.T,
                ecsx=jnp.exp(csx), dtex=jnp.exp(last - csx), cdx=jnp.exp(last), dtx=_expand(dt, e3))


def _decay(parts, h):
    seg = parts["cs"][:, h:h + 1] - parts["cs_t"][h:h + 1, :]
    return jnp.exp(jnp.where(parts["tril"], seg, NEG))


def _group_cols(a, g):
    return a[:, 256 * g:256 * (g + 1)]


def _ssd_fwd(xbc, z, dtr, conv_w, conv_b, dtb, alog, dskx, ssm_w, mats, dep):
    CPS = SSD_FWD_CPS
    T = xbc.shape[0]
    nc = T // BLK

    def body(u_ref, tail_ref, z_ref, dtr_ref, cw_ref, cb_ref, dtb_ref, al_ref, dk_ref, sw_ref, e3_ref, tril3_ref,
             dep_ref, yn_ref, yp_ref, st_ref, co_ref, s_scr):
        n = pl.program_id(0)

        @pl.when(n == 0)
        def _():
            s_scr[...] = jnp.zeros_like(s_scr)

        lane = _iota((BLK, 128), 1)
        lo = lane < 64
        for sub in range(CPS):
            rows = slice(BLK * sub, BLK * (sub + 1))
            u = u_ref[rows, :]
            tail = jnp.where(n > 0, tail_ref[...], 0.0) if sub == 0 else u_ref[BLK * sub - 8:BLK * sub, :]
            co = cb_ref[...] + cw_ref[3:4, :] * u
            for j in range(1, CONVK):
                co = co + cw_ref[3 - j:4 - j, :] * _shift_down(u, tail, j)
            co_ref[rows, :] = co
            xc = co * _sigmoid(co)
            pt = _ssd_parts(dtr_ref[rows, :], dtb_ref[...], al_ref[...], e3_ref[...], tril3_ref[...])
            xs = xc[:, :SW]
            bm = [xc[:, 512:640].astype(BF16), xc[:, 640:768].astype(BF16)]
            cm = [xc[:, 768:896].astype(BF16), xc[:, 896:1024].astype(BF16)]
            s_in = s_scr[...]
            st_ref[sub] = s_in
            xdt = xs * pt["dtx"]
            xde = (xdt * pt["dtex"]).astype(BF16)
            ys, s_new = [], []
            for g in range(2):
                cb = _dot_nt(cm[g], bm[g])
                yoff = _dot(cm[g], _group_cols(s_in, g).astype(BF16))
                s_new.append(_dot_tn(bm[g], _group_cols(xde, g)))
                for jj in range(2):
                    j = 2 * g + jj
                    chunk = xdt[:, 128 * j:128 * (j + 1)]
                    g_ev = (cb * _decay(pt, 2 * j)).astype(BF16)
                    g_od = (cb * _decay(pt, 2 * j + 1)).astype(BF16)
                    yd = (_dot(g_ev, jnp.where(lo, chunk, 0.0).astype(BF16))
                          + _dot(g_od, jnp.where(lo, 0.0, chunk).astype(BF16)))
                    ys.append(yd + yoff[:, 128 * jj:128 * (jj + 1)] * pt["ecsx"][:, 128 * j:128 * (j + 1)])
            y = jnp.concatenate(ys, axis=1) + xs * dk_ref[...]
            s_scr[...] = s_in * pt["cdx"] + jnp.concatenate(s_new, axis=1)
            yp_ref[rows, :] = y
            zv = z_ref[rows, :]
            yz = y * (zv * _sigmoid(zv))
            outs = []
            for g in range(2):
                yg = _group_cols(yz, g)
                outs.append(yg * lax.rsqrt(jnp.mean(yg * yg, axis=-1, keepdims=True) + EPS))
            yn_ref[rows, :] = (jnp.concatenate(outs, axis=1) * sw_ref[...]).astype(BF16)

    e3, _, tril3, _ = mats
    RB = CPS * BLK
    tail8 = lambda n: jnp.maximum(n * (RB // 8) - 1, 0)
    full = lambda a: pl.BlockSpec(a.shape, lambda n: (0,) * a.ndim)
    return pl.pallas_call(
        body, name="ssd_fwd", grid=(nc // CPS,),
        in_specs=[pl.BlockSpec((RB, CONVC), lambda n: (n, 0)), pl.BlockSpec((8, CONVC), lambda n: (tail8(n), 0)),
                  pl.BlockSpec((RB, SW), lambda n: (n, 0)), pl.BlockSpec((RB, 128), lambda n: (n, 0)),
                  full(conv_w), full(conv_b), full(dtb), full(alog), full(dskx), full(ssm_w), full(e3), full(tril3),
                  DEP_SPEC],
        out_specs=[pl.BlockSpec((RB, SW), lambda n: (n, 0)), pl.BlockSpec((RB, SW), lambda n: (n, 0)),
                   pl.BlockSpec((CPS, NST, SW), lambda n: (n, 0, 0)), pl.BlockSpec((RB, CONVC), lambda n: (n, 0))],
        out_shape=[jax.ShapeDtypeStruct((T, SW), BF16), jax.ShapeDtypeStruct((T, SW), F32),
                   jax.ShapeDtypeStruct((nc, NST, SW), F32), jax.ShapeDtypeStruct((T, CONVC), F32)],
        scratch_shapes=[pltpu.VMEM((NST, SW), F32)],
        compiler_params=_cp("arbitrary"),
    )(xbc, xbc, z, dtr, conv_w, conv_b, dtb, alog, dskx, ssm_w, e3, tril3, dep)


def _ssd_bwd(xbc, co_all, z, dtr, ypre, states, dmix, conv_w, dtb, alog, dskx, ssm_w, mats, dep):
    T = xbc.shape[0]
    nsteps = T // (CPS * BLK)

    def body(*refs):
        per_chunk, consts, out_ref, carried = refs[:7], refs[7:16], refs[17], refs[18:]
        i = pl.program_id(0)

        @pl.when(i == 0)
        def _():
            for r in carried:
                r[...] = jnp.zeros_like(r)

        for sub in reversed(range(CPS)):
            rows = slice(BLK * sub, BLK * (sub + 1))
            views = [r.at[sub:sub + 1] if k == 5 else r.at[rows, :] for k, r in enumerate(per_chunk)]
            chunk(*views, *consts, out_ref.at[rows, :], *carried)

        @pl.when(i == nsteps - 1)
        def _():
            dsk_ref, dskx_scr = carried[3], carried[8]
            dsk_ref[...] = _head_sums(jnp.broadcast_to(dskx_scr[...], (8, SW)), consts[6][...])[0:1]

    def chunk(u_ref, co_ref, z_ref, dtr_ref, yp_ref, st_ref, dyn_ref, cw_ref, dtb_ref, al_ref, dk_ref, sw_ref,
              e3_ref, et2_ref, tril3_ref, triu3_ref,
              out_ref, dcw_ref, dcb_ref, dsw_ref, dsk_ref, ddtb_ref, dav_ref, ds_scr, dco_scr, dskx_scr):
        co = co_ref[...]
        sg = _sigmoid(co)
        xc = co * sg
        pt = _ssd_parts(dtr_ref[...], dtb_ref[...], al_ref[...], e3_ref[...], tril3_ref[...])
        dtx, ecsx, dtex, cdx = pt["dtx"], pt["ecsx"], pt["dtex"], pt["cdx"]
        xs = xc[:, :SW]
        bm = [xc[:, 512:640].astype(BF16), xc[:, 640:768].astype(BF16)]
        cm = [xc[:, 768:896].astype(BF16), xc[:, 896:1024].astype(BF16)]
        s_in = st_ref[0]
        ds_out = ds_scr[...]
        e_t = et2_ref[...]

        zv = z_ref[...]
        sz = _sigmoid(zv)
        silu_z = zv * sz
        ypre = yp_ref[...]
        yz = ypre * silu_z
        dyn = dyn_ref[...]
        sw = sw_ref[...]
        dyz, yns = [], []
        for g in range(2):
            yg = _group_cols(yz, g)
            r = lax.rsqrt(jnp.mean(yg * yg, axis=-1, keepdims=True) + EPS)
            yn = yg * r
            dg = _group_cols(dyn, g) * _group_cols(sw, g)
            dyz.append(r * (dg - yn * jnp.mean(dg * yn, axis=-1, keepdims=True)))
            yns.append(yn)
        dyz = jnp.concatenate(dyz, axis=1)
        dsw_ref[...] += jnp.sum(dyn * jnp.concatenate(yns, axis=1), axis=0, keepdims=True)
        dy = dyz * silu_z
        dz = dyz * ypre * (sz * (1.0 + zv * (1.0 - sz)))

        xdt = xs * dtx
        xdt_b = xdt.astype(BF16)
        edy = (ecsx * dy).astype(BF16)
        xde = (xdt * dtex).astype(BF16)
        lane = _iota((BLK, 128), 1)
        lo = lane < 64
        row8 = _iota((8, 128), 0)
        dcs = jnp.zeros((BLK, 128), F32)
        col_rows = jnp.zeros((8, 128), F32)
        dxdt, bds, yoff, dbs, dcs_g, ds_new = [], [], [], [], [], []
        for g in range(2):
            s_g = _group_cols(s_in, g).astype(BF16)
            dso_g = _group_cols(ds_out, g).astype(BF16)
            cb = _dot_nt(cm[g], bm[g])
            bds.append(_dot(bm[g], dso_g))
            yoff.append(_dot(cm[g], s_g))
            dcb_g = jnp.zeros((BLK, BLK), F32)
            for jj in range(2):
                j = 2 * g + jj
                dy_c = dy[:, 128 * j:128 * (j + 1)]
                xdt_c = xdt_b[:, 128 * j:128 * (j + 1)]
                acc = jnp.zeros((BLK, 128), F32)
                for par in range(2):
                    h = 2 * j + par
                    lm = _decay(pt, h)
                    gm = cb * lm
                    dy_m = (jnp.where(lo, dy_c, 0.0) if par == 0 else jnp.where(lo, 0.0, dy_c)).astype(BF16)
                    dg_h = _dot_nt(dy_m, xdt_c)
                    w_h = dg_h * gm
                    dcs = dcs + jnp.where(lane == h, jnp.sum(w_h, axis=1, keepdims=True), 0.0)
                    col_rows = col_rows + jnp.where(row8 == h, jnp.sum(w_h, axis=0, keepdims=True), 0.0)
                    dcb_g = dcb_g + dg_h * lm
                    acc = acc + _dot_tn(gm.astype(BF16), dy_m)
                dxdt.append(acc)
            dcb_b = dcb_g.astype(BF16)
            dcs_g.append(_dot(dcb_b, bm[g]) + _dot_nt(_group_cols(edy, g), s_g))
            dbs.append(_dot_tn(dcb_b, cm[g]) + _dot_nt(_group_cols(xde, g), dso_g))
            ds_new.append(_dot_tn(cm[g], _group_cols(edy, g)))
        bds = jnp.concatenate(bds, axis=1)
        yoff = jnp.concatenate(yoff, axis=1) * ecsx
        dxdt = jnp.concatenate(dxdt, axis=1) + dtex * bds
        ds_scr[...] = cdx * ds_out + jnp.concatenate(ds_new, axis=1)

        t_m = _head_sums(dtex * xdt * bds, e_t)
        colsum_t = jnp.concatenate([col_rows, jnp.zeros((BLK - 8, 128), F32)], axis=0).T
        cd = jnp.exp(pt["cs"][BLK - 1:BLK, :])
        sds = jnp.sum(s_in * ds_out, axis=0, keepdims=True)
        last_row = jnp.sum(t_m, axis=0, keepdims=True) + cd * _head_sums(jnp.broadcast_to(sds, (8, SW)), e_t)[0:1]
        dcs = dcs - colsum_t + _head_sums(dy * yoff, e_t) - t_m
        dcs = dcs + jnp.where(_iota((BLK, 128), 0) == BLK - 1, last_row, 0.0)
        da = _run_sum(triu3_ref[...], dcs)
        dt = pt["dt"]
        ddt = da * pt["a_neg"] + _head_sums(dxdt * xs, e_t)
        dav_ref[...] += jnp.sum(da * dt, axis=0, keepdims=True)
        ddtr = ddt * _sigmoid(pt["xx"])
        ddtb_ref[...] += jnp.sum(ddtr, axis=0, keepdims=True)
        dxs = dxdt * dtx + dy * dk_ref[...]
        dskx_scr[...] += jnp.sum(dy * xs, axis=0, keepdims=True)
        dxc = jnp.concatenate([dxs, dbs[0], dbs[1], dcs_g[0], dcs_g[1]], axis=1)
        dco = dxc * (sg * (1.0 + co * (1.0 - sg)))

        dcb_ref[...] += jnp.sum(dco, axis=0, keepdims=True)
        u = u_ref[...]
        head = dco_scr[...]
        du = jnp.zeros_like(dco)
        for j in range(CONVK):
            up_j = dco if j == 0 else _shift_up(dco, head, j)
            dcw_ref[3 - j:4 - j, :] += jnp.sum(up_j * u, axis=0, keepdims=True)
            du = du + cw_ref[3 - j:4 - j, :] * up_j
        dco_scr[...] = dco[0:8]
        out_ref[:, 0:512] = dz.astype(BF16)
        out_ref[:, 512:1536] = du.astype(BF16)
        out_ref[:, 1536:1664] = ddtr.astype(BF16)

    e3, et2, tril3, triu3 = mats
    RB = CPS * BLK
    rev = lambda i: nsteps - 1 - i
    full = lambda a: pl.BlockSpec(a.shape, lambda i: (0,) * a.ndim)
    acc = lambda r, c: pl.BlockSpec((r, c), lambda i: (0, 0))
    return pl.pallas_call(
        body, name="ssd_bwd", grid=(nsteps,),
        in_specs=[pl.BlockSpec((RB, CONVC), lambda i: (rev(i), 0)), pl.BlockSpec((RB, CONVC), lambda i: (rev(i), 0)),
                  pl.BlockSpec((RB, SW), lambda i: (rev(i), 0)), pl.BlockSpec((RB, 128), lambda i: (rev(i), 0)),
                  pl.BlockSpec((RB, SW), lambda i: (rev(i), 0)), pl.BlockSpec((CPS, NST, SW), lambda i: (rev(i), 0, 0)),
                  pl.BlockSpec((RB, SW), lambda i: (rev(i), 1)),
                  full(conv_w), full(dtb), full(alog), full(dskx), full(ssm_w),
                  full(e3), full(et2), full(tril3), full(triu3), DEP_SPEC],
        out_specs=[pl.BlockSpec((RB, 1664), lambda i: (rev(i), 0)),
                   acc(CONVK, CONVC), acc(1, CONVC), acc(1, SW), acc(1, 128), acc(1, 128), acc(1, 128)],
        out_shape=[jax.ShapeDtypeStruct((T, 1664), BF16),
                   jax.ShapeDtypeStruct((CONVK, CONVC), F32), jax.ShapeDtypeStruct((1, CONVC), F32),
                   jax.ShapeDtypeStruct((1, SW), F32), jax.ShapeDtypeStruct((1, 128), F32),
                   jax.ShapeDtypeStruct((1, 128), F32), jax.ShapeDtypeStruct((1, 128), F32)],
        scratch_shapes=[pltpu.VMEM((NST, SW), F32), pltpu.VMEM((8, CONVC), F32), pltpu.VMEM((1, SW), F32)],
        compiler_params=_cp("arbitrary"),
    )(xbc, co_all, z, dtr, ypre, states, dmix, conv_w, dtb, alog, dskx, ssm_w, e3, et2, tril3, triu3, dep)


def _mix_ffn(x, attn, ynorm, tgt, mod6, norm2_w, final_w, w_out, w_gu, w_gu_own, s_arr, w_dn, tm):
    T = x.shape[0]
    nt = T // tm

    def body(x_ref, a_ref, y_ref, t_ref, mod_ref, n2_ref, fw_ref, wo_hbm, wgu_hbm, own_hbm, s_ref, wdn_hbm,
             sq_ref, dmix_ref, dx1_ref, h2_ref, act_ref, df_ref, dgu_ref, do_ref, sm_ref,
             wo, wgu, wdn, sems):
        i = pl.program_id(0)

        @pl.when(i == 0)
        def _():
            cps = [pltpu.make_async_copy(s, d, sems.at[k]) for k, (s, d) in
                   enumerate(((wo_hbm, wo), (wgu_hbm, wgu), (wdn_hbm, wdn)))]
            for c in cps:
                c.start()
            for c in cps:
                c.wait()
            own = pltpu.make_async_copy(
                own_hbm, wgu.at[:, pl.ds(pl.multiple_of(s_ref[0] * GU_SH, 128), GU_SH)], sems.at[3])
            own.start()
            own.wait()
            sq_ref[...] = jnp.zeros_like(sq_ref)
            sm_ref[...] = jnp.zeros_like(sm_ref)

        gate1, shift2, scale2, gate2 = mod_ref[2:3, :], mod_ref[3:4, :], mod_ref[4:5, :], mod_ref[5:6, :]
        n2w, fw = n2_ref[...], fw_ref[...]
        o = _dot(a_ref[...], wo[0:AW, :]) + _dot(y_ref[...], wo[AW:D, :])
        x1 = x_ref[...] + gate1 * o
        r2 = lax.rsqrt(jnp.mean(x1 * x1, axis=-1, keepdims=True) + EPS)
        xh2 = x1 * r2
        n2 = xh2 * n2w
        h2b = (n2 * (1.0 + scale2) + shift2).astype(BF16)
        h2_ref[...] = h2b
        f = jnp.zeros((tm, D), F32)
        saved = []
        for a, b in FF_SPLITS:
            gp = _dot(h2b, wgu[:, a:b])
            upj = _dot(h2b, wgu[:, DFF + a:DFF + b])
            sg = _sigmoid(gp)
            sl = gp * sg
            actb = (sl * upj).astype(BF16)
            act_ref[:, a:b] = actb
            f = f + _dot(actb, wdn[a:b, :])
            saved.append((gp, upj, sg, sl))
        x2 = x1 + gate2 * f
        r3 = lax.rsqrt(jnp.mean(x2 * x2, axis=-1, keepdims=True) + EPS)
        xh3 = x2 * r3
        err = xh3 * fw - t_ref[...]
        sq_ref[...] += jnp.sum(err * err, axis=0, keepdims=True)
        dy = err * (1.0 / D)
        dfw = jnp.sum(dy * xh3, axis=0, keepdims=True)
        dxh3 = dy * fw
        dx2 = r3 * (dxh3 - xh3 * jnp.mean(dxh3 * xh3, axis=-1, keepdims=True))
        dgate2 = jnp.sum(dx2 * f, axis=0, keepdims=True)
        dfb = (dx2 * gate2).astype(BF16)
        df_ref[...] = dfb
        dh2 = jnp.zeros((tm, D), F32)
        for (a, b), (gp, upj, sg, sl) in zip(FF_SPLITS, saved):
            dact = _dot_nt(dfb, wdn[a:b, :])
            dg = (dact * upj * (sg * (1.0 + gp * (1.0 - sg)))).astype(BF16)
            du = (dact * sl).astype(BF16)
            dgu_ref[:, a:b] = dg
            dgu_ref[:, DFF + a:DFF + b] = du
            dh2 = dh2 + _dot_nt(dg, wgu[:, a:b]) + _dot_nt(du, wgu[:, DFF + a:DFF + b])
        dshift2 = jnp.sum(dh2, axis=0, keepdims=True)
        dscale2 = jnp.sum(dh2 * n2, axis=0, keepdims=True)
        dn2 = dh2 * (1.0 + scale2)
        dn2w = jnp.sum(dn2 * xh2, axis=0, keepdims=True)
        dxh2 = dn2 * n2w
        dx1 = dx2 + r2 * (dxh2 - xh2 * jnp.mean(dxh2 * xh2, axis=-1, keepdims=True))
        dx1_ref[...] = dx1
        dgate1 = jnp.sum(dx1 * o, axis=0, keepdims=True)
        dob = (dx1 * gate1).astype(BF16)
        do_ref[...] = dob
        dmix_ref[...] = _dot_nt(dob, wo[...])
        sm_ref[...] += jnp.concatenate(
            [dfw, dn2w, dshift2, dscale2, dgate2, dgate1, jnp.zeros((2, D), F32)], axis=0)

    row = lambda w: pl.BlockSpec((tm, w), lambda i: (i, 0))
    full = lambda a: pl.BlockSpec(a.shape, lambda i: (0,) * a.ndim)
    anyspec = pl.BlockSpec(memory_space=pl.ANY)
    return pl.pallas_call(
        body, name="mix_ffn", grid=(nt,),
        in_specs=[row(D), row(AW), row(SW), row(D), full(mod6), full(norm2_w), full(final_w), anyspec, anyspec, anyspec,
                  pl.BlockSpec(memory_space=pltpu.SMEM), anyspec],
        out_specs=[pl.BlockSpec((1, D), lambda i: (0, 0)), row(D), row(D), row(D),
                   row(DFF), row(D), row(2 * DFF), row(D), pl.BlockSpec((8, D), lambda i: (0, 0))],
        out_shape=[jax.ShapeDtypeStruct((1, D), F32), jax.ShapeDtypeStruct((T, D), F32), jax.ShapeDtypeStruct((T, D), F32),
                   jax.ShapeDtypeStruct((T, D), BF16), jax.ShapeDtypeStruct((T, DFF), BF16),
                   jax.ShapeDtypeStruct((T, D), BF16), jax.ShapeDtypeStruct((T, 2 * DFF), BF16),
                   jax.ShapeDtypeStruct((T, D), BF16), jax.ShapeDtypeStruct((8, D), F32)],
        scratch_shapes=[pltpu.VMEM((D, D), BF16), pltpu.VMEM((D, 2 * DFF), BF16), pltpu.VMEM((DFF, D), BF16),
                        pltpu.SemaphoreType.DMA((4,))],
        compiler_params=_cp("arbitrary"),
    )(x, attn, ynorm, tgt, mod6, norm2_w, final_w, w_out, w_gu, w_gu_own, s_arr, w_dn)


def _in_proj_bwd(x, dx1, dqkv, dzxd, mod6, norm1_w, w_pad, tm, dep):
    T = x.shape[0]

    def body(x_ref, dx1_ref, dq_ref, dz_ref, mod_ref, nw_ref, w_hbm, dep_ref, gx_ref, sm_ref, w_vmem, sem):
        _load_resident(w_hbm, w_vmem, sem)

        @pl.when(pl.program_id(0) == 0)
        def _():
            sm_ref[...] = jnp.zeros_like(sm_ref)

        nw = nw_ref[...]
        scale1 = mod_ref[1:2, :]
        sums = jnp.zeros((8, D), F32)
        for rows in (slice(0, tm // 2), slice(tm // 2, tm)):
            dh = _dot_nt(dq_ref[rows, :], w_vmem[:, 0:768]) + _dot_nt(dz_ref[rows, :], w_vmem[:, 768:IN_PAD])
            xv = x_ref[rows, :]
            r = lax.rsqrt(jnp.mean(xv * xv, axis=-1, keepdims=True) + EPS)
            xh = xv * r
            n1 = xh * nw
            dshift = jnp.sum(dh, axis=0, keepdims=True)
            dscale = jnp.sum(dh * n1, axis=0, keepdims=True)
            dn = dh * (1.0 + scale1)
            dnw = jnp.sum(dn * xh, axis=0, keepdims=True)
            dxh = dn * nw
            gx_ref[rows, :] = dx1_ref[rows, :] + r * (dxh - xh * jnp.mean(dxh * xh, axis=-1, keepdims=True))
            sums = sums + jnp.concatenate([dnw, dshift, dscale, jnp.zeros((5, D), F32)], axis=0)
        sm_ref[...] += sums

    row = lambda w: pl.BlockSpec((tm, w), lambda i: (i, 0))
    full = lambda a: pl.BlockSpec(a.shape, lambda i: (0,) * a.ndim)
    return pl.pallas_call(
        body, name="in_proj_bwd", grid=(T // tm,),
        in_specs=[row(D), row(D), row(768), row(1664), full(mod6), full(norm1_w), pl.BlockSpec(memory_space=pl.ANY),
                  DEP_SPEC],
        out_specs=[row(D), pl.BlockSpec((8, D), lambda i: (0, 0))],
        out_shape=[jax.ShapeDtypeStruct((T, D), F32), jax.ShapeDtypeStruct((8, D), F32)],
        scratch_shapes=[pltpu.VMEM((D, IN_PAD), BF16), pltpu.SemaphoreType.DMA],
        compiler_params=_cp("arbitrary"),
    )(x, dx1, dqkv, dzxd, mod6, norm1_w, w_pad, dep)


def _tn_matmul(a, b, K, N, tt, name, dep):
    T = a.shape[0]
    ja, jb = a.shape[1] // K, b.shape[1] // N
    J = max(ja, jb)

    def body(a_ref, b_ref, dep_ref, o_ref):
        t = pl.program_id(1)
        prod = _dot_tn(a_ref[...], b_ref[...])

        @pl.when(t == 0)
        def _():
            o_ref[0] = prod

        @pl.when(t > 0)
        def _():
            o_ref[0] += prod

    return pl.pallas_call(
        body, name=name, grid=(J, T // tt),
        in_specs=[pl.BlockSpec((tt, K), lambda j, t: (t, j if ja > 1 else 0)),
                  pl.BlockSpec((tt, N), lambda j, t: (t, j if jb > 1 else 0)),
                  pl.BlockSpec((8, 128), lambda j, t: (0, 0))],
        out_specs=pl.BlockSpec((1, K, N), lambda j, t: (j, 0, 0)),
        out_shape=jax.ShapeDtypeStruct((J, K, N), F32),
        compiler_params=_cp("parallel", "arbitrary"),
    )(a, b, dep)


def _accumulate(o_ref, rows, prod):
    @pl.when(pl.program_id(0) == 0)
    def _():
        o_ref[rows, :] = prod

    @pl.when(pl.program_id(0) > 0)
    def _():
        o_ref[rows, :] += prod


def _tn_matmul_rows(a0, a1, b, tt, name, dep):
    T, K = a0.shape
    N = b.shape[1]

    def body(a0_ref, a1_ref, b_ref, dep_ref, o_ref):
        for k, a_ref in enumerate((a0_ref, a1_ref)):
            _accumulate(o_ref, slice(k * K, (k + 1) * K), _dot_tn(a_ref[...], b_ref[...]))

    tile = lambda w: pl.BlockSpec((tt, w), lambda t: (t, 0))
    return pl.pallas_call(
        body, name=name, grid=(T // tt,), in_specs=[tile(K), tile(K), tile(N), DEP_SPEC],
        out_specs=pl.BlockSpec((2 * K, N), lambda t: (0, 0)), out_shape=jax.ShapeDtypeStruct((2 * K, N), F32),
        compiler_params=_cp("arbitrary"),
    )(a0, a1, b, dep)


def _tn_matmul_cols(a, b0, b1, tt, name, dep):
    T, K = a.shape

    def body(a_ref, b0_ref, b1_ref, dep_ref, o0_ref, o1_ref):
        for b_ref, o_ref in ((b0_ref, o0_ref), (b1_ref, o1_ref)):
            _accumulate(o_ref, slice(None), _dot_tn(a_ref[...], b_ref[...]))

    tile = lambda w: pl.BlockSpec((tt, w), lambda t: (t, 0))
    whole = lambda w: pl.BlockSpec((K, w), lambda t: (0, 0))
    return pl.pallas_call(
        body, name=name, grid=(T // tt,), in_specs=[tile(K), tile(b0.shape[1]), tile(b1.shape[1]), DEP_SPEC],
        out_specs=[whole(b0.shape[1]), whole(b1.shape[1])],
        out_shape=[jax.ShapeDtypeStruct((K, b.shape[1]), F32) for b in (b0, b1)],
        compiler_params=_cp("arbitrary"),
    )(a, b0, b1, dep)


def _adam_math(w, g, m, v):
    m = B1 * m + (1.0 - B1) * g
    v = B2 * v + (1.0 - B2) * (g * g)
    m_hat = m / (1.0 - B1 ** STEP)
    v_hat = v / (1.0 - B2 ** STEP)
    delta = -LR * (m_hat / (jnp.sqrt(v_hat) + AEPS) + WD * w)
    return delta, m, v


def _adam_2d(w, mine, land, m, v, c_arr, rb, name, dep):
    R, C = w.shape
    nbh = R // 2 // rb

    def body(c_ref, w_ref, mine_ref, land_ref, m_ref, v_ref, dep_ref, go_ref, d_ref, mo_ref, vo_ref):
        g = jnp.where(pl.program_id(0) // nbh == c_ref[0], mine_ref[...], land_ref[...])
        d, mn, vn = _adam_math(w_ref[...], g, m_ref[...], v_ref[...])
        go_ref[...] = g
        d_ref[...] = d
        mo_ref[...] = mn
        vo_ref[...] = vn

    spec = pl.BlockSpec((rb, C), lambda i, c_ref: (i, 0))
    mine_spec = pl.BlockSpec((rb, C), lambda i, c_ref: (jnp.clip(i - c_ref[0] * nbh, 0, nbh - 1), 0))
    return pl.pallas_call(
        body, name=name,
        grid_spec=pltpu.PrefetchScalarGridSpec(
            num_scalar_prefetch=1, grid=(R // rb,), in_specs=[spec, mine_spec, spec, spec, spec, DEP_SPEC],
            out_specs=[spec] * 4),
        out_shape=[jax.ShapeDtypeStruct((R, C), F32)] * 4, compiler_params=_cp("parallel"),
    )(c_arr, w, mine, land, m, v, dep)


def _adam_w_in(w3, mine, land, m3, v3, c_arr):
    n = w3.shape[0]

    def body(c_ref, w_hbm, mine_ref, land_ref, m_hbm, v_hbm, g_hbm, d_hbm, mo_hbm, vo_hbm, bufs, sems):
        ins = [pltpu.make_async_copy(src.at[:, 0], bufs.at[k], sems.at[k]) for k, src in enumerate((w_hbm, m_hbm, v_hbm))]
        for cp in ins:
            cp.start()
        half = D // 2
        top = jnp.where(c_ref[0] == 0, mine_ref[...], land_ref[0:half, :])
        bot = jnp.where(c_ref[0] == 1, mine_ref[...], land_ref[half:D, :])
        g = jnp.concatenate([top, bot], axis=0)
        eye = (_iota((D, D), 0) == _iota((D, D), 1)).astype(BF16)
        g_t = jnp.zeros((n, D), F32)
        r = g
        for i in range(3):
            p = r.astype(BF16)
            g_t = g_t + _dot_tn(p, eye)
            if i < 2:
                r = r - p.astype(F32)
        for cp in ins:
            cp.wait()
        d, mn, vn = _adam_math(bufs[0], g_t, bufs[1], bufs[2])
        for k, val in enumerate((g_t, d, mn, vn)):
            bufs[3 + k] = val
        outs = [pltpu.make_async_copy(bufs.at[3 + k], dst.at[:, 0], sems.at[3 + k])
                for k, dst in enumerate((g_hbm, d_hbm, mo_hbm, vo_hbm))]
        for cp in outs:
            cp.start()
        for cp in outs:
            cp.wait()

    anyspec = pl.BlockSpec(memory_space=pl.ANY)
    vm = pl.BlockSpec(memory_space=pltpu.VMEM)
    return pl.pallas_call(
        body, name="adam_w_in",
        in_specs=[pl.BlockSpec(memory_space=pltpu.SMEM), anyspec, vm, vm, anyspec, anyspec], out_specs=[anyspec] * 4,
        out_shape=[jax.ShapeDtypeStruct(w3.shape, F32)] * 4,
        scratch_shapes=[pltpu.VMEM((7, n, D), F32), pltpu.SemaphoreType.DMA((7,))],
        compiler_params=pltpu.CompilerParams(vmem_limit_bytes=VMEM_LIMIT),
    )(c_arr, w3, mine, land, m3, v3)


def _adam_w_ada(gat, allv, s_arr, w, m, v, rb):
    R, C = w.shape

    def body(s_ref, c_ref, dm_ref, w_ref, m_ref, v_ref, g_ref, d_ref, mo_ref, vo_ref):
        cm = _rows_select(c_ref, rb)
        g = lax.dot_general(cm * _sigmoid(cm), _rows_select(dm_ref, C), (((0,), (0,)), ((), ())), precision=HI,
                            preferred_element_type=F32)
        d, mn, vn = _adam_math(w_ref[...], g, m_ref[...], v_ref[...])
        g_ref[...] = g
        d_ref[...] = d
        mo_ref[...] = mn
        vo_ref[...] = vn

    spec = pl.BlockSpec((rb, C), lambda i, s_ref: (i, 0))
    return pl.pallas_call(
        body, name="adam_w_ada",
        grid_spec=pltpu.PrefetchScalarGridSpec(
            num_scalar_prefetch=1, grid=(R // rb,),
            in_specs=[pl.BlockSpec((8, 1, rb), lambda i, s_ref: (0, 0, i)),
                      pl.BlockSpec((8, 1, C), lambda i, s_ref: (0, 0, s_ref[0])), spec, spec, spec],
            out_specs=[spec] * 4),
        out_shape=[jax.ShapeDtypeStruct((R, C), F32)] * 4, compiler_params=_cp("parallel"),
    )(s_arr, gat, allv, w, m, v)


def _adam_small(tot, segs, ws, ms, vs):
    k = len(ws)
    extra = [sg for sg in segs if not isinstance(sg, tuple)]
    ne = len(extra)

    def body(*refs):
        tot_ref, g_x = refs[0], list(refs[1:1 + ne])
        w, m, v = [refs[1 + ne + j * k:1 + ne + (j + 1) * k] for j in range(3)]
        g_o, d_o, m_o, v_o = [refs[1 + ne + (3 + j) * k:1 + ne + (4 + j) * k] for j in range(4)]
        for i in range(k):
            gi = tot_ref[:, segs[i][0]:segs[i][0] + segs[i][1]] if isinstance(segs[i], tuple) else g_x.pop(0)[...]
            d, mn, vn = _adam_math(w[i][...], gi, m[i][...], v[i][...])
            g_o[i][...] = gi
            d_o[i][...] = d
            m_o[i][...] = mn
            v_o[i][...] = vn

    shapes = [jax.ShapeDtypeStruct(w.shape, F32) for w in ws]
    vm = pl.BlockSpec(memory_space=pltpu.VMEM)
    outs = pl.pallas_call(
        body, name="adam_small", in_specs=[vm] * (1 + ne + 3 * k), out_specs=[vm] * (4 * k), out_shape=shapes * 4,
    )(tot, *extra, *ws, *ms, *vs)
    return outs[0:k], outs[k:2 * k], outs[2 * k:3 * k], outs[3 * k:4 * k]


def _pos():
    return lax.axis_index("x"), lax.axis_index("y"), lax.axis_index("c")


def _flip(v, bit):
    return 1 - v if bit else v


def _peer(k):
    x, y, c = _pos()
    return (_flip(x, (k >> 2) & 1), _flip(y, (k >> 1) & 1), _flip(c, k & 1))


def _logical(p):
    return 4 * p[0] + 2 * p[1] + p[2]


def _gather8(src_ref, dst_ref, send_sems, recv_sems):
    me = _logical(_pos())
    dst_ref[pl.ds(me, 1)] = src_ref[...][None]
    copies = []
    for k in range(1, 8):
        cp = pltpu.make_async_remote_copy(src_ref, dst_ref.at[me], send_sems.at[k - 1], recv_sems.at[k - 1],
                                          device_id=_peer(k), device_id_type=MESH)
        cp.start()
        copies.append(cp)
    for k in range(1, 8):
        pltpu.make_async_remote_copy(src_ref, dst_ref.at[_logical(_peer(k))], send_sems.at[k - 1], recv_sems.at[k - 1],
                                     device_id=_peer(k), device_id_type=MESH).wait_recv()
    for cp in copies:
        cp.wait_send()


def _rows_select(ref3, width):
    row = _iota((8, width), 0)
    out = jnp.zeros((8, width), F32)
    for i in range(8):
        out = jnp.where(row == i, ref3[i][:, 0:width], out)
    return out


def _mod_exchange(payload, w_ada_s, b_ada4):
    n_sh = w_ada_s.shape[1]

    def body(pay_ref, w_ref, b_ref, gat_ref, mod_ref, token, p3, sa, ra, sb, rb):
        token[...] = jnp.zeros_like(token)
        x, y, c = _pos()
        me = _logical((x, y, c))
        my_s = 2 * x + y
        _gather8(pay_ref, gat_ref, sa, ra)
        cmat = _rows_select(gat_ref, D)
        prod = _dot_hi(cmat * _sigmoid(cmat), w_ref[...])
        for b in range(8):
            p3[b] = prod[b:b + 1, :]
        mod_ref[pl.ds(my_s, 1)] = p3[pl.ds(me, 1)] + b_ref[pl.ds(my_s, 1)]
        ks = (2, 4, 6)
        copies = []
        for i, k in enumerate(ks):
            pr = _peer(k)
            cp = pltpu.make_async_remote_copy(p3.at[_logical(pr)], mod_ref.at[my_s], sb.at[i], rb.at[i],
                                              device_id=pr, device_id_type=MESH)
            cp.start()
            copies.append(cp)
        for i, k in enumerate(ks):
            pr = _peer(k)
            s_src = 2 * pr[0] + pr[1]
            pltpu.make_async_remote_copy(p3.at[0], mod_ref.at[s_src], sb.at[i], rb.at[i],
                                         device_id=pr, device_id_type=MESH).wait_recv()
            mod_ref[pl.ds(s_src, 1)] = mod_ref[pl.ds(s_src, 1)] + b_ref[pl.ds(s_src, 1)]
        for cp in copies:
            cp.wait_send()

    vm = pl.BlockSpec(memory_space=pltpu.VMEM)
    return pl.pallas_call(
        body, name="mod_exchange", in_specs=[vm, vm, vm], out_specs=[vm, vm, vm],
        out_shape=[jax.ShapeDtypeStruct((8, 1, payload.shape[1]), F32), jax.ShapeDtypeStruct((4, 1, n_sh), F32),
                   jax.ShapeDtypeStruct((8, 128), F32)],
        scratch_shapes=[pltpu.VMEM((8, 1, n_sh), F32), pltpu.SemaphoreType.DMA((7,)), pltpu.SemaphoreType.DMA((7,)),
                        pltpu.SemaphoreType.DMA((3,)), pltpu.SemaphoreType.DMA((3,))],
        compiler_params=pltpu.CompilerParams(vmem_limit_bytes=VMEM_LIMIT),
    )(payload, w_ada_s, b_ada4)


def _chips():
    x, y, _ = _pos()
    out = []
    for k in (1, 2, 3):
        px, py = _flip(x, (k >> 1) & 1), _flip(y, k & 1)
        out.append((px, py, 2 * px + py))
    return out


def _half_rows(ref, which):
    half = ref.shape[-2] // 2
    return pl.ds(pl.multiple_of(which * half, 8), half)


def _plan_small():
    def plan(refs):
        me = _logical(_pos())
        return [(refs[0], refs[1].at[me], _peer(k), refs[1].at[_logical(_peer(k))]) for k in range(1, 8)]
    return plan


def _small_sum(vec, land, me_arr):
    n = vec.shape[1]

    def body(me_ref, v_ref, land_ref, tot_ref, all_ref):
        tot = None
        for i in range(8):
            row = jnp.where(me_ref[0] == i, v_ref[...], land_ref[i])
            all_ref[i] = row
            tot = row if i == 0 else tot + row
        tot_ref[...] = tot

    return pl.pallas_call(
        body, name="small_sum",
        grid_spec=pltpu.PrefetchScalarGridSpec(
            num_scalar_prefetch=1, grid=(1,),
            in_specs=[pl.BlockSpec((1, n), lambda i, me_ref: (0, 0)), pl.BlockSpec((8, 1, n), lambda i, me_ref: (0, 0, 0))],
            out_specs=[pl.BlockSpec((1, n), lambda i, me_ref: (0, 0)),
                       pl.BlockSpec((8, 1, n), lambda i, me_ref: (0, 0, 0))]),
        out_shape=[jax.ShapeDtypeStruct((1, n), F32), jax.ShapeDtypeStruct((8, 1, n), F32)],
        compiler_params=_cp("arbitrary"),
    )(me_arr, vec, land)


def _add_half(g, sib, c_arr, rb, name):
    _, R, C = g.shape
    half = R // 2
    nb = half // rb

    def body(c_ref, g_ref, s_ref, o_ref):
        o_ref[...] = (g_ref[...] + s_ref[...]).astype(BF16)

    return pl.pallas_call(
        body, name=name,
        grid_spec=pltpu.PrefetchScalarGridSpec(
            num_scalar_prefetch=1, grid=(4, nb),
            in_specs=[pl.BlockSpec((1, rb, C), lambda s, i, c_ref: (s, c_ref[0] * nb + i, 0)),
                      pl.BlockSpec((1, rb, C), lambda s, i, c_ref: (s, i, 0))],
            out_specs=pl.BlockSpec((1, rb, C), lambda s, i, c_ref: (s, i, 0))),
        out_shape=jax.ShapeDtypeStruct((4, half, C), BF16),
        compiler_params=_cp("parallel", "parallel"),
    )(c_arr, g, sib)


def _add_half_in(gq, gz, sibq, sibz, c_arr, rb):
    half = D // 2
    nq = gq.shape[1]
    wide = -(-IN_SH // 128) * 128

    def sel(rows, first, lo):
        return (_iota((rows, wide), 0) + (first - lo) == _iota((rows, wide), 1)).astype(BF16)

    def body(c_ref, gq_ref, gz_ref, sq_ref, sz_ref, o_ref):
        q = (gq_ref[...] + sq_ref[...]).astype(BF16)
        z = (gz_ref[...] + sz_ref[...]).astype(BF16)
        for s in range(4):
            lo, hi = s * IN_SH, (s + 1) * IN_SH
            acc = jnp.zeros((rb, wide), F32)
            if lo < nq:
                a0, a1 = lo // 128 * 128, min(nq, -(-min(hi, nq) // 128) * 128)
                acc = acc + _dot(q[:, a0:a1], sel(a1 - a0, a0, lo))
            if hi > nq:
                a0, a1 = (max(lo, nq) - nq) // 128 * 128, -(-(hi - nq) // 128) * 128
                acc = acc + _dot(z[:, a0:a1], sel(a1 - a0, nq + a0, lo))
            o_ref[s] = acc[:, :IN_SH].astype(BF16)

    nb = half // rb
    mine = lambda w: pl.BlockSpec((rb, w), lambda i, c_ref: (c_ref[0] * nb + i, 0))
    sib = lambda w: pl.BlockSpec((rb, w), lambda i, c_ref: (i, 0))
    return pl.pallas_call(
        body, name="grad_add_in",
        grid_spec=pltpu.PrefetchScalarGridSpec(
            num_scalar_prefetch=1, grid=(nb,),
            in_specs=[mine(nq), mine(gz.shape[1]), sib(nq), sib(gz.shape[1])],
            out_specs=pl.BlockSpec((4, rb, IN_SH), lambda i, c_ref: (0, i, 0))),
        out_shape=jax.ShapeDtypeStruct((4, half, IN_SH), BF16),
        compiler_params=_cp("parallel"),
    )(c_arr, gq, gz, sibq, sibz)


def _sum4(parts, land, s_arr, rb, name):
    _, H, C = land.shape

    def body(s_ref, own_ref, r_ref, o_ref):
        own = own_ref[0].astype(F32)
        tot = jnp.zeros((rb, C), F32)
        for j in range(4):
            tot = tot + jnp.where(s_ref[0] == j, own, r_ref[j].astype(F32))
        o_ref[...] = tot

    return pl.pallas_call(
        body, name=name,
        grid_spec=pltpu.PrefetchScalarGridSpec(
            num_scalar_prefetch=1, grid=(H // rb,),
            in_specs=[pl.BlockSpec((1, rb, C), lambda i, s_ref: (s_ref[0], i, 0)),
                      pl.BlockSpec((4, rb, C), lambda i, s_ref: (0, i, 0))],
            out_specs=pl.BlockSpec((rb, C), lambda i, s_ref: (i, 0))),
        out_shape=jax.ShapeDtypeStruct((H, C), F32), compiler_params=_cp("parallel"),
    )(s_arr, parts, land)


HBM_SPEC = pl.BlockSpec(memory_space=pltpu.HBM)
SEM_SPEC = pl.BlockSpec(memory_space=pltpu.SEMAPHORE)
EFFECT = pltpu.SideEffectType.DATAFLOW_SIDE_EFFECTING


def _split_start(name, bufs, n_sem, plan, dep):
    nb = len(bufs)

    def body(*refs):
        ins, send, recv, token = refs[:nb], refs[nb + 1], refs[nb + 2], refs[-1]
        for i, (src, dst, dev, _) in enumerate(plan(ins)):
            pltpu.make_async_remote_copy(src, dst, send.at[i], recv.at[i], device_id=dev, device_id_type=MESH).start()
        token[...] = jnp.zeros_like(token)

    outs = pl.pallas_call(
        body, name=name,
        out_shape=(pltpu.SemaphoreType.DMA((n_sem,)), pltpu.SemaphoreType.DMA((n_sem,)),
                   *[pltpu.HBM(b.shape, b.dtype) for b in bufs], jax.ShapeDtypeStruct((8, 128), F32)),
        in_specs=[HBM_SPEC] * nb + [pl.BlockSpec(memory_space=pl.ANY)],
        out_specs=(SEM_SPEC, SEM_SPEC, *([HBM_SPEC] * nb), pl.BlockSpec(memory_space=pltpu.VMEM)),
        input_output_aliases={i: 2 + i for i in range(nb)},
        compiler_params=pltpu.CompilerParams(has_side_effects=EFFECT),
    )(*[pltpu.with_memory_space_constraint(b, pltpu.HBM) for b in bufs], dep)
    return outs[0], outs[1], list(outs[2:2 + nb]), outs[-1]


def _split_wait(name, send, recv, bufs, after, plan):
    nb = len(bufs)
    after = list(after) if isinstance(after, (list, tuple)) else [after]

    def body(*refs):
        ins, send_s, recv_s = refs[:nb], refs[nb], refs[nb + 1]
        for i, (src, dst, dev, mine) in enumerate(plan(ins)):
            pltpu.make_async_remote_copy(src, dst, send_s.at[i], recv_s.at[i], device_id=dev,
                                         device_id_type=MESH).wait_send()
            pltpu.make_async_remote_copy(src, mine, send_s.at[i], recv_s.at[i], device_id=dev,
                                         device_id_type=MESH).wait_recv()

    outs = pl.pallas_call(
        body, name=name, out_shape=[pltpu.HBM(b.shape, b.dtype) for b in bufs],
        in_specs=[HBM_SPEC] * nb + [SEM_SPEC, SEM_SPEC] + [HBM_SPEC] * len(after),
        out_specs=[HBM_SPEC] * nb, input_output_aliases={i: i for i in range(nb)},
        compiler_params=pltpu.CompilerParams(has_side_effects=EFFECT),
    )(*bufs, send, recv, *[pltpu.with_memory_space_constraint(a, pltpu.HBM) for a in after])
    return list(outs)


def _copies_now(name, bufs, n_sem, plan):
    nb = len(bufs)

    def body(*refs):
        ins, token, send, recv = refs[:nb], refs[2 * nb], refs[-2], refs[-1]
        token[...] = jnp.zeros_like(token)
        todo = plan(ins)
        for i, (src, dst, dev, _) in enumerate(todo):
            pltpu.make_async_remote_copy(src, dst, send.at[i], recv.at[i], device_id=dev, device_id_type=MESH).start()
        for i, (src, dst, dev, mine) in enumerate(todo):
            pltpu.make_async_remote_copy(src, mine, send.at[i], recv.at[i], device_id=dev, device_id_type=MESH).wait_recv()
        for i, (src, dst, dev, _) in enumerate(todo):
            pltpu.make_async_remote_copy(src, dst, send.at[i], recv.at[i], device_id=dev, device_id_type=MESH).wait_send()

    outs = pl.pallas_call(
        body, name=name,
        out_shape=[pltpu.HBM(b.shape, b.dtype) for b in bufs] + [jax.ShapeDtypeStruct((8, 128), F32)],
        in_specs=[HBM_SPEC] * nb, out_specs=[HBM_SPEC] * nb + [pl.BlockSpec(memory_space=pltpu.VMEM)],
        input_output_aliases={i: i for i in range(nb)},
        scratch_shapes=[pltpu.SemaphoreType.DMA((n_sem,)), pltpu.SemaphoreType.DMA((n_sem,))],
    )(*[pltpu.with_memory_space_constraint(b, pltpu.HBM) for b in bufs])
    return list(outs[:nb]), outs[nb]


def _slot(land, s, rows, cols):
    if cols is None:
        return land.at[s, rows]
    return land.at[rows, pl.ds(pl.multiple_of(s * cols, 128), cols)]


def _plan_gather_ici(cols):
    nw = len(cols)

    def plan(refs):
        x, y, c = _pos()
        my_s = 2 * x + y
        out = []
        for w in range(nw):
            mine = _half_rows(refs[w], c)
            for px, py, ps in _chips():
                out.append((refs[w].at[mine], _slot(refs[nw + w], my_s, mine, cols[w]), (px, py, c),
                            _slot(refs[nw + w], ps, mine, cols[w])))
        return out
    return plan


def _plan_gather_fwd(cols, rows):
    def plan(refs):
        x, y, c = _pos()
        out = []
        for w in range(len(cols)):
            half = rows[w] // 2
            mine = pl.ds(pl.multiple_of(c * half, 8), half)
            other = pl.ds(pl.multiple_of((1 - c) * half, 8), half)
            for px, py, ps in _chips():
                got = _slot(refs[w], ps, mine, cols[w])
                out.append((got, got, (x, y, 1 - c), _slot(refs[w], ps, other, cols[w])))
        return out
    return plan


def _plan_swap(nw):
    def plan(refs):
        x, y, c = _pos()
        return [(refs[w].at[:, _half_rows(refs[w], 1 - c)], refs[nw + w], (x, y, 1 - c), refs[nw + w])
                for w in range(nw)]
    return plan


def _plan_swap_rows(nw):
    def plan(refs):
        x, y, c = _pos()
        return [(refs[w].at[_half_rows(refs[w], 1 - c)], refs[nw + w], (x, y, 1 - c), refs[nw + w])
                for w in range(nw)]
    return plan


def _plan_scatter(nw):
    def plan(refs):
        x, y, c = _pos()
        my_s = 2 * x + y
        out = []
        for w in range(nw):
            for px, py, ps in _chips():
                out.append((refs[w].at[ps], refs[nw + w].at[my_s], (px, py, c), refs[nw + w].at[ps]))
        return out
    return plan


def _plan_join(nw):
    def plan(refs):
        x, y, c = _pos()
        out = []
        for w in range(nw):
            land = refs[nw + w]
            out.append((refs[w], land.at[_half_rows(land, c)], (x, y, 1 - c), land.at[_half_rows(land, 1 - c)]))
        return out
    return plan


def _hbm_empty(shape, dtype):
    return pltpu.with_memory_space_constraint(lax.empty(shape, dtype), pltpu.HBM)


def _put_slot(land, own, slot):
    return lax.dynamic_update_slice(land, own[None], (slot,) + (0,) * own.ndim)


def _pad_lanes(a, n):
    return jnp.pad(a, ((0, 0), (0, n - a.shape[1])))


def kernel(x, c, positions, w_ada, b_ada, norm1_w, w_in, conv_w, conv_b, dt_bias, a_log, d_skip, attn_sinks, ssm_norm_w, w_out, norm2_w, w_gate_up, w_down, final_norm_w, loss_target, m_w_ada, m_b_ada, m_norm1_w, m_w_in, m_conv_w, m_conv_b, m_dt_bias, m_a_log, m_d_skip, m_attn_sinks, m_ssm_norm_w, m_w_out, m_norm2_w, m_w_gate_up, m_w_down, m_final_norm_w, v_w_ada, v_b_ada, v_norm1_w, v_w_in, v_conv_w, v_conv_b, v_dt_bias, v_a_log, v_d_skip, v_attn_sinks, v_ssm_norm_w, v_w_out, v_norm2_w, v_w_gate_up, v_w_down, v_final_norm_w):
    T = x.shape[1]
    tm = min(256, T)
    xi, yi, ci = lax.axis_index("x"), lax.axis_index("y"), lax.axis_index("c")
    my_s = 2 * xi + yi
    xs = x[0]
    tgt = loss_target[0]

    payload = jnp.concatenate([c, conv_w[0].reshape(1, CONVK * 256)], axis=1)
    gat, mod4, tok = _mod_exchange(payload, w_ada[0], b_ada.reshape(4, 1, 1536))
    mod6 = mod4.reshape(6, D)
    cw_dev = gat[:, 0, D:].reshape(4, 2, CONVK, 256)[:, 0]
    conv_full = cw_dev.transpose(1, 0, 2).reshape(CONVK, CONVC)

    w_in_b = w_in[0].astype(BF16)
    s_i, r_i, bufs, tok = _split_start("wgather_in_ici_start", [w_in_b, _hbm_empty((4,) + w_in_b.shape, BF16)], 3,
                                       _plan_gather_ici([None]), tok)
    inv_freq = (10000.0 ** (-jnp.arange(32, dtype=F32) / 32))
    cos, sin_s = _rope_tables(positions, inv_freq.reshape(32, 1), min(512, T), tok)
    late = [w_out[0].astype(BF16), w_gate_up[0].astype(BF16), w_down[0].astype(BF16)]
    bufs = _split_wait("wgather_in_ici_wait", s_i, r_i, bufs, [cos] + late, _plan_gather_ici([None]))
    own_in = bufs[0]
    bufs, tok = _copies_now("wgather_in_fwd", bufs[1:], 3, _plan_gather_fwd([None], [D]))
    g_in = _put_slot(bufs[0], own_in, my_s)
    w_pad = jnp.concatenate([g_in[0], g_in[1], g_in[2], g_in[3], jnp.zeros((D, IN_PAD - IN_PROJ), BF16)], axis=1)

    lands = [_hbm_empty((4, D // 4, D), BF16), _hbm_empty((D, 2 * DFF), BF16), _hbm_empty((4, DFF // 4, D), BF16)]
    cols3, rows3 = [None, GU_SH, None], [D // 4, D, DFF // 4]
    s_a, r_a, bufs, tok = _split_start("wgather_ici_start", late + lands, 9, _plan_gather_ici(cols3), tok)

    qkv, z, xbc, dtr, h1b = _in_proj_fwd(xs, cos, sin_s, mod6, norm1_w, w_pad, min(512, T), tok)
    sinks = attn_sinks
    attn, lse = _attn_fwd(qkv, sinks)
    bufs = _split_wait("wgather_ici_wait", s_a, r_a, bufs, attn, _plan_gather_ici(cols3))
    late = bufs[:3]
    s_b, r_b, lands, tok = _split_start("wgather_fwd_start", bufs[3:], 9, _plan_gather_fwd(cols3, rows3), attn)
    dtb = _pad_lanes(dt_bias, 128)
    alog = _pad_lanes(a_log, 128)
    dskx = jnp.repeat(d_skip, HD, axis=1)
    mats = _ssd_mats()
    ynorm, ypre, states, conv_pre = _ssd_fwd(xbc, z, dtr, conv_full, conv_b, dtb, alog, dskx, ssm_norm_w, mats, tok)
    lands = _split_wait("wgather_fwd_wait", s_b, r_b, lands, ynorm, _plan_gather_fwd(cols3, rows3))
    w_out_f = _put_slot(lands[0], late[0], my_s).reshape(D, D)
    w_dn_f = _put_slot(lands[2], late[2], my_s).reshape(DFF, D)
    s_arr = my_s.reshape(1).astype(jnp.int32)

    fw2 = final_norm_w.reshape(1, D)
    sq, dmix, dx1, h2b, act, dfb, dgu, dob, sm_ffn = _mix_ffn(
        xs, attn, ynorm, tgt, mod6, norm2_w, fw2, w_out_f, lands[1], late[1], s_arr, w_dn_f, tm)

    tt = min(2048, T)
    c_arr = ci.reshape(1).astype(jnp.int32)
    tok0 = jnp.zeros((8, 128), F32)
    gw_dn4 = _tn_matmul(act, dfb, GU_SH, D, tt, "dw_down", tok0).reshape(4, DFF // 4, D)
    gw_gu4 = _tn_matmul(h2b, dgu, D, GU_SH, tt, "dw_gate_up", tok0)
    gw_out4 = _tn_matmul_rows(attn, ynorm, dob, tt, "dw_out", tok0).reshape(4, D // 4, D)
    big1 = [gw_out4, gw_gu4, gw_dn4]
    rbs1 = [128, 512, 352]
    sib1 = [_hbm_empty((4, g.shape[1] // 2, g.shape[2]), F32) for g in big1]
    s_c, r_c, bufs, tok = _split_start("gswap_start", big1 + sib1, 3, _plan_swap(3), tok0)

    dzxd, d_cw, d_cb, d_sw, d_sk, d_dtb, d_av = _ssd_bwd(
        xbc, conv_pre, z, dtr, ypre, states, dmix, conv_full, dtb, alog, dskx, ssm_norm_w, mats, tok)
    bufs = _split_wait("gswap_wait", s_c, r_c, bufs, dzxd, _plan_swap(3))
    sums1 = [_add_half(g, s, c_arr, rb, "grad_add_%d" % i)
             for i, (g, s, rb) in enumerate(zip(bufs[:3], bufs[3:], rbs1))]
    land1 = [_hbm_empty(p.shape, BF16) for p in sums1]
    s_d, r_d, bufs, tok = _split_start("gscatter_start", sums1 + land1, 9, _plan_scatter(3), tok0)
    dqkv, d_sinks = _attn_bwd(qkv, sinks, lse, dmix, cos, sin_s, tok)
    bufs = _split_wait("gscatter_wait", s_d, r_d, bufs, dqkv, _plan_scatter(3))
    halves1 = [_sum4(p, l, s_arr, rb, "grad_sum_%d" % i)
               for i, (p, l, rb) in enumerate(zip(bufs[:3], bufs[3:], rbs1))]
    full1 = [_hbm_empty((2 * h.shape[0], h.shape[1]), F32) for h in halves1]
    s_e, r_e, bufs, tok = _split_start("gjoin_start", halves1 + full1, 3, _plan_join(3), tok0)
    gq, gz = _tn_matmul_cols(h1b, dqkv, dzxd, min(1024, T), "dw_in", tok)
    joined1 = _split_wait("gjoin_wait", s_e, r_e, bufs, [gq, gz], _plan_join(3))

    sibs = [_hbm_empty((D // 2, g.shape[1]), F32) for g in (gq, gz)]
    s_f, r_f, bufs, tok = _split_start("gswap_in_start", [gq, gz] + sibs, 2, _plan_swap_rows(2), tok0)
    g_dn_s, d_dn, m_dn, v_dn = _adam_2d(w_down[0], joined1[2], joined1[5], m_w_down[0], v_w_down[0], c_arr, 352,
                                        "adam_w_down", tok)
    g_gu_s, d_gu, m_gu, v_gu = _adam_2d(w_gate_up[0], joined1[1], joined1[4], m_w_gate_up[0], v_w_gate_up[0], c_arr,
                                        256, "adam_w_gate_up", tok)
    g_out_s, d_out, m_out, v_out = _adam_2d(w_out[0], joined1[0], joined1[3], m_w_out[0], v_w_out[0], c_arr, 128,
                                            "adam_w_out", tok)
    bufs = _split_wait("gswap_in_wait", s_f, r_f, bufs, [d_dn, d_gu, d_out], _plan_swap_rows(2))
    sum0 = _add_half_in(bufs[0], bufs[1], bufs[2], bufs[3], c_arr, min(256, D // 2))
    s_g, r_g, bufs, tok = _split_start("gscatter_in_start", [sum0, _hbm_empty(sum0.shape, BF16)], 3, _plan_scatter(1),
                                       tok0)
    grad_x, sm_in = _in_proj_bwd(xs, dx1, dqkv, dzxd, mod6, norm1_w, w_pad, min(512, T), tok)

    a_neg = -jnp.exp(alog)
    pieces = [sm_in[1:2], sm_in[2:3], sm_ffn[5:6], sm_ffn[2:3], sm_ffn[3:4], sm_ffn[4:5],
              sm_in[0:1], sm_ffn[1:2], sm_ffn[0:1], d_cb, d_cw.reshape(1, CONVK * CONVC),
              _pad_lanes(d_sw, SW), d_dtb, d_av * a_neg, d_sk, d_sinks,
              _pad_lanes((0.5 / D * jnp.sum(sq)).reshape(1, 1), 128)]
    vec = jnp.concatenate(pieces, axis=1)
    s_h, r_h, rows8, tok_small = _split_start("small_start", [vec, _hbm_empty((8,) + vec.shape, F32)], 7,
                                              _plan_small(), tok0)

    bufs = _split_wait("gscatter_in_wait", s_g, r_g, bufs, [grad_x, tok_small], _plan_scatter(1))
    half0 = _sum4(bufs[0], bufs[1], s_arr, 512, "grad_sum_in")
    joined0, _ = _copies_now("gjoin_in", [half0, _hbm_empty((D, IN_SH), F32)], 1, _plan_join(1))
    native = lambda a: a.transpose(2, 0, 1)
    adam_in = _adam_w_in(native(w_in), joined0[0], joined0[1], native(m_w_in), native(v_w_in), c_arr)
    g_in_s, d_in, m_in, v_in = [a.transpose(1, 2, 0) for a in adam_in]
    rows8 = _split_wait("small_wait", s_h, r_h, rows8, [adam_in[1]], _plan_small())
    tot, allv = _small_sum(rows8[0], rows8[1], (4 * xi + 2 * yi + ci).reshape(1).astype(jnp.int32))
    o = 0
    offs = []
    for p in pieces:
        offs.append(o)
        o += p.shape[1]
    seg = lambda i, n: (offs[i], n)
    g_conv_w = lax.dynamic_slice_in_dim(
        tot[:, offs[10]:offs[10] + CONVK * CONVC].reshape(CONVK, CONVC), my_s * 256, 256, axis=1)
    loss = tot[0, offs[16]]

    small_names = ["b_ada", "norm1_w", "conv_w", "conv_b", "dt_bias", "a_log", "d_skip", "attn_sinks", "ssm_norm_w",
                   "norm2_w", "final_norm_w"]
    small_g = [(0, 6 * D), seg(6, D), g_conv_w, seg(9, D), seg(12, 8), seg(13, 8), seg(14, 8), seg(15, 8),
               seg(11, SW), seg(7, D), seg(8, D)]
    as2d = lambda a: a.reshape(-1, a.shape[-1])
    small_w = [as2d(a) for a in (b_ada, norm1_w, conv_w, conv_b, dt_bias, a_log, d_skip, attn_sinks, ssm_norm_w,
                                 norm2_w, final_norm_w)]
    small_m = [as2d(a) for a in (m_b_ada, m_norm1_w, m_conv_w, m_conv_b, m_dt_bias, m_a_log, m_d_skip, m_attn_sinks,
                                 m_ssm_norm_w, m_norm2_w, m_final_norm_w)]
    small_v = [as2d(a) for a in (v_b_ada, v_norm1_w, v_conv_w, v_conv_b, v_dt_bias, v_a_log, v_d_skip, v_attn_sinks,
                                 v_ssm_norm_w, v_norm2_w, v_final_norm_w)]
    small_g, sd, smn, svn = _adam_small(tot, small_g, small_w, small_m, small_v)
    g_ada, d_ada, m_ada, v_ada = _adam_w_ada(gat, allv, s_arr, w_ada[0], m_w_ada[0], v_w_ada[0], 256)

    order = ["w_ada", "b_ada", "norm1_w", "w_in", "conv_w", "conv_b", "dt_bias", "a_log", "d_skip", "attn_sinks",
             "ssm_norm_w", "w_out", "norm2_w", "w_gate_up", "w_down", "final_norm_w"]
    shapes = dict(w_ada=w_ada.shape, b_ada=b_ada.shape, norm1_w=norm1_w.shape, w_in=w_in.shape, conv_w=conv_w.shape,
                  conv_b=conv_b.shape, dt_bias=dt_bias.shape, a_log=a_log.shape, d_skip=d_skip.shape,
                  attn_sinks=attn_sinks.shape, ssm_norm_w=ssm_norm_w.shape, w_out=w_out.shape, norm2_w=norm2_w.shape,
                  w_gate_up=w_gate_up.shape, w_down=w_down.shape, final_norm_w=final_norm_w.shape)
    grads = dict(w_ada=g_ada, w_in=g_in_s, w_out=g_out_s, w_gate_up=g_gu_s, w_down=g_dn_s)
    deltas = dict(w_ada=d_ada, w_in=d_in, w_out=d_out, w_gate_up=d_gu, w_down=d_dn)
    new_m = dict(w_ada=m_ada, w_in=m_in, w_out=m_out, w_gate_up=m_gu, w_down=m_dn)
    new_v = dict(w_ada=v_ada, w_in=v_in, w_out=v_out, w_gate_up=v_gu, w_down=v_dn)
    for i, nme in enumerate(small_names):
        grads[nme], deltas[nme], new_m[nme], new_v[nme] = small_g[i], sd[i], smn[i], svn[i]
    outs = [loss, grad_x[None]]
    for table in (grads, deltas, new_m, new_v):
        outs += [table[nme].reshape(shapes[nme]) for nme in order]
    return tuple(outs)
```

```python
import functools
import math

import jax
import jax.numpy as jnp
from jax import lax
from jax.experimental import pallas as pl
from jax.experimental.pallas import tpu as pltpu

F32 = jnp.float32
BF16 = jnp.bfloat16
HI = lax.Precision.HIGHEST
MESH = pl.DeviceIdType.MESH

D = 1024
HD = 64
AW = 512
SW = 512
NST = 128
CONVK = 4
CONVC = 1024
BLK = 128
CPS = 4
SSD_FWD_CPS = 8
ATTN_BPS = 8
IN_PROJ = 2312
IN_PAD = 2432
IN_SH = IN_PROJ // 4
DFF = 2816
GU_SH = 1408
FF_SPLITS = ((0, 1536), (1536, 2816))
EPS = 1e-6
NEG = -1e30
LR, B1, B2, AEPS, WD, STEP = 0.001, 0.9, 0.999, 1e-08, 0.01, 10
VMEM_LIMIT = 58 * 1024 * 1024


def _cp(*sem):
    return pltpu.CompilerParams(dimension_semantics=sem or None, vmem_limit_bytes=VMEM_LIMIT)


def _dot(a, b):
    return jnp.dot(a, b, preferred_element_type=F32)


def _dot_nt(a, b):
    return lax.dot_general(a, b, (((1,), (1,)), ((), ())), preferred_element_type=F32)


def _dot_tn(a, b):
    return lax.dot_general(a, b, (((0,), (0,)), ((), ())), preferred_element_type=F32)


def _dot_hi(a, b):
    return jnp.dot(a, b, precision=HI, preferred_element_type=F32)


def _sigmoid(x):
    return 1.0 / (1.0 + jnp.exp(-x))


def _iota(shape, dim):
    return lax.broadcasted_iota(jnp.int32, shape, dim)


def _load_resident(hbm_ref, vmem_ref, sem):
    @pl.when(pl.program_id(0) == 0)
    def _():
        cp = pltpu.make_async_copy(hbm_ref, vmem_ref, sem)
        cp.start()
        cp.wait()


def _swap32(t):
    lane = _iota(t.shape, 1)
    return jnp.where((lane & 63) < 32, pltpu.roll(t, 96, 1), pltpu.roll(t, 32, 1))


def _rope_fwd(t, cos, sin_s):
    return t * cos + _swap32(t) * sin_s


def _rope_bwd(t, cos, sin_s):
    return t * cos - _swap32(t) * sin_s


DEP_SPEC = pl.BlockSpec((8, 128), lambda *_: (0, 0))


def _rope_tables(pos_row, inv_freq_col, tm, dep):
    T = pos_row.shape[1]
    lane, row = jnp.arange(128)[None, :], jnp.arange(96)[:, None]
    pick = (lane % 32) == (row % 32)
    sel_cos = pick.astype(BF16)
    sel_sin = jnp.where(pick, jnp.where(lane % 64 < 32, -1.0, 1.0), 0.0).astype(BF16)

    def body(p_ref, f_ref, sc_ref, ss_ref, dep_ref, cos_ref, sin_ref):
        ang = f_ref[...] * p_ref[...].astype(F32)
        cos_ref[...] = _dot_tn(_pieces(jnp.cos(ang), 3, 0), sc_ref[...])
        sin_ref[...] = _dot_tn(_pieces(jnp.sin(ang), 3, 0), ss_ref[...])

    full = lambda a: pl.BlockSpec(a.shape, lambda i: (0,) * a.ndim)
    return pl.pallas_call(
        body, name="rope_tables", grid=(T // tm,),
        in_specs=[pl.BlockSpec((1, tm), lambda i: (0, i)), full(inv_freq_col), full(sel_cos), full(sel_sin), DEP_SPEC],
        out_specs=[pl.BlockSpec((tm, 128), lambda i: (i, 0))] * 2,
        out_shape=[jax.ShapeDtypeStruct((T, 128), F32)] * 2,
        compiler_params=_cp("parallel"),
    )(pos_row, inv_freq_col, sel_cos, sel_sin, dep)


def _in_proj_fwd(x, cos, sin_s, mod6, norm1_w, w_pad, tm, dep):
    T = x.shape[0]

    def body(x_ref, cos_ref, sin_ref, mod_ref, nw_ref, w_hbm, dep_ref, qkv_ref, z_ref, xbc_ref, dt_ref, h_ref, w_vmem,
             sem):
        _load_resident(w_hbm, w_vmem, sem)
        xv = x_ref[...]
        r = lax.rsqrt(jnp.mean(xv * xv, axis=-1, keepdims=True) + EPS)
        h = (xv * r * nw_ref[...]) * (1.0 + mod_ref[1:2, :]) + mod_ref[0:1, :]
        hb = h.astype(BF16)
        h_ref[...] = hb
        proj = _dot(hb, w_vmem[...])
        cs, sn = cos_ref[...], sin_ref[...]
        for j in range(5):
            qkv_ref[:, 128 * j:128 * (j + 1)] = _rope_fwd(proj[:, 128 * j:128 * (j + 1)], cs, sn).astype(BF16)
        qkv_ref[:, 640:768] = proj[:, 640:768].astype(BF16)
        z_ref[...] = proj[:, 768:1280]
        xbc_ref[...] = proj[:, 1280:2304]
        dt_ref[...] = proj[:, 2304:2432]

    row = lambda w: pl.BlockSpec((tm, w), lambda i: (i, 0))
    full = lambda a: pl.BlockSpec(a.shape, lambda i: (0,) * a.ndim)
    return pl.pallas_call(
        body, name="in_proj_fwd", grid=(T // tm,),
        in_specs=[row(D), row(128), row(128), full(mod6), full(norm1_w), pl.BlockSpec(memory_space=pl.ANY), DEP_SPEC],
        out_specs=[row(768), row(512), row(1024), row(128), row(D)],
        out_shape=[jax.ShapeDtypeStruct((T, 768), BF16), jax.ShapeDtypeStruct((T, 512), F32),
                   jax.ShapeDtypeStruct((T, 1024), F32), jax.ShapeDtypeStruct((T, 128), F32),
                   jax.ShapeDtypeStruct((T, D), BF16)],
        scratch_shapes=[pltpu.VMEM((D, IN_PAD), BF16), pltpu.SemaphoreType.DMA],
        compiler_params=_cp("arbitrary"),
    )(x, cos, sin_s, mod6, norm1_w, w_pad, dep)


def _head_variants(pair, j):
    lane = _iota(pair.shape, 1)
    lo = lane < 64
    kv = j // 2
    ev = jnp.where(lo, pair, 0.0)
    od = jnp.where(lo, 0.0, pair)
    if kv == 0:
        od = pltpu.roll(od, 64, 1)
    else:
        ev = pltpu.roll(ev, 64, 1)
    return ev.astype(BF16), od.astype(BF16)


def _kv_variants(vcat):
    lane = _iota(vcat.shape, 1)
    lo = lane < 64
    v0 = jnp.where(lo, vcat, 0.0)
    v1 = jnp.where(lo, 0.0, vcat)
    out = {
        (0, 0): v0, (0, 1): pltpu.roll(v0, 64, 1),
        (1, 0): pltpu.roll(v1, 64, 1), (1, 1): v1,
    }
    return {k: v.astype(BF16) for k, v in out.items()}


def _fold_masks(n):
    upper = _iota((BLK, BLK), 1) > _iota((BLK, BLK), 0)
    return upper, upper & (n == 0)


def _attn_fwd(qkv, sinks):
    CPS = ATTN_BPS
    T = qkv.shape[0]
    nsteps = T // (CPS * BLK)

    def body(sink_ref, q_ref, kc_ref, kp_ref, vc_ref, vp_ref, o_ref, lse_ref):
        for sub in range(CPS):
            rows, before = slice(BLK * sub, BLK * (sub + 1)), slice(BLK * (sub - 1), BLK * sub)
            block(pl.program_id(0) * CPS + sub, sink_ref, q_ref.at[rows, :], kc_ref.at[rows, :],
                  kp_ref if sub == 0 else kc_ref.at[before, :], vc_ref.at[rows, :],
                  vp_ref if sub == 0 else vc_ref.at[before, :], o_ref.at[rows, :], lse_ref.at[rows, :])

    def block(n, sink_ref, q_ref, kc_ref, kp_ref, vc_ref, vp_ref, o_ref, lse_ref):
        vpv = _kv_variants(vp_ref[...].astype(F32))
        vcv = _kv_variants(vc_ref[...].astype(F32))
        q_all = jnp.concatenate(
            [v for j in range(4) for v in _head_variants(q_ref[:, 128 * j:128 * (j + 1)].astype(F32), j)], axis=0)
        s_prev = _dot_nt(q_all, kp_ref[...])
        s_cur = _dot_nt(q_all, kc_ref[...])
        upper, dead = _fold_masks(n)
        lane = _iota((BLK, 128), 1)
        lse_acc = jnp.zeros((BLK, 128), F32)
        for jj in range(4):
            acc = jnp.zeros((BLK, 128), F32)
            for par in range(2):
                h = 2 * jj + par
                rows = slice(h * BLK, (h + 1) * BLK)
                sink = sink_ref[0, h]
                s = jnp.where(dead, NEG, jnp.where(upper, s_prev[rows], s_cur[rows]) * 0.125)
                m = jnp.maximum(jnp.max(s, axis=1, keepdims=True), sink)
                p = jnp.exp(s - m)
                den = jnp.sum(p, axis=1, keepdims=True) + jnp.exp(sink - m)
                pn = p * (1.0 / den)
                acc = (acc + _dot(jnp.where(upper, pn, 0.0).astype(BF16), vpv[(jj // 2, par)])
                       + _dot(jnp.where(upper, 0.0, pn).astype(BF16), vcv[(jj // 2, par)]))
                lse_acc = jnp.where(lane == h, m + jnp.log(den), lse_acc)
            o_ref[:, 128 * jj:128 * (jj + 1)] = acc.astype(BF16)
        lse_ref[...] = lse_acc

    RB = CPS * BLK
    prev = lambda n: jnp.maximum(n * CPS - 1, 0)
    return pl.pallas_call(
        body, name="attn_fwd", grid=(nsteps,),
        in_specs=[pl.BlockSpec(memory_space=pltpu.SMEM),
                  pl.BlockSpec((RB, 512), lambda n: (n, 0)),
                  pl.BlockSpec((RB, 128), lambda n: (n, 4)),
                  pl.BlockSpec((BLK, 128), lambda n: (prev(n), 4)),
                  pl.BlockSpec((RB, 128), lambda n: (n, 5)),
                  pl.BlockSpec((BLK, 128), lambda n: (prev(n), 5))],
        out_specs=[pl.BlockSpec((RB, 512), lambda n: (n, 0)), pl.BlockSpec((RB, 128), lambda n: (n, 0))],
        out_shape=[jax.ShapeDtypeStruct((T, 512), BF16), jax.ShapeDtypeStruct((T, 128), F32)],
        compiler_params=_cp("parallel"),
    )(sinks, qkv, qkv, qkv, qkv, qkv)


def _attn_bwd(qkv, sinks, lse, dmix, cos, sin_s, dep):
    T = qkv.shape[0]
    nb = T // BLK

    def body(sink_ref, q_ref, kc_ref, kp_ref, vc_ref, vp_ref, lse_ref, do_ref, cq_ref, sq_ref, ck_ref, sk_ref,
             dep_ref, out_ref, ds_ref, dq_car, dk_car, dv_car):
        n = pl.program_id(0)
        lane = _iota((BLK, 128), 1)

        @pl.when(n == 0)
        def _():
            ds_ref[...] = jnp.zeros_like(ds_ref)
            dq_car[...] = jnp.zeros_like(dq_car)
            dk_car[...] = jnp.zeros_like(dk_car)
            dv_car[...] = jnp.zeros_like(dv_car)

        @pl.when(n < nb)
        def _():
            kp, kc, vp, vc = kp_ref[...], kc_ref[...], vp_ref[...], vc_ref[...]
            kpv = _kv_variants(kp.astype(F32))
            kcv = _kv_variants(kc.astype(F32))
            lse_v = lse_ref[...]
            q_all = jnp.concatenate(
                [v for j in range(4) for v in _head_variants(q_ref[:, 128 * j:128 * (j + 1)].astype(F32), j)], axis=0)
            do_all = jnp.concatenate(
                [v for j in range(4) for v in _head_variants(do_ref[:, 128 * j:128 * (j + 1)], j)], axis=0)
            s_prev, s_cur = _dot_nt(q_all, kp), _dot_nt(q_all, kc)
            dp_prev, dp_cur = _dot_nt(do_all, vp), _dot_nt(do_all, vc)
            upper, dead = _fold_masks(n)
            out_ref[:, 0:512] = dq_car[...]
            dsk = jnp.zeros((1, 128), F32)
            ds_u, ds_l, p_u, p_l = [], [], [], []
            for jj in range(4):
                dq_acc = jnp.zeros((BLK, 128), F32)
                for par in range(2):
                    h = 2 * jj + par
                    rows = slice(h * BLK, (h + 1) * BLK)
                    lse_h = jnp.sum(jnp.where(lane == h, lse_v, 0.0), axis=1, keepdims=True)
                    s = jnp.where(dead, NEG, jnp.where(upper, s_prev[rows], s_cur[rows]) * 0.125)
                    p = jnp.exp(s - lse_h)
                    dp = jnp.where(upper, dp_prev[rows], dp_cur[rows])
                    delta = jnp.sum(p * dp, axis=1, keepdims=True)
                    ds = p * (dp - delta) * 0.125
                    dsu, dsl = jnp.where(upper, ds, 0.0).astype(BF16), jnp.where(upper, 0.0, ds).astype(BF16)
                    dq_acc = dq_acc + _dot(dsu, kpv[(jj // 2, par)]) + _dot(dsl, kcv[(jj // 2, par)])
                    ds_u.append(dsu)
                    ds_l.append(dsl)
                    p_u.append(jnp.where(upper, p, 0.0).astype(BF16))
                    p_l.append(jnp.where(upper, 0.0, p).astype(BF16))
                    dsk = dsk + jnp.where(lane[0:1] == h, -jnp.sum(jnp.exp(sink_ref[0, h] - lse_h) * delta), 0.0)
                dq_car[:, 128 * jj:128 * (jj + 1)] = _rope_bwd(dq_acc, cq_ref[...], sq_ref[...]).astype(BF16)
            stack = lambda parts: jnp.concatenate(parts, axis=0)
            dk_prev, dk_cur = _dot_tn(stack(ds_u), q_all), _dot_tn(stack(ds_l), q_all)
            dv_prev, dv_cur = _dot_tn(stack(p_u), do_all), _dot_tn(stack(p_l), do_all)
            ds_ref[...] += dsk
            out_ref[:, 512:640] = _rope_bwd(dk_car[...] + dk_prev, ck_ref[...], sk_ref[...]).astype(BF16)
            out_ref[:, 640:768] = (dv_car[...] + dv_prev).astype(BF16)
            dk_car[...] = dk_cur
            dv_car[...] = dv_cur

        @pl.when(n == nb)
        def _():
            out_ref[:, 0:512] = dq_car[...]
            out_ref[:, 512:640] = _rope_bwd(dk_car[...], ck_ref[...], sk_ref[...]).astype(BF16)
            out_ref[:, 640:768] = dv_car[...].astype(BF16)

    cur = lambda n: jnp.minimum(n, nb - 1)
    prev = lambda n: jnp.maximum(cur(n) - 1, 0)
    outb = lambda n: jnp.maximum(n - 1, 0)
    return pl.pallas_call(
        body, name="attn_bwd", grid=(nb + 1,),
        in_specs=[pl.BlockSpec(memory_space=pltpu.SMEM),
                  pl.BlockSpec((BLK, 512), lambda n: (cur(n), 0)),
                  pl.BlockSpec((BLK, 128), lambda n: (cur(n), 4)),
                  pl.BlockSpec((BLK, 128), lambda n: (prev(n), 4)),
                  pl.BlockSpec((BLK, 128), lambda n: (cur(n), 5)),
                  pl.BlockSpec((BLK, 128), lambda n: (prev(n), 5)),
                  pl.BlockSpec((BLK, 128), lambda n: (cur(n), 0)),
                  pl.BlockSpec((BLK, 512), lambda n: (cur(n), 0)),
                  pl.BlockSpec((BLK, 128), lambda n: (cur(n), 0)),
                  pl.BlockSpec((BLK, 128), lambda n: (cur(n), 0)),
                  pl.BlockSpec((BLK, 128), lambda n: (outb(n), 0)),
                  pl.BlockSpec((BLK, 128), lambda n: (outb(n), 0)), DEP_SPEC],
        out_specs=[pl.BlockSpec((BLK, 768), lambda n: (outb(n), 0)), pl.BlockSpec((1, 128), lambda n: (0, 0))],
        out_shape=[jax.ShapeDtypeStruct((T, 768), BF16), jax.ShapeDtypeStruct((1, 128), F32)],
        scratch_shapes=[pltpu.VMEM((BLK, 512), BF16), pltpu.VMEM((BLK, 128), F32), pltpu.VMEM((BLK, 128), F32)],
        compiler_params=_cp("arbitrary"),
    )(sinks, qkv, qkv, qkv, qkv, qkv, lse, dmix, cos, sin_s, cos, sin_s, dep)


def _ssd_mats():
    e = jnp.arange(SW)[None, :] // HD == jnp.arange(128)[:, None]
    tri = jnp.arange(BLK)[None, :] <= jnp.arange(BLK)[:, None]
    return (jnp.tile(e, (3, 1)).astype(BF16), jnp.tile(e.T, (2, 1)).astype(BF16),
            jnp.tile(tri, (1, 3)).astype(BF16), jnp.tile(tri.T, (1, 3)).astype(BF16))


def _pieces(x, n, axis):
    out, r = [], x
    for i in range(n):
        p = r.astype(BF16)
        out.append(p)
        if i + 1 < n:
            r = r - p.astype(F32)
    return jnp.concatenate(out, axis=axis)


def _expand(x, e3):
    return _dot(_pieces(x, 3, 1), e3)


def _head_sums(x, et2):
    return _dot(_pieces(x, 2, 1), et2)


def _run_sum(tri3, x):
    return _dot(tri3, _pieces(x, 3, 0))


def _shift_down(u, tail, j):
    rolled = pltpu.roll(u, j, 0)
    first = jnp.where(_iota(tail.shape, 0) < j, pltpu.roll(tail, j, 0), rolled[0:8])
    return jnp.concatenate([first, rolled[8:]], axis=0)


def _shift_up(d, head, j):
    rolled = pltpu.roll(d, BLK - j, 0)
    last = jnp.where(_iota(head.shape, 0) >= 8 - j, pltpu.roll(head, 8 - j, 0), rolled[BLK - 8:])
    return jnp.concatenate([rolled[:BLK - 8], last], axis=0)


def _ssd_parts(dtr, dtb, alog, e3, tril3):
    xx = dtr + dtb
    dt = jnp.maximum(xx, 0.0) + jnp.log(1.0 + jnp.exp(-jnp.abs(xx)))
    a_neg = -jnp.exp(alog)
    tril = _iota((BLK, BLK), 1) <= _iota((BLK, BLK), 0)
    cs = _run_sum(tril3, dt * a_neg)
    csx = _expand(cs, e3)
    last = csx[BLK - 1:BLK, :]
    return dict(xx=xx, dt=dt, a_neg=a_neg, tril=tril, cs=cs, cs_t=cs.T,
                ecsx=jnp.exp(csx), dtex=jnp.exp(last - csx), cdx=jnp.exp(last), dtx=_expand(dt, e3))


def _decay(parts, h):
    seg = parts["cs"][:, h:h + 1] - parts["cs_t"][h:h + 1, :]
    return jnp.exp(jnp.where(parts["tril"], seg, NEG))


def _group_cols(a, g):
    return a[:, 256 * g:256 * (g + 1)]


def _ssd_fwd(xbc, z, dtr, conv_w, conv_b, dtb, alog, dskx, ssm_w, mats, dep):
    CPS = SSD_FWD_CPS
    T = xbc.shape[0]
    nc = T // BLK

    def body(u_ref, tail_ref, z_ref, dtr_ref, cw_ref, cb_ref, dtb_ref, al_ref, dk_ref, sw_ref, e3_ref, tril3_ref,
             dep_ref, yn_ref, yp_ref, st_ref, co_ref, s_scr):
        n = pl.program_id(0)

        @pl.when(n == 0)
        def _():
            s_scr[...] = jnp.zeros_like(s_scr)

        lane = _iota((BLK, 128), 1)
        lo = lane < 64
        for sub in range(CPS):
            rows = slice(BLK * sub, BLK * (sub + 1))
            u = u_ref[rows, :]
            tail = jnp.where(n > 0, tail_ref[...], 0.0) if sub == 0 else u_ref[BLK * sub - 8:BLK * sub, :]
            co = cb_ref[...] + cw_ref[3:4, :] * u
            for j in range(1, CONVK):
                co = co + cw_ref[3 - j:4 - j, :] * _shift_down(u, tail, j)
            co_ref[rows, :] = co
            xc = co * _sigmoid(co)
            pt = _ssd_parts(dtr_ref[rows, :], dtb_ref[...], al_ref[...], e3_ref[...], tril3_ref[...])
            xs = xc[:, :SW]
            bm = [xc[:, 512:640].astype(BF16), xc[:, 640:768].astype(BF16)]
            cm = [xc[:, 768:896].astype(BF16), xc[:, 896:1024].astype(BF16)]
            s_in = s_scr[...]
            st_ref[sub] = s_in
            xdt = xs * pt["dtx"]
            xde = (xdt * pt["dtex"]).astype(BF16)
            ys, s_new = [], []
            for g in range(2):
                cb = _dot_nt(cm[g], bm[g])
                yoff = _dot(cm[g], _group_cols(s_in, g).astype(BF16))
                s_new.append(_dot_tn(bm[g], _group_cols(xde, g)))
                for jj in range(2):
                    j = 2 * g + jj
                    chunk = xdt[:, 128 * j:128 * (j + 1)]
                    g_ev = (cb * _decay(pt, 2 * j)).astype(BF16)
                    g_od = (cb * _decay(pt, 2 * j + 1)).astype(BF16)
                    yd = (_dot(g_ev, jnp.where(lo, chunk, 0.0).astype(BF16))
                          + _dot(g_od, jnp.where(lo, 0.0, chunk).astype(BF16)))
                    ys.append(yd + yoff[:, 128 * jj:128 * (jj + 1)] * pt["ecsx"][:, 128 * j:128 * (j + 1)])
            y = jnp.concatenate(ys, axis=1) + xs * dk_ref[...]
            s_scr[...] = s_in * pt["cdx"] + jnp.concatenate(s_new, axis=1)
            yp_ref[rows, :] = y
            zv = z_ref[rows, :]
            yz = y * (zv * _sigmoid(zv))
            outs = []
            for g in range(2):
                yg = _group_cols(yz, g)
                outs.append(yg * lax.rsqrt(jnp.mean(yg * yg, axis=-1, keepdims=True) + EPS))
            yn_ref[rows, :] = (jnp.concatenate(outs, axis=1) * sw_ref[...]).astype(BF16)

    e3, _, tril3, _ = mats
    RB = CPS * BLK
    tail8 = lambda n: jnp.maximum(n * (RB // 8) - 1, 0)
    full = lambda a: pl.BlockSpec(a.shape, lambda n: (0,) * a.ndim)
    return pl.pallas_call(
        body, name="ssd_fwd", grid=(nc // CPS,),
        in_specs=[pl.BlockSpec((RB, CONVC), lambda n: (n, 0)), pl.BlockSpec((8, CONVC), lambda n: (tail8(n), 0)),
                  pl.BlockSpec((RB, SW), lambda n: (n, 0)), pl.BlockSpec((RB, 128), lambda n: (n, 0)),
                  full(conv_w), full(conv_b), full(dtb), full(alog), full(dskx), full(ssm_w), full(e3), full(tril3),
                  DEP_SPEC],
        out_specs=[pl.BlockSpec((RB, SW), lambda n: (n, 0)), pl.BlockSpec((RB, SW), lambda n: (n, 0)),
                   pl.BlockSpec((CPS, NST, SW), lambda n: (n, 0, 0)), pl.BlockSpec((RB, CONVC), lambda n: (n, 0))],
        out_shape=[jax.ShapeDtypeStruct((T, SW), BF16), jax.ShapeDtypeStruct((T, SW), F32),
                   jax.ShapeDtypeStruct((nc, NST, SW), F32), jax.ShapeDtypeStruct((T, CONVC), F32)],
        scratch_shapes=[pltpu.VMEM((NST, SW), F32)],
        compiler_params=_cp("arbitrary"),
    )(xbc, xbc, z, dtr, conv_w, conv_b, dtb, alog, dskx, ssm_w, e3, tril3, dep)


def _ssd_bwd(xbc, co_all, z, dtr, ypre, states, dmix, conv_w, dtb, alog, dskx, ssm_w, mats, dep):
    T = xbc.shape[0]
    nsteps = T // (CPS * BLK)

    def body(*refs):
        per_chunk, consts, out_ref, carried = refs[:7], refs[7:16], refs[17], refs[18:]
        i = pl.program_id(0)

        @pl.when(i == 0)
        def _():
            for r in carried:
                r[...] = jnp.zeros_like(r)

        for sub in reversed(range(CPS)):
            rows = slice(BLK * sub, BLK * (sub + 1))
            views = [r.at[sub:sub + 1] if k == 5 else r.at[rows, :] for k, r in enumerate(per_chunk)]
            chunk(*views, *consts, out_ref.at[rows, :], *carried)

        @pl.when(i == nsteps - 1)
        def _():
            dsk_ref, dskx_scr = carried[3], carried[8]
            dsk_ref[...] = _head_sums(jnp.broadcast_to(dskx_scr[...], (8, SW)), consts[6][...])[0:1]

    def chunk(u_ref, co_ref, z_ref, dtr_ref, yp_ref, st_ref, dyn_ref, cw_ref, dtb_ref, al_ref, dk_ref, sw_ref,
              e3_ref, et2_ref, tril3_ref, triu3_ref,
              out_ref, dcw_ref, dcb_ref, dsw_ref, dsk_ref, ddtb_ref, dav_ref, ds_scr, dco_scr, dskx_scr):
        co = co_ref[...]
        sg = _sigmoid(co)
        xc = co * sg
        pt = _ssd_parts(dtr_ref[...], dtb_ref[...], al_ref[...], e3_ref[...], tril3_ref[...])
        dtx, ecsx, dtex, cdx = pt["dtx"], pt["ecsx"], pt["dtex"], pt["cdx"]
        xs = xc[:, :SW]
        bm = [xc[:, 512:640].astype(BF16), xc[:, 640:768].astype(BF16)]
        cm = [xc[:, 768:896].astype(BF16), xc[:, 896:1024].astype(BF16)]
        s_in = st_ref[0]
        ds_out = ds_scr[...]
        e_t = et2_ref[...]

        zv = z_ref[...]
        sz = _sigmoid(zv)
        silu_z = zv * sz
        ypre = yp_ref[...]
        yz = ypre * silu_z
        dyn = dyn_ref[...]
        sw = sw_ref[...]
        dyz, yns = [], []
        for g in range(2):
            yg = _group_cols(yz, g)
            r = lax.rsqrt(jnp.mean(yg * yg, axis=-1, keepdims=True) + EPS)
            yn = yg * r
            dg = _group_cols(dyn, g) * _group_cols(sw, g)
            dyz.append(r * (dg - yn * jnp.mean(dg * yn, axis=-1, keepdims=True)))
            yns.append(yn)
        dyz = jnp.concatenate(dyz, axis=1)
        dsw_ref[...] += jnp.sum(dyn * jnp.concatenate(yns, axis=1), axis=0, keepdims=True)
        dy = dyz * silu_z
        dz = dyz * ypre * (sz * (1.0 + zv * (1.0 - sz)))

        xdt = xs * dtx
        xdt_b = xdt.astype(BF16)
        edy = (ecsx * dy).astype(BF16)
        xde = (xdt * dtex).astype(BF16)
        lane = _iota((BLK, 128), 1)
        lo = lane < 64
        row8 = _iota((8, 128), 0)
        dcs = jnp.zeros((BLK, 128), F32)
        col_rows = jnp.zeros((8, 128), F32)
        dxdt, bds, yoff, dbs, dcs_g, ds_new = [], [], [], [], [], []
        for g in range(2):
            s_g = _group_cols(s_in, g).astype(BF16)
            dso_g = _group_cols(ds_out, g).astype(BF16)
            cb = _dot_nt(cm[g], bm[g])
            bds.append(_dot(bm[g], dso_g))
            yoff.append(_dot(cm[g], s_g))
            dcb_g = jnp.zeros((BLK, BLK), F32)
            for jj in range(2):
                j = 2 * g + jj
                dy_c = dy[:, 128 * j:128 * (j + 1)]
                xdt_c = xdt_b[:, 128 * j:128 * (j + 1)]
                acc = jnp.zeros((BLK, 128), F32)
                for par in range(2):
                    h = 2 * j + par
                    lm = _decay(pt, h)
                    gm = cb * lm
                    dy_m = (jnp.where(lo, dy_c, 0.0) if par == 0 else jnp.where(lo, 0.0, dy_c)).astype(BF16)
                    dg_h = _dot_nt(dy_m, xdt_c)
                    w_h = dg_h * gm
                    dcs = dcs + jnp.where(lane == h, jnp.sum(w_h, axis=1, keepdims=True), 0.0)
                    col_rows = col_rows + jnp.where(row8 == h, jnp.sum(w_h, axis=0, keepdims=True), 0.0)
                    dcb_g = dcb_g + dg_h * lm
                    acc = acc + _dot_tn(gm.astype(BF16), dy_m)
                dxdt.append(acc)
            dcb_b = dcb_g.astype(BF16)
            dcs_g.append(_dot(dcb_b, bm[g]) + _dot_nt(_group_cols(edy, g), s_g))
            dbs.append(_dot_tn(dcb_b, cm[g]) + _dot_nt(_group_cols(xde, g), dso_g))
            ds_new.append(_dot_tn(cm[g], _group_cols(edy, g)))
        bds = jnp.concatenate(bds, axis=1)
        yoff = jnp.concatenate(yoff, axis=1) * ecsx
        dxdt = jnp.concatenate(dxdt, axis=1) + dtex * bds
        ds_scr[...] = cdx * ds_out + jnp.concatenate(ds_new, axis=1)

        t_m = _head_sums(dtex * xdt * bds, e_t)
        colsum_t = jnp.concatenate([col_rows, jnp.zeros((BLK - 8, 128), F32)], axis=0).T
        cd = jnp.exp(pt["cs"][BLK - 1:BLK, :])
        sds = jnp.sum(s_in * ds_out, axis=0, keepdims=True)
        last_row = jnp.sum(t_m, axis=0, keepdims=True) + cd * _head_sums(jnp.broadcast_to(sds, (8, SW)), e_t)[0:1]
        dcs = dcs - colsum_t + _head_sums(dy * yoff, e_t) - t_m
        dcs = dcs + jnp.where(_iota((BLK, 128), 0) == BLK - 1, last_row, 0.0)
        da = _run_sum(triu3_ref[...], dcs)
        dt = pt["dt"]
        ddt = da * pt["a_neg"] + _head_sums(dxdt * xs, e_t)
        dav_ref[...] += jnp.sum(da * dt, axis=0, keepdims=True)
        ddtr = ddt * _sigmoid(pt["xx"])
        ddtb_ref[...] += jnp.sum(ddtr, axis=0, keepdims=True)
        dxs = dxdt * dtx + dy * dk_ref[...]
        dskx_scr[...] += jnp.sum(dy * xs, axis=0, keepdims=True)
        dxc = jnp.concatenate([dxs, dbs[0], dbs[1], dcs_g[0], dcs_g[1]], axis=1)
        dco = dxc * (sg * (1.0 + co * (1.0 - sg)))

        dcb_ref[...] += jnp.sum(dco, axis=0, keepdims=True)
        u = u_ref[...]
        head = dco_scr[...]
        du = jnp.zeros_like(dco)
        for j in range(CONVK):
            up_j = dco if j == 0 else _shift_up(dco, head, j)
            dcw_ref[3 - j:4 - j, :] += jnp.sum(up_j * u, axis=0, keepdims=True)
            du = du + cw_ref[3 - j:4 - j, :] * up_j
        dco_scr[...] = dco[0:8]
        out_ref[:, 0:512] = dz.astype(BF16)
        out_ref[:, 512:1536] = du.astype(BF16)
        out_ref[:, 1536:1664] = ddtr.astype(BF16)

    e3, et2, tril3, triu3 = mats
    RB = CPS * BLK
    rev = lambda i: nsteps - 1 - i
    full = lambda a: pl.BlockSpec(a.shape, lambda i: (0,) * a.ndim)
    acc = lambda r, c: pl.BlockSpec((r, c), lambda i: (0, 0))
    return pl.pallas_call(
        body, name="ssd_bwd", grid=(nsteps,),
        in_specs=[pl.BlockSpec((RB, CONVC), lambda i: (rev(i), 0)), pl.BlockSpec((RB, CONVC), lambda i: (rev(i), 0)),
                  pl.BlockSpec((RB, SW), lambda i: (rev(i), 0)), pl.BlockSpec((RB, 128), lambda i: (rev(i), 0)),
                  pl.BlockSpec((RB, SW), lambda i: (rev(i), 0)), pl.BlockSpec((CPS, NST, SW), lambda i: (rev(i), 0, 0)),
                  pl.BlockSpec((RB, SW), lambda i: (rev(i), 1)),
                  full(conv_w), full(dtb), full(alog), full(dskx), full(ssm_w),
                  full(e3), full(et2), full(tril3), full(triu3), DEP_SPEC],
        out_specs=[pl.BlockSpec((RB, 1664), lambda i: (rev(i), 0)),
                   acc(CONVK, CONVC), acc(1, CONVC), acc(1, SW), acc(1, 128), acc(1, 128), acc(1, 128)],
        out_shape=[jax.ShapeDtypeStruct((T, 1664), BF16),
                   jax.ShapeDtypeStruct((CONVK, CONVC), F32), jax.ShapeDtypeStruct((1, CONVC), F32),
                   jax.ShapeDtypeStruct((1, SW), F32), jax.ShapeDtypeStruct((1, 128), F32),
                   jax.ShapeDtypeStruct((1, 128), F32), jax.ShapeDtypeStruct((1, 128), F32)],
        scratch_shapes=[pltpu.VMEM((NST, SW), F32), pltpu.VMEM((8, CONVC), F32), pltpu.VMEM((1, SW), F32)],
        compiler_params=_cp("arbitrary"),
    )(xbc, co_all, z, dtr, ypre, states, dmix, conv_w, dtb, alog, dskx, ssm_w, e3, et2, tril3, triu3, dep)


def _mix_ffn(x, attn, ynorm, tgt, mod6, norm2_w, final_w, w_out, w_gu, w_gu_own, s_arr, w_dn, tm):
    T = x.shape[0]
    nt = T // tm

    def body(x_ref, a_ref, y_ref, t_ref, mod_ref, n2_ref, fw_ref, wo_hbm, wgu_hbm, own_hbm, s_ref, wdn_hbm,
             sq_ref, dmix_ref, dx1_ref, h2_ref, act_ref, df_ref, dgu_ref, do_ref, sm_ref,
             wo, wgu, wdn, sems):
        i = pl.program_id(0)

        @pl.when(i == 0)
        def _():
            cps = [pltpu.make_async_copy(s, d, sems.at[k]) for k, (s, d) in
                   enumerate(((wo_hbm, wo), (wgu_hbm, wgu), (wdn_hbm, wdn)))]
            for c in cps:
                c.start()
            for c in cps:
                c.wait()
            own = pltpu.make_async_copy(
                own_hbm, wgu.at[:, pl.ds(pl.multiple_of(s_ref[0] * GU_SH, 128), GU_SH)], sems.at[3])
            own.start()
            own.wait()
            sq_ref[...] = jnp.zeros_like(sq_ref)
            sm_ref[...] = jnp.zeros_like(sm_ref)

        gate1, shift2, scale2, gate2 = mod_ref[2:3, :], mod_ref[3:4, :], mod_ref[4:5, :], mod_ref[5:6, :]
        n2w, fw = n2_ref[...], fw_ref[...]
        o = _dot(a_ref[...], wo[0:AW, :]) + _dot(y_ref[...], wo[AW:D, :])
        x1 = x_ref[...] + gate1 * o
        r2 = lax.rsqrt(jnp.mean(x1 * x1, axis=-1, keepdims=True) + EPS)
        xh2 = x1 * r2
        n2 = xh2 * n2w
        h2b = (n2 * (1.0 + scale2) + shift2).astype(BF16)
        h2_ref[...] = h2b
        f = jnp.zeros((tm, D), F32)
        saved = []
        for a, b in FF_SPLITS:
            gp = _dot(h2b, wgu[:, a:b])
            upj = _dot(h2b, wgu[:, DFF + a:DFF + b])
            sg = _sigmoid(gp)
            sl = gp * sg
            actb = (sl * upj).astype(BF16)
            act_ref[:, a:b] = actb
            f = f + _dot(actb, wdn[a:b, :])
            saved.append((gp, upj, sg, sl))
        x2 = x1 + gate2 * f
        r3 = lax.rsqrt(jnp.mean(x2 * x2, axis=-1, keepdims=True) + EPS)
        xh3 = x2 * r3
        err = xh3 * fw - t_ref[...]
        sq_ref[...] += jnp.sum(err * err, axis=0, keepdims=True)
        dy = err * (1.0 / D)
        dfw = jnp.sum(dy * xh3, axis=0, keepdims=True)
        dxh3 = dy * fw
        dx2 = r3 * (dxh3 - xh3 * jnp.mean(dxh3 * xh3, axis=-1, keepdims=True))
        dgate2 = jnp.sum(dx2 * f, axis=0, keepdims=True)
        dfb = (dx2 * gate2).astype(BF16)
        df_ref[...] = dfb
        dh2 = jnp.zeros((tm, D), F32)
        for (a, b), (gp, upj, sg, sl) in zip(FF_SPLITS, saved):
            dact = _dot_nt(dfb, wdn[a:b, :])
            dg = (dact * upj * (sg * (1.0 + gp * (1.0 - sg)))).astype(BF16)
            du = (dact * sl).astype(BF16)
            dgu_ref[:, a:b] = dg
            dgu_ref[:, DFF + a:DFF + b] = du
            dh2 = dh2 + _dot_nt(dg, wgu[:, a:b]) + _dot_nt(du, wgu[:, DFF + a:DFF + b])
        dshift2 = jnp.sum(dh2, axis=0, keepdims=True)
        dscale2 = jnp.sum(dh2 * n2, axis=0, keepdims=True)
        dn2 = dh2 * (1.0 + scale2)
        dn2w = jnp.sum(dn2 * xh2, axis=0, keepdims=True)
        dxh2 = dn2 * n2w
        dx1 = dx2 + r2 * (dxh2 - xh2 * jnp.mean(dxh2 * xh2, axis=-1, keepdims=True))
        dx1_ref[...] = dx1
        dgate1 = jnp.sum(dx1 * o, axis=0, keepdims=True)
        dob = (dx1 * gate1).astype(BF16)
        do_ref[...] = dob
        dmix_ref[...] = _dot_nt(dob, wo[...])
        sm_ref[...] += jnp.concatenate(
            [dfw, dn2w, dshift2, dscale2, dgate2, dgate1, jnp.zeros((2, D), F32)], axis=0)

    row = lambda w: pl.BlockSpec((tm, w), lambda i: (i, 0))
    full = lambda a: pl.BlockSpec(a.shape, lambda i: (0,) * a.ndim)
    anyspec = pl.BlockSpec(memory_space=pl.ANY)
    return pl.pallas_call(
        body, name="mix_ffn", grid=(nt,),
        in_specs=[row(D), row(AW), row(SW), row(D), full(mod6), full(norm2_w), full(final_w), anyspec, anyspec, anyspec,
                  pl.BlockSpec(memory_space=pltpu.SMEM), anyspec],
        out_specs=[pl.BlockSpec((1, D), lambda i: (0, 0)), row(D), row(D), row(D),
                   row(DFF), row(D), row(2 * DFF), row(D), pl.BlockSpec((8, D), lambda i: (0, 0))],
        out_shape=[jax.ShapeDtypeStruct((1, D), F32), jax.ShapeDtypeStruct((T, D), F32), jax.ShapeDtypeStruct((T, D), F32),
                   jax.ShapeDtypeStruct((T, D), BF16), jax.ShapeDtypeStruct((T, DFF), BF16),
                   jax.ShapeDtypeStruct((T, D), BF16), jax.ShapeDtypeStruct((T, 2 * DFF), BF16),
                   jax.ShapeDtypeStruct((T, D), BF16), jax.ShapeDtypeStruct((8, D), F32)],
        scratch_shapes=[pltpu.VMEM((D, D), BF16), pltpu.VMEM((D, 2 * DFF), BF16), pltpu.VMEM((DFF, D), BF16),
                        pltpu.SemaphoreType.DMA((4,))],
        compiler_params=_cp("arbitrary"),
    )(x, attn, ynorm, tgt, mod6, norm2_w, final_w, w_out, w_gu, w_gu_own, s_arr, w_dn)


def _in_proj_bwd(x, dx1, dqkv, dzxd, mod6, norm1_w, w_pad, tm, dep):
    T = x.shape[0]

    def body(x_ref, dx1_ref, dq_ref, dz_ref, mod_ref, nw_ref, w_hbm, dep_ref, gx_ref, sm_ref, w_vmem, sem):
        _load_resident(w_hbm, w_vmem, sem)

        @pl.when(pl.program_id(0) == 0)
        def _():
            sm_ref[...] = jnp.zeros_like(sm_ref)

        nw = nw_ref[...]
        scale1 = mod_ref[1:2, :]
        sums = jnp.zeros((8, D), F32)
        for rows in (slice(0, tm // 2), slice(tm // 2, tm)):
            dh = _dot_nt(dq_ref[rows, :], w_vmem[:, 0:768]) + _dot_nt(dz_ref[rows, :], w_vmem[:, 768:IN_PAD])
            xv = x_ref[rows, :]
            r = lax.rsqrt(jnp.mean(xv * xv, axis=-1, keepdims=True) + EPS)
            xh = xv * r
            n1 = xh * nw
            dshift = jnp.sum(dh, axis=0, keepdims=True)
            dscale = jnp.sum(dh * n1, axis=0, keepdims=True)
            dn = dh * (1.0 + scale1)
            dnw = jnp.sum(dn * xh, axis=0, keepdims=True)
            dxh = dn * nw
            gx_ref[rows, :] = dx1_ref[rows, :] + r * (dxh - xh * jnp.mean(dxh * xh, axis=-1, keepdims=True))
            sums = sums + jnp.concatenate([dnw, dshift, dscale, jnp.zeros((5, D), F32)], axis=0)
        sm_ref[...] += sums

    row = lambda w: pl.BlockSpec((tm, w), lambda i: (i, 0))
    full = lambda a: pl.BlockSpec(a.shape, lambda i: (0,) * a.ndim)
    return pl.pallas_call(
        body, name="in_proj_bwd", grid=(T // tm,),
        in_specs=[row(D), row(D), row(768), row(1664), full(mod6), full(norm1_w), pl.BlockSpec(memory_space=pl.ANY),
                  DEP_SPEC],
        out_specs=[row(D), pl.BlockSpec((8, D), lambda i: (0, 0))],
        out_shape=[jax.ShapeDtypeStruct((T, D), F32), jax.ShapeDtypeStruct((8, D), F32)],
        scratch_shapes=[pltpu.VMEM((D, IN_PAD), BF16), pltpu.SemaphoreType.DMA],
        compiler_params=_cp("arbitrary"),
    )(x, dx1, dqkv, dzxd, mod6, norm1_w, w_pad, dep)


def _tn_matmul(a, b, K, N, tt, name, dep):
    T = a.shape[0]
    ja, jb = a.shape[1] // K, b.shape[1] // N
    J = max(ja, jb)

    def body(a_ref, b_ref, dep_ref, o_ref):
        t = pl.program_id(1)
        prod = _dot_tn(a_ref[...], b_ref[...])

        @pl.when(t == 0)
        def _():
            o_ref[0] = prod

        @pl.when(t > 0)
        def _():
            o_ref[0] += prod

    return pl.pallas_call(
        body, name=name, grid=(J, T // tt),
        in_specs=[pl.BlockSpec((tt, K), lambda j, t: (t, j if ja > 1 else 0)),
                  pl.BlockSpec((tt, N), lambda j, t: (t, j if jb > 1 else 0)),
                  pl.BlockSpec((8, 128), lambda j, t: (0, 0))],
        out_specs=pl.BlockSpec((1, K, N), lambda j, t: (j, 0, 0)),
        out_shape=jax.ShapeDtypeStruct((J, K, N), F32),
        compiler_params=_cp("parallel", "arbitrary"),
    )(a, b, dep)


def _accumulate(o_ref, rows, prod):
    @pl.when(pl.program_id(0) == 0)
    def _():
        o_ref[rows, :] = prod

    @pl.when(pl.program_id(0) > 0)
    def _():
        o_ref[rows, :] += prod


def _tn_matmul_rows(a0, a1, b, tt, name, dep):
    T, K = a0.shape
    N = b.shape[1]

    def body(a0_ref, a1_ref, b_ref, dep_ref, o_ref):
        for k, a_ref in enumerate((a0_ref, a1_ref)):
            _accumulate(o_ref, slice(k * K, (k + 1) * K), _dot_tn(a_ref[...], b_ref[...]))

    tile = lambda w: pl.BlockSpec((tt, w), lambda t: (t, 0))
    return pl.pallas_call(
        body, name=name, grid=(T // tt,), in_specs=[tile(K), tile(K), tile(N), DEP_SPEC],
        out_specs=pl.BlockSpec((2 * K, N), lambda t: (0, 0)), out_shape=jax.ShapeDtypeStruct((2 * K, N), F32),
        compiler_params=_cp("arbitrary"),
    )(a0, a1, b, dep)


def _tn_matmul_cols(a, b0, b1, tt, name, dep):
    T, K = a.shape

    def body(a_ref, b0_ref, b1_ref, dep_ref, o0_ref, o1_ref):
        for b_ref, o_ref in ((b0_ref, o0_ref), (b1_ref, o1_ref)):
            _accumulate(o_ref, slice(None), _dot_tn(a_ref[...], b_ref[...]))

    tile = lambda w: pl.BlockSpec((tt, w), lambda t: (t, 0))
    whole = lambda w: pl.BlockSpec((K, w), lambda t: (0, 0))
    return pl.pallas_call(
        body, name=name, grid=(T // tt,), in_specs=[tile(K), tile(b0.shape[1]), tile(b1.shape[1]), DEP_SPEC],
        out_specs=[whole(b0.shape[1]), whole(b1.shape[1])],
        out_shape=[jax.ShapeDtypeStruct((K, b.shape[1]), F32) for b in (b0, b1)],
        compiler_params=_cp("arbitrary"),
    )(a, b0, b1, dep)


def _adam_math(w, g, m, v):
    m = B1 * m + (1.0 - B1) * g
    v = B2 * v + (1.0 - B2) * (g * g)
    m_hat = m / (1.0 - B1 ** STEP)
    v_hat = v / (1.0 - B2 ** STEP)
    delta = -LR * (m_hat / (jnp.sqrt(v_hat) + AEPS) + WD * w)
    return delta, m, v


def _adam_2d(w, mine, land, m, v, c_arr, rb, name, dep):
    R, C = w.shape
    nbh = R // 2 // rb

    def body(c_ref, w_ref, mine_ref, land_ref, m_ref, v_ref, dep_ref, go_ref, d_ref, mo_ref, vo_ref):
        g = jnp.where(pl.program_id(0) // nbh == c_ref[0], mine_ref[...], land_ref[...])
        d, mn, vn = _adam_math(w_ref[...], g, m_ref[...], v_ref[...])
        go_ref[...] = g
        d_ref[...] = d
        mo_ref[...] = mn
        vo_ref[...] = vn

    spec = pl.BlockSpec((rb, C), lambda i, c_ref: (i, 0))
    mine_spec = pl.BlockSpec((rb, C), lambda i, c_ref: (jnp.clip(i - c_ref[0] * nbh, 0, nbh - 1), 0))
    return pl.pallas_call(
        body, name=name,
        grid_spec=pltpu.PrefetchScalarGridSpec(
            num_scalar_prefetch=1, grid=(R // rb,), in_specs=[spec, mine_spec, spec, spec, spec, DEP_SPEC],
            out_specs=[spec] * 4),
        out_shape=[jax.ShapeDtypeStruct((R, C), F32)] * 4, compiler_params=_cp("parallel"),
    )(c_arr, w, mine, land, m, v, dep)


def _adam_w_in(w3, mine, land, m3, v3, c_arr):
    n = w3.shape[0]

    def body(c_ref, w_hbm, mine_ref, land_ref, m_hbm, v_hbm, g_hbm, d_hbm, mo_hbm, vo_hbm, bufs, sems):
        ins = [pltpu.make_async_copy(src.at[:, 0], bufs.at[k], sems.at[k]) for k, src in enumerate((w_hbm, m_hbm, v_hbm))]
        for cp in ins:
            cp.start()
        half = D // 2
        top = jnp.where(c_ref[0] == 0, mine_ref[...], land_ref[0:half, :])
        bot = jnp.where(c_ref[0] == 1, mine_ref[...], land_ref[half:D, :])
        g = jnp.concatenate([top, bot], axis=0)
        eye = (_iota((D, D), 0) == _iota((D, D), 1)).astype(BF16)
        g_t = jnp.zeros((n, D), F32)
        r = g
        for i in range(3):
            p = r.astype(BF16)
            g_t = g_t + _dot_tn(p, eye)
            if i < 2:
                r = r - p.astype(F32)
        for cp in ins:
            cp.wait()
        d, mn, vn = _adam_math(bufs[0], g_t, bufs[1], bufs[2])
        for k, val in enumerate((g_t, d, mn, vn)):
            bufs[3 + k] = val
        outs = [pltpu.make_async_copy(bufs.at[3 + k], dst.at[:, 0], sems.at[3 + k])
                for k, dst in enumerate((g_hbm, d_hbm, mo_hbm, vo_hbm))]
        for cp in outs:
            cp.start()
        for cp in outs:
            cp.wait()

    anyspec = pl.BlockSpec(memory_space=pl.ANY)
    vm = pl.BlockSpec(memory_space=pltpu.VMEM)
    return pl.pallas_call(
        body, name="adam_w_in",
        in_specs=[pl.BlockSpec(memory_space=pltpu.SMEM), anyspec, vm, vm, anyspec, anyspec], out_specs=[anyspec] * 4,
        out_shape=[jax.ShapeDtypeStruct(w3.shape, F32)] * 4,
        scratch_shapes=[pltpu.VMEM((7, n, D), F32), pltpu.SemaphoreType.DMA((7,))],
        compiler_params=pltpu.CompilerParams(vmem_limit_bytes=VMEM_LIMIT),
    )(c_arr, w3, mine, land, m3, v3)


def _adam_w_ada(gat, allv, s_arr, w, m, v, rb):
    R, C = w.shape

    def body(s_ref, c_ref, dm_ref, w_ref, m_ref, v_ref, g_ref, d_ref, mo_ref, vo_ref):
        cm = _rows_select(c_ref, rb)
        g = lax.dot_general(cm * _sigmoid(cm), _rows_select(dm_ref, C), (((0,), (0,)), ((), ())), precision=HI,
                            preferred_element_type=F32)
        d, mn, vn = _adam_math(w_ref[...], g, m_ref[...], v_ref[...])
        g_ref[...] = g
        d_ref[...] = d
        mo_ref[...] = mn
        vo_ref[...] = vn

    spec = pl.BlockSpec((rb, C), lambda i, s_ref: (i, 0))
    return pl.pallas_call(
        body, name="adam_w_ada",
        grid_spec=pltpu.PrefetchScalarGridSpec(
            num_scalar_prefetch=1, grid=(R // rb,),
            in_specs=[pl.BlockSpec((8, 1, rb), lambda i, s_ref: (0, 0, i)),
                      pl.BlockSpec((8, 1, C), lambda i, s_ref: (0, 0, s_ref[0])), spec, spec, spec],
            out_specs=[spec] * 4),
        out_shape=[jax.ShapeDtypeStruct((R, C), F32)] * 4, compiler_params=_cp("parallel"),
    )(s_arr, gat, allv, w, m, v)


def _adam_small(tot, segs, ws, ms, vs):
    k = len(ws)
    extra = [sg for sg in segs if not isinstance(sg, tuple)]
    ne = len(extra)

    def body(*refs):
        tot_ref, g_x = refs[0], list(refs[1:1 + ne])
        w, m, v = [refs[1 + ne + j * k:1 + ne + (j + 1) * k] for j in range(3)]
        g_o, d_o, m_o, v_o = [refs[1 + ne + (3 + j) * k:1 + ne + (4 + j) * k] for j in range(4)]
        for i in range(k):
            gi = tot_ref[:, segs[i][0]:segs[i][0] + segs[i][1]] if isinstance(segs[i], tuple) else g_x.pop(0)[...]
            d, mn, vn = _adam_math(w[i][...], gi, m[i][...], v[i][...])
            g_o[i][...] = gi
            d_o[i][...] = d
            m_o[i][...] = mn
            v_o[i][...] = vn

    shapes = [jax.ShapeDtypeStruct(w.shape, F32) for w in ws]
    vm = pl.BlockSpec(memory_space=pltpu.VMEM)
    outs = pl.pallas_call(
        body, name="adam_small", in_specs=[vm] * (1 + ne + 3 * k), out_specs=[vm] * (4 * k), out_shape=shapes * 4,
    )(tot, *extra, *ws, *ms, *vs)
    return outs[0:k], outs[k:2 * k], outs[2 * k:3 * k], outs[3 * k:4 * k]


def _pos():
    return lax.axis_index("x"), lax.axis_index("y"), lax.axis_index("c")


def _flip(v, bit):
    return 1 - v if bit else v


def _peer(k):
    x, y, c = _pos()
    return (_flip(x, (k >> 2) & 1), _flip(y, (k >> 1) & 1), _flip(c, k & 1))


def _logical(p):
    return 4 * p[0] + 2 * p[1] + p[2]


def _gather8(src_ref, dst_ref, send_sems, recv_sems):
    me = _logical(_pos())
    dst_ref[pl.ds(me, 1)] = src_ref[...][None]
    copies = []
    for k in range(1, 8):
        cp = pltpu.make_async_remote_copy(src_ref, dst_ref.at[me], send_sems.at[k - 1], recv_sems.at[k - 1],
                                          device_id=_peer(k), device_id_type=MESH)
        cp.start()
        copies.append(cp)
    for k in range(1, 8):
        pltpu.make_async_remote_copy(src_ref, dst_ref.at[_logical(_peer(k))], send_sems.at[k - 1], recv_sems.at[k - 1],
                                     device_id=_peer(k), device_id_type=MESH).wait_recv()
    for cp in copies:
        cp.wait_send()


def _rows_select(ref3, width):
    row = _iota((8, width), 0)
    out = jnp.zeros((8, width), F32)
    for i in range(8):
        out = jnp.where(row == i, ref3[i][:, 0:width], out)
    return out


def _mod_exchange(payload, w_ada_s, b_ada4):
    n_sh = w_ada_s.shape[1]

    def body(pay_ref, w_ref, b_ref, gat_ref, mod_ref, token, p3, sa, ra, sb, rb):
        token[...] = jnp.zeros_like(token)
        x, y, c = _pos()
        me = _logical((x, y, c))
        my_s = 2 * x + y
        _gather8(pay_ref, gat_ref, sa, ra)
        cmat = _rows_select(gat_ref, D)
        prod = _dot_hi(cmat * _sigmoid(cmat), w_ref[...])
        for b in range(8):
            p3[b] = prod[b:b + 1, :]
        mod_ref[pl.ds(my_s, 1)] = p3[pl.ds(me, 1)] + b_ref[pl.ds(my_s, 1)]
        ks = (2, 4, 6)
        copies = []
        for i, k in enumerate(ks):
            pr = _peer(k)
            cp = pltpu.make_async_remote_copy(p3.at[_logical(pr)], mod_ref.at[my_s], sb.at[i], rb.at[i],
                                              device_id=pr, device_id_type=MESH)
            cp.start()
            copies.append(cp)
        for i, k in enumerate(ks):
            pr = _peer(k)
            s_src = 2 * pr[0] + pr[1]
            pltpu.make_async_remote_copy(p3.at[0], mod_ref.at[s_src], sb.at[i], rb.at[i],
                                         device_id=pr, device_id_type=MESH).wait_recv()
            mod_ref[pl.ds(s_src, 1)] = mod_ref[pl.ds(s_src, 1)] + b_ref[pl.ds(s_src, 1)]
        for cp in copies:
            cp.wait_send()

    vm = pl.BlockSpec(memory_space=pltpu.VMEM)
    return pl.pallas_call(
        body, name="mod_exchange", in_specs=[vm, vm, vm], out_specs=[vm, vm, vm],
        out_shape=[jax.ShapeDtypeStruct((8, 1, payload.shape[1]), F32), jax.ShapeDtypeStruct((4, 1, n_sh), F32),
                   jax.ShapeDtypeStruct((8, 128), F32)],
        scratch_shapes=[pltpu.VMEM((8, 1, n_sh), F32), pltpu.SemaphoreType.DMA((7,)), pltpu.SemaphoreType.DMA((7,)),
                        pltpu.SemaphoreType.DMA((3,)), pltpu.SemaphoreType.DMA((3,))],
        compiler_params=pltpu.CompilerParams(vmem_limit_bytes=VMEM_LIMIT),
    )(payload, w_ada_s, b_ada4)


def _chips():
    x, y, _ = _pos()
    out = []
    for k in (1, 2, 3):
        px, py = _flip(x, (k >> 1) & 1), _flip(y, k & 1)
        out.append((px, py, 2 * px + py))
    return out


def _half_rows(ref, which):
    half = ref.shape[-2] // 2
    return pl.ds(pl.multiple_of(which * half, 8), half)


def _plan_small():
    def plan(refs):
        me = _logical(_pos())
        return [(refs[0], refs[1].at[me], _peer(k), refs[1].at[_logical(_peer(k))]) for k in range(1, 8)]
    return plan


def _small_sum(vec, land, me_arr):
    n = vec.shape[1]

    def body(me_ref, v_ref, land_ref, tot_ref, all_ref):
        tot = None
        for i in range(8):
            row = jnp.where(me_ref[0] == i, v_ref[...], land_ref[i])
            all_ref[i] = row
            tot = row if i == 0 else tot + row
        tot_ref[...] = tot

    return pl.pallas_call(
        body, name="small_sum",
        grid_spec=pltpu.PrefetchScalarGridSpec(
            num_scalar_prefetch=1, grid=(1,),
            in_specs=[pl.BlockSpec((1, n), lambda i, me_ref: (0, 0)), pl.BlockSpec((8, 1, n), lambda i, me_ref: (0, 0, 0))],
            out_specs=[pl.BlockSpec((1, n), lambda i, me_ref: (0, 0)),
                       pl.BlockSpec((8, 1, n), lambda i, me_ref: (0, 0, 0))]),
        out_shape=[jax.ShapeDtypeStruct((1, n), F32), jax.ShapeDtypeStruct((8, 1, n), F32)],
        compiler_params=_cp("arbitrary"),
    )(me_arr, vec, land)


def _add_half(g, sib, c_arr, rb, name):
    _, R, C = g.shape
    half = R // 2
    nb = half // rb

    def body(c_ref, g_ref, s_ref, o_ref):
        o_ref[...] = (g_ref[...] + s_ref[...]).astype(BF16)

    return pl.pallas_call(
        body, name=name,
        grid_spec=pltpu.PrefetchScalarGridSpec(
            num_scalar_prefetch=1, grid=(4, nb),
            in_specs=[pl.BlockSpec((1, rb, C), lambda s, i, c_ref: (s, c_ref[0] * nb + i, 0)),
                      pl.BlockSpec((1, rb, C), lambda s, i, c_ref: (s, i, 0))],
            out_specs=pl.BlockSpec((1, rb, C), lambda s, i, c_ref: (s, i, 0))),
        out_shape=jax.ShapeDtypeStruct((4, half, C), BF16),
        compiler_params=_cp("parallel", "parallel"),
    )(c_arr, g, sib)


def _add_half_in(gq, gz, sibq, sibz, c_arr, rb):
    half = D // 2
    nq = gq.shape[1]
    wide = -(-IN_SH // 128) * 128

    def sel(rows, first, lo):
        return (_iota((rows, wide), 0) + (first - lo) == _iota((rows, wide), 1)).astype(BF16)

    def body(c_ref, gq_ref, gz_ref, sq_ref, sz_ref, o_ref):
        q = (gq_ref[...] + sq_ref[...]).astype(BF16)
        z = (gz_ref[...] + sz_ref[...]).astype(BF16)
        for s in range(4):
            lo, hi = s * IN_SH, (s + 1) * IN_SH
            acc = jnp.zeros((rb, wide), F32)
            if lo < nq:
                a0, a1 = lo // 128 * 128, min(nq, -(-min(hi, nq) // 128) * 128)
                acc = acc + _dot(q[:, a0:a1], sel(a1 - a0, a0, lo))
            if hi > nq:
                a0, a1 = (max(lo, nq) - nq) // 128 * 128, -(-(hi - nq) // 128) * 128
                acc = acc + _dot(z[:, a0:a1], sel(a1 - a0, nq + a0, lo))
            o_ref[s] = acc[:, :IN_SH].astype(BF16)

    nb = half // rb
    mine = lambda w: pl.BlockSpec((rb, w), lambda i, c_ref: (c_ref[0] * nb + i, 0))
    sib = lambda w: pl.BlockSpec((rb, w), lambda i, c_ref: (i, 0))
    return pl.pallas_call(
        body, name="grad_add_in",
        grid_spec=pltpu.PrefetchScalarGridSpec(
            num_scalar_prefetch=1, grid=(nb,),
            in_specs=[mine(nq), mine(gz.shape[1]), sib(nq), sib(gz.shape[1])],
            out_specs=pl.BlockSpec((4, rb, IN_SH), lambda i, c_ref: (0, i, 0))),
        out_shape=jax.ShapeDtypeStruct((4, half, IN_SH), BF16),
        compiler_params=_cp("parallel"),
    )(c_arr, gq, gz, sibq, sibz)


def _sum4(parts, land, s_arr, rb, name):
    _, H, C = land.shape

    def body(s_ref, own_ref, r_ref, o_ref):
        own = own_ref[0].astype(F32)
        tot = jnp.zeros((rb, C), F32)
        for j in range(4):
            tot = tot + jnp.where(s_ref[0] == j, own, r_ref[j].astype(F32))
        o_ref[...] = tot

    return pl.pallas_call(
        body, name=name,
        grid_spec=pltpu.PrefetchScalarGridSpec(
            num_scalar_prefetch=1, grid=(H // rb,),
            in_specs=[pl.BlockSpec((1, rb, C), lambda i, s_ref: (s_ref[0], i, 0)),
                      pl.BlockSpec((4, rb, C), lambda i, s_ref: (0, i, 0))],
            out_specs=pl.BlockSpec((rb, C), lambda i, s_ref: (i, 0))),
        out_shape=jax.ShapeDtypeStruct((H, C), F32), compiler_params=_cp("parallel"),
    )(s_arr, parts, land)


HBM_SPEC = pl.BlockSpec(memory_space=pltpu.HBM)
SEM_SPEC = pl.BlockSpec(memory_space=pltpu.SEMAPHORE)
EFFECT = pltpu.SideEffectType.DATAFLOW_SIDE_EFFECTING


def _split_start(name, bufs, n_sem, plan, dep):
    nb = len(bufs)

    def body(*refs):
        ins, send, recv, token = refs[:nb], refs[nb + 1], refs[nb + 2], refs[-1]
        for i, (src, dst, dev, _) in enumerate(plan(ins)):
            pltpu.make_async_remote_copy(src, dst, send.at[i], recv.at[i], device_id=dev, device_id_type=MESH).start()
        token[...] = jnp.zeros_like(token)

    outs = pl.pallas_call(
        body, name=name,
        out_shape=(pltpu.SemaphoreType.DMA((n_sem,)), pltpu.SemaphoreType.DMA((n_sem,)),
                   *[pltpu.HBM(b.shape, b.dtype) for b in bufs], jax.ShapeDtypeStruct((8, 128), F32)),
        in_specs=[HBM_SPEC] * nb + [pl.BlockSpec(memory_space=pl.ANY)],
        out_specs=(SEM_SPEC, SEM_SPEC, *([HBM_SPEC] * nb), pl.BlockSpec(memory_space=pltpu.VMEM)),
        input_output_aliases={i: 2 + i for i in range(nb)},
        compiler_params=pltpu.CompilerParams(has_side_effects=EFFECT),
    )(*[pltpu.with_memory_space_constraint(b, pltpu.HBM) for b in bufs], dep)
    return outs[0], outs[1], list(outs[2:2 + nb]), outs[-1]


def _split_wait(name, send, recv, bufs, after, plan):
    nb = len(bufs)
    after = list(after) if isinstance(after, (list, tuple)) else [after]

    def body(*refs):
        ins, send_s, recv_s = refs[:nb], refs[nb], refs[nb + 1]
        for i, (src, dst, dev, mine) in enumerate(plan(ins)):
            pltpu.make_async_remote_copy(src, dst, send_s.at[i], recv_s.at[i], device_id=dev,
                                         device_id_type=MESH).wait_send()
            pltpu.make_async_remote_copy(src, mine, send_s.at[i], recv_s.at[i], device_id=dev,
                                         device_id_type=MESH).wait_recv()

    outs = pl.pallas_call(
        body, name=name, out_shape=[pltpu.HBM(b.shape, b.dtype) for b in bufs],
        in_specs=[HBM_SPEC] * nb + [SEM_SPEC, SEM_SPEC] + [HBM_SPEC] * len(after),
        out_specs=[HBM_SPEC] * nb, input_output_aliases={i: i for i in range(nb)},
        compiler_params=pltpu.CompilerParams(has_side_effects=EFFECT),
    )(*bufs, send, recv, *[pltpu.with_memory_space_constraint(a, pltpu.HBM) for a in after])
    return list(outs)


def _copies_now(name, bufs, n_sem, plan):
    nb = len(bufs)

    def body(*refs):
        ins, token, send, recv = refs[:nb], refs[2 * nb], refs[-2], refs[-1]
        token[...] = jnp.zeros_like(token)
        todo = plan(ins)
        for i, (src, dst, dev, _) in enumerate(todo):
            pltpu.make_async_remote_copy(src, dst, send.at[i], recv.at[i], device_id=dev, device_id_type=MESH).start()
        for i, (src, dst, dev, mine) in enumerate(todo):
            pltpu.make_async_remote_copy(src, mine, send.at[i], recv.at[i], device_id=dev, device_id_type=MESH).wait_recv()
        for i, (src, dst, dev, _) in enumerate(todo):
            pltpu.make_async_remote_copy(src, dst, send.at[i], recv.at[i], device_id=dev, device_id_type=MESH).wait_send()

    outs = pl.pallas_call(
        body, name=name,
        out_shape=[pltpu.HBM(b.shape, b.dtype) for b in bufs] + [jax.ShapeDtypeStruct((8, 128), F32)],
        in_specs=[HBM_SPEC] * nb, out_specs=[HBM_SPEC] * nb + [pl.BlockSpec(memory_space=pltpu.VMEM)],
        input_output_aliases={i: i for i in range(nb)},
        scratch_shapes=[pltpu.SemaphoreType.DMA((n_sem,)), pltpu.SemaphoreType.DMA((n_sem,))],
    )(*[pltpu.with_memory_space_constraint(b, pltpu.HBM) for b in bufs])
    return list(outs[:nb]), outs[nb]


def _slot(land, s, rows, cols):
    if cols is None:
        return land.at[s, rows]
    return land.at[rows, pl.ds(pl.multiple_of(s * cols, 128), cols)]


def _plan_gather_ici(cols):
    nw = len(cols)

    def plan(refs):
        x, y, c = _pos()
        my_s = 2 * x + y
        out = []
        for w in range(nw):
            mine = _half_rows(refs[w], c)
            for px, py, ps in _chips():
                out.append((refs[w].at[mine], _slot(refs[nw + w], my_s, mine, cols[w]), (px, py, c),
                            _slot(refs[nw + w], ps, mine, cols[w])))
        return out
    return plan


def _plan_gather_fwd(cols, rows):
    def plan(refs):
        x, y, c = _pos()
        out = []
        for w in range(len(cols)):
            half = rows[w] // 2
            mine = pl.ds(pl.multiple_of(c * half, 8), half)
            other = pl.ds(pl.multiple_of((1 - c) * half, 8), half)
            for px, py, ps in _chips():
                got = _slot(refs[w], ps, mine, cols[w])
                out.append((got, got, (x, y, 1 - c), _slot(refs[w], ps, other, cols[w])))
        return out
    return plan


def _plan_swap(nw):
    def plan(refs):
        x, y, c = _pos()
        return [(refs[w].at[:, _half_rows(refs[w], 1 - c)], refs[nw + w], (x, y, 1 - c), refs[nw + w])
                for w in range(nw)]
    return plan


def _plan_swap_rows(nw):
    def plan(refs):
        x, y, c = _pos()
        return [(refs[w].at[_half_rows(refs[w], 1 - c)], refs[nw + w], (x, y, 1 - c), refs[nw + w])
                for w in range(nw)]
    return plan


def _plan_scatter(nw):
    def plan(refs):
        x, y, c = _pos()
        my_s = 2 * x + y
        out = []
        for w in range(nw):
            for px, py, ps in _chips():
                out.append((refs[w].at[ps], refs[nw + w].at[my_s], (px, py, c), refs[nw + w].at[ps]))
        return out
    return plan


def _plan_join(nw):
    def plan(refs):
        x, y, c = _pos()
        out = []
        for w in range(nw):
            land = refs[nw + w]
            out.append((refs[w], land.at[_half_rows(land, c)], (x, y, 1 - c), land.at[_half_rows(land, 1 - c)]))
        return out
    return plan


def _hbm_empty(shape, dtype):
    return pltpu.with_memory_space_constraint(lax.empty(shape, dtype), pltpu.HBM)


def _put_slot(land, own, slot):
    return lax.dynamic_update_slice(land, own[None], (slot,) + (0,) * own.ndim)


def _pad_lanes(a, n):
    return jnp.pad(a, ((0, 0), (0, n - a.shape[1])))


def kernel(x, c, positions, w_ada, b_ada, norm1_w, w_in, conv_w, conv_b, dt_bias, a_log, d_skip, attn_sinks, ssm_norm_w, w_out, norm2_w, w_gate_up, w_down, final_norm_w, loss_target, m_w_ada, m_b_ada, m_norm1_w, m_w_in, m_conv_w, m_conv_b, m_dt_bias, m_a_log, m_d_skip, m_attn_sinks, m_ssm_norm_w, m_w_out, m_norm2_w, m_w_gate_up, m_w_down, m_final_norm_w, v_w_ada, v_b_ada, v_norm1_w, v_w_in, v_conv_w, v_conv_b, v_dt_bias, v_a_log, v_d_skip, v_attn_sinks, v_ssm_norm_w, v_w_out, v_norm2_w, v_w_gate_up, v_w_down, v_final_norm_w):
    T = x.shape[1]
    tm = min(256, T)
    xi, yi, ci = lax.axis_index("x"), lax.axis_index("y"), lax.axis_index("c")
    my_s = 2 * xi + yi
    xs = x[0]
    tgt = loss_target[0]

    payload = jnp.concatenate([c, conv_w[0].reshape(1, CONVK * 256)], axis=1)
    gat, mod4, tok = _mod_exchange(payload, w_ada[0], b_ada.reshape(4, 1, 1536))
    mod6 = mod4.reshape(6, D)
    cw_dev = gat[:, 0, D:].reshape(4, 2, CONVK, 256)[:, 0]
    conv_full = cw_dev.transpose(1, 0, 2).reshape(CONVK, CONVC)

    w_in_b = w_in[0].astype(BF16)
    s_i, r_i, bufs, tok = _split_start("wgather_in_ici_start", [w_in_b, _hbm_empty((4,) + w_in_b.shape, BF16)], 3,
                                       _plan_gather_ici([None]), tok)
    inv_freq = (10000.0 ** (-jnp.arange(32, dtype=F32) / 32))
    cos, sin_s = _rope_tables(positions, inv_freq.reshape(32, 1), min(512, T), tok)
    late = [w_out[0].astype(BF16), w_gate_up[0].astype(BF16), w_down[0].astype(BF16)]
    bufs = _split_wait("wgather_in_ici_wait", s_i, r_i, bufs, [cos] + late, _plan_gather_ici([None]))
    own_in = bufs[0]
    bufs, tok = _copies_now("wgather_in_fwd", bufs[1:], 3, _plan_gather_fwd([None], [D]))
    g_in = _put_slot(bufs[0], own_in, my_s)
    w_pad = jnp.concatenate([g_in[0], g_in[1], g_in[2], g_in[3], jnp.zeros((D, IN_PAD - IN_PROJ), BF16)], axis=1)

    lands = [_hbm_empty((4, D // 4, D), BF16), _hbm_empty((D, 2 * DFF), BF16), _hbm_empty((4, DFF // 4, D), BF16)]
    cols3, rows3 = [None, GU_SH, None], [D // 4, D, DFF // 4]
    s_a, r_a, bufs, tok = _split_start("wgather_ici_start", late + lands, 9, _plan_gather_ici(cols3), tok)

    qkv, z, xbc, dtr, h1b = _in_proj_fwd(xs, cos, sin_s, mod6, norm1_w, w_pad, min(512, T), tok)
    sinks = attn_sinks
    attn, lse = _attn_fwd(qkv, sinks)
    bufs = _split_wait("wgather_ici_wait", s_a, r_a, bufs, attn, _plan_gather_ici(cols3))
    late = bufs[:3]
    s_b, r_b, lands, tok = _split_start("wgather_fwd_start", bufs[3:], 9, _plan_gather_fwd(cols3, rows3), attn)
    dtb = _pad_lanes(dt_bias, 128)
    alog = _pad_lanes(a_log, 128)
    dskx = jnp.repeat(d_skip, HD, axis=1)
    mats = _ssd_mats()
    ynorm, ypre, states, conv_pre = _ssd_fwd(xbc, z, dtr, conv_full, conv_b, dtb, alog, dskx, ssm_norm_w, mats, tok)
    lands = _split_wait("wgather_fwd_wait", s_b, r_b, lands, ynorm, _plan_gather_fwd(cols3, rows3))
    w_out_f = _put_slot(lands[0], late[0], my_s).reshape(D, D)
    w_dn_f = _put_slot(lands[2], late[2], my_s).reshape(DFF, D)
    s_arr = my_s.reshape(1).astype(jnp.int32)

    fw2 = final_norm_w.reshape(1, D)
    sq, dmix, dx1, h2b, act, dfb, dgu, dob, sm_ffn = _mix_ffn(
        xs, attn, ynorm, tgt, mod6, norm2_w, fw2, w_out_f, lands[1], late[1], s_arr, w_dn_f, tm)

    tt = min(2048, T)
    c_arr = ci.reshape(1).astype(jnp.int32)
    tok0 = jnp.zeros((8, 128), F32)
    gw_dn4 = _tn_matmul(act, dfb, GU_SH, D, tt, "dw_down", tok0).reshape(4, DFF // 4, D)
    gw_gu4 = _tn_matmul(h2b, dgu, D, GU_SH, tt, "dw_gate_up", tok0)
    gw_out4 = _tn_matmul_rows(attn, ynorm, dob, tt, "dw_out", tok0).reshape(4, D // 4, D)
    big1 = [gw_out4, gw_gu4, gw_dn4]
    rbs1 = [128, 512, 352]
    sib1 = [_hbm_empty((4, g.shape[1] // 2, g.shape[2]), F32) for g in big1]
    s_c, r_c, bufs, tok = _split_start("gswap_start", big1 + sib1, 3, _plan_swap(3), tok0)

    dzxd, d_cw, d_cb, d_sw, d_sk, d_dtb, d_av = _ssd_bwd(
        xbc, conv_pre, z, dtr, ypre, states, dmix, conv_full, dtb, alog, dskx, ssm_norm_w, mats, tok)
    bufs = _split_wait("gswap_wait", s_c, r_c, bufs, dzxd, _plan_swap(3))
    sums1 = [_add_half(g, s, c_arr, rb, "grad_add_%d" % i)
             for i, (g, s, rb) in enumerate(zip(bufs[:3], bufs[3:], rbs1))]
    land1 = [_hbm_empty(p.shape, BF16) for p in sums1]
    s_d, r_d, bufs, tok = _split_start("gscatter_start", sums1 + land1, 9, _plan_scatter(3), tok0)
    dqkv, d_sinks = _attn_bwd(qkv, sinks, lse, dmix, cos, sin_s, tok)
    bufs = _split_wait("gscatter_wait", s_d, r_d, bufs, dqkv, _plan_scatter(3))
    halves1 = [_sum4(p, l, s_arr, rb, "grad_sum_%d" % i)
               for i, (p, l, rb) in enumerate(zip(bufs[:3], bufs[3:], rbs1))]
    full1 = [_hbm_empty((2 * h.shape[0], h.shape[1]), F32) for h in halves1]
    s_e, r_e, bufs, tok = _split_start("gjoin_start", halves1 + full1, 3, _plan_join(3), tok0)
    gq, gz = _tn_matmul_cols(h1b, dqkv, dzxd, tt, "dw_in", tok)
    joined1 = _split_wait("gjoin_wait", s_e, r_e, bufs, [gq, gz], _plan_join(3))

    sibs = [_hbm_empty((D // 2, g.shape[1]), F32) for g in (gq, gz)]
    s_f, r_f, bufs, tok = _split_start("gswap_in_start", [gq, gz] + sibs, 2, _plan_swap_rows(2), tok0)
    g_dn_s, d_dn, m_dn, v_dn = _adam_2d(w_down[0], joined1[2], joined1[5], m_w_down[0], v_w_down[0], c_arr, 352,
                                        "adam_w_down", tok)
    g_gu_s, d_gu, m_gu, v_gu = _adam_2d(w_gate_up[0], joined1[1], joined1[4], m_w_gate_up[0], v_w_gate_up[0], c_arr,
                                        256, "adam_w_gate_up", tok)
    g_out_s, d_out, m_out, v_out = _adam_2d(w_out[0], joined1[0], joined1[3], m_w_out[0], v_w_out[0], c_arr, 128,
                                            "adam_w_out", tok)
    bufs = _split_wait("gswap_in_wait", s_f, r_f, bufs, [d_dn, d_gu, d_out], _plan_swap_rows(2))
    sum0 = _add_half_in(bufs[0], bufs[1], bufs[2], bufs[3], c_arr, min(256, D // 2))
    s_g, r_g, bufs, tok = _split_start("gscatter_in_start", [sum0, _hbm_empty(sum0.shape, BF16)], 3, _plan_scatter(1),
                                       tok0)
    grad_x, sm_in = _in_proj_bwd(xs, dx1, dqkv, dzxd, mod6, norm1_w, w_pad, min(512, T), tok)

    a_neg = -jnp.exp(alog)
    pieces = [sm_in[1:2], sm_in[2:3], sm_ffn[5:6], sm_ffn[2:3], sm_ffn[3:4], sm_ffn[4:5],
              sm_in[0:1], sm_ffn[1:2], sm_ffn[0:1], d_cb, d_cw.reshape(1, CONVK * CONVC),
              _pad_lanes(d_sw, SW), d_dtb, d_av * a_neg, d_sk, d_sinks,
              _pad_lanes((0.5 / D * jnp.sum(sq)).reshape(1, 1), 128)]
    vec = jnp.concatenate(pieces, axis=1)
    s_h, r_h, rows8, tok_small = _split_start("small_start", [vec, _hbm_empty((8,) + vec.shape, F32)], 7,
                                              _plan_small(), tok0)

    bufs = _split_wait("gscatter_in_wait", s_g, r_g, bufs, [grad_x, tok_small], _plan_scatter(1))
    half0 = _sum4(bufs[0], bufs[1], s_arr, 512, "grad_sum_in")
    joined0, _ = _copies_now("gjoin_in", [half0, _hbm_empty((D, IN_SH), F32)], 1, _plan_join(1))
    native = lambda a: a.transpose(2, 0, 1)
    adam_in = _adam_w_in(native(w_in), joined0[0], joined0[1], native(m_w_in), native(v_w_in), c_arr)
    g_in_s, d_in, m_in, v_in = [a.transpose(1, 2, 0) for a in adam_in]
    rows8 = _split_wait("small_wait", s_h, r_h, rows8, [adam_in[1]], _plan_small())
    tot, allv = _small_sum(rows8[0], rows8[1], (4 * xi + 2 * yi + ci).reshape(1).astype(jnp.int32))
    o = 0
    offs = []
    for p in pieces:
        offs.append(o)
        o += p.shape[1]
    seg = lambda i, n: (offs[i], n)
    g_conv_w = lax.dynamic_slice_in_dim(
        tot[:, offs[10]:offs[10] + CONVK * CONVC].reshape(CONVK, CONVC), my_s * 256, 256, axis=1)
    loss = tot[0, offs[16]]

    small_names = ["b_ada", "norm1_w", "conv_w", "conv_b", "dt_bias", "a_log", "d_skip", "attn_sinks", "ssm_norm_w",
                   "norm2_w", "final_norm_w"]
    small_g = [(0, 6 * D), seg(6, D), g_conv_w, seg(9, D), seg(12, 8), seg(13, 8), seg(14, 8), seg(15, 8),
               seg(11, SW), seg(7, D), seg(8, D)]
    as2d = lambda a: a.reshape(-1, a.shape[-1])
    small_w = [as2d(a) for a in (b_ada, norm1_w, conv_w, conv_b, dt_bias, a_log, d_skip, attn_sinks, ssm_norm_w,
                                 norm2_w, final_norm_w)]
    small_m = [as2d(a) for a in (m_b_ada, m_norm1_w, m_conv_w, m_conv_b, m_dt_bias, m_a_log, m_d_skip, m_attn_sinks,
                                 m_ssm_norm_w, m_norm2_w, m_final_norm_w)]
    small_v = [as2d(a) for a in (v_b_ada, v_norm1_w, v_conv_w, v_conv_b, v_dt_bias, v_a_log, v_d_skip, v_attn_sinks,
                                 v_ssm_norm_w, v_norm2_w, v_final_norm_w)]
    small_g, sd, smn, svn = _adam_small(tot, small_g, small_w, small_m, small_v)
    g_ada, d_ada, m_ada, v_ada = _adam_w_ada(gat, allv, s_arr, w_ada[0], m_w_ada[0], v_w_ada[0], 256)

    order = ["w_ada", "b_ada", "norm1_w", "w_in", "conv_w", "conv_b", "dt_bias", "a_log", "d_skip", "attn_sinks",
             "ssm_norm_w", "w_out", "norm2_w", "w_gate_up", "w_down", "final_norm_w"]
    shapes = dict(w_ada=w_ada.shape, b_ada=b_ada.shape, norm1_w=norm1_w.shape, w_in=w_in.shape, conv_w=conv_w.shape,
                  conv_b=conv_b.shape, dt_bias=dt_bias.shape, a_log=a_log.shape, d_skip=d_skip.shape,
                  attn_sinks=attn_sinks.shape, ssm_norm_w=ssm_norm_w.shape, w_out=w_out.shape, norm2_w=norm2_w.shape,
                  w_gate_up=w_gate_up.shape, w_down=w_down.shape, final_norm_w=final_norm_w.shape)
    grads = dict(w_ada=g_ada, w_in=g_in_s, w_out=g_out_s, w_gate_up=g_gu_s, w_down=g_dn_s)
    deltas = dict(w_ada=d_ada, w_in=d_in, w_out=d_out, w_gate_up=d_gu, w_down=d_dn)
    new_m = dict(w_ada=m_ada, w_in=m_in, w_out=m_out, w_gate_up=m_gu, w_down=m_dn)
    new_v = dict(w_ada=v_ada, w_in=v_in, w_out=v_out, w_gate_up=v_gu, w_down=v_dn)
    for i, nme in enumerate(small_names):
        grads[nme], deltas[nme], new_m[nme], new_v[nme] = small_g[i], sd[i], smn[i], svn[i]
    outs = [loss, grad_x[None]]
    for table in (grads, deltas, new_m, new_v):
        outs += [table[nme].reshape(shapes[nme]) for nme in order]
    return tuple(outs)
```

```python
import functools
import math

import jax
import jax.numpy as jnp
from jax import lax
from jax.experimental import pallas as pl
from jax.experimental.pallas import tpu as pltpu

F32 = jnp.float32
BF16 = jnp.bfloat16
HI = lax.Precision.HIGHEST
MESH = pl.DeviceIdType.MESH

D = 1024
HD = 64
AW = 512
SW = 512
NST = 128
CONVK = 4
CONVC = 1024
BLK = 128
CPS = 4
SSD_FWD_CPS = 8
ATTN_BPS = 8
IN_PROJ = 2312
IN_PAD = 2432
IN_SH = IN_PROJ // 4
DFF = 2816
GU_SH = 1408
FF_SPLITS = ((0, 1536), (1536, 2816))
EPS = 1e-6
NEG = -1e30
LR, B1, B2, AEPS, WD, STEP = 0.001, 0.9, 0.999, 1e-08, 0.01, 10
VMEM_LIMIT = 58 * 1024 * 1024


def _cp(*sem):
    return pltpu.CompilerParams(dimension_semantics=sem or None, vmem_limit_bytes=VMEM_LIMIT)


def _dot(a, b):
    return jnp.dot(a, b, preferred_element_type=F32)


def _dot_nt(a, b):
    return lax.dot_general(a, b, (((1,), (1,)), ((), ())), preferred_element_type=F32)


def _dot_tn(a, b):
    return lax.dot_general(a, b, (((0,), (0,)), ((), ())), preferred_element_type=F32)


def _dot_hi(a, b):
    return jnp.dot(a, b, precision=HI, preferred_element_type=F32)


def _sigmoid(x):
    return 1.0 / (1.0 + jnp.exp(-x))


def _iota(shape, dim):
    return lax.broadcasted_iota(jnp.int32, shape, dim)


def _load_resident(hbm_ref, vmem_ref, sem):
    @pl.when(pl.program_id(0) == 0)
    def _():
        cp = pltpu.make_async_copy(hbm_ref, vmem_ref, sem)
        cp.start()
        cp.wait()


def _swap32(t):
    lane = _iota(t.shape, 1)
    return jnp.where((lane & 63) < 32, pltpu.roll(t, 96, 1), pltpu.roll(t, 32, 1))


def _rope_fwd(t, cos, sin_s):
    return t * cos + _swap32(t) * sin_s


def _rope_bwd(t, cos, sin_s):
    return t * cos - _swap32(t) * sin_s


DEP_SPEC = pl.BlockSpec((8, 128), lambda *_: (0, 0))


def _rope_tables(pos_row, inv_freq_col, tm, dep):
    T = pos_row.shape[1]
    lane, row = jnp.arange(128)[None, :], jnp.arange(96)[:, None]
    pick = (lane % 32) == (row % 32)
    sel_cos = pick.astype(BF16)
    sel_sin = jnp.where(pick, jnp.where(lane % 64 < 32, -1.0, 1.0), 0.0).astype(BF16)

    def body(p_ref, f_ref, sc_ref, ss_ref, dep_ref, cos_ref, sin_ref):
        ang = f_ref[...] * p_ref[...].astype(F32)
        cos_ref[...] = _dot_tn(_pieces(jnp.cos(ang), 3, 0), sc_ref[...])
        sin_ref[...] = _dot_tn(_pieces(jnp.sin(ang), 3, 0), ss_ref[...])

    full = lambda a: pl.BlockSpec(a.shape, lambda i: (0,) * a.ndim)
    return pl.pallas_call(
        body, name="rope_tables", grid=(T // tm,),
        in_specs=[pl.BlockSpec((1, tm), lambda i: (0, i)), full(inv_freq_col), full(sel_cos), full(sel_sin), DEP_SPEC],
        out_specs=[pl.BlockSpec((tm, 128), lambda i: (i, 0))] * 2,
        out_shape=[jax.ShapeDtypeStruct((T, 128), F32)] * 2,
        compiler_params=_cp("parallel"),
    )(pos_row, inv_freq_col, sel_cos, sel_sin, dep)


def _in_proj_fwd(x, cos, sin_s, mod6, norm1_w, w_pad, tm, dep):
    T = x.shape[0]

    def body(x_ref, cos_ref, sin_ref, mod_ref, nw_ref, w_hbm, dep_ref, qkv_ref, z_ref, xbc_ref, dt_ref, h_ref, w_vmem,
             sem):
        _load_resident(w_hbm, w_vmem, sem)
        xv = x_ref[...]
        r = lax.rsqrt(jnp.mean(xv * xv, axis=-1, keepdims=True) + EPS)
        h = (xv * r * nw_ref[...]) * (1.0 + mod_ref[1:2, :]) + mod_ref[0:1, :]
        hb = h.astype(BF16)
        h_ref[...] = hb
        proj = _dot(hb, w_vmem[...])
        cs, sn = cos_ref[...], sin_ref[...]
        for j in range(5):
            qkv_ref[:, 128 * j:128 * (j + 1)] = _rope_fwd(proj[:, 128 * j:128 * (j + 1)], cs, sn).astype(BF16)
        qkv_ref[:, 640:768] = proj[:, 640:768].astype(BF16)
        z_ref[...] = proj[:, 768:1280]
        xbc_ref[...] = proj[:, 1280:2304]
        dt_ref[...] = proj[:, 2304:2432]

    row = lambda w: pl.BlockSpec((tm, w), lambda i: (i, 0))
    full = lambda a: pl.BlockSpec(a.shape, lambda i: (0,) * a.ndim)
    return pl.pallas_call(
        body, name="in_proj_fwd", grid=(T // tm,),
        in_specs=[row(D), row(128), row(128), full(mod6), full(norm1_w), pl.BlockSpec(memory_space=pl.ANY), DEP_SPEC],
        out_specs=[row(768), row(512), row(1024), row(128), row(D)],
        out_shape=[jax.ShapeDtypeStruct((T, 768), BF16), jax.ShapeDtypeStruct((T, 512), F32),
                   jax.ShapeDtypeStruct((T, 1024), F32), jax.ShapeDtypeStruct((T, 128), F32),
                   jax.ShapeDtypeStruct((T, D), BF16)],
        scratch_shapes=[pltpu.VMEM((D, IN_PAD), BF16), pltpu.SemaphoreType.DMA],
        compiler_params=_cp("arbitrary"),
    )(x, cos, sin_s, mod6, norm1_w, w_pad, dep)


def _head_variants(pair, j):
    lane = _iota(pair.shape, 1)
    lo = lane < 64
    kv = j // 2
    ev = jnp.where(lo, pair, 0.0)
    od = jnp.where(lo, 0.0, pair)
    if kv == 0:
        od = pltpu.roll(od, 64, 1)
    else:
        ev = pltpu.roll(ev, 64, 1)
    return ev.astype(BF16), od.astype(BF16)


def _kv_variants(vcat):
    lane = _iota(vcat.shape, 1)
    lo = lane < 64
    v0 = jnp.where(lo, vcat, 0.0)
    v1 = jnp.where(lo, 0.0, vcat)
    out = {
        (0, 0): v0, (0, 1): pltpu.roll(v0, 64, 1),
        (1, 0): pltpu.roll(v1, 64, 1), (1, 1): v1,
    }
    return {k: v.astype(BF16) for k, v in out.items()}


def _fold_masks(n):
    upper = _iota((BLK, BLK), 1) > _iota((BLK, BLK), 0)
    return upper, upper & (n == 0)


def _attn_fwd(qkv, sinks):
    CPS = ATTN_BPS
    T = qkv.shape[0]
    nsteps = T // (CPS * BLK)

    def body(sink_ref, q_ref, kc_ref, kp_ref, vc_ref, vp_ref, o_ref, lse_ref):
        for sub in range(CPS):
            rows, before = slice(BLK * sub, BLK * (sub + 1)), slice(BLK * (sub - 1), BLK * sub)
            block(pl.program_id(0) * CPS + sub, sink_ref, q_ref.at[rows, :], kc_ref.at[rows, :],
                  kp_ref if sub == 0 else kc_ref.at[before, :], vc_ref.at[rows, :],
                  vp_ref if sub == 0 else vc_ref.at[before, :], o_ref.at[rows, :], lse_ref.at[rows, :])

    def block(n, sink_ref, q_ref, kc_ref, kp_ref, vc_ref, vp_ref, o_ref, lse_ref):
        vpv = _kv_variants(vp_ref[...].astype(F32))
        vcv = _kv_variants(vc_ref[...].astype(F32))
        q_all = jnp.concatenate(
            [v for j in range(4) for v in _head_variants(q_ref[:, 128 * j:128 * (j + 1)].astype(F32), j)], axis=0)
        s_prev = _dot_nt(q_all, kp_ref[...])
        s_cur = _dot_nt(q_all, kc_ref[...])
        upper, dead = _fold_masks(n)
        lane = _iota((BLK, 128), 1)
        lse_acc = jnp.zeros((BLK, 128), F32)
        for jj in range(4):
            acc = jnp.zeros((BLK, 128), F32)
            for par in range(2):
                h = 2 * jj + par
                rows = slice(h * BLK, (h + 1) * BLK)
                sink = sink_ref[0, h]
                s = jnp.where(dead, NEG, jnp.where(upper, s_prev[rows], s_cur[rows]) * 0.125)
                m = jnp.maximum(jnp.max(s, axis=1, keepdims=True), sink)
                p = jnp.exp(s - m)
                den = jnp.sum(p, axis=1, keepdims=True) + jnp.exp(sink - m)
                pn = p * (1.0 / den)
                acc = (acc + _dot(jnp.where(upper, pn, 0.0).astype(BF16), vpv[(jj // 2, par)])
                       + _dot(jnp.where(upper, 0.0, pn).astype(BF16), vcv[(jj // 2, par)]))
                lse_acc = jnp.where(lane == h, m + jnp.log(den), lse_acc)
            o_ref[:, 128 * jj:128 * (jj + 1)] = acc.astype(BF16)
        lse_ref[...] = lse_acc

    RB = CPS * BLK
    prev = lambda n: jnp.maximum(n * CPS - 1, 0)
    return pl.pallas_call(
        body, name="attn_fwd", grid=(nsteps,),
        in_specs=[pl.BlockSpec(memory_space=pltpu.SMEM),
                  pl.BlockSpec((RB, 512), lambda n: (n, 0)),
                  pl.BlockSpec((RB, 128), lambda n: (n, 4)),
                  pl.BlockSpec((BLK, 128), lambda n: (prev(n), 4)),
                  pl.BlockSpec((RB, 128), lambda n: (n, 5)),
                  pl.BlockSpec((BLK, 128), lambda n: (prev(n), 5))],
        out_specs=[pl.BlockSpec((RB, 512), lambda n: (n, 0)), pl.BlockSpec((RB, 128), lambda n: (n, 0))],
        out_shape=[jax.ShapeDtypeStruct((T, 512), BF16), jax.ShapeDtypeStruct((T, 128), F32)],
        compiler_params=_cp("parallel"),
    )(sinks, qkv, qkv, qkv, qkv, qkv)


def _attn_bwd(qkv, sinks, lse, dmix, cos, sin_s, dep):
    T = qkv.shape[0]
    nb = T // BLK

    def body(sink_ref, q_ref, kc_ref, kp_ref, vc_ref, vp_ref, lse_ref, do_ref, cq_ref, sq_ref, ck_ref, sk_ref,
             dep_ref, out_ref, ds_ref, dq_car, dk_car, dv_car):
        n = pl.program_id(0)
        lane = _iota((BLK, 128), 1)

        @pl.when(n == 0)
        def _():
            ds_ref[...] = jnp.zeros_like(ds_ref)
            dq_car[...] = jnp.zeros_like(dq_car)
            dk_car[...] = jnp.zeros_like(dk_car)
            dv_car[...] = jnp.zeros_like(dv_car)

        @pl.when(n < nb)
        def _():
            kp, kc, vp, vc = kp_ref[...], kc_ref[...], vp_ref[...], vc_ref[...]
            kpv = _kv_variants(kp.astype(F32))
            kcv = _kv_variants(kc.astype(F32))
            lse_v = lse_ref[...]
            q_all = jnp.concatenate(
                [v for j in range(4) for v in _head_variants(q_ref[:, 128 * j:128 * (j + 1)].astype(F32), j)], axis=0)
            do_all = jnp.concatenate(
                [v for j in range(4) for v in _head_variants(do_ref[:, 128 * j:128 * (j + 1)], j)], axis=0)
            s_prev, s_cur = _dot_nt(q_all, kp), _dot_nt(q_all, kc)
            dp_prev, dp_cur = _dot_nt(do_all, vp), _dot_nt(do_all, vc)
            upper, dead = _fold_masks(n)
            out_ref[:, 0:512] = dq_car[...]
            dsk = jnp.zeros((1, 128), F32)
            ds_u, ds_l, p_u, p_l = [], [], [], []
            for jj in range(4):
                dq_acc = jnp.zeros((BLK, 128), F32)
                for par in range(2):
                    h = 2 * jj + par
                    rows = slice(h * BLK, (h + 1) * BLK)
                    lse_h = jnp.sum(jnp.where(lane == h, lse_v, 0.0), axis=1, keepdims=True)
                    s = jnp.where(dead, NEG, jnp.where(upper, s_prev[rows], s_cur[rows]) * 0.125)
                    p = jnp.exp(s - lse_h)
                    dp = jnp.where(upper, dp_prev[rows], dp_cur[rows])
                    delta = jnp.sum(p * dp, axis=1, keepdims=True)
                    ds = p * (dp - delta) * 0.125
                    dsu, dsl = jnp.where(upper, ds, 0.0).astype(BF16), jnp.where(upper, 0.0, ds).astype(BF16)
                    dq_acc = dq_acc + _dot(dsu, kpv[(jj // 2, par)]) + _dot(dsl, kcv[(jj // 2, par)])
                    ds_u.append(dsu)
                    ds_l.append(dsl)
                    p_u.append(jnp.where(upper, p, 0.0).astype(BF16))
                    p_l.append(jnp.where(upper, 0.0, p).astype(BF16))
                    dsk = dsk + jnp.where(lane[0:1] == h, -jnp.sum(jnp.exp(sink_ref[0, h] - lse_h) * delta), 0.0)
                dq_car[:, 128 * jj:128 * (jj + 1)] = _rope_bwd(dq_acc, cq_ref[...], sq_ref[...]).astype(BF16)
            stack = lambda parts: jnp.concatenate(parts, axis=0)
            dk_prev, dk_cur = _dot_tn(stack(ds_u), q_all), _dot_tn(stack(ds_l), q_all)
            dv_prev, dv_cur = _dot_tn(stack(p_u), do_all), _dot_tn(stack(p_l), do_all)
            ds_ref[...] += dsk
            out_ref[:, 512:640] = _rope_bwd(dk_car[...] + dk_prev, ck_ref[...], sk_ref[...]).astype(BF16)
            out_ref[:, 640:768] = (dv_car[...] + dv_prev).astype(BF16)
            dk_car[...] = dk_cur
            dv_car[...] = dv_cur

        @pl.when(n == nb)
        def _():
            out_ref[:, 0:512] = dq_car[...]
            out_ref[:, 512:640] = _rope_bwd(dk_car[...], ck_ref[...], sk_ref[...]).astype(BF16)
            out_ref[:, 640:768] = dv_car[...].astype(BF16)

    cur = lambda n: jnp.minimum(n, nb - 1)
    prev = lambda n: jnp.maximum(cur(n) - 1, 0)
    outb = lambda n: jnp.maximum(n - 1, 0)
    return pl.pallas_call(
        body, name="attn_bwd", grid=(nb + 1,),
        in_specs=[pl.BlockSpec(memory_space=pltpu.SMEM),
                  pl.BlockSpec((BLK, 512), lambda n: (cur(n), 0)),
                  pl.BlockSpec((BLK, 128), lambda n: (cur(n), 4)),
                  pl.BlockSpec((BLK, 128), lambda n: (prev(n), 4)),
                  pl.BlockSpec((BLK, 128), lambda n: (cur(n), 5)),
                  pl.BlockSpec((BLK, 128), lambda n: (prev(n), 5)),
                  pl.BlockSpec((BLK, 128), lambda n: (cur(n), 0)),
                  pl.BlockSpec((BLK, 512), lambda n: (cur(n), 0)),
                  pl.BlockSpec((BLK, 128), lambda n: (cur(n), 0)),
                  pl.BlockSpec((BLK, 128), lambda n: (cur(n), 0)),
                  pl.BlockSpec((BLK, 128), lambda n: (outb(n), 0)),
                  pl.BlockSpec((BLK, 128), lambda n: (outb(n), 0)), DEP_SPEC],
        out_specs=[pl.BlockSpec((BLK, 768), lambda n: (outb(n), 0)), pl.BlockSpec((1, 128), lambda n: (0, 0))],
        out_shape=[jax.ShapeDtypeStruct((T, 768), BF16), jax.ShapeDtypeStruct((1, 128), F32)],
        scratch_shapes=[pltpu.VMEM((BLK, 512), BF16), pltpu.VMEM((BLK, 128), F32), pltpu.VMEM((BLK, 128), F32)],
        compiler_params=_cp("arbitrary"),
    )(sinks, qkv, qkv, qkv, qkv, qkv, lse, dmix, cos, sin_s, cos, sin_s, dep)


def _ssd_mats():
    e = jnp.arange(SW)[None, :] // HD == jnp.arange(128)[:, None]
    tri = jnp.arange(BLK)[None, :] <= jnp.arange(BLK)[:, None]
    return (jnp.tile(e, (3, 1)).astype(BF16), jnp.tile(e.T, (2, 1)).astype(BF16),
            jnp.tile(tri, (1, 3)).astype(BF16), jnp.tile(tri.T, (1, 3)).astype(BF16))


def _pieces(x, n, axis):
    out, r = [], x
    for i in range(n):
        p = r.astype(BF16)
        out.append(p)
        if i + 1 < n:
            r = r - p.astype(F32)
    return jnp.concatenate(out, axis=axis)


def _expand(x, e3):
    return _dot(_pieces(x, 3, 1), e3)


def _head_sums(x, et2):
    return _dot(_pieces(x, 2, 1), et2)


def _run_sum(tri3, x):
    return _dot(tri3, _pieces(x, 3, 0))


def _shift_down(u, tail, j):
    rolled = pltpu.roll(u, j, 0)
    first = jnp.where(_iota(tail.shape, 0) < j, pltpu.roll(tail, j, 0), rolled[0:8])
    return jnp.concatenate([first, rolled[8:]], axis=0)


def _shift_up(d, head, j):
    rolled = pltpu.roll(d, BLK - j, 0)
    last = jnp.where(_iota(head.shape, 0) >= 8 - j, pltpu.roll(head, 8 - j, 0), rolled[BLK - 8:])
    return jnp.concatenate([rolled[:BLK - 8], last], axis=0)


def _ssd_parts(dtr, dtb, alog, e3, tril3):
    xx = dtr + dtb
    dt = jnp.maximum(xx, 0.0) + jnp.log(1.0 + jnp.exp(-jnp.abs(xx)))
    a_neg = -jnp.exp(alog)
    tril = _iota((BLK, BLK), 1) <= _iota((BLK, BLK), 0)
    cs = _run_sum(tril3, dt * a_neg)
    csx = _expand(cs, e3)
    last = csx[BLK - 1:BLK, :]
    return dict(xx=xx, dt=dt, a_neg=a_neg, tril=tril, cs=cs, cs_t=cs.T,
                ecsx=jnp.exp(csx), dtex=jnp.exp(last - csx), cdx=jnp.exp(last), dtx=_expand(dt, e3))


def _decay(parts, h):
    seg = parts["cs"][:, h:h + 1] - parts["cs_t"][h:h + 1, :]
    return jnp.exp(jnp.where(parts["tril"], seg, NEG))


def _group_cols(a, g):
    return a[:, 256 * g:256 * (g + 1)]


def _ssd_fwd(xbc, z, dtr, conv_w, conv_b, dtb, alog, dskx, ssm_w, mats, dep):
    CPS = SSD_FWD_CPS
    T = xbc.shape[0]
    nc = T // BLK

    def body(u_ref, tail_ref, z_ref, dtr_ref, cw_ref, cb_ref, dtb_ref, al_ref, dk_ref, sw_ref, e3_ref, tril3_ref,
             dep_ref, yn_ref, yp_ref, st_ref, co_ref, s_scr):
        n = pl.program_id(0)

        @pl.when(n == 0)
        def _():
            s_scr[...] = jnp.zeros_like(s_scr)

        lane = _iota((BLK, 128), 1)
        lo = lane < 64
        for sub in range(CPS):
            rows = slice(BLK * sub, BLK * (sub + 1))
            u = u_ref[rows, :]
            tail = jnp.where(n > 0, tail_ref[...], 0.0) if sub == 0 else u_ref[BLK * sub - 8:BLK * sub, :]
            co = cb_ref[...] + cw_ref[3:4, :] * u
            for j in range(1, CONVK):
                co = co + cw_ref[3 - j:4 - j, :] * _shift_down(u, tail, j)
            co_ref[rows, :] = co
            xc = co * _sigmoid(co)
            pt = _ssd_parts(dtr_ref[rows, :], dtb_ref[...], al_ref[...], e3_ref[...], tril3_ref[...])
            xs = xc[:, :SW]
            bm = [xc[:, 512:640].astype(BF16), xc[:, 640:768].astype(BF16)]
            cm = [xc[:, 768:896].astype(BF16), xc[:, 896:1024].astype(BF16)]
            s_in = s_scr[...]
            st_ref[sub] = s_in
            xdt = xs * pt["dtx"]
            xde = (xdt * pt["dtex"]).astype(BF16)
            ys, s_new = [], []
            for g in range(2):
                cb = _dot_nt(cm[g], bm[g])
                yoff = _dot(cm[g], _group_cols(s_in, g).astype(BF16))
                s_new.append(_dot_tn(bm[g], _group_cols(xde, g)))
                for jj in range(2):
                    j = 2 * g + jj
                    chunk = xdt[:, 128 * j:128 * (j + 1)]
                    g_ev = (cb * _decay(pt, 2 * j)).astype(BF16)
                    g_od = (cb * _decay(pt, 2 * j + 1)).astype(BF16)
                    yd = (_dot(g_ev, jnp.where(lo, chunk, 0.0).astype(BF16))
                          + _dot(g_od, jnp.where(lo, 0.0, chunk).astype(BF16)))
                    ys.append(yd + yoff[:, 128 * jj:128 * (jj + 1)] * pt["ecsx"][:, 128 * j:128 * (j + 1)])
            y = jnp.concatenate(ys, axis=1) + xs * dk_ref[...]
            s_scr[...] = s_in * pt["cdx"] + jnp.concatenate(s_new, axis=1)
            yp_ref[rows, :] = y
            zv = z_ref[rows, :]
            yz = y * (zv * _sigmoid(zv))
            outs = []
            for g in range(2):
                yg = _group_cols(yz, g)
                outs.append(yg * lax.rsqrt(jnp.mean(yg * yg, axis=-1, keepdims=True) + EPS))
            yn_ref[rows, :] = (jnp.concatenate(outs, axis=1) * sw_ref[...]).astype(BF16)

    e3, _, tril3, _ = mats
    RB = CPS * BLK
    tail8 = lambda n: jnp.maximum(n * (RB // 8) - 1, 0)
    full = lambda a: pl.BlockSpec(a.shape, lambda n: (0,) * a.ndim)
    return pl.pallas_call(
        body, name="ssd_fwd", grid=(nc // CPS,),
        in_specs=[pl.BlockSpec((RB, CONVC), lambda n: (n, 0)), pl.BlockSpec((8, CONVC), lambda n: (tail8(n), 0)),
                  pl.BlockSpec((RB, SW), lambda n: (n, 0)), pl.BlockSpec((RB, 128), lambda n: (n, 0)),
                  full(conv_w), full(conv_b), full(dtb), full(alog), full(dskx), full(ssm_w), full(e3), full(tril3),
                  DEP_SPEC],
        out_specs=[pl.BlockSpec((RB, SW), lambda n: (n, 0)), pl.BlockSpec((RB, SW), lambda n: (n, 0)),
                   pl.BlockSpec((CPS, NST, SW), lambda n: (n, 0, 0)), pl.BlockSpec((RB, CONVC), lambda n: (n, 0))],
        out_shape=[jax.ShapeDtypeStruct((T, SW), BF16), jax.ShapeDtypeStruct((T, SW), F32),
                   jax.ShapeDtypeStruct((nc, NST, SW), F32), jax.ShapeDtypeStruct((T, CONVC), F32)],
        scratch_shapes=[pltpu.VMEM((NST, SW), F32)],
        compiler_params=_cp("arbitrary"),
    )(xbc, xbc, z, dtr, conv_w, conv_b, dtb, alog, dskx, ssm_w, e3, tril3, dep)


def _ssd_bwd(xbc, co_all, z, dtr, ypre, states, dmix, conv_w, dtb, alog, dskx, ssm_w, mats, dep):
    T = xbc.shape[0]
    nsteps = T // (CPS * BLK)

    def body(*refs):
        per_chunk, consts, out_ref, carried = refs[:7], refs[7:16], refs[17], refs[18:]
        i = pl.program_id(0)

        @pl.when(i == 0)
        def _():
            for r in carried:
                r[...] = jnp.zeros_like(r)

        for sub in reversed(range(CPS)):
            rows = slice(BLK * sub, BLK * (sub + 1))
            views = [r.at[sub:sub + 1] if k == 5 else r.at[rows, :] for k, r in enumerate(per_chunk)]
            chunk(*views, *consts, out_ref.at[rows, :], *carried)

        @pl.when(i == nsteps - 1)
        def _():
            dsk_ref, dskx_scr = carried[3], carried[8]
            dsk_ref[...] = _head_sums(jnp.broadcast_to(dskx_scr[...], (8, SW)), consts[6][...])[0:1]

    def chunk(u_ref, co_ref, z_ref, dtr_ref, yp_ref, st_ref, dyn_ref, cw_ref, dtb_ref, al_ref, dk_ref, sw_ref,
              e3_ref, et2_ref, tril3_ref, triu3_ref,
              out_ref, dcw_ref, dcb_ref, dsw_ref, dsk_ref, ddtb_ref, dav_ref, ds_scr, dco_scr, dskx_scr):
        co = co_ref[...]
        sg = _sigmoid(co)
        xc = co * sg
        pt = _ssd_parts(dtr_ref[...], dtb_ref[...], al_ref[...], e3_ref[...], tril3_ref[...])
        dtx, ecsx, dtex, cdx = pt["dtx"], pt["ecsx"], pt["dtex"], pt["cdx"]
        xs = xc[:, :SW]
        bm = [xc[:, 512:640].astype(BF16), xc[:, 640:768].astype(BF16)]
        cm = [xc[:, 768:896].astype(BF16), xc[:, 896:1024].astype(BF16)]
        s_in = st_ref[0]
        ds_out = ds_scr[...]
        e_t = et2_ref[...]

        zv = z_ref[...]
        sz = _sigmoid(zv)
        silu_z = zv * sz
        ypre = yp_ref[...]
        yz = ypre * silu_z
        dyn = dyn_ref[...]
        sw = sw_ref[...]
        dyz, yns = [], []
        for g in range(2):
            yg = _group_cols(yz, g)
            r = lax.rsqrt(jnp.mean(yg * yg, axis=-1, keepdims=True) + EPS)
            yn = yg * r
            dg = _group_cols(dyn, g) * _group_cols(sw, g)
            dyz.append(r * (dg - yn * jnp.mean(dg * yn, axis=-1, keepdims=True)))
            yns.append(yn)
        dyz = jnp.concatenate(dyz, axis=1)
        dsw_ref[...] += jnp.sum(dyn * jnp.concatenate(yns, axis=1), axis=0, keepdims=True)
        dy = dyz * silu_z
        dz = dyz * ypre * (sz * (1.0 + zv * (1.0 - sz)))

        xdt = xs * dtx
        xdt_b = xdt.astype(BF16)
        edy = (ecsx * dy).astype(BF16)
        xde = (xdt * dtex).astype(BF16)
        lane = _iota((BLK, 128), 1)
        lo = lane < 64
        row8 = _iota((8, 128), 0)
        dcs = jnp.zeros((BLK, 128), F32)
        col_rows = jnp.zeros((8, 128), F32)
        dxdt, bds, yoff, dbs, dcs_g, ds_new = [], [], [], [], [], []
        for g in range(2):
            s_g = _group_cols(s_in, g).astype(BF16)
            dso_g = _group_cols(ds_out, g).astype(BF16)
            cb = _dot_nt(cm[g], bm[g])
            bds.append(_dot(bm[g], dso_g))
            yoff.append(_dot(cm[g], s_g))
            dcb_g = jnp.zeros((BLK, BLK), F32)
            for jj in range(2):
                j = 2 * g + jj
                dy_c = dy[:, 128 * j:128 * (j + 1)]
                xdt_c = xdt_b[:, 128 * j:128 * (j + 1)]
                acc = jnp.zeros((BLK, 128), F32)
                for par in range(2):
                    h = 2 * j + par
                    lm = _decay(pt, h)
                    gm = cb * lm
                    dy_m = (jnp.where(lo, dy_c, 0.0) if par == 0 else jnp.where(lo, 0.0, dy_c)).astype(BF16)
                    dg_h = _dot_nt(dy_m, xdt_c)
                    w_h = dg_h * gm
                    dcs = dcs + jnp.where(lane == h, jnp.sum(w_h, axis=1, keepdims=True), 0.0)
                    col_rows = col_rows + jnp.where(row8 == h, jnp.sum(w_h, axis=0, keepdims=True), 0.0)
                    dcb_g = dcb_g + dg_h * lm
                    acc = acc + _dot_tn(gm.astype(BF16), dy_m)
                dxdt.append(acc)
            dcb_b = dcb_g.astype(BF16)
            dcs_g.append(_dot(dcb_b, bm[g]) + _dot_nt(_group_cols(edy, g), s_g))
            dbs.append(_dot_tn(dcb_b, cm[g]) + _dot_nt(_group_cols(xde, g), dso_g))
            ds_new.append(_dot_tn(cm[g], _group_cols(edy, g)))
        bds = jnp.concatenate(bds, axis=1)
        yoff = jnp.concatenate(yoff, axis=1) * ecsx
        dxdt = jnp.concatenate(dxdt, axis=1) + dtex * bds
        ds_scr[...] = cdx * ds_out + jnp.concatenate(ds_new, axis=1)

        t_m = _head_sums(dtex * xdt * bds, e_t)
        colsum_t = jnp.concatenate([col_rows, jnp.zeros((BLK - 8, 128), F32)], axis=0).T
        cd = jnp.exp(pt["cs"][BLK - 1:BLK, :])
        sds = jnp.sum(s_in * ds_out, axis=0, keepdims=True)
        last_row = jnp.sum(t_m, axis=0, keepdims=True) + cd * _head_sums(jnp.broadcast_to(sds, (8, SW)), e_t)[0:1]
        dcs = dcs - colsum_t + _head_sums(dy * yoff, e_t) - t_m
        dcs = dcs + jnp.where(_iota((BLK, 128), 0) == BLK - 1, last_row, 0.0)
        da = _run_sum(triu3_ref[...], dcs)
        dt = pt["dt"]
        ddt = da * pt["a_neg"] + _head_sums(dxdt * xs, e_t)
        dav_ref[...] += jnp.sum(da * dt, axis=0, keepdims=True)
        ddtr = ddt * _sigmoid(pt["xx"])
        ddtb_ref[...] += jnp.sum(ddtr, axis=0, keepdims=True)
        dxs = dxdt * dtx + dy * dk_ref[...]
        dskx_scr[...] += jnp.sum(dy * xs, axis=0, keepdims=True)
        dxc = jnp.concatenate([dxs, dbs[0], dbs[1], dcs_g[0], dcs_g[1]], axis=1)
        dco = dxc * (sg * (1.0 + co * (1.0 - sg)))

        dcb_ref[...] += jnp.sum(dco, axis=0, keepdims=True)
        u = u_ref[...]
        head = dco_scr[...]
        du = jnp.zeros_like(dco)
        for j in range(CONVK):
            up_j = dco if j == 0 else _shift_up(dco, head, j)
            dcw_ref[3 - j:4 - j, :] += jnp.sum(up_j * u, axis=0, keepdims=True)
            du = du + cw_ref[3 - j:4 - j, :] * up_j
        dco_scr[...] = dco[0:8]
        out_ref[:, 0:512] = dz.astype(BF16)
        out_ref[:, 512:1536] = du.astype(BF16)
        out_ref[:, 1536:1664] = ddtr.astype(BF16)

    e3, et2, tril3, triu3 = mats
    RB = CPS * BLK
    rev = lambda i: nsteps - 1 - i
    full = lambda a: pl.BlockSpec(a.shape, lambda i: (0,) * a.ndim)
    acc = lambda r, c: pl.BlockSpec((r, c), lambda i: (0, 0))
    return pl.pallas_call(
        body, name="ssd_bwd", grid=(nsteps,),
        in_specs=[pl.BlockSpec((RB, CONVC), lambda i: (rev(i), 0)), pl.BlockSpec((RB, CONVC), lambda i: (rev(i), 0)),
                  pl.BlockSpec((RB, SW), lambda i: (rev(i), 0)), pl.BlockSpec((RB, 128), lambda i: (rev(i), 0)),
                  pl.BlockSpec((RB, SW), lambda i: (rev(i), 0)), pl.BlockSpec((CPS, NST, SW), lambda i: (rev(i), 0, 0)),
                  pl.BlockSpec((RB, SW), lambda i: (rev(i), 1)),
                  full(conv_w), full(dtb), full(alog), full(dskx), full(ssm_w),
                  full(e3), full(et2), full(tril3), full(triu3), DEP_SPEC],
        out_specs=[pl.BlockSpec((RB, 1664), lambda i: (rev(i), 0)),
                   acc(CONVK, CONVC), acc(1, CONVC), acc(1, SW), acc(1, 128), acc(1, 128), acc(1, 128)],
        out_shape=[jax.ShapeDtypeStruct((T, 1664), BF16),
                   jax.ShapeDtypeStruct((CONVK, CONVC), F32), jax.ShapeDtypeStruct((1, CONVC), F32),
                   jax.ShapeDtypeStruct((1, SW), F32), jax.ShapeDtypeStruct((1, 128), F32),
                   jax.ShapeDtypeStruct((1, 128), F32), jax.ShapeDtypeStruct((1, 128), F32)],
        scratch_shapes=[pltpu.VMEM((NST, SW), F32), pltpu.VMEM((8, CONVC), F32), pltpu.VMEM((1, SW), F32)],
        compiler_params=_cp("arbitrary"),
    )(xbc, co_all, z, dtr, ypre, states, dmix, conv_w, dtb, alog, dskx, ssm_w, e3, et2, tril3, triu3, dep)


def _mix_ffn(x, attn, ynorm, tgt, mod6, norm2_w, final_w, w_out, w_gu, w_gu_own, s_arr, w_dn, tm):
    T = x.shape[0]
    nt = T // tm

    def body(x_ref, a_ref, y_ref, t_ref, mod_ref, n2_ref, fw_ref, wo_hbm, wgu_hbm, own_hbm, s_ref, wdn_hbm,
             sq_ref, dmix_ref, dx1_ref, h2_ref, act_ref, df_ref, dgu_ref, do_ref, sm_ref,
             wo, wgu, wdn, sems):
        i = pl.program_id(0)

        @pl.when(i == 0)
        def _():
            cps = [pltpu.make_async_copy(s, d, sems.at[k]) for k, (s, d) in
                   enumerate(((wo_hbm, wo), (wgu_hbm, wgu), (wdn_hbm, wdn)))]
            for c in cps:
                c.start()
            for c in cps:
                c.wait()
            own = pltpu.make_async_copy(
                own_hbm, wgu.at[:, pl.ds(pl.multiple_of(s_ref[0] * GU_SH, 128), GU_SH)], sems.at[3])
            own.start()
            own.wait()
            sq_ref[...] = jnp.zeros_like(sq_ref)
            sm_ref[...] = jnp.zeros_like(sm_ref)

        gate1, shift2, scale2, gate2 = mod_ref[2:3, :], mod_ref[3:4, :], mod_ref[4:5, :], mod_ref[5:6, :]
        n2w, fw = n2_ref[...], fw_ref[...]
        o = _dot(a_ref[...], wo[0:AW, :]) + _dot(y_ref[...], wo[AW:D, :])
        x1 = x_ref[...] + gate1 * o
        r2 = lax.rsqrt(jnp.mean(x1 * x1, axis=-1, keepdims=True) + EPS)
        xh2 = x1 * r2
        n2 = xh2 * n2w
        h2b = (n2 * (1.0 + scale2) + shift2).astype(BF16)
        h2_ref[...] = h2b
        f = jnp.zeros((tm, D), F32)
        saved = []
        for a, b in FF_SPLITS:
            gp = _dot(h2b, wgu[:, a:b])
            upj = _dot(h2b, wgu[:, DFF + a:DFF + b])
            sg = _sigmoid(gp)
            sl = gp * sg
            actb = (sl * upj).astype(BF16)
            act_ref[:, a:b] = actb
            f = f + _dot(actb, wdn[a:b, :])
            saved.append((gp, upj, sg, sl))
        x2 = x1 + gate2 * f
        r3 = lax.rsqrt(jnp.mean(x2 * x2, axis=-1, keepdims=True) + EPS)
        xh3 = x2 * r3
        err = xh3 * fw - t_ref[...]
        sq_ref[...] += jnp.sum(err * err, axis=0, keepdims=True)
        dy = err * (1.0 / D)
        dfw = jnp.sum(dy * xh3, axis=0, keepdims=True)
        dxh3 = dy * fw
        dx2 = r3 * (dxh3 - xh3 * jnp.mean(dxh3 * xh3, axis=-1, keepdims=True))
        dgate2 = jnp.sum(dx2 * f, axis=0, keepdims=True)
        dfb = (dx2 * gate2).astype(BF16)
        df_ref[...] = dfb
        dh2 = jnp.zeros((tm, D), F32)
        for (a, b), (gp, upj, sg, sl) in zip(FF_SPLITS, saved):
            dact = _dot_nt(dfb, wdn[a:b, :])
            dg = (dact * upj * (sg * (1.0 + gp * (1.0 - sg)))).astype(BF16)
            du = (dact * sl).astype(BF16)
            dgu_ref[:, a:b] = dg
            dgu_ref[:, DFF + a:DFF + b] = du
            dh2 = dh2 + _dot_nt(dg, wgu[:, a:b]) + _dot_nt(du, wgu[:, DFF + a:DFF + b])
        dshift2 = jnp.sum(dh2, axis=0, keepdims=True)
        dscale2 = jnp.sum(dh2 * n2, axis=0, keepdims=True)
        dn2 = dh2 * (1.0 + scale2)
        dn2w = jnp.sum(dn2 * xh2, axis=0, keepdims=True)
        dxh2 = dn2 * n2w
        dx1 = dx2 + r2 * (dxh2 - xh2 * jnp.mean(dxh2 * xh2, axis=-1, keepdims=True))
        dx1_ref[...] = dx1
        dgate1 = jnp.sum(dx1 * o, axis=0, keepdims=True)
        dob = (dx1 * gate1).astype(BF16)
        do_ref[...] = dob
        dmix_ref[...] = _dot_nt(dob, wo[...])
        sm_ref[...] += jnp.concatenate(
            [dfw, dn2w, dshift2, dscale2, dgate2, dgate1, jnp.zeros((2, D), F32)], axis=0)

    row = lambda w: pl.BlockSpec((tm, w), lambda i: (i, 0))
    full = lambda a: pl.BlockSpec(a.shape, lambda i: (0,) * a.ndim)
    anyspec = pl.BlockSpec(memory_space=pl.ANY)
    return pl.pallas_call(
        body, name="mix_ffn", grid=(nt,),
        in_specs=[row(D), row(AW), row(SW), row(D), full(mod6), full(norm2_w), full(final_w), anyspec, anyspec, anyspec,
                  pl.BlockSpec(memory_space=pltpu.SMEM), anyspec],
        out_specs=[pl.BlockSpec((1, D), lambda i: (0, 0)), row(D), row(D), row(D),
                   row(DFF), row(D), row(2 * DFF), row(D), pl.BlockSpec((8, D), lambda i: (0, 0))],
        out_shape=[jax.ShapeDtypeStruct((1, D), F32), jax.ShapeDtypeStruct((T, D), F32), jax.ShapeDtypeStruct((T, D), F32),
                   jax.ShapeDtypeStruct((T, D), BF16), jax.ShapeDtypeStruct((T, DFF), BF16),
                   jax.ShapeDtypeStruct((T, D), BF16), jax.ShapeDtypeStruct((T, 2 * DFF), BF16),
                   jax.ShapeDtypeStruct((T, D), BF16), jax.ShapeDtypeStruct((8, D), F32)],
        scratch_shapes=[pltpu.VMEM((D, D), BF16), pltpu.VMEM((D, 2 * DFF), BF16), pltpu.VMEM((DFF, D), BF16),
                        pltpu.SemaphoreType.DMA((4,))],
        compiler_params=_cp("arbitrary"),
    )(x, attn, ynorm, tgt, mod6, norm2_w, final_w, w_out, w_gu, w_gu_own, s_arr, w_dn)


def _in_proj_bwd(x, dx1, dqkv, dzxd, mod6, norm1_w, w_pad, tm, dep):
    T = x.shape[0]

    def body(x_ref, dx1_ref, dq_ref, dz_ref, mod_ref, nw_ref, w_hbm, dep_ref, gx_ref, sm_ref, w_vmem, sem):
        _load_resident(w_hbm, w_vmem, sem)

        @pl.when(pl.program_id(0) == 0)
        def _():
            sm_ref[...] = jnp.zeros_like(sm_ref)

        nw = nw_ref[...]
        scale1 = mod_ref[1:2, :]
        sums = jnp.zeros((8, D), F32)
        for rows in (slice(0, tm // 2), slice(tm // 2, tm)):
            dh = _dot_nt(dq_ref[rows, :], w_vmem[:, 0:768]) + _dot_nt(dz_ref[rows, :], w_vmem[:, 768:IN_PAD])
            xv = x_ref[rows, :]
            r = lax.rsqrt(jnp.mean(xv * xv, axis=-1, keepdims=True) + EPS)
            xh = xv * r
            n1 = xh * nw
            dshift = jnp.sum(dh, axis=0, keepdims=True)
            dscale = jnp.sum(dh * n1, axis=0, keepdims=True)
            dn = dh * (1.0 + scale1)
            dnw = jnp.sum(dn * xh, axis=0, keepdims=True)
            dxh = dn * nw
            gx_ref[rows, :] = dx1_ref[rows, :] + r * (dxh - xh * jnp.mean(dxh * xh, axis=-1, keepdims=True))
            sums = sums + jnp.concatenate([dnw, dshift, dscale, jnp.zeros((5, D), F32)], axis=0)
        sm_ref[...] += sums

    row = lambda w: pl.BlockSpec((tm, w), lambda i: (i, 0))
    full = lambda a: pl.BlockSpec(a.shape, lambda i: (0,) * a.ndim)
    return pl.pallas_call(
        body, name="in_proj_bwd", grid=(T // tm,),
        in_specs=[row(D), row(D), row(768), row(1664), full(mod6), full(norm1_w), pl.BlockSpec(memory_space=pl.ANY),
                  DEP_SPEC],
        out_specs=[row(D), pl.BlockSpec((8, D), lambda i: (0, 0))],
        out_shape=[jax.ShapeDtypeStruct((T, D), F32), jax.ShapeDtypeStruct((8, D), F32)],
        scratch_shapes=[pltpu.VMEM((D, IN_PAD), BF16), pltpu.SemaphoreType.DMA],
        compiler_params=_cp("arbitrary"),
    )(x, dx1, dqkv, dzxd, mod6, norm1_w, w_pad, dep)


def _tn_matmul(a, b, K, N, tt, name, dep):
    T = a.shape[0]
    ja, jb = a.shape[1] // K, b.shape[1] // N
    J = max(ja, jb)

    def body(a_ref, b_ref, dep_ref, o_ref):
        t = pl.program_id(1)
        prod = _dot_tn(a_ref[...], b_ref[...])

        @pl.when(t == 0)
        def _():
            o_ref[0] = prod

        @pl.when(t > 0)
        def _():
            o_ref[0] += prod

    return pl.pallas_call(
        body, name=name, grid=(J, T // tt),
        in_specs=[pl.BlockSpec((tt, K), lambda j, t: (t, j if ja > 1 else 0)),
                  pl.BlockSpec((tt, N), lambda j, t: (t, j if jb > 1 else 0)),
                  pl.BlockSpec((8, 128), lambda j, t: (0, 0))],
        out_specs=pl.BlockSpec((1, K, N), lambda j, t: (j, 0, 0)),
        out_shape=jax.ShapeDtypeStruct((J, K, N), F32),
        compiler_params=_cp("parallel", "arbitrary"),
    )(a, b, dep)


def _accumulate(o_ref, rows, prod):
    @pl.when(pl.program_id(0) == 0)
    def _():
        o_ref[rows, :] = prod

    @pl.when(pl.program_id(0) > 0)
    def _():
        o_ref[rows, :] += prod


def _tn_matmul_rows(a0, a1, b, tt, name, dep):
    T, K = a0.shape
    N = b.shape[1]

    def body(a0_ref, a1_ref, b_ref, dep_ref, o_ref):
        for k, a_ref in enumerate((a0_ref, a1_ref)):
            _accumulate(o_ref, slice(k * K, (k + 1) * K), _dot_tn(a_ref[...], b_ref[...]))

    tile = lambda w: pl.BlockSpec((tt, w), lambda t: (t, 0))
    return pl.pallas_call(
        body, name=name, grid=(T // tt,), in_specs=[tile(K), tile(K), tile(N), DEP_SPEC],
        out_specs=pl.BlockSpec((2 * K, N), lambda t: (0, 0)), out_shape=jax.ShapeDtypeStruct((2 * K, N), F32),
        compiler_params=_cp("arbitrary"),
    )(a0, a1, b, dep)


def _adam_math(w, g, m, v):
    m = B1 * m + (1.0 - B1) * g
    v = B2 * v + (1.0 - B2) * (g * g)
    m_hat = m / (1.0 - B1 ** STEP)
    v_hat = v / (1.0 - B2 ** STEP)
    delta = -LR * (m_hat / (jnp.sqrt(v_hat) + AEPS) + WD * w)
    return delta, m, v


def _adam_2d(w, mine, land, m, v, c_arr, rb, name, dep):
    R, C = w.shape
    nbh = R // 2 // rb

    def body(c_ref, w_ref, mine_ref, land_ref, m_ref, v_ref, dep_ref, go_ref, d_ref, mo_ref, vo_ref):
        g = jnp.where(pl.program_id(0) // nbh == c_ref[0], mine_ref[...], land_ref[...])
        d, mn, vn = _adam_math(w_ref[...], g, m_ref[...], v_ref[...])
        go_ref[...] = g
        d_ref[...] = d
        mo_ref[...] = mn
        vo_ref[...] = vn

    spec = pl.BlockSpec((rb, C), lambda i, c_ref: (i, 0))
    mine_spec = pl.BlockSpec((rb, C), lambda i, c_ref: (jnp.clip(i - c_ref[0] * nbh, 0, nbh - 1), 0))
    return pl.pallas_call(
        body, name=name,
        grid_spec=pltpu.PrefetchScalarGridSpec(
            num_scalar_prefetch=1, grid=(R // rb,), in_specs=[spec, mine_spec, spec, spec, spec, DEP_SPEC],
            out_specs=[spec] * 4),
        out_shape=[jax.ShapeDtypeStruct((R, C), F32)] * 4, compiler_params=_cp("parallel"),
    )(c_arr, w, mine, land, m, v, dep)


def _adam_w_in(w3, mine, land, m3, v3, c_arr):
    n = w3.shape[0]

    def body(c_ref, w_hbm, mine_ref, land_ref, m_hbm, v_hbm, g_hbm, d_hbm, mo_hbm, vo_hbm, bufs, sems):
        ins = [pltpu.make_async_copy(src.at[:, 0], bufs.at[k], sems.at[k]) for k, src in enumerate((w_hbm, m_hbm, v_hbm))]
        for cp in ins:
            cp.start()
        half = D // 2
        top = jnp.where(c_ref[0] == 0, mine_ref[...], land_ref[0:half, :])
        bot = jnp.where(c_ref[0] == 1, mine_ref[...], land_ref[half:D, :])
        g = jnp.concatenate([top, bot], axis=0)
        eye = (_iota((D, D), 0) == _iota((D, D), 1)).astype(BF16)
        g_t = jnp.zeros((n, D), F32)
        r = g
        for i in range(3):
            p = r.astype(BF16)
            g_t = g_t + _dot_tn(p, eye)
            if i < 2:
                r = r - p.astype(F32)
        for cp in ins:
            cp.wait()
        d, mn, vn = _adam_math(bufs[0], g_t, bufs[1], bufs[2])
        for k, val in enumerate((g_t, d, mn, vn)):
            bufs[3 + k] = val
        outs = [pltpu.make_async_copy(bufs.at[3 + k], dst.at[:, 0], sems.at[3 + k])
                for k, dst in enumerate((g_hbm, d_hbm, mo_hbm, vo_hbm))]
        for cp in outs:
            cp.start()
        for cp in outs:
            cp.wait()

    anyspec = pl.BlockSpec(memory_space=pl.ANY)
    vm = pl.BlockSpec(memory_space=pltpu.VMEM)
    return pl.pallas_call(
        body, name="adam_w_in",
        in_specs=[pl.BlockSpec(memory_space=pltpu.SMEM), anyspec, vm, vm, anyspec, anyspec], out_specs=[anyspec] * 4,
        out_shape=[jax.ShapeDtypeStruct(w3.shape, F32)] * 4,
        scratch_shapes=[pltpu.VMEM((7, n, D), F32), pltpu.SemaphoreType.DMA((7,))],
        compiler_params=pltpu.CompilerParams(vmem_limit_bytes=VMEM_LIMIT),
    )(c_arr, w3, mine, land, m3, v3)


def _adam_w_ada(gat, allv, s_arr, w, m, v, rb):
    R, C = w.shape

    def body(s_ref, c_ref, dm_ref, w_ref, m_ref, v_ref, g_ref, d_ref, mo_ref, vo_ref):
        cm = _rows_select(c_ref, rb)
        g = lax.dot_general(cm * _sigmoid(cm), _rows_select(dm_ref, C), (((0,), (0,)), ((), ())), precision=HI,
                            preferred_element_type=F32)
        d, mn, vn = _adam_math(w_ref[...], g, m_ref[...], v_ref[...])
        g_ref[...] = g
        d_ref[...] = d
        mo_ref[...] = mn
        vo_ref[...] = vn

    spec = pl.BlockSpec((rb, C), lambda i, s_ref: (i, 0))
    return pl.pallas_call(
        body, name="adam_w_ada",
        grid_spec=pltpu.PrefetchScalarGridSpec(
            num_scalar_prefetch=1, grid=(R // rb,),
            in_specs=[pl.BlockSpec((8, 1, rb), lambda i, s_ref: (0, 0, i)),
                      pl.BlockSpec((8, 1, C), lambda i, s_ref: (0, 0, s_ref[0])), spec, spec, spec],
            out_specs=[spec] * 4),
        out_shape=[jax.ShapeDtypeStruct((R, C), F32)] * 4, compiler_params=_cp("parallel"),
    )(s_arr, gat, allv, w, m, v)


def _adam_small(tot, segs, ws, ms, vs):
    k = len(ws)
    extra = [sg for sg in segs if not isinstance(sg, tuple)]
    ne = len(extra)

    def body(*refs):
        tot_ref, g_x = refs[0], list(refs[1:1 + ne])
        w, m, v = [refs[1 + ne + j * k:1 + ne + (j + 1) * k] for j in range(3)]
        g_o, d_o, m_o, v_o = [refs[1 + ne + (3 + j) * k:1 + ne + (4 + j) * k] for j in range(4)]
        for i in range(k):
            gi = tot_ref[:, segs[i][0]:segs[i][0] + segs[i][1]] if isinstance(segs[i], tuple) else g_x.pop(0)[...]
            d, mn, vn = _adam_math(w[i][...], gi, m[i][...], v[i][...])
            g_o[i][...] = gi
            d_o[i][...] = d
            m_o[i][...] = mn
            v_o[i][...] = vn

    shapes = [jax.ShapeDtypeStruct(w.shape, F32) for w in ws]
    vm = pl.BlockSpec(memory_space=pltpu.VMEM)
    outs = pl.pallas_call(
        body, name="adam_small", in_specs=[vm] * (1 + ne + 3 * k), out_specs=[vm] * (4 * k), out_shape=shapes * 4,
    )(tot, *extra, *ws, *ms, *vs)
    return outs[0:k], outs[k:2 * k], outs[2 * k:3 * k], outs[3 * k:4 * k]


def _pos():
    return lax.axis_index("x"), lax.axis_index("y"), lax.axis_index("c")


def _flip(v, bit):
    return 1 - v if bit else v


def _peer(k):
    x, y, c = _pos()
    return (_flip(x, (k >> 2) & 1), _flip(y, (k >> 1) & 1), _flip(c, k & 1))


def _logical(p):
    return 4 * p[0] + 2 * p[1] + p[2]


def _gather8(src_ref, dst_ref, send_sems, recv_sems):
    me = _logical(_pos())
    dst_ref[pl.ds(me, 1)] = src_ref[...][None]
    copies = []
    for k in range(1, 8):
        cp = pltpu.make_async_remote_copy(src_ref, dst_ref.at[me], send_sems.at[k - 1], recv_sems.at[k - 1],
                                          device_id=_peer(k), device_id_type=MESH)
        cp.start()
        copies.append(cp)
    for k in range(1, 8):
        pltpu.make_async_remote_copy(src_ref, dst_ref.at[_logical(_peer(k))], send_sems.at[k - 1], recv_sems.at[k - 1],
                                     device_id=_peer(k), device_id_type=MESH).wait_recv()
    for cp in copies:
        cp.wait_send()


def _rows_select(ref3, width):
    row = _iota((8, width), 0)
    out = jnp.zeros((8, width), F32)
    for i in range(8):
        out = jnp.where(row == i, ref3[i][:, 0:width], out)
    return out


def _mod_exchange(payload, w_ada_s, b_ada4):
    n_sh = w_ada_s.shape[1]

    def body(pay_ref, w_ref, b_ref, gat_ref, mod_ref, token, p3, sa, ra, sb, rb):
        token[...] = jnp.zeros_like(token)
        x, y, c = _pos()
        me = _logical((x, y, c))
        my_s = 2 * x + y
        _gather8(pay_ref, gat_ref, sa, ra)
        cmat = _rows_select(gat_ref, D)
        prod = _dot_hi(cmat * _sigmoid(cmat), w_ref[...])
        for b in range(8):
            p3[b] = prod[b:b + 1, :]
        mod_ref[pl.ds(my_s, 1)] = p3[pl.ds(me, 1)] + b_ref[pl.ds(my_s, 1)]
        ks = (2, 4, 6)
        copies = []
        for i, k in enumerate(ks):
            pr = _peer(k)
            cp = pltpu.make_async_remote_copy(p3.at[_logical(pr)], mod_ref.at[my_s], sb.at[i], rb.at[i],
                                              device_id=pr, device_id_type=MESH)
            cp.start()
            copies.append(cp)
        for i, k in enumerate(ks):
            pr = _peer(k)
            s_src = 2 * pr[0] + pr[1]
            pltpu.make_async_remote_copy(p3.at[0], mod_ref.at[s_src], sb.at[i], rb.at[i],
                                         device_id=pr, device_id_type=MESH).wait_recv()
            mod_ref[pl.ds(s_src, 1)] = mod_ref[pl.ds(s_src, 1)] + b_ref[pl.ds(s_src, 1)]
        for cp in copies:
            cp.wait_send()

    vm = pl.BlockSpec(memory_space=pltpu.VMEM)
    return pl.pallas_call(
        body, name="mod_exchange", in_specs=[vm, vm, vm], out_specs=[vm, vm, vm],
        out_shape=[jax.ShapeDtypeStruct((8, 1, payload.shape[1]), F32), jax.ShapeDtypeStruct((4, 1, n_sh), F32),
                   jax.ShapeDtypeStruct((8, 128), F32)],
        scratch_shapes=[pltpu.VMEM((8, 1, n_sh), F32), pltpu.SemaphoreType.DMA((7,)), pltpu.SemaphoreType.DMA((7,)),
                        pltpu.SemaphoreType.DMA((3,)), pltpu.SemaphoreType.DMA((3,))],
        compiler_params=pltpu.CompilerParams(vmem_limit_bytes=VMEM_LIMIT),
    )(payload, w_ada_s, b_ada4)


def _chips():
    x, y, _ = _pos()
    out = []
    for k in (1, 2, 3):
        px, py = _flip(x, (k >> 1) & 1), _flip(y, k & 1)
        out.append((px, py, 2 * px + py))
    return out


def _half_rows(ref, which):
    half = ref.shape[-2] // 2
    return pl.ds(pl.multiple_of(which * half, 8), half)


def _plan_small():
    def plan(refs):
        me = _logical(_pos())
        return [(refs[0], refs[1].at[me], _peer(k), refs[1].at[_logical(_peer(k))]) for k in range(1, 8)]
    return plan


def _small_sum(vec, land, me_arr):
    n = vec.shape[1]

    def body(me_ref, v_ref, land_ref, tot_ref, all_ref):
        tot = None
        for i in range(8):
            row = jnp.where(me_ref[0] == i, v_ref[...], land_ref[i])
            all_ref[i] = row
            tot = row if i == 0 else tot + row
        tot_ref[...] = tot

    return pl.pallas_call(
        body, name="small_sum",
        grid_spec=pltpu.PrefetchScalarGridSpec(
            num_scalar_prefetch=1, grid=(1,),
            in_specs=[pl.BlockSpec((1, n), lambda i, me_ref: (0, 0)), pl.BlockSpec((8, 1, n), lambda i, me_ref: (0, 0, 0))],
            out_specs=[pl.BlockSpec((1, n), lambda i, me_ref: (0, 0)),
                       pl.BlockSpec((8, 1, n), lambda i, me_ref: (0, 0, 0))]),
        out_shape=[jax.ShapeDtypeStruct((1, n), F32), jax.ShapeDtypeStruct((8, 1, n), F32)],
        compiler_params=_cp("arbitrary"),
    )(me_arr, vec, land)


def _add_half(g, sib, c_arr, rb, name):
    _, R, C = g.shape
    half = R // 2
    nb = half // rb

    def body(c_ref, g_ref, s_ref, o_ref):
        o_ref[...] = (g_ref[...] + s_ref[...]).astype(BF16)

    return pl.pallas_call(
        body, name=name,
        grid_spec=pltpu.PrefetchScalarGridSpec(
            num_scalar_prefetch=1, grid=(4, nb),
            in_specs=[pl.BlockSpec((1, rb, C), lambda s, i, c_ref: (s, c_ref[0] * nb + i, 0)),
                      pl.BlockSpec((1, rb, C), lambda s, i, c_ref: (s, i, 0))],
            out_specs=pl.BlockSpec((1, rb, C), lambda s, i, c_ref: (s, i, 0))),
        out_shape=jax.ShapeDtypeStruct((4, half, C), BF16),
        compiler_params=_cp("parallel", "parallel"),
    )(c_arr, g, sib)


def _add_half_in(gq, gz, sibq, sibz, c_arr, rb):
    half = D // 2
    nq = gq.shape[1]
    wide = -(-IN_SH // 128) * 128

    def sel(rows, first, lo):
        return (_iota((rows, wide), 0) + (first - lo) == _iota((rows, wide), 1)).astype(BF16)

    def body(c_ref, gq_ref, gz_ref, sq_ref, sz_ref, o_ref):
        q = (gq_ref[...] + sq_ref[...]).astype(BF16)
        z = (gz_ref[...] + sz_ref[...]).astype(BF16)
        for s in range(4):
            lo, hi = s * IN_SH, (s + 1) * IN_SH
            acc = jnp.zeros((rb, wide), F32)
            if lo < nq:
                a0, a1 = lo // 128 * 128, min(nq, -(-min(hi, nq) // 128) * 128)
                acc = acc + _dot(q[:, a0:a1], sel(a1 - a0, a0, lo))
            if hi > nq:
                a0, a1 = (max(lo, nq) - nq) // 128 * 128, -(-(hi - nq) // 128) * 128
                acc = acc + _dot(z[:, a0:a1], sel(a1 - a0, nq + a0, lo))
            o_ref[s] = acc[:, :IN_SH].astype(BF16)

    nb = half // rb
    mine = lambda w: pl.BlockSpec((rb, w), lambda i, c_ref: (c_ref[0] * nb + i, 0))
    sib = lambda w: pl.BlockSpec((rb, w), lambda i, c_ref: (i, 0))
    return pl.pallas_call(
        body, name="grad_add_in",
        grid_spec=pltpu.PrefetchScalarGridSpec(
            num_scalar_prefetch=1, grid=(nb,),
            in_specs=[mine(nq), mine(gz.shape[1]), sib(nq), sib(gz.shape[1])],
            out_specs=pl.BlockSpec((4, rb, IN_SH), lambda i, c_ref: (0, i, 0))),
        out_shape=jax.ShapeDtypeStruct((4, half, IN_SH), BF16),
        compiler_params=_cp("parallel"),
    )(c_arr, gq, gz, sibq, sibz)


def _sum4(parts, land, s_arr, rb, name):
    _, H, C = land.shape

    def body(s_ref, own_ref, r_ref, o_ref):
        own = own_ref[0].astype(F32)
        tot = jnp.zeros((rb, C), F32)
        for j in range(4):
            tot = tot + jnp.where(s_ref[0] == j, own, r_ref[j].astype(F32))
        o_ref[...] = tot

    return pl.pallas_call(
        body, name=name,
        grid_spec=pltpu.PrefetchScalarGridSpec(
            num_scalar_prefetch=1, grid=(H // rb,),
            in_specs=[pl.BlockSpec((1, rb, C), lambda i, s_ref: (s_ref[0], i, 0)),
                      pl.BlockSpec((4, rb, C), lambda i, s_ref: (0, i, 0))],
            out_specs=pl.BlockSpec((rb, C), lambda i, s_ref: (i, 0))),
        out_shape=jax.ShapeDtypeStruct((H, C), F32), compiler_params=_cp("parallel"),
    )(s_arr, parts, land)


HBM_SPEC = pl.BlockSpec(memory_space=pltpu.HBM)
SEM_SPEC = pl.BlockSpec(memory_space=pltpu.SEMAPHORE)
EFFECT = pltpu.SideEffectType.DATAFLOW_SIDE_EFFECTING


def _split_start(name, bufs, n_sem, plan, dep):
    nb = len(bufs)

    def body(*refs):
        ins, send, recv, token = refs[:nb], refs[nb + 1], refs[nb + 2], refs[-1]
        for i, (src, dst, dev, _) in enumerate(plan(ins)):
            pltpu.make_async_remote_copy(src, dst, send.at[i], recv.at[i], device_id=dev, device_id_type=MESH).start()
        token[...] = jnp.zeros_like(token)

    outs = pl.pallas_call(
        body, name=name,
        out_shape=(pltpu.SemaphoreType.DMA((n_sem,)), pltpu.SemaphoreType.DMA((n_sem,)),
                   *[pltpu.HBM(b.shape, b.dtype) for b in bufs], jax.ShapeDtypeStruct((8, 128), F32)),
        in_specs=[HBM_SPEC] * nb + [pl.BlockSpec(memory_space=pl.ANY)],
        out_specs=(SEM_SPEC, SEM_SPEC, *([HBM_SPEC] * nb), pl.BlockSpec(memory_space=pltpu.VMEM)),
        input_output_aliases={i: 2 + i for i in range(nb)},
        compiler_params=pltpu.CompilerParams(has_side_effects=EFFECT),
    )(*[pltpu.with_memory_space_constraint(b, pltpu.HBM) for b in bufs], dep)
    return outs[0], outs[1], list(outs[2:2 + nb]), outs[-1]


def _split_wait(name, send, recv, bufs, after, plan):
    nb = len(bufs)
    after = list(after) if isinstance(after, (list, tuple)) else [after]

    def body(*refs):
        ins, send_s, recv_s = refs[:nb], refs[nb], refs[nb + 1]
        for i, (src, dst, dev, mine) in enumerate(plan(ins)):
            pltpu.make_async_remote_copy(src, dst, send_s.at[i], recv_s.at[i], device_id=dev,
                                         device_id_type=MESH).wait_send()
            pltpu.make_async_remote_copy(src, mine, send_s.at[i], recv_s.at[i], device_id=dev,
                                         device_id_type=MESH).wait_recv()

    outs = pl.pallas_call(
        body, name=name, out_shape=[pltpu.HBM(b.shape, b.dtype) for b in bufs],
        in_specs=[HBM_SPEC] * nb + [SEM_SPEC, SEM_SPEC] + [HBM_SPEC] * len(after),
        out_specs=[HBM_SPEC] * nb, input_output_aliases={i: i for i in range(nb)},
        compiler_params=pltpu.CompilerParams(has_side_effects=EFFECT),
    )(*bufs, send, recv, *[pltpu.with_memory_space_constraint(a, pltpu.HBM) for a in after])
    return list(outs)


def _copies_now(name, bufs, n_sem, plan):
    nb = len(bufs)

    def body(*refs):
        ins, token, send, recv = refs[:nb], refs[2 * nb], refs[-2], refs[-1]
        token[...] = jnp.zeros_like(token)
        todo = plan(ins)
        for i, (src, dst, dev, _) in enumerate(todo):
            pltpu.make_async_remote_copy(src, dst, send.at[i], recv.at[i], device_id=dev, device_id_type=MESH).start()
        for i, (src, dst, dev, mine) in enumerate(todo):
            pltpu.make_async_remote_copy(src, mine, send.at[i], recv.at[i], device_id=dev, device_id_type=MESH).wait_recv()
        for i, (src, dst, dev, _) in enumerate(todo):
            pltpu.make_async_remote_copy(src, dst, send.at[i], recv.at[i], device_id=dev, device_id_type=MESH).wait_send()

    outs = pl.pallas_call(
        body, name=name,
        out_shape=[pltpu.HBM(b.shape, b.dtype) for b in bufs] + [jax.ShapeDtypeStruct((8, 128), F32)],
        in_specs=[HBM_SPEC] * nb, out_specs=[HBM_SPEC] * nb + [pl.BlockSpec(memory_space=pltpu.VMEM)],
        input_output_aliases={i: i for i in range(nb)},
        scratch_shapes=[pltpu.SemaphoreType.DMA((n_sem,)), pltpu.SemaphoreType.DMA((n_sem,))],
    )(*[pltpu.with_memory_space_constraint(b, pltpu.HBM) for b in bufs])
    return list(outs[:nb]), outs[nb]


def _slot(land, s, rows, cols):
    if cols is None:
        return land.at[s, rows]
    return land.at[rows, pl.ds(pl.multiple_of(s * cols, 128), cols)]


def _plan_gather_ici(cols):
    nw = len(cols)

    def plan(refs):
        x, y, c = _pos()
        my_s = 2 * x + y
        out = []
        for w in range(nw):
            mine = _half_rows(refs[w], c)
            for px, py, ps in _chips():
                out.append((refs[w].at[mine], _slot(refs[nw + w], my_s, mine, cols[w]), (px, py, c),
                            _slot(refs[nw + w], ps, mine, cols[w])))
        return out
    return plan


def _plan_gather_fwd(cols, rows):
    def plan(refs):
        x, y, c = _pos()
        out = []
        for w in range(len(cols)):
            half = rows[w] // 2
            mine = pl.ds(pl.multiple_of(c * half, 8), half)
            other = pl.ds(pl.multiple_of((1 - c) * half, 8), half)
            for px, py, ps in _chips():
                got = _slot(refs[w], ps, mine, cols[w])
                out.append((got, got, (x, y, 1 - c), _slot(refs[w], ps, other, cols[w])))
        return out
    return plan


def _plan_swap(nw):
    def plan(refs):
        x, y, c = _pos()
        return [(refs[w].at[:, _half_rows(refs[w], 1 - c)], refs[nw + w], (x, y, 1 - c), refs[nw + w])
                for w in range(nw)]
    return plan


def _plan_swap_rows(nw):
    def plan(refs):
        x, y, c = _pos()
        return [(refs[w].at[_half_rows(refs[w], 1 - c)], refs[nw + w], (x, y, 1 - c), refs[nw + w])
                for w in range(nw)]
    return plan


def _plan_scatter(nw):
    def plan(refs):
        x, y, c = _pos()
        my_s = 2 * x + y
        out = []
        for w in range(nw):
            for px, py, ps in _chips():
                out.append((refs[w].at[ps], refs[nw + w].at[my_s], (px, py, c), refs[nw + w].at[ps]))
        return out
    return plan


def _plan_join(nw):
    def plan(refs):
        x, y, c = _pos()
        out = []
        for w in range(nw):
            land = refs[nw + w]
            out.append((refs[w], land.at[_half_rows(land, c)], (x, y, 1 - c), land.at[_half_rows(land, 1 - c)]))
        return out
    return plan


def _hbm_empty(shape, dtype):
    return pltpu.with_memory_space_constraint(lax.empty(shape, dtype), pltpu.HBM)


def _put_slot(land, own, slot):
    return lax.dynamic_update_slice(land, own[None], (slot,) + (0,) * own.ndim)


def _w_in_assemble(land, own, s_arr, rb):
    wide = -(-IN_SH // 128) * 128
    starts = [s * IN_SH // 128 * 128 for s in range(4)]
    ends = [min(IN_PAD, -(-(s + 1) * IN_SH // 128) * 128) for s in range(4)]

    def body(s_ref, land_ref, own_ref, o_ref, parts):
        @pl.when(pl.program_id(0) == 0)
        def _():
            parts[...] = jnp.zeros_like(parts)

        acc = []
        for s in range(4):
            parts[s, :, 0:IN_SH] = jnp.where(s_ref[0] == s, own_ref[...], land_ref[s])
            w = ends[s] - starts[s]
            sel = (_iota((wide, w), 0) + (s * IN_SH - starts[s]) == _iota((wide, w), 1)).astype(BF16)
            acc.append(_dot(parts[s], sel))
        for s in range(4):
            lo = starts[s] if s == 0 else ends[s - 1]
            hi = starts[s + 1] if s < 3 else ends[s]
            o_ref[:, lo:hi] = acc[s][:, lo - starts[s]:hi - starts[s]].astype(BF16)
            if s < 3:
                a, b = starts[s + 1], ends[s]
                o_ref[:, a:b] = (acc[s][:, a - starts[s]:b - starts[s]] + acc[s + 1][:, 0:b - a]).astype(BF16)

    return pl.pallas_call(
        body, name="w_in_assemble",
        grid_spec=pltpu.PrefetchScalarGridSpec(
            num_scalar_prefetch=1, grid=(D // rb,),
            in_specs=[pl.BlockSpec((4, rb, IN_SH), lambda i, s_ref: (0, i, 0)),
                      pl.BlockSpec((rb, IN_SH), lambda i, s_ref: (i, 0))],
            out_specs=pl.BlockSpec((rb, IN_PAD), lambda i, s_ref: (i, 0)),
            scratch_shapes=[pltpu.VMEM((4, rb, wide), BF16)]),
        out_shape=jax.ShapeDtypeStruct((D, IN_PAD), BF16), compiler_params=_cp("arbitrary"),
    )(s_arr, land, own)


def _pad_lanes(a, n):
    return jnp.pad(a, ((0, 0), (0, n - a.shape[1])))


def kernel(x, c, positions, w_ada, b_ada, norm1_w, w_in, conv_w, conv_b, dt_bias, a_log, d_skip, attn_sinks, ssm_norm_w, w_out, norm2_w, w_gate_up, w_down, final_norm_w, loss_target, m_w_ada, m_b_ada, m_norm1_w, m_w_in, m_conv_w, m_conv_b, m_dt_bias, m_a_log, m_d_skip, m_attn_sinks, m_ssm_norm_w, m_w_out, m_norm2_w, m_w_gate_up, m_w_down, m_final_norm_w, v_w_ada, v_b_ada, v_norm1_w, v_w_in, v_conv_w, v_conv_b, v_dt_bias, v_a_log, v_d_skip, v_attn_sinks, v_ssm_norm_w, v_w_out, v_norm2_w, v_w_gate_up, v_w_down, v_final_norm_w):
    T = x.shape[1]
    tm = min(256, T)
    xi, yi, ci = lax.axis_index("x"), lax.axis_index("y"), lax.axis_index("c")
    my_s = 2 * xi + yi
    xs = x[0]
    tgt = loss_target[0]

    payload = jnp.concatenate([c, conv_w[0].reshape(1, CONVK * 256)], axis=1)
    gat, mod4, tok = _mod_exchange(payload, w_ada[0], b_ada.reshape(4, 1, 1536))
    mod6 = mod4.reshape(6, D)
    cw_dev = gat[:, 0, D:].reshape(4, 2, CONVK, 256)[:, 0]
    conv_full = cw_dev.transpose(1, 0, 2).reshape(CONVK, CONVC)

    w_in_b = w_in[0].astype(BF16)
    s_i, r_i, bufs, tok = _split_start("wgather_in_ici_start", [w_in_b, _hbm_empty((4,) + w_in_b.shape, BF16)], 3,
                                       _plan_gather_ici([None]), tok)
    inv_freq = (10000.0 ** (-jnp.arange(32, dtype=F32) / 32))
    cos, sin_s = _rope_tables(positions, inv_freq.reshape(32, 1), min(512, T), tok)
    late = [w_out[0].astype(BF16), w_gate_up[0].astype(BF16), w_down[0].astype(BF16)]
    bufs = _split_wait("wgather_in_ici_wait", s_i, r_i, bufs, [cos] + late, _plan_gather_ici([None]))
    own_in = bufs[0]
    bufs, tok = _copies_now("wgather_in_fwd", bufs[1:], 3, _plan_gather_fwd([None], [D]))
    s_arr = my_s.reshape(1).astype(jnp.int32)
    w_pad = _w_in_assemble(bufs[0], own_in, s_arr, 256)

    lands = [_hbm_empty((4, D // 4, D), BF16), _hbm_empty((D, 2 * DFF), BF16), _hbm_empty((4, DFF // 4, D), BF16)]
    cols3, rows3 = [None, GU_SH, None], [D // 4, D, DFF // 4]
    s_a, r_a, bufs, tok = _split_start("wgather_ici_start", late + lands, 9, _plan_gather_ici(cols3), tok)

    qkv, z, xbc, dtr, h1b = _in_proj_fwd(xs, cos, sin_s, mod6, norm1_w, w_pad, min(512, T), tok)
    sinks = attn_sinks
    attn, lse = _attn_fwd(qkv, sinks)
    bufs = _split_wait("wgather_ici_wait", s_a, r_a, bufs, attn, _plan_gather_ici(cols3))
    late = bufs[:3]
    s_b, r_b, lands, tok = _split_start("wgather_fwd_start", bufs[3:], 9, _plan_gather_fwd(cols3, rows3), attn)
    dtb = _pad_lanes(dt_bias, 128)
    alog = _pad_lanes(a_log, 128)
    dskx = jnp.repeat(d_skip, HD, axis=1)
    mats = _ssd_mats()
    ynorm, ypre, states, conv_pre = _ssd_fwd(xbc, z, dtr, conv_full, conv_b, dtb, alog, dskx, ssm_norm_w, mats, tok)
    lands = _split_wait("wgather_fwd_wait", s_b, r_b, lands, ynorm, _plan_gather_fwd(cols3, rows3))
    w_out_f = _put_slot(lands[0], late[0], my_s).reshape(D, D)
    w_dn_f = _put_slot(lands[2], late[2], my_s).reshape(DFF, D)

    fw2 = final_norm_w.reshape(1, D)
    sq, dmix, dx1, h2b, act, dfb, dgu, dob, sm_ffn = _mix_ffn(
        xs, attn, ynorm, tgt, mod6, norm2_w, fw2, w_out_f, lands[1], late[1], s_arr, w_dn_f, tm)

    tt = min(2048, T)
    c_arr = ci.reshape(1).astype(jnp.int32)
    tok0 = jnp.zeros((8, 128), F32)
    gw_dn4 = _tn_matmul(act, dfb, GU_SH, D, tt, "dw_down", tok0).reshape(4, DFF // 4, D)
    gw_gu4 = _tn_matmul(h2b, dgu, D, GU_SH, tt, "dw_gate_up", tok0)
    gw_out4 = _tn_matmul_rows(attn, ynorm, dob, tt, "dw_out", tok0).reshape(4, D // 4, D)
    big1 = [gw_out4, gw_gu4, gw_dn4]
    rbs1 = [128, 512, 352]
    sib1 = [_hbm_empty((4, g.shape[1] // 2, g.shape[2]), F32) for g in big1]
    s_c, r_c, bufs, tok = _split_start("gswap_start", big1 + sib1, 3, _plan_swap(3), tok0)

    dzxd, d_cw, d_cb, d_sw, d_sk, d_dtb, d_av = _ssd_bwd(
        xbc, conv_pre, z, dtr, ypre, states, dmix, conv_full, dtb, alog, dskx, ssm_norm_w, mats, tok)
    bufs = _split_wait("gswap_wait", s_c, r_c, bufs, dzxd, _plan_swap(3))
    sums1 = [_add_half(g, s, c_arr, rb, "grad_add_%d" % i)
             for i, (g, s, rb) in enumerate(zip(bufs[:3], bufs[3:], rbs1))]
    land1 = [_hbm_empty(p.shape, BF16) for p in sums1]
    s_d, r_d, bufs, tok = _split_start("gscatter_start", sums1 + land1, 9, _plan_scatter(3), tok0)
    dqkv, d_sinks = _attn_bwd(qkv, sinks, lse, dmix, cos, sin_s, tok)
    bufs = _split_wait("gscatter_wait", s_d, r_d, bufs, dqkv, _plan_scatter(3))
    halves1 = [_sum4(p, l, s_arr, rb, "grad_sum_%d" % i)
               for i, (p, l, rb) in enumerate(zip(bufs[:3], bufs[3:], rbs1))]
    full1 = [_hbm_empty((2 * h.shape[0], h.shape[1]), F32) for h in halves1]
    s_e, r_e, bufs, tok = _split_start("gjoin_start", halves1 + full1, 3, _plan_join(3), tok0)
    gq = _tn_matmul(h1b, dqkv, D, 768, tt, "dw_in_qkv", tok)[0]
    gz = _tn_matmul(h1b, dzxd, D, IN_PAD - 768, tt, "dw_in_zxd", tok)[0]
    joined1 = _split_wait("gjoin_wait", s_e, r_e, bufs, [gq, gz], _plan_join(3))

    sibs = [_hbm_empty((D // 2, g.shape[1]), F32) for g in (gq, gz)]
    s_f, r_f, bufs, tok = _split_start("gswap_in_start", [gq, gz] + sibs, 2, _plan_swap_rows(2), tok0)
    g_dn_s, d_dn, m_dn, v_dn = _adam_2d(w_down[0], joined1[2], joined1[5], m_w_down[0], v_w_down[0], c_arr, 352,
                                        "adam_w_down", tok)
    g_gu_s, d_gu, m_gu, v_gu = _adam_2d(w_gate_up[0], joined1[1], joined1[4], m_w_gate_up[0], v_w_gate_up[0], c_arr,
                                        256, "adam_w_gate_up", tok)
    g_out_s, d_out, m_out, v_out = _adam_2d(w_out[0], joined1[0], joined1[3], m_w_out[0], v_w_out[0], c_arr, 128,
                                            "adam_w_out", tok)
    bufs = _split_wait("gswap_in_wait", s_f, r_f, bufs, [d_dn, d_gu, d_out], _plan_swap_rows(2))
    sum0 = _add_half_in(bufs[0], bufs[1], bufs[2], bufs[3], c_arr, min(256, D // 2))
    s_g, r_g, bufs, tok = _split_start("gscatter_in_start", [sum0, _hbm_empty(sum0.shape, BF16)], 3, _plan_scatter(1),
                                       tok0)
    grad_x, sm_in = _in_proj_bwd(xs, dx1, dqkv, dzxd, mod6, norm1_w, w_pad, min(512, T), tok)

    a_neg = -jnp.exp(alog)
    pieces = [sm_in[1:2], sm_in[2:3], sm_ffn[5:6], sm_ffn[2:3], sm_ffn[3:4], sm_ffn[4:5],
              sm_in[0:1], sm_ffn[1:2], sm_ffn[0:1], d_cb, d_cw.reshape(1, CONVK * CONVC),
              _pad_lanes(d_sw, SW), d_dtb, d_av * a_neg, d_sk, d_sinks,
              _pad_lanes((0.5 / D * jnp.sum(sq)).reshape(1, 1), 128)]
    vec = jnp.concatenate(pieces, axis=1)
    s_h, r_h, rows8, tok_small = _split_start("small_start", [vec, _hbm_empty((8,) + vec.shape, F32)], 7,
                                              _plan_small(), tok0)

    bufs = _split_wait("gscatter_in_wait", s_g, r_g, bufs, [grad_x, tok_small], _plan_scatter(1))
    half0 = _sum4(bufs[0], bufs[1], s_arr, 512, "grad_sum_in")
    joined0, _ = _copies_now("gjoin_in", [half0, _hbm_empty((D, IN_SH), F32)], 1, _plan_join(1))
    native = lambda a: a.transpose(2, 0, 1)
    adam_in = _adam_w_in(native(w_in), joined0[0], joined0[1], native(m_w_in), native(v_w_in), c_arr)
    g_in_s, d_in, m_in, v_in = [a.transpose(1, 2, 0) for a in adam_in]
    rows8 = _split_wait("small_wait", s_h, r_h, rows8, [adam_in[1]], _plan_small())
    tot, allv = _small_sum(rows8[0], rows8[1], (4 * xi + 2 * yi + ci).reshape(1).astype(jnp.int32))
    o = 0
    offs = []
    for p in pieces:
        offs.append(o)
        o += p.shape[1]
    seg = lambda i, n: (offs[i], n)
    g_conv_w = lax.dynamic_slice_in_dim(
        tot[:, offs[10]:offs[10] + CONVK * CONVC].reshape(CONVK, CONVC), my_s * 256, 256, axis=1)
    loss = tot[0, offs[16]]

    small_names = ["b_ada", "norm1_w", "conv_w", "conv_b", "dt_bias", "a_log", "d_skip", "attn_sinks", "ssm_norm_w",
                   "norm2_w", "final_norm_w"]
    small_g = [(0, 6 * D), seg(6, D), g_conv_w, seg(9, D), seg(12, 8), seg(13, 8), seg(14, 8), seg(15, 8),
               seg(11, SW), seg(7, D), seg(8, D)]
    as2d = lambda a: a.reshape(-1, a.shape[-1])
    small_w = [as2d(a) for a in (b_ada, norm1_w, conv_w, conv_b, dt_bias, a_log, d_skip, attn_sinks, ssm_norm_w,
                                 norm2_w, final_norm_w)]
    small_m = [as2d(a) for a in (m_b_ada, m_norm1_w, m_conv_w, m_conv_b, m_dt_bias, m_a_log, m_d_skip, m_attn_sinks,
                                 m_ssm_norm_w, m_norm2_w, m_final_norm_w)]
    small_v = [as2d(a) for a in (v_b_ada, v_norm1_w, v_conv_w, v_conv_b, v_dt_bias, v_a_log, v_d_skip, v_attn_sinks,
                                 v_ssm_norm_w, v_norm2_w, v_final_norm_w)]
    small_g, sd, smn, svn = _adam_small(tot, small_g, small_w, small_m, small_v)
    g_ada, d_ada, m_ada, v_ada = _adam_w_ada(gat, allv, s_arr, w_ada[0], m_w_ada[0], v_w_ada[0], 256)

    order = ["w_ada", "b_ada", "norm1_w", "w_in", "conv_w", "conv_b", "dt_bias", "a_log", "d_skip", "attn_sinks",
             "ssm_norm_w", "w_out", "norm2_w", "w_gate_up", "w_down", "final_norm_w"]
    shapes = dict(w_ada=w_ada.shape, b_ada=b_ada.shape, norm1_w=norm1_w.shape, w_in=w_in.shape, conv_w=conv_w.shape,
                  conv_b=conv_b.shape, dt_bias=dt_bias.shape, a_log=a_log.shape, d_skip=d_skip.shape,
                  attn_sinks=attn_sinks.shape, ssm_norm_w=ssm_norm_w.shape, w_out=w_out.shape, norm2_w=norm2_w.shape,
                  w_gate_up=w_gate_up.shape, w_down=w_down.shape, final_norm_w=final_norm_w.shape)
    grads = dict(w_ada=g_ada, w_in=g_in_s, w_out=g_out_s, w_gate_up=g_gu_s, w_down=g_dn_s)
    deltas = dict(w_ada=d_ada, w_in=d_in, w_out=d_out, w_gate_up=d_gu, w_down=d_dn)
    new_m = dict(w_ada=m_ada, w_in=m_in, w_out=m_out, w_gate_up=m_gu, w_down=m_dn)
    new_v = dict(w_ada=v_ada, w_in=v_in, w_out=v_out, w_gate_up=v_gu, w_down=v_dn)
    for i, nme in enumerate(small_names):
        grads[nme], deltas[nme], new_m[nme], new_v[nme] = small_g[i], sd[i], smn[i], svn[i]
    outs = [loss, grad_x[None]]
    for table in (grads, deltas, new_m, new_v):
        outs += [table[nme].reshape(shapes[nme]) for nme in order]
    return tuple(outs)
```

```python
import functools
import math

import jax
import jax.numpy as jnp
from jax import lax
from jax.experimental import pallas as pl
from jax.experimental.pallas import tpu as pltpu

F32 = jnp.float32
BF16 = jnp.bfloat16
HI = lax.Precision.HIGHEST
MESH = pl.DeviceIdType.MESH

D = 1024
HD = 64
AW = 512
SW = 512
NST = 128
CONVK = 4
CONVC = 1024
BLK = 128
CPS = 8
SSD_FWD_CPS = 8
ATTN_BPS = 8
IN_PROJ = 2312
IN_PAD = 2432
IN_SH = IN_PROJ // 4
DFF = 2816
GU_SH = 1408
FF_SPLITS = ((0, 1536), (1536, 2816))
EPS = 1e-6
NEG = -1e30
LR, B1, B2, AEPS, WD, STEP = 0.001, 0.9, 0.999, 1e-08, 0.01, 10
VMEM_LIMIT = 58 * 1024 * 1024


def _cp(*sem):
    return pltpu.CompilerParams(dimension_semantics=sem or None, vmem_limit_bytes=VMEM_LIMIT)


def _dot(a, b):
    return jnp.dot(a, b, preferred_element_type=F32)


def _dot_nt(a, b):
    return lax.dot_general(a, b, (((1,), (1,)), ((), ())), preferred_element_type=F32)


def _dot_tn(a, b):
    return lax.dot_general(a, b, (((0,), (0,)), ((), ())), preferred_element_type=F32)


def _dot_hi(a, b):
    return jnp.dot(a, b, precision=HI, preferred_element_type=F32)


def _sigmoid(x):
    return 1.0 / (1.0 + jnp.exp(-x))


def _iota(shape, dim):
    return lax.broadcasted_iota(jnp.int32, shape, dim)


def _load_resident(hbm_ref, vmem_ref, sem):
    @pl.when(pl.program_id(0) == 0)
    def _():
        cp = pltpu.make_async_copy(hbm_ref, vmem_ref, sem)
        cp.start()
        cp.wait()


def _swap32(t):
    lane = _iota(t.shape, 1)
    return jnp.where((lane & 63) < 32, pltpu.roll(t, 96, 1), pltpu.roll(t, 32, 1))


def _rope_fwd(t, cos, sin_s):
    return t * cos + _swap32(t) * sin_s


def _rope_bwd(t, cos, sin_s):
    return t * cos - _swap32(t) * sin_s


DEP_SPEC = pl.BlockSpec((8, 128), lambda *_: (0, 0))


def _rope_tables(pos_row, inv_freq_col, tm, dep):
    T = pos_row.shape[1]
    lane, row = jnp.arange(128)[None, :], jnp.arange(96)[:, None]
    pick = (lane % 32) == (row % 32)
    sel_cos = pick.astype(BF16)
    sel_sin = jnp.where(pick, jnp.where(lane % 64 < 32, -1.0, 1.0), 0.0).astype(BF16)

    def body(p_ref, f_ref, sc_ref, ss_ref, dep_ref, cos_ref, sin_ref):
        ang = f_ref[...] * p_ref[...].astype(F32)
        cos_ref[...] = _dot_tn(_pieces(jnp.cos(ang), 3, 0), sc_ref[...])
        sin_ref[...] = _dot_tn(_pieces(jnp.sin(ang), 3, 0), ss_ref[...])

    full = lambda a: pl.BlockSpec(a.shape, lambda i: (0,) * a.ndim)
    return pl.pallas_call(
        body, name="rope_tables", grid=(T // tm,),
        in_specs=[pl.BlockSpec((1, tm), lambda i: (0, i)), full(inv_freq_col), full(sel_cos), full(sel_sin), DEP_SPEC],
        out_specs=[pl.BlockSpec((tm, 128), lambda i: (i, 0))] * 2,
        out_shape=[jax.ShapeDtypeStruct((T, 128), F32)] * 2,
        compiler_params=_cp("parallel"),
    )(pos_row, inv_freq_col, sel_cos, sel_sin, dep)


def _in_proj_fwd(x, cos, sin_s, mod6, norm1_w, w_pad, tm, dep):
    T = x.shape[0]

    def body(x_ref, cos_ref, sin_ref, mod_ref, nw_ref, w_hbm, dep_ref, qkv_ref, z_ref, xbc_ref, dt_ref, h_ref, w_vmem,
             sem):
        _load_resident(w_hbm, w_vmem, sem)
        xv = x_ref[...]
        r = lax.rsqrt(jnp.mean(xv * xv, axis=-1, keepdims=True) + EPS)
        h = (xv * r * nw_ref[...]) * (1.0 + mod_ref[1:2, :]) + mod_ref[0:1, :]
        hb = h.astype(BF16)
        h_ref[...] = hb
        proj = _dot(hb, w_vmem[...])
        cs, sn = cos_ref[...], sin_ref[...]
        for j in range(5):
            qkv_ref[:, 128 * j:128 * (j + 1)] = _rope_fwd(proj[:, 128 * j:128 * (j + 1)], cs, sn).astype(BF16)
        qkv_ref[:, 640:768] = proj[:, 640:768].astype(BF16)
        z_ref[...] = proj[:, 768:1280]
        xbc_ref[...] = proj[:, 1280:2304]
        dt_ref[...] = proj[:, 2304:2432]

    row = lambda w: pl.BlockSpec((tm, w), lambda i: (i, 0))
    full = lambda a: pl.BlockSpec(a.shape, lambda i: (0,) * a.ndim)
    return pl.pallas_call(
        body, name="in_proj_fwd", grid=(T // tm,),
        in_specs=[row(D), row(128), row(128), full(mod6), full(norm1_w), pl.BlockSpec(memory_space=pl.ANY), DEP_SPEC],
        out_specs=[row(768), row(512), row(1024), row(128), row(D)],
        out_shape=[jax.ShapeDtypeStruct((T, 768), BF16), jax.ShapeDtypeStruct((T, 512), F32),
                   jax.ShapeDtypeStruct((T, 1024), F32), jax.ShapeDtypeStruct((T, 128), F32),
                   jax.ShapeDtypeStruct((T, D), BF16)],
        scratch_shapes=[pltpu.VMEM((D, IN_PAD), BF16), pltpu.SemaphoreType.DMA],
        compiler_params=_cp("arbitrary"),
    )(x, cos, sin_s, mod6, norm1_w, w_pad, dep)


def _head_variants(pair, j):
    lane = _iota(pair.shape, 1)
    lo = lane < 64
    kv = j // 2
    ev = jnp.where(lo, pair, 0.0)
    od = jnp.where(lo, 0.0, pair)
    if kv == 0:
        od = pltpu.roll(od, 64, 1)
    else:
        ev = pltpu.roll(ev, 64, 1)
    return ev.astype(BF16), od.astype(BF16)


def _kv_variants(vcat):
    lane = _iota(vcat.shape, 1)
    lo = lane < 64
    v0 = jnp.where(lo, vcat, 0.0)
    v1 = jnp.where(lo, 0.0, vcat)
    out = {
        (0, 0): v0, (0, 1): pltpu.roll(v0, 64, 1),
        (1, 0): pltpu.roll(v1, 64, 1), (1, 1): v1,
    }
    return {k: v.astype(BF16) for k, v in out.items()}


def _fold_masks(n):
    upper = _iota((BLK, BLK), 1) > _iota((BLK, BLK), 0)
    return upper, upper & (n == 0)


def _attn_fwd(qkv, sinks):
    CPS = ATTN_BPS
    T = qkv.shape[0]
    nsteps = T // (CPS * BLK)

    def body(sink_ref, q_ref, kc_ref, kp_ref, vc_ref, vp_ref, o_ref, lse_ref):
        for sub in range(CPS):
            rows, before = slice(BLK * sub, BLK * (sub + 1)), slice(BLK * (sub - 1), BLK * sub)
            block(pl.program_id(0) * CPS + sub, sink_ref, q_ref.at[rows, :], kc_ref.at[rows, :],
                  kp_ref if sub == 0 else kc_ref.at[before, :], vc_ref.at[rows, :],
                  vp_ref if sub == 0 else vc_ref.at[before, :], o_ref.at[rows, :], lse_ref.at[rows, :])

    def block(n, sink_ref, q_ref, kc_ref, kp_ref, vc_ref, vp_ref, o_ref, lse_ref):
        vpv = _kv_variants(vp_ref[...].astype(F32))
        vcv = _kv_variants(vc_ref[...].astype(F32))
        q_all = jnp.concatenate(
            [v for j in range(4) for v in _head_variants(q_ref[:, 128 * j:128 * (j + 1)].astype(F32), j)], axis=0)
        s_prev = _dot_nt(q_all, kp_ref[...])
        s_cur = _dot_nt(q_all, kc_ref[...])
        upper, dead = _fold_masks(n)
        lane = _iota((BLK, 128), 1)
        lse_acc = jnp.zeros((BLK, 128), F32)
        for jj in range(4):
            acc = jnp.zeros((BLK, 128), F32)
            for par in range(2):
                h = 2 * jj + par
                rows = slice(h * BLK, (h + 1) * BLK)
                sink = sink_ref[0, h]
                s = jnp.where(dead, NEG, jnp.where(upper, s_prev[rows], s_cur[rows]) * 0.125)
                m = jnp.maximum(jnp.max(s, axis=1, keepdims=True), sink)
                p = jnp.exp(s - m)
                den = jnp.sum(p, axis=1, keepdims=True) + jnp.exp(sink - m)
                pn = p * (1.0 / den)
                acc = (acc + _dot(jnp.where(upper, pn, 0.0).astype(BF16), vpv[(jj // 2, par)])
                       + _dot(jnp.where(upper, 0.0, pn).astype(BF16), vcv[(jj // 2, par)]))
                lse_acc = jnp.where(lane == h, m + jnp.log(den), lse_acc)
            o_ref[:, 128 * jj:128 * (jj + 1)] = acc.astype(BF16)
        lse_ref[...] = lse_acc

    RB = CPS * BLK
    prev = lambda n: jnp.maximum(n * CPS - 1, 0)
    return pl.pallas_call(
        body, name="attn_fwd", grid=(nsteps,),
        in_specs=[pl.BlockSpec(memory_space=pltpu.SMEM),
                  pl.BlockSpec((RB, 512), lambda n: (n, 0)),
                  pl.BlockSpec((RB, 128), lambda n: (n, 4)),
                  pl.BlockSpec((BLK, 128), lambda n: (prev(n), 4)),
                  pl.BlockSpec((RB, 128), lambda n: (n, 5)),
                  pl.BlockSpec((BLK, 128), lambda n: (prev(n), 5))],
        out_specs=[pl.BlockSpec((RB, 512), lambda n: (n, 0)), pl.BlockSpec((RB, 128), lambda n: (n, 0))],
        out_shape=[jax.ShapeDtypeStruct((T, 512), BF16), jax.ShapeDtypeStruct((T, 128), F32)],
        compiler_params=_cp("parallel"),
    )(sinks, qkv, qkv, qkv, qkv, qkv)


def _attn_bwd(qkv, sinks, lse, dmix, cos, sin_s, dep):
    T = qkv.shape[0]
    nb = T // BLK

    def body(sink_ref, q_ref, kc_ref, kp_ref, vc_ref, vp_ref, lse_ref, do_ref, cq_ref, sq_ref, ck_ref, sk_ref,
             dep_ref, out_ref, ds_ref, dq_car, dk_car, dv_car):
        n = pl.program_id(0)
        lane = _iota((BLK, 128), 1)

        @pl.when(n == 0)
        def _():
            ds_ref[...] = jnp.zeros_like(ds_ref)
            dq_car[...] = jnp.zeros_like(dq_car)
            dk_car[...] = jnp.zeros_like(dk_car)
            dv_car[...] = jnp.zeros_like(dv_car)

        @pl.when(n < nb)
        def _():
            kp, kc, vp, vc = kp_ref[...], kc_ref[...], vp_ref[...], vc_ref[...]
            kpv = _kv_variants(kp.astype(F32))
            kcv = _kv_variants(kc.astype(F32))
            lse_v = lse_ref[...]
            q_all = jnp.concatenate(
                [v for j in range(4) for v in _head_variants(q_ref[:, 128 * j:128 * (j + 1)].astype(F32), j)], axis=0)
            do_all = jnp.concatenate(
                [v for j in range(4) for v in _head_variants(do_ref[:, 128 * j:128 * (j + 1)], j)], axis=0)
            s_prev, s_cur = _dot_nt(q_all, kp), _dot_nt(q_all, kc)
            dp_prev, dp_cur = _dot_nt(do_all, vp), _dot_nt(do_all, vc)
            upper, dead = _fold_masks(n)
            out_ref[:, 0:512] = dq_car[...]
            dsk = jnp.zeros((1, 128), F32)
            ds_u, ds_l, p_u, p_l = [], [], [], []
            for jj in range(4):
                dq_acc = jnp.zeros((BLK, 128), F32)
                for par in range(2):
                    h = 2 * jj + par
                    rows = slice(h * BLK, (h + 1) * BLK)
                    lse_h = jnp.sum(jnp.where(lane == h, lse_v, 0.0), axis=1, keepdims=True)
                    s = jnp.where(dead, NEG, jnp.where(upper, s_prev[rows], s_cur[rows]) * 0.125)
                    p = jnp.exp(s - lse_h)
                    dp = jnp.where(upper, dp_prev[rows], dp_cur[rows])
                    delta = jnp.sum(p * dp, axis=1, keepdims=True)
                    ds = p * (dp - delta) * 0.125
                    dsu, dsl = jnp.where(upper, ds, 0.0).astype(BF16), jnp.where(upper, 0.0, ds).astype(BF16)
                    dq_acc = dq_acc + _dot(dsu, kpv[(jj // 2, par)]) + _dot(dsl, kcv[(jj // 2, par)])
                    ds_u.append(dsu)
                    ds_l.append(dsl)
                    p_u.append(jnp.where(upper, p, 0.0).astype(BF16))
                    p_l.append(jnp.where(upper, 0.0, p).astype(BF16))
                    dsk = dsk + jnp.where(lane[0:1] == h, -jnp.sum(jnp.exp(sink_ref[0, h] - lse_h) * delta), 0.0)
                dq_car[:, 128 * jj:128 * (jj + 1)] = _rope_bwd(dq_acc, cq_ref[...], sq_ref[...]).astype(BF16)
            stack = lambda parts: jnp.concatenate(parts, axis=0)
            dk_prev, dk_cur = _dot_tn(stack(ds_u), q_all), _dot_tn(stack(ds_l), q_all)
            dv_prev, dv_cur = _dot_tn(stack(p_u), do_all), _dot_tn(stack(p_l), do_all)
            ds_ref[...] += dsk
            out_ref[:, 512:640] = _rope_bwd(dk_car[...] + dk_prev, ck_ref[...], sk_ref[...]).astype(BF16)
            out_ref[:, 640:768] = (dv_car[...] + dv_prev).astype(BF16)
            dk_car[...] = dk_cur
            dv_car[...] = dv_cur

        @pl.when(n == nb)
        def _():
            out_ref[:, 0:512] = dq_car[...]
            out_ref[:, 512:640] = _rope_bwd(dk_car[...], ck_ref[...], sk_ref[...]).astype(BF16)
            out_ref[:, 640:768] = dv_car[...].astype(BF16)

    cur = lambda n: jnp.minimum(n, nb - 1)
    prev = lambda n: jnp.maximum(cur(n) - 1, 0)
    outb = lambda n: jnp.maximum(n - 1, 0)
    return pl.pallas_call(
        body, name="attn_bwd", grid=(nb + 1,),
        in_specs=[pl.BlockSpec(memory_space=pltpu.SMEM),
                  pl.BlockSpec((BLK, 512), lambda n: (cur(n), 0)),
                  pl.BlockSpec((BLK, 128), lambda n: (cur(n), 4)),
                  pl.BlockSpec((BLK, 128), lambda n: (prev(n), 4)),
                  pl.BlockSpec((BLK, 128), lambda n: (cur(n), 5)),
                  pl.BlockSpec((BLK, 128), lambda n: (prev(n), 5)),
                  pl.BlockSpec((BLK, 128), lambda n: (cur(n), 0)),
                  pl.BlockSpec((BLK, 512), lambda n: (cur(n), 0)),
                  pl.BlockSpec((BLK, 128), lambda n: (cur(n), 0)),
                  pl.BlockSpec((BLK, 128), lambda n: (cur(n), 0)),
                  pl.BlockSpec((BLK, 128), lambda n: (outb(n), 0)),
                  pl.BlockSpec((BLK, 128), lambda n: (outb(n), 0)), DEP_SPEC],
        out_specs=[pl.BlockSpec((BLK, 768), lambda n: (outb(n), 0)), pl.BlockSpec((1, 128), lambda n: (0, 0))],
        out_shape=[jax.ShapeDtypeStruct((T, 768), BF16), jax.ShapeDtypeStruct((1, 128), F32)],
        scratch_shapes=[pltpu.VMEM((BLK, 512), BF16), pltpu.VMEM((BLK, 128), F32), pltpu.VMEM((BLK, 128), F32)],
        compiler_params=_cp("arbitrary"),
    )(sinks, qkv, qkv, qkv, qkv, qkv, lse, dmix, cos, sin_s, cos, sin_s, dep)


def _ssd_mats():
    e = jnp.arange(SW)[None, :] // HD == jnp.arange(128)[:, None]
    tri = jnp.arange(BLK)[None, :] <= jnp.arange(BLK)[:, None]
    return (jnp.tile(e, (3, 1)).astype(BF16), jnp.tile(e.T, (2, 1)).astype(BF16),
            jnp.tile(tri, (1, 3)).astype(BF16), jnp.tile(tri.T, (1, 3)).astype(BF16))


def _pieces(x, n, axis):
    out, r = [], x
    for i in range(n):
        p = r.astype(BF16)
        out.append(p)
        if i + 1 < n:
            r = r - p.astype(F32)
    return jnp.concatenate(out, axis=axis)


def _expand(x, e3):
    return _dot(_pieces(x, 3, 1), e3)


def _head_sums(x, et2):
    return _dot(_pieces(x, 2, 1), et2)


def _run_sum(tri3, x):
    return _dot(tri3, _pieces(x, 3, 0))


def _shift_down(u, tail, j):
    rolled = pltpu.roll(u, j, 0)
    first = jnp.where(_iota(tail.shape, 0) < j, pltpu.roll(tail, j, 0), rolled[0:8])
    return jnp.concatenate([first, rolled[8:]], axis=0)


def _shift_up(d, head, j):
    rolled = pltpu.roll(d, BLK - j, 0)
    last = jnp.where(_iota(head.shape, 0) >= 8 - j, pltpu.roll(head, 8 - j, 0), rolled[BLK - 8:])
    return jnp.concatenate([rolled[:BLK - 8], last], axis=0)


def _ssd_parts(dtr, dtb, alog, e3, tril3):
    xx = dtr + dtb
    dt = jnp.maximum(xx, 0.0) + jnp.log(1.0 + jnp.exp(-jnp.abs(xx)))
    a_neg = -jnp.exp(alog)
    tril = _iota((BLK, BLK), 1) <= _iota((BLK, BLK), 0)
    cs = _run_sum(tril3, dt * a_neg)
    csx = _expand(cs, e3)
    last = csx[BLK - 1:BLK, :]
    return dict(xx=xx, dt=dt, a_neg=a_neg, tril=tril, cs=cs, cs_t=cs.T,
                ecsx=jnp.exp(csx), dtex=jnp.exp(last - csx), cdx=jnp.exp(last), dtx=_expand(dt, e3))


def _decay(parts, h):
    seg = parts["cs"][:, h:h + 1] - parts["cs_t"][h:h + 1, :]
    return jnp.exp(jnp.where(parts["tril"], seg, NEG))


def _group_cols(a, g):
    return a[:, 256 * g:256 * (g + 1)]


def _ssd_fwd(xbc, z, dtr, conv_w, conv_b, dtb, alog, dskx, ssm_w, mats, dep):
    CPS = SSD_FWD_CPS
    T = xbc.shape[0]
    nc = T // BLK

    def body(u_ref, tail_ref, z_ref, dtr_ref, cw_ref, cb_ref, dtb_ref, al_ref, dk_ref, sw_ref, e3_ref, tril3_ref,
             dep_ref, yn_ref, yp_ref, st_ref, co_ref, s_scr):
        n = pl.program_id(0)

        @pl.when(n == 0)
        def _():
            s_scr[...] = jnp.zeros_like(s_scr)

        lane = _iota((BLK, 128), 1)
        lo = lane < 64
        for sub in range(CPS):
            rows = slice(BLK * sub, BLK * (sub + 1))
            u = u_ref[rows, :]
            tail = jnp.where(n > 0, tail_ref[...], 0.0) if sub == 0 else u_ref[BLK * sub - 8:BLK * sub, :]
            co = cb_ref[...] + cw_ref[3:4, :] * u
            for j in range(1, CONVK):
                co = co + cw_ref[3 - j:4 - j, :] * _shift_down(u, tail, j)
            co_ref[rows, :] = co
            xc = co * _sigmoid(co)
            pt = _ssd_parts(dtr_ref[rows, :], dtb_ref[...], al_ref[...], e3_ref[...], tril3_ref[...])
            xs = xc[:, :SW]
            bm = [xc[:, 512:640].astype(BF16), xc[:, 640:768].astype(BF16)]
            cm = [xc[:, 768:896].astype(BF16), xc[:, 896:1024].astype(BF16)]
            s_in = s_scr[...]
            st_ref[sub] = s_in
            xdt = xs * pt["dtx"]
            xde = (xdt * pt["dtex"]).astype(BF16)
            ys, s_new = [], []
            for g in range(2):
                cb = _dot_nt(cm[g], bm[g])
                yoff = _dot(cm[g], _group_cols(s_in, g).astype(BF16))
                s_new.append(_dot_tn(bm[g], _group_cols(xde, g)))
                for jj in range(2):
                    j = 2 * g + jj
                    chunk = xdt[:, 128 * j:128 * (j + 1)]
                    g_ev = (cb * _decay(pt, 2 * j)).astype(BF16)
                    g_od = (cb * _decay(pt, 2 * j + 1)).astype(BF16)
                    yd = (_dot(g_ev, jnp.where(lo, chunk, 0.0).astype(BF16))
                          + _dot(g_od, jnp.where(lo, 0.0, chunk).astype(BF16)))
                    ys.append(yd + yoff[:, 128 * jj:128 * (jj + 1)] * pt["ecsx"][:, 128 * j:128 * (j + 1)])
            y = jnp.concatenate(ys, axis=1) + xs * dk_ref[...]
            s_scr[...] = s_in * pt["cdx"] + jnp.concatenate(s_new, axis=1)
            yp_ref[rows, :] = y
            zv = z_ref[rows, :]
            yz = y * (zv * _sigmoid(zv))
            outs = []
            for g in range(2):
                yg = _group_cols(yz, g)
                outs.append(yg * lax.rsqrt(jnp.mean(yg * yg, axis=-1, keepdims=True) + EPS))
            yn_ref[rows, :] = (jnp.concatenate(outs, axis=1) * sw_ref[...]).astype(BF16)

    e3, _, tril3, _ = mats
    RB = CPS * BLK
    tail8 = lambda n: jnp.maximum(n * (RB // 8) - 1, 0)
    full = lambda a: pl.BlockSpec(a.shape, lambda n: (0,) * a.ndim)
    return pl.pallas_call(
        body, name="ssd_fwd", grid=(nc // CPS,),
        in_specs=[pl.BlockSpec((RB, CONVC), lambda n: (n, 0)), pl.BlockSpec((8, CONVC), lambda n: (tail8(n), 0)),
                  pl.BlockSpec((RB, SW), lambda n: (n, 0)), pl.BlockSpec((RB, 128), lambda n: (n, 0)),
                  full(conv_w), full(conv_b), full(dtb), full(alog), full(dskx), full(ssm_w), full(e3), full(tril3),
                  DEP_SPEC],
        out_specs=[pl.BlockSpec((RB, SW), lambda n: (n, 0)), pl.BlockSpec((RB, SW), lambda n: (n, 0)),
                   pl.BlockSpec((CPS, NST, SW), lambda n: (n, 0, 0)), pl.BlockSpec((RB, CONVC), lambda n: (n, 0))],
        out_shape=[jax.ShapeDtypeStruct((T, SW), BF16), jax.ShapeDtypeStruct((T, SW), F32),
                   jax.ShapeDtypeStruct((nc, NST, SW), F32), jax.ShapeDtypeStruct((T, CONVC), F32)],
        scratch_shapes=[pltpu.VMEM((NST, SW), F32)],
        compiler_params=_cp("arbitrary"),
    )(xbc, xbc, z, dtr, conv_w, conv_b, dtb, alog, dskx, ssm_w, e3, tril3, dep)


def _ssd_bwd(xbc, co_all, z, dtr, ypre, states, dmix, conv_w, dtb, alog, dskx, ssm_w, mats, dep):
    T = xbc.shape[0]
    nsteps = T // (CPS * BLK)

    def body(*refs):
        per_chunk, consts, out_ref, carried = refs[:7], refs[7:16], refs[17], refs[18:]
        i = pl.program_id(0)

        @pl.when(i == 0)
        def _():
            for r in carried:
                r[...] = jnp.zeros_like(r)

        for sub in reversed(range(CPS)):
            rows = slice(BLK * sub, BLK * (sub + 1))
            views = [r.at[sub:sub + 1] if k == 5 else r.at[rows, :] for k, r in enumerate(per_chunk)]
            chunk(*views, *consts, out_ref.at[rows, :], *carried)

        @pl.when(i == nsteps - 1)
        def _():
            dsk_ref, dskx_scr = carried[3], carried[8]
            dsk_ref[...] = _head_sums(jnp.broadcast_to(dskx_scr[...], (8, SW)), consts[6][...])[0:1]

    def chunk(u_ref, co_ref, z_ref, dtr_ref, yp_ref, st_ref, dyn_ref, cw_ref, dtb_ref, al_ref, dk_ref, sw_ref,
              e3_ref, et2_ref, tril3_ref, triu3_ref,
              out_ref, dcw_ref, dcb_ref, dsw_ref, dsk_ref, ddtb_ref, dav_ref, ds_scr, dco_scr, dskx_scr):
        co = co_ref[...]
        sg = _sigmoid(co)
        xc = co * sg
        pt = _ssd_parts(dtr_ref[...], dtb_ref[...], al_ref[...], e3_ref[...], tril3_ref[...])
        dtx, ecsx, dtex, cdx = pt["dtx"], pt["ecsx"], pt["dtex"], pt["cdx"]
        xs = xc[:, :SW]
        bm = [xc[:, 512:640].astype(BF16), xc[:, 640:768].astype(BF16)]
        cm = [xc[:, 768:896].astype(BF16), xc[:, 896:1024].astype(BF16)]
        s_in = st_ref[0]
        ds_out = ds_scr[...]
        e_t = et2_ref[...]

        zv = z_ref[...]
        sz = _sigmoid(zv)
        silu_z = zv * sz
        ypre = yp_ref[...]
        yz = ypre * silu_z
        dyn = dyn_ref[...]
        sw = sw_ref[...]
        dyz, yns = [], []
        for g in range(2):
            yg = _group_cols(yz, g)
            r = lax.rsqrt(jnp.mean(yg * yg, axis=-1, keepdims=True) + EPS)
            yn = yg * r
            dg = _group_cols(dyn, g) * _group_cols(sw, g)
            dyz.append(r * (dg - yn * jnp.mean(dg * yn, axis=-1, keepdims=True)))
            yns.append(yn)
        dyz = jnp.concatenate(dyz, axis=1)
        dsw_ref[...] += jnp.sum(dyn * jnp.concatenate(yns, axis=1), axis=0, keepdims=True)
        dy = dyz * silu_z
        dz = dyz * ypre * (sz * (1.0 + zv * (1.0 - sz)))

        xdt = xs * dtx
        xdt_b = xdt.astype(BF16)
        edy = (ecsx * dy).astype(BF16)
        xde = (xdt * dtex).astype(BF16)
        lane = _iota((BLK, 128), 1)
        lo = lane < 64
        row8 = _iota((8, 128), 0)
        dcs = jnp.zeros((BLK, 128), F32)
        col_rows = jnp.zeros((8, 128), F32)
        dxdt, bds, yoff, dbs, dcs_g, ds_new = [], [], [], [], [], []
        for g in range(2):
            s_g = _group_cols(s_in, g).astype(BF16)
            dso_g = _group_cols(ds_out, g).astype(BF16)
            cb = _dot_nt(cm[g], bm[g])
            bds.append(_dot(bm[g], dso_g))
            yoff.append(_dot(cm[g], s_g))
            dcb_g = jnp.zeros((BLK, BLK), F32)
            for jj in range(2):
                j = 2 * g + jj
                dy_c = dy[:, 128 * j:128 * (j + 1)]
                xdt_c = xdt_b[:, 128 * j:128 * (j + 1)]
                acc = jnp.zeros((BLK, 128), F32)
                for par in range(2):
                    h = 2 * j + par
                    lm = _decay(pt, h)
                    gm = cb * lm
                    dy_m = (jnp.where(lo, dy_c, 0.0) if par == 0 else jnp.where(lo, 0.0, dy_c)).astype(BF16)
                    dg_h = _dot_nt(dy_m, xdt_c)
                    w_h = dg_h * gm
                    dcs = dcs + jnp.where(lane == h, jnp.sum(w_h, axis=1, keepdims=True), 0.0)
                    col_rows = col_rows + jnp.where(row8 == h, jnp.sum(w_h, axis=0, keepdims=True), 0.0)
                    dcb_g = dcb_g + dg_h * lm
                    acc = acc + _dot_tn(gm.astype(BF16), dy_m)
                dxdt.append(acc)
            dcb_b = dcb_g.astype(BF16)
            dcs_g.append(_dot(dcb_b, bm[g]) + _dot_nt(_group_cols(edy, g), s_g))
            dbs.append(_dot_tn(dcb_b, cm[g]) + _dot_nt(_group_cols(xde, g), dso_g))
            ds_new.append(_dot_tn(cm[g], _group_cols(edy, g)))
        bds = jnp.concatenate(bds, axis=1)
        yoff = jnp.concatenate(yoff, axis=1) * ecsx
        dxdt = jnp.concatenate(dxdt, axis=1) + dtex * bds
        ds_scr[...] = cdx * ds_out + jnp.concatenate(ds_new, axis=1)

        t_m = _head_sums(dtex * xdt * bds, e_t)
        colsum_t = jnp.concatenate([col_rows, jnp.zeros((BLK - 8, 128), F32)], axis=0).T
        cd = jnp.exp(pt["cs"][BLK - 1:BLK, :])
        sds = jnp.sum(s_in * ds_out, axis=0, keepdims=True)
        last_row = jnp.sum(t_m, axis=0, keepdims=True) + cd * _head_sums(jnp.broadcast_to(sds, (8, SW)), e_t)[0:1]
        dcs = dcs - colsum_t + _head_sums(dy * yoff, e_t) - t_m
        dcs = dcs + jnp.where(_iota((BLK, 128), 0) == BLK - 1, last_row, 0.0)
        da = _run_sum(triu3_ref[...], dcs)
        dt = pt["dt"]
        ddt = da * pt["a_neg"] + _head_sums(dxdt * xs, e_t)
        dav_ref[...] += jnp.sum(da * dt, axis=0, keepdims=True)
        ddtr = ddt * _sigmoid(pt["xx"])
        ddtb_ref[...] += jnp.sum(ddtr, axis=0, keepdims=True)
        dxs = dxdt * dtx + dy * dk_ref[...]
        dskx_scr[...] += jnp.sum(dy * xs, axis=0, keepdims=True)
        dxc = jnp.concatenate([dxs, dbs[0], dbs[1], dcs_g[0], dcs_g[1]], axis=1)
        dco = dxc * (sg * (1.0 + co * (1.0 - sg)))

        dcb_ref[...] += jnp.sum(dco, axis=0, keepdims=True)
        u = u_ref[...]
        head = dco_scr[...]
        du = jnp.zeros_like(dco)
        for j in range(CONVK):
            up_j = dco if j == 0 else _shift_up(dco, head, j)
            dcw_ref[3 - j:4 - j, :] += jnp.sum(up_j * u, axis=0, keepdims=True)
            du = du + cw_ref[3 - j:4 - j, :] * up_j
        dco_scr[...] = dco[0:8]
        out_ref[:, 0:512] = dz.astype(BF16)
        out_ref[:, 512:1536] = du.astype(BF16)
        out_ref[:, 1536:1664] = ddtr.astype(BF16)

    e3, et2, tril3, triu3 = mats
    RB = CPS * BLK
    rev = lambda i: nsteps - 1 - i
    full = lambda a: pl.BlockSpec(a.shape, lambda i: (0,) * a.ndim)
    acc = lambda r, c: pl.BlockSpec((r, c), lambda i: (0, 0))
    return pl.pallas_call(
        body, name="ssd_bwd", grid=(nsteps,),
        in_specs=[pl.BlockSpec((RB, CONVC), lambda i: (rev(i), 0)), pl.BlockSpec((RB, CONVC), lambda i: (rev(i), 0)),
                  pl.BlockSpec((RB, SW), lambda i: (rev(i), 0)), pl.BlockSpec((RB, 128), lambda i: (rev(i), 0)),
                  pl.BlockSpec((RB, SW), lambda i: (rev(i), 0)), pl.BlockSpec((CPS, NST, SW), lambda i: (rev(i), 0, 0)),
                  pl.BlockSpec((RB, SW), lambda i: (rev(i), 1)),
                  full(conv_w), full(dtb), full(alog), full(dskx), full(ssm_w),
                  full(e3), full(et2), full(tril3), full(triu3), DEP_SPEC],
        out_specs=[pl.BlockSpec((RB, 1664), lambda i: (rev(i), 0)),
                   acc(CONVK, CONVC), acc(1, CONVC), acc(1, SW), acc(1, 128), acc(1, 128), acc(1, 128)],
        out_shape=[jax.ShapeDtypeStruct((T, 1664), BF16),
                   jax.ShapeDtypeStruct((CONVK, CONVC), F32), jax.ShapeDtypeStruct((1, CONVC), F32),
                   jax.ShapeDtypeStruct((1, SW), F32), jax.ShapeDtypeStruct((1, 128), F32),
                   jax.ShapeDtypeStruct((1, 128), F32), jax.ShapeDtypeStruct((1, 128), F32)],
        scratch_shapes=[pltpu.VMEM((NST, SW), F32), pltpu.VMEM((8, CONVC), F32), pltpu.VMEM((1, SW), F32)],
        compiler_params=_cp("arbitrary"),
    )(xbc, co_all, z, dtr, ypre, states, dmix, conv_w, dtb, alog, dskx, ssm_w, e3, et2, tril3, triu3, dep)


def _mix_ffn(x, attn, ynorm, tgt, mod6, norm2_w, final_w, w_out, w_gu, w_gu_own, s_arr, w_dn, tm):
    T = x.shape[0]
    nt = T // tm

    def body(x_ref, a_ref, y_ref, t_ref, mod_ref, n2_ref, fw_ref, wo_hbm, wgu_hbm, own_hbm, s_ref, wdn_hbm,
             sq_ref, dmix_ref, dx1_ref, h2_ref, act_ref, df_ref, dgu_ref, do_ref, sm_ref,
             wo, wgu, wdn, sems):
        i = pl.program_id(0)

        @pl.when(i == 0)
        def _():
            cps = [pltpu.make_async_copy(s, d, sems.at[k]) for k, (s, d) in
                   enumerate(((wo_hbm, wo), (wgu_hbm, wgu), (wdn_hbm, wdn)))]
            for c in cps:
                c.start()
            for c in cps:
                c.wait()
            own = pltpu.make_async_copy(
                own_hbm, wgu.at[:, pl.ds(pl.multiple_of(s_ref[0] * GU_SH, 128), GU_SH)], sems.at[3])
            own.start()
            own.wait()
            sq_ref[...] = jnp.zeros_like(sq_ref)
            sm_ref[...] = jnp.zeros_like(sm_ref)

        gate1, shift2, scale2, gate2 = mod_ref[2:3, :], mod_ref[3:4, :], mod_ref[4:5, :], mod_ref[5:6, :]
        n2w, fw = n2_ref[...], fw_ref[...]
        o = _dot(a_ref[...], wo[0:AW, :]) + _dot(y_ref[...], wo[AW:D, :])
        x1 = x_ref[...] + gate1 * o
        r2 = lax.rsqrt(jnp.mean(x1 * x1, axis=-1, keepdims=True) + EPS)
        xh2 = x1 * r2
        n2 = xh2 * n2w
        h2b = (n2 * (1.0 + scale2) + shift2).astype(BF16)
        h2_ref[...] = h2b
        f = jnp.zeros((tm, D), F32)
        saved = []
        for a, b in FF_SPLITS:
            gp = _dot(h2b, wgu[:, a:b])
            upj = _dot(h2b, wgu[:, DFF + a:DFF + b])
            sg = _sigmoid(gp)
            sl = gp * sg
            actb = (sl * upj).astype(BF16)
            act_ref[:, a:b] = actb
            f = f + _dot(actb, wdn[a:b, :])
            saved.append((gp, upj, sg, sl))
        x2 = x1 + gate2 * f
        r3 = lax.rsqrt(jnp.mean(x2 * x2, axis=-1, keepdims=True) + EPS)
        xh3 = x2 * r3
        err = xh3 * fw - t_ref[...]
        sq_ref[...] += jnp.sum(err * err, axis=0, keepdims=True)
        dy = err * (1.0 / D)
        dfw = jnp.sum(dy * xh3, axis=0, keepdims=True)
        dxh3 = dy * fw
        dx2 = r3 * (dxh3 - xh3 * jnp.mean(dxh3 * xh3, axis=-1, keepdims=True))
        dgate2 = jnp.sum(dx2 * f, axis=0, keepdims=True)
        dfb = (dx2 * gate2).astype(BF16)
        df_ref[...] = dfb
        dh2 = jnp.zeros((tm, D), F32)
        for (a, b), (gp, upj, sg, sl) in zip(FF_SPLITS, saved):
            dact = _dot_nt(dfb, wdn[a:b, :])
            dg = (dact * upj * (sg * (1.0 + gp * (1.0 - sg)))).astype(BF16)
            du = (dact * sl).astype(BF16)
            dgu_ref[:, a:b] = dg
            dgu_ref[:, DFF + a:DFF + b] = du
            dh2 = dh2 + _dot_nt(dg, wgu[:, a:b]) + _dot_nt(du, wgu[:, DFF + a:DFF + b])
        dshift2 = jnp.sum(dh2, axis=0, keepdims=True)
        dscale2 = jnp.sum(dh2 * n2, axis=0, keepdims=True)
        dn2 = dh2 * (1.0 + scale2)
        dn2w = jnp.sum(dn2 * xh2, axis=0, keepdims=True)
        dxh2 = dn2 * n2w
        dx1 = dx2 + r2 * (dxh2 - xh2 * jnp.mean(dxh2 * xh2, axis=-1, keepdims=True))
        dx1_ref[...] = dx1
        dgate1 = jnp.sum(dx1 * o, axis=0, keepdims=True)
        dob = (dx1 * gate1).astype(BF16)
        do_ref[...] = dob
        dmix_ref[...] = _dot_nt(dob, wo[...])
        sm_ref[...] += jnp.concatenate(
            [dfw, dn2w, dshift2, dscale2, dgate2, dgate1, jnp.zeros((2, D), F32)], axis=0)

    row = lambda w: pl.BlockSpec((tm, w), lambda i: (i, 0))
    full = lambda a: pl.BlockSpec(a.shape, lambda i: (0,) * a.ndim)
    anyspec = pl.BlockSpec(memory_space=pl.ANY)
    return pl.pallas_call(
        body, name="mix_ffn", grid=(nt,),
        in_specs=[row(D), row(AW), row(SW), row(D), full(mod6), full(norm2_w), full(final_w), anyspec, anyspec, anyspec,
                  pl.BlockSpec(memory_space=pltpu.SMEM), anyspec],
        out_specs=[pl.BlockSpec((1, D), lambda i: (0, 0)), row(D), row(D), row(D),
                   row(DFF), row(D), row(2 * DFF), row(D), pl.BlockSpec((8, D), lambda i: (0, 0))],
        out_shape=[jax.ShapeDtypeStruct((1, D), F32), jax.ShapeDtypeStruct((T, D), F32), jax.ShapeDtypeStruct((T, D), F32),
                   jax.ShapeDtypeStruct((T, D), BF16), jax.ShapeDtypeStruct((T, DFF), BF16),
                   jax.ShapeDtypeStruct((T, D), BF16), jax.ShapeDtypeStruct((T, 2 * DFF), BF16),
                   jax.ShapeDtypeStruct((T, D), BF16), jax.ShapeDtypeStruct((8, D), F32)],
        scratch_shapes=[pltpu.VMEM((D, D), BF16), pltpu.VMEM((D, 2 * DFF), BF16), pltpu.VMEM((DFF, D), BF16),
                        pltpu.SemaphoreType.DMA((4,))],
        compiler_params=_cp("arbitrary"),
    )(x, attn, ynorm, tgt, mod6, norm2_w, final_w, w_out, w_gu, w_gu_own, s_arr, w_dn)


def _in_proj_bwd(x, dx1, dqkv, dzxd, mod6, norm1_w, w_pad, tm, dep):
    T = x.shape[0]

    def body(x_ref, dx1_ref, dq_ref, dz_ref, mod_ref, nw_ref, w_hbm, dep_ref, gx_ref, sm_ref, w_vmem, sem):
        _load_resident(w_hbm, w_vmem, sem)

        @pl.when(pl.program_id(0) == 0)
        def _():
            sm_ref[...] = jnp.zeros_like(sm_ref)

        nw = nw_ref[...]
        scale1 = mod_ref[1:2, :]
        sums = jnp.zeros((8, D), F32)
        for rows in (slice(0, tm // 2), slice(tm // 2, tm)):
            dh = _dot_nt(dq_ref[rows, :], w_vmem[:, 0:768]) + _dot_nt(dz_ref[rows, :], w_vmem[:, 768:IN_PAD])
            xv = x_ref[rows, :]
            r = lax.rsqrt(jnp.mean(xv * xv, axis=-1, keepdims=True) + EPS)
            xh = xv * r
            n1 = xh * nw
            dshift = jnp.sum(dh, axis=0, keepdims=True)
            dscale = jnp.sum(dh * n1, axis=0, keepdims=True)
            dn = dh * (1.0 + scale1)
            dnw = jnp.sum(dn * xh, axis=0, keepdims=True)
            dxh = dn * nw
            gx_ref[rows, :] = dx1_ref[rows, :] + r * (dxh - xh * jnp.mean(dxh * xh, axis=-1, keepdims=True))
            sums = sums + jnp.concatenate([dnw, dshift, dscale, jnp.zeros((5, D), F32)], axis=0)
        sm_ref[...] += sums

    row = lambda w: pl.BlockSpec((tm, w), lambda i: (i, 0))
    full = lambda a: pl.BlockSpec(a.shape, lambda i: (0,) * a.ndim)
    return pl.pallas_call(
        body, name="in_proj_bwd", grid=(T // tm,),
        in_specs=[row(D), row(D), row(768), row(1664), full(mod6), full(norm1_w), pl.BlockSpec(memory_space=pl.ANY),
                  DEP_SPEC],
        out_specs=[row(D), pl.BlockSpec((8, D), lambda i: (0, 0))],
        out_shape=[jax.ShapeDtypeStruct((T, D), F32), jax.ShapeDtypeStruct((8, D), F32)],
        scratch_shapes=[pltpu.VMEM((D, IN_PAD), BF16), pltpu.SemaphoreType.DMA],
        compiler_params=_cp("arbitrary"),
    )(x, dx1, dqkv, dzxd, mod6, norm1_w, w_pad, dep)


def _tn_matmul(a, b, K, N, tt, name, dep):
    T = a.shape[0]
    ja, jb = a.shape[1] // K, b.shape[1] // N
    J = max(ja, jb)

    def body(a_ref, b_ref, dep_ref, o_ref):
        t = pl.program_id(1)
        prod = _dot_tn(a_ref[...], b_ref[...])

        @pl.when(t == 0)
        def _():
            o_ref[0] = prod

        @pl.when(t > 0)
        def _():
            o_ref[0] += prod

    return pl.pallas_call(
        body, name=name, grid=(J, T // tt),
        in_specs=[pl.BlockSpec((tt, K), lambda j, t: (t, j if ja > 1 else 0)),
                  pl.BlockSpec((tt, N), lambda j, t: (t, j if jb > 1 else 0)),
                  pl.BlockSpec((8, 128), lambda j, t: (0, 0))],
        out_specs=pl.BlockSpec((1, K, N), lambda j, t: (j, 0, 0)),
        out_shape=jax.ShapeDtypeStruct((J, K, N), F32),
        compiler_params=_cp("parallel", "arbitrary"),
    )(a, b, dep)


def _accumulate(o_ref, rows, prod):
    @pl.when(pl.program_id(0) == 0)
    def _():
        o_ref[rows, :] = prod

    @pl.when(pl.program_id(0) > 0)
    def _():
        o_ref[rows, :] += prod


def _tn_matmul_rows(a0, a1, b, tt, name, dep):
    T, K = a0.shape
    N = b.shape[1]

    def body(a0_ref, a1_ref, b_ref, dep_ref, o_ref):
        for k, a_ref in enumerate((a0_ref, a1_ref)):
            _accumulate(o_ref, slice(k * K, (k + 1) * K), _dot_tn(a_ref[...], b_ref[...]))

    tile = lambda w: pl.BlockSpec((tt, w), lambda t: (t, 0))
    return pl.pallas_call(
        body, name=name, grid=(T // tt,), in_specs=[tile(K), tile(K), tile(N), DEP_SPEC],
        out_specs=pl.BlockSpec((2 * K, N), lambda t: (0, 0)), out_shape=jax.ShapeDtypeStruct((2 * K, N), F32),
        compiler_params=_cp("arbitrary"),
    )(a0, a1, b, dep)


def _adam_math(w, g, m, v):
    m = B1 * m + (1.0 - B1) * g
    v = B2 * v + (1.0 - B2) * (g * g)
    m_hat = m / (1.0 - B1 ** STEP)
    v_hat = v / (1.0 - B2 ** STEP)
    delta = -LR * (m_hat / (jnp.sqrt(v_hat) + AEPS) + WD * w)
    return delta, m, v


def _adam_2d(w, mine, land, m, v, c_arr, rb, name, dep):
    R, C = w.shape
    nbh = R // 2 // rb

    def body(c_ref, w_ref, mine_ref, land_ref, m_ref, v_ref, dep_ref, go_ref, d_ref, mo_ref, vo_ref):
        g = jnp.where(pl.program_id(0) // nbh == c_ref[0], mine_ref[...], land_ref[...])
        d, mn, vn = _adam_math(w_ref[...], g, m_ref[...], v_ref[...])
        go_ref[...] = g
        d_ref[...] = d
        mo_ref[...] = mn
        vo_ref[...] = vn

    spec = pl.BlockSpec((rb, C), lambda i, c_ref: (i, 0))
    mine_spec = pl.BlockSpec((rb, C), lambda i, c_ref: (jnp.clip(i - c_ref[0] * nbh, 0, nbh - 1), 0))
    return pl.pallas_call(
        body, name=name,
        grid_spec=pltpu.PrefetchScalarGridSpec(
            num_scalar_prefetch=1, grid=(R // rb,), in_specs=[spec, mine_spec, spec, spec, spec, DEP_SPEC],
            out_specs=[spec] * 4),
        out_shape=[jax.ShapeDtypeStruct((R, C), F32)] * 4, compiler_params=_cp("parallel"),
    )(c_arr, w, mine, land, m, v, dep)


def _adam_w_in(w3, mine, land, m3, v3, c_arr):
    n = w3.shape[0]

    def body(c_ref, w_hbm, mine_ref, land_ref, m_hbm, v_hbm, g_hbm, d_hbm, mo_hbm, vo_hbm, bufs, sems):
        ins = [pltpu.make_async_copy(src.at[:, 0], bufs.at[k], sems.at[k]) for k, src in enumerate((w_hbm, m_hbm, v_hbm))]
        for cp in ins:
            cp.start()
        half = D // 2
        top = jnp.where(c_ref[0] == 0, mine_ref[...], land_ref[0:half, :])
        bot = jnp.where(c_ref[0] == 1, mine_ref[...], land_ref[half:D, :])
        g = jnp.concatenate([top, bot], axis=0)
        eye = (_iota((D, D), 0) == _iota((D, D), 1)).astype(BF16)
        g_t = jnp.zeros((n, D), F32)
        r = g
        for i in range(3):
            p = r.astype(BF16)
            g_t = g_t + _dot_tn(p, eye)
            if i < 2:
                r = r - p.astype(F32)
        for cp in ins:
            cp.wait()
        d, mn, vn = _adam_math(bufs[0], g_t, bufs[1], bufs[2])
        for k, val in enumerate((g_t, d, mn, vn)):
            bufs[3 + k] = val
        outs = [pltpu.make_async_copy(bufs.at[3 + k], dst.at[:, 0], sems.at[3 + k])
                for k, dst in enumerate((g_hbm, d_hbm, mo_hbm, vo_hbm))]
        for cp in outs:
            cp.start()
        for cp in outs:
            cp.wait()

    anyspec = pl.BlockSpec(memory_space=pl.ANY)
    vm = pl.BlockSpec(memory_space=pltpu.VMEM)
    return pl.pallas_call(
        body, name="adam_w_in",
        in_specs=[pl.BlockSpec(memory_space=pltpu.SMEM), anyspec, vm, vm, anyspec, anyspec], out_specs=[anyspec] * 4,
        out_shape=[jax.ShapeDtypeStruct(w3.shape, F32)] * 4,
        scratch_shapes=[pltpu.VMEM((7, n, D), F32), pltpu.SemaphoreType.DMA((7,))],
        compiler_params=pltpu.CompilerParams(vmem_limit_bytes=VMEM_LIMIT),
    )(c_arr, w3, mine, land, m3, v3)


def _adam_w_ada(gat, allv, s_arr, w, m, v, rb):
    R, C = w.shape

    def body(s_ref, c_ref, dm_ref, w_ref, m_ref, v_ref, g_ref, d_ref, mo_ref, vo_ref):
        cm = _rows_select(c_ref, rb)
        g = lax.dot_general(cm * _sigmoid(cm), _rows_select(dm_ref, C), (((0,), (0,)), ((), ())), precision=HI,
                            preferred_element_type=F32)
        d, mn, vn = _adam_math(w_ref[...], g, m_ref[...], v_ref[...])
        g_ref[...] = g
        d_ref[...] = d
        mo_ref[...] = mn
        vo_ref[...] = vn

    spec = pl.BlockSpec((rb, C), lambda i, s_ref: (i, 0))
    return pl.pallas_call(
        body, name="adam_w_ada",
        grid_spec=pltpu.PrefetchScalarGridSpec(
            num_scalar_prefetch=1, grid=(R // rb,),
            in_specs=[pl.BlockSpec((8, 1, rb), lambda i, s_ref: (0, 0, i)),
                      pl.BlockSpec((8, 1, C), lambda i, s_ref: (0, 0, s_ref[0])), spec, spec, spec],
            out_specs=[spec] * 4),
        out_shape=[jax.ShapeDtypeStruct((R, C), F32)] * 4, compiler_params=_cp("parallel"),
    )(s_arr, gat, allv, w, m, v)


def _adam_small(tot, segs, ws, ms, vs):
    k = len(ws)
    extra = [sg for sg in segs if not isinstance(sg, tuple)]
    ne = len(extra)

    def body(*refs):
        tot_ref, g_x = refs[0], list(refs[1:1 + ne])
        w, m, v = [refs[1 + ne + j * k:1 + ne + (j + 1) * k] for j in range(3)]
        g_o, d_o, m_o, v_o = [refs[1 + ne + (3 + j) * k:1 + ne + (4 + j) * k] for j in range(4)]
        for i in range(k):
            gi = tot_ref[:, segs[i][0]:segs[i][0] + segs[i][1]] if isinstance(segs[i], tuple) else g_x.pop(0)[...]
            d, mn, vn = _adam_math(w[i][...], gi, m[i][...], v[i][...])
            g_o[i][...] = gi
            d_o[i][...] = d
            m_o[i][...] = mn
            v_o[i][...] = vn

    shapes = [jax.ShapeDtypeStruct(w.shape, F32) for w in ws]
    vm = pl.BlockSpec(memory_space=pltpu.VMEM)
    outs = pl.pallas_call(
        body, name="adam_small", in_specs=[vm] * (1 + ne + 3 * k), out_specs=[vm] * (4 * k), out_shape=shapes * 4,
    )(tot, *extra, *ws, *ms, *vs)
    return outs[0:k], outs[k:2 * k], outs[2 * k:3 * k], outs[3 * k:4 * k]


def _pos():
    return lax.axis_index("x"), lax.axis_index("y"), lax.axis_index("c")


def _flip(v, bit):
    return 1 - v if bit else v


def _peer(k):
    x, y, c = _pos()
    return (_flip(x, (k >> 2) & 1), _flip(y, (k >> 1) & 1), _flip(c, k & 1))


def _logical(p):
    return 4 * p[0] + 2 * p[1] + p[2]


def _gather8(src_ref, dst_ref, send_sems, recv_sems):
    me = _logical(_pos())
    dst_ref[pl.ds(me, 1)] = src_ref[...][None]
    copies = []
    for k in range(1, 8):
        cp = pltpu.make_async_remote_copy(src_ref, dst_ref.at[me], send_sems.at[k - 1], recv_sems.at[k - 1],
                                          device_id=_peer(k), device_id_type=MESH)
        cp.start()
        copies.append(cp)
    for k in range(1, 8):
        pltpu.make_async_remote_copy(src_ref, dst_ref.at[_logical(_peer(k))], send_sems.at[k - 1], recv_sems.at[k - 1],
                                     device_id=_peer(k), device_id_type=MESH).wait_recv()
    for cp in copies:
        cp.wait_send()


def _rows_select(ref3, width):
    row = _iota((8, width), 0)
    out = jnp.zeros((8, width), F32)
    for i in range(8):
        out = jnp.where(row == i, ref3[i][:, 0:width], out)
    return out


def _mod_exchange(payload, w_ada_s, b_ada4):
    n_sh = w_ada_s.shape[1]

    def body(pay_ref, w_ref, b_ref, gat_ref, mod_ref, token, p3, sa, ra, sb, rb):
        token[...] = jnp.zeros_like(token)
        x, y, c = _pos()
        me = _logical((x, y, c))
        my_s = 2 * x + y
        _gather8(pay_ref, gat_ref, sa, ra)
        cmat = _rows_select(gat_ref, D)
        prod = _dot_hi(cmat * _sigmoid(cmat), w_ref[...])
        for b in range(8):
            p3[b] = prod[b:b + 1, :]
        mod_ref[pl.ds(my_s, 1)] = p3[pl.ds(me, 1)] + b_ref[pl.ds(my_s, 1)]
        ks = (2, 4, 6)
        copies = []
        for i, k in enumerate(ks):
            pr = _peer(k)
            cp = pltpu.make_async_remote_copy(p3.at[_logical(pr)], mod_ref.at[my_s], sb.at[i], rb.at[i],
                                              device_id=pr, device_id_type=MESH)
            cp.start()
            copies.append(cp)
        for i, k in enumerate(ks):
            pr = _peer(k)
            s_src = 2 * pr[0] + pr[1]
            pltpu.make_async_remote_copy(p3.at[0], mod_ref.at[s_src], sb.at[i], rb.at[i],
                                         device_id=pr, device_id_type=MESH).wait_recv()
            mod_ref[pl.ds(s_src, 1)] = mod_ref[pl.ds(s_src, 1)] + b_ref[pl.ds(s_src, 1)]
        for cp in copies:
            cp.wait_send()

    vm = pl.BlockSpec(memory_space=pltpu.VMEM)
    return pl.pallas_call(
        body, name="mod_exchange", in_specs=[vm, vm, vm], out_specs=[vm, vm, vm],
        out_shape=[jax.ShapeDtypeStruct((8, 1, payload.shape[1]), F32), jax.ShapeDtypeStruct((4, 1, n_sh), F32),
                   jax.ShapeDtypeStruct((8, 128), F32)],
        scratch_shapes=[pltpu.VMEM((8, 1, n_sh), F32), pltpu.SemaphoreType.DMA((7,)), pltpu.SemaphoreType.DMA((7,)),
                        pltpu.SemaphoreType.DMA((3,)), pltpu.SemaphoreType.DMA((3,))],
        compiler_params=pltpu.CompilerParams(vmem_limit_bytes=VMEM_LIMIT),
    )(payload, w_ada_s, b_ada4)


def _chips():
    x, y, _ = _pos()
    out = []
    for k in (1, 2, 3):
        px, py = _flip(x, (k >> 1) & 1), _flip(y, k & 1)
        out.append((px, py, 2 * px + py))
    return out


def _half_rows(ref, which):
    half = ref.shape[-2] // 2
    return pl.ds(pl.multiple_of(which * half, 8), half)


def _plan_small():
    def plan(refs):
        me = _logical(_pos())
        return [(refs[0], refs[1].at[me], _peer(k), refs[1].at[_logical(_peer(k))]) for k in range(1, 8)]
    return plan


def _small_sum(vec, land, me_arr):
    n = vec.shape[1]

    def body(me_ref, v_ref, land_ref, tot_ref, all_ref):
        tot = None
        for i in range(8):
            row = jnp.where(me_ref[0] == i, v_ref[...], land_ref[i])
            all_ref[i] = row
            tot = row if i == 0 else tot + row
        tot_ref[...] = tot

    return pl.pallas_call(
        body, name="small_sum",
        grid_spec=pltpu.PrefetchScalarGridSpec(
            num_scalar_prefetch=1, grid=(1,),
            in_specs=[pl.BlockSpec((1, n), lambda i, me_ref: (0, 0)), pl.BlockSpec((8, 1, n), lambda i, me_ref: (0, 0, 0))],
            out_specs=[pl.BlockSpec((1, n), lambda i, me_ref: (0, 0)),
                       pl.BlockSpec((8, 1, n), lambda i, me_ref: (0, 0, 0))]),
        out_shape=[jax.ShapeDtypeStruct((1, n), F32), jax.ShapeDtypeStruct((8, 1, n), F32)],
        compiler_params=_cp("arbitrary"),
    )(me_arr, vec, land)


def _add_half(g, sib, c_arr, rb, name):
    _, R, C = g.shape
    half = R // 2
    nb = half // rb

    def body(c_ref, g_ref, s_ref, o_ref):
        o_ref[...] = (g_ref[...] + s_ref[...]).astype(BF16)

    return pl.pallas_call(
        body, name=name,
        grid_spec=pltpu.PrefetchScalarGridSpec(
            num_scalar_prefetch=1, grid=(4, nb),
            in_specs=[pl.BlockSpec((1, rb, C), lambda s, i, c_ref: (s, c_ref[0] * nb + i, 0)),
                      pl.BlockSpec((1, rb, C), lambda s, i, c_ref: (s, i, 0))],
            out_specs=pl.BlockSpec((1, rb, C), lambda s, i, c_ref: (s, i, 0))),
        out_shape=jax.ShapeDtypeStruct((4, half, C), BF16),
        compiler_params=_cp("parallel", "parallel"),
    )(c_arr, g, sib)


def _add_half_in(gq, gz, sibq, sibz, c_arr, rb):
    half = D // 2
    nq = gq.shape[1]
    wide = -(-IN_SH // 128) * 128

    def sel(rows, first, lo):
        return (_iota((rows, wide), 0) + (first - lo) == _iota((rows, wide), 1)).astype(BF16)

    def body(c_ref, gq_ref, gz_ref, sq_ref, sz_ref, o_ref):
        q = (gq_ref[...] + sq_ref[...]).astype(BF16)
        z = (gz_ref[...] + sz_ref[...]).astype(BF16)
        for s in range(4):
            lo, hi = s * IN_SH, (s + 1) * IN_SH
            acc = jnp.zeros((rb, wide), F32)
            if lo < nq:
                a0, a1 = lo // 128 * 128, min(nq, -(-min(hi, nq) // 128) * 128)
                acc = acc + _dot(q[:, a0:a1], sel(a1 - a0, a0, lo))
            if hi > nq:
                a0, a1 = (max(lo, nq) - nq) // 128 * 128, -(-(hi - nq) // 128) * 128
                acc = acc + _dot(z[:, a0:a1], sel(a1 - a0, nq + a0, lo))
            o_ref[s] = acc[:, :IN_SH].astype(BF16)

    nb = half // rb
    mine = lambda w: pl.BlockSpec((rb, w), lambda i, c_ref: (c_ref[0] * nb + i, 0))
    sib = lambda w: pl.BlockSpec((rb, w), lambda i, c_ref: (i, 0))
    return pl.pallas_call(
        body, name="grad_add_in",
        grid_spec=pltpu.PrefetchScalarGridSpec(
            num_scalar_prefetch=1, grid=(nb,),
            in_specs=[mine(nq), mine(gz.shape[1]), sib(nq), sib(gz.shape[1])],
            out_specs=pl.BlockSpec((4, rb, IN_SH), lambda i, c_ref: (0, i, 0))),
        out_shape=jax.ShapeDtypeStruct((4, half, IN_SH), BF16),
        compiler_params=_cp("parallel"),
    )(c_arr, gq, gz, sibq, sibz)


def _sum4(parts, land, s_arr, rb, name):
    _, H, C = land.shape

    def body(s_ref, own_ref, r_ref, o_ref):
        own = own_ref[0].astype(F32)
        tot = jnp.zeros((rb, C), F32)
        for j in range(4):
            tot = tot + jnp.where(s_ref[0] == j, own, r_ref[j].astype(F32))
        o_ref[...] = tot

    return pl.pallas_call(
        body, name=name,
        grid_spec=pltpu.PrefetchScalarGridSpec(
            num_scalar_prefetch=1, grid=(H // rb,),
            in_specs=[pl.BlockSpec((1, rb, C), lambda i, s_ref: (s_ref[0], i, 0)),
                      pl.BlockSpec((4, rb, C), lambda i, s_ref: (0, i, 0))],
            out_specs=pl.BlockSpec((rb, C), lambda i, s_ref: (i, 0))),
        out_shape=jax.ShapeDtypeStruct((H, C), F32), compiler_params=_cp("parallel"),
    )(s_arr, parts, land)


HBM_SPEC = pl.BlockSpec(memory_space=pltpu.HBM)
SEM_SPEC = pl.BlockSpec(memory_space=pltpu.SEMAPHORE)
EFFECT = pltpu.SideEffectType.DATAFLOW_SIDE_EFFECTING


def _split_start(name, bufs, n_sem, plan, dep):
    nb = len(bufs)

    def body(*refs):
        ins, send, recv, token = refs[:nb], refs[nb + 1], refs[nb + 2], refs[-1]
        for i, (src, dst, dev, _) in enumerate(plan(ins)):
            pltpu.make_async_remote_copy(src, dst, send.at[i], recv.at[i], device_id=dev, device_id_type=MESH).start()
        token[...] = jnp.zeros_like(token)

    outs = pl.pallas_call(
        body, name=name,
        out_shape=(pltpu.SemaphoreType.DMA((n_sem,)), pltpu.SemaphoreType.DMA((n_sem,)),
                   *[pltpu.HBM(b.shape, b.dtype) for b in bufs], jax.ShapeDtypeStruct((8, 128), F32)),
        in_specs=[HBM_SPEC] * nb + [pl.BlockSpec(memory_space=pl.ANY)],
        out_specs=(SEM_SPEC, SEM_SPEC, *([HBM_SPEC] * nb), pl.BlockSpec(memory_space=pltpu.VMEM)),
        input_output_aliases={i: 2 + i for i in range(nb)},
        compiler_params=pltpu.CompilerParams(has_side_effects=EFFECT),
    )(*[pltpu.with_memory_space_constraint(b, pltpu.HBM) for b in bufs], dep)
    return outs[0], outs[1], list(outs[2:2 + nb]), outs[-1]


def _split_wait(name, send, recv, bufs, after, plan):
    nb = len(bufs)
    after = list(after) if isinstance(after, (list, tuple)) else [after]

    def body(*refs):
        ins, send_s, recv_s = refs[:nb], refs[nb], refs[nb + 1]
        for i, (src, dst, dev, mine) in enumerate(plan(ins)):
            pltpu.make_async_remote_copy(src, dst, send_s.at[i], recv_s.at[i], device_id=dev,
                                         device_id_type=MESH).wait_send()
            pltpu.make_async_remote_copy(src, mine, send_s.at[i], recv_s.at[i], device_id=dev,
                                         device_id_type=MESH).wait_recv()

    outs = pl.pallas_call(
        body, name=name, out_shape=[pltpu.HBM(b.shape, b.dtype) for b in bufs],
        in_specs=[HBM_SPEC] * nb + [SEM_SPEC, SEM_SPEC] + [HBM_SPEC] * len(after),
        out_specs=[HBM_SPEC] * nb, input_output_aliases={i: i for i in range(nb)},
        compiler_params=pltpu.CompilerParams(has_side_effects=EFFECT),
    )(*bufs, send, recv, *[pltpu.with_memory_space_constraint(a, pltpu.HBM) for a in after])
    return list(outs)


def _copies_now(name, bufs, n_sem, plan):
    nb = len(bufs)

    def body(*refs):
        ins, token, send, recv = refs[:nb], refs[2 * nb], refs[-2], refs[-1]
        token[...] = jnp.zeros_like(token)
        todo = plan(ins)
        for i, (src, dst, dev, _) in enumerate(todo):
            pltpu.make_async_remote_copy(src, dst, send.at[i], recv.at[i], device_id=dev, device_id_type=MESH).start()
        for i, (src, dst, dev, mine) in enumerate(todo):
            pltpu.make_async_remote_copy(src, mine, send.at[i], recv.at[i], device_id=dev, device_id_type=MESH).wait_recv()
        for i, (src, dst, dev, _) in enumerate(todo):
            pltpu.make_async_remote_copy(src, dst, send.at[i], recv.at[i], device_id=dev, device_id_type=MESH).wait_send()

    outs = pl.pallas_call(
        body, name=name,
        out_shape=[pltpu.HBM(b.shape, b.dtype) for b in bufs] + [jax.ShapeDtypeStruct((8, 128), F32)],
        in_specs=[HBM_SPEC] * nb, out_specs=[HBM_SPEC] * nb + [pl.BlockSpec(memory_space=pltpu.VMEM)],
        input_output_aliases={i: i for i in range(nb)},
        scratch_shapes=[pltpu.SemaphoreType.DMA((n_sem,)), pltpu.SemaphoreType.DMA((n_sem,))],
    )(*[pltpu.with_memory_space_constraint(b, pltpu.HBM) for b in bufs])
    return list(outs[:nb]), outs[nb]


def _slot(land, s, rows, cols):
    if cols is None:
        return land.at[s, rows]
    return land.at[rows, pl.ds(pl.multiple_of(s * cols, 128), cols)]


def _plan_gather_ici(cols):
    nw = len(cols)

    def plan(refs):
        x, y, c = _pos()
        my_s = 2 * x + y
        out = []
        for w in range(nw):
            mine = _half_rows(refs[w], c)
            for px, py, ps in _chips():
                out.append((refs[w].at[mine], _slot(refs[nw + w], my_s, mine, cols[w]), (px, py, c),
                            _slot(refs[nw + w], ps, mine, cols[w])))
        return out
    return plan


def _plan_gather_fwd(cols, rows):
    def plan(refs):
        x, y, c = _pos()
        out = []
        for w in range(len(cols)):
            half = rows[w] // 2
            mine = pl.ds(pl.multiple_of(c * half, 8), half)
            other = pl.ds(pl.multiple_of((1 - c) * half, 8), half)
            for px, py, ps in _chips():
                got = _slot(refs[w], ps, mine, cols[w])
                out.append((got, got, (x, y, 1 - c), _slot(refs[w], ps, other, cols[w])))
        return out
    return plan


def _plan_swap(nw):
    def plan(refs):
        x, y, c = _pos()
        return [(refs[w].at[:, _half_rows(refs[w], 1 - c)], refs[nw + w], (x, y, 1 - c), refs[nw + w])
                for w in range(nw)]
    return plan


def _plan_swap_rows(nw):
    def plan(refs):
        x, y, c = _pos()
        return [(refs[w].at[_half_rows(refs[w], 1 - c)], refs[nw + w], (x, y, 1 - c), refs[nw + w])
                for w in range(nw)]
    return plan


def _plan_scatter(nw):
    def plan(refs):
        x, y, c = _pos()
        my_s = 2 * x + y
        out = []
        for w in range(nw):
            for px, py, ps in _chips():
                out.append((refs[w].at[ps], refs[nw + w].at[my_s], (px, py, c), refs[nw + w].at[ps]))
        return out
    return plan


def _plan_join(nw):
    def plan(refs):
        x, y, c = _pos()
        out = []
        for w in range(nw):
            land = refs[nw + w]
            out.append((refs[w], land.at[_half_rows(land, c)], (x, y, 1 - c), land.at[_half_rows(land, 1 - c)]))
        return out
    return plan


def _hbm_empty(shape, dtype):
    return pltpu.with_memory_space_constraint(lax.empty(shape, dtype), pltpu.HBM)


def _put_slot(land, own, slot):
    return lax.dynamic_update_slice(land, own[None], (slot,) + (0,) * own.ndim)


def _w_in_assemble(land, own, s_arr, rb):
    wide = -(-IN_SH // 128) * 128
    starts = [s * IN_SH // 128 * 128 for s in range(4)]
    ends = [min(IN_PAD, -(-(s + 1) * IN_SH // 128) * 128) for s in range(4)]

    def body(s_ref, land_ref, own_ref, o_ref, parts):
        @pl.when(pl.program_id(0) == 0)
        def _():
            parts[...] = jnp.zeros_like(parts)

        acc = []
        for s in range(4):
            parts[s, :, 0:IN_SH] = jnp.where(s_ref[0] == s, own_ref[...], land_ref[s])
            w = ends[s] - starts[s]
            sel = (_iota((wide, w), 0) + (s * IN_SH - starts[s]) == _iota((wide, w), 1)).astype(BF16)
            acc.append(_dot(parts[s], sel))
        for s in range(4):
            lo = starts[s] if s == 0 else ends[s - 1]
            hi = starts[s + 1] if s < 3 else ends[s]
            o_ref[:, lo:hi] = acc[s][:, lo - starts[s]:hi - starts[s]].astype(BF16)
            if s < 3:
                a, b = starts[s + 1], ends[s]
                o_ref[:, a:b] = (acc[s][:, a - starts[s]:b - starts[s]] + acc[s + 1][:, 0:b - a]).astype(BF16)

    return pl.pallas_call(
        body, name="w_in_assemble",
        grid_spec=pltpu.PrefetchScalarGridSpec(
            num_scalar_prefetch=1, grid=(D // rb,),
            in_specs=[pl.BlockSpec((4, rb, IN_SH), lambda i, s_ref: (0, i, 0)),
                      pl.BlockSpec((rb, IN_SH), lambda i, s_ref: (i, 0))],
            out_specs=pl.BlockSpec((rb, IN_PAD), lambda i, s_ref: (i, 0)),
            scratch_shapes=[pltpu.VMEM((4, rb, wide), BF16)]),
        out_shape=jax.ShapeDtypeStruct((D, IN_PAD), BF16), compiler_params=_cp("arbitrary"),
    )(s_arr, land, own)


def _pad_lanes(a, n):
    return jnp.pad(a, ((0, 0), (0, n - a.shape[1])))


def kernel(x, c, positions, w_ada, b_ada, norm1_w, w_in, conv_w, conv_b, dt_bias, a_log, d_skip, attn_sinks, ssm_norm_w, w_out, norm2_w, w_gate_up, w_down, final_norm_w, loss_target, m_w_ada, m_b_ada, m_norm1_w, m_w_in, m_conv_w, m_conv_b, m_dt_bias, m_a_log, m_d_skip, m_attn_sinks, m_ssm_norm_w, m_w_out, m_norm2_w, m_w_gate_up, m_w_down, m_final_norm_w, v_w_ada, v_b_ada, v_norm1_w, v_w_in, v_conv_w, v_conv_b, v_dt_bias, v_a_log, v_d_skip, v_attn_sinks, v_ssm_norm_w, v_w_out, v_norm2_w, v_w_gate_up, v_w_down, v_final_norm_w):
    T = x.shape[1]
    tm = min(256, T)
    xi, yi, ci = lax.axis_index("x"), lax.axis_index("y"), lax.axis_index("c")
    my_s = 2 * xi + yi
    xs = x[0]
    tgt = loss_target[0]

    payload = jnp.concatenate([c, conv_w[0].reshape(1, CONVK * 256)], axis=1)
    gat, mod4, tok = _mod_exchange(payload, w_ada[0], b_ada.reshape(4, 1, 1536))
    mod6 = mod4.reshape(6, D)
    cw_dev = gat[:, 0, D:].reshape(4, 2, CONVK, 256)[:, 0]
    conv_full = cw_dev.transpose(1, 0, 2).reshape(CONVK, CONVC)

    w_in_b = w_in[0].astype(BF16)
    s_i, r_i, bufs, tok = _split_start("wgather_in_ici_start", [w_in_b, _hbm_empty((4,) + w_in_b.shape, BF16)], 3,
                                       _plan_gather_ici([None]), tok)
    inv_freq = (10000.0 ** (-jnp.arange(32, dtype=F32) / 32))
    cos, sin_s = _rope_tables(positions, inv_freq.reshape(32, 1), min(512, T), tok)
    late = [w_out[0].astype(BF16), w_gate_up[0].astype(BF16), w_down[0].astype(BF16)]
    bufs = _split_wait("wgather_in_ici_wait", s_i, r_i, bufs, [cos] + late, _plan_gather_ici([None]))
    own_in = bufs[0]
    bufs, tok = _copies_now("wgather_in_fwd", bufs[1:], 3, _plan_gather_fwd([None], [D]))
    s_arr = my_s.reshape(1).astype(jnp.int32)
    w_pad = _w_in_assemble(bufs[0], own_in, s_arr, 256)

    lands = [_hbm_empty((4, D // 4, D), BF16), _hbm_empty((D, 2 * DFF), BF16), _hbm_empty((4, DFF // 4, D), BF16)]
    cols3, rows3 = [None, GU_SH, None], [D // 4, D, DFF // 4]
    s_a, r_a, bufs, tok = _split_start("wgather_ici_start", late + lands, 9, _plan_gather_ici(cols3), tok)

    qkv, z, xbc, dtr, h1b = _in_proj_fwd(xs, cos, sin_s, mod6, norm1_w, w_pad, min(512, T), tok)
    sinks = attn_sinks
    attn, lse = _attn_fwd(qkv, sinks)
    bufs = _split_wait("wgather_ici_wait", s_a, r_a, bufs, attn, _plan_gather_ici(cols3))
    late = bufs[:3]
    s_b, r_b, lands, tok = _split_start("wgather_fwd_start", bufs[3:], 9, _plan_gather_fwd(cols3, rows3), attn)
    dtb = _pad_lanes(dt_bias, 128)
    alog = _pad_lanes(a_log, 128)
    dskx = jnp.repeat(d_skip, HD, axis=1)
    mats = _ssd_mats()
    ynorm, ypre, states, conv_pre = _ssd_fwd(xbc, z, dtr, conv_full, conv_b, dtb, alog, dskx, ssm_norm_w, mats, tok)
    lands = _split_wait("wgather_fwd_wait", s_b, r_b, lands, ynorm, _plan_gather_fwd(cols3, rows3))
    w_out_f = _put_slot(lands[0], late[0], my_s).reshape(D, D)
    w_dn_f = _put_slot(lands[2], late[2], my_s).reshape(DFF, D)

    fw2 = final_norm_w.reshape(1, D)
    sq, dmix, dx1, h2b, act, dfb, dgu, dob, sm_ffn = _mix_ffn(
        xs, attn, ynorm, tgt, mod6, norm2_w, fw2, w_out_f, lands[1], late[1], s_arr, w_dn_f, tm)

    tt = min(2048, T)
    c_arr = ci.reshape(1).astype(jnp.int32)
    tok0 = jnp.zeros((8, 128), F32)
    gw_dn4 = _tn_matmul(act, dfb, GU_SH, D, tt, "dw_down", tok0).reshape(4, DFF // 4, D)
    gw_gu4 = _tn_matmul(h2b, dgu, D, GU_SH, tt, "dw_gate_up", tok0)
    gw_out4 = _tn_matmul_rows(attn, ynorm, dob, tt, "dw_out", tok0).reshape(4, D // 4, D)
    big1 = [gw_out4, gw_gu4, gw_dn4]
    rbs1 = [128, 512, 352]
    sib1 = [_hbm_empty((4, g.shape[1] // 2, g.shape[2]), F32) for g in big1]
    s_c, r_c, bufs, tok = _split_start("gswap_start", big1 + sib1, 3, _plan_swap(3), tok0)

    dzxd, d_cw, d_cb, d_sw, d_sk, d_dtb, d_av = _ssd_bwd(
        xbc, conv_pre, z, dtr, ypre, states, dmix, conv_full, dtb, alog, dskx, ssm_norm_w, mats, tok)
    bufs = _split_wait("gswap_wait", s_c, r_c, bufs, dzxd, _plan_swap(3))
    sums1 = [_add_half(g, s, c_arr, rb, "grad_add_%d" % i)
             for i, (g, s, rb) in enumerate(zip(bufs[:3], bufs[3:], rbs1))]
    land1 = [_hbm_empty(p.shape, BF16) for p in sums1]
    s_d, r_d, bufs, tok = _split_start("gscatter_start", sums1 + land1, 9, _plan_scatter(3), tok0)
    dqkv, d_sinks = _attn_bwd(qkv, sinks, lse, dmix, cos, sin_s, tok)
    bufs = _split_wait("gscatter_wait", s_d, r_d, bufs, dqkv, _plan_scatter(3))
    halves1 = [_sum4(p, l, s_arr, rb, "grad_sum_%d" % i)
               for i, (p, l, rb) in enumerate(zip(bufs[:3], bufs[3:], rbs1))]
    full1 = [_hbm_empty((2 * h.shape[0], h.shape[1]), F32) for h in halves1]
    s_e, r_e, bufs, tok = _split_start("gjoin_start", halves1 + full1, 3, _plan_join(3), tok0)
    gq = _tn_matmul(h1b, dqkv, D, 768, tt, "dw_in_qkv", tok)[0]
    gz = _tn_matmul(h1b, dzxd, D, IN_PAD - 768, tt, "dw_in_zxd", tok)[0]
    joined1 = _split_wait("gjoin_wait", s_e, r_e, bufs, [gq, gz], _plan_join(3))

    sibs = [_hbm_empty((D // 2, g.shape[1]), F32) for g in (gq, gz)]
    s_f, r_f, bufs, tok = _split_start("gswap_in_start", [gq, gz] + sibs, 2, _plan_swap_rows(2), tok0)
    g_dn_s, d_dn, m_dn, v_dn = _adam_2d(w_down[0], joined1[2], joined1[5], m_w_down[0], v_w_down[0], c_arr, 352,
                                        "adam_w_down", tok)
    g_gu_s, d_gu, m_gu, v_gu = _adam_2d(w_gate_up[0], joined1[1], joined1[4], m_w_gate_up[0], v_w_gate_up[0], c_arr,
                                        256, "adam_w_gate_up", tok)
    g_out_s, d_out, m_out, v_out = _adam_2d(w_out[0], joined1[0], joined1[3], m_w_out[0], v_w_out[0], c_arr, 128,
                                            "adam_w_out", tok)
    bufs = _split_wait("gswap_in_wait", s_f, r_f, bufs, [d_dn, d_gu, d_out], _plan_swap_rows(2))
    sum0 = _add_half_in(bufs[0], bufs[1], bufs[2], bufs[3], c_arr, min(256, D // 2))
    s_g, r_g, bufs, tok = _split_start("gscatter_in_start", [sum0, _hbm_empty(sum0.shape, BF16)], 3, _plan_scatter(1),
                                       tok0)
    grad_x, sm_in = _in_proj_bwd(xs, dx1, dqkv, dzxd, mod6, norm1_w, w_pad, min(512, T), tok)

    a_neg = -jnp.exp(alog)
    pieces = [sm_in[1:2], sm_in[2:3], sm_ffn[5:6], sm_ffn[2:3], sm_ffn[3:4], sm_ffn[4:5],
              sm_in[0:1], sm_ffn[1:2], sm_ffn[0:1], d_cb, d_cw.reshape(1, CONVK * CONVC),
              _pad_lanes(d_sw, SW), d_dtb, d_av * a_neg, d_sk, d_sinks,
              _pad_lanes((0.5 / D * jnp.sum(sq)).reshape(1, 1), 128)]
    vec = jnp.concatenate(pieces, axis=1)
    s_h, r_h, rows8, tok_small = _split_start("small_start", [vec, _hbm_empty((8,) + vec.shape, F32)], 7,
                                              _plan_small(), tok0)

    bufs = _split_wait("gscatter_in_wait", s_g, r_g, bufs, [grad_x, tok_small], _plan_scatter(1))
    half0 = _sum4(bufs[0], bufs[1], s_arr, 512, "grad_sum_in")
    joined0, _ = _copies_now("gjoin_in", [half0, _hbm_empty((D, IN_SH), F32)], 1, _plan_join(1))
    native = lambda a: a.transpose(2, 0, 1)
    adam_in = _adam_w_in(native(w_in), joined0[0], joined0[1], native(m_w_in), native(v_w_in), c_arr)
    g_in_s, d_in, m_in, v_in = [a.transpose(1, 2, 0) for a in adam_in]
    rows8 = _split_wait("small_wait", s_h, r_h, rows8, [adam_in[1]], _plan_small())
    tot, allv = _small_sum(rows8[0], rows8[1], (4 * xi + 2 * yi + ci).reshape(1).astype(jnp.int32))
    o = 0
    offs = []
    for p in pieces:
        offs.append(o)
        o += p.shape[1]
    seg = lambda i, n: (offs[i], n)
    g_conv_w = lax.dynamic_slice_in_dim(
        tot[:, offs[10]:offs[10] + CONVK * CONVC].reshape(CONVK, CONVC), my_s * 256, 256, axis=1)
    loss = tot[0, offs[16]]

    small_names = ["b_ada", "norm1_w", "conv_w", "conv_b", "dt_bias", "a_log", "d_skip", "attn_sinks", "ssm_norm_w",
                   "norm2_w", "final_norm_w"]
    small_g = [(0, 6 * D), seg(6, D), g_conv_w, seg(9, D), seg(12, 8), seg(13, 8), seg(14, 8), seg(15, 8),
               seg(11, SW), seg(7, D), seg(8, D)]
    as2d = lambda a: a.reshape(-1, a.shape[-1])
    small_w = [as2d(a) for a in (b_ada, norm1_w, conv_w, conv_b, dt_bias, a_log, d_skip, attn_sinks, ssm_norm_w,
                                 norm2_w, final_norm_w)]
    small_m = [as2d(a) for a in (m_b_ada, m_norm1_w, m_conv_w, m_conv_b, m_dt_bias, m_a_log, m_d_skip, m_attn_sinks,
                                 m_ssm_norm_w, m_norm2_w, m_final_norm_w)]
    small_v = [as2d(a) for a in (v_b_ada, v_norm1_w, v_conv_w, v_conv_b, v_dt_bias, v_a_log, v_d_skip, v_attn_sinks,
                                 v_ssm_norm_w, v_norm2_w, v_final_norm_w)]
    small_g, sd, smn, svn = _adam_small(tot, small_g, small_w, small_m, small_v)
    g_ada, d_ada, m_ada, v_ada = _adam_w_ada(gat, allv, s_arr, w_ada[0], m_w_ada[0], v_w_ada[0], 256)

    order = ["w_ada", "b_ada", "norm1_w", "w_in", "conv_w", "conv_b", "dt_bias", "a_log", "d_skip", "attn_sinks",
             "ssm_norm_w", "w_out", "norm2_w", "w_gate_up", "w_down", "final_norm_w"]
    shapes = dict(w_ada=w_ada.shape, b_ada=b_ada.shape, norm1_w=norm1_w.shape, w_in=w_in.shape, conv_w=conv_w.shape,
                  conv_b=conv_b.shape, dt_bias=dt_bias.shape, a_log=a_log.shape, d_skip=d_skip.shape,
                  attn_sinks=attn_sinks.shape, ssm_norm_w=ssm_norm_w.shape, w_out=w_out.shape, norm2_w=norm2_w.shape,
                  w_gate_up=w_gate_up.shape, w_down=w_down.shape, final_norm_w=final_norm_w.shape)
    grads = dict(w_ada=g_ada, w_in=g_in_s, w_out=g_out_s, w_gate_up=g_gu_s, w_down=g_dn_s)
    deltas = dict(w_ada=d_ada, w_in=d_in, w_out=d_out, w_gate_up=d_gu, w_down=d_dn)
    new_m = dict(w_ada=m_ada, w_in=m_in, w_out=m_out, w_gate_up=m_gu, w_down=m_dn)
    new_v = dict(w_ada=v_ada, w_in=v_in, w_out=v_out, w_gate_up=v_gu, w_down=v_dn)
    for i, nme in enumerate(small_names):
        grads[nme], deltas[nme], new_m[nme], new_v[nme] = small_g[i], sd[i], smn[i], svn[i]
    outs = [loss, grad_x[None]]
    for table in (grads, deltas, new_m, new_v):
        outs += [table[nme].reshape(shapes[nme]) for nme in order]
    return tuple(outs)
```

```python
import functools
import math

import jax
import jax.numpy as jnp
from jax import lax
from jax.experimental import pallas as pl
from jax.experimental.pallas import tpu as pltpu

F32 = jnp.float32
BF16 = jnp.bfloat16
HI = lax.Precision.HIGHEST
MESH = pl.DeviceIdType.MESH

D = 1024
HD = 64
AW = 512
SW = 512
NST = 128
CONVK = 4
CONVC = 1024
BLK = 128
CPS = 4
SSD_FWD_CPS = 8
ATTN_BPS = 8
IN_PROJ = 2312
IN_PAD = 2432
IN_SH = IN_PROJ // 4
DFF = 2816
GU_SH = 1408
FF_SPLITS = ((0, 1536), (1536, 2816))
EPS = 1e-6
NEG = -1e30
LR, B1, B2, AEPS, WD, STEP = 0.001, 0.9, 0.999, 1e-08, 0.01, 10
VMEM_LIMIT = 58 * 1024 * 1024


def _cp(*sem):
    return pltpu.CompilerParams(dimension_semantics=sem or None, vmem_limit_bytes=VMEM_LIMIT)


def _dot(a, b):
    return jnp.dot(a, b, preferred_element_type=F32)


def _dot_nt(a, b):
    return lax.dot_general(a, b, (((1,), (1,)), ((), ())), preferred_element_type=F32)


def _dot_tn(a, b):
    return lax.dot_general(a, b, (((0,), (0,)), ((), ())), preferred_element_type=F32)


def _dot_hi(a, b):
    return jnp.dot(a, b, precision=HI, preferred_element_type=F32)


def _sigmoid(x):
    return 1.0 / (1.0 + jnp.exp(-x))


def _iota(shape, dim):
    return lax.broadcasted_iota(jnp.int32, shape, dim)


def _load_resident(hbm_ref, vmem_ref, sem):
    @pl.when(pl.program_id(0) == 0)
    def _():
        cp = pltpu.make_async_copy(hbm_ref, vmem_ref, sem)
        cp.start()
        cp.wait()


def _swap32(t):
    lane = _iota(t.shape, 1)
    return jnp.where((lane & 63) < 32, pltpu.roll(t, 96, 1), pltpu.roll(t, 32, 1))


def _rope_fwd(t, cos, sin_s):
    return t * cos + _swap32(t) * sin_s


def _rope_bwd(t, cos, sin_s):
    return t * cos - _swap32(t) * sin_s


DEP_SPEC = pl.BlockSpec((8, 128), lambda *_: (0, 0))


def _rope_tables(pos_row, inv_freq_col, tm, dep):
    T = pos_row.shape[1]
    lane, row = jnp.arange(128)[None, :], jnp.arange(96)[:, None]
    pick = (lane % 32) == (row % 32)
    sel_cos = pick.astype(BF16)
    sel_sin = jnp.where(pick, jnp.where(lane % 64 < 32, -1.0, 1.0), 0.0).astype(BF16)

    def body(p_ref, f_ref, sc_ref, ss_ref, dep_ref, cos_ref, sin_ref):
        ang = f_ref[...] * p_ref[...].astype(F32)
        cos_ref[...] = _dot_tn(_pieces(jnp.cos(ang), 3, 0), sc_ref[...])
        sin_ref[...] = _dot_tn(_pieces(jnp.sin(ang), 3, 0), ss_ref[...])

    full = lambda a: pl.BlockSpec(a.shape, lambda i: (0,) * a.ndim)
    return pl.pallas_call(
        body, name="rope_tables", grid=(T // tm,),
        in_specs=[pl.BlockSpec((1, tm), lambda i: (0, i)), full(inv_freq_col), full(sel_cos), full(sel_sin), DEP_SPEC],
        out_specs=[pl.BlockSpec((tm, 128), lambda i: (i, 0))] * 2,
        out_shape=[jax.ShapeDtypeStruct((T, 128), F32)] * 2,
        compiler_params=_cp("parallel"),
    )(pos_row, inv_freq_col, sel_cos, sel_sin, dep)


def _in_proj_fwd(x, cos, sin_s, mod6, norm1_w, w_pad, tm, dep):
    T = x.shape[0]

    def body(x_ref, cos_ref, sin_ref, mod_ref, nw_ref, w_hbm, dep_ref, qkv_ref, z_ref, xbc_ref, dt_ref, h_ref, w_vmem,
             sem):
        _load_resident(w_hbm, w_vmem, sem)
        xv = x_ref[...]
        r = lax.rsqrt(jnp.mean(xv * xv, axis=-1, keepdims=True) + EPS)
        h = (xv * r * nw_ref[...]) * (1.0 + mod_ref[1:2, :]) + mod_ref[0:1, :]
        hb = h.astype(BF16)
        h_ref[...] = hb
        proj = _dot(hb, w_vmem[...])
        cs, sn = cos_ref[...], sin_ref[...]
        for j in range(5):
            qkv_ref[:, 128 * j:128 * (j + 1)] = _rope_fwd(proj[:, 128 * j:128 * (j + 1)], cs, sn).astype(BF16)
        qkv_ref[:, 640:768] = proj[:, 640:768].astype(BF16)
        z_ref[...] = proj[:, 768:1280]
        xbc_ref[...] = proj[:, 1280:2304]
        dt_ref[...] = proj[:, 2304:2432]

    row = lambda w: pl.BlockSpec((tm, w), lambda i: (i, 0))
    full = lambda a: pl.BlockSpec(a.shape, lambda i: (0,) * a.ndim)
    return pl.pallas_call(
        body, name="in_proj_fwd", grid=(T // tm,),
        in_specs=[row(D), row(128), row(128), full(mod6), full(norm1_w), pl.BlockSpec(memory_space=pl.ANY), DEP_SPEC],
        out_specs=[row(768), row(512), row(1024), row(128), row(D)],
        out_shape=[jax.ShapeDtypeStruct((T, 768), BF16), jax.ShapeDtypeStruct((T, 512), F32),
                   jax.ShapeDtypeStruct((T, 1024), F32), jax.ShapeDtypeStruct((T, 128), F32),
                   jax.ShapeDtypeStruct((T, D), BF16)],
        scratch_shapes=[pltpu.VMEM((D, IN_PAD), BF16), pltpu.SemaphoreType.DMA],
        compiler_params=_cp("arbitrary"),
    )(x, cos, sin_s, mod6, norm1_w, w_pad, dep)


def _head_variants(pair, j):
    lane = _iota(pair.shape, 1)
    lo = lane < 64
    kv = j // 2
    ev = jnp.where(lo, pair, 0.0)
    od = jnp.where(lo, 0.0, pair)
    if kv == 0:
        od = pltpu.roll(od, 64, 1)
    else:
        ev = pltpu.roll(ev, 64, 1)
    return ev.astype(BF16), od.astype(BF16)


def _kv_variants(vcat):
    lane = _iota(vcat.shape, 1)
    lo = lane < 64
    v0 = jnp.where(lo, vcat, 0.0)
    v1 = jnp.where(lo, 0.0, vcat)
    out = {
        (0, 0): v0, (0, 1): pltpu.roll(v0, 64, 1),
        (1, 0): pltpu.roll(v1, 64, 1), (1, 1): v1,
    }
    return {k: v.astype(BF16) for k, v in out.items()}


def _fold_masks(n):
    upper = _iota((BLK, BLK), 1) > _iota((BLK, BLK), 0)
    return upper, upper & (n == 0)


def _attn_fwd(qkv, sinks):
    CPS = ATTN_BPS
    T = qkv.shape[0]
    nsteps = T // (CPS * BLK)

    def body(sink_ref, q_ref, kc_ref, kp_ref, vc_ref, vp_ref, o_ref, lse_ref):
        for sub in range(CPS):
            rows, before = slice(BLK * sub, BLK * (sub + 1)), slice(BLK * (sub - 1), BLK * sub)
            block(pl.program_id(0) * CPS + sub, sink_ref, q_ref.at[rows, :], kc_ref.at[rows, :],
                  kp_ref if sub == 0 else kc_ref.at[before, :], vc_ref.at[rows, :],
                  vp_ref if sub == 0 else vc_ref.at[before, :], o_ref.at[rows, :], lse_ref.at[rows, :])

    def block(n, sink_ref, q_ref, kc_ref, kp_ref, vc_ref, vp_ref, o_ref, lse_ref):
        vpv = _kv_variants(vp_ref[...].astype(F32))
        vcv = _kv_variants(vc_ref[...].astype(F32))
        q_all = jnp.concatenate(
            [v for j in range(4) for v in _head_variants(q_ref[:, 128 * j:128 * (j + 1)].astype(F32), j)], axis=0)
        s_prev = _dot_nt(q_all, kp_ref[...])
        s_cur = _dot_nt(q_all, kc_ref[...])
        upper, dead = _fold_masks(n)
        lane = _iota((BLK, 128), 1)
        lse_acc = jnp.zeros((BLK, 128), F32)
        for jj in range(4):
            acc = jnp.zeros((BLK, 128), F32)
            for par in range(2):
                h = 2 * jj + par
                rows = slice(h * BLK, (h + 1) * BLK)
                sink = sink_ref[0, h]
                s = jnp.where(dead, NEG, jnp.where(upper, s_prev[rows], s_cur[rows]) * 0.125)
                m = jnp.maximum(jnp.max(s, axis=1, keepdims=True), sink)
                p = jnp.exp(s - m)
                den = jnp.sum(p, axis=1, keepdims=True) + jnp.exp(sink - m)
                pn = p * (1.0 / den)
                acc = (acc + _dot(jnp.where(upper, pn, 0.0).astype(BF16), vpv[(jj // 2, par)])
                       + _dot(jnp.where(upper, 0.0, pn).astype(BF16), vcv[(jj // 2, par)]))
                lse_acc = jnp.where(lane == h, m + jnp.log(den), lse_acc)
            o_ref[:, 128 * jj:128 * (jj + 1)] = acc.astype(BF16)
        lse_ref[...] = lse_acc

    RB = CPS * BLK
    prev = lambda n: jnp.maximum(n * CPS - 1, 0)
    return pl.pallas_call(
        body, name="attn_fwd", grid=(nsteps,),
        in_specs=[pl.BlockSpec(memory_space=pltpu.SMEM),
                  pl.BlockSpec((RB, 512), lambda n: (n, 0)),
                  pl.BlockSpec((RB, 128), lambda n: (n, 4)),
                  pl.BlockSpec((BLK, 128), lambda n: (prev(n), 4)),
                  pl.BlockSpec((RB, 128), lambda n: (n, 5)),
                  pl.BlockSpec((BLK, 128), lambda n: (prev(n), 5))],
        out_specs=[pl.BlockSpec((RB, 512), lambda n: (n, 0)), pl.BlockSpec((RB, 128), lambda n: (n, 0))],
        out_shape=[jax.ShapeDtypeStruct((T, 512), BF16), jax.ShapeDtypeStruct((T, 128), F32)],
        compiler_params=_cp("parallel"),
    )(sinks, qkv, qkv, qkv, qkv, qkv)


def _attn_bwd(qkv, sinks, lse, dmix, cos, sin_s, dep):
    T = qkv.shape[0]
    nb = T // BLK

    def body(sink_ref, q_ref, kc_ref, kp_ref, vc_ref, vp_ref, lse_ref, do_ref, cq_ref, sq_ref, ck_ref, sk_ref,
             dep_ref, out_ref, ds_ref, dq_car, dk_car, dv_car):
        n = pl.program_id(0)
        lane = _iota((BLK, 128), 1)

        @pl.when(n == 0)
        def _():
            ds_ref[...] = jnp.zeros_like(ds_ref)
            dq_car[...] = jnp.zeros_like(dq_car)
            dk_car[...] = jnp.zeros_like(dk_car)
            dv_car[...] = jnp.zeros_like(dv_car)

        @pl.when(n < nb)
        def _():
            kp, kc, vp, vc = kp_ref[...], kc_ref[...], vp_ref[...], vc_ref[...]
            kpv = _kv_variants(kp.astype(F32))
            kcv = _kv_variants(kc.astype(F32))
            lse_v = lse_ref[...]
            q_all = jnp.concatenate(
                [v for j in range(4) for v in _head_variants(q_ref[:, 128 * j:128 * (j + 1)].astype(F32), j)], axis=0)
            do_all = jnp.concatenate(
                [v for j in range(4) for v in _head_variants(do_ref[:, 128 * j:128 * (j + 1)], j)], axis=0)
            s_prev, s_cur = _dot_nt(q_all, kp), _dot_nt(q_all, kc)
            dp_prev, dp_cur = _dot_nt(do_all, vp), _dot_nt(do_all, vc)
            upper, dead = _fold_masks(n)
            out_ref[:, 0:512] = dq_car[...]
            dsk = jnp.zeros((1, 128), F32)
            ds_u, ds_l, p_u, p_l = [], [], [], []
            for jj in range(4):
                dq_acc = jnp.zeros((BLK, 128), F32)
                for par in range(2):
                    h = 2 * jj + par
                    rows = slice(h * BLK, (h + 1) * BLK)
                    lse_h = jnp.sum(jnp.where(lane == h, lse_v, 0.0), axis=1, keepdims=True)
                    s = jnp.where(dead, NEG, jnp.where(upper, s_prev[rows], s_cur[rows]) * 0.125)
                    p = jnp.exp(s - lse_h)
                    dp = jnp.where(upper, dp_prev[rows], dp_cur[rows])
                    delta = jnp.sum(p * dp, axis=1, keepdims=True)
                    ds = p * (dp - delta) * 0.125
                    dsu, dsl = jnp.where(upper, ds, 0.0).astype(BF16), jnp.where(upper, 0.0, ds).astype(BF16)
                    dq_acc = dq_acc + _dot(dsu, kpv[(jj // 2, par)]) + _dot(dsl, kcv[(jj // 2, par)])
                    ds_u.append(dsu)
                    ds_l.append(dsl)
                    p_u.append(jnp.where(upper, p, 0.0).astype(BF16))
                    p_l.append(jnp.where(upper, 0.0, p).astype(BF16))
                    dsk = dsk + jnp.where(lane[0:1] == h, -jnp.sum(jnp.exp(sink_ref[0, h] - lse_h) * delta), 0.0)
                dq_car[:, 128 * jj:128 * (jj + 1)] = _rope_bwd(dq_acc, cq_ref[...], sq_ref[...]).astype(BF16)
            stack = lambda parts: jnp.concatenate(parts, axis=0)
            dk_prev, dk_cur = _dot_tn(stack(ds_u), q_all), _dot_tn(stack(ds_l), q_all)
            dv_prev, dv_cur = _dot_tn(stack(p_u), do_all), _dot_tn(stack(p_l), do_all)
            ds_ref[...] += dsk
            out_ref[:, 512:640] = _rope_bwd(dk_car[...] + dk_prev, ck_ref[...], sk_ref[...]).astype(BF16)
            out_ref[:, 640:768] = (dv_car[...] + dv_prev).astype(BF16)
            dk_car[...] = dk_cur
            dv_car[...] = dv_cur

        @pl.when(n == nb)
        def _():
            out_ref[:, 0:512] = dq_car[...]
            out_ref[:, 512:640] = _rope_bwd(dk_car[...], ck_ref[...], sk_ref[...]).astype(BF16)
            out_ref[:, 640:768] = dv_car[...].astype(BF16)

    cur = lambda n: jnp.minimum(n, nb - 1)
    prev = lambda n: jnp.maximum(cur(n) - 1, 0)
    outb = lambda n: jnp.maximum(n - 1, 0)
    return pl.pallas_call(
        body, name="attn_bwd", grid=(nb + 1,),
        in_specs=[pl.BlockSpec(memory_space=pltpu.SMEM),
                  pl.BlockSpec((BLK, 512), lambda n: (cur(n), 0)),
                  pl.BlockSpec((BLK, 128), lambda n: (cur(n), 4)),
                  pl.BlockSpec((BLK, 128), lambda n: (prev(n), 4)),
                  pl.BlockSpec((BLK, 128), lambda n: (cur(n), 5)),
                  pl.BlockSpec((BLK, 128), lambda n: (prev(n), 5)),
                  pl.BlockSpec((BLK, 128), lambda n: (cur(n), 0)),
                  pl.BlockSpec((BLK, 512), lambda n: (cur(n), 0)),
                  pl.BlockSpec((BLK, 128), lambda n: (cur(n), 0)),
                  pl.BlockSpec((BLK, 128), lambda n: (cur(n), 0)),
                  pl.BlockSpec((BLK, 128), lambda n: (outb(n), 0)),
                  pl.BlockSpec((BLK, 128), lambda n: (outb(n), 0)), DEP_SPEC],
        out_specs=[pl.BlockSpec((BLK, 768), lambda n: (outb(n), 0)), pl.BlockSpec((1, 128), lambda n: (0, 0))],
        out_shape=[jax.ShapeDtypeStruct((T, 768), BF16), jax.ShapeDtypeStruct((1, 128), F32)],
        scratch_shapes=[pltpu.VMEM((BLK, 512), BF16), pltpu.VMEM((BLK, 128), F32), pltpu.VMEM((BLK, 128), F32)],
        compiler_params=_cp("arbitrary"),
    )(sinks, qkv, qkv, qkv, qkv, qkv, lse, dmix, cos, sin_s, cos, sin_s, dep)


def _ssd_mats():
    e = jnp.arange(SW)[None, :] // HD == jnp.arange(128)[:, None]
    tri = jnp.arange(BLK)[None, :] <= jnp.arange(BLK)[:, None]
    return (jnp.tile(e, (3, 1)).astype(BF16), jnp.tile(e.T, (2, 1)).astype(BF16),
            jnp.tile(tri, (1, 3)).astype(BF16), jnp.tile(tri.T, (1, 3)).astype(BF16))


def _pieces(x, n, axis):
    out, r = [], x
    for i in range(n):
        p = r.astype(BF16)
        out.append(p)
        if i + 1 < n:
            r = r - p.astype(F32)
    return jnp.concatenate(out, axis=axis)


def _expand(x, e3):
    return _dot(_pieces(x, 3, 1), e3)


def _head_sums(x, et2):
    return _dot(_pieces(x, 2, 1), et2)


def _run_sum(tri3, x):
    return _dot(tri3, _pieces(x, 3, 0))


def _shift_down(u, tail, j):
    rolled = pltpu.roll(u, j, 0)
    first = jnp.where(_iota(tail.shape, 0) < j, pltpu.roll(tail, j, 0), rolled[0:8])
    return jnp.concatenate([first, rolled[8:]], axis=0)


def _shift_up(d, head, j):
    rolled = pltpu.roll(d, BLK - j, 0)
    last = jnp.where(_iota(head.shape, 0) >= 8 - j, pltpu.roll(head, 8 - j, 0), rolled[BLK - 8:])
    return jnp.concatenate([rolled[:BLK - 8], last], axis=0)


def _ssd_parts(dtr, dtb, alog, e3, tril3):
    xx = dtr + dtb
    dt = jnp.maximum(xx, 0.0) + jnp.log(1.0 + jnp.exp(-jnp.abs(xx)))
    a_neg = -jnp.exp(alog)
    tril = _iota((BLK, BLK), 1) <= _iota((BLK, BLK), 0)
    cs = _run_sum(tril3, dt * a_neg)
    csx = _expand(cs, e3)
    last = csx[BLK - 1:BLK, :]
    return dict(xx=xx, dt=dt, a_neg=a_neg, tril=tril, cs=cs, cs_t=cs.T,
                ecsx=jnp.exp(csx), dtex=jnp.exp(last - csx), cdx=jnp.exp(last), dtx=_expand(dt, e3))


def _decay(parts, h):
    seg = parts["cs"][:, h:h + 1] - parts["cs_t"][h:h + 1, :]
    return jnp.exp(jnp.where(parts["tril"], seg, NEG))


def _group_cols(a, g):
    return a[:, 256 * g:256 * (g + 1)]


def _ssd_fwd(xbc, z, dtr, conv_w, conv_b, dtb, alog, dskx, ssm_w, mats, dep):
    CPS = SSD_FWD_CPS
    T = xbc.shape[0]
    nc = T // BLK

    def body(u_ref, tail_ref, z_ref, dtr_ref, cw_ref, cb_ref, dtb_ref, al_ref, dk_ref, sw_ref, e3_ref, tril3_ref,
             dep_ref, yn_ref, yp_ref, st_ref, co_ref, s_scr):
        n = pl.program_id(0)

        @pl.when(n == 0)
        def _():
            s_scr[...] = jnp.zeros_like(s_scr)

        lane = _iota((BLK, 128), 1)
        lo = lane < 64
        for sub in range(CPS):
            rows = slice(BLK * sub, BLK * (sub + 1))
            u = u_ref[rows, :]
            tail = jnp.where(n > 0, tail_ref[...], 0.0) if sub == 0 else u_ref[BLK * sub - 8:BLK * sub, :]
            co = cb_ref[...] + cw_ref[3:4, :] * u
            for j in range(1, CONVK):
                co = co + cw_ref[3 - j:4 - j, :] * _shift_down(u, tail, j)
            co_ref[rows, :] = co
            xc = co * _sigmoid(co)
            pt = _ssd_parts(dtr_ref[rows, :], dtb_ref[...], al_ref[...], e3_ref[...], tril3_ref[...])
            xs = xc[:, :SW]
            bm = [xc[:, 512:640].astype(BF16), xc[:, 640:768].astype(BF16)]
            cm = [xc[:, 768:896].astype(BF16), xc[:, 896:1024].astype(BF16)]
            s_in = s_scr[...]
            st_ref[sub] = s_in
            xdt = xs * pt["dtx"]
            xde = (xdt * pt["dtex"]).astype(BF16)
            ys, s_new = [], []
            for g in range(2):
                cb = _dot_nt(cm[g], bm[g])
                yoff = _dot(cm[g], _group_cols(s_in, g).astype(BF16))
                s_new.append(_dot_tn(bm[g], _group_cols(xde, g)))
                for jj in range(2):
                    j = 2 * g + jj
                    chunk = xdt[:, 128 * j:128 * (j + 1)]
                    g_ev = (cb * _decay(pt, 2 * j)).astype(BF16)
                    g_od = (cb * _decay(pt, 2 * j + 1)).astype(BF16)
                    yd = (_dot(g_ev, jnp.where(lo, chunk, 0.0).astype(BF16))
                          + _dot(g_od, jnp.where(lo, 0.0, chunk).astype(BF16)))
                    ys.append(yd + yoff[:, 128 * jj:128 * (jj + 1)] * pt["ecsx"][:, 128 * j:128 * (j + 1)])
            y = jnp.concatenate(ys, axis=1) + xs * dk_ref[...]
            s_scr[...] = s_in * pt["cdx"] + jnp.concatenate(s_new, axis=1)
            yp_ref[rows, :] = y
            zv = z_ref[rows, :]
            yz = y * (zv * _sigmoid(zv))
            outs = []
            for g in range(2):
                yg = _group_cols(yz, g)
                outs.append(yg * lax.rsqrt(jnp.mean(yg * yg, axis=-1, keepdims=True) + EPS))
            yn_ref[rows, :] = (jnp.concatenate(outs, axis=1) * sw_ref[...]).astype(BF16)

    e3, _, tril3, _ = mats
    RB = CPS * BLK
    tail8 = lambda n: jnp.maximum(n * (RB // 8) - 1, 0)
    full = lambda a: pl.BlockSpec(a.shape, lambda n: (0,) * a.ndim)
    return pl.pallas_call(
        body, name="ssd_fwd", grid=(nc // CPS,),
        in_specs=[pl.BlockSpec((RB, CONVC), lambda n: (n, 0)), pl.BlockSpec((8, CONVC), lambda n: (tail8(n), 0)),
                  pl.BlockSpec((RB, SW), lambda n: (n, 0)), pl.BlockSpec((RB, 128), lambda n: (n, 0)),
                  full(conv_w), full(conv_b), full(dtb), full(alog), full(dskx), full(ssm_w), full(e3), full(tril3),
                  DEP_SPEC],
        out_specs=[pl.BlockSpec((RB, SW), lambda n: (n, 0)), pl.BlockSpec((RB, SW), lambda n: (n, 0)),
                   pl.BlockSpec((CPS, NST, SW), lambda n: (n, 0, 0)), pl.BlockSpec((RB, CONVC), lambda n: (n, 0))],
        out_shape=[jax.ShapeDtypeStruct((T, SW), BF16), jax.ShapeDtypeStruct((T, SW), F32),
                   jax.ShapeDtypeStruct((nc, NST, SW), F32), jax.ShapeDtypeStruct((T, CONVC), F32)],
        scratch_shapes=[pltpu.VMEM((NST, SW), F32)],
        compiler_params=_cp("arbitrary"),
    )(xbc, xbc, z, dtr, conv_w, conv_b, dtb, alog, dskx, ssm_w, e3, tril3, dep)


def _ssd_bwd(xbc, co_all, z, dtr, ypre, states, dmix, conv_w, dtb, alog, dskx, ssm_w, mats, dep):
    T = xbc.shape[0]
    nsteps = T // (CPS * BLK)

    def body(*refs):
        per_chunk, consts, out_ref, carried = refs[:7], refs[7:16], refs[17], refs[18:]
        i = pl.program_id(0)

        @pl.when(i == 0)
        def _():
            for r in carried:
                r[...] = jnp.zeros_like(r)

        for sub in reversed(range(CPS)):
            rows = slice(BLK * sub, BLK * (sub + 1))
            views = [r.at[sub:sub + 1] if k == 5 else r.at[rows, :] for k, r in enumerate(per_chunk)]
            chunk(*views, *consts, out_ref.at[rows, :], *carried)

        @pl.when(i == nsteps - 1)
        def _():
            dsk_ref, dskx_scr = carried[3], carried[8]
            dsk_ref[...] = _head_sums(jnp.broadcast_to(dskx_scr[...], (8, SW)), consts[6][...])[0:1]

    def chunk(u_ref, co_ref, z_ref, dtr_ref, yp_ref, st_ref, dyn_ref, cw_ref, dtb_ref, al_ref, dk_ref, sw_ref,
              e3_ref, et2_ref, tril3_ref, triu3_ref,
              out_ref, dcw_ref, dcb_ref, dsw_ref, dsk_ref, ddtb_ref, dav_ref, ds_scr, dco_scr, dskx_scr):
        co = co_ref[...]
        sg = _sigmoid(co)
        xc = co * sg
        pt = _ssd_parts(dtr_ref[...], dtb_ref[...], al_ref[...], e3_ref[...], tril3_ref[...])
        dtx, ecsx, dtex, cdx = pt["dtx"], pt["ecsx"], pt["dtex"], pt["cdx"]
        xs = xc[:, :SW]
        bm = [xc[:, 512:640].astype(BF16), xc[:, 640:768].astype(BF16)]
        cm = [xc[:, 768:896].astype(BF16), xc[:, 896:1024].astype(BF16)]
        s_in = st_ref[0]
        ds_out = ds_scr[...]
        e_t = et2_ref[...]

        zv = z_ref[...]
        sz = _sigmoid(zv)
        silu_z = zv * sz
        ypre = yp_ref[...]
        yz = ypre * silu_z
        dyn = dyn_ref[...]
        sw = sw_ref[...]
        dyz, yns = [], []
        for g in range(2):
            yg = _group_cols(yz, g)
            r = lax.rsqrt(jnp.mean(yg * yg, axis=-1, keepdims=True) + EPS)
            yn = yg * r
            dg = _group_cols(dyn, g) * _group_cols(sw, g)
            dyz.append(r * (dg - yn * jnp.mean(dg * yn, axis=-1, keepdims=True)))
            yns.append(yn)
        dyz = jnp.concatenate(dyz, axis=1)
        dsw_ref[...] += jnp.sum(dyn * jnp.concatenate(yns, axis=1), axis=0, keepdims=True)
        dy = dyz * silu_z
        dz = dyz * ypre * (sz * (1.0 + zv * (1.0 - sz)))

        xdt = xs * dtx
        xdt_b = xdt.astype(BF16)
        edy = (ecsx * dy).astype(BF16)
        xde = (xdt * dtex).astype(BF16)
        lane = _iota((BLK, 128), 1)
        lo = lane < 64
        row8 = _iota((8, 128), 0)
        dcs = jnp.zeros((BLK, 128), F32)
        col_rows = jnp.zeros((8, 128), F32)
        dxdt, bds, yoff, dbs, dcs_g, ds_new = [], [], [], [], [], []
        for g in range(2):
            s_g = _group_cols(s_in, g).astype(BF16)
            dso_g = _group_cols(ds_out, g).astype(BF16)
            cb = _dot_nt(cm[g], bm[g])
            bds.append(_dot(bm[g], dso_g))
            yoff.append(_dot(cm[g], s_g))
            dcb_g = jnp.zeros((BLK, BLK), F32)
            for jj in range(2):
                j = 2 * g + jj
                dy_c = dy[:, 128 * j:128 * (j + 1)]
                xdt_c = xdt_b[:, 128 * j:128 * (j + 1)]
                acc = jnp.zeros((BLK, 128), F32)
                for par in range(2):
                    h = 2 * j + par
                    lm = _decay(pt, h)
                    gm = cb * lm
                    dy_m = (jnp.where(lo, dy_c, 0.0) if par == 0 else jnp.where(lo, 0.0, dy_c)).astype(BF16)
                    dg_h = _dot_nt(dy_m, xdt_c)
                    w_h = dg_h * gm
                    dcs = dcs + jnp.where(lane == h, jnp.sum(w_h, axis=1, keepdims=True), 0.0)
                    col_rows = col_rows + jnp.where(row8 == h, jnp.sum(w_h, axis=0, keepdims=True), 0.0)
                    dcb_g = dcb_g + dg_h * lm
                    acc = acc + _dot_tn(gm.astype(BF16), dy_m)
                dxdt.append(acc)
            dcb_b = dcb_g.astype(BF16)
            dcs_g.append(_dot(dcb_b, bm[g]) + _dot_nt(_group_cols(edy, g), s_g))
            dbs.append(_dot_tn(dcb_b, cm[g]) + _dot_nt(_group_cols(xde, g), dso_g))
            ds_new.append(_dot_tn(cm[g], _group_cols(edy, g)))
        bds = jnp.concatenate(bds, axis=1)
        yoff = jnp.concatenate(yoff, axis=1) * ecsx
        dxdt = jnp.concatenate(dxdt, axis=1) + dtex * bds
        ds_scr[...] = cdx * ds_out + jnp.concatenate(ds_new, axis=1)

        t_m = _head_sums(dtex * xdt * bds, e_t)
        colsum_t = jnp.concatenate([col_rows, jnp.zeros((BLK - 8, 128), F32)], axis=0).T
        cd = jnp.exp(pt["cs"][BLK - 1:BLK, :])
        sds = jnp.sum(s_in * ds_out, axis=0, keepdims=True)
        last_row = jnp.sum(t_m, axis=0, keepdims=True) + cd * _head_sums(jnp.broadcast_to(sds, (8, SW)), e_t)[0:1]
        dcs = dcs - colsum_t + _head_sums(dy * yoff, e_t) - t_m
        dcs = dcs + jnp.where(_iota((BLK, 128), 0) == BLK - 1, last_row, 0.0)
        da = _run_sum(triu3_ref[...], dcs)
        dt = pt["dt"]
        ddt = da * pt["a_neg"] + _head_sums(dxdt * xs, e_t)
        dav_ref[...] += jnp.sum(da * dt, axis=0, keepdims=True)
        ddtr = ddt * _sigmoid(pt["xx"])
        ddtb_ref[...] += jnp.sum(ddtr, axis=0, keepdims=True)
        dxs = dxdt * dtx + dy * dk_ref[...]
        dskx_scr[...] += jnp.sum(dy * xs, axis=0, keepdims=True)
        dxc = jnp.concatenate([dxs, dbs[0], dbs[1], dcs_g[0], dcs_g[1]], axis=1)
        dco = dxc * (sg * (1.0 + co * (1.0 - sg)))

        dcb_ref[...] += jnp.sum(dco, axis=0, keepdims=True)
        u = u_ref[...]
        head = dco_scr[...]
        du = jnp.zeros_like(dco)
        for j in range(CONVK):
            up_j = dco if j == 0 else _shift_up(dco, head, j)
            dcw_ref[3 - j:4 - j, :] += jnp.sum(up_j * u, axis=0, keepdims=True)
            du = du + cw_ref[3 - j:4 - j, :] * up_j
        dco_scr[...] = dco[0:8]
        out_ref[:, 0:512] = dz.astype(BF16)
        out_ref[:, 512:1536] = du.astype(BF16)
        out_ref[:, 1536:1664] = ddtr.astype(BF16)

    e3, et2, tril3, triu3 = mats
    RB = CPS * BLK
    rev = lambda i: nsteps - 1 - i
    full = lambda a: pl.BlockSpec(a.shape, lambda i: (0,) * a.ndim)
    acc = lambda r, c: pl.BlockSpec((r, c), lambda i: (0, 0))
    return pl.pallas_call(
        body, name="ssd_bwd", grid=(nsteps,),
        in_specs=[pl.BlockSpec((RB, CONVC), lambda i: (rev(i), 0)), pl.BlockSpec((RB, CONVC), lambda i: (rev(i), 0)),
                  pl.BlockSpec((RB, SW), lambda i: (rev(i), 0)), pl.BlockSpec((RB, 128), lambda i: (rev(i), 0)),
                  pl.BlockSpec((RB, SW), lambda i: (rev(i), 0)), pl.BlockSpec((CPS, NST, SW), lambda i: (rev(i), 0, 0)),
                  pl.BlockSpec((RB, SW), lambda i: (rev(i), 1)),
                  full(conv_w), full(dtb), full(alog), full(dskx), full(ssm_w),
                  full(e3), full(et2), full(tril3), full(triu3), DEP_SPEC],
        out_specs=[pl.BlockSpec((RB, 1664), lambda i: (rev(i), 0)),
                   acc(CONVK, CONVC), acc(1, CONVC), acc(1, SW), acc(1, 128), acc(1, 128), acc(1, 128)],
        out_shape=[jax.ShapeDtypeStruct((T, 1664), BF16),
                   jax.ShapeDtypeStruct((CONVK, CONVC), F32), jax.ShapeDtypeStruct((1, CONVC), F32),
                   jax.ShapeDtypeStruct((1, SW), F32), jax.ShapeDtypeStruct((1, 128), F32),
                   jax.ShapeDtypeStruct((1, 128), F32), jax.ShapeDtypeStruct((1, 128), F32)],
        scratch_shapes=[pltpu.VMEM((NST, SW), F32), pltpu.VMEM((8, CONVC), F32), pltpu.VMEM((1, SW), F32)],
        compiler_params=_cp("arbitrary"),
    )(xbc, co_all, z, dtr, ypre, states, dmix, conv_w, dtb, alog, dskx, ssm_w, e3, et2, tril3, triu3, dep)


def _mix_ffn(x, attn, ynorm, tgt, mod6, norm2_w, final_w, w_out, w_gu, w_gu_own, s_arr, w_dn, tm):
    T = x.shape[0]
    nt = T // tm

    def body(x_ref, a_ref, y_ref, t_ref, mod_ref, n2_ref, fw_ref, wo_hbm, wgu_hbm, own_hbm, s_ref, wdn_hbm,
             sq_ref, dmix_ref, dx1_ref, h2_ref, act_ref, df_ref, dgu_ref, do_ref, sm_ref,
             wo, wgu, wdn, sems):
        i = pl.program_id(0)

        @pl.when(i == 0)
        def _():
            cps = [pltpu.make_async_copy(s, d, sems.at[k]) for k, (s, d) in
                   enumerate(((wo_hbm, wo), (wgu_hbm, wgu), (wdn_hbm, wdn)))]
            for c in cps:
                c.start()
            for c in cps:
                c.wait()
            own = pltpu.make_async_copy(
                own_hbm, wgu.at[:, pl.ds(pl.multiple_of(s_ref[0] * GU_SH, 128), GU_SH)], sems.at[3])
            own.start()
            own.wait()
            sq_ref[...] = jnp.zeros_like(sq_ref)
            sm_ref[...] = jnp.zeros_like(sm_ref)

        gate1, shift2, scale2, gate2 = mod_ref[2:3, :], mod_ref[3:4, :], mod_ref[4:5, :], mod_ref[5:6, :]
        n2w, fw = n2_ref[...], fw_ref[...]
        o = _dot(a_ref[...], wo[0:AW, :]) + _dot(y_ref[...], wo[AW:D, :])
        x1 = x_ref[...] + gate1 * o
        r2 = lax.rsqrt(jnp.mean(x1 * x1, axis=-1, keepdims=True) + EPS)
        xh2 = x1 * r2
        n2 = xh2 * n2w
        h2b = (n2 * (1.0 + scale2) + shift2).astype(BF16)
        h2_ref[...] = h2b
        f = jnp.zeros((tm, D), F32)
        saved = []
        for a, b in FF_SPLITS:
            gp = _dot(h2b, wgu[:, a:b])
            upj = _dot(h2b, wgu[:, DFF + a:DFF + b])
            sg = _sigmoid(gp)
            sl = gp * sg
            actb = (sl * upj).astype(BF16)
            act_ref[:, a:b] = actb
            f = f + _dot(actb, wdn[a:b, :])
            saved.append((gp, upj, sg, sl))
        x2 = x1 + gate2 * f
        r3 = lax.rsqrt(jnp.mean(x2 * x2, axis=-1, keepdims=True) + EPS)
        xh3 = x2 * r3
        err = xh3 * fw - t_ref[...]
        sq_ref[...] += jnp.sum(err * err, axis=0, keepdims=True)
        dy = err * (1.0 / D)
        dfw = jnp.sum(dy * xh3, axis=0, keepdims=True)
        dxh3 = dy * fw
        dx2 = r3 * (dxh3 - xh3 * jnp.mean(dxh3 * xh3, axis=-1, keepdims=True))
        dgate2 = jnp.sum(dx2 * f, axis=0, keepdims=True)
        dfb = (dx2 * gate2).astype(BF16)
        df_ref[...] = dfb
        dh2 = jnp.zeros((tm, D), F32)
        for (a, b), (gp, upj, sg, sl) in zip(FF_SPLITS, saved):
            dact = _dot_nt(dfb, wdn[a:b, :])
            dg = (dact * upj * (sg * (1.0 + gp * (1.0 - sg)))).astype(BF16)
            du = (dact * sl).astype(BF16)
            dgu_ref[:, a:b] = dg
            dgu_ref[:, DFF + a:DFF + b] = du
            dh2 = dh2 + _dot_nt(dg, wgu[:, a:b]) + _dot_nt(du, wgu[:, DFF + a:DFF + b])
        dshift2 = jnp.sum(dh2, axis=0, keepdims=True)
        dscale2 = jnp.sum(dh2 * n2, axis=0, keepdims=True)
        dn2 = dh2 * (1.0 + scale2)
        dn2w = jnp.sum(dn2 * xh2, axis=0, keepdims=True)
        dxh2 = dn2 * n2w
        dx1 = dx2 + r2 * (dxh2 - xh2 * jnp.mean(dxh2 * xh2, axis=-1, keepdims=True))
        dx1_ref[...] = dx1
        dgate1 = jnp.sum(dx1 * o, axis=0, keepdims=True)
        dob = (dx1 * gate1).astype(BF16)
        do_ref[...] = dob
        dmix_ref[...] = _dot_nt(dob, wo[...])
        sm_ref[...] += jnp.concatenate(
            [dfw, dn2w, dshift2, dscale2, dgate2, dgate1, jnp.zeros((2, D), F32)], axis=0)

    row = lambda w: pl.BlockSpec((tm, w), lambda i: (i, 0))
    full = lambda a: pl.BlockSpec(a.shape, lambda i: (0,) * a.ndim)
    anyspec = pl.BlockSpec(memory_space=pl.ANY)
    return pl.pallas_call(
        body, name="mix_ffn", grid=(nt,),
        in_specs=[row(D), row(AW), row(SW), row(D), full(mod6), full(norm2_w), full(final_w), anyspec, anyspec, anyspec,
                  pl.BlockSpec(memory_space=pltpu.SMEM), anyspec],
        out_specs=[pl.BlockSpec((1, D), lambda i: (0, 0)), row(D), row(D), row(D),
                   row(DFF), row(D), row(2 * DFF), row(D), pl.BlockSpec((8, D), lambda i: (0, 0))],
        out_shape=[jax.ShapeDtypeStruct((1, D), F32), jax.ShapeDtypeStruct((T, D), F32), jax.ShapeDtypeStruct((T, D), F32),
                   jax.ShapeDtypeStruct((T, D), BF16), jax.ShapeDtypeStruct((T, DFF), BF16),
                   jax.ShapeDtypeStruct((T, D), BF16), jax.ShapeDtypeStruct((T, 2 * DFF), BF16),
                   jax.ShapeDtypeStruct((T, D), BF16), jax.ShapeDtypeStruct((8, D), F32)],
        scratch_shapes=[pltpu.VMEM((D, D), BF16), pltpu.VMEM((D, 2 * DFF), BF16), pltpu.VMEM((DFF, D), BF16),
                        pltpu.SemaphoreType.DMA((4,))],
        compiler_params=_cp("arbitrary"),
    )(x, attn, ynorm, tgt, mod6, norm2_w, final_w, w_out, w_gu, w_gu_own, s_arr, w_dn)


def _in_proj_bwd(x, dx1, dqkv, dzxd, mod6, norm1_w, w_pad, tm, dep):
    T = x.shape[0]

    def body(x_ref, dx1_ref, dq_ref, dz_ref, mod_ref, nw_ref, w_hbm, dep_ref, gx_ref, sm_ref, w_vmem, sem):
        _load_resident(w_hbm, w_vmem, sem)

        @pl.when(pl.program_id(0) == 0)
        def _():
            sm_ref[...] = jnp.zeros_like(sm_ref)

        nw = nw_ref[...]
        scale1 = mod_ref[1:2, :]
        sums = jnp.zeros((8, D), F32)
        for rows in (slice(0, tm // 2), slice(tm // 2, tm)):
            dh = _dot_nt(dq_ref[rows, :], w_vmem[:, 0:768]) + _dot_nt(dz_ref[rows, :], w_vmem[:, 768:IN_PAD])
            xv = x_ref[rows, :]
            r = lax.rsqrt(jnp.mean(xv * xv, axis=-1, keepdims=True) + EPS)
            xh = xv * r
            n1 = xh * nw
            dshift = jnp.sum(dh, axis=0, keepdims=True)
            dscale = jnp.sum(dh * n1, axis=0, keepdims=True)
            dn = dh * (1.0 + scale1)
            dnw = jnp.sum(dn * xh, axis=0, keepdims=True)
            dxh = dn * nw
            gx_ref[rows, :] = dx1_ref[rows, :] + r * (dxh - xh * jnp.mean(dxh * xh, axis=-1, keepdims=True))
            sums = sums + jnp.concatenate([dnw, dshift, dscale, jnp.zeros((5, D), F32)], axis=0)
        sm_ref[...] += sums

    row = lambda w: pl.BlockSpec((tm, w), lambda i: (i, 0))
    full = lambda a: pl.BlockSpec(a.shape, lambda i: (0,) * a.ndim)
    return pl.pallas_call(
        body, name="in_proj_bwd", grid=(T // tm,),
        in_specs=[row(D), row(D), row(768), row(1664), full(mod6), full(norm1_w), pl.BlockSpec(memory_space=pl.ANY),
                  DEP_SPEC],
        out_specs=[row(D), pl.BlockSpec((8, D), lambda i: (0, 0))],
        out_shape=[jax.ShapeDtypeStruct((T, D), F32), jax.ShapeDtypeStruct((8, D), F32)],
        scratch_shapes=[pltpu.VMEM((D, IN_PAD), BF16), pltpu.SemaphoreType.DMA],
        compiler_params=_cp("arbitrary"),
    )(x, dx1, dqkv, dzxd, mod6, norm1_w, w_pad, dep)


def _tn_matmul(a, b, K, N, tt, name, dep):
    T = a.shape[0]
    ja, jb = a.shape[1] // K, b.shape[1] // N
    J = max(ja, jb)

    def body(a_ref, b_ref, dep_ref, o_ref):
        t = pl.program_id(1)
        prod = _dot_tn(a_ref[...], b_ref[...])

        @pl.when(t == 0)
        def _():
            o_ref[0] = prod

        @pl.when(t > 0)
        def _():
            o_ref[0] += prod

    return pl.pallas_call(
        body, name=name, grid=(J, T // tt),
        in_specs=[pl.BlockSpec((tt, K), lambda j, t: (t, j if ja > 1 else 0)),
                  pl.BlockSpec((tt, N), lambda j, t: (t, j if jb > 1 else 0)),
                  pl.BlockSpec((8, 128), lambda j, t: (0, 0))],
        out_specs=pl.BlockSpec((1, K, N), lambda j, t: (j, 0, 0)),
        out_shape=jax.ShapeDtypeStruct((J, K, N), F32),
        compiler_params=_cp("parallel", "arbitrary"),
    )(a, b, dep)


def _accumulate(o_ref, rows, prod):
    @pl.when(pl.program_id(0) == 0)
    def _():
        o_ref[rows, :] = prod

    @pl.when(pl.program_id(0) > 0)
    def _():
        o_ref[rows, :] += prod


def _tn_matmul_rows(a0, a1, b, tt, name, dep):
    T, K = a0.shape
    N = b.shape[1]

    def body(a0_ref, a1_ref, b_ref, dep_ref, o_ref):
        for k, a_ref in enumerate((a0_ref, a1_ref)):
            _accumulate(o_ref, slice(k * K, (k + 1) * K), _dot_tn(a_ref[...], b_ref[...]))

    tile = lambda w: pl.BlockSpec((tt, w), lambda t: (t, 0))
    return pl.pallas_call(
        body, name=name, grid=(T // tt,), in_specs=[tile(K), tile(K), tile(N), DEP_SPEC],
        out_specs=pl.BlockSpec((2 * K, N), lambda t: (0, 0)), out_shape=jax.ShapeDtypeStruct((2 * K, N), F32),
        compiler_params=_cp("arbitrary"),
    )(a0, a1, b, dep)


def _adam_math(w, g, m, v):
    m = B1 * m + (1.0 - B1) * g
    v = B2 * v + (1.0 - B2) * (g * g)
    m_hat = m / (1.0 - B1 ** STEP)
    v_hat = v / (1.0 - B2 ** STEP)
    delta = -LR * (m_hat / (jnp.sqrt(v_hat) + AEPS) + WD * w)
    return delta, m, v


def _adam_2d(w, mine, land, m, v, c_arr, rb, name, dep):
    R, C = w.shape
    nbh = R // 2 // rb

    def body(c_ref, w_ref, mine_ref, land_ref, m_ref, v_ref, dep_ref, go_ref, d_ref, mo_ref, vo_ref):
        g = jnp.where(pl.program_id(0) // nbh == c_ref[0], mine_ref[...], land_ref[...])
        d, mn, vn = _adam_math(w_ref[...], g, m_ref[...], v_ref[...])
        go_ref[...] = g
        d_ref[...] = d
        mo_ref[...] = mn
        vo_ref[...] = vn

    spec = pl.BlockSpec((rb, C), lambda i, c_ref: (i, 0))
    mine_spec = pl.BlockSpec((rb, C), lambda i, c_ref: (jnp.clip(i - c_ref[0] * nbh, 0, nbh - 1), 0))
    return pl.pallas_call(
        body, name=name,
        grid_spec=pltpu.PrefetchScalarGridSpec(
            num_scalar_prefetch=1, grid=(R // rb,), in_specs=[spec, mine_spec, spec, spec, spec, DEP_SPEC],
            out_specs=[spec] * 4),
        out_shape=[jax.ShapeDtypeStruct((R, C), F32)] * 4, compiler_params=_cp("parallel"),
    )(c_arr, w, mine, land, m, v, dep)


def _adam_w_in(w3, grad, m3, v3):
    n = w3.shape[0]

    def body(w_hbm, grad_ref, m_hbm, v_hbm, g_hbm, d_hbm, mo_hbm, vo_hbm, bufs, sems):
        ins = [pltpu.make_async_copy(src.at[:, 0], bufs.at[k], sems.at[k]) for k, src in enumerate((w_hbm, m_hbm, v_hbm))]
        for cp in ins:
            cp.start()
        g = grad_ref[...]
        eye =(_iota((D, D), 0) == _iota((D, D), 1)).astype(BF16)
        g_t = jnp.zeros((n, D), F32)
        r = g
        for i in range(3):
            p = r.astype(BF16)
            g_t = g_t + _dot_tn(p, eye)
            if i < 2:
                r = r - p.astype(F32)
        for cp in ins:
            cp.wait()
        d, mn, vn = _adam_math(bufs[0], g_t, bufs[1], bufs[2])
        for k, val in enumerate((g_t, d, mn, vn)):
            bufs[3 + k] = val
        outs = [pltpu.make_async_copy(bufs.at[3 + k], dst.at[:, 0], sems.at[3 + k])
                for k, dst in enumerate((g_hbm, d_hbm, mo_hbm, vo_hbm))]
        for cp in outs:
            cp.start()
        for cp in outs:
            cp.wait()

    anyspec = pl.BlockSpec(memory_space=pl.ANY)
    vm = pl.BlockSpec(memory_space=pltpu.VMEM)
    return pl.pallas_call(
        body, name="adam_w_in",
        in_specs=[anyspec, vm, anyspec, anyspec], out_specs=[anyspec] * 4,
        out_shape=[jax.ShapeDtypeStruct(w3.shape, F32)] * 4,
        scratch_shapes=[pltpu.VMEM((7, n, D), F32), pltpu.SemaphoreType.DMA((7,))],
        compiler_params=pltpu.CompilerParams(vmem_limit_bytes=VMEM_LIMIT),
    )(w3, grad, m3, v3)


def _adam_w_ada(gat, allv, s_arr, w, m, v, rb):
    R, C = w.shape

    def body(s_ref, c_ref, dm_ref, w_ref, m_ref, v_ref, g_ref, d_ref, mo_ref, vo_ref):
        cm = _rows_select(c_ref, rb)
        g = lax.dot_general(cm * _sigmoid(cm), _rows_select(dm_ref, C), (((0,), (0,)), ((), ())), precision=HI,
                            preferred_element_type=F32)
        d, mn, vn = _adam_math(w_ref[...], g, m_ref[...], v_ref[...])
        g_ref[...] = g
        d_ref[...] = d
        mo_ref[...] = mn
        vo_ref[...] = vn

    spec = pl.BlockSpec((rb, C), lambda i, s_ref: (i, 0))
    return pl.pallas_call(
        body, name="adam_w_ada",
        grid_spec=pltpu.PrefetchScalarGridSpec(
            num_scalar_prefetch=1, grid=(R // rb,),
            in_specs=[pl.BlockSpec((8, 1, rb), lambda i, s_ref: (0, 0, i)),
                      pl.BlockSpec((8, 1, C), lambda i, s_ref: (0, 0, s_ref[0])), spec, spec, spec],
            out_specs=[spec] * 4),
        out_shape=[jax.ShapeDtypeStruct((R, C), F32)] * 4, compiler_params=_cp("parallel"),
    )(s_arr, gat, allv, w, m, v)


def _adam_small(tot, segs, ws, ms, vs):
    k = len(ws)
    extra = [sg for sg in segs if not isinstance(sg, tuple)]
    ne = len(extra)

    def body(*refs):
        tot_ref, g_x = refs[0], list(refs[1:1 + ne])
        w, m, v = [refs[1 + ne + j * k:1 + ne + (j + 1) * k] for j in range(3)]
        g_o, d_o, m_o, v_o = [refs[1 + ne + (3 + j) * k:1 + ne + (4 + j) * k] for j in range(4)]
        for i in range(k):
            gi = tot_ref[:, segs[i][0]:segs[i][0] + segs[i][1]] if isinstance(segs[i], tuple) else g_x.pop(0)[...]
            d, mn, vn = _adam_math(w[i][...], gi, m[i][...], v[i][...])
            g_o[i][...] = gi
            d_o[i][...] = d
            m_o[i][...] = mn
            v_o[i][...] = vn

    shapes = [jax.ShapeDtypeStruct(w.shape, F32) for w in ws]
    vm = pl.BlockSpec(memory_space=pltpu.VMEM)
    outs = pl.pallas_call(
        body, name="adam_small", in_specs=[vm] * (1 + ne + 3 * k), out_specs=[vm] * (4 * k), out_shape=shapes * 4,
    )(tot, *extra, *ws, *ms, *vs)
    return outs[0:k], outs[k:2 * k], outs[2 * k:3 * k], outs[3 * k:4 * k]


def _pos():
    return lax.axis_index("x"), lax.axis_index("y"), lax.axis_index("c")


def _flip(v, bit):
    return 1 - v if bit else v


def _peer(k):
    x, y, c = _pos()
    return (_flip(x, (k >> 2) & 1), _flip(y, (k >> 1) & 1), _flip(c, k & 1))


def _logical(p):
    return 4 * p[0] + 2 * p[1] + p[2]


def _gather8(src_ref, dst_ref, send_sems, recv_sems):
    me = _logical(_pos())
    dst_ref[pl.ds(me, 1)] = src_ref[...][None]
    copies = []
    for k in range(1, 8):
        cp = pltpu.make_async_remote_copy(src_ref, dst_ref.at[me], send_sems.at[k - 1], recv_sems.at[k - 1],
                                          device_id=_peer(k), device_id_type=MESH)
        cp.start()
        copies.append(cp)
    for k in range(1, 8):
        pltpu.make_async_remote_copy(src_ref, dst_ref.at[_logical(_peer(k))], send_sems.at[k - 1], recv_sems.at[k - 1],
                                     device_id=_peer(k), device_id_type=MESH).wait_recv()
    for cp in copies:
        cp.wait_send()


def _rows_select(ref3, width):
    row = _iota((8, width), 0)
    out = jnp.zeros((8, width), F32)
    for i in range(8):
        out = jnp.where(row == i, ref3[i][:, 0:width], out)
    return out


def _mod_exchange(payload, w_ada_s, b_ada4):
    n_sh = w_ada_s.shape[1]

    def body(pay_ref, w_ref, b_ref, gat_ref, mod_ref, token, p3, sa, ra, sb, rb):
        token[...] = jnp.zeros_like(token)
        x, y, c = _pos()
        me = _logical((x, y, c))
        my_s = 2 * x + y
        _gather8(pay_ref, gat_ref, sa, ra)
        cmat = _rows_select(gat_ref, D)
        prod = _dot_hi(cmat * _sigmoid(cmat), w_ref[...])
        for b in range(8):
            p3[b] = prod[b:b + 1, :]
        mod_ref[pl.ds(my_s, 1)] = p3[pl.ds(me, 1)] + b_ref[pl.ds(my_s, 1)]
        ks = (2, 4, 6)
        copies = []
        for i, k in enumerate(ks):
            pr = _peer(k)
            cp = pltpu.make_async_remote_copy(p3.at[_logical(pr)], mod_ref.at[my_s], sb.at[i], rb.at[i],
                                              device_id=pr, device_id_type=MESH)
            cp.start()
            copies.append(cp)
        for i, k in enumerate(ks):
            pr = _peer(k)
            s_src = 2 * pr[0] + pr[1]
            pltpu.make_async_remote_copy(p3.at[0], mod_ref.at[s_src], sb.at[i], rb.at[i],
                                         device_id=pr, device_id_type=MESH).wait_recv()
            mod_ref[pl.ds(s_src, 1)] = mod_ref[pl.ds(s_src, 1)] + b_ref[pl.ds(s_src, 1)]
        for cp in copies:
            cp.wait_send()

    vm = pl.BlockSpec(memory_space=pltpu.VMEM)
    return pl.pallas_call(
        body, name="mod_exchange", in_specs=[vm, vm, vm], out_specs=[vm, vm, vm],
        out_shape=[jax.ShapeDtypeStruct((8, 1, payload.shape[1]), F32), jax.ShapeDtypeStruct((4, 1, n_sh), F32),
                   jax.ShapeDtypeStruct((8, 128), F32)],
        scratch_shapes=[pltpu.VMEM((8, 1, n_sh), F32), pltpu.SemaphoreType.DMA((7,)), pltpu.SemaphoreType.DMA((7,)),
                        pltpu.SemaphoreType.DMA((3,)), pltpu.SemaphoreType.DMA((3,))],
        compiler_params=pltpu.CompilerParams(vmem_limit_bytes=VMEM_LIMIT),
    )(payload, w_ada_s, b_ada4)


def _chips():
    x, y, _ = _pos()
    out = []
    for k in (1, 2, 3):
        px, py = _flip(x, (k >> 1) & 1), _flip(y, k & 1)
        out.append((px, py, 2 * px + py))
    return out


def _half_rows(ref, which):
    half = ref.shape[-2] // 2
    return pl.ds(pl.multiple_of(which * half, 8), half)


def _plan_small():
    def plan(refs):
        me = _logical(_pos())
        return [(refs[0], refs[1].at[me], _peer(k), refs[1].at[_logical(_peer(k))]) for k in range(1, 8)]
    return plan


def _small_sum(vec, land, me_arr):
    n = vec.shape[1]

    def body(me_ref, v_ref, land_ref, tot_ref, all_ref):
        tot = None
        for i in range(8):
            row = jnp.where(me_ref[0] == i, v_ref[...], land_ref[i])
            all_ref[i] = row
            tot = row if i == 0 else tot + row
        tot_ref[...] = tot

    return pl.pallas_call(
        body, name="small_sum",
        grid_spec=pltpu.PrefetchScalarGridSpec(
            num_scalar_prefetch=1, grid=(1,),
            in_specs=[pl.BlockSpec((1, n), lambda i, me_ref: (0, 0)), pl.BlockSpec((8, 1, n), lambda i, me_ref: (0, 0, 0))],
            out_specs=[pl.BlockSpec((1, n), lambda i, me_ref: (0, 0)),
                       pl.BlockSpec((8, 1, n), lambda i, me_ref: (0, 0, 0))]),
        out_shape=[jax.ShapeDtypeStruct((1, n), F32), jax.ShapeDtypeStruct((8, 1, n), F32)],
        compiler_params=_cp("arbitrary"),
    )(me_arr, vec, land)


def _add_half(g, sib, c_arr, rb, name):
    _, R, C = g.shape
    half = R // 2
    nb = half // rb

    def body(c_ref, g_ref, s_ref, o_ref):
        o_ref[...] = (g_ref[...] + s_ref[...]).astype(BF16)

    return pl.pallas_call(
        body, name=name,
        grid_spec=pltpu.PrefetchScalarGridSpec(
            num_scalar_prefetch=1, grid=(4, nb),
            in_specs=[pl.BlockSpec((1, rb, C), lambda s, i, c_ref: (s, c_ref[0] * nb + i, 0)),
                      pl.BlockSpec((1, rb, C), lambda s, i, c_ref: (s, i, 0))],
            out_specs=pl.BlockSpec((1, rb, C), lambda s, i, c_ref: (s, i, 0))),
        out_shape=jax.ShapeDtypeStruct((4, half, C), BF16),
        compiler_params=_cp("parallel", "parallel"),
    )(c_arr, g, sib)


def _add_half_in(gq, gz, sibq, sibz, c_arr, rb):
    half = D // 2
    nq = gq.shape[1]
    wide = -(-IN_SH // 128) * 128

    def sel(rows, first, lo):
        return (_iota((rows, wide), 0) + (first - lo) == _iota((rows, wide), 1)).astype(BF16)

    def body(c_ref, gq_ref, gz_ref, sq_ref, sz_ref, o_ref):
        q = (gq_ref[...] + sq_ref[...]).astype(BF16)
        z = (gz_ref[...] + sz_ref[...]).astype(BF16)
        for s in range(4):
            lo, hi = s * IN_SH, (s + 1) * IN_SH
            acc = jnp.zeros((rb, wide), F32)
            if lo < nq:
                a0, a1 = lo // 128 * 128, min(nq, -(-min(hi, nq) // 128) * 128)
                acc = acc + _dot(q[:, a0:a1], sel(a1 - a0, a0, lo))
            if hi > nq:
                a0, a1 = (max(lo, nq) - nq) // 128 * 128, -(-(hi - nq) // 128) * 128
                acc = acc + _dot(z[:, a0:a1], sel(a1 - a0, nq + a0, lo))
            o_ref[s] = acc[:, :IN_SH].astype(BF16)

    nb = half // rb
    mine = lambda w: pl.BlockSpec((rb, w), lambda i, c_ref: (c_ref[0] * nb + i, 0))
    sib = lambda w: pl.BlockSpec((rb, w), lambda i, c_ref: (i, 0))
    return pl.pallas_call(
        body, name="grad_add_in",
        grid_spec=pltpu.PrefetchScalarGridSpec(
            num_scalar_prefetch=1, grid=(nb,),
            in_specs=[mine(nq), mine(gz.shape[1]), sib(nq), sib(gz.shape[1])],
            out_specs=pl.BlockSpec((4, rb, IN_SH), lambda i, c_ref: (0, i, 0))),
        out_shape=jax.ShapeDtypeStruct((4, half, IN_SH), BF16),
        compiler_params=_cp("parallel"),
    )(c_arr, gq, gz, sibq, sibz)


def _sum4(parts, land, s_arr, rb, name):
    _, H, C = land.shape

    def body(s_ref, own_ref, r_ref, o_ref):
        own = own_ref[0].astype(F32)
        tot = jnp.zeros((rb, C), F32)
        for j in range(4):
            tot = tot + jnp.where(s_ref[0] == j, own, r_ref[j].astype(F32))
        o_ref[...] = tot

    return pl.pallas_call(
        body, name=name,
        grid_spec=pltpu.PrefetchScalarGridSpec(
            num_scalar_prefetch=1, grid=(H // rb,),
            in_specs=[pl.BlockSpec((1, rb, C), lambda i, s_ref: (s_ref[0], i, 0)),
                      pl.BlockSpec((4, rb, C), lambda i, s_ref: (0, i, 0))],
            out_specs=pl.BlockSpec((rb, C), lambda i, s_ref: (i, 0))),
        out_shape=jax.ShapeDtypeStruct((H, C), F32), compiler_params=_cp("parallel"),
    )(s_arr, parts, land)


HBM_SPEC = pl.BlockSpec(memory_space=pltpu.HBM)
SEM_SPEC = pl.BlockSpec(memory_space=pltpu.SEMAPHORE)
EFFECT = pltpu.SideEffectType.DATAFLOW_SIDE_EFFECTING


def _split_start(name, bufs, n_sem, plan, dep):
    nb = len(bufs)

    def body(*refs):
        ins, send, recv, token = refs[:nb], refs[nb + 1], refs[nb + 2], refs[-1]
        for i, (src, dst, dev, _) in enumerate(plan(ins)):
            pltpu.make_async_remote_copy(src, dst, send.at[i], recv.at[i], device_id=dev, device_id_type=MESH).start()
        token[...] = jnp.zeros_like(token)

    outs = pl.pallas_call(
        body, name=name,
        out_shape=(pltpu.SemaphoreType.DMA((n_sem,)), pltpu.SemaphoreType.DMA((n_sem,)),
                   *[pltpu.HBM(b.shape, b.dtype) for b in bufs], jax.ShapeDtypeStruct((8, 128), F32)),
        in_specs=[HBM_SPEC] * nb + [pl.BlockSpec(memory_space=pl.ANY)],
        out_specs=(SEM_SPEC, SEM_SPEC, *([HBM_SPEC] * nb), pl.BlockSpec(memory_space=pltpu.VMEM)),
        input_output_aliases={i: 2 + i for i in range(nb)},
        compiler_params=pltpu.CompilerParams(has_side_effects=EFFECT),
    )(*[pltpu.with_memory_space_constraint(b, pltpu.HBM) for b in bufs], dep)
    return outs[0], outs[1], list(outs[2:2 + nb]), outs[-1]


def _split_wait(name, send, recv, bufs, after, plan):
    nb = len(bufs)
    after = list(after) if isinstance(after, (list, tuple)) else [after]

    def body(*refs):
        ins, send_s, recv_s = refs[:nb], refs[nb], refs[nb + 1]
        for i, (src, dst, dev, mine) in enumerate(plan(ins)):
            pltpu.make_async_remote_copy(src, dst, send_s.at[i], recv_s.at[i], device_id=dev,
                                         device_id_type=MESH).wait_send()
            pltpu.make_async_remote_copy(src, mine, send_s.at[i], recv_s.at[i], device_id=dev,
                                         device_id_type=MESH).wait_recv()

    outs = pl.pallas_call(
        body, name=name, out_shape=[pltpu.HBM(b.shape, b.dtype) for b in bufs],
        in_specs=[HBM_SPEC] * nb + [SEM_SPEC, SEM_SPEC] + [HBM_SPEC] * len(after),
        out_specs=[HBM_SPEC] * nb, input_output_aliases={i: i for i in range(nb)},
        compiler_params=pltpu.CompilerParams(has_side_effects=EFFECT),
    )(*bufs, send, recv, *[pltpu.with_memory_space_constraint(a, pltpu.HBM) for a in after])
    return list(outs)


def _copies_now(name, bufs, n_sem, plan):
    nb = len(bufs)

    def body(*refs):
        ins, token, send, recv = refs[:nb], refs[2 * nb], refs[-2], refs[-1]
        token[...] = jnp.zeros_like(token)
        todo = plan(ins)
        for i, (src, dst, dev, _) in enumerate(todo):
            pltpu.make_async_remote_copy(src, dst, send.at[i], recv.at[i], device_id=dev, device_id_type=MESH).start()
        for i, (src, dst, dev, mine) in enumerate(todo):
            pltpu.make_async_remote_copy(src, mine, send.at[i], recv.at[i], device_id=dev, device_id_type=MESH).wait_recv()
        for i, (src, dst, dev, _) in enumerate(todo):
            pltpu.make_async_remote_copy(src, dst, send.at[i], recv.at[i], device_id=dev, device_id_type=MESH).wait_send()

    outs = pl.pallas_call(
        body, name=name,
        out_shape=[pltpu.HBM(b.shape, b.dtype) for b in bufs] + [jax.ShapeDtypeStruct((8, 128), F32)],
        in_specs=[HBM_SPEC] * nb, out_specs=[HBM_SPEC] * nb + [pl.BlockSpec(memory_space=pltpu.VMEM)],
        input_output_aliases={i: i for i in range(nb)},
        scratch_shapes=[pltpu.SemaphoreType.DMA((n_sem,)), pltpu.SemaphoreType.DMA((n_sem,))],
    )(*[pltpu.with_memory_space_constraint(b, pltpu.HBM) for b in bufs])
    return list(outs[:nb]), outs[nb]


def _slot(land, s, rows, cols):
    if cols is None:
        return land.at[s, rows]
    return land.at[rows, pl.ds(pl.multiple_of(s * cols, 128), cols)]


def _plan_gather_ici(cols):
    nw = len(cols)

    def plan(refs):
        x, y, c = _pos()
        my_s = 2 * x + y
        out = []
        for w in range(nw):
            mine = _half_rows(refs[w], c)
            for px, py, ps in _chips():
                out.append((refs[w].at[mine], _slot(refs[nw + w], my_s, mine, cols[w]), (px, py, c),
                            _slot(refs[nw + w], ps, mine, cols[w])))
        return out
    return plan


def _plan_gather_fwd(cols, rows):
    def plan(refs):
        x, y, c = _pos()
        out = []
        for w in range(len(cols)):
            half = rows[w] // 2
            mine = pl.ds(pl.multiple_of(c * half, 8), half)
            other = pl.ds(pl.multiple_of((1 - c) * half, 8), half)
            for px, py, ps in _chips():
                got = _slot(refs[w], ps, mine, cols[w])
                out.append((got, got, (x, y, 1 - c), _slot(refs[w], ps, other, cols[w])))
        return out
    return plan


def _plan_swap(nw):
    def plan(refs):
        x, y, c = _pos()
        return [(refs[w].at[:, _half_rows(refs[w], 1 - c)], refs[nw + w], (x, y, 1 - c), refs[nw + w])
                for w in range(nw)]
    return plan


def _plan_swap_rows(nw):
    def plan(refs):
        x, y, c = _pos()
        return [(refs[w].at[_half_rows(refs[w], 1 - c)], refs[nw + w], (x, y, 1 - c), refs[nw + w])
                for w in range(nw)]
    return plan


def _plan_scatter(nw):
    def plan(refs):
        x, y, c = _pos()
        my_s = 2 * x + y
        out = []
        for w in range(nw):
            for px, py, ps in _chips():
                out.append((refs[w].at[ps], refs[nw + w].at[my_s], (px, py, c), refs[nw + w].at[ps]))
        return out
    return plan


def _plan_scatter_both():
    def plan(refs):
        x, y, c = _pos()
        my_s = 2 * x + y
        src, land = refs
        out = []
        for px, py, ps in _chips():
            out.append((src.at[ps], land.at[my_s, c], (px, py, c), land.at[ps, c]))
            out.append((src.at[ps], land.at[my_s, c], (px, py, 1 - c), land.at[ps, 1 - c]))
        out.append((src.at[my_s], land.at[my_s, c], (x, y, 1 - c), land.at[my_s, 1 - c]))
        return out
    return plan


def _sum4_both(parts, land, s_arr, c_arr):
    _, _, H, C = land.shape

    def body(s_ref, c_ref, own_ref, r_ref, o_ref):
        mine = pl.program_id(0) == c_ref[0]
        own = own_ref[0].astype(F32)
        tot = jnp.zeros((H, C), F32)
        for j in range(4):
            tot = tot + jnp.where(jnp.logical_and(mine, s_ref[0] == j), own, r_ref[j, 0].astype(F32))
        o_ref[0] = tot

    return pl.pallas_call(
        body, name="grad_sum_in",
        grid_spec=pltpu.PrefetchScalarGridSpec(
            num_scalar_prefetch=2, grid=(2,),
            in_specs=[pl.BlockSpec((1, H, C), lambda h, s_ref, c_ref: (s_ref[0], 0, 0)),
                      pl.BlockSpec((4, 1, H, C), lambda h, s_ref, c_ref: (0, h, 0, 0))],
            out_specs=pl.BlockSpec((1, H, C), lambda h, s_ref, c_ref: (h, 0, 0))),
        out_shape=jax.ShapeDtypeStruct((2, H, C), F32), compiler_params=_cp("parallel"),
    )(s_arr, c_arr, parts, land).reshape(2 * H, C)


def _plan_join(nw):
    def plan(refs):
        x, y, c = _pos()
        out = []
        for w in range(nw):
            land = refs[nw + w]
            out.append((refs[w], land.at[_half_rows(land, c)], (x, y, 1 - c), land.at[_half_rows(land, 1 - c)]))
        return out
    return plan


def _hbm_empty(shape, dtype):
    return pltpu.with_memory_space_constraint(lax.empty(shape, dtype), pltpu.HBM)


def _put_slot(land, own, slot):
    return lax.dynamic_update_slice(land, own[None], (slot,) + (0,) * own.ndim)


def _w_in_assemble(land, own, s_arr, rb):
    wide = -(-IN_SH // 128) * 128
    starts = [s * IN_SH // 128 * 128 for s in range(4)]
    ends = [min(IN_PAD, -(-(s + 1) * IN_SH // 128) * 128) for s in range(4)]

    def body(s_ref, land_ref, own_ref, o_ref, parts):
        @pl.when(pl.program_id(0) == 0)
        def _():
            parts[...] = jnp.zeros_like(parts)

        acc = []
        for s in range(4):
            parts[s, :, 0:IN_SH] = jnp.where(s_ref[0] == s, own_ref[...], land_ref[s])
            w = ends[s] - starts[s]
            sel = (_iota((wide, w), 0) + (s * IN_SH - starts[s]) == _iota((wide, w), 1)).astype(BF16)
            acc.append(_dot(parts[s], sel))
        for s in range(4):
            lo = starts[s] if s == 0 else ends[s - 1]
            hi = starts[s + 1] if s < 3 else ends[s]
            o_ref[:, lo:hi] = acc[s][:, lo - starts[s]:hi - starts[s]].astype(BF16)
            if s < 3:
                a, b = starts[s + 1], ends[s]
                o_ref[:, a:b] = (acc[s][:, a - starts[s]:b - starts[s]] + acc[s + 1][:, 0:b - a]).astype(BF16)

    return pl.pallas_call(
        body, name="w_in_assemble",
        grid_spec=pltpu.PrefetchScalarGridSpec(
            num_scalar_prefetch=1, grid=(D // rb,),
            in_specs=[pl.BlockSpec((4, rb, IN_SH), lambda i, s_ref: (0, i, 0)),
                      pl.BlockSpec((rb, IN_SH), lambda i, s_ref: (i, 0))],
            out_specs=pl.BlockSpec((rb, IN_PAD), lambda i, s_ref: (i, 0)),
            scratch_shapes=[pltpu.VMEM((4, rb, wide), BF16)]),
        out_shape=jax.ShapeDtypeStruct((D, IN_PAD), BF16), compiler_params=_cp("arbitrary"),
    )(s_arr, land, own)


def _pad_lanes(a, n):
    return jnp.pad(a, ((0, 0), (0, n - a.shape[1])))


def kernel(x, c, positions, w_ada, b_ada, norm1_w, w_in, conv_w, conv_b, dt_bias, a_log, d_skip, attn_sinks, ssm_norm_w, w_out, norm2_w, w_gate_up, w_down, final_norm_w, loss_target, m_w_ada, m_b_ada, m_norm1_w, m_w_in, m_conv_w, m_conv_b, m_dt_bias, m_a_log, m_d_skip, m_attn_sinks, m_ssm_norm_w, m_w_out, m_norm2_w, m_w_gate_up, m_w_down, m_final_norm_w, v_w_ada, v_b_ada, v_norm1_w, v_w_in, v_conv_w, v_conv_b, v_dt_bias, v_a_log, v_d_skip, v_attn_sinks, v_ssm_norm_w, v_w_out, v_norm2_w, v_w_gate_up, v_w_down, v_final_norm_w):
    T = x.shape[1]
    tm = min(256, T)
    xi, yi, ci = lax.axis_index("x"), lax.axis_index("y"), lax.axis_index("c")
    my_s = 2 * xi + yi
    xs = x[0]
    tgt = loss_target[0]

    payload = jnp.concatenate([c, conv_w[0].reshape(1, CONVK * 256)], axis=1)
    gat, mod4, tok = _mod_exchange(payload, w_ada[0], b_ada.reshape(4, 1, 1536))
    mod6 = mod4.reshape(6, D)
    cw_dev = gat[:, 0, D:].reshape(4, 2, CONVK, 256)[:, 0]
    conv_full = cw_dev.transpose(1, 0, 2).reshape(CONVK, CONVC)

    w_in_b = w_in[0].astype(BF16)
    s_i, r_i, bufs, tok = _split_start("wgather_in_ici_start", [w_in_b, _hbm_empty((4,) + w_in_b.shape, BF16)], 3,
                                       _plan_gather_ici([None]), tok)
    inv_freq = (10000.0 ** (-jnp.arange(32, dtype=F32) / 32))
    cos, sin_s = _rope_tables(positions, inv_freq.reshape(32, 1), min(512, T), tok)
    late = [w_out[0].astype(BF16), w_gate_up[0].astype(BF16), w_down[0].astype(BF16)]
    bufs = _split_wait("wgather_in_ici_wait", s_i, r_i, bufs, [cos] + late, _plan_gather_ici([None]))
    own_in = bufs[0]
    bufs, tok = _copies_now("wgather_in_fwd", bufs[1:], 3, _plan_gather_fwd([None], [D]))
    s_arr = my_s.reshape(1).astype(jnp.int32)
    w_pad = _w_in_assemble(bufs[0], own_in, s_arr, 256)

    lands = [_hbm_empty((4, D // 4, D), BF16), _hbm_empty((D, 2 * DFF), BF16), _hbm_empty((4, DFF // 4, D), BF16)]
    cols3, rows3 = [None, GU_SH, None], [D // 4, D, DFF // 4]
    s_a, r_a, bufs, tok = _split_start("wgather_ici_start", late + lands, 9, _plan_gather_ici(cols3), tok)

    qkv, z, xbc, dtr, h1b = _in_proj_fwd(xs, cos, sin_s, mod6, norm1_w, w_pad, min(512, T), tok)
    sinks = attn_sinks
    attn, lse = _attn_fwd(qkv, sinks)
    bufs = _split_wait("wgather_ici_wait", s_a, r_a, bufs, attn, _plan_gather_ici(cols3))
    late = bufs[:3]
    s_b, r_b, lands, tok = _split_start("wgather_fwd_start", bufs[3:], 9, _plan_gather_fwd(cols3, rows3), attn)
    dtb = _pad_lanes(dt_bias, 128)
    alog = _pad_lanes(a_log, 128)
    dskx = jnp.repeat(d_skip, HD, axis=1)
    mats = _ssd_mats()
    ynorm, ypre, states, conv_pre = _ssd_fwd(xbc, z, dtr, conv_full, conv_b, dtb, alog, dskx, ssm_norm_w, mats, tok)
    lands = _split_wait("wgather_fwd_wait", s_b, r_b, lands, ynorm, _plan_gather_fwd(cols3, rows3))
    w_out_f = _put_slot(lands[0], late[0], my_s).reshape(D, D)
    w_dn_f = _put_slot(lands[2], late[2], my_s).reshape(DFF, D)

    fw2 = final_norm_w.reshape(1, D)
    sq, dmix, dx1, h2b, act, dfb, dgu, dob, sm_ffn = _mix_ffn(
        xs, attn, ynorm, tgt, mod6, norm2_w, fw2, w_out_f, lands[1], late[1], s_arr, w_dn_f, tm)

    tt = min(2048, T)
    c_arr = ci.reshape(1).astype(jnp.int32)
    tok0 = jnp.zeros((8, 128), F32)
    gw_dn4 = _tn_matmul(act, dfb, GU_SH, D, tt, "dw_down", tok0).reshape(4, DFF // 4, D)
    gw_gu4 = _tn_matmul(h2b, dgu, D, GU_SH, tt, "dw_gate_up", tok0)
    gw_out4 = _tn_matmul_rows(attn, ynorm, dob, tt, "dw_out", tok0).reshape(4, D // 4, D)
    big1 = [gw_out4, gw_gu4, gw_dn4]
    rbs1 = [128, 512, 352]
    sib1 = [_hbm_empty((4, g.shape[1] // 2, g.shape[2]), F32) for g in big1]
    s_c, r_c, bufs, tok = _split_start("gswap_start", big1 + sib1, 3, _plan_swap(3), tok0)

    dzxd, d_cw, d_cb, d_sw, d_sk, d_dtb, d_av = _ssd_bwd(
        xbc, conv_pre, z, dtr, ypre, states, dmix, conv_full, dtb, alog, dskx, ssm_norm_w, mats, tok)
    bufs = _split_wait("gswap_wait", s_c, r_c, bufs, dzxd, _plan_swap(3))
    sums1 = [_add_half(g, s, c_arr, rb, "grad_add_%d" % i)
             for i, (g, s, rb) in enumerate(zip(bufs[:3], bufs[3:], rbs1))]
    land1 = [_hbm_empty(p.shape, BF16) for p in sums1]
    s_d, r_d, bufs, tok = _split_start("gscatter_start", sums1 + land1, 9, _plan_scatter(3), tok0)
    dqkv, d_sinks = _attn_bwd(qkv, sinks, lse, dmix, cos, sin_s, tok)
    bufs = _split_wait("gscatter_wait", s_d, r_d, bufs, dqkv, _plan_scatter(3))
    halves1 = [_sum4(p, l, s_arr, rb, "grad_sum_%d" % i)
               for i, (p, l, rb) in enumerate(zip(bufs[:3], bufs[3:], rbs1))]
    full1 = [_hbm_empty((2 * h.shape[0], h.shape[1]), F32) for h in halves1]
    s_e, r_e, bufs, tok = _split_start("gjoin_start", halves1 + full1, 3, _plan_join(3), tok0)
    gq = _tn_matmul(h1b, dqkv, D, 768, tt, "dw_in_qkv", tok)[0]
    gz = _tn_matmul(h1b, dzxd, D, IN_PAD - 768, tt, "dw_in_zxd", tok)[0]
    joined1 = _split_wait("gjoin_wait", s_e, r_e, bufs, [gq, gz], _plan_join(3))

    sibs = [_hbm_empty((D // 2, g.shape[1]), F32) for g in (gq, gz)]
    s_f, r_f, bufs, tok = _split_start("gswap_in_start", [gq, gz] + sibs, 2, _plan_swap_rows(2), tok0)
    g_dn_s, d_dn, m_dn, v_dn = _adam_2d(w_down[0], joined1[2], joined1[5], m_w_down[0], v_w_down[0], c_arr, 352,
                                        "adam_w_down", tok)
    g_gu_s, d_gu, m_gu, v_gu = _adam_2d(w_gate_up[0], joined1[1], joined1[4], m_w_gate_up[0], v_w_gate_up[0], c_arr,
                                        256, "adam_w_gate_up", tok)
    g_out_s, d_out, m_out, v_out = _adam_2d(w_out[0], joined1[0], joined1[3], m_w_out[0], v_w_out[0], c_arr, 128,
                                            "adam_w_out", tok)
    bufs = _split_wait("gswap_in_wait", s_f, r_f, bufs, [d_dn, d_gu, d_out], _plan_swap_rows(2))
    sum0 = _add_half_in(bufs[0], bufs[1], bufs[2], bufs[3], c_arr, min(256, D // 2))
    s_g, r_g, bufs, tok = _split_start("gscatter_in_start", [sum0, _hbm_empty((4, 2) + sum0.shape[1:], BF16)], 7,
                                       _plan_scatter_both(), tok0)
    grad_x, sm_in = _in_proj_bwd(xs, dx1, dqkv, dzxd, mod6, norm1_w, w_pad, min(512, T), tok)

    a_neg = -jnp.exp(alog)
    pieces = [sm_in[1:2], sm_in[2:3], sm_ffn[5:6], sm_ffn[2:3], sm_ffn[3:4], sm_ffn[4:5],
              sm_in[0:1], sm_ffn[1:2], sm_ffn[0:1], d_cb, d_cw.reshape(1, CONVK * CONVC),
              _pad_lanes(d_sw, SW), d_dtb, d_av * a_neg, d_sk, d_sinks,
              _pad_lanes((0.5 / D * jnp.sum(sq)).reshape(1, 1), 128)]
    vec = jnp.concatenate(pieces, axis=1)
    s_h, r_h, rows8, tok_small = _split_start("small_start", [vec, _hbm_empty((8,) + vec.shape, F32)], 7,
                                              _plan_small(), tok0)

    bufs = _split_wait("gscatter_in_wait", s_g, r_g, bufs, [grad_x, tok_small], _plan_scatter_both())
    gw_in_s = _sum4_both(bufs[0], bufs[1], s_arr, c_arr)
    native = lambda a: a.transpose(2, 0, 1)
    adam_in = _adam_w_in(native(w_in), gw_in_s, native(m_w_in), native(v_w_in))
    g_in_s, d_in, m_in, v_in = [a.transpose(1, 2, 0) for a in adam_in]
    rows8 = _split_wait("small_wait", s_h, r_h, rows8, [adam_in[1]], _plan_small())
    tot, allv = _small_sum(rows8[0], rows8[1], (4 * xi + 2 * yi + ci).reshape(1).astype(jnp.int32))
    o = 0
    offs = []
    for p in pieces:
        offs.append(o)
        o += p.shape[1]
    seg = lambda i, n: (offs[i], n)
    g_conv_w = lax.dynamic_slice_in_dim(
        tot[:, offs[10]:offs[10] + CONVK * CONVC].reshape(CONVK, CONVC), my_s * 256, 256, axis=1)
    loss = tot[0, offs[16]]

    small_names = ["b_ada", "norm1_w", "conv_w", "conv_b", "dt_bias", "a_log", "d_skip", "attn_sinks", "ssm_norm_w",
                   "norm2_w", "final_norm_w"]
    small_g = [(0, 6 * D), seg(6, D), g_conv_w, seg(9, D), seg(12, 8), seg(13, 8), seg(14, 8), seg(15, 8),
               seg(11, SW), seg(7, D), seg(8, D)]
    as2d = lambda a: a.reshape(-1, a.shape[-1])
    small_w = [as2d(a) for a in (b_ada, norm1_w, conv_w, conv_b, dt_bias, a_log, d_skip, attn_sinks, ssm_norm_w,
                                 norm2_w, final_norm_w)]
    small_m = [as2d(a) for a in (m_b_ada, m_norm1_w, m_conv_w, m_conv_b, m_dt_bias, m_a_log, m_d_skip, m_attn_sinks,
                                 m_ssm_norm_w, m_norm2_w, m_final_norm_w)]
    small_v = [as2d(a) for a in (v_b_ada, v_norm1_w, v_conv_w, v_conv_b, v_dt_bias, v_a_log, v_d_skip, v_attn_sinks,
                                 v_ssm_norm_w, v_norm2_w, v_final_norm_w)]
    small_g, sd, smn, svn = _adam_small(tot, small_g, small_w, small_m, small_v)
    g_ada, d_ada, m_ada, v_ada = _adam_w_ada(gat, allv, s_arr, w_ada[0], m_w_ada[0], v_w_ada[0], 256)

    order = ["w_ada", "b_ada", "norm1_w", "w_in", "conv_w", "conv_b", "dt_bias", "a_log", "d_skip", "attn_sinks",
             "ssm_norm_w", "w_out", "norm2_w", "w_gate_up", "w_down", "final_norm_w"]
    shapes = dict(w_ada=w_ada.shape, b_ada=b_ada.shape, norm1_w=norm1_w.shape, w_in=w_in.shape, conv_w=conv_w.shape,
                  conv_b=conv_b.shape, dt_bias=dt_bias.shape, a_log=a_log.shape, d_skip=d_skip.shape,
                  attn_sinks=attn_sinks.shape, ssm_norm_w=ssm_norm_w.shape, w_out=w_out.shape, norm2_w=norm2_w.shape,
                  w_gate_up=w_gate_up.shape, w_down=w_down.shape, final_norm_w=final_norm_w.shape)
    grads = dict(w_ada=g_ada, w_in=g_in_s, w_out=g_out_s, w_gate_up=g_gu_s, w_down=g_dn_s)
    deltas = dict(w_ada=d_ada, w_in=d_in, w_out=d_out, w_gate_up=d_gu, w_down=d_dn)
    new_m = dict(w_ada=m_ada, w_in=m_in, w_out=m_out, w_gate_up=m_gu, w_down=m_dn)
    new_v = dict(w_ada=v_ada, w_in=v_in, w_out=v_out, w_gate_up=v_gu, w_down=v_dn)
    for i, nme in enumerate(small_names):
        grads[nme], deltas[nme], new_m[nme], new_v[nme] = small_g[i], sd[i], smn[i], svn[i]
    outs = [loss, grad_x[None]]
    for table in (grads, deltas, new_m, new_v):
        outs += [table[nme].reshape(shapes[nme]) for nme in order]
    return tuple(outs)
```

```python
import functools
import math

import jax
import jax.numpy as jnp
from jax import lax
from jax.experimental import pallas as pl
from jax.experimental.pallas import tpu as pltpu

F32 = jnp.float32
BF16 = jnp.bfloat16
HI = lax.Precision.HIGHEST
MESH = pl.DeviceIdType.MESH

D = 1024
HD = 64
AW = 512
SW = 512
NST = 128
CONVK = 4
CONVC = 1024
BLK = 128
CPS = 4
SSD_FWD_CPS = 8
ATTN_BPS = 8
IN_PROJ = 2312
IN_PAD = 2432
IN_SH = IN_PROJ // 4
DFF = 2816
GU_SH = 1408
FF_SPLITS = ((0, 1536), (1536, 2816))
EPS = 1e-6
NEG = -1e30
LR, B1, B2, AEPS, WD, STEP = 0.001, 0.9, 0.999, 1e-08, 0.01, 10
VMEM_LIMIT = 58 * 1024 * 1024


def _cp(*sem):
    return pltpu.CompilerParams(dimension_semantics=sem or None, vmem_limit_bytes=VMEM_LIMIT)


def _dot(a, b):
    return jnp.dot(a, b, preferred_element_type=F32)


def _dot_nt(a, b):
    return lax.dot_general(a, b, (((1,), (1,)), ((), ())), preferred_element_type=F32)


def _dot_tn(a, b):
    return lax.dot_general(a, b, (((0,), (0,)), ((), ())), preferred_element_type=F32)


def _dot_hi(a, b):
    return jnp.dot(a, b, precision=HI, preferred_element_type=F32)


def _sigmoid(x):
    return 1.0 / (1.0 + jnp.exp(-x))


def _iota(shape, dim):
    return lax.broadcasted_iota(jnp.int32, shape, dim)


def _load_resident(hbm_ref, vmem_ref, sem):
    @pl.when(pl.program_id(0) == 0)
    def _():
        cp = pltpu.make_async_copy(hbm_ref, vmem_ref, sem)
        cp.start()
        cp.wait()


def _swap32(t):
    lane = _iota(t.shape, 1)
    return jnp.where((lane & 63) < 32, pltpu.roll(t, 96, 1), pltpu.roll(t, 32, 1))


def _rope_fwd(t, cos, sin_s):
    return t * cos + _swap32(t) * sin_s


def _rope_bwd(t, cos, sin_s):
    return t * cos - _swap32(t) * sin_s


DEP_SPEC = pl.BlockSpec((8, 128), lambda *_: (0, 0))


def _rope_tables(pos_row, inv_freq_col, tm, dep):
    T = pos_row.shape[1]
    lane, row = jnp.arange(128)[None, :], jnp.arange(96)[:, None]
    pick = (lane % 32) == (row % 32)
    sel_cos = pick.astype(BF16)
    sel_sin = jnp.where(pick, jnp.where(lane % 64 < 32, -1.0, 1.0), 0.0).astype(BF16)

    def body(p_ref, f_ref, sc_ref, ss_ref, dep_ref, cos_ref, sin_ref):
        ang = f_ref[...] * p_ref[...].astype(F32)
        cos_ref[...] = _dot_tn(_pieces(jnp.cos(ang), 3, 0), sc_ref[...])
        sin_ref[...] = _dot_tn(_pieces(jnp.sin(ang), 3, 0), ss_ref[...])

    full = lambda a: pl.BlockSpec(a.shape, lambda i: (0,) * a.ndim)
    return pl.pallas_call(
        body, name="rope_tables", grid=(T // tm,),
        in_specs=[pl.BlockSpec((1, tm), lambda i: (0, i)), full(inv_freq_col), full(sel_cos), full(sel_sin), DEP_SPEC],
        out_specs=[pl.BlockSpec((tm, 128), lambda i: (i, 0))] * 2,
        out_shape=[jax.ShapeDtypeStruct((T, 128), F32)] * 2,
        compiler_params=_cp("parallel"),
    )(pos_row, inv_freq_col, sel_cos, sel_sin, dep)


def _in_proj_fwd(x, cos, sin_s, mod6, norm1_w, w_pad, tm, dep):
    T = x.shape[0]

    def body(x_ref, cos_ref, sin_ref, mod_ref, nw_ref, w_hbm, dep_ref, qkv_ref, z_ref, xbc_ref, dt_ref, h_ref, w_vmem,
             sem):
        _load_resident(w_hbm, w_vmem, sem)
        xv = x_ref[...]
        r = lax.rsqrt(jnp.mean(xv * xv, axis=-1, keepdims=True) + EPS)
        h = (xv * r * nw_ref[...]) * (1.0 + mod_ref[1:2, :]) + mod_ref[0:1, :]
        hb = h.astype(BF16)
        h_ref[...] = hb
        proj = _dot(hb, w_vmem[...])
        cs, sn = cos_ref[...], sin_ref[...]
        for j in range(5):
            qkv_ref[:, 128 * j:128 * (j + 1)] = _rope_fwd(proj[:, 128 * j:128 * (j + 1)], cs, sn).astype(BF16)
        qkv_ref[:, 640:768] = proj[:, 640:768].astype(BF16)
        z_ref[...] = proj[:, 768:1280]
        xbc_ref[...] = proj[:, 1280:2304]
        dt_ref[...] = proj[:, 2304:2432]

    row = lambda w: pl.BlockSpec((tm, w), lambda i: (i, 0))
    full = lambda a: pl.BlockSpec(a.shape, lambda i: (0,) * a.ndim)
    return pl.pallas_call(
        body, name="in_proj_fwd", grid=(T // tm,),
        in_specs=[row(D), row(128), row(128), full(mod6), full(norm1_w), pl.BlockSpec(memory_space=pl.ANY), DEP_SPEC],
        out_specs=[row(768), row(512), row(1024), row(128), row(D)],
        out_shape=[jax.ShapeDtypeStruct((T, 768), BF16), jax.ShapeDtypeStruct((T, 512), F32),
                   jax.ShapeDtypeStruct((T, 1024), F32), jax.ShapeDtypeStruct((T, 128), F32),
                   jax.ShapeDtypeStruct((T, D), BF16)],
        scratch_shapes=[pltpu.VMEM((D, IN_PAD), BF16), pltpu.SemaphoreType.DMA],
        compiler_params=_cp("arbitrary"),
    )(x, cos, sin_s, mod6, norm1_w, w_pad, dep)


def _head_variants(pair, j):
    lane = _iota(pair.shape, 1)
    lo = lane < 64
    kv = j // 2
    ev = jnp.where(lo, pair, 0.0)
    od = jnp.where(lo, 0.0, pair)
    if kv == 0:
        od = pltpu.roll(od, 64, 1)
    else:
        ev = pltpu.roll(ev, 64, 1)
    return ev.astype(BF16), od.astype(BF16)


def _kv_variants(vcat):
    lane = _iota(vcat.shape, 1)
    lo = lane < 64
    v0 = jnp.where(lo, vcat, 0.0)
    v1 = jnp.where(lo, 0.0, vcat)
    out = {
        (0, 0): v0, (0, 1): pltpu.roll(v0, 64, 1),
        (1, 0): pltpu.roll(v1, 64, 1), (1, 1): v1,
    }
    return {k: v.astype(BF16) for k, v in out.items()}


def _fold_masks(n):
    upper = _iota((BLK, BLK), 1) > _iota((BLK, BLK), 0)
    return upper, upper & (n == 0)


def _attn_fwd(qkv, sinks):
    CPS = ATTN_BPS
    T = qkv.shape[0]
    nsteps = T // (CPS * BLK)

    def body(sink_ref, q_ref, kc_ref, kp_ref, vc_ref, vp_ref, o_ref, lse_ref):
        for sub in range(CPS):
            rows, before = slice(BLK * sub, BLK * (sub + 1)), slice(BLK * (sub - 1), BLK * sub)
            block(pl.program_id(0) * CPS + sub, sink_ref, q_ref.at[rows, :], kc_ref.at[rows, :],
                  kp_ref if sub == 0 else kc_ref.at[before, :], vc_ref.at[rows, :],
                  vp_ref if sub == 0 else vc_ref.at[before, :], o_ref.at[rows, :], lse_ref.at[rows, :])

    def block(n, sink_ref, q_ref, kc_ref, kp_ref, vc_ref, vp_ref, o_ref, lse_ref):
        vpv = _kv_variants(vp_ref[...].astype(F32))
        vcv = _kv_variants(vc_ref[...].astype(F32))
        q_all = jnp.concatenate(
            [v for j in range(4) for v in _head_variants(q_ref[:, 128 * j:128 * (j + 1)].astype(F32), j)], axis=0)
        s_prev = _dot_nt(q_all, kp_ref[...])
        s_cur = _dot_nt(q_all, kc_ref[...])
        upper, dead = _fold_masks(n)
        lane = _iota((BLK, 128), 1)
        lse_acc = jnp.zeros((BLK, 128), F32)
        for jj in range(4):
            acc = jnp.zeros((BLK, 128), F32)
            for par in range(2):
                h = 2 * jj + par
                rows = slice(h * BLK, (h + 1) * BLK)
                sink = sink_ref[0, h]
                s = jnp.where(dead, NEG, jnp.where(upper, s_prev[rows], s_cur[rows]) * 0.125)
                m = jnp.maximum(jnp.max(s, axis=1, keepdims=True), sink)
                p = jnp.exp(s - m)
                den = jnp.sum(p, axis=1, keepdims=True) + jnp.exp(sink - m)
                pn = p * (1.0 / den)
                acc = (acc + _dot(jnp.where(upper, pn, 0.0).astype(BF16), vpv[(jj // 2, par)])
                       + _dot(jnp.where(upper, 0.0, pn).astype(BF16), vcv[(jj // 2, par)]))
                lse_acc = jnp.where(lane == h, m + jnp.log(den), lse_acc)
            o_ref[:, 128 * jj:128 * (jj + 1)] = acc.astype(BF16)
        lse_ref[...] = lse_acc

    RB = CPS * BLK
    prev = lambda n: jnp.maximum(n * CPS - 1, 0)
    return pl.pallas_call(
        body, name="attn_fwd", grid=(nsteps,),
        in_specs=[pl.BlockSpec(memory_space=pltpu.SMEM),
                  pl.BlockSpec((RB, 512), lambda n: (n, 0)),
                  pl.BlockSpec((RB, 128), lambda n: (n, 4)),
                  pl.BlockSpec((BLK, 128), lambda n: (prev(n), 4)),
                  pl.BlockSpec((RB, 128), lambda n: (n, 5)),
                  pl.BlockSpec((BLK, 128), lambda n: (prev(n), 5))],
        out_specs=[pl.BlockSpec((RB, 512), lambda n: (n, 0)), pl.BlockSpec((RB, 128), lambda n: (n, 0))],
        out_shape=[jax.ShapeDtypeStruct((T, 512), BF16), jax.ShapeDtypeStruct((T, 128), F32)],
        compiler_params=_cp("parallel"),
    )(sinks, qkv, qkv, qkv, qkv, qkv)


def _attn_bwd(qkv, sinks, lse, dmix, cos, sin_s, dep):
    T = qkv.shape[0]
    nb = T // BLK

    def body(sink_ref, q_ref, kc_ref, kp_ref, vc_ref, vp_ref, lse_ref, do_ref, cq_ref, sq_ref, ck_ref, sk_ref,
             dep_ref, out_ref, ds_ref, dq_car, dk_car, dv_car):
        n = pl.program_id(0)
        lane = _iota((BLK, 128), 1)

        @pl.when(n == 0)
        def _():
            ds_ref[...] = jnp.zeros_like(ds_ref)
            dq_car[...] = jnp.zeros_like(dq_car)
            dk_car[...] = jnp.zeros_like(dk_car)
            dv_car[...] = jnp.zeros_like(dv_car)

        @pl.when(n < nb)
        def _():
            kp, kc, vp, vc = kp_ref[...], kc_ref[...], vp_ref[...], vc_ref[...]
            kpv = _kv_variants(kp.astype(F32))
            kcv = _kv_variants(kc.astype(F32))
            lse_v = lse_ref[...]
            q_all = jnp.concatenate(
                [v for j in range(4) for v in _head_variants(q_ref[:, 128 * j:128 * (j + 1)].astype(F32), j)], axis=0)
            do_all = jnp.concatenate(
                [v for j in range(4) for v in _head_variants(do_ref[:, 128 * j:128 * (j + 1)], j)], axis=0)
            s_prev, s_cur = _dot_nt(q_all, kp), _dot_nt(q_all, kc)
            dp_prev, dp_cur = _dot_nt(do_all, vp), _dot_nt(do_all, vc)
            upper, dead = _fold_masks(n)
            out_ref[:, 0:512] = dq_car[...]
            dsk = jnp.zeros((1, 128), F32)
            ds_u, ds_l, p_u, p_l = [], [], [], []
            for jj in range(4):
                dq_acc = jnp.zeros((BLK, 128), F32)
                for par in range(2):
                    h = 2 * jj + par
                    rows = slice(h * BLK, (h + 1) * BLK)
                    lse_h = jnp.sum(jnp.where(lane == h, lse_v, 0.0), axis=1, keepdims=True)
                    s = jnp.where(dead, NEG, jnp.where(upper, s_prev[rows], s_cur[rows]) * 0.125)
                    p = jnp.exp(s - lse_h)
                    dp = jnp.where(upper, dp_prev[rows], dp_cur[rows])
                    delta = jnp.sum(p * dp, axis=1, keepdims=True)
                    ds = p * (dp - delta) * 0.125
                    dsu, dsl = jnp.where(upper, ds, 0.0).astype(BF16), jnp.where(upper, 0.0, ds).astype(BF16)
                    dq_acc = dq_acc + _dot(dsu, kpv[(jj // 2, par)]) + _dot(dsl, kcv[(jj // 2, par)])
                    ds_u.append(dsu)
                    ds_l.append(dsl)
                    p_u.append(jnp.where(upper, p, 0.0).astype(BF16))
                    p_l.append(jnp.where(upper, 0.0, p).astype(BF16))
                    dsk = dsk + jnp.where(lane[0:1] == h, -jnp.sum(jnp.exp(sink_ref[0, h] - lse_h) * delta), 0.0)
                dq_car[:, 128 * jj:128 * (jj + 1)] = _rope_bwd(dq_acc, cq_ref[...], sq_ref[...]).astype(BF16)
            stack = lambda parts: jnp.concatenate(parts, axis=0)
            dk_prev, dk_cur = _dot_tn(stack(ds_u), q_all), _dot_tn(stack(ds_l), q_all)
            dv_prev, dv_cur = _dot_tn(stack(p_u), do_all), _dot_tn(stack(p_l), do_all)
            ds_ref[...] += dsk
            out_ref[:, 512:640] = _rope_bwd(dk_car[...] + dk_prev, ck_ref[...], sk_ref[...]).astype(BF16)
            out_ref[:, 640:768] = (dv_car[...] + dv_prev).astype(BF16)
            dk_car[...] = dk_cur
            dv_car[...] = dv_cur

        @pl.when(n == nb)
        def _():
            out_ref[:, 0:512] = dq_car[...]
            out_ref[:, 512:640] = _rope_bwd(dk_car[...], ck_ref[...], sk_ref[...]).astype(BF16)
            out_ref[:, 640:768] = dv_car[...].astype(BF16)

    cur = lambda n: jnp.minimum(n, nb - 1)
    prev = lambda n: jnp.maximum(cur(n) - 1, 0)
    outb = lambda n: jnp.maximum(n - 1, 0)
    return pl.pallas_call(
        body, name="attn_bwd", grid=(nb + 1,),
        in_specs=[pl.BlockSpec(memory_space=pltpu.SMEM),
                  pl.BlockSpec((BLK, 512), lambda n: (cur(n), 0)),
                  pl.BlockSpec((BLK, 128), lambda n: (cur(n), 4)),
                  pl.BlockSpec((BLK, 128), lambda n: (prev(n), 4)),
                  pl.BlockSpec((BLK, 128), lambda n: (cur(n), 5)),
                  pl.BlockSpec((BLK, 128), lambda n: (prev(n), 5)),
                  pl.BlockSpec((BLK, 128), lambda n: (cur(n), 0)),
                  pl.BlockSpec((BLK, 512), lambda n: (cur(n), 0)),
                  pl.BlockSpec((BLK, 128), lambda n: (cur(n), 0)),
                  pl.BlockSpec((BLK, 128), lambda n: (cur(n), 0)),
                  pl.BlockSpec((BLK, 128), lambda n: (outb(n), 0)),
                  pl.BlockSpec((BLK, 128), lambda n: (outb(n), 0)), DEP_SPEC],
        out_specs=[pl.BlockSpec((BLK, 768), lambda n: (outb(n), 0)), pl.BlockSpec((1, 128), lambda n: (0, 0))],
        out_shape=[jax.ShapeDtypeStruct((T, 768), BF16), jax.ShapeDtypeStruct((1, 128), F32)],
        scratch_shapes=[pltpu.VMEM((BLK, 512), BF16), pltpu.VMEM((BLK, 128), F32), pltpu.VMEM((BLK, 128), F32)],
        compiler_params=_cp("arbitrary"),
    )(sinks, qkv, qkv, qkv, qkv, qkv, lse, dmix, cos, sin_s, cos, sin_s, dep)


def _ssd_mats():
    e = jnp.arange(SW)[None, :] // HD == jnp.arange(128)[:, None]
    tri = jnp.arange(BLK)[None, :] <= jnp.arange(BLK)[:, None]
    return (jnp.tile(e, (3, 1)).astype(BF16), jnp.tile(e.T, (2, 1)).astype(BF16),
            jnp.tile(tri, (1, 3)).astype(BF16), jnp.tile(tri.T, (1, 3)).astype(BF16))


def _pieces(x, n, axis):
    out, r = [], x
    for i in range(n):
        p = r.astype(BF16)
        out.append(p)
        if i + 1 < n:
            r = r - p.astype(F32)
    return jnp.concatenate(out, axis=axis)


def _expand(x, e3):
    return _dot(_pieces(x, 3, 1), e3)


def _head_sums(x, et2):
    return _dot(_pieces(x, 2, 1), et2)


def _run_sum(tri3, x):
    return _dot(tri3, _pieces(x, 3, 0))


def _shift_down(u, tail, j):
    rolled = pltpu.roll(u, j, 0)
    first = jnp.where(_iota(tail.shape, 0) < j, pltpu.roll(tail, j, 0), rolled[0:8])
    return jnp.concatenate([first, rolled[8:]], axis=0)


def _shift_up(d, head, j):
    rolled = pltpu.roll(d, BLK - j, 0)
    last = jnp.where(_iota(head.shape, 0) >= 8 - j, pltpu.roll(head, 8 - j, 0), rolled[BLK - 8:])
    return jnp.concatenate([rolled[:BLK - 8], last], axis=0)


def _ssd_parts(dtr, dtb, alog, e3, tril3):
    xx = dtr + dtb
    dt = jnp.maximum(xx, 0.0) + jnp.log(1.0 + jnp.exp(-jnp.abs(xx)))
    a_neg = -jnp.exp(alog)
    tril = _iota((BLK, BLK), 1) <= _iota((BLK, BLK), 0)
    cs = _run_sum(tril3, dt * a_neg)
    csx = _expand(cs, e3)
    last = csx[BLK - 1:BLK, :]
    return dict(xx=xx, dt=dt, a_neg=a_neg, tril=tril, cs=cs, cs_t=cs.T,
                ecsx=jnp.exp(csx), dtex=jnp.exp(last - csx), cdx=jnp.exp(last), dtx=_expand(dt, e3))


def _decay(parts, h):
    seg = parts["cs"][:, h:h + 1] - parts["cs_t"][h:h + 1, :]
    return jnp.exp(jnp.where(parts["tril"], seg, NEG))


def _group_cols(a, g):
    return a[:, 256 * g:256 * (g + 1)]


def _ssd_fwd(xbc, z, dtr, conv_w, conv_b, dtb, alog, dskx, ssm_w, mats, dep):
    CPS = SSD_FWD_CPS
    T = xbc.shape[0]
    nc = T // BLK

    def body(u_ref, tail_ref, z_ref, dtr_ref, cw_ref, cb_ref, dtb_ref, al_ref, dk_ref, sw_ref, e3_ref, tril3_ref,
             dep_ref, yn_ref, yp_ref, st_ref, co_ref, s_scr):
        n = pl.program_id(0)

        @pl.when(n == 0)
        def _():
            s_scr[...] = jnp.zeros_like(s_scr)

        lane = _iota((BLK, 128), 1)
        lo = lane < 64
        for sub in range(CPS):
            rows = slice(BLK * sub, BLK * (sub + 1))
            u = u_ref[rows, :]
            tail = jnp.where(n > 0, tail_ref[...], 0.0) if sub == 0 else u_ref[BLK * sub - 8:BLK * sub, :]
            co = cb_ref[...] + cw_ref[3:4, :] * u
            for j in range(1, CONVK):
                co = co + cw_ref[3 - j:4 - j, :] * _shift_down(u, tail, j)
            co_ref[rows, :] = co
            xc = co * _sigmoid(co)
            pt = _ssd_parts(dtr_ref[rows, :], dtb_ref[...], al_ref[...], e3_ref[...], tril3_ref[...])
            xs = xc[:, :SW]
            bm = [xc[:, 512:640].astype(BF16), xc[:, 640:768].astype(BF16)]
            cm = [xc[:, 768:896].astype(BF16), xc[:, 896:1024].astype(BF16)]
            s_in = s_scr[...]
            st_ref[sub] = s_in
            xdt = xs * pt["dtx"]
            xde = (xdt * pt["dtex"]).astype(BF16)
            ys, s_new = [], []
            for g in range(2):
                cb = _dot_nt(cm[g], bm[g])
                yoff = _dot(cm[g], _group_cols(s_in, g).astype(BF16))
                s_new.append(_dot_tn(bm[g], _group_cols(xde, g)))
                for jj in range(2):
                    j = 2 * g + jj
                    chunk = xdt[:, 128 * j:128 * (j + 1)]
                    g_ev = (cb * _decay(pt, 2 * j)).astype(BF16)
                    g_od = (cb * _decay(pt, 2 * j + 1)).astype(BF16)
                    yd = (_dot(g_ev, jnp.where(lo, chunk, 0.0).astype(BF16))
                          + _dot(g_od, jnp.where(lo, 0.0, chunk).astype(BF16)))
                    ys.append(yd + yoff[:, 128 * jj:128 * (jj + 1)] * pt["ecsx"][:, 128 * j:128 * (j + 1)])
            y = jnp.concatenate(ys, axis=1) + xs * dk_ref[...]
            s_scr[...] = s_in * pt["cdx"] + jnp.concatenate(s_new, axis=1)
            yp_ref[rows, :] = y
            zv = z_ref[rows, :]
            yz = y * (zv * _sigmoid(zv))
            outs = []
            for g in range(2):
                yg = _group_cols(yz, g)
                outs.append(yg * lax.rsqrt(jnp.mean(yg * yg, axis=-1, keepdims=True) + EPS))
            yn_ref[rows, :] = (jnp.concatenate(outs, axis=1) * sw_ref[...]).astype(BF16)

    e3, _, tril3, _ = mats
    RB = CPS * BLK
    tail8 = lambda n: jnp.maximum(n * (RB // 8) - 1, 0)
    full = lambda a: pl.BlockSpec(a.shape, lambda n: (0,) * a.ndim)
    return pl.pallas_call(
        body, name="ssd_fwd", grid=(nc // CPS,),
        in_specs=[pl.BlockSpec((RB, CONVC), lambda n: (n, 0)), pl.BlockSpec((8, CONVC), lambda n: (tail8(n), 0)),
                  pl.BlockSpec((RB, SW), lambda n: (n, 0)), pl.BlockSpec((RB, 128), lambda n: (n, 0)),
                  full(conv_w), full(conv_b), full(dtb), full(alog), full(dskx), full(ssm_w), full(e3), full(tril3),
                  DEP_SPEC],
        out_specs=[pl.BlockSpec((RB, SW), lambda n: (n, 0)), pl.BlockSpec((RB, SW), lambda n: (n, 0)),
                   pl.BlockSpec((CPS, NST, SW), lambda n: (n, 0, 0)), pl.BlockSpec((RB, CONVC), lambda n: (n, 0))],
        out_shape=[jax.ShapeDtypeStruct((T, SW), BF16), jax.ShapeDtypeStruct((T, SW), F32),
                   jax.ShapeDtypeStruct((nc, NST, SW), F32), jax.ShapeDtypeStruct((T, CONVC), F32)],
        scratch_shapes=[pltpu.VMEM((NST, SW), F32)],
        compiler_params=_cp("arbitrary"),
    )(xbc, xbc, z, dtr, conv_w, conv_b, dtb, alog, dskx, ssm_w, e3, tril3, dep)


def _ssd_bwd(xbc, co_all, z, dtr, ypre, states, dmix, conv_w, dtb, alog, dskx, ssm_w, mats, dep):
    T = xbc.shape[0]
    nsteps = T // (CPS * BLK)

    def body(*refs):
        per_chunk, consts, out_ref, carried = refs[:7], refs[7:16], refs[17], refs[18:]
        i = pl.program_id(0)

        @pl.when(i == 0)
        def _():
            for r in carried:
                r[...] = jnp.zeros_like(r)

        for sub in reversed(range(CPS)):
            rows = slice(BLK * sub, BLK * (sub + 1))
            views = [r.at[sub:sub + 1] if k == 5 else r.at[rows, :] for k, r in enumerate(per_chunk)]
            chunk(*views, *consts, out_ref.at[rows, :], *carried)

        @pl.when(i == nsteps - 1)
        def _():
            dsk_ref, dskx_scr = carried[3], carried[8]
            dsk_ref[...] = _head_sums(jnp.broadcast_to(dskx_scr[...], (8, SW)), consts[6][...])[0:1]

    def chunk(u_ref, co_ref, z_ref, dtr_ref, yp_ref, st_ref, dyn_ref, cw_ref, dtb_ref, al_ref, dk_ref, sw_ref,
              e3_ref, et2_ref, tril3_ref, triu3_ref,
              out_ref, dcw_ref, dcb_ref, dsw_ref, dsk_ref, ddtb_ref, dav_ref, ds_scr, dco_scr, dskx_scr):
        co = co_ref[...]
        sg = _sigmoid(co)
        xc = co * sg
        pt = _ssd_parts(dtr_ref[...], dtb_ref[...], al_ref[...], e3_ref[...], tril3_ref[...])
        dtx, ecsx, dtex, cdx = pt["dtx"], pt["ecsx"], pt["dtex"], pt["cdx"]
        xs = xc[:, :SW]
        bm = [xc[:, 512:640].astype(BF16), xc[:, 640:768].astype(BF16)]
        cm = [xc[:, 768:896].astype(BF16), xc[:, 896:1024].astype(BF16)]
        s_in = st_ref[0]
        ds_out = ds_scr[...]
        e_t = et2_ref[...]

        zv = z_ref[...]
        sz = _sigmoid(zv)
        silu_z = zv * sz
        ypre = yp_ref[...]
        yz = ypre * silu_z
        dyn = dyn_ref[...]
        sw = sw_ref[...]
        dyz, yns = [], []
        for g in range(2):
            yg = _group_cols(yz, g)
            r = lax.rsqrt(jnp.mean(yg * yg, axis=-1, keepdims=True) + EPS)
            yn = yg * r
            dg = _group_cols(dyn, g) * _group_cols(sw, g)
            dyz.append(r * (dg - yn * jnp.mean(dg * yn, axis=-1, keepdims=True)))
            yns.append(yn)
        dyz = jnp.concatenate(dyz, axis=1)
        dsw_ref[...] += jnp.sum(dyn * jnp.concatenate(yns, axis=1), axis=0, keepdims=True)
        dy = dyz * silu_z
        dz = dyz * ypre * (sz * (1.0 + zv * (1.0 - sz)))

        xdt = xs * dtx
        xdt_b = xdt.astype(BF16)
        edy = (ecsx * dy).astype(BF16)
        xde = (xdt * dtex).astype(BF16)
        lane = _iota((BLK, 128), 1)
        lo = lane < 64
        row8 = _iota((8, 128), 0)
        dcs = jnp.zeros((BLK, 128), F32)
        col_rows = jnp.zeros((8, 128), F32)
        dxdt, bds, yoff, dbs, dcs_g, ds_new = [], [], [], [], [], []
        for g in range(2):
            s_g = _group_cols(s_in, g).astype(BF16)
            dso_g = _group_cols(ds_out, g).astype(BF16)
            cb = _dot_nt(cm[g], bm[g])
            bds.append(_dot(bm[g], dso_g))
            yoff.append(_dot(cm[g], s_g))
            dcb_g = jnp.zeros((BLK, BLK), F32)
            for jj in range(2):
                j = 2 * g + jj
                dy_c = dy[:, 128 * j:128 * (j + 1)]
                xdt_c = xdt_b[:, 128 * j:128 * (j + 1)]
                acc = jnp.zeros((BLK, 128), F32)
                for par in range(2):
                    h = 2 * j + par
                    lm = _decay(pt, h)
                    gm = cb * lm
                    dy_m = (jnp.where(lo, dy_c, 0.0) if par == 0 else jnp.where(lo, 0.0, dy_c)).astype(BF16)
                    dg_h = _dot_nt(dy_m, xdt_c)
                    w_h = dg_h * gm
                    dcs = dcs + jnp.where(lane == h, jnp.sum(w_h, axis=1, keepdims=True), 0.0)
                    col_rows = col_rows + jnp.where(row8 == h, jnp.sum(w_h, axis=0, keepdims=True), 0.0)
                    dcb_g = dcb_g + dg_h * lm
                    acc = acc + _dot_tn(gm.astype(BF16), dy_m)
                dxdt.append(acc)
            dcb_b = dcb_g.astype(BF16)
            dcs_g.append(_dot(dcb_b, bm[g]) + _dot_nt(_group_cols(edy, g), s_g))
            dbs.append(_dot_tn(dcb_b, cm[g]) + _dot_nt(_group_cols(xde, g), dso_g))
            ds_new.append(_dot_tn(cm[g], _group_cols(edy, g)))
        bds = jnp.concatenate(bds, axis=1)
        yoff = jnp.concatenate(yoff, axis=1) * ecsx
        dxdt = jnp.concatenate(dxdt, axis=1) + dtex * bds
        ds_scr[...] = cdx * ds_out + jnp.concatenate(ds_new, axis=1)

        t_m = _head_sums(dtex * xdt * bds, e_t)
        colsum_t = jnp.concatenate([col_rows, jnp.zeros((BLK - 8, 128), F32)], axis=0).T
        cd = jnp.exp(pt["cs"][BLK - 1:BLK, :])
        sds = jnp.sum(s_in * ds_out, axis=0, keepdims=True)
        last_row = jnp.sum(t_m, axis=0, keepdims=True) + cd * _head_sums(jnp.broadcast_to(sds, (8, SW)), e_t)[0:1]
        dcs = dcs - colsum_t + _head_sums(dy * yoff, e_t) - t_m
        dcs = dcs + jnp.where(_iota((BLK, 128), 0) == BLK - 1, last_row, 0.0)
        da = _run_sum(triu3_ref[...], dcs)
        dt = pt["dt"]
        ddt = da * pt["a_neg"] + _head_sums(dxdt * xs, e_t)
        dav_ref[...] += jnp.sum(da * dt, axis=0, keepdims=True)
        ddtr = ddt * _sigmoid(pt["xx"])
        ddtb_ref[...] += jnp.sum(ddtr, axis=0, keepdims=True)
        dxs = dxdt * dtx + dy * dk_ref[...]
        dskx_scr[...] += jnp.sum(dy * xs, axis=0, keepdims=True)
        dxc = jnp.concatenate([dxs, dbs[0], dbs[1], dcs_g[0], dcs_g[1]], axis=1)
        dco = dxc * (sg * (1.0 + co * (1.0 - sg)))

        dcb_ref[...] += jnp.sum(dco, axis=0, keepdims=True)
        u = u_ref[...]
        head = dco_scr[...]
        du = jnp.zeros_like(dco)
        for j in range(CONVK):
            up_j = dco if j == 0 else _shift_up(dco, head, j)
            dcw_ref[3 - j:4 - j, :] += jnp.sum(up_j * u, axis=0, keepdims=True)
            du = du + cw_ref[3 - j:4 - j, :] * up_j
        dco_scr[...] = dco[0:8]
        out_ref[:, 0:512] = dz.astype(BF16)
        out_ref[:, 512:1536] = du.astype(BF16)
        out_ref[:, 1536:1664] = ddtr.astype(BF16)

    e3, et2, tril3, triu3 = mats
    RB = CPS * BLK
    rev = lambda i: nsteps - 1 - i
    full = lambda a: pl.BlockSpec(a.shape, lambda i: (0,) * a.ndim)
    acc = lambda r, c: pl.BlockSpec((r, c), lambda i: (0, 0))
    return pl.pallas_call(
        body, name="ssd_bwd", grid=(nsteps,),
        in_specs=[pl.BlockSpec((RB, CONVC), lambda i: (rev(i), 0)), pl.BlockSpec((RB, CONVC), lambda i: (rev(i), 0)),
                  pl.BlockSpec((RB, SW), lambda i: (rev(i), 0)), pl.BlockSpec((RB, 128), lambda i: (rev(i), 0)),
                  pl.BlockSpec((RB, SW), lambda i: (rev(i), 0)), pl.BlockSpec((CPS, NST, SW), lambda i: (rev(i), 0, 0)),
                  pl.BlockSpec((RB, SW), lambda i: (rev(i), 1)),
                  full(conv_w), full(dtb), full(alog), full(dskx), full(ssm_w),
                  full(e3), full(et2), full(tril3), full(triu3), DEP_SPEC],
        out_specs=[pl.BlockSpec((RB, 1664), lambda i: (rev(i), 0)),
                   acc(CONVK, CONVC), acc(1, CONVC), acc(1, SW), acc(1, 128), acc(1, 128), acc(1, 128)],
        out_shape=[jax.ShapeDtypeStruct((T, 1664), BF16),
                   jax.ShapeDtypeStruct((CONVK, CONVC), F32), jax.ShapeDtypeStruct((1, CONVC), F32),
                   jax.ShapeDtypeStruct((1, SW), F32), jax.ShapeDtypeStruct((1, 128), F32),
                   jax.ShapeDtypeStruct((1, 128), F32), jax.ShapeDtypeStruct((1, 128), F32)],
        scratch_shapes=[pltpu.VMEM((NST, SW), F32), pltpu.VMEM((8, CONVC), F32), pltpu.VMEM((1, SW), F32)],
        compiler_params=_cp("arbitrary"),
    )(xbc, co_all, z, dtr, ypre, states, dmix, conv_w, dtb, alog, dskx, ssm_w, e3, et2, tril3, triu3, dep)


def _mix_ffn(x, attn, ynorm, tgt, mod6, norm2_w, final_w, w_out, w_gu, w_gu_own, s_arr, w_dn, tm):
    T = x.shape[0]
    nt = T // tm

    def body(x_ref, a_ref, y_ref, t_ref, mod_ref, n2_ref, fw_ref, wo_hbm, wgu_hbm, own_hbm, s_ref, wdn_hbm,
             sq_ref, dmix_ref, dx1_ref, h2_ref, act_ref, df_ref, dgu_ref, do_ref, sm_ref,
             wo, wgu, wdn, sems):
        i = pl.program_id(0)

        @pl.when(i == 0)
        def _():
            cps = [pltpu.make_async_copy(s, d, sems.at[k]) for k, (s, d) in
                   enumerate(((wo_hbm, wo), (wgu_hbm, wgu), (wdn_hbm, wdn)))]
            for c in cps:
                c.start()
            for c in cps:
                c.wait()
            own = pltpu.make_async_copy(
                own_hbm, wgu.at[:, pl.ds(pl.multiple_of(s_ref[0] * GU_SH, 128), GU_SH)], sems.at[3])
            own.start()
            own.wait()
            sq_ref[...] = jnp.zeros_like(sq_ref)
            sm_ref[...] = jnp.zeros_like(sm_ref)

        gate1, shift2, scale2, gate2 = mod_ref[2:3, :], mod_ref[3:4, :], mod_ref[4:5, :], mod_ref[5:6, :]
        n2w, fw = n2_ref[...], fw_ref[...]
        o = _dot(a_ref[...], wo[0:AW, :]) + _dot(y_ref[...], wo[AW:D, :])
        x1 = x_ref[...] + gate1 * o
        r2 = lax.rsqrt(jnp.mean(x1 * x1, axis=-1, keepdims=True) + EPS)
        xh2 = x1 * r2
        n2 = xh2 * n2w
        h2b = (n2 * (1.0 + scale2) + shift2).astype(BF16)
        h2_ref[...] = h2b
        f = jnp.zeros((tm, D), F32)
        saved = []
        for a, b in FF_SPLITS:
            gp = _dot(h2b, wgu[:, a:b])
            upj = _dot(h2b, wgu[:, DFF + a:DFF + b])
            sg = _sigmoid(gp)
            sl = gp * sg
            actb = (sl * upj).astype(BF16)
            act_ref[:, a:b] = actb
            f = f + _dot(actb, wdn[a:b, :])
            saved.append((gp, upj, sg, sl))
        x2 = x1 + gate2 * f
        r3 = lax.rsqrt(jnp.mean(x2 * x2, axis=-1, keepdims=True) + EPS)
        xh3 = x2 * r3
        err = xh3 * fw - t_ref[...]
        sq_ref[...] += jnp.sum(err * err, axis=0, keepdims=True)
        dy = err * (1.0 / D)
        dfw = jnp.sum(dy * xh3, axis=0, keepdims=True)
        dxh3 = dy * fw
        dx2 = r3 * (dxh3 - xh3 * jnp.mean(dxh3 * xh3, axis=-1, keepdims=True))
        dgate2 = jnp.sum(dx2 * f, axis=0, keepdims=True)
        dfb = (dx2 * gate2).astype(BF16)
        df_ref[...] = dfb
        dh2 = jnp.zeros((tm, D), F32)
        for (a, b), (gp, upj, sg, sl) in zip(FF_SPLITS, saved):
            dact = _dot_nt(dfb, wdn[a:b, :])
            dg = (dact * upj * (sg * (1.0 + gp * (1.0 - sg)))).astype(BF16)
            du = (dact * sl).astype(BF16)
            dgu_ref[:, a:b] = dg
            dgu_ref[:, DFF + a:DFF + b] = du
            dh2 = dh2 + _dot_nt(dg, wgu[:, a:b]) + _dot_nt(du, wgu[:, DFF + a:DFF + b])
        dshift2 = jnp.sum(dh2, axis=0, keepdims=True)
        dscale2 = jnp.sum(dh2 * n2, axis=0, keepdims=True)
        dn2 = dh2 * (1.0 + scale2)
        dn2w = jnp.sum(dn2 * xh2, axis=0, keepdims=True)
        dxh2 = dn2 * n2w
        dx1 = dx2 + r2 * (dxh2 - xh2 * jnp.mean(dxh2 * xh2, axis=-1, keepdims=True))
        dx1_ref[...] = dx1
        dgate1 = jnp.sum(dx1 * o, axis=0, keepdims=True)
        dob = (dx1 * gate1).astype(BF16)
        do_ref[...] = dob
        dmix_ref[...] = _dot_nt(dob, wo[...])
        sm_ref[...] += jnp.concatenate(
            [dfw, dn2w, dshift2, dscale2, dgate2, dgate1, jnp.zeros((2, D), F32)], axis=0)

    row = lambda w: pl.BlockSpec((tm, w), lambda i: (i, 0))
    full = lambda a: pl.BlockSpec(a.shape, lambda i: (0,) * a.ndim)
    anyspec = pl.BlockSpec(memory_space=pl.ANY)
    return pl.pallas_call(
        body, name="mix_ffn", grid=(nt,),
        in_specs=[row(D), row(AW), row(SW), row(D), full(mod6), full(norm2_w), full(final_w), anyspec, anyspec, anyspec,
                  pl.BlockSpec(memory_space=pltpu.SMEM), anyspec],
        out_specs=[pl.BlockSpec((1, D), lambda i: (0, 0)), row(D), row(D), row(D),
                   row(DFF), row(D), row(2 * DFF), row(D), pl.BlockSpec((8, D), lambda i: (0, 0))],
        out_shape=[jax.ShapeDtypeStruct((1, D), F32), jax.ShapeDtypeStruct((T, D), F32), jax.ShapeDtypeStruct((T, D), F32),
                   jax.ShapeDtypeStruct((T, D), BF16), jax.ShapeDtypeStruct((T, DFF), BF16),
                   jax.ShapeDtypeStruct((T, D), BF16), jax.ShapeDtypeStruct((T, 2 * DFF), BF16),
                   jax.ShapeDtypeStruct((T, D), BF16), jax.ShapeDtypeStruct((8, D), F32)],
        scratch_shapes=[pltpu.VMEM((D, D), BF16), pltpu.VMEM((D, 2 * DFF), BF16), pltpu.VMEM((DFF, D), BF16),
                        pltpu.SemaphoreType.DMA((4,))],
        compiler_params=_cp("arbitrary"),
    )(x, attn, ynorm, tgt, mod6, norm2_w, final_w, w_out, w_gu, w_gu_own, s_arr, w_dn)


def _in_proj_bwd(x, dx1, dqkv, dzxd, mod6, norm1_w, w_pad, tm, dep):
    T = x.shape[0]

    def body(x_ref, dx1_ref, dq_ref, dz_ref, mod_ref, nw_ref, w_hbm, dep_ref, gx_ref, sm_ref, w_vmem, sem):
        _load_resident(w_hbm, w_vmem, sem)

        @pl.when(pl.program_id(0) == 0)
        def _():
            sm_ref[...] = jnp.zeros_like(sm_ref)

        nw = nw_ref[...]
        scale1 = mod_ref[1:2, :]
        sums = jnp.zeros((8, D), F32)
        for rows in (slice(0, tm // 2), slice(tm // 2, tm)):
            dh = _dot_nt(dq_ref[rows, :], w_vmem[:, 0:768]) + _dot_nt(dz_ref[rows, :], w_vmem[:, 768:IN_PAD])
            xv = x_ref[rows, :]
            r = lax.rsqrt(jnp.mean(xv * xv, axis=-1, keepdims=True) + EPS)
            xh = xv * r
            n1 = xh * nw
            dshift = jnp.sum(dh, axis=0, keepdims=True)
            dscale = jnp.sum(dh * n1, axis=0, keepdims=True)
            dn = dh * (1.0 + scale1)
            dnw = jnp.sum(dn * xh, axis=0, keepdims=True)
            dxh = dn * nw
            gx_ref[rows, :] = dx1_ref[rows, :] + r * (dxh - xh * jnp.mean(dxh * xh, axis=-1, keepdims=True))
            sums = sums + jnp.concatenate([dnw, dshift, dscale, jnp.zeros((5, D), F32)], axis=0)
        sm_ref[...] += sums

    row = lambda w: pl.BlockSpec((tm, w), lambda i: (i, 0))
    full = lambda a: pl.BlockSpec(a.shape, lambda i: (0,) * a.ndim)
    return pl.pallas_call(
        body, name="in_proj_bwd", grid=(T // tm,),
        in_specs=[row(D), row(D), row(768), row(1664), full(mod6), full(norm1_w), pl.BlockSpec(memory_space=pl.ANY),
                  DEP_SPEC],
        out_specs=[row(D), pl.BlockSpec((8, D), lambda i: (0, 0))],
        out_shape=[jax.ShapeDtypeStruct((T, D), F32), jax.ShapeDtypeStruct((8, D), F32)],
        scratch_shapes=[pltpu.VMEM((D, IN_PAD), BF16), pltpu.SemaphoreType.DMA],
        compiler_params=_cp("arbitrary"),
    )(x, dx1, dqkv, dzxd, mod6, norm1_w, w_pad, dep)


def _tn_matmul(a, b, K, N, tt, name, dep):
    T = a.shape[0]
    ja, jb = a.shape[1] // K, b.shape[1] // N
    J = max(ja, jb)

    def body(a_ref, b_ref, dep_ref, o_ref):
        t = pl.program_id(1)
        prod = _dot_tn(a_ref[...], b_ref[...])

        @pl.when(t == 0)
        def _():
            o_ref[0] = prod

        @pl.when(t > 0)
        def _():
            o_ref[0] += prod

    return pl.pallas_call(
        body, name=name, grid=(J, T // tt),
        in_specs=[pl.BlockSpec((tt, K), lambda j, t: (t, j if ja > 1 else 0)),
                  pl.BlockSpec((tt, N), lambda j, t: (t, j if jb > 1 else 0)),
                  pl.BlockSpec((8, 128), lambda j, t: (0, 0))],
        out_specs=pl.BlockSpec((1, K, N), lambda j, t: (j, 0, 0)),
        out_shape=jax.ShapeDtypeStruct((J, K, N), F32),
        compiler_params=_cp("parallel", "arbitrary"),
    )(a, b, dep)


def _accumulate(o_ref, rows, prod):
    @pl.when(pl.program_id(0) == 0)
    def _():
        o_ref[rows, :] = prod

    @pl.when(pl.program_id(0) > 0)
    def _():
        o_ref[rows, :] += prod


def _tn_matmul_rows(a0, a1, b, tt, name, dep):
    T, K = a0.shape
    N = b.shape[1]

    def body(a0_ref, a1_ref, b_ref, dep_ref, o_ref):
        for k, a_ref in enumerate((a0_ref, a1_ref)):
            _accumulate(o_ref, slice(k * K, (k + 1) * K), _dot_tn(a_ref[...], b_ref[...]))

    tile = lambda w: pl.BlockSpec((tt, w), lambda t: (t, 0))
    return pl.pallas_call(
        body, name=name, grid=(T // tt,), in_specs=[tile(K), tile(K), tile(N), DEP_SPEC],
        out_specs=pl.BlockSpec((2 * K, N), lambda t: (0, 0)), out_shape=jax.ShapeDtypeStruct((2 * K, N), F32),
        compiler_params=_cp("arbitrary"),
    )(a0, a1, b, dep)


def _adam_math(w, g, m, v):
    m = B1 * m + (1.0 - B1) * g
    v = B2 * v + (1.0 - B2) * (g * g)
    m_hat = m / (1.0 - B1 ** STEP)
    v_hat = v / (1.0 - B2 ** STEP)
    delta = -LR * (m_hat / (jnp.sqrt(v_hat) + AEPS) + WD * w)
    return delta, m, v


def _adam_2d(w, mine, land, m, v, c_arr, rb, name, dep):
    R, C = w.shape
    nbh = R // 2 // rb

    def body(c_ref, w_ref, mine_ref, land_ref, m_ref, v_ref, dep_ref, go_ref, d_ref, mo_ref, vo_ref):
        g = jnp.where(pl.program_id(0) // nbh == c_ref[0], mine_ref[...], land_ref[...])
        d, mn, vn = _adam_math(w_ref[...], g, m_ref[...], v_ref[...])
        go_ref[...] = g
        d_ref[...] = d
        mo_ref[...] = mn
        vo_ref[...] = vn

    spec = pl.BlockSpec((rb, C), lambda i, c_ref: (i, 0))
    mine_spec = pl.BlockSpec((rb, C), lambda i, c_ref: (jnp.clip(i - c_ref[0] * nbh, 0, nbh - 1), 0))
    return pl.pallas_call(
        body, name=name,
        grid_spec=pltpu.PrefetchScalarGridSpec(
            num_scalar_prefetch=1, grid=(R // rb,), in_specs=[spec, mine_spec, spec, spec, spec, DEP_SPEC],
            out_specs=[spec] * 4),
        out_shape=[jax.ShapeDtypeStruct((R, C), F32)] * 4, compiler_params=_cp("parallel"),
    )(c_arr, w, mine, land, m, v, dep)


def _adam_w_in(w3, grad, m3, v3):
    n = w3.shape[0]

    def body(w_hbm, grad_ref, m_hbm, v_hbm, g_hbm, d_hbm, mo_hbm, vo_hbm, bufs, sems):
        ins = [pltpu.make_async_copy(src.at[:, 0], bufs.at[k], sems.at[k]) for k, src in enumerate((w_hbm, m_hbm, v_hbm))]
        for cp in ins:
            cp.start()
        g = grad_ref[...]
        eye =(_iota((D, D), 0) == _iota((D, D), 1)).astype(BF16)
        g_t = jnp.zeros((n, D), F32)
        r = g
        for i in range(3):
            p = r.astype(BF16)
            g_t = g_t + _dot_tn(p, eye)
            if i < 2:
                r = r - p.astype(F32)
        for cp in ins:
            cp.wait()
        d, mn, vn = _adam_math(bufs[0], g_t, bufs[1], bufs[2])
        for k, val in enumerate((g_t, d, mn, vn)):
            bufs[3 + k] = val
        outs = [pltpu.make_async_copy(bufs.at[3 + k], dst.at[:, 0], sems.at[3 + k])
                for k, dst in enumerate((g_hbm, d_hbm, mo_hbm, vo_hbm))]
        for cp in outs:
            cp.start()
        for cp in outs:
            cp.wait()

    anyspec = pl.BlockSpec(memory_space=pl.ANY)
    vm = pl.BlockSpec(memory_space=pltpu.VMEM)
    return pl.pallas_call(
        body, name="adam_w_in",
        in_specs=[anyspec, vm, anyspec, anyspec], out_specs=[anyspec] * 4,
        out_shape=[jax.ShapeDtypeStruct(w3.shape, F32)] * 4,
        scratch_shapes=[pltpu.VMEM((7, n, D), F32), pltpu.SemaphoreType.DMA((7,))],
        compiler_params=pltpu.CompilerParams(vmem_limit_bytes=VMEM_LIMIT),
    )(w3, grad, m3, v3)


def _adam_w_ada(gat, allv, s_arr, w, m, v, rb):
    R, C = w.shape

    def body(s_ref, c_ref, dm_ref, w_ref, m_ref, v_ref, g_ref, d_ref, mo_ref, vo_ref):
        cm = _rows_select(c_ref, rb)
        g = lax.dot_general(cm * _sigmoid(cm), _rows_select(dm_ref, C), (((0,), (0,)), ((), ())), precision=HI,
                            preferred_element_type=F32)
        d, mn, vn = _adam_math(w_ref[...], g, m_ref[...], v_ref[...])
        g_ref[...] = g
        d_ref[...] = d
        mo_ref[...] = mn
        vo_ref[...] = vn

    spec = pl.BlockSpec((rb, C), lambda i, s_ref: (i, 0))
    return pl.pallas_call(
        body, name="adam_w_ada",
        grid_spec=pltpu.PrefetchScalarGridSpec(
            num_scalar_prefetch=1, grid=(R // rb,),
            in_specs=[pl.BlockSpec((8, 1, rb), lambda i, s_ref: (0, 0, i)),
                      pl.BlockSpec((8, 1, C), lambda i, s_ref: (0, 0, s_ref[0])), spec, spec, spec],
            out_specs=[spec] * 4),
        out_shape=[jax.ShapeDtypeStruct((R, C), F32)] * 4, compiler_params=_cp("parallel"),
    )(s_arr, gat, allv, w, m, v)


def _adam_small(tot, segs, ws, ms, vs):
    k = len(ws)
    extra = [sg for sg in segs if not isinstance(sg, tuple)]
    ne = len(extra)

    def body(*refs):
        tot_ref, g_x = refs[0], list(refs[1:1 + ne])
        w, m, v = [refs[1 + ne + j * k:1 + ne + (j + 1) * k] for j in range(3)]
        g_o, d_o, m_o, v_o = [refs[1 + ne + (3 + j) * k:1 + ne + (4 + j) * k] for j in range(4)]
        for i in range(k):
            gi = tot_ref[:, segs[i][0]:segs[i][0] + segs[i][1]] if isinstance(segs[i], tuple) else g_x.pop(0)[...]
            d, mn, vn = _adam_math(w[i][...], gi, m[i][...], v[i][...])
            g_o[i][...] = gi
            d_o[i][...] = d
            m_o[i][...] = mn
            v_o[i][...] = vn

    shapes = [jax.ShapeDtypeStruct(w.shape, F32) for w in ws]
    vm = pl.BlockSpec(memory_space=pltpu.VMEM)
    outs = pl.pallas_call(
        body, name="adam_small", in_specs=[vm] * (1 + ne + 3 * k), out_specs=[vm] * (4 * k), out_shape=shapes * 4,
    )(tot, *extra, *ws, *ms, *vs)
    return outs[0:k], outs[k:2 * k], outs[2 * k:3 * k], outs[3 * k:4 * k]


def _pos():
    return lax.axis_index("x"), lax.axis_index("y"), lax.axis_index("c")


def _flip(v, bit):
    return 1 - v if bit else v


def _peer(k):
    x, y, c = _pos()
    return (_flip(x, (k >> 2) & 1), _flip(y, (k >> 1) & 1), _flip(c, k & 1))


def _logical(p):
    return 4 * p[0] + 2 * p[1] + p[2]


def _gather8(src_ref, dst_ref, send_sems, recv_sems, meanwhile):
    me = _logical(_pos())
    dst_ref[pl.ds(me, 1)] = src_ref[...][None]
    copies = []
    for k in range(1, 8):
        cp = pltpu.make_async_remote_copy(src_ref, dst_ref.at[me], send_sems.at[k - 1], recv_sems.at[k - 1],
                                          device_id=_peer(k), device_id_type=MESH)
        cp.start()
        copies.append(cp)
    meanwhile()
    for k in range(1, 8):
        pltpu.make_async_remote_copy(src_ref, dst_ref.at[_logical(_peer(k))], send_sems.at[k - 1], recv_sems.at[k - 1],
                                     device_id=_peer(k), device_id_type=MESH).wait_recv()
    for cp in copies:
        cp.wait_send()


def _rows_select(ref3, width):
    row = _iota((8, width), 0)
    out = jnp.zeros((8, width), F32)
    for i in range(8):
        out = jnp.where(row == i, ref3[i][:, 0:width], out)
    return out


def _mod_exchange(payload, w_ada_s, b_ada4, w_in3):
    n_sh = w_ada_s.shape[1]
    n_in = w_in3.shape[0]
    wide = -(-n_in // 128) * 128

    def body(pay_ref, w_hbm, b_ref, win_hbm, gat_ref, mod_ref, token, winb_ref, p3, w_v, win_v, win_z, sa, ra, sb, rb, ls):
        token[...] = jnp.zeros_like(token)
        x, y, c = _pos()
        me = _logical((x, y, c))
        my_s = 2 * x + y
        load_w = pltpu.make_async_copy(w_hbm, w_v, ls.at[0])
        load_in = pltpu.make_async_copy(win_hbm.at[:, 0], win_v, ls.at[1])
        load_w.start()
        load_in.start()

        def local_work():
            win_z[...] = jnp.zeros_like(win_z)
            load_in.wait()
            win_z[0:n_in, :] = win_v[...].astype(BF16)
            eye = (_iota((wide, wide), 0) == _iota((wide, wide), 1)).astype(BF16)
            winb_ref[...] = _dot_tn(win_z[...], eye)[:, 0:n_in].astype(BF16)
            load_w.wait()

        _gather8(pay_ref, gat_ref, sa, ra, local_work)
        cmat = _rows_select(gat_ref, D)
        prod = _dot_hi(cmat * _sigmoid(cmat), w_v[...])
        for b in range(8):
            p3[b] = prod[b:b + 1, :]
        mod_ref[pl.ds(my_s, 1)] = p3[pl.ds(me, 1)] + b_ref[pl.ds(my_s, 1)]
        ks = (2, 4, 6)
        copies = []
        for i, k in enumerate(ks):
            pr = _peer(k)
            cp = pltpu.make_async_remote_copy(p3.at[_logical(pr)], mod_ref.at[my_s], sb.at[i], rb.at[i],
                                              device_id=pr, device_id_type=MESH)
            cp.start()
            copies.append(cp)
        for i, k in enumerate(ks):
            pr = _peer(k)
            s_src = 2 * pr[0] + pr[1]
            pltpu.make_async_remote_copy(p3.at[0], mod_ref.at[s_src], sb.at[i], rb.at[i],
                                         device_id=pr, device_id_type=MESH).wait_recv()
            mod_ref[pl.ds(s_src, 1)] = mod_ref[pl.ds(s_src, 1)] + b_ref[pl.ds(s_src, 1)]
        for cp in copies:
            cp.wait_send()

    vm = pl.BlockSpec(memory_space=pltpu.VMEM)
    anyspec = pl.BlockSpec(memory_space=pl.ANY)
    return pl.pallas_call(
        body, name="mod_exchange", in_specs=[vm, anyspec, vm, anyspec], out_specs=[vm, vm, vm, vm],
        out_shape=[jax.ShapeDtypeStruct((8, 1, payload.shape[1]), F32), jax.ShapeDtypeStruct((4, 1, n_sh), F32),
                   jax.ShapeDtypeStruct((8, 128), F32), jax.ShapeDtypeStruct((D, n_in), BF16)],
        scratch_shapes=[pltpu.VMEM((8, 1, n_sh), F32), pltpu.VMEM(w_ada_s.shape, F32), pltpu.VMEM((n_in, D), F32),
                        pltpu.VMEM((wide, D), BF16), pltpu.SemaphoreType.DMA((7,)), pltpu.SemaphoreType.DMA((7,)),
                        pltpu.SemaphoreType.DMA((3,)), pltpu.SemaphoreType.DMA((3,)), pltpu.SemaphoreType.DMA((2,))],
        compiler_params=pltpu.CompilerParams(vmem_limit_bytes=VMEM_LIMIT),
    )(payload, w_ada_s, b_ada4, w_in3)


def _chips():
    x, y, _ = _pos()
    out = []
    for k in (1, 2, 3):
        px, py = _flip(x, (k >> 1) & 1), _flip(y, k & 1)
        out.append((px, py, 2 * px + py))
    return out


def _half_rows(ref, which):
    half = ref.shape[-2] // 2
    return pl.ds(pl.multiple_of(which * half, 8), half)


def _plan_small():
    def plan(refs):
        me = _logical(_pos())
        return [(refs[0], refs[1].at[me], _peer(k), refs[1].at[_logical(_peer(k))]) for k in range(1, 8)]
    return plan


def _small_sum(vec, land, me_arr):
    n = vec.shape[1]

    def body(me_ref, v_ref, land_ref, tot_ref, all_ref):
        tot = None
        for i in range(8):
            row = jnp.where(me_ref[0] == i, v_ref[...], land_ref[i])
            all_ref[i] = row
            tot = row if i == 0 else tot + row
        tot_ref[...] = tot

    return pl.pallas_call(
        body, name="small_sum",
        grid_spec=pltpu.PrefetchScalarGridSpec(
            num_scalar_prefetch=1, grid=(1,),
            in_specs=[pl.BlockSpec((1, n), lambda i, me_ref: (0, 0)), pl.BlockSpec((8, 1, n), lambda i, me_ref: (0, 0, 0))],
            out_specs=[pl.BlockSpec((1, n), lambda i, me_ref: (0, 0)),
                       pl.BlockSpec((8, 1, n), lambda i, me_ref: (0, 0, 0))]),
        out_shape=[jax.ShapeDtypeStruct((1, n), F32), jax.ShapeDtypeStruct((8, 1, n), F32)],
        compiler_params=_cp("arbitrary"),
    )(me_arr, vec, land)


def _add_half(g, sib, c_arr, rb, name):
    _, R, C = g.shape
    half = R // 2
    nb = half // rb

    def body(c_ref, g_ref, s_ref, o_ref):
        o_ref[...] = (g_ref[...] + s_ref[...]).astype(BF16)

    return pl.pallas_call(
        body, name=name,
        grid_spec=pltpu.PrefetchScalarGridSpec(
            num_scalar_prefetch=1, grid=(4, nb),
            in_specs=[pl.BlockSpec((1, rb, C), lambda s, i, c_ref: (s, c_ref[0] * nb + i, 0)),
                      pl.BlockSpec((1, rb, C), lambda s, i, c_ref: (s, i, 0))],
            out_specs=pl.BlockSpec((1, rb, C), lambda s, i, c_ref: (s, i, 0))),
        out_shape=jax.ShapeDtypeStruct((4, half, C), BF16),
        compiler_params=_cp("parallel", "parallel"),
    )(c_arr, g, sib)


def _add_half_in(gq, gz, sibq, sibz, c_arr, rb):
    half = D // 2
    nq = gq.shape[1]
    wide = -(-IN_SH // 128) * 128

    def sel(rows, first, lo):
        return (_iota((rows, wide), 0) + (first - lo) == _iota((rows, wide), 1)).astype(BF16)

    def body(c_ref, gq_ref, gz_ref, sq_ref, sz_ref, o_ref):
        q = (gq_ref[...] + sq_ref[...]).astype(BF16)
        z = (gz_ref[...] + sz_ref[...]).astype(BF16)
        for s in range(4):
            lo, hi = s * IN_SH, (s + 1) * IN_SH
            acc = jnp.zeros((rb, wide), F32)
            if lo < nq:
                a0, a1 = lo // 128 * 128, min(nq, -(-min(hi, nq) // 128) * 128)
                acc = acc + _dot(q[:, a0:a1], sel(a1 - a0, a0, lo))
            if hi > nq:
                a0, a1 = (max(lo, nq) - nq) // 128 * 128, -(-(hi - nq) // 128) * 128
                acc = acc + _dot(z[:, a0:a1], sel(a1 - a0, nq + a0, lo))
            o_ref[s] = acc[:, :IN_SH].astype(BF16)

    nb = half // rb
    mine = lambda w: pl.BlockSpec((rb, w), lambda i, c_ref: (c_ref[0] * nb + i, 0))
    sib = lambda w: pl.BlockSpec((rb, w), lambda i, c_ref: (i, 0))
    return pl.pallas_call(
        body, name="grad_add_in",
        grid_spec=pltpu.PrefetchScalarGridSpec(
            num_scalar_prefetch=1, grid=(nb,),
            in_specs=[mine(nq), mine(gz.shape[1]), sib(nq), sib(gz.shape[1])],
            out_specs=pl.BlockSpec((4, rb, IN_SH), lambda i, c_ref: (0, i, 0))),
        out_shape=jax.ShapeDtypeStruct((4, half, IN_SH), BF16),
        compiler_params=_cp("parallel"),
    )(c_arr, gq, gz, sibq, sibz)


def _sum4(parts, land, s_arr, rb, name):
    _, H, C = land.shape

    def body(s_ref, own_ref, r_ref, o_ref):
        own = own_ref[0].astype(F32)
        tot = jnp.zeros((rb, C), F32)
        for j in range(4):
            tot = tot + jnp.where(s_ref[0] == j, own, r_ref[j].astype(F32))
        o_ref[...] = tot

    return pl.pallas_call(
        body, name=name,
        grid_spec=pltpu.PrefetchScalarGridSpec(
            num_scalar_prefetch=1, grid=(H // rb,),
            in_specs=[pl.BlockSpec((1, rb, C), lambda i, s_ref: (s_ref[0], i, 0)),
                      pl.BlockSpec((4, rb, C), lambda i, s_ref: (0, i, 0))],
            out_specs=pl.BlockSpec((rb, C), lambda i, s_ref: (i, 0))),
        out_shape=jax.ShapeDtypeStruct((H, C), F32), compiler_params=_cp("parallel"),
    )(s_arr, parts, land)


HBM_SPEC = pl.BlockSpec(memory_space=pltpu.HBM)
SEM_SPEC = pl.BlockSpec(memory_space=pltpu.SEMAPHORE)
EFFECT = pltpu.SideEffectType.DATAFLOW_SIDE_EFFECTING


def _split_start(name, bufs, n_sem, plan, dep):
    nb = len(bufs)

    def body(*refs):
        ins, send, recv, token = refs[:nb], refs[nb + 1], refs[nb + 2], refs[-1]
        for i, (src, dst, dev, _) in enumerate(plan(ins)):
            pltpu.make_async_remote_copy(src, dst, send.at[i], recv.at[i], device_id=dev, device_id_type=MESH).start()
        token[...] = jnp.zeros_like(token)

    outs = pl.pallas_call(
        body, name=name,
        out_shape=(pltpu.SemaphoreType.DMA((n_sem,)), pltpu.SemaphoreType.DMA((n_sem,)),
                   *[pltpu.HBM(b.shape, b.dtype) for b in bufs], jax.ShapeDtypeStruct((8, 128), F32)),
        in_specs=[HBM_SPEC] * nb + [pl.BlockSpec(memory_space=pl.ANY)],
        out_specs=(SEM_SPEC, SEM_SPEC, *([HBM_SPEC] * nb), pl.BlockSpec(memory_space=pltpu.VMEM)),
        input_output_aliases={i: 2 + i for i in range(nb)},
        compiler_params=pltpu.CompilerParams(has_side_effects=EFFECT),
    )(*[pltpu.with_memory_space_constraint(b, pltpu.HBM) for b in bufs], dep)
    return outs[0], outs[1], list(outs[2:2 + nb]), outs[-1]


def _split_wait(name, send, recv, bufs, after, plan):
    nb = len(bufs)
    after = list(after) if isinstance(after, (list, tuple)) else [after]

    def body(*refs):
        ins, send_s, recv_s = refs[:nb], refs[nb], refs[nb + 1]
        for i, (src, dst, dev, mine) in enumerate(plan(ins)):
            pltpu.make_async_remote_copy(src, dst, send_s.at[i], recv_s.at[i], device_id=dev,
                                         device_id_type=MESH).wait_send()
            pltpu.make_async_remote_copy(src, mine, send_s.at[i], recv_s.at[i], device_id=dev,
                                         device_id_type=MESH).wait_recv()

    outs = pl.pallas_call(
        body, name=name, out_shape=[pltpu.HBM(b.shape, b.dtype) for b in bufs],
        in_specs=[HBM_SPEC] * nb + [SEM_SPEC, SEM_SPEC] + [HBM_SPEC] * len(after),
        out_specs=[HBM_SPEC] * nb, input_output_aliases={i: i for i in range(nb)},
        compiler_params=pltpu.CompilerParams(has_side_effects=EFFECT),
    )(*bufs, send, recv, *[pltpu.with_memory_space_constraint(a, pltpu.HBM) for a in after])
    return list(outs)


def _copies_now(name, bufs, n_sem, plan):
    nb = len(bufs)

    def body(*refs):
        ins, token, send, recv = refs[:nb], refs[2 * nb], refs[-2], refs[-1]
        token[...] = jnp.zeros_like(token)
        todo = plan(ins)
        for i, (src, dst, dev, _) in enumerate(todo):
            pltpu.make_async_remote_copy(src, dst, send.at[i], recv.at[i], device_id=dev, device_id_type=MESH).start()
        for i, (src, dst, dev, mine) in enumerate(todo):
            pltpu.make_async_remote_copy(src, mine, send.at[i], recv.at[i], device_id=dev, device_id_type=MESH).wait_recv()
        for i, (src, dst, dev, _) in enumerate(todo):
            pltpu.make_async_remote_copy(src, dst, send.at[i], recv.at[i], device_id=dev, device_id_type=MESH).wait_send()

    outs = pl.pallas_call(
        body, name=name,
        out_shape=[pltpu.HBM(b.shape, b.dtype) for b in bufs] + [jax.ShapeDtypeStruct((8, 128), F32)],
        in_specs=[HBM_SPEC] * nb, out_specs=[HBM_SPEC] * nb + [pl.BlockSpec(memory_space=pltpu.VMEM)],
        input_output_aliases={i: i for i in range(nb)},
        scratch_shapes=[pltpu.SemaphoreType.DMA((n_sem,)), pltpu.SemaphoreType.DMA((n_sem,))],
    )(*[pltpu.with_memory_space_constraint(b, pltpu.HBM) for b in bufs])
    return list(outs[:nb]), outs[nb]


def _slot(land, s, rows, cols):
    if cols is None:
        return land.at[s, rows]
    return land.at[rows, pl.ds(pl.multiple_of(s * cols, 128), cols)]


def _plan_gather_ici(cols):
    nw = len(cols)

    def plan(refs):
        x, y, c = _pos()
        my_s = 2 * x + y
        out = []
        for w in range(nw):
            mine = _half_rows(refs[w], c)
            for px, py, ps in _chips():
                out.append((refs[w].at[mine], _slot(refs[nw + w], my_s, mine, cols[w]), (px, py, c),
                            _slot(refs[nw + w], ps, mine, cols[w])))
        return out
    return plan


def _plan_gather_fwd(cols, rows):
    def plan(refs):
        x, y, c = _pos()
        out = []
        for w in range(len(cols)):
            half = rows[w] // 2
            mine = pl.ds(pl.multiple_of(c * half, 8), half)
            other = pl.ds(pl.multiple_of((1 - c) * half, 8), half)
            for px, py, ps in _chips():
                got = _slot(refs[w], ps, mine, cols[w])
                out.append((got, got, (x, y, 1 - c), _slot(refs[w], ps, other, cols[w])))
        return out
    return plan


def _plan_swap(nw):
    def plan(refs):
        x, y, c = _pos()
        return [(refs[w].at[:, _half_rows(refs[w], 1 - c)], refs[nw + w], (x, y, 1 - c), refs[nw + w])
                for w in range(nw)]
    return plan


def _plan_swap_rows(nw):
    def plan(refs):
        x, y, c = _pos()
        return [(refs[w].at[_half_rows(refs[w], 1 - c)], refs[nw + w], (x, y, 1 - c), refs[nw + w])
                for w in range(nw)]
    return plan


def _plan_scatter(nw):
    def plan(refs):
        x, y, c = _pos()
        my_s = 2 * x + y
        out = []
        for w in range(nw):
            for px, py, ps in _chips():
                out.append((refs[w].at[ps], refs[nw + w].at[my_s], (px, py, c), refs[nw + w].at[ps]))
        return out
    return plan


def _plan_scatter_both():
    def plan(refs):
        x, y, c = _pos()
        my_s = 2 * x + y
        src, land = refs
        out = []
        for px, py, ps in _chips():
            out.append((src.at[ps], land.at[my_s, c], (px, py, c), land.at[ps, c]))
            out.append((src.at[ps], land.at[my_s, c], (px, py, 1 - c), land.at[ps, 1 - c]))
        out.append((src.at[my_s], land.at[my_s, c], (x, y, 1 - c), land.at[my_s, 1 - c]))
        return out
    return plan


def _sum4_both(parts, land, s_arr, c_arr):
    _, _, H, C = land.shape

    def body(s_ref, c_ref, own_ref, r_ref, o_ref):
        mine = pl.program_id(0) == c_ref[0]
        own = own_ref[0].astype(F32)
        tot = jnp.zeros((H, C), F32)
        for j in range(4):
            tot = tot + jnp.where(jnp.logical_and(mine, s_ref[0] == j), own, r_ref[j, 0].astype(F32))
        o_ref[0] = tot

    return pl.pallas_call(
        body, name="grad_sum_in",
        grid_spec=pltpu.PrefetchScalarGridSpec(
            num_scalar_prefetch=2, grid=(2,),
            in_specs=[pl.BlockSpec((1, H, C), lambda h, s_ref, c_ref: (s_ref[0], 0, 0)),
                      pl.BlockSpec((4, 1, H, C), lambda h, s_ref, c_ref: (0, h, 0, 0))],
            out_specs=pl.BlockSpec((1, H, C), lambda h, s_ref, c_ref: (h, 0, 0))),
        out_shape=jax.ShapeDtypeStruct((2, H, C), F32), compiler_params=_cp("parallel"),
    )(s_arr, c_arr, parts, land).reshape(2 * H, C)


def _plan_join(nw):
    def plan(refs):
        x, y, c = _pos()
        out = []
        for w in range(nw):
            land = refs[nw + w]
            out.append((refs[w], land.at[_half_rows(land, c)], (x, y, 1 - c), land.at[_half_rows(land, 1 - c)]))
        return out
    return plan


def _hbm_empty(shape, dtype):
    return pltpu.with_memory_space_constraint(lax.empty(shape, dtype), pltpu.HBM)


def _put_slot(land, own, slot):
    return lax.dynamic_update_slice(land, own[None], (slot,) + (0,) * own.ndim)


def _w_in_assemble(land, own, s_arr, rb):
    wide = -(-IN_SH // 128) * 128
    starts = [s * IN_SH // 128 * 128 for s in range(4)]
    ends = [min(IN_PAD, -(-(s + 1) * IN_SH // 128) * 128) for s in range(4)]

    def body(s_ref, land_ref, own_ref, o_ref, parts):
        @pl.when(pl.program_id(0) == 0)
        def _():
            parts[...] = jnp.zeros_like(parts)

        acc = []
        for s in range(4):
            parts[s, :, 0:IN_SH] = jnp.where(s_ref[0] == s, own_ref[...], land_ref[s])
            w = ends[s] - starts[s]
            sel = (_iota((wide, w), 0) + (s * IN_SH - starts[s]) == _iota((wide, w), 1)).astype(BF16)
            acc.append(_dot(parts[s], sel))
        for s in range(4):
            lo = starts[s] if s == 0 else ends[s - 1]
            hi = starts[s + 1] if s < 3 else ends[s]
            o_ref[:, lo:hi] = acc[s][:, lo - starts[s]:hi - starts[s]].astype(BF16)
            if s < 3:
                a, b = starts[s + 1], ends[s]
                o_ref[:, a:b] = (acc[s][:, a - starts[s]:b - starts[s]] + acc[s + 1][:, 0:b - a]).astype(BF16)

    return pl.pallas_call(
        body, name="w_in_assemble",
        grid_spec=pltpu.PrefetchScalarGridSpec(
            num_scalar_prefetch=1, grid=(D // rb,),
            in_specs=[pl.BlockSpec((4, rb, IN_SH), lambda i, s_ref: (0, i, 0)),
                      pl.BlockSpec((rb, IN_SH), lambda i, s_ref: (i, 0))],
            out_specs=pl.BlockSpec((rb, IN_PAD), lambda i, s_ref: (i, 0)),
            scratch_shapes=[pltpu.VMEM((4, rb, wide), BF16)]),
        out_shape=jax.ShapeDtypeStruct((D, IN_PAD), BF16), compiler_params=_cp("arbitrary"),
    )(s_arr, land, own)


def _pad_lanes(a, n):
    return jnp.pad(a, ((0, 0), (0, n - a.shape[1])))


def kernel(x, c, positions, w_ada, b_ada, norm1_w, w_in, conv_w, conv_b, dt_bias, a_log, d_skip, attn_sinks, ssm_norm_w, w_out, norm2_w, w_gate_up, w_down, final_norm_w, loss_target, m_w_ada, m_b_ada, m_norm1_w, m_w_in, m_conv_w, m_conv_b, m_dt_bias, m_a_log, m_d_skip, m_attn_sinks, m_ssm_norm_w, m_w_out, m_norm2_w, m_w_gate_up, m_w_down, m_final_norm_w, v_w_ada, v_b_ada, v_norm1_w, v_w_in, v_conv_w, v_conv_b, v_dt_bias, v_a_log, v_d_skip, v_attn_sinks, v_ssm_norm_w, v_w_out, v_norm2_w, v_w_gate_up, v_w_down, v_final_norm_w):
    T = x.shape[1]
    tm = min(256, T)
    xi, yi, ci = lax.axis_index("x"), lax.axis_index("y"), lax.axis_index("c")
    my_s = 2 * xi + yi
    xs = x[0]
    tgt = loss_target[0]

    payload = jnp.concatenate([c, conv_w[0].reshape(1, CONVK * 256)], axis=1)
    gat, mod4, tok, w_in_b = _mod_exchange(payload, w_ada[0], b_ada.reshape(4, 1, 1536), w_in.transpose(2, 0, 1))
    mod6 = mod4.reshape(6, D)
    cw_dev = gat[:, 0, D:].reshape(4, 2, CONVK, 256)[:, 0]
    conv_full = cw_dev.transpose(1, 0, 2).reshape(CONVK, CONVC)

    s_i, r_i, bufs, tok = _split_start("wgather_in_ici_start", [w_in_b, _hbm_empty((4,) + w_in_b.shape, BF16)], 3,
                                       _plan_gather_ici([None]), tok)
    inv_freq = (10000.0 ** (-jnp.arange(32, dtype=F32) / 32))
    cos, sin_s = _rope_tables(positions, inv_freq.reshape(32, 1), min(512, T), tok)
    late = [w_out[0].astype(BF16), w_gate_up[0].astype(BF16), w_down[0].astype(BF16)]
    bufs = _split_wait("wgather_in_ici_wait", s_i, r_i, bufs, [cos] + late, _plan_gather_ici([None]))
    own_in = bufs[0]
    bufs, tok = _copies_now("wgather_in_fwd", bufs[1:], 3, _plan_gather_fwd([None], [D]))
    s_arr = my_s.reshape(1).astype(jnp.int32)
    w_pad = _w_in_assemble(bufs[0], own_in, s_arr, 256)

    lands = [_hbm_empty((4, D // 4, D), BF16), _hbm_empty((D, 2 * DFF), BF16), _hbm_empty((4, DFF // 4, D), BF16)]
    cols3, rows3 = [None, GU_SH, None], [D // 4, D, DFF // 4]
    s_a, r_a, bufs, tok = _split_start("wgather_ici_start", late + lands, 9, _plan_gather_ici(cols3), tok)

    qkv, z, xbc, dtr, h1b = _in_proj_fwd(xs, cos, sin_s, mod6, norm1_w, w_pad, min(512, T), tok)
    sinks = attn_sinks
    attn, lse = _attn_fwd(qkv, sinks)
    bufs = _split_wait("wgather_ici_wait", s_a, r_a, bufs, attn, _plan_gather_ici(cols3))
    late = bufs[:3]
    s_b, r_b, lands, tok = _split_start("wgather_fwd_start", bufs[3:], 9, _plan_gather_fwd(cols3, rows3), attn)
    dtb = _pad_lanes(dt_bias, 128)
    alog = _pad_lanes(a_log, 128)
    dskx = jnp.repeat(d_skip, HD, axis=1)
    mats = _ssd_mats()
    ynorm, ypre, states, conv_pre = _ssd_fwd(xbc, z, dtr, conv_full, conv_b, dtb, alog, dskx, ssm_norm_w, mats, tok)
    lands = _split_wait("wgather_fwd_wait", s_b, r_b, lands, ynorm, _plan_gather_fwd(cols3, rows3))
    w_out_f = _put_slot(lands[0], late[0], my_s).reshape(D, D)
    w_dn_f = _put_slot(lands[2], late[2], my_s).reshape(DFF, D)

    fw2 = final_norm_w.reshape(1, D)
    sq, dmix, dx1, h2b, act, dfb, dgu, dob, sm_ffn = _mix_ffn(
        xs, attn, ynorm, tgt, mod6, norm2_w, fw2, w_out_f, lands[1], late[1], s_arr, w_dn_f, tm)

    tt = min(2048, T)
    c_arr = ci.reshape(1).astype(jnp.int32)
    tok0 = jnp.zeros((8, 128), F32)
    gw_dn4 = _tn_matmul(act, dfb, GU_SH, D, tt, "dw_down", tok0).reshape(4, DFF // 4, D)
    gw_gu4 = _tn_matmul(h2b, dgu, D, GU_SH, tt, "dw_gate_up", tok0)
    gw_out4 = _tn_matmul_rows(attn, ynorm, dob, tt, "dw_out", tok0).reshape(4, D // 4, D)
    big1 = [gw_out4, gw_gu4, gw_dn4]
    rbs1 = [128, 512, 352]
    sib1 = [_hbm_empty((4, g.shape[1] // 2, g.shape[2]), F32) for g in big1]
    s_c, r_c, bufs, tok = _split_start("gswap_start", big1 + sib1, 3, _plan_swap(3), tok0)

    dzxd, d_cw, d_cb, d_sw, d_sk, d_dtb, d_av = _ssd_bwd(
        xbc, conv_pre, z, dtr, ypre, states, dmix, conv_full, dtb, alog, dskx, ssm_norm_w, mats, tok)
    bufs = _split_wait("gswap_wait", s_c, r_c, bufs, dzxd, _plan_swap(3))
    sums1 = [_add_half(g, s, c_arr, rb, "grad_add_%d" % i)
             for i, (g, s, rb) in enumerate(zip(bufs[:3], bufs[3:], rbs1))]
    land1 = [_hbm_empty(p.shape, BF16) for p in sums1]
    s_d, r_d, bufs, tok = _split_start("gscatter_start", sums1 + land1, 9, _plan_scatter(3), tok0)
    dqkv, d_sinks = _attn_bwd(qkv, sinks, lse, dmix, cos, sin_s, tok)
    bufs = _split_wait("gscatter_wait", s_d, r_d, bufs, dqkv, _plan_scatter(3))
    halves1 = [_sum4(p, l, s_arr, rb, "grad_sum_%d" % i)
               for i, (p, l, rb) in enumerate(zip(bufs[:3], bufs[3:], rbs1))]
    full1 = [_hbm_empty((2 * h.shape[0], h.shape[1]), F32) for h in halves1]
    s_e, r_e, bufs, tok = _split_start("gjoin_start", halves1 + full1, 3, _plan_join(3), tok0)
    gq = _tn_matmul(h1b, dqkv, D, 768, tt, "dw_in_qkv", tok)[0]
    gz = _tn_matmul(h1b, dzxd, D, IN_PAD - 768, tt, "dw_in_zxd", tok)[0]
    joined1 = _split_wait("gjoin_wait", s_e, r_e, bufs, [gq, gz], _plan_join(3))

    sibs = [_hbm_empty((D // 2, g.shape[1]), F32) for g in (gq, gz)]
    s_f, r_f, bufs, tok = _split_start("gswap_in_start", [gq, gz] + sibs, 2, _plan_swap_rows(2), tok0)
    g_dn_s, d_dn, m_dn, v_dn = _adam_2d(w_down[0], joined1[2], joined1[5], m_w_down[0], v_w_down[0], c_arr, 352,
                                        "adam_w_down", tok)
    g_gu_s, d_gu, m_gu, v_gu = _adam_2d(w_gate_up[0], joined1[1], joined1[4], m_w_gate_up[0], v_w_gate_up[0], c_arr,
                                        256, "adam_w_gate_up", tok)
    g_out_s, d_out, m_out, v_out = _adam_2d(w_out[0], joined1[0], joined1[3], m_w_out[0], v_w_out[0], c_arr, 128,
                                            "adam_w_out", tok)
    bufs = _split_wait("gswap_in_wait", s_f, r_f, bufs, [d_dn, d_gu, d_out], _plan_swap_rows(2))
    sum0 = _add_half_in(bufs[0], bufs[1], bufs[2], bufs[3], c_arr, min(256, D // 2))
    s_g, r_g, bufs, tok = _split_start("gscatter_in_start", [sum0, _hbm_empty((4, 2) + sum0.shape[1:], BF16)], 7,
                                       _plan_scatter_both(), tok0)
    grad_x, sm_in = _in_proj_bwd(xs, dx1, dqkv, dzxd, mod6, norm1_w, w_pad, min(512, T), tok)

    a_neg = -jnp.exp(alog)
    pieces = [sm_in[1:2], sm_in[2:3], sm_ffn[5:6], sm_ffn[2:3], sm_ffn[3:4], sm_ffn[4:5],
              sm_in[0:1], sm_ffn[1:2], sm_ffn[0:1], d_cb, d_cw.reshape(1, CONVK * CONVC),
              _pad_lanes(d_sw, SW), d_dtb, d_av * a_neg, d_sk, d_sinks,
              _pad_lanes((0.5 / D * jnp.sum(sq)).reshape(1, 1), 128)]
    vec = jnp.concatenate(pieces, axis=1)
    s_h, r_h, rows8, tok_small = _split_start("small_start", [vec, _hbm_empty((8,) + vec.shape, F32)], 7,
                                              _plan_small(), tok0)

    bufs = _split_wait("gscatter_in_wait", s_g, r_g, bufs, [grad_x, tok_small], _plan_scatter_both())
    gw_in_s = _sum4_both(bufs[0], bufs[1], s_arr, c_arr)
    native = lambda a: a.transpose(2, 0, 1)
    adam_in = _adam_w_in(native(w_in), gw_in_s, native(m_w_in), native(v_w_in))
    g_in_s, d_in, m_in, v_in = [a.transpose(1, 2, 0) for a in adam_in]
    rows8 = _split_wait("small_wait", s_h, r_h, rows8, [adam_in[1]], _plan_small())
    tot, allv = _small_sum(rows8[0], rows8[1], (4 * xi + 2 * yi + ci).reshape(1).astype(jnp.int32))
    o = 0
    offs = []
    for p in pieces:
        offs.append(o)
        o += p.shape[1]
    seg = lambda i, n: (offs[i], n)
    g_conv_w = lax.dynamic_slice_in_dim(
        tot[:, offs[10]:offs[10] + CONVK * CONVC].reshape(CONVK, CONVC), my_s * 256, 256, axis=1)
    loss = tot[0, offs[16]]

    small_names = ["b_ada", "norm1_w", "conv_w", "conv_b", "dt_bias", "a_log", "d_skip", "attn_sinks", "ssm_norm_w",
                   "norm2_w", "final_norm_w"]
    small_g = [(0, 6 * D), seg(6, D), g_conv_w, seg(9, D), seg(12, 8), seg(13, 8), seg(14, 8), seg(15, 8),
               seg(11, SW), seg(7, D), seg(8, D)]
    as2d = lambda a: a.reshape(-1, a.shape[-1])
    small_w = [as2d(a) for a in (b_ada, norm1_w, conv_w, conv_b, dt_bias, a_log, d_skip, attn_sinks, ssm_norm_w,
                                 norm2_w, final_norm_w)]
    small_m = [as2d(a) for a in (m_b_ada, m_norm1_w, m_conv_w, m_conv_b, m_dt_bias, m_a_log, m_d_skip, m_attn_sinks,
                                 m_ssm_norm_w, m_norm2_w, m_final_norm_w)]
    small_v = [as2d(a) for a in (v_b_ada, v_norm1_w, v_conv_w, v_conv_b, v_dt_bias, v_a_log, v_d_skip, v_attn_sinks,
                                 v_ssm_norm_w, v_norm2_w, v_final_norm_w)]
    small_g, sd, smn, svn = _adam_small(tot, small_g, small_w, small_m, small_v)
    g_ada, d_ada, m_ada, v_ada = _adam_w_ada(gat, allv, s_arr, w_ada[0], m_w_ada[0], v_w_ada[0], 256)

    order = ["w_ada", "b_ada", "norm1_w", "w_in", "conv_w", "conv_b", "dt_bias", "a_log", "d_skip", "attn_sinks",
             "ssm_norm_w", "w_out", "norm2_w", "w_gate_up", "w_down", "final_norm_w"]
    shapes = dict(w_ada=w_ada.shape, b_ada=b_ada.shape, norm1_w=norm1_w.shape, w_in=w_in.shape, conv_w=conv_w.shape,
                  conv_b=conv_b.shape, dt_bias=dt_bias.shape, a_log=a_log.shape, d_skip=d_skip.shape,
                  attn_sinks=attn_sinks.shape, ssm_norm_w=ssm_norm_w.shape, w_out=w_out.shape, norm2_w=norm2_w.shape,
                  w_gate_up=w_gate_up.shape, w_down=w_down.shape, final_norm_w=final_norm_w.shape)
    grads = dict(w_ada=g_ada, w_in=g_in_s, w_out=g_out_s, w_gate_up=g_gu_s, w_down=g_dn_s)
    deltas = dict(w_ada=d_ada, w_in=d_in, w_out=d_out, w_gate_up=d_gu, w_down=d_dn)
    new_m = dict(w_ada=m_ada, w_in=m_in, w_out=m_out, w_gate_up=m_gu, w_down=m_dn)
    new_v = dict(w_ada=v_ada, w_in=v_in, w_out=v_out, w_gate_up=v_gu, w_down=v_dn)
    for i, nme in enumerate(small_names):
        grads[nme], deltas[nme], new_m[nme], new_v[nme] = small_g[i], sd[i], smn[i], svn[i]
    outs = [loss, grad_x[None]]
    for table in (grads, deltas, new_m, new_v):
        outs += [table[nme].reshape(shapes[nme]) for nme in order]
    return tuple(outs)
```

```python
import functools
import math

import jax
import jax.numpy as jnp
from jax import lax
from jax.experimental import pallas as pl
from jax.experimental.pallas import tpu as pltpu

F32 = jnp.float32
BF16 = jnp.bfloat16
HI = lax.Precision.HIGHEST
MESH = pl.DeviceIdType.MESH

D = 1024
HD = 64
AW = 512
SW = 512
NST = 128
CONVK = 4
CONVC = 1024
BLK = 128
CPS = 4
SSD_FWD_CPS = 8
ATTN_BPS = 8
IN_PROJ = 2312
IN_PAD = 2432
IN_SH = IN_PROJ // 4
DFF = 2816
GU_SH = 1408
FF_SPLITS = ((0, 1536), (1536, 2816))
EPS = 1e-6
NEG = -1e30
LR, B1, B2, AEPS, WD, STEP = 0.001, 0.9, 0.999, 1e-08, 0.01, 10
VMEM_LIMIT = 58 * 1024 * 1024


def _cp(*sem):
    return pltpu.CompilerParams(dimension_semantics=sem or None, vmem_limit_bytes=VMEM_LIMIT)


def _dot(a, b):
    return jnp.dot(a, b, preferred_element_type=F32)


def _dot_nt(a, b):
    return lax.dot_general(a, b, (((1,), (1,)), ((), ())), preferred_element_type=F32)


def _dot_tn(a, b):
    return lax.dot_general(a, b, (((0,), (0,)), ((), ())), preferred_element_type=F32)


def _dot_hi(a, b):
    return jnp.dot(a, b, precision=HI, preferred_element_type=F32)


def _sigmoid(x):
    return 1.0 / (1.0 + jnp.exp(-x))


def _iota(shape, dim):
    return lax.broadcasted_iota(jnp.int32, shape, dim)


def _load_resident(hbm_ref, vmem_ref, sem):
    @pl.when(pl.program_id(0) == 0)
    def _():
        cp = pltpu.make_async_copy(hbm_ref, vmem_ref, sem)
        cp.start()
        cp.wait()


def _swap32(t):
    lane = _iota(t.shape, 1)
    return jnp.where((lane & 63) < 32, pltpu.roll(t, 96, 1), pltpu.roll(t, 32, 1))


def _rope_fwd(t, cos, sin_s):
    return t * cos + _swap32(t) * sin_s


def _rope_bwd(t, cos, sin_s):
    return t * cos - _swap32(t) * sin_s


DEP_SPEC = pl.BlockSpec((8, 128), lambda *_: (0, 0))


def _rope_tables(pos_row, inv_freq_col, tm, dep):
    T = pos_row.shape[1]
    lane, row = jnp.arange(128)[None, :], jnp.arange(96)[:, None]
    pick = (lane % 32) == (row % 32)
    sel_cos = pick.astype(BF16)
    sel_sin = jnp.where(pick, jnp.where(lane % 64 < 32, -1.0, 1.0), 0.0).astype(BF16)

    def body(p_ref, f_ref, sc_ref, ss_ref, dep_ref, cos_ref, sin_ref):
        ang = f_ref[...] * p_ref[...].astype(F32)
        cos_ref[...] = _dot_tn(_pieces(jnp.cos(ang), 3, 0), sc_ref[...])
        sin_ref[...] = _dot_tn(_pieces(jnp.sin(ang), 3, 0), ss_ref[...])

    full = lambda a: pl.BlockSpec(a.shape, lambda i: (0,) * a.ndim)
    return pl.pallas_call(
        body, name="rope_tables", grid=(T // tm,),
        in_specs=[pl.BlockSpec((1, tm), lambda i: (0, i)), full(inv_freq_col), full(sel_cos), full(sel_sin), DEP_SPEC],
        out_specs=[pl.BlockSpec((tm, 128), lambda i: (i, 0))] * 2,
        out_shape=[jax.ShapeDtypeStruct((T, 128), F32)] * 2,
        compiler_params=_cp("parallel"),
    )(pos_row, inv_freq_col, sel_cos, sel_sin, dep)


def _in_proj_fwd(x, cos, sin_s, mod6, norm1_w, w_pad, tm, dep):
    T = x.shape[0]

    def body(x_ref, cos_ref, sin_ref, mod_ref, nw_ref, w_hbm, dep_ref, qkv_ref, z_ref, xbc_ref, dt_ref, h_ref, w_vmem,
             sem):
        _load_resident(w_hbm, w_vmem, sem)
        xv = x_ref[...]
        r = lax.rsqrt(jnp.mean(xv * xv, axis=-1, keepdims=True) + EPS)
        h = (xv * r * nw_ref[...]) * (1.0 + mod_ref[1:2, :]) + mod_ref[0:1, :]
        hb = h.astype(BF16)
        h_ref[...] = hb
        proj = _dot(hb, w_vmem[...])
        cs, sn = cos_ref[...], sin_ref[...]
        for j in range(5):
            qkv_ref[:, 128 * j:128 * (j + 1)] = _rope_fwd(proj[:, 128 * j:128 * (j + 1)], cs, sn).astype(BF16)
        qkv_ref[:, 640:768] = proj[:, 640:768].astype(BF16)
        z_ref[...] = proj[:, 768:1280]
        xbc_ref[...] = proj[:, 1280:2304]
        dt_ref[...] = proj[:, 2304:2432]

    row = lambda w: pl.BlockSpec((tm, w), lambda i: (i, 0))
    full = lambda a: pl.BlockSpec(a.shape, lambda i: (0,) * a.ndim)
    return pl.pallas_call(
        body, name="in_proj_fwd", grid=(T // tm,),
        in_specs=[row(D), row(128), row(128), full(mod6), full(norm1_w), pl.BlockSpec(memory_space=pl.ANY), DEP_SPEC],
        out_specs=[row(768), row(512), row(1024), row(128), row(D)],
        out_shape=[jax.ShapeDtypeStruct((T, 768), BF16), jax.ShapeDtypeStruct((T, 512), F32),
                   jax.ShapeDtypeStruct((T, 1024), F32), jax.ShapeDtypeStruct((T, 128), F32),
                   jax.ShapeDtypeStruct((T, D), BF16)],
        scratch_shapes=[pltpu.VMEM((D, IN_PAD), BF16), pltpu.SemaphoreType.DMA],
        compiler_params=_cp("arbitrary"),
    )(x, cos, sin_s, mod6, norm1_w, w_pad, dep)


def _head_variants(pair, j):
    lane = _iota(pair.shape, 1)
    lo = lane < 64
    kv = j // 2
    ev = jnp.where(lo, pair, 0.0)
    od = jnp.where(lo, 0.0, pair)
    if kv == 0:
        od = pltpu.roll(od, 64, 1)
    else:
        ev = pltpu.roll(ev, 64, 1)
    return ev.astype(BF16), od.astype(BF16)


def _kv_variants(vcat):
    lane = _iota(vcat.shape, 1)
    lo = lane < 64
    v0 = jnp.where(lo, vcat, 0.0)
    v1 = jnp.where(lo, 0.0, vcat)
    out = {
        (0, 0): v0, (0, 1): pltpu.roll(v0, 64, 1),
        (1, 0): pltpu.roll(v1, 64, 1), (1, 1): v1,
    }
    return {k: v.astype(BF16) for k, v in out.items()}


def _fold_masks(n):
    upper = _iota((BLK, BLK), 1) > _iota((BLK, BLK), 0)
    return upper, upper & (n == 0)


def _attn_fwd(qkv, sinks):
    CPS = ATTN_BPS
    T = qkv.shape[0]
    nsteps = T // (CPS * BLK)

    def body(sink_ref, q_ref, kc_ref, kp_ref, vc_ref, vp_ref, o_ref, lse_ref):
        for sub in range(CPS):
            rows, before = slice(BLK * sub, BLK * (sub + 1)), slice(BLK * (sub - 1), BLK * sub)
            block(pl.program_id(0) * CPS + sub, sink_ref, q_ref.at[rows, :], kc_ref.at[rows, :],
                  kp_ref if sub == 0 else kc_ref.at[before, :], vc_ref.at[rows, :],
                  vp_ref if sub == 0 else vc_ref.at[before, :], o_ref.at[rows, :], lse_ref.at[rows, :])

    def block(n, sink_ref, q_ref, kc_ref, kp_ref, vc_ref, vp_ref, o_ref, lse_ref):
        vpv = _kv_variants(vp_ref[...].astype(F32))
        vcv = _kv_variants(vc_ref[...].astype(F32))
        q_all = jnp.concatenate(
            [v for j in range(4) for v in _head_variants(q_ref[:, 128 * j:128 * (j + 1)].astype(F32), j)], axis=0)
        s_prev = _dot_nt(q_all, kp_ref[...])
        s_cur = _dot_nt(q_all, kc_ref[...])
        upper, dead = _fold_masks(n)
        lane = _iota((BLK, 128), 1)
        lse_acc = jnp.zeros((BLK, 128), F32)
        for jj in range(4):
            acc = jnp.zeros((BLK, 128), F32)
            for par in range(2):
                h = 2 * jj + par
                rows = slice(h * BLK, (h + 1) * BLK)
                sink = sink_ref[0, h]
                s = jnp.where(dead, NEG, jnp.where(upper, s_prev[rows], s_cur[rows]) * 0.125)
                m = jnp.maximum(jnp.max(s, axis=1, keepdims=True), sink)
                p = jnp.exp(s - m)
                den = jnp.sum(p, axis=1, keepdims=True) + jnp.exp(sink - m)
                pn = p * (1.0 / den)
                acc = (acc + _dot(jnp.where(upper, pn, 0.0).astype(BF16), vpv[(jj // 2, par)])
                       + _dot(jnp.where(upper, 0.0, pn).astype(BF16), vcv[(jj // 2, par)]))
                lse_acc = jnp.where(lane == h, m + jnp.log(den), lse_acc)
            o_ref[:, 128 * jj:128 * (jj + 1)] = acc.astype(BF16)
        lse_ref[...] = lse_acc

    RB = CPS * BLK
    prev = lambda n: jnp.maximum(n * CPS - 1, 0)
    return pl.pallas_call(
        body, name="attn_fwd", grid=(nsteps,),
        in_specs=[pl.BlockSpec(memory_space=pltpu.SMEM),
                  pl.BlockSpec((RB, 512), lambda n: (n, 0)),
                  pl.BlockSpec((RB, 128), lambda n: (n, 4)),
                  pl.BlockSpec((BLK, 128), lambda n: (prev(n), 4)),
                  pl.BlockSpec((RB, 128), lambda n: (n, 5)),
                  pl.BlockSpec((BLK, 128), lambda n: (prev(n), 5))],
        out_specs=[pl.BlockSpec((RB, 512), lambda n: (n, 0)), pl.BlockSpec((RB, 128), lambda n: (n, 0))],
        out_shape=[jax.ShapeDtypeStruct((T, 512), BF16), jax.ShapeDtypeStruct((T, 128), F32)],
        compiler_params=_cp("parallel"),
    )(sinks, qkv, qkv, qkv, qkv, qkv)


def _attn_bwd(qkv, sinks, lse, dmix, cos, sin_s, dep):
    T = qkv.shape[0]
    nb = T // BLK

    def body(sink_ref, q_ref, kc_ref, kp_ref, vc_ref, vp_ref, lse_ref, do_ref, cq_ref, sq_ref, ck_ref, sk_ref,
             dep_ref, out_ref, ds_ref, dq_car, dk_car, dv_car):
        n = pl.program_id(0)
        lane = _iota((BLK, 128), 1)

        @pl.when(n == 0)
        def _():
            ds_ref[...] = jnp.zeros_like(ds_ref)
            dq_car[...] = jnp.zeros_like(dq_car)
            dk_car[...] = jnp.zeros_like(dk_car)
            dv_car[...] = jnp.zeros_like(dv_car)

        @pl.when(n < nb)
        def _():
            kp, kc, vp, vc = kp_ref[...], kc_ref[...], vp_ref[...], vc_ref[...]
            kpv = _kv_variants(kp.astype(F32))
            kcv = _kv_variants(kc.astype(F32))
            lse_v = lse_ref[...]
            q_all = jnp.concatenate(
                [v for j in range(4) for v in _head_variants(q_ref[:, 128 * j:128 * (j + 1)].astype(F32), j)], axis=0)
            do_all = jnp.concatenate(
                [v for j in range(4) for v in _head_variants(do_ref[:, 128 * j:128 * (j + 1)], j)], axis=0)
            s_prev, s_cur = _dot_nt(q_all, kp), _dot_nt(q_all, kc)
            dp_prev, dp_cur = _dot_nt(do_all, vp), _dot_nt(do_all, vc)
            upper, dead = _fold_masks(n)
            out_ref[:, 0:512] = dq_car[...]
            dsk = jnp.zeros((1, 128), F32)
            ds_u, ds_l, p_u, p_l = [], [], [], []
            for jj in range(4):
                dq_acc = jnp.zeros((BLK, 128), F32)
                for par in range(2):
                    h = 2 * jj + par
                    rows = slice(h * BLK, (h + 1) * BLK)
                    lse_h = jnp.sum(jnp.where(lane == h, lse_v, 0.0), axis=1, keepdims=True)
                    s = jnp.where(dead, NEG, jnp.where(upper, s_prev[rows], s_cur[rows]) * 0.125)
                    p = jnp.exp(s - lse_h)
                    dp = jnp.where(upper, dp_prev[rows], dp_cur[rows])
                    delta = jnp.sum(p * dp, axis=1, keepdims=True)
                    ds = p * (dp - delta) * 0.125
                    dsu, dsl = jnp.where(upper, ds, 0.0).astype(BF16), jnp.where(upper, 0.0, ds).astype(BF16)
                    dq_acc = dq_acc + _dot(dsu, kpv[(jj // 2, par)]) + _dot(dsl, kcv[(jj // 2, par)])
                    ds_u.append(dsu)
                    ds_l.append(dsl)
                    p_u.append(jnp.where(upper, p, 0.0).astype(BF16))
                    p_l.append(jnp.where(upper, 0.0, p).astype(BF16))
                    dsk = dsk + jnp.where(lane[0:1] == h, -jnp.sum(jnp.exp(sink_ref[0, h] - lse_h) * delta), 0.0)
                dq_car[:, 128 * jj:128 * (jj + 1)] = _rope_bwd(dq_acc, cq_ref[...], sq_ref[...]).astype(BF16)
            stack = lambda parts: jnp.concatenate(parts, axis=0)
            dk_prev, dk_cur = _dot_tn(stack(ds_u), q_all), _dot_tn(stack(ds_l), q_all)
            dv_prev, dv_cur = _dot_tn(stack(p_u), do_all), _dot_tn(stack(p_l), do_all)
            ds_ref[...] += dsk
            out_ref[:, 512:640] = _rope_bwd(dk_car[...] + dk_prev, ck_ref[...], sk_ref[...]).astype(BF16)
            out_ref[:, 640:768] = (dv_car[...] + dv_prev).astype(BF16)
            dk_car[...] = dk_cur
            dv_car[...] = dv_cur

        @pl.when(n == nb)
        def _():
            out_ref[:, 0:512] = dq_car[...]
            out_ref[:, 512:640] = _rope_bwd(dk_car[...], ck_ref[...], sk_ref[...]).astype(BF16)
            out_ref[:, 640:768] = dv_car[...].astype(BF16)

    cur = lambda n: jnp.minimum(n, nb - 1)
    prev = lambda n: jnp.maximum(cur(n) - 1, 0)
    outb = lambda n: jnp.maximum(n - 1, 0)
    return pl.pallas_call(
        body, name="attn_bwd", grid=(nb + 1,),
        in_specs=[pl.BlockSpec(memory_space=pltpu.SMEM),
                  pl.BlockSpec((BLK, 512), lambda n: (cur(n), 0)),
                  pl.BlockSpec((BLK, 128), lambda n: (cur(n), 4)),
                  pl.BlockSpec((BLK, 128), lambda n: (prev(n), 4)),
                  pl.BlockSpec((BLK, 128), lambda n: (cur(n), 5)),
                  pl.BlockSpec((BLK, 128), lambda n: (prev(n), 5)),
                  pl.BlockSpec((BLK, 128), lambda n: (cur(n), 0)),
                  pl.BlockSpec((BLK, 512), lambda n: (cur(n), 0)),
                  pl.BlockSpec((BLK, 128), lambda n: (cur(n), 0)),
                  pl.BlockSpec((BLK, 128), lambda n: (cur(n), 0)),
                  pl.BlockSpec((BLK, 128), lambda n: (outb(n), 0)),
                  pl.BlockSpec((BLK, 128), lambda n: (outb(n), 0)), DEP_SPEC],
        out_specs=[pl.BlockSpec((BLK, 768), lambda n: (outb(n), 0)), pl.BlockSpec((1, 128), lambda n: (0, 0))],
        out_shape=[jax.ShapeDtypeStruct((T, 768), BF16), jax.ShapeDtypeStruct((1, 128), F32)],
        scratch_shapes=[pltpu.VMEM((BLK, 512), BF16), pltpu.VMEM((BLK, 128), F32), pltpu.VMEM((BLK, 128), F32)],
        compiler_params=_cp("arbitrary"),
    )(sinks, qkv, qkv, qkv, qkv, qkv, lse, dmix, cos, sin_s, cos, sin_s, dep)


def _ssd_mats():
    e = jnp.arange(SW)[None, :] // HD == jnp.arange(128)[:, None]
    tri = jnp.arange(BLK)[None, :] <= jnp.arange(BLK)[:, None]
    return (jnp.tile(e, (3, 1)).astype(BF16), jnp.tile(e.T, (2, 1)).astype(BF16),
            jnp.tile(tri, (1, 3)).astype(BF16), jnp.tile(tri.T, (1, 3)).astype(BF16))


def _pieces(x, n, axis):
    out, r = [], x
    for i in range(n):
        p = r.astype(BF16)
        out.append(p)
        if i + 1 < n:
            r = r - p.astype(F32)
    return jnp.concatenate(out, axis=axis)


def _expand(x, e3):
    return _dot(_pieces(x, 3, 1), e3)


def _head_sums(x, et2):
    return _dot(_pieces(x, 2, 1), et2)


def _run_sum(tri3, x):
    return _dot(tri3, _pieces(x, 3, 0))


def _shift_down(u, tail, j):
    rolled = pltpu.roll(u, j, 0)
    first = jnp.where(_iota(tail.shape, 0) < j, pltpu.roll(tail, j, 0), rolled[0:8])
    return jnp.concatenate([first, rolled[8:]], axis=0)


def _shift_up(d, head, j):
    rolled = pltpu.roll(d, BLK - j, 0)
    last = jnp.where(_iota(head.shape, 0) >= 8 - j, pltpu.roll(head, 8 - j, 0), rolled[BLK - 8:])
    return jnp.concatenate([rolled[:BLK - 8], last], axis=0)


def _ssd_parts(dtr, dtb, alog, e3, tril3):
    xx = dtr + dtb
    dt = jnp.maximum(xx, 0.0) + jnp.log(1.0 + jnp.exp(-jnp.abs(xx)))
    a_neg = -jnp.exp(alog)
    tril = _iota((BLK, BLK), 1) <= _iota((BLK, BLK), 0)
    cs = _run_sum(tril3, dt * a_neg)
    csx = _expand(cs, e3)
    last = csx[BLK - 1:BLK, :]
    return dict(xx=xx, dt=dt, a_neg=a_neg, tril=tril, cs=cs, cs_t=cs.T,
                ecsx=jnp.exp(csx), dtex=jnp.exp(last - csx), cdx=jnp.exp(last), dtx=_expand(dt, e3))


def _decay(parts, h):
    seg = parts["cs"][:, h:h + 1] - parts["cs_t"][h:h + 1, :]
    return jnp.exp(jnp.where(parts["tril"], seg, NEG))


def _group_cols(a, g):
    return a[:, 256 * g:256 * (g + 1)]


def _ssd_fwd(xbc, z, dtr, conv_w, conv_b, dtb, alog, dskx, ssm_w, mats, dep):
    CPS = SSD_FWD_CPS
    T = xbc.shape[0]
    nc = T // BLK

    def body(u_ref, tail_ref, z_ref, dtr_ref, cw_ref, cb_ref, dtb_ref, al_ref, dk_ref, sw_ref, e3_ref, tril3_ref,
             dep_ref, yn_ref, yp_ref, st_ref, co_ref, s_scr):
        n = pl.program_id(0)

        @pl.when(n == 0)
        def _():
            s_scr[...] = jnp.zeros_like(s_scr)

        lane = _iota((BLK, 128), 1)
        lo = lane < 64
        for sub in range(CPS):
            rows = slice(BLK * sub, BLK * (sub + 1))
            u = u_ref[rows, :]
            tail = jnp.where(n > 0, tail_ref[...], 0.0) if sub == 0 else u_ref[BLK * sub - 8:BLK * sub, :]
            co = cb_ref[...] + cw_ref[3:4, :] * u
            for j in range(1, CONVK):
                co = co + cw_ref[3 - j:4 - j, :] * _shift_down(u, tail, j)
            co_ref[rows, :] = co
            xc = co * _sigmoid(co)
            pt = _ssd_parts(dtr_ref[rows, :], dtb_ref[...], al_ref[...], e3_ref[...], tril3_ref[...])
            xs = xc[:, :SW]
            bm = [xc[:, 512:640].astype(BF16), xc[:, 640:768].astype(BF16)]
            cm = [xc[:, 768:896].astype(BF16), xc[:, 896:1024].astype(BF16)]
            s_in = s_scr[...]
            st_ref[sub] = s_in
            xdt = xs * pt["dtx"]
            xde = (xdt * pt["dtex"]).astype(BF16)
            ys, s_new = [], []
            for g in range(2):
                cb = _dot_nt(cm[g], bm[g])
                yoff = _dot(cm[g], _group_cols(s_in, g).astype(BF16))
                s_new.append(_dot_tn(bm[g], _group_cols(xde, g)))
                for jj in range(2):
                    j = 2 * g + jj
                    chunk = xdt[:, 128 * j:128 * (j + 1)]
                    g_ev = (cb * _decay(pt, 2 * j)).astype(BF16)
                    g_od = (cb * _decay(pt, 2 * j + 1)).astype(BF16)
                    yd = (_dot(g_ev, jnp.where(lo, chunk, 0.0).astype(BF16))
                          + _dot(g_od, jnp.where(lo, 0.0, chunk).astype(BF16)))
                    ys.append(yd + yoff[:, 128 * jj:128 * (jj + 1)] * pt["ecsx"][:, 128 * j:128 * (j + 1)])
            y = jnp.concatenate(ys, axis=1) + xs * dk_ref[...]
            s_scr[...] = s_in * pt["cdx"] + jnp.concatenate(s_new, axis=1)
            yp_ref[rows, :] = y
            zv = z_ref[rows, :]
            yz = y * (zv * _sigmoid(zv))
            outs = []
            for g in range(2):
                yg = _group_cols(yz, g)
                outs.append(yg * lax.rsqrt(jnp.mean(yg * yg, axis=-1, keepdims=True) + EPS))
            yn_ref[rows, :] = (jnp.concatenate(outs, axis=1) * sw_ref[...]).astype(BF16)

    e3, _, tril3, _ = mats
    RB = CPS * BLK
    tail8 = lambda n: jnp.maximum(n * (RB // 8) - 1, 0)
    full = lambda a: pl.BlockSpec(a.shape, lambda n: (0,) * a.ndim)
    return pl.pallas_call(
        body, name="ssd_fwd", grid=(nc // CPS,),
        in_specs=[pl.BlockSpec((RB, CONVC), lambda n: (n, 0)), pl.BlockSpec((8, CONVC), lambda n: (tail8(n), 0)),
                  pl.BlockSpec((RB, SW), lambda n: (n, 0)), pl.BlockSpec((RB, 128), lambda n: (n, 0)),
                  full(conv_w), full(conv_b), full(dtb), full(alog), full(dskx), full(ssm_w), full(e3), full(tril3),
                  DEP_SPEC],
        out_specs=[pl.BlockSpec((RB, SW), lambda n: (n, 0)), pl.BlockSpec((RB, SW), lambda n: (n, 0)),
                   pl.BlockSpec((CPS, NST, SW), lambda n: (n, 0, 0)), pl.BlockSpec((RB, CONVC), lambda n: (n, 0))],
        out_shape=[jax.ShapeDtypeStruct((T, SW), BF16), jax.ShapeDtypeStruct((T, SW), F32),
                   jax.ShapeDtypeStruct((nc, NST, SW), F32), jax.ShapeDtypeStruct((T, CONVC), F32)],
        scratch_shapes=[pltpu.VMEM((NST, SW), F32)],
        compiler_params=_cp("arbitrary"),
    )(xbc, xbc, z, dtr, conv_w, conv_b, dtb, alog, dskx, ssm_w, e3, tril3, dep)


def _ssd_bwd(xbc, co_all, z, dtr, ypre, states, dmix, conv_w, dtb, alog, dskx, ssm_w, mats, dep):
    T = xbc.shape[0]
    nsteps = T // (CPS * BLK)

    def body(*refs):
        per_chunk, consts, out_ref, carried = refs[:7], refs[7:16], refs[17], refs[18:]
        i = pl.program_id(0)

        @pl.when(i == 0)
        def _():
            for r in carried:
                r[...] = jnp.zeros_like(r)

        for sub in reversed(range(CPS)):
            rows = slice(BLK * sub, BLK * (sub + 1))
            views = [r.at[sub:sub + 1] if k == 5 else r.at[rows, :] for k, r in enumerate(per_chunk)]
            chunk(*views, *consts, out_ref.at[rows, :], *carried)

        @pl.when(i == nsteps - 1)
        def _():
            dsk_ref, dskx_scr = carried[3], carried[8]
            dsk_ref[...] = _head_sums(jnp.broadcast_to(dskx_scr[...], (8, SW)), consts[6][...])[0:1]

    def chunk(u_ref, co_ref, z_ref, dtr_ref, yp_ref, st_ref, dyn_ref, cw_ref, dtb_ref, al_ref, dk_ref, sw_ref,
              e3_ref, et2_ref, tril3_ref, triu3_ref,
              out_ref, dcw_ref, dcb_ref, dsw_ref, dsk_ref, ddtb_ref, dav_ref, ds_scr, dco_scr, dskx_scr):
        co = co_ref[...]
        sg = _sigmoid(co)
        xc = co * sg
        pt = _ssd_parts(dtr_ref[...], dtb_ref[...], al_ref[...], e3_ref[...], tril3_ref[...])
        dtx, ecsx, dtex, cdx = pt["dtx"], pt["ecsx"], pt["dtex"], pt["cdx"]
        xs = xc[:, :SW]
        bm = [xc[:, 512:640].astype(BF16), xc[:, 640:768].astype(BF16)]
        cm = [xc[:, 768:896].astype(BF16), xc[:, 896:1024].astype(BF16)]
        s_in = st_ref[0]
        ds_out = ds_scr[...]
        e_t = et2_ref[...]

        zv = z_ref[...]
        sz = _sigmoid(zv)
        silu_z = zv * sz
        ypre = yp_ref[...]
        yz = ypre * silu_z
        dyn = dyn_ref[...]
        sw = sw_ref[...]
        dyz, yns = [], []
        for g in range(2):
            yg = _group_cols(yz, g)
            r = lax.rsqrt(jnp.mean(yg * yg, axis=-1, keepdims=True) + EPS)
            yn = yg * r
            dg = _group_cols(dyn, g) * _group_cols(sw, g)
            dyz.append(r * (dg - yn * jnp.mean(dg * yn, axis=-1, keepdims=True)))
            yns.append(yn)
        dyz = jnp.concatenate(dyz, axis=1)
        dsw_ref[...] += jnp.sum(dyn * jnp.concatenate(yns, axis=1), axis=0, keepdims=True)
        dy = dyz * silu_z
        dz = dyz * ypre * (sz * (1.0 + zv * (1.0 - sz)))

        xdt = xs * dtx
        xdt_b = xdt.astype(BF16)
        edy = (ecsx * dy).astype(BF16)
        xde = (xdt * dtex).astype(BF16)
        lane = _iota((BLK, 128), 1)
        lo = lane < 64
        row8 = _iota((8, 128), 0)
        dcs = jnp.zeros((BLK, 128), F32)
        col_rows = jnp.zeros((8, 128), F32)
        dxdt, bds, yoff, dbs, dcs_g, ds_new = [], [], [], [], [], []
        for g in range(2):
            s_g = _group_cols(s_in, g).astype(BF16)
            dso_g = _group_cols(ds_out, g).astype(BF16)
            cb = _dot_nt(cm[g], bm[g])
            bds.append(_dot(bm[g], dso_g))
            yoff.append(_dot(cm[g], s_g))
            dcb_g = jnp.zeros((BLK, BLK), F32)
            for jj in range(2):
                j = 2 * g + jj
                dy_c = dy[:, 128 * j:128 * (j + 1)]
                xdt_c = xdt_b[:, 128 * j:128 * (j + 1)]
                acc = jnp.zeros((BLK, 128), F32)
                for par in range(2):
                    h = 2 * j + par
                    lm = _decay(pt, h)
                    gm = cb * lm
                    dy_m = (jnp.where(lo, dy_c, 0.0) if par == 0 else jnp.where(lo, 0.0, dy_c)).astype(BF16)
                    dg_h = _dot_nt(dy_m, xdt_c)
                    w_h = dg_h * gm
                    dcs = dcs + jnp.where(lane == h, jnp.sum(w_h, axis=1, keepdims=True), 0.0)
                    col_rows = col_rows + jnp.where(row8 == h, jnp.sum(w_h, axis=0, keepdims=True), 0.0)
                    dcb_g = dcb_g + dg_h * lm
                    acc = acc + _dot_tn(gm.astype(BF16), dy_m)
                dxdt.append(acc)
            dcb_b = dcb_g.astype(BF16)
            dcs_g.append(_dot(dcb_b, bm[g]) + _dot_nt(_group_cols(edy, g), s_g))
            dbs.append(_dot_tn(dcb_b, cm[g]) + _dot_nt(_group_cols(xde, g), dso_g))
            ds_new.append(_dot_tn(cm[g], _group_cols(edy, g)))
        bds = jnp.concatenate(bds, axis=1)
        yoff = jnp.concatenate(yoff, axis=1) * ecsx
        dxdt = jnp.concatenate(dxdt, axis=1) + dtex * bds
        ds_scr[...] = cdx * ds_out + jnp.concatenate(ds_new, axis=1)

        t_m = _head_sums(dtex * xdt * bds, e_t)
        colsum_t = jnp.concatenate([col_rows, jnp.zeros((BLK - 8, 128), F32)], axis=0).T
        cd = jnp.exp(pt["cs"][BLK - 1:BLK, :])
        sds = jnp.sum(s_in * ds_out, axis=0, keepdims=True)
        last_row = jnp.sum(t_m, axis=0, keepdims=True) + cd * _head_sums(jnp.broadcast_to(sds, (8, SW)), e_t)[0:1]
        dcs = dcs - colsum_t + _head_sums(dy * yoff, e_t) - t_m
        dcs = dcs + jnp.where(_iota((BLK, 128), 0) == BLK - 1, last_row, 0.0)
        da = _run_sum(triu3_ref[...], dcs)
        dt = pt["dt"]
        ddt = da * pt["a_neg"] + _head_sums(dxdt * xs, e_t)
        dav_ref[...] += jnp.sum(da * dt, axis=0, keepdims=True)
        ddtr = ddt * _sigmoid(pt["xx"])
        ddtb_ref[...] += jnp.sum(ddtr, axis=0, keepdims=True)
        dxs = dxdt * dtx + dy * dk_ref[...]
        dskx_scr[...] += jnp.sum(dy * xs, axis=0, keepdims=True)
        dxc = jnp.concatenate([dxs, dbs[0], dbs[1], dcs_g[0], dcs_g[1]], axis=1)
        dco = dxc * (sg * (1.0 + co * (1.0 - sg)))

        dcb_ref[...] += jnp.sum(dco, axis=0, keepdims=True)
        u = u_ref[...]
        head = dco_scr[...]
        du = jnp.zeros_like(dco)
        for j in range(CONVK):
            up_j = dco if j == 0 else _shift_up(dco, head, j)
            dcw_ref[3 - j:4 - j, :] += jnp.sum(up_j * u, axis=0, keepdims=True)
            du = du + cw_ref[3 - j:4 - j, :] * up_j
        dco_scr[...] = dco[0:8]
        out_ref[:, 0:512] = dz.astype(BF16)
        out_ref[:, 512:1536] = du.astype(BF16)
        out_ref[:, 1536:1664] = ddtr.astype(BF16)

    e3, et2, tril3, triu3 = mats
    RB = CPS * BLK
    rev = lambda i: nsteps - 1 - i
    full = lambda a: pl.BlockSpec(a.shape, lambda i: (0,) * a.ndim)
    acc = lambda r, c: pl.BlockSpec((r, c), lambda i: (0, 0))
    return pl.pallas_call(
        body, name="ssd_bwd", grid=(nsteps,),
        in_specs=[pl.BlockSpec((RB, CONVC), lambda i: (rev(i), 0)), pl.BlockSpec((RB, CONVC), lambda i: (rev(i), 0)),
                  pl.BlockSpec((RB, SW), lambda i: (rev(i), 0)), pl.BlockSpec((RB, 128), lambda i: (rev(i), 0)),
                  pl.BlockSpec((RB, SW), lambda i: (rev(i), 0)), pl.BlockSpec((CPS, NST, SW), lambda i: (rev(i), 0, 0)),
                  pl.BlockSpec((RB, SW), lambda i: (rev(i), 1)),
                  full(conv_w), full(dtb), full(alog), full(dskx), full(ssm_w),
                  full(e3), full(et2), full(tril3), full(triu3), DEP_SPEC],
        out_specs=[pl.BlockSpec((RB, 1664), lambda i: (rev(i), 0)),
                   acc(CONVK, CONVC), acc(1, CONVC), acc(1, SW), acc(1, 128), acc(1, 128), acc(1, 128)],
        out_shape=[jax.ShapeDtypeStruct((T, 1664), BF16),
                   jax.ShapeDtypeStruct((CONVK, CONVC), F32), jax.ShapeDtypeStruct((1, CONVC), F32),
                   jax.ShapeDtypeStruct((1, SW), F32), jax.ShapeDtypeStruct((1, 128), F32),
                   jax.ShapeDtypeStruct((1, 128), F32), jax.ShapeDtypeStruct((1, 128), F32)],
        scratch_shapes=[pltpu.VMEM((NST, SW), F32), pltpu.VMEM((8, CONVC), F32), pltpu.VMEM((1, SW), F32)],
        compiler_params=_cp("arbitrary"),
    )(xbc, co_all, z, dtr, ypre, states, dmix, conv_w, dtb, alog, dskx, ssm_w, e3, et2, tril3, triu3, dep)


def _mix_ffn(x, attn, ynorm, tgt, mod6, norm2_w, final_w, w_out, w_gu, w_gu_own, s_arr, w_dn, tm):
    T = x.shape[0]
    nt = T // tm

    def body(x_ref, a_ref, y_ref, t_ref, mod_ref, n2_ref, fw_ref, wo_hbm, wgu_hbm, own_hbm, s_ref, wdn_hbm,
             sq_ref, dmix_ref, dx1_ref, h2_ref, act_ref, df_ref, dgu_ref, do_ref, sm_ref,
             wo, wgu, wdn, sems):
        i = pl.program_id(0)

        @pl.when(i == 0)
        def _():
            cps = [pltpu.make_async_copy(s, d, sems.at[k]) for k, (s, d) in
                   enumerate(((wo_hbm, wo), (wgu_hbm, wgu), (wdn_hbm, wdn)))]
            for c in cps:
                c.start()
            for c in cps:
                c.wait()
            own = pltpu.make_async_copy(
                own_hbm, wgu.at[:, pl.ds(pl.multiple_of(s_ref[0] * GU_SH, 128), GU_SH)], sems.at[3])
            own.start()
            own.wait()
            sq_ref[...] = jnp.zeros_like(sq_ref)
            sm_ref[...] = jnp.zeros_like(sm_ref)

        gate1, shift2, scale2, gate2 = mod_ref[2:3, :], mod_ref[3:4, :], mod_ref[4:5, :], mod_ref[5:6, :]
        n2w, fw = n2_ref[...], fw_ref[...]
        o = _dot(a_ref[...], wo[0:AW, :]) + _dot(y_ref[...], wo[AW:D, :])
        x1 = x_ref[...] + gate1 * o
        r2 = lax.rsqrt(jnp.mean(x1 * x1, axis=-1, keepdims=True) + EPS)
        xh2 = x1 * r2
        n2 = xh2 * n2w
        h2b = (n2 * (1.0 + scale2) + shift2).astype(BF16)
        h2_ref[...] = h2b
        f = jnp.zeros((tm, D), F32)
        saved = []
        for a, b in FF_SPLITS:
            gp = _dot(h2b, wgu[:, a:b])
            upj = _dot(h2b, wgu[:, DFF + a:DFF + b])
            sg = _sigmoid(gp)
            sl = gp * sg
            actb = (sl * upj).astype(BF16)
            act_ref[:, a:b] = actb
            f = f + _dot(actb, wdn[a:b, :])
            saved.append((gp, upj, sg, sl))
        x2 = x1 + gate2 * f
        r3 = lax.rsqrt(jnp.mean(x2 * x2, axis=-1, keepdims=True) + EPS)
        xh3 = x2 * r3
        err = xh3 * fw - t_ref[...]
        sq_ref[...] += jnp.sum(err * err, axis=0, keepdims=True)
        dy = err * (1.0 / D)
        dfw = jnp.sum(dy * xh3, axis=0, keepdims=True)
        dxh3 = dy * fw
        dx2 = r3 * (dxh3 - xh3 * jnp.mean(dxh3 * xh3, axis=-1, keepdims=True))
        dgate2 = jnp.sum(dx2 * f, axis=0, keepdims=True)
        dfb = (dx2 * gate2).astype(BF16)
        df_ref[...] = dfb
        dh2 = jnp.zeros((tm, D), F32)
        for (a, b), (gp, upj, sg, sl) in zip(FF_SPLITS, saved):
            dact = _dot_nt(dfb, wdn[a:b, :])
            dg = (dact * upj * (sg * (1.0 + gp * (1.0 - sg)))).astype(BF16)
            du = (dact * sl).astype(BF16)
            dgu_ref[:, a:b] = dg
            dgu_ref[:, DFF + a:DFF + b] = du
            dh2 = dh2 + _dot_nt(dg, wgu[:, a:b]) + _dot_nt(du, wgu[:, DFF + a:DFF + b])
        dshift2 = jnp.sum(dh2, axis=0, keepdims=True)
        dscale2 = jnp.sum(dh2 * n2, axis=0, keepdims=True)
        dn2 = dh2 * (1.0 + scale2)
        dn2w = jnp.sum(dn2 * xh2, axis=0, keepdims=True)
        dxh2 = dn2 * n2w
        dx1 = dx2 + r2 * (dxh2 - xh2 * jnp.mean(dxh2 * xh2, axis=-1, keepdims=True))
        dx1_ref[...] = dx1
        dgate1 = jnp.sum(dx1 * o, axis=0, keepdims=True)
        dob = (dx1 * gate1).astype(BF16)
        do_ref[...] = dob
        dmix_ref[...] = _dot_nt(dob, wo[...])
        sm_ref[...] += jnp.concatenate(
            [dfw, dn2w, dshift2, dscale2, dgate2, dgate1, jnp.zeros((2, D), F32)], axis=0)

    row = lambda w: pl.BlockSpec((tm, w), lambda i: (i, 0))
    full = lambda a: pl.BlockSpec(a.shape, lambda i: (0,) * a.ndim)
    anyspec = pl.BlockSpec(memory_space=pl.ANY)
    return pl.pallas_call(
        body, name="mix_ffn", grid=(nt,),
        in_specs=[row(D), row(AW), row(SW), row(D), full(mod6), full(norm2_w), full(final_w), anyspec, anyspec, anyspec,
                  pl.BlockSpec(memory_space=pltpu.SMEM), anyspec],
        out_specs=[pl.BlockSpec((1, D), lambda i: (0, 0)), row(D), row(D), row(D),
                   row(DFF), row(D), row(2 * DFF), row(D), pl.BlockSpec((8, D), lambda i: (0, 0))],
        out_shape=[jax.ShapeDtypeStruct((1, D), F32), jax.ShapeDtypeStruct((T, D), F32), jax.ShapeDtypeStruct((T, D), F32),
                   jax.ShapeDtypeStruct((T, D), BF16), jax.ShapeDtypeStruct((T, DFF), BF16),
                   jax.ShapeDtypeStruct((T, D), BF16), jax.ShapeDtypeStruct((T, 2 * DFF), BF16),
                   jax.ShapeDtypeStruct((T, D), BF16), jax.ShapeDtypeStruct((8, D), F32)],
        scratch_shapes=[pltpu.VMEM((D, D), BF16), pltpu.VMEM((D, 2 * DFF), BF16), pltpu.VMEM((DFF, D), BF16),
                        pltpu.SemaphoreType.DMA((4,))],
        compiler_params=_cp("arbitrary"),
    )(x, attn, ynorm, tgt, mod6, norm2_w, final_w, w_out, w_gu, w_gu_own, s_arr, w_dn)


def _in_proj_bwd(x, dx1, dqkv, dzxd, mod6, norm1_w, w_pad, tm, dep):
    T = x.shape[0]

    def body(x_ref, dx1_ref, dq_ref, dz_ref, mod_ref, nw_ref, w_hbm, dep_ref, gx_ref, sm_ref, w_vmem, sem):
        _load_resident(w_hbm, w_vmem, sem)

        @pl.when(pl.program_id(0) == 0)
        def _():
            sm_ref[...] = jnp.zeros_like(sm_ref)

        nw = nw_ref[...]
        scale1 = mod_ref[1:2, :]
        sums = jnp.zeros((8, D), F32)
        for rows in (slice(0, tm // 2), slice(tm // 2, tm)):
            dh = _dot_nt(dq_ref[rows, :], w_vmem[:, 0:768]) + _dot_nt(dz_ref[rows, :], w_vmem[:, 768:IN_PAD])
            xv = x_ref[rows, :]
            r = lax.rsqrt(jnp.mean(xv * xv, axis=-1, keepdims=True) + EPS)
            xh = xv * r
            n1 = xh * nw
            dshift = jnp.sum(dh, axis=0, keepdims=True)
            dscale = jnp.sum(dh * n1, axis=0, keepdims=True)
            dn = dh * (1.0 + scale1)
            dnw = jnp.sum(dn * xh, axis=0, keepdims=True)
            dxh = dn * nw
            gx_ref[rows, :] = dx1_ref[rows, :] + r * (dxh - xh * jnp.mean(dxh * xh, axis=-1, keepdims=True))
            sums = sums + jnp.concatenate([dnw, dshift, dscale, jnp.zeros((5, D), F32)], axis=0)
        sm_ref[...] += sums

    row = lambda w: pl.BlockSpec((tm, w), lambda i: (i, 0))
    full = lambda a: pl.BlockSpec(a.shape, lambda i: (0,) * a.ndim)
    return pl.pallas_call(
        body, name="in_proj_bwd", grid=(T // tm,),
        in_specs=[row(D), row(D), row(768), row(1664), full(mod6), full(norm1_w), pl.BlockSpec(memory_space=pl.ANY),
                  DEP_SPEC],
        out_specs=[row(D), pl.BlockSpec((8, D), lambda i: (0, 0))],
        out_shape=[jax.ShapeDtypeStruct((T, D), F32), jax.ShapeDtypeStruct((8, D), F32)],
        scratch_shapes=[pltpu.VMEM((D, IN_PAD), BF16), pltpu.SemaphoreType.DMA],
        compiler_params=_cp("arbitrary"),
    )(x, dx1, dqkv, dzxd, mod6, norm1_w, w_pad, dep)


def _tn_matmul(a, b, K, N, tt, name, dep):
    T = a.shape[0]
    ja, jb = a.shape[1] // K, b.shape[1] // N
    J = max(ja, jb)

    def body(a_ref, b_ref, dep_ref, o_ref):
        t = pl.program_id(1)
        prod = _dot_tn(a_ref[...], b_ref[...])

        @pl.when(t == 0)
        def _():
            o_ref[0] = prod

        @pl.when(t > 0)
        def _():
            o_ref[0] += prod

    return pl.pallas_call(
        body, name=name, grid=(J, T // tt),
        in_specs=[pl.BlockSpec((tt, K), lambda j, t: (t, j if ja > 1 else 0)),
                  pl.BlockSpec((tt, N), lambda j, t: (t, j if jb > 1 else 0)),
                  pl.BlockSpec((8, 128), lambda j, t: (0, 0))],
        out_specs=pl.BlockSpec((1, K, N), lambda j, t: (j, 0, 0)),
        out_shape=jax.ShapeDtypeStruct((J, K, N), F32),
        compiler_params=_cp("parallel", "arbitrary"),
    )(a, b, dep)


def _accumulate(o_ref, rows, prod):
    @pl.when(pl.program_id(0) == 0)
    def _():
        o_ref[rows, :] = prod

    @pl.when(pl.program_id(0) > 0)
    def _():
        o_ref[rows, :] += prod


def _tn_matmul_rows(a0, a1, b, tt, name, dep):
    T, K = a0.shape
    N = b.shape[1]

    def body(a0_ref, a1_ref, b_ref, dep_ref, o_ref):
        for k, a_ref in enumerate((a0_ref, a1_ref)):
            _accumulate(o_ref, slice(k * K, (k + 1) * K), _dot_tn(a_ref[...], b_ref[...]))

    tile = lambda w: pl.BlockSpec((tt, w), lambda t: (t, 0))
    return pl.pallas_call(
        body, name=name, grid=(T // tt,), in_specs=[tile(K), tile(K), tile(N), DEP_SPEC],
        out_specs=pl.BlockSpec((2 * K, N), lambda t: (0, 0)), out_shape=jax.ShapeDtypeStruct((2 * K, N), F32),
        compiler_params=_cp("arbitrary"),
    )(a0, a1, b, dep)


def _adam_math(w, g, m, v):
    m = B1 * m + (1.0 - B1) * g
    v = B2 * v + (1.0 - B2) * (g * g)
    m_hat = m / (1.0 - B1 ** STEP)
    v_hat = v / (1.0 - B2 ** STEP)
    delta = -LR * (m_hat / (jnp.sqrt(v_hat) + AEPS) + WD * w)
    return delta, m, v


def _adam_2d(w, mine, land, m, v, c_arr, rb, name, dep):
    R, C = w.shape
    nbh = R // 2 // rb

    def body(c_ref, w_ref, mine_ref, land_ref, m_ref, v_ref, dep_ref, go_ref, d_ref, mo_ref, vo_ref):
        g = jnp.where(pl.program_id(0) // nbh == c_ref[0], mine_ref[...], land_ref[...])
        d, mn, vn = _adam_math(w_ref[...], g, m_ref[...], v_ref[...])
        go_ref[...] = g
        d_ref[...] = d
        mo_ref[...] = mn
        vo_ref[...] = vn

    spec = pl.BlockSpec((rb, C), lambda i, c_ref: (i, 0))
    mine_spec = pl.BlockSpec((rb, C), lambda i, c_ref: (jnp.clip(i - c_ref[0] * nbh, 0, nbh - 1), 0))
    return pl.pallas_call(
        body, name=name,
        grid_spec=pltpu.PrefetchScalarGridSpec(
            num_scalar_prefetch=1, grid=(R // rb,), in_specs=[spec, mine_spec, spec, spec, spec, DEP_SPEC],
            out_specs=[spec] * 4),
        out_shape=[jax.ShapeDtypeStruct((R, C), F32)] * 4, compiler_params=_cp("parallel"),
    )(c_arr, w, mine, land, m, v, dep)


def _adam_w_in(w3, grad, m3, v3):
    n = w3.shape[0]

    def body(w_hbm, grad_ref, m_hbm, v_hbm, g_hbm, d_hbm, mo_hbm, vo_hbm, bufs, sems):
        ins = [pltpu.make_async_copy(src.at[:, 0], bufs.at[k], sems.at[k]) for k, src in enumerate((w_hbm, m_hbm, v_hbm))]
        for cp in ins:
            cp.start()
        g = grad_ref[...]
        eye =(_iota((D, D), 0) == _iota((D, D), 1)).astype(BF16)
        g_t = jnp.zeros((n, D), F32)
        r = g
        for i in range(3):
            p = r.astype(BF16)
            g_t = g_t + _dot_tn(p, eye)
            if i < 2:
                r = r - p.astype(F32)
        for cp in ins:
            cp.wait()
        d, mn, vn = _adam_math(bufs[0], g_t, bufs[1], bufs[2])
        for k, val in enumerate((g_t, d, mn, vn)):
            bufs[3 + k] = val
        outs = [pltpu.make_async_copy(bufs.at[3 + k], dst.at[:, 0], sems.at[3 + k])
                for k, dst in enumerate((g_hbm, d_hbm, mo_hbm, vo_hbm))]
        for cp in outs:
            cp.start()
        for cp in outs:
            cp.wait()

    anyspec = pl.BlockSpec(memory_space=pl.ANY)
    vm = pl.BlockSpec(memory_space=pltpu.VMEM)
    return pl.pallas_call(
        body, name="adam_w_in",
        in_specs=[anyspec, vm, anyspec, anyspec], out_specs=[anyspec] * 4,
        out_shape=[jax.ShapeDtypeStruct(w3.shape, F32)] * 4,
        scratch_shapes=[pltpu.VMEM((7, n, D), F32), pltpu.SemaphoreType.DMA((7,))],
        compiler_params=pltpu.CompilerParams(vmem_limit_bytes=VMEM_LIMIT),
    )(w3, grad, m3, v3)


def _adam_w_ada(gat, allv, s_arr, w, m, v, rb):
    R, C = w.shape

    def body(s_ref, c_ref, dm_ref, w_ref, m_ref, v_ref, g_ref, d_ref, mo_ref, vo_ref):
        cm = _rows_select(c_ref, rb)
        g = lax.dot_general(cm * _sigmoid(cm), _rows_select(dm_ref, C), (((0,), (0,)), ((), ())), precision=HI,
                            preferred_element_type=F32)
        d, mn, vn = _adam_math(w_ref[...], g, m_ref[...], v_ref[...])
        g_ref[...] = g
        d_ref[...] = d
        mo_ref[...] = mn
        vo_ref[...] = vn

    spec = pl.BlockSpec((rb, C), lambda i, s_ref: (i, 0))
    return pl.pallas_call(
        body, name="adam_w_ada",
        grid_spec=pltpu.PrefetchScalarGridSpec(
            num_scalar_prefetch=1, grid=(R // rb,),
            in_specs=[pl.BlockSpec((8, 1, rb), lambda i, s_ref: (0, 0, i)),
                      pl.BlockSpec((8, 1, C), lambda i, s_ref: (0, 0, s_ref[0])), spec, spec, spec],
            out_specs=[spec] * 4),
        out_shape=[jax.ShapeDtypeStruct((R, C), F32)] * 4, compiler_params=_cp("parallel"),
    )(s_arr, gat, allv, w, m, v)


def _adam_small(tot, segs, ws, ms, vs):
    k = len(ws)
    extra = [sg for sg in segs if not isinstance(sg, tuple)]
    ne = len(extra)

    def body(*refs):
        tot_ref, g_x = refs[0], list(refs[1:1 + ne])
        w, m, v = [refs[1 + ne + j * k:1 + ne + (j + 1) * k] for j in range(3)]
        g_o, d_o, m_o, v_o = [refs[1 + ne + (3 + j) * k:1 + ne + (4 + j) * k] for j in range(4)]
        for i in range(k):
            gi = tot_ref[:, segs[i][0]:segs[i][0] + segs[i][1]] if isinstance(segs[i], tuple) else g_x.pop(0)[...]
            d, mn, vn = _adam_math(w[i][...], gi, m[i][...], v[i][...])
            g_o[i][...] = gi
            d_o[i][...] = d
            m_o[i][...] = mn
            v_o[i][...] = vn

    shapes = [jax.ShapeDtypeStruct(w.shape, F32) for w in ws]
    vm = pl.BlockSpec(memory_space=pltpu.VMEM)
    outs = pl.pallas_call(
        body, name="adam_small", in_specs=[vm] * (1 + ne + 3 * k), out_specs=[vm] * (4 * k), out_shape=shapes * 4,
    )(tot, *extra, *ws, *ms, *vs)
    return outs[0:k], outs[k:2 * k], outs[2 * k:3 * k], outs[3 * k:4 * k]


def _pos():
    return lax.axis_index("x"), lax.axis_index("y"), lax.axis_index("c")


def _flip(v, bit):
    return 1 - v if bit else v


def _peer(k):
    x, y, c = _pos()
    return (_flip(x, (k >> 2) & 1), _flip(y, (k >> 1) & 1), _flip(c, k & 1))


def _logical(p):
    return 4 * p[0] + 2 * p[1] + p[2]


def _gather8(src_ref, dst_ref, send_sems, recv_sems, meanwhile):
    me = _logical(_pos())
    dst_ref[pl.ds(me, 1)] = src_ref[...][None]
    copies = []
    for k in range(1, 8):
        cp = pltpu.make_async_remote_copy(src_ref, dst_ref.at[me], send_sems.at[k - 1], recv_sems.at[k - 1],
                                          device_id=_peer(k), device_id_type=MESH)
        cp.start()
        copies.append(cp)
    meanwhile()
    for k in range(1, 8):
        pltpu.make_async_remote_copy(src_ref, dst_ref.at[_logical(_peer(k))], send_sems.at[k - 1], recv_sems.at[k - 1],
                                     device_id=_peer(k), device_id_type=MESH).wait_recv()
    for cp in copies:
        cp.wait_send()


def _rows_select(ref3, width):
    row = _iota((8, width), 0)
    out = jnp.zeros((8, width), F32)
    for i in range(8):
        out = jnp.where(row == i, ref3[i][:, 0:width], out)
    return out


def _mod_exchange(c_row, cw, w_ada_s, b_ada4, w_in3):
    n_sh = w_ada_s.shape[1]
    n_in = w_in3.shape[0]
    wide = -(-n_in // 128) * 128

    def body(c_ref, cw_ref, w_hbm, b_ref, win_hbm, gat_ref, mod_ref, token, winb_ref, pay_ref, p3, w_v, win_v, win_z,
             sa, ra, sb, rb, ls):
        token[...] = jnp.zeros_like(token)
        pay_ref[:, 0:D] = c_ref[...]
        for k in range(CONVK):
            pay_ref[:, D + 256 * k:D + 256 * (k + 1)] = cw_ref[k:k + 1, :]
        x, y, c = _pos()
        me = _logical((x, y, c))
        my_s = 2 * x + y
        load_w = pltpu.make_async_copy(w_hbm, w_v, ls.at[0])
        load_in = pltpu.make_async_copy(win_hbm.at[:, 0], win_v, ls.at[1])
        load_w.start()
        load_in.start()

        def local_work():
            win_z[...] = jnp.zeros_like(win_z)
            load_in.wait()
            win_z[0:n_in, :] = win_v[...].astype(BF16)
            eye = (_iota((wide, wide), 0) == _iota((wide, wide), 1)).astype(BF16)
            winb_ref[...] = _dot_tn(win_z[...], eye)[:, 0:n_in].astype(BF16)
            load_w.wait()

        _gather8(pay_ref, gat_ref, sa, ra, local_work)
        cmat = _rows_select(gat_ref, D)
        prod = _dot_hi(cmat * _sigmoid(cmat), w_v[...])
        for b in range(8):
            p3[b] = prod[b:b + 1, :]
        mod_ref[pl.ds(my_s, 1)] = p3[pl.ds(me, 1)] + b_ref[pl.ds(my_s, 1)]
        ks = (2, 4, 6)
        copies = []
        for i, k in enumerate(ks):
            pr = _peer(k)
            cp = pltpu.make_async_remote_copy(p3.at[_logical(pr)], mod_ref.at[my_s], sb.at[i], rb.at[i],
                                              device_id=pr, device_id_type=MESH)
            cp.start()
            copies.append(cp)
        for i, k in enumerate(ks):
            pr = _peer(k)
            s_src = 2 * pr[0] + pr[1]
            pltpu.make_async_remote_copy(p3.at[0], mod_ref.at[s_src], sb.at[i], rb.at[i],
                                         device_id=pr, device_id_type=MESH).wait_recv()
            mod_ref[pl.ds(s_src, 1)] = mod_ref[pl.ds(s_src, 1)] + b_ref[pl.ds(s_src, 1)]
        for cp in copies:
            cp.wait_send()

    vm = pl.BlockSpec(memory_space=pltpu.VMEM)
    anyspec = pl.BlockSpec(memory_space=pl.ANY)
    return pl.pallas_call(
        body, name="mod_exchange", in_specs=[vm, vm, anyspec, vm, anyspec], out_specs=[vm, vm, vm, vm],
        out_shape=[jax.ShapeDtypeStruct((8, 1, D + CONVK * 256), F32), jax.ShapeDtypeStruct((4, 1, n_sh), F32),
                   jax.ShapeDtypeStruct((8, 128), F32), jax.ShapeDtypeStruct((D, n_in), BF16)],
        scratch_shapes=[pltpu.VMEM((1, D + CONVK * 256), F32), pltpu.VMEM((8, 1, n_sh), F32), pltpu.VMEM(w_ada_s.shape, F32), pltpu.VMEM((n_in, D), F32),
                        pltpu.VMEM((wide, D), BF16), pltpu.SemaphoreType.DMA((7,)), pltpu.SemaphoreType.DMA((7,)),
                        pltpu.SemaphoreType.DMA((3,)), pltpu.SemaphoreType.DMA((3,)), pltpu.SemaphoreType.DMA((2,))],
        compiler_params=pltpu.CompilerParams(vmem_limit_bytes=VMEM_LIMIT),
    )(c_row, cw, w_ada_s, b_ada4, w_in3)


def _chips():
    x, y, _ = _pos()
    out = []
    for k in (1, 2, 3):
        px, py = _flip(x, (k >> 1) & 1), _flip(y, k & 1)
        out.append((px, py, 2 * px + py))
    return out


def _half_rows(ref, which):
    half = ref.shape[-2] // 2
    return pl.ds(pl.multiple_of(which * half, 8), half)


def _plan_small():
    def plan(refs):
        me = _logical(_pos())
        return [(refs[0], refs[1].at[me], _peer(k), refs[1].at[_logical(_peer(k))]) for k in range(1, 8)]
    return plan


def _small_sum(vec, land, me_arr):
    n = vec.shape[1]

    def body(me_ref, v_ref, land_ref, tot_ref, all_ref):
        tot = None
        for i in range(8):
            row = jnp.where(me_ref[0] == i, v_ref[...], land_ref[i])
            all_ref[i] = row
            tot = row if i == 0 else tot + row
        tot_ref[...] = tot

    return pl.pallas_call(
        body, name="small_sum",
        grid_spec=pltpu.PrefetchScalarGridSpec(
            num_scalar_prefetch=1, grid=(1,),
            in_specs=[pl.BlockSpec((1, n), lambda i, me_ref: (0, 0)), pl.BlockSpec((8, 1, n), lambda i, me_ref: (0, 0, 0))],
            out_specs=[pl.BlockSpec((1, n), lambda i, me_ref: (0, 0)),
                       pl.BlockSpec((8, 1, n), lambda i, me_ref: (0, 0, 0))]),
        out_shape=[jax.ShapeDtypeStruct((1, n), F32), jax.ShapeDtypeStruct((8, 1, n), F32)],
        compiler_params=_cp("arbitrary"),
    )(me_arr, vec, land)


def _add_half(g, sib, c_arr, rb, name):
    _, R, C = g.shape
    half = R // 2
    nb = half // rb

    def body(c_ref, g_ref, s_ref, o_ref):
        o_ref[...] = (g_ref[...] + s_ref[...]).astype(BF16)

    return pl.pallas_call(
        body, name=name,
        grid_spec=pltpu.PrefetchScalarGridSpec(
            num_scalar_prefetch=1, grid=(4, nb),
            in_specs=[pl.BlockSpec((1, rb, C), lambda s, i, c_ref: (s, c_ref[0] * nb + i, 0)),
                      pl.BlockSpec((1, rb, C), lambda s, i, c_ref: (s, i, 0))],
            out_specs=pl.BlockSpec((1, rb, C), lambda s, i, c_ref: (s, i, 0))),
        out_shape=jax.ShapeDtypeStruct((4, half, C), BF16),
        compiler_params=_cp("parallel", "parallel"),
    )(c_arr, g, sib)


def _add_half_in(gq, gz, sibq, sibz, c_arr, rb):
    half = D // 2
    nq = gq.shape[1]
    wide = -(-IN_SH // 128) * 128

    def sel(rows, first, lo):
        return (_iota((rows, wide), 0) + (first - lo) == _iota((rows, wide), 1)).astype(BF16)

    def body(c_ref, gq_ref, gz_ref, sq_ref, sz_ref, o_ref):
        q = (gq_ref[...] + sq_ref[...]).astype(BF16)
        z = (gz_ref[...] + sz_ref[...]).astype(BF16)
        for s in range(4):
            lo, hi = s * IN_SH, (s + 1) * IN_SH
            acc = jnp.zeros((rb, wide), F32)
            if lo < nq:
                a0, a1 = lo // 128 * 128, min(nq, -(-min(hi, nq) // 128) * 128)
                acc = acc + _dot(q[:, a0:a1], sel(a1 - a0, a0, lo))
            if hi > nq:
                a0, a1 = (max(lo, nq) - nq) // 128 * 128, -(-(hi - nq) // 128) * 128
                acc = acc + _dot(z[:, a0:a1], sel(a1 - a0, nq + a0, lo))
            o_ref[s] = acc[:, :IN_SH].astype(BF16)

    nb = half // rb
    mine = lambda w: pl.BlockSpec((rb, w), lambda i, c_ref: (c_ref[0] * nb + i, 0))
    sib = lambda w: pl.BlockSpec((rb, w), lambda i, c_ref: (i, 0))
    return pl.pallas_call(
        body, name="grad_add_in",
        grid_spec=pltpu.PrefetchScalarGridSpec(
            num_scalar_prefetch=1, grid=(nb,),
            in_specs=[mine(nq), mine(gz.shape[1]), sib(nq), sib(gz.shape[1])],
            out_specs=pl.BlockSpec((4, rb, IN_SH), lambda i, c_ref: (0, i, 0))),
        out_shape=jax.ShapeDtypeStruct((4, half, IN_SH), BF16),
        compiler_params=_cp("parallel"),
    )(c_arr, gq, gz, sibq, sibz)


def _sum4(parts, land, s_arr, rb, name):
    _, H, C = land.shape

    def body(s_ref, own_ref, r_ref, o_ref):
        own = own_ref[0].astype(F32)
        tot = jnp.zeros((rb, C), F32)
        for j in range(4):
            tot = tot + jnp.where(s_ref[0] == j, own, r_ref[j].astype(F32))
        o_ref[...] = tot

    return pl.pallas_call(
        body, name=name,
        grid_spec=pltpu.PrefetchScalarGridSpec(
            num_scalar_prefetch=1, grid=(H // rb,),
            in_specs=[pl.BlockSpec((1, rb, C), lambda i, s_ref: (s_ref[0], i, 0)),
                      pl.BlockSpec((4, rb, C), lambda i, s_ref: (0, i, 0))],
            out_specs=pl.BlockSpec((rb, C), lambda i, s_ref: (i, 0))),
        out_shape=jax.ShapeDtypeStruct((H, C), F32), compiler_params=_cp("parallel"),
    )(s_arr, parts, land)


HBM_SPEC = pl.BlockSpec(memory_space=pltpu.HBM)
SEM_SPEC = pl.BlockSpec(memory_space=pltpu.SEMAPHORE)
EFFECT = pltpu.SideEffectType.DATAFLOW_SIDE_EFFECTING


def _split_start(name, bufs, n_sem, plan, dep):
    nb = len(bufs)

    def body(*refs):
        ins, send, recv, token = refs[:nb], refs[nb + 1], refs[nb + 2], refs[-1]
        for i, (src, dst, dev, _) in enumerate(plan(ins)):
            pltpu.make_async_remote_copy(src, dst, send.at[i], recv.at[i], device_id=dev, device_id_type=MESH).start()
        token[...] = jnp.zeros_like(token)

    outs = pl.pallas_call(
        body, name=name,
        out_shape=(pltpu.SemaphoreType.DMA((n_sem,)), pltpu.SemaphoreType.DMA((n_sem,)),
                   *[pltpu.HBM(b.shape, b.dtype) for b in bufs], jax.ShapeDtypeStruct((8, 128), F32)),
        in_specs=[HBM_SPEC] * nb + [pl.BlockSpec(memory_space=pl.ANY)],
        out_specs=(SEM_SPEC, SEM_SPEC, *([HBM_SPEC] * nb), pl.BlockSpec(memory_space=pltpu.VMEM)),
        input_output_aliases={i: 2 + i for i in range(nb)},
        compiler_params=pltpu.CompilerParams(has_side_effects=EFFECT),
    )(*[pltpu.with_memory_space_constraint(b, pltpu.HBM) for b in bufs], dep)
    return outs[0], outs[1], list(outs[2:2 + nb]), outs[-1]


def _split_wait(name, send, recv, bufs, after, plan):
    nb = len(bufs)
    after = list(after) if isinstance(after, (list, tuple)) else [after]

    def body(*refs):
        ins, send_s, recv_s = refs[:nb], refs[nb], refs[nb + 1]
        for i, (src, dst, dev, mine) in enumerate(plan(ins)):
            pltpu.make_async_remote_copy(src, dst, send_s.at[i], recv_s.at[i], device_id=dev,
                                         device_id_type=MESH).wait_send()
            pltpu.make_async_remote_copy(src, mine, send_s.at[i], recv_s.at[i], device_id=dev,
                                         device_id_type=MESH).wait_recv()

    outs = pl.pallas_call(
        body, name=name, out_shape=[pltpu.HBM(b.shape, b.dtype) for b in bufs],
        in_specs=[HBM_SPEC] * nb + [SEM_SPEC, SEM_SPEC] + [HBM_SPEC] * len(after),
        out_specs=[HBM_SPEC] * nb, input_output_aliases={i: i for i in range(nb)},
        compiler_params=pltpu.CompilerParams(has_side_effects=EFFECT),
    )(*bufs, send, recv, *[pltpu.with_memory_space_constraint(a, pltpu.HBM) for a in after])
    return list(outs)


def _split_wait_start(name, send, recv, bufs, after, plan, bufs2, n_sem2, plan2):
    nb, nb2 = len(bufs), len(bufs2)
    after = list(after) if isinstance(after, (list, tuple)) else [after]
    n_in = nb + 2 + nb2 + len(after)

    def body(*refs):
        ins, send_s, recv_s, ins2 = refs[:nb], refs[nb], refs[nb + 1], refs[nb + 2:nb + 2 + nb2]
        send2, recv2, token = refs[n_in + nb], refs[n_in + nb + 1], refs[-1]
        for i, (src, dst, dev, mine) in enumerate(plan(ins)):
            pltpu.make_async_remote_copy(src, dst, send_s.at[i], recv_s.at[i], device_id=dev,
                                         device_id_type=MESH).wait_send()
            pltpu.make_async_remote_copy(src, mine, send_s.at[i], recv_s.at[i], device_id=dev,
                                         device_id_type=MESH).wait_recv()
        for i, (src, dst, dev, _) in enumerate(plan2(ins2)):
            pltpu.make_async_remote_copy(src, dst, send2.at[i], recv2.at[i], device_id=dev, device_id_type=MESH).start()
        token[...] = jnp.zeros_like(token)

    hbm = lambda b: pltpu.with_memory_space_constraint(b, pltpu.HBM)
    outs = pl.pallas_call(
        body, name=name,
        out_shape=(*[pltpu.HBM(b.shape, b.dtype) for b in bufs], pltpu.SemaphoreType.DMA((n_sem2,)),
                   pltpu.SemaphoreType.DMA((n_sem2,)), *[pltpu.HBM(b.shape, b.dtype) for b in bufs2],
                   jax.ShapeDtypeStruct((8, 128), F32)),
        in_specs=[HBM_SPEC] * nb + [SEM_SPEC, SEM_SPEC] + [HBM_SPEC] * (nb2 + len(after)),
        out_specs=(*([HBM_SPEC] * nb), SEM_SPEC, SEM_SPEC, *([HBM_SPEC] * nb2), pl.BlockSpec(memory_space=pltpu.VMEM)),
        input_output_aliases={**{i: i for i in range(nb)}, **{nb + 2 + j: nb + 2 + j for j in range(nb2)}},
        compiler_params=pltpu.CompilerParams(has_side_effects=EFFECT),
    )(*bufs, send, recv, *[hbm(b) for b in bufs2], *[hbm(a) for a in after])
    return list(outs[:nb]), outs[nb], outs[nb + 1], list(outs[nb + 2:nb + 2 + nb2]), outs[-1]


def _copies_now(name, bufs, n_sem, plan):
    nb = len(bufs)

    def body(*refs):
        ins, token, send, recv = refs[:nb], refs[2 * nb], refs[-2], refs[-1]
        token[...] = jnp.zeros_like(token)
        todo = plan(ins)
        for i, (src, dst, dev, _) in enumerate(todo):
            pltpu.make_async_remote_copy(src, dst, send.at[i], recv.at[i], device_id=dev, device_id_type=MESH).start()
        for i, (src, dst, dev, mine) in enumerate(todo):
            pltpu.make_async_remote_copy(src, mine, send.at[i], recv.at[i], device_id=dev, device_id_type=MESH).wait_recv()
        for i, (src, dst, dev, _) in enumerate(todo):
            pltpu.make_async_remote_copy(src, dst, send.at[i], recv.at[i], device_id=dev, device_id_type=MESH).wait_send()

    outs = pl.pallas_call(
        body, name=name,
        out_shape=[pltpu.HBM(b.shape, b.dtype) for b in bufs] + [jax.ShapeDtypeStruct((8, 128), F32)],
        in_specs=[HBM_SPEC] * nb, out_specs=[HBM_SPEC] * nb + [pl.BlockSpec(memory_space=pltpu.VMEM)],
        input_output_aliases={i: i for i in range(nb)},
        scratch_shapes=[pltpu.SemaphoreType.DMA((n_sem,)), pltpu.SemaphoreType.DMA((n_sem,))],
    )(*[pltpu.with_memory_space_constraint(b, pltpu.HBM) for b in bufs])
    return list(outs[:nb]), outs[nb]


def _slot(land, s, rows, cols):
    if cols is None:
        return land.at[s, rows]
    return land.at[rows, pl.ds(pl.multiple_of(s * cols, 128), cols)]


def _plan_gather_ici(cols):
    nw = len(cols)

    def plan(refs):
        x, y, c = _pos()
        my_s = 2 * x + y
        out = []
        for w in range(nw):
            mine = _half_rows(refs[w], c)
            for px, py, ps in _chips():
                out.append((refs[w].at[mine], _slot(refs[nw + w], my_s, mine, cols[w]), (px, py, c),
                            _slot(refs[nw + w], ps, mine, cols[w])))
        return out
    return plan


def _plan_gather_fwd(cols, rows):
    def plan(refs):
        x, y, c = _pos()
        out = []
        for w in range(len(cols)):
            half = rows[w] // 2
            mine = pl.ds(pl.multiple_of(c * half, 8), half)
            other = pl.ds(pl.multiple_of((1 - c) * half, 8), half)
            for px, py, ps in _chips():
                got = _slot(refs[w], ps, mine, cols[w])
                out.append((got, got, (x, y, 1 - c), _slot(refs[w], ps, other, cols[w])))
        return out
    return plan


def _plan_swap(nw):
    def plan(refs):
        x, y, c = _pos()
        return [(refs[w].at[:, _half_rows(refs[w], 1 - c)], refs[nw + w], (x, y, 1 - c), refs[nw + w])
                for w in range(nw)]
    return plan


def _plan_swap_rows(nw):
    def plan(refs):
        x, y, c = _pos()
        return [(refs[w].at[_half_rows(refs[w], 1 - c)], refs[nw + w], (x, y, 1 - c), refs[nw + w])
                for w in range(nw)]
    return plan


def _plan_scatter(nw):
    def plan(refs):
        x, y, c = _pos()
        my_s = 2 * x + y
        out = []
        for w in range(nw):
            for px, py, ps in _chips():
                out.append((refs[w].at[ps], refs[nw + w].at[my_s], (px, py, c), refs[nw + w].at[ps]))
        return out
    return plan


def _plan_scatter_both():
    def plan(refs):
        x, y, c = _pos()
        my_s = 2 * x + y
        src, land = refs
        out = []
        for px, py, ps in _chips():
            out.append((src.at[ps], land.at[my_s, c], (px, py, c), land.at[ps, c]))
            out.append((src.at[ps], land.at[my_s, c], (px, py, 1 - c), land.at[ps, 1 - c]))
        out.append((src.at[my_s], land.at[my_s, c], (x, y, 1 - c), land.at[my_s, 1 - c]))
        return out
    return plan


def _sum4_both(parts, land, s_arr, c_arr):
    _, _, H, C = land.shape

    def body(s_ref, c_ref, own_ref, r_ref, o_ref):
        mine = pl.program_id(0) == c_ref[0]
        own = own_ref[0].astype(F32)
        tot = jnp.zeros((H, C), F32)
        for j in range(4):
            tot = tot + jnp.where(jnp.logical_and(mine, s_ref[0] == j), own, r_ref[j, 0].astype(F32))
        o_ref[0] = tot

    return pl.pallas_call(
        body, name="grad_sum_in",
        grid_spec=pltpu.PrefetchScalarGridSpec(
            num_scalar_prefetch=2, grid=(2,),
            in_specs=[pl.BlockSpec((1, H, C), lambda h, s_ref, c_ref: (s_ref[0], 0, 0)),
                      pl.BlockSpec((4, 1, H, C), lambda h, s_ref, c_ref: (0, h, 0, 0))],
            out_specs=pl.BlockSpec((1, H, C), lambda h, s_ref, c_ref: (h, 0, 0))),
        out_shape=jax.ShapeDtypeStruct((2, H, C), F32), compiler_params=_cp("parallel"),
    )(s_arr, c_arr, parts, land).reshape(2 * H, C)


def _plan_join(nw):
    def plan(refs):
        x, y, c = _pos()
        out = []
        for w in range(nw):
            land = refs[nw + w]
            out.append((refs[w], land.at[_half_rows(land, c)], (x, y, 1 - c), land.at[_half_rows(land, 1 - c)]))
        return out
    return plan


def _hbm_empty(shape, dtype):
    return pltpu.with_memory_space_constraint(lax.empty(shape, dtype), pltpu.HBM)


def _put_slot(land, own, slot):
    return lax.dynamic_update_slice(land, own[None], (slot,) + (0,) * own.ndim)


def _w_in_assemble(land, own, s_arr, rb):
    wide = -(-IN_SH // 128) * 128
    starts = [s * IN_SH // 128 * 128 for s in range(4)]
    ends = [min(IN_PAD, -(-(s + 1) * IN_SH // 128) * 128) for s in range(4)]

    def body(s_ref, land_ref, own_ref, o_ref, parts):
        @pl.when(pl.program_id(0) == 0)
        def _():
            parts[...] = jnp.zeros_like(parts)

        acc = []
        for s in range(4):
            parts[s, :, 0:IN_SH] = jnp.where(s_ref[0] == s, own_ref[...], land_ref[s])
            w = ends[s] - starts[s]
            sel = (_iota((wide, w), 0) + (s * IN_SH - starts[s]) == _iota((wide, w), 1)).astype(BF16)
            acc.append(_dot(parts[s], sel))
        for s in range(4):
            lo = starts[s] if s == 0 else ends[s - 1]
            hi = starts[s + 1] if s < 3 else ends[s]
            o_ref[:, lo:hi] = acc[s][:, lo - starts[s]:hi - starts[s]].astype(BF16)
            if s < 3:
                a, b = starts[s + 1], ends[s]
                o_ref[:, a:b] = (acc[s][:, a - starts[s]:b - starts[s]] + acc[s + 1][:, 0:b - a]).astype(BF16)

    return pl.pallas_call(
        body, name="w_in_assemble",
        grid_spec=pltpu.PrefetchScalarGridSpec(
            num_scalar_prefetch=1, grid=(D // rb,),
            in_specs=[pl.BlockSpec((4, rb, IN_SH), lambda i, s_ref: (0, i, 0)),
                      pl.BlockSpec((rb, IN_SH), lambda i, s_ref: (i, 0))],
            out_specs=pl.BlockSpec((rb, IN_PAD), lambda i, s_ref: (i, 0)),
            scratch_shapes=[pltpu.VMEM((4, rb, wide), BF16)]),
        out_shape=jax.ShapeDtypeStruct((D, IN_PAD), BF16), compiler_params=_cp("arbitrary"),
    )(s_arr, land, own)


def _pad_lanes(a, n):
    return jnp.pad(a, ((0, 0), (0, n - a.shape[1])))


def kernel(x, c, positions, w_ada, b_ada, norm1_w, w_in, conv_w, conv_b, dt_bias, a_log, d_skip, attn_sinks, ssm_norm_w, w_out, norm2_w, w_gate_up, w_down, final_norm_w, loss_target, m_w_ada, m_b_ada, m_norm1_w, m_w_in, m_conv_w, m_conv_b, m_dt_bias, m_a_log, m_d_skip, m_attn_sinks, m_ssm_norm_w, m_w_out, m_norm2_w, m_w_gate_up, m_w_down, m_final_norm_w, v_w_ada, v_b_ada, v_norm1_w, v_w_in, v_conv_w, v_conv_b, v_dt_bias, v_a_log, v_d_skip, v_attn_sinks, v_ssm_norm_w, v_w_out, v_norm2_w, v_w_gate_up, v_w_down, v_final_norm_w):
    T = x.shape[1]
    tm = min(256, T)
    xi, yi, ci = lax.axis_index("x"), lax.axis_index("y"), lax.axis_index("c")
    my_s = 2 * xi + yi
    xs = x[0]
    tgt = loss_target[0]

    gat, mod4, tok, w_in_b = _mod_exchange(c, conv_w[0], w_ada[0], b_ada.reshape(4, 1, 1536), w_in.transpose(2, 0, 1))
    mod6 = mod4.reshape(6, D)
    cw_dev = gat[:, 0, D:].reshape(4, 2, CONVK, 256)[:, 0]
    conv_full = cw_dev.transpose(1, 0, 2).reshape(CONVK, CONVC)

    s_i, r_i, bufs, tok = _split_start("wgather_in_ici_start", [w_in_b, _hbm_empty((4,) + w_in_b.shape, BF16)], 3,
                                       _plan_gather_ici([None]), tok)
    inv_freq = (10000.0 ** (-jnp.arange(32, dtype=F32) / 32))
    cos, sin_s = _rope_tables(positions, inv_freq.reshape(32, 1), min(512, T), tok)
    late = [w_out[0].astype(BF16), w_gate_up[0].astype(BF16), w_down[0].astype(BF16)]
    lands = [_hbm_empty((4, D // 4, D), BF16), _hbm_empty((D, 2 * DFF), BF16), _hbm_empty((4, DFF // 4, D), BF16)]
    cols3, rows3 = [None, GU_SH, None], [D // 4, D, DFF // 4]
    bufs, s_a, r_a, bufs_late, tok = _split_wait_start(
        "wgather_in_ici_wait", s_i, r_i, bufs, cos, _plan_gather_ici([None]), late + lands, 9, _plan_gather_ici(cols3))
    own_in = bufs[0]
    bufs, tok = _copies_now("wgather_in_fwd", bufs[1:], 3, _plan_gather_fwd([None], [D]))
    s_arr = my_s.reshape(1).astype(jnp.int32)
    w_pad = _w_in_assemble(bufs[0], own_in, s_arr, 256)
    bufs = bufs_late

    qkv, z, xbc, dtr, h1b = _in_proj_fwd(xs, cos, sin_s, mod6, norm1_w, w_pad, min(512, T), tok)
    sinks = attn_sinks
    attn, lse = _attn_fwd(qkv, sinks)
    bufs = _split_wait("wgather_ici_wait", s_a, r_a, bufs, attn, _plan_gather_ici(cols3))
    late = bufs[:3]
    s_b, r_b, lands, tok = _split_start("wgather_fwd_start", bufs[3:], 9, _plan_gather_fwd(cols3, rows3), attn)
    dtb = _pad_lanes(dt_bias, 128)
    alog = _pad_lanes(a_log, 128)
    dskx = jnp.repeat(d_skip, HD, axis=1)
    mats = _ssd_mats()
    ynorm, ypre, states, conv_pre = _ssd_fwd(xbc, z, dtr, conv_full, conv_b, dtb, alog, dskx, ssm_norm_w, mats, tok)
    lands = _split_wait("wgather_fwd_wait", s_b, r_b, lands, ynorm, _plan_gather_fwd(cols3, rows3))
    w_out_f = _put_slot(lands[0], late[0], my_s).reshape(D, D)
    w_dn_f = _put_slot(lands[2], late[2], my_s).reshape(DFF, D)

    fw2 = final_norm_w.reshape(1, D)
    sq, dmix, dx1, h2b, act, dfb, dgu, dob, sm_ffn = _mix_ffn(
        xs, attn, ynorm, tgt, mod6, norm2_w, fw2, w_out_f, lands[1], late[1], s_arr, w_dn_f, tm)

    tt = min(2048, T)
    c_arr = ci.reshape(1).astype(jnp.int32)
    tok0 = jnp.zeros((8, 128), F32)
    gw_dn4 = _tn_matmul(act, dfb, GU_SH, D, tt, "dw_down", tok0).reshape(4, DFF // 4, D)
    gw_gu4 = _tn_matmul(h2b, dgu, D, GU_SH, tt, "dw_gate_up", tok0)
    gw_out4 = _tn_matmul_rows(attn, ynorm, dob, tt, "dw_out", tok0).reshape(4, D // 4, D)
    big1 = [gw_out4, gw_gu4, gw_dn4]
    rbs1 = [128, 512, 352]
    sib1 = [_hbm_empty((4, g.shape[1] // 2, g.shape[2]), F32) for g in big1]
    s_c, r_c, bufs, tok = _split_start("gswap_start", big1 + sib1, 3, _plan_swap(3), tok0)

    dzxd, d_cw, d_cb, d_sw, d_sk, d_dtb, d_av = _ssd_bwd(
        xbc, conv_pre, z, dtr, ypre, states, dmix, conv_full, dtb, alog, dskx, ssm_norm_w, mats, tok)
    bufs = _split_wait("gswap_wait", s_c, r_c, bufs, dzxd, _plan_swap(3))
    sums1 = [_add_half(g, s, c_arr, rb, "grad_add_%d" % i)
             for i, (g, s, rb) in enumerate(zip(bufs[:3], bufs[3:], rbs1))]
    land1 = [_hbm_empty(p.shape, BF16) for p in sums1]
    s_d, r_d, bufs, tok = _split_start("gscatter_start", sums1 + land1, 9, _plan_scatter(3), tok0)
    dqkv, d_sinks = _attn_bwd(qkv, sinks, lse, dmix, cos, sin_s, tok)
    bufs = _split_wait("gscatter_wait", s_d, r_d, bufs, dqkv, _plan_scatter(3))
    halves1 = [_sum4(p, l, s_arr, rb, "grad_sum_%d" % i)
               for i, (p, l, rb) in enumerate(zip(bufs[:3], bufs[3:], rbs1))]
    full1 = [_hbm_empty((2 * h.shape[0], h.shape[1]), F32) for h in halves1]
    s_e, r_e, bufs, tok = _split_start("gjoin_start", halves1 + full1, 3, _plan_join(3), tok0)
    gq = _tn_matmul(h1b, dqkv, D, 768, tt, "dw_in_qkv", tok)[0]
    gz = _tn_matmul(h1b, dzxd, D, IN_PAD - 768, tt, "dw_in_zxd", tok)[0]
    joined1 = _split_wait("gjoin_wait", s_e, r_e, bufs, [gq, gz], _plan_join(3))

    sibs = [_hbm_empty((D // 2, g.shape[1]), F32) for g in (gq, gz)]
    s_f, r_f, bufs, tok = _split_start("gswap_in_start", [gq, gz] + sibs, 2, _plan_swap_rows(2), tok0)
    g_dn_s, d_dn, m_dn, v_dn = _adam_2d(w_down[0], joined1[2], joined1[5], m_w_down[0], v_w_down[0], c_arr, 352,
                                        "adam_w_down", tok)
    g_gu_s, d_gu, m_gu, v_gu = _adam_2d(w_gate_up[0], joined1[1], joined1[4], m_w_gate_up[0], v_w_gate_up[0], c_arr,
                                        256, "adam_w_gate_up", tok)
    g_out_s, d_out, m_out, v_out = _adam_2d(w_out[0], joined1[0], joined1[3], m_w_out[0], v_w_out[0], c_arr, 128,
                                            "adam_w_out", tok)
    bufs = _split_wait("gswap_in_wait", s_f, r_f, bufs, [d_dn, d_gu, d_out], _plan_swap_rows(2))
    sum0 = _add_half_in(bufs[0], bufs[1], bufs[2], bufs[3], c_arr, min(256, D // 2))
    s_g, r_g, bufs, tok = _split_start("gscatter_in_start", [sum0, _hbm_empty((4, 2) + sum0.shape[1:], BF16)], 7,
                                       _plan_scatter_both(), tok0)
    grad_x, sm_in = _in_proj_bwd(xs, dx1, dqkv, dzxd, mod6, norm1_w, w_pad, min(512, T), tok)

    a_neg = -jnp.exp(alog)
    pieces = [sm_in[1:2], sm_in[2:3], sm_ffn[5:6], sm_ffn[2:3], sm_ffn[3:4], sm_ffn[4:5],
              sm_in[0:1], sm_ffn[1:2], sm_ffn[0:1], d_cb, d_cw.reshape(1, CONVK * CONVC),
              _pad_lanes(d_sw, SW), d_dtb, d_av * a_neg, d_sk, d_sinks,
              _pad_lanes((0.5 / D * jnp.sum(sq)).reshape(1, 1), 128)]
    vec = jnp.concatenate(pieces, axis=1)
    s_h, r_h, rows8, tok_small = _split_start("small_start", [vec, _hbm_empty((8,) + vec.shape, F32)], 7,
                                              _plan_small(), tok0)

    bufs = _split_wait("gscatter_in_wait", s_g, r_g, bufs, [grad_x, tok_small], _plan_scatter_both())
    gw_in_s = _sum4_both(bufs[0], bufs[1], s_arr, c_arr)
    native = lambda a: a.transpose(2, 0, 1)
    adam_in = _adam_w_in(native(w_in), gw_in_s, native(m_w_in), native(v_w_in))
    g_in_s, d_in, m_in, v_in = [a.transpose(1, 2, 0) for a in adam_in]
    rows8 = _split_wait("small_wait", s_h, r_h, rows8, [adam_in[1]], _plan_small())
    tot, allv = _small_sum(rows8[0], rows8[1], (4 * xi + 2 * yi + ci).reshape(1).astype(jnp.int32))
    o = 0
    offs = []
    for p in pieces:
        offs.append(o)
        o += p.shape[1]
    seg = lambda i, n: (offs[i], n)
    g_conv_w = lax.dynamic_slice_in_dim(
        tot[:, offs[10]:offs[10] + CONVK * CONVC].reshape(CONVK, CONVC), my_s * 256, 256, axis=1)
    loss = tot[0, offs[16]]

    small_names = ["b_ada", "norm1_w", "conv_w", "conv_b", "dt_bias", "a_log", "d_skip", "attn_sinks", "ssm_norm_w",
                   "norm2_w", "final_norm_w"]
    small_g = [(0, 6 * D), seg(6, D), g_conv_w, seg(9, D), seg(12, 8), seg(13, 8), seg(14, 8), seg(15, 8),
               seg(11, SW), seg(7, D), seg(8, D)]
    as2d = lambda a: a.reshape(-1, a.shape[-1])
    small_w = [as2d(a) for a in (b_ada, norm1_w, conv_w, conv_b, dt_bias, a_log, d_skip, attn_sinks, ssm_norm_w,
                                 norm2_w, final_norm_w)]
    small_m = [as2d(a) for a in (m_b_ada, m_norm1_w, m_conv_w, m_conv_b, m_dt_bias, m_a_log, m_d_skip, m_attn_sinks,
                                 m_ssm_norm_w, m_norm2_w, m_final_norm_w)]
    small_v = [as2d(a) for a in (v_b_ada, v_norm1_w, v_conv_w, v_conv_b, v_dt_bias, v_a_log, v_d_skip, v_attn_sinks,
                                 v_ssm_norm_w, v_norm2_w, v_final_norm_w)]
    small_g, sd, smn, svn = _adam_small(tot, small_g, small_w, small_m, small_v)
    g_ada, d_ada, m_ada, v_ada = _adam_w_ada(gat, allv, s_arr, w_ada[0], m_w_ada[0], v_w_ada[0], 256)

    order = ["w_ada", "b_ada", "norm1_w", "w_in", "conv_w", "conv_b", "dt_bias", "a_log", "d_skip", "attn_sinks",
             "ssm_norm_w", "w_out", "norm2_w", "w_gate_up", "w_down", "final_norm_w"]
    shapes = dict(w_ada=w_ada.shape, b_ada=b_ada.shape, norm1_w=norm1_w.shape, w_in=w_in.shape, conv_w=conv_w.shape,
                  conv_b=conv_b.shape, dt_bias=dt_bias.shape, a_log=a_log.shape, d_skip=d_skip.shape,
                  attn_sinks=attn_sinks.shape, ssm_norm_w=ssm_norm_w.shape, w_out=w_out.shape, norm2_w=norm2_w.shape,
                  w_gate_up=w_gate_up.shape, w_down=w_down.shape, final_norm_w=final_norm_w.shape)
    grads = dict(w_ada=g_ada, w_in=g_in_s, w_out=g_out_s, w_gate_up=g_gu_s, w_down=g_dn_s)
    deltas = dict(w_ada=d_ada, w_in=d_in, w_out=d_out, w_gate_up=d_gu, w_down=d_dn)
    new_m = dict(w_ada=m_ada, w_in=m_in, w_out=m_out, w_gate_up=m_gu, w_down=m_dn)
    new_v = dict(w_ada=v_ada, w_in=v_in, w_out=v_out, w_gate_up=v_gu, w_down=v_dn)
    for i, nme in enumerate(small_names):
        grads[nme], deltas[nme], new_m[nme], new_v[nme] = small_g[i], sd[i], smn[i], svn[i]
    outs = [loss, grad_x[None]]
    for table in (grads, deltas, new_m, new_v):
        outs += [table[nme].reshape(shapes[nme]) for nme in order]
    return tuple(outs)
```

```python
import functools
import math

import jax
import jax.numpy as jnp
from jax import lax
from jax.experimental import pallas as pl
from jax.experimental.pallas import tpu as pltpu

F32 = jnp.float32
BF16 = jnp.bfloat16
HI = lax.Precision.HIGHEST
MESH = pl.DeviceIdType.MESH

D = 1024
HD = 64
AW = 512
SW = 512
NST = 128
CONVK = 4
CONVC = 1024
BLK = 128
CPS = 4
SSD_FWD_CPS = 8
ATTN_BPS = 8
IN_PROJ = 2312
IN_PAD = 2432
IN_SH = IN_PROJ // 4
DFF = 2816
GU_SH = 1408
FF_SPLITS = ((0, 1536), (1536, 2816))
EPS = 1e-6
NEG = -1e30
LR, B1, B2, AEPS, WD, STEP = 0.001, 0.9, 0.999, 1e-08, 0.01, 10
VMEM_LIMIT = 58 * 1024 * 1024


def _cp(*sem):
    return pltpu.CompilerParams(dimension_semantics=sem or None, vmem_limit_bytes=VMEM_LIMIT)


def _dot(a, b):
    return jnp.dot(a, b, preferred_element_type=F32)


def _dot_nt(a, b):
    return lax.dot_general(a, b, (((1,), (1,)), ((), ())), preferred_element_type=F32)


def _dot_tn(a, b):
    return lax.dot_general(a, b, (((0,), (0,)), ((), ())), preferred_element_type=F32)


def _dot_hi(a, b):
    return jnp.dot(a, b, precision=HI, preferred_element_type=F32)


def _sigmoid(x):
    return 1.0 / (1.0 + jnp.exp(-x))


def _iota(shape, dim):
    return lax.broadcasted_iota(jnp.int32, shape, dim)


def _load_resident(hbm_ref, vmem_ref, sem):
    @pl.when(pl.program_id(0) == 0)
    def _():
        cp = pltpu.make_async_copy(hbm_ref, vmem_ref, sem)
        cp.start()
        cp.wait()


def _swap32(t):
    lane = _iota(t.shape, 1)
    return jnp.where((lane & 63) < 32, pltpu.roll(t, 96, 1), pltpu.roll(t, 32, 1))


def _rope_fwd(t, cos, sin_s):
    return t * cos + _swap32(t) * sin_s


def _rope_bwd(t, cos, sin_s):
    return t * cos - _swap32(t) * sin_s


DEP_SPEC = pl.BlockSpec((8, 128), lambda *_: (0, 0))


def _rope_tables(pos_row, inv_freq_col, tm, dep):
    T = pos_row.shape[1]
    lane, row = jnp.arange(128)[None, :], jnp.arange(96)[:, None]
    pick = (lane % 32) == (row % 32)
    sel_cos = pick.astype(BF16)
    sel_sin = jnp.where(pick, jnp.where(lane % 64 < 32, -1.0, 1.0), 0.0).astype(BF16)

    def body(p_ref, f_ref, sc_ref, ss_ref, dep_ref, cos_ref, sin_ref):
        ang = f_ref[...] * p_ref[...].astype(F32)
        cos_ref[...] = _dot_tn(_pieces(jnp.cos(ang), 3, 0), sc_ref[...])
        sin_ref[...] = _dot_tn(_pieces(jnp.sin(ang), 3, 0), ss_ref[...])

    full = lambda a: pl.BlockSpec(a.shape, lambda i: (0,) * a.ndim)
    return pl.pallas_call(
        body, name="rope_tables", grid=(T // tm,),
        in_specs=[pl.BlockSpec((1, tm), lambda i: (0, i)), full(inv_freq_col), full(sel_cos), full(sel_sin), DEP_SPEC],
        out_specs=[pl.BlockSpec((tm, 128), lambda i: (i, 0))] * 2,
        out_shape=[jax.ShapeDtypeStruct((T, 128), F32)] * 2,
        compiler_params=_cp("parallel"),
    )(pos_row, inv_freq_col, sel_cos, sel_sin, dep)


def _in_proj_fwd(x, cos, sin_s, mod6, norm1_w, w_pad, tm, dep):
    T = x.shape[0]

    def body(x_ref, cos_ref, sin_ref, mod_ref, nw_ref, w_hbm, dep_ref, qkv_ref, z_ref, xbc_ref, dt_ref, h_ref, w_vmem,
             sem):
        _load_resident(w_hbm, w_vmem, sem)
        xv = x_ref[...]
        r = lax.rsqrt(jnp.mean(xv * xv, axis=-1, keepdims=True) + EPS)
        h = (xv * r * nw_ref[...]) * (1.0 + mod_ref[1:2, :]) + mod_ref[0:1, :]
        hb = h.astype(BF16)
        h_ref[...] = hb
        proj = _dot(hb, w_vmem[...])
        cs, sn = cos_ref[...], sin_ref[...]
        for j in range(5):
            qkv_ref[:, 128 * j:128 * (j + 1)] = _rope_fwd(proj[:, 128 * j:128 * (j + 1)], cs, sn).astype(BF16)
        qkv_ref[:, 640:768] = proj[:, 640:768].astype(BF16)
        z_ref[...] = proj[:, 768:1280]
        xbc_ref[...] = proj[:, 1280:2304]
        dt_ref[...] = proj[:, 2304:2432]

    row = lambda w: pl.BlockSpec((tm, w), lambda i: (i, 0))
    full = lambda a: pl.BlockSpec(a.shape, lambda i: (0,) * a.ndim)
    return pl.pallas_call(
        body, name="in_proj_fwd", grid=(T // tm,),
        in_specs=[row(D), row(128), row(128), full(mod6), full(norm1_w), pl.BlockSpec(memory_space=pl.ANY), DEP_SPEC],
        out_specs=[row(768), row(512), row(1024), row(128), row(D)],
        out_shape=[jax.ShapeDtypeStruct((T, 768), BF16), jax.ShapeDtypeStruct((T, 512), F32),
                   jax.ShapeDtypeStruct((T, 1024), F32), jax.ShapeDtypeStruct((T, 128), F32),
                   jax.ShapeDtypeStruct((T, D), BF16)],
        scratch_shapes=[pltpu.VMEM((D, IN_PAD), BF16), pltpu.SemaphoreType.DMA],
        compiler_params=_cp("arbitrary"),
    )(x, cos, sin_s, mod6, norm1_w, w_pad, dep)


def _head_variants(pair, j):
    lane = _iota(pair.shape, 1)
    lo = lane < 64
    kv = j // 2
    ev = jnp.where(lo, pair, 0.0)
    od = jnp.where(lo, 0.0, pair)
    if kv == 0:
        od = pltpu.roll(od, 64, 1)
    else:
        ev = pltpu.roll(ev, 64, 1)
    return ev.astype(BF16), od.astype(BF16)


def _kv_variants(vcat):
    lane = _iota(vcat.shape, 1)
    lo = lane < 64
    v0 = jnp.where(lo, vcat, 0.0)
    v1 = jnp.where(lo, 0.0, vcat)
    out = {
        (0, 0): v0, (0, 1): pltpu.roll(v0, 64, 1),
        (1, 0): pltpu.roll(v1, 64, 1), (1, 1): v1,
    }
    return {k: v.astype(BF16) for k, v in out.items()}


def _fold_masks(n):
    upper = _iota((BLK, BLK), 1) > _iota((BLK, BLK), 0)
    return upper, upper & (n == 0)


def _attn_fwd(qkv, sinks):
    CPS = ATTN_BPS
    T = qkv.shape[0]
    nsteps = T // (CPS * BLK)

    def body(sink_ref, q_ref, kc_ref, kp_ref, vc_ref, vp_ref, o_ref, lse_ref):
        for sub in range(CPS):
            rows, before = slice(BLK * sub, BLK * (sub + 1)), slice(BLK * (sub - 1), BLK * sub)
            block(pl.program_id(0) * CPS + sub, sink_ref, q_ref.at[rows, :], kc_ref.at[rows, :],
                  kp_ref if sub == 0 else kc_ref.at[before, :], vc_ref.at[rows, :],
                  vp_ref if sub == 0 else vc_ref.at[before, :], o_ref.at[rows, :], lse_ref.at[rows, :])

    def block(n, sink_ref, q_ref, kc_ref, kp_ref, vc_ref, vp_ref, o_ref, lse_ref):
        vpv = _kv_variants(vp_ref[...].astype(F32))
        vcv = _kv_variants(vc_ref[...].astype(F32))
        q_all = jnp.concatenate(
            [v for j in range(4) for v in _head_variants(q_ref[:, 128 * j:128 * (j + 1)].astype(F32), j)], axis=0)
        s_prev = _dot_nt(q_all, kp_ref[...])
        s_cur = _dot_nt(q_all, kc_ref[...])
        upper, dead = _fold_masks(n)
        lane = _iota((BLK, 128), 1)
        lse_acc = jnp.zeros((BLK, 128), F32)
        for jj in range(4):
            acc = jnp.zeros((BLK, 128), F32)
            for par in range(2):
                h = 2 * jj + par
                rows = slice(h * BLK, (h + 1) * BLK)
                sink = sink_ref[0, h]
                s = jnp.where(dead, NEG, jnp.where(upper, s_prev[rows], s_cur[rows]) * 0.125)
                m = jnp.maximum(jnp.max(s, axis=1, keepdims=True), sink)
                p = jnp.exp(s - m)
                den = jnp.sum(p, axis=1, keepdims=True) + jnp.exp(sink - m)
                pn = p * (1.0 / den)
                acc = (acc + _dot(jnp.where(upper, pn, 0.0).astype(BF16), vpv[(jj // 2, par)])
                       + _dot(jnp.where(upper, 0.0, pn).astype(BF16), vcv[(jj // 2, par)]))
                lse_acc = jnp.where(lane == h, m + jnp.log(den), lse_acc)
            o_ref[:, 128 * jj:128 * (jj + 1)] = acc.astype(BF16)
        lse_ref[...] = lse_acc

    RB = CPS * BLK
    prev = lambda n: jnp.maximum(n * CPS - 1, 0)
    return pl.pallas_call(
        body, name="attn_fwd", grid=(nsteps,),
        in_specs=[pl.BlockSpec(memory_space=pltpu.SMEM),
                  pl.BlockSpec((RB, 512), lambda n: (n, 0)),
                  pl.BlockSpec((RB, 128), lambda n: (n, 4)),
                  pl.BlockSpec((BLK, 128), lambda n: (prev(n), 4)),
                  pl.BlockSpec((RB, 128), lambda n: (n, 5)),
                  pl.BlockSpec((BLK, 128), lambda n: (prev(n), 5))],
        out_specs=[pl.BlockSpec((RB, 512), lambda n: (n, 0)), pl.BlockSpec((RB, 128), lambda n: (n, 0))],
        out_shape=[jax.ShapeDtypeStruct((T, 512), BF16), jax.ShapeDtypeStruct((T, 128), F32)],
        compiler_params=_cp("parallel"),
    )(sinks, qkv, qkv, qkv, qkv, qkv)


def _attn_bwd(qkv, sinks, lse, dmix, cos, sin_s, dep):
    T = qkv.shape[0]
    nb = T // BLK

    def body(sink_ref, q_ref, kc_ref, kp_ref, vc_ref, vp_ref, lse_ref, do_ref, cq_ref, sq_ref, ck_ref, sk_ref,
             dep_ref, out_ref, ds_ref, dq_car, dk_car, dv_car):
        n = pl.program_id(0)
        lane = _iota((BLK, 128), 1)

        @pl.when(n == 0)
        def _():
            ds_ref[...] = jnp.zeros_like(ds_ref)
            dq_car[...] = jnp.zeros_like(dq_car)
            dk_car[...] = jnp.zeros_like(dk_car)
            dv_car[...] = jnp.zeros_like(dv_car)

        @pl.when(n < nb)
        def _():
            kp, kc, vp, vc = kp_ref[...], kc_ref[...], vp_ref[...], vc_ref[...]
            kpv = _kv_variants(kp.astype(F32))
            kcv = _kv_variants(kc.astype(F32))
            lse_v = lse_ref[...]
            q_all = jnp.concatenate(
                [v for j in range(4) for v in _head_variants(q_ref[:, 128 * j:128 * (j + 1)].astype(F32), j)], axis=0)
            do_all = jnp.concatenate(
                [v for j in range(4) for v in _head_variants(do_ref[:, 128 * j:128 * (j + 1)], j)], axis=0)
            s_prev, s_cur = _dot_nt(q_all, kp), _dot_nt(q_all, kc)
            dp_prev, dp_cur = _dot_nt(do_all, vp), _dot_nt(do_all, vc)
            upper, dead = _fold_masks(n)
            out_ref[:, 0:512] = dq_car[...]
            dsk = jnp.zeros((1, 128), F32)
            ds_u, ds_l, p_u, p_l = [], [], [], []
            for jj in range(4):
                dq_acc = jnp.zeros((BLK, 128), F32)
                for par in range(2):
                    h = 2 * jj + par
                    rows = slice(h * BLK, (h + 1) * BLK)
                    lse_h = jnp.sum(jnp.where(lane == h, lse_v, 0.0), axis=1, keepdims=True)
                    s = jnp.where(dead, NEG, jnp.where(upper, s_prev[rows], s_cur[rows]) * 0.125)
                    p = jnp.exp(s - lse_h)
                    dp = jnp.where(upper, dp_prev[rows], dp_cur[rows])
                    delta = jnp.sum(p * dp, axis=1, keepdims=True)
                    ds = p * (dp - delta) * 0.125
                    dsu, dsl = jnp.where(upper, ds, 0.0).astype(BF16), jnp.where(upper, 0.0, ds).astype(BF16)
                    dq_acc = dq_acc + _dot(dsu, kpv[(jj // 2, par)]) + _dot(dsl, kcv[(jj // 2, par)])
                    ds_u.append(dsu)
                    ds_l.append(dsl)
                    p_u.append(jnp.where(upper, p, 0.0).astype(BF16))
                    p_l.append(jnp.where(upper, 0.0, p).astype(BF16))
                    dsk = dsk + jnp.where(lane[0:1] == h, -jnp.sum(jnp.exp(sink_ref[0, h] - lse_h) * delta), 0.0)
                dq_car[:, 128 * jj:128 * (jj + 1)] = _rope_bwd(dq_acc, cq_ref[...], sq_ref[...]).astype(BF16)
            stack = lambda parts: jnp.concatenate(parts, axis=0)
            dk_prev, dk_cur = _dot_tn(stack(ds_u), q_all), _dot_tn(stack(ds_l), q_all)
            dv_prev, dv_cur = _dot_tn(stack(p_u), do_all), _dot_tn(stack(p_l), do_all)
            ds_ref[...] += dsk
            out_ref[:, 512:640] = _rope_bwd(dk_car[...] + dk_prev, ck_ref[...], sk_ref[...]).astype(BF16)
            out_ref[:, 640:768] = (dv_car[...] + dv_prev).astype(BF16)
            dk_car[...] = dk_cur
            dv_car[...] = dv_cur

        @pl.when(n == nb)
        def _():
            out_ref[:, 0:512] = dq_car[...]
            out_ref[:, 512:640] = _rope_bwd(dk_car[...], ck_ref[...], sk_ref[...]).astype(BF16)
            out_ref[:, 640:768] = dv_car[...].astype(BF16)

    cur = lambda n: jnp.minimum(n, nb - 1)
    prev = lambda n: jnp.maximum(cur(n) - 1, 0)
    outb = lambda n: jnp.maximum(n - 1, 0)
    return pl.pallas_call(
        body, name="attn_bwd", grid=(nb + 1,),
        in_specs=[pl.BlockSpec(memory_space=pltpu.SMEM),
                  pl.BlockSpec((BLK, 512), lambda n: (cur(n), 0)),
                  pl.BlockSpec((BLK, 128), lambda n: (cur(n), 4)),
                  pl.BlockSpec((BLK, 128), lambda n: (prev(n), 4)),
                  pl.BlockSpec((BLK, 128), lambda n: (cur(n), 5)),
                  pl.BlockSpec((BLK, 128), lambda n: (prev(n), 5)),
                  pl.BlockSpec((BLK, 128), lambda n: (cur(n), 0)),
                  pl.BlockSpec((BLK, 512), lambda n: (cur(n), 0)),
                  pl.BlockSpec((BLK, 128), lambda n: (cur(n), 0)),
                  pl.BlockSpec((BLK, 128), lambda n: (cur(n), 0)),
                  pl.BlockSpec((BLK, 128), lambda n: (outb(n), 0)),
                  pl.BlockSpec((BLK, 128), lambda n: (outb(n), 0)), DEP_SPEC],
        out_specs=[pl.BlockSpec((BLK, 768), lambda n: (outb(n), 0)), pl.BlockSpec((1, 128), lambda n: (0, 0))],
        out_shape=[jax.ShapeDtypeStruct((T, 768), BF16), jax.ShapeDtypeStruct((1, 128), F32)],
        scratch_shapes=[pltpu.VMEM((BLK, 512), BF16), pltpu.VMEM((BLK, 128), F32), pltpu.VMEM((BLK, 128), F32)],
        compiler_params=_cp("arbitrary"),
    )(sinks, qkv, qkv, qkv, qkv, qkv, lse, dmix, cos, sin_s, cos, sin_s, dep)


def _ssd_mats():
    e = jnp.arange(SW)[None, :] // HD == jnp.arange(128)[:, None]
    tri = jnp.arange(BLK)[None, :] <= jnp.arange(BLK)[:, None]
    return (jnp.tile(e, (3, 1)).astype(BF16), jnp.tile(e.T, (2, 1)).astype(BF16),
            jnp.tile(tri, (1, 3)).astype(BF16), jnp.tile(tri.T, (1, 3)).astype(BF16))


def _pieces(x, n, axis):
    out, r = [], x
    for i in range(n):
        p = r.astype(BF16)
        out.append(p)
        if i + 1 < n:
            r = r - p.astype(F32)
    return jnp.concatenate(out, axis=axis)


def _expand(x, e3):
    return _dot(_pieces(x, 3, 1), e3)


def _head_sums(x, et2):
    return _dot(_pieces(x, 2, 1), et2)


def _run_sum(tri3, x):
    return _dot(tri3, _pieces(x, 3, 0))


def _shift_down(u, tail, j):
    rolled = pltpu.roll(u, j, 0)
    first = jnp.where(_iota(tail.shape, 0) < j, pltpu.roll(tail, j, 0), rolled[0:8])
    return jnp.concatenate([first, rolled[8:]], axis=0)


def _shift_up(d, head, j):
    rolled = pltpu.roll(d, BLK - j, 0)
    last = jnp.where(_iota(head.shape, 0) >= 8 - j, pltpu.roll(head, 8 - j, 0), rolled[BLK - 8:])
    return jnp.concatenate([rolled[:BLK - 8], last], axis=0)


def _ssd_parts(dtr, dtb, alog, e3, tril3):
    xx = dtr + dtb
    dt = jnp.maximum(xx, 0.0) + jnp.log(1.0 + jnp.exp(-jnp.abs(xx)))
    a_neg = -jnp.exp(alog)
    tril = _iota((BLK, BLK), 1) <= _iota((BLK, BLK), 0)
    cs = _run_sum(tril3, dt * a_neg)
    csx = _expand(cs, e3)
    last = csx[BLK - 1:BLK, :]
    return dict(xx=xx, dt=dt, a_neg=a_neg, tril=tril, cs=cs, cs_t=cs.T,
                ecsx=jnp.exp(csx), dtex=jnp.exp(last - csx), cdx=jnp.exp(last), dtx=_expand(dt, e3))


def _decay(parts, h):
    seg = parts["cs"][:, h:h + 1] - parts["cs_t"][h:h + 1, :]
    return jnp.exp(jnp.where(parts["tril"], seg, NEG))


def _group_cols(a, g):
    return a[:, 256 * g:256 * (g + 1)]


def _ssd_fwd(xbc, z, dtr, conv_w, conv_b, dtb, alog, dskx, ssm_w, mats, dep):
    CPS = SSD_FWD_CPS
    T = xbc.shape[0]
    nc = T // BLK

    def body(u_ref, tail_ref, z_ref, dtr_ref, cw_ref, cb_ref, dtb_ref, al_ref, dk_ref, sw_ref, e3_ref, tril3_ref,
             dep_ref, yn_ref, yp_ref, st_ref, co_ref, s_scr):
        n = pl.program_id(0)

        @pl.when(n == 0)
        def _():
            s_scr[...] = jnp.zeros_like(s_scr)

        lane = _iota((BLK, 128), 1)
        lo = lane < 64
        for sub in range(CPS):
            rows = slice(BLK * sub, BLK * (sub + 1))
            u = u_ref[rows, :]
            tail = jnp.where(n > 0, tail_ref[...], 0.0) if sub == 0 else u_ref[BLK * sub - 8:BLK * sub, :]
            co = cb_ref[...] + cw_ref[3:4, :] * u
            for j in range(1, CONVK):
                co = co + cw_ref[3 - j:4 - j, :] * _shift_down(u, tail, j)
            co_ref[rows, :] = co
            xc = co * _sigmoid(co)
            pt = _ssd_parts(dtr_ref[rows, :], dtb_ref[...], al_ref[...], e3_ref[...], tril3_ref[...])
            xs = xc[:, :SW]
            bm = [xc[:, 512:640].astype(BF16), xc[:, 640:768].astype(BF16)]
            cm = [xc[:, 768:896].astype(BF16), xc[:, 896:1024].astype(BF16)]
            s_in = s_scr[...]
            st_ref[sub] = s_in
            xdt = xs * pt["dtx"]
            xde = (xdt * pt["dtex"]).astype(BF16)
            ys, s_new = [], []
            for g in range(2):
                cb = _dot_nt(cm[g], bm[g])
                yoff = _dot(cm[g], _group_cols(s_in, g).astype(BF16))
                s_new.append(_dot_tn(bm[g], _group_cols(xde, g)))
                for jj in range(2):
                    j = 2 * g + jj
                    chunk = xdt[:, 128 * j:128 * (j + 1)]
                    g_ev = (cb * _decay(pt, 2 * j)).astype(BF16)
                    g_od = (cb * _decay(pt, 2 * j + 1)).astype(BF16)
                    yd = (_dot(g_ev, jnp.where(lo, chunk, 0.0).astype(BF16))
                          + _dot(g_od, jnp.where(lo, 0.0, chunk).astype(BF16)))
                    ys.append(yd + yoff[:, 128 * jj:128 * (jj + 1)] * pt["ecsx"][:, 128 * j:128 * (j + 1)])
            y = jnp.concatenate(ys, axis=1) + xs * dk_ref[...]
            s_scr[...] = s_in * pt["cdx"] + jnp.concatenate(s_new, axis=1)
            yp_ref[rows, :] = y
            zv = z_ref[rows, :]
            yz = y * (zv * _sigmoid(zv))
            outs = []
            for g in range(2):
                yg = _group_cols(yz, g)
                outs.append(yg * lax.rsqrt(jnp.mean(yg * yg, axis=-1, keepdims=True) + EPS))
            yn_ref[rows, :] = (jnp.concatenate(outs, axis=1) * sw_ref[...]).astype(BF16)

    e3, _, tril3, _ = mats
    RB = CPS * BLK
    tail8 = lambda n: jnp.maximum(n * (RB // 8) - 1, 0)
    full = lambda a: pl.BlockSpec(a.shape, lambda n: (0,) * a.ndim)
    return pl.pallas_call(
        body, name="ssd_fwd", grid=(nc // CPS,),
        in_specs=[pl.BlockSpec((RB, CONVC), lambda n: (n, 0)), pl.BlockSpec((8, CONVC), lambda n: (tail8(n), 0)),
                  pl.BlockSpec((RB, SW), lambda n: (n, 0)), pl.BlockSpec((RB, 128), lambda n: (n, 0)),
                  full(conv_w), full(conv_b), full(dtb), full(alog), full(dskx), full(ssm_w), full(e3), full(tril3),
                  DEP_SPEC],
        out_specs=[pl.BlockSpec((RB, SW), lambda n: (n, 0)), pl.BlockSpec((RB, SW), lambda n: (n, 0)),
                   pl.BlockSpec((CPS, NST, SW), lambda n: (n, 0, 0)), pl.BlockSpec((RB, CONVC), lambda n: (n, 0))],
        out_shape=[jax.ShapeDtypeStruct((T, SW), BF16), jax.ShapeDtypeStruct((T, SW), F32),
                   jax.ShapeDtypeStruct((nc, NST, SW), F32), jax.ShapeDtypeStruct((T, CONVC), F32)],
        scratch_shapes=[pltpu.VMEM((NST, SW), F32)],
        compiler_params=_cp("arbitrary"),
    )(xbc, xbc, z, dtr, conv_w, conv_b, dtb, alog, dskx, ssm_w, e3, tril3, dep)


def _ssd_bwd(xbc, co_all, z, dtr, ypre, states, dmix, conv_w, dtb, alog, dskx, ssm_w, mats, dep):
    T = xbc.shape[0]
    nsteps = T // (CPS * BLK)

    def body(*refs):
        per_chunk, consts, out_ref, carried = refs[:7], refs[7:16], refs[17], refs[18:]
        i = pl.program_id(0)

        @pl.when(i == 0)
        def _():
            for r in carried:
                r[...] = jnp.zeros_like(r)

        for sub in reversed(range(CPS)):
            rows = slice(BLK * sub, BLK * (sub + 1))
            views = [r.at[sub:sub + 1] if k == 5 else r.at[rows, :] for k, r in enumerate(per_chunk)]
            chunk(*views, *consts, out_ref.at[rows, :], *carried)

        @pl.when(i == nsteps - 1)
        def _():
            dsk_ref, dskx_scr = carried[3], carried[8]
            dsk_ref[...] = _head_sums(jnp.broadcast_to(dskx_scr[...], (8, SW)), consts[6][...])[0:1]

    def chunk(u_ref, co_ref, z_ref, dtr_ref, yp_ref, st_ref, dyn_ref, cw_ref, dtb_ref, al_ref, dk_ref, sw_ref,
              e3_ref, et2_ref, tril3_ref, triu3_ref,
              out_ref, dcw_ref, dcb_ref, dsw_ref, dsk_ref, ddtb_ref, dav_ref, ds_scr, dco_scr, dskx_scr):
        co = co_ref[...]
        sg = _sigmoid(co)
        xc = co * sg
        pt = _ssd_parts(dtr_ref[...], dtb_ref[...], al_ref[...], e3_ref[...], tril3_ref[...])
        dtx, ecsx, dtex, cdx = pt["dtx"], pt["ecsx"], pt["dtex"], pt["cdx"]
        xs = xc[:, :SW]
        bm = [xc[:, 512:640].astype(BF16), xc[:, 640:768].astype(BF16)]
        cm = [xc[:, 768:896].astype(BF16), xc[:, 896:1024].astype(BF16)]
        s_in = st_ref[0]
        ds_out = ds_scr[...]
        e_t = et2_ref[...]

        zv = z_ref[...]
        sz = _sigmoid(zv)
        silu_z = zv * sz
        ypre = yp_ref[...]
        yz = ypre * silu_z
        dyn = dyn_ref[...]
        sw = sw_ref[...]
        dyz, yns = [], []
        for g in range(2):
            yg = _group_cols(yz, g)
            r = lax.rsqrt(jnp.mean(yg * yg, axis=-1, keepdims=True) + EPS)
            yn = yg * r
            dg = _group_cols(dyn, g) * _group_cols(sw, g)
            dyz.append(r * (dg - yn * jnp.mean(dg * yn, axis=-1, keepdims=True)))
            yns.append(yn)
        dyz = jnp.concatenate(dyz, axis=1)
        dsw_ref[...] += jnp.sum(dyn * jnp.concatenate(yns, axis=1), axis=0, keepdims=True)
        dy = dyz * silu_z
        dz = dyz * ypre * (sz * (1.0 + zv * (1.0 - sz)))

        xdt = xs * dtx
        xdt_b = xdt.astype(BF16)
        edy = (ecsx * dy).astype(BF16)
        xde = (xdt * dtex).astype(BF16)
        lane = _iota((BLK, 128), 1)
        lo = lane < 64
        row8 = _iota((8, 128), 0)
        dcs = jnp.zeros((BLK, 128), F32)
        col_rows = jnp.zeros((8, 128), F32)
        dxdt, bds, yoff, dbs, dcs_g, ds_new = [], [], [], [], [], []
        for g in range(2):
            s_g = _group_cols(s_in, g).astype(BF16)
            dso_g = _group_cols(ds_out, g).astype(BF16)
            cb = _dot_nt(cm[g], bm[g])
            bds.append(_dot(bm[g], dso_g))
            yoff.append(_dot(cm[g], s_g))
            dcb_g = jnp.zeros((BLK, BLK), F32)
            for jj in range(2):
                j = 2 * g + jj
                dy_c = dy[:, 128 * j:128 * (j + 1)]
                xdt_c = xdt_b[:, 128 * j:128 * (j + 1)]
                acc = jnp.zeros((BLK, 128), F32)
                for par in range(2):
                    h = 2 * j + par
                    lm = _decay(pt, h)
                    gm = cb * lm
                    dy_m = (jnp.where(lo, dy_c, 0.0) if par == 0 else jnp.where(lo, 0.0, dy_c)).astype(BF16)
                    dg_h = _dot_nt(dy_m, xdt_c)
                    w_h = dg_h * gm
                    dcs = dcs + jnp.where(lane == h, jnp.sum(w_h, axis=1, keepdims=True), 0.0)
                    col_rows = col_rows + jnp.where(row8 == h, jnp.sum(w_h, axis=0, keepdims=True), 0.0)
                    dcb_g = dcb_g + dg_h * lm
                    acc = acc + _dot_tn(gm.astype(BF16), dy_m)
                dxdt.append(acc)
            dcb_b = dcb_g.astype(BF16)
            dcs_g.append(_dot(dcb_b, bm[g]) + _dot_nt(_group_cols(edy, g), s_g))
            dbs.append(_dot_tn(dcb_b, cm[g]) + _dot_nt(_group_cols(xde, g), dso_g))
            ds_new.append(_dot_tn(cm[g], _group_cols(edy, g)))
        bds = jnp.concatenate(bds, axis=1)
        yoff = jnp.concatenate(yoff, axis=1) * ecsx
        dxdt = jnp.concatenate(dxdt, axis=1) + dtex * bds
        ds_scr[...] = cdx * ds_out + jnp.concatenate(ds_new, axis=1)

        t_m = _head_sums(dtex * xdt * bds, e_t)
        colsum_t = jnp.concatenate([col_rows, jnp.zeros((BLK - 8, 128), F32)], axis=0).T
        cd = jnp.exp(pt["cs"][BLK - 1:BLK, :])
        sds = jnp.sum(s_in * ds_out, axis=0, keepdims=True)
        last_row = jnp.sum(t_m, axis=0, keepdims=True) + cd * _head_sums(jnp.broadcast_to(sds, (8, SW)), e_t)[0:1]
        dcs = dcs - colsum_t + _head_sums(dy * yoff, e_t) - t_m
        dcs = dcs + jnp.where(_iota((BLK, 128), 0) == BLK - 1, last_row, 0.0)
        da = _run_sum(triu3_ref[...], dcs)
        dt = pt["dt"]
        ddt = da * pt["a_neg"] + _head_sums(dxdt * xs, e_t)
        dav_ref[...] += jnp.sum(da * dt, axis=0, keepdims=True)
        ddtr = ddt * _sigmoid(pt["xx"])
        ddtb_ref[...] += jnp.sum(ddtr, axis=0, keepdims=True)
        dxs = dxdt * dtx + dy * dk_ref[...]
        dskx_scr[...] += jnp.sum(dy * xs, axis=0, keepdims=True)
        dxc = jnp.concatenate([dxs, dbs[0], dbs[1], dcs_g[0], dcs_g[1]], axis=1)
        dco = dxc * (sg * (1.0 + co * (1.0 - sg)))

        dcb_ref[...] += jnp.sum(dco, axis=0, keepdims=True)
        u = u_ref[...]
        head = dco_scr[...]
        du = jnp.zeros_like(dco)
        for j in range(CONVK):
            up_j = dco if j == 0 else _shift_up(dco, head, j)
            dcw_ref[3 - j:4 - j, :] += jnp.sum(up_j * u, axis=0, keepdims=True)
            du = du + cw_ref[3 - j:4 - j, :] * up_j
        dco_scr[...] = dco[0:8]
        out_ref[:, 0:512] = dz.astype(BF16)
        out_ref[:, 512:1536] = du.astype(BF16)
        out_ref[:, 1536:1664] = ddtr.astype(BF16)

    e3, et2, tril3, triu3 = mats
    RB = CPS * BLK
    rev = lambda i: nsteps - 1 - i
    full = lambda a: pl.BlockSpec(a.shape, lambda i: (0,) * a.ndim)
    acc = lambda r, c: pl.BlockSpec((r, c), lambda i: (0, 0))
    return pl.pallas_call(
        body, name="ssd_bwd", grid=(nsteps,),
        in_specs=[pl.BlockSpec((RB, CONVC), lambda i: (rev(i), 0)), pl.BlockSpec((RB, CONVC), lambda i: (rev(i), 0)),
                  pl.BlockSpec((RB, SW), lambda i: (rev(i), 0)), pl.BlockSpec((RB, 128), lambda i: (rev(i), 0)),
                  pl.BlockSpec((RB, SW), lambda i: (rev(i), 0)), pl.BlockSpec((CPS, NST, SW), lambda i: (rev(i), 0, 0)),
                  pl.BlockSpec((RB, SW), lambda i: (rev(i), 1)),
                  full(conv_w), full(dtb), full(alog), full(dskx), full(ssm_w),
                  full(e3), full(et2), full(tril3), full(triu3), DEP_SPEC],
        out_specs=[pl.BlockSpec((RB, 1664), lambda i: (rev(i), 0)),
                   acc(CONVK, CONVC), acc(1, CONVC), acc(1, SW), acc(1, 128), acc(1, 128), acc(1, 128)],
        out_shape=[jax.ShapeDtypeStruct((T, 1664), BF16),
                   jax.ShapeDtypeStruct((CONVK, CONVC), F32), jax.ShapeDtypeStruct((1, CONVC), F32),
                   jax.ShapeDtypeStruct((1, SW), F32), jax.ShapeDtypeStruct((1, 128), F32),
                   jax.ShapeDtypeStruct((1, 128), F32), jax.ShapeDtypeStruct((1, 128), F32)],
        scratch_shapes=[pltpu.VMEM((NST, SW), F32), pltpu.VMEM((8, CONVC), F32), pltpu.VMEM((1, SW), F32)],
        compiler_params=_cp("arbitrary"),
    )(xbc, co_all, z, dtr, ypre, states, dmix, conv_w, dtb, alog, dskx, ssm_w, e3, et2, tril3, triu3, dep)


def _mix_ffn(x, attn, ynorm, tgt, mod6, norm2_w, final_w, w_out, w_gu, w_gu_own, s_arr, w_dn, tm):
    T = x.shape[0]
    nt = T // tm

    def body(x_ref, a_ref, y_ref, t_ref, mod_ref, n2_ref, fw_ref, wo_hbm, wgu_hbm, own_hbm, s_ref, wdn_hbm,
             sq_ref, dmix_ref, dx1_ref, h2_ref, act_ref, df_ref, dgu_ref, do_ref, sm_ref,
             wo, wgu, wdn, sems):
        i = pl.program_id(0)

        @pl.when(i == 0)
        def _():
            cps = [pltpu.make_async_copy(s, d, sems.at[k]) for k, (s, d) in
                   enumerate(((wo_hbm, wo), (wgu_hbm, wgu), (wdn_hbm, wdn)))]
            for c in cps:
                c.start()
            for c in cps:
                c.wait()
            own = pltpu.make_async_copy(
                own_hbm, wgu.at[:, pl.ds(pl.multiple_of(s_ref[0] * GU_SH, 128), GU_SH)], sems.at[3])
            own.start()
            own.wait()
            sq_ref[...] = jnp.zeros_like(sq_ref)
            sm_ref[...] = jnp.zeros_like(sm_ref)

        gate1, shift2, scale2, gate2 = mod_ref[2:3, :], mod_ref[3:4, :], mod_ref[4:5, :], mod_ref[5:6, :]
        n2w, fw = n2_ref[...], fw_ref[...]
        o = _dot(a_ref[...], wo[0:AW, :]) + _dot(y_ref[...], wo[AW:D, :])
        x1 = x_ref[...] + gate1 * o
        r2 = lax.rsqrt(jnp.mean(x1 * x1, axis=-1, keepdims=True) + EPS)
        xh2 = x1 * r2
        n2 = xh2 * n2w
        h2b = (n2 * (1.0 + scale2) + shift2).astype(BF16)
        h2_ref[...] = h2b
        f = jnp.zeros((tm, D), F32)
        saved = []
        for a, b in FF_SPLITS:
            gp = _dot(h2b, wgu[:, a:b])
            upj = _dot(h2b, wgu[:, DFF + a:DFF + b])
            sg = _sigmoid(gp)
            sl = gp * sg
            actb = (sl * upj).astype(BF16)
            act_ref[:, a:b] = actb
            f = f + _dot(actb, wdn[a:b, :])
            saved.append((gp, upj, sg, sl))
        x2 = x1 + gate2 * f
        r3 = lax.rsqrt(jnp.mean(x2 * x2, axis=-1, keepdims=True) + EPS)
        xh3 = x2 * r3
        err = xh3 * fw - t_ref[...]
        sq_ref[...] += jnp.sum(err * err, axis=0, keepdims=True)
        dy = err * (1.0 / D)
        dfw = jnp.sum(dy * xh3, axis=0, keepdims=True)
        dxh3 = dy * fw
        dx2 = r3 * (dxh3 - xh3 * jnp.mean(dxh3 * xh3, axis=-1, keepdims=True))
        dgate2 = jnp.sum(dx2 * f, axis=0, keepdims=True)
        dfb = (dx2 * gate2).astype(BF16)
        df_ref[...] = dfb
        dh2 = jnp.zeros((tm, D), F32)
        for (a, b), (gp, upj, sg, sl) in zip(FF_SPLITS, saved):
            dact = _dot_nt(dfb, wdn[a:b, :])
            dg = (dact * upj * (sg * (1.0 + gp * (1.0 - sg)))).astype(BF16)
            du = (dact * sl).astype(BF16)
            dgu_ref[:, a:b] = dg
            dgu_ref[:, DFF + a:DFF + b] = du
            dh2 = dh2 + _dot_nt(dg, wgu[:, a:b]) + _dot_nt(du, wgu[:, DFF + a:DFF + b])
        dshift2 = jnp.sum(dh2, axis=0, keepdims=True)
        dscale2 = jnp.sum(dh2 * n2, axis=0, keepdims=True)
        dn2 = dh2 * (1.0 + scale2)
        dn2w = jnp.sum(dn2 * xh2, axis=0, keepdims=True)
        dxh2 = dn2 * n2w
        dx1 = dx2 + r2 * (dxh2 - xh2 * jnp.mean(dxh2 * xh2, axis=-1, keepdims=True))
        dx1_ref[...] = dx1
        dgate1 = jnp.sum(dx1 * o, axis=0, keepdims=True)
        dob = (dx1 * gate1).astype(BF16)
        do_ref[...] = dob
        dmix_ref[...] = _dot_nt(dob, wo[...])
        sm_ref[...] += jnp.concatenate(
            [dfw, dn2w, dshift2, dscale2, dgate2, dgate1, jnp.zeros((2, D), F32)], axis=0)

    row = lambda w: pl.BlockSpec((tm, w), lambda i: (i, 0))
    full = lambda a: pl.BlockSpec(a.shape, lambda i: (0,) * a.ndim)
    anyspec = pl.BlockSpec(memory_space=pl.ANY)
    return pl.pallas_call(
        body, name="mix_ffn", grid=(nt,),
        in_specs=[row(D), row(AW), row(SW), row(D), full(mod6), full(norm2_w), full(final_w), anyspec, anyspec, anyspec,
                  pl.BlockSpec(memory_space=pltpu.SMEM), anyspec],
        out_specs=[pl.BlockSpec((1, D), lambda i: (0, 0)), row(D), row(D), row(D),
                   row(DFF), row(D), row(2 * DFF), row(D), pl.BlockSpec((8, D), lambda i: (0, 0))],
        out_shape=[jax.ShapeDtypeStruct((1, D), F32), jax.ShapeDtypeStruct((T, D), F32), jax.ShapeDtypeStruct((T, D), F32),
                   jax.ShapeDtypeStruct((T, D), BF16), jax.ShapeDtypeStruct((T, DFF), BF16),
                   jax.ShapeDtypeStruct((T, D), BF16), jax.ShapeDtypeStruct((T, 2 * DFF), BF16),
                   jax.ShapeDtypeStruct((T, D), BF16), jax.ShapeDtypeStruct((8, D), F32)],
        scratch_shapes=[pltpu.VMEM((D, D), BF16), pltpu.VMEM((D, 2 * DFF), BF16), pltpu.VMEM((DFF, D), BF16),
                        pltpu.SemaphoreType.DMA((4,))],
        compiler_params=_cp("arbitrary"),
    )(x, attn, ynorm, tgt, mod6, norm2_w, final_w, w_out, w_gu, w_gu_own, s_arr, w_dn)


def _in_proj_bwd(x, dx1, dqkv, dzxd, mod6, norm1_w, w_pad, tm, dep):
    T = x.shape[0]

    def body(x_ref, dx1_ref, dq_ref, dz_ref, mod_ref, nw_ref, w_hbm, dep_ref, gx_ref, sm_ref, w_vmem, sem):
        _load_resident(w_hbm, w_vmem, sem)

        @pl.when(pl.program_id(0) == 0)
        def _():
            sm_ref[...] = jnp.zeros_like(sm_ref)

        nw = nw_ref[...]
        scale1 = mod_ref[1:2, :]
        sums = jnp.zeros((8, D), F32)
        for rows in (slice(0, tm // 2), slice(tm // 2, tm)):
            dh = _dot_nt(dq_ref[rows, :], w_vmem[:, 0:768]) + _dot_nt(dz_ref[rows, :], w_vmem[:, 768:IN_PAD])
            xv = x_ref[rows, :]
            r = lax.rsqrt(jnp.mean(xv * xv, axis=-1, keepdims=True) + EPS)
            xh = xv * r
            n1 = xh * nw
            dshift = jnp.sum(dh, axis=0, keepdims=True)
            dscale = jnp.sum(dh * n1, axis=0, keepdims=True)
            dn = dh * (1.0 + scale1)
            dnw = jnp.sum(dn * xh, axis=0, keepdims=True)
            dxh = dn * nw
            gx_ref[rows, :] = dx1_ref[rows, :] + r * (dxh - xh * jnp.mean(dxh * xh, axis=-1, keepdims=True))
            sums = sums + jnp.concatenate([dnw, dshift, dscale, jnp.zeros((5, D), F32)], axis=0)
        sm_ref[...] += sums

    row = lambda w: pl.BlockSpec((tm, w), lambda i: (i, 0))
    full = lambda a: pl.BlockSpec(a.shape, lambda i: (0,) * a.ndim)
    return pl.pallas_call(
        body, name="in_proj_bwd", grid=(T // tm,),
        in_specs=[row(D), row(D), row(768), row(1664), full(mod6), full(norm1_w), pl.BlockSpec(memory_space=pl.ANY),
                  DEP_SPEC],
        out_specs=[row(D), pl.BlockSpec((8, D), lambda i: (0, 0))],
        out_shape=[jax.ShapeDtypeStruct((T, D), F32), jax.ShapeDtypeStruct((8, D), F32)],
        scratch_shapes=[pltpu.VMEM((D, IN_PAD), BF16), pltpu.SemaphoreType.DMA],
        compiler_params=_cp("arbitrary"),
    )(x, dx1, dqkv, dzxd, mod6, norm1_w, w_pad, dep)


def _tn_matmul(a, b, K, N, tt, name, dep):
    T = a.shape[0]
    ja, jb = a.shape[1] // K, b.shape[1] // N
    J = max(ja, jb)

    def body(a_ref, b_ref, dep_ref, o_ref):
        t = pl.program_id(1)
        prod = _dot_tn(a_ref[...], b_ref[...])

        @pl.when(t == 0)
        def _():
            o_ref[0] = prod

        @pl.when(t > 0)
        def _():
            o_ref[0] += prod

    return pl.pallas_call(
        body, name=name, grid=(J, T // tt),
        in_specs=[pl.BlockSpec((tt, K), lambda j, t: (t, j if ja > 1 else 0)),
                  pl.BlockSpec((tt, N), lambda j, t: (t, j if jb > 1 else 0)),
                  pl.BlockSpec((8, 128), lambda j, t: (0, 0))],
        out_specs=pl.BlockSpec((1, K, N), lambda j, t: (j, 0, 0)),
        out_shape=jax.ShapeDtypeStruct((J, K, N), F32),
        compiler_params=_cp("parallel", "arbitrary"),
    )(a, b, dep)


def _accumulate(o_ref, rows, prod):
    @pl.when(pl.program_id(0) == 0)
    def _():
        o_ref[rows, :] = prod

    @pl.when(pl.program_id(0) > 0)
    def _():
        o_ref[rows, :] += prod


def _tn_matmul_rows(a0, a1, b, tt, name, dep):
    T, K = a0.shape
    N = b.shape[1]

    def body(a0_ref, a1_ref, b_ref, dep_ref, o_ref):
        for k, a_ref in enumerate((a0_ref, a1_ref)):
            _accumulate(o_ref, slice(k * K, (k + 1) * K), _dot_tn(a_ref[...], b_ref[...]))

    tile = lambda w: pl.BlockSpec((tt, w), lambda t: (t, 0))
    return pl.pallas_call(
        body, name=name, grid=(T // tt,), in_specs=[tile(K), tile(K), tile(N), DEP_SPEC],
        out_specs=pl.BlockSpec((2 * K, N), lambda t: (0, 0)), out_shape=jax.ShapeDtypeStruct((2 * K, N), F32),
        compiler_params=_cp("arbitrary"),
    )(a0, a1, b, dep)


def _adam_math(w, g, m, v):
    m = B1 * m + (1.0 - B1) * g
    v = B2 * v + (1.0 - B2) * (g * g)
    m_hat = m / (1.0 - B1 ** STEP)
    v_hat = v / (1.0 - B2 ** STEP)
    delta = -LR * (m_hat / (jnp.sqrt(v_hat) + AEPS) + WD * w)
    return delta, m, v


def _adam_2d(w, mine, land, m, v, c_arr, rb, name, dep):
    R, C = w.shape
    nbh = R // 2 // rb

    def body(c_ref, w_ref, mine_ref, land_ref, m_ref, v_ref, dep_ref, go_ref, d_ref, mo_ref, vo_ref):
        g = jnp.where(pl.program_id(0) // nbh == c_ref[0], mine_ref[...], land_ref[...])
        d, mn, vn = _adam_math(w_ref[...], g, m_ref[...], v_ref[...])
        go_ref[...] = g
        d_ref[...] = d
        mo_ref[...] = mn
        vo_ref[...] = vn

    spec = pl.BlockSpec((rb, C), lambda i, c_ref: (i, 0))
    mine_spec = pl.BlockSpec((rb, C), lambda i, c_ref: (jnp.clip(i - c_ref[0] * nbh, 0, nbh - 1), 0))
    return pl.pallas_call(
        body, name=name,
        grid_spec=pltpu.PrefetchScalarGridSpec(
            num_scalar_prefetch=1, grid=(R // rb,), in_specs=[spec, mine_spec, spec, spec, spec, DEP_SPEC],
            out_specs=[spec] * 4),
        out_shape=[jax.ShapeDtypeStruct((R, C), F32)] * 4, compiler_params=_cp("parallel"),
    )(c_arr, w, mine, land, m, v, dep)


def _adam_w_in(w3, grad, m3, v3):
    n = w3.shape[0]

    def body(w_hbm, grad_ref, m_hbm, v_hbm, g_hbm, d_hbm, mo_hbm, vo_hbm, bufs, sems):
        ins = [pltpu.make_async_copy(src.at[:, 0], bufs.at[k], sems.at[k]) for k, src in enumerate((w_hbm, m_hbm, v_hbm))]
        for cp in ins:
            cp.start()
        g = grad_ref[...]
        eye =(_iota((D, D), 0) == _iota((D, D), 1)).astype(BF16)
        g_t = jnp.zeros((n, D), F32)
        r = g
        for i in range(3):
            p = r.astype(BF16)
            g_t = g_t + _dot_tn(p, eye)
            if i < 2:
                r = r - p.astype(F32)
        for cp in ins:
            cp.wait()
        d, mn, vn = _adam_math(bufs[0], g_t, bufs[1], bufs[2])
        for k, val in enumerate((g_t, d, mn, vn)):
            bufs[3 + k] = val
        outs = [pltpu.make_async_copy(bufs.at[3 + k], dst.at[:, 0], sems.at[3 + k])
                for k, dst in enumerate((g_hbm, d_hbm, mo_hbm, vo_hbm))]
        for cp in outs:
            cp.start()
        for cp in outs:
            cp.wait()

    anyspec = pl.BlockSpec(memory_space=pl.ANY)
    vm = pl.BlockSpec(memory_space=pltpu.VMEM)
    return pl.pallas_call(
        body, name="adam_w_in",
        in_specs=[anyspec, vm, anyspec, anyspec], out_specs=[anyspec] * 4,
        out_shape=[jax.ShapeDtypeStruct(w3.shape, F32)] * 4,
        scratch_shapes=[pltpu.VMEM((7, n, D), F32), pltpu.SemaphoreType.DMA((7,))],
        compiler_params=pltpu.CompilerParams(vmem_limit_bytes=VMEM_LIMIT),
    )(w3, grad, m3, v3)


def _adam_w_ada(gat, allv, s_arr, w, m, v, rb):
    R, C = w.shape

    def body(s_ref, c_ref, dm_ref, w_ref, m_ref, v_ref, g_ref, d_ref, mo_ref, vo_ref):
        cm = _rows_select(c_ref, rb)
        g = lax.dot_general(cm * _sigmoid(cm), _rows_select(dm_ref, C), (((0,), (0,)), ((), ())), precision=HI,
                            preferred_element_type=F32)
        d, mn, vn = _adam_math(w_ref[...], g, m_ref[...], v_ref[...])
        g_ref[...] = g
        d_ref[...] = d
        mo_ref[...] = mn
        vo_ref[...] = vn

    spec = pl.BlockSpec((rb, C), lambda i, s_ref: (i, 0))
    return pl.pallas_call(
        body, name="adam_w_ada",
        grid_spec=pltpu.PrefetchScalarGridSpec(
            num_scalar_prefetch=1, grid=(R // rb,),
            in_specs=[pl.BlockSpec((8, 1, rb), lambda i, s_ref: (0, 0, i)),
                      pl.BlockSpec((8, 1, C), lambda i, s_ref: (0, 0, s_ref[0])), spec, spec, spec],
            out_specs=[spec] * 4),
        out_shape=[jax.ShapeDtypeStruct((R, C), F32)] * 4, compiler_params=_cp("parallel"),
    )(s_arr, gat, allv, w, m, v)


def _adam_small(tot, segs, ws, ms, vs):
    k = len(ws)
    extra = [sg for sg in segs if not isinstance(sg, tuple)]
    ne = len(extra)

    def body(*refs):
        tot_ref, g_x = refs[0], list(refs[1:1 + ne])
        w, m, v = [refs[1 + ne + j * k:1 + ne + (j + 1) * k] for j in range(3)]
        g_o, d_o, m_o, v_o = [refs[1 + ne + (3 + j) * k:1 + ne + (4 + j) * k] for j in range(4)]
        for i in range(k):
            gi = tot_ref[:, segs[i][0]:segs[i][0] + segs[i][1]] if isinstance(segs[i], tuple) else g_x.pop(0)[...]
            d, mn, vn = _adam_math(w[i][...], gi, m[i][...], v[i][...])
            g_o[i][...] = gi
            d_o[i][...] = d
            m_o[i][...] = mn
            v_o[i][...] = vn

    shapes = [jax.ShapeDtypeStruct(w.shape, F32) for w in ws]
    vm = pl.BlockSpec(memory_space=pltpu.VMEM)
    outs = pl.pallas_call(
        body, name="adam_small", in_specs=[vm] * (1 + ne + 3 * k), out_specs=[vm] * (4 * k), out_shape=shapes * 4,
    )(tot, *extra, *ws, *ms, *vs)
    return outs[0:k], outs[k:2 * k], outs[2 * k:3 * k], outs[3 * k:4 * k]


def _pos():
    return lax.axis_index("x"), lax.axis_index("y"), lax.axis_index("c")


def _flip(v, bit):
    return 1 - v if bit else v


def _peer(k):
    x, y, c = _pos()
    return (_flip(x, (k >> 2) & 1), _flip(y, (k >> 1) & 1), _flip(c, k & 1))


def _logical(p):
    return 4 * p[0] + 2 * p[1] + p[2]


def _gather8(src_ref, dst_ref, send_sems, recv_sems, meanwhile):
    me = _logical(_pos())
    dst_ref[pl.ds(me, 1)] = src_ref[...][None]
    copies = []
    for k in range(1, 8):
        cp = pltpu.make_async_remote_copy(src_ref, dst_ref.at[me], send_sems.at[k - 1], recv_sems.at[k - 1],
                                          device_id=_peer(k), device_id_type=MESH)
        cp.start()
        copies.append(cp)
    meanwhile()
    for k in range(1, 8):
        pltpu.make_async_remote_copy(src_ref, dst_ref.at[_logical(_peer(k))], send_sems.at[k - 1], recv_sems.at[k - 1],
                                     device_id=_peer(k), device_id_type=MESH).wait_recv()
    for cp in copies:
        cp.wait_send()


def _rows_select(ref3, width):
    row = _iota((8, width), 0)
    out = jnp.zeros((8, width), F32)
    for i in range(8):
        out = jnp.where(row == i, ref3[i][:, 0:width], out)
    return out


def _mod_exchange(c_row, cw, w_ada_s, b_ada4, w_in3):
    n_sh = w_ada_s.shape[1]
    n_in = w_in3.shape[0]
    wide = -(-n_in // 128) * 128

    def body(c_ref, cw_ref, w_hbm, b_ref, win_hbm, gat_ref, mod_ref, token, winb_ref, pay_ref, p3, w_v, win_v, win_z,
             sa, ra, sb, rb, ls):
        token[...] = jnp.zeros_like(token)
        pay_ref[:, 0:D] = c_ref[...]
        for k in range(CONVK):
            pay_ref[:, D + 256 * k:D + 256 * (k + 1)] = cw_ref[k:k + 1, :]
        x, y, c = _pos()
        me = _logical((x, y, c))
        my_s = 2 * x + y
        load_w = pltpu.make_async_copy(w_hbm, w_v, ls.at[0])
        load_in = pltpu.make_async_copy(win_hbm.at[:, 0], win_v, ls.at[1])
        load_w.start()
        load_in.start()

        def local_work():
            win_z[...] = jnp.zeros_like(win_z)
            load_in.wait()
            win_z[0:n_in, :] = win_v[...].astype(BF16)
            eye = (_iota((wide, wide), 0) == _iota((wide, wide), 1)).astype(BF16)
            winb_ref[...] = _dot_tn(win_z[...], eye)[:, 0:n_in].astype(BF16)
            load_w.wait()

        _gather8(pay_ref, gat_ref, sa, ra, local_work)
        cmat = _rows_select(gat_ref, D)
        prod = _dot_hi(cmat * _sigmoid(cmat), w_v[...])
        for b in range(8):
            p3[b] = prod[b:b + 1, :]
        mod_ref[pl.ds(my_s, 1)] = p3[pl.ds(me, 1)] + b_ref[pl.ds(my_s, 1)]
        ks = (2, 4, 6)
        copies = []
        for i, k in enumerate(ks):
            pr = _peer(k)
            cp = pltpu.make_async_remote_copy(p3.at[_logical(pr)], mod_ref.at[my_s], sb.at[i], rb.at[i],
                                              device_id=pr, device_id_type=MESH)
            cp.start()
            copies.append(cp)
        for i, k in enumerate(ks):
            pr = _peer(k)
            s_src = 2 * pr[0] + pr[1]
            pltpu.make_async_remote_copy(p3.at[0], mod_ref.at[s_src], sb.at[i], rb.at[i],
                                         device_id=pr, device_id_type=MESH).wait_recv()
            mod_ref[pl.ds(s_src, 1)] = mod_ref[pl.ds(s_src, 1)] + b_ref[pl.ds(s_src, 1)]
        for cp in copies:
            cp.wait_send()

    vm = pl.BlockSpec(memory_space=pltpu.VMEM)
    anyspec = pl.BlockSpec(memory_space=pl.ANY)
    return pl.pallas_call(
        body, name="mod_exchange", in_specs=[vm, vm, anyspec, vm, anyspec], out_specs=[vm, vm, vm, vm],
        out_shape=[jax.ShapeDtypeStruct((8, 1, D + CONVK * 256), F32), jax.ShapeDtypeStruct((4, 1, n_sh), F32),
                   jax.ShapeDtypeStruct((8, 128), F32), jax.ShapeDtypeStruct((D, n_in), BF16)],
        scratch_shapes=[pltpu.VMEM((1, D + CONVK * 256), F32), pltpu.VMEM((8, 1, n_sh), F32), pltpu.VMEM(w_ada_s.shape, F32), pltpu.VMEM((n_in, D), F32),
                        pltpu.VMEM((wide, D), BF16), pltpu.SemaphoreType.DMA((7,)), pltpu.SemaphoreType.DMA((7,)),
                        pltpu.SemaphoreType.DMA((3,)), pltpu.SemaphoreType.DMA((3,)), pltpu.SemaphoreType.DMA((2,))],
        compiler_params=pltpu.CompilerParams(vmem_limit_bytes=VMEM_LIMIT),
    )(c_row, cw, w_ada_s, b_ada4, w_in3)


def _chips():
    x, y, _ = _pos()
    out = []
    for k in (1, 2, 3):
        px, py = _flip(x, (k >> 1) & 1), _flip(y, k & 1)
        out.append((px, py, 2 * px + py))
    return out


def _half_rows(ref, which):
    half = ref.shape[-2] // 2
    return pl.ds(pl.multiple_of(which * half, 8), half)


def _plan_small():
    def plan(refs):
        me = _logical(_pos())
        return [(refs[0], refs[1].at[me], _peer(k), refs[1].at[_logical(_peer(k))]) for k in range(1, 8)]
    return plan


def _small_sum(vec, land, me_arr):
    n = vec.shape[1]

    def body(me_ref, v_ref, land_ref, tot_ref, all_ref):
        tot = None
        for i in range(8):
            row = jnp.where(me_ref[0] == i, v_ref[...], land_ref[i])
            all_ref[i] = row
            tot = row if i == 0 else tot + row
        tot_ref[...] = tot

    return pl.pallas_call(
        body, name="small_sum",
        grid_spec=pltpu.PrefetchScalarGridSpec(
            num_scalar_prefetch=1, grid=(1,),
            in_specs=[pl.BlockSpec((1, n), lambda i, me_ref: (0, 0)), pl.BlockSpec((8, 1, n), lambda i, me_ref: (0, 0, 0))],
            out_specs=[pl.BlockSpec((1, n), lambda i, me_ref: (0, 0)),
                       pl.BlockSpec((8, 1, n), lambda i, me_ref: (0, 0, 0))]),
        out_shape=[jax.ShapeDtypeStruct((1, n), F32), jax.ShapeDtypeStruct((8, 1, n), F32)],
        compiler_params=_cp("arbitrary"),
    )(me_arr, vec, land)


def _add_half(g, sib, c_arr, rb, name):
    _, R, C = g.shape
    half = R // 2
    nb = half // rb

    def body(c_ref, g_ref, s_ref, o_ref):
        o_ref[...] = (g_ref[...] + s_ref[...]).astype(BF16)

    return pl.pallas_call(
        body, name=name,
        grid_spec=pltpu.PrefetchScalarGridSpec(
            num_scalar_prefetch=1, grid=(4, nb),
            in_specs=[pl.BlockSpec((1, rb, C), lambda s, i, c_ref: (s, c_ref[0] * nb + i, 0)),
                      pl.BlockSpec((1, rb, C), lambda s, i, c_ref: (s, i, 0))],
            out_specs=pl.BlockSpec((1, rb, C), lambda s, i, c_ref: (s, i, 0))),
        out_shape=jax.ShapeDtypeStruct((4, half, C), BF16),
        compiler_params=_cp("parallel", "parallel"),
    )(c_arr, g, sib)


def _add_half_in(gq, gz, sibq, sibz, c_arr, rb):
    half = D // 2
    nq = gq.shape[1]
    wide = -(-IN_SH // 128) * 128

    def sel(rows, first, lo):
        return (_iota((rows, wide), 0) + (first - lo) == _iota((rows, wide), 1)).astype(BF16)

    def body(c_ref, gq_ref, gz_ref, sq_ref, sz_ref, o_ref):
        q = (gq_ref[...] + sq_ref[...]).astype(BF16)
        z = (gz_ref[...] + sz_ref[...]).astype(BF16)
        for s in range(4):
            lo, hi = s * IN_SH, (s + 1) * IN_SH
            acc = jnp.zeros((rb, wide), F32)
            if lo < nq:
                a0, a1 = lo // 128 * 128, min(nq, -(-min(hi, nq) // 128) * 128)
                acc = acc + _dot(q[:, a0:a1], sel(a1 - a0, a0, lo))
            if hi > nq:
                a0, a1 = (max(lo, nq) - nq) // 128 * 128, -(-(hi - nq) // 128) * 128
                acc = acc + _dot(z[:, a0:a1], sel(a1 - a0, nq + a0, lo))
            o_ref[s] = acc[:, :IN_SH].astype(BF16)

    nb = half // rb
    mine = lambda w: pl.BlockSpec((rb, w), lambda i, c_ref: (c_ref[0] * nb + i, 0))
    sib = lambda w: pl.BlockSpec((rb, w), lambda i, c_ref: (i, 0))
    return pl.pallas_call(
        body, name="grad_add_in",
        grid_spec=pltpu.PrefetchScalarGridSpec(
            num_scalar_prefetch=1, grid=(nb,),
            in_specs=[mine(nq), mine(gz.shape[1]), sib(nq), sib(gz.shape[1])],
            out_specs=pl.BlockSpec((4, rb, IN_SH), lambda i, c_ref: (0, i, 0))),
        out_shape=jax.ShapeDtypeStruct((4, half, IN_SH), BF16),
        compiler_params=_cp("parallel"),
    )(c_arr, gq, gz, sibq, sibz)


def _sum4(parts, land, s_arr, rb, name):
    _, H, C = land.shape

    def body(s_ref, own_ref, r_ref, o_ref):
        own = own_ref[0].astype(F32)
        tot = jnp.zeros((rb, C), F32)
        for j in range(4):
            tot = tot + jnp.where(s_ref[0] == j, own, r_ref[j].astype(F32))
        o_ref[...] = tot

    return pl.pallas_call(
        body, name=name,
        grid_spec=pltpu.PrefetchScalarGridSpec(
            num_scalar_prefetch=1, grid=(H // rb,),
            in_specs=[pl.BlockSpec((1, rb, C), lambda i, s_ref: (s_ref[0], i, 0)),
                      pl.BlockSpec((4, rb, C), lambda i, s_ref: (0, i, 0))],
            out_specs=pl.BlockSpec((rb, C), lambda i, s_ref: (i, 0))),
        out_shape=jax.ShapeDtypeStruct((H, C), F32), compiler_params=_cp("parallel"),
    )(s_arr, parts, land)


HBM_SPEC = pl.BlockSpec(memory_space=pltpu.HBM)
SEM_SPEC = pl.BlockSpec(memory_space=pltpu.SEMAPHORE)
EFFECT = pltpu.SideEffectType.DATAFLOW_SIDE_EFFECTING


def _split_start(name, bufs, n_sem, plan, dep):
    nb = len(bufs)

    def body(*refs):
        ins, send, recv, token = refs[:nb], refs[nb + 1], refs[nb + 2], refs[-1]
        for i, (src, dst, dev, _) in enumerate(plan(ins)):
            pltpu.make_async_remote_copy(src, dst, send.at[i], recv.at[i], device_id=dev, device_id_type=MESH).start()
        token[...] = jnp.zeros_like(token)

    outs = pl.pallas_call(
        body, name=name,
        out_shape=(pltpu.SemaphoreType.DMA((n_sem,)), pltpu.SemaphoreType.DMA((n_sem,)),
                   *[pltpu.HBM(b.shape, b.dtype) for b in bufs], jax.ShapeDtypeStruct((8, 128), F32)),
        in_specs=[HBM_SPEC] * nb + [pl.BlockSpec(memory_space=pl.ANY)],
        out_specs=(SEM_SPEC, SEM_SPEC, *([HBM_SPEC] * nb), pl.BlockSpec(memory_space=pltpu.VMEM)),
        input_output_aliases={i: 2 + i for i in range(nb)},
        compiler_params=pltpu.CompilerParams(has_side_effects=EFFECT),
    )(*[pltpu.with_memory_space_constraint(b, pltpu.HBM) for b in bufs], dep)
    return outs[0], outs[1], list(outs[2:2 + nb]), outs[-1]


def _split_wait(name, send, recv, bufs, after, plan):
    nb = len(bufs)
    after = list(after) if isinstance(after, (list, tuple)) else [after]

    def body(*refs):
        ins, send_s, recv_s = refs[:nb], refs[nb], refs[nb + 1]
        for i, (src, dst, dev, mine) in enumerate(plan(ins)):
            pltpu.make_async_remote_copy(src, dst, send_s.at[i], recv_s.at[i], device_id=dev,
                                         device_id_type=MESH).wait_send()
            pltpu.make_async_remote_copy(src, mine, send_s.at[i], recv_s.at[i], device_id=dev,
                                         device_id_type=MESH).wait_recv()

    outs = pl.pallas_call(
        body, name=name, out_shape=[pltpu.HBM(b.shape, b.dtype) for b in bufs],
        in_specs=[HBM_SPEC] * nb + [SEM_SPEC, SEM_SPEC] + [HBM_SPEC] * len(after),
        out_specs=[HBM_SPEC] * nb, input_output_aliases={i: i for i in range(nb)},
        compiler_params=pltpu.CompilerParams(has_side_effects=EFFECT),
    )(*bufs, send, recv, *[pltpu.with_memory_space_constraint(a, pltpu.HBM) for a in after])
    return list(outs)


def _split_wait_start(name, send, recv, bufs, after, plan, bufs2, n_sem2, plan2):
    again = bufs2 if isinstance(bufs2, int) else None
    bufs2 = [] if again is not None else bufs2
    nb, nb2 = len(bufs), len(bufs2)
    after = list(after) if isinstance(after, (list, tuple)) else [after]
    n_in = nb + 2 + nb2 + len(after)

    def body(*refs):
        ins, send_s, recv_s, ins2 = refs[:nb], refs[nb], refs[nb + 1], refs[nb + 2:nb + 2 + nb2]
        if again is not None:
            ins2 = ins[again:]
        send2, recv2, token = refs[n_in + nb], refs[n_in + nb + 1], refs[-1]
        for i, (src, dst, dev, mine) in enumerate(plan(ins)):
            pltpu.make_async_remote_copy(src, dst, send_s.at[i], recv_s.at[i], device_id=dev,
                                         device_id_type=MESH).wait_send()
            pltpu.make_async_remote_copy(src, mine, send_s.at[i], recv_s.at[i], device_id=dev,
                                         device_id_type=MESH).wait_recv()
        for i, (src, dst, dev, _) in enumerate(plan2(ins2)):
            pltpu.make_async_remote_copy(src, dst, send2.at[i], recv2.at[i], device_id=dev, device_id_type=MESH).start()
        token[...] = jnp.zeros_like(token)

    hbm = lambda b: pltpu.with_memory_space_constraint(b, pltpu.HBM)
    outs = pl.pallas_call(
        body, name=name,
        out_shape=(*[pltpu.HBM(b.shape, b.dtype) for b in bufs], pltpu.SemaphoreType.DMA((n_sem2,)),
                   pltpu.SemaphoreType.DMA((n_sem2,)), *[pltpu.HBM(b.shape, b.dtype) for b in bufs2],
                   jax.ShapeDtypeStruct((8, 128), F32)),
        in_specs=[HBM_SPEC] * nb + [SEM_SPEC, SEM_SPEC] + [HBM_SPEC] * (nb2 + len(after)),
        out_specs=(*([HBM_SPEC] * nb), SEM_SPEC, SEM_SPEC, *([HBM_SPEC] * nb2), pl.BlockSpec(memory_space=pltpu.VMEM)),
        input_output_aliases={**{i: i for i in range(nb)}, **{nb + 2 + j: nb + 2 + j for j in range(nb2)}},
        compiler_params=pltpu.CompilerParams(has_side_effects=EFFECT),
    )(*bufs, send, recv, *[hbm(b) for b in bufs2], *[hbm(a) for a in after])
    return list(outs[:nb]), outs[nb], outs[nb + 1], list(outs[nb + 2:nb + 2 + nb2]), outs[-1]


def _copies_now(name, bufs, n_sem, plan):
    nb = len(bufs)

    def body(*refs):
        ins, token, send, recv = refs[:nb], refs[2 * nb], refs[-2], refs[-1]
        token[...] = jnp.zeros_like(token)
        todo = plan(ins)
        for i, (src, dst, dev, _) in enumerate(todo):
            pltpu.make_async_remote_copy(src, dst, send.at[i], recv.at[i], device_id=dev, device_id_type=MESH).start()
        for i, (src, dst, dev, mine) in enumerate(todo):
            pltpu.make_async_remote_copy(src, mine, send.at[i], recv.at[i], device_id=dev, device_id_type=MESH).wait_recv()
        for i, (src, dst, dev, _) in enumerate(todo):
            pltpu.make_async_remote_copy(src, dst, send.at[i], recv.at[i], device_id=dev, device_id_type=MESH).wait_send()

    outs = pl.pallas_call(
        body, name=name,
        out_shape=[pltpu.HBM(b.shape, b.dtype) for b in bufs] + [jax.ShapeDtypeStruct((8, 128), F32)],
        in_specs=[HBM_SPEC] * nb, out_specs=[HBM_SPEC] * nb + [pl.BlockSpec(memory_space=pltpu.VMEM)],
        input_output_aliases={i: i for i in range(nb)},
        scratch_shapes=[pltpu.SemaphoreType.DMA((n_sem,)), pltpu.SemaphoreType.DMA((n_sem,))],
    )(*[pltpu.with_memory_space_constraint(b, pltpu.HBM) for b in bufs])
    return list(outs[:nb]), outs[nb]


def _slot(land, s, rows, cols):
    if cols is None:
        return land.at[s, rows]
    return land.at[rows, pl.ds(pl.multiple_of(s * cols, 128), cols)]


def _plan_gather_ici(cols):
    nw = len(cols)

    def plan(refs):
        x, y, c = _pos()
        my_s = 2 * x + y
        out = []
        for w in range(nw):
            mine = _half_rows(refs[w], c)
            for px, py, ps in _chips():
                out.append((refs[w].at[mine], _slot(refs[nw + w], my_s, mine, cols[w]), (px, py, c),
                            _slot(refs[nw + w], ps, mine, cols[w])))
        return out
    return plan


def _plan_gather_fwd(cols, rows):
    def plan(refs):
        x, y, c = _pos()
        out = []
        for w in range(len(cols)):
            half = rows[w] // 2
            mine = pl.ds(pl.multiple_of(c * half, 8), half)
            other = pl.ds(pl.multiple_of((1 - c) * half, 8), half)
            for px, py, ps in _chips():
                got = _slot(refs[w], ps, mine, cols[w])
                out.append((got, got, (x, y, 1 - c), _slot(refs[w], ps, other, cols[w])))
        return out
    return plan


def _plan_swap(nw):
    def plan(refs):
        x, y, c = _pos()
        return [(refs[w].at[:, _half_rows(refs[w], 1 - c)], refs[nw + w], (x, y, 1 - c), refs[nw + w])
                for w in range(nw)]
    return plan


def _plan_swap_rows(nw):
    def plan(refs):
        x, y, c = _pos()
        return [(refs[w].at[_half_rows(refs[w], 1 - c)], refs[nw + w], (x, y, 1 - c), refs[nw + w])
                for w in range(nw)]
    return plan


def _plan_scatter(nw):
    def plan(refs):
        x, y, c = _pos()
        my_s = 2 * x + y
        out = []
        for w in range(nw):
            for px, py, ps in _chips():
                out.append((refs[w].at[ps], refs[nw + w].at[my_s], (px, py, c), refs[nw + w].at[ps]))
        return out
    return plan


def _plan_scatter_both():
    def plan(refs):
        x, y, c = _pos()
        my_s = 2 * x + y
        src, land = refs
        out = []
        for px, py, ps in _chips():
            out.append((src.at[ps], land.at[my_s, c], (px, py, c), land.at[ps, c]))
            out.append((src.at[ps], land.at[my_s, c], (px, py, 1 - c), land.at[ps, 1 - c]))
        out.append((src.at[my_s], land.at[my_s, c], (x, y, 1 - c), land.at[my_s, 1 - c]))
        return out
    return plan


def _sum4_both(parts, land, s_arr, c_arr):
    _, _, H, C = land.shape

    def body(s_ref, c_ref, own_ref, r_ref, o_ref):
        mine = pl.program_id(0) == c_ref[0]
        own = own_ref[0].astype(F32)
        tot = jnp.zeros((H, C), F32)
        for j in range(4):
            tot = tot + jnp.where(jnp.logical_and(mine, s_ref[0] == j), own, r_ref[j, 0].astype(F32))
        o_ref[0] = tot

    return pl.pallas_call(
        body, name="grad_sum_in",
        grid_spec=pltpu.PrefetchScalarGridSpec(
            num_scalar_prefetch=2, grid=(2,),
            in_specs=[pl.BlockSpec((1, H, C), lambda h, s_ref, c_ref: (s_ref[0], 0, 0)),
                      pl.BlockSpec((4, 1, H, C), lambda h, s_ref, c_ref: (0, h, 0, 0))],
            out_specs=pl.BlockSpec((1, H, C), lambda h, s_ref, c_ref: (h, 0, 0))),
        out_shape=jax.ShapeDtypeStruct((2, H, C), F32), compiler_params=_cp("parallel"),
    )(s_arr, c_arr, parts, land).reshape(2 * H, C)


def _plan_join(nw):
    def plan(refs):
        x, y, c = _pos()
        out = []
        for w in range(nw):
            land = refs[nw + w]
            out.append((refs[w], land.at[_half_rows(land, c)], (x, y, 1 - c), land.at[_half_rows(land, 1 - c)]))
        return out
    return plan


def _hbm_empty(shape, dtype):
    return pltpu.with_memory_space_constraint(lax.empty(shape, dtype), pltpu.HBM)


def _put_slot(land, own, slot):
    return lax.dynamic_update_slice(land, own[None], (slot,) + (0,) * own.ndim)


def _w_in_assemble(land, own, s_arr, rb):
    wide = -(-IN_SH // 128) * 128
    starts = [s * IN_SH // 128 * 128 for s in range(4)]
    ends = [min(IN_PAD, -(-(s + 1) * IN_SH // 128) * 128) for s in range(4)]

    def body(s_ref, land_ref, own_ref, o_ref, parts):
        @pl.when(pl.program_id(0) == 0)
        def _():
            parts[...] = jnp.zeros_like(parts)

        acc = []
        for s in range(4):
            parts[s, :, 0:IN_SH] = jnp.where(s_ref[0] == s, own_ref[...], land_ref[s])
            w = ends[s] - starts[s]
            sel = (_iota((wide, w), 0) + (s * IN_SH - starts[s]) == _iota((wide, w), 1)).astype(BF16)
            acc.append(_dot(parts[s], sel))
        for s in range(4):
            lo = starts[s] if s == 0 else ends[s - 1]
            hi = starts[s + 1] if s < 3 else ends[s]
            o_ref[:, lo:hi] = acc[s][:, lo - starts[s]:hi - starts[s]].astype(BF16)
            if s < 3:
                a, b = starts[s + 1], ends[s]
                o_ref[:, a:b] = (acc[s][:, a - starts[s]:b - starts[s]] + acc[s + 1][:, 0:b - a]).astype(BF16)

    return pl.pallas_call(
        body, name="w_in_assemble",
        grid_spec=pltpu.PrefetchScalarGridSpec(
            num_scalar_prefetch=1, grid=(D // rb,),
            in_specs=[pl.BlockSpec((4, rb, IN_SH), lambda i, s_ref: (0, i, 0)),
                      pl.BlockSpec((rb, IN_SH), lambda i, s_ref: (i, 0))],
            out_specs=pl.BlockSpec((rb, IN_PAD), lambda i, s_ref: (i, 0)),
            scratch_shapes=[pltpu.VMEM((4, rb, wide), BF16)]),
        out_shape=jax.ShapeDtypeStruct((D, IN_PAD), BF16), compiler_params=_cp("arbitrary"),
    )(s_arr, land, own)


def _pad_lanes(a, n):
    return jnp.pad(a, ((0, 0), (0, n - a.shape[1])))


def kernel(x, c, positions, w_ada, b_ada, norm1_w, w_in, conv_w, conv_b, dt_bias, a_log, d_skip, attn_sinks, ssm_norm_w, w_out, norm2_w, w_gate_up, w_down, final_norm_w, loss_target, m_w_ada, m_b_ada, m_norm1_w, m_w_in, m_conv_w, m_conv_b, m_dt_bias, m_a_log, m_d_skip, m_attn_sinks, m_ssm_norm_w, m_w_out, m_norm2_w, m_w_gate_up, m_w_down, m_final_norm_w, v_w_ada, v_b_ada, v_norm1_w, v_w_in, v_conv_w, v_conv_b, v_dt_bias, v_a_log, v_d_skip, v_attn_sinks, v_ssm_norm_w, v_w_out, v_norm2_w, v_w_gate_up, v_w_down, v_final_norm_w):
    T = x.shape[1]
    tm = min(256, T)
    xi, yi, ci = lax.axis_index("x"), lax.axis_index("y"), lax.axis_index("c")
    my_s = 2 * xi + yi
    xs = x[0]
    tgt = loss_target[0]

    gat, mod4, tok, w_in_b = _mod_exchange(c, conv_w[0], w_ada[0], b_ada.reshape(4, 1, 1536), w_in.transpose(2, 0, 1))
    mod6 = mod4.reshape(6, D)
    cw_dev = gat[:, 0, D:].reshape(4, 2, CONVK, 256)[:, 0]
    conv_full = cw_dev.transpose(1, 0, 2).reshape(CONVK, CONVC)

    s_i, r_i, bufs, tok = _split_start("wgather_in_ici_start", [w_in_b, _hbm_empty((4,) + w_in_b.shape, BF16)], 3,
                                       _plan_gather_ici([None]), tok)
    inv_freq = (10000.0 ** (-jnp.arange(32, dtype=F32) / 32))
    cos, sin_s = _rope_tables(positions, inv_freq.reshape(32, 1), min(512, T), tok)
    late = [w_out[0].astype(BF16), w_gate_up[0].astype(BF16), w_down[0].astype(BF16)]
    lands = [_hbm_empty((4, D // 4, D), BF16), _hbm_empty((D, 2 * DFF), BF16), _hbm_empty((4, DFF // 4, D), BF16)]
    cols3, rows3 = [None, GU_SH, None], [D // 4, D, DFF // 4]
    bufs, s_a, r_a, bufs_late, tok = _split_wait_start(
        "wgather_in_ici_wait", s_i, r_i, bufs, cos, _plan_gather_ici([None]), late + lands, 9, _plan_gather_ici(cols3))
    own_in = bufs[0]
    bufs, tok = _copies_now("wgather_in_fwd", bufs[1:], 3, _plan_gather_fwd([None], [D]))
    s_arr = my_s.reshape(1).astype(jnp.int32)
    w_pad = _w_in_assemble(bufs[0], own_in, s_arr, 256)
    bufs = bufs_late

    qkv, z, xbc, dtr, h1b = _in_proj_fwd(xs, cos, sin_s, mod6, norm1_w, w_pad, min(512, T), tok)
    sinks = attn_sinks
    attn, lse = _attn_fwd(qkv, sinks)
    bufs, s_b, r_b, _, tok = _split_wait_start("wgather_ici_wait", s_a, r_a, bufs, attn, _plan_gather_ici(cols3),
                                               3, 9, _plan_gather_fwd(cols3, rows3))
    late, lands = bufs[:3], bufs[3:]
    dtb = _pad_lanes(dt_bias, 128)
    alog = _pad_lanes(a_log, 128)
    dskx = jnp.repeat(d_skip, HD, axis=1)
    mats = _ssd_mats()
    ynorm, ypre, states, conv_pre = _ssd_fwd(xbc, z, dtr, conv_full, conv_b, dtb, alog, dskx, ssm_norm_w, mats, tok)
    lands = _split_wait("wgather_fwd_wait", s_b, r_b, lands, ynorm, _plan_gather_fwd(cols3, rows3))
    w_out_f = _put_slot(lands[0], late[0], my_s).reshape(D, D)
    w_dn_f = _put_slot(lands[2], late[2], my_s).reshape(DFF, D)

    fw2 = final_norm_w.reshape(1, D)
    sq, dmix, dx1, h2b, act, dfb, dgu, dob, sm_ffn = _mix_ffn(
        xs, attn, ynorm, tgt, mod6, norm2_w, fw2, w_out_f, lands[1], late[1], s_arr, w_dn_f, tm)

    tt = min(2048, T)
    c_arr = ci.reshape(1).astype(jnp.int32)
    tok0 = jnp.zeros((8, 128), F32)
    gw_dn4 = _tn_matmul(act, dfb, GU_SH, D, tt, "dw_down", tok0).reshape(4, DFF // 4, D)
    gw_gu4 = _tn_matmul(h2b, dgu, D, GU_SH, tt, "dw_gate_up", tok0)
    gw_out4 = _tn_matmul_rows(attn, ynorm, dob, tt, "dw_out", tok0).reshape(4, D // 4, D)
    big1 = [gw_out4, gw_gu4, gw_dn4]
    rbs1 = [128, 512, 352]
    sib1 = [_hbm_empty((4, g.shape[1] // 2, g.shape[2]), F32) for g in big1]
    s_c, r_c, bufs, tok = _split_start("gswap_start", big1 + sib1, 3, _plan_swap(3), tok0)

    dzxd, d_cw, d_cb, d_sw, d_sk, d_dtb, d_av = _ssd_bwd(
        xbc, conv_pre, z, dtr, ypre, states, dmix, conv_full, dtb, alog, dskx, ssm_norm_w, mats, tok)
    bufs = _split_wait("gswap_wait", s_c, r_c, bufs, dzxd, _plan_swap(3))
    sums1 = [_add_half(g, s, c_arr, rb, "grad_add_%d" % i)
             for i, (g, s, rb) in enumerate(zip(bufs[:3], bufs[3:], rbs1))]
    land1 = [_hbm_empty(p.shape, BF16) for p in sums1]
    s_d, r_d, bufs, tok = _split_start("gscatter_start", sums1 + land1, 9, _plan_scatter(3), tok0)
    dqkv, d_sinks = _attn_bwd(qkv, sinks, lse, dmix, cos, sin_s, tok)
    bufs = _split_wait("gscatter_wait", s_d, r_d, bufs, dqkv, _plan_scatter(3))
    halves1 = [_sum4(p, l, s_arr, rb, "grad_sum_%d" % i)
               for i, (p, l, rb) in enumerate(zip(bufs[:3], bufs[3:], rbs1))]
    full1 = [_hbm_empty((2 * h.shape[0], h.shape[1]), F32) for h in halves1]
    s_e, r_e, bufs, tok = _split_start("gjoin_start", halves1 + full1, 3, _plan_join(3), tok0)
    gq = _tn_matmul(h1b, dqkv, D, 768, tt, "dw_in_qkv", tok)[0]
    gz = _tn_matmul(h1b, dzxd, D, IN_PAD - 768, tt, "dw_in_zxd", tok)[0]

    sibs = [_hbm_empty((D // 2, g.shape[1]), F32) for g in (gq, gz)]
    joined1, s_f, r_f, bufs, tok = _split_wait_start("gjoin_wait", s_e, r_e, bufs, [], _plan_join(3),
                                                     [gq, gz] + sibs, 2, _plan_swap_rows(2))
    g_dn_s, d_dn, m_dn, v_dn = _adam_2d(w_down[0], joined1[2], joined1[5], m_w_down[0], v_w_down[0], c_arr, 352,
                                        "adam_w_down", tok)
    g_gu_s, d_gu, m_gu, v_gu = _adam_2d(w_gate_up[0], joined1[1], joined1[4], m_w_gate_up[0], v_w_gate_up[0], c_arr,
                                        256, "adam_w_gate_up", tok)
    g_out_s, d_out, m_out, v_out = _adam_2d(w_out[0], joined1[0], joined1[3], m_w_out[0], v_w_out[0], c_arr, 128,
                                            "adam_w_out", tok)
    bufs = _split_wait("gswap_in_wait", s_f, r_f, bufs, [d_dn, d_gu, d_out], _plan_swap_rows(2))
    sum0 = _add_half_in(bufs[0], bufs[1], bufs[2], bufs[3], c_arr, min(256, D // 2))
    s_g, r_g, bufs, tok = _split_start("gscatter_in_start", [sum0, _hbm_empty((4, 2) + sum0.shape[1:], BF16)], 7,
                                       _plan_scatter_both(), tok0)
    grad_x, sm_in = _in_proj_bwd(xs, dx1, dqkv, dzxd, mod6, norm1_w, w_pad, min(512, T), tok)

    a_neg = -jnp.exp(alog)
    pieces = [sm_in[1:2], sm_in[2:3], sm_ffn[5:6], sm_ffn[2:3], sm_ffn[3:4], sm_ffn[4:5],
              sm_in[0:1], sm_ffn[1:2], sm_ffn[0:1], d_cb, d_cw.reshape(1, CONVK * CONVC),
              _pad_lanes(d_sw, SW), d_dtb, d_av * a_neg, d_sk, d_sinks,
              _pad_lanes((0.5 / D * jnp.sum(sq)).reshape(1, 1), 128)]
    vec = jnp.concatenate(pieces, axis=1)

    bufs, s_h, r_h, rows8, _ = _split_wait_start(
        "gscatter_in_wait", s_g, r_g, bufs, grad_x, _plan_scatter_both(),
        [vec, _hbm_empty((8,) + vec.shape, F32)], 7, _plan_small())
    gw_in_s = _sum4_both(bufs[0], bufs[1], s_arr, c_arr)
    native = lambda a: a.transpose(2, 0, 1)
    adam_in = _adam_w_in(native(w_in), gw_in_s, native(m_w_in), native(v_w_in))
    g_in_s, d_in, m_in, v_in = [a.transpose(1, 2, 0) for a in adam_in]
    rows8 = _split_wait("small_wait", s_h, r_h, rows8, [adam_in[1]], _plan_small())
    tot, allv = _small_sum(rows8[0], rows8[1], (4 * xi + 2 * yi + ci).reshape(1).astype(jnp.int32))
    o = 0
    offs = []
    for p in pieces:
        offs.append(o)
        o += p.shape[1]
    seg = lambda i, n: (offs[i], n)
    g_conv_w = lax.dynamic_slice_in_dim(
        tot[:, offs[10]:offs[10] + CONVK * CONVC].reshape(CONVK, CONVC), my_s * 256, 256, axis=1)
    loss = tot[0, offs[16]]

    small_names = ["b_ada", "norm1_w", "conv_w", "conv_b", "dt_bias", "a_log", "d_skip", "attn_sinks", "ssm_norm_w",
                   "norm2_w", "final_norm_w"]
    small_g = [(0, 6 * D), seg(6, D), g_conv_w, seg(9, D), seg(12, 8), seg(13, 8), seg(14, 8), seg(15, 8),
               seg(11, SW), seg(7, D), seg(8, D)]
    as2d = lambda a: a.reshape(-1, a.shape[-1])
    small_w = [as2d(a) for a in (b_ada, norm1_w, conv_w, conv_b, dt_bias, a_log, d_skip, attn_sinks, ssm_norm_w,
                                 norm2_w, final_norm_w)]
    small_m = [as2d(a) for a in (m_b_ada, m_norm1_w, m_conv_w, m_conv_b, m_dt_bias, m_a_log, m_d_skip, m_attn_sinks,
                                 m_ssm_norm_w, m_norm2_w, m_final_norm_w)]
    small_v = [as2d(a) for a in (v_b_ada, v_norm1_w, v_conv_w, v_conv_b, v_dt_bias, v_a_log, v_d_skip, v_attn_sinks,
                                 v_ssm_norm_w, v_norm2_w, v_final_norm_w)]
    small_g, sd, smn, svn = _adam_small(tot, small_g, small_w, small_m, small_v)
    g_ada, d_ada, m_ada, v_ada = _adam_w_ada(gat, allv, s_arr, w_ada[0], m_w_ada[0], v_w_ada[0], 256)

    order = ["w_ada", "b_ada", "norm1_w", "w_in", "conv_w", "conv_b", "dt_bias", "a_log", "d_skip", "attn_sinks",
             "ssm_norm_w", "w_out", "norm2_w", "w_gate_up", "w_down", "final_norm_w"]
    shapes = dict(w_ada=w_ada.shape, b_ada=b_ada.shape, norm1_w=norm1_w.shape, w_in=w_in.shape, conv_w=conv_w.shape,
                  conv_b=conv_b.shape, dt_bias=dt_bias.shape, a_log=a_log.shape, d_skip=d_skip.shape,
                  attn_sinks=attn_sinks.shape, ssm_norm_w=ssm_norm_w.shape, w_out=w_out.shape, norm2_w=norm2_w.shape,
                  w_gate_up=w_gate_up.shape, w_down=w_down.shape, final_norm_w=final_norm_w.shape)
    grads = dict(w_ada=g_ada, w_in=g_in_s, w_out=g_out_s, w_gate_up=g_gu_s, w_down=g_dn_s)
    deltas = dict(w_ada=d_ada, w_in=d_in, w_out=d_out, w_gate_up=d_gu, w_down=d_dn)
    new_m = dict(w_ada=m_ada, w_in=m_in, w_out=m_out, w_gate_up=m_gu, w_down=m_dn)
    new_v = dict(w_ada=v_ada, w_in=v_in, w_out=v_out, w_gate_up=v_gu, w_down=v_dn)
    for i, nme in enumerate(small_names):
        grads[nme], deltas[nme], new_m[nme], new_v[nme] = small_g[i], sd[i], smn[i], svn[i]
    outs = [loss, grad_x[None]]
    for table in (grads, deltas, new_m, new_v):
        outs += [table[nme].reshape(shapes[nme]) for nme in order]
    return tuple(outs)
```

```python
import functools
import math

import jax
import jax.numpy as jnp
from jax import lax
from jax.experimental import pallas as pl
from jax.experimental.pallas import tpu as pltpu

F32 = jnp.float32
BF16 = jnp.bfloat16
HI = lax.Precision.HIGHEST
MESH = pl.DeviceIdType.MESH

D = 1024
HD = 64
AW = 512
SW = 512
NST = 128
CONVK = 4
CONVC = 1024
BLK = 128
CPS = 4
SSD_FWD_CPS = 8
ATTN_BPS = 8
IN_PROJ = 2312
IN_PAD = 2432
IN_SH = IN_PROJ // 4
DFF = 2816
GU_SH = 1408
FF_SPLITS = ((0, 1536), (1536, 2816))
EPS = 1e-6
NEG = -1e30
LR, B1, B2, AEPS, WD, STEP = 0.001, 0.9, 0.999, 1e-08, 0.01, 10
VMEM_LIMIT = 58 * 1024 * 1024


def _cp(*sem):
    return pltpu.CompilerParams(dimension_semantics=sem or None, vmem_limit_bytes=VMEM_LIMIT)


def _dot(a, b):
    return jnp.dot(a, b, preferred_element_type=F32)


def _dot_nt(a, b):
    return lax.dot_general(a, b, (((1,), (1,)), ((), ())), preferred_element_type=F32)


def _dot_tn(a, b):
    return lax.dot_general(a, b, (((0,), (0,)), ((), ())), preferred_element_type=F32)


def _dot_hi(a, b):
    return jnp.dot(a, b, precision=HI, preferred_element_type=F32)


def _sigmoid(x):
    return 1.0 / (1.0 + jnp.exp(-x))


def _iota(shape, dim):
    return lax.broadcasted_iota(jnp.int32, shape, dim)


def _load_resident(hbm_ref, vmem_ref, sem):
    @pl.when(pl.program_id(0) == 0)
    def _():
        cp = pltpu.make_async_copy(hbm_ref, vmem_ref, sem)
        cp.start()
        cp.wait()


def _swap32(t):
    lane = _iota(t.shape, 1)
    return jnp.where((lane & 63) < 32, pltpu.roll(t, 96, 1), pltpu.roll(t, 32, 1))


def _rope_fwd(t, cos, sin_s):
    return t * cos + _swap32(t) * sin_s


def _rope_bwd(t, cos, sin_s):
    return t * cos - _swap32(t) * sin_s


DEP_SPEC = pl.BlockSpec((8, 128), lambda *_: (0, 0))


def _rope_tables(pos_row, inv_freq_col, tm, dep):
    T = pos_row.shape[1]
    lane, row = jnp.arange(128)[None, :], jnp.arange(96)[:, None]
    pick = (lane % 32) == (row % 32)
    sel_cos = pick.astype(BF16)
    sel_sin = jnp.where(pick, jnp.where(lane % 64 < 32, -1.0, 1.0), 0.0).astype(BF16)

    def body(p_ref, f_ref, sc_ref, ss_ref, dep_ref, cos_ref, sin_ref):
        ang = f_ref[...] * p_ref[...].astype(F32)
        cos_ref[...] = _dot_tn(_pieces(jnp.cos(ang), 3, 0), sc_ref[...])
        sin_ref[...] = _dot_tn(_pieces(jnp.sin(ang), 3, 0), ss_ref[...])

    full = lambda a: pl.BlockSpec(a.shape, lambda i: (0,) * a.ndim)
    return pl.pallas_call(
        body, name="rope_tables", grid=(T // tm,),
        in_specs=[pl.BlockSpec((1, tm), lambda i: (0, i)), full(inv_freq_col), full(sel_cos), full(sel_sin), DEP_SPEC],
        out_specs=[pl.BlockSpec((tm, 128), lambda i: (i, 0))] * 2,
        out_shape=[jax.ShapeDtypeStruct((T, 128), F32)] * 2,
        compiler_params=_cp("parallel"),
    )(pos_row, inv_freq_col, sel_cos, sel_sin, dep)


def _in_proj_fwd(x, cos, sin_s, mod6, norm1_w, w_pad, tm, dep):
    T = x.shape[0]

    def body(x_ref, cos_ref, sin_ref, mod_ref, nw_ref, w_hbm, dep_ref, qkv_ref, z_ref, xbc_ref, dt_ref, h_ref, w_vmem,
             sem):
        _load_resident(w_hbm, w_vmem, sem)
        xv = x_ref[...]
        r = lax.rsqrt(jnp.mean(xv * xv, axis=-1, keepdims=True) + EPS)
        h = (xv * r * nw_ref[...]) * (1.0 + mod_ref[1:2, :]) + mod_ref[0:1, :]
        hb = h.astype(BF16)
        h_ref[...] = hb
        proj = _dot(hb, w_vmem[...])
        cs, sn = cos_ref[...], sin_ref[...]
        for j in range(5):
            qkv_ref[:, 128 * j:128 * (j + 1)] = _rope_fwd(proj[:, 128 * j:128 * (j + 1)], cs, sn).astype(BF16)
        qkv_ref[:, 640:768] = proj[:, 640:768].astype(BF16)
        z_ref[...] = proj[:, 768:1280]
        xbc_ref[...] = proj[:, 1280:2304]
        dt_ref[...] = proj[:, 2304:2432]

    row = lambda w: pl.BlockSpec((tm, w), lambda i: (i, 0))
    full = lambda a: pl.BlockSpec(a.shape, lambda i: (0,) * a.ndim)
    return pl.pallas_call(
        body, name="in_proj_fwd", grid=(T // tm,),
        in_specs=[row(D), row(128), row(128), full(mod6), full(norm1_w), pl.BlockSpec(memory_space=pl.ANY), DEP_SPEC],
        out_specs=[row(768), row(512), row(1024), row(128), row(D)],
        out_shape=[jax.ShapeDtypeStruct((T, 768), BF16), jax.ShapeDtypeStruct((T, 512), F32),
                   jax.ShapeDtypeStruct((T, 1024), F32), jax.ShapeDtypeStruct((T, 128), F32),
                   jax.ShapeDtypeStruct((T, D), BF16)],
        scratch_shapes=[pltpu.VMEM((D, IN_PAD), BF16), pltpu.SemaphoreType.DMA],
        compiler_params=_cp("arbitrary"),
    )(x, cos, sin_s, mod6, norm1_w, w_pad, dep)


def _head_variants(pair, j):
    lane = _iota(pair.shape, 1)
    lo = lane < 64
    kv = j // 2
    ev = jnp.where(lo, pair, 0.0)
    od = jnp.where(lo, 0.0, pair)
    if kv == 0:
        od = pltpu.roll(od, 64, 1)
    else:
        ev = pltpu.roll(ev, 64, 1)
    return ev.astype(BF16), od.astype(BF16)


def _kv_variants(vcat):
    lane = _iota(vcat.shape, 1)
    lo = lane < 64
    v0 = jnp.where(lo, vcat, 0.0)
    v1 = jnp.where(lo, 0.0, vcat)
    out = {
        (0, 0): v0, (0, 1): pltpu.roll(v0, 64, 1),
        (1, 0): pltpu.roll(v1, 64, 1), (1, 1): v1,
    }
    return {k: v.astype(BF16) for k, v in out.items()}


def _fold_masks(n):
    upper = _iota((BLK, BLK), 1) > _iota((BLK, BLK), 0)
    return upper, upper & (n == 0)


def _attn_fwd(qkv, sinks):
    CPS = ATTN_BPS
    T = qkv.shape[0]
    nsteps = T // (CPS * BLK)

    def body(sink_ref, q_ref, kc_ref, kp_ref, vc_ref, vp_ref, o_ref, lse_ref):
        for sub in range(CPS):
            rows, before = slice(BLK * sub, BLK * (sub + 1)), slice(BLK * (sub - 1), BLK * sub)
            block(pl.program_id(0) * CPS + sub, sink_ref, q_ref.at[rows, :], kc_ref.at[rows, :],
                  kp_ref if sub == 0 else kc_ref.at[before, :], vc_ref.at[rows, :],
                  vp_ref if sub == 0 else vc_ref.at[before, :], o_ref.at[rows, :], lse_ref.at[rows, :])

    def block(n, sink_ref, q_ref, kc_ref, kp_ref, vc_ref, vp_ref, o_ref, lse_ref):
        vpv = _kv_variants(vp_ref[...].astype(F32))
        vcv = _kv_variants(vc_ref[...].astype(F32))
        q_all = jnp.concatenate(
            [v for j in range(4) for v in _head_variants(q_ref[:, 128 * j:128 * (j + 1)].astype(F32), j)], axis=0)
        s_prev = _dot_nt(q_all, kp_ref[...])
        s_cur = _dot_nt(q_all, kc_ref[...])
        upper, dead = _fold_masks(n)
        lane = _iota((BLK, 128), 1)
        lse_acc = jnp.zeros((BLK, 128), F32)
        for jj in range(4):
            acc = jnp.zeros((BLK, 128), F32)
            for par in range(2):
                h = 2 * jj + par
                rows = slice(h * BLK, (h + 1) * BLK)
                sink = sink_ref[0, h]
                s = jnp.where(dead, NEG, jnp.where(upper, s_prev[rows], s_cur[rows]) * 0.125)
                m = jnp.maximum(jnp.max(s, axis=1, keepdims=True), sink)
                p = jnp.exp(s - m)
                den = jnp.sum(p, axis=1, keepdims=True) + jnp.exp(sink - m)
                pn = p * (1.0 / den)
                acc = (acc + _dot(jnp.where(upper, pn, 0.0).astype(BF16), vpv[(jj // 2, par)])
                       + _dot(jnp.where(upper, 0.0, pn).astype(BF16), vcv[(jj // 2, par)]))
                lse_acc = jnp.where(lane == h, m + jnp.log(den), lse_acc)
            o_ref[:, 128 * jj:128 * (jj + 1)] = acc.astype(BF16)
        lse_ref[...] = lse_acc

    RB = CPS * BLK
    prev = lambda n: jnp.maximum(n * CPS - 1, 0)
    return pl.pallas_call(
        body, name="attn_fwd", grid=(nsteps,),
        in_specs=[pl.BlockSpec(memory_space=pltpu.SMEM),
                  pl.BlockSpec((RB, 512), lambda n: (n, 0)),
                  pl.BlockSpec((RB, 128), lambda n: (n, 4)),
                  pl.BlockSpec((BLK, 128), lambda n: (prev(n), 4)),
                  pl.BlockSpec((RB, 128), lambda n: (n, 5)),
                  pl.BlockSpec((BLK, 128), lambda n: (prev(n), 5))],
        out_specs=[pl.BlockSpec((RB, 512), lambda n: (n, 0)), pl.BlockSpec((RB, 128), lambda n: (n, 0))],
        out_shape=[jax.ShapeDtypeStruct((T, 512), BF16), jax.ShapeDtypeStruct((T, 128), F32)],
        compiler_params=_cp("parallel"),
    )(sinks, qkv, qkv, qkv, qkv, qkv)


def _attn_bwd(qkv, sinks, lse, dmix, cos, sin_s, dep):
    T = qkv.shape[0]
    nb = T // BLK

    def body(sink_ref, q_ref, kc_ref, kp_ref, vc_ref, vp_ref, lse_ref, do_ref, cq_ref, sq_ref, ck_ref, sk_ref,
             dep_ref, out_ref, ds_ref, dq_car, dk_car, dv_car):
        n = pl.program_id(0)
        lane = _iota((BLK, 128), 1)

        @pl.when(n == 0)
        def _():
            ds_ref[...] = jnp.zeros_like(ds_ref)
            dq_car[...] = jnp.zeros_like(dq_car)
            dk_car[...] = jnp.zeros_like(dk_car)
            dv_car[...] = jnp.zeros_like(dv_car)

        @pl.when(n < nb)
        def _():
            kp, kc, vp, vc = kp_ref[...], kc_ref[...], vp_ref[...], vc_ref[...]
            kpv = _kv_variants(kp.astype(F32))
            kcv = _kv_variants(kc.astype(F32))
            lse_v = lse_ref[...]
            q_all = jnp.concatenate(
                [v for j in range(4) for v in _head_variants(q_ref[:, 128 * j:128 * (j + 1)].astype(F32), j)], axis=0)
            do_all = jnp.concatenate(
                [v for j in range(4) for v in _head_variants(do_ref[:, 128 * j:128 * (j + 1)], j)], axis=0)
            s_prev, s_cur = _dot_nt(q_all, kp), _dot_nt(q_all, kc)
            dp_prev, dp_cur = _dot_nt(do_all, vp), _dot_nt(do_all, vc)
            upper, dead = _fold_masks(n)
            out_ref[:, 0:512] = dq_car[...]
            dsk = jnp.zeros((1, 128), F32)
            ds_u, ds_l, p_u, p_l = [], [], [], []
            for jj in range(4):
                dq_acc = jnp.zeros((BLK, 128), F32)
                for par in range(2):
                    h = 2 * jj + par
                    rows = slice(h * BLK, (h + 1) * BLK)
                    lse_h = jnp.sum(jnp.where(lane == h, lse_v, 0.0), axis=1, keepdims=True)
                    s = jnp.where(dead, NEG, jnp.where(upper, s_prev[rows], s_cur[rows]) * 0.125)
                    p = jnp.exp(s - lse_h)
                    dp = jnp.where(upper, dp_prev[rows], dp_cur[rows])
                    delta = jnp.sum(p * dp, axis=1, keepdims=True)
                    ds = p * (dp - delta) * 0.125
                    dsu, dsl = jnp.where(upper, ds, 0.0).astype(BF16), jnp.where(upper, 0.0, ds).astype(BF16)
                    dq_acc = dq_acc + _dot(dsu, kpv[(jj // 2, par)]) + _dot(dsl, kcv[(jj // 2, par)])
                    ds_u.append(dsu)
                    ds_l.append(dsl)
                    p_u.append(jnp.where(upper, p, 0.0).astype(BF16))
                    p_l.append(jnp.where(upper, 0.0, p).astype(BF16))
                    dsk = dsk + jnp.where(lane[0:1] == h, -jnp.sum(jnp.exp(sink_ref[0, h] - lse_h) * delta), 0.0)
                dq_car[:, 128 * jj:128 * (jj + 1)] = _rope_bwd(dq_acc, cq_ref[...], sq_ref[...]).astype(BF16)
            stack = lambda parts: jnp.concatenate(parts, axis=0)
            dk_prev, dk_cur = _dot_tn(stack(ds_u), q_all), _dot_tn(stack(ds_l), q_all)
            dv_prev, dv_cur = _dot_tn(stack(p_u), do_all), _dot_tn(stack(p_l), do_all)
            ds_ref[...] += dsk
            out_ref[:, 512:640] = _rope_bwd(dk_car[...] + dk_prev, ck_ref[...], sk_ref[...]).astype(BF16)
            out_ref[:, 640:768] = (dv_car[...] + dv_prev).astype(BF16)
            dk_car[...] = dk_cur
            dv_car[...] = dv_cur

        @pl.when(n == nb)
        def _():
            out_ref[:, 0:512] = dq_car[...]
            out_ref[:, 512:640] = _rope_bwd(dk_car[...], ck_ref[...], sk_ref[...]).astype(BF16)
            out_ref[:, 640:768] = dv_car[...].astype(BF16)

    cur = lambda n: jnp.minimum(n, nb - 1)
    prev = lambda n: jnp.maximum(cur(n) - 1, 0)
    outb = lambda n: jnp.maximum(n - 1, 0)
    return pl.pallas_call(
        body, name="attn_bwd", grid=(nb + 1,),
        in_specs=[pl.BlockSpec(memory_space=pltpu.SMEM),
                  pl.BlockSpec((BLK, 512), lambda n: (cur(n), 0)),
                  pl.BlockSpec((BLK, 128), lambda n: (cur(n), 4)),
                  pl.BlockSpec((BLK, 128), lambda n: (prev(n), 4)),
                  pl.BlockSpec((BLK, 128), lambda n: (cur(n), 5)),
                  pl.BlockSpec((BLK, 128), lambda n: (prev(n), 5)),
                  pl.BlockSpec((BLK, 128), lambda n: (cur(n), 0)),
                  pl.BlockSpec((BLK, 512), lambda n: (cur(n), 0)),
                  pl.BlockSpec((BLK, 128), lambda n: (cur(n), 0)),
                  pl.BlockSpec((BLK, 128), lambda n: (cur(n), 0)),
                  pl.BlockSpec((BLK, 128), lambda n: (outb(n), 0)),
                  pl.BlockSpec((BLK, 128), lambda n: (outb(n), 0)), DEP_SPEC],
        out_specs=[pl.BlockSpec((BLK, 768), lambda n: (outb(n), 0)), pl.BlockSpec((1, 128), lambda n: (0, 0))],
        out_shape=[jax.ShapeDtypeStruct((T, 768), BF16), jax.ShapeDtypeStruct((1, 128), F32)],
        scratch_shapes=[pltpu.VMEM((BLK, 512), BF16), pltpu.VMEM((BLK, 128), F32), pltpu.VMEM((BLK, 128), F32)],
        compiler_params=_cp("arbitrary"),
    )(sinks, qkv, qkv, qkv, qkv, qkv, lse, dmix, cos, sin_s, cos, sin_s, dep)


def _ssd_mats():
    e = jnp.arange(SW)[None, :] // HD == jnp.arange(128)[:, None]
    tri = jnp.arange(BLK)[None, :] <= jnp.arange(BLK)[:, None]
    return (jnp.tile(e, (3, 1)).astype(BF16), jnp.tile(e.T, (2, 1)).astype(BF16),
            jnp.tile(tri, (1, 3)).astype(BF16), jnp.tile(tri.T, (1, 3)).astype(BF16))


def _pieces(x, n, axis):
    out, r = [], x
    for i in range(n):
        p = r.astype(BF16)
        out.append(p)
        if i + 1 < n:
            r = r - p.astype(F32)
    return jnp.concatenate(out, axis=axis)


def _expand(x, e3):
    return _dot(_pieces(x, 3, 1), e3)


def _head_sums(x, et2):
    return _dot(_pieces(x, 2, 1), et2)


def _run_sum(tri3, x):
    return _dot(tri3, _pieces(x, 3, 0))


def _shift_down(u, tail, j):
    rolled = pltpu.roll(u, j, 0)
    first = jnp.where(_iota(tail.shape, 0) < j, pltpu.roll(tail, j, 0), rolled[0:8])
    return jnp.concatenate([first, rolled[8:]], axis=0)


def _shift_up(d, head, j):
    rolled = pltpu.roll(d, BLK - j, 0)
    last = jnp.where(_iota(head.shape, 0) >= 8 - j, pltpu.roll(head, 8 - j, 0), rolled[BLK - 8:])
    return jnp.concatenate([rolled[:BLK - 8], last], axis=0)


def _ssd_parts(dtr, dtb, alog, e3, tril3):
    xx = dtr + dtb
    dt = jnp.maximum(xx, 0.0) + jnp.log(1.0 + jnp.exp(-jnp.abs(xx)))
    a_neg = -jnp.exp(alog)
    tril = _iota((BLK, BLK), 1) <= _iota((BLK, BLK), 0)
    cs = _run_sum(tril3, dt * a_neg)
    csx = _expand(cs, e3)
    last = csx[BLK - 1:BLK, :]
    return dict(xx=xx, dt=dt, a_neg=a_neg, tril=tril, cs=cs, cs_t=cs.T,
                ecsx=jnp.exp(csx), dtex=jnp.exp(last - csx), cdx=jnp.exp(last), dtx=_expand(dt, e3))


def _decay(parts, h):
    seg = parts["cs"][:, h:h + 1] - parts["cs_t"][h:h + 1, :]
    return jnp.exp(jnp.where(parts["tril"], seg, NEG))


def _group_cols(a, g):
    return a[:, 256 * g:256 * (g + 1)]


def _ssd_fwd(xbc, z, dtr, conv_w, conv_b, dtb, alog, dskx, ssm_w, mats, dep):
    CPS = SSD_FWD_CPS
    T = xbc.shape[0]
    nc = T // BLK

    def body(u_ref, tail_ref, z_ref, dtr_ref, cw_ref, cb_ref, dtb_ref, al_ref, dk_ref, sw_ref, e3_ref, tril3_ref,
             dep_ref, yn_ref, yp_ref, st_ref, co_ref, s_scr):
        n = pl.program_id(0)

        @pl.when(n == 0)
        def _():
            s_scr[...] = jnp.zeros_like(s_scr)

        lane = _iota((BLK, 128), 1)
        lo = lane < 64
        for sub in range(CPS):
            rows = slice(BLK * sub, BLK * (sub + 1))
            u = u_ref[rows, :]
            tail = jnp.where(n > 0, tail_ref[...], 0.0) if sub == 0 else u_ref[BLK * sub - 8:BLK * sub, :]
            co = cb_ref[...] + cw_ref[3:4, :] * u
            for j in range(1, CONVK):
                co = co + cw_ref[3 - j:4 - j, :] * _shift_down(u, tail, j)
            co_ref[rows, :] = co
            xc = co * _sigmoid(co)
            pt = _ssd_parts(dtr_ref[rows, :], dtb_ref[...], al_ref[...], e3_ref[...], tril3_ref[...])
            xs = xc[:, :SW]
            bm = [xc[:, 512:640].astype(BF16), xc[:, 640:768].astype(BF16)]
            cm = [xc[:, 768:896].astype(BF16), xc[:, 896:1024].astype(BF16)]
            s_in = s_scr[...]
            st_ref[sub] = s_in
            xdt = xs * pt["dtx"]
            xde = (xdt * pt["dtex"]).astype(BF16)
            ys, s_new = [], []
            for g in range(2):
                cb = _dot_nt(cm[g], bm[g])
                yoff = _dot(cm[g], _group_cols(s_in, g).astype(BF16))
                s_new.append(_dot_tn(bm[g], _group_cols(xde, g)))
                for jj in range(2):
                    j = 2 * g + jj
                    chunk = xdt[:, 128 * j:128 * (j + 1)]
                    g_ev = (cb * _decay(pt, 2 * j)).astype(BF16)
                    g_od = (cb * _decay(pt, 2 * j + 1)).astype(BF16)
                    yd = (_dot(g_ev, jnp.where(lo, chunk, 0.0).astype(BF16))
                          + _dot(g_od, jnp.where(lo, 0.0, chunk).astype(BF16)))
                    ys.append(yd + yoff[:, 128 * jj:128 * (jj + 1)] * pt["ecsx"][:, 128 * j:128 * (j + 1)])
            y = jnp.concatenate(ys, axis=1) + xs * dk_ref[...]
            s_scr[...] = s_in * pt["cdx"] + jnp.concatenate(s_new, axis=1)
            yp_ref[rows, :] = y
            zv = z_ref[rows, :]
            yz = y * (zv * _sigmoid(zv))
            outs = []
            for g in range(2):
                yg = _group_cols(yz, g)
                outs.append(yg * lax.rsqrt(jnp.mean(yg * yg, axis=-1, keepdims=True) + EPS))
            yn_ref[rows, :] = (jnp.concatenate(outs, axis=1) * sw_ref[...]).astype(BF16)

    e3, _, tril3, _ = mats
    RB = CPS * BLK
    tail8 = lambda n: jnp.maximum(n * (RB // 8) - 1, 0)
    full = lambda a: pl.BlockSpec(a.shape, lambda n: (0,) * a.ndim)
    return pl.pallas_call(
        body, name="ssd_fwd", grid=(nc // CPS,),
        in_specs=[pl.BlockSpec((RB, CONVC), lambda n: (n, 0)), pl.BlockSpec((8, CONVC), lambda n: (tail8(n), 0)),
                  pl.BlockSpec((RB, SW), lambda n: (n, 0)), pl.BlockSpec((RB, 128), lambda n: (n, 0)),
                  full(conv_w), full(conv_b), full(dtb), full(alog), full(dskx), full(ssm_w), full(e3), full(tril3),
                  DEP_SPEC],
        out_specs=[pl.BlockSpec((RB, SW), lambda n: (n, 0)), pl.BlockSpec((RB, SW), lambda n: (n, 0)),
                   pl.BlockSpec((CPS, NST, SW), lambda n: (n, 0, 0)), pl.BlockSpec((RB, CONVC), lambda n: (n, 0))],
        out_shape=[jax.ShapeDtypeStruct((T, SW), BF16), jax.ShapeDtypeStruct((T, SW), F32),
                   jax.ShapeDtypeStruct((nc, NST, SW), F32), jax.ShapeDtypeStruct((T, CONVC), F32)],
        scratch_shapes=[pltpu.VMEM((NST, SW), F32)],
        compiler_params=_cp("arbitrary"),
    )(xbc, xbc, z, dtr, conv_w, conv_b, dtb, alog, dskx, ssm_w, e3, tril3, dep)


def _ssd_bwd(xbc, co_all, z, dtr, ypre, states, dmix, conv_w, dtb, alog, dskx, ssm_w, mats, dep):
    T = xbc.shape[0]
    nsteps = T // (CPS * BLK)

    def body(*refs):
        per_chunk, consts, out_ref, carried = refs[:7], refs[7:16], refs[17], refs[18:]
        i = pl.program_id(0)

        @pl.when(i == 0)
        def _():
            for r in carried:
                r[...] = jnp.zeros_like(r)

        for sub in reversed(range(CPS)):
            rows = slice(BLK * sub, BLK * (sub + 1))
            views = [r.at[sub:sub + 1] if k == 5 else r.at[rows, :] for k, r in enumerate(per_chunk)]
            chunk(*views, *consts, out_ref.at[rows, :], *carried)

        @pl.when(i == nsteps - 1)
        def _():
            dsk_ref, dskx_scr = carried[3], carried[8]
            dsk_ref[...] = _head_sums(jnp.broadcast_to(dskx_scr[...], (8, SW)), consts[6][...])[0:1]

    def chunk(u_ref, co_ref, z_ref, dtr_ref, yp_ref, st_ref, dyn_ref, cw_ref, dtb_ref, al_ref, dk_ref, sw_ref,
              e3_ref, et2_ref, tril3_ref, triu3_ref,
              out_ref, dcw_ref, dcb_ref, dsw_ref, dsk_ref, ddtb_ref, dav_ref, ds_scr, dco_scr, dskx_scr):
        co = co_ref[...]
        sg = _sigmoid(co)
        xc = co * sg
        pt = _ssd_parts(dtr_ref[...], dtb_ref[...], al_ref[...], e3_ref[...], tril3_ref[...])
        dtx, ecsx, dtex, cdx = pt["dtx"], pt["ecsx"], pt["dtex"], pt["cdx"]
        xs = xc[:, :SW]
        bm = [xc[:, 512:640].astype(BF16), xc[:, 640:768].astype(BF16)]
        cm = [xc[:, 768:896].astype(BF16), xc[:, 896:1024].astype(BF16)]
        s_in = st_ref[0]
        ds_out = ds_scr[...]
        e_t = et2_ref[...]

        zv = z_ref[...]
        sz = _sigmoid(zv)
        silu_z = zv * sz
        ypre = yp_ref[...]
        yz = ypre * silu_z
        dyn = dyn_ref[...]
        sw = sw_ref[...]
        dyz, yns = [], []
        for g in range(2):
            yg = _group_cols(yz, g)
            r = lax.rsqrt(jnp.mean(yg * yg, axis=-1, keepdims=True) + EPS)
            yn = yg * r
            dg = _group_cols(dyn, g) * _group_cols(sw, g)
            dyz.append(r * (dg - yn * jnp.mean(dg * yn, axis=-1, keepdims=True)))
            yns.append(yn)
        dyz = jnp.concatenate(dyz, axis=1)
        dsw_ref[...] += jnp.sum(dyn * jnp.concatenate(yns, axis=1), axis=0, keepdims=True)
        dy = dyz * silu_z
        dz = dyz * ypre * (sz * (1.0 + zv * (1.0 - sz)))

        xdt = xs * dtx
        xdt_b = xdt.astype(BF16)
        edy = (ecsx * dy).astype(BF16)
        xde = (xdt * dtex).astype(BF16)
        lane = _iota((BLK, 128), 1)
        lo = lane < 64
        row8 = _iota((8, 128), 0)
        dcs = jnp.zeros((BLK, 128), F32)
        col_rows = jnp.zeros((8, 128), F32)
        dxdt, bds, yoff, dbs, dcs_g, ds_new = [], [], [], [], [], []
        for g in range(2):
            s_g = _group_cols(s_in, g).astype(BF16)
            dso_g = _group_cols(ds_out, g).astype(BF16)
            cb = _dot_nt(cm[g], bm[g])
            bds.append(_dot(bm[g], dso_g))
            yoff.append(_dot(cm[g], s_g))
            dcb_g = jnp.zeros((BLK, BLK), F32)
            for jj in range(2):
                j = 2 * g + jj
                dy_c = dy[:, 128 * j:128 * (j + 1)]
                xdt_c = xdt_b[:, 128 * j:128 * (j + 1)]
                acc = jnp.zeros((BLK, 128), F32)
                for par in range(2):
                    h = 2 * j + par
                    lm = _decay(pt, h)
                    gm = cb * lm
                    dy_m = (jnp.where(lo, dy_c, 0.0) if par == 0 else jnp.where(lo, 0.0, dy_c)).astype(BF16)
                    dg_h = _dot_nt(dy_m, xdt_c)
                    w_h = dg_h * gm
                    dcs = dcs + jnp.where(lane == h, jnp.sum(w_h, axis=1, keepdims=True), 0.0)
                    col_rows = col_rows + jnp.where(row8 == h, jnp.sum(w_h, axis=0, keepdims=True), 0.0)
                    dcb_g = dcb_g + dg_h * lm
                    acc = acc + _dot_tn(gm.astype(BF16), dy_m)
                dxdt.append(acc)
            dcb_b = dcb_g.astype(BF16)
            dcs_g.append(_dot(dcb_b, bm[g]) + _dot_nt(_group_cols(edy, g), s_g))
            dbs.append(_dot_tn(dcb_b, cm[g]) + _dot_nt(_group_cols(xde, g), dso_g))
            ds_new.append(_dot_tn(cm[g], _group_cols(edy, g)))
        bds = jnp.concatenate(bds, axis=1)
        yoff = jnp.concatenate(yoff, axis=1) * ecsx
        dxdt = jnp.concatenate(dxdt, axis=1) + dtex * bds
        ds_scr[...] = cdx * ds_out + jnp.concatenate(ds_new, axis=1)

        t_m = _head_sums(dtex * xdt * bds, e_t)
        colsum_t = jnp.concatenate([col_rows, jnp.zeros((BLK - 8, 128), F32)], axis=0).T
        cd = jnp.exp(pt["cs"][BLK - 1:BLK, :])
        sds = jnp.sum(s_in * ds_out, axis=0, keepdims=True)
        last_row = jnp.sum(t_m, axis=0, keepdims=True) + cd * _head_sums(jnp.broadcast_to(sds, (8, SW)), e_t)[0:1]
        dcs = dcs - colsum_t + _head_sums(dy * yoff, e_t) - t_m
        dcs = dcs + jnp.where(_iota((BLK, 128), 0) == BLK - 1, last_row, 0.0)
        da = _run_sum(triu3_ref[...], dcs)
        dt = pt["dt"]
        ddt = da * pt["a_neg"] + _head_sums(dxdt * xs, e_t)
        dav_ref[...] += jnp.sum(da * dt, axis=0, keepdims=True)
        ddtr = ddt * _sigmoid(pt["xx"])
        ddtb_ref[...] += jnp.sum(ddtr, axis=0, keepdims=True)
        dxs = dxdt * dtx + dy * dk_ref[...]
        dskx_scr[...] += jnp.sum(dy * xs, axis=0, keepdims=True)
        dxc = jnp.concatenate([dxs, dbs[0], dbs[1], dcs_g[0], dcs_g[1]], axis=1)
        dco = dxc * (sg * (1.0 + co * (1.0 - sg)))

        dcb_ref[...] += jnp.sum(dco, axis=0, keepdims=True)
        u = u_ref[...]
        head = dco_scr[...]
        du = jnp.zeros_like(dco)
        for j in range(CONVK):
            up_j = dco if j == 0 else _shift_up(dco, head, j)
            dcw_ref[3 - j:4 - j, :] += jnp.sum(up_j * u, axis=0, keepdims=True)
            du = du + cw_ref[3 - j:4 - j, :] * up_j
        dco_scr[...] = dco[0:8]
        out_ref[:, 0:512] = dz.astype(BF16)
        out_ref[:, 512:1536] = du.astype(BF16)
        out_ref[:, 1536:1664] = ddtr.astype(BF16)

    e3, et2, tril3, triu3 = mats
    RB = CPS * BLK
    rev = lambda i: nsteps - 1 - i
    full = lambda a: pl.BlockSpec(a.shape, lambda i: (0,) * a.ndim)
    acc = lambda r, c: pl.BlockSpec((r, c), lambda i: (0, 0))
    return pl.pallas_call(
        body, name="ssd_bwd", grid=(nsteps,),
        in_specs=[pl.BlockSpec((RB, CONVC), lambda i: (rev(i), 0)), pl.BlockSpec((RB, CONVC), lambda i: (rev(i), 0)),
                  pl.BlockSpec((RB, SW), lambda i: (rev(i), 0)), pl.BlockSpec((RB, 128), lambda i: (rev(i), 0)),
                  pl.BlockSpec((RB, SW), lambda i: (rev(i), 0)), pl.BlockSpec((CPS, NST, SW), lambda i: (rev(i), 0, 0)),
                  pl.BlockSpec((RB, SW), lambda i: (rev(i), 1)),
                  full(conv_w), full(dtb), full(alog), full(dskx), full(ssm_w),
                  full(e3), full(et2), full(tril3), full(triu3), DEP_SPEC],
        out_specs=[pl.BlockSpec((RB, 1664), lambda i: (rev(i), 0)),
                   acc(CONVK, CONVC), acc(1, CONVC), acc(1, SW), acc(1, 128), acc(1, 128), acc(1, 128)],
        out_shape=[jax.ShapeDtypeStruct((T, 1664), BF16),
                   jax.ShapeDtypeStruct((CONVK, CONVC), F32), jax.ShapeDtypeStruct((1, CONVC), F32),
                   jax.ShapeDtypeStruct((1, SW), F32), jax.ShapeDtypeStruct((1, 128), F32),
                   jax.ShapeDtypeStruct((1, 128), F32), jax.ShapeDtypeStruct((1, 128), F32)],
        scratch_shapes=[pltpu.VMEM((NST, SW), F32), pltpu.VMEM((8, CONVC), F32), pltpu.VMEM((1, SW), F32)],
        compiler_params=_cp("arbitrary"),
    )(xbc, co_all, z, dtr, ypre, states, dmix, conv_w, dtb, alog, dskx, ssm_w, e3, et2, tril3, triu3, dep)


def _mix_ffn(x, attn, ynorm, tgt, mod6, norm2_w, final_w, w_out, w_gu, w_gu_own, s_arr, w_dn, w_out_own, w_dn_own, tm):
    T = x.shape[0]
    nt = T // tm

    def body(x_ref, a_ref, y_ref, t_ref, mod_ref, n2_ref, fw_ref, wo_hbm, wgu_hbm, own_hbm, s_ref, wdn_hbm, wo_own, wdn_own,
             sq_ref, dmix_ref, dx1_ref, h2_ref, act_ref, df_ref, dgu_ref, do_ref, sm_ref,
             wo, wgu, wdn, sems):
        i = pl.program_id(0)

        @pl.when(i == 0)
        def _():
            cps = [pltpu.make_async_copy(s, d, sems.at[k]) for k, (s, d) in
                   enumerate(((wo_hbm, wo), (wgu_hbm, wgu), (wdn_hbm, wdn)))]
            for c in cps:
                c.start()
            for c in cps:
                c.wait()
            rows = lambda n: pl.ds(pl.multiple_of(s_ref[0] * n, 16), n)
            owns = [pltpu.make_async_copy(
                        own_hbm, wgu.at[:, pl.ds(pl.multiple_of(s_ref[0] * GU_SH, 128), GU_SH)], sems.at[3]),
                    pltpu.make_async_copy(wo_own, wo.at[rows(D // 4), :], sems.at[4]),
                    pltpu.make_async_copy(wdn_own, wdn.at[rows(DFF // 4), :], sems.at[5])]
            for c in owns:
                c.start()
            for c in owns:
                c.wait()
            sq_ref[...] = jnp.zeros_like(sq_ref)
            sm_ref[...] = jnp.zeros_like(sm_ref)

        gate1, shift2, scale2, gate2 = mod_ref[2:3, :], mod_ref[3:4, :], mod_ref[4:5, :], mod_ref[5:6, :]
        n2w, fw = n2_ref[...], fw_ref[...]
        o = _dot(a_ref[...], wo[0:AW, :]) + _dot(y_ref[...], wo[AW:D, :])
        x1 = x_ref[...] + gate1 * o
        r2 = lax.rsqrt(jnp.mean(x1 * x1, axis=-1, keepdims=True) + EPS)
        xh2 = x1 * r2
        n2 = xh2 * n2w
        h2b = (n2 * (1.0 + scale2) + shift2).astype(BF16)
        h2_ref[...] = h2b
        f = jnp.zeros((tm, D), F32)
        saved = []
        for a, b in FF_SPLITS:
            gp = _dot(h2b, wgu[:, a:b])
            upj = _dot(h2b, wgu[:, DFF + a:DFF + b])
            sg = _sigmoid(gp)
            sl = gp * sg
            actb = (sl * upj).astype(BF16)
            act_ref[:, a:b] = actb
            f = f + _dot(actb, wdn[a:b, :])
            saved.append((gp, upj, sg, sl))
        x2 = x1 + gate2 * f
        r3 = lax.rsqrt(jnp.mean(x2 * x2, axis=-1, keepdims=True) + EPS)
        xh3 = x2 * r3
        err = xh3 * fw - t_ref[...]
        sq_ref[...] += jnp.sum(err * err, axis=0, keepdims=True)
        dy = err * (1.0 / D)
        dfw = jnp.sum(dy * xh3, axis=0, keepdims=True)
        dxh3 = dy * fw
        dx2 = r3 * (dxh3 - xh3 * jnp.mean(dxh3 * xh3, axis=-1, keepdims=True))
        dgate2 = jnp.sum(dx2 * f, axis=0, keepdims=True)
        dfb = (dx2 * gate2).astype(BF16)
        df_ref[...] = dfb
        dh2 = jnp.zeros((tm, D), F32)
        for (a, b), (gp, upj, sg, sl) in zip(FF_SPLITS, saved):
            dact = _dot_nt(dfb, wdn[a:b, :])
            dg = (dact * upj * (sg * (1.0 + gp * (1.0 - sg)))).astype(BF16)
            du = (dact * sl).astype(BF16)
            dgu_ref[:, a:b] = dg
            dgu_ref[:, DFF + a:DFF + b] = du
            dh2 = dh2 + _dot_nt(dg, wgu[:, a:b]) + _dot_nt(du, wgu[:, DFF + a:DFF + b])
        dshift2 = jnp.sum(dh2, axis=0, keepdims=True)
        dscale2 = jnp.sum(dh2 * n2, axis=0, keepdims=True)
        dn2 = dh2 * (1.0 + scale2)
        dn2w = jnp.sum(dn2 * xh2, axis=0, keepdims=True)
        dxh2 = dn2 * n2w
        dx1 = dx2 + r2 * (dxh2 - xh2 * jnp.mean(dxh2 * xh2, axis=-1, keepdims=True))
        dx1_ref[...] = dx1
        dgate1 = jnp.sum(dx1 * o, axis=0, keepdims=True)
        dob = (dx1 * gate1).astype(BF16)
        do_ref[...] = dob
        dmix_ref[...] = _dot_nt(dob, wo[...])
        sm_ref[...] += jnp.concatenate(
            [dfw, dn2w, dshift2, dscale2, dgate2, dgate1, jnp.zeros((2, D), F32)], axis=0)

    row = lambda w: pl.BlockSpec((tm, w), lambda i: (i, 0))
    full = lambda a: pl.BlockSpec(a.shape, lambda i: (0,) * a.ndim)
    anyspec = pl.BlockSpec(memory_space=pl.ANY)
    return pl.pallas_call(
        body, name="mix_ffn", grid=(nt,),
        in_specs=[row(D), row(AW), row(SW), row(D), full(mod6), full(norm2_w), full(final_w), anyspec, anyspec, anyspec,
                  pl.BlockSpec(memory_space=pltpu.SMEM), anyspec, anyspec, anyspec],
        out_specs=[pl.BlockSpec((1, D), lambda i: (0, 0)), row(D), row(D), row(D),
                   row(DFF), row(D), row(2 * DFF), row(D), pl.BlockSpec((8, D), lambda i: (0, 0))],
        out_shape=[jax.ShapeDtypeStruct((1, D), F32), jax.ShapeDtypeStruct((T, D), F32), jax.ShapeDtypeStruct((T, D), F32),
                   jax.ShapeDtypeStruct((T, D), BF16), jax.ShapeDtypeStruct((T, DFF), BF16),
                   jax.ShapeDtypeStruct((T, D), BF16), jax.ShapeDtypeStruct((T, 2 * DFF), BF16),
                   jax.ShapeDtypeStruct((T, D), BF16), jax.ShapeDtypeStruct((8, D), F32)],
        scratch_shapes=[pltpu.VMEM((D, D), BF16), pltpu.VMEM((D, 2 * DFF), BF16), pltpu.VMEM((DFF, D), BF16),
                        pltpu.SemaphoreType.DMA((6,))],
        compiler_params=_cp("arbitrary"),
    )(x, attn, ynorm, tgt, mod6, norm2_w, final_w, w_out, w_gu, w_gu_own, s_arr, w_dn, w_out_own, w_dn_own)


def _in_proj_bwd(x, dx1, dqkv, dzxd, mod6, norm1_w, w_pad, tm, dep):
    T = x.shape[0]

    def body(x_ref, dx1_ref, dq_ref, dz_ref, mod_ref, nw_ref, w_hbm, dep_ref, gx_ref, sm_ref, w_vmem, sem):
        _load_resident(w_hbm, w_vmem, sem)

        @pl.when(pl.program_id(0) == 0)
        def _():
            sm_ref[...] = jnp.zeros_like(sm_ref)

        nw = nw_ref[...]
        scale1 = mod_ref[1:2, :]
        sums = jnp.zeros((8, D), F32)
        for rows in (slice(0, tm // 2), slice(tm // 2, tm)):
            dh = _dot_nt(dq_ref[rows, :], w_vmem[:, 0:768]) + _dot_nt(dz_ref[rows, :], w_vmem[:, 768:IN_PAD])
            xv = x_ref[rows, :]
            r = lax.rsqrt(jnp.mean(xv * xv, axis=-1, keepdims=True) + EPS)
            xh = xv * r
            n1 = xh * nw
            dshift = jnp.sum(dh, axis=0, keepdims=True)
            dscale = jnp.sum(dh * n1, axis=0, keepdims=True)
            dn = dh * (1.0 + scale1)
            dnw = jnp.sum(dn * xh, axis=0, keepdims=True)
            dxh = dn * nw
            gx_ref[rows, :] = dx1_ref[rows, :] + r * (dxh - xh * jnp.mean(dxh * xh, axis=-1, keepdims=True))
            sums = sums + jnp.concatenate([dnw, dshift, dscale, jnp.zeros((5, D), F32)], axis=0)
        sm_ref[...] += sums

    row = lambda w: pl.BlockSpec((tm, w), lambda i: (i, 0))
    full = lambda a: pl.BlockSpec(a.shape, lambda i: (0,) * a.ndim)
    return pl.pallas_call(
        body, name="in_proj_bwd", grid=(T // tm,),
        in_specs=[row(D), row(D), row(768), row(1664), full(mod6), full(norm1_w), pl.BlockSpec(memory_space=pl.ANY),
                  DEP_SPEC],
        out_specs=[row(D), pl.BlockSpec((8, D), lambda i: (0, 0))],
        out_shape=[jax.ShapeDtypeStruct((T, D), F32), jax.ShapeDtypeStruct((8, D), F32)],
        scratch_shapes=[pltpu.VMEM((D, IN_PAD), BF16), pltpu.SemaphoreType.DMA],
        compiler_params=_cp("arbitrary"),
    )(x, dx1, dqkv, dzxd, mod6, norm1_w, w_pad, dep)


def _tn_matmul(a, b, K, N, tt, name, dep):
    T = a.shape[0]
    ja, jb = a.shape[1] // K, b.shape[1] // N
    J = max(ja, jb)

    def body(a_ref, b_ref, dep_ref, o_ref):
        t = pl.program_id(1)
        prod = _dot_tn(a_ref[...], b_ref[...])

        @pl.when(t == 0)
        def _():
            o_ref[0] = prod

        @pl.when(t > 0)
        def _():
            o_ref[0] += prod

    return pl.pallas_call(
        body, name=name, grid=(J, T // tt),
        in_specs=[pl.BlockSpec((tt, K), lambda j, t: (t, j if ja > 1 else 0)),
                  pl.BlockSpec((tt, N), lambda j, t: (t, j if jb > 1 else 0)),
                  pl.BlockSpec((8, 128), lambda j, t: (0, 0))],
        out_specs=pl.BlockSpec((1, K, N), lambda j, t: (j, 0, 0)),
        out_shape=jax.ShapeDtypeStruct((J, K, N), F32),
        compiler_params=_cp("parallel", "arbitrary"),
    )(a, b, dep)


def _accumulate(o_ref, rows, prod):
    @pl.when(pl.program_id(0) == 0)
    def _():
        o_ref[rows, :] = prod

    @pl.when(pl.program_id(0) > 0)
    def _():
        o_ref[rows, :] += prod


def _tn_matmul_rows(a0, a1, b, tt, name, dep):
    T, K = a0.shape
    N = b.shape[1]

    def body(a0_ref, a1_ref, b_ref, dep_ref, o_ref):
        for k, a_ref in enumerate((a0_ref, a1_ref)):
            _accumulate(o_ref, slice(k * K, (k + 1) * K), _dot_tn(a_ref[...], b_ref[...]))

    tile = lambda w: pl.BlockSpec((tt, w), lambda t: (t, 0))
    return pl.pallas_call(
        body, name=name, grid=(T // tt,), in_specs=[tile(K), tile(K), tile(N), DEP_SPEC],
        out_specs=pl.BlockSpec((2 * K, N), lambda t: (0, 0)), out_shape=jax.ShapeDtypeStruct((2 * K, N), F32),
        compiler_params=_cp("arbitrary"),
    )(a0, a1, b, dep)


def _adam_math(w, g, m, v):
    m = B1 * m + (1.0 - B1) * g
    v = B2 * v + (1.0 - B2) * (g * g)
    m_hat = m / (1.0 - B1 ** STEP)
    v_hat = v / (1.0 - B2 ** STEP)
    delta = -LR * (m_hat / (jnp.sqrt(v_hat) + AEPS) + WD * w)
    return delta, m, v


def _adam_2d(w, mine, land, m, v, c_arr, rb, name, dep):
    R, C = w.shape
    nbh = R // 2 // rb

    def body(c_ref, w_ref, mine_ref, land_ref, m_ref, v_ref, dep_ref, go_ref, d_ref, mo_ref, vo_ref):
        g = jnp.where(pl.program_id(0) // nbh == c_ref[0], mine_ref[...], land_ref[...])
        d, mn, vn = _adam_math(w_ref[...], g, m_ref[...], v_ref[...])
        go_ref[...] = g
        d_ref[...] = d
        mo_ref[...] = mn
        vo_ref[...] = vn

    spec = pl.BlockSpec((rb, C), lambda i, c_ref: (i, 0))
    mine_spec = pl.BlockSpec((rb, C), lambda i, c_ref: (jnp.clip(i - c_ref[0] * nbh, 0, nbh - 1), 0))
    return pl.pallas_call(
        body, name=name,
        grid_spec=pltpu.PrefetchScalarGridSpec(
            num_scalar_prefetch=1, grid=(R // rb,), in_specs=[spec, mine_spec, spec, spec, spec, DEP_SPEC],
            out_specs=[spec] * 4),
        out_shape=[jax.ShapeDtypeStruct((R, C), F32)] * 4, compiler_params=_cp("parallel"),
    )(c_arr, w, mine, land, m, v, dep)


def _adam_w_in(w3, grad, m3, v3):
    n = w3.shape[0]

    def body(w_hbm, grad_ref, m_hbm, v_hbm, g_hbm, d_hbm, mo_hbm, vo_hbm, bufs, sems):
        ins = [pltpu.make_async_copy(src.at[:, 0], bufs.at[k], sems.at[k]) for k, src in enumerate((w_hbm, m_hbm, v_hbm))]
        for cp in ins:
            cp.start()
        g = grad_ref[...]
        eye =(_iota((D, D), 0) == _iota((D, D), 1)).astype(BF16)
        g_t = jnp.zeros((n, D), F32)
        r = g
        for i in range(3):
            p = r.astype(BF16)
            g_t = g_t + _dot_tn(p, eye)
            if i < 2:
                r = r - p.astype(F32)
        for cp in ins:
            cp.wait()
        d, mn, vn = _adam_math(bufs[0], g_t, bufs[1], bufs[2])
        for k, val in enumerate((g_t, d, mn, vn)):
            bufs[3 + k] = val
        outs = [pltpu.make_async_copy(bufs.at[3 + k], dst.at[:, 0], sems.at[3 + k])
                for k, dst in enumerate((g_hbm, d_hbm, mo_hbm, vo_hbm))]
        for cp in outs:
            cp.start()
        for cp in outs:
            cp.wait()

    anyspec = pl.BlockSpec(memory_space=pl.ANY)
    vm = pl.BlockSpec(memory_space=pltpu.VMEM)
    return pl.pallas_call(
        body, name="adam_w_in",
        in_specs=[anyspec, vm, anyspec, anyspec], out_specs=[anyspec] * 4,
        out_shape=[jax.ShapeDtypeStruct(w3.shape, F32)] * 4,
        scratch_shapes=[pltpu.VMEM((7, n, D), F32), pltpu.SemaphoreType.DMA((7,))],
        compiler_params=pltpu.CompilerParams(vmem_limit_bytes=VMEM_LIMIT),
    )(w3, grad, m3, v3)


def _adam_w_ada(gat, allv, s_arr, w, m, v, rb):
    R, C = w.shape

    def body(s_ref, c_ref, dm_ref, w_ref, m_ref, v_ref, g_ref, d_ref, mo_ref, vo_ref):
        cm = _rows_select(c_ref, rb)
        g = lax.dot_general(cm * _sigmoid(cm), _rows_select(dm_ref, C), (((0,), (0,)), ((), ())), precision=HI,
                            preferred_element_type=F32)
        d, mn, vn = _adam_math(w_ref[...], g, m_ref[...], v_ref[...])
        g_ref[...] = g
        d_ref[...] = d
        mo_ref[...] = mn
        vo_ref[...] = vn

    spec = pl.BlockSpec((rb, C), lambda i, s_ref: (i, 0))
    return pl.pallas_call(
        body, name="adam_w_ada",
        grid_spec=pltpu.PrefetchScalarGridSpec(
            num_scalar_prefetch=1, grid=(R // rb,),
            in_specs=[pl.BlockSpec((8, 1, rb), lambda i, s_ref: (0, 0, i)),
                      pl.BlockSpec((8, 1, C), lambda i, s_ref: (0, 0, s_ref[0])), spec, spec, spec],
            out_specs=[spec] * 4),
        out_shape=[jax.ShapeDtypeStruct((R, C), F32)] * 4, compiler_params=_cp("parallel"),
    )(s_arr, gat, allv, w, m, v)


def _adam_small(tot, segs, ws, ms, vs):
    k = len(ws)
    extra = [sg for sg in segs if not isinstance(sg, tuple)]
    ne = len(extra)

    def body(*refs):
        tot_ref, g_x = refs[0], list(refs[1:1 + ne])
        w, m, v = [refs[1 + ne + j * k:1 + ne + (j + 1) * k] for j in range(3)]
        g_o, d_o, m_o, v_o = [refs[1 + ne + (3 + j) * k:1 + ne + (4 + j) * k] for j in range(4)]
        for i in range(k):
            gi = tot_ref[:, segs[i][0]:segs[i][0] + segs[i][1]] if isinstance(segs[i], tuple) else g_x.pop(0)[...]
            d, mn, vn = _adam_math(w[i][...], gi, m[i][...], v[i][...])
            g_o[i][...] = gi
            d_o[i][...] = d
            m_o[i][...] = mn
            v_o[i][...] = vn

    shapes = [jax.ShapeDtypeStruct(w.shape, F32) for w in ws]
    vm = pl.BlockSpec(memory_space=pltpu.VMEM)
    outs = pl.pallas_call(
        body, name="adam_small", in_specs=[vm] * (1 + ne + 3 * k), out_specs=[vm] * (4 * k), out_shape=shapes * 4,
    )(tot, *extra, *ws, *ms, *vs)
    return outs[0:k], outs[k:2 * k], outs[2 * k:3 * k], outs[3 * k:4 * k]


def _pos():
    return lax.axis_index("x"), lax.axis_index("y"), lax.axis_index("c")


def _flip(v, bit):
    return 1 - v if bit else v


def _peer(k):
    x, y, c = _pos()
    return (_flip(x, (k >> 2) & 1), _flip(y, (k >> 1) & 1), _flip(c, k & 1))


def _logical(p):
    return 4 * p[0] + 2 * p[1] + p[2]


def _gather8(src_ref, dst_ref, send_sems, recv_sems, meanwhile):
    me = _logical(_pos())
    dst_ref[pl.ds(me, 1)] = src_ref[...][None]
    copies = []
    for k in range(1, 8):
        cp = pltpu.make_async_remote_copy(src_ref, dst_ref.at[me], send_sems.at[k - 1], recv_sems.at[k - 1],
                                          device_id=_peer(k), device_id_type=MESH)
        cp.start()
        copies.append(cp)
    meanwhile()
    for k in range(1, 8):
        pltpu.make_async_remote_copy(src_ref, dst_ref.at[_logical(_peer(k))], send_sems.at[k - 1], recv_sems.at[k - 1],
                                     device_id=_peer(k), device_id_type=MESH).wait_recv()
    for cp in copies:
        cp.wait_send()


def _rows_select(ref3, width):
    row = _iota((8, width), 0)
    out = jnp.zeros((8, width), F32)
    for i in range(8):
        out = jnp.where(row == i, ref3[i][:, 0:width], out)
    return out


def _mod_exchange(c_row, cw, w_ada_s, b_ada4, w_in3):
    n_sh = w_ada_s.shape[1]
    n_in = w_in3.shape[0]
    wide = -(-n_in // 128) * 128

    def body(c_ref, cw_ref, w_hbm, b_ref, win_hbm, gat_ref, mod_ref, token, winb_ref, pay_ref, p3, w_v, win_v, win_z,
             sa, ra, sb, rb, ls):
        token[...] = jnp.zeros_like(token)
        pay_ref[:, 0:D] = c_ref[...]
        for k in range(CONVK):
            pay_ref[:, D + 256 * k:D + 256 * (k + 1)] = cw_ref[k:k + 1, :]
        x, y, c = _pos()
        me = _logical((x, y, c))
        my_s = 2 * x + y
        load_w = pltpu.make_async_copy(w_hbm, w_v, ls.at[0])
        load_in = pltpu.make_async_copy(win_hbm.at[:, 0], win_v, ls.at[1])
        load_w.start()
        load_in.start()

        def local_work():
            win_z[...] = jnp.zeros_like(win_z)
            load_in.wait()
            win_z[0:n_in, :] = win_v[...].astype(BF16)
            eye = (_iota((wide, wide), 0) == _iota((wide, wide), 1)).astype(BF16)
            winb_ref[...] = _dot_tn(win_z[...], eye)[:, 0:n_in].astype(BF16)
            load_w.wait()

        _gather8(pay_ref, gat_ref, sa, ra, local_work)
        cmat = _rows_select(gat_ref, D)
        prod = _dot_hi(cmat * _sigmoid(cmat), w_v[...])
        for b in range(8):
            p3[b] = prod[b:b + 1, :]
        mod_ref[pl.ds(my_s, 1)] = p3[pl.ds(me, 1)] + b_ref[pl.ds(my_s, 1)]
        ks = (2, 4, 6)
        copies = []
        for i, k in enumerate(ks):
            pr = _peer(k)
            cp = pltpu.make_async_remote_copy(p3.at[_logical(pr)], mod_ref.at[my_s], sb.at[i], rb.at[i],
                                              device_id=pr, device_id_type=MESH)
            cp.start()
            copies.append(cp)
        for i, k in enumerate(ks):
            pr = _peer(k)
            s_src = 2 * pr[0] + pr[1]
            pltpu.make_async_remote_copy(p3.at[0], mod_ref.at[s_src], sb.at[i], rb.at[i],
                                         device_id=pr, device_id_type=MESH).wait_recv()
            mod_ref[pl.ds(s_src, 1)] = mod_ref[pl.ds(s_src, 1)] + b_ref[pl.ds(s_src, 1)]
        for cp in copies:
            cp.wait_send()

    vm = pl.BlockSpec(memory_space=pltpu.VMEM)
    anyspec = pl.BlockSpec(memory_space=pl.ANY)
    return pl.pallas_call(
        body, name="mod_exchange", in_specs=[vm, vm, anyspec, vm, anyspec], out_specs=[vm, vm, vm, vm],
        out_shape=[jax.ShapeDtypeStruct((8, 1, D + CONVK * 256), F32), jax.ShapeDtypeStruct((4, 1, n_sh), F32),
                   jax.ShapeDtypeStruct((8, 128), F32), jax.ShapeDtypeStruct((D, n_in), BF16)],
        scratch_shapes=[pltpu.VMEM((1, D + CONVK * 256), F32), pltpu.VMEM((8, 1, n_sh), F32), pltpu.VMEM(w_ada_s.shape, F32), pltpu.VMEM((n_in, D), F32),
                        pltpu.VMEM((wide, D), BF16), pltpu.SemaphoreType.DMA((7,)), pltpu.SemaphoreType.DMA((7,)),
                        pltpu.SemaphoreType.DMA((3,)), pltpu.SemaphoreType.DMA((3,)), pltpu.SemaphoreType.DMA((2,))],
        compiler_params=pltpu.CompilerParams(vmem_limit_bytes=VMEM_LIMIT),
    )(c_row, cw, w_ada_s, b_ada4, w_in3)


def _chips():
    x, y, _ = _pos()
    out = []
    for k in (1, 2, 3):
        px, py = _flip(x, (k >> 1) & 1), _flip(y, k & 1)
        out.append((px, py, 2 * px + py))
    return out


def _half_rows(ref, which):
    half = ref.shape[-2] // 2
    return pl.ds(pl.multiple_of(which * half, 8), half)


def _plan_small():
    def plan(refs):
        me = _logical(_pos())
        return [(refs[0], refs[1].at[me], _peer(k), refs[1].at[_logical(_peer(k))]) for k in range(1, 8)]
    return plan


def _small_sum(vec, land, me_arr):
    n = vec.shape[1]

    def body(me_ref, v_ref, land_ref, tot_ref, all_ref):
        tot = None
        for i in range(8):
            row = jnp.where(me_ref[0] == i, v_ref[...], land_ref[i])
            all_ref[i] = row
            tot = row if i == 0 else tot + row
        tot_ref[...] = tot

    return pl.pallas_call(
        body, name="small_sum",
        grid_spec=pltpu.PrefetchScalarGridSpec(
            num_scalar_prefetch=1, grid=(1,),
            in_specs=[pl.BlockSpec((1, n), lambda i, me_ref: (0, 0)), pl.BlockSpec((8, 1, n), lambda i, me_ref: (0, 0, 0))],
            out_specs=[pl.BlockSpec((1, n), lambda i, me_ref: (0, 0)),
                       pl.BlockSpec((8, 1, n), lambda i, me_ref: (0, 0, 0))]),
        out_shape=[jax.ShapeDtypeStruct((1, n), F32), jax.ShapeDtypeStruct((8, 1, n), F32)],
        compiler_params=_cp("arbitrary"),
    )(me_arr, vec, land)


def _add_half(g, sib, c_arr, rb, name):
    _, R, C = g.shape
    half = R // 2
    nb = half // rb

    def body(c_ref, g_ref, s_ref, o_ref):
        o_ref[...] = (g_ref[...] + s_ref[...]).astype(BF16)

    return pl.pallas_call(
        body, name=name,
        grid_spec=pltpu.PrefetchScalarGridSpec(
            num_scalar_prefetch=1, grid=(4, nb),
            in_specs=[pl.BlockSpec((1, rb, C), lambda s, i, c_ref: (s, c_ref[0] * nb + i, 0)),
                      pl.BlockSpec((1, rb, C), lambda s, i, c_ref: (s, i, 0))],
            out_specs=pl.BlockSpec((1, rb, C), lambda s, i, c_ref: (s, i, 0))),
        out_shape=jax.ShapeDtypeStruct((4, half, C), BF16),
        compiler_params=_cp("parallel", "parallel"),
    )(c_arr, g, sib)


def _add_half_in(gq, gz, sibq, sibz, c_arr, rb):
    half = D // 2
    nq = gq.shape[1]
    wide = -(-IN_SH // 128) * 128

    def sel(rows, first, lo):
        return (_iota((rows, wide), 0) + (first - lo) == _iota((rows, wide), 1)).astype(BF16)

    def body(c_ref, gq_ref, gz_ref, sq_ref, sz_ref, o_ref):
        q = (gq_ref[...] + sq_ref[...]).astype(BF16)
        z = (gz_ref[...] + sz_ref[...]).astype(BF16)
        for s in range(4):
            lo, hi = s * IN_SH, (s + 1) * IN_SH
            acc = jnp.zeros((rb, wide), F32)
            if lo < nq:
                a0, a1 = lo // 128 * 128, min(nq, -(-min(hi, nq) // 128) * 128)
                acc = acc + _dot(q[:, a0:a1], sel(a1 - a0, a0, lo))
            if hi > nq:
                a0, a1 = (max(lo, nq) - nq) // 128 * 128, -(-(hi - nq) // 128) * 128
                acc = acc + _dot(z[:, a0:a1], sel(a1 - a0, nq + a0, lo))
            o_ref[s] = acc[:, :IN_SH].astype(BF16)

    nb = half // rb
    mine = lambda w: pl.BlockSpec((rb, w), lambda i, c_ref: (c_ref[0] * nb + i, 0))
    sib = lambda w: pl.BlockSpec((rb, w), lambda i, c_ref: (i, 0))
    return pl.pallas_call(
        body, name="grad_add_in",
        grid_spec=pltpu.PrefetchScalarGridSpec(
            num_scalar_prefetch=1, grid=(nb,),
            in_specs=[mine(nq), mine(gz.shape[1]), sib(nq), sib(gz.shape[1])],
            out_specs=pl.BlockSpec((4, rb, IN_SH), lambda i, c_ref: (0, i, 0))),
        out_shape=jax.ShapeDtypeStruct((4, half, IN_SH), BF16),
        compiler_params=_cp("parallel"),
    )(c_arr, gq, gz, sibq, sibz)


def _sum4(parts, land, s_arr, rb, name):
    _, H, C = land.shape

    def body(s_ref, own_ref, r_ref, o_ref):
        own = own_ref[0].astype(F32)
        tot = jnp.zeros((rb, C), F32)
        for j in range(4):
            tot = tot + jnp.where(s_ref[0] == j, own, r_ref[j].astype(F32))
        o_ref[...] = tot

    return pl.pallas_call(
        body, name=name,
        grid_spec=pltpu.PrefetchScalarGridSpec(
            num_scalar_prefetch=1, grid=(H // rb,),
            in_specs=[pl.BlockSpec((1, rb, C), lambda i, s_ref: (s_ref[0], i, 0)),
                      pl.BlockSpec((4, rb, C), lambda i, s_ref: (0, i, 0))],
            out_specs=pl.BlockSpec((rb, C), lambda i, s_ref: (i, 0))),
        out_shape=jax.ShapeDtypeStruct((H, C), F32), compiler_params=_cp("parallel"),
    )(s_arr, parts, land)


HBM_SPEC = pl.BlockSpec(memory_space=pltpu.HBM)
SEM_SPEC = pl.BlockSpec(memory_space=pltpu.SEMAPHORE)
EFFECT = pltpu.SideEffectType.DATAFLOW_SIDE_EFFECTING


def _split_start(name, bufs, n_sem, plan, dep):
    nb = len(bufs)

    def body(*refs):
        ins, send, recv, token = refs[:nb], refs[nb + 1], refs[nb + 2], refs[-1]
        for i, (src, dst, dev, _) in enumerate(plan(ins)):
            pltpu.make_async_remote_copy(src, dst, send.at[i], recv.at[i], device_id=dev, device_id_type=MESH).start()
        token[...] = jnp.zeros_like(token)

    outs = pl.pallas_call(
        body, name=name,
        out_shape=(pltpu.SemaphoreType.DMA((n_sem,)), pltpu.SemaphoreType.DMA((n_sem,)),
                   *[pltpu.HBM(b.shape, b.dtype) for b in bufs], jax.ShapeDtypeStruct((8, 128), F32)),
        in_specs=[HBM_SPEC] * nb + [pl.BlockSpec(memory_space=pl.ANY)],
        out_specs=(SEM_SPEC, SEM_SPEC, *([HBM_SPEC] * nb), pl.BlockSpec(memory_space=pltpu.VMEM)),
        input_output_aliases={i: 2 + i for i in range(nb)},
        compiler_params=pltpu.CompilerParams(has_side_effects=EFFECT),
    )(*[pltpu.with_memory_space_constraint(b, pltpu.HBM) for b in bufs], dep)
    return outs[0], outs[1], list(outs[2:2 + nb]), outs[-1]


def _split_wait(name, send, recv, bufs, after, plan):
    nb = len(bufs)
    after = list(after) if isinstance(after, (list, tuple)) else [after]

    def body(*refs):
        ins, send_s, recv_s = refs[:nb], refs[nb], refs[nb + 1]
        for i, (src, dst, dev, mine) in enumerate(plan(ins)):
            pltpu.make_async_remote_copy(src, dst, send_s.at[i], recv_s.at[i], device_id=dev,
                                         device_id_type=MESH).wait_send()
            pltpu.make_async_remote_copy(src, mine, send_s.at[i], recv_s.at[i], device_id=dev,
                                         device_id_type=MESH).wait_recv()

    outs = pl.pallas_call(
        body, name=name, out_shape=[pltpu.HBM(b.shape, b.dtype) for b in bufs],
        in_specs=[HBM_SPEC] * nb + [SEM_SPEC, SEM_SPEC] + [HBM_SPEC] * len(after),
        out_specs=[HBM_SPEC] * nb, input_output_aliases={i: i for i in range(nb)},
        compiler_params=pltpu.CompilerParams(has_side_effects=EFFECT),
    )(*bufs, send, recv, *[pltpu.with_memory_space_constraint(a, pltpu.HBM) for a in after])
    return list(outs)


def _split_wait_start(name, send, recv, bufs, after, plan, bufs2, n_sem2, plan2):
    nb, nb2 = len(bufs), len(bufs2)
    after = list(after) if isinstance(after, (list, tuple)) else [after]
    n_in = nb + 2 + nb2 + len(after)

    def body(*refs):
        ins, send_s, recv_s, ins2 = refs[:nb], refs[nb], refs[nb + 1], refs[nb + 2:nb + 2 + nb2]
        send2, recv2, token = refs[n_in + nb], refs[n_in + nb + 1], refs[-1]
        for i, (src, dst, dev, mine) in enumerate(plan(ins)):
            pltpu.make_async_remote_copy(src, dst, send_s.at[i], recv_s.at[i], device_id=dev,
                                         device_id_type=MESH).wait_send()
            pltpu.make_async_remote_copy(src, mine, send_s.at[i], recv_s.at[i], device_id=dev,
                                         device_id_type=MESH).wait_recv()
        for i, (src, dst, dev, _) in enumerate(plan2(ins2)):
            pltpu.make_async_remote_copy(src, dst, send2.at[i], recv2.at[i], device_id=dev, device_id_type=MESH).start()
        token[...] = jnp.zeros_like(token)

    hbm = lambda b: pltpu.with_memory_space_constraint(b, pltpu.HBM)
    outs = pl.pallas_call(
        body, name=name,
        out_shape=(*[pltpu.HBM(b.shape, b.dtype) for b in bufs], pltpu.SemaphoreType.DMA((n_sem2,)),
                   pltpu.SemaphoreType.DMA((n_sem2,)), *[pltpu.HBM(b.shape, b.dtype) for b in bufs2],
                   jax.ShapeDtypeStruct((8, 128), F32)),
        in_specs=[HBM_SPEC] * nb + [SEM_SPEC, SEM_SPEC] + [HBM_SPEC] * (nb2 + len(after)),
        out_specs=(*([HBM_SPEC] * nb), SEM_SPEC, SEM_SPEC, *([HBM_SPEC] * nb2), pl.BlockSpec(memory_space=pltpu.VMEM)),
        input_output_aliases={**{i: i for i in range(nb)}, **{nb + 2 + j: nb + 2 + j for j in range(nb2)}},
        compiler_params=pltpu.CompilerParams(has_side_effects=EFFECT),
    )(*bufs, send, recv, *[hbm(b) for b in bufs2], *[hbm(a) for a in after])
    return list(outs[:nb]), outs[nb], outs[nb + 1], list(outs[nb + 2:nb + 2 + nb2]), outs[-1]


def _copies_now(name, bufs, n_sem, plan):
    nb = len(bufs)

    def body(*refs):
        ins, token, send, recv = refs[:nb], refs[2 * nb], refs[-2], refs[-1]
        token[...] = jnp.zeros_like(token)
        todo = plan(ins)
        for i, (src, dst, dev, _) in enumerate(todo):
            pltpu.make_async_remote_copy(src, dst, send.at[i], recv.at[i], device_id=dev, device_id_type=MESH).start()
        for i, (src, dst, dev, mine) in enumerate(todo):
            pltpu.make_async_remote_copy(src, mine, send.at[i], recv.at[i], device_id=dev, device_id_type=MESH).wait_recv()
        for i, (src, dst, dev, _) in enumerate(todo):
            pltpu.make_async_remote_copy(src, dst, send.at[i], recv.at[i], device_id=dev, device_id_type=MESH).wait_send()

    outs = pl.pallas_call(
        body, name=name,
        out_shape=[pltpu.HBM(b.shape, b.dtype) for b in bufs] + [jax.ShapeDtypeStruct((8, 128), F32)],
        in_specs=[HBM_SPEC] * nb, out_specs=[HBM_SPEC] * nb + [pl.BlockSpec(memory_space=pltpu.VMEM)],
        input_output_aliases={i: i for i in range(nb)},
        scratch_shapes=[pltpu.SemaphoreType.DMA((n_sem,)), pltpu.SemaphoreType.DMA((n_sem,))],
    )(*[pltpu.with_memory_space_constraint(b, pltpu.HBM) for b in bufs])
    return list(outs[:nb]), outs[nb]


def _slot(land, s, rows, cols):
    if cols is None:
        return land.at[s, rows]
    return land.at[rows, pl.ds(pl.multiple_of(s * cols, 128), cols)]


def _plan_gather_ici(cols):
    nw = len(cols)

    def plan(refs):
        x, y, c = _pos()
        my_s = 2 * x + y
        out = []
        for w in range(nw):
            mine = _half_rows(refs[w], c)
            for px, py, ps in _chips():
                out.append((refs[w].at[mine], _slot(refs[nw + w], my_s, mine, cols[w]), (px, py, c),
                            _slot(refs[nw + w], ps, mine, cols[w])))
        return out
    return plan


def _plan_gather_fwd(cols, rows):
    def plan(refs):
        x, y, c = _pos()
        out = []
        for w in range(len(cols)):
            half = rows[w] // 2
            mine = pl.ds(pl.multiple_of(c * half, 8), half)
            other = pl.ds(pl.multiple_of((1 - c) * half, 8), half)
            for px, py, ps in _chips():
                got = _slot(refs[w], ps, mine, cols[w])
                out.append((got, got, (x, y, 1 - c), _slot(refs[w], ps, other, cols[w])))
        return out
    return plan


def _plan_swap(nw):
    def plan(refs):
        x, y, c = _pos()
        return [(refs[w].at[:, _half_rows(refs[w], 1 - c)], refs[nw + w], (x, y, 1 - c), refs[nw + w])
                for w in range(nw)]
    return plan


def _plan_swap_rows(nw):
    def plan(refs):
        x, y, c = _pos()
        return [(refs[w].at[_half_rows(refs[w], 1 - c)], refs[nw + w], (x, y, 1 - c), refs[nw + w])
                for w in range(nw)]
    return plan


def _plan_scatter(nw):
    def plan(refs):
        x, y, c = _pos()
        my_s = 2 * x + y
        out = []
        for w in range(nw):
            for px, py, ps in _chips():
                out.append((refs[w].at[ps], refs[nw + w].at[my_s], (px, py, c), refs[nw + w].at[ps]))
        return out
    return plan


def _plan_scatter_both():
    def plan(refs):
        x, y, c = _pos()
        my_s = 2 * x + y
        src, land = refs
        out = []
        for px, py, ps in _chips():
            out.append((src.at[ps], land.at[my_s, c], (px, py, c), land.at[ps, c]))
            out.append((src.at[ps], land.at[my_s, c], (px, py, 1 - c), land.at[ps, 1 - c]))
        out.append((src.at[my_s], land.at[my_s, c], (x, y, 1 - c), land.at[my_s, 1 - c]))
        return out
    return plan


def _sum4_both(parts, land, s_arr, c_arr):
    _, _, H, C = land.shape

    def body(s_ref, c_ref, own_ref, r_ref, o_ref):
        mine = pl.program_id(0) == c_ref[0]
        own = own_ref[0].astype(F32)
        tot = jnp.zeros((H, C), F32)
        for j in range(4):
            tot = tot + jnp.where(jnp.logical_and(mine, s_ref[0] == j), own, r_ref[j, 0].astype(F32))
        o_ref[0] = tot

    return pl.pallas_call(
        body, name="grad_sum_in",
        grid_spec=pltpu.PrefetchScalarGridSpec(
            num_scalar_prefetch=2, grid=(2,),
            in_specs=[pl.BlockSpec((1, H, C), lambda h, s_ref, c_ref: (s_ref[0], 0, 0)),
                      pl.BlockSpec((4, 1, H, C), lambda h, s_ref, c_ref: (0, h, 0, 0))],
            out_specs=pl.BlockSpec((1, H, C), lambda h, s_ref, c_ref: (h, 0, 0))),
        out_shape=jax.ShapeDtypeStruct((2, H, C), F32), compiler_params=_cp("parallel"),
    )(s_arr, c_arr, parts, land).reshape(2 * H, C)


def _plan_join(nw):
    def plan(refs):
        x, y, c = _pos()
        out = []
        for w in range(nw):
            land = refs[nw + w]
            out.append((refs[w], land.at[_half_rows(land, c)], (x, y, 1 - c), land.at[_half_rows(land, 1 - c)]))
        return out
    return plan


def _hbm_empty(shape, dtype):
    return pltpu.with_memory_space_constraint(lax.empty(shape, dtype), pltpu.HBM)


def _w_in_assemble(land, own, s_arr, rb):
    wide = -(-IN_SH // 128) * 128
    starts = [s * IN_SH // 128 * 128 for s in range(4)]
    ends = [min(IN_PAD, -(-(s + 1) * IN_SH // 128) * 128) for s in range(4)]

    def body(s_ref, land_ref, own_ref, o_ref, parts):
        @pl.when(pl.program_id(0) == 0)
        def _():
            parts[...] = jnp.zeros_like(parts)

        acc = []
        for s in range(4):
            parts[s, :, 0:IN_SH] = jnp.where(s_ref[0] == s, own_ref[...], land_ref[s])
            w = ends[s] - starts[s]
            sel = (_iota((wide, w), 0) + (s * IN_SH - starts[s]) == _iota((wide, w), 1)).astype(BF16)
            acc.append(_dot(parts[s], sel))
        for s in range(4):
            lo = starts[s] if s == 0 else ends[s - 1]
            hi = starts[s + 1] if s < 3 else ends[s]
            o_ref[:, lo:hi] = acc[s][:, lo - starts[s]:hi - starts[s]].astype(BF16)
            if s < 3:
                a, b = starts[s + 1], ends[s]
                o_ref[:, a:b] = (acc[s][:, a - starts[s]:b - starts[s]] + acc[s + 1][:, 0:b - a]).astype(BF16)

    return pl.pallas_call(
        body, name="w_in_assemble",
        grid_spec=pltpu.PrefetchScalarGridSpec(
            num_scalar_prefetch=1, grid=(D // rb,),
            in_specs=[pl.BlockSpec((4, rb, IN_SH), lambda i, s_ref: (0, i, 0)),
                      pl.BlockSpec((rb, IN_SH), lambda i, s_ref: (i, 0))],
            out_specs=pl.BlockSpec((rb, IN_PAD), lambda i, s_ref: (i, 0)),
            scratch_shapes=[pltpu.VMEM((4, rb, wide), BF16)]),
        out_shape=jax.ShapeDtypeStruct((D, IN_PAD), BF16), compiler_params=_cp("arbitrary"),
    )(s_arr, land, own)


def _pad_lanes(a, n):
    return jnp.pad(a, ((0, 0), (0, n - a.shape[1])))


def kernel(x, c, positions, w_ada, b_ada, norm1_w, w_in, conv_w, conv_b, dt_bias, a_log, d_skip, attn_sinks, ssm_norm_w, w_out, norm2_w, w_gate_up, w_down, final_norm_w, loss_target, m_w_ada, m_b_ada, m_norm1_w, m_w_in, m_conv_w, m_conv_b, m_dt_bias, m_a_log, m_d_skip, m_attn_sinks, m_ssm_norm_w, m_w_out, m_norm2_w, m_w_gate_up, m_w_down, m_final_norm_w, v_w_ada, v_b_ada, v_norm1_w, v_w_in, v_conv_w, v_conv_b, v_dt_bias, v_a_log, v_d_skip, v_attn_sinks, v_ssm_norm_w, v_w_out, v_norm2_w, v_w_gate_up, v_w_down, v_final_norm_w):
    T = x.shape[1]
    tm = min(256, T)
    xi, yi, ci = lax.axis_index("x"), lax.axis_index("y"), lax.axis_index("c")
    my_s = 2 * xi + yi
    xs = x[0]
    tgt = loss_target[0]

    gat, mod4, tok, w_in_b = _mod_exchange(c, conv_w[0], w_ada[0], b_ada.reshape(4, 1, 1536), w_in.transpose(2, 0, 1))
    mod6 = mod4.reshape(6, D)
    cw_dev = gat[:, 0, D:].reshape(4, 2, CONVK, 256)[:, 0]
    conv_full = cw_dev.transpose(1, 0, 2).reshape(CONVK, CONVC)

    s_i, r_i, bufs, tok = _split_start("wgather_in_ici_start", [w_in_b, _hbm_empty((4,) + w_in_b.shape, BF16)], 3,
                                       _plan_gather_ici([None]), tok)
    inv_freq = (10000.0 ** (-jnp.arange(32, dtype=F32) / 32))
    cos, sin_s = _rope_tables(positions, inv_freq.reshape(32, 1), min(512, T), tok)
    late = [w_out[0].astype(BF16), w_gate_up[0].astype(BF16), w_down[0].astype(BF16)]
    lands = [_hbm_empty((4, D // 4, D), BF16), _hbm_empty((D, 2 * DFF), BF16), _hbm_empty((4, DFF // 4, D), BF16)]
    cols3, rows3 = [None, GU_SH, None], [D // 4, D, DFF // 4]
    bufs, s_a, r_a, bufs_late, tok = _split_wait_start(
        "wgather_in_ici_wait", s_i, r_i, bufs, cos, _plan_gather_ici([None]), late + lands, 9, _plan_gather_ici(cols3))
    own_in = bufs[0]
    bufs, tok = _copies_now("wgather_in_fwd", bufs[1:], 3, _plan_gather_fwd([None], [D]))
    s_arr = my_s.reshape(1).astype(jnp.int32)
    w_pad = _w_in_assemble(bufs[0], own_in, s_arr, 256)
    bufs = bufs_late

    qkv, z, xbc, dtr, h1b = _in_proj_fwd(xs, cos, sin_s, mod6, norm1_w, w_pad, min(512, T), tok)
    sinks = attn_sinks
    attn, lse = _attn_fwd(qkv, sinks)
    bufs = _split_wait("wgather_ici_wait", s_a, r_a, bufs, attn, _plan_gather_ici(cols3))
    late = bufs[:3]
    s_b, r_b, lands, tok = _split_start("wgather_fwd_start", bufs[3:], 9, _plan_gather_fwd(cols3, rows3), attn)
    dtb = _pad_lanes(dt_bias, 128)
    alog = _pad_lanes(a_log, 128)
    dskx = jnp.repeat(d_skip, HD, axis=1)
    mats = _ssd_mats()
    ynorm, ypre, states, conv_pre = _ssd_fwd(xbc, z, dtr, conv_full, conv_b, dtb, alog, dskx, ssm_norm_w, mats, tok)
    lands = _split_wait("wgather_fwd_wait", s_b, r_b, lands, ynorm, _plan_gather_fwd(cols3, rows3))
    w_out_f = lands[0].reshape(D, D)
    w_dn_f = lands[2].reshape(DFF, D)

    fw2 = final_norm_w.reshape(1, D)
    sq, dmix, dx1, h2b, act, dfb, dgu, dob, sm_ffn = _mix_ffn(
        xs, attn, ynorm, tgt, mod6, norm2_w, fw2, w_out_f, lands[1], late[1], s_arr, w_dn_f, late[0], late[2], tm)

    tt = min(2048, T)
    c_arr = ci.reshape(1).astype(jnp.int32)
    tok0 = jnp.zeros((8, 128), F32)
    gw_dn4 = _tn_matmul(act, dfb, GU_SH, D, tt, "dw_down", tok0).reshape(4, DFF // 4, D)
    gw_gu4 = _tn_matmul(h2b, dgu, D, GU_SH, tt, "dw_gate_up", tok0)
    gw_out4 = _tn_matmul_rows(attn, ynorm, dob, tt, "dw_out", tok0).reshape(4, D // 4, D)
    big1 = [gw_out4, gw_gu4, gw_dn4]
    rbs1 = [128, 512, 352]
    sib1 = [_hbm_empty((4, g.shape[1] // 2, g.shape[2]), F32) for g in big1]
    s_c, r_c, bufs, tok = _split_start("gswap_start", big1 + sib1, 3, _plan_swap(3), tok0)

    dzxd, d_cw, d_cb, d_sw, d_sk, d_dtb, d_av = _ssd_bwd(
        xbc, conv_pre, z, dtr, ypre, states, dmix, conv_full, dtb, alog, dskx, ssm_norm_w, mats, tok)
    bufs = _split_wait("gswap_wait", s_c, r_c, bufs, dzxd, _plan_swap(3))
    sums1 = [_add_half(g, s, c_arr, rb, "grad_add_%d" % i)
             for i, (g, s, rb) in enumerate(zip(bufs[:3], bufs[3:], rbs1))]
    land1 = [_hbm_empty(p.shape, BF16) for p in sums1]
    s_d, r_d, bufs, tok = _split_start("gscatter_start", sums1 + land1, 9, _plan_scatter(3), tok0)
    dqkv, d_sinks = _attn_bwd(qkv, sinks, lse, dmix, cos, sin_s, tok)
    bufs = _split_wait("gscatter_wait", s_d, r_d, bufs, dqkv, _plan_scatter(3))
    halves1 = [_sum4(p, l, s_arr, rb, "grad_sum_%d" % i)
               for i, (p, l, rb) in enumerate(zip(bufs[:3], bufs[3:], rbs1))]
    full1 = [_hbm_empty((2 * h.shape[0], h.shape[1]), F32) for h in halves1]
    s_e, r_e, bufs, tok = _split_start("gjoin_start", halves1 + full1, 3, _plan_join(3), tok0)
    gq = _tn_matmul(h1b, dqkv, D, 768, tt, "dw_in_qkv", tok)[0]
    gz = _tn_matmul(h1b, dzxd, D, IN_PAD - 768, tt, "dw_in_zxd", tok)[0]
    joined1 = _split_wait("gjoin_wait", s_e, r_e, bufs, [gq, gz], _plan_join(3))

    sibs = [_hbm_empty((D // 2, g.shape[1]), F32) for g in (gq, gz)]
    s_f, r_f, bufs, tok = _split_start("gswap_in_start", [gq, gz] + sibs, 2, _plan_swap_rows(2), tok0)
    g_dn_s, d_dn, m_dn, v_dn = _adam_2d(w_down[0], joined1[2], joined1[5], m_w_down[0], v_w_down[0], c_arr, 352,
                                        "adam_w_down", tok)
    g_gu_s, d_gu, m_gu, v_gu = _adam_2d(w_gate_up[0], joined1[1], joined1[4], m_w_gate_up[0], v_w_gate_up[0], c_arr,
                                        256, "adam_w_gate_up", tok)
    g_out_s, d_out, m_out, v_out = _adam_2d(w_out[0], joined1[0], joined1[3], m_w_out[0], v_w_out[0], c_arr, 128,
                                            "adam_w_out", tok)
    bufs = _split_wait("gswap_in_wait", s_f, r_f, bufs, [d_dn, d_gu, d_out], _plan_swap_rows(2))
    sum0 = _add_half_in(bufs[0], bufs[1], bufs[2], bufs[3], c_arr, min(256, D // 2))
    s_g, r_g, bufs, tok = _split_start("gscatter_in_start", [sum0, _hbm_empty((4, 2) + sum0.shape[1:], BF16)], 7,
                                       _plan_scatter_both(), tok0)
    grad_x, sm_in = _in_proj_bwd(xs, dx1, dqkv, dzxd, mod6, norm1_w, w_pad, min(512, T), tok)

    a_neg = -jnp.exp(alog)
    pieces = [sm_in[1:2], sm_in[2:3], sm_ffn[5:6], sm_ffn[2:3], sm_ffn[3:4], sm_ffn[4:5],
              sm_in[0:1], sm_ffn[1:2], sm_ffn[0:1], d_cb, d_cw.reshape(1, CONVK * CONVC),
              _pad_lanes(d_sw, SW), d_dtb, d_av * a_neg, d_sk, d_sinks,
              _pad_lanes((0.5 / D * jnp.sum(sq)).reshape(1, 1), 128)]
    vec = jnp.concatenate(pieces, axis=1)
    s_h, r_h, rows8, tok_small = _split_start("small_start", [vec, _hbm_empty((8,) + vec.shape, F32)], 7,
                                              _plan_small(), tok0)

    bufs = _split_wait("gscatter_in_wait", s_g, r_g, bufs, [grad_x, tok_small], _plan_scatter_both())
    gw_in_s = _sum4_both(bufs[0], bufs[1], s_arr, c_arr)
    native = lambda a: a.transpose(2, 0, 1)
    adam_in = _adam_w_in(native(w_in), gw_in_s, native(m_w_in), native(v_w_in))
    g_in_s, d_in, m_in, v_in = [a.transpose(1, 2, 0) for a in adam_in]
    rows8 = _split_wait("small_wait", s_h, r_h, rows8, [adam_in[1]], _plan_small())
    tot, allv = _small_sum(rows8[0], rows8[1], (4 * xi + 2 * yi + ci).reshape(1).astype(jnp.int32))
    o = 0
    offs = []
    for p in pieces:
        offs.append(o)
        o += p.shape[1]
    seg = lambda i, n: (offs[i], n)
    g_conv_w = lax.dynamic_slice_in_dim(
        tot[:, offs[10]:offs[10] + CONVK * CONVC].reshape(CONVK, CONVC), my_s * 256, 256, axis=1)
    loss = tot[0, offs[16]]

    small_names = ["b_ada", "norm1_w", "conv_w", "conv_b", "dt_bias", "a_log", "d_skip", "attn_sinks", "ssm_norm_w",
                   "norm2_w", "final_norm_w"]
    small_g = [(0, 6 * D), seg(6, D), g_conv_w, seg(9, D), seg(12, 8), seg(13, 8), seg(14, 8), seg(15, 8),
               seg(11, SW), seg(7, D), seg(8, D)]
    as2d = lambda a: a.reshape(-1, a.shape[-1])
    small_w = [as2d(a) for a in (b_ada, norm1_w, conv_w, conv_b, dt_bias, a_log, d_skip, attn_sinks, ssm_norm_w,
                                 norm2_w, final_norm_w)]
    small_m = [as2d(a) for a in (m_b_ada, m_norm1_w, m_conv_w, m_conv_b, m_dt_bias, m_a_log, m_d_skip, m_attn_sinks,
                                 m_ssm_norm_w, m_norm2_w, m_final_norm_w)]
    small_v = [as2d(a) for a in (v_b_ada, v_norm1_w, v_conv_w, v_conv_b, v_dt_bias, v_a_log, v_d_skip, v_attn_sinks,
                                 v_ssm_norm_w, v_norm2_w, v_final_norm_w)]
    small_g, sd, smn, svn = _adam_small(tot, small_g, small_w, small_m, small_v)
    g_ada, d_ada, m_ada, v_ada = _adam_w_ada(gat, allv, s_arr, w_ada[0], m_w_ada[0], v_w_ada[0], 256)

    order = ["w_ada", "b_ada", "norm1_w", "w_in", "conv_w", "conv_b", "dt_bias", "a_log", "d_skip", "attn_sinks",
             "ssm_norm_w", "w_out", "norm2_w", "w_gate_up", "w_down", "final_norm_w"]
    shapes = dict(w_ada=w_ada.shape, b_ada=b_ada.shape, norm1_w=norm1_w.shape, w_in=w_in.shape, conv_w=conv_w.shape,
                  conv_b=conv_b.shape, dt_bias=dt_bias.shape, a_log=a_log.shape, d_skip=d_skip.shape,
                  attn_sinks=attn_sinks.shape, ssm_norm_w=ssm_norm_w.shape, w_out=w_out.shape, norm2_w=norm2_w.shape,
                  w_gate_up=w_gate_up.shape, w_down=w_down.shape, final_norm_w=final_norm_w.shape)
    grads = dict(w_ada=g_ada, w_in=g_in_s, w_out=g_out_s, w_gate_up=g_gu_s, w_down=g_dn_s)
    deltas = dict(w_ada=d_ada, w_in=d_in, w_out=d_out, w_gate_up=d_gu, w_down=d_dn)
    new_m = dict(w_ada=m_ada, w_in=m_in, w_out=m_out, w_gate_up=m_gu, w_down=m_dn)
    new_v = dict(w_ada=v_ada, w_in=v_in, w_out=v_out, w_gate_up=v_gu, w_down=v_dn)
    for i, nme in enumerate(small_names):
        grads[nme], deltas[nme], new_m[nme], new_v[nme] = small_g[i], sd[i], smn[i], svn[i]
    outs = [loss, grad_x[None]]
    for table in (grads, deltas, new_m, new_v):
        outs += [table[nme].reshape(shapes[nme]) for nme in order]
    return tuple(outs)
```

```python
import functools
import math

import jax
import jax.numpy as jnp
from jax import lax
from jax.experimental import pallas as pl
from jax.experimental.pallas import tpu as pltpu

F32 = jnp.float32
BF16 = jnp.bfloat16
HI = lax.Precision.HIGHEST
MESH = pl.DeviceIdType.MESH

D = 1024
HD = 64
AW = 512
SW = 512
NST = 128
CONVK = 4
CONVC = 1024
BLK = 128
CPS = 4
SSD_FWD_CPS = 8
ATTN_BPS = 8
IN_PROJ = 2312
IN_PAD = 2432
IN_SH = IN_PROJ // 4
DFF = 2816
GU_SH = 1408
FF_SPLITS = ((0, 1536), (1536, 2816))
EPS = 1e-6
NEG = -1e30
LR, B1, B2, AEPS, WD, STEP = 0.001, 0.9, 0.999, 1e-08, 0.01, 10
VMEM_LIMIT = 58 * 1024 * 1024


def _cp(*sem):
    return pltpu.CompilerParams(dimension_semantics=sem or None, vmem_limit_bytes=VMEM_LIMIT)


def _dot(a, b):
    return jnp.dot(a, b, preferred_element_type=F32)


def _dot_nt(a, b):
    return lax.dot_general(a, b, (((1,), (1,)), ((), ())), preferred_element_type=F32)


def _dot_tn(a, b):
    return lax.dot_general(a, b, (((0,), (0,)), ((), ())), preferred_element_type=F32)


def _dot_hi(a, b):
    return jnp.dot(a, b, precision=HI, preferred_element_type=F32)


def _sigmoid(x):
    return 1.0 / (1.0 + jnp.exp(-x))


def _iota(shape, dim):
    return lax.broadcasted_iota(jnp.int32, shape, dim)


def _load_resident(hbm_ref, vmem_ref, sem):
    @pl.when(pl.program_id(0) == 0)
    def _():
        cp = pltpu.make_async_copy(hbm_ref, vmem_ref, sem)
        cp.start()
        cp.wait()


def _swap32(t):
    lane = _iota(t.shape, 1)
    return jnp.where((lane & 63) < 32, pltpu.roll(t, 96, 1), pltpu.roll(t, 32, 1))


def _rope_fwd(t, cos, sin_s):
    return t * cos + _swap32(t) * sin_s


def _rope_bwd(t, cos, sin_s):
    return t * cos - _swap32(t) * sin_s


DEP_SPEC = pl.BlockSpec((8, 128), lambda *_: (0, 0))


def _rope_tables(pos_row, inv_freq_col, tm, dep):
    T = pos_row.shape[1]
    lane, row = jnp.arange(128)[None, :], jnp.arange(96)[:, None]
    pick = (lane % 32) == (row % 32)
    sel_cos = pick.astype(BF16)
    sel_sin = jnp.where(pick, jnp.where(lane % 64 < 32, -1.0, 1.0), 0.0).astype(BF16)

    def body(p_ref, f_ref, sc_ref, ss_ref, dep_ref, cos_ref, sin_ref):
        ang = f_ref[...] * p_ref[...].astype(F32)
        cos_ref[...] = _dot_tn(_pieces(jnp.cos(ang), 3, 0), sc_ref[...])
        sin_ref[...] = _dot_tn(_pieces(jnp.sin(ang), 3, 0), ss_ref[...])

    full = lambda a: pl.BlockSpec(a.shape, lambda i: (0,) * a.ndim)
    return pl.pallas_call(
        body, name="rope_tables", grid=(T // tm,),
        in_specs=[pl.BlockSpec((1, tm), lambda i: (0, i)), full(inv_freq_col), full(sel_cos), full(sel_sin), DEP_SPEC],
        out_specs=[pl.BlockSpec((tm, 128), lambda i: (i, 0))] * 2,
        out_shape=[jax.ShapeDtypeStruct((T, 128), F32)] * 2,
        compiler_params=_cp("parallel"),
    )(pos_row, inv_freq_col, sel_cos, sel_sin, dep)


def _in_proj_fwd(x, cos, sin_s, mod6, norm1_w, w_pad, tm, dep):
    T = x.shape[0]

    def body(x_ref, cos_ref, sin_ref, mod_ref, nw_ref, w_hbm, dep_ref, qkv_ref, z_ref, xbc_ref, dt_ref, h_ref, w_vmem,
             sem):
        _load_resident(w_hbm, w_vmem, sem)
        xv = x_ref[...]
        r = lax.rsqrt(jnp.mean(xv * xv, axis=-1, keepdims=True) + EPS)
        h = (xv * r * nw_ref[...]) * (1.0 + mod_ref[1:2, :]) + mod_ref[0:1, :]
        hb = h.astype(BF16)
        h_ref[...] = hb
        proj = _dot(hb, w_vmem[...])
        cs, sn = cos_ref[...], sin_ref[...]
        for j in range(5):
            qkv_ref[:, 128 * j:128 * (j + 1)] = _rope_fwd(proj[:, 128 * j:128 * (j + 1)], cs, sn).astype(BF16)
        qkv_ref[:, 640:768] = proj[:, 640:768].astype(BF16)
        z_ref[...] = proj[:, 768:1280]
        xbc_ref[...] = proj[:, 1280:2304]
        dt_ref[...] = proj[:, 2304:2432]

    row = lambda w: pl.BlockSpec((tm, w), lambda i: (i, 0))
    full = lambda a: pl.BlockSpec(a.shape, lambda i: (0,) * a.ndim)
    return pl.pallas_call(
        body, name="in_proj_fwd", grid=(T // tm,),
        in_specs=[row(D), row(128), row(128), full(mod6), full(norm1_w), pl.BlockSpec(memory_space=pl.ANY), DEP_SPEC],
        out_specs=[row(768), row(512), row(1024), row(128), row(D)],
        out_shape=[jax.ShapeDtypeStruct((T, 768), BF16), jax.ShapeDtypeStruct((T, 512), F32),
                   jax.ShapeDtypeStruct((T, 1024), F32), jax.ShapeDtypeStruct((T, 128), F32),
                   jax.ShapeDtypeStruct((T, D), BF16)],
        scratch_shapes=[pltpu.VMEM((D, IN_PAD), BF16), pltpu.SemaphoreType.DMA],
        compiler_params=_cp("arbitrary"),
    )(x, cos, sin_s, mod6, norm1_w, w_pad, dep)


def _head_variants(pair, j):
    lane = _iota(pair.shape, 1)
    lo = lane < 64
    kv = j // 2
    ev = jnp.where(lo, pair, 0.0)
    od = jnp.where(lo, 0.0, pair)
    if kv == 0:
        od = pltpu.roll(od, 64, 1)
    else:
        ev = pltpu.roll(ev, 64, 1)
    return ev.astype(BF16), od.astype(BF16)


def _kv_variants(vcat):
    lane = _iota(vcat.shape, 1)
    lo = lane < 64
    v0 = jnp.where(lo, vcat, 0.0)
    v1 = jnp.where(lo, 0.0, vcat)
    out = {
        (0, 0): v0, (0, 1): pltpu.roll(v0, 64, 1),
        (1, 0): pltpu.roll(v1, 64, 1), (1, 1): v1,
    }
    return {k: v.astype(BF16) for k, v in out.items()}


def _fold_masks(n):
    upper = _iota((BLK, BLK), 1) > _iota((BLK, BLK), 0)
    return upper, upper & (n == 0)


def _attn_fwd(qkv, sinks):
    CPS = ATTN_BPS
    T = qkv.shape[0]
    nsteps = T // (CPS * BLK)

    def body(sink_ref, q_ref, kc_ref, kp_ref, vc_ref, vp_ref, o_ref, lse_ref):
        for sub in range(CPS):
            rows, before = slice(BLK * sub, BLK * (sub + 1)), slice(BLK * (sub - 1), BLK * sub)
            block(pl.program_id(0) * CPS + sub, sink_ref, q_ref.at[rows, :], kc_ref.at[rows, :],
                  kp_ref if sub == 0 else kc_ref.at[before, :], vc_ref.at[rows, :],
                  vp_ref if sub == 0 else vc_ref.at[before, :], o_ref.at[rows, :], lse_ref.at[rows, :])

    def block(n, sink_ref, q_ref, kc_ref, kp_ref, vc_ref, vp_ref, o_ref, lse_ref):
        vpv = _kv_variants(vp_ref[...].astype(F32))
        vcv = _kv_variants(vc_ref[...].astype(F32))
        q_all = jnp.concatenate(
            [v for j in range(4) for v in _head_variants(q_ref[:, 128 * j:128 * (j + 1)].astype(F32), j)], axis=0)
        s_prev = _dot_nt(q_all, kp_ref[...])
        s_cur = _dot_nt(q_all, kc_ref[...])
        upper, dead = _fold_masks(n)
        lane = _iota((BLK, 128), 1)
        lse_acc = jnp.zeros((BLK, 128), F32)
        for jj in range(4):
            acc = jnp.zeros((BLK, 128), F32)
            for par in range(2):
                h = 2 * jj + par
                rows = slice(h * BLK, (h + 1) * BLK)
                sink = sink_ref[0, h]
                s = jnp.where(dead, NEG, jnp.where(upper, s_prev[rows], s_cur[rows]) * 0.125)
                m = jnp.maximum(jnp.max(s, axis=1, keepdims=True), sink)
                p = jnp.exp(s - m)
                den = jnp.sum(p, axis=1, keepdims=True) + jnp.exp(sink - m)
                pn = p * (1.0 / den)
                acc = (acc + _dot(jnp.where(upper, pn, 0.0).astype(BF16), vpv[(jj // 2, par)])
                       + _dot(jnp.where(upper, 0.0, pn).astype(BF16), vcv[(jj // 2, par)]))
                lse_acc = jnp.where(lane == h, m + jnp.log(den), lse_acc)
            o_ref[:, 128 * jj:128 * (jj + 1)] = acc.astype(BF16)
        lse_ref[...] = lse_acc

    RB = CPS * BLK
    prev = lambda n: jnp.maximum(n * CPS - 1, 0)
    return pl.pallas_call(
        body, name="attn_fwd", grid=(nsteps,),
        in_specs=[pl.BlockSpec(memory_space=pltpu.SMEM),
                  pl.BlockSpec((RB, 512), lambda n: (n, 0)),
                  pl.BlockSpec((RB, 128), lambda n: (n, 4)),
                  pl.BlockSpec((BLK, 128), lambda n: (prev(n), 4)),
                  pl.BlockSpec((RB, 128), lambda n: (n, 5)),
                  pl.BlockSpec((BLK, 128), lambda n: (prev(n), 5))],
        out_specs=[pl.BlockSpec((RB, 512), lambda n: (n, 0)), pl.BlockSpec((RB, 128), lambda n: (n, 0))],
        out_shape=[jax.ShapeDtypeStruct((T, 512), BF16), jax.ShapeDtypeStruct((T, 128), F32)],
        compiler_params=_cp("parallel"),
    )(sinks, qkv, qkv, qkv, qkv, qkv)


def _attn_bwd(qkv, sinks, lse, dmix, cos, sin_s, dep):
    T = qkv.shape[0]
    nb = T // BLK

    def body(sink_ref, q_ref, kc_ref, kp_ref, vc_ref, vp_ref, lse_ref, do_ref, cq_ref, sq_ref, ck_ref, sk_ref,
             dep_ref, out_ref, ds_ref, dq_car, dk_car, dv_car):
        n = pl.program_id(0)
        lane = _iota((BLK, 128), 1)

        @pl.when(n == 0)
        def _():
            ds_ref[...] = jnp.zeros_like(ds_ref)
            dq_car[...] = jnp.zeros_like(dq_car)
            dk_car[...] = jnp.zeros_like(dk_car)
            dv_car[...] = jnp.zeros_like(dv_car)

        @pl.when(n < nb)
        def _():
            kp, kc, vp, vc = kp_ref[...], kc_ref[...], vp_ref[...], vc_ref[...]
            kpv = _kv_variants(kp.astype(F32))
            kcv = _kv_variants(kc.astype(F32))
            lse_v = lse_ref[...]
            q_all = jnp.concatenate(
                [v for j in range(4) for v in _head_variants(q_ref[:, 128 * j:128 * (j + 1)].astype(F32), j)], axis=0)
            do_all = jnp.concatenate(
                [v for j in range(4) for v in _head_variants(do_ref[:, 128 * j:128 * (j + 1)], j)], axis=0)
            s_prev, s_cur = _dot_nt(q_all, kp), _dot_nt(q_all, kc)
            dp_prev, dp_cur = _dot_nt(do_all, vp), _dot_nt(do_all, vc)
            upper, dead = _fold_masks(n)
            out_ref[:, 0:512] = dq_car[...]
            dsk = jnp.zeros((1, 128), F32)
            ds_u, ds_l, p_u, p_l = [], [], [], []
            for jj in range(4):
                dq_acc = jnp.zeros((BLK, 128), F32)
                for par in range(2):
                    h = 2 * jj + par
                    rows = slice(h * BLK, (h + 1) * BLK)
                    lse_h = jnp.sum(jnp.where(lane == h, lse_v, 0.0), axis=1, keepdims=True)
                    s = jnp.where(dead, NEG, jnp.where(upper, s_prev[rows], s_cur[rows]) * 0.125)
                    p = jnp.exp(s - lse_h)
                    dp = jnp.where(upper, dp_prev[rows], dp_cur[rows])
                    delta = jnp.sum(p * dp, axis=1, keepdims=True)
                    ds = p * (dp - delta) * 0.125
                    dsu, dsl = jnp.where(upper, ds, 0.0).astype(BF16), jnp.where(upper, 0.0, ds).astype(BF16)
                    dq_acc = dq_acc + _dot(dsu, kpv[(jj // 2, par)]) + _dot(dsl, kcv[(jj // 2, par)])
                    ds_u.append(dsu)
                    ds_l.append(dsl)
                    p_u.append(jnp.where(upper, p, 0.0).astype(BF16))
                    p_l.append(jnp.where(upper, 0.0, p).astype(BF16))
                    dsk = dsk + jnp.where(lane[0:1] == h, -jnp.sum(jnp.exp(sink_ref[0, h] - lse_h) * delta), 0.0)
                dq_car[:, 128 * jj:128 * (jj + 1)] = _rope_bwd(dq_acc, cq_ref[...], sq_ref[...]).astype(BF16)
            stack = lambda parts: jnp.concatenate(parts, axis=0)
            dk_prev, dk_cur = _dot_tn(stack(ds_u), q_all), _dot_tn(stack(ds_l), q_all)
            dv_prev, dv_cur = _dot_tn(stack(p_u), do_all), _dot_tn(stack(p_l), do_all)
            ds_ref[...] += dsk
            out_ref[:, 512:640] = _rope_bwd(dk_car[...] + dk_prev, ck_ref[...], sk_ref[...]).astype(BF16)
            out_ref[:, 640:768] = (dv_car[...] + dv_prev).astype(BF16)
            dk_car[...] = dk_cur
            dv_car[...] = dv_cur

        @pl.when(n == nb)
        def _():
            out_ref[:, 0:512] = dq_car[...]
            out_ref[:, 512:640] = _rope_bwd(dk_car[...], ck_ref[...], sk_ref[...]).astype(BF16)
            out_ref[:, 640:768] = dv_car[...].astype(BF16)

    cur = lambda n: jnp.minimum(n, nb - 1)
    prev = lambda n: jnp.maximum(cur(n) - 1, 0)
    outb = lambda n: jnp.maximum(n - 1, 0)
    return pl.pallas_call(
        body, name="attn_bwd", grid=(nb + 1,),
        in_specs=[pl.BlockSpec(memory_space=pltpu.SMEM),
                  pl.BlockSpec((BLK, 512), lambda n: (cur(n), 0)),
                  pl.BlockSpec((BLK, 128), lambda n: (cur(n), 4)),
                  pl.BlockSpec((BLK, 128), lambda n: (prev(n), 4)),
                  pl.BlockSpec((BLK, 128), lambda n: (cur(n), 5)),
                  pl.BlockSpec((BLK, 128), lambda n: (prev(n), 5)),
                  pl.BlockSpec((BLK, 128), lambda n: (cur(n), 0)),
                  pl.BlockSpec((BLK, 512), lambda n: (cur(n), 0)),
                  pl.BlockSpec((BLK, 128), lambda n: (cur(n), 0)),
                  pl.BlockSpec((BLK, 128), lambda n: (cur(n), 0)),
                  pl.BlockSpec((BLK, 128), lambda n: (outb(n), 0)),
                  pl.BlockSpec((BLK, 128), lambda n: (outb(n), 0)), DEP_SPEC],
        out_specs=[pl.BlockSpec((BLK, 768), lambda n: (outb(n), 0)), pl.BlockSpec((1, 128), lambda n: (0, 0))],
        out_shape=[jax.ShapeDtypeStruct((T, 768), BF16), jax.ShapeDtypeStruct((1, 128), F32)],
        scratch_shapes=[pltpu.VMEM((BLK, 512), BF16), pltpu.VMEM((BLK, 128), F32), pltpu.VMEM((BLK, 128), F32)],
        compiler_params=_cp("arbitrary"),
    )(sinks, qkv, qkv, qkv, qkv, qkv, lse, dmix, cos, sin_s, cos, sin_s, dep)


def _ssd_mats():
    e = jnp.arange(SW)[None, :] // HD == jnp.arange(128)[:, None]
    tri = jnp.arange(BLK)[None, :] <= jnp.arange(BLK)[:, None]
    return (jnp.tile(e, (3, 1)).astype(BF16), jnp.tile(e.T, (2, 1)).astype(BF16),
            jnp.tile(tri, (1, 3)).astype(BF16), jnp.tile(tri.T, (1, 3)).astype(BF16))


def _pieces(x, n, axis):
    out, r = [], x
    for i in range(n):
        p = r.astype(BF16)
        out.append(p)
        if i + 1 < n:
            r = r - p.astype(F32)
    return jnp.concatenate(out, axis=axis)


def _expand(x, e3):
    return _dot(_pieces(x, 3, 1), e3)


def _head_sums(x, et2):
    return _dot(_pieces(x, 2, 1), et2)


def _run_sum(tri3, x):
    return _dot(tri3, _pieces(x, 3, 0))


def _shift_down(u, tail, j):
    rolled = pltpu.roll(u, j, 0)
    first = jnp.where(_iota(tail.shape, 0) < j, pltpu.roll(tail, j, 0), rolled[0:8])
    return jnp.concatenate([first, rolled[8:]], axis=0)


def _shift_up(d, head, j):
    rolled = pltpu.roll(d, BLK - j, 0)
    last = jnp.where(_iota(head.shape, 0) >= 8 - j, pltpu.roll(head, 8 - j, 0), rolled[BLK - 8:])
    return jnp.concatenate([rolled[:BLK - 8], last], axis=0)


def _ssd_parts(dtr, dtb, alog, e3, tril3):
    xx = dtr + dtb
    dt = jnp.maximum(xx, 0.0) + jnp.log(1.0 + jnp.exp(-jnp.abs(xx)))
    a_neg = -jnp.exp(alog)
    tril = _iota((BLK, BLK), 1) <= _iota((BLK, BLK), 0)
    cs = _run_sum(tril3, dt * a_neg)
    csx = _expand(cs, e3)
    last = csx[BLK - 1:BLK, :]
    return dict(xx=xx, dt=dt, a_neg=a_neg, tril=tril, cs=cs, cs_t=cs.T,
                ecsx=jnp.exp(csx), dtex=jnp.exp(last - csx), cdx=jnp.exp(last), dtx=_expand(dt, e3))


def _decay(parts, h):
    seg = parts["cs"][:, h:h + 1] - parts["cs_t"][h:h + 1, :]
    return jnp.exp(jnp.where(parts["tril"], seg, NEG))


def _group_cols(a, g):
    return a[:, 256 * g:256 * (g + 1)]


def _ssd_fwd(xbc, z, dtr, conv_w, conv_b, dtb, alog, dskx, ssm_w, mats, dep):
    CPS = SSD_FWD_CPS
    T = xbc.shape[0]
    nc = T // BLK

    def body(u_ref, tail_ref, z_ref, dtr_ref, cw_ref, cb_ref, dtb_ref, al_ref, dk_ref, sw_ref, e3_ref, tril3_ref,
             dep_ref, yn_ref, yp_ref, st_ref, co_ref, s_scr):
        n = pl.program_id(0)

        @pl.when(n == 0)
        def _():
            s_scr[...] = jnp.zeros_like(s_scr)

        lane = _iota((BLK, 128), 1)
        lo = lane < 64
        for sub in range(CPS):
            rows = slice(BLK * sub, BLK * (sub + 1))
            u = u_ref[rows, :]
            tail = jnp.where(n > 0, tail_ref[...], 0.0) if sub == 0 else u_ref[BLK * sub - 8:BLK * sub, :]
            co = cb_ref[...] + cw_ref[3:4, :] * u
            for j in range(1, CONVK):
                co = co + cw_ref[3 - j:4 - j, :] * _shift_down(u, tail, j)
            co_ref[rows, :] = co
            xc = co * _sigmoid(co)
            pt = _ssd_parts(dtr_ref[rows, :], dtb_ref[...], al_ref[...], e3_ref[...], tril3_ref[...])
            xs = xc[:, :SW]
            bm = [xc[:, 512:640].astype(BF16), xc[:, 640:768].astype(BF16)]
            cm = [xc[:, 768:896].astype(BF16), xc[:, 896:1024].astype(BF16)]
            s_in = s_scr[...]
            st_ref[sub] = s_in
            xdt = xs * pt["dtx"]
            xde = (xdt * pt["dtex"]).astype(BF16)
            ys, s_new = [], []
            for g in range(2):
                cb = _dot_nt(cm[g], bm[g])
                yoff = _dot(cm[g], _group_cols(s_in, g).astype(BF16))
                s_new.append(_dot_tn(bm[g], _group_cols(xde, g)))
                for jj in range(2):
                    j = 2 * g + jj
                    chunk = xdt[:, 128 * j:128 * (j + 1)]
                    g_ev = (cb * _decay(pt, 2 * j)).astype(BF16)
                    g_od = (cb * _decay(pt, 2 * j + 1)).astype(BF16)
                    yd = (_dot(g_ev, jnp.where(lo, chunk, 0.0).astype(BF16))
                          + _dot(g_od, jnp.where(lo, 0.0, chunk).astype(BF16)))
                    ys.append(yd + yoff[:, 128 * jj:128 * (jj + 1)] * pt["ecsx"][:, 128 * j:128 * (j + 1)])
            y = jnp.concatenate(ys, axis=1) + xs * dk_ref[...]
            s_scr[...] = s_in * pt["cdx"] + jnp.concatenate(s_new, axis=1)
            yp_ref[rows, :] = y
            zv = z_ref[rows, :]
            yz = y * (zv * _sigmoid(zv))
            outs = []
            for g in range(2):
                yg = _group_cols(yz, g)
                outs.append(yg * lax.rsqrt(jnp.mean(yg * yg, axis=-1, keepdims=True) + EPS))
            yn_ref[rows, :] = (jnp.concatenate(outs, axis=1) * sw_ref[...]).astype(BF16)

    e3, _, tril3, _ = mats
    RB = CPS * BLK
    tail8 = lambda n: jnp.maximum(n * (RB // 8) - 1, 0)
    full = lambda a: pl.BlockSpec(a.shape, lambda n: (0,) * a.ndim)
    return pl.pallas_call(
        body, name="ssd_fwd", grid=(nc // CPS,),
        in_specs=[pl.BlockSpec((RB, CONVC), lambda n: (n, 0)), pl.BlockSpec((8, CONVC), lambda n: (tail8(n), 0)),
                  pl.BlockSpec((RB, SW), lambda n: (n, 0)), pl.BlockSpec((RB, 128), lambda n: (n, 0)),
                  full(conv_w), full(conv_b), full(dtb), full(alog), full(dskx), full(ssm_w), full(e3), full(tril3),
                  DEP_SPEC],
        out_specs=[pl.BlockSpec((RB, SW), lambda n: (n, 0)), pl.BlockSpec((RB, SW), lambda n: (n, 0)),
                   pl.BlockSpec((CPS, NST, SW), lambda n: (n, 0, 0)), pl.BlockSpec((RB, CONVC), lambda n: (n, 0))],
        out_shape=[jax.ShapeDtypeStruct((T, SW), BF16), jax.ShapeDtypeStruct((T, SW), F32),
                   jax.ShapeDtypeStruct((nc, NST, SW), F32), jax.ShapeDtypeStruct((T, CONVC), F32)],
        scratch_shapes=[pltpu.VMEM((NST, SW), F32)],
        compiler_params=_cp("arbitrary"),
    )(xbc, xbc, z, dtr, conv_w, conv_b, dtb, alog, dskx, ssm_w, e3, tril3, dep)


def _ssd_bwd(xbc, co_all, z, dtr, ypre, states, dmix, conv_w, dtb, alog, dskx, ssm_w, mats, dep):
    T = xbc.shape[0]
    nsteps = T // (CPS * BLK)

    def body(*refs):
        per_chunk, consts, out_ref, carried = refs[:7], refs[7:16], refs[17], refs[18:]
        i = pl.program_id(0)

        @pl.when(i == 0)
        def _():
            for r in carried:
                r[...] = jnp.zeros_like(r)

        for sub in reversed(range(CPS)):
            rows = slice(BLK * sub, BLK * (sub + 1))
            views = [r.at[sub:sub + 1] if k == 5 else r.at[rows, :] for k, r in enumerate(per_chunk)]
            chunk(*views, *consts, out_ref.at[rows, :], *carried)

        @pl.when(i == nsteps - 1)
        def _():
            dsk_ref, dskx_scr = carried[3], carried[8]
            dsk_ref[...] = _head_sums(jnp.broadcast_to(dskx_scr[...], (8, SW)), consts[6][...])[0:1]

    def chunk(u_ref, co_ref, z_ref, dtr_ref, yp_ref, st_ref, dyn_ref, cw_ref, dtb_ref, al_ref, dk_ref, sw_ref,
              e3_ref, et2_ref, tril3_ref, triu3_ref,
              out_ref, dcw_ref, dcb_ref, dsw_ref, dsk_ref, ddtb_ref, dav_ref, ds_scr, dco_scr, dskx_scr):
        co = co_ref[...]
        sg = _sigmoid(co)
        xc = co * sg
        pt = _ssd_parts(dtr_ref[...], dtb_ref[...], al_ref[...], e3_ref[...], tril3_ref[...])
        dtx, ecsx, dtex, cdx = pt["dtx"], pt["ecsx"], pt["dtex"], pt["cdx"]
        xs = xc[:, :SW]
        bm = [xc[:, 512:640].astype(BF16), xc[:, 640:768].astype(BF16)]
        cm = [xc[:, 768:896].astype(BF16), xc[:, 896:1024].astype(BF16)]
        s_in = st_ref[0]
        ds_out = ds_scr[...]
        e_t = et2_ref[...]

        zv = z_ref[...]
        sz = _sigmoid(zv)
        silu_z = zv * sz
        ypre = yp_ref[...]
        yz = ypre * silu_z
        dyn = dyn_ref[...]
        sw = sw_ref[...]
        dyz, yns = [], []
        for g in range(2):
            yg = _group_cols(yz, g)
            r = lax.rsqrt(jnp.mean(yg * yg, axis=-1, keepdims=True) + EPS)
            yn = yg * r
            dg = _group_cols(dyn, g) * _group_cols(sw, g)
            dyz.append(r * (dg - yn * jnp.mean(dg * yn, axis=-1, keepdims=True)))
            yns.append(yn)
        dyz = jnp.concatenate(dyz, axis=1)
        dsw_ref[...] += jnp.sum(dyn * jnp.concatenate(yns, axis=1), axis=0, keepdims=True)
        dy = dyz * silu_z
        dz = dyz * ypre * (sz * (1.0 + zv * (1.0 - sz)))

        xdt = xs * dtx
        xdt_b = xdt.astype(BF16)
        edy = (ecsx * dy).astype(BF16)
        xde = (xdt * dtex).astype(BF16)
        lane = _iota((BLK, 128), 1)
        lo = lane < 64
        row8 = _iota((8, 128), 0)
        dcs = jnp.zeros((BLK, 128), F32)
        col_rows = jnp.zeros((8, 128), F32)
        dxdt, bds, yoff, dbs, dcs_g, ds_new = [], [], [], [], [], []
        for g in range(2):
            s_g = _group_cols(s_in, g).astype(BF16)
            dso_g = _group_cols(ds_out, g).astype(BF16)
            cb = _dot_nt(cm[g], bm[g])
            bds.append(_dot(bm[g], dso_g))
            yoff.append(_dot(cm[g], s_g))
            dcb_g = jnp.zeros((BLK, BLK), F32)
            for jj in range(2):
                j = 2 * g + jj
                dy_c = dy[:, 128 * j:128 * (j + 1)]
                xdt_c = xdt_b[:, 128 * j:128 * (j + 1)]
                acc = jnp.zeros((BLK, 128), F32)
                for par in range(2):
                    h = 2 * j + par
                    lm = _decay(pt, h)
                    gm = cb * lm
                    dy_m = (jnp.where(lo, dy_c, 0.0) if par == 0 else jnp.where(lo, 0.0, dy_c)).astype(BF16)
                    dg_h = _dot_nt(dy_m, xdt_c)
                    w_h = dg_h * gm
                    dcs = dcs + jnp.where(lane == h, jnp.sum(w_h, axis=1, keepdims=True), 0.0)
                    col_rows = col_rows + jnp.where(row8 == h, jnp.sum(w_h, axis=0, keepdims=True), 0.0)
                    dcb_g = dcb_g + dg_h * lm
                    acc = acc + _dot_tn(gm.astype(BF16), dy_m)
                dxdt.append(acc)
            dcb_b = dcb_g.astype(BF16)
            dcs_g.append(_dot(dcb_b, bm[g]) + _dot_nt(_group_cols(edy, g), s_g))
            dbs.append(_dot_tn(dcb_b, cm[g]) + _dot_nt(_group_cols(xde, g), dso_g))
            ds_new.append(_dot_tn(cm[g], _group_cols(edy, g)))
        bds = jnp.concatenate(bds, axis=1)
        yoff = jnp.concatenate(yoff, axis=1) * ecsx
        dxdt = jnp.concatenate(dxdt, axis=1) + dtex * bds
        ds_scr[...] = cdx * ds_out + jnp.concatenate(ds_new, axis=1)

        t_m = _head_sums(dtex * xdt * bds, e_t)
        colsum_t = jnp.concatenate([col_rows, jnp.zeros((BLK - 8, 128), F32)], axis=0).T
        cd = jnp.exp(pt["cs"][BLK - 1:BLK, :])
        sds = jnp.sum(s_in * ds_out, axis=0, keepdims=True)
        last_row = jnp.sum(t_m, axis=0, keepdims=True) + cd * _head_sums(jnp.broadcast_to(sds, (8, SW)), e_t)[0:1]
        dcs = dcs - colsum_t + _head_sums(dy * yoff, e_t) - t_m
        dcs = dcs + jnp.where(_iota((BLK, 128), 0) == BLK - 1, last_row, 0.0)
        da = _run_sum(triu3_ref[...], dcs)
        dt = pt["dt"]
        ddt = da * pt["a_neg"] + _head_sums(dxdt * xs, e_t)
        dav_ref[...] += jnp.sum(da * dt, axis=0, keepdims=True)
        ddtr = ddt * _sigmoid(pt["xx"])
        ddtb_ref[...] += jnp.sum(ddtr, axis=0, keepdims=True)
        dxs = dxdt * dtx + dy * dk_ref[...]
        dskx_scr[...] += jnp.sum(dy * xs, axis=0, keepdims=True)
        dxc = jnp.concatenate([dxs, dbs[0], dbs[1], dcs_g[0], dcs_g[1]], axis=1)
        dco = dxc * (sg * (1.0 + co * (1.0 - sg)))

        dcb_ref[...] += jnp.sum(dco, axis=0, keepdims=True)
        u = u_ref[...]
        head = dco_scr[...]
        du = jnp.zeros_like(dco)
        for j in range(CONVK):
            up_j = dco if j == 0 else _shift_up(dco, head, j)
            dcw_ref[3 - j:4 - j, :] += jnp.sum(up_j * u, axis=0, keepdims=True)
            du = du + cw_ref[3 - j:4 - j, :] * up_j
        dco_scr[...] = dco[0:8]
        out_ref[:, 0:512] = dz.astype(BF16)
        out_ref[:, 512:1536] = du.astype(BF16)
        out_ref[:, 1536:1664] = ddtr.astype(BF16)

    e3, et2, tril3, triu3 = mats
    RB = CPS * BLK
    rev = lambda i: nsteps - 1 - i
    full = lambda a: pl.BlockSpec(a.shape, lambda i: (0,) * a.ndim)
    acc = lambda r, c: pl.BlockSpec((r, c), lambda i: (0, 0))
    return pl.pallas_call(
        body, name="ssd_bwd", grid=(nsteps,),
        in_specs=[pl.BlockSpec((RB, CONVC), lambda i: (rev(i), 0)), pl.BlockSpec((RB, CONVC), lambda i: (rev(i), 0)),
                  pl.BlockSpec((RB, SW), lambda i: (rev(i), 0)), pl.BlockSpec((RB, 128), lambda i: (rev(i), 0)),
                  pl.BlockSpec((RB, SW), lambda i: (rev(i), 0)), pl.BlockSpec((CPS, NST, SW), lambda i: (rev(i), 0, 0)),
                  pl.BlockSpec((RB, SW), lambda i: (rev(i), 1)),
                  full(conv_w), full(dtb), full(alog), full(dskx), full(ssm_w),
                  full(e3), full(et2), full(tril3), full(triu3), DEP_SPEC],
        out_specs=[pl.BlockSpec((RB, 1664), lambda i: (rev(i), 0)),
                   acc(CONVK, CONVC), acc(1, CONVC), acc(1, SW), acc(1, 128), acc(1, 128), acc(1, 128)],
        out_shape=[jax.ShapeDtypeStruct((T, 1664), BF16),
                   jax.ShapeDtypeStruct((CONVK, CONVC), F32), jax.ShapeDtypeStruct((1, CONVC), F32),
                   jax.ShapeDtypeStruct((1, SW), F32), jax.ShapeDtypeStruct((1, 128), F32),
                   jax.ShapeDtypeStruct((1, 128), F32), jax.ShapeDtypeStruct((1, 128), F32)],
        scratch_shapes=[pltpu.VMEM((NST, SW), F32), pltpu.VMEM((8, CONVC), F32), pltpu.VMEM((1, SW), F32)],
        compiler_params=_cp("arbitrary"),
    )(xbc, co_all, z, dtr, ypre, states, dmix, conv_w, dtb, alog, dskx, ssm_w, e3, et2, tril3, triu3, dep)


def _mix_ffn(x, attn, ynorm, tgt, mod6, norm2_w, final_w, w_out, w_gu, w_gu_own, s_arr, w_dn, w_out_own, w_dn_own, tm):
    T = x.shape[0]
    nt = T // tm

    def body(x_ref, a_ref, y_ref, t_ref, mod_ref, n2_ref, fw_ref, wo_hbm, wgu_hbm, own_hbm, s_ref, wdn_hbm, wo_own, wdn_own,
             sq_ref, dmix_ref, dx1_ref, h2_ref, act_ref, df_ref, dgu_ref, do_ref, sm_ref,
             wo, wgu, wdn, sems):
        i = pl.program_id(0)

        @pl.when(i == 0)
        def _():
            cps = [pltpu.make_async_copy(s, d, sems.at[k]) for k, (s, d) in
                   enumerate(((wo_hbm, wo), (wgu_hbm, wgu), (wdn_hbm, wdn)))]
            for c in cps:
                c.start()
            for c in cps:
                c.wait()
            rows = lambda n: pl.ds(pl.multiple_of(s_ref[0] * n, 16), n)
            owns = [pltpu.make_async_copy(
                        own_hbm, wgu.at[:, pl.ds(pl.multiple_of(s_ref[0] * GU_SH, 128), GU_SH)], sems.at[3]),
                    pltpu.make_async_copy(wo_own, wo.at[rows(D // 4), :], sems.at[4]),
                    pltpu.make_async_copy(wdn_own, wdn.at[rows(DFF // 4), :], sems.at[5])]
            for c in owns:
                c.start()
            for c in owns:
                c.wait()
            sq_ref[...] = jnp.zeros_like(sq_ref)
            sm_ref[...] = jnp.zeros_like(sm_ref)

        gate1, shift2, scale2, gate2 = mod_ref[2:3, :], mod_ref[3:4, :], mod_ref[4:5, :], mod_ref[5:6, :]
        n2w, fw = n2_ref[...], fw_ref[...]
        o = _dot(a_ref[...], wo[0:AW, :]) + _dot(y_ref[...], wo[AW:D, :])
        x1 = x_ref[...] + gate1 * o
        r2 = lax.rsqrt(jnp.mean(x1 * x1, axis=-1, keepdims=True) + EPS)
        xh2 = x1 * r2
        n2 = xh2 * n2w
        h2b = (n2 * (1.0 + scale2) + shift2).astype(BF16)
        h2_ref[...] = h2b
        f = jnp.zeros((tm, D), F32)
        saved = []
        for a, b in FF_SPLITS:
            gp = _dot(h2b, wgu[:, a:b])
            upj = _dot(h2b, wgu[:, DFF + a:DFF + b])
            sg = _sigmoid(gp)
            sl = gp * sg
            actb = (sl * upj).astype(BF16)
            act_ref[:, a:b] = actb
            f = f + _dot(actb, wdn[a:b, :])
            saved.append((gp, upj, sg, sl))
        x2 = x1 + gate2 * f
        r3 = lax.rsqrt(jnp.mean(x2 * x2, axis=-1, keepdims=True) + EPS)
        xh3 = x2 * r3
        err = xh3 * fw - t_ref[...]
        sq_ref[...] += jnp.sum(err * err, axis=0, keepdims=True)
        dy = err * (1.0 / D)
        dfw = jnp.sum(dy * xh3, axis=0, keepdims=True)
        dxh3 = dy * fw
        dx2 = r3 * (dxh3 - xh3 * jnp.mean(dxh3 * xh3, axis=-1, keepdims=True))
        dgate2 = jnp.sum(dx2 * f, axis=0, keepdims=True)
        dfb = (dx2 * gate2).astype(BF16)
        df_ref[...] = dfb
        dh2 = jnp.zeros((tm, D), F32)
        for (a, b), (gp, upj, sg, sl) in zip(FF_SPLITS, saved):
            dact = _dot_nt(dfb, wdn[a:b, :])
            dg = (dact * upj * (sg * (1.0 + gp * (1.0 - sg)))).astype(BF16)
            du = (dact * sl).astype(BF16)
            dgu_ref[:, a:b] = dg
            dgu_ref[:, DFF + a:DFF + b] = du
            dh2 = dh2 + _dot_nt(dg, wgu[:, a:b]) + _dot_nt(du, wgu[:, DFF + a:DFF + b])
        dshift2 = jnp.sum(dh2, axis=0, keepdims=True)
        dscale2 = jnp.sum(dh2 * n2, axis=0, keepdims=True)
        dn2 = dh2 * (1.0 + scale2)
        dn2w = jnp.sum(dn2 * xh2, axis=0, keepdims=True)
        dxh2 = dn2 * n2w
        dx1 = dx2 + r2 * (dxh2 - xh2 * jnp.mean(dxh2 * xh2, axis=-1, keepdims=True))
        dx1_ref[...] = dx1
        dgate1 = jnp.sum(dx1 * o, axis=0, keepdims=True)
        dob = (dx1 * gate1).astype(BF16)
        do_ref[...] = dob
        dmix_ref[...] = _dot_nt(dob, wo[...])
        sm_ref[...] += jnp.concatenate(
            [dfw, dn2w, dshift2, dscale2, dgate2, dgate1, jnp.zeros((2, D), F32)], axis=0)

    row = lambda w: pl.BlockSpec((tm, w), lambda i: (i, 0))
    full = lambda a: pl.BlockSpec(a.shape, lambda i: (0,) * a.ndim)
    anyspec = pl.BlockSpec(memory_space=pl.ANY)
    return pl.pallas_call(
        body, name="mix_ffn", grid=(nt,),
        in_specs=[row(D), row(AW), row(SW), row(D), full(mod6), full(norm2_w), full(final_w), anyspec, anyspec, anyspec,
                  pl.BlockSpec(memory_space=pltpu.SMEM), anyspec, anyspec, anyspec],
        out_specs=[pl.BlockSpec((1, D), lambda i: (0, 0)), row(D), row(D), row(D),
                   row(DFF), row(D), row(2 * DFF), row(D), pl.BlockSpec((8, D), lambda i: (0, 0))],
        out_shape=[jax.ShapeDtypeStruct((1, D), F32), jax.ShapeDtypeStruct((T, D), F32), jax.ShapeDtypeStruct((T, D), F32),
                   jax.ShapeDtypeStruct((T, D), BF16), jax.ShapeDtypeStruct((T, DFF), BF16),
                   jax.ShapeDtypeStruct((T, D), BF16), jax.ShapeDtypeStruct((T, 2 * DFF), BF16),
                   jax.ShapeDtypeStruct((T, D), BF16), jax.ShapeDtypeStruct((8, D), F32)],
        scratch_shapes=[pltpu.VMEM((D, D), BF16), pltpu.VMEM((D, 2 * DFF), BF16), pltpu.VMEM((DFF, D), BF16),
                        pltpu.SemaphoreType.DMA((6,))],
        compiler_params=_cp("arbitrary"),
    )(x, attn, ynorm, tgt, mod6, norm2_w, final_w, w_out, w_gu, w_gu_own, s_arr, w_dn, w_out_own, w_dn_own)


def _in_proj_bwd(x, dx1, dqkv, dzxd, mod6, norm1_w, w_pad, tm, dep):
    T = x.shape[0]

    def body(x_ref, dx1_ref, dq_ref, dz_ref, mod_ref, nw_ref, w_hbm, dep_ref, gx_ref, sm_ref, w_vmem, sem):
        _load_resident(w_hbm, w_vmem, sem)

        @pl.when(pl.program_id(0) == 0)
        def _():
            sm_ref[...] = jnp.zeros_like(sm_ref)

        nw = nw_ref[...]
        scale1 = mod_ref[1:2, :]
        sums = jnp.zeros((8, D), F32)
        for rows in (slice(0, tm // 2), slice(tm // 2, tm)):
            dh = _dot_nt(dq_ref[rows, :], w_vmem[:, 0:768]) + _dot_nt(dz_ref[rows, :], w_vmem[:, 768:IN_PAD])
            xv = x_ref[rows, :]
            r = lax.rsqrt(jnp.mean(xv * xv, axis=-1, keepdims=True) + EPS)
            xh = xv * r
            n1 = xh * nw
            dshift = jnp.sum(dh, axis=0, keepdims=True)
            dscale = jnp.sum(dh * n1, axis=0, keepdims=True)
            dn = dh * (1.0 + scale1)
            dnw = jnp.sum(dn * xh, axis=0, keepdims=True)
            dxh = dn * nw
            gx_ref[rows, :] = dx1_ref[rows, :] + r * (dxh - xh * jnp.mean(dxh * xh, axis=-1, keepdims=True))
            sums = sums + jnp.concatenate([dnw, dshift, dscale, jnp.zeros((5, D), F32)], axis=0)
        sm_ref[...] += sums

    row = lambda w: pl.BlockSpec((tm, w), lambda i: (i, 0))
    full = lambda a: pl.BlockSpec(a.shape, lambda i: (0,) * a.ndim)
    return pl.pallas_call(
        body, name="in_proj_bwd", grid=(T // tm,),
        in_specs=[row(D), row(D), row(768), row(1664), full(mod6), full(norm1_w), pl.BlockSpec(memory_space=pl.ANY),
                  DEP_SPEC],
        out_specs=[row(D), pl.BlockSpec((8, D), lambda i: (0, 0))],
        out_shape=[jax.ShapeDtypeStruct((T, D), F32), jax.ShapeDtypeStruct((8, D), F32)],
        scratch_shapes=[pltpu.VMEM((D, IN_PAD), BF16), pltpu.SemaphoreType.DMA],
        compiler_params=_cp("arbitrary"),
    )(x, dx1, dqkv, dzxd, mod6, norm1_w, w_pad, dep)


def _tn_matmul(a, b, K, N, tt, name, dep):
    T = a.shape[0]
    ja, jb = a.shape[1] // K, b.shape[1] // N
    J = max(ja, jb)

    def body(a_ref, b_ref, dep_ref, o_ref):
        t = pl.program_id(1)
        prod = _dot_tn(a_ref[...], b_ref[...])

        @pl.when(t == 0)
        def _():
            o_ref[0] = prod

        @pl.when(t > 0)
        def _():
            o_ref[0] += prod

    return pl.pallas_call(
        body, name=name, grid=(J, T // tt),
        in_specs=[pl.BlockSpec((tt, K), lambda j, t: (t, j if ja > 1 else 0)),
                  pl.BlockSpec((tt, N), lambda j, t: (t, j if jb > 1 else 0)),
                  pl.BlockSpec((8, 128), lambda j, t: (0, 0))],
        out_specs=pl.BlockSpec((1, K, N), lambda j, t: (j, 0, 0)),
        out_shape=jax.ShapeDtypeStruct((J, K, N), F32),
        compiler_params=_cp("parallel", "arbitrary"),
    )(a, b, dep)


def _accumulate(o_ref, rows, prod):
    @pl.when(pl.program_id(0) == 0)
    def _():
        o_ref[rows, :] = prod

    @pl.when(pl.program_id(0) > 0)
    def _():
        o_ref[rows, :] += prod


def _tn_matmul_rows(a0, a1, b, tt, name, dep):
    T, K = a0.shape
    N = b.shape[1]

    def body(a0_ref, a1_ref, b_ref, dep_ref, o_ref):
        for k, a_ref in enumerate((a0_ref, a1_ref)):
            _accumulate(o_ref, slice(k * K, (k + 1) * K), _dot_tn(a_ref[...], b_ref[...]))

    tile = lambda w: pl.BlockSpec((tt, w), lambda t: (t, 0))
    return pl.pallas_call(
        body, name=name, grid=(T // tt,), in_specs=[tile(K), tile(K), tile(N), DEP_SPEC],
        out_specs=pl.BlockSpec((2 * K, N), lambda t: (0, 0)), out_shape=jax.ShapeDtypeStruct((2 * K, N), F32),
        compiler_params=_cp("arbitrary"),
    )(a0, a1, b, dep)


def _adam_math(w, g, m, v):
    m = B1 * m + (1.0 - B1) * g
    v = B2 * v + (1.0 - B2) * (g * g)
    m_hat = m / (1.0 - B1 ** STEP)
    v_hat = v / (1.0 - B2 ** STEP)
    delta = -LR * (m_hat / (jnp.sqrt(v_hat) + AEPS) + WD * w)
    return delta, m, v


def _adam_2d(w, mine, land, m, v, c_arr, rb, name, dep):
    R, C = w.shape
    nbh = R // 2 // rb

    def body(c_ref, w_ref, mine_ref, land_ref, m_ref, v_ref, dep_ref, go_ref, d_ref, mo_ref, vo_ref):
        g = jnp.where(pl.program_id(0) // nbh == c_ref[0], mine_ref[...], land_ref[...])
        d, mn, vn = _adam_math(w_ref[...], g, m_ref[...], v_ref[...])
        go_ref[...] = g
        d_ref[...] = d
        mo_ref[...] = mn
        vo_ref[...] = vn

    spec = pl.BlockSpec((rb, C), lambda i, c_ref: (i, 0))
    mine_spec = pl.BlockSpec((rb, C), lambda i, c_ref: (jnp.clip(i - c_ref[0] * nbh, 0, nbh - 1), 0))
    return pl.pallas_call(
        body, name=name,
        grid_spec=pltpu.PrefetchScalarGridSpec(
            num_scalar_prefetch=1, grid=(R // rb,), in_specs=[spec, mine_spec, spec, spec, spec, DEP_SPEC],
            out_specs=[spec] * 4),
        out_shape=[jax.ShapeDtypeStruct((R, C), F32)] * 4, compiler_params=_cp("parallel"),
    )(c_arr, w, mine, land, m, v, dep)


def _adam_w_in(w3, grad, m3, v3):
    n = w3.shape[0]

    def body(w_hbm, grad_ref, m_hbm, v_hbm, g_hbm, d_hbm, mo_hbm, vo_hbm, bufs, sems):
        ins = [pltpu.make_async_copy(src.at[:, 0], bufs.at[k], sems.at[k]) for k, src in enumerate((w_hbm, m_hbm, v_hbm))]
        for cp in ins:
            cp.start()
        g = grad_ref[...]
        eye =(_iota((D, D), 0) == _iota((D, D), 1)).astype(BF16)
        g_t = jnp.zeros((n, D), F32)
        r = g
        for i in range(3):
            p = r.astype(BF16)
            g_t = g_t + _dot_tn(p, eye)
            if i < 2:
                r = r - p.astype(F32)
        for cp in ins:
            cp.wait()
        d, mn, vn = _adam_math(bufs[0], g_t, bufs[1], bufs[2])
        for k, val in enumerate((g_t, d, mn, vn)):
            bufs[3 + k] = val
        outs = [pltpu.make_async_copy(bufs.at[3 + k], dst.at[:, 0], sems.at[3 + k])
                for k, dst in enumerate((g_hbm, d_hbm, mo_hbm, vo_hbm))]
        for cp in outs:
            cp.start()
        for cp in outs:
            cp.wait()

    anyspec = pl.BlockSpec(memory_space=pl.ANY)
    vm = pl.BlockSpec(memory_space=pltpu.VMEM)
    return pl.pallas_call(
        body, name="adam_w_in",
        in_specs=[anyspec, vm, anyspec, anyspec], out_specs=[anyspec] * 4,
        out_shape=[jax.ShapeDtypeStruct(w3.shape, F32)] * 4,
        scratch_shapes=[pltpu.VMEM((7, n, D), F32), pltpu.SemaphoreType.DMA((7,))],
        compiler_params=pltpu.CompilerParams(vmem_limit_bytes=VMEM_LIMIT),
    )(w3, grad, m3, v3)


def _adam_w_ada(gat, allv, s_arr, w, m, v, rb):
    R, C = w.shape

    def body(s_ref, c_ref, dm_ref, w_ref, m_ref, v_ref, g_ref, d_ref, mo_ref, vo_ref):
        cm = _rows_select(c_ref, rb)
        g = lax.dot_general(cm * _sigmoid(cm), _rows_select(dm_ref, C), (((0,), (0,)), ((), ())), precision=HI,
                            preferred_element_type=F32)
        d, mn, vn = _adam_math(w_ref[...], g, m_ref[...], v_ref[...])
        g_ref[...] = g
        d_ref[...] = d
        mo_ref[...] = mn
        vo_ref[...] = vn

    spec = pl.BlockSpec((rb, C), lambda i, s_ref: (i, 0))
    return pl.pallas_call(
        body, name="adam_w_ada",
        grid_spec=pltpu.PrefetchScalarGridSpec(
            num_scalar_prefetch=1, grid=(R // rb,),
            in_specs=[pl.BlockSpec((8, 1, rb), lambda i, s_ref: (0, 0, i)),
                      pl.BlockSpec((8, 1, C), lambda i, s_ref: (0, 0, s_ref[0])), spec, spec, spec],
            out_specs=[spec] * 4),
        out_shape=[jax.ShapeDtypeStruct((R, C), F32)] * 4, compiler_params=_cp("parallel"),
    )(s_arr, gat, allv, w, m, v)


def _adam_small(tot, segs, ws, ms, vs):
    k = len(ws)
    extra = [sg for sg in segs if not isinstance(sg, tuple)]
    ne = len(extra)

    def body(*refs):
        tot_ref, g_x = refs[0], list(refs[1:1 + ne])
        w, m, v = [refs[1 + ne + j * k:1 + ne + (j + 1) * k] for j in range(3)]
        g_o, d_o, m_o, v_o = [refs[1 + ne + (3 + j) * k:1 + ne + (4 + j) * k] for j in range(4)]
        for i in range(k):
            gi = tot_ref[:, segs[i][0]:segs[i][0] + segs[i][1]] if isinstance(segs[i], tuple) else g_x.pop(0)[...]
            d, mn, vn = _adam_math(w[i][...], gi, m[i][...], v[i][...])
            g_o[i][...] = gi
            d_o[i][...] = d
            m_o[i][...] = mn
            v_o[i][...] = vn

    shapes = [jax.ShapeDtypeStruct(w.shape, F32) for w in ws]
    vm = pl.BlockSpec(memory_space=pltpu.VMEM)
    outs = pl.pallas_call(
        body, name="adam_small", in_specs=[vm] * (1 + ne + 3 * k), out_specs=[vm] * (4 * k), out_shape=shapes * 4,
    )(tot, *extra, *ws, *ms, *vs)
    return outs[0:k], outs[k:2 * k], outs[2 * k:3 * k], outs[3 * k:4 * k]


def _pos():
    return lax.axis_index("x"), lax.axis_index("y"), lax.axis_index("c")


def _flip(v, bit):
    return 1 - v if bit else v


def _peer(k):
    x, y, c = _pos()
    return (_flip(x, (k >> 2) & 1), _flip(y, (k >> 1) & 1), _flip(c, k & 1))


def _logical(p):
    return 4 * p[0] + 2 * p[1] + p[2]


def _gather8(src_ref, dst_ref, send_sems, recv_sems, meanwhile):
    me = _logical(_pos())
    dst_ref[pl.ds(me, 1)] = src_ref[...][None]
    copies = []
    for k in range(1, 8):
        cp = pltpu.make_async_remote_copy(src_ref, dst_ref.at[me], send_sems.at[k - 1], recv_sems.at[k - 1],
                                          device_id=_peer(k), device_id_type=MESH)
        cp.start()
        copies.append(cp)
    meanwhile()
    for k in range(1, 8):
        pltpu.make_async_remote_copy(src_ref, dst_ref.at[_logical(_peer(k))], send_sems.at[k - 1], recv_sems.at[k - 1],
                                     device_id=_peer(k), device_id_type=MESH).wait_recv()
    for cp in copies:
        cp.wait_send()


def _rows_select(ref3, width):
    row = _iota((8, width), 0)
    out = jnp.zeros((8, width), F32)
    for i in range(8):
        out = jnp.where(row == i, ref3[i][:, 0:width], out)
    return out


def _mod_exchange(c_row, cw, w_ada_s, b_ada4, w_in3):
    n_sh = w_ada_s.shape[1]
    n_in = w_in3.shape[0]
    wide = -(-n_in // 128) * 128

    def body(c_ref, cw_ref, w_hbm, b_ref, win_hbm, gat_ref, mod_ref, token, winb_ref, pay_ref, p3, w_v, win_v, win_z,
             sa, ra, sb, rb, ls):
        token[...] = jnp.zeros_like(token)
        pay_ref[:, 0:D] = c_ref[...]
        for k in range(CONVK):
            pay_ref[:, D + 256 * k:D + 256 * (k + 1)] = cw_ref[k:k + 1, :]
        x, y, c = _pos()
        me = _logical((x, y, c))
        my_s = 2 * x + y
        load_w = pltpu.make_async_copy(w_hbm, w_v, ls.at[0])
        load_in = pltpu.make_async_copy(win_hbm.at[:, 0], win_v, ls.at[1])
        load_w.start()
        load_in.start()

        def local_work():
            win_z[...] = jnp.zeros_like(win_z)
            load_in.wait()
            win_z[0:n_in, :] = win_v[...].astype(BF16)
            eye = (_iota((wide, wide), 0) == _iota((wide, wide), 1)).astype(BF16)
            winb_ref[...] = _dot_tn(win_z[...], eye)[:, 0:n_in].astype(BF16)
            load_w.wait()

        _gather8(pay_ref, gat_ref, sa, ra, local_work)
        cmat = _rows_select(gat_ref, D)
        prod = _dot_hi(cmat * _sigmoid(cmat), w_v[...])
        for b in range(8):
            p3[b] = prod[b:b + 1, :]
        mod_ref[pl.ds(my_s, 1)] = p3[pl.ds(me, 1)] + b_ref[pl.ds(my_s, 1)]
        ks = (2, 4, 6)
        copies = []
        for i, k in enumerate(ks):
            pr = _peer(k)
            cp = pltpu.make_async_remote_copy(p3.at[_logical(pr)], mod_ref.at[my_s], sb.at[i], rb.at[i],
                                              device_id=pr, device_id_type=MESH)
            cp.start()
            copies.append(cp)
        for i, k in enumerate(ks):
            pr = _peer(k)
            s_src = 2 * pr[0] + pr[1]
            pltpu.make_async_remote_copy(p3.at[0], mod_ref.at[s_src], sb.at[i], rb.at[i],
                                         device_id=pr, device_id_type=MESH).wait_recv()
            mod_ref[pl.ds(s_src, 1)] = mod_ref[pl.ds(s_src, 1)] + b_ref[pl.ds(s_src, 1)]
        for cp in copies:
            cp.wait_send()

    vm = pl.BlockSpec(memory_space=pltpu.VMEM)
    anyspec = pl.BlockSpec(memory_space=pl.ANY)
    return pl.pallas_call(
        body, name="mod_exchange", in_specs=[vm, vm, anyspec, vm, anyspec], out_specs=[vm, vm, vm, vm],
        out_shape=[jax.ShapeDtypeStruct((8, 1, D + CONVK * 256), F32), jax.ShapeDtypeStruct((4, 1, n_sh), F32),
                   jax.ShapeDtypeStruct((8, 128), F32), jax.ShapeDtypeStruct((D, n_in), BF16)],
        scratch_shapes=[pltpu.VMEM((1, D + CONVK * 256), F32), pltpu.VMEM((8, 1, n_sh), F32), pltpu.VMEM(w_ada_s.shape, F32), pltpu.VMEM((n_in, D), F32),
                        pltpu.VMEM((wide, D), BF16), pltpu.SemaphoreType.DMA((7,)), pltpu.SemaphoreType.DMA((7,)),
                        pltpu.SemaphoreType.DMA((3,)), pltpu.SemaphoreType.DMA((3,)), pltpu.SemaphoreType.DMA((2,))],
        compiler_params=pltpu.CompilerParams(vmem_limit_bytes=VMEM_LIMIT),
    )(c_row, cw, w_ada_s, b_ada4, w_in3)


def _chips():
    x, y, _ = _pos()
    out = []
    for k in (1, 2, 3):
        px, py = _flip(x, (k >> 1) & 1), _flip(y, k & 1)
        out.append((px, py, 2 * px + py))
    return out


def _half_rows(ref, which):
    half = ref.shape[-2] // 2
    return pl.ds(pl.multiple_of(which * half, 8), half)


def _plan_small():
    def plan(refs):
        me = _logical(_pos())
        return [(refs[0], refs[1].at[me], _peer(k), refs[1].at[_logical(_peer(k))]) for k in range(1, 8)]
    return plan


def _small_sum(vec, land, me_arr):
    n = vec.shape[1]

    def body(me_ref, v_ref, land_ref, tot_ref, all_ref):
        tot = None
        for i in range(8):
            row = jnp.where(me_ref[0] == i, v_ref[...], land_ref[i])
            all_ref[i] = row
            tot = row if i == 0 else tot + row
        tot_ref[...] = tot

    return pl.pallas_call(
        body, name="small_sum",
        grid_spec=pltpu.PrefetchScalarGridSpec(
            num_scalar_prefetch=1, grid=(1,),
            in_specs=[pl.BlockSpec((1, n), lambda i, me_ref: (0, 0)), pl.BlockSpec((8, 1, n), lambda i, me_ref: (0, 0, 0))],
            out_specs=[pl.BlockSpec((1, n), lambda i, me_ref: (0, 0)),
                       pl.BlockSpec((8, 1, n), lambda i, me_ref: (0, 0, 0))]),
        out_shape=[jax.ShapeDtypeStruct((1, n), F32), jax.ShapeDtypeStruct((8, 1, n), F32)],
        compiler_params=_cp("arbitrary"),
    )(me_arr, vec, land)


def _add_half(g, sib, c_arr, rb, name):
    _, R, C = g.shape
    half = R // 2
    nb = half // rb

    def body(c_ref, g_ref, s_ref, o_ref):
        o_ref[...] = (g_ref[...] + s_ref[...]).astype(BF16)

    return pl.pallas_call(
        body, name=name,
        grid_spec=pltpu.PrefetchScalarGridSpec(
            num_scalar_prefetch=1, grid=(4, nb),
            in_specs=[pl.BlockSpec((1, rb, C), lambda s, i, c_ref: (s, c_ref[0] * nb + i, 0)),
                      pl.BlockSpec((1, rb, C), lambda s, i, c_ref: (s, i, 0))],
            out_specs=pl.BlockSpec((1, rb, C), lambda s, i, c_ref: (s, i, 0))),
        out_shape=jax.ShapeDtypeStruct((4, half, C), BF16),
        compiler_params=_cp("parallel", "parallel"),
    )(c_arr, g, sib)


def _add_half_in(gq, gz, sibq, sibz, c_arr, rb):
    half = D // 2
    nq = gq.shape[1]
    wide = -(-IN_SH // 128) * 128

    def sel(rows, first, lo):
        return (_iota((rows, wide), 0) + (first - lo) == _iota((rows, wide), 1)).astype(BF16)

    def body(c_ref, gq_ref, gz_ref, sq_ref, sz_ref, o_ref):
        q = (gq_ref[...] + sq_ref[...]).astype(BF16)
        z = (gz_ref[...] + sz_ref[...]).astype(BF16)
        for s in range(4):
            lo, hi = s * IN_SH, (s + 1) * IN_SH
            acc = jnp.zeros((rb, wide), F32)
            if lo < nq:
                a0, a1 = lo // 128 * 128, min(nq, -(-min(hi, nq) // 128) * 128)
                acc = acc + _dot(q[:, a0:a1], sel(a1 - a0, a0, lo))
            if hi > nq:
                a0, a1 = (max(lo, nq) - nq) // 128 * 128, -(-(hi - nq) // 128) * 128
                acc = acc + _dot(z[:, a0:a1], sel(a1 - a0, nq + a0, lo))
            o_ref[s] = acc[:, :IN_SH].astype(BF16)

    nb = half // rb
    mine = lambda w: pl.BlockSpec((rb, w), lambda i, c_ref: (c_ref[0] * nb + i, 0))
    sib = lambda w: pl.BlockSpec((rb, w), lambda i, c_ref: (i, 0))
    return pl.pallas_call(
        body, name="grad_add_in",
        grid_spec=pltpu.PrefetchScalarGridSpec(
            num_scalar_prefetch=1, grid=(nb,),
            in_specs=[mine(nq), mine(gz.shape[1]), sib(nq), sib(gz.shape[1])],
            out_specs=pl.BlockSpec((4, rb, IN_SH), lambda i, c_ref: (0, i, 0))),
        out_shape=jax.ShapeDtypeStruct((4, half, IN_SH), BF16),
        compiler_params=_cp("parallel"),
    )(c_arr, gq, gz, sibq, sibz)


def _sum4(parts, land, s_arr, rb, name):
    _, H, C = land.shape

    def body(s_ref, own_ref, r_ref, o_ref):
        own = own_ref[0].astype(F32)
        tot = jnp.zeros((rb, C), F32)
        for j in range(4):
            tot = tot + jnp.where(s_ref[0] == j, own, r_ref[j].astype(F32))
        o_ref[...] = tot

    return pl.pallas_call(
        body, name=name,
        grid_spec=pltpu.PrefetchScalarGridSpec(
            num_scalar_prefetch=1, grid=(H // rb,),
            in_specs=[pl.BlockSpec((1, rb, C), lambda i, s_ref: (s_ref[0], i, 0)),
                      pl.BlockSpec((4, rb, C), lambda i, s_ref: (0, i, 0))],
            out_specs=pl.BlockSpec((rb, C), lambda i, s_ref: (i, 0))),
        out_shape=jax.ShapeDtypeStruct((H, C), F32), compiler_params=_cp("parallel"),
    )(s_arr, parts, land)


HBM_SPEC = pl.BlockSpec(memory_space=pltpu.HBM)
SEM_SPEC = pl.BlockSpec(memory_space=pltpu.SEMAPHORE)
EFFECT = pltpu.SideEffectType.DATAFLOW_SIDE_EFFECTING


def _split_start(name, bufs, n_sem, plan, dep):
    nb = len(bufs)

    def body(*refs):
        ins, send, recv, token = refs[:nb], refs[nb + 1], refs[nb + 2], refs[-1]
        for i, (src, dst, dev, _) in enumerate(plan(ins)):
            pltpu.make_async_remote_copy(src, dst, send.at[i], recv.at[i], device_id=dev, device_id_type=MESH).start()
        token[...] = jnp.zeros_like(token)

    outs = pl.pallas_call(
        body, name=name,
        out_shape=(pltpu.SemaphoreType.DMA((n_sem,)), pltpu.SemaphoreType.DMA((n_sem,)),
                   *[pltpu.HBM(b.shape, b.dtype) for b in bufs], jax.ShapeDtypeStruct((8, 128), F32)),
        in_specs=[HBM_SPEC] * nb + [pl.BlockSpec(memory_space=pl.ANY)],
        out_specs=(SEM_SPEC, SEM_SPEC, *([HBM_SPEC] * nb), pl.BlockSpec(memory_space=pltpu.VMEM)),
        input_output_aliases={i: 2 + i for i in range(nb)},
        compiler_params=pltpu.CompilerParams(has_side_effects=EFFECT),
    )(*[pltpu.with_memory_space_constraint(b, pltpu.HBM) for b in bufs], dep)
    return outs[0], outs[1], list(outs[2:2 + nb]), outs[-1]


def _split_wait(name, send, recv, bufs, after, plan):
    nb = len(bufs)
    after = list(after) if isinstance(after, (list, tuple)) else [after]

    def body(*refs):
        ins, send_s, recv_s = refs[:nb], refs[nb], refs[nb + 1]
        for i, (src, dst, dev, mine) in enumerate(plan(ins)):
            pltpu.make_async_remote_copy(src, dst, send_s.at[i], recv_s.at[i], device_id=dev,
                                         device_id_type=MESH).wait_send()
            pltpu.make_async_remote_copy(src, mine, send_s.at[i], recv_s.at[i], device_id=dev,
                                         device_id_type=MESH).wait_recv()

    outs = pl.pallas_call(
        body, name=name, out_shape=[pltpu.HBM(b.shape, b.dtype) for b in bufs],
        in_specs=[HBM_SPEC] * nb + [SEM_SPEC, SEM_SPEC] + [HBM_SPEC] * len(after),
        out_specs=[HBM_SPEC] * nb, input_output_aliases={i: i for i in range(nb)},
        compiler_params=pltpu.CompilerParams(has_side_effects=EFFECT),
    )(*bufs, send, recv, *[pltpu.with_memory_space_constraint(a, pltpu.HBM) for a in after])
    return list(outs)


def _split_wait_start(name, send, recv, bufs, after, plan, bufs2, n_sem2, plan2):
    nb, nb2 = len(bufs), len(bufs2)
    after = list(after) if isinstance(after, (list, tuple)) else [after]
    n_in = nb + 2 + nb2 + len(after)

    def body(*refs):
        ins, send_s, recv_s, ins2 = refs[:nb], refs[nb], refs[nb + 1], refs[nb + 2:nb + 2 + nb2]
        send2, recv2, token = refs[n_in + nb], refs[n_in + nb + 1], refs[-1]
        for i, (src, dst, dev, mine) in enumerate(plan(ins)):
            pltpu.make_async_remote_copy(src, dst, send_s.at[i], recv_s.at[i], device_id=dev,
                                         device_id_type=MESH).wait_send()
            pltpu.make_async_remote_copy(src, mine, send_s.at[i], recv_s.at[i], device_id=dev,
                                         device_id_type=MESH).wait_recv()
        for i, (src, dst, dev, _) in enumerate(plan2(ins2)):
            pltpu.make_async_remote_copy(src, dst, send2.at[i], recv2.at[i], device_id=dev, device_id_type=MESH).start()
        token[...] = jnp.zeros_like(token)

    hbm = lambda b: pltpu.with_memory_space_constraint(b, pltpu.HBM)
    outs = pl.pallas_call(
        body, name=name,
        out_shape=(*[pltpu.HBM(b.shape, b.dtype) for b in bufs], pltpu.SemaphoreType.DMA((n_sem2,)),
                   pltpu.SemaphoreType.DMA((n_sem2,)), *[pltpu.HBM(b.shape, b.dtype) for b in bufs2],
                   jax.ShapeDtypeStruct((8, 128), F32)),
        in_specs=[HBM_SPEC] * nb + [SEM_SPEC, SEM_SPEC] + [HBM_SPEC] * (nb2 + len(after)),
        out_specs=(*([HBM_SPEC] * nb), SEM_SPEC, SEM_SPEC, *([HBM_SPEC] * nb2), pl.BlockSpec(memory_space=pltpu.VMEM)),
        input_output_aliases={**{i: i for i in range(nb)}, **{nb + 2 + j: nb + 2 + j for j in range(nb2)}},
        compiler_params=pltpu.CompilerParams(has_side_effects=EFFECT),
    )(*bufs, send, recv, *[hbm(b) for b in bufs2], *[hbm(a) for a in after])
    return list(outs[:nb]), outs[nb], outs[nb + 1], list(outs[nb + 2:nb + 2 + nb2]), outs[-1]


def _copies_now(name, bufs, n_sem, plan):
    nb = len(bufs)

    def body(*refs):
        ins, token, send, recv = refs[:nb], refs[2 * nb], refs[-2], refs[-1]
        token[...] = jnp.zeros_like(token)
        todo = plan(ins)
        for i, (src, dst, dev, _) in enumerate(todo):
            pltpu.make_async_remote_copy(src, dst, send.at[i], recv.at[i], device_id=dev, device_id_type=MESH).start()
        for i, (src, dst, dev, mine) in enumerate(todo):
            pltpu.make_async_remote_copy(src, mine, send.at[i], recv.at[i], device_id=dev, device_id_type=MESH).wait_recv()
        for i, (src, dst, dev, _) in enumerate(todo):
            pltpu.make_async_remote_copy(src, dst, send.at[i], recv.at[i], device_id=dev, device_id_type=MESH).wait_send()

    outs = pl.pallas_call(
        body, name=name,
        out_shape=[pltpu.HBM(b.shape, b.dtype) for b in bufs] + [jax.ShapeDtypeStruct((8, 128), F32)],
        in_specs=[HBM_SPEC] * nb, out_specs=[HBM_SPEC] * nb + [pl.BlockSpec(memory_space=pltpu.VMEM)],
        input_output_aliases={i: i for i in range(nb)},
        scratch_shapes=[pltpu.SemaphoreType.DMA((n_sem,)), pltpu.SemaphoreType.DMA((n_sem,))],
    )(*[pltpu.with_memory_space_constraint(b, pltpu.HBM) for b in bufs])
    return list(outs[:nb]), outs[nb]


def _slot(land, s, rows, cols):
    if cols is None:
        return land.at[s, rows]
    return land.at[rows, pl.ds(pl.multiple_of(s * cols, 128), cols)]


def _plan_gather_ici(cols):
    nw = len(cols)

    def plan(refs):
        x, y, c = _pos()
        my_s = 2 * x + y
        out = []
        for w in range(nw):
            mine = _half_rows(refs[w], c)
            for px, py, ps in _chips():
                out.append((refs[w].at[mine], _slot(refs[nw + w], my_s, mine, cols[w]), (px, py, c),
                            _slot(refs[nw + w], ps, mine, cols[w])))
        return out
    return plan


def _plan_gather_fwd(cols, rows):
    def plan(refs):
        x, y, c = _pos()
        out = []
        for w in range(len(cols)):
            half = rows[w] // 2
            mine = pl.ds(pl.multiple_of(c * half, 8), half)
            other = pl.ds(pl.multiple_of((1 - c) * half, 8), half)
            for px, py, ps in _chips():
                got = _slot(refs[w], ps, mine, cols[w])
                out.append((got, got, (x, y, 1 - c), _slot(refs[w], ps, other, cols[w])))
        return out
    return plan


def _plan_swap(nw):
    def plan(refs):
        x, y, c = _pos()
        return [(refs[w].at[:, _half_rows(refs[w], 1 - c)], refs[nw + w], (x, y, 1 - c), refs[nw + w])
                for w in range(nw)]
    return plan


def _plan_swap_rows(nw):
    def plan(refs):
        x, y, c = _pos()
        return [(refs[w].at[_half_rows(refs[w], 1 - c)], refs[nw + w], (x, y, 1 - c), refs[nw + w])
                for w in range(nw)]
    return plan


def _plan_scatter(nw):
    def plan(refs):
        x, y, c = _pos()
        my_s = 2 * x + y
        out = []
        for w in range(nw):
            for px, py, ps in _chips():
                out.append((refs[w].at[ps], refs[nw + w].at[my_s], (px, py, c), refs[nw + w].at[ps]))
        return out
    return plan


def _plan_scatter_both():
    def plan(refs):
        x, y, c = _pos()
        my_s = 2 * x + y
        src, land = refs
        out = []
        for px, py, ps in _chips():
            out.append((src.at[ps], land.at[my_s, c], (px, py, c), land.at[ps, c]))
            out.append((src.at[ps], land.at[my_s, c], (px, py, 1 - c), land.at[ps, 1 - c]))
        out.append((src.at[my_s], land.at[my_s, c], (x, y, 1 - c), land.at[my_s, 1 - c]))
        return out
    return plan


def _sum4_both(parts, land, s_arr, c_arr):
    _, _, H, C = land.shape

    def body(s_ref, c_ref, own_ref, r_ref, o_ref):
        mine = pl.program_id(0) == c_ref[0]
        own = own_ref[0].astype(F32)
        tot = jnp.zeros((H, C), F32)
        for j in range(4):
            tot = tot + jnp.where(jnp.logical_and(mine, s_ref[0] == j), own, r_ref[j, 0].astype(F32))
        o_ref[0] = tot

    return pl.pallas_call(
        body, name="grad_sum_in",
        grid_spec=pltpu.PrefetchScalarGridSpec(
            num_scalar_prefetch=2, grid=(2,),
            in_specs=[pl.BlockSpec((1, H, C), lambda h, s_ref, c_ref: (s_ref[0], 0, 0)),
                      pl.BlockSpec((4, 1, H, C), lambda h, s_ref, c_ref: (0, h, 0, 0))],
            out_specs=pl.BlockSpec((1, H, C), lambda h, s_ref, c_ref: (h, 0, 0))),
        out_shape=jax.ShapeDtypeStruct((2, H, C), F32), compiler_params=_cp("parallel"),
    )(s_arr, c_arr, parts, land).reshape(2 * H, C)


def _plan_join(nw):
    def plan(refs):
        x, y, c = _pos()
        out = []
        for w in range(nw):
            land = refs[nw + w]
            out.append((refs[w], land.at[_half_rows(land, c)], (x, y, 1 - c), land.at[_half_rows(land, 1 - c)]))
        return out
    return plan


def _hbm_empty(shape, dtype):
    return pltpu.with_memory_space_constraint(lax.empty(shape, dtype), pltpu.HBM)


def _w_in_assemble(land, own, s_arr, rb):
    wide = -(-IN_SH // 128) * 128
    starts = [s * IN_SH // 128 * 128 for s in range(4)]
    ends = [min(IN_PAD, -(-(s + 1) * IN_SH // 128) * 128) for s in range(4)]

    def body(s_ref, land_ref, own_ref, o_ref, parts):
        @pl.when(pl.program_id(0) == 0)
        def _():
            parts[...] = jnp.zeros_like(parts)

        acc = []
        for s in range(4):
            parts[s, :, 0:IN_SH] = jnp.where(s_ref[0] == s, own_ref[...], land_ref[s])
            w = ends[s] - starts[s]
            sel = (_iota((wide, w), 0) + (s * IN_SH - starts[s]) == _iota((wide, w), 1)).astype(BF16)
            acc.append(_dot(parts[s], sel))
        for s in range(4):
            lo = starts[s] if s == 0 else ends[s - 1]
            hi = starts[s + 1] if s < 3 else ends[s]
            o_ref[:, lo:hi] = acc[s][:, lo - starts[s]:hi - starts[s]].astype(BF16)
            if s < 3:
                a, b = starts[s + 1], ends[s]
                o_ref[:, a:b] = (acc[s][:, a - starts[s]:b - starts[s]] + acc[s + 1][:, 0:b - a]).astype(BF16)

    return pl.pallas_call(
        body, name="w_in_assemble",
        grid_spec=pltpu.PrefetchScalarGridSpec(
            num_scalar_prefetch=1, grid=(D // rb,),
            in_specs=[pl.BlockSpec((4, rb, IN_SH), lambda i, s_ref: (0, i, 0)),
                      pl.BlockSpec((rb, IN_SH), lambda i, s_ref: (i, 0))],
            out_specs=pl.BlockSpec((rb, IN_PAD), lambda i, s_ref: (i, 0)),
            scratch_shapes=[pltpu.VMEM((4, rb, wide), BF16)]),
        out_shape=jax.ShapeDtypeStruct((D, IN_PAD), BF16), compiler_params=_cp("arbitrary"),
    )(s_arr, land, own)


def _pad_lanes(a, n):
    return jnp.pad(a, ((0, 0), (0, n - a.shape[1])))


def kernel(x, c, positions, w_ada, b_ada, norm1_w, w_in, conv_w, conv_b, dt_bias, a_log, d_skip, attn_sinks, ssm_norm_w, w_out, norm2_w, w_gate_up, w_down, final_norm_w, loss_target, m_w_ada, m_b_ada, m_norm1_w, m_w_in, m_conv_w, m_conv_b, m_dt_bias, m_a_log, m_d_skip, m_attn_sinks, m_ssm_norm_w, m_w_out, m_norm2_w, m_w_gate_up, m_w_down, m_final_norm_w, v_w_ada, v_b_ada, v_norm1_w, v_w_in, v_conv_w, v_conv_b, v_dt_bias, v_a_log, v_d_skip, v_attn_sinks, v_ssm_norm_w, v_w_out, v_norm2_w, v_w_gate_up, v_w_down, v_final_norm_w):
    T = x.shape[1]
    tm = min(256, T)
    xi, yi, ci = lax.axis_index("x"), lax.axis_index("y"), lax.axis_index("c")
    my_s = 2 * xi + yi
    xs = x[0]
    tgt = loss_target[0]

    gat, mod4, tok, w_in_b = _mod_exchange(c, conv_w[0], w_ada[0], b_ada.reshape(4, 1, 1536), w_in.transpose(2, 0, 1))
    mod6 = mod4.reshape(6, D)
    cw_dev = gat[:, 0, D:].reshape(4, 2, CONVK, 256)[:, 0]
    conv_full = cw_dev.transpose(1, 0, 2).reshape(CONVK, CONVC)

    s_i, r_i, bufs, tok = _split_start("wgather_in_ici_start", [w_in_b, _hbm_empty((4,) + w_in_b.shape, BF16)], 3,
                                       _plan_gather_ici([None]), tok)
    inv_freq = (10000.0 ** (-jnp.arange(32, dtype=F32) / 32))
    cos, sin_s = _rope_tables(positions, inv_freq.reshape(32, 1), min(512, T), tok)
    late = [w_out[0].astype(BF16), w_gate_up[0].astype(BF16), w_down[0].astype(BF16)]
    lands = [_hbm_empty((4, D // 4, D), BF16), _hbm_empty((D, 2 * DFF), BF16), _hbm_empty((4, DFF // 4, D), BF16)]
    cols3, rows3 = [None, GU_SH, None], [D // 4, D, DFF // 4]
    bufs, s_a, r_a, bufs_late, tok = _split_wait_start(
        "wgather_in_ici_wait", s_i, r_i, bufs, cos, _plan_gather_ici([None]), late + lands, 9, _plan_gather_ici(cols3))
    own_in = bufs[0]
    bufs, tok = _copies_now("wgather_in_fwd", bufs[1:], 3, _plan_gather_fwd([None], [D]))
    s_arr = my_s.reshape(1).astype(jnp.int32)
    w_pad = _w_in_assemble(bufs[0], own_in, s_arr, 512)
    bufs = bufs_late

    qkv, z, xbc, dtr, h1b = _in_proj_fwd(xs, cos, sin_s, mod6, norm1_w, w_pad, min(512, T), tok)
    sinks = attn_sinks
    attn, lse = _attn_fwd(qkv, sinks)
    bufs = _split_wait("wgather_ici_wait", s_a, r_a, bufs, attn, _plan_gather_ici(cols3))
    late = bufs[:3]
    s_b, r_b, lands, tok = _split_start("wgather_fwd_start", bufs[3:], 9, _plan_gather_fwd(cols3, rows3), attn)
    dtb = _pad_lanes(dt_bias, 128)
    alog = _pad_lanes(a_log, 128)
    dskx = jnp.repeat(d_skip, HD, axis=1)
    mats = _ssd_mats()
    ynorm, ypre, states, conv_pre = _ssd_fwd(xbc, z, dtr, conv_full, conv_b, dtb, alog, dskx, ssm_norm_w, mats, tok)
    lands = _split_wait("wgather_fwd_wait", s_b, r_b, lands, ynorm, _plan_gather_fwd(cols3, rows3))
    w_out_f = lands[0].reshape(D, D)
    w_dn_f = lands[2].reshape(DFF, D)

    fw2 = final_norm_w.reshape(1, D)
    sq, dmix, dx1, h2b, act, dfb, dgu, dob, sm_ffn = _mix_ffn(
        xs, attn, ynorm, tgt, mod6, norm2_w, fw2, w_out_f, lands[1], late[1], s_arr, w_dn_f, late[0], late[2], tm)

    tt = min(2048, T)
    c_arr = ci.reshape(1).astype(jnp.int32)
    tok0 = jnp.zeros((8, 128), F32)
    gw_dn4 = _tn_matmul(act, dfb, GU_SH, D, tt, "dw_down", tok0).reshape(4, DFF // 4, D)
    gw_gu4 = _tn_matmul(h2b, dgu, D, GU_SH, tt, "dw_gate_up", tok0)
    gw_out4 = _tn_matmul_rows(attn, ynorm, dob, tt, "dw_out", tok0).reshape(4, D // 4, D)
    big1 = [gw_out4, gw_gu4, gw_dn4]
    rbs1 = [128, 512, 352]
    sib1 = [_hbm_empty((4, g.shape[1] // 2, g.shape[2]), F32) for g in big1]
    s_c, r_c, bufs, tok = _split_start("gswap_start", big1 + sib1, 3, _plan_swap(3), tok0)

    dzxd, d_cw, d_cb, d_sw, d_sk, d_dtb, d_av = _ssd_bwd(
        xbc, conv_pre, z, dtr, ypre, states, dmix, conv_full, dtb, alog, dskx, ssm_norm_w, mats, tok)
    bufs = _split_wait("gswap_wait", s_c, r_c, bufs, dzxd, _plan_swap(3))
    sums1 = [_add_half(g, s, c_arr, rb, "grad_add_%d" % i)
             for i, (g, s, rb) in enumerate(zip(bufs[:3], bufs[3:], rbs1))]
    land1 = [_hbm_empty(p.shape, BF16) for p in sums1]
    s_d, r_d, bufs, tok = _split_start("gscatter_start", sums1 + land1, 9, _plan_scatter(3), tok0)
    dqkv, d_sinks = _attn_bwd(qkv, sinks, lse, dmix, cos, sin_s, tok)
    bufs = _split_wait("gscatter_wait", s_d, r_d, bufs, dqkv, _plan_scatter(3))
    halves1 = [_sum4(p, l, s_arr, rb, "grad_sum_%d" % i)
               for i, (p, l, rb) in enumerate(zip(bufs[:3], bufs[3:], rbs1))]
    full1 = [_hbm_empty((2 * h.shape[0], h.shape[1]), F32) for h in halves1]
    s_e, r_e, bufs, tok = _split_start("gjoin_start", halves1 + full1, 3, _plan_join(3), tok0)
    gq = _tn_matmul(h1b, dqkv, D, 768, tt, "dw_in_qkv", tok)[0]
    gz = _tn_matmul(h1b, dzxd, D, IN_PAD - 768, tt, "dw_in_zxd", tok)[0]
    joined1 = _split_wait("gjoin_wait", s_e, r_e, bufs, [gq, gz], _plan_join(3))

    sibs = [_hbm_empty((D // 2, g.shape[1]), F32) for g in (gq, gz)]
    s_f, r_f, bufs, tok = _split_start("gswap_in_start", [gq, gz] + sibs, 2, _plan_swap_rows(2), tok0)
    g_dn_s, d_dn, m_dn, v_dn = _adam_2d(w_down[0], joined1[2], joined1[5], m_w_down[0], v_w_down[0], c_arr, 352,
                                        "adam_w_down", tok)
    g_gu_s, d_gu, m_gu, v_gu = _adam_2d(w_gate_up[0], joined1[1], joined1[4], m_w_gate_up[0], v_w_gate_up[0], c_arr,
                                        256, "adam_w_gate_up", tok)
    g_out_s, d_out, m_out, v_out = _adam_2d(w_out[0], joined1[0], joined1[3], m_w_out[0], v_w_out[0], c_arr, 128,
                                            "adam_w_out", tok)
    bufs = _split_wait("gswap_in_wait", s_f, r_f, bufs, [d_dn, d_gu, d_out], _plan_swap_rows(2))
    sum0 = _add_half_in(bufs[0], bufs[1], bufs[2], bufs[3], c_arr, D // 2)
    s_g, r_g, bufs, tok = _split_start("gscatter_in_start", [sum0, _hbm_empty((4, 2) + sum0.shape[1:], BF16)], 7,
                                       _plan_scatter_both(), tok0)
    grad_x, sm_in = _in_proj_bwd(xs, dx1, dqkv, dzxd, mod6, norm1_w, w_pad, min(512, T), tok)

    a_neg = -jnp.exp(alog)
    pieces = [sm_in[1:2], sm_in[2:3], sm_ffn[5:6], sm_ffn[2:3], sm_ffn[3:4], sm_ffn[4:5],
              sm_in[0:1], sm_ffn[1:2], sm_ffn[0:1], d_cb, d_cw.reshape(1, CONVK * CONVC),
              _pad_lanes(d_sw, SW), d_dtb, d_av * a_neg, d_sk, d_sinks,
              _pad_lanes((0.5 / D * jnp.sum(sq)).reshape(1, 1), 128)]
    vec = jnp.concatenate(pieces, axis=1)
    s_h, r_h, rows8, tok_small = _split_start("small_start", [vec, _hbm_empty((8,) + vec.shape, F32)], 7,
                                              _plan_small(), tok0)

    bufs = _split_wait("gscatter_in_wait", s_g, r_g, bufs, [grad_x, tok_small], _plan_scatter_both())
    gw_in_s = _sum4_both(bufs[0], bufs[1], s_arr, c_arr)
    native = lambda a: a.transpose(2, 0, 1)
    adam_in = _adam_w_in(native(w_in), gw_in_s, native(m_w_in), native(v_w_in))
    g_in_s, d_in, m_in, v_in = [a.transpose(1, 2, 0) for a in adam_in]
    rows8 = _split_wait("small_wait", s_h, r_h, rows8, [adam_in[1]], _plan_small())
    tot, allv = _small_sum(rows8[0], rows8[1], (4 * xi + 2 * yi + ci).reshape(1).astype(jnp.int32))
    o = 0
    offs = []
    for p in pieces:
        offs.append(o)
        o += p.shape[1]
    seg = lambda i, n: (offs[i], n)
    g_conv_w = lax.dynamic_slice_in_dim(
        tot[:, offs[10]:offs[10] + CONVK * CONVC].reshape(CONVK, CONVC), my_s * 256, 256, axis=1)
    loss = tot[0, offs[16]]

    small_names = ["b_ada", "norm1_w", "conv_w", "conv_b", "dt_bias", "a_log", "d_skip", "attn_sinks", "ssm_norm_w",
                   "norm2_w", "final_norm_w"]
    small_g = [(0, 6 * D), seg(6, D), g_conv_w, seg(9, D), seg(12, 8), seg(13, 8), seg(14, 8), seg(15, 8),
               seg(11, SW), seg(7, D), seg(8, D)]
    as2d = lambda a: a.reshape(-1, a.shape[-1])
    small_w = [as2d(a) for a in (b_ada, norm1_w, conv_w, conv_b, dt_bias, a_log, d_skip, attn_sinks, ssm_norm_w,
                                 norm2_w, final_norm_w)]
    small_m = [as2d(a) for a in (m_b_ada, m_norm1_w, m_conv_w, m_conv_b, m_dt_bias, m_a_log, m_d_skip, m_attn_sinks,
                                 m_ssm_norm_w, m_norm2_w, m_final_norm_w)]
    small_v = [as2d(a) for a in (v_b_ada, v_norm1_w, v_conv_w, v_conv_b, v_dt_bias, v_a_log, v_d_skip, v_attn_sinks,
                                 v_ssm_norm_w, v_norm2_w, v_final_norm_w)]
    small_g, sd, smn, svn = _adam_small(tot, small_g, small_w, small_m, small_v)
    g_ada, d_ada, m_ada, v_ada = _adam_w_ada(gat, allv, s_arr, w_ada[0], m_w_ada[0], v_w_ada[0], 256)

    order = ["w_ada", "b_ada", "norm1_w", "w_in", "conv_w", "conv_b", "dt_bias", "a_log", "d_skip", "attn_sinks",
             "ssm_norm_w", "w_out", "norm2_w", "w_gate_up", "w_down", "final_norm_w"]
    shapes = dict(w_ada=w_ada.shape, b_ada=b_ada.shape, norm1_w=norm1_w.shape, w_in=w_in.shape, conv_w=conv_w.shape,
                  conv_b=conv_b.shape, dt_bias=dt_bias.shape, a_log=a_log.shape, d_skip=d_skip.shape,
                  attn_sinks=attn_sinks.shape, ssm_norm_w=ssm_norm_w.shape, w_out=w_out.shape, norm2_w=norm2_w.shape,
                  w_gate_up=w_gate_up.shape, w_down=w_down.shape, final_norm_w=final_norm_w.shape)
    grads = dict(w_ada=g_ada, w_in=g_in_s, w_out=g_out_s, w_gate_up=g_gu_s, w_down=g_dn_s)
    deltas = dict(w_ada=d_ada, w_in=d_in, w_out=d_out, w_gate_up=d_gu, w_down=d_dn)
    new_m = dict(w_ada=m_ada, w_in=m_in, w_out=m_out, w_gate_up=m_gu, w_down=m_dn)
    new_v = dict(w_ada=v_ada, w_in=v_in, w_out=v_out, w_gate_up=v_gu, w_down=v_dn)
    for i, nme in enumerate(small_names):
        grads[nme], deltas[nme], new_m[nme], new_v[nme] = small_g[i], sd[i], smn[i], svn[i]
    outs = [loss, grad_x[None]]
    for table in (grads, deltas, new_m, new_v):
        outs += [table[nme].reshape(shapes[nme]) for nme in order]
    return tuple(outs)
```

```python
import functools
import math

import jax
import jax.numpy as jnp
from jax import lax
from jax.experimental import pallas as pl
from jax.experimental.pallas import tpu as pltpu

F32 = jnp.float32
BF16 = jnp.bfloat16
HI = lax.Precision.HIGHEST
MESH = pl.DeviceIdType.MESH

D = 1024
HD = 64
AW = 512
SW = 512
NST = 128
CONVK = 4
CONVC = 1024
BLK = 128
CPS = 4
SSD_FWD_CPS = 8
ATTN_BPS = 8
IN_PROJ = 2312
IN_PAD = 2432
IN_SH = IN_PROJ // 4
DFF = 2816
GU_SH = 1408
FF_SPLITS = ((0, 1536), (1536, 2816))
EPS = 1e-6
NEG = -1e30
LR, B1, B2, AEPS, WD, STEP = 0.001, 0.9, 0.999, 1e-08, 0.01, 10
VMEM_LIMIT = 58 * 1024 * 1024


def _cp(*sem):
    return pltpu.CompilerParams(dimension_semantics=sem or None, vmem_limit_bytes=VMEM_LIMIT)


def _dot(a, b):
    return jnp.dot(a, b, preferred_element_type=F32)


def _dot_nt(a, b):
    return lax.dot_general(a, b, (((1,), (1,)), ((), ())), preferred_element_type=F32)


def _dot_tn(a, b):
    return lax.dot_general(a, b, (((0,), (0,)), ((), ())), preferred_element_type=F32)


def _dot_hi(a, b):
    return jnp.dot(a, b, precision=HI, preferred_element_type=F32)


def _sigmoid(x):
    return 1.0 / (1.0 + jnp.exp(-x))


def _iota(shape, dim):
    return lax.broadcasted_iota(jnp.int32, shape, dim)


def _load_resident(hbm_ref, vmem_ref, sem):
    @pl.when(pl.program_id(0) == 0)
    def _():
        cp = pltpu.make_async_copy(hbm_ref, vmem_ref, sem)
        cp.start()
        cp.wait()


def _swap32(t):
    lane = _iota(t.shape, 1)
    return jnp.where((lane & 63) < 32, pltpu.roll(t, 96, 1), pltpu.roll(t, 32, 1))


def _rope_fwd(t, cos, sin_s):
    return t * cos + _swap32(t) * sin_s


def _rope_bwd(t, cos, sin_s):
    return t * cos - _swap32(t) * sin_s


DEP_SPEC = pl.BlockSpec((8, 128), lambda *_: (0, 0))


def _rope_tables(pos_row, inv_freq_col, tm, dep):
    T = pos_row.shape[1]
    lane, row = jnp.arange(128)[None, :], jnp.arange(96)[:, None]
    pick = (lane % 32) == (row % 32)
    sel_cos = pick.astype(BF16)
    sel_sin = jnp.where(pick, jnp.where(lane % 64 < 32, -1.0, 1.0), 0.0).astype(BF16)

    def body(p_ref, f_ref, sc_ref, ss_ref, dep_ref, cos_ref, sin_ref):
        ang = f_ref[...] * p_ref[...].astype(F32)
        cos_ref[...] = _dot_tn(_pieces(jnp.cos(ang), 3, 0), sc_ref[...])
        sin_ref[...] = _dot_tn(_pieces(jnp.sin(ang), 3, 0), ss_ref[...])

    full = lambda a: pl.BlockSpec(a.shape, lambda i: (0,) * a.ndim)
    return pl.pallas_call(
        body, name="rope_tables", grid=(T // tm,),
        in_specs=[pl.BlockSpec((1, tm), lambda i: (0, i)), full(inv_freq_col), full(sel_cos), full(sel_sin), DEP_SPEC],
        out_specs=[pl.BlockSpec((tm, 128), lambda i: (i, 0))] * 2,
        out_shape=[jax.ShapeDtypeStruct((T, 128), F32)] * 2,
        compiler_params=_cp("parallel"),
    )(pos_row, inv_freq_col, sel_cos, sel_sin, dep)


def _in_proj_fwd(x, cos, sin_s, mod6, norm1_w, w_pad, tm, dep):
    T = x.shape[0]

    def body(x_ref, cos_ref, sin_ref, mod_ref, nw_ref, w_hbm, dep_ref, qkv_ref, z_ref, xbc_ref, dt_ref, h_ref, w_vmem,
             sem):
        _load_resident(w_hbm, w_vmem, sem)
        xv = x_ref[...]
        r = lax.rsqrt(jnp.mean(xv * xv, axis=-1, keepdims=True) + EPS)
        h = (xv * r * nw_ref[...]) * (1.0 + mod_ref[1:2, :]) + mod_ref[0:1, :]
        hb = h.astype(BF16)
        h_ref[...] = hb
        proj = _dot(hb, w_vmem[...])
        cs, sn = cos_ref[...], sin_ref[...]
        for j in range(5):
            qkv_ref[:, 128 * j:128 * (j + 1)] = _rope_fwd(proj[:, 128 * j:128 * (j + 1)], cs, sn).astype(BF16)
        qkv_ref[:, 640:768] = proj[:, 640:768].astype(BF16)
        z_ref[...] = proj[:, 768:1280]
        xbc_ref[...] = proj[:, 1280:2304]
        dt_ref[...] = proj[:, 2304:2432]

    row = lambda w: pl.BlockSpec((tm, w), lambda i: (i, 0))
    full = lambda a: pl.BlockSpec(a.shape, lambda i: (0,) * a.ndim)
    return pl.pallas_call(
        body, name="in_proj_fwd", grid=(T // tm,),
        in_specs=[row(D), row(128), row(128), full(mod6), full(norm1_w), pl.BlockSpec(memory_space=pl.ANY), DEP_SPEC],
        out_specs=[row(768), row(512), row(1024), row(128), row(D)],
        out_shape=[jax.ShapeDtypeStruct((T, 768), BF16), jax.ShapeDtypeStruct((T, 512), F32),
                   jax.ShapeDtypeStruct((T, 1024), F32), jax.ShapeDtypeStruct((T, 128), F32),
                   jax.ShapeDtypeStruct((T, D), BF16)],
        scratch_shapes=[pltpu.VMEM((D, IN_PAD), BF16), pltpu.SemaphoreType.DMA],
        compiler_params=_cp("arbitrary"),
    )(x, cos, sin_s, mod6, norm1_w, w_pad, dep)


def _head_variants(pair, j):
    lane = _iota(pair.shape, 1)
    lo = lane < 64
    kv = j // 2
    ev = jnp.where(lo, pair, 0.0)
    od = jnp.where(lo, 0.0, pair)
    if kv == 0:
        od = pltpu.roll(od, 64, 1)
    else:
        ev = pltpu.roll(ev, 64, 1)
    return ev.astype(BF16), od.astype(BF16)


def _kv_variants(vcat):
    lane = _iota(vcat.shape, 1)
    lo = lane < 64
    v0 = jnp.where(lo, vcat, 0.0)
    v1 = jnp.where(lo, 0.0, vcat)
    out = {
        (0, 0): v0, (0, 1): pltpu.roll(v0, 64, 1),
        (1, 0): pltpu.roll(v1, 64, 1), (1, 1): v1,
    }
    return {k: v.astype(BF16) for k, v in out.items()}


def _fold_masks(n):
    upper = _iota((BLK, BLK), 1) > _iota((BLK, BLK), 0)
    return upper, upper & (n == 0)


def _attn_fwd(qkv, sinks):
    CPS = ATTN_BPS
    T = qkv.shape[0]
    nsteps = T // (CPS * BLK)

    def body(sink_ref, q_ref, kc_ref, kp_ref, vc_ref, vp_ref, o_ref, lse_ref):
        for sub in range(CPS):
            rows, before = slice(BLK * sub, BLK * (sub + 1)), slice(BLK * (sub - 1), BLK * sub)
            block(pl.program_id(0) * CPS + sub, sink_ref, q_ref.at[rows, :], kc_ref.at[rows, :],
                  kp_ref if sub == 0 else kc_ref.at[before, :], vc_ref.at[rows, :],
                  vp_ref if sub == 0 else vc_ref.at[before, :], o_ref.at[rows, :], lse_ref.at[rows, :])

    def block(n, sink_ref, q_ref, kc_ref, kp_ref, vc_ref, vp_ref, o_ref, lse_ref):
        vpv = _kv_variants(vp_ref[...].astype(F32))
        vcv = _kv_variants(vc_ref[...].astype(F32))
        q_all = jnp.concatenate(
            [v for j in range(4) for v in _head_variants(q_ref[:, 128 * j:128 * (j + 1)].astype(F32), j)], axis=0)
        s_prev = _dot_nt(q_all, kp_ref[...])
        s_cur = _dot_nt(q_all, kc_ref[...])
        upper, dead = _fold_masks(n)
        lane = _iota((BLK, 128), 1)
        lse_acc = jnp.zeros((BLK, 128), F32)
        for jj in range(4):
            acc = jnp.zeros((BLK, 128), F32)
            for par in range(2):
                h = 2 * jj + par
                rows = slice(h * BLK, (h + 1) * BLK)
                sink = sink_ref[0, h]
                s = jnp.where(dead, NEG, jnp.where(upper, s_prev[rows], s_cur[rows]) * 0.125)
                m = jnp.maximum(jnp.max(s, axis=1, keepdims=True), sink)
                p = jnp.exp(s - m)
                den = jnp.sum(p, axis=1, keepdims=True) + jnp.exp(sink - m)
                pn = p * (1.0 / den)
                acc = (acc + _dot(jnp.where(upper, pn, 0.0).astype(BF16), vpv[(jj // 2, par)])
                       + _dot(jnp.where(upper, 0.0, pn).astype(BF16), vcv[(jj // 2, par)]))
                lse_acc = jnp.where(lane == h, m + jnp.log(den), lse_acc)
            o_ref[:, 128 * jj:128 * (jj + 1)] = acc.astype(BF16)
        lse_ref[...] = lse_acc

    RB = CPS * BLK
    prev = lambda n: jnp.maximum(n * CPS - 1, 0)
    return pl.pallas_call(
        body, name="attn_fwd", grid=(nsteps,),
        in_specs=[pl.BlockSpec(memory_space=pltpu.SMEM),
                  pl.BlockSpec((RB, 512), lambda n: (n, 0)),
                  pl.BlockSpec((RB, 128), lambda n: (n, 4)),
                  pl.BlockSpec((BLK, 128), lambda n: (prev(n), 4)),
                  pl.BlockSpec((RB, 128), lambda n: (n, 5)),
                  pl.BlockSpec((BLK, 128), lambda n: (prev(n), 5))],
        out_specs=[pl.BlockSpec((RB, 512), lambda n: (n, 0)), pl.BlockSpec((RB, 128), lambda n: (n, 0))],
        out_shape=[jax.ShapeDtypeStruct((T, 512), BF16), jax.ShapeDtypeStruct((T, 128), F32)],
        compiler_params=_cp("parallel"),
    )(sinks, qkv, qkv, qkv, qkv, qkv)


def _attn_bwd(qkv, sinks, lse, dmix, cos, sin_s, dep):
    T = qkv.shape[0]
    nb = T // BLK

    def body(sink_ref, q_ref, kc_ref, kp_ref, vc_ref, vp_ref, lse_ref, do_ref, cq_ref, sq_ref, ck_ref, sk_ref,
             dep_ref, out_ref, ds_ref, dq_car, dk_car, dv_car):
        n = pl.program_id(0)
        lane = _iota((BLK, 128), 1)

        @pl.when(n == 0)
        def _():
            ds_ref[...] = jnp.zeros_like(ds_ref)
            dq_car[...] = jnp.zeros_like(dq_car)
            dk_car[...] = jnp.zeros_like(dk_car)
            dv_car[...] = jnp.zeros_like(dv_car)

        @pl.when(n < nb)
        def _():
            kp, kc, vp, vc = kp_ref[...], kc_ref[...], vp_ref[...], vc_ref[...]
            kpv = _kv_variants(kp.astype(F32))
            kcv = _kv_variants(kc.astype(F32))
            lse_v = lse_ref[...]
            q_all = jnp.concatenate(
                [v for j in range(4) for v in _head_variants(q_ref[:, 128 * j:128 * (j + 1)].astype(F32), j)], axis=0)
            do_all = jnp.concatenate(
                [v for j in range(4) for v in _head_variants(do_ref[:, 128 * j:128 * (j + 1)], j)], axis=0)
            s_prev, s_cur = _dot_nt(q_all, kp), _dot_nt(q_all, kc)
            dp_prev, dp_cur = _dot_nt(do_all, vp), _dot_nt(do_all, vc)
            upper, dead = _fold_masks(n)
            out_ref[:, 0:512] = dq_car[...]
            dsk = jnp.zeros((1, 128), F32)
            ds_u, ds_l, p_u, p_l = [], [], [], []
            for jj in range(4):
                dq_acc = jnp.zeros((BLK, 128), F32)
                for par in range(2):
                    h = 2 * jj + par
                    rows = slice(h * BLK, (h + 1) * BLK)
                    lse_h = jnp.sum(jnp.where(lane == h, lse_v, 0.0), axis=1, keepdims=True)
                    s = jnp.where(dead, NEG, jnp.where(upper, s_prev[rows], s_cur[rows]) * 0.125)
                    p = jnp.exp(s - lse_h)
                    dp = jnp.where(upper, dp_prev[rows], dp_cur[rows])
                    delta = jnp.sum(p * dp, axis=1, keepdims=True)
                    ds = p * (dp - delta) * 0.125
                    dsu, dsl = jnp.where(upper, ds, 0.0).astype(BF16), jnp.where(upper, 0.0, ds).astype(BF16)
                    dq_acc = dq_acc + _dot(dsu, kpv[(jj // 2, par)]) + _dot(dsl, kcv[(jj // 2, par)])
                    ds_u.append(dsu)
                    ds_l.append(dsl)
                    p_u.append(jnp.where(upper, p, 0.0).astype(BF16))
                    p_l.append(jnp.where(upper, 0.0, p).astype(BF16))
                    dsk = dsk + jnp.where(lane[0:1] == h, -jnp.sum(jnp.exp(sink_ref[0, h] - lse_h) * delta), 0.0)
                dq_car[:, 128 * jj:128 * (jj + 1)] = _rope_bwd(dq_acc, cq_ref[...], sq_ref[...]).astype(BF16)
            stack = lambda parts: jnp.concatenate(parts, axis=0)
            dk_prev, dk_cur = _dot_tn(stack(ds_u), q_all), _dot_tn(stack(ds_l), q_all)
            dv_prev, dv_cur = _dot_tn(stack(p_u), do_all), _dot_tn(stack(p_l), do_all)
            ds_ref[...] += dsk
            out_ref[:, 512:640] = _rope_bwd(dk_car[...] + dk_prev, ck_ref[...], sk_ref[...]).astype(BF16)
            out_ref[:, 640:768] = (dv_car[...] + dv_prev).astype(BF16)
            dk_car[...] = dk_cur
            dv_car[...] = dv_cur

        @pl.when(n == nb)
        def _():
            out_ref[:, 0:512] = dq_car[...]
            out_ref[:, 512:640] = _rope_bwd(dk_car[...], ck_ref[...], sk_ref[...]).astype(BF16)
            out_ref[:, 640:768] = dv_car[...].astype(BF16)

    cur = lambda n: jnp.minimum(n, nb - 1)
    prev = lambda n: jnp.maximum(cur(n) - 1, 0)
    outb = lambda n: jnp.maximum(n - 1, 0)
    return pl.pallas_call(
        body, name="attn_bwd", grid=(nb + 1,),
        in_specs=[pl.BlockSpec(memory_space=pltpu.SMEM),
                  pl.BlockSpec((BLK, 512), lambda n: (cur(n), 0)),
                  pl.BlockSpec((BLK, 128), lambda n: (cur(n), 4)),
                  pl.BlockSpec((BLK, 128), lambda n: (prev(n), 4)),
                  pl.BlockSpec((BLK, 128), lambda n: (cur(n), 5)),
                  pl.BlockSpec((BLK, 128), lambda n: (prev(n), 5)),
                  pl.BlockSpec((BLK, 128), lambda n: (cur(n), 0)),
                  pl.BlockSpec((BLK, 512), lambda n: (cur(n), 0)),
                  pl.BlockSpec((BLK, 128), lambda n: (cur(n), 0)),
                  pl.BlockSpec((BLK, 128), lambda n: (cur(n), 0)),
                  pl.BlockSpec((BLK, 128), lambda n: (outb(n), 0)),
                  pl.BlockSpec((BLK, 128), lambda n: (outb(n), 0)), DEP_SPEC],
        out_specs=[pl.BlockSpec((BLK, 768), lambda n: (outb(n), 0)), pl.BlockSpec((1, 128), lambda n: (0, 0))],
        out_shape=[jax.ShapeDtypeStruct((T, 768), BF16), jax.ShapeDtypeStruct((1, 128), F32)],
        scratch_shapes=[pltpu.VMEM((BLK, 512), BF16), pltpu.VMEM((BLK, 128), F32), pltpu.VMEM((BLK, 128), F32)],
        compiler_params=_cp("arbitrary"),
    )(sinks, qkv, qkv, qkv, qkv, qkv, lse, dmix, cos, sin_s, cos, sin_s, dep)


def _ssd_mats():
    e = jnp.arange(SW)[None, :] // HD == jnp.arange(128)[:, None]
    tri = jnp.arange(BLK)[None, :] <= jnp.arange(BLK)[:, None]
    return (jnp.tile(e, (3, 1)).astype(BF16), jnp.tile(e.T, (2, 1)).astype(BF16),
            jnp.tile(tri, (1, 3)).astype(BF16), jnp.tile(tri.T, (1, 3)).astype(BF16))


def _pieces(x, n, axis):
    out, r = [], x
    for i in range(n):
        p = r.astype(BF16)
        out.append(p)
        if i + 1 < n:
            r = r - p.astype(F32)
    return jnp.concatenate(out, axis=axis)


def _expand(x, e3):
    return _dot(_pieces(x, 3, 1), e3)


def _head_sums(x, et2):
    return _dot(_pieces(x, 2, 1), et2)


def _run_sum(tri3, x):
    return _dot(tri3, _pieces(x, 3, 0))


def _shift_down(u, tail, j):
    rolled = pltpu.roll(u, j, 0)
    first = jnp.where(_iota(tail.shape, 0) < j, pltpu.roll(tail, j, 0), rolled[0:8])
    return jnp.concatenate([first, rolled[8:]], axis=0)


def _shift_up(d, head, j):
    rolled = pltpu.roll(d, BLK - j, 0)
    last = jnp.where(_iota(head.shape, 0) >= 8 - j, pltpu.roll(head, 8 - j, 0), rolled[BLK - 8:])
    return jnp.concatenate([rolled[:BLK - 8], last], axis=0)


def _ssd_parts(dtr, dtb, alog, e3, tril3):
    xx = dtr + dtb
    dt = jnp.maximum(xx, 0.0) + jnp.log(1.0 + jnp.exp(-jnp.abs(xx)))
    a_neg = -jnp.exp(alog)
    tril = _iota((BLK, BLK), 1) <= _iota((BLK, BLK), 0)
    cs = _run_sum(tril3, dt * a_neg)
    csx = _expand(cs, e3)
    last = csx[BLK - 1:BLK, :]
    return dict(xx=xx, dt=dt, a_neg=a_neg, tril=tril, cs=cs, cs_t=cs.T,
                ecsx=jnp.exp(csx), dtex=jnp.exp(last - csx), cdx=jnp.exp(last), dtx=_expand(dt, e3))


def _decay(parts, h):
    seg = parts["cs"][:, h:h + 1] - parts["cs_t"][h:h + 1, :]
    return jnp.exp(jnp.where(parts["tril"], seg, NEG))


def _group_cols(a, g):
    return a[:, 256 * g:256 * (g + 1)]


def _ssd_fwd(xbc, z, dtr, conv_w, conv_b, dtb, alog, dskx, ssm_w, mats, dep):
    CPS = SSD_FWD_CPS
    T = xbc.shape[0]
    nc = T // BLK

    def body(u_ref, tail_ref, z_ref, dtr_ref, cw_ref, cb_ref, dtb_ref, al_ref, dk_ref, sw_ref, e3_ref, tril3_ref,
             dep_ref, yn_ref, yp_ref, st_ref, co_ref, s_scr):
        n = pl.program_id(0)

        @pl.when(n == 0)
        def _():
            s_scr[...] = jnp.zeros_like(s_scr)

        lane = _iota((BLK, 128), 1)
        lo = lane < 64
        for sub in range(CPS):
            rows = slice(BLK * sub, BLK * (sub + 1))
            u = u_ref[rows, :]
            tail = jnp.where(n > 0, tail_ref[...], 0.0) if sub == 0 else u_ref[BLK * sub - 8:BLK * sub, :]
            co = cb_ref[...] + cw_ref[3:4, :] * u
            for j in range(1, CONVK):
                co = co + cw_ref[3 - j:4 - j, :] * _shift_down(u, tail, j)
            co_ref[rows, :] = co
            xc = co * _sigmoid(co)
            pt = _ssd_parts(dtr_ref[rows, :], dtb_ref[...], al_ref[...], e3_ref[...], tril3_ref[...])
            xs = xc[:, :SW]
            bm = [xc[:, 512:640].astype(BF16), xc[:, 640:768].astype(BF16)]
            cm = [xc[:, 768:896].astype(BF16), xc[:, 896:1024].astype(BF16)]
            s_in = s_scr[...]
            st_ref[sub] = s_in
            xdt = xs * pt["dtx"]
            xde = (xdt * pt["dtex"]).astype(BF16)
            ys, s_new = [], []
            for g in range(2):
                cb = _dot_nt(cm[g], bm[g])
                yoff = _dot(cm[g], _group_cols(s_in, g).astype(BF16))
                s_new.append(_dot_tn(bm[g], _group_cols(xde, g)))
                for jj in range(2):
                    j = 2 * g + jj
                    chunk = xdt[:, 128 * j:128 * (j + 1)]
                    g_ev = (cb * _decay(pt, 2 * j)).astype(BF16)
                    g_od = (cb * _decay(pt, 2 * j + 1)).astype(BF16)
                    yd = (_dot(g_ev, jnp.where(lo, chunk, 0.0).astype(BF16))
                          + _dot(g_od, jnp.where(lo, 0.0, chunk).astype(BF16)))
                    ys.append(yd + yoff[:, 128 * jj:128 * (jj + 1)] * pt["ecsx"][:, 128 * j:128 * (j + 1)])
            y = jnp.concatenate(ys, axis=1) + xs * dk_ref[...]
            s_scr[...] = s_in * pt["cdx"] + jnp.concatenate(s_new, axis=1)
            yp_ref[rows, :] = y
            zv = z_ref[rows, :]
            yz = y * (zv * _sigmoid(zv))
            outs = []
            for g in range(2):
                yg = _group_cols(yz, g)
                outs.append(yg * lax.rsqrt(jnp.mean(yg * yg, axis=-1, keepdims=True) + EPS))
            yn_ref[rows, :] = (jnp.concatenate(outs, axis=1) * sw_ref[...]).astype(BF16)

    e3, _, tril3, _ = mats
    RB = CPS * BLK
    tail8 = lambda n: jnp.maximum(n * (RB // 8) - 1, 0)
    full = lambda a: pl.BlockSpec(a.shape, lambda n: (0,) * a.ndim)
    return pl.pallas_call(
        body, name="ssd_fwd", grid=(nc // CPS,),
        in_specs=[pl.BlockSpec((RB, CONVC), lambda n: (n, 0)), pl.BlockSpec((8, CONVC), lambda n: (tail8(n), 0)),
                  pl.BlockSpec((RB, SW), lambda n: (n, 0)), pl.BlockSpec((RB, 128), lambda n: (n, 0)),
                  full(conv_w), full(conv_b), full(dtb), full(alog), full(dskx), full(ssm_w), full(e3), full(tril3),
                  DEP_SPEC],
        out_specs=[pl.BlockSpec((RB, SW), lambda n: (n, 0)), pl.BlockSpec((RB, SW), lambda n: (n, 0)),
                   pl.BlockSpec((CPS, NST, SW), lambda n: (n, 0, 0)), pl.BlockSpec((RB, CONVC), lambda n: (n, 0))],
        out_shape=[jax.ShapeDtypeStruct((T, SW), BF16), jax.ShapeDtypeStruct((T, SW), F32),
                   jax.ShapeDtypeStruct((nc, NST, SW), F32), jax.ShapeDtypeStruct((T, CONVC), F32)],
        scratch_shapes=[pltpu.VMEM((NST, SW), F32)],
        compiler_params=_cp("arbitrary"),
    )(xbc, xbc, z, dtr, conv_w, conv_b, dtb, alog, dskx, ssm_w, e3, tril3, dep)


def _ssd_bwd(xbc, co_all, z, dtr, ypre, states, dmix, conv_w, dtb, alog, dskx, ssm_w, mats, dep):
    T = xbc.shape[0]
    nsteps = T // (CPS * BLK)

    def body(*refs):
        per_chunk, consts, out_ref, carried = refs[:7], refs[7:16], refs[17], refs[18:]
        i = pl.program_id(0)

        @pl.when(i == 0)
        def _():
            for r in carried:
                r[...] = jnp.zeros_like(r)

        for sub in reversed(range(CPS)):
            rows = slice(BLK * sub, BLK * (sub + 1))
            views = [r.at[sub:sub + 1] if k == 5 else r.at[rows, :] for k, r in enumerate(per_chunk)]
            chunk(*views, *consts, out_ref.at[rows, :], *carried)

        @pl.when(i == nsteps - 1)
        def _():
            dsk_ref, dskx_scr = carried[3], carried[8]
            dsk_ref[...] = _head_sums(jnp.broadcast_to(dskx_scr[...], (8, SW)), consts[6][...])[0:1]

    def chunk(u_ref, co_ref, z_ref, dtr_ref, yp_ref, st_ref, dyn_ref, cw_ref, dtb_ref, al_ref, dk_ref, sw_ref,
              e3_ref, et2_ref, tril3_ref, triu3_ref,
              out_ref, dcw_ref, dcb_ref, dsw_ref, dsk_ref, ddtb_ref, dav_ref, ds_scr, dco_scr, dskx_scr):
        co = co_ref[...]
        sg = _sigmoid(co)
        xc = co * sg
        pt = _ssd_parts(dtr_ref[...], dtb_ref[...], al_ref[...], e3_ref[...], tril3_ref[...])
        dtx, ecsx, dtex, cdx = pt["dtx"], pt["ecsx"], pt["dtex"], pt["cdx"]
        xs = xc[:, :SW]
        bm = [xc[:, 512:640].astype(BF16), xc[:, 640:768].astype(BF16)]
        cm = [xc[:, 768:896].astype(BF16), xc[:, 896:1024].astype(BF16)]
        s_in = st_ref[0]
        ds_out = ds_scr[...]
        e_t = et2_ref[...]

        zv = z_ref[...]
        sz = _sigmoid(zv)
        silu_z = zv * sz
        ypre = yp_ref[...]
        yz = ypre * silu_z
        dyn = dyn_ref[...]
        sw = sw_ref[...]
        dyz, yns = [], []
        for g in range(2):
            yg = _group_cols(yz, g)
            r = lax.rsqrt(jnp.mean(yg * yg, axis=-1, keepdims=True) + EPS)
            yn = yg * r
            dg = _group_cols(dyn, g) * _group_cols(sw, g)
            dyz.append(r * (dg - yn * jnp.mean(dg * yn, axis=-1, keepdims=True)))
            yns.append(yn)
        dyz = jnp.concatenate(dyz, axis=1)
        dsw_ref[...] += jnp.sum(dyn * jnp.concatenate(yns, axis=1), axis=0, keepdims=True)
        dy = dyz * silu_z
        dz = dyz * ypre * (sz * (1.0 + zv * (1.0 - sz)))

        xdt = xs * dtx
        xdt_b = xdt.astype(BF16)
        edy = (ecsx * dy).astype(BF16)
        xde = (xdt * dtex).astype(BF16)
        lane = _iota((BLK, 128), 1)
        lo = lane < 64
        row8 = _iota((8, 128), 0)
        dcs = jnp.zeros((BLK, 128), F32)
        col_rows = jnp.zeros((8, 128), F32)
        dxdt, bds, yoff, dbs, dcs_g, ds_new = [], [], [], [], [], []
        for g in range(2):
            s_g = _group_cols(s_in, g).astype(BF16)
            dso_g = _group_cols(ds_out, g).astype(BF16)
            cb = _dot_nt(cm[g], bm[g])
            bds.append(_dot(bm[g], dso_g))
            yoff.append(_dot(cm[g], s_g))
            dcb_g = jnp.zeros((BLK, BLK), F32)
            for jj in range(2):
                j = 2 * g + jj
                dy_c = dy[:, 128 * j:128 * (j + 1)]
                xdt_c = xdt_b[:, 128 * j:128 * (j + 1)]
                acc = jnp.zeros((BLK, 128), F32)
                for par in range(2):
                    h = 2 * j + par
                    lm = _decay(pt, h)
                    gm = cb * lm
                    dy_m = (jnp.where(lo, dy_c, 0.0) if par == 0 else jnp.where(lo, 0.0, dy_c)).astype(BF16)
                    dg_h = _dot_nt(dy_m, xdt_c)
                    w_h = dg_h * gm
                    dcs = dcs + jnp.where(lane == h, jnp.sum(w_h, axis=1, keepdims=True), 0.0)
                    col_rows = col_rows + jnp.where(row8 == h, jnp.sum(w_h, axis=0, keepdims=True), 0.0)
                    dcb_g = dcb_g + dg_h * lm
                    acc = acc + _dot_tn(gm.astype(BF16), dy_m)
                dxdt.append(acc)
            dcb_b = dcb_g.astype(BF16)
            dcs_g.append(_dot(dcb_b, bm[g]) + _dot_nt(_group_cols(edy, g), s_g))
            dbs.append(_dot_tn(dcb_b, cm[g]) + _dot_nt(_group_cols(xde, g), dso_g))
            ds_new.append(_dot_tn(cm[g], _group_cols(edy, g)))
        bds = jnp.concatenate(bds, axis=1)
        yoff = jnp.concatenate(yoff, axis=1) * ecsx
        dxdt = jnp.concatenate(dxdt, axis=1) + dtex * bds
        ds_scr[...] = cdx * ds_out + jnp.concatenate(ds_new, axis=1)

        t_m = _head_sums(dtex * xdt * bds, e_t)
        colsum_t = jnp.concatenate([col_rows, jnp.zeros((BLK - 8, 128), F32)], axis=0).T
        cd = jnp.exp(pt["cs"][BLK - 1:BLK, :])
        sds = jnp.sum(s_in * ds_out, axis=0, keepdims=True)
        last_row = jnp.sum(t_m, axis=0, keepdims=True) + cd * _head_sums(jnp.broadcast_to(sds, (8, SW)), e_t)[0:1]
        dcs = dcs - colsum_t + _head_sums(dy * yoff, e_t) - t_m
        dcs = dcs + jnp.where(_iota((BLK, 128), 0) == BLK - 1, last_row, 0.0)
        da = _run_sum(triu3_ref[...], dcs)
        dt = pt["dt"]
        ddt = da * pt["a_neg"] + _head_sums(dxdt * xs, e_t)
        dav_ref[...] += jnp.sum(da * dt, axis=0, keepdims=True)
        ddtr = ddt * _sigmoid(pt["xx"])
        ddtb_ref[...] += jnp.sum(ddtr, axis=0, keepdims=True)
        dxs = dxdt * dtx + dy * dk_ref[...]
        dskx_scr[...] += jnp.sum(dy * xs, axis=0, keepdims=True)
        dxc = jnp.concatenate([dxs, dbs[0], dbs[1], dcs_g[0], dcs_g[1]], axis=1)
        dco = dxc * (sg * (1.0 + co * (1.0 - sg)))

        dcb_ref[...] += jnp.sum(dco, axis=0, keepdims=True)
        u = u_ref[...]
        head = dco_scr[...]
        du = jnp.zeros_like(dco)
        for j in range(CONVK):
            up_j = dco if j == 0 else _shift_up(dco, head, j)
            dcw_ref[3 - j:4 - j, :] += jnp.sum(up_j * u, axis=0, keepdims=True)
            du = du + cw_ref[3 - j:4 - j, :] * up_j
        dco_scr[...] = dco[0:8]
        out_ref[:, 0:512] = dz.astype(BF16)
        out_ref[:, 512:1536] = du.astype(BF16)
        out_ref[:, 1536:1664] = ddtr.astype(BF16)

    e3, et2, tril3, triu3 = mats
    RB = CPS * BLK
    rev = lambda i: nsteps - 1 - i
    full = lambda a: pl.BlockSpec(a.shape, lambda i: (0,) * a.ndim)
    acc = lambda r, c: pl.BlockSpec((r, c), lambda i: (0, 0))
    return pl.pallas_call(
        body, name="ssd_bwd", grid=(nsteps,),
        in_specs=[pl.BlockSpec((RB, CONVC), lambda i: (rev(i), 0)), pl.BlockSpec((RB, CONVC), lambda i: (rev(i), 0)),
                  pl.BlockSpec((RB, SW), lambda i: (rev(i), 0)), pl.BlockSpec((RB, 128), lambda i: (rev(i), 0)),
                  pl.BlockSpec((RB, SW), lambda i: (rev(i), 0)), pl.BlockSpec((CPS, NST, SW), lambda i: (rev(i), 0, 0)),
                  pl.BlockSpec((RB, SW), lambda i: (rev(i), 1)),
                  full(conv_w), full(dtb), full(alog), full(dskx), full(ssm_w),
                  full(e3), full(et2), full(tril3), full(triu3), DEP_SPEC],
        out_specs=[pl.BlockSpec((RB, 1664), lambda i: (rev(i), 0)),
                   acc(CONVK, CONVC), acc(1, CONVC), acc(1, SW), acc(1, 128), acc(1, 128), acc(1, 128)],
        out_shape=[jax.ShapeDtypeStruct((T, 1664), BF16),
                   jax.ShapeDtypeStruct((CONVK, CONVC), F32), jax.ShapeDtypeStruct((1, CONVC), F32),
                   jax.ShapeDtypeStruct((1, SW), F32), jax.ShapeDtypeStruct((1, 128), F32),
                   jax.ShapeDtypeStruct((1, 128), F32), jax.ShapeDtypeStruct((1, 128), F32)],
        scratch_shapes=[pltpu.VMEM((NST, SW), F32), pltpu.VMEM((8, CONVC), F32), pltpu.VMEM((1, SW), F32)],
        compiler_params=_cp("arbitrary"),
    )(xbc, co_all, z, dtr, ypre, states, dmix, conv_w, dtb, alog, dskx, ssm_w, e3, et2, tril3, triu3, dep)


def _mix_ffn(x, attn, ynorm, tgt, mod6, norm2_w, final_w, w_out, w_gu, w_gu_own, s_arr, w_dn, w_out_own, w_dn_own, tm):
    T = x.shape[0]
    nt = T // tm

    def body(x_ref, a_ref, y_ref, t_ref, mod_ref, n2_ref, fw_ref, wo_hbm, wgu_hbm, own_hbm, s_ref, wdn_hbm, wo_own, wdn_own,
             sq_ref, dmix_ref, dx1_ref, h2_ref, act_ref, df_ref, dgu_ref, do_ref, sm_ref,
             wo, wgu, wdn, sems):
        i = pl.program_id(0)

        @pl.when(i == 0)
        def _():
            cps = [pltpu.make_async_copy(s, d, sems.at[k]) for k, (s, d) in
                   enumerate(((wo_hbm, wo), (wgu_hbm, wgu), (wdn_hbm, wdn)))]
            for c in cps:
                c.start()
            for c in cps:
                c.wait()
            rows = lambda n: pl.ds(pl.multiple_of(s_ref[0] * n, 16), n)
            owns = [pltpu.make_async_copy(
                        own_hbm, wgu.at[:, pl.ds(pl.multiple_of(s_ref[0] * GU_SH, 128), GU_SH)], sems.at[3]),
                    pltpu.make_async_copy(wo_own, wo.at[rows(D // 4), :], sems.at[4]),
                    pltpu.make_async_copy(wdn_own, wdn.at[rows(DFF // 4), :], sems.at[5])]
            for c in owns:
                c.start()
            for c in owns:
                c.wait()
            sq_ref[...] = jnp.zeros_like(sq_ref)
            sm_ref[...] = jnp.zeros_like(sm_ref)

        gate1, shift2, scale2, gate2 = mod_ref[2:3, :], mod_ref[3:4, :], mod_ref[4:5, :], mod_ref[5:6, :]
        n2w, fw = n2_ref[...], fw_ref[...]
        o = _dot(a_ref[...], wo[0:AW, :]) + _dot(y_ref[...], wo[AW:D, :])
        x1 = x_ref[...] + gate1 * o
        r2 = lax.rsqrt(jnp.mean(x1 * x1, axis=-1, keepdims=True) + EPS)
        xh2 = x1 * r2
        n2 = xh2 * n2w
        h2b = (n2 * (1.0 + scale2) + shift2).astype(BF16)
        h2_ref[...] = h2b
        f = jnp.zeros((tm, D), F32)
        saved = []
        for a, b in FF_SPLITS:
            gp = _dot(h2b, wgu[:, a:b])
            upj = _dot(h2b, wgu[:, DFF + a:DFF + b])
            sg = _sigmoid(gp)
            sl = gp * sg
            actb = (sl * upj).astype(BF16)
            act_ref[:, a:b] = actb
            f = f + _dot(actb, wdn[a:b, :])
            saved.append((gp, upj, sg, sl))
        x2 = x1 + gate2 * f
        r3 = lax.rsqrt(jnp.mean(x2 * x2, axis=-1, keepdims=True) + EPS)
        xh3 = x2 * r3
        err = xh3 * fw - t_ref[...]
        sq_ref[...] += jnp.sum(err * err, axis=0, keepdims=True)
        dy = err * (1.0 / D)
        dfw = jnp.sum(dy * xh3, axis=0, keepdims=True)
        dxh3 = dy * fw
        dx2 = r3 * (dxh3 - xh3 * jnp.mean(dxh3 * xh3, axis=-1, keepdims=True))
        dgate2 = jnp.sum(dx2 * f, axis=0, keepdims=True)
        dfb = (dx2 * gate2).astype(BF16)
        df_ref[...] = dfb
        dh2 = jnp.zeros((tm, D), F32)
        for (a, b), (gp, upj, sg, sl) in zip(FF_SPLITS, saved):
            dact = _dot_nt(dfb, wdn[a:b, :])
            dg = (dact * upj * (sg * (1.0 + gp * (1.0 - sg)))).astype(BF16)
            du = (dact * sl).astype(BF16)
            dgu_ref[:, a:b] = dg
            dgu_ref[:, DFF + a:DFF + b] = du
            dh2 = dh2 + _dot_nt(dg, wgu[:, a:b]) + _dot_nt(du, wgu[:, DFF + a:DFF + b])
        dshift2 = jnp.sum(dh2, axis=0, keepdims=True)
        dscale2 = jnp.sum(dh2 * n2, axis=0, keepdims=True)
        dn2 = dh2 * (1.0 + scale2)
        dn2w = jnp.sum(dn2 * xh2, axis=0, keepdims=True)
        dxh2 = dn2 * n2w
        dx1 = dx2 + r2 * (dxh2 - xh2 * jnp.mean(dxh2 * xh2, axis=-1, keepdims=True))
        dx1_ref[...] = dx1
        dgate1 = jnp.sum(dx1 * o, axis=0, keepdims=True)
        dob = (dx1 * gate1).astype(BF16)
        do_ref[...] = dob
        dmix_ref[...] = _dot_nt(dob, wo[...])
        sm_ref[...] += jnp.concatenate(
            [dfw, dn2w, dshift2, dscale2, dgate2, dgate1, jnp.zeros((2, D), F32)], axis=0)

    row = lambda w: pl.BlockSpec((tm, w), lambda i: (i, 0))
    full = lambda a: pl.BlockSpec(a.shape, lambda i: (0,) * a.ndim)
    anyspec = pl.BlockSpec(memory_space=pl.ANY)
    return pl.pallas_call(
        body, name="mix_ffn", grid=(nt,),
        in_specs=[row(D), row(AW), row(SW), row(D), full(mod6), full(norm2_w), full(final_w), anyspec, anyspec, anyspec,
                  pl.BlockSpec(memory_space=pltpu.SMEM), anyspec, anyspec, anyspec],
        out_specs=[pl.BlockSpec((1, D), lambda i: (0, 0)), row(D), row(D), row(D),
                   row(DFF), row(D), row(2 * DFF), row(D), pl.BlockSpec((8, D), lambda i: (0, 0))],
        out_shape=[jax.ShapeDtypeStruct((1, D), F32), jax.ShapeDtypeStruct((T, D), F32), jax.ShapeDtypeStruct((T, D), F32),
                   jax.ShapeDtypeStruct((T, D), BF16), jax.ShapeDtypeStruct((T, DFF), BF16),
                   jax.ShapeDtypeStruct((T, D), BF16), jax.ShapeDtypeStruct((T, 2 * DFF), BF16),
                   jax.ShapeDtypeStruct((T, D), BF16), jax.ShapeDtypeStruct((8, D), F32)],
        scratch_shapes=[pltpu.VMEM((D, D), BF16), pltpu.VMEM((D, 2 * DFF), BF16), pltpu.VMEM((DFF, D), BF16),
                        pltpu.SemaphoreType.DMA((6,))],
        compiler_params=_cp("arbitrary"),
    )(x, attn, ynorm, tgt, mod6, norm2_w, final_w, w_out, w_gu, w_gu_own, s_arr, w_dn, w_out_own, w_dn_own)


def _in_proj_bwd(x, dx1, dqkv, dzxd, mod6, norm1_w, w_pad, tm, dep):
    T = x.shape[0]

    def body(x_hbm, dx1_hbm, dq_ref, dz_ref, mod_ref, nw_ref, w_hbm, dep_ref, gx_ref, sm_ref, w_vmem, sem, xb, db, ring):
        i, n = pl.program_id(0), pl.num_programs(0)

        def fetch(t, slot):
            rows_t = pl.ds(pl.multiple_of(t * tm, tm), tm)
            return (pltpu.make_async_copy(x_hbm.at[rows_t], xb.at[slot], ring.at[slot]),
                    pltpu.make_async_copy(dx1_hbm.at[rows_t], db.at[slot], ring.at[3 + slot]))

        @pl.when(i == 0)
        def _():
            for t in range(2):
                for c in fetch(t, t):
                    c.start()
            sm_ref[...] = jnp.zeros_like(sm_ref)

        @pl.when(i + 2 < n)
        def _():
            for c in fetch(i + 2, (i + 2) % 3):
                c.start()

        _load_resident(w_hbm, w_vmem, sem)
        slot = i % 3
        for c in fetch(i, slot):
            c.wait()
        x_ref, dx1_ref = xb.at[slot], db.at[slot]

        nw = nw_ref[...]
        scale1 = mod_ref[1:2, :]
        sums = jnp.zeros((8, D), F32)
        for rows in (slice(0, tm // 2), slice(tm // 2, tm)):
            dh = _dot_nt(dq_ref[rows, :], w_vmem[:, 0:768]) + _dot_nt(dz_ref[rows, :], w_vmem[:, 768:IN_PAD])
            xv = x_ref[rows, :]
            r = lax.rsqrt(jnp.mean(xv * xv, axis=-1, keepdims=True) + EPS)
            xh = xv * r
            n1 = xh * nw
            dshift = jnp.sum(dh, axis=0, keepdims=True)
            dscale = jnp.sum(dh * n1, axis=0, keepdims=True)
            dn = dh * (1.0 + scale1)
            dnw = jnp.sum(dn * xh, axis=0, keepdims=True)
            dxh = dn * nw
            gx_ref[rows, :] = dx1_ref[rows, :] + r * (dxh - xh * jnp.mean(dxh * xh, axis=-1, keepdims=True))
            sums = sums + jnp.concatenate([dnw, dshift, dscale, jnp.zeros((5, D), F32)], axis=0)
        sm_ref[...] += sums

    row = lambda w: pl.BlockSpec((tm, w), lambda i: (i, 0))
    full = lambda a: pl.BlockSpec(a.shape, lambda i: (0,) * a.ndim)
    anyspec = pl.BlockSpec(memory_space=pl.ANY)
    return pl.pallas_call(
        body, name="in_proj_bwd", grid=(T // tm,),
        in_specs=[anyspec, anyspec, row(768), row(1664), full(mod6), full(norm1_w), anyspec, DEP_SPEC],
        out_specs=[row(D), pl.BlockSpec((8, D), lambda i: (0, 0))],
        out_shape=[jax.ShapeDtypeStruct((T, D), F32), jax.ShapeDtypeStruct((8, D), F32)],
        scratch_shapes=[pltpu.VMEM((D, IN_PAD), BF16), pltpu.SemaphoreType.DMA, pltpu.VMEM((3, tm, D), F32),
                        pltpu.VMEM((3, tm, D), F32), pltpu.SemaphoreType.DMA((6,))],
        compiler_params=_cp("arbitrary"),
    )(x, dx1, dqkv, dzxd, mod6, norm1_w, w_pad, dep)


def _tn_matmul(a, b, K, N, tt, name, dep):
    T = a.shape[0]
    ja, jb = a.shape[1] // K, b.shape[1] // N
    J = max(ja, jb)

    def body(a_ref, b_ref, dep_ref, o_ref):
        t = pl.program_id(1)
        prod = _dot_tn(a_ref[...], b_ref[...])

        @pl.when(t == 0)
        def _():
            o_ref[0] = prod

        @pl.when(t > 0)
        def _():
            o_ref[0] += prod

    return pl.pallas_call(
        body, name=name, grid=(J, T // tt),
        in_specs=[pl.BlockSpec((tt, K), lambda j, t: (t, j if ja > 1 else 0)),
                  pl.BlockSpec((tt, N), lambda j, t: (t, j if jb > 1 else 0)),
                  pl.BlockSpec((8, 128), lambda j, t: (0, 0))],
        out_specs=pl.BlockSpec((1, K, N), lambda j, t: (j, 0, 0)),
        out_shape=jax.ShapeDtypeStruct((J, K, N), F32),
        compiler_params=_cp("parallel", "arbitrary"),
    )(a, b, dep)


def _accumulate(o_ref, rows, prod):
    @pl.when(pl.program_id(0) == 0)
    def _():
        o_ref[rows, :] = prod

    @pl.when(pl.program_id(0) > 0)
    def _():
        o_ref[rows, :] += prod


def _tn_matmul_rows(a0, a1, b, tt, name, dep):
    T, K = a0.shape
    N = b.shape[1]

    def body(a0_ref, a1_ref, b_ref, dep_ref, o_ref):
        for k, a_ref in enumerate((a0_ref, a1_ref)):
            _accumulate(o_ref, slice(k * K, (k + 1) * K), _dot_tn(a_ref[...], b_ref[...]))

    tile = lambda w: pl.BlockSpec((tt, w), lambda t: (t, 0))
    return pl.pallas_call(
        body, name=name, grid=(T // tt,), in_specs=[tile(K), tile(K), tile(N), DEP_SPEC],
        out_specs=pl.BlockSpec((2 * K, N), lambda t: (0, 0)), out_shape=jax.ShapeDtypeStruct((2 * K, N), F32),
        compiler_params=_cp("arbitrary"),
    )(a0, a1, b, dep)


def _adam_math(w, g, m, v):
    m = B1 * m + (1.0 - B1) * g
    v = B2 * v + (1.0 - B2) * (g * g)
    m_hat = m / (1.0 - B1 ** STEP)
    v_hat = v / (1.0 - B2 ** STEP)
    delta = -LR * (m_hat / (jnp.sqrt(v_hat) + AEPS) + WD * w)
    return delta, m, v


def _adam_2d(w, mine, land, m, v, c_arr, rb, name, dep):
    R, C = w.shape
    nbh = R // 2 // rb

    def body(c_ref, w_ref, mine_ref, land_ref, m_ref, v_ref, dep_ref, go_ref, d_ref, mo_ref, vo_ref):
        g = jnp.where(pl.program_id(0) // nbh == c_ref[0], mine_ref[...], land_ref[...])
        d, mn, vn = _adam_math(w_ref[...], g, m_ref[...], v_ref[...])
        go_ref[...] = g
        d_ref[...] = d
        mo_ref[...] = mn
        vo_ref[...] = vn

    spec = pl.BlockSpec((rb, C), lambda i, c_ref: (i, 0))
    mine_spec = pl.BlockSpec((rb, C), lambda i, c_ref: (jnp.clip(i - c_ref[0] * nbh, 0, nbh - 1), 0))
    return pl.pallas_call(
        body, name=name,
        grid_spec=pltpu.PrefetchScalarGridSpec(
            num_scalar_prefetch=1, grid=(R // rb,), in_specs=[spec, mine_spec, spec, spec, spec, DEP_SPEC],
            out_specs=[spec] * 4),
        out_shape=[jax.ShapeDtypeStruct((R, C), F32)] * 4, compiler_params=_cp("parallel"),
    )(c_arr, w, mine, land, m, v, dep)


def _adam_w_in(w3, grad, m3, v3):
    n = w3.shape[0]

    def body(w_hbm, grad_ref, m_hbm, v_hbm, g_hbm, d_hbm, mo_hbm, vo_hbm, bufs, sems):
        ins = [pltpu.make_async_copy(src.at[:, 0], bufs.at[k], sems.at[k]) for k, src in enumerate((w_hbm, m_hbm, v_hbm))]
        for cp in ins:
            cp.start()
        g = grad_ref[...]
        eye =(_iota((D, D), 0) == _iota((D, D), 1)).astype(BF16)
        g_t = jnp.zeros((n, D), F32)
        r = g
        for i in range(3):
            p = r.astype(BF16)
            g_t = g_t + _dot_tn(p, eye)
            if i < 2:
                r = r - p.astype(F32)
        for cp in ins:
            cp.wait()
        d, mn, vn = _adam_math(bufs[0], g_t, bufs[1], bufs[2])
        for k, val in enumerate((g_t, d, mn, vn)):
            bufs[3 + k] = val
        outs = [pltpu.make_async_copy(bufs.at[3 + k], dst.at[:, 0], sems.at[3 + k])
                for k, dst in enumerate((g_hbm, d_hbm, mo_hbm, vo_hbm))]
        for cp in outs:
            cp.start()
        for cp in outs:
            cp.wait()

    anyspec = pl.BlockSpec(memory_space=pl.ANY)
    vm = pl.BlockSpec(memory_space=pltpu.VMEM)
    return pl.pallas_call(
        body, name="adam_w_in",
        in_specs=[anyspec, vm, anyspec, anyspec], out_specs=[anyspec] * 4,
        out_shape=[jax.ShapeDtypeStruct(w3.shape, F32)] * 4,
        scratch_shapes=[pltpu.VMEM((7, n, D), F32), pltpu.SemaphoreType.DMA((7,))],
        compiler_params=pltpu.CompilerParams(vmem_limit_bytes=VMEM_LIMIT),
    )(w3, grad, m3, v3)


def _adam_w_ada(gat, allv, s_arr, w, m, v, rb):
    R, C = w.shape

    def body(s_ref, c_ref, dm_ref, w_ref, m_ref, v_ref, g_ref, d_ref, mo_ref, vo_ref):
        cm = _rows_select(c_ref, rb)
        g = lax.dot_general(cm * _sigmoid(cm), _rows_select(dm_ref, C), (((0,), (0,)), ((), ())), precision=HI,
                            preferred_element_type=F32)
        d, mn, vn = _adam_math(w_ref[...], g, m_ref[...], v_ref[...])
        g_ref[...] = g
        d_ref[...] = d
        mo_ref[...] = mn
        vo_ref[...] = vn

    spec = pl.BlockSpec((rb, C), lambda i, s_ref: (i, 0))
    return pl.pallas_call(
        body, name="adam_w_ada",
        grid_spec=pltpu.PrefetchScalarGridSpec(
            num_scalar_prefetch=1, grid=(R // rb,),
            in_specs=[pl.BlockSpec((8, 1, rb), lambda i, s_ref: (0, 0, i)),
                      pl.BlockSpec((8, 1, C), lambda i, s_ref: (0, 0, s_ref[0])), spec, spec, spec],
            out_specs=[spec] * 4),
        out_shape=[jax.ShapeDtypeStruct((R, C), F32)] * 4, compiler_params=_cp("parallel"),
    )(s_arr, gat, allv, w, m, v)


def _adam_small(tot, segs, ws, ms, vs):
    k = len(ws)
    extra = [sg for sg in segs if not isinstance(sg, tuple)]
    ne = len(extra)

    def body(*refs):
        tot_ref, g_x = refs[0], list(refs[1:1 + ne])
        w, m, v = [refs[1 + ne + j * k:1 + ne + (j + 1) * k] for j in range(3)]
        g_o, d_o, m_o, v_o = [refs[1 + ne + (3 + j) * k:1 + ne + (4 + j) * k] for j in range(4)]
        for i in range(k):
            gi = tot_ref[:, segs[i][0]:segs[i][0] + segs[i][1]] if isinstance(segs[i], tuple) else g_x.pop(0)[...]
            d, mn, vn = _adam_math(w[i][...], gi, m[i][...], v[i][...])
            g_o[i][...] = gi
            d_o[i][...] = d
            m_o[i][...] = mn
            v_o[i][...] = vn

    shapes = [jax.ShapeDtypeStruct(w.shape, F32) for w in ws]
    vm = pl.BlockSpec(memory_space=pltpu.VMEM)
    outs = pl.pallas_call(
        body, name="adam_small", in_specs=[vm] * (1 + ne + 3 * k), out_specs=[vm] * (4 * k), out_shape=shapes * 4,
    )(tot, *extra, *ws, *ms, *vs)
    return outs[0:k], outs[k:2 * k], outs[2 * k:3 * k], outs[3 * k:4 * k]


def _pos():
    return lax.axis_index("x"), lax.axis_index("y"), lax.axis_index("c")


def _flip(v, bit):
    return 1 - v if bit else v


def _peer(k):
    x, y, c = _pos()
    return (_flip(x, (k >> 2) & 1), _flip(y, (k >> 1) & 1), _flip(c, k & 1))


def _logical(p):
    return 4 * p[0] + 2 * p[1] + p[2]


def _gather8(src_ref, dst_ref, send_sems, recv_sems, meanwhile):
    me = _logical(_pos())
    dst_ref[pl.ds(me, 1)] = src_ref[...][None]
    copies = []
    for k in range(1, 8):
        cp = pltpu.make_async_remote_copy(src_ref, dst_ref.at[me], send_sems.at[k - 1], recv_sems.at[k - 1],
                                          device_id=_peer(k), device_id_type=MESH)
        cp.start()
        copies.append(cp)
    meanwhile()
    for k in range(1, 8):
        pltpu.make_async_remote_copy(src_ref, dst_ref.at[_logical(_peer(k))], send_sems.at[k - 1], recv_sems.at[k - 1],
                                     device_id=_peer(k), device_id_type=MESH).wait_recv()
    for cp in copies:
        cp.wait_send()


def _rows_select(ref3, width):
    row = _iota((8, width), 0)
    out = jnp.zeros((8, width), F32)
    for i in range(8):
        out = jnp.where(row == i, ref3[i][:, 0:width], out)
    return out


def _mod_exchange(c_row, cw, w_ada_s, b_ada4, w_in3):
    n_sh = w_ada_s.shape[1]
    n_in = w_in3.shape[0]
    wide = -(-n_in // 128) * 128

    def body(c_ref, cw_ref, w_hbm, b_ref, win_hbm, gat_ref, mod_ref, token, winb_ref, pay_ref, p3, w_v, win_v, win_z,
             sa, ra, sb, rb, ls):
        token[...] = jnp.zeros_like(token)
        pay_ref[:, 0:D] = c_ref[...]
        for k in range(CONVK):
            pay_ref[:, D + 256 * k:D + 256 * (k + 1)] = cw_ref[k:k + 1, :]
        x, y, c = _pos()
        me = _logical((x, y, c))
        my_s = 2 * x + y
        load_w = pltpu.make_async_copy(w_hbm, w_v, ls.at[0])
        load_in = pltpu.make_async_copy(win_hbm.at[:, 0], win_v, ls.at[1])
        load_w.start()
        load_in.start()

        def local_work():
            win_z[...] = jnp.zeros_like(win_z)
            load_in.wait()
            win_z[0:n_in, :] = win_v[...].astype(BF16)
            eye = (_iota((wide, wide), 0) == _iota((wide, wide), 1)).astype(BF16)
            winb_ref[...] = _dot_tn(win_z[...], eye)[:, 0:n_in].astype(BF16)
            load_w.wait()

        _gather8(pay_ref, gat_ref, sa, ra, local_work)
        cmat = _rows_select(gat_ref, D)
        prod = _dot_hi(cmat * _sigmoid(cmat), w_v[...])
        for b in range(8):
            p3[b] = prod[b:b + 1, :]
        mod_ref[pl.ds(my_s, 1)] = p3[pl.ds(me, 1)] + b_ref[pl.ds(my_s, 1)]
        ks = (2, 4, 6)
        copies = []
        for i, k in enumerate(ks):
            pr = _peer(k)
            cp = pltpu.make_async_remote_copy(p3.at[_logical(pr)], mod_ref.at[my_s], sb.at[i], rb.at[i],
                                              device_id=pr, device_id_type=MESH)
            cp.start()
            copies.append(cp)
        for i, k in enumerate(ks):
            pr = _peer(k)
            s_src = 2 * pr[0] + pr[1]
            pltpu.make_async_remote_copy(p3.at[0], mod_ref.at[s_src], sb.at[i], rb.at[i],
                                         device_id=pr, device_id_type=MESH).wait_recv()
            mod_ref[pl.ds(s_src, 1)] = mod_ref[pl.ds(s_src, 1)] + b_ref[pl.ds(s_src, 1)]
        for cp in copies:
            cp.wait_send()

    vm = pl.BlockSpec(memory_space=pltpu.VMEM)
    anyspec = pl.BlockSpec(memory_space=pl.ANY)
    return pl.pallas_call(
        body, name="mod_exchange", in_specs=[vm, vm, anyspec, vm, anyspec], out_specs=[vm, vm, vm, vm],
        out_shape=[jax.ShapeDtypeStruct((8, 1, D + CONVK * 256), F32), jax.ShapeDtypeStruct((4, 1, n_sh), F32),
                   jax.ShapeDtypeStruct((8, 128), F32), jax.ShapeDtypeStruct((D, n_in), BF16)],
        scratch_shapes=[pltpu.VMEM((1, D + CONVK * 256), F32), pltpu.VMEM((8, 1, n_sh), F32), pltpu.VMEM(w_ada_s.shape, F32), pltpu.VMEM((n_in, D), F32),
                        pltpu.VMEM((wide, D), BF16), pltpu.SemaphoreType.DMA((7,)), pltpu.SemaphoreType.DMA((7,)),
                        pltpu.SemaphoreType.DMA((3,)), pltpu.SemaphoreType.DMA((3,)), pltpu.SemaphoreType.DMA((2,))],
        compiler_params=pltpu.CompilerParams(vmem_limit_bytes=VMEM_LIMIT),
    )(c_row, cw, w_ada_s, b_ada4, w_in3)


def _chips():
    x, y, _ = _pos()
    out = []
    for k in (1, 2, 3):
        px, py = _flip(x, (k >> 1) & 1), _flip(y, k & 1)
        out.append((px, py, 2 * px + py))
    return out


def _half_rows(ref, which):
    half = ref.shape[-2] // 2
    return pl.ds(pl.multiple_of(which * half, 8), half)


def _plan_small():
    def plan(refs):
        me = _logical(_pos())
        return [(refs[0], refs[1].at[me], _peer(k), refs[1].at[_logical(_peer(k))]) for k in range(1, 8)]
    return plan


def _small_sum(vec, land, me_arr):
    n = vec.shape[1]

    def body(me_ref, v_ref, land_ref, tot_ref, all_ref):
        tot = None
        for i in range(8):
            row = jnp.where(me_ref[0] == i, v_ref[...], land_ref[i])
            all_ref[i] = row
            tot = row if i == 0 else tot + row
        tot_ref[...] = tot

    return pl.pallas_call(
        body, name="small_sum",
        grid_spec=pltpu.PrefetchScalarGridSpec(
            num_scalar_prefetch=1, grid=(1,),
            in_specs=[pl.BlockSpec((1, n), lambda i, me_ref: (0, 0)), pl.BlockSpec((8, 1, n), lambda i, me_ref: (0, 0, 0))],
            out_specs=[pl.BlockSpec((1, n), lambda i, me_ref: (0, 0)),
                       pl.BlockSpec((8, 1, n), lambda i, me_ref: (0, 0, 0))]),
        out_shape=[jax.ShapeDtypeStruct((1, n), F32), jax.ShapeDtypeStruct((8, 1, n), F32)],
        compiler_params=_cp("arbitrary"),
    )(me_arr, vec, land)


def _add_half(g, sib, c_arr, rb, name):
    _, R, C = g.shape
    half = R // 2
    nb = half // rb

    def body(c_ref, g_ref, s_ref, o_ref):
        o_ref[...] = (g_ref[...] + s_ref[...]).astype(BF16)

    return pl.pallas_call(
        body, name=name,
        grid_spec=pltpu.PrefetchScalarGridSpec(
            num_scalar_prefetch=1, grid=(4, nb),
            in_specs=[pl.BlockSpec((1, rb, C), lambda s, i, c_ref: (s, c_ref[0] * nb + i, 0)),
                      pl.BlockSpec((1, rb, C), lambda s, i, c_ref: (s, i, 0))],
            out_specs=pl.BlockSpec((1, rb, C), lambda s, i, c_ref: (s, i, 0))),
        out_shape=jax.ShapeDtypeStruct((4, half, C), BF16),
        compiler_params=_cp("parallel", "parallel"),
    )(c_arr, g, sib)


def _add_half_in(gq, gz, sibq, sibz, c_arr, rb):
    half = D // 2
    nq = gq.shape[1]
    wide = -(-IN_SH // 128) * 128

    def sel(rows, first, lo):
        return (_iota((rows, wide), 0) + (first - lo) == _iota((rows, wide), 1)).astype(BF16)

    def body(c_ref, gq_ref, gz_ref, sq_ref, sz_ref, o_ref):
        q = (gq_ref[...] + sq_ref[...]).astype(BF16)
        z = (gz_ref[...] + sz_ref[...]).astype(BF16)
        for s in range(4):
            lo, hi = s * IN_SH, (s + 1) * IN_SH
            acc = jnp.zeros((rb, wide), F32)
            if lo < nq:
                a0, a1 = lo // 128 * 128, min(nq, -(-min(hi, nq) // 128) * 128)
                acc = acc + _dot(q[:, a0:a1], sel(a1 - a0, a0, lo))
            if hi > nq:
                a0, a1 = (max(lo, nq) - nq) // 128 * 128, -(-(hi - nq) // 128) * 128
                acc = acc + _dot(z[:, a0:a1], sel(a1 - a0, nq + a0, lo))
            o_ref[s] = acc[:, :IN_SH].astype(BF16)

    nb = half // rb
    mine = lambda w: pl.BlockSpec((rb, w), lambda i, c_ref: (c_ref[0] * nb + i, 0))
    sib = lambda w: pl.BlockSpec((rb, w), lambda i, c_ref: (i, 0))
    return pl.pallas_call(
        body, name="grad_add_in",
        grid_spec=pltpu.PrefetchScalarGridSpec(
            num_scalar_prefetch=1, grid=(nb,),
            in_specs=[mine(nq), mine(gz.shape[1]), sib(nq), sib(gz.shape[1])],
            out_specs=pl.BlockSpec((4, rb, IN_SH), lambda i, c_ref: (0, i, 0))),
        out_shape=jax.ShapeDtypeStruct((4, half, IN_SH), BF16),
        compiler_params=_cp("parallel"),
    )(c_arr, gq, gz, sibq, sibz)


def _sum4(parts, land, s_arr, rb, name):
    _, H, C = land.shape

    def body(s_ref, own_ref, r_ref, o_ref):
        own = own_ref[0].astype(F32)
        tot = jnp.zeros((rb, C), F32)
        for j in range(4):
            tot = tot + jnp.where(s_ref[0] == j, own, r_ref[j].astype(F32))
        o_ref[...] = tot

    return pl.pallas_call(
        body, name=name,
        grid_spec=pltpu.PrefetchScalarGridSpec(
            num_scalar_prefetch=1, grid=(H // rb,),
            in_specs=[pl.BlockSpec((1, rb, C), lambda i, s_ref: (s_ref[0], i, 0)),
                      pl.BlockSpec((4, rb, C), lambda i, s_ref: (0, i, 0))],
            out_specs=pl.BlockSpec((rb, C), lambda i, s_ref: (i, 0))),
        out_shape=jax.ShapeDtypeStruct((H, C), F32), compiler_params=_cp("parallel"),
    )(s_arr, parts, land)


HBM_SPEC = pl.BlockSpec(memory_space=pltpu.HBM)
SEM_SPEC = pl.BlockSpec(memory_space=pltpu.SEMAPHORE)
EFFECT = pltpu.SideEffectType.DATAFLOW_SIDE_EFFECTING


def _split_start(name, bufs, n_sem, plan, dep):
    nb = len(bufs)

    def body(*refs):
        ins, send, recv, token = refs[:nb], refs[nb + 1], refs[nb + 2], refs[-1]
        for i, (src, dst, dev, _) in enumerate(plan(ins)):
            pltpu.make_async_remote_copy(src, dst, send.at[i], recv.at[i], device_id=dev, device_id_type=MESH).start()
        token[...] = jnp.zeros_like(token)

    outs = pl.pallas_call(
        body, name=name,
        out_shape=(pltpu.SemaphoreType.DMA((n_sem,)), pltpu.SemaphoreType.DMA((n_sem,)),
                   *[pltpu.HBM(b.shape, b.dtype) for b in bufs], jax.ShapeDtypeStruct((8, 128), F32)),
        in_specs=[HBM_SPEC] * nb + [pl.BlockSpec(memory_space=pl.ANY)],
        out_specs=(SEM_SPEC, SEM_SPEC, *([HBM_SPEC] * nb), pl.BlockSpec(memory_space=pltpu.VMEM)),
        input_output_aliases={i: 2 + i for i in range(nb)},
        compiler_params=pltpu.CompilerParams(has_side_effects=EFFECT),
    )(*[pltpu.with_memory_space_constraint(b, pltpu.HBM) for b in bufs], dep)
    return outs[0], outs[1], list(outs[2:2 + nb]), outs[-1]


def _split_wait(name, send, recv, bufs, after, plan):
    nb = len(bufs)
    after = list(after) if isinstance(after, (list, tuple)) else [after]

    def body(*refs):
        ins, send_s, recv_s = refs[:nb], refs[nb], refs[nb + 1]
        for i, (src, dst, dev, mine) in enumerate(plan(ins)):
            pltpu.make_async_remote_copy(src, dst, send_s.at[i], recv_s.at[i], device_id=dev,
                                         device_id_type=MESH).wait_send()
            pltpu.make_async_remote_copy(src, mine, send_s.at[i], recv_s.at[i], device_id=dev,
                                         device_id_type=MESH).wait_recv()

    outs = pl.pallas_call(
        body, name=name, out_shape=[pltpu.HBM(b.shape, b.dtype) for b in bufs],
        in_specs=[HBM_SPEC] * nb + [SEM_SPEC, SEM_SPEC] + [HBM_SPEC] * len(after),
        out_specs=[HBM_SPEC] * nb, input_output_aliases={i: i for i in range(nb)},
        compiler_params=pltpu.CompilerParams(has_side_effects=EFFECT),
    )(*bufs, send, recv, *[pltpu.with_memory_space_constraint(a, pltpu.HBM) for a in after])
    return list(outs)


def _split_wait_start(name, send, recv, bufs, after, plan, bufs2, n_sem2, plan2):
    nb, nb2 = len(bufs), len(bufs2)
    after = list(after) if isinstance(after, (list, tuple)) else [after]
    n_in = nb + 2 + nb2 + len(after)

    def body(*refs):
        ins, send_s, recv_s, ins2 = refs[:nb], refs[nb], refs[nb + 1], refs[nb + 2:nb + 2 + nb2]
        send2, recv2, token = refs[n_in + nb], refs[n_in + nb + 1], refs[-1]
        for i, (src, dst, dev, mine) in enumerate(plan(ins)):
            pltpu.make_async_remote_copy(src, dst, send_s.at[i], recv_s.at[i], device_id=dev,
                                         device_id_type=MESH).wait_send()
            pltpu.make_async_remote_copy(src, mine, send_s.at[i], recv_s.at[i], device_id=dev,
                                         device_id_type=MESH).wait_recv()
        for i, (src, dst, dev, _) in enumerate(plan2(ins2)):
            pltpu.make_async_remote_copy(src, dst, send2.at[i], recv2.at[i], device_id=dev, device_id_type=MESH).start()
        token[...] = jnp.zeros_like(token)

    hbm = lambda b: pltpu.with_memory_space_constraint(b, pltpu.HBM)
    outs = pl.pallas_call(
        body, name=name,
        out_shape=(*[pltpu.HBM(b.shape, b.dtype) for b in bufs], pltpu.SemaphoreType.DMA((n_sem2,)),
                   pltpu.SemaphoreType.DMA((n_sem2,)), *[pltpu.HBM(b.shape, b.dtype) for b in bufs2],
                   jax.ShapeDtypeStruct((8, 128), F32)),
        in_specs=[HBM_SPEC] * nb + [SEM_SPEC, SEM_SPEC] + [HBM_SPEC] * (nb2 + len(after)),
        out_specs=(*([HBM_SPEC] * nb), SEM_SPEC, SEM_SPEC, *([HBM_SPEC] * nb2), pl.BlockSpec(memory_space=pltpu.VMEM)),
        input_output_aliases={**{i: i for i in range(nb)}, **{nb + 2 + j: nb + 2 + j for j in range(nb2)}},
        compiler_params=pltpu.CompilerParams(has_side_effects=EFFECT),
    )(*bufs, send, recv, *[hbm(b) for b in bufs2], *[hbm(a) for a in after])
    return list(outs[:nb]), outs[nb], outs[nb + 1], list(outs[nb + 2:nb + 2 + nb2]), outs[-1]


def _copies_now(name, bufs, n_sem, plan):
    nb = len(bufs)

    def body(*refs):
        ins, token, send, recv = refs[:nb], refs[2 * nb], refs[-2], refs[-1]
        token[...] = jnp.zeros_like(token)
        todo = plan(ins)
        for i, (src, dst, dev, _) in enumerate(todo):
            pltpu.make_async_remote_copy(src, dst, send.at[i], recv.at[i], device_id=dev, device_id_type=MESH).start()
        for i, (src, dst, dev, mine) in enumerate(todo):
            pltpu.make_async_remote_copy(src, mine, send.at[i], recv.at[i], device_id=dev, device_id_type=MESH).wait_recv()
        for i, (src, dst, dev, _) in enumerate(todo):
            pltpu.make_async_remote_copy(src, dst, send.at[i], recv.at[i], device_id=dev, device_id_type=MESH).wait_send()

    outs = pl.pallas_call(
        body, name=name,
        out_shape=[pltpu.HBM(b.shape, b.dtype) for b in bufs] + [jax.ShapeDtypeStruct((8, 128), F32)],
        in_specs=[HBM_SPEC] * nb, out_specs=[HBM_SPEC] * nb + [pl.BlockSpec(memory_space=pltpu.VMEM)],
        input_output_aliases={i: i for i in range(nb)},
        scratch_shapes=[pltpu.SemaphoreType.DMA((n_sem,)), pltpu.SemaphoreType.DMA((n_sem,))],
    )(*[pltpu.with_memory_space_constraint(b, pltpu.HBM) for b in bufs])
    return list(outs[:nb]), outs[nb]


def _slot(land, s, rows, cols):
    if cols is None:
        return land.at[s, rows]
    return land.at[rows, pl.ds(pl.multiple_of(s * cols, 128), cols)]


def _plan_gather_ici(cols):
    nw = len(cols)

    def plan(refs):
        x, y, c = _pos()
        my_s = 2 * x + y
        out = []
        for w in range(nw):
            mine = _half_rows(refs[w], c)
            for px, py, ps in _chips():
                out.append((refs[w].at[mine], _slot(refs[nw + w], my_s, mine, cols[w]), (px, py, c),
                            _slot(refs[nw + w], ps, mine, cols[w])))
        return out
    return plan


def _plan_gather_fwd(cols, rows):
    def plan(refs):
        x, y, c = _pos()
        out = []
        for w in range(len(cols)):
            half = rows[w] // 2
            mine = pl.ds(pl.multiple_of(c * half, 8), half)
            other = pl.ds(pl.multiple_of((1 - c) * half, 8), half)
            for px, py, ps in _chips():
                got = _slot(refs[w], ps, mine, cols[w])
                out.append((got, got, (x, y, 1 - c), _slot(refs[w], ps, other, cols[w])))
        return out
    return plan


def _plan_swap(nw):
    def plan(refs):
        x, y, c = _pos()
        return [(refs[w].at[:, _half_rows(refs[w], 1 - c)], refs[nw + w], (x, y, 1 - c), refs[nw + w])
                for w in range(nw)]
    return plan


def _plan_swap_rows(nw):
    def plan(refs):
        x, y, c = _pos()
        return [(refs[w].at[_half_rows(refs[w], 1 - c)], refs[nw + w], (x, y, 1 - c), refs[nw + w])
                for w in range(nw)]
    return plan


def _plan_scatter(nw):
    def plan(refs):
        x, y, c = _pos()
        my_s = 2 * x + y
        out = []
        for w in range(nw):
            for px, py, ps in _chips():
                out.append((refs[w].at[ps], refs[nw + w].at[my_s], (px, py, c), refs[nw + w].at[ps]))
        return out
    return plan


def _plan_scatter_both():
    def plan(refs):
        x, y, c = _pos()
        my_s = 2 * x + y
        src, land = refs
        out = []
        for px, py, ps in _chips():
            out.append((src.at[ps], land.at[my_s, c], (px, py, c), land.at[ps, c]))
            out.append((src.at[ps], land.at[my_s, c], (px, py, 1 - c), land.at[ps, 1 - c]))
        out.append((src.at[my_s], land.at[my_s, c], (x, y, 1 - c), land.at[my_s, 1 - c]))
        return out
    return plan


def _sum4_both(parts, land, s_arr, c_arr):
    _, _, H, C = land.shape

    def body(s_ref, c_ref, own_ref, r_ref, o_ref):
        mine = pl.program_id(0) == c_ref[0]
        own = own_ref[0].astype(F32)
        tot = jnp.zeros((H, C), F32)
        for j in range(4):
            tot = tot + jnp.where(jnp.logical_and(mine, s_ref[0] == j), own, r_ref[j, 0].astype(F32))
        o_ref[0] = tot

    return pl.pallas_call(
        body, name="grad_sum_in",
        grid_spec=pltpu.PrefetchScalarGridSpec(
            num_scalar_prefetch=2, grid=(2,),
            in_specs=[pl.BlockSpec((1, H, C), lambda h, s_ref, c_ref: (s_ref[0], 0, 0)),
                      pl.BlockSpec((4, 1, H, C), lambda h, s_ref, c_ref: (0, h, 0, 0))],
            out_specs=pl.BlockSpec((1, H, C), lambda h, s_ref, c_ref: (h, 0, 0))),
        out_shape=jax.ShapeDtypeStruct((2, H, C), F32), compiler_params=_cp("parallel"),
    )(s_arr, c_arr, parts, land).reshape(2 * H, C)


def _plan_join(nw):
    def plan(refs):
        x, y, c = _pos()
        out = []
        for w in range(nw):
            land = refs[nw + w]
            out.append((refs[w], land.at[_half_rows(land, c)], (x, y, 1 - c), land.at[_half_rows(land, 1 - c)]))
        return out
    return plan


def _hbm_empty(shape, dtype):
    return pltpu.with_memory_space_constraint(lax.empty(shape, dtype), pltpu.HBM)


def _w_in_assemble(land, own, s_arr, rb):
    wide = -(-IN_SH // 128) * 128
    starts = [s * IN_SH // 128 * 128 for s in range(4)]
    ends = [min(IN_PAD, -(-(s + 1) * IN_SH // 128) * 128) for s in range(4)]

    def body(s_ref, land_ref, own_ref, o_ref, parts):
        @pl.when(pl.program_id(0) == 0)
        def _():
            parts[...] = jnp.zeros_like(parts)

        acc = []
        for s in range(4):
            parts[s, :, 0:IN_SH] = jnp.where(s_ref[0] == s, own_ref[...], land_ref[s])
            w = ends[s] - starts[s]
            sel = (_iota((wide, w), 0) + (s * IN_SH - starts[s]) == _iota((wide, w), 1)).astype(BF16)
            acc.append(_dot(parts[s], sel))
        for s in range(4):
            lo = starts[s] if s == 0 else ends[s - 1]
            hi = starts[s + 1] if s < 3 else ends[s]
            o_ref[:, lo:hi] = acc[s][:, lo - starts[s]:hi - starts[s]].astype(BF16)
            if s < 3:
                a, b = starts[s + 1], ends[s]
                o_ref[:, a:b] = (acc[s][:, a - starts[s]:b - starts[s]] + acc[s + 1][:, 0:b - a]).astype(BF16)

    return pl.pallas_call(
        body, name="w_in_assemble",
        grid_spec=pltpu.PrefetchScalarGridSpec(
            num_scalar_prefetch=1, grid=(D // rb,),
            in_specs=[pl.BlockSpec((4, rb, IN_SH), lambda i, s_ref: (0, i, 0)),
                      pl.BlockSpec((rb, IN_SH), lambda i, s_ref: (i, 0))],
            out_specs=pl.BlockSpec((rb, IN_PAD), lambda i, s_ref: (i, 0)),
            scratch_shapes=[pltpu.VMEM((4, rb, wide), BF16)]),
        out_shape=jax.ShapeDtypeStruct((D, IN_PAD), BF16), compiler_params=_cp("arbitrary"),
    )(s_arr, land, own)


def _pad_lanes(a, n):
    return jnp.pad(a, ((0, 0), (0, n - a.shape[1])))


def kernel(x, c, positions, w_ada, b_ada, norm1_w, w_in, conv_w, conv_b, dt_bias, a_log, d_skip, attn_sinks, ssm_norm_w, w_out, norm2_w, w_gate_up, w_down, final_norm_w, loss_target, m_w_ada, m_b_ada, m_norm1_w, m_w_in, m_conv_w, m_conv_b, m_dt_bias, m_a_log, m_d_skip, m_attn_sinks, m_ssm_norm_w, m_w_out, m_norm2_w, m_w_gate_up, m_w_down, m_final_norm_w, v_w_ada, v_b_ada, v_norm1_w, v_w_in, v_conv_w, v_conv_b, v_dt_bias, v_a_log, v_d_skip, v_attn_sinks, v_ssm_norm_w, v_w_out, v_norm2_w, v_w_gate_up, v_w_down, v_final_norm_w):
    T = x.shape[1]
    tm = min(256, T)
    xi, yi, ci = lax.axis_index("x"), lax.axis_index("y"), lax.axis_index("c")
    my_s = 2 * xi + yi
    xs = x[0]
    tgt = loss_target[0]

    gat, mod4, tok, w_in_b = _mod_exchange(c, conv_w[0], w_ada[0], b_ada.reshape(4, 1, 1536), w_in.transpose(2, 0, 1))
    mod6 = mod4.reshape(6, D)
    cw_dev = gat[:, 0, D:].reshape(4, 2, CONVK, 256)[:, 0]
    conv_full = cw_dev.transpose(1, 0, 2).reshape(CONVK, CONVC)

    s_i, r_i, bufs, tok = _split_start("wgather_in_ici_start", [w_in_b, _hbm_empty((4,) + w_in_b.shape, BF16)], 3,
                                       _plan_gather_ici([None]), tok)
    inv_freq = (10000.0 ** (-jnp.arange(32, dtype=F32) / 32))
    cos, sin_s = _rope_tables(positions, inv_freq.reshape(32, 1), min(512, T), tok)
    late = [w_out[0].astype(BF16), w_gate_up[0].astype(BF16), w_down[0].astype(BF16)]
    lands = [_hbm_empty((4, D // 4, D), BF16), _hbm_empty((D, 2 * DFF), BF16), _hbm_empty((4, DFF // 4, D), BF16)]
    cols3, rows3 = [None, GU_SH, None], [D // 4, D, DFF // 4]
    bufs, s_a, r_a, bufs_late, tok = _split_wait_start(
        "wgather_in_ici_wait", s_i, r_i, bufs, cos, _plan_gather_ici([None]), late + lands, 9, _plan_gather_ici(cols3))
    own_in = bufs[0]
    bufs, tok = _copies_now("wgather_in_fwd", bufs[1:], 3, _plan_gather_fwd([None], [D]))
    s_arr = my_s.reshape(1).astype(jnp.int32)
    w_pad = _w_in_assemble(bufs[0], own_in, s_arr, 512)
    bufs = bufs_late

    qkv, z, xbc, dtr, h1b = _in_proj_fwd(xs, cos, sin_s, mod6, norm1_w, w_pad, min(512, T), tok)
    sinks = attn_sinks
    attn, lse = _attn_fwd(qkv, sinks)
    bufs = _split_wait("wgather_ici_wait", s_a, r_a, bufs, attn, _plan_gather_ici(cols3))
    late = bufs[:3]
    s_b, r_b, lands, tok = _split_start("wgather_fwd_start", bufs[3:], 9, _plan_gather_fwd(cols3, rows3), attn)
    dtb = _pad_lanes(dt_bias, 128)
    alog = _pad_lanes(a_log, 128)
    dskx = jnp.repeat(d_skip, HD, axis=1)
    mats = _ssd_mats()
    ynorm, ypre, states, conv_pre = _ssd_fwd(xbc, z, dtr, conv_full, conv_b, dtb, alog, dskx, ssm_norm_w, mats, tok)
    lands = _split_wait("wgather_fwd_wait", s_b, r_b, lands, ynorm, _plan_gather_fwd(cols3, rows3))
    w_out_f = lands[0].reshape(D, D)
    w_dn_f = lands[2].reshape(DFF, D)

    fw2 = final_norm_w.reshape(1, D)
    sq, dmix, dx1, h2b, act, dfb, dgu, dob, sm_ffn = _mix_ffn(
        xs, attn, ynorm, tgt, mod6, norm2_w, fw2, w_out_f, lands[1], late[1], s_arr, w_dn_f, late[0], late[2], tm)

    tt = min(2048, T)
    c_arr = ci.reshape(1).astype(jnp.int32)
    tok0 = jnp.zeros((8, 128), F32)
    gw_dn4 = _tn_matmul(act, dfb, GU_SH, D, tt, "dw_down", tok0).reshape(4, DFF // 4, D)
    gw_gu4 = _tn_matmul(h2b, dgu, D, GU_SH, tt, "dw_gate_up", tok0)
    gw_out4 = _tn_matmul_rows(attn, ynorm, dob, tt, "dw_out", tok0).reshape(4, D // 4, D)
    big1 = [gw_out4, gw_gu4, gw_dn4]
    rbs1 = [128, 512, 352]
    sib1 = [_hbm_empty((4, g.shape[1] // 2, g.shape[2]), F32) for g in big1]
    s_c, r_c, bufs, tok = _split_start("gswap_start", big1 + sib1, 3, _plan_swap(3), tok0)

    dzxd, d_cw, d_cb, d_sw, d_sk, d_dtb, d_av = _ssd_bwd(
        xbc, conv_pre, z, dtr, ypre, states, dmix, conv_full, dtb, alog, dskx, ssm_norm_w, mats, tok)
    bufs = _split_wait("gswap_wait", s_c, r_c, bufs, dzxd, _plan_swap(3))
    sums1 = [_add_half(g, s, c_arr, rb, "grad_add_%d" % i)
             for i, (g, s, rb) in enumerate(zip(bufs[:3], bufs[3:], rbs1))]
    land1 = [_hbm_empty(p.shape, BF16) for p in sums1]
    s_d, r_d, bufs, tok = _split_start("gscatter_start", sums1 + land1, 9, _plan_scatter(3), tok0)
    dqkv, d_sinks = _attn_bwd(qkv, sinks, lse, dmix, cos, sin_s, tok)
    bufs = _split_wait("gscatter_wait", s_d, r_d, bufs, dqkv, _plan_scatter(3))
    halves1 = [_sum4(p, l, s_arr, rb, "grad_sum_%d" % i)
               for i, (p, l, rb) in enumerate(zip(bufs[:3], bufs[3:], rbs1))]
    full1 = [_hbm_empty((2 * h.shape[0], h.shape[1]), F32) for h in halves1]
    s_e, r_e, bufs, tok = _split_start("gjoin_start", halves1 + full1, 3, _plan_join(3), tok0)
    gq = _tn_matmul(h1b, dqkv, D, 768, tt, "dw_in_qkv", tok)[0]
    gz = _tn_matmul(h1b, dzxd, D, IN_PAD - 768, tt, "dw_in_zxd", tok)[0]
    joined1 = _split_wait("gjoin_wait", s_e, r_e, bufs, [gq, gz], _plan_join(3))

    sibs = [_hbm_empty((D // 2, g.shape[1]), F32) for g in (gq, gz)]
    s_f, r_f, bufs, tok = _split_start("gswap_in_start", [gq, gz] + sibs, 2, _plan_swap_rows(2), tok0)
    g_dn_s, d_dn, m_dn, v_dn = _adam_2d(w_down[0], joined1[2], joined1[5], m_w_down[0], v_w_down[0], c_arr, 352,
                                        "adam_w_down", tok)
    g_gu_s, d_gu, m_gu, v_gu = _adam_2d(w_gate_up[0], joined1[1], joined1[4], m_w_gate_up[0], v_w_gate_up[0], c_arr,
                                        256, "adam_w_gate_up", tok)
    g_out_s, d_out, m_out, v_out = _adam_2d(w_out[0], joined1[0], joined1[3], m_w_out[0], v_w_out[0], c_arr, 128,
                                            "adam_w_out", tok)
    bufs = _split_wait("gswap_in_wait", s_f, r_f, bufs, [d_dn, d_gu, d_out], _plan_swap_rows(2))
    sum0 = _add_half_in(bufs[0], bufs[1], bufs[2], bufs[3], c_arr, D // 2)
    s_g, r_g, bufs, tok = _split_start("gscatter_in_start", [sum0, _hbm_empty((4, 2) + sum0.shape[1:], BF16)], 7,
                                       _plan_scatter_both(), tok0)
    grad_x, sm_in = _in_proj_bwd(xs, dx1, dqkv, dzxd, mod6, norm1_w, w_pad, min(512, T), tok)

    a_neg = -jnp.exp(alog)
    pieces = [sm_in[1:2], sm_in[2:3], sm_ffn[5:6], sm_ffn[2:3], sm_ffn[3:4], sm_ffn[4:5],
              sm_in[0:1], sm_ffn[1:2], sm_ffn[0:1], d_cb, d_cw.reshape(1, CONVK * CONVC),
              _pad_lanes(d_sw, SW), d_dtb, d_av * a_neg, d_sk, d_sinks,
              _pad_lanes((0.5 / D * jnp.sum(sq)).reshape(1, 1), 128)]
    vec = jnp.concatenate(pieces, axis=1)
    s_h, r_h, rows8, tok_small = _split_start("small_start", [vec, _hbm_empty((8,) + vec.shape, F32)], 7,
                                              _plan_small(), tok0)

    bufs = _split_wait("gscatter_in_wait", s_g, r_g, bufs, [grad_x, tok_small], _plan_scatter_both())
    gw_in_s = _sum4_both(bufs[0], bufs[1], s_arr, c_arr)
    native = lambda a: a.transpose(2, 0, 1)
    adam_in = _adam_w_in(native(w_in), gw_in_s, native(m_w_in), native(v_w_in))
    g_in_s, d_in, m_in, v_in = [a.transpose(1, 2, 0) for a in adam_in]
    rows8 = _split_wait("small_wait", s_h, r_h, rows8, [adam_in[1]], _plan_small())
    tot, allv = _small_sum(rows8[0], rows8[1], (4 * xi + 2 * yi + ci).reshape(1).astype(jnp.int32))
    o = 0
    offs = []
    for p in pieces:
        offs.append(o)
        o += p.shape[1]
    seg = lambda i, n: (offs[i], n)
    g_conv_w = lax.dynamic_slice_in_dim(
        tot[:, offs[10]:offs[10] + CONVK * CONVC].reshape(CONVK, CONVC), my_s * 256, 256, axis=1)
    loss = tot[0, offs[16]]

    small_names = ["b_ada", "norm1_w", "conv_w", "conv_b", "dt_bias", "a_log", "d_skip", "attn_sinks", "ssm_norm_w",
                   "norm2_w", "final_norm_w"]
    small_g = [(0, 6 * D), seg(6, D), g_conv_w, seg(9, D), seg(12, 8), seg(13, 8), seg(14, 8), seg(15, 8),
               seg(11, SW), seg(7, D), seg(8, D)]
    as2d = lambda a: a.reshape(-1, a.shape[-1])
    small_w = [as2d(a) for a in (b_ada, norm1_w, conv_w, conv_b, dt_bias, a_log, d_skip, attn_sinks, ssm_norm_w,
                                 norm2_w, final_norm_w)]
    small_m = [as2d(a) for a in (m_b_ada, m_norm1_w, m_conv_w, m_conv_b, m_dt_bias, m_a_log, m_d_skip, m_attn_sinks,
                                 m_ssm_norm_w, m_norm2_w, m_final_norm_w)]
    small_v = [as2d(a) for a in (v_b_ada, v_norm1_w, v_conv_w, v_conv_b, v_dt_bias, v_a_log, v_d_skip, v_attn_sinks,
                                 v_ssm_norm_w, v_norm2_w, v_final_norm_w)]
    small_g, sd, smn, svn = _adam_small(tot, small_g, small_w, small_m, small_v)
    g_ada, d_ada, m_ada, v_ada = _adam_w_ada(gat, allv, s_arr, w_ada[0], m_w_ada[0], v_w_ada[0], 256)

    order = ["w_ada", "b_ada", "norm1_w", "w_in", "conv_w", "conv_b", "dt_bias", "a_log", "d_skip", "attn_sinks",
             "ssm_norm_w", "w_out", "norm2_w", "w_gate_up", "w_down", "final_norm_w"]
    shapes = dict(w_ada=w_ada.shape, b_ada=b_ada.shape, norm1_w=norm1_w.shape, w_in=w_in.shape, conv_w=conv_w.shape,
                  conv_b=conv_b.shape, dt_bias=dt_bias.shape, a_log=a_log.shape, d_skip=d_skip.shape,
                  attn_sinks=attn_sinks.shape, ssm_norm_w=ssm_norm_w.shape, w_out=w_out.shape, norm2_w=norm2_w.shape,
                  w_gate_up=w_gate_up.shape, w_down=w_down.shape, final_norm_w=final_norm_w.shape)
    grads = dict(w_ada=g_ada, w_in=g_in_s, w_out=g_out_s, w_gate_up=g_gu_s, w_down=g_dn_s)
    deltas = dict(w_ada=d_ada, w_in=d_in, w_out=d_out, w_gate_up=d_gu, w_down=d_dn)
    new_m = dict(w_ada=m_ada, w_in=m_in, w_out=m_out, w_gate_up=m_gu, w_down=m_dn)
    new_v = dict(w_ada=v_ada, w_in=v_in, w_out=v_out, w_gate_up=v_gu, w_down=v_dn)
    for i, nme in enumerate(small_names):
        grads[nme], deltas[nme], new_m[nme], new_v[nme] = small_g[i], sd[i], smn[i], svn[i]
    outs = [loss, grad_x[None]]
    for table in (grads, deltas, new_m, new_v):
        outs += [table[nme].reshape(shapes[nme]) for nme in order]
    return tuple(outs)
```
